```python
import jax, jax.numpy as jnp
from jax import lax
import numpy as np

D_MODEL = 1024
BATCH = 8
SEQ = 8192
DEPTH = 1

D_RNN = 1024
RNN_BLOCKS = 8
RNN_BLOCK_W = D_RNN // RNN_BLOCKS
CONV_W = 4
LRU_C = 8.0
N_HEADS = 8
HEAD_DIM = 128
D_ATTN = N_HEADS * HEAD_DIM
ROT_DIM = HEAD_DIM // 4
ROPE_THETA = 500000.0
DILATED_GROUPS = ((128, 1), (512, 4), (2048, 16))
NEG_INF = -1e30
IN_WIDTHS = (D_RNN, D_RNN, D_ATTN, D_ATTN, D_ATTN, D_ATTN, D_MODEL, D_MODEL)
D_IN = sum(IN_WIDTHS)
IN_SPLITS = [int(s) for s in np.cumsum(IN_WIDTHS)[:-1]]
NORM_EPS = 1e-6

kernel_name = "hybrid_rglru_dilated_attn_block"


def rms_norm(x, g):
    xf = x.astype(jnp.float32)
    y = xf * lax.rsqrt(jnp.mean(xf * xf, axis=-1, keepdims=True) + NORM_EPS)
    return (y * g.astype(jnp.float32)).astype(x.dtype)


def causal_depthwise_conv(x, w, b):
    y = lax.conv_general_dilated(
        x, w[:, None, :], window_strides=(1,), padding=[(CONV_W - 1, 0)],
        dimension_numbers=("NWC", "WIO", "NWC"), feature_group_count=x.shape[-1])
    return y + b


def rg_lru(x, w_a, b_a, w_x, b_x, lam, positions):
    B, S, _ = x.shape
    xf = x.astype(jnp.float32)
    xh = xf.reshape(B, S, RNN_BLOCKS, RNN_BLOCK_W)
    r = jax.nn.sigmoid(jnp.einsum("bshi,hij->bshj", xh, w_a.astype(jnp.float32)) + b_a.astype(jnp.float32))
    i = jax.nn.sigmoid(jnp.einsum("bshi,hij->bshj", xh, w_x.astype(jnp.float32)) + b_x.astype(jnp.float32))
    r = r.reshape(B, S, D_RNN)
    i = i.reshape(B, S, D_RNN)
    log_a = -LRU_C * r * jax.nn.softplus(-lam.astype(jnp.float32))
    reset = (positions == 0)[..., None]
    a = jnp.where(reset, 0.0, jnp.exp(log_a))
    mult = jnp.where(reset, 1.0, jnp.sqrt(-jnp.expm1(2.0 * log_a)))
    bx = mult * i * xf

    def combine(left, right):
        a1, b1 = left
        a2, b2 = right
        return a1 * a2, a2 * b1 + b2

    _, h = lax.associative_scan(combine, (a, bx), axis=1)
    return h


def apply_partial_rope(t, cos, sin):
    tf = t.astype(jnp.float32)
    half = ROT_DIM // 2
    x1 = tf[..., :half]
    x2 = tf[..., half:ROT_DIM]
    return jnp.concatenate([x1 * cos - x2 * sin, x2 * cos + x1 * sin, tf[..., ROT_DIM:]], axis=-1)


def dilated_window_attention(q, k, v, window, dilation):
    B, S, H, Dh = q.shape
    blk = window // dilation
    span = blk * dilation
    s_pad = -(-S // span) * span
    nb = s_pad // span

    def to_blocks(t):
        t = jnp.pad(t, ((0, 0), (0, s_pad - S), (0, 0), (0, 0)))
        return t.reshape(B, nb, blk, dilation, H, Dh)

    qb, kb, vb = to_blocks(q), to_blocks(k), to_blocks(v)
    pad_prev = ((0, 0), (1, 0), (0, 0), (0, 0), (0, 0), (0, 0))
    kk = jnp.concatenate([jnp.pad(kb[:, :-1], pad_prev), kb], axis=2)
    vv = jnp.concatenate([jnp.pad(vb[:, :-1], pad_prev), vb], axis=2)
    s = jnp.einsum("bnqrhd,bnkrhd->bnrhqk", qb, kk) * (HEAD_DIM ** -0.5)
    qi = jnp.arange(blk)[:, None]
    ki = jnp.arange(2 * blk)[None, :]
    dist = blk + qi - ki
    band = (dist >= 0) & (dist <= blk)
    n_idx = jnp.arange(nb)[:, None, None]
    valid = band[None] & ((n_idx > 0) | (ki[None] >= blk))
    s = jnp.where(valid[None, :, None, None], s, NEG_INF)
    lse = jax.nn.logsumexp(s, axis=-1)
    p = jnp.exp(s - lse[..., None])
    o = jnp.einsum("bnrhqk,bnkrhd->bnqrhd", p, vv)
    o = o.reshape(B, s_pad, H, Dh)[:, :S]
    lse = lse.transpose(0, 1, 4, 2, 3).reshape(B, s_pad, H)[:, :S]
    return o, lse


def dilated_attention_mixture(q, k, v):
    outs, lses = [], []
    for window, dilation in DILATED_GROUPS:
        o, l = dilated_window_attention(q, k, v, window, dilation)
        outs.append(o)
        lses.append(l)
    w = jax.nn.softmax(jnp.stack(lses, axis=0), axis=0)
    return jnp.einsum("gbsh,gbshd->bshd", w, jnp.stack(outs, axis=0))


def _fwd_setup_inputs(seed: int = 0) -> dict:
    key = jax.random.key(seed)
    ks = jax.random.split(key, 20)
    f32 = jnp.float32
    nrm = lambda k, shape, scale: jax.random.normal(k, shape, f32) * scale
    x = jax.random.normal(ks[0], (BATCH, SEQ, D_MODEL), f32)
    c = jax.random.normal(ks[1], (BATCH, D_MODEL), f32)
    positions = jnp.broadcast_to(jnp.arange(SEQ, dtype=jnp.int32)[None, :], (BATCH, SEQ))
    g_norm = 1.0 + nrm(ks[2], (DEPTH, D_MODEL), 0.02)
    w_mod = nrm(ks[3], (DEPTH, D_MODEL, 3 * D_MODEL), 0.5 * D_MODEL ** -0.5)
    b_mod = nrm(ks[4], (DEPTH, 3 * D_MODEL), 0.01)
    w_in = nrm(ks[5], (DEPTH, D_MODEL, D_IN), D_MODEL ** -0.5)
    b_gate = nrm(ks[6], (DEPTH, 2 * D_MODEL), 0.01)
    conv_w = nrm(ks[7], (DEPTH, CONV_W, D_RNN), CONV_W ** -0.5)
    conv_b = nrm(ks[8], (DEPTH, D_RNN), 0.01)
    w_a = nrm(ks[9], (DEPTH, RNN_BLOCKS, RNN_BLOCK_W, RNN_BLOCK_W), RNN_BLOCK_W ** -0.5)
    b_a = nrm(ks[10], (DEPTH, RNN_BLOCKS, RNN_BLOCK_W), 0.01)
    w_x = nrm(ks[11], (DEPTH, RNN_BLOCKS, RNN_BLOCK_W, RNN_BLOCK_W), RNN_BLOCK_W ** -0.5)
    b_x = nrm(ks[12], (DEPTH, RNN_BLOCKS, RNN_BLOCK_W), 0.01)
    a0 = jax.random.uniform(ks[13], (DEPTH, D_RNN), f32, 0.9, 0.999)
    sig = a0 ** (1.0 / LRU_C)
    lam = jnp.log(sig) - jnp.log1p(-sig)
    w_out_rnn = nrm(ks[14], (DEPTH, D_RNN, D_MODEL), D_RNN ** -0.5)
    w_out_attn = nrm(ks[15], (DEPTH, D_ATTN, D_MODEL), D_ATTN ** -0.5)
    w_o = nrm(ks[16], (DEPTH, D_MODEL, D_MODEL), D_MODEL ** -0.5)
    g_final = 1.0 + nrm(ks[17], (D_MODEL,), 0.02)
    return {"x": x, "c": c, "positions": positions, "g_norm": g_norm, "w_mod": w_mod,
            "b_mod": b_mod, "w_in": w_in, "b_gate": b_gate, "conv_w": conv_w, "conv_b": conv_b,
            "w_a": w_a, "b_a": b_a, "w_x": w_x, "b_x": b_x, "lam": lam, "w_out_rnn": w_out_rnn,
            "w_out_attn": w_out_attn, "w_o": w_o, "g_final": g_final}


def _fwd_reference(x, c, positions, g_norm, w_mod, b_mod, w_in, b_gate, conv_w, conv_b,
              w_a, b_a, w_x, b_x, lam, w_out_rnn, w_out_attn, w_o, g_final):
    B, S, _ = x.shape
    dt = x.dtype
    inv_freq = ROPE_THETA ** (-jnp.arange(0, ROT_DIM, 2, dtype=jnp.float32) / ROT_DIM)
    ang = positions.astype(jnp.float32)[..., None] * inv_freq
    cos = jnp.cos(ang)[:, :, None, :]
    sin = jnp.sin(ang)[:, :, None, :]
    c_act = jax.nn.silu(c)
    for l in range(DEPTH):
        mod = c_act @ w_mod[l] + b_mod[l]
        shift, scale, gate = jnp.split(mod, 3, axis=-1)
        h = rms_norm(x, g_norm[l]) * (1.0 + scale[:, None, :]) + shift[:, None, :]
        proj = h @ w_in[l]
        x_rnn, z_rnn, q, k, v, z_attn, g_r, g_a = jnp.split(proj, IN_SPLITS, axis=-1)
        xc = causal_depthwise_conv(x_rnn, conv_w[l], conv_b[l])
        hr = rg_lru(xc, w_a[l], b_a[l], w_x[l], b_x[l], lam[l], positions)
        y_rnn = (hr * jax.nn.silu(z_rnn.astype(jnp.float32))).astype(dt) @ w_out_rnn[l]
        qh = apply_partial_rope(q.reshape(B, S, N_HEADS, HEAD_DIM), cos, sin)
        kh = apply_partial_rope(k.reshape(B, S, N_HEADS, HEAD_DIM), cos, sin)
        vh = v.reshape(B, S, N_HEADS, HEAD_DIM).astype(jnp.float32)
        o = dilated_attention_mixture(qh, kh, vh).reshape(B, S, D_ATTN)
        y_attn = (o * jax.nn.silu(z_attn.astype(jnp.float32))).astype(dt) @ w_out_attn[l]
        bg_r, bg_a = jnp.split(b_gate[l], 2, axis=-1)
        merged = jax.nn.sigmoid(g_r + bg_r) * y_rnn + jax.nn.sigmoid(g_a + bg_a) * y_attn
        x = x + gate[:, None, :] * (merged @ w_o[l])
    return rms_norm(x, g_final)


import jax as _jax
import jax.numpy as _jnp

TWIN_FORMAT = 'train_step'
FWD_PARAMS = ['x', 'c', 'positions', 'g_norm', 'w_mod', 'b_mod', 'w_in', 'b_gate', 'conv_w', 'conv_b', 'w_a', 'b_a', 'w_x', 'b_x', 'lam', 'w_out_rnn', 'w_out_attn', 'w_o', 'g_final']
TWIN_WEIGHTS = ['g_norm', 'w_mod', 'b_mod', 'w_in', 'b_gate', 'conv_w', 'conv_b', 'w_a', 'b_a', 'w_x', 'b_x', 'lam', 'w_out_rnn', 'w_out_attn', 'w_o', 'g_final']
TWIN_DIFF_INPUT = 'x'
TWIN_INPUTS = ['x', 'c', 'positions', 'g_norm', 'w_mod', 'b_mod', 'w_in', 'b_gate', 'conv_w', 'conv_b', 'w_a', 'b_a', 'w_x', 'b_x', 'lam', 'w_out_rnn', 'w_out_attn', 'w_o', 'g_final', 'loss_target', 'm_g_norm', 'm_w_mod', 'm_b_mod', 'm_w_in', 'm_b_gate', 'm_conv_w', 'm_conv_b', 'm_w_a', 'm_b_a', 'm_w_x', 'm_b_x', 'm_lam', 'm_w_out_rnn', 'm_w_out_attn', 'm_w_o', 'm_g_final', 'v_g_norm', 'v_w_mod', 'v_b_mod', 'v_w_in', 'v_b_gate', 'v_conv_w', 'v_conv_b', 'v_w_a', 'v_b_a', 'v_w_x', 'v_b_x', 'v_lam', 'v_w_out_rnn', 'v_w_out_attn', 'v_w_o', 'v_g_final']
TWIN_OUTPUTS = ['loss', 'grad_x', 'grad_g_norm', 'grad_w_mod', 'grad_b_mod', 'grad_w_in', 'grad_b_gate', 'grad_conv_w', 'grad_conv_b', 'grad_w_a', 'grad_b_a', 'grad_w_x', 'grad_b_x', 'grad_lam', 'grad_w_out_rnn', 'grad_w_out_attn', 'grad_w_o', 'grad_g_final', 'delta_g_norm', 'delta_w_mod', 'delta_b_mod', 'delta_w_in', 'delta_b_gate', 'delta_conv_w', 'delta_conv_b', 'delta_w_a', 'delta_b_a', 'delta_w_x', 'delta_b_x', 'delta_lam', 'delta_w_out_rnn', 'delta_w_out_attn', 'delta_w_o', 'delta_g_final', 'new_m_g_norm', 'new_m_w_mod', 'new_m_b_mod', 'new_m_w_in', 'new_m_b_gate', 'new_m_conv_w', 'new_m_conv_b', 'new_m_w_a', 'new_m_b_a', 'new_m_w_x', 'new_m_b_x', 'new_m_lam', 'new_m_w_out_rnn', 'new_m_w_out_attn', 'new_m_w_o', 'new_m_g_final', 'new_v_g_norm', 'new_v_w_mod', 'new_v_b_mod', 'new_v_w_in', 'new_v_b_gate', 'new_v_conv_w', 'new_v_conv_b', 'new_v_w_a', 'new_v_b_a', 'new_v_w_x', 'new_v_b_x', 'new_v_lam', 'new_v_w_out_rnn', 'new_v_w_out_attn', 'new_v_w_o', 'new_v_g_final']
TWIN_LEAF_KINDS = {'loss': 'loss', 'grad_x': 'grad_x', 'grad_g_norm': 'grad_w', 'grad_w_mod': 'grad_w', 'grad_b_mod': 'grad_w', 'grad_w_in': 'grad_w', 'grad_b_gate': 'grad_w', 'grad_conv_w': 'grad_w', 'grad_conv_b': 'grad_w', 'grad_w_a': 'grad_w', 'grad_b_a': 'grad_w', 'grad_w_x': 'grad_w', 'grad_b_x': 'grad_w', 'grad_lam': 'grad_w', 'grad_w_out_rnn': 'grad_w', 'grad_w_out_attn': 'grad_w', 'grad_w_o': 'grad_w', 'grad_g_final': 'grad_w', 'delta_g_norm': 'delta_w', 'delta_w_mod': 'delta_w', 'delta_b_mod': 'delta_w', 'delta_w_in': 'delta_w', 'delta_b_gate': 'delta_w', 'delta_conv_w': 'delta_w', 'delta_conv_b': 'delta_w', 'delta_w_a': 'delta_w', 'delta_b_a': 'delta_w', 'delta_w_x': 'delta_w', 'delta_b_x': 'delta_w', 'delta_lam': 'delta_w', 'delta_w_out_rnn': 'delta_w', 'delta_w_out_attn': 'delta_w', 'delta_w_o': 'delta_w', 'delta_g_final': 'delta_w', 'new_m_g_norm': 'new_m', 'new_m_w_mod': 'new_m', 'new_m_b_mod': 'new_m', 'new_m_w_in': 'new_m', 'new_m_b_gate': 'new_m', 'new_m_conv_w': 'new_m', 'new_m_conv_b': 'new_m', 'new_m_w_a': 'new_m', 'new_m_b_a': 'new_m', 'new_m_w_x': 'new_m', 'new_m_b_x': 'new_m', 'new_m_lam': 'new_m', 'new_m_w_out_rnn': 'new_m', 'new_m_w_out_attn': 'new_m', 'new_m_w_o': 'new_m', 'new_m_g_final': 'new_m', 'new_v_g_norm': 'new_v', 'new_v_w_mod': 'new_v', 'new_v_b_mod': 'new_v', 'new_v_w_in': 'new_v', 'new_v_b_gate': 'new_v', 'new_v_conv_w': 'new_v', 'new_v_conv_b': 'new_v', 'new_v_w_a': 'new_v', 'new_v_b_a': 'new_v', 'new_v_w_x': 'new_v', 'new_v_b_x': 'new_v', 'new_v_lam': 'new_v', 'new_v_w_out_rnn': 'new_v', 'new_v_w_out_attn': 'new_v', 'new_v_w_o': 'new_v', 'new_v_g_final': 'new_v'}


def _forward(args):
    return _fwd_reference(*[args[k] for k in FWD_PARAMS])


def _output_shape():
    out = _jax.eval_shape(lambda: _forward(_fwd_setup_inputs(0)))
    return out.shape, out.dtype

N_MICROBATCH = 1
ADAM_LR = 0.001
ADAM_B1 = 0.9
ADAM_B2 = 0.999
ADAM_EPS = 1e-08
ADAM_WD = 0.01
ADAM_STEP = 10
PER_EXAMPLE_BATCH_AXIS = {'x': 0, 'c': 0, 'positions': 0, 'loss_target': 0}
SHARED_INPUTS = []
_WEIGHT_DTYPES = {'g_norm': _jnp.float32, 'w_mod': _jnp.float32, 'b_mod': _jnp.float32, 'w_in': _jnp.float32, 'b_gate': _jnp.float32, 'conv_w': _jnp.float32, 'conv_b': _jnp.float32, 'w_a': _jnp.float32, 'b_a': _jnp.float32, 'w_x': _jnp.float32, 'b_x': _jnp.float32, 'lam': _jnp.float32, 'w_out_rnn': _jnp.float32, 'w_out_attn': _jnp.float32, 'w_o': _jnp.float32, 'g_final': _jnp.float32}
MOMENT_SCALE = {'g_norm': 9.917819e-02, 'w_mod': 1.378470e-01, 'b_mod': 2.234792e-01, 'w_in': 4.107958e-02, 'b_gate': 1.872678e-02, 'conv_w': 1.011874e-01, 'conv_b': 3.351325e-01, 'w_a': 8.394356e-03, 'b_a': 1.363135e-02, 'w_x': 1.719596e-02, 'b_x': 3.748110e-02, 'lam': 3.544274e-02, 'w_out_rnn': 7.792758e-02, 'w_out_attn': 1.137391e-02, 'w_o': 7.493246e-02, 'g_final': 6.408178e+01}


def _to_microbatches(a, axis):
    t = _jnp.moveaxis(a, axis, 0)
    t = t.reshape((N_MICROBATCH, t.shape[0] // N_MICROBATCH) + t.shape[1:])
    return _jnp.moveaxis(t, 1, axis + 1)


def setup_inputs(seed: int = 0) -> dict:
    inp = _fwd_setup_inputs(seed)
    key = _jax.random.fold_in(_jax.random.key(seed), 7919)
    shape, _ = _output_shape()
    out = dict(inp)
    out["loss_target"] = _jax.random.normal(_jax.random.fold_in(key, 0), shape, _jnp.float32)
    for i, name in enumerate(TWIN_WEIGHTS):
        w = inp[name].astype(_jnp.float32)
        if MOMENT_SCALE is None:
            s = _jnp.sqrt(_jnp.mean(_jnp.square(w)) + 1e-30)
        else:
            s = MOMENT_SCALE[name]
        km, kv = _jax.random.split(_jax.random.fold_in(key, i + 1))
        out[name] = w
        out["m_" + name] = s * _jax.random.normal(km, w.shape, _jnp.float32)
        out["v_" + name] = (s * s) * _jax.random.uniform(kv, w.shape, _jnp.float32, 0.5, 1.5)
    if N_MICROBATCH > 1:
        for name, axis in PER_EXAMPLE_BATCH_AXIS.items():
            out[name] = _to_microbatches(out[name], axis)
    return {'x': out['x'], 'c': out['c'], 'positions': out['positions'], 'g_norm': out['g_norm'], 'w_mod': out['w_mod'], 'b_mod': out['b_mod'], 'w_in': out['w_in'], 'b_gate': out['b_gate'], 'conv_w': out['conv_w'], 'conv_b': out['conv_b'], 'w_a': out['w_a'], 'b_a': out['b_a'], 'w_x': out['w_x'], 'b_x': out['b_x'], 'lam': out['lam'], 'w_out_rnn': out['w_out_rnn'], 'w_out_attn': out['w_out_attn'], 'w_o': out['w_o'], 'g_final': out['g_final'], 'loss_target': out['loss_target'], 'm_g_norm': out['m_g_norm'], 'm_w_mod': out['m_w_mod'], 'm_b_mod': out['m_b_mod'], 'm_w_in': out['m_w_in'], 'm_b_gate': out['m_b_gate'], 'm_conv_w': out['m_conv_w'], 'm_conv_b': out['m_conv_b'], 'm_w_a': out['m_w_a'], 'm_b_a': out['m_b_a'], 'm_w_x': out['m_w_x'], 'm_b_x': out['m_b_x'], 'm_lam': out['m_lam'], 'm_w_out_rnn': out['m_w_out_rnn'], 'm_w_out_attn': out['m_w_out_attn'], 'm_w_o': out['m_w_o'], 'm_g_final': out['m_g_final'], 'v_g_norm': out['v_g_norm'], 'v_w_mod': out['v_w_mod'], 'v_b_mod': out['v_b_mod'], 'v_w_in': out['v_w_in'], 'v_b_gate': out['v_b_gate'], 'v_conv_w': out['v_conv_w'], 'v_conv_b': out['v_conv_b'], 'v_w_a': out['v_w_a'], 'v_b_a': out['v_b_a'], 'v_w_x': out['v_w_x'], 'v_b_x': out['v_b_x'], 'v_lam': out['v_lam'], 'v_w_out_rnn': out['v_w_out_rnn'], 'v_w_out_attn': out['v_w_out_attn'], 'v_w_o': out['v_w_o'], 'v_g_final': out['v_g_final']}


def _loss(weights, diff, rest, loss_target):
    with _jax.named_scope("forward"):
        args = {**rest, TWIN_DIFF_INPUT: diff, **{k: w.astype(_WEIGHT_DTYPES[k]) for k, w in weights.items()}}
        y = _forward(args)
    with _jax.named_scope("loss_head"):
        err = _jnp.square(y.astype(_jnp.float32) - loss_target)
        return 0.5 * _jnp.sum(_jnp.mean(err, axis=-1)) if err.ndim else 0.5 * err


def _adamw(w, g, m, v):
    m = ADAM_B1 * m + (1.0 - ADAM_B1) * g
    v = ADAM_B2 * v + (1.0 - ADAM_B2) * _jnp.square(g)
    m_hat = m / (1.0 - ADAM_B1 ** ADAM_STEP)
    v_hat = v / (1.0 - ADAM_B2 ** ADAM_STEP)
    delta = -ADAM_LR * (m_hat / (_jnp.sqrt(v_hat) + ADAM_EPS) + ADAM_WD * w)
    return delta, m, v


def reference(x, c, positions, g_norm, w_mod, b_mod, w_in, b_gate, conv_w, conv_b, w_a, b_a, w_x, b_x, lam, w_out_rnn, w_out_attn, w_o, g_final, loss_target, m_g_norm, m_w_mod, m_b_mod, m_w_in, m_b_gate, m_conv_w, m_conv_b, m_w_a, m_b_a, m_w_x, m_b_x, m_lam, m_w_out_rnn, m_w_out_attn, m_w_o, m_g_final, v_g_norm, v_w_mod, v_b_mod, v_w_in, v_b_gate, v_conv_w, v_conv_b, v_w_a, v_b_a, v_w_x, v_b_x, v_lam, v_w_out_rnn, v_w_out_attn, v_w_o, v_g_final):
    given = dict(x=x, c=c, positions=positions, g_norm=g_norm, w_mod=w_mod, b_mod=b_mod, w_in=w_in, b_gate=b_gate, conv_w=conv_w, conv_b=conv_b, w_a=w_a, b_a=b_a, w_x=w_x, b_x=b_x, lam=lam, w_out_rnn=w_out_rnn, w_out_attn=w_out_attn, w_o=w_o, g_final=g_final, loss_target=loss_target, m_g_norm=m_g_norm, m_w_mod=m_w_mod, m_b_mod=m_b_mod, m_w_in=m_w_in, m_b_gate=m_b_gate, m_conv_w=m_conv_w, m_conv_b=m_conv_b, m_w_a=m_w_a, m_b_a=m_b_a, m_w_x=m_w_x, m_b_x=m_b_x, m_lam=m_lam, m_w_out_rnn=m_w_out_rnn, m_w_out_attn=m_w_out_attn, m_w_o=m_w_o, m_g_final=m_g_final, v_g_norm=v_g_norm, v_w_mod=v_w_mod, v_b_mod=v_b_mod, v_w_in=v_w_in, v_b_gate=v_b_gate, v_conv_w=v_conv_w, v_conv_b=v_conv_b, v_w_a=v_w_a, v_b_a=v_b_a, v_w_x=v_w_x, v_b_x=v_b_x, v_lam=v_lam, v_w_out_rnn=v_w_out_rnn, v_w_out_attn=v_w_out_attn, v_w_o=v_w_o, v_g_final=v_g_final)
    weights = {n: given[n] for n in TWIN_WEIGHTS}
    shared = {n: given[n] for n in SHARED_INPUTS}
    per_example = {n: given[n] for n in ['x', 'c', 'positions']}
    grad_fn = _jax.value_and_grad(_loss, argnums=(0, 1))

    def one_microbatch(ex, loss_target):
        ex = dict(ex)
        diff = ex.pop(TWIN_DIFF_INPUT)
        return grad_fn(weights, diff, {**shared, **ex}, loss_target)

    if N_MICROBATCH == 1:
        loss, (grad_w, grad_x) = one_microbatch(per_example, given["loss_target"])
    else:
        def body(carry, xs):
            loss_sum, grad_sum = carry
            l_k, (gw_k, gx_k) = one_microbatch(xs[0], xs[1])
            with _jax.named_scope("update"):
                return (loss_sum + l_k, _jax.tree.map(_jnp.add, grad_sum, gw_k)), gx_k

        init = (_jnp.zeros((), _jnp.float32), _jax.tree.map(_jnp.zeros_like, weights))
        (loss, grad_w), grad_x = _jax.lax.scan(body, init, (per_example, given["loss_target"]))
    with _jax.named_scope("update"):
        delta_w, new_m, new_v = {}, {}, {}
        for n in TWIN_WEIGHTS:
            delta_w[n], new_m[n], new_v[n] = _adamw(weights[n], grad_w[n], given["m_" + n], given["v_" + n])
    return (loss, grad_x, *[grad_w[n] for n in TWIN_WEIGHTS], *[delta_w[n] for n in TWIN_WEIGHTS],
            *[new_m[n] for n in TWIN_WEIGHTS], *[new_v[n] for n in TWIN_WEIGHTS])
```

```python
import functools

import jax
import jax.numpy as jnp
from jax import lax
from jax.experimental import pallas as pl
from jax.experimental.pallas import tpu as pltpu

F32 = jnp.float32
BF16 = jnp.bfloat16
MESH = pl.DeviceIdType.MESH

D_MODEL = 1024
N_HEADS = 8
HEAD_DIM = 128
RNN_BLOCKS = 8
N_DEV = 8
ROT_HALF = 16
ROPE_THETA = 500000.0
DILATIONS = (1, 4, 16)
KEY_BLOCK = 128
SPAN = KEY_BLOCK * DILATIONS[-1]
ATTN_SCALE = HEAD_DIM ** -0.5
NORM_EPS = 1e-6
LRU_C = 8.0
NEG_INF = -1e30
ADAM_LR, ADAM_B1, ADAM_B2, ADAM_EPS, ADAM_WD, ADAM_STEP = 0.001, 0.9, 0.999, 1e-08, 0.01, 10

SUBLANES = 8
VMEM_LIMIT = 56 * 1024 * 1024
PROJ_ROWS = 512
RNN_ROWS = 256
HUB_ROWS = 128
DX_ROWS = 256
WGRAD_ROWS = 1024
ADD_ROWS = 256


def _params(sem=None, vmem=None):
    return pltpu.CompilerParams(dimension_semantics=sem, vmem_limit_bytes=vmem)


def _dot(a, b):
    return jnp.dot(a, b, preferred_element_type=F32)


def _dot_nt(a, b):
    return lax.dot_general(a, b, (((1,), (1,)), ((), ())), preferred_element_type=F32)


def _dot_tn(a, b):
    return lax.dot_general(a, b, (((0,), (0,)), ((), ())), preferred_element_type=F32)


def _sigmoid(z):
    return 1.0 / (1.0 + jnp.exp(-z))


def _expm1_nonpos(z):
    series = z * (1.0 + z * (1.0 / 2) * (1.0 + z * (1.0 / 3) * (1.0 + z * (1.0 / 4) * (
        1.0 + z * (1.0 / 5) * (1.0 + z * (1.0 / 6))))))
    return jnp.where(z > -0.25, series, jnp.exp(z) - 1.0)


def _my_pos():
    return lax.axis_index("x"), lax.axis_index("y"), lax.axis_index("c")


def _flip(pos, k):
    x, y, c = pos
    return ((1 - x) if k & 4 else x, (1 - y) if k & 2 else y, (1 - c) if k & 1 else c)


def _index(pos):
    return 4 * pos[0] + 2 * pos[1] + pos[2]


def _ag_small(name, v):
    rows, cols = v.shape

    def body(v_ref, out_ref, send_sems, recv_sems):
        me = _my_pos()
        out_ref[_index(me)] = v_ref[...]
        sends = []
        for k in range(1, N_DEV):
            cp = pltpu.make_async_remote_copy(
                src_ref=v_ref, dst_ref=out_ref.at[_index(me)], send_sem=send_sems.at[k - 1],
                recv_sem=recv_sems.at[k - 1], device_id=_flip(me, k), device_id_type=MESH)
            cp.start()
            sends.append(cp)
        for k in range(1, N_DEV):
            peer = _flip(me, k)
            pltpu.make_async_remote_copy(
                src_ref=v_ref, dst_ref=out_ref.at[_index(peer)], send_sem=send_sems.at[k - 1],
                recv_sem=recv_sems.at[k - 1], device_id=peer, device_id_type=MESH).wait_recv()
        for cp in sends:
            cp.wait_send()

    return pl.pallas_call(
        body, name=name,
        out_shape=jax.ShapeDtypeStruct((N_DEV, rows, cols), v.dtype),
        in_specs=[pl.BlockSpec(memory_space=pltpu.VMEM)],
        out_specs=pl.BlockSpec(memory_space=pltpu.VMEM),
        scratch_shapes=[pltpu.SemaphoreType.DMA((N_DEV - 1,)), pltpu.SemaphoreType.DMA((N_DEV - 1,))],
        compiler_params=_params(None, VMEM_LIMIT),
    )(v)


def _ag_big(name, shards):
    n = len(shards)

    def body(*refs):
        ins, outs = refs[:n], refs[n:2 * n]
        send_sems, recv_sems, local_sems = refs[2 * n:]
        me = _my_pos()
        sib = _flip(me, 1)
        chips = [2, 4, 6]

        def copy(a, k, block, to, src=None):
            rows = outs[a].at[_index(block)]
            return pltpu.make_async_remote_copy(
                src_ref=rows if src is None else src, dst_ref=rows,
                send_sem=send_sems.at[a * 7 + k], recv_sem=recv_sems.at[a * 7 + k],
                device_id=to, device_id_type=MESH)

        started = []
        for a in range(n):
            mine = pltpu.make_async_copy(ins[a], outs[a].at[_index(me)], local_sems.at[a])
            mine.start()
            started.append(mine)
        sends = []
        for a in range(n):
            first = [copy(a, 0, me, sib, src=ins[a])]
            first += [copy(a, 1 + j, me, _flip(me, ch), src=ins[a]) for j, ch in enumerate(chips)]
            for cp in first:
                cp.start()
            sends += first
        for j, ch in enumerate(chips):
            for a in range(n):
                copy(a, 1 + j, _flip(me, ch), me).wait_recv()
                fwd = copy(a, 4 + j, _flip(me, ch), sib)
                fwd.start()
                sends.append(fwd)
        for a in range(n):
            copy(a, 0, sib, me).wait_recv()
            for j, ch in enumerate(chips):
                copy(a, 4 + j, _flip(sib, ch), me).wait_recv()
        for cp in sends:
            cp.wait_send()
        for mine in started:
            mine.wait()

    any_spec = pl.BlockSpec(memory_space=pl.ANY)
    return pl.pallas_call(
        body, name=name,
        out_shape=[jax.ShapeDtypeStruct((N_DEV,) + s.shape, s.dtype) for s in shards],
        in_specs=[any_spec] * n, out_specs=[any_spec] * n,
        scratch_shapes=[pltpu.SemaphoreType.DMA((7 * n,)), pltpu.SemaphoreType.DMA((7 * n,)),
                        pltpu.SemaphoreType.DMA((n,))],
    )(*shards)


def _rs_to_sibling(name, stacks):
    n = len(stacks)

    def body(*refs):
        ins, outs = refs[:n], refs[n:2 * n]
        send_sems, recv_sems = refs[2 * n:]
        me = _my_pos()
        sib = _flip(me, 1)
        sends = []
        for a in range(n):
            for m in range(4):
                target = _flip(sib, 2 * m)
                cp = pltpu.make_async_remote_copy(
                    src_ref=ins[a].at[_index(target)], dst_ref=outs[a].at[m],
                    send_sem=send_sems.at[a * 4 + m], recv_sem=recv_sems.at[a * 4 + m],
                    device_id=sib, device_id_type=MESH)
                cp.start()
                sends.append(cp)
        for cp in sends:
            cp.wait_recv()
        for cp in sends:
            cp.wait_send()

    any_spec = pl.BlockSpec(memory_space=pl.ANY)
    return pl.pallas_call(
        body, name=name,
        out_shape=[jax.ShapeDtypeStruct((4,) + s.shape[1:], s.dtype) for s in stacks],
        in_specs=[any_spec] * n, out_specs=[any_spec] * n,
        scratch_shapes=[pltpu.SemaphoreType.DMA((4 * n,)), pltpu.SemaphoreType.DMA((4 * n,))],
    )(*stacks)


def _rs_to_chips(name, sums):
    n = len(sums)

    def body(*refs):
        ins, outs = refs[:n], refs[n:2 * n]
        send_sems, recv_sems = refs[2 * n:]
        me = _my_pos()
        sends = []
        for a in range(n):
            for m in range(1, 4):
                cp = pltpu.make_async_remote_copy(
                    src_ref=ins[a].at[m], dst_ref=outs[a].at[m - 1],
                    send_sem=send_sems.at[a * 3 + m - 1], recv_sem=recv_sems.at[a * 3 + m - 1],
                    device_id=_flip(me, 2 * m), device_id_type=MESH)
                cp.start()
                sends.append(cp)
        for cp in sends:
            cp.wait_recv()
        for cp in sends:
            cp.wait_send()

    any_spec = pl.BlockSpec(memory_space=pl.ANY)
    return pl.pallas_call(
        body, name=name,
        out_shape=[jax.ShapeDtypeStruct((3,) + s.shape[1:], s.dtype) for s in sums],
        in_specs=[any_spec] * n, out_specs=[any_spec] * n,
        scratch_shapes=[pltpu.SemaphoreType.DMA((3 * n,)), pltpu.SemaphoreType.DMA((3 * n,))],
    )(*sums)


def _add_sibling(name, stack, recv, targets):
    _, rows, cols = stack.shape
    tr = min(rows, ADD_ROWS)

    def body(t_ref, a_ref, b_ref, o_ref):
        o_ref[...] = a_ref[...] + b_ref[...]

    return pl.pallas_call(
        body, name=name,
        out_shape=jax.ShapeDtypeStruct((4, rows, cols), F32),
        grid_spec=pltpu.PrefetchScalarGridSpec(
            num_scalar_prefetch=1, grid=(4, rows // tr),
            in_specs=[pl.BlockSpec((None, tr, cols), lambda m, i, t: (t[m], i, 0)),
                      pl.BlockSpec((None, tr, cols), lambda m, i, t: (m, i, 0))],
            out_specs=pl.BlockSpec((None, tr, cols), lambda m, i, t: (m, i, 0))),
        compiler_params=_params(("arbitrary", "arbitrary")),
    )(targets, stack, recv)


def _add_chips(name, sums, recv):
    _, rows, cols = sums.shape
    tr = min(rows, ADD_ROWS)

    def body(a_ref, b_ref, o_ref):
        o_ref[...] = ((a_ref[...] + b_ref[0]) + b_ref[1]) + b_ref[2]

    return pl.pallas_call(
        body, name=name,
        out_shape=jax.ShapeDtypeStruct((rows, cols), F32),
        grid=(rows // tr,),
        in_specs=[pl.BlockSpec((None, tr, cols), lambda i: (0, i, 0)),
                  pl.BlockSpec((3, tr, cols), lambda i: (0, i, 0))],
        out_specs=pl.BlockSpec((tr, cols), lambda i: (i, 0)),
        compiler_params=_params(("arbitrary",)),
    )(sums, recv)


def _sum_devices(name, gathered):
    _, rows, cols = gathered.shape

    def body(g_ref, o_ref):
        acc = g_ref[0]
        for k in range(1, N_DEV):
            acc = acc + g_ref[k]
        o_ref[...] = acc

    return pl.pallas_call(
        body, name=name, out_shape=jax.ShapeDtypeStruct((rows, cols), F32),
        compiler_params=_params(None, VMEM_LIMIT),
    )(gathered)


def _mod_fwd(c_all, w_mod):
    def body(c_ref, w_ref, o_ref):
        c = c_ref[...]
        o_ref[...] = jnp.dot(c * _sigmoid(c), w_ref[...], preferred_element_type=F32,
                             precision=lax.Precision.HIGHEST)

    return pl.pallas_call(
        body, name="mod_fwd", out_shape=jax.ShapeDtypeStruct((N_DEV, w_mod.shape[1]), F32),
    )(c_all, w_mod)


def _mod_bwd(c_all, dmod_all, dmod_cols):
    def body(c_ref, da_ref, dc_ref, gb_ref, gw_ref):
        c = c_ref[...]
        acc = da_ref[0:1, :]
        for b in range(1, N_DEV):
            acc = acc + da_ref[b:b + 1, :]
        gb_ref[...] = acc
        gw_ref[...] = lax.dot_general(c * _sigmoid(c), dc_ref[...], (((0,), (0,)), ((), ())),
                                      preferred_element_type=F32, precision=lax.Precision.HIGHEST)

    return pl.pallas_call(
        body, name="mod_bwd",
        out_shape=[jax.ShapeDtypeStruct((1, dmod_all.shape[1]), F32),
                   jax.ShapeDtypeStruct((c_all.shape[1], dmod_cols.shape[1]), F32)],
    )(c_all, dmod_all, dmod_cols)


def _rope_partner(t):
    lane = lax.broadcasted_iota(jnp.int32, t.shape, 1)
    return jnp.where(lane < ROT_HALF, pltpu.roll(t, HEAD_DIM - ROT_HALF, 1), pltpu.roll(t, ROT_HALF, 1))


def _norm_proj(x, mod, b_mod, g_norm, w_in_all, cosf, sinf):
    seq = x.shape[0]
    tm = PROJ_ROWS

    def body(x_ref, mod_ref, bmod_ref, g_ref, w_ref, cos_ref, sin_ref,
             h_ref, rstd_ref, pf_ref, q_ref, k_ref, v_ref, h_scr):
        j = pl.program_id(1)

        @pl.when(j == 0)
        def _():
            xf = x_ref[...]
            rstd = lax.rsqrt(jnp.mean(xf * xf, axis=-1, keepdims=True) + NORM_EPS)
            shift = mod_ref[:, 0:D_MODEL] + bmod_ref[:, 0:D_MODEL]
            scale = mod_ref[:, D_MODEL:2 * D_MODEL] + bmod_ref[:, D_MODEL:2 * D_MODEL]
            hb = (((xf * rstd) * g_ref[...]) * (1.0 + scale) + shift).astype(BF16)
            h_scr[...] = hb
            h_ref[...] = hb
            rstd_ref[...] = rstd

        acc = _dot(h_scr[...], w_ref[...])

        @pl.when((j < 2) | (j > 4))
        def _():
            pf_ref[...] = acc

        def heads(dst_ref, rotate):
            for hh in range(N_HEADS):
                t = acc[:, hh * HEAD_DIM:(hh + 1) * HEAD_DIM]
                if rotate:
                    t = t * cos_ref[...] + _rope_partner(t) * sin_ref[...]
                dst_ref[hh] = t.astype(BF16)

        @pl.when(j == 2)
        def _():
            heads(q_ref, True)

        @pl.when(j == 3)
        def _():
            heads(k_ref, True)

        @pl.when(j == 4)
        def _():
            heads(v_ref, False)

    def pf_slot(i, j):
        return (i, jnp.where(j < 2, j, jnp.where(j < 5, 1, j - 3)))

    hm = jax.ShapeDtypeStruct((N_HEADS, seq, HEAD_DIM), BF16)
    hm_spec = pl.BlockSpec((N_HEADS, tm, HEAD_DIM), lambda i, j: (0, i, 0))
    row = lambda i, j: (i, 0)
    const = lambda i, j: (0, 0)
    return pl.pallas_call(
        body, name="norm_proj",
        out_shape=[jax.ShapeDtypeStruct((seq, D_MODEL), BF16), jax.ShapeDtypeStruct((seq, 1), F32),
                   jax.ShapeDtypeStruct((seq, 5 * D_MODEL), F32), hm, hm, hm],
        grid=(seq // tm, 8),
        in_specs=[pl.BlockSpec((tm, D_MODEL), row), pl.BlockSpec((1, 3 * D_MODEL), const),
                  pl.BlockSpec((1, 3 * D_MODEL), const), pl.BlockSpec((1, D_MODEL), const),
                  pl.BlockSpec((None, D_MODEL, D_MODEL), lambda i, j: (j, 0, 0)),
                  pl.BlockSpec((tm, HEAD_DIM), row), pl.BlockSpec((tm, HEAD_DIM), row)],
        out_specs=[pl.BlockSpec((tm, D_MODEL), row), pl.BlockSpec((tm, 1), row),
                   pl.BlockSpec((tm, D_MODEL), pf_slot), hm_spec, hm_spec, hm_spec],
        scratch_shapes=[pltpu.VMEM((tm, D_MODEL), BF16)],
        compiler_params=_params(("arbitrary", "arbitrary"), VMEM_LIMIT),
    )(x, mod, b_mod, g_norm, w_in_all, cosf, sinf)


def _shift_down(v, s, head):
    rows = v.shape[0]
    row = lax.broadcasted_iota(jnp.int32, v.shape, 0)
    fill = jnp.concatenate([pltpu.roll(head, s, 0), jnp.zeros((rows - SUBLANES, v.shape[1]), v.dtype)], axis=0)
    return jnp.where(row < s, fill, pltpu.roll(v, s, 0))


def _shift_up(v, s, tail):
    rows = v.shape[0]
    row = lax.broadcasted_iota(jnp.int32, v.shape, 0)
    fill = jnp.concatenate([jnp.zeros((rows - SUBLANES, v.shape[1]), v.dtype),
                            pltpu.roll(tail, SUBLANES - s, 0)], axis=0)
    return jnp.where(row >= rows - s, fill, pltpu.roll(v, rows - s, 0))


def _scan_fwd(a, b):
    rows = a.shape[0]
    row = lax.broadcasted_iota(jnp.int32, a.shape, 0)
    k = 1
    while k < rows:
        a_s = jnp.where(row >= k, pltpu.roll(a, k, 0), 1.0)
        b_s = jnp.where(row >= k, pltpu.roll(b, k, 0), 0.0)
        b = a * b_s + b
        a = a * a_s
        k *= 2
    return a, b


def _scan_rev(a, b):
    rows = a.shape[0]
    row = lax.broadcasted_iota(jnp.int32, a.shape, 0)
    k = 1
    while k < rows:
        a_s = jnp.where(row < rows - k, pltpu.roll(a, rows - k, 0), 1.0)
        b_s = jnp.where(row < rows - k, pltpu.roll(b, rows - k, 0), 0.0)
        b = a * b_s + b
        a = a * a_s
        k *= 2
    return b


def _conv_taps(xr, head):
    return [_shift_down(xr, 3, head), _shift_down(xr, 2, head), _shift_down(xr, 1, head), xr]


def _rnn_gates(xc, wa, ba, wx, bx, lam, keep):
    xcb = xc.astype(BF16)
    r = _sigmoid(_dot(xcb, wa.astype(BF16)) + ba)
    i = _sigmoid(_dot(xcb, wx.astype(BF16)) + bx)
    softplus = jnp.maximum(-lam, 0.0) + jnp.log(1.0 + jnp.exp(-jnp.abs(lam)))
    cl = -LRU_C * softplus
    log_a = cl * r
    a_raw = jnp.exp(log_a)
    mult_raw = jnp.sqrt(-_expm1_nonpos(2.0 * log_a))
    live = keep > 0.0
    return r, i, cl, a_raw, mult_raw, jnp.where(live, a_raw, 0.0), jnp.where(live, mult_raw, 1.0), live


def _rnn_specs(seq, rows, time_of):
    per = rows // SUBLANES
    vec = pl.BlockSpec((None, 1, 128), lambda hb, n: (hb, 0, 0))
    mat = pl.BlockSpec((None, 128, 128), lambda hb, n: (hb, 0, 0))
    return [pl.BlockSpec((rows, 128), lambda hb, n: (time_of(n), hb)),
            pl.BlockSpec((SUBLANES, 128), lambda hb, n: (jnp.maximum(time_of(n) * per - 1, 0), hb)),
            pl.BlockSpec((rows, 1), lambda hb, n: (time_of(n), 0)),
            pl.BlockSpec((None, SUBLANES, 128), lambda hb, n: (hb, 0, 0)),
            vec, mat, vec, mat, vec, vec]


def _rnn_fwd(pf, keep, conv_w8, conv_b, w_a, b_a, w_x, b_x, lam):
    seq = pf.shape[0]
    rows = RNN_ROWS

    def body(x_ref, xh_ref, keep_ref, cw_ref, cb_ref, wa_ref, ba_ref, wx_ref, bx_ref, lam_ref, hr_ref, carry):
        n = pl.program_id(1)

        @pl.when(n == 0)
        def _():
            carry[...] = jnp.zeros_like(carry)

        xr = x_ref[...]
        head = jnp.where(n > 0, xh_ref[...], 0.0)
        taps = _conv_taps(xr, head)
        xc = cb_ref[...] + sum(cw_ref[k:k + 1, :] * taps[k] for k in range(4))
        _, i, _, _, _, a, mult, _ = _rnn_gates(xc, wa_ref[...], ba_ref[...], wx_ref[...], bx_ref[...],
                                               lam_ref[...], keep_ref[...])
        a_cum, h_loc = _scan_fwd(a, mult * i * xc)
        h = h_loc + a_cum * carry[SUBLANES - 1:SUBLANES, :]
        hr_ref[...] = h
        carry[...] = h[rows - SUBLANES:rows, :]

    return pl.pallas_call(
        body, name="rnn_fwd",
        out_shape=jax.ShapeDtypeStruct((seq, D_MODEL), F32),
        grid=(RNN_BLOCKS, seq // rows),
        in_specs=_rnn_specs(seq, rows, lambda n: n),
        out_specs=pl.BlockSpec((rows, 128), lambda hb, n: (n, hb)),
        scratch_shapes=[pltpu.VMEM((SUBLANES, 128), F32)],
        compiler_params=_params(("arbitrary", "arbitrary"), VMEM_LIMIT),
    )(pf, pf, keep, conv_w8, conv_b, w_a, b_a, w_x, b_x, lam)


def _rnn_bwd(pf, hr, dhr, keep, conv_w8, conv_b, w_a, b_a, w_x, b_x, lam):
    seq = pf.shape[0]
    rows = RNN_ROWS
    nchunk = seq // rows
    per = rows // SUBLANES
    time_of = lambda n: nchunk - 1 - n

    def body(x_ref, xh_ref, keep_ref, cw_ref, cb_ref, wa_ref, ba_ref, wx_ref, bx_ref, lam_ref,
             hr_ref, hrh_ref, dhr_ref,
             dx_ref, gcw_ref, gcb_ref, gwa_ref, gba_ref, gwx_ref, gbx_ref, glam_ref,
             g_carry, dxc_tail):
        n = pl.program_id(1)
        first_in_time = n == nchunk - 1

        @pl.when(n == 0)
        def _():
            g_carry[...] = jnp.zeros_like(g_carry)
            dxc_tail[...] = jnp.zeros_like(dxc_tail)
            for ref in (gcw_ref, gcb_ref, gwa_ref, gba_ref, gwx_ref, gbx_ref, glam_ref):
                ref[...] = jnp.zeros_like(ref)

        xr = x_ref[...]
        head = jnp.where(first_in_time, 0.0, xh_ref[...])
        taps = _conv_taps(xr, head)
        cw = cw_ref[...]
        xc = cb_ref[...] + sum(cw[k:k + 1, :] * taps[k] for k in range(4))
        wa, wx, lam = wa_ref[...], wx_ref[...], lam_ref[...]
        r, i, cl, a_raw, mult_raw, a, mult, live = _rnn_gates(xc, wa, ba_ref[...], wx, bx_ref[...], lam,
                                                               keep_ref[...])
        h_prev = _shift_down(hr_ref[...], 1, jnp.where(first_in_time, 0.0, hrh_ref[...]))

        row = lax.broadcasted_iota(jnp.int32, xr.shape, 0)
        last = row == rows - 1
        a_next = jnp.where(last, 0.0, pltpu.roll(a, rows - 1, 0))
        g = _scan_rev(a_next, dhr_ref[...] + jnp.where(last, g_carry[0:1, :], 0.0))
        g_carry[...] = jnp.broadcast_to(a[0:1, :] * g[0:1, :], g_carry.shape)

        da = g * h_prev
        dmult = g * i * xc
        di = g * mult * xc
        dxc = g * mult * i
        dlog_a = jnp.where(live, da * a_raw - dmult * a_raw * a_raw / mult_raw, 0.0)
        dpa = (dlog_a * cl) * r * (1.0 - r)
        dpx = di * i * (1.0 - i)
        glam_ref[...] += jnp.sum(dlog_a * r, axis=0, keepdims=True) * (LRU_C * _sigmoid(-lam))
        xcb, dpab, dpxb = xc.astype(BF16), dpa.astype(BF16), dpx.astype(BF16)
        gwa_ref[...] += _dot_tn(xcb, dpab)
        gwx_ref[...] += _dot_tn(xcb, dpxb)
        gba_ref[...] += jnp.sum(dpa, axis=0, keepdims=True)
        gbx_ref[...] += jnp.sum(dpx, axis=0, keepdims=True)
        dxc = dxc + _dot_nt(dpab, wa.astype(BF16)) + _dot_nt(dpxb, wx.astype(BF16))

        gcb_ref[...] += jnp.sum(dxc, axis=0, keepdims=True)
        for k in range(4):
            gcw_ref[k:k + 1, :] += jnp.sum(dxc * taps[k], axis=0, keepdims=True)
        tail = dxc_tail[...]
        dx = cw[3:4, :] * dxc
        for k in range(3):
            dx = dx + cw[k:k + 1, :] * _shift_up(dxc, 3 - k, tail)
        dx_ref[...] = dx.astype(BF16)
        dxc_tail[...] = dxc[0:SUBLANES, :]

    blk = lambda hb, n: (hb, 0, 0)
    chunk = pl.BlockSpec((rows, 128), lambda hb, n: (time_of(n), hb))
    vec_out = pl.BlockSpec((None, 1, 128), blk)
    mat_out = pl.BlockSpec((None, 128, 128), blk)
    vec_shape = jax.ShapeDtypeStruct((RNN_BLOCKS, 1, 128), F32)
    mat_shape = jax.ShapeDtypeStruct((RNN_BLOCKS, 128, 128), F32)
    return pl.pallas_call(
        body, name="rnn_bwd",
        out_shape=[jax.ShapeDtypeStruct((seq, D_MODEL), BF16),
                   jax.ShapeDtypeStruct((RNN_BLOCKS, SUBLANES, 128), F32), vec_shape,
                   mat_shape, vec_shape, mat_shape, vec_shape, vec_shape],
        grid=(RNN_BLOCKS, nchunk),
        in_specs=_rnn_specs(seq, rows, time_of) + [
            chunk, pl.BlockSpec((SUBLANES, 128), lambda hb, n: (jnp.maximum(time_of(n) * per - 1, 0), hb)), chunk],
        out_specs=[chunk, pl.BlockSpec((None, SUBLANES, 128), blk), vec_out,
                   mat_out, vec_out, mat_out, vec_out, vec_out],
        scratch_shapes=[pltpu.VMEM((SUBLANES, 128), F32), pltpu.VMEM((SUBLANES, 128), F32)],
        compiler_params=_params(("arbitrary", "arbitrary"), VMEM_LIMIT),
    )(pf, pf, keep, conv_w8, conv_b, w_a, b_a, w_x, b_x, lam, hr, hr, dhr)


def _band_masks():
    qi = lax.broadcasted_iota(jnp.int32, (KEY_BLOCK, KEY_BLOCK), 0)
    ki = lax.broadcasted_iota(jnp.int32, (KEY_BLOCK, KEY_BLOCK), 1)
    return ki <= qi, ki >= qi


def _attn_fwd(name, q, k, v, dil, state):
    nh, seq, _ = q.shape
    rows, width = SPAN // dil, dil * HEAD_DIM
    nb = rows // KEY_BLOCK

    def body(*refs):
        q_ref, k_ref, v_ref, kp_ref, vp_ref = refs[:5]
        if state is None:
            o_ref, l_ref = refs[5:]
        else:
            op_ref, lp_ref, o_ref, l_ref = refs[5:]
        n = pl.program_id(1)
        own, before = _band_masks()
        for r in range(dil):
            cs = slice(r * HEAD_DIM, (r + 1) * HEAD_DIM)
            for j in range(nb):
                rs = slice(j * KEY_BLOCK, (j + 1) * KEY_BLOCK)
                qb, kc, vc = q_ref[rs, cs], k_ref[rs, cs], v_ref[rs, cs]
                if j == 0:
                    kp, vp = kp_ref[:, cs], vp_ref[:, cs]
                    pmask = jnp.logical_and(before, n > 0)
                else:
                    ps = slice((j - 1) * KEY_BLOCK, j * KEY_BLOCK)
                    kp, vp, pmask = k_ref[ps, cs], v_ref[ps, cs], before
                sc = jnp.where(own, _dot_nt(qb, kc) * ATTN_SCALE, NEG_INF)
                sp = jnp.where(pmask, _dot_nt(qb, kp) * ATTN_SCALE, NEG_INF)
                m = jnp.maximum(jnp.max(sc, axis=-1, keepdims=True), jnp.max(sp, axis=-1, keepdims=True))
                if state is not None:
                    lp = lp_ref[rs, r:r + 1]
                    m = jnp.maximum(m, lp)
                pc, pp = jnp.exp(sc - m), jnp.exp(sp - m)
                den = jnp.sum(pc, axis=-1, keepdims=True) + jnp.sum(pp, axis=-1, keepdims=True)
                acc = _dot(pc.astype(BF16), vc) + _dot(pp.astype(BF16), vp)
                if state is not None:
                    wp = jnp.exp(lp - m)
                    den = den + wp
                    acc = acc + wp * op_ref[rs, cs]
                o_ref[rs, cs] = acc * (1.0 / den)
                l_ref[rs, r:r + 1] = m + jnp.log(den)

    view = lambda t: t.reshape(nh, seq // dil, width)
    blk = pl.BlockSpec((None, rows, width), lambda h, n: (h, n, 0))
    pblk = pl.BlockSpec((None, KEY_BLOCK, width), lambda h, n: (h, jnp.maximum(n * nb - 1, 0), 0))
    sblk = pl.BlockSpec((None, rows, dil), lambda h, n: (h, n, 0))
    args = [view(q), view(k), view(v), view(k), view(v)]
    specs = [blk, blk, blk, pblk, pblk]
    if state is not None:
        args += [view(state[0]), state[1].reshape(nh, seq // dil, dil)]
        specs += [blk, sblk]
    o, lse = pl.pallas_call(
        body, name=name,
        out_shape=[jax.ShapeDtypeStruct((nh, seq // dil, width), F32),
                   jax.ShapeDtypeStruct((nh, seq // dil, dil), F32)],
        grid=(nh, seq // SPAN), in_specs=specs, out_specs=[blk, sblk],
        compiler_params=_params(("arbitrary", "arbitrary"), VMEM_LIMIT),
    )(*args)
    return o.reshape(nh, seq, HEAD_DIM), lse.reshape(nh, seq, 1)


def _attn_bwd(name, q, k, v, do, lse, delta, dil, state):
    nh, seq, _ = q.shape
    rows, width = SPAN // dil, dil * HEAD_DIM
    nb = rows // KEY_BLOCK
    nchunk = seq // SPAN

    def body(*refs):
        (q_ref, k_ref, v_ref, do_ref, l_ref, dl_ref, kp_ref, vp_ref,
         qn_ref, don_ref, ln_ref, dln_ref) = refs[:12]
        if state is None:
            dq_ref, dk_ref, dv_ref = refs[12:]
            for ref in (dq_ref, dk_ref, dv_ref):
                ref[...] = jnp.zeros_like(ref)
        else:
            dqp_ref, dkp_ref, dvp_ref, dq_ref, dk_ref, dv_ref = refs[12:]
            dq_ref[...] = dqp_ref[...]
            dk_ref[...] = dkp_ref[...]
            dv_ref[...] = dvp_ref[...]
        n = pl.program_id(1)
        own, before = _band_masks()

        def tile(qb, dob, lq, dlq, kb, vb, mask):
            s = jnp.where(mask, _dot_nt(qb, kb) * ATTN_SCALE, NEG_INF)
            p = jnp.exp(s - lq)
            ds = p * (_dot_nt(dob, vb) - dlq) * ATTN_SCALE
            return p.astype(BF16), ds.astype(BF16)

        for r in range(dil):
            cs = slice(r * HEAD_DIM, (r + 1) * HEAD_DIM)
            for j in range(nb):
                rs = slice(j * KEY_BLOCK, (j + 1) * KEY_BLOCK)
                qb, dob, kc, vc = q_ref[rs, cs], do_ref[rs, cs], k_ref[rs, cs], v_ref[rs, cs]
                lq, dlq = l_ref[rs, r:r + 1], dl_ref[rs, r:r + 1]
                p, ds = tile(qb, dob, lq, dlq, kc, vc, own)
                dv_ref[rs, cs] += _dot_tn(p, dob)
                dk_ref[rs, cs] += _dot_tn(ds, qb)
                dq_ref[rs, cs] += _dot(ds, kc)
                if j == 0:
                    p, ds = tile(qb, dob, lq, dlq, kp_ref[:, cs], vp_ref[:, cs], jnp.logical_and(before, n > 0))
                    dq_ref[rs, cs] += _dot(ds, kp_ref[:, cs])
                else:
                    ps = slice((j - 1) * KEY_BLOCK, j * KEY_BLOCK)
                    p, ds = tile(qb, dob, lq, dlq, k_ref[ps, cs], v_ref[ps, cs], before)
                    dq_ref[rs, cs] += _dot(ds, k_ref[ps, cs])
                    dv_ref[ps, cs] += _dot_tn(p, dob)
                    dk_ref[ps, cs] += _dot_tn(ds, qb)
            ls = slice((nb - 1) * KEY_BLOCK, nb * KEY_BLOCK)
            qn, don = qn_ref[:, cs], don_ref[:, cs]
            p, ds = tile(qn, don, ln_ref[:, r:r + 1], dln_ref[:, r:r + 1], k_ref[ls, cs], v_ref[ls, cs],
                         jnp.logical_and(before, n < nchunk - 1))
            dv_ref[ls, cs] += _dot_tn(p, don)
            dk_ref[ls, cs] += _dot_tn(ds, qn)

    view = lambda t: t.reshape(nh, seq // dil, width)
    sview = lambda t: t.reshape(nh, seq // dil, dil)
    last_block = seq // dil // KEY_BLOCK - 1
    blk = pl.BlockSpec((None, rows, width), lambda h, n: (h, n, 0))
    sblk = pl.BlockSpec((None, rows, dil), lambda h, n: (h, n, 0))
    prev = lambda h, n: (h, jnp.maximum(n * nb - 1, 0), 0)
    nxt = lambda h, n: (h, jnp.minimum((n + 1) * nb, last_block), 0)
    args = [view(q), view(k), view(v), view(do), sview(lse), sview(delta), view(k), view(v),
            view(q), view(do), sview(lse), sview(delta)]
    specs = [blk, blk, blk, blk, sblk, sblk,
             pl.BlockSpec((None, KEY_BLOCK, width), prev), pl.BlockSpec((None, KEY_BLOCK, width), prev),
             pl.BlockSpec((None, KEY_BLOCK, width), nxt), pl.BlockSpec((None, KEY_BLOCK, width), nxt),
             pl.BlockSpec((None, KEY_BLOCK, dil), nxt), pl.BlockSpec((None, KEY_BLOCK, dil), nxt)]
    if state is not None:
        args += [view(t) for t in state]
        specs += [blk, blk, blk]
    shape = jax.ShapeDtypeStruct((nh, seq // dil, width), F32)
    outs = pl.pallas_call(
        body, name=name, out_shape=[shape, shape, shape],
        grid=(nh, nchunk), in_specs=specs, out_specs=[blk, blk, blk],
        compiler_params=_params(("arbitrary", "arbitrary"), VMEM_LIMIT),
    )(*args)
    return tuple(t.reshape(nh, seq, HEAD_DIM) for t in outs)


def _dqkv_finish(dq, dk, dv, cosf, sinf):
    nh, seq, _ = dq.shape
    tm = DX_ROWS

    def body(dq_ref, dk_ref, dv_ref, cos_ref, sin_ref, oq_ref, ok_ref, ov_ref):
        for src, dst, rotate in ((dq_ref, oq_ref, True), (dk_ref, ok_ref, True), (dv_ref, ov_ref, False)):
            for hh in range(nh):
                t = src[hh]
                if rotate:
                    t = t * cos_ref[...] - _rope_partner(t) * sin_ref[...]
                dst[:, hh * HEAD_DIM:(hh + 1) * HEAD_DIM] = t.astype(BF16)

    hm = pl.BlockSpec((nh, tm, HEAD_DIM), lambda i: (0, i, 0))
    tab = pl.BlockSpec((tm, HEAD_DIM), lambda i: (i, 0))
    out = pl.BlockSpec((tm, D_MODEL), lambda i: (i, 0))
    shape = jax.ShapeDtypeStruct((seq, D_MODEL), BF16)
    return pl.pallas_call(
        body, name="dqkv_finish", out_shape=[shape, shape, shape], grid=(seq // tm,),
        in_specs=[hm, hm, hm, tab, tab], out_specs=[out, out, out],
        compiler_params=_params(("arbitrary",), VMEM_LIMIT),
    )(dq, dk, dv, cosf, sinf)


def _hub(x, tgt, hr, pf, o_hm, mod, b_mod, b_gate, g_final, w_out_rnn, w_out_attn, w_o):
    seq = x.shape[0]
    tm = HUB_ROWS
    nsteps = seq // tm

    def body(x_ref, t_ref, hr_ref, zr_ref, za_ref, gr_ref, ga_ref, o_ref, mod_ref, bmod_ref, bg_ref, gf_ref,
             wr_hbm, wa_hbm, wo_hbm,
             dx2_ref, dhr_ref, dzr_ref, do_ref, dza_ref, dgr_ref, dga_ref, delta_ref,
             ggf_ref, gbg_ref, dgate_ref, loss_ref, gwr_hbm, gwa_hbm, gwo_hbm,
             wr, wa, wo, gwr, gwa, gwo, sem):
        step = pl.program_id(0)

        @pl.when(step == 0)
        def _():
            for src, dst in ((wr_hbm, wr), (wa_hbm, wa), (wo_hbm, wo)):
                cp = pltpu.make_async_copy(src, dst, sem)
                cp.start()
                cp.wait()
            for ref in (gwr, gwa, gwo, ggf_ref, gbg_ref, dgate_ref, loss_ref):
                ref[...] = jnp.zeros_like(ref)

        gate = mod_ref[:, 2 * D_MODEL:] + bmod_ref[:, 2 * D_MODEL:]
        gfin = gf_ref[...]
        hr_t, zr, za = hr_ref[...], zr_ref[...], za_ref[...]
        o = jnp.concatenate([o_ref[hh] for hh in range(N_HEADS)], axis=1)
        sig_zr, sig_za = _sigmoid(zr), _sigmoid(za)
        silu_zr, silu_za = zr * sig_zr, za * sig_za
        u_rnn = (hr_t * silu_zr).astype(BF16)
        u_attn = (o * silu_za).astype(BF16)
        y_rnn = _dot(u_rnn, wr[...])
        y_attn = _dot(u_attn, wa[...])
        sr = _sigmoid(gr_ref[...] + bg_ref[:, :D_MODEL])
        sa = _sigmoid(ga_ref[...] + bg_ref[:, D_MODEL:])
        merged = (sr * y_rnn + sa * y_attn).astype(BF16)
        mo = _dot(merged, wo[...])
        x2 = x_ref[...] + gate * mo
        rstd = lax.rsqrt(jnp.mean(x2 * x2, axis=-1, keepdims=True) + NORM_EPS)
        xn = x2 * rstd
        err = xn * gfin - t_ref[...]
        loss_ref[...] += 0.5 * jnp.sum(jnp.sum(err * err, axis=-1, keepdims=True) * (1.0 / D_MODEL),
                                       axis=0, keepdims=True)

        dy = err * (1.0 / D_MODEL)
        ggf_ref[...] += jnp.sum(dy * xn, axis=0, keepdims=True)
        dxn = dy * gfin
        dx2 = rstd * (dxn - xn * jnp.mean(dxn * xn, axis=-1, keepdims=True))
        dx2_ref[...] = dx2
        dgate_ref[...] += jnp.sum(dx2 * mo, axis=0, keepdims=True)
        dmo = (dx2 * gate).astype(BF16)
        dmerged = _dot_nt(dmo, wo[...])
        gwo[...] += _dot_tn(merged, dmo)
        dy_rnn = (dmerged * sr).astype(BF16)
        dy_attn = (dmerged * sa).astype(BF16)
        dg_r = dmerged * y_rnn * sr * (1.0 - sr)
        dg_a = dmerged * y_attn * sa * (1.0 - sa)
        dgr_ref[...] = dg_r.astype(BF16)
        dga_ref[...] = dg_a.astype(BF16)
        gbg_ref[:, :D_MODEL] += jnp.sum(dg_r, axis=0, keepdims=True)
        gbg_ref[:, D_MODEL:] += jnp.sum(dg_a, axis=0, keepdims=True)
        du_rnn = _dot_nt(dy_rnn, wr[...])
        gwr[...] += _dot_tn(u_rnn, dy_rnn)
        du_attn = _dot_nt(dy_attn, wa[...])
        gwa[...] += _dot_tn(u_attn, dy_attn)
        dhr_ref[...] = du_rnn * silu_zr
        dzr_ref[...] = (du_rnn * hr_t * (sig_zr * (1.0 + zr * (1.0 - sig_zr)))).astype(BF16)
        dza_ref[...] = (du_attn * o * (sig_za * (1.0 + za * (1.0 - sig_za)))).astype(BF16)
        d_o = du_attn * silu_za
        for hh in range(N_HEADS):
            cs = slice(hh * HEAD_DIM, (hh + 1) * HEAD_DIM)
            do_ref[hh] = d_o[:, cs].astype(BF16)
            delta_ref[hh] = jnp.sum(d_o[:, cs] * o[:, cs], axis=-1, keepdims=True)

        @pl.when(step == nsteps - 1)
        def _():
            for src, dst in ((gwr, gwr_hbm), (gwa, gwa_hbm), (gwo, gwo_hbm)):
                cp = pltpu.make_async_copy(src, dst, sem)
                cp.start()
                cp.wait()

    row = pl.BlockSpec((tm, D_MODEL), lambda i: (i, 0))
    piece = lambda slot: pl.BlockSpec((tm, D_MODEL), lambda i: (i, slot))
    hm = pl.BlockSpec((N_HEADS, tm, HEAD_DIM), lambda i: (0, i, 0))
    const = lambda cols: pl.BlockSpec((1, cols), lambda i: (0, 0))
    any_spec = pl.BlockSpec(memory_space=pl.ANY)
    act_f32 = jax.ShapeDtypeStruct((seq, D_MODEL), F32)
    act_bf16 = jax.ShapeDtypeStruct((seq, D_MODEL), BF16)
    wshape = jax.ShapeDtypeStruct((D_MODEL, D_MODEL), F32)
    return pl.pallas_call(
        body, name="hub",
        out_shape=[act_f32, act_f32, act_bf16, jax.ShapeDtypeStruct((N_HEADS, seq, HEAD_DIM), BF16),
                   act_bf16, act_bf16, act_bf16, jax.ShapeDtypeStruct((N_HEADS, seq, 1), F32),
                   jax.ShapeDtypeStruct((1, D_MODEL), F32), jax.ShapeDtypeStruct((1, 2 * D_MODEL), F32),
                   jax.ShapeDtypeStruct((1, D_MODEL), F32), jax.ShapeDtypeStruct((1, 1), F32),
                   wshape, wshape, wshape],
        grid=(nsteps,),
        in_specs=[row, row, row, piece(1), piece(2), piece(3), piece(4), hm,
                  const(3 * D_MODEL), const(3 * D_MODEL), const(2 * D_MODEL), const(D_MODEL),
                  any_spec, any_spec, any_spec],
        out_specs=[row, row, row, hm, row, row, row, pl.BlockSpec((N_HEADS, tm, 1), lambda i: (0, i, 0)),
                   const(D_MODEL), const(2 * D_MODEL), const(D_MODEL), const(1),
                   any_spec, any_spec, any_spec],
        scratch_shapes=[pltpu.VMEM((D_MODEL, D_MODEL), BF16)] * 3 + [pltpu.VMEM((D_MODEL, D_MODEL), F32)] * 3
                       + [pltpu.SemaphoreType.DMA],
        compiler_params=_params(("arbitrary",), VMEM_LIMIT),
    )(x, tgt, hr, pf, pf, pf, pf, o_hm, mod, b_mod, b_gate, g_final, w_out_rnn, w_out_attn, w_o)


def _dh_dx(pieces, w_in_all, x, rstd, dx2, mod, b_mod, g_norm):
    seq = x.shape[0]
    tm = DX_ROWS

    def body(*refs):
        p_refs = refs[:8]
        w_hbm, x_ref, rstd_ref, dx2_ref, mod_ref, bmod_ref, g_ref = refs[8:15]
        gx_ref, dshift_ref, dscale_ref, ggn_ref, w_scr, sem = refs[15:]
        step = pl.program_id(0)

        @pl.when(step == 0)
        def _():
            cp = pltpu.make_async_copy(w_hbm, w_scr, sem)
            cp.start()
            cp.wait()
            for ref in (dshift_ref, dscale_ref, ggn_ref):
                ref[...] = jnp.zeros_like(ref)

        dh = _dot_nt(p_refs[0][...], w_scr[0])
        for j in range(1, 8):
            dh = dh + _dot_nt(p_refs[j][...], w_scr[j])
        scale1 = 1.0 + mod_ref[:, D_MODEL:2 * D_MODEL] + bmod_ref[:, D_MODEL:2 * D_MODEL]
        g = g_ref[...]
        rstd_t = rstd_ref[...]
        xn = x_ref[...] * rstd_t
        dshift_ref[...] += jnp.sum(dh, axis=0, keepdims=True)
        dscale_ref[...] += jnp.sum(dh * (xn * g), axis=0, keepdims=True)
        ggn_ref[...] += jnp.sum(dh * scale1 * xn, axis=0, keepdims=True)
        dxn = dh * (g * scale1)
        gx_ref[...] = rstd_t * (dxn - xn * jnp.mean(dxn * xn, axis=-1, keepdims=True)) + dx2_ref[...]

    row = pl.BlockSpec((tm, D_MODEL), lambda i: (i, 0))
    const = lambda cols: pl.BlockSpec((1, cols), lambda i: (0, 0))
    vec = jax.ShapeDtypeStruct((1, D_MODEL), F32)
    return pl.pallas_call(
        body, name="dh_dx",
        out_shape=[jax.ShapeDtypeStruct((seq, D_MODEL), F32), vec, vec, vec],
        grid=(seq // tm,),
        in_specs=[row] * 8 + [pl.BlockSpec(memory_space=pl.ANY), row, pl.BlockSpec((tm, 1), lambda i: (i, 0)),
                              row, const(3 * D_MODEL), const(3 * D_MODEL), const(D_MODEL)],
        out_specs=[row, const(D_MODEL), const(D_MODEL), const(D_MODEL)],
        scratch_shapes=[pltpu.VMEM((8, D_MODEL, D_MODEL), BF16), pltpu.SemaphoreType.DMA],
        compiler_params=_params(("arbitrary",), VMEM_LIMIT),
    )(*pieces, w_in_all, x, rstd, dx2, mod, b_mod, g_norm)


def _w_in_grad(h, pieces):
    seq = h.shape[0]
    tk = WGRAD_ROWS
    nk = seq // tk

    def body(*refs):
        h_ref, p_refs, out_ref = refs[0], refs[1:9], refs[9]
        j, kk = pl.program_id(0), pl.program_id(1)

        @pl.when(kk == 0)
        def _():
            out_ref[...] = jnp.zeros_like(out_ref)

        for m in range(8):
            @pl.when(j == m)
            def _(m=m):
                out_ref[...] += _dot_tn(h_ref[...], p_refs[m][...])

    def piece_spec(m):
        return pl.BlockSpec((tk, D_MODEL), lambda j, kk: (jnp.where(j == m, kk, jnp.where(j < m, 0, nk - 1)), 0))

    return pl.pallas_call(
        body, name="w_in_grad",
        out_shape=jax.ShapeDtypeStruct((8, D_MODEL, D_MODEL), F32),
        grid=(8, nk),
        in_specs=[pl.BlockSpec((tk, D_MODEL), lambda j, kk: (kk, 0))] + [piece_spec(m) for m in range(8)],
        out_specs=pl.BlockSpec((None, D_MODEL, D_MODEL), lambda j, kk: (j, 0, 0)),
        compiler_params=_params(("arbitrary", "arbitrary"), VMEM_LIMIT),
    )(h, *pieces)


def _adamw(name, w, g, m, v):
    rows, cols = w.shape
    tr = rows if rows <= 256 else 256

    def body(w_ref, g_ref, m_ref, v_ref, d_ref, nm_ref, nv_ref):
        gv = g_ref[...]
        nm = ADAM_B1 * m_ref[...] + (1.0 - ADAM_B1) * gv
        nv = ADAM_B2 * v_ref[...] + (1.0 - ADAM_B2) * (gv * gv)
        m_hat = nm / (1.0 - ADAM_B1 ** ADAM_STEP)
        v_hat = nv / (1.0 - ADAM_B2 ** ADAM_STEP)
        d_ref[...] = -ADAM_LR * (m_hat / (jnp.sqrt(v_hat) + ADAM_EPS) + ADAM_WD * w_ref[...])
        nm_ref[...] = nm
        nv_ref[...] = nv

    spec = pl.BlockSpec((tr, cols), lambda i: (i, 0))
    shape = jax.ShapeDtypeStruct((rows, cols), F32)
    return pl.pallas_call(
        body, name=name, out_shape=[shape, shape, shape], grid=(rows // tr,),
        in_specs=[spec] * 4, out_specs=[spec] * 3,
        compiler_params=_params(("arbitrary",)),
    )(w, g, m, v)


def kernel(x, c, positions, g_norm, w_mod, b_mod, w_in, b_gate, conv_w, conv_b, w_a, b_a, w_x, b_x, lam, w_out_rnn, w_out_attn, w_o, g_final, loss_target, m_g_norm, m_w_mod, m_b_mod, m_w_in, m_b_gate, m_conv_w, m_conv_b, m_w_a, m_b_a, m_w_x, m_b_x, m_lam, m_w_out_rnn, m_w_out_attn, m_w_o, m_g_final, v_g_norm, v_w_mod, v_b_mod, v_w_in, v_b_gate, v_conv_w, v_conv_b, v_w_a, v_b_a, v_w_x, v_b_x, v_lam, v_w_out_rnn, v_w_out_attn, v_w_o, v_g_final):
    seq = x.shape[1]
    me = _index(_my_pos())
    xs, tgt = x[0], loss_target[0]

    pos = positions[0].astype(F32)[:, None]
    inv_freq = ROPE_THETA ** (-jnp.arange(0, 2 * ROT_HALF, 2, dtype=F32) / (2 * ROT_HALF))
    ang = pos * inv_freq
    rest = HEAD_DIM - 2 * ROT_HALF
    cosf = jnp.concatenate([jnp.cos(ang), jnp.cos(ang), jnp.ones((seq, rest), F32)], axis=1)
    sinf = jnp.concatenate([-jnp.sin(ang), jnp.sin(ang), jnp.zeros((seq, rest), F32)], axis=1)
    keep = (positions[0] != 0).astype(F32)[:, None]

    w_in_all, w_or_all, w_oa_all, w_o_all = _ag_big(
        "gather_weights", [w_in[0].astype(BF16), w_out_rnn[0].astype(BF16),
                           w_out_attn[0].astype(BF16), w_o[0].astype(BF16)])
    w_or_all, w_oa_all, w_o_all = (t.reshape(D_MODEL, D_MODEL) for t in (w_or_all, w_oa_all, w_o_all))
    conv_w8 = _ag_small("gather_conv_w", jnp.pad(conv_w[0], ((0, SUBLANES - 4), (0, 0))))
    c_all = _ag_small("gather_c", jnp.broadcast_to(c, (SUBLANES, D_MODEL)))[:, 0, :]
    mod_cols = w_mod.shape[2]
    mod_part = _ag_small("gather_mod", _mod_fwd(c_all, w_mod[0]))
    mod = lax.dynamic_index_in_dim(mod_part, me, axis=1, keepdims=False).reshape(1, N_DEV * mod_cols)

    blocks = lambda t: t.reshape(RNN_BLOCKS, 1, 128)
    rnn_params = (conv_w8, blocks(conv_b), w_a[0], blocks(b_a), w_x[0], blocks(b_x), blocks(lam))

    h, rstd, pf, q, k, v = _norm_proj(xs, mod, b_mod, g_norm, w_in_all, cosf, sinf)
    hr = _rnn_fwd(pf, keep, *rnn_params)
    state = None
    for dil in DILATIONS:
        state = _attn_fwd("attn_fwd_d%d" % dil, q, k, v, dil, state)
    o, lse = state

    (dx2, dhr, dz_rnn, d_o, dz_attn, dg_r, dg_a, delta, gp_g_final, gp_b_gate, dgate, loss_part,
     gp_w_or, gp_w_oa, gp_w_o) = _hub(xs, tgt, hr, pf, o, mod, b_mod, b_gate, g_final.reshape(1, D_MODEL),
                                      w_or_all, w_oa_all, w_o_all)
    grads = None
    for dil in DILATIONS:
        grads = _attn_bwd("attn_bwd_d%d" % dil, q, k, v, d_o, lse, delta, dil, grads)
    dq, dk, dv = _dqkv_finish(*grads, cosf, sinf)
    dx_rnn, gp_conv_w, gp_conv_b, gp_w_a, gp_b_a, gp_w_x, gp_b_x, gp_lam = _rnn_bwd(pf, hr, dhr, keep, *rnn_params)
    pieces = [dx_rnn, dz_rnn, dq, dk, dv, dz_attn, dg_r, dg_a]
    grad_x, dshift, dscale, gp_g_norm = _dh_dx(pieces, w_in_all, xs, rstd, dx2, mod, b_mod, g_norm)
    gp_w_in = _w_in_grad(h, pieces)

    dmod = jnp.concatenate([dshift, dscale, dgate], axis=1)
    dmod_all = _ag_small("gather_dmod", jnp.broadcast_to(dmod, (SUBLANES, 3 * D_MODEL)))[:, 0, :]
    dmod_cols = lax.dynamic_slice_in_dim(dmod_all, me * mod_cols, mod_cols, axis=1)
    g_b_mod, g_w_mod = _mod_bwd(c_all, dmod_all, dmod_cols)

    flat = lambda t: t.reshape(-1, 128)
    small = [flat(gp_g_norm), flat(gp_b_gate), flat(gp_conv_b), flat(gp_b_a), flat(gp_b_x), flat(gp_lam),
             flat(gp_g_final), flat(gp_conv_w), jnp.broadcast_to(loss_part, (SUBLANES, 128)),
             flat(gp_w_a), flat(gp_w_x)]
    sizes = [t.shape[0] for t in small]
    total = _sum_devices("sum_small", _ag_small("gather_small_grads", jnp.concatenate(small, axis=0)))
    offs = [sum(sizes[:i]) for i in range(len(sizes))]
    (g_g_norm, g_b_gate, g_conv_b, g_b_a, g_b_x, g_lam, g_g_final, g_conv_w_all, loss_rows, g_w_a, g_w_x) = (
        total[o_:o_ + s_] for o_, s_ in zip(offs, sizes))
    loss = loss_rows[0, 0]
    g_conv_w = lax.dynamic_index_in_dim(g_conv_w_all.reshape(RNN_BLOCKS, SUBLANES, 128), me, axis=0,
                                        keepdims=False)[:4]

    stacks = [gp_w_in, gp_w_or.reshape(N_DEV, 128, D_MODEL), gp_w_oa.reshape(N_DEV, 128, D_MODEL),
              gp_w_o.reshape(N_DEV, 128, D_MODEL)]
    from_sib = _rs_to_sibling("rs_sibling", stacks)
    targets = jnp.bitwise_xor(me, 2 * jnp.arange(4, dtype=jnp.int32)).astype(jnp.int32)
    sums = [_add_sibling("rs_add_sibling_%d" % a, s_, r_, targets) for a, (s_, r_) in enumerate(zip(stacks, from_sib))]
    from_chips = _rs_to_chips("rs_chips", sums)
    g_w_in, g_w_or, g_w_oa, g_w_o = (
        _add_chips("rs_add_chips_%d" % a, s_, r_) for a, (s_, r_) in enumerate(zip(sums, from_chips)))

    weights = [
        ("g_norm", g_norm, g_g_norm, m_g_norm, v_g_norm, (SUBLANES, 128)),
        ("w_mod", w_mod, g_w_mod, m_w_mod, v_w_mod, (D_MODEL, mod_cols)),
        ("b_mod", b_mod, g_b_mod, m_b_mod, v_b_mod, (3 * SUBLANES, 128)),
        ("w_in", w_in, g_w_in, m_w_in, v_w_in, (D_MODEL, D_MODEL)),
        ("b_gate", b_gate, g_b_gate, m_b_gate, v_b_gate, (2 * SUBLANES, 128)),
        ("conv_w", conv_w, g_conv_w, m_conv_w, v_conv_w, (4, 128)),
        ("conv_b", conv_b, g_conv_b, m_conv_b, v_conv_b, (SUBLANES, 128)),
        ("w_a", w_a, g_w_a, m_w_a, v_w_a, (RNN_BLOCKS * 128, 128)),
        ("b_a", b_a, g_b_a, m_b_a, v_b_a, (SUBLANES, 128)),
        ("w_x", w_x, g_w_x, m_w_x, v_w_x, (RNN_BLOCKS * 128, 128)),
        ("b_x", b_x, g_b_x, m_b_x, v_b_x, (SUBLANES, 128)),
        ("lam", lam, g_lam, m_lam, v_lam, (SUBLANES, 128)),
        ("w_out_rnn", w_out_rnn, g_w_or, m_w_out_rnn, v_w_out_rnn, (128, D_MODEL)),
        ("w_out_attn", w_out_attn, g_w_oa, m_w_out_attn, v_w_out_attn, (128, D_MODEL)),
        ("w_o", w_o, g_w_o, m_w_o, v_w_o, (128, D_MODEL)),
        ("g_final", g_final, g_g_final, m_g_final, v_g_final, (SUBLANES, 128)),
    ]
    out_g, out_d, out_m, out_v = [], [], [], []
    for name, w_, g_, m_, v_, shape2 in weights:
        d_, nm_, nv_ = _adamw("adamw_" + name, w_.reshape(shape2), g_.reshape(shape2), m_.reshape(shape2),
                              v_.reshape(shape2))
        out_g.append(g_.reshape(w_.shape))
        out_d.append(d_.reshape(w_.shape))
        out_m.append(nm_.reshape(w_.shape))
        out_v.append(nv_.reshape(w_.shape))
    return (loss, grad_x[None], *out_g, *out_d, *out_m, *out_v)
```

```python
import jax
import jax.numpy as jnp
from jax import lax
from jax.experimental import pallas as pl
from jax.experimental.pallas import tpu as pltpu

F32 = jnp.float32
BF16 = jnp.bfloat16
MESH = pl.DeviceIdType.MESH

D_MODEL = 1024
N_HEADS = 8
HEAD_DIM = 128
RNN_BLOCKS = 8
N_DEV = 8
ROT_HALF = 16
ROPE_THETA = 500000.0
DILATIONS = (1, 4, 16)
KEY_BLOCK = 128
SPAN = KEY_BLOCK * DILATIONS[-1]
ATTN_SCALE = HEAD_DIM ** -0.5
NORM_EPS = 1e-6
LRU_C = 8.0
NEG_INF = -1e30
ADAM_LR, ADAM_B1, ADAM_B2, ADAM_EPS, ADAM_WD, ADAM_STEP = 0.001, 0.9, 0.999, 1e-08, 0.01, 10

SUBLANES = 8
VMEM_LIMIT = 56 * 1024 * 1024
PROJ_ROWS = 512
RNN_ROWS = 256
HUB_ROWS = 128
DX_ROWS = 256
WGRAD_ROWS = 1024
ADD_ROWS = 256


def _params(sem=None, vmem=None):
    return pltpu.CompilerParams(dimension_semantics=sem, vmem_limit_bytes=vmem)


def _dot(a, b):
    return jnp.dot(a, b, preferred_element_type=F32)


def _dot_nt(a, b):
    return lax.dot_general(a, b, (((1,), (1,)), ((), ())), preferred_element_type=F32)


def _dot_tn(a, b):
    return lax.dot_general(a, b, (((0,), (0,)), ((), ())), preferred_element_type=F32)


def _sigmoid(z):
    return 1.0 / (1.0 + jnp.exp(-z))


def _expm1_nonpos(z):
    series = z * (1.0 + z * (1.0 / 2) * (1.0 + z * (1.0 / 3) * (1.0 + z * (1.0 / 4) * (
        1.0 + z * (1.0 / 5) * (1.0 + z * (1.0 / 6))))))
    return jnp.where(z > -0.25, series, jnp.exp(z) - 1.0)


def _my_pos():
    return lax.axis_index("x"), lax.axis_index("y"), lax.axis_index("c")


def _flip(pos, k):
    x, y, c = pos
    return ((1 - x) if k & 4 else x, (1 - y) if k & 2 else y, (1 - c) if k & 1 else c)


def _index(pos):
    return 4 * pos[0] + 2 * pos[1] + pos[2]


def _ag_small(name, v):
    rows, cols = v.shape

    def body(v_ref, out_ref, send_sems, recv_sems):
        me = _my_pos()
        out_ref[_index(me)] = v_ref[...]
        sends = []
        for k in range(1, N_DEV):
            cp = pltpu.make_async_remote_copy(
                src_ref=v_ref, dst_ref=out_ref.at[_index(me)], send_sem=send_sems.at[k - 1],
                recv_sem=recv_sems.at[k - 1], device_id=_flip(me, k), device_id_type=MESH)
            cp.start()
            sends.append(cp)
        for k in range(1, N_DEV):
            peer = _flip(me, k)
            pltpu.make_async_remote_copy(
                src_ref=v_ref, dst_ref=out_ref.at[_index(peer)], send_sem=send_sems.at[k - 1],
                recv_sem=recv_sems.at[k - 1], device_id=peer, device_id_type=MESH).wait_recv()
        for cp in sends:
            cp.wait_send()

    return pl.pallas_call(
        body, name=name,
        out_shape=jax.ShapeDtypeStruct((N_DEV, rows, cols), v.dtype),
        in_specs=[pl.BlockSpec(memory_space=pltpu.VMEM)],
        out_specs=pl.BlockSpec(memory_space=pltpu.VMEM),
        scratch_shapes=[pltpu.SemaphoreType.DMA((N_DEV - 1,)), pltpu.SemaphoreType.DMA((N_DEV - 1,))],
        compiler_params=_params(None, VMEM_LIMIT),
    )(v)


def _ag_big(name, shards):
    n = len(shards)

    def body(*refs):
        ins, outs = refs[:n], refs[n:2 * n]
        send_sems, recv_sems, local_sems = refs[2 * n:]
        me = _my_pos()
        sib = _flip(me, 1)
        chips = [2, 4, 6]

        def copy(a, k, block, to, src=None):
            rows = outs[a].at[_index(block)]
            return pltpu.make_async_remote_copy(
                src_ref=rows if src is None else src, dst_ref=rows,
                send_sem=send_sems.at[a * 7 + k], recv_sem=recv_sems.at[a * 7 + k],
                device_id=to, device_id_type=MESH)

        started = []
        for a in range(n):
            mine = pltpu.make_async_copy(ins[a], outs[a].at[_index(me)], local_sems.at[a])
            mine.start()
            started.append(mine)
        sends = []
        for a in range(n):
            first = [copy(a, 0, me, sib, src=ins[a])]
            first += [copy(a, 1 + j, me, _flip(me, ch), src=ins[a]) for j, ch in enumerate(chips)]
            for cp in first:
                cp.start()
            sends += first
        for j, ch in enumerate(chips):
            for a in range(n):
                copy(a, 1 + j, _flip(me, ch), me).wait_recv()
                fwd = copy(a, 4 + j, _flip(me, ch), sib)
                fwd.start()
                sends.append(fwd)
        for a in range(n):
            copy(a, 0, sib, me).wait_recv()
            for j, ch in enumerate(chips):
                copy(a, 4 + j, _flip(sib, ch), me).wait_recv()
        for cp in sends:
            cp.wait_send()
        for mine in started:
            mine.wait()

    any_spec = pl.BlockSpec(memory_space=pl.ANY)
    return pl.pallas_call(
        body, name=name,
        out_shape=[jax.ShapeDtypeStruct((N_DEV,) + s.shape, s.dtype) for s in shards],
        in_specs=[any_spec] * n, out_specs=[any_spec] * n,
        scratch_shapes=[pltpu.SemaphoreType.DMA((7 * n,)), pltpu.SemaphoreType.DMA((7 * n,)),
                        pltpu.SemaphoreType.DMA((n,))],
    )(*shards)


def _rs_to_sibling(name, stacks):
    n = len(stacks)

    def body(*refs):
        ins, outs = refs[:n], refs[n:2 * n]
        send_sems, recv_sems = refs[2 * n:]
        me = _my_pos()
        sib = _flip(me, 1)
        sends = []
        for a in range(n):
            for m in range(4):
                target = _flip(sib, 2 * m)
                cp = pltpu.make_async_remote_copy(
                    src_ref=ins[a].at[_index(target)], dst_ref=outs[a].at[m],
                    send_sem=send_sems.at[a * 4 + m], recv_sem=recv_sems.at[a * 4 + m],
                    device_id=sib, device_id_type=MESH)
                cp.start()
                sends.append(cp)
        for cp in sends:
            cp.wait_recv()
        for cp in sends:
            cp.wait_send()

    any_spec = pl.BlockSpec(memory_space=pl.ANY)
    return pl.pallas_call(
        body, name=name,
        out_shape=[jax.ShapeDtypeStruct((4,) + s.shape[1:], s.dtype) for s in stacks],
        in_specs=[any_spec] * n, out_specs=[any_spec] * n,
        scratch_shapes=[pltpu.SemaphoreType.DMA((4 * n,)), pltpu.SemaphoreType.DMA((4 * n,))],
    )(*stacks)


def _rs_to_chips(name, sums):
    n = len(sums)

    def body(*refs):
        ins, outs = refs[:n], refs[n:2 * n]
        send_sems, recv_sems = refs[2 * n:]
        me = _my_pos()
        sends = []
        for a in range(n):
            for m in range(1, 4):
                cp = pltpu.make_async_remote_copy(
                    src_ref=ins[a].at[m], dst_ref=outs[a].at[m - 1],
                    send_sem=send_sems.at[a * 3 + m - 1], recv_sem=recv_sems.at[a * 3 + m - 1],
                    device_id=_flip(me, 2 * m), device_id_type=MESH)
                cp.start()
                sends.append(cp)
        for cp in sends:
            cp.wait_recv()
        for cp in sends:
            cp.wait_send()

    any_spec = pl.BlockSpec(memory_space=pl.ANY)
    return pl.pallas_call(
        body, name=name,
        out_shape=[jax.ShapeDtypeStruct((3,) + s.shape[1:], s.dtype) for s in sums],
        in_specs=[any_spec] * n, out_specs=[any_spec] * n,
        scratch_shapes=[pltpu.SemaphoreType.DMA((3 * n,)), pltpu.SemaphoreType.DMA((3 * n,))],
    )(*sums)


def _add_sibling(name, stack, recv, targets):
    _, rows, cols = stack.shape
    tr = min(rows, ADD_ROWS)

    def body(t_ref, a_ref, b_ref, o_ref):
        o_ref[...] = a_ref[...] + b_ref[...]

    return pl.pallas_call(
        body, name=name,
        out_shape=jax.ShapeDtypeStruct((4, rows, cols), F32),
        grid_spec=pltpu.PrefetchScalarGridSpec(
            num_scalar_prefetch=1, grid=(4, rows // tr),
            in_specs=[pl.BlockSpec((None, tr, cols), lambda m, i, t: (t[m], i, 0)),
                      pl.BlockSpec((None, tr, cols), lambda m, i, t: (m, i, 0))],
            out_specs=pl.BlockSpec((None, tr, cols), lambda m, i, t: (m, i, 0))),
        compiler_params=_params(("arbitrary", "arbitrary")),
    )(targets, stack, recv)


def _add_chips(name, sums, recv):
    _, rows, cols = sums.shape
    tr = min(rows, ADD_ROWS)

    def body(a_ref, b_ref, o_ref):
        o_ref[...] = ((a_ref[...] + b_ref[0]) + b_ref[1]) + b_ref[2]

    return pl.pallas_call(
        body, name=name,
        out_shape=jax.ShapeDtypeStruct((rows, cols), F32),
        grid=(rows // tr,),
        in_specs=[pl.BlockSpec((None, tr, cols), lambda i: (0, i, 0)),
                  pl.BlockSpec((3, tr, cols), lambda i: (0, i, 0))],
        out_specs=pl.BlockSpec((tr, cols), lambda i: (i, 0)),
        compiler_params=_params(("arbitrary",)),
    )(sums, recv)


def _sum_devices(name, gathered):
    _, rows, cols = gathered.shape

    def body(g_ref, o_ref):
        acc = g_ref[0]
        for k in range(1, N_DEV):
            acc = acc + g_ref[k]
        o_ref[...] = acc

    return pl.pallas_call(
        body, name=name, out_shape=jax.ShapeDtypeStruct((rows, cols), F32),
        compiler_params=_params(None, VMEM_LIMIT),
    )(gathered)


def _mod_fwd(c_all, w_mod):
    def body(c_ref, w_ref, o_ref):
        c = c_ref[...]
        o_ref[...] = jnp.dot(c * _sigmoid(c), w_ref[...], preferred_element_type=F32,
                             precision=lax.Precision.HIGHEST)

    return pl.pallas_call(
        body, name="mod_fwd", out_shape=jax.ShapeDtypeStruct((N_DEV, w_mod.shape[1]), F32),
    )(c_all, w_mod)


def _mod_bwd(c_all, dmod_all, dmod_cols):
    def body(c_ref, da_ref, dc_ref, gb_ref, gw_ref):
        c = c_ref[...]
        acc = da_ref[0:1, :]
        for b in range(1, N_DEV):
            acc = acc + da_ref[b:b + 1, :]
        gb_ref[...] = acc
        gw_ref[...] = lax.dot_general(c * _sigmoid(c), dc_ref[...], (((0,), (0,)), ((), ())),
                                      preferred_element_type=F32, precision=lax.Precision.HIGHEST)

    return pl.pallas_call(
        body, name="mod_bwd",
        out_shape=[jax.ShapeDtypeStruct((1, dmod_all.shape[1]), F32),
                   jax.ShapeDtypeStruct((c_all.shape[1], dmod_cols.shape[1]), F32)],
    )(c_all, dmod_all, dmod_cols)


def _rope_partner(t):
    lane = lax.broadcasted_iota(jnp.int32, t.shape, 1)
    return jnp.where(lane < ROT_HALF, pltpu.roll(t, HEAD_DIM - ROT_HALF, 1), pltpu.roll(t, ROT_HALF, 1))


def _norm_proj(x, mod, b_mod, g_norm, w_in_all, cosf, sinf):
    seq = x.shape[0]
    tm = PROJ_ROWS

    def body(x_ref, mod_ref, bmod_ref, g_ref, w_ref, cos_ref, sin_ref,
             h_ref, pf_ref, q_ref, k_ref, v_ref, h_scr):
        j = pl.program_id(1)

        @pl.when(j == 0)
        def _():
            xf = x_ref[...]
            rstd = lax.rsqrt(jnp.mean(xf * xf, axis=-1, keepdims=True) + NORM_EPS)
            shift = mod_ref[:, 0:D_MODEL] + bmod_ref[:, 0:D_MODEL]
            scale = mod_ref[:, D_MODEL:2 * D_MODEL] + bmod_ref[:, D_MODEL:2 * D_MODEL]
            hb = (((xf * rstd) * g_ref[...]) * (1.0 + scale) + shift).astype(BF16)
            h_scr[...] = hb
            h_ref[...] = hb

        acc = _dot(h_scr[...], w_ref[...])

        @pl.when((j < 2) | (j > 4))
        def _():
            pf_ref[...] = acc

        def heads(dst_ref, rotate, gain):
            for hh in range(N_HEADS):
                t = acc[:, hh * HEAD_DIM:(hh + 1) * HEAD_DIM]
                if rotate:
                    t = t * cos_ref[...] + _rope_partner(t) * sin_ref[...]
                dst_ref[hh] = t if gain is None else t * gain

        @pl.when(j == 2)
        def _():
            heads(q_ref, True, ATTN_SCALE)

        @pl.when(j == 3)
        def _():
            heads(k_ref, True, None)

        @pl.when(j == 4)
        def _():
            heads(v_ref, False, None)

    def pf_slot(i, j):
        return (i, jnp.where(j < 2, j, jnp.where(j < 5, 1, j - 3)))

    hm = jax.ShapeDtypeStruct((N_HEADS, seq, HEAD_DIM), F32)
    hm_spec = pl.BlockSpec((N_HEADS, tm, HEAD_DIM), lambda i, j: (0, i, 0))
    row = lambda i, j: (i, 0)
    const = lambda i, j: (0, 0)
    return pl.pallas_call(
        body, name="norm_proj",
        out_shape=[jax.ShapeDtypeStruct((seq, D_MODEL), BF16),
                   jax.ShapeDtypeStruct((seq, 5 * D_MODEL), F32), hm, hm, hm],
        grid=(seq // tm, 8),
        in_specs=[pl.BlockSpec((tm, D_MODEL), row), pl.BlockSpec((1, 3 * D_MODEL), const),
                  pl.BlockSpec((1, 3 * D_MODEL), const), pl.BlockSpec((1, D_MODEL), const),
                  pl.BlockSpec((None, D_MODEL, D_MODEL), lambda i, j: (j, 0, 0)),
                  pl.BlockSpec((tm, HEAD_DIM), row), pl.BlockSpec((tm, HEAD_DIM), row)],
        out_specs=[pl.BlockSpec((tm, D_MODEL), row),
                   pl.BlockSpec((tm, D_MODEL), pf_slot), hm_spec, hm_spec, hm_spec],
        scratch_shapes=[pltpu.VMEM((tm, D_MODEL), BF16)],
        compiler_params=_params(("arbitrary", "arbitrary"), VMEM_LIMIT),
    )(x, mod, b_mod, g_norm, w_in_all, cosf, sinf)


def _shift_down(v, s, head):
    rows = v.shape[0]
    row = lax.broadcasted_iota(jnp.int32, v.shape, 0)
    fill = jnp.concatenate([pltpu.roll(head, s, 0), jnp.zeros((rows - SUBLANES, v.shape[1]), v.dtype)], axis=0)
    return jnp.where(row < s, fill, pltpu.roll(v, s, 0))


def _shift_up(v, s, tail):
    rows = v.shape[0]
    row = lax.broadcasted_iota(jnp.int32, v.shape, 0)
    fill = jnp.concatenate([jnp.zeros((rows - SUBLANES, v.shape[1]), v.dtype),
                            pltpu.roll(tail, SUBLANES - s, 0)], axis=0)
    return jnp.where(row >= rows - s, fill, pltpu.roll(v, rows - s, 0))


def _scan_fwd(a, b):
    rows = a.shape[0]
    row = lax.broadcasted_iota(jnp.int32, a.shape, 0)
    k = 1
    while k < rows:
        a_s = jnp.where(row >= k, pltpu.roll(a, k, 0), 1.0)
        b_s = jnp.where(row >= k, pltpu.roll(b, k, 0), 0.0)
        b = a * b_s + b
        a = a * a_s
        k *= 2
    return a, b


def _scan_rev(a, b):
    rows = a.shape[0]
    row = lax.broadcasted_iota(jnp.int32, a.shape, 0)
    k = 1
    while k < rows:
        a_s = jnp.where(row < rows - k, pltpu.roll(a, rows - k, 0), 1.0)
        b_s = jnp.where(row < rows - k, pltpu.roll(b, rows - k, 0), 0.0)
        b = a * b_s + b
        a = a * a_s
        k *= 2
    return b


def _conv_taps(xr, head):
    return [_shift_down(xr, 3, head), _shift_down(xr, 2, head), _shift_down(xr, 1, head), xr]


def _rnn_gates(xc, wa, ba, wx, bx, lam, keep):
    xcb = xc.astype(BF16)
    r = _sigmoid(_dot(xcb, wa.astype(BF16)) + ba)
    i = _sigmoid(_dot(xcb, wx.astype(BF16)) + bx)
    softplus = jnp.maximum(-lam, 0.0) + jnp.log(1.0 + jnp.exp(-jnp.abs(lam)))
    cl = -LRU_C * softplus
    log_a = cl * r
    a_raw = jnp.exp(log_a)
    mult_raw = jnp.sqrt(-_expm1_nonpos(2.0 * log_a))
    live = keep > 0.0
    return r, i, cl, a_raw, mult_raw, jnp.where(live, a_raw, 0.0), jnp.where(live, mult_raw, 1.0), live


def _rnn_specs(seq, rows, time_of):
    per = rows // SUBLANES
    vec = pl.BlockSpec((None, 1, 128), lambda hb, n: (hb, 0, 0))
    mat = pl.BlockSpec((None, 128, 128), lambda hb, n: (hb, 0, 0))
    return [pl.BlockSpec((rows, 128), lambda hb, n: (time_of(n), hb)),
            pl.BlockSpec((SUBLANES, 128), lambda hb, n: (jnp.maximum(time_of(n) * per - 1, 0), hb)),
            pl.BlockSpec((rows, 1), lambda hb, n: (time_of(n), 0)),
            pl.BlockSpec((None, SUBLANES, 128), lambda hb, n: (hb, 0, 0)),
            vec, mat, vec, mat, vec, vec]


def _rnn_fwd(pf, keep, conv_w8, conv_b, w_a, b_a, w_x, b_x, lam):
    seq = pf.shape[0]
    rows = RNN_ROWS

    def body(x_ref, xh_ref, keep_ref, cw_ref, cb_ref, wa_ref, ba_ref, wx_ref, bx_ref, lam_ref, hr_ref, carry):
        n = pl.program_id(1)

        @pl.when(n == 0)
        def _():
            carry[...] = jnp.zeros_like(carry)

        xr = x_ref[...]
        head = jnp.where(n > 0, xh_ref[...], 0.0)
        taps = _conv_taps(xr, head)
        xc = cb_ref[...] + sum(cw_ref[k:k + 1, :] * taps[k] for k in range(4))
        _, i, _, _, _, a, mult, _ = _rnn_gates(xc, wa_ref[...], ba_ref[...], wx_ref[...], bx_ref[...],
                                               lam_ref[...], keep_ref[...])
        a_cum, h_loc = _scan_fwd(a, mult * i * xc)
        h = h_loc + a_cum * carry[SUBLANES - 1:SUBLANES, :]
        hr_ref[...] = h
        carry[...] = h[rows - SUBLANES:rows, :]

    return pl.pallas_call(
        body, name="rnn_fwd",
        out_shape=jax.ShapeDtypeStruct((seq, D_MODEL), F32),
        grid=(RNN_BLOCKS, seq // rows),
        in_specs=_rnn_specs(seq, rows, lambda n: n),
        out_specs=pl.BlockSpec((rows, 128), lambda hb, n: (n, hb)),
        scratch_shapes=[pltpu.VMEM((SUBLANES, 128), F32)],
        compiler_params=_params(("arbitrary", "arbitrary"), VMEM_LIMIT),
    )(pf, pf, keep, conv_w8, conv_b, w_a, b_a, w_x, b_x, lam)


def _rnn_bwd(pf, hr, dhr, keep, conv_w8, conv_b, w_a, b_a, w_x, b_x, lam):
    seq = pf.shape[0]
    rows = RNN_ROWS
    nchunk = seq // rows
    per = rows // SUBLANES
    time_of = lambda n: nchunk - 1 - n

    def body(x_ref, xh_ref, keep_ref, cw_ref, cb_ref, wa_ref, ba_ref, wx_ref, bx_ref, lam_ref,
             hr_ref, hrh_ref, dhr_ref,
             dx_ref, gcw_ref, gcb_ref, gwa_ref, gba_ref, gwx_ref, gbx_ref, glam_ref,
             g_carry, dxc_tail):
        n = pl.program_id(1)
        first_in_time = n == nchunk - 1

        @pl.when(n == 0)
        def _():
            g_carry[...] = jnp.zeros_like(g_carry)
            dxc_tail[...] = jnp.zeros_like(dxc_tail)
            for ref in (gcw_ref, gcb_ref, gwa_ref, gba_ref, gwx_ref, gbx_ref, glam_ref):
                ref[...] = jnp.zeros_like(ref)

        xr = x_ref[...]
        head = jnp.where(first_in_time, 0.0, xh_ref[...])
        taps = _conv_taps(xr, head)
        cw = cw_ref[...]
        xc = cb_ref[...] + sum(cw[k:k + 1, :] * taps[k] for k in range(4))
        wa, wx, lam = wa_ref[...], wx_ref[...], lam_ref[...]
        r, i, cl, a_raw, mult_raw, a, mult, live = _rnn_gates(xc, wa, ba_ref[...], wx, bx_ref[...], lam,
                                                               keep_ref[...])
        h_prev = _shift_down(hr_ref[...], 1, jnp.where(first_in_time, 0.0, hrh_ref[...]))

        row = lax.broadcasted_iota(jnp.int32, xr.shape, 0)
        last = row == rows - 1
        a_next = jnp.where(last, 0.0, pltpu.roll(a, rows - 1, 0))
        g = _scan_rev(a_next, dhr_ref[...] + jnp.where(last, g_carry[0:1, :], 0.0))
        g_carry[...] = jnp.broadcast_to(a[0:1, :] * g[0:1, :], g_carry.shape)

        da = g * h_prev
        dmult = g * i * xc
        di = g * mult * xc
        dxc = g * mult * i
        dlog_a = jnp.where(live, da * a_raw - dmult * a_raw * a_raw / mult_raw, 0.0)
        dpa = (dlog_a * cl) * r * (1.0 - r)
        dpx = di * i * (1.0 - i)
        glam_ref[...] += jnp.sum(dlog_a * r, axis=0, keepdims=True) * (LRU_C * _sigmoid(-lam))
        xcb, dpab, dpxb = xc.astype(BF16), dpa.astype(BF16), dpx.astype(BF16)
        gwa_ref[...] += _dot_tn(xcb, dpab)
        gwx_ref[...] += _dot_tn(xcb, dpxb)
        gba_ref[...] += jnp.sum(dpa, axis=0, keepdims=True)
        gbx_ref[...] += jnp.sum(dpx, axis=0, keepdims=True)
        dxc = dxc + _dot_nt(dpab, wa.astype(BF16)) + _dot_nt(dpxb, wx.astype(BF16))

        gcb_ref[...] += jnp.sum(dxc, axis=0, keepdims=True)
        for k in range(4):
            gcw_ref[k:k + 1, :] += jnp.sum(dxc * taps[k], axis=0, keepdims=True)
        tail = dxc_tail[...]
        dx = cw[3:4, :] * dxc
        for k in range(3):
            dx = dx + cw[k:k + 1, :] * _shift_up(dxc, 3 - k, tail)
        dx_ref[...] = dx.astype(BF16)
        dxc_tail[...] = dxc[0:SUBLANES, :]

    blk = lambda hb, n: (hb, 0, 0)
    chunk = pl.BlockSpec((rows, 128), lambda hb, n: (time_of(n), hb))
    vec_out = pl.BlockSpec((None, 1, 128), blk)
    mat_out = pl.BlockSpec((None, 128, 128), blk)
    vec_shape = jax.ShapeDtypeStruct((RNN_BLOCKS, 1, 128), F32)
    mat_shape = jax.ShapeDtypeStruct((RNN_BLOCKS, 128, 128), F32)
    return pl.pallas_call(
        body, name="rnn_bwd",
        out_shape=[jax.ShapeDtypeStruct((seq, D_MODEL), BF16),
                   jax.ShapeDtypeStruct((RNN_BLOCKS, SUBLANES, 128), F32), vec_shape,
                   mat_shape, vec_shape, mat_shape, vec_shape, vec_shape],
        grid=(RNN_BLOCKS, nchunk),
        in_specs=_rnn_specs(seq, rows, time_of) + [
            chunk, pl.BlockSpec((SUBLANES, 128), lambda hb, n: (jnp.maximum(time_of(n) * per - 1, 0), hb)), chunk],
        out_specs=[chunk, pl.BlockSpec((None, SUBLANES, 128), blk), vec_out,
                   mat_out, vec_out, mat_out, vec_out, vec_out],
        scratch_shapes=[pltpu.VMEM((SUBLANES, 128), F32), pltpu.VMEM((SUBLANES, 128), F32)],
        compiler_params=_params(("arbitrary", "arbitrary"), VMEM_LIMIT),
    )(pf, pf, keep, conv_w8, conv_b, w_a, b_a, w_x, b_x, lam, hr, hr, dhr)


def _unit_rows(dil, r, j):
    start = j * KEY_BLOCK * dil + r
    return pl.ds(start, KEY_BLOCK) if dil == 1 else pl.ds(start, KEY_BLOCK, stride=dil)


def _attn_fwd(q, k, v):
    nh, seq, _ = q.shape
    nchunk = seq // SPAN
    nblk = SPAN // KEY_BLOCK

    def body(q_ref, k_ref, v_ref, kp_ref, vp_ref, o_ref, l1_ref, l4_ref, l16_ref, acc, m_s, l_s):
        n = pl.program_id(1)
        qi = lax.broadcasted_iota(jnp.int32, (KEY_BLOCK, KEY_BLOCK), 0)
        ki = lax.broadcasted_iota(jnp.int32, (KEY_BLOCK, KEY_BLOCK), 1)
        bias_own = jnp.where(ki <= qi, 0.0, NEG_INF)
        bias_before = jnp.where(ki >= qi, 0.0, NEG_INF)
        bias_mid = jnp.concatenate([bias_before, bias_own], axis=1)
        bias_first = jnp.concatenate([jnp.where(n > 0, bias_before, NEG_INF), bias_own], axis=1)
        ones = jnp.ones((2 * KEY_BLOCK, HEAD_DIM), BF16)
        for gi, dil in enumerate(DILATIONS):
            nb = nblk // dil
            for r in range(dil):
                for j in range(nb):
                    rows = _unit_rows(dil, r, j)
                    if j == 0:
                        prow = _unit_rows(dil, r, nb - 1)
                        kp, vp, bias = kp_ref[prow, :], vp_ref[prow, :], bias_first
                    else:
                        prow = _unit_rows(dil, r, j - 1)
                        kp, vp, bias = k_ref[prow, :], v_ref[prow, :], bias_mid
                    qb = q_ref[rows, :].astype(BF16)
                    kcat = jnp.concatenate([kp, k_ref[rows, :]], axis=0).astype(BF16)
                    vcat = jnp.concatenate([vp, v_ref[rows, :]], axis=0).astype(BF16)
                    vaug = jnp.concatenate([vcat, ones], axis=1)
                    s = _dot_nt(qb, kcat) + bias
                    mx = jnp.max(s, axis=-1, keepdims=True)
                    if gi == 0:
                        m_new = jnp.broadcast_to(mx, (KEY_BLOCK, HEAD_DIM))
                    else:
                        m_old = m_s[rows, :]
                        m_new = jnp.maximum(m_old, mx)
                    p = jnp.exp(s - jnp.concatenate([m_new, m_new], axis=1))
                    pv = _dot(p.astype(BF16), vaug)
                    if gi == 0:
                        acc[rows, :] = pv[:, :HEAD_DIM]
                        l_s[rows, :] = pv[:, HEAD_DIM:]
                    else:
                        alpha = jnp.exp(m_old - m_new)
                        acc[rows, :] = alpha * acc[rows, :] + pv[:, :HEAD_DIM]
                        l_s[rows, :] = alpha * l_s[rows, :] + pv[:, HEAD_DIM:]
                    m_s[rows, :] = m_new
        den = l_s[...]
        o_ref[...] = acc[...] * (1.0 / den)
        m_s[...] = m_s[...] + jnp.log(den)
        diag = qi == ki
        for dil, out in zip(DILATIONS, (l1_ref, l4_ref, l16_ref)):
            nb = nblk // dil
            for r in range(dil):
                for j in range(nb):
                    blk = m_s[_unit_rows(dil, r, j), :]
                    out[r * nb + j:r * nb + j + 1, :] = jnp.sum(jnp.where(diag, blk, 0.0), axis=0, keepdims=True)

    blk = pl.BlockSpec((None, SPAN, HEAD_DIM), lambda h, n: (h, n, 0))
    pblk = pl.BlockSpec((None, SPAN, HEAD_DIM), lambda h, n: (h, jnp.maximum(n - 1, 0), 0))
    lblk = pl.BlockSpec((None, nblk, KEY_BLOCK), lambda h, n: (h, n, 0))
    lshape = jax.ShapeDtypeStruct((nh, seq // KEY_BLOCK, KEY_BLOCK), F32)
    span_f32 = pltpu.VMEM((SPAN, HEAD_DIM), F32)
    o, l1, l4, l16 = pl.pallas_call(
        body, name="attn_fwd",
        out_shape=[jax.ShapeDtypeStruct((nh, seq, HEAD_DIM), F32), lshape, lshape, lshape],
        grid=(nh, nchunk), in_specs=[blk, blk, blk, pblk, pblk], out_specs=[blk, lblk, lblk, lblk],
        scratch_shapes=[span_f32, span_f32, span_f32],
        compiler_params=_params(("arbitrary", "arbitrary"), VMEM_LIMIT),
    )(q, k, v, k, v)
    return o, (l1, l4, l16)


def _attn_bwd(q, k, v, do, o, lses, cosf, sinf):
    nh, seq, _ = q.shape
    nchunk = seq // SPAN
    nblk = SPAN // KEY_BLOCK

    def body(q_ref, k_ref, v_ref, do_ref, o_ref, kp_ref, vp_ref, qn_ref, don_ref, on_ref,
             l1_ref, l4_ref, l16_ref, l1n_ref, l4n_ref, l16n_ref, cos_ref, sin_ref,
             dq_ref, dk_ref, dv_ref, dq_acc, dk_acc, dv_acc):
        n = pl.program_id(1)
        ki = lax.broadcasted_iota(jnp.int32, (KEY_BLOCK, KEY_BLOCK), 0)
        qi = lax.broadcasted_iota(jnp.int32, (KEY_BLOCK, KEY_BLOCK), 1)
        bias_own = jnp.where(ki <= qi, 0.0, NEG_INF)
        bias_before = jnp.where(ki >= qi, 0.0, NEG_INF)
        bias_mid = jnp.concatenate([bias_before, bias_own], axis=0)
        bias_first = jnp.concatenate([jnp.where(n > 0, bias_before, NEG_INF), bias_own], axis=0)
        bias_next = jnp.where(n < nchunk - 1, bias_before, NEG_INF)
        ones8 = jnp.ones((SUBLANES, HEAD_DIM), BF16)
        for ref in (dq_acc, dk_acc, dv_acc):
            ref[...] = jnp.zeros_like(ref)

        def row_dot(a, b):
            prod = a * b
            hi = prod.astype(BF16)
            lo = (prod - hi.astype(F32)).astype(BF16)
            return (_dot_nt(ones8, hi) + _dot_nt(ones8, lo))[0:1, :]

        def tile(kb, vb, qb, dob, lse_row, delta_row, bias):
            pt = jnp.exp(_dot_nt(kb, qb) + bias - lse_row)
            dst = pt * (_dot_nt(vb, dob) - delta_row)
            return pt.astype(BF16), dst.astype(BF16)

        for dil, l_ref, ln_ref in zip(DILATIONS, (l1_ref, l4_ref, l16_ref), (l1n_ref, l4n_ref, l16n_ref)):
            nb = nblk // dil
            for r in range(dil):
                for j in range(nb):
                    rows = _unit_rows(dil, r, j)
                    if j == 0:
                        prow = _unit_rows(dil, r, nb - 1)
                        kp, vp, bias = kp_ref[prow, :], vp_ref[prow, :], bias_first
                    else:
                        prow = _unit_rows(dil, r, j - 1)
                        kp, vp, bias = k_ref[prow, :], v_ref[prow, :], bias_mid
                    dof = do_ref[rows, :]
                    qb, dob = q_ref[rows, :].astype(BF16), dof.astype(BF16)
                    kcat = jnp.concatenate([kp, k_ref[rows, :]], axis=0).astype(BF16)
                    vcat = jnp.concatenate([vp, v_ref[rows, :]], axis=0).astype(BF16)
                    pt, dst = tile(kcat, vcat, qb, dob, l_ref[r * nb + j:r * nb + j + 1, :],
                                   row_dot(dof, o_ref[rows, :]), bias)
                    dvc, dkc = _dot(pt, dob), _dot(dst, qb)
                    dq_acc[rows, :] += _dot_tn(dst, kcat)
                    dk_acc[rows, :] += dkc[KEY_BLOCK:, :]
                    dv_acc[rows, :] += dvc[KEY_BLOCK:, :]
                    if j > 0:
                        dk_acc[prow, :] += dkc[:KEY_BLOCK, :]
                        dv_acc[prow, :] += dvc[:KEY_BLOCK, :]
                lrow, nrow = _unit_rows(dil, r, nb - 1), _unit_rows(dil, r, 0)
                donf = don_ref[nrow, :]
                qnb, donb = qn_ref[nrow, :].astype(BF16), donf.astype(BF16)
                pt, dst = tile(k_ref[lrow, :].astype(BF16), v_ref[lrow, :].astype(BF16), qnb, donb,
                               ln_ref[r * nb:r * nb + 1, :], row_dot(donf, on_ref[nrow, :]), bias_next)
                dk_acc[lrow, :] += _dot(dst, qnb)
                dv_acc[lrow, :] += _dot(pt, donb)

        cos, sin = cos_ref[...], sin_ref[...]
        dq, dk = dq_acc[...], dk_acc[...]
        dq_ref[...] = ((dq * cos - _rope_partner(dq) * sin) * ATTN_SCALE).astype(BF16)
        dk_ref[...] = (dk * cos - _rope_partner(dk) * sin).astype(BF16)
        dv_ref[...] = dv_acc[...].astype(BF16)

    last = nchunk - 1
    cur = lambda h, n: (h, n, 0)
    prev = lambda h, n: (h, jnp.maximum(n - 1, 0), 0)
    nxt = lambda h, n: (h, jnp.minimum(n + 1, last), 0)
    blk = lambda idx: pl.BlockSpec((None, SPAN, HEAD_DIM), idx)
    lblk = lambda idx: pl.BlockSpec((None, nblk, KEY_BLOCK), idx)
    tab = pl.BlockSpec((SPAN, HEAD_DIM), lambda h, n: (n, 0))
    out = pl.BlockSpec((SPAN, HEAD_DIM), lambda h, n: (n, h))
    shape = jax.ShapeDtypeStruct((seq, nh * HEAD_DIM), BF16)
    span_f32 = pltpu.VMEM((SPAN, HEAD_DIM), F32)
    return pl.pallas_call(
        body, name="attn_bwd", out_shape=[shape, shape, shape], grid=(nh, nchunk),
        in_specs=[blk(cur)] * 5 + [blk(prev)] * 2 + [blk(nxt)] * 3 + [lblk(cur)] * 3 + [lblk(nxt)] * 3 + [tab, tab],
        out_specs=[out, out, out],
        scratch_shapes=[span_f32, span_f32, span_f32],
        compiler_params=_params(("arbitrary", "arbitrary"), VMEM_LIMIT),
    )(q, k, v, do, o, k, v, q, do, o, *lses, *lses, cosf, sinf)


def _hub(x, tgt, hr, pf, o_hm, mod, b_mod, b_gate, g_final, w_out_rnn, w_out_attn, w_o):
    seq = x.shape[0]
    tm = HUB_ROWS
    nsteps = seq // tm

    def body(x_ref, t_ref, hr_ref, zr_ref, za_ref, gr_ref, ga_ref, o_ref, mod_ref, bmod_ref, bg_ref, gf_ref,
             wr_hbm, wa_hbm, wo_hbm,
             dx2_ref, dhr_ref, dzr_ref, do_ref, dza_ref, dgr_ref, dga_ref,
             ggf_ref, gbg_ref, dgate_ref, loss_ref, gwr_hbm, gwa_hbm, gwo_hbm,
             wr, wa, wo, gwr, gwa, gwo, sem):
        step = pl.program_id(0)

        @pl.when(step == 0)
        def _():
            for src, dst in ((wr_hbm, wr), (wa_hbm, wa), (wo_hbm, wo)):
                cp = pltpu.make_async_copy(src, dst, sem)
                cp.start()
                cp.wait()
            for ref in (gwr, gwa, gwo, ggf_ref, gbg_ref, dgate_ref, loss_ref):
                ref[...] = jnp.zeros_like(ref)

        gate = mod_ref[:, 2 * D_MODEL:] + bmod_ref[:, 2 * D_MODEL:]
        gfin = gf_ref[...]
        hr_t, zr, za = hr_ref[...], zr_ref[...], za_ref[...]
        o = jnp.concatenate([o_ref[hh] for hh in range(N_HEADS)], axis=1)
        sig_zr, sig_za = _sigmoid(zr), _sigmoid(za)
        silu_zr, silu_za = zr * sig_zr, za * sig_za
        u_rnn = (hr_t * silu_zr).astype(BF16)
        u_attn = (o * silu_za).astype(BF16)
        y_rnn = _dot(u_rnn, wr[...])
        y_attn = _dot(u_attn, wa[...])
        sr = _sigmoid(gr_ref[...] + bg_ref[:, :D_MODEL])
        sa = _sigmoid(ga_ref[...] + bg_ref[:, D_MODEL:])
        merged = (sr * y_rnn + sa * y_attn).astype(BF16)
        mo = _dot(merged, wo[...])
        x2 = x_ref[...] + gate * mo
        rstd = lax.rsqrt(jnp.mean(x2 * x2, axis=-1, keepdims=True) + NORM_EPS)
        xn = x2 * rstd
        err = xn * gfin - t_ref[...]
        loss_ref[...] += 0.5 * jnp.sum(jnp.sum(err * err, axis=-1, keepdims=True) * (1.0 / D_MODEL),
                                       axis=0, keepdims=True)

        dy = err * (1.0 / D_MODEL)
        ggf_ref[...] += jnp.sum(dy * xn, axis=0, keepdims=True)
        dxn = dy * gfin
        dx2 = rstd * (dxn - xn * jnp.mean(dxn * xn, axis=-1, keepdims=True))
        dx2_ref[...] = dx2
        dgate_ref[...] += jnp.sum(dx2 * mo, axis=0, keepdims=True)
        dmo = (dx2 * gate).astype(BF16)
        dmerged = _dot_nt(dmo, wo[...])
        gwo[...] += _dot_tn(merged, dmo)
        dy_rnn = (dmerged * sr).astype(BF16)
        dy_attn = (dmerged * sa).astype(BF16)
        dg_r = dmerged * y_rnn * sr * (1.0 - sr)
        dg_a = dmerged * y_attn * sa * (1.0 - sa)
        dgr_ref[...] = dg_r.astype(BF16)
        dga_ref[...] = dg_a.astype(BF16)
        gbg_ref[:, :D_MODEL] += jnp.sum(dg_r, axis=0, keepdims=True)
        gbg_ref[:, D_MODEL:] += jnp.sum(dg_a, axis=0, keepdims=True)
        du_rnn = _dot_nt(dy_rnn, wr[...])
        gwr[...] += _dot_tn(u_rnn, dy_rnn)
        du_attn = _dot_nt(dy_attn, wa[...])
        gwa[...] += _dot_tn(u_attn, dy_attn)
        dhr_ref[...] = du_rnn * silu_zr
        dzr_ref[...] = (du_rnn * hr_t * (sig_zr * (1.0 + zr * (1.0 - sig_zr)))).astype(BF16)
        dza_ref[...] = (du_attn * o * (sig_za * (1.0 + za * (1.0 - sig_za)))).astype(BF16)
        d_o = du_attn * silu_za
        for hh in range(N_HEADS):
            do_ref[hh] = d_o[:, hh * HEAD_DIM:(hh + 1) * HEAD_DIM]

        @pl.when(step == nsteps - 1)
        def _():
            for src, dst in ((gwr, gwr_hbm), (gwa, gwa_hbm), (gwo, gwo_hbm)):
                cp = pltpu.make_async_copy(src, dst, sem)
                cp.start()
                cp.wait()

    row = pl.BlockSpec((tm, D_MODEL), lambda i: (i, 0))
    piece = lambda slot: pl.BlockSpec((tm, D_MODEL), lambda i: (i, slot))
    hm = pl.BlockSpec((N_HEADS, tm, HEAD_DIM), lambda i: (0, i, 0))
    const = lambda cols: pl.BlockSpec((1, cols), lambda i: (0, 0))
    any_spec = pl.BlockSpec(memory_space=pl.ANY)
    act_f32 = jax.ShapeDtypeStruct((seq, D_MODEL), F32)
    act_bf16 = jax.ShapeDtypeStruct((seq, D_MODEL), BF16)
    wshape = jax.ShapeDtypeStruct((D_MODEL, D_MODEL), F32)
    return pl.pallas_call(
        body, name="hub",
        out_shape=[act_f32, act_f32, act_bf16, jax.ShapeDtypeStruct((N_HEADS, seq, HEAD_DIM), F32),
                   act_bf16, act_bf16, act_bf16,
                   jax.ShapeDtypeStruct((1, D_MODEL), F32), jax.ShapeDtypeStruct((1, 2 * D_MODEL), F32),
                   jax.ShapeDtypeStruct((1, D_MODEL), F32), jax.ShapeDtypeStruct((1, 1), F32),
                   wshape, wshape, wshape],
        grid=(nsteps,),
        in_specs=[row, row, row, piece(1), piece(2), piece(3), piece(4), hm,
                  const(3 * D_MODEL), const(3 * D_MODEL), const(2 * D_MODEL), const(D_MODEL),
                  any_spec, any_spec, any_spec],
        out_specs=[row, row, row, hm, row, row, row,
                   const(D_MODEL), const(2 * D_MODEL), const(D_MODEL), const(1),
                   any_spec, any_spec, any_spec],
        scratch_shapes=[pltpu.VMEM((D_MODEL, D_MODEL), BF16)] * 3 + [pltpu.VMEM((D_MODEL, D_MODEL), F32)] * 3
                       + [pltpu.SemaphoreType.DMA],
        compiler_params=_params(("arbitrary",), VMEM_LIMIT),
    )(x, tgt, hr, pf, pf, pf, pf, o_hm, mod, b_mod, b_gate, g_final, w_out_rnn, w_out_attn, w_o)


def _dh_dx(pieces, w_in_all, x, dx2, mod, b_mod, g_norm):
    seq = x.shape[0]
    tm = DX_ROWS

    def body(*refs):
        p_refs = refs[:8]
        w_hbm, x_ref, dx2_ref, mod_ref, bmod_ref, g_ref = refs[8:14]
        gx_ref, dshift_ref, dscale_ref, ggn_ref, w_scr, sem = refs[14:]
        step = pl.program_id(0)

        @pl.when(step == 0)
        def _():
            cp = pltpu.make_async_copy(w_hbm, w_scr, sem)
            cp.start()
            cp.wait()
            for ref in (dshift_ref, dscale_ref, ggn_ref):
                ref[...] = jnp.zeros_like(ref)

        dh = _dot_nt(p_refs[0][...], w_scr[0])
        for j in range(1, 8):
            dh = dh + _dot_nt(p_refs[j][...], w_scr[j])
        scale1 = 1.0 + mod_ref[:, D_MODEL:2 * D_MODEL] + bmod_ref[:, D_MODEL:2 * D_MODEL]
        g = g_ref[...]
        xf = x_ref[...]
        rstd_t = lax.rsqrt(jnp.mean(xf * xf, axis=-1, keepdims=True) + NORM_EPS)
        xn = xf * rstd_t
        dshift_ref[...] += jnp.sum(dh, axis=0, keepdims=True)
        dscale_ref[...] += jnp.sum(dh * (xn * g), axis=0, keepdims=True)
        ggn_ref[...] += jnp.sum(dh * scale1 * xn, axis=0, keepdims=True)
        dxn = dh * (g * scale1)
        gx_ref[...] = rstd_t * (dxn - xn * jnp.mean(dxn * xn, axis=-1, keepdims=True)) + dx2_ref[...]

    row = pl.BlockSpec((tm, D_MODEL), lambda i: (i, 0))
    const = lambda cols: pl.BlockSpec((1, cols), lambda i: (0, 0))
    vec = jax.ShapeDtypeStruct((1, D_MODEL), F32)
    return pl.pallas_call(
        body, name="dh_dx",
        out_shape=[jax.ShapeDtypeStruct((seq, D_MODEL), F32), vec, vec, vec],
        grid=(seq // tm,),
        in_specs=[row] * 8 + [pl.BlockSpec(memory_space=pl.ANY), row, row,
                              const(3 * D_MODEL), const(3 * D_MODEL), const(D_MODEL)],
        out_specs=[row, const(D_MODEL), const(D_MODEL), const(D_MODEL)],
        scratch_shapes=[pltpu.VMEM((8, D_MODEL, D_MODEL), BF16), pltpu.SemaphoreType.DMA],
        compiler_params=_params(("arbitrary",), VMEM_LIMIT),
    )(*pieces, w_in_all, x, dx2, mod, b_mod, g_norm)


def _w_in_grad(h, pieces):
    seq = h.shape[0]
    tk = WGRAD_ROWS
    nk = seq // tk

    def body(*refs):
        h_ref, p_refs, out_ref = refs[0], refs[1:9], refs[9]
        j, kk = pl.program_id(0), pl.program_id(1)

        @pl.when(kk == 0)
        def _():
            out_ref[...] = jnp.zeros_like(out_ref)

        for m in range(8):
            @pl.when(j == m)
            def _(m=m):
                out_ref[...] += _dot_tn(h_ref[...], p_refs[m][...])

    def piece_spec(m):
        return pl.BlockSpec((tk, D_MODEL), lambda j, kk: (jnp.where(j == m, kk, jnp.where(j < m, 0, nk - 1)), 0))

    return pl.pallas_call(
        body, name="w_in_grad",
        out_shape=jax.ShapeDtypeStruct((8, D_MODEL, D_MODEL), F32),
        grid=(8, nk),
        in_specs=[pl.BlockSpec((tk, D_MODEL), lambda j, kk: (kk, 0))] + [piece_spec(m) for m in range(8)],
        out_specs=pl.BlockSpec((None, D_MODEL, D_MODEL), lambda j, kk: (j, 0, 0)),
        compiler_params=_params(("arbitrary", "arbitrary"), VMEM_LIMIT),
    )(h, *pieces)


def _adamw(name, w, g, m, v):
    rows, cols = w.shape
    tr = rows if rows <= 256 else 256

    def body(w_ref, g_ref, m_ref, v_ref, d_ref, nm_ref, nv_ref):
        gv = g_ref[...]
        nm = ADAM_B1 * m_ref[...] + (1.0 - ADAM_B1) * gv
        nv = ADAM_B2 * v_ref[...] + (1.0 - ADAM_B2) * (gv * gv)
        m_hat = nm / (1.0 - ADAM_B1 ** ADAM_STEP)
        v_hat = nv / (1.0 - ADAM_B2 ** ADAM_STEP)
        d_ref[...] = -ADAM_LR * (m_hat / (jnp.sqrt(v_hat) + ADAM_EPS) + ADAM_WD * w_ref[...])
        nm_ref[...] = nm
        nv_ref[...] = nv

    spec = pl.BlockSpec((tr, cols), lambda i: (i, 0))
    shape = jax.ShapeDtypeStruct((rows, cols), F32)
    return pl.pallas_call(
        body, name=name, out_shape=[shape, shape, shape], grid=(rows // tr,),
        in_specs=[spec] * 4, out_specs=[spec] * 3,
        compiler_params=_params(("arbitrary",)),
    )(w, g, m, v)


def kernel(x, c, positions, g_norm, w_mod, b_mod, w_in, b_gate, conv_w, conv_b, w_a, b_a, w_x, b_x, lam, w_out_rnn, w_out_attn, w_o, g_final, loss_target, m_g_norm, m_w_mod, m_b_mod, m_w_in, m_b_gate, m_conv_w, m_conv_b, m_w_a, m_b_a, m_w_x, m_b_x, m_lam, m_w_out_rnn, m_w_out_attn, m_w_o, m_g_final, v_g_norm, v_w_mod, v_b_mod, v_w_in, v_b_gate, v_conv_w, v_conv_b, v_w_a, v_b_a, v_w_x, v_b_x, v_lam, v_w_out_rnn, v_w_out_attn, v_w_o, v_g_final):
    seq = x.shape[1]
    me = _index(_my_pos())
    xs, tgt = x[0], loss_target[0]

    pos = positions[0].astype(F32)[:, None]
    inv_freq = ROPE_THETA ** (-jnp.arange(0, 2 * ROT_HALF, 2, dtype=F32) / (2 * ROT_HALF))
    ang = pos * inv_freq
    rest = HEAD_DIM - 2 * ROT_HALF
    cosf = jnp.concatenate([jnp.cos(ang), jnp.cos(ang), jnp.ones((seq, rest), F32)], axis=1)
    sinf = jnp.concatenate([-jnp.sin(ang), jnp.sin(ang), jnp.zeros((seq, rest), F32)], axis=1)
    keep = (positions[0] != 0).astype(F32)[:, None]

    w_in_all, w_or_all, w_oa_all, w_o_all = _ag_big(
        "gather_weights", [w_in[0].astype(BF16), w_out_rnn[0].astype(BF16),
                           w_out_attn[0].astype(BF16), w_o[0].astype(BF16)])
    w_or_all, w_oa_all, w_o_all = (t.reshape(D_MODEL, D_MODEL) for t in (w_or_all, w_oa_all, w_o_all))
    conv_w8 = _ag_small("gather_conv_w", jnp.pad(conv_w[0], ((0, SUBLANES - 4), (0, 0))))
    c_all = _ag_small("gather_c", jnp.broadcast_to(c, (SUBLANES, D_MODEL)))[:, 0, :]
    mod_cols = w_mod.shape[2]
    mod_part = _ag_small("gather_mod", _mod_fwd(c_all, w_mod[0]))
    mod = lax.dynamic_index_in_dim(mod_part, me, axis=1, keepdims=False).reshape(1, N_DEV * mod_cols)

    blocks = lambda t: t.reshape(RNN_BLOCKS, 1, 128)
    rnn_params = (conv_w8, blocks(conv_b), w_a[0], blocks(b_a), w_x[0], blocks(b_x), blocks(lam))

    h, pf, q, k, v = _norm_proj(xs, mod, b_mod, g_norm, w_in_all, cosf, sinf)
    hr = _rnn_fwd(pf, keep, *rnn_params)
    o, lses = _attn_fwd(q, k, v)

    (dx2, dhr, dz_rnn, d_o, dz_attn, dg_r, dg_a, gp_g_final, gp_b_gate, dgate, loss_part,
     gp_w_or, gp_w_oa, gp_w_o) = _hub(xs, tgt, hr, pf, o, mod, b_mod, b_gate, g_final.reshape(1, D_MODEL),
                                      w_or_all, w_oa_all, w_o_all)
    dq, dk, dv = _attn_bwd(q, k, v, d_o, o, lses, cosf, sinf)
    dx_rnn, gp_conv_w, gp_conv_b, gp_w_a, gp_b_a, gp_w_x, gp_b_x, gp_lam = _rnn_bwd(pf, hr, dhr, keep, *rnn_params)
    pieces = [dx_rnn, dz_rnn, dq, dk, dv, dz_attn, dg_r, dg_a]
    grad_x, dshift, dscale, gp_g_norm = _dh_dx(pieces, w_in_all, xs, dx2, mod, b_mod, g_norm)
    gp_w_in = _w_in_grad(h, pieces)

    dmod = jnp.concatenate([dshift, dscale, dgate], axis=1)
    dmod_all = _ag_small("gather_dmod", jnp.broadcast_to(dmod, (SUBLANES, 3 * D_MODEL)))[:, 0, :]
    dmod_cols = lax.dynamic_slice_in_dim(dmod_all, me * mod_cols, mod_cols, axis=1)
    g_b_mod, g_w_mod = _mod_bwd(c_all, dmod_all, dmod_cols)

    flat = lambda t: t.reshape(-1, 128)
    small = [flat(gp_g_norm), flat(gp_b_gate), flat(gp_conv_b), flat(gp_b_a), flat(gp_b_x), flat(gp_lam),
             flat(gp_g_final), flat(gp_conv_w), jnp.broadcast_to(loss_part, (SUBLANES, 128)),
             flat(gp_w_a), flat(gp_w_x)]
    sizes = [t.shape[0] for t in small]
    total = _sum_devices("sum_small", _ag_small("gather_small_grads", jnp.concatenate(small, axis=0)))
    offs = [sum(sizes[:i]) for i in range(len(sizes))]
    (g_g_norm, g_b_gate, g_conv_b, g_b_a, g_b_x, g_lam, g_g_final, g_conv_w_all, loss_rows, g_w_a, g_w_x) = (
        total[o_:o_ + s_] for o_, s_ in zip(offs, sizes))
    loss = loss_rows[0, 0]
    g_conv_w = lax.dynamic_index_in_dim(g_conv_w_all.reshape(RNN_BLOCKS, SUBLANES, 128), me, axis=0,
                                        keepdims=False)[:4]

    stacks = [gp_w_in, gp_w_or.reshape(N_DEV, 128, D_MODEL), gp_w_oa.reshape(N_DEV, 128, D_MODEL),
              gp_w_o.reshape(N_DEV, 128, D_MODEL)]
    from_sib = _rs_to_sibling("rs_sibling", stacks)
    targets = jnp.bitwise_xor(me, 2 * jnp.arange(4, dtype=jnp.int32)).astype(jnp.int32)
    sums = [_add_sibling("rs_add_sibling_%d" % a, s_, r_, targets) for a, (s_, r_) in enumerate(zip(stacks, from_sib))]
    from_chips = _rs_to_chips("rs_chips", sums)
    g_w_in, g_w_or, g_w_oa, g_w_o = (
        _add_chips("rs_add_chips_%d" % a, s_, r_) for a, (s_, r_) in enumerate(zip(sums, from_chips)))

    weights = [
        ("g_norm", g_norm, g_g_norm, m_g_norm, v_g_norm, (SUBLANES, 128)),
        ("w_mod", w_mod, g_w_mod, m_w_mod, v_w_mod, (D_MODEL, mod_cols)),
        ("b_mod", b_mod, g_b_mod, m_b_mod, v_b_mod, (3 * SUBLANES, 128)),
        ("w_in", w_in, g_w_in, m_w_in, v_w_in, (D_MODEL, D_MODEL)),
        ("b_gate", b_gate, g_b_gate, m_b_gate, v_b_gate, (2 * SUBLANES, 128)),
        ("conv_w", conv_w, g_conv_w, m_conv_w, v_conv_w, (4, 128)),
        ("conv_b", conv_b, g_conv_b, m_conv_b, v_conv_b, (SUBLANES, 128)),
        ("w_a", w_a, g_w_a, m_w_a, v_w_a, (RNN_BLOCKS * 128, 128)),
        ("b_a", b_a, g_b_a, m_b_a, v_b_a, (SUBLANES, 128)),
        ("w_x", w_x, g_w_x, m_w_x, v_w_x, (RNN_BLOCKS * 128, 128)),
        ("b_x", b_x, g_b_x, m_b_x, v_b_x, (SUBLANES, 128)),
        ("lam", lam, g_lam, m_lam, v_lam, (SUBLANES, 128)),
        ("w_out_rnn", w_out_rnn, g_w_or, m_w_out_rnn, v_w_out_rnn, (128, D_MODEL)),
        ("w_out_attn", w_out_attn, g_w_oa, m_w_out_attn, v_w_out_attn, (128, D_MODEL)),
        ("w_o", w_o, g_w_o, m_w_o, v_w_o, (128, D_MODEL)),
        ("g_final", g_final, g_g_final, m_g_final, v_g_final, (SUBLANES, 128)),
    ]
    out_g, out_d, out_m, out_v = [], [], [], []
    for name, w_, g_, m_, v_, shape2 in weights:
        d_, nm_, nv_ = _adamw("adamw_" + name, w_.reshape(shape2), g_.reshape(shape2), m_.reshape(shape2),
                              v_.reshape(shape2))
        out_g.append(g_.reshape(w_.shape))
        out_d.append(d_.reshape(w_.shape))
        out_m.append(nm_.reshape(w_.shape))
        out_v.append(nv_.reshape(w_.shape))
    return (loss, grad_x[None], *out_g, *out_d, *out_m, *out_v)
```

```python
import jax
import jax.numpy as jnp
from jax import lax
from jax.experimental import pallas as pl
from jax.experimental.pallas import tpu as pltpu

F32 = jnp.float32
BF16 = jnp.bfloat16
MESH = pl.DeviceIdType.MESH

D_MODEL = 1024
N_HEADS = 8
HEAD_DIM = 128
RNN_BLOCKS = 8
N_DEV = 8
ROT_HALF = 16
ROPE_THETA = 500000.0
DILATIONS = (1, 4, 16)
KEY_BLOCK = 128
SPAN = KEY_BLOCK * DILATIONS[-1]
ATTN_SCALE = HEAD_DIM ** -0.5
NORM_EPS = 1e-6
LRU_C = 8.0
NEG_INF = -1e30
ADAM_LR, ADAM_B1, ADAM_B2, ADAM_EPS, ADAM_WD, ADAM_STEP = 0.001, 0.9, 0.999, 1e-08, 0.01, 10

SUBLANES = 8
VMEM_LIMIT = 56 * 1024 * 1024
PROJ_ROWS = 512
RNN_ROWS = 256
HUB_ROWS = 256
DX_ROWS = 256
WGRAD_ROWS = 1024
ADD_ROWS = 256


def _params(sem=None, vmem=None):
    return pltpu.CompilerParams(dimension_semantics=sem, vmem_limit_bytes=vmem)


def _dot(a, b):
    return jnp.dot(a, b, preferred_element_type=F32)


def _dot_nt(a, b):
    return lax.dot_general(a, b, (((1,), (1,)), ((), ())), preferred_element_type=F32)


def _dot_tn(a, b):
    return lax.dot_general(a, b, (((0,), (0,)), ((), ())), preferred_element_type=F32)


def _sigmoid(z):
    return 1.0 / (1.0 + jnp.exp(-z))


def _expm1_nonpos(z):
    series = z * (1.0 + z * (1.0 / 2) * (1.0 + z * (1.0 / 3) * (1.0 + z * (1.0 / 4) * (
        1.0 + z * (1.0 / 5) * (1.0 + z * (1.0 / 6))))))
    return jnp.where(z > -0.25, series, jnp.exp(z) - 1.0)


def _my_pos():
    return lax.axis_index("x"), lax.axis_index("y"), lax.axis_index("c")


def _flip(pos, k):
    x, y, c = pos
    return ((1 - x) if k & 4 else x, (1 - y) if k & 2 else y, (1 - c) if k & 1 else c)


def _index(pos):
    return 4 * pos[0] + 2 * pos[1] + pos[2]


def _ag_small(name, v):
    rows, cols = v.shape

    def body(v_ref, out_ref, send_sems, recv_sems):
        me = _my_pos()
        out_ref[_index(me)] = v_ref[...]
        sends = []
        for k in range(1, N_DEV):
            cp = pltpu.make_async_remote_copy(
                src_ref=v_ref, dst_ref=out_ref.at[_index(me)], send_sem=send_sems.at[k - 1],
                recv_sem=recv_sems.at[k - 1], device_id=_flip(me, k), device_id_type=MESH)
            cp.start()
            sends.append(cp)
        for k in range(1, N_DEV):
            peer = _flip(me, k)
            pltpu.make_async_remote_copy(
                src_ref=v_ref, dst_ref=out_ref.at[_index(peer)], send_sem=send_sems.at[k - 1],
                recv_sem=recv_sems.at[k - 1], device_id=peer, device_id_type=MESH).wait_recv()
        for cp in sends:
            cp.wait_send()

    return pl.pallas_call(
        body, name=name,
        out_shape=jax.ShapeDtypeStruct((N_DEV, rows, cols), v.dtype),
        in_specs=[pl.BlockSpec(memory_space=pltpu.VMEM)],
        out_specs=pl.BlockSpec(memory_space=pltpu.VMEM),
        scratch_shapes=[pltpu.SemaphoreType.DMA((N_DEV - 1,)), pltpu.SemaphoreType.DMA((N_DEV - 1,))],
        compiler_params=_params(None, VMEM_LIMIT),
    )(v)


def _ag_big(name, shards):
    n = len(shards)

    def body(*refs):
        ins, outs = refs[:n], refs[n:2 * n]
        send_sems, recv_sems, local_sems = refs[2 * n:]
        me = _my_pos()
        sib = _flip(me, 1)
        chips = [2, 4, 6]

        def copy(a, k, block, to, src=None):
            rows = outs[a].at[_index(block)]
            return pltpu.make_async_remote_copy(
                src_ref=rows if src is None else src, dst_ref=rows,
                send_sem=send_sems.at[a * 7 + k], recv_sem=recv_sems.at[a * 7 + k],
                device_id=to, device_id_type=MESH)

        started = []
        for a in range(n):
            mine = pltpu.make_async_copy(ins[a], outs[a].at[_index(me)], local_sems.at[a])
            mine.start()
            started.append(mine)
        sends = []
        for a in range(n):
            first = [copy(a, 0, me, sib, src=ins[a])]
            first += [copy(a, 1 + j, me, _flip(me, ch), src=ins[a]) for j, ch in enumerate(chips)]
            for cp in first:
                cp.start()
            sends += first
        for j, ch in enumerate(chips):
            for a in range(n):
                copy(a, 1 + j, _flip(me, ch), me).wait_recv()
                fwd = copy(a, 4 + j, _flip(me, ch), sib)
                fwd.start()
                sends.append(fwd)
        for a in range(n):
            copy(a, 0, sib, me).wait_recv()
            for j, ch in enumerate(chips):
                copy(a, 4 + j, _flip(sib, ch), me).wait_recv()
        for cp in sends:
            cp.wait_send()
        for mine in started:
            mine.wait()

    any_spec = pl.BlockSpec(memory_space=pl.ANY)
    return pl.pallas_call(
        body, name=name,
        out_shape=[jax.ShapeDtypeStruct((N_DEV,) + s.shape, s.dtype) for s in shards],
        in_specs=[any_spec] * n, out_specs=[any_spec] * n,
        scratch_shapes=[pltpu.SemaphoreType.DMA((7 * n,)), pltpu.SemaphoreType.DMA((7 * n,)),
                        pltpu.SemaphoreType.DMA((n,))],
    )(*shards)


def _rs_to_sibling(name, stacks):
    n = len(stacks)

    def body(*refs):
        ins, outs = refs[:n], refs[n:2 * n]
        send_sems, recv_sems = refs[2 * n:]
        me = _my_pos()
        sib = _flip(me, 1)
        sends = []
        for a in range(n):
            for m in range(4):
                target = _flip(sib, 2 * m)
                cp = pltpu.make_async_remote_copy(
                    src_ref=ins[a].at[_index(target)], dst_ref=outs[a].at[m],
                    send_sem=send_sems.at[a * 4 + m], recv_sem=recv_sems.at[a * 4 + m],
                    device_id=sib, device_id_type=MESH)
                cp.start()
                sends.append(cp)
        for cp in sends:
            cp.wait_recv()
        for cp in sends:
            cp.wait_send()

    any_spec = pl.BlockSpec(memory_space=pl.ANY)
    return pl.pallas_call(
        body, name=name,
        out_shape=[jax.ShapeDtypeStruct((4,) + s.shape[1:], s.dtype) for s in stacks],
        in_specs=[any_spec] * n, out_specs=[any_spec] * n,
        scratch_shapes=[pltpu.SemaphoreType.DMA((4 * n,)), pltpu.SemaphoreType.DMA((4 * n,))],
    )(*stacks)


def _rs_to_chips(name, sums):
    n = len(sums)

    def body(*refs):
        ins, outs = refs[:n], refs[n:2 * n]
        send_sems, recv_sems = refs[2 * n:]
        me = _my_pos()
        sends = []
        for a in range(n):
            for m in range(1, 4):
                cp = pltpu.make_async_remote_copy(
                    src_ref=ins[a].at[m - 1], dst_ref=outs[a].at[m - 1],
                    send_sem=send_sems.at[a * 3 + m - 1], recv_sem=recv_sems.at[a * 3 + m - 1],
                    device_id=_flip(me, 2 * m), device_id_type=MESH)
                cp.start()
                sends.append(cp)
        for cp in sends:
            cp.wait_recv()
        for cp in sends:
            cp.wait_send()

    any_spec = pl.BlockSpec(memory_space=pl.ANY)
    return pl.pallas_call(
        body, name=name,
        out_shape=[jax.ShapeDtypeStruct((3,) + s.shape[1:], s.dtype) for s in sums],
        in_specs=[any_spec] * n, out_specs=[any_spec] * n,
        scratch_shapes=[pltpu.SemaphoreType.DMA((3 * n,)), pltpu.SemaphoreType.DMA((3 * n,))],
    )(*sums)


def _add_sibling(name, stack, recv, targets):
    _, rows, cols = stack.shape
    tr = min(rows, ADD_ROWS)

    def own_body(t_ref, a_ref, b_ref, o_ref):
        o_ref[...] = a_ref[...] + b_ref[...]

    own = pl.pallas_call(
        own_body, name=name + "_own",
        out_shape=jax.ShapeDtypeStruct((rows, cols), F32),
        grid_spec=pltpu.PrefetchScalarGridSpec(
            num_scalar_prefetch=1, grid=(rows // tr,),
            in_specs=[pl.BlockSpec((None, tr, cols), lambda i, t: (t[0], i, 0)),
                      pl.BlockSpec((None, tr, cols), lambda i, t: (0, i, 0))],
            out_specs=pl.BlockSpec((tr, cols), lambda i, t: (i, 0))),
        compiler_params=_params(("arbitrary",)),
    )(targets, stack, recv)

    def send_body(t_ref, a_ref, b_ref, o_ref):
        o_ref[...] = (a_ref[...] + b_ref[...]).astype(BF16)

    send = pl.pallas_call(
        send_body, name=name + "_send",
        out_shape=jax.ShapeDtypeStruct((3, rows, cols), BF16),
        grid_spec=pltpu.PrefetchScalarGridSpec(
            num_scalar_prefetch=1, grid=(3, rows // tr),
            in_specs=[pl.BlockSpec((None, tr, cols), lambda m, i, t: (t[m + 1], i, 0)),
                      pl.BlockSpec((None, tr, cols), lambda m, i, t: (m + 1, i, 0))],
            out_specs=pl.BlockSpec((None, tr, cols), lambda m, i, t: (m, i, 0))),
        compiler_params=_params(("arbitrary", "arbitrary")),
    )(targets, stack, recv)
    return own, send


def _add_chips(name, own, recv):
    rows, cols = own.shape
    tr = min(rows, ADD_ROWS)

    def body(a_ref, b_ref, o_ref):
        o_ref[...] = ((a_ref[...] + b_ref[0].astype(F32)) + b_ref[1].astype(F32)) + b_ref[2].astype(F32)

    return pl.pallas_call(
        body, name=name,
        out_shape=jax.ShapeDtypeStruct((rows, cols), F32),
        grid=(rows // tr,),
        in_specs=[pl.BlockSpec((tr, cols), lambda i: (i, 0)),
                  pl.BlockSpec((3, tr, cols), lambda i: (0, i, 0))],
        out_specs=pl.BlockSpec((tr, cols), lambda i: (i, 0)),
        compiler_params=_params(("arbitrary",)),
    )(own, recv)


def _allreduce_small(name, v):
    rows, cols = v.shape
    half = rows // 2
    assert rows % (2 * SUBLANES) == 0

    def body(v_ref, out_ref, from_sib, chip_half, from_chips, send_sems, recv_sems):
        me = _my_pos()
        sib = _flip(me, 1)
        mine = pl.ds(pl.multiple_of(me[2] * half, SUBLANES), half)
        theirs = pl.ds(pl.multiple_of((1 - me[2]) * half, SUBLANES), half)

        def copy(k, src, dst, to):
            return pltpu.make_async_remote_copy(src_ref=src, dst_ref=dst, send_sem=send_sems.at[k],
                                                recv_sem=recv_sems.at[k], device_id=to, device_id_type=MESH)

        to_sib = copy(0, v_ref.at[theirs], from_sib, sib)
        to_sib.start()
        to_sib.wait_recv()
        chip_half[...] = v_ref[mine, :] + from_sib[...]
        to_chips = [copy(m, chip_half, from_chips.at[m - 1], _flip(me, 2 * m)) for m in range(1, 4)]
        for cp in to_chips:
            cp.start()
        for cp in to_chips:
            cp.wait_recv()
        my_chip = 2 * me[0] + me[1]
        total = None
        for chip in range(4):
            slot = jnp.maximum(jnp.bitwise_xor(chip, my_chip) - 1, 0)
            part = jnp.where(chip == my_chip, chip_half[...], from_chips[slot])
            total = part if total is None else total + part
        out_ref[mine, :] = total
        swap = copy(4, out_ref.at[mine], out_ref.at[mine], sib)
        swap.start()
        copy(4, out_ref.at[theirs], out_ref.at[theirs], sib).wait_recv()
        for cp in [to_sib, swap] + to_chips:
            cp.wait_send()

    return pl.pallas_call(
        body, name=name, out_shape=jax.ShapeDtypeStruct((rows, cols), F32),
        in_specs=[pl.BlockSpec(memory_space=pltpu.VMEM)],
        out_specs=pl.BlockSpec(memory_space=pltpu.VMEM),
        scratch_shapes=[pltpu.VMEM((half, cols), F32), pltpu.VMEM((half, cols), F32),
                        pltpu.VMEM((3, half, cols), F32),
                        pltpu.SemaphoreType.DMA((5,)), pltpu.SemaphoreType.DMA((5,))],
        compiler_params=_params(None, VMEM_LIMIT),
    )(v)


def _mod_fwd(c_all, w_mod):
    def body(c_ref, w_ref, o_ref):
        c = c_ref[...]
        o_ref[...] = jnp.dot(c * _sigmoid(c), w_ref[...], preferred_element_type=F32,
                             precision=lax.Precision.HIGHEST)

    return pl.pallas_call(
        body, name="mod_fwd", out_shape=jax.ShapeDtypeStruct((N_DEV, w_mod.shape[1]), F32),
    )(c_all, w_mod)


def _mod_bwd(c_all, dmod_all, dmod_cols):
    def body(c_ref, da_ref, dc_ref, gb_ref, gw_ref):
        c = c_ref[...]
        acc = da_ref[0:1, :]
        for b in range(1, N_DEV):
            acc = acc + da_ref[b:b + 1, :]
        gb_ref[...] = acc
        gw_ref[...] = lax.dot_general(c * _sigmoid(c), dc_ref[...], (((0,), (0,)), ((), ())),
                                      preferred_element_type=F32, precision=lax.Precision.HIGHEST)

    return pl.pallas_call(
        body, name="mod_bwd",
        out_shape=[jax.ShapeDtypeStruct((1, dmod_all.shape[1]), F32),
                   jax.ShapeDtypeStruct((c_all.shape[1], dmod_cols.shape[1]), F32)],
    )(c_all, dmod_all, dmod_cols)


def _rope_partner(t):
    lane = lax.broadcasted_iota(jnp.int32, t.shape, 1)
    return jnp.where(lane < ROT_HALF, pltpu.roll(t, HEAD_DIM - ROT_HALF, 1), pltpu.roll(t, ROT_HALF, 1))


def _norm_proj(x, mod, b_mod, g_norm, w_in_all, cosf, sinf):
    seq = x.shape[0]
    tm = PROJ_ROWS

    def body(x_ref, mod_ref, bmod_ref, g_ref, w_ref, cos_ref, sin_ref,
             h_ref, pf_ref, q_ref, k_ref, v_ref, h_scr):
        j = pl.program_id(1)

        @pl.when(j == 0)
        def _():
            xf = x_ref[...]
            rstd = lax.rsqrt(jnp.mean(xf * xf, axis=-1, keepdims=True) + NORM_EPS)
            shift = mod_ref[:, 0:D_MODEL] + bmod_ref[:, 0:D_MODEL]
            scale = mod_ref[:, D_MODEL:2 * D_MODEL] + bmod_ref[:, D_MODEL:2 * D_MODEL]
            hb = (((xf * rstd) * g_ref[...]) * (1.0 + scale) + shift).astype(BF16)
            h_scr[...] = hb
            h_ref[...] = hb

        @pl.when((j < 2) | (j > 4))
        def _():
            pf_ref[...] = _dot(h_scr[...], w_ref[...])

        def heads(dst_ref, rotate, gain):
            for pair in range(N_HEADS // 2):
                both = _dot(h_scr[...], w_ref[:, 2 * pair * HEAD_DIM:2 * (pair + 1) * HEAD_DIM])
                for hh in (2 * pair, 2 * pair + 1):
                    t = both[:, (hh % 2) * HEAD_DIM:(hh % 2 + 1) * HEAD_DIM]
                    if rotate:
                        t = t * cos_ref[...] + _rope_partner(t) * sin_ref[...]
                    dst_ref[hh] = t if gain is None else t * gain

        @pl.when(j == 2)
        def _():
            heads(q_ref, True, ATTN_SCALE)

        @pl.when(j == 3)
        def _():
            heads(k_ref, True, None)

        @pl.when(j == 4)
        def _():
            heads(v_ref, False, None)

    def pf_slot(i, j):
        return (i, jnp.where(j < 2, j, jnp.where(j < 5, 1, j - 3)))

    hm = jax.ShapeDtypeStruct((N_HEADS, seq, HEAD_DIM), F32)
    hm_spec = pl.BlockSpec((N_HEADS, tm, HEAD_DIM), lambda i, j: (0, i, 0))
    row = lambda i, j: (i, 0)
    const = lambda i, j: (0, 0)
    return pl.pallas_call(
        body, name="norm_proj",
        out_shape=[jax.ShapeDtypeStruct((seq, D_MODEL), BF16),
                   jax.ShapeDtypeStruct((seq, 5 * D_MODEL), F32), hm, hm, hm],
        grid=(seq // tm, 8),
        in_specs=[pl.BlockSpec((tm, D_MODEL), row), pl.BlockSpec((1, 3 * D_MODEL), const),
                  pl.BlockSpec((1, 3 * D_MODEL), const), pl.BlockSpec((1, D_MODEL), const),
                  pl.BlockSpec((None, D_MODEL, D_MODEL), lambda i, j: (j, 0, 0)),
                  pl.BlockSpec((tm, HEAD_DIM), row), pl.BlockSpec((tm, HEAD_DIM), row)],
        out_specs=[pl.BlockSpec((tm, D_MODEL), row),
                   pl.BlockSpec((tm, D_MODEL), pf_slot), hm_spec, hm_spec, hm_spec],
        scratch_shapes=[pltpu.VMEM((tm, D_MODEL), BF16)],
        compiler_params=_params(("arbitrary", "arbitrary"), VMEM_LIMIT),
    )(x, mod, b_mod, g_norm, w_in_all, cosf, sinf)


def _shift_down(v, s, head):
    rows = v.shape[0]
    row = lax.broadcasted_iota(jnp.int32, v.shape, 0)
    fill = jnp.concatenate([pltpu.roll(head, s, 0), jnp.zeros((rows - SUBLANES, v.shape[1]), v.dtype)], axis=0)
    return jnp.where(row < s, fill, pltpu.roll(v, s, 0))


def _shift_up(v, s, tail):
    rows = v.shape[0]
    row = lax.broadcasted_iota(jnp.int32, v.shape, 0)
    fill = jnp.concatenate([jnp.zeros((rows - SUBLANES, v.shape[1]), v.dtype),
                            pltpu.roll(tail, SUBLANES - s, 0)], axis=0)
    return jnp.where(row >= rows - s, fill, pltpu.roll(v, rows - s, 0))


def _scan_fwd(a, b):
    rows = a.shape[0]
    row = lax.broadcasted_iota(jnp.int32, a.shape, 0)
    k = 1
    while k < rows:
        a_s = jnp.where(row >= k, pltpu.roll(a, k, 0), 1.0)
        b_s = jnp.where(row >= k, pltpu.roll(b, k, 0), 0.0)
        b = a * b_s + b
        a = a * a_s
        k *= 2
    return a, b


def _scan_rev(a, b):
    rows = a.shape[0]
    row = lax.broadcasted_iota(jnp.int32, a.shape, 0)
    k = 1
    while k < rows:
        a_s = jnp.where(row < rows - k, pltpu.roll(a, rows - k, 0), 1.0)
        b_s = jnp.where(row < rows - k, pltpu.roll(b, rows - k, 0), 0.0)
        b = a * b_s + b
        a = a * a_s
        k *= 2
    return b


def _conv_taps(xr, head):
    return [_shift_down(xr, 3, head), _shift_down(xr, 2, head), _shift_down(xr, 1, head), xr]


def _rnn_gates(xc, wa, ba, wx, bx, lam, keep):
    xcb = xc.astype(BF16)
    r = _sigmoid(_dot(xcb, wa.astype(BF16)) + ba)
    i = _sigmoid(_dot(xcb, wx.astype(BF16)) + bx)
    softplus = jnp.maximum(-lam, 0.0) + jnp.log(1.0 + jnp.exp(-jnp.abs(lam)))
    cl = -LRU_C * softplus
    log_a = cl * r
    a_raw = jnp.exp(log_a)
    mult_raw = jnp.sqrt(-_expm1_nonpos(2.0 * log_a))
    live = keep > 0.0
    return r, i, cl, a_raw, mult_raw, jnp.where(live, a_raw, 0.0), jnp.where(live, mult_raw, 1.0), live


def _rnn_specs(seq, rows, time_of):
    per = rows // SUBLANES
    vec = pl.BlockSpec((None, 1, 128), lambda hb, n: (hb, 0, 0))
    mat = pl.BlockSpec((None, 128, 128), lambda hb, n: (hb, 0, 0))
    return [pl.BlockSpec((rows, 128), lambda hb, n: (time_of(n), hb)),
            pl.BlockSpec((SUBLANES, 128), lambda hb, n: (jnp.maximum(time_of(n) * per - 1, 0), hb)),
            pl.BlockSpec((rows, 1), lambda hb, n: (time_of(n), 0)),
            pl.BlockSpec((None, SUBLANES, 128), lambda hb, n: (hb, 0, 0)),
            vec, mat, vec, mat, vec, vec]


def _rnn_fwd(pf, keep, conv_w8, conv_b, w_a, b_a, w_x, b_x, lam):
    seq = pf.shape[0]
    rows = RNN_ROWS

    def body(x_ref, xh_ref, keep_ref, cw_ref, cb_ref, wa_ref, ba_ref, wx_ref, bx_ref, lam_ref, hr_ref, carry):
        n = pl.program_id(1)

        @pl.when(n == 0)
        def _():
            carry[...] = jnp.zeros_like(carry)

        xr = x_ref[...]
        head = jnp.where(n > 0, xh_ref[...], 0.0)
        taps = _conv_taps(xr, head)
        xc = cb_ref[...] + sum(cw_ref[k:k + 1, :] * taps[k] for k in range(4))
        _, i, _, _, _, a, mult, _ = _rnn_gates(xc, wa_ref[...], ba_ref[...], wx_ref[...], bx_ref[...],
                                               lam_ref[...], keep_ref[...])
        a_cum, h_loc = _scan_fwd(a, mult * i * xc)
        h = h_loc + a_cum * carry[SUBLANES - 1:SUBLANES, :]
        hr_ref[...] = h
        carry[...] = h[rows - SUBLANES:rows, :]

    return pl.pallas_call(
        body, name="rnn_fwd",
        out_shape=jax.ShapeDtypeStruct((seq, D_MODEL), F32),
        grid=(RNN_BLOCKS, seq // rows),
        in_specs=_rnn_specs(seq, rows, lambda n: n),
        out_specs=pl.BlockSpec((rows, 128), lambda hb, n: (n, hb)),
        scratch_shapes=[pltpu.VMEM((SUBLANES, 128), F32)],
        compiler_params=_params(("arbitrary", "arbitrary"), VMEM_LIMIT),
    )(pf, pf, keep, conv_w8, conv_b, w_a, b_a, w_x, b_x, lam)


def _rnn_bwd(pf, hr, dhr, keep, conv_w8, conv_b, w_a, b_a, w_x, b_x, lam):
    seq = pf.shape[0]
    rows = RNN_ROWS
    nchunk = seq // rows
    per = rows // SUBLANES
    time_of = lambda n: nchunk - 1 - n

    def body(x_ref, xh_ref, keep_ref, cw_ref, cb_ref, wa_ref, ba_ref, wx_ref, bx_ref, lam_ref,
             hr_ref, hrh_ref, dhr_ref,
             dx_ref, gcw_ref, gcb_ref, gwa_ref, gba_ref, gwx_ref, gbx_ref, glam_ref,
             g_carry, dxc_tail):
        n = pl.program_id(1)
        first_in_time = n == nchunk - 1

        @pl.when(n == 0)
        def _():
            g_carry[...] = jnp.zeros_like(g_carry)
            dxc_tail[...] = jnp.zeros_like(dxc_tail)
            for ref in (gcw_ref, gcb_ref, gwa_ref, gba_ref, gwx_ref, gbx_ref, glam_ref):
                ref[...] = jnp.zeros_like(ref)

        xr = x_ref[...]
        head = jnp.where(first_in_time, 0.0, xh_ref[...])
        taps = _conv_taps(xr, head)
        cw = cw_ref[...]
        xc = cb_ref[...] + sum(cw[k:k + 1, :] * taps[k] for k in range(4))
        wa, wx, lam = wa_ref[...], wx_ref[...], lam_ref[...]
        r, i, cl, a_raw, mult_raw, a, mult, live = _rnn_gates(xc, wa, ba_ref[...], wx, bx_ref[...], lam,
                                                               keep_ref[...])
        h_prev = _shift_down(hr_ref[...], 1, jnp.where(first_in_time, 0.0, hrh_ref[...]))

        row = lax.broadcasted_iota(jnp.int32, xr.shape, 0)
        last = row == rows - 1
        a_next = jnp.where(last, 0.0, pltpu.roll(a, rows - 1, 0))
        g = _scan_rev(a_next, dhr_ref[...] + jnp.where(last, g_carry[0:1, :], 0.0))
        g_carry[...] = jnp.broadcast_to(a[0:1, :] * g[0:1, :], g_carry.shape)

        da = g * h_prev
        dmult = g * i * xc
        di = g * mult * xc
        dxc = g * mult * i
        dlog_a = jnp.where(live, da * a_raw - dmult * a_raw * a_raw / mult_raw, 0.0)
        dpa = (dlog_a * cl) * r * (1.0 - r)
        dpx = di * i * (1.0 - i)
        glam_ref[...] += jnp.sum(dlog_a * r, axis=0, keepdims=True) * (LRU_C * _sigmoid(-lam))
        xcb, dpab, dpxb = xc.astype(BF16), dpa.astype(BF16), dpx.astype(BF16)
        gwa_ref[...] += _dot_tn(xcb, dpab)
        gwx_ref[...] += _dot_tn(xcb, dpxb)
        gba_ref[...] += jnp.sum(dpa, axis=0, keepdims=True)
        gbx_ref[...] += jnp.sum(dpx, axis=0, keepdims=True)
        dxc = dxc + _dot_nt(dpab, wa.astype(BF16)) + _dot_nt(dpxb, wx.astype(BF16))

        gcb_ref[...] += jnp.sum(dxc, axis=0, keepdims=True)
        for k in range(4):
            gcw_ref[k:k + 1, :] += jnp.sum(dxc * taps[k], axis=0, keepdims=True)
        tail = dxc_tail[...]
        dx = cw[3:4, :] * dxc
        for k in range(3):
            dx = dx + cw[k:k + 1, :] * _shift_up(dxc, 3 - k, tail)
        dx_ref[...] = dx.astype(BF16)
        dxc_tail[...] = dxc[0:SUBLANES, :]

    blk = lambda hb, n: (hb, 0, 0)
    chunk = pl.BlockSpec((rows, 128), lambda hb, n: (time_of(n), hb))
    vec_out = pl.BlockSpec((None, 1, 128), blk)
    mat_out = pl.BlockSpec((None, 128, 128), blk)
    vec_shape = jax.ShapeDtypeStruct((RNN_BLOCKS, 1, 128), F32)
    mat_shape = jax.ShapeDtypeStruct((RNN_BLOCKS, 128, 128), F32)
    return pl.pallas_call(
        body, name="rnn_bwd",
        out_shape=[jax.ShapeDtypeStruct((seq, D_MODEL), BF16),
                   jax.ShapeDtypeStruct((RNN_BLOCKS, SUBLANES, 128), F32), vec_shape,
                   mat_shape, vec_shape, mat_shape, vec_shape, vec_shape],
        grid=(RNN_BLOCKS, nchunk),
        in_specs=_rnn_specs(seq, rows, time_of) + [
            chunk, pl.BlockSpec((SUBLANES, 128), lambda hb, n: (jnp.maximum(time_of(n) * per - 1, 0), hb)), chunk],
        out_specs=[chunk, pl.BlockSpec((None, SUBLANES, 128), blk), vec_out,
                   mat_out, vec_out, mat_out, vec_out, vec_out],
        scratch_shapes=[pltpu.VMEM((SUBLANES, 128), F32), pltpu.VMEM((SUBLANES, 128), F32)],
        compiler_params=_params(("arbitrary", "arbitrary"), VMEM_LIMIT),
    )(pf, pf, keep, conv_w8, conv_b, w_a, b_a, w_x, b_x, lam, hr, hr, dhr)


def _unit_rows(dil, r, j):
    start = j * KEY_BLOCK * dil + r
    return pl.ds(start, KEY_BLOCK) if dil == 1 else pl.ds(start, KEY_BLOCK, stride=dil)


def _attn_fwd(q, k, v):
    nh, seq, _ = q.shape
    nchunk = seq // SPAN
    nblk = SPAN // KEY_BLOCK

    def body(q_ref, k_ref, v_ref, kp_ref, vp_ref, o_ref, l1_ref, l4_ref, l16_ref, acc, m_s, l_s):
        n = pl.program_id(1)
        qi = lax.broadcasted_iota(jnp.int32, (KEY_BLOCK, KEY_BLOCK), 0)
        ki = lax.broadcasted_iota(jnp.int32, (KEY_BLOCK, KEY_BLOCK), 1)
        bias_own = jnp.where(ki <= qi, 0.0, NEG_INF)
        bias_before = jnp.where(ki >= qi, 0.0, NEG_INF)
        bias_mid = jnp.concatenate([bias_before, bias_own], axis=1)
        bias_first = jnp.concatenate([jnp.where(n > 0, bias_before, NEG_INF), bias_own], axis=1)
        ones = jnp.ones((2 * KEY_BLOCK, HEAD_DIM), BF16)
        for gi, dil in enumerate(DILATIONS):
            nb = nblk // dil
            for r in range(dil):
                for j in range(nb):
                    rows = _unit_rows(dil, r, j)
                    if j == 0:
                        prow = _unit_rows(dil, r, nb - 1)
                        kp, vp, bias = kp_ref[prow, :], vp_ref[prow, :], bias_first
                    else:
                        prow = _unit_rows(dil, r, j - 1)
                        kp, vp, bias = k_ref[prow, :], v_ref[prow, :], bias_mid
                    qb = q_ref[rows, :].astype(BF16)
                    kcat = jnp.concatenate([kp, k_ref[rows, :]], axis=0).astype(BF16)
                    vcat = jnp.concatenate([vp, v_ref[rows, :]], axis=0).astype(BF16)
                    vaug = jnp.concatenate([vcat, ones], axis=1)
                    s = _dot_nt(qb, kcat) + bias
                    mx = jnp.max(s, axis=-1, keepdims=True)
                    if gi == 0:
                        m_new = jnp.broadcast_to(mx, (KEY_BLOCK, HEAD_DIM))
                    else:
                        m_old = m_s[rows, :]
                        m_new = jnp.maximum(m_old, mx)
                    p = jnp.exp(s - jnp.concatenate([m_new, m_new], axis=1))
                    pv = _dot(p.astype(BF16), vaug)
                    if gi == 0:
                        acc[rows, :] = pv[:, :HEAD_DIM]
                        l_s[rows, :] = pv[:, HEAD_DIM:]
                    else:
                        alpha = jnp.exp(m_old - m_new)
                        acc[rows, :] = alpha * acc[rows, :] + pv[:, :HEAD_DIM]
                        l_s[rows, :] = alpha * l_s[rows, :] + pv[:, HEAD_DIM:]
                    m_s[rows, :] = m_new
        den = l_s[...]
        o_ref[...] = acc[...] * (1.0 / den)
        m_s[...] = m_s[...] + jnp.log(den)
        diag = qi == ki
        for dil, out in zip(DILATIONS, (l1_ref, l4_ref, l16_ref)):
            nb = nblk // dil
            for r in range(dil):
                for j in range(nb):
                    blk = m_s[_unit_rows(dil, r, j), :]
                    out[r * nb + j:r * nb + j + 1, :] = jnp.sum(jnp.where(diag, blk, 0.0), axis=0, keepdims=True)

    blk = pl.BlockSpec((None, SPAN, HEAD_DIM), lambda h, n: (h, n, 0))
    pblk = pl.BlockSpec((None, SPAN, HEAD_DIM), lambda h, n: (h, jnp.maximum(n - 1, 0), 0))
    lblk = pl.BlockSpec((None, nblk, KEY_BLOCK), lambda h, n: (h, n, 0))
    lshape = jax.ShapeDtypeStruct((nh, seq // KEY_BLOCK, KEY_BLOCK), F32)
    span_f32 = pltpu.VMEM((SPAN, HEAD_DIM), F32)
    o, l1, l4, l16 = pl.pallas_call(
        body, name="attn_fwd",
        out_shape=[jax.ShapeDtypeStruct((nh, seq, HEAD_DIM), F32), lshape, lshape, lshape],
        grid=(nh, nchunk), in_specs=[blk, blk, blk, pblk, pblk], out_specs=[blk, lblk, lblk, lblk],
        scratch_shapes=[span_f32, span_f32, span_f32],
        compiler_params=_params(("arbitrary", "arbitrary"), VMEM_LIMIT),
    )(q, k, v, k, v)
    return o, (l1, l4, l16)


def _attn_bwd(q, k, v, do, o, lses, cosf, sinf):
    nh, seq, _ = q.shape
    nchunk = seq // SPAN
    nblk = SPAN // KEY_BLOCK

    def body(q_ref, k_ref, v_ref, do_ref, o_ref, kp_ref, vp_ref, qn_ref, don_ref, on_ref,
             l1_ref, l4_ref, l16_ref, l1n_ref, l4n_ref, l16n_ref, cos_ref, sin_ref,
             dq_ref, dk_ref, dv_ref, dq_acc, dk_acc, dv_acc):
        n = pl.program_id(1)
        ki = lax.broadcasted_iota(jnp.int32, (KEY_BLOCK, KEY_BLOCK), 0)
        qi = lax.broadcasted_iota(jnp.int32, (KEY_BLOCK, KEY_BLOCK), 1)
        bias_own = jnp.where(ki <= qi, 0.0, NEG_INF)
        bias_before = jnp.where(ki >= qi, 0.0, NEG_INF)
        bias_mid = jnp.concatenate([bias_before, bias_own], axis=0)
        bias_first = jnp.concatenate([jnp.where(n > 0, bias_before, NEG_INF), bias_own], axis=0)
        bias_next = jnp.where(n < nchunk - 1, bias_before, NEG_INF)
        ones8 = jnp.ones((SUBLANES, HEAD_DIM), BF16)
        for ref in (dq_acc, dk_acc, dv_acc):
            ref[...] = jnp.zeros_like(ref)

        def row_dot(a, b):
            prod = a * b
            hi = prod.astype(BF16)
            lo = (prod - hi.astype(F32)).astype(BF16)
            return (_dot_nt(ones8, hi) + _dot_nt(ones8, lo))[0:1, :]

        def tile(kb, vb, qb, dob, lse_row, delta_row, bias):
            pt = jnp.exp(_dot_nt(kb, qb) + bias - lse_row)
            dst = pt * (_dot_nt(vb, dob) - delta_row)
            return pt.astype(BF16), dst.astype(BF16)

        for dil, l_ref, ln_ref in zip(DILATIONS, (l1_ref, l4_ref, l16_ref), (l1n_ref, l4n_ref, l16n_ref)):
            nb = nblk // dil
            for r in range(dil):
                for j in range(nb):
                    rows = _unit_rows(dil, r, j)
                    if j == 0:
                        prow = _unit_rows(dil, r, nb - 1)
                        kp, vp, bias = kp_ref[prow, :], vp_ref[prow, :], bias_first
                    else:
                        prow = _unit_rows(dil, r, j - 1)
                        kp, vp, bias = k_ref[prow, :], v_ref[prow, :], bias_mid
                    dof = do_ref[rows, :]
                    qb, dob = q_ref[rows, :].astype(BF16), dof.astype(BF16)
                    kcat = jnp.concatenate([kp, k_ref[rows, :]], axis=0).astype(BF16)
                    vcat = jnp.concatenate([vp, v_ref[rows, :]], axis=0).astype(BF16)
                    pt, dst = tile(kcat, vcat, qb, dob, l_ref[r * nb + j:r * nb + j + 1, :],
                                   row_dot(dof, o_ref[rows, :]), bias)
                    dvc, dkc = _dot(pt, dob), _dot(dst, qb)
                    dq_acc[rows, :] += _dot_tn(dst, kcat)
                    dk_acc[rows, :] += dkc[KEY_BLOCK:, :]
                    dv_acc[rows, :] += dvc[KEY_BLOCK:, :]
                    if j > 0:
                        dk_acc[prow, :] += dkc[:KEY_BLOCK, :]
                        dv_acc[prow, :] += dvc[:KEY_BLOCK, :]
                lrow, nrow = _unit_rows(dil, r, nb - 1), _unit_rows(dil, r, 0)
                donf = don_ref[nrow, :]
                qnb, donb = qn_ref[nrow, :].astype(BF16), donf.astype(BF16)
                pt, dst = tile(k_ref[lrow, :].astype(BF16), v_ref[lrow, :].astype(BF16), qnb, donb,
                               ln_ref[r * nb:r * nb + 1, :], row_dot(donf, on_ref[nrow, :]), bias_next)
                dk_acc[lrow, :] += _dot(dst, qnb)
                dv_acc[lrow, :] += _dot(pt, donb)

        cos, sin = cos_ref[...], sin_ref[...]
        dq, dk = dq_acc[...], dk_acc[...]
        dq_ref[...] = ((dq * cos - _rope_partner(dq) * sin) * ATTN_SCALE).astype(BF16)
        dk_ref[...] = (dk * cos - _rope_partner(dk) * sin).astype(BF16)
        dv_ref[...] = dv_acc[...].astype(BF16)

    last = nchunk - 1
    cur = lambda h, n: (h, n, 0)
    prev = lambda h, n: (h, jnp.maximum(n - 1, 0), 0)
    nxt = lambda h, n: (h, jnp.minimum(n + 1, last), 0)
    blk = lambda idx: pl.BlockSpec((None, SPAN, HEAD_DIM), idx)
    lblk = lambda idx: pl.BlockSpec((None, nblk, KEY_BLOCK), idx)
    tab = pl.BlockSpec((SPAN, HEAD_DIM), lambda h, n: (n, 0))
    out = pl.BlockSpec((SPAN, HEAD_DIM), lambda h, n: (n, h))
    shape = jax.ShapeDtypeStruct((seq, nh * HEAD_DIM), BF16)
    span_f32 = pltpu.VMEM((SPAN, HEAD_DIM), F32)
    return pl.pallas_call(
        body, name="attn_bwd", out_shape=[shape, shape, shape], grid=(nh, nchunk),
        in_specs=[blk(cur)] * 5 + [blk(prev)] * 2 + [blk(nxt)] * 3 + [lblk(cur)] * 3 + [lblk(nxt)] * 3 + [tab, tab],
        out_specs=[out, out, out],
        scratch_shapes=[span_f32, span_f32, span_f32],
        compiler_params=_params(("arbitrary", "arbitrary"), VMEM_LIMIT),
    )(q, k, v, do, o, k, v, q, do, o, *lses, *lses, cosf, sinf)


def _hub(x, tgt, hr, pf, o_hm, mod, b_mod, b_gate, g_final, w_out_rnn, w_out_attn, w_o):
    seq = x.shape[0]
    tm = HUB_ROWS
    nsteps = seq // tm

    def body(x_ref, t_ref, hr_ref, zr_ref, za_ref, gr_ref, ga_ref, o_ref, mod_ref, bmod_ref, bg_ref, gf_ref,
             wr_hbm, wa_hbm, wo_hbm,
             dx2_ref, dhr_ref, dzr_ref, do_ref, dza_ref, dgr_ref, dga_ref,
             ur_ref, dyr_ref, ua_ref, dya_ref, mg_ref, dmo_ref,
             ggf_ref, gbg_ref, dgate_ref, loss_ref,
             wr, wa, wo, sem):
        step = pl.program_id(0)

        @pl.when(step == 0)
        def _():
            for src, dst in ((wr_hbm, wr), (wa_hbm, wa), (wo_hbm, wo)):
                cp = pltpu.make_async_copy(src, dst, sem)
                cp.start()
                cp.wait()
            for ref in (ggf_ref, gbg_ref, dgate_ref, loss_ref):
                ref[...] = jnp.zeros_like(ref)

        gate = mod_ref[:, 2 * D_MODEL:] + bmod_ref[:, 2 * D_MODEL:]
        gfin = gf_ref[...]
        hr_t, zr, za = hr_ref[...], zr_ref[...], za_ref[...]
        o = jnp.concatenate([o_ref[hh] for hh in range(N_HEADS)], axis=1)
        sig_zr, sig_za = _sigmoid(zr), _sigmoid(za)
        silu_zr, silu_za = zr * sig_zr, za * sig_za
        u_rnn = (hr_t * silu_zr).astype(BF16)
        u_attn = (o * silu_za).astype(BF16)
        y_rnn = _dot(u_rnn, wr[...])
        y_attn = _dot(u_attn, wa[...])
        sr = _sigmoid(gr_ref[...] + bg_ref[:, :D_MODEL])
        sa = _sigmoid(ga_ref[...] + bg_ref[:, D_MODEL:])
        merged = (sr * y_rnn + sa * y_attn).astype(BF16)
        mo = _dot(merged, wo[...])
        x2 = x_ref[...] + gate * mo
        rstd = lax.rsqrt(jnp.mean(x2 * x2, axis=-1, keepdims=True) + NORM_EPS)
        xn = x2 * rstd
        err = xn * gfin - t_ref[...]
        loss_ref[...] += 0.5 * jnp.sum(jnp.sum(err * err, axis=-1, keepdims=True) * (1.0 / D_MODEL),
                                       axis=0, keepdims=True)

        dy = err * (1.0 / D_MODEL)
        ggf_ref[...] += jnp.sum(dy * xn, axis=0, keepdims=True)
        dxn = dy * gfin
        dx2 = rstd * (dxn - xn * jnp.mean(dxn * xn, axis=-1, keepdims=True))
        dx2_ref[...] = dx2
        dgate_ref[...] += jnp.sum(dx2 * mo, axis=0, keepdims=True)
        dmo = (dx2 * gate).astype(BF16)
        dmerged = _dot_nt(dmo, wo[...])
        mg_ref[...] = merged
        dmo_ref[...] = dmo
        dy_rnn = (dmerged * sr).astype(BF16)
        dy_attn = (dmerged * sa).astype(BF16)
        dg_r = dmerged * y_rnn * sr * (1.0 - sr)
        dg_a = dmerged * y_attn * sa * (1.0 - sa)
        dgr_ref[...] = dg_r.astype(BF16)
        dga_ref[...] = dg_a.astype(BF16)
        gbg_ref[:, :D_MODEL] += jnp.sum(dg_r, axis=0, keepdims=True)
        gbg_ref[:, D_MODEL:] += jnp.sum(dg_a, axis=0, keepdims=True)
        du_rnn = _dot_nt(dy_rnn, wr[...])
        du_attn = _dot_nt(dy_attn, wa[...])
        ur_ref[...] = u_rnn
        dyr_ref[...] = dy_rnn
        ua_ref[...] = u_attn
        dya_ref[...] = dy_attn
        dhr_ref[...] = du_rnn * silu_zr
        dzr_ref[...] = (du_rnn * hr_t * (sig_zr * (1.0 + zr * (1.0 - sig_zr)))).astype(BF16)
        dza_ref[...] = (du_attn * o * (sig_za * (1.0 + za * (1.0 - sig_za)))).astype(BF16)
        d_o = du_attn * silu_za
        for hh in range(N_HEADS):
            do_ref[hh] = d_o[:, hh * HEAD_DIM:(hh + 1) * HEAD_DIM]

    row = pl.BlockSpec((tm, D_MODEL), lambda i: (i, 0))
    piece = lambda slot: pl.BlockSpec((tm, D_MODEL), lambda i: (i, slot))
    hm = pl.BlockSpec((N_HEADS, tm, HEAD_DIM), lambda i: (0, i, 0))
    const = lambda cols: pl.BlockSpec((1, cols), lambda i: (0, 0))
    any_spec = pl.BlockSpec(memory_space=pl.ANY)
    act_f32 = jax.ShapeDtypeStruct((seq, D_MODEL), F32)
    act_bf16 = jax.ShapeDtypeStruct((seq, D_MODEL), BF16)
    return pl.pallas_call(
        body, name="hub",
        out_shape=[act_f32, act_f32, act_bf16, jax.ShapeDtypeStruct((N_HEADS, seq, HEAD_DIM), F32),
                   act_bf16, act_bf16, act_bf16] + [act_bf16] * 6 + [
                   jax.ShapeDtypeStruct((1, D_MODEL), F32), jax.ShapeDtypeStruct((1, 2 * D_MODEL), F32),
                   jax.ShapeDtypeStruct((1, D_MODEL), F32), jax.ShapeDtypeStruct((1, 1), F32)],
        grid=(nsteps,),
        in_specs=[row, row, row, piece(1), piece(2), piece(3), piece(4), hm,
                  const(3 * D_MODEL), const(3 * D_MODEL), const(2 * D_MODEL), const(D_MODEL),
                  any_spec, any_spec, any_spec],
        out_specs=[row, row, row, hm, row, row, row] + [row] * 6 + [
                   const(D_MODEL), const(2 * D_MODEL), const(D_MODEL), const(1)],
        scratch_shapes=[pltpu.VMEM((D_MODEL, D_MODEL), BF16)] * 3 + [pltpu.SemaphoreType.DMA],
        compiler_params=_params(("arbitrary",), VMEM_LIMIT),
    )(x, tgt, hr, pf, pf, pf, pf, o_hm, mod, b_mod, b_gate, g_final, w_out_rnn, w_out_attn, w_o)


def _pair_grads(name, lefts, rights):
    n = len(rights)
    shared = len(lefts) == 1
    seq = rights[0].shape[0]
    tk = WGRAD_ROWS
    nk = seq // tk

    def body(*refs):
        l_refs, r_refs, out_ref = refs[:len(lefts)], refs[len(lefts):len(lefts) + n], refs[len(lefts) + n]
        j, kk = pl.program_id(0), pl.program_id(1)

        @pl.when(kk == 0)
        def _():
            out_ref[...] = jnp.zeros_like(out_ref)

        for m in range(n):
            @pl.when(j == m)
            def _(m=m):
                out_ref[...] += _dot_tn(l_refs[0 if shared else m][...], r_refs[m][...])

    def spec(m):
        return pl.BlockSpec((tk, D_MODEL), lambda j, kk: (jnp.where(j == m, kk, jnp.where(j < m, 0, nk - 1)), 0))

    left_specs = [pl.BlockSpec((tk, D_MODEL), lambda j, kk: (kk, 0))] if shared else [spec(m) for m in range(n)]
    return pl.pallas_call(
        body, name=name,
        out_shape=jax.ShapeDtypeStruct((n, D_MODEL, D_MODEL), F32),
        grid=(n, nk),
        in_specs=left_specs + [spec(m) for m in range(n)],
        out_specs=pl.BlockSpec((None, D_MODEL, D_MODEL), lambda j, kk: (j, 0, 0)),
        compiler_params=_params(("arbitrary", "arbitrary"), VMEM_LIMIT),
    )(*lefts, *rights)


def _dh_dx(pieces, w_in_all, x, dx2, mod, b_mod, g_norm):
    seq = x.shape[0]
    tm = DX_ROWS

    def body(*refs):
        p_refs = refs[:8]
        w_hbm, x_ref, dx2_ref, mod_ref, bmod_ref, g_ref = refs[8:14]
        gx_ref, dshift_ref, dscale_ref, ggn_ref, w_scr, sem = refs[14:]
        step = pl.program_id(0)

        @pl.when(step == 0)
        def _():
            cp = pltpu.make_async_copy(w_hbm, w_scr, sem)
            cp.start()
            cp.wait()
            for ref in (dshift_ref, dscale_ref, ggn_ref):
                ref[...] = jnp.zeros_like(ref)

        dh = _dot_nt(p_refs[0][...], w_scr[0])
        for j in range(1, 8):
            dh = dh + _dot_nt(p_refs[j][...], w_scr[j])
        scale1 = 1.0 + mod_ref[:, D_MODEL:2 * D_MODEL] + bmod_ref[:, D_MODEL:2 * D_MODEL]
        g = g_ref[...]
        xf = x_ref[...]
        rstd_t = lax.rsqrt(jnp.mean(xf * xf, axis=-1, keepdims=True) + NORM_EPS)
        xn = xf * rstd_t
        dshift_ref[...] += jnp.sum(dh, axis=0, keepdims=True)
        dscale_ref[...] += jnp.sum(dh * (xn * g), axis=0, keepdims=True)
        ggn_ref[...] += jnp.sum(dh * scale1 * xn, axis=0, keepdims=True)
        dxn = dh * (g * scale1)
        gx_ref[...] = rstd_t * (dxn - xn * jnp.mean(dxn * xn, axis=-1, keepdims=True)) + dx2_ref[...]

    row = pl.BlockSpec((tm, D_MODEL), lambda i: (i, 0))
    const = lambda cols: pl.BlockSpec((1, cols), lambda i: (0, 0))
    vec = jax.ShapeDtypeStruct((1, D_MODEL), F32)
    return pl.pallas_call(
        body, name="dh_dx",
        out_shape=[jax.ShapeDtypeStruct((seq, D_MODEL), F32), vec, vec, vec],
        grid=(seq // tm,),
        in_specs=[row] * 8 + [pl.BlockSpec(memory_space=pl.ANY), row, row,
                              const(3 * D_MODEL), const(3 * D_MODEL), const(D_MODEL)],
        out_specs=[row, const(D_MODEL), const(D_MODEL), const(D_MODEL)],
        scratch_shapes=[pltpu.VMEM((8, D_MODEL, D_MODEL), BF16), pltpu.SemaphoreType.DMA],
        compiler_params=_params(("arbitrary",), VMEM_LIMIT),
    )(*pieces, w_in_all, x, dx2, mod, b_mod, g_norm)


def _adamw(name, w, g, m, v):
    rows, cols = w.shape
    tr = rows if rows <= 256 else 256

    def body(w_ref, g_ref, m_ref, v_ref, d_ref, nm_ref, nv_ref):
        gv = g_ref[...]
        nm = ADAM_B1 * m_ref[...] + (1.0 - ADAM_B1) * gv
        nv = ADAM_B2 * v_ref[...] + (1.0 - ADAM_B2) * (gv * gv)
        m_hat = nm / (1.0 - ADAM_B1 ** ADAM_STEP)
        v_hat = nv / (1.0 - ADAM_B2 ** ADAM_STEP)
        d_ref[...] = -ADAM_LR * (m_hat / (jnp.sqrt(v_hat) + ADAM_EPS) + ADAM_WD * w_ref[...])
        nm_ref[...] = nm
        nv_ref[...] = nv

    spec = pl.BlockSpec((tr, cols), lambda i: (i, 0))
    shape = jax.ShapeDtypeStruct((rows, cols), F32)
    return pl.pallas_call(
        body, name=name, out_shape=[shape, shape, shape], grid=(rows // tr,),
        in_specs=[spec] * 4, out_specs=[spec] * 3,
        compiler_params=_params(("arbitrary",)),
    )(w, g, m, v)


def kernel(x, c, positions, g_norm, w_mod, b_mod, w_in, b_gate, conv_w, conv_b, w_a, b_a, w_x, b_x, lam, w_out_rnn, w_out_attn, w_o, g_final, loss_target, m_g_norm, m_w_mod, m_b_mod, m_w_in, m_b_gate, m_conv_w, m_conv_b, m_w_a, m_b_a, m_w_x, m_b_x, m_lam, m_w_out_rnn, m_w_out_attn, m_w_o, m_g_final, v_g_norm, v_w_mod, v_b_mod, v_w_in, v_b_gate, v_conv_w, v_conv_b, v_w_a, v_b_a, v_w_x, v_b_x, v_lam, v_w_out_rnn, v_w_out_attn, v_w_o, v_g_final):
    seq = x.shape[1]
    me = _index(_my_pos())
    xs, tgt = x[0], loss_target[0]

    pos = positions[0].astype(F32)[:, None]
    inv_freq = ROPE_THETA ** (-jnp.arange(0, 2 * ROT_HALF, 2, dtype=F32) / (2 * ROT_HALF))
    ang = pos * inv_freq
    rest = HEAD_DIM - 2 * ROT_HALF
    cosf = jnp.concatenate([jnp.cos(ang), jnp.cos(ang), jnp.ones((seq, rest), F32)], axis=1)
    sinf = jnp.concatenate([-jnp.sin(ang), jnp.sin(ang), jnp.zeros((seq, rest), F32)], axis=1)
    keep = (positions[0] != 0).astype(F32)[:, None]

    w_in_all, w_or_all, w_oa_all, w_o_all = _ag_big(
        "gather_weights", [w_in[0].astype(BF16), w_out_rnn[0].astype(BF16),
                           w_out_attn[0].astype(BF16), w_o[0].astype(BF16)])
    w_or_all, w_oa_all, w_o_all = (t.reshape(D_MODEL, D_MODEL) for t in (w_or_all, w_oa_all, w_o_all))
    conv_w8 = _ag_small("gather_conv_w", jnp.pad(conv_w[0], ((0, SUBLANES - 4), (0, 0))))
    c_all = _ag_small("gather_c", jnp.broadcast_to(c, (SUBLANES, D_MODEL)))[:, 0, :]
    mod_cols = w_mod.shape[2]
    mod_part = _ag_small("gather_mod", _mod_fwd(c_all, w_mod[0]))
    mod = lax.dynamic_index_in_dim(mod_part, me, axis=1, keepdims=False).reshape(1, N_DEV * mod_cols)

    blocks = lambda t: t.reshape(RNN_BLOCKS, 1, 128)
    rnn_params = (conv_w8, blocks(conv_b), w_a[0], blocks(b_a), w_x[0], blocks(b_x), blocks(lam))

    h, pf, q, k, v = _norm_proj(xs, mod, b_mod, g_norm, w_in_all, cosf, sinf)
    hr = _rnn_fwd(pf, keep, *rnn_params)
    o, lses = _attn_fwd(q, k, v)

    (dx2, dhr, dz_rnn, d_o, dz_attn, dg_r, dg_a, u_rnn, dy_rnn, u_attn, dy_attn, merged, dmo,
     gp_g_final, gp_b_gate, dgate, loss_part) = _hub(
        xs, tgt, hr, pf, o, mod, b_mod, b_gate, g_final.reshape(1, D_MODEL), w_or_all, w_oa_all, w_o_all)
    gp_w_or, gp_w_oa, gp_w_o = _pair_grads("out_grads", [u_rnn, u_attn, merged], [dy_rnn, dy_attn, dmo])
    dq, dk, dv = _attn_bwd(q, k, v, d_o, o, lses, cosf, sinf)
    dx_rnn, gp_conv_w, gp_conv_b, gp_w_a, gp_b_a, gp_w_x, gp_b_x, gp_lam = _rnn_bwd(pf, hr, dhr, keep, *rnn_params)
    pieces = [dx_rnn, dz_rnn, dq, dk, dv, dz_attn, dg_r, dg_a]
    grad_x, dshift, dscale, gp_g_norm = _dh_dx(pieces, w_in_all, xs, dx2, mod, b_mod, g_norm)
    gp_w_in = _pair_grads("w_in_grad", [h], pieces)

    dmod = jnp.concatenate([dshift, dscale, dgate], axis=1)
    dmod_all = _ag_small("gather_dmod", jnp.broadcast_to(dmod, (SUBLANES, 3 * D_MODEL)))[:, 0, :]
    dmod_cols = lax.dynamic_slice_in_dim(dmod_all, me * mod_cols, mod_cols, axis=1)
    g_b_mod, g_w_mod = _mod_bwd(c_all, dmod_all, dmod_cols)

    flat = lambda t: t.reshape(-1, 128)
    small = [flat(gp_g_norm), flat(gp_b_gate), flat(gp_conv_b), flat(gp_b_a), flat(gp_b_x), flat(gp_lam),
             flat(gp_g_final), flat(gp_conv_w), jnp.broadcast_to(loss_part, (SUBLANES, 128)),
             flat(gp_w_a), flat(gp_w_x)]
    sizes = [t.shape[0] for t in small]
    small.append(jnp.zeros((-sum(sizes) % (2 * SUBLANES), 128), F32))
    total = _allreduce_small("allreduce_small_grads", jnp.concatenate(small, axis=0))
    offs = [sum(sizes[:i]) for i in range(len(sizes))]
    (g_g_norm, g_b_gate, g_conv_b, g_b_a, g_b_x, g_lam, g_g_final, g_conv_w_all, loss_rows, g_w_a, g_w_x) = (
        total[o_:o_ + s_] for o_, s_ in zip(offs, sizes))
    loss = loss_rows[0, 0]
    g_conv_w = lax.dynamic_index_in_dim(g_conv_w_all.reshape(RNN_BLOCKS, SUBLANES, 128), me, axis=0,
                                        keepdims=False)[:4]

    stacks = [gp_w_in, gp_w_or.reshape(N_DEV, 128, D_MODEL), gp_w_oa.reshape(N_DEV, 128, D_MODEL),
              gp_w_o.reshape(N_DEV, 128, D_MODEL)]
    from_sib = _rs_to_sibling("rs_sibling", stacks)
    targets = jnp.bitwise_xor(me, 2 * jnp.arange(4, dtype=jnp.int32)).astype(jnp.int32)
    sums = [_add_sibling("rs_add_sibling_%d" % a, s_, r_, targets) for a, (s_, r_) in enumerate(zip(stacks, from_sib))]
    from_chips = _rs_to_chips("rs_chips", [send for _, send in sums])
    g_w_in, g_w_or, g_w_oa, g_w_o = (
        _add_chips("rs_add_chips_%d" % a, own, r_) for a, ((own, _), r_) in enumerate(zip(sums, from_chips)))

    weights = [
        ("g_norm", g_norm, g_g_norm, m_g_norm, v_g_norm, (SUBLANES, 128)),
        ("w_mod", w_mod, g_w_mod, m_w_mod, v_w_mod, (D_MODEL, mod_cols)),
        ("b_mod", b_mod, g_b_mod, m_b_mod, v_b_mod, (3 * SUBLANES, 128)),
        ("w_in", w_in, g_w_in, m_w_in, v_w_in, (D_MODEL, D_MODEL)),
        ("b_gate", b_gate, g_b_gate, m_b_gate, v_b_gate, (2 * SUBLANES, 128)),
        ("conv_w", conv_w, g_conv_w, m_conv_w, v_conv_w, (4, 128)),
        ("conv_b", conv_b, g_conv_b, m_conv_b, v_conv_b, (SUBLANES, 128)),
        ("w_a", w_a, g_w_a, m_w_a, v_w_a, (RNN_BLOCKS * 128, 128)),
        ("b_a", b_a, g_b_a, m_b_a, v_b_a, (SUBLANES, 128)),
        ("w_x", w_x, g_w_x, m_w_x, v_w_x, (RNN_BLOCKS * 128, 128)),
        ("b_x", b_x, g_b_x, m_b_x, v_b_x, (SUBLANES, 128)),
        ("lam", lam, g_lam, m_lam, v_lam, (SUBLANES, 128)),
        ("w_out_rnn", w_out_rnn, g_w_or, m_w_out_rnn, v_w_out_rnn, (128, D_MODEL)),
        ("w_out_attn", w_out_attn, g_w_oa, m_w_out_attn, v_w_out_attn, (128, D_MODEL)),
        ("w_o", w_o, g_w_o, m_w_o, v_w_o, (128, D_MODEL)),
        ("g_final", g_final, g_g_final, m_g_final, v_g_final, (SUBLANES, 128)),
    ]
    out_g, out_d, out_m, out_v = [], [], [], []
    for name, w_, g_, m_, v_, shape2 in weights:
        d_, nm_, nv_ = _adamw("adamw_" + name, w_.reshape(shape2), g_.reshape(shape2), m_.reshape(shape2),
                              v_.reshape(shape2))
        out_g.append(g_.reshape(w_.shape))
        out_d.append(d_.reshape(w_.shape))
        out_m.append(nm_.reshape(w_.shape))
        out_v.append(nv_.reshape(w_.shape))
    return (loss, grad_x[None], *out_g, *out_d, *out_m, *out_v)
```

```python
import jax
import jax.numpy as jnp
from jax import lax
from jax.experimental import pallas as pl
from jax.experimental.pallas import tpu as pltpu

F32 = jnp.float32
BF16 = jnp.bfloat16
MESH = pl.DeviceIdType.MESH

D_MODEL = 1024
N_HEADS = 8
HEAD_DIM = 128
RNN_BLOCKS = 8
N_DEV = 8
ROT_HALF = 16
ROPE_THETA = 500000.0
DILATIONS = (1, 4, 16)
KEY_BLOCK = 128
SPAN = KEY_BLOCK * DILATIONS[-1]
ATTN_SCALE = HEAD_DIM ** -0.5
NORM_EPS = 1e-6
LRU_C = 8.0
NEG_INF = -1e30
ADAM_LR, ADAM_B1, ADAM_B2, ADAM_EPS, ADAM_WD, ADAM_STEP = 0.001, 0.9, 0.999, 1e-08, 0.01, 10

SUBLANES = 8
VMEM_LIMIT = 56 * 1024 * 1024
PROJ_ROWS = 512
RNN_ROWS = 512
HUB_ROWS = 256
DX_ROWS = 256
WGRAD_ROWS = 1024
ADD_ROWS = 256


def _params(sem=None, vmem=None):
    return pltpu.CompilerParams(dimension_semantics=sem, vmem_limit_bytes=vmem)


def _dot(a, b):
    return jnp.dot(a, b, preferred_element_type=F32)


def _dot_nt(a, b):
    return lax.dot_general(a, b, (((1,), (1,)), ((), ())), preferred_element_type=F32)


def _dot_tn(a, b):
    return lax.dot_general(a, b, (((0,), (0,)), ((), ())), preferred_element_type=F32)


def _sigmoid(z):
    return 1.0 / (1.0 + jnp.exp(-z))


def _expm1_nonpos(z, exp_z):
    return jnp.where(z > -0.01, z * (1.0 + 0.5 * z), exp_z - 1.0)


def _my_pos():
    return lax.axis_index("x"), lax.axis_index("y"), lax.axis_index("c")


def _flip(pos, k):
    x, y, c = pos
    return ((1 - x) if k & 4 else x, (1 - y) if k & 2 else y, (1 - c) if k & 1 else c)


def _index(pos):
    return 4 * pos[0] + 2 * pos[1] + pos[2]


def _ag_small(name, v):
    rows, cols = v.shape

    def body(v_ref, out_ref, send_sems, recv_sems):
        me = _my_pos()
        out_ref[_index(me)] = v_ref[...]
        sends = []
        for k in range(1, N_DEV):
            cp = pltpu.make_async_remote_copy(
                src_ref=v_ref, dst_ref=out_ref.at[_index(me)], send_sem=send_sems.at[k - 1],
                recv_sem=recv_sems.at[k - 1], device_id=_flip(me, k), device_id_type=MESH)
            cp.start()
            sends.append(cp)
        for k in range(1, N_DEV):
            peer = _flip(me, k)
            pltpu.make_async_remote_copy(
                src_ref=v_ref, dst_ref=out_ref.at[_index(peer)], send_sem=send_sems.at[k - 1],
                recv_sem=recv_sems.at[k - 1], device_id=peer, device_id_type=MESH).wait_recv()
        for cp in sends:
            cp.wait_send()

    return pl.pallas_call(
        body, name=name,
        out_shape=jax.ShapeDtypeStruct((N_DEV, rows, cols), v.dtype),
        in_specs=[pl.BlockSpec(memory_space=pltpu.VMEM)],
        out_specs=pl.BlockSpec(memory_space=pltpu.VMEM),
        scratch_shapes=[pltpu.SemaphoreType.DMA((N_DEV - 1,)), pltpu.SemaphoreType.DMA((N_DEV - 1,))],
        compiler_params=_params(None, VMEM_LIMIT),
    )(v)


def _ag_big(name, shards):
    n = len(shards)

    def body(*refs):
        ins, outs = refs[:n], refs[n:2 * n]
        send_sems, recv_sems, local_sems = refs[2 * n:]
        me = _my_pos()
        sib = _flip(me, 1)
        chips = [2, 4, 6]

        def copy(a, k, block, to, src=None):
            rows = outs[a].at[_index(block)]
            return pltpu.make_async_remote_copy(
                src_ref=rows if src is None else src, dst_ref=rows,
                send_sem=send_sems.at[a * 7 + k], recv_sem=recv_sems.at[a * 7 + k],
                device_id=to, device_id_type=MESH)

        started = []
        for a in range(n):
            mine = pltpu.make_async_copy(ins[a], outs[a].at[_index(me)], local_sems.at[a])
            mine.start()
            started.append(mine)
        sends = []
        for a in range(n):
            first = [copy(a, 0, me, sib, src=ins[a])]
            first += [copy(a, 1 + j, me, _flip(me, ch), src=ins[a]) for j, ch in enumerate(chips)]
            for cp in first:
                cp.start()
            sends += first
        for j, ch in enumerate(chips):
            for a in range(n):
                copy(a, 1 + j, _flip(me, ch), me).wait_recv()
                fwd = copy(a, 4 + j, _flip(me, ch), sib)
                fwd.start()
                sends.append(fwd)
        for a in range(n):
            copy(a, 0, sib, me).wait_recv()
            for j, ch in enumerate(chips):
                copy(a, 4 + j, _flip(sib, ch), me).wait_recv()
        for cp in sends:
            cp.wait_send()
        for mine in started:
            mine.wait()

    any_spec = pl.BlockSpec(memory_space=pl.ANY)
    return pl.pallas_call(
        body, name=name,
        out_shape=[jax.ShapeDtypeStruct((N_DEV,) + s.shape, s.dtype) for s in shards],
        in_specs=[any_spec] * n, out_specs=[any_spec] * n,
        scratch_shapes=[pltpu.SemaphoreType.DMA((7 * n,)), pltpu.SemaphoreType.DMA((7 * n,)),
                        pltpu.SemaphoreType.DMA((n,))],
    )(*shards)


def _rs_to_sibling(name, stacks):
    n = len(stacks)

    def body(*refs):
        ins, outs = refs[:n], refs[n:2 * n]
        send_sems, recv_sems = refs[2 * n:]
        me = _my_pos()
        sib = _flip(me, 1)
        sends = []
        for a in range(n):
            for m in range(4):
                target = _flip(sib, 2 * m)
                cp = pltpu.make_async_remote_copy(
                    src_ref=ins[a].at[_index(target)], dst_ref=outs[a].at[m],
                    send_sem=send_sems.at[a * 4 + m], recv_sem=recv_sems.at[a * 4 + m],
                    device_id=sib, device_id_type=MESH)
                cp.start()
                sends.append(cp)
        for cp in sends:
            cp.wait_recv()
        for cp in sends:
            cp.wait_send()

    any_spec = pl.BlockSpec(memory_space=pl.ANY)
    return pl.pallas_call(
        body, name=name,
        out_shape=[jax.ShapeDtypeStruct((4,) + s.shape[1:], s.dtype) for s in stacks],
        in_specs=[any_spec] * n, out_specs=[any_spec] * n,
        scratch_shapes=[pltpu.SemaphoreType.DMA((4 * n,)), pltpu.SemaphoreType.DMA((4 * n,))],
    )(*stacks)


def _rs_to_chips(name, sums):
    n = len(sums)

    def body(*refs):
        ins, outs = refs[:n], refs[n:2 * n]
        send_sems, recv_sems = refs[2 * n:]
        me = _my_pos()
        sends = []
        for a in range(n):
            for m in range(1, 4):
                cp = pltpu.make_async_remote_copy(
                    src_ref=ins[a].at[m - 1], dst_ref=outs[a].at[m - 1],
                    send_sem=send_sems.at[a * 3 + m - 1], recv_sem=recv_sems.at[a * 3 + m - 1],
                    device_id=_flip(me, 2 * m), device_id_type=MESH)
                cp.start()
                sends.append(cp)
        for cp in sends:
            cp.wait_recv()
        for cp in sends:
            cp.wait_send()

    any_spec = pl.BlockSpec(memory_space=pl.ANY)
    return pl.pallas_call(
        body, name=name,
        out_shape=[jax.ShapeDtypeStruct((3,) + s.shape[1:], s.dtype) for s in sums],
        in_specs=[any_spec] * n, out_specs=[any_spec] * n,
        scratch_shapes=[pltpu.SemaphoreType.DMA((3 * n,)), pltpu.SemaphoreType.DMA((3 * n,))],
    )(*sums)


def _add_sibling(name, stack, recv, targets):
    _, rows, cols = stack.shape
    tr = min(rows, ADD_ROWS)

    def own_body(t_ref, a_ref, b_ref, o_ref):
        o_ref[...] = a_ref[...] + b_ref[...]

    own = pl.pallas_call(
        own_body, name=name + "_own",
        out_shape=jax.ShapeDtypeStruct((rows, cols), F32),
        grid_spec=pltpu.PrefetchScalarGridSpec(
            num_scalar_prefetch=1, grid=(rows // tr,),
            in_specs=[pl.BlockSpec((None, tr, cols), lambda i, t: (t[0], i, 0)),
                      pl.BlockSpec((None, tr, cols), lambda i, t: (0, i, 0))],
            out_specs=pl.BlockSpec((tr, cols), lambda i, t: (i, 0))),
        compiler_params=_params(("arbitrary",)),
    )(targets, stack, recv)

    def send_body(t_ref, a_ref, b_ref, o_ref):
        o_ref[...] = (a_ref[...] + b_ref[...]).astype(BF16)

    send = pl.pallas_call(
        send_body, name=name + "_send",
        out_shape=jax.ShapeDtypeStruct((3, rows, cols), BF16),
        grid_spec=pltpu.PrefetchScalarGridSpec(
            num_scalar_prefetch=1, grid=(3, rows // tr),
            in_specs=[pl.BlockSpec((None, tr, cols), lambda m, i, t: (t[m + 1], i, 0)),
                      pl.BlockSpec((None, tr, cols), lambda m, i, t: (m + 1, i, 0))],
            out_specs=pl.BlockSpec((None, tr, cols), lambda m, i, t: (m, i, 0))),
        compiler_params=_params(("arbitrary", "arbitrary")),
    )(targets, stack, recv)
    return own, send


def _add_chips(name, own, recv):
    rows, cols = own.shape
    tr = min(rows, ADD_ROWS)

    def body(a_ref, b_ref, o_ref):
        o_ref[...] = ((a_ref[...] + b_ref[0].astype(F32)) + b_ref[1].astype(F32)) + b_ref[2].astype(F32)

    return pl.pallas_call(
        body, name=name,
        out_shape=jax.ShapeDtypeStruct((rows, cols), F32),
        grid=(rows // tr,),
        in_specs=[pl.BlockSpec((tr, cols), lambda i: (i, 0)),
                  pl.BlockSpec((3, tr, cols), lambda i: (0, i, 0))],
        out_specs=pl.BlockSpec((tr, cols), lambda i: (i, 0)),
        compiler_params=_params(("arbitrary",)),
    )(own, recv)


def _allreduce_small(name, v):
    rows, cols = v.shape
    half = rows // 2
    assert rows % (2 * SUBLANES) == 0

    def body(v_ref, out_ref, from_sib, chip_half, from_chips, send_sems, recv_sems):
        me = _my_pos()
        sib = _flip(me, 1)
        mine = pl.ds(pl.multiple_of(me[2] * half, SUBLANES), half)
        theirs = pl.ds(pl.multiple_of((1 - me[2]) * half, SUBLANES), half)

        def copy(k, src, dst, to):
            return pltpu.make_async_remote_copy(src_ref=src, dst_ref=dst, send_sem=send_sems.at[k],
                                                recv_sem=recv_sems.at[k], device_id=to, device_id_type=MESH)

        to_sib = copy(0, v_ref.at[theirs], from_sib, sib)
        to_sib.start()
        to_sib.wait_recv()
        chip_half[...] = v_ref[mine, :] + from_sib[...]
        to_chips = [copy(m, chip_half, from_chips.at[m - 1], _flip(me, 2 * m)) for m in range(1, 4)]
        for cp in to_chips:
            cp.start()
        for cp in to_chips:
            cp.wait_recv()
        my_chip = 2 * me[0] + me[1]
        total = None
        for chip in range(4):
            slot = jnp.maximum(jnp.bitwise_xor(chip, my_chip) - 1, 0)
            part = jnp.where(chip == my_chip, chip_half[...], from_chips[slot])
            total = part if total is None else total + part
        out_ref[mine, :] = total
        swap = copy(4, out_ref.at[mine], out_ref.at[mine], sib)
        swap.start()
        copy(4, out_ref.at[theirs], out_ref.at[theirs], sib).wait_recv()
        for cp in [to_sib, swap] + to_chips:
            cp.wait_send()

    return pl.pallas_call(
        body, name=name, out_shape=jax.ShapeDtypeStruct((rows, cols), F32),
        in_specs=[pl.BlockSpec(memory_space=pltpu.VMEM)],
        out_specs=pl.BlockSpec(memory_space=pltpu.VMEM),
        scratch_shapes=[pltpu.VMEM((half, cols), F32), pltpu.VMEM((half, cols), F32),
                        pltpu.VMEM((3, half, cols), F32),
                        pltpu.SemaphoreType.DMA((5,)), pltpu.SemaphoreType.DMA((5,))],
        compiler_params=_params(None, VMEM_LIMIT),
    )(v)


def _mod_fwd(c_all, w_mod):
    def body(c_ref, w_ref, o_ref):
        c = c_ref[...]
        o_ref[...] = jnp.dot(c * _sigmoid(c), w_ref[...], preferred_element_type=F32,
                             precision=lax.Precision.HIGHEST)

    return pl.pallas_call(
        body, name="mod_fwd", out_shape=jax.ShapeDtypeStruct((N_DEV, w_mod.shape[1]), F32),
    )(c_all, w_mod)


def _mod_bwd(c_all, dmod_all, dmod_cols):
    def body(c_ref, da_ref, dc_ref, gb_ref, gw_ref):
        c = c_ref[...]
        acc = da_ref[0:1, :]
        for b in range(1, N_DEV):
            acc = acc + da_ref[b:b + 1, :]
        gb_ref[...] = acc
        gw_ref[...] = lax.dot_general(c * _sigmoid(c), dc_ref[...], (((0,), (0,)), ((), ())),
                                      preferred_element_type=F32, precision=lax.Precision.HIGHEST)

    return pl.pallas_call(
        body, name="mod_bwd",
        out_shape=[jax.ShapeDtypeStruct((1, dmod_all.shape[1]), F32),
                   jax.ShapeDtypeStruct((c_all.shape[1], dmod_cols.shape[1]), F32)],
    )(c_all, dmod_all, dmod_cols)


def _rope_partner(t):
    lane = lax.broadcasted_iota(jnp.int32, t.shape, 1)
    return jnp.where(lane < ROT_HALF, pltpu.roll(t, HEAD_DIM - ROT_HALF, 1), pltpu.roll(t, ROT_HALF, 1))


def _norm(x, mod, b_mod, g_norm):
    seq = x.shape[0]
    tm = PROJ_ROWS

    def body(x_ref, mod_ref, bmod_ref, g_ref, h_ref):
        xf = x_ref[...]
        rstd = lax.rsqrt(jnp.mean(xf * xf, axis=-1, keepdims=True) + NORM_EPS)
        shift = mod_ref[:, 0:D_MODEL] + bmod_ref[:, 0:D_MODEL]
        scale = mod_ref[:, D_MODEL:2 * D_MODEL] + bmod_ref[:, D_MODEL:2 * D_MODEL]
        h_ref[...] = (((xf * rstd) * g_ref[...]) * (1.0 + scale) + shift).astype(BF16)

    row = pl.BlockSpec((tm, D_MODEL), lambda i: (i, 0))
    const = lambda cols: pl.BlockSpec((1, cols), lambda i: (0, 0))
    return pl.pallas_call(
        body, name="norm", out_shape=jax.ShapeDtypeStruct((seq, D_MODEL), BF16), grid=(seq // tm,),
        in_specs=[row, const(3 * D_MODEL), const(3 * D_MODEL), const(D_MODEL)], out_specs=row,
        compiler_params=_params(("arbitrary",), VMEM_LIMIT),
    )(x, mod, b_mod, g_norm)


def _proj(h, w_in_all, cosf, sinf):
    seq = h.shape[0]
    tm = PROJ_ROWS
    last = seq // tm - 1

    def body(h_ref, w_ref, cos_ref, sin_ref, pf_ref, q_ref, k_ref, v_ref):
        j = pl.program_id(0)

        @pl.when((j < 2) | (j > 4))
        def _():
            pf_ref[...] = _dot(h_ref[...], w_ref[...])

        def heads(dst_ref, rotate, gain):
            for pair in range(N_HEADS // 2):
                both = _dot(h_ref[...], w_ref[:, 2 * pair * HEAD_DIM:2 * (pair + 1) * HEAD_DIM])
                for hh in (2 * pair, 2 * pair + 1):
                    t = both[:, (hh % 2) * HEAD_DIM:(hh % 2 + 1) * HEAD_DIM]
                    if rotate:
                        t = t * cos_ref[...] + _rope_partner(t) * sin_ref[...]
                    dst_ref[hh] = t if gain is None else t * gain

        @pl.when(j == 2)
        def _():
            heads(q_ref, True, ATTN_SCALE)

        @pl.when(j == 3)
        def _():
            heads(k_ref, True, None)

        @pl.when(j == 4)
        def _():
            heads(v_ref, False, None)

    def pf_block(j, i):
        f32_piece = (j < 2) | (j > 4)
        return (jnp.where(f32_piece, i, last), jnp.where(j < 2, j, jnp.where(j < 5, 1, j - 3)))

    def hm_block(piece):
        return lambda j, i: (0, jnp.where(j == piece, i, jnp.where(j < piece, 0, last)), 0)

    hm = jax.ShapeDtypeStruct((N_HEADS, seq, HEAD_DIM), F32)
    hm_spec = lambda piece: pl.BlockSpec((N_HEADS, tm, HEAD_DIM), hm_block(piece))
    row = lambda j, i: (i, 0)
    return pl.pallas_call(
        body, name="proj",
        out_shape=[jax.ShapeDtypeStruct((seq, 5 * D_MODEL), F32), hm, hm, hm],
        grid=(8, seq // tm),
        in_specs=[pl.BlockSpec((tm, D_MODEL), row),
                  pl.BlockSpec((None, D_MODEL, D_MODEL), lambda j, i: (j, 0, 0)),
                  pl.BlockSpec((tm, HEAD_DIM), row), pl.BlockSpec((tm, HEAD_DIM), row)],
        out_specs=[pl.BlockSpec((tm, D_MODEL), pf_block), hm_spec(2), hm_spec(3), hm_spec(4)],
        compiler_params=_params(("arbitrary", "arbitrary"), VMEM_LIMIT),
    )(h, w_in_all, cosf, sinf)


def _shift_down(v, s, head):
    rows = v.shape[0]
    row = lax.broadcasted_iota(jnp.int32, v.shape, 0)
    fill = jnp.concatenate([pltpu.roll(head, s, 0), jnp.zeros((rows - SUBLANES, v.shape[1]), v.dtype)], axis=0)
    return jnp.where(row < s, fill, pltpu.roll(v, s, 0))


def _shift_up(v, s, tail):
    rows = v.shape[0]
    row = lax.broadcasted_iota(jnp.int32, v.shape, 0)
    fill = jnp.concatenate([jnp.zeros((rows - SUBLANES, v.shape[1]), v.dtype),
                            pltpu.roll(tail, SUBLANES - s, 0)], axis=0)
    return jnp.where(row >= rows - s, fill, pltpu.roll(v, rows - s, 0))


def _doubling(a, b, period, reverse):
    rows = a.shape[0]
    pos = lax.broadcasted_iota(jnp.int32, a.shape, 0) & (period - 1)
    k = 1
    while k < period:
        inside = (pos < period - k) if reverse else (pos >= k)
        shift = rows - k if reverse else k
        a_s = jnp.where(inside, pltpu.roll(a, shift, 0), 1.0)
        b_s = jnp.where(inside, pltpu.roll(b, shift, 0), 0.0)
        b = a * b_s + b
        a = a * a_s
        k *= 2
    return a, b


def _scan(a, b, boundary, reverse, a_scr, b_scr, spread):
    rows = a.shape[0]
    ntile = rows // SUBLANES
    a_scr[...], b_scr[...] = _doubling(a, b, SUBLANES, reverse)
    ends = pl.ds(0 if reverse else SUBLANES - 1, ntile, stride=SUBLANES)
    a_end, x_end = _doubling(a_scr[ends, :], b_scr[ends, :], ntile, reverse)
    x_end = x_end + a_end * boundary
    tile = lax.broadcasted_iota(jnp.int32, x_end.shape, 0)
    if reverse:
        incoming = jnp.where(tile == ntile - 1, boundary, pltpu.roll(x_end, ntile - 1, 0))
        last = x_end[0:1, :]
    else:
        incoming = jnp.where(tile == 0, boundary, pltpu.roll(x_end, 1, 0))
        last = x_end[ntile - 1:ntile, :]
    for s in range(SUBLANES):
        spread[pl.ds(s, ntile, stride=SUBLANES), :] = incoming
    return b_scr[...] + a_scr[...] * spread[...], last


def _conv_taps(xr, head):
    return [_shift_down(xr, 3, head), _shift_down(xr, 2, head), _shift_down(xr, 1, head), xr]


def _rnn_gates(xc, wa, ba, wx, bx, lam, keep):
    xcb = xc.astype(BF16)
    r = _sigmoid(_dot(xcb, wa.astype(BF16)) + ba)
    i = _sigmoid(_dot(xcb, wx.astype(BF16)) + bx)
    softplus = jnp.maximum(-lam, 0.0) + jnp.log(1.0 + jnp.exp(-jnp.abs(lam)))
    cl = -LRU_C * softplus
    log_a = cl * r
    a_raw = jnp.exp(log_a)
    mult_raw = jnp.sqrt(-_expm1_nonpos(2.0 * log_a, a_raw * a_raw))
    live = keep > 0.0
    return r, i, cl, a_raw, mult_raw, jnp.where(live, a_raw, 0.0), jnp.where(live, mult_raw, 1.0), live


def _rnn_specs(seq, rows, time_of):
    per = rows // SUBLANES
    vec = pl.BlockSpec((None, 1, 128), lambda hb, n: (hb, 0, 0))
    mat = pl.BlockSpec((None, 128, 128), lambda hb, n: (hb, 0, 0))
    return [pl.BlockSpec((rows, 128), lambda hb, n: (time_of(n), hb)),
            pl.BlockSpec((SUBLANES, 128), lambda hb, n: (jnp.maximum(time_of(n) * per - 1, 0), hb)),
            pl.BlockSpec((rows, 1), lambda hb, n: (time_of(n), 0)),
            pl.BlockSpec((None, SUBLANES, 128), lambda hb, n: (hb, 0, 0)),
            vec, mat, vec, mat, vec, vec]


def _rnn_fwd(pf, keep, conv_w8, conv_b, w_a, b_a, w_x, b_x, lam):
    seq = pf.shape[0]
    rows = RNN_ROWS

    def body(x_ref, xh_ref, keep_ref, cw_ref, cb_ref, wa_ref, ba_ref, wx_ref, bx_ref, lam_ref, hr_ref,
             carry, a_scr, b_scr, spread):
        n = pl.program_id(1)

        @pl.when(n == 0)
        def _():
            carry[...] = jnp.zeros_like(carry)

        xr = x_ref[...]
        head = jnp.where(n > 0, xh_ref[...], 0.0)
        taps = _conv_taps(xr, head)
        xc = cb_ref[...] + sum(cw_ref[k:k + 1, :] * taps[k] for k in range(4))
        _, i, _, _, _, a, mult, _ = _rnn_gates(xc, wa_ref[...], ba_ref[...], wx_ref[...], bx_ref[...],
                                               lam_ref[...], keep_ref[...])
        h, last = _scan(a, mult * i * xc, carry[0:1, :], False, a_scr, b_scr, spread)
        hr_ref[...] = h
        carry[...] = jnp.broadcast_to(last, carry.shape)

    chunk_f32 = pltpu.VMEM((rows, 128), F32)
    return pl.pallas_call(
        body, name="rnn_fwd",
        out_shape=jax.ShapeDtypeStruct((seq, D_MODEL), F32),
        grid=(RNN_BLOCKS, seq // rows),
        in_specs=_rnn_specs(seq, rows, lambda n: n),
        out_specs=pl.BlockSpec((rows, 128), lambda hb, n: (n, hb)),
        scratch_shapes=[pltpu.VMEM((SUBLANES, 128), F32), chunk_f32, chunk_f32, chunk_f32],
        compiler_params=_params(("arbitrary", "arbitrary"), VMEM_LIMIT),
    )(pf, pf, keep, conv_w8, conv_b, w_a, b_a, w_x, b_x, lam)


def _rnn_bwd(pf, hr, dhr, keep, conv_w8, conv_b, w_a, b_a, w_x, b_x, lam):
    seq = pf.shape[0]
    rows = RNN_ROWS
    nchunk = seq // rows
    per = rows // SUBLANES
    time_of = lambda n: nchunk - 1 - n

    def body(x_ref, xh_ref, keep_ref, cw_ref, cb_ref, wa_ref, ba_ref, wx_ref, bx_ref, lam_ref,
             hr_ref, hrh_ref, dhr_ref,
             dx_ref, gcw_ref, gcb_ref, gwa_ref, gba_ref, gwx_ref, gbx_ref, glam_ref,
             g_carry, dxc_tail, a_scr, b_scr, spread):
        n = pl.program_id(1)
        first_in_time = n == nchunk - 1

        @pl.when(n == 0)
        def _():
            g_carry[...] = jnp.zeros_like(g_carry)
            dxc_tail[...] = jnp.zeros_like(dxc_tail)
            for ref in (gcw_ref, gcb_ref, gwa_ref, gba_ref, gwx_ref, gbx_ref, glam_ref):
                ref[...] = jnp.zeros_like(ref)

        xr = x_ref[...]
        head = jnp.where(first_in_time, 0.0, xh_ref[...])
        taps = _conv_taps(xr, head)
        cw = cw_ref[...]
        xc = cb_ref[...] + sum(cw[k:k + 1, :] * taps[k] for k in range(4))
        wa, wx, lam = wa_ref[...], wx_ref[...], lam_ref[...]
        r, i, cl, a_raw, mult_raw, a, mult, live = _rnn_gates(xc, wa, ba_ref[...], wx, bx_ref[...], lam,
                                                               keep_ref[...])
        h_prev = _shift_down(hr_ref[...], 1, jnp.where(first_in_time, 0.0, hrh_ref[...]))

        row = lax.broadcasted_iota(jnp.int32, xr.shape, 0)
        last = row == rows - 1
        a_next = jnp.where(last, 0.0, pltpu.roll(a, rows - 1, 0))
        g, g_first = _scan(a_next, dhr_ref[...] + jnp.where(last, g_carry[0:1, :], 0.0),
                           jnp.zeros((1, 128), F32), True, a_scr, b_scr, spread)
        g_carry[...] = jnp.broadcast_to(a[0:1, :] * g_first, g_carry.shape)

        da = g * h_prev
        dmult = g * i * xc
        di = g * mult * xc
        dxc = g * mult * i
        dlog_a = jnp.where(live, da * a_raw - dmult * a_raw * a_raw / mult_raw, 0.0)
        dpa = (dlog_a * cl) * r * (1.0 - r)
        dpx = di * i * (1.0 - i)
        glam_ref[...] += jnp.sum(dlog_a * r, axis=0, keepdims=True) * (LRU_C * _sigmoid(-lam))
        xcb, dpab, dpxb = xc.astype(BF16), dpa.astype(BF16), dpx.astype(BF16)
        gwa_ref[...] += _dot_tn(xcb, dpab)
        gwx_ref[...] += _dot_tn(xcb, dpxb)
        gba_ref[...] += jnp.sum(dpa, axis=0, keepdims=True)
        gbx_ref[...] += jnp.sum(dpx, axis=0, keepdims=True)
        dxc = dxc + _dot_nt(dpab, wa.astype(BF16)) + _dot_nt(dpxb, wx.astype(BF16))

        gcb_ref[...] += jnp.sum(dxc, axis=0, keepdims=True)
        for k in range(4):
            gcw_ref[k:k + 1, :] += jnp.sum(dxc * taps[k], axis=0, keepdims=True)
        tail = dxc_tail[...]
        dx = cw[3:4, :] * dxc
        for k in range(3):
            dx = dx + cw[k:k + 1, :] * _shift_up(dxc, 3 - k, tail)
        dx_ref[...] = dx.astype(BF16)
        dxc_tail[...] = dxc[0:SUBLANES, :]

    blk = lambda hb, n: (hb, 0, 0)
    chunk = pl.BlockSpec((rows, 128), lambda hb, n: (time_of(n), hb))
    vec_out = pl.BlockSpec((None, 1, 128), blk)
    mat_out = pl.BlockSpec((None, 128, 128), blk)
    vec_shape = jax.ShapeDtypeStruct((RNN_BLOCKS, 1, 128), F32)
    mat_shape = jax.ShapeDtypeStruct((RNN_BLOCKS, 128, 128), F32)
    return pl.pallas_call(
        body, name="rnn_bwd",
        out_shape=[jax.ShapeDtypeStruct((seq, D_MODEL), BF16),
                   jax.ShapeDtypeStruct((RNN_BLOCKS, SUBLANES, 128), F32), vec_shape,
                   mat_shape, vec_shape, mat_shape, vec_shape, vec_shape],
        grid=(RNN_BLOCKS, nchunk),
        in_specs=_rnn_specs(seq, rows, time_of) + [
            chunk, pl.BlockSpec((SUBLANES, 128), lambda hb, n: (jnp.maximum(time_of(n) * per - 1, 0), hb)), chunk],
        out_specs=[chunk, pl.BlockSpec((None, SUBLANES, 128), blk), vec_out,
                   mat_out, vec_out, mat_out, vec_out, vec_out],
        scratch_shapes=[pltpu.VMEM((SUBLANES, 128), F32), pltpu.VMEM((SUBLANES, 128), F32)]
                       + [pltpu.VMEM((rows, 128), F32)] * 3,
        compiler_params=_params(("arbitrary", "arbitrary"), VMEM_LIMIT),
    )(pf, pf, keep, conv_w8, conv_b, w_a, b_a, w_x, b_x, lam, hr, hr, dhr)


def _unit_rows(dil, r, j):
    start = j * KEY_BLOCK * dil + r
    return pl.ds(start, KEY_BLOCK) if dil == 1 else pl.ds(start, KEY_BLOCK, stride=dil)


def _attn_fwd(q, k, v):
    nh, seq, _ = q.shape
    nchunk = seq // SPAN
    nblk = SPAN // KEY_BLOCK

    def body(q_ref, k_ref, v_ref, kp_ref, vp_ref, o_ref, l1_ref, l4_ref, l16_ref, acc, m_s, l_s):
        n = pl.program_id(1)
        qi = lax.broadcasted_iota(jnp.int32, (KEY_BLOCK, KEY_BLOCK), 0)
        ki = lax.broadcasted_iota(jnp.int32, (KEY_BLOCK, KEY_BLOCK), 1)
        bias_own = jnp.where(ki <= qi, 0.0, NEG_INF)
        bias_before = jnp.where(ki >= qi, 0.0, NEG_INF)
        bias_mid = jnp.concatenate([bias_before, bias_own], axis=1)
        bias_first = jnp.concatenate([jnp.where(n > 0, bias_before, NEG_INF), bias_own], axis=1)
        ones = jnp.ones((2 * KEY_BLOCK, HEAD_DIM), BF16)
        for gi, dil in enumerate(DILATIONS):
            nb = nblk // dil
            for r in range(dil):
                for j in range(nb):
                    rows = _unit_rows(dil, r, j)
                    if j == 0:
                        prow = _unit_rows(dil, r, nb - 1)
                        kp, vp, bias = kp_ref[prow, :], vp_ref[prow, :], bias_first
                    else:
                        prow = _unit_rows(dil, r, j - 1)
                        kp, vp, bias = k_ref[prow, :], v_ref[prow, :], bias_mid
                    qb = q_ref[rows, :].astype(BF16)
                    kcat = jnp.concatenate([kp, k_ref[rows, :]], axis=0).astype(BF16)
                    vcat = jnp.concatenate([vp, v_ref[rows, :]], axis=0).astype(BF16)
                    vaug = jnp.concatenate([vcat, ones], axis=1)
                    s = _dot_nt(qb, kcat) + bias
                    mx = jnp.max(s, axis=-1, keepdims=True)
                    if gi == 0:
                        m_new = jnp.broadcast_to(mx, (KEY_BLOCK, HEAD_DIM))
                    else:
                        m_old = m_s[rows, :]
                        m_new = jnp.maximum(m_old, mx)
                    p = jnp.exp(s - jnp.concatenate([m_new, m_new], axis=1))
                    pv = _dot(p.astype(BF16), vaug)
                    if gi == 0:
                        acc[rows, :] = pv[:, :HEAD_DIM]
                        l_s[rows, :] = pv[:, HEAD_DIM:]
                    else:
                        alpha = jnp.exp(m_old - m_new)
                        acc[rows, :] = alpha * acc[rows, :] + pv[:, :HEAD_DIM]
                        l_s[rows, :] = alpha * l_s[rows, :] + pv[:, HEAD_DIM:]
                    m_s[rows, :] = m_new
        den = l_s[...]
        o_ref[...] = acc[...] * (1.0 / den)
        m_s[...] = m_s[...] + jnp.log(den)
        diag = qi == ki
        for dil, out in zip(DILATIONS, (l1_ref, l4_ref, l16_ref)):
            nb = nblk // dil
            for r in range(dil):
                for j in range(nb):
                    blk = m_s[_unit_rows(dil, r, j), :]
                    out[r * nb + j:r * nb + j + 1, :] = jnp.sum(jnp.where(diag, blk, 0.0), axis=0, keepdims=True)

    blk = pl.BlockSpec((None, SPAN, HEAD_DIM), lambda h, n: (h, n, 0))
    pblk = pl.BlockSpec((None, SPAN, HEAD_DIM), lambda h, n: (h, jnp.maximum(n - 1, 0), 0))
    lblk = pl.BlockSpec((None, nblk, KEY_BLOCK), lambda h, n: (h, n, 0))
    lshape = jax.ShapeDtypeStruct((nh, seq // KEY_BLOCK, KEY_BLOCK), F32)
    span_f32 = pltpu.VMEM((SPAN, HEAD_DIM), F32)
    o, l1, l4, l16 = pl.pallas_call(
        body, name="attn_fwd",
        out_shape=[jax.ShapeDtypeStruct((nh, seq, HEAD_DIM), F32), lshape, lshape, lshape],
        grid=(nh, nchunk), in_specs=[blk, blk, blk, pblk, pblk], out_specs=[blk, lblk, lblk, lblk],
        scratch_shapes=[span_f32, span_f32, span_f32],
        compiler_params=_params(("arbitrary", "arbitrary"), VMEM_LIMIT),
    )(q, k, v, k, v)
    return o, (l1, l4, l16)


def _attn_bwd(q, k, v, do, o, lses, cosf, sinf):
    nh, seq, _ = q.shape
    nchunk = seq // SPAN
    nblk = SPAN // KEY_BLOCK

    def body(q_ref, k_ref, v_ref, do_ref, o_ref, kp_ref, vp_ref, qn_ref, don_ref, on_ref,
             l1_ref, l4_ref, l16_ref, l1n_ref, l4n_ref, l16n_ref, cos_ref, sin_ref,
             dq_ref, dk_ref, dv_ref, dq_acc, dk_acc, dv_acc):
        n = pl.program_id(1)
        ki = lax.broadcasted_iota(jnp.int32, (KEY_BLOCK, KEY_BLOCK), 0)
        qi = lax.broadcasted_iota(jnp.int32, (KEY_BLOCK, KEY_BLOCK), 1)
        bias_own = jnp.where(ki <= qi, 0.0, NEG_INF)
        bias_before = jnp.where(ki >= qi, 0.0, NEG_INF)
        bias_mid = jnp.concatenate([bias_before, bias_own], axis=0)
        bias_first = jnp.concatenate([jnp.where(n > 0, bias_before, NEG_INF), bias_own], axis=0)
        bias_next = jnp.where(n < nchunk - 1, bias_before, NEG_INF)
        ones8 = jnp.ones((SUBLANES, HEAD_DIM), BF16)
        for ref in (dq_acc, dk_acc, dv_acc):
            ref[...] = jnp.zeros_like(ref)

        def row_dot(a, b):
            prod = a * b
            hi = prod.astype(BF16)
            lo = (prod - hi.astype(F32)).astype(BF16)
            return (_dot_nt(ones8, hi) + _dot_nt(ones8, lo))[0:1, :]

        def tile(kb, vb, qb, dob, lse_row, delta_row, bias):
            pt = jnp.exp(_dot_nt(kb, qb) + bias - lse_row)
            dst = pt * (_dot_nt(vb, dob) - delta_row)
            return pt.astype(BF16), dst.astype(BF16)

        for dil, l_ref, ln_ref in zip(DILATIONS, (l1_ref, l4_ref, l16_ref), (l1n_ref, l4n_ref, l16n_ref)):
            nb = nblk // dil
            for r in range(dil):
                for j in range(nb):
                    rows = _unit_rows(dil, r, j)
                    if j == 0:
                        prow = _unit_rows(dil, r, nb - 1)
                        kp, vp, bias = kp_ref[prow, :], vp_ref[prow, :], bias_first
                    else:
                        prow = _unit_rows(dil, r, j - 1)
                        kp, vp, bias = k_ref[prow, :], v_ref[prow, :], bias_mid
                    dof = do_ref[rows, :]
                    qb, dob = q_ref[rows, :].astype(BF16), dof.astype(BF16)
                    kcat = jnp.concatenate([kp, k_ref[rows, :]], axis=0).astype(BF16)
                    vcat = jnp.concatenate([vp, v_ref[rows, :]], axis=0).astype(BF16)
                    pt, dst = tile(kcat, vcat, qb, dob, l_ref[r * nb + j:r * nb + j + 1, :],
                                   row_dot(dof, o_ref[rows, :]), bias)
                    dvc, dkc = _dot(pt, dob), _dot(dst, qb)
                    dq_acc[rows, :] += _dot_tn(dst, kcat)
                    dk_acc[rows, :] += dkc[KEY_BLOCK:, :]
                    dv_acc[rows, :] += dvc[KEY_BLOCK:, :]
                    if j > 0:
                        dk_acc[prow, :] += dkc[:KEY_BLOCK, :]
                        dv_acc[prow, :] += dvc[:KEY_BLOCK, :]
                lrow, nrow = _unit_rows(dil, r, nb - 1), _unit_rows(dil, r, 0)
                donf = don_ref[nrow, :]
                qnb, donb = qn_ref[nrow, :].astype(BF16), donf.astype(BF16)
                pt, dst = tile(k_ref[lrow, :].astype(BF16), v_ref[lrow, :].astype(BF16), qnb, donb,
                               ln_ref[r * nb:r * nb + 1, :], row_dot(donf, on_ref[nrow, :]), bias_next)
                dk_acc[lrow, :] += _dot(dst, qnb)
                dv_acc[lrow, :] += _dot(pt, donb)

        cos, sin = cos_ref[...], sin_ref[...]
        dq, dk = dq_acc[...], dk_acc[...]
        dq_ref[...] = ((dq * cos - _rope_partner(dq) * sin) * ATTN_SCALE).astype(BF16)
        dk_ref[...] = (dk * cos - _rope_partner(dk) * sin).astype(BF16)
        dv_ref[...] = dv_acc[...].astype(BF16)

    last = nchunk - 1
    cur = lambda h, n: (h, n, 0)
    prev = lambda h, n: (h, jnp.maximum(n - 1, 0), 0)
    nxt = lambda h, n: (h, jnp.minimum(n + 1, last), 0)
    blk = lambda idx: pl.BlockSpec((None, SPAN, HEAD_DIM), idx)
    lblk = lambda idx: pl.BlockSpec((None, nblk, KEY_BLOCK), idx)
    tab = pl.BlockSpec((SPAN, HEAD_DIM), lambda h, n: (n, 0))
    out = pl.BlockSpec((SPAN, HEAD_DIM), lambda h, n: (n, h))
    shape = jax.ShapeDtypeStruct((seq, nh * HEAD_DIM), BF16)
    span_f32 = pltpu.VMEM((SPAN, HEAD_DIM), F32)
    return pl.pallas_call(
        body, name="attn_bwd", out_shape=[shape, shape, shape], grid=(nh, nchunk),
        in_specs=[blk(cur)] * 5 + [blk(prev)] * 2 + [blk(nxt)] * 3 + [lblk(cur)] * 3 + [lblk(nxt)] * 3 + [tab, tab],
        out_specs=[out, out, out],
        scratch_shapes=[span_f32, span_f32, span_f32],
        compiler_params=_params(("arbitrary", "arbitrary"), VMEM_LIMIT),
    )(q, k, v, do, o, k, v, q, do, o, *lses, *lses, cosf, sinf)


def _hub(x, tgt, hr, pf, o_hm, mod, b_mod, b_gate, g_final, w_out_rnn, w_out_attn, w_o):
    seq = x.shape[0]
    tm = HUB_ROWS
    nsteps = seq // tm

    def body(x_ref, t_ref, hr_ref, zr_ref, za_ref, gr_ref, ga_ref, o_ref, mod_ref, bmod_ref, bg_ref, gf_ref,
             wr_hbm, wa_hbm, wo_hbm,
             dx2_ref, dhr_ref, dzr_ref, do_ref, dza_ref, dgr_ref, dga_ref,
             ur_ref, dyr_ref, ua_ref, dya_ref, mg_ref, dmo_ref,
             ggf_ref, gbg_ref, dgate_ref, loss_ref,
             wr, wa, wo, sem):
        step = pl.program_id(0)

        @pl.when(step == 0)
        def _():
            for src, dst in ((wr_hbm, wr), (wa_hbm, wa), (wo_hbm, wo)):
                cp = pltpu.make_async_copy(src, dst, sem)
                cp.start()
                cp.wait()
            for ref in (ggf_ref, gbg_ref, dgate_ref, loss_ref):
                ref[...] = jnp.zeros_like(ref)

        gate = mod_ref[:, 2 * D_MODEL:] + bmod_ref[:, 2 * D_MODEL:]
        gfin = gf_ref[...]
        hr_t, zr, za = hr_ref[...], zr_ref[...], za_ref[...]
        o = jnp.concatenate([o_ref[hh] for hh in range(N_HEADS)], axis=1)
        sig_zr, sig_za = _sigmoid(zr), _sigmoid(za)
        silu_zr, silu_za = zr * sig_zr, za * sig_za
        u_rnn = (hr_t * silu_zr).astype(BF16)
        u_attn = (o * silu_za).astype(BF16)
        y_rnn = _dot(u_rnn, wr[...])
        y_attn = _dot(u_attn, wa[...])
        sr = _sigmoid(gr_ref[...] + bg_ref[:, :D_MODEL])
        sa = _sigmoid(ga_ref[...] + bg_ref[:, D_MODEL:])
        merged = (sr * y_rnn + sa * y_attn).astype(BF16)
        mo = _dot(merged, wo[...])
        x2 = x_ref[...] + gate * mo
        rstd = lax.rsqrt(jnp.mean(x2 * x2, axis=-1, keepdims=True) + NORM_EPS)
        xn = x2 * rstd
        err = xn * gfin - t_ref[...]
        loss_ref[...] += 0.5 * jnp.sum(jnp.sum(err * err, axis=-1, keepdims=True) * (1.0 / D_MODEL),
                                       axis=0, keepdims=True)

        dy = err * (1.0 / D_MODEL)
        ggf_ref[...] += jnp.sum(dy * xn, axis=0, keepdims=True)
        dxn = dy * gfin
        dx2 = rstd * (dxn - xn * jnp.mean(dxn * xn, axis=-1, keepdims=True))
        dx2_ref[...] = dx2
        dgate_ref[...] += jnp.sum(dx2 * mo, axis=0, keepdims=True)
        dmo = (dx2 * gate).astype(BF16)
        dmerged = _dot_nt(dmo, wo[...])
        mg_ref[...] = merged
        dmo_ref[...] = dmo
        dy_rnn = (dmerged * sr).astype(BF16)
        dy_attn = (dmerged * sa).astype(BF16)
        dg_r = dmerged * y_rnn * sr * (1.0 - sr)
        dg_a = dmerged * y_attn * sa * (1.0 - sa)
        dgr_ref[...] = dg_r.astype(BF16)
        dga_ref[...] = dg_a.astype(BF16)
        gbg_ref[:, :D_MODEL] += jnp.sum(dg_r, axis=0, keepdims=True)
        gbg_ref[:, D_MODEL:] += jnp.sum(dg_a, axis=0, keepdims=True)
        du_rnn = _dot_nt(dy_rnn, wr[...])
        du_attn = _dot_nt(dy_attn, wa[...])
        ur_ref[...] = u_rnn
        dyr_ref[...] = dy_rnn
        ua_ref[...] = u_attn
        dya_ref[...] = dy_attn
        dhr_ref[...] = du_rnn * silu_zr
        dzr_ref[...] = (du_rnn * hr_t * (sig_zr * (1.0 + zr * (1.0 - sig_zr)))).astype(BF16)
        dza_ref[...] = (du_attn * o * (sig_za * (1.0 + za * (1.0 - sig_za)))).astype(BF16)
        d_o = du_attn * silu_za
        for hh in range(N_HEADS):
            do_ref[hh] = d_o[:, hh * HEAD_DIM:(hh + 1) * HEAD_DIM]

    row = pl.BlockSpec((tm, D_MODEL), lambda i: (i, 0))
    piece = lambda slot: pl.BlockSpec((tm, D_MODEL), lambda i: (i, slot))
    hm = pl.BlockSpec((N_HEADS, tm, HEAD_DIM), lambda i: (0, i, 0))
    const = lambda cols: pl.BlockSpec((1, cols), lambda i: (0, 0))
    any_spec = pl.BlockSpec(memory_space=pl.ANY)
    act_f32 = jax.ShapeDtypeStruct((seq, D_MODEL), F32)
    act_bf16 = jax.ShapeDtypeStruct((seq, D_MODEL), BF16)
    return pl.pallas_call(
        body, name="hub",
        out_shape=[act_f32, act_f32, act_bf16, jax.ShapeDtypeStruct((N_HEADS, seq, HEAD_DIM), F32),
                   act_bf16, act_bf16, act_bf16] + [act_bf16] * 6 + [
                   jax.ShapeDtypeStruct((1, D_MODEL), F32), jax.ShapeDtypeStruct((1, 2 * D_MODEL), F32),
                   jax.ShapeDtypeStruct((1, D_MODEL), F32), jax.ShapeDtypeStruct((1, 1), F32)],
        grid=(nsteps,),
        in_specs=[row, row, row, piece(1), piece(2), piece(3), piece(4), hm,
                  const(3 * D_MODEL), const(3 * D_MODEL), const(2 * D_MODEL), const(D_MODEL),
                  any_spec, any_spec, any_spec],
        out_specs=[row, row, row, hm, row, row, row] + [row] * 6 + [
                   const(D_MODEL), const(2 * D_MODEL), const(D_MODEL), const(1)],
        scratch_shapes=[pltpu.VMEM((D_MODEL, D_MODEL), BF16)] * 3 + [pltpu.SemaphoreType.DMA],
        compiler_params=_params(("arbitrary",), VMEM_LIMIT),
    )(x, tgt, hr, pf, pf, pf, pf, o_hm, mod, b_mod, b_gate, g_final, w_out_rnn, w_out_attn, w_o)


def _pair_grads(name, lefts, rights):
    n = len(rights)
    shared = len(lefts) == 1
    seq = rights[0].shape[0]
    tk = WGRAD_ROWS
    nk = seq // tk

    def body(*refs):
        l_refs, r_refs, out_ref = refs[:len(lefts)], refs[len(lefts):len(lefts) + n], refs[len(lefts) + n]
        j, kk = pl.program_id(0), pl.program_id(1)

        @pl.when(kk == 0)
        def _():
            out_ref[...] = jnp.zeros_like(out_ref)

        for m in range(n):
            @pl.when(j == m)
            def _(m=m):
                out_ref[...] += _dot_tn(l_refs[0 if shared else m][...], r_refs[m][...])

    def spec(m):
        return pl.BlockSpec((tk, D_MODEL), lambda j, kk: (jnp.where(j == m, kk, jnp.where(j < m, 0, nk - 1)), 0))

    left_specs = [pl.BlockSpec((tk, D_MODEL), lambda j, kk: (kk, 0))] if shared else [spec(m) for m in range(n)]
    return pl.pallas_call(
        body, name=name,
        out_shape=jax.ShapeDtypeStruct((n, D_MODEL, D_MODEL), F32),
        grid=(n, nk),
        in_specs=left_specs + [spec(m) for m in range(n)],
        out_specs=pl.BlockSpec((None, D_MODEL, D_MODEL), lambda j, kk: (j, 0, 0)),
        compiler_params=_params(("arbitrary", "arbitrary"), VMEM_LIMIT),
    )(*lefts, *rights)


def _dh_dx(pieces, w_in_all, x, dx2, mod, b_mod, g_norm):
    seq = x.shape[0]
    tm = DX_ROWS

    def body(*refs):
        p_refs = refs[:8]
        w_hbm, x_ref, dx2_ref, mod_ref, bmod_ref, g_ref = refs[8:14]
        gx_ref, dshift_ref, dscale_ref, ggn_ref, w_scr, sem = refs[14:]
        step = pl.program_id(0)

        @pl.when(step == 0)
        def _():
            cp = pltpu.make_async_copy(w_hbm, w_scr, sem)
            cp.start()
            cp.wait()
            for ref in (dshift_ref, dscale_ref, ggn_ref):
                ref[...] = jnp.zeros_like(ref)

        dh = _dot_nt(p_refs[0][...], w_scr[0])
        for j in range(1, 8):
            dh = dh + _dot_nt(p_refs[j][...], w_scr[j])
        scale1 = 1.0 + mod_ref[:, D_MODEL:2 * D_MODEL] + bmod_ref[:, D_MODEL:2 * D_MODEL]
        g = g_ref[...]
        xf = x_ref[...]
        rstd_t = lax.rsqrt(jnp.mean(xf * xf, axis=-1, keepdims=True) + NORM_EPS)
        xn = xf * rstd_t
        dshift_ref[...] += jnp.sum(dh, axis=0, keepdims=True)
        dscale_ref[...] += jnp.sum(dh * (xn * g), axis=0, keepdims=True)
        ggn_ref[...] += jnp.sum(dh * scale1 * xn, axis=0, keepdims=True)
        dxn = dh * (g * scale1)
        gx_ref[...] = rstd_t * (dxn - xn * jnp.mean(dxn * xn, axis=-1, keepdims=True)) + dx2_ref[...]

    row = pl.BlockSpec((tm, D_MODEL), lambda i: (i, 0))
    const = lambda cols: pl.BlockSpec((1, cols), lambda i: (0, 0))
    vec = jax.ShapeDtypeStruct((1, D_MODEL), F32)
    return pl.pallas_call(
        body, name="dh_dx",
        out_shape=[jax.ShapeDtypeStruct((seq, D_MODEL), F32), vec, vec, vec],
        grid=(seq // tm,),
        in_specs=[row] * 8 + [pl.BlockSpec(memory_space=pl.ANY), row, row,
                              const(3 * D_MODEL), const(3 * D_MODEL), const(D_MODEL)],
        out_specs=[row, const(D_MODEL), const(D_MODEL), const(D_MODEL)],
        scratch_shapes=[pltpu.VMEM((8, D_MODEL, D_MODEL), BF16), pltpu.SemaphoreType.DMA],
        compiler_params=_params(("arbitrary",), VMEM_LIMIT),
    )(*pieces, w_in_all, x, dx2, mod, b_mod, g_norm)


def _adamw(name, w, g, m, v):
    rows, cols = w.shape
    tr = rows if rows <= 256 else 256

    def body(w_ref, g_ref, m_ref, v_ref, d_ref, nm_ref, nv_ref):
        gv = g_ref[...]
        nm = ADAM_B1 * m_ref[...] + (1.0 - ADAM_B1) * gv
        nv = ADAM_B2 * v_ref[...] + (1.0 - ADAM_B2) * (gv * gv)
        m_hat = nm / (1.0 - ADAM_B1 ** ADAM_STEP)
        v_hat = nv / (1.0 - ADAM_B2 ** ADAM_STEP)
        d_ref[...] = -ADAM_LR * (m_hat / (jnp.sqrt(v_hat) + ADAM_EPS) + ADAM_WD * w_ref[...])
        nm_ref[...] = nm
        nv_ref[...] = nv

    spec = pl.BlockSpec((tr, cols), lambda i: (i, 0))
    shape = jax.ShapeDtypeStruct((rows, cols), F32)
    return pl.pallas_call(
        body, name=name, out_shape=[shape, shape, shape], grid=(rows // tr,),
        in_specs=[spec] * 4, out_specs=[spec] * 3,
        compiler_params=_params(("arbitrary",)),
    )(w, g, m, v)


def kernel(x, c, positions, g_norm, w_mod, b_mod, w_in, b_gate, conv_w, conv_b, w_a, b_a, w_x, b_x, lam, w_out_rnn, w_out_attn, w_o, g_final, loss_target, m_g_norm, m_w_mod, m_b_mod, m_w_in, m_b_gate, m_conv_w, m_conv_b, m_w_a, m_b_a, m_w_x, m_b_x, m_lam, m_w_out_rnn, m_w_out_attn, m_w_o, m_g_final, v_g_norm, v_w_mod, v_b_mod, v_w_in, v_b_gate, v_conv_w, v_conv_b, v_w_a, v_b_a, v_w_x, v_b_x, v_lam, v_w_out_rnn, v_w_out_attn, v_w_o, v_g_final):
    seq = x.shape[1]
    me = _index(_my_pos())
    xs, tgt = x[0], loss_target[0]

    pos = positions[0].astype(F32)[:, None]
    inv_freq = ROPE_THETA ** (-jnp.arange(0, 2 * ROT_HALF, 2, dtype=F32) / (2 * ROT_HALF))
    ang = pos * inv_freq
    rest = HEAD_DIM - 2 * ROT_HALF
    cosf = jnp.concatenate([jnp.cos(ang), jnp.cos(ang), jnp.ones((seq, rest), F32)], axis=1)
    sinf = jnp.concatenate([-jnp.sin(ang), jnp.sin(ang), jnp.zeros((seq, rest), F32)], axis=1)
    keep = (positions[0] != 0).astype(F32)[:, None]

    w_in_all, w_or_all, w_oa_all, w_o_all = _ag_big(
        "gather_weights", [w_in[0].astype(BF16), w_out_rnn[0].astype(BF16),
                           w_out_attn[0].astype(BF16), w_o[0].astype(BF16)])
    w_or_all, w_oa_all, w_o_all = (t.reshape(D_MODEL, D_MODEL) for t in (w_or_all, w_oa_all, w_o_all))
    conv_w8 = _ag_small("gather_conv_w", jnp.pad(conv_w[0], ((0, SUBLANES - 4), (0, 0))))
    c_all = _ag_small("gather_c", jnp.broadcast_to(c, (SUBLANES, D_MODEL)))[:, 0, :]
    mod_cols = w_mod.shape[2]
    mod_part = _ag_small("gather_mod", _mod_fwd(c_all, w_mod[0]))
    mod = lax.dynamic_index_in_dim(mod_part, me, axis=1, keepdims=False).reshape(1, N_DEV * mod_cols)

    blocks = lambda t: t.reshape(RNN_BLOCKS, 1, 128)
    rnn_params = (conv_w8, blocks(conv_b), w_a[0], blocks(b_a), w_x[0], blocks(b_x), blocks(lam))

    h = _norm(xs, mod, b_mod, g_norm)
    pf, q, k, v = _proj(h, w_in_all, cosf, sinf)
    hr = _rnn_fwd(pf, keep, *rnn_params)
    o, lses = _attn_fwd(q, k, v)

    (dx2, dhr, dz_rnn, d_o, dz_attn, dg_r, dg_a, u_rnn, dy_rnn, u_attn, dy_attn, merged, dmo,
     gp_g_final, gp_b_gate, dgate, loss_part) = _hub(
        xs, tgt, hr, pf, o, mod, b_mod, b_gate, g_final.reshape(1, D_MODEL), w_or_all, w_oa_all, w_o_all)
    gp_w_or, gp_w_oa, gp_w_o = _pair_grads("out_grads", [u_rnn, u_attn, merged], [dy_rnn, dy_attn, dmo])
    dq, dk, dv = _attn_bwd(q, k, v, d_o, o, lses, cosf, sinf)
    dx_rnn, gp_conv_w, gp_conv_b, gp_w_a, gp_b_a, gp_w_x, gp_b_x, gp_lam = _rnn_bwd(pf, hr, dhr, keep, *rnn_params)
    pieces = [dx_rnn, dz_rnn, dq, dk, dv, dz_attn, dg_r, dg_a]
    grad_x, dshift, dscale, gp_g_norm = _dh_dx(pieces, w_in_all, xs, dx2, mod, b_mod, g_norm)
    gp_w_in = _pair_grads("w_in_grad", [h], pieces)

    dmod = jnp.concatenate([dshift, dscale, dgate], axis=1)
    dmod_all = _ag_small("gather_dmod", jnp.broadcast_to(dmod, (SUBLANES, 3 * D_MODEL)))[:, 0, :]
    dmod_cols = lax.dynamic_slice_in_dim(dmod_all, me * mod_cols, mod_cols, axis=1)
    g_b_mod, g_w_mod = _mod_bwd(c_all, dmod_all, dmod_cols)

    flat = lambda t: t.reshape(-1, 128)
    small = [flat(gp_g_norm), flat(gp_b_gate), flat(gp_conv_b), flat(gp_b_a), flat(gp_b_x), flat(gp_lam),
             flat(gp_g_final), flat(gp_conv_w), jnp.broadcast_to(loss_part, (SUBLANES, 128)),
             flat(gp_w_a), flat(gp_w_x)]
    sizes = [t.shape[0] for t in small]
    small.append(jnp.zeros((-sum(sizes) % (2 * SUBLANES), 128), F32))
    total = _allreduce_small("allreduce_small_grads", jnp.concatenate(small, axis=0))
    offs = [sum(sizes[:i]) for i in range(len(sizes))]
    (g_g_norm, g_b_gate, g_conv_b, g_b_a, g_b_x, g_lam, g_g_final, g_conv_w_all, loss_rows, g_w_a, g_w_x) = (
        total[o_:o_ + s_] for o_, s_ in zip(offs, sizes))
    loss = loss_rows[0, 0]
    g_conv_w = lax.dynamic_index_in_dim(g_conv_w_all.reshape(RNN_BLOCKS, SUBLANES, 128), me, axis=0,
                                        keepdims=False)[:4]

    stacks = [gp_w_in, gp_w_or.reshape(N_DEV, 128, D_MODEL), gp_w_oa.reshape(N_DEV, 128, D_MODEL),
              gp_w_o.reshape(N_DEV, 128, D_MODEL)]
    from_sib = _rs_to_sibling("rs_sibling", stacks)
    targets = jnp.bitwise_xor(me, 2 * jnp.arange(4, dtype=jnp.int32)).astype(jnp.int32)
    sums = [_add_sibling("rs_add_sibling_%d" % a, s_, r_, targets) for a, (s_, r_) in enumerate(zip(stacks, from_sib))]
    from_chips = _rs_to_chips("rs_chips", [send for _, send in sums])
    g_w_in, g_w_or, g_w_oa, g_w_o = (
        _add_chips("rs_add_chips_%d" % a, own, r_) for a, ((own, _), r_) in enumerate(zip(sums, from_chips)))

    weights = [
        ("g_norm", g_norm, g_g_norm, m_g_norm, v_g_norm, (SUBLANES, 128)),
        ("w_mod", w_mod, g_w_mod, m_w_mod, v_w_mod, (D_MODEL, mod_cols)),
        ("b_mod", b_mod, g_b_mod, m_b_mod, v_b_mod, (3 * SUBLANES, 128)),
        ("w_in", w_in, g_w_in, m_w_in, v_w_in, (D_MODEL, D_MODEL)),
        ("b_gate", b_gate, g_b_gate, m_b_gate, v_b_gate, (2 * SUBLANES, 128)),
        ("conv_w", conv_w, g_conv_w, m_conv_w, v_conv_w, (4, 128)),
        ("conv_b", conv_b, g_conv_b, m_conv_b, v_conv_b, (SUBLANES, 128)),
        ("w_a", w_a, g_w_a, m_w_a, v_w_a, (RNN_BLOCKS * 128, 128)),
        ("b_a", b_a, g_b_a, m_b_a, v_b_a, (SUBLANES, 128)),
        ("w_x", w_x, g_w_x, m_w_x, v_w_x, (RNN_BLOCKS * 128, 128)),
        ("b_x", b_x, g_b_x, m_b_x, v_b_x, (SUBLANES, 128)),
        ("lam", lam, g_lam, m_lam, v_lam, (SUBLANES, 128)),
        ("w_out_rnn", w_out_rnn, g_w_or, m_w_out_rnn, v_w_out_rnn, (128, D_MODEL)),
        ("w_out_attn", w_out_attn, g_w_oa, m_w_out_attn, v_w_out_attn, (128, D_MODEL)),
        ("w_o", w_o, g_w_o, m_w_o, v_w_o, (128, D_MODEL)),
        ("g_final", g_final, g_g_final, m_g_final, v_g_final, (SUBLANES, 128)),
    ]
    out_g, out_d, out_m, out_v = [], [], [], []
    for name, w_, g_, m_, v_, shape2 in weights:
        d_, nm_, nv_ = _adamw("adamw_" + name, w_.reshape(shape2), g_.reshape(shape2), m_.reshape(shape2),
                              v_.reshape(shape2))
        out_g.append(g_.reshape(w_.shape))
        out_d.append(d_.reshape(w_.shape))
        out_m.append(nm_.reshape(w_.shape))
        out_v.append(nv_.reshape(w_.shape))
    return (loss, grad_x[None], *out_g, *out_d, *out_m, *out_v)
```

```python
import jax
import jax.numpy as jnp
from jax import lax
from jax.experimental import pallas as pl
from jax.experimental.pallas import tpu as pltpu

F32 = jnp.float32
BF16 = jnp.bfloat16
MESH = pl.DeviceIdType.MESH

D_MODEL = 1024
N_HEADS = 8
HEAD_DIM = 128
RNN_BLOCKS = 8
N_DEV = 8
ROT_HALF = 16
ROPE_THETA = 500000.0
DILATIONS = (1, 4, 16)
KEY_BLOCK = 128
SPAN = KEY_BLOCK * DILATIONS[-1]
ATTN_SCALE = HEAD_DIM ** -0.5
NORM_EPS = 1e-6
LRU_C = 8.0
NEG_INF = -1e30
ADAM_LR, ADAM_B1, ADAM_B2, ADAM_EPS, ADAM_WD, ADAM_STEP = 0.001, 0.9, 0.999, 1e-08, 0.01, 10

SUBLANES = 8
VMEM_LIMIT = 56 * 1024 * 1024
PROJ_ROWS = 1024
RNN_ROWS = 512
HUB_ROWS = 256
DX_ROWS = 256
WGRAD_ROWS = 1024
ADD_ROWS = 256


def _params(sem=None, vmem=None):
    return pltpu.CompilerParams(dimension_semantics=sem, vmem_limit_bytes=vmem)


def _dot(a, b):
    return jnp.dot(a, b, preferred_element_type=F32)


def _dot_nt(a, b):
    return lax.dot_general(a, b, (((1,), (1,)), ((), ())), preferred_element_type=F32)


def _dot_tn(a, b):
    return lax.dot_general(a, b, (((0,), (0,)), ((), ())), preferred_element_type=F32)


def _sigmoid(z):
    return 1.0 / (1.0 + jnp.exp(-z))


def _expm1_nonpos(z, exp_z):
    return jnp.where(z > -0.01, z * (1.0 + 0.5 * z), exp_z - 1.0)


def _my_pos():
    return lax.axis_index("x"), lax.axis_index("y"), lax.axis_index("c")


def _flip(pos, k):
    x, y, c = pos
    return ((1 - x) if k & 4 else x, (1 - y) if k & 2 else y, (1 - c) if k & 1 else c)


def _index(pos):
    return 4 * pos[0] + 2 * pos[1] + pos[2]


def _ag_small(name, v):
    rows, cols = v.shape

    def body(v_ref, out_ref, send_sems, recv_sems):
        me = _my_pos()
        out_ref[_index(me)] = v_ref[...]
        sends = []
        for k in range(1, N_DEV):
            cp = pltpu.make_async_remote_copy(
                src_ref=v_ref, dst_ref=out_ref.at[_index(me)], send_sem=send_sems.at[k - 1],
                recv_sem=recv_sems.at[k - 1], device_id=_flip(me, k), device_id_type=MESH)
            cp.start()
            sends.append(cp)
        for k in range(1, N_DEV):
            peer = _flip(me, k)
            pltpu.make_async_remote_copy(
                src_ref=v_ref, dst_ref=out_ref.at[_index(peer)], send_sem=send_sems.at[k - 1],
                recv_sem=recv_sems.at[k - 1], device_id=peer, device_id_type=MESH).wait_recv()
        for cp in sends:
            cp.wait_send()

    return pl.pallas_call(
        body, name=name,
        out_shape=jax.ShapeDtypeStruct((N_DEV, rows, cols), v.dtype),
        in_specs=[pl.BlockSpec(memory_space=pltpu.VMEM)],
        out_specs=pl.BlockSpec(memory_space=pltpu.VMEM),
        scratch_shapes=[pltpu.SemaphoreType.DMA((N_DEV - 1,)), pltpu.SemaphoreType.DMA((N_DEV - 1,))],
        compiler_params=_params(None, VMEM_LIMIT),
    )(v)


def _ag_big(name, shards):
    n = len(shards)

    def body(*refs):
        ins, outs = refs[:n], refs[n:2 * n]
        send_sems, recv_sems, local_sems = refs[2 * n:]
        me = _my_pos()
        sib = _flip(me, 1)
        chips = [2, 4, 6]

        def copy(a, k, block, to, src=None):
            rows = outs[a].at[_index(block)]
            return pltpu.make_async_remote_copy(
                src_ref=rows if src is None else src, dst_ref=rows,
                send_sem=send_sems.at[a * 7 + k], recv_sem=recv_sems.at[a * 7 + k],
                device_id=to, device_id_type=MESH)

        started = []
        for a in range(n):
            mine = pltpu.make_async_copy(ins[a], outs[a].at[_index(me)], local_sems.at[a])
            mine.start()
            started.append(mine)
        sends = []
        for a in range(n):
            first = [copy(a, 0, me, sib, src=ins[a])]
            first += [copy(a, 1 + j, me, _flip(me, ch), src=ins[a]) for j, ch in enumerate(chips)]
            for cp in first:
                cp.start()
            sends += first
        for j, ch in enumerate(chips):
            for a in range(n):
                copy(a, 1 + j, _flip(me, ch), me).wait_recv()
                fwd = copy(a, 4 + j, _flip(me, ch), sib)
                fwd.start()
                sends.append(fwd)
        for a in range(n):
            copy(a, 0, sib, me).wait_recv()
            for j, ch in enumerate(chips):
                copy(a, 4 + j, _flip(sib, ch), me).wait_recv()
        for cp in sends:
            cp.wait_send()
        for mine in started:
            mine.wait()

    any_spec = pl.BlockSpec(memory_space=pl.ANY)
    return pl.pallas_call(
        body, name=name,
        out_shape=[jax.ShapeDtypeStruct((N_DEV,) + s.shape, s.dtype) for s in shards],
        in_specs=[any_spec] * n, out_specs=[any_spec] * n,
        scratch_shapes=[pltpu.SemaphoreType.DMA((7 * n,)), pltpu.SemaphoreType.DMA((7 * n,)),
                        pltpu.SemaphoreType.DMA((n,))],
    )(*shards)


def _rs_to_sibling(name, stacks):
    n = len(stacks)

    def body(*refs):
        ins, outs = refs[:n], refs[n:2 * n]
        send_sems, recv_sems = refs[2 * n:]
        me = _my_pos()
        sib = _flip(me, 1)
        sends = []
        for a in range(n):
            for m in range(4):
                target = _flip(sib, 2 * m)
                cp = pltpu.make_async_remote_copy(
                    src_ref=ins[a].at[_index(target)], dst_ref=outs[a].at[m],
                    send_sem=send_sems.at[a * 4 + m], recv_sem=recv_sems.at[a * 4 + m],
                    device_id=sib, device_id_type=MESH)
                cp.start()
                sends.append(cp)
        for cp in sends:
            cp.wait_recv()
        for cp in sends:
            cp.wait_send()

    any_spec = pl.BlockSpec(memory_space=pl.ANY)
    return pl.pallas_call(
        body, name=name,
        out_shape=[jax.ShapeDtypeStruct((4,) + s.shape[1:], s.dtype) for s in stacks],
        in_specs=[any_spec] * n, out_specs=[any_spec] * n,
        scratch_shapes=[pltpu.SemaphoreType.DMA((4 * n,)), pltpu.SemaphoreType.DMA((4 * n,))],
    )(*stacks)


def _rs_to_chips(name, sums):
    n = len(sums)

    def body(*refs):
        ins, outs = refs[:n], refs[n:2 * n]
        send_sems, recv_sems = refs[2 * n:]
        me = _my_pos()
        sends = []
        for a in range(n):
            for m in range(1, 4):
                cp = pltpu.make_async_remote_copy(
                    src_ref=ins[a].at[m - 1], dst_ref=outs[a].at[m - 1],
                    send_sem=send_sems.at[a * 3 + m - 1], recv_sem=recv_sems.at[a * 3 + m - 1],
                    device_id=_flip(me, 2 * m), device_id_type=MESH)
                cp.start()
                sends.append(cp)
        for cp in sends:
            cp.wait_recv()
        for cp in sends:
            cp.wait_send()

    any_spec = pl.BlockSpec(memory_space=pl.ANY)
    return pl.pallas_call(
        body, name=name,
        out_shape=[jax.ShapeDtypeStruct((3,) + s.shape[1:], s.dtype) for s in sums],
        in_specs=[any_spec] * n, out_specs=[any_spec] * n,
        scratch_shapes=[pltpu.SemaphoreType.DMA((3 * n,)), pltpu.SemaphoreType.DMA((3 * n,))],
    )(*sums)


def _add_sibling(name, stack, recv, targets):
    _, rows, cols = stack.shape
    tr = min(rows, ADD_ROWS)

    def own_body(t_ref, a_ref, b_ref, o_ref):
        o_ref[...] = a_ref[...] + b_ref[...]

    own = pl.pallas_call(
        own_body, name=name + "_own",
        out_shape=jax.ShapeDtypeStruct((rows, cols), F32),
        grid_spec=pltpu.PrefetchScalarGridSpec(
            num_scalar_prefetch=1, grid=(rows // tr,),
            in_specs=[pl.BlockSpec((None, tr, cols), lambda i, t: (t[0], i, 0)),
                      pl.BlockSpec((None, tr, cols), lambda i, t: (0, i, 0))],
            out_specs=pl.BlockSpec((tr, cols), lambda i, t: (i, 0))),
        compiler_params=_params(("arbitrary",)),
    )(targets, stack, recv)

    def send_body(t_ref, a_ref, b_ref, o_ref):
        o_ref[...] = (a_ref[...] + b_ref[...]).astype(BF16)

    send = pl.pallas_call(
        send_body, name=name + "_send",
        out_shape=jax.ShapeDtypeStruct((3, rows, cols), BF16),
        grid_spec=pltpu.PrefetchScalarGridSpec(
            num_scalar_prefetch=1, grid=(3, rows // tr),
            in_specs=[pl.BlockSpec((None, tr, cols), lambda m, i, t: (t[m + 1], i, 0)),
                      pl.BlockSpec((None, tr, cols), lambda m, i, t: (m + 1, i, 0))],
            out_specs=pl.BlockSpec((None, tr, cols), lambda m, i, t: (m, i, 0))),
        compiler_params=_params(("arbitrary", "arbitrary")),
    )(targets, stack, recv)
    return own, send


def _add_chips(name, own, recv):
    rows, cols = own.shape
    tr = min(rows, ADD_ROWS)

    def body(a_ref, b_ref, o_ref):
        o_ref[...] = ((a_ref[...] + b_ref[0].astype(F32)) + b_ref[1].astype(F32)) + b_ref[2].astype(F32)

    return pl.pallas_call(
        body, name=name,
        out_shape=jax.ShapeDtypeStruct((rows, cols), F32),
        grid=(rows // tr,),
        in_specs=[pl.BlockSpec((tr, cols), lambda i: (i, 0)),
                  pl.BlockSpec((3, tr, cols), lambda i: (0, i, 0))],
        out_specs=pl.BlockSpec((tr, cols), lambda i: (i, 0)),
        compiler_params=_params(("arbitrary",)),
    )(own, recv)


def _allreduce_small(name, v):
    rows, cols = v.shape
    half = rows // 2
    assert rows % (2 * SUBLANES) == 0

    def body(v_ref, out_ref, from_sib, chip_half, from_chips, send_sems, recv_sems):
        me = _my_pos()
        sib = _flip(me, 1)
        mine = pl.ds(pl.multiple_of(me[2] * half, SUBLANES), half)
        theirs = pl.ds(pl.multiple_of((1 - me[2]) * half, SUBLANES), half)

        def copy(k, src, dst, to):
            return pltpu.make_async_remote_copy(src_ref=src, dst_ref=dst, send_sem=send_sems.at[k],
                                                recv_sem=recv_sems.at[k], device_id=to, device_id_type=MESH)

        to_sib = copy(0, v_ref.at[theirs], from_sib, sib)
        to_sib.start()
        to_sib.wait_recv()
        chip_half[...] = v_ref[mine, :] + from_sib[...]
        to_chips = [copy(m, chip_half, from_chips.at[m - 1], _flip(me, 2 * m)) for m in range(1, 4)]
        for cp in to_chips:
            cp.start()
        for cp in to_chips:
            cp.wait_recv()
        my_chip = 2 * me[0] + me[1]
        total = None
        for chip in range(4):
            slot = jnp.maximum(jnp.bitwise_xor(chip, my_chip) - 1, 0)
            part = jnp.where(chip == my_chip, chip_half[...], from_chips[slot])
            total = part if total is None else total + part
        out_ref[mine, :] = total
        swap = copy(4, out_ref.at[mine], out_ref.at[mine], sib)
        swap.start()
        copy(4, out_ref.at[theirs], out_ref.at[theirs], sib).wait_recv()
        for cp in [to_sib, swap] + to_chips:
            cp.wait_send()

    return pl.pallas_call(
        body, name=name, out_shape=jax.ShapeDtypeStruct((rows, cols), F32),
        in_specs=[pl.BlockSpec(memory_space=pltpu.VMEM)],
        out_specs=pl.BlockSpec(memory_space=pltpu.VMEM),
        scratch_shapes=[pltpu.VMEM((half, cols), F32), pltpu.VMEM((half, cols), F32),
                        pltpu.VMEM((3, half, cols), F32),
                        pltpu.SemaphoreType.DMA((5,)), pltpu.SemaphoreType.DMA((5,))],
        compiler_params=_params(None, VMEM_LIMIT),
    )(v)


def _mod_fwd(c_all, w_mod):
    def body(c_ref, w_ref, o_ref):
        c = c_ref[...]
        o_ref[...] = jnp.dot(c * _sigmoid(c), w_ref[...], preferred_element_type=F32,
                             precision=lax.Precision.HIGHEST)

    return pl.pallas_call(
        body, name="mod_fwd", out_shape=jax.ShapeDtypeStruct((N_DEV, w_mod.shape[1]), F32),
    )(c_all, w_mod)


def _mod_bwd(c_all, dmod_all, dmod_cols):
    def body(c_ref, da_ref, dc_ref, gb_ref, gw_ref):
        c = c_ref[...]
        acc = da_ref[0:1, :]
        for b in range(1, N_DEV):
            acc = acc + da_ref[b:b + 1, :]
        gb_ref[...] = acc
        gw_ref[...] = lax.dot_general(c * _sigmoid(c), dc_ref[...], (((0,), (0,)), ((), ())),
                                      preferred_element_type=F32, precision=lax.Precision.HIGHEST)

    return pl.pallas_call(
        body, name="mod_bwd",
        out_shape=[jax.ShapeDtypeStruct((1, dmod_all.shape[1]), F32),
                   jax.ShapeDtypeStruct((c_all.shape[1], dmod_cols.shape[1]), F32)],
    )(c_all, dmod_all, dmod_cols)


def _rope_partner(t):
    lane = lax.broadcasted_iota(jnp.int32, t.shape, 1)
    return jnp.where(lane < ROT_HALF, pltpu.roll(t, HEAD_DIM - ROT_HALF, 1), pltpu.roll(t, ROT_HALF, 1))


def _norm(x, mod, b_mod, g_norm):
    seq = x.shape[0]
    tm = PROJ_ROWS

    def body(x_ref, mod_ref, bmod_ref, g_ref, h_ref):
        xf = x_ref[...]
        rstd = lax.rsqrt(jnp.mean(xf * xf, axis=-1, keepdims=True) + NORM_EPS)
        shift = mod_ref[:, 0:D_MODEL] + bmod_ref[:, 0:D_MODEL]
        scale = mod_ref[:, D_MODEL:2 * D_MODEL] + bmod_ref[:, D_MODEL:2 * D_MODEL]
        h_ref[...] = (((xf * rstd) * g_ref[...]) * (1.0 + scale) + shift).astype(BF16)

    row = pl.BlockSpec((tm, D_MODEL), lambda i: (i, 0))
    const = lambda cols: pl.BlockSpec((1, cols), lambda i: (0, 0))
    return pl.pallas_call(
        body, name="norm", out_shape=jax.ShapeDtypeStruct((seq, D_MODEL), BF16), grid=(seq // tm,),
        in_specs=[row, const(3 * D_MODEL), const(3 * D_MODEL), const(D_MODEL)], out_specs=row,
        compiler_params=_params(("arbitrary",), VMEM_LIMIT),
    )(x, mod, b_mod, g_norm)


def _proj(h, w_in_all, cosf, sinf):
    seq = h.shape[0]
    tm = PROJ_ROWS
    last = seq // tm - 1

    def body(h_ref, w_ref, cos_ref, sin_ref, pf_ref, q_ref, k_ref, v_ref):
        j = pl.program_id(0)

        @pl.when((j < 2) | (j > 4))
        def _():
            pf_ref[...] = _dot(h_ref[...], w_ref[...])

        def heads(dst_ref, rotate, gain):
            for pair in range(N_HEADS // 2):
                both = _dot(h_ref[...], w_ref[:, 2 * pair * HEAD_DIM:2 * (pair + 1) * HEAD_DIM])
                for hh in (2 * pair, 2 * pair + 1):
                    t = both[:, (hh % 2) * HEAD_DIM:(hh % 2 + 1) * HEAD_DIM]
                    if rotate:
                        t = t * cos_ref[...] + _rope_partner(t) * sin_ref[...]
                    dst_ref[hh] = t if gain is None else t * gain

        @pl.when(j == 2)
        def _():
            heads(q_ref, True, ATTN_SCALE)

        @pl.when(j == 3)
        def _():
            heads(k_ref, True, None)

        @pl.when(j == 4)
        def _():
            heads(v_ref, False, None)

    def pf_block(j, i):
        f32_piece = (j < 2) | (j > 4)
        return (jnp.where(f32_piece, i, last), jnp.where(j < 2, j, jnp.where(j < 5, 1, j - 3)))

    def hm_block(piece):
        return lambda j, i: (0, jnp.where(j == piece, i, jnp.where(j < piece, 0, last)), 0)

    hm = jax.ShapeDtypeStruct((N_HEADS, seq, HEAD_DIM), F32)
    hm_spec = lambda piece: pl.BlockSpec((N_HEADS, tm, HEAD_DIM), hm_block(piece))
    row = lambda j, i: (i, 0)
    return pl.pallas_call(
        body, name="proj",
        out_shape=[jax.ShapeDtypeStruct((seq, 5 * D_MODEL), F32), hm, hm, hm],
        grid=(8, seq // tm),
        in_specs=[pl.BlockSpec((tm, D_MODEL), row),
                  pl.BlockSpec((None, D_MODEL, D_MODEL), lambda j, i: (j, 0, 0)),
                  pl.BlockSpec((tm, HEAD_DIM), row), pl.BlockSpec((tm, HEAD_DIM), row)],
        out_specs=[pl.BlockSpec((tm, D_MODEL), pf_block), hm_spec(2), hm_spec(3), hm_spec(4)],
        compiler_params=_params(("arbitrary", "arbitrary"), VMEM_LIMIT),
    )(h, w_in_all, cosf, sinf)


def _shift_down(v, s, head):
    rows = v.shape[0]
    row = lax.broadcasted_iota(jnp.int32, v.shape, 0)
    fill = jnp.concatenate([pltpu.roll(head, s, 0), jnp.zeros((rows - SUBLANES, v.shape[1]), v.dtype)], axis=0)
    return jnp.where(row < s, fill, pltpu.roll(v, s, 0))


def _shift_up(v, s, tail):
    rows = v.shape[0]
    row = lax.broadcasted_iota(jnp.int32, v.shape, 0)
    fill = jnp.concatenate([jnp.zeros((rows - SUBLANES, v.shape[1]), v.dtype),
                            pltpu.roll(tail, SUBLANES - s, 0)], axis=0)
    return jnp.where(row >= rows - s, fill, pltpu.roll(v, rows - s, 0))


def _doubling(a, b, period, reverse):
    rows = a.shape[0]
    pos = lax.broadcasted_iota(jnp.int32, a.shape, 0) & (period - 1)
    k = 1
    while k < period:
        inside = (pos < period - k) if reverse else (pos >= k)
        shift = rows - k if reverse else k
        a_s = jnp.where(inside, pltpu.roll(a, shift, 0), 1.0)
        b_s = jnp.where(inside, pltpu.roll(b, shift, 0), 0.0)
        b = a * b_s + b
        a = a * a_s
        k *= 2
    return a, b


def _scan(a, b, boundary, reverse, a_scr, b_scr, spread):
    rows = a.shape[0]
    ntile = rows // SUBLANES
    a_scr[...], b_scr[...] = _doubling(a, b, SUBLANES, reverse)
    ends = pl.ds(0 if reverse else SUBLANES - 1, ntile, stride=SUBLANES)
    a_end, x_end = _doubling(a_scr[ends, :], b_scr[ends, :], ntile, reverse)
    x_end = x_end + a_end * boundary
    tile = lax.broadcasted_iota(jnp.int32, x_end.shape, 0)
    if reverse:
        incoming = jnp.where(tile == ntile - 1, boundary, pltpu.roll(x_end, ntile - 1, 0))
        last = x_end[0:1, :]
    else:
        incoming = jnp.where(tile == 0, boundary, pltpu.roll(x_end, 1, 0))
        last = x_end[ntile - 1:ntile, :]
    for s in range(SUBLANES):
        spread[pl.ds(s, ntile, stride=SUBLANES), :] = incoming
    return b_scr[...] + a_scr[...] * spread[...], last


def _conv_taps(xr, head):
    return [_shift_down(xr, 3, head), _shift_down(xr, 2, head), _shift_down(xr, 1, head), xr]


def _rnn_gates(xc, wa, ba, wx, bx, lam, keep):
    xcb = xc.astype(BF16)
    r = _sigmoid(_dot(xcb, wa.astype(BF16)) + ba)
    i = _sigmoid(_dot(xcb, wx.astype(BF16)) + bx)
    softplus = jnp.maximum(-lam, 0.0) + jnp.log(1.0 + jnp.exp(-jnp.abs(lam)))
    cl = -LRU_C * softplus
    log_a = cl * r
    a_raw = jnp.exp(log_a)
    mult_raw = jnp.sqrt(-_expm1_nonpos(2.0 * log_a, a_raw * a_raw))
    live = keep > 0.0
    return r, i, cl, a_raw, mult_raw, jnp.where(live, a_raw, 0.0), jnp.where(live, mult_raw, 1.0), live


def _rnn_specs(seq, rows, time_of):
    per = rows // SUBLANES
    vec = pl.BlockSpec((None, 1, 128), lambda hb, n: (hb, 0, 0))
    mat = pl.BlockSpec((None, 128, 128), lambda hb, n: (hb, 0, 0))
    return [pl.BlockSpec((rows, 128), lambda hb, n: (time_of(n), hb)),
            pl.BlockSpec((SUBLANES, 128), lambda hb, n: (jnp.maximum(time_of(n) * per - 1, 0), hb)),
            pl.BlockSpec((rows, 1), lambda hb, n: (time_of(n), 0)),
            pl.BlockSpec((None, SUBLANES, 128), lambda hb, n: (hb, 0, 0)),
            vec, mat, vec, mat, vec, vec]


def _rnn_fwd(pf, keep, conv_w8, conv_b, w_a, b_a, w_x, b_x, lam):
    seq = pf.shape[0]
    rows = RNN_ROWS

    def body(x_ref, xh_ref, keep_ref, cw_ref, cb_ref, wa_ref, ba_ref, wx_ref, bx_ref, lam_ref, hr_ref,
             carry, a_scr, b_scr, spread):
        n = pl.program_id(1)

        @pl.when(n == 0)
        def _():
            carry[...] = jnp.zeros_like(carry)

        xr = x_ref[...]
        head = jnp.where(n > 0, xh_ref[...], 0.0)
        taps = _conv_taps(xr, head)
        xc = cb_ref[...] + sum(cw_ref[k:k + 1, :] * taps[k] for k in range(4))
        _, i, _, _, _, a, mult, _ = _rnn_gates(xc, wa_ref[...], ba_ref[...], wx_ref[...], bx_ref[...],
                                               lam_ref[...], keep_ref[...])
        h, last = _scan(a, mult * i * xc, carry[0:1, :], False, a_scr, b_scr, spread)
        hr_ref[...] = h
        carry[...] = jnp.broadcast_to(last, carry.shape)

    chunk_f32 = pltpu.VMEM((rows, 128), F32)
    return pl.pallas_call(
        body, name="rnn_fwd",
        out_shape=jax.ShapeDtypeStruct((seq, D_MODEL), F32),
        grid=(RNN_BLOCKS, seq // rows),
        in_specs=_rnn_specs(seq, rows, lambda n: n),
        out_specs=pl.BlockSpec((rows, 128), lambda hb, n: (n, hb)),
        scratch_shapes=[pltpu.VMEM((SUBLANES, 128), F32), chunk_f32, chunk_f32, chunk_f32],
        compiler_params=_params(("arbitrary", "arbitrary"), VMEM_LIMIT),
    )(pf, pf, keep, conv_w8, conv_b, w_a, b_a, w_x, b_x, lam)


def _rnn_bwd(pf, hr, dhr, keep, conv_w8, conv_b, w_a, b_a, w_x, b_x, lam):
    seq = pf.shape[0]
    rows = RNN_ROWS
    nchunk = seq // rows
    per = rows // SUBLANES
    time_of = lambda n: nchunk - 1 - n

    def body(x_ref, xh_ref, keep_ref, cw_ref, cb_ref, wa_ref, ba_ref, wx_ref, bx_ref, lam_ref,
             hr_ref, hrh_ref, dhr_ref,
             dx_ref, gcw_ref, gcb_ref, gwa_ref, gba_ref, gwx_ref, gbx_ref, glam_ref,
             g_carry, dxc_tail, a_scr, b_scr, spread):
        n = pl.program_id(1)
        first_in_time = n == nchunk - 1

        @pl.when(n == 0)
        def _():
            g_carry[...] = jnp.zeros_like(g_carry)
            dxc_tail[...] = jnp.zeros_like(dxc_tail)
            for ref in (gcw_ref, gcb_ref, gwa_ref, gba_ref, gwx_ref, gbx_ref, glam_ref):
                ref[...] = jnp.zeros_like(ref)

        xr = x_ref[...]
        head = jnp.where(first_in_time, 0.0, xh_ref[...])
        taps = _conv_taps(xr, head)
        cw = cw_ref[...]
        xc = cb_ref[...] + sum(cw[k:k + 1, :] * taps[k] for k in range(4))
        wa, wx, lam = wa_ref[...], wx_ref[...], lam_ref[...]
        r, i, cl, a_raw, mult_raw, a, mult, live = _rnn_gates(xc, wa, ba_ref[...], wx, bx_ref[...], lam,
                                                               keep_ref[...])
        h_prev = _shift_down(hr_ref[...], 1, jnp.where(first_in_time, 0.0, hrh_ref[...]))

        row = lax.broadcasted_iota(jnp.int32, xr.shape, 0)
        last = row == rows - 1
        a_next = jnp.where(last, 0.0, pltpu.roll(a, rows - 1, 0))
        g, g_first = _scan(a_next, dhr_ref[...] + jnp.where(last, g_carry[0:1, :], 0.0),
                           jnp.zeros((1, 128), F32), True, a_scr, b_scr, spread)
        g_carry[...] = jnp.broadcast_to(a[0:1, :] * g_first, g_carry.shape)

        da = g * h_prev
        dmult = g * i * xc
        di = g * mult * xc
        dxc = g * mult * i
        dlog_a = jnp.where(live, da * a_raw - dmult * a_raw * a_raw / mult_raw, 0.0)
        dpa = (dlog_a * cl) * r * (1.0 - r)
        dpx = di * i * (1.0 - i)
        glam_ref[...] += jnp.sum(dlog_a * r, axis=0, keepdims=True) * (LRU_C * _sigmoid(-lam))
        xcb, dpab, dpxb = xc.astype(BF16), dpa.astype(BF16), dpx.astype(BF16)
        gwa_ref[...] += _dot_tn(xcb, dpab)
        gwx_ref[...] += _dot_tn(xcb, dpxb)
        gba_ref[...] += jnp.sum(dpa, axis=0, keepdims=True)
        gbx_ref[...] += jnp.sum(dpx, axis=0, keepdims=True)
        dxc = dxc + _dot_nt(dpab, wa.astype(BF16)) + _dot_nt(dpxb, wx.astype(BF16))

        gcb_ref[...] += jnp.sum(dxc, axis=0, keepdims=True)
        for k in range(4):
            gcw_ref[k:k + 1, :] += jnp.sum(dxc * taps[k], axis=0, keepdims=True)
        tail = dxc_tail[...]
        dx = cw[3:4, :] * dxc
        for k in range(3):
            dx = dx + cw[k:k + 1, :] * _shift_up(dxc, 3 - k, tail)
        dx_ref[...] = dx.astype(BF16)
        dxc_tail[...] = dxc[0:SUBLANES, :]

    blk = lambda hb, n: (hb, 0, 0)
    chunk = pl.BlockSpec((rows, 128), lambda hb, n: (time_of(n), hb))
    vec_out = pl.BlockSpec((None, 1, 128), blk)
    mat_out = pl.BlockSpec((None, 128, 128), blk)
    vec_shape = jax.ShapeDtypeStruct((RNN_BLOCKS, 1, 128), F32)
    mat_shape = jax.ShapeDtypeStruct((RNN_BLOCKS, 128, 128), F32)
    return pl.pallas_call(
        body, name="rnn_bwd",
        out_shape=[jax.ShapeDtypeStruct((seq, D_MODEL), BF16),
                   jax.ShapeDtypeStruct((RNN_BLOCKS, SUBLANES, 128), F32), vec_shape,
                   mat_shape, vec_shape, mat_shape, vec_shape, vec_shape],
        grid=(RNN_BLOCKS, nchunk),
        in_specs=_rnn_specs(seq, rows, time_of) + [
            chunk, pl.BlockSpec((SUBLANES, 128), lambda hb, n: (jnp.maximum(time_of(n) * per - 1, 0), hb)), chunk],
        out_specs=[chunk, pl.BlockSpec((None, SUBLANES, 128), blk), vec_out,
                   mat_out, vec_out, mat_out, vec_out, vec_out],
        scratch_shapes=[pltpu.VMEM((SUBLANES, 128), F32), pltpu.VMEM((SUBLANES, 128), F32)]
                       + [pltpu.VMEM((rows, 128), F32)] * 3,
        compiler_params=_params(("arbitrary", "arbitrary"), VMEM_LIMIT),
    )(pf, pf, keep, conv_w8, conv_b, w_a, b_a, w_x, b_x, lam, hr, hr, dhr)


def _unit_rows(dil, r, j):
    start = j * KEY_BLOCK * dil + r
    return pl.ds(start, KEY_BLOCK) if dil == 1 else pl.ds(start, KEY_BLOCK, stride=dil)


def _attn_fwd(q, k, v):
    nh, seq, _ = q.shape
    nchunk = seq // SPAN
    nblk = SPAN // KEY_BLOCK

    def body(q_ref, k_ref, v_ref, kp_ref, vp_ref, o_ref, l1_ref, l4_ref, l16_ref, acc, m_s, l_s):
        n = pl.program_id(1)
        qi = lax.broadcasted_iota(jnp.int32, (KEY_BLOCK, KEY_BLOCK), 0)
        ki = lax.broadcasted_iota(jnp.int32, (KEY_BLOCK, KEY_BLOCK), 1)
        bias_own = jnp.where(ki <= qi, 0.0, NEG_INF)
        bias_before = jnp.where(ki >= qi, 0.0, NEG_INF)
        bias_mid = jnp.concatenate([bias_before, bias_own], axis=1)
        bias_first = jnp.concatenate([jnp.where(n > 0, bias_before, NEG_INF), bias_own], axis=1)
        ones = jnp.ones((2 * KEY_BLOCK, HEAD_DIM), BF16)
        for gi, dil in enumerate(DILATIONS):
            nb = nblk // dil
            for r in range(dil):
                for j in range(nb):
                    rows = _unit_rows(dil, r, j)
                    if j == 0:
                        prow = _unit_rows(dil, r, nb - 1)
                        kp, vp, bias = kp_ref[prow, :], vp_ref[prow, :], bias_first
                    else:
                        prow = _unit_rows(dil, r, j - 1)
                        kp, vp, bias = k_ref[prow, :], v_ref[prow, :], bias_mid
                    qb = q_ref[rows, :].astype(BF16)
                    kcat = jnp.concatenate([kp, k_ref[rows, :]], axis=0).astype(BF16)
                    vcat = jnp.concatenate([vp, v_ref[rows, :]], axis=0).astype(BF16)
                    vaug = jnp.concatenate([vcat, ones], axis=1)
                    s = _dot_nt(qb, kcat) + bias
                    mx = jnp.max(s, axis=-1, keepdims=True)
                    if gi == 0:
                        m_new = jnp.broadcast_to(mx, (KEY_BLOCK, HEAD_DIM))
                    else:
                        m_old = m_s[rows, :]
                        m_new = jnp.maximum(m_old, mx)
                    p = jnp.exp(s - jnp.concatenate([m_new, m_new], axis=1))
                    pv = _dot(p.astype(BF16), vaug)
                    if gi == 0:
                        acc[rows, :] = pv[:, :HEAD_DIM]
                        l_s[rows, :] = pv[:, HEAD_DIM:]
                    else:
                        alpha = jnp.exp(m_old - m_new)
                        acc[rows, :] = alpha * acc[rows, :] + pv[:, :HEAD_DIM]
                        l_s[rows, :] = alpha * l_s[rows, :] + pv[:, HEAD_DIM:]
                    m_s[rows, :] = m_new
        den = l_s[...]
        o_ref[...] = acc[...] * (1.0 / den)
        m_s[...] = m_s[...] + jnp.log(den)
        diag = qi == ki
        for dil, out in zip(DILATIONS, (l1_ref, l4_ref, l16_ref)):
            nb = nblk // dil
            for r in range(dil):
                for j in range(nb):
                    blk = m_s[_unit_rows(dil, r, j), :]
                    out[r * nb + j:r * nb + j + 1, :] = jnp.sum(jnp.where(diag, blk, 0.0), axis=0, keepdims=True)

    blk = pl.BlockSpec((None, SPAN, HEAD_DIM), lambda h, n: (h, n, 0))
    pblk = pl.BlockSpec((None, SPAN, HEAD_DIM), lambda h, n: (h, jnp.maximum(n - 1, 0), 0))
    lblk = pl.BlockSpec((None, nblk, KEY_BLOCK), lambda h, n: (h, n, 0))
    lshape = jax.ShapeDtypeStruct((nh, seq // KEY_BLOCK, KEY_BLOCK), F32)
    span_f32 = pltpu.VMEM((SPAN, HEAD_DIM), F32)
    o, l1, l4, l16 = pl.pallas_call(
        body, name="attn_fwd",
        out_shape=[jax.ShapeDtypeStruct((nh, seq, HEAD_DIM), F32), lshape, lshape, lshape],
        grid=(nh, nchunk), in_specs=[blk, blk, blk, pblk, pblk], out_specs=[blk, lblk, lblk, lblk],
        scratch_shapes=[span_f32, span_f32, span_f32],
        compiler_params=_params(("arbitrary", "arbitrary"), VMEM_LIMIT),
    )(q, k, v, k, v)
    return o, (l1, l4, l16)


def _to_residue_major(src, tmp, dst):
    quarter = SPAN // 4
    for r4 in range(4):
        tmp[r4 * quarter:(r4 + 1) * quarter, :] = src[pl.ds(r4, quarter, stride=4), :]
    for r4 in range(4):
        for rp in range(4):
            r = r4 + 4 * rp
            dst[r * KEY_BLOCK:(r + 1) * KEY_BLOCK, :] = tmp[pl.ds(r4 * quarter + rp, KEY_BLOCK, stride=4), :]


def _add_from_residue_major(src, tmp, acc):
    quarter = SPAN // 4
    for r4 in range(4):
        for rp in range(4):
            r = r4 + 4 * rp
            tmp[pl.ds(r4 * quarter + rp, KEY_BLOCK, stride=4), :] = src[r * KEY_BLOCK:(r + 1) * KEY_BLOCK, :]
    for r4 in range(4):
        acc[pl.ds(r4, quarter, stride=4), :] += tmp[r4 * quarter:(r4 + 1) * quarter, :]


def _attn_bwd(q, k, v, do, o, lses, cosf, sinf):
    nh, seq, _ = q.shape
    nchunk = seq // SPAN
    nblk = SPAN // KEY_BLOCK
    wide = DILATIONS[-1]
    assert SPAN == wide * KEY_BLOCK

    def body(q_ref, k_ref, v_ref, do_ref, o_ref, kp_ref, vp_ref, l1_ref, l4_ref, l16_ref,
             cos_ref, sin_ref, cosp_ref, sinp_ref, dq_ref, dk_ref, dv_ref,
             dq_acc, dkc_acc, dvc_acc, dkp_acc, dvp_acc, q16, k16, v16, do16, o16, k16p, v16p,
             dq16, dkc16, dvc16, dkp16, dvp16, tmp):
        n = pl.program_id(1)
        ki = lax.broadcasted_iota(jnp.int32, (KEY_BLOCK, KEY_BLOCK), 0)
        qi = lax.broadcasted_iota(jnp.int32, (KEY_BLOCK, KEY_BLOCK), 1)
        bias_own = jnp.where(ki <= qi, 0.0, NEG_INF)
        bias_before = jnp.where(ki >= qi, 0.0, NEG_INF)
        bias_mid = jnp.concatenate([bias_before, bias_own], axis=0)
        bias_first = jnp.concatenate([jnp.where(n > 0, bias_before, NEG_INF), bias_own], axis=0)
        ones8 = jnp.ones((SUBLANES, HEAD_DIM), BF16)

        def row_dot(a, b):
            prod = a * b
            hi = prod.astype(BF16)
            lo = (prod - hi.astype(F32)).astype(BF16)
            return (_dot_nt(ones8, hi) + _dot_nt(ones8, lo))[0:1, :]

        def unit(qf, dof, of, kp, kc, vp, vc, lse_row, bias):
            qb, dob = qf.astype(BF16), dof.astype(BF16)
            kcat = jnp.concatenate([kp, kc], axis=0).astype(BF16)
            vcat = jnp.concatenate([vp, vc], axis=0).astype(BF16)
            pt = jnp.exp(_dot_nt(kcat, qb) + bias - lse_row)
            dst = (pt * (_dot_nt(vcat, dob) - row_dot(dof, of))).astype(BF16)
            return _dot_tn(dst, kcat), _dot(dst, qb), _dot(pt.astype(BF16), dob)

        @pl.when(n == 0)
        def _():
            for ref in (dkp_acc, dvp_acc, dkp16, dvp16, k16p, v16p):
                ref[...] = jnp.zeros_like(ref)

        @pl.when(n < nchunk)
        def _():
            for ref in (dq_acc, dkc_acc, dvc_acc, dq16, dkc16, dvc16):
                ref[...] = jnp.zeros_like(ref)
            for src, dst in ((q_ref, q16), (k_ref, k16), (v_ref, v16), (do_ref, do16), (o_ref, o16)):
                _to_residue_major(src, tmp, dst)
            for dil, l_ref in zip(DILATIONS[:-1], (l1_ref, l4_ref)):
                nb = nblk // dil
                for r in range(dil):
                    for j in range(nb):
                        rows = _unit_rows(dil, r, j)
                        if j == 0:
                            prow = _unit_rows(dil, r, nb - 1)
                            kp, vp, bias = kp_ref[prow, :], vp_ref[prow, :], bias_first
                        else:
                            prow = _unit_rows(dil, r, j - 1)
                            kp, vp, bias = k_ref[prow, :], v_ref[prow, :], bias_mid
                        dq, dkk, dvv = unit(q_ref[rows, :], do_ref[rows, :], o_ref[rows, :], kp, k_ref[rows, :],
                                            vp, v_ref[rows, :], l_ref[r * nb + j:r * nb + j + 1, :], bias)
                        dq_acc[rows, :] += dq
                        dkc_acc[rows, :] += dkk[KEY_BLOCK:, :]
                        dvc_acc[rows, :] += dvv[KEY_BLOCK:, :]
                        before_k, before_v = (dkp_acc, dvp_acc) if j == 0 else (dkc_acc, dvc_acc)
                        before_k[prow, :] += dkk[:KEY_BLOCK, :]
                        before_v[prow, :] += dvv[:KEY_BLOCK, :]
            for r in range(wide):
                rows = pl.ds(r * KEY_BLOCK, KEY_BLOCK)
                dq, dkk, dvv = unit(q16[rows, :], do16[rows, :], o16[rows, :], k16p[rows, :], k16[rows, :],
                                    v16p[rows, :], v16[rows, :], l16_ref[r:r + 1, :], bias_first)
                dq16[rows, :] += dq
                dkc16[rows, :] += dkk[KEY_BLOCK:, :]
                dvc16[rows, :] += dvv[KEY_BLOCK:, :]
                dkp16[rows, :] += dkk[:KEY_BLOCK, :]
                dvp16[rows, :] += dvv[:KEY_BLOCK, :]
            _add_from_residue_major(dq16, tmp, dq_acc)
            dq = dq_acc[...]
            dq_ref[...] = ((dq * cos_ref[...] - _rope_partner(dq) * sin_ref[...]) * ATTN_SCALE).astype(BF16)

        @pl.when(n > 0)
        def _():
            _add_from_residue_major(dkp16, tmp, dkp_acc)
            _add_from_residue_major(dvp16, tmp, dvp_acc)
            dk = dkp_acc[...]
            dk_ref[...] = (dk * cosp_ref[...] - _rope_partner(dk) * sinp_ref[...]).astype(BF16)
            dv_ref[...] = dvp_acc[...].astype(BF16)

        @pl.when(n < nchunk)
        def _():
            for src, dst in ((dkc_acc, dkp_acc), (dvc_acc, dvp_acc), (dkc16, dkp16), (dvc16, dvp16),
                             (k16, k16p), (v16, v16p)):
                dst[...] = src[...]

    last = nchunk - 1
    cur = lambda h, n: (h, jnp.minimum(n, last), 0)
    prev = lambda h, n: (h, jnp.clip(n - 1, 0, last), 0)
    blk = lambda idx: pl.BlockSpec((None, SPAN, HEAD_DIM), idx)
    lblk = pl.BlockSpec((None, nblk, KEY_BLOCK), cur)
    tab = pl.BlockSpec((SPAN, HEAD_DIM), lambda h, n: (jnp.minimum(n, last), 0))
    tabp = pl.BlockSpec((SPAN, HEAD_DIM), lambda h, n: (jnp.clip(n - 1, 0, last), 0))
    out_q = pl.BlockSpec((SPAN, HEAD_DIM), lambda h, n: (jnp.minimum(n, last), h))
    out_kv = pl.BlockSpec((SPAN, HEAD_DIM), lambda h, n: (jnp.clip(n - 1, 0, last), h))
    shape = jax.ShapeDtypeStruct((seq, nh * HEAD_DIM), BF16)
    return pl.pallas_call(
        body, name="attn_bwd", out_shape=[shape, shape, shape], grid=(nh, nchunk + 1),
        in_specs=[blk(cur)] * 5 + [blk(prev)] * 2 + [lblk] * 3 + [tab, tab, tabp, tabp],
        out_specs=[out_q, out_kv, out_kv],
        scratch_shapes=[pltpu.VMEM((SPAN, HEAD_DIM), F32)] * 18,
        compiler_params=_params(("arbitrary", "arbitrary"), VMEM_LIMIT),
    )(q, k, v, do, o, k, v, *lses, cosf, sinf, cosf, sinf)


def _hub(x, tgt, hr, pf, o_hm, mod, b_mod, b_gate, g_final, w_out_rnn, w_out_attn, w_o):
    seq = x.shape[0]
    tm = HUB_ROWS
    nsteps = seq // tm

    def body(x_ref, t_ref, hr_ref, zr_ref, za_ref, gr_ref, ga_ref, o_ref, mod_ref, bmod_ref, bg_ref, gf_ref,
             wr_hbm, wa_hbm, wo_hbm,
             dx2_ref, dhr_ref, dzr_ref, do_ref, dza_ref, dgr_ref, dga_ref,
             ur_ref, dyr_ref, ua_ref, dya_ref, mg_ref, dmo_ref,
             ggf_ref, gbg_ref, dgate_ref, loss_ref,
             wr, wa, wo, sem):
        step = pl.program_id(0)

        @pl.when(step == 0)
        def _():
            for src, dst in ((wr_hbm, wr), (wa_hbm, wa), (wo_hbm, wo)):
                cp = pltpu.make_async_copy(src, dst, sem)
                cp.start()
                cp.wait()
            for ref in (ggf_ref, gbg_ref, dgate_ref, loss_ref):
                ref[...] = jnp.zeros_like(ref)

        gate = mod_ref[:, 2 * D_MODEL:] + bmod_ref[:, 2 * D_MODEL:]
        gfin = gf_ref[...]
        hr_t, zr, za = hr_ref[...], zr_ref[...], za_ref[...]
        o = jnp.concatenate([o_ref[hh] for hh in range(N_HEADS)], axis=1)
        sig_zr, sig_za = _sigmoid(zr), _sigmoid(za)
        silu_zr, silu_za = zr * sig_zr, za * sig_za
        u_rnn = (hr_t * silu_zr).astype(BF16)
        u_attn = (o * silu_za).astype(BF16)
        y_rnn = _dot(u_rnn, wr[...])
        y_attn = _dot(u_attn, wa[...])
        sr = _sigmoid(gr_ref[...] + bg_ref[:, :D_MODEL])
        sa = _sigmoid(ga_ref[...] + bg_ref[:, D_MODEL:])
        merged = (sr * y_rnn + sa * y_attn).astype(BF16)
        mo = _dot(merged, wo[...])
        x2 = x_ref[...] + gate * mo
        rstd = lax.rsqrt(jnp.mean(x2 * x2, axis=-1, keepdims=True) + NORM_EPS)
        xn = x2 * rstd
        err = xn * gfin - t_ref[...]
        loss_ref[...] += 0.5 * jnp.sum(jnp.sum(err * err, axis=-1, keepdims=True) * (1.0 / D_MODEL),
                                       axis=0, keepdims=True)

        dy = err * (1.0 / D_MODEL)
        ggf_ref[...] += jnp.sum(dy * xn, axis=0, keepdims=True)
        dxn = dy * gfin
        dx2 = rstd * (dxn - xn * jnp.mean(dxn * xn, axis=-1, keepdims=True))
        dx2_ref[...] = dx2
        dgate_ref[...] += jnp.sum(dx2 * mo, axis=0, keepdims=True)
        dmo = (dx2 * gate).astype(BF16)
        dmerged = _dot_nt(dmo, wo[...])
        mg_ref[...] = merged
        dmo_ref[...] = dmo
        dy_rnn = (dmerged * sr).astype(BF16)
        dy_attn = (dmerged * sa).astype(BF16)
        dg_r = dmerged * y_rnn * sr * (1.0 - sr)
        dg_a = dmerged * y_attn * sa * (1.0 - sa)
        dgr_ref[...] = dg_r.astype(BF16)
        dga_ref[...] = dg_a.astype(BF16)
        gbg_ref[:, :D_MODEL] += jnp.sum(dg_r, axis=0, keepdims=True)
        gbg_ref[:, D_MODEL:] += jnp.sum(dg_a, axis=0, keepdims=True)
        du_rnn = _dot_nt(dy_rnn, wr[...])
        du_attn = _dot_nt(dy_attn, wa[...])
        ur_ref[...] = u_rnn
        dyr_ref[...] = dy_rnn
        ua_ref[...] = u_attn
        dya_ref[...] = dy_attn
        dhr_ref[...] = du_rnn * silu_zr
        dzr_ref[...] = (du_rnn * hr_t * (sig_zr * (1.0 + zr * (1.0 - sig_zr)))).astype(BF16)
        dza_ref[...] = (du_attn * o * (sig_za * (1.0 + za * (1.0 - sig_za)))).astype(BF16)
        d_o = du_attn * silu_za
        for hh in range(N_HEADS):
            do_ref[hh] = d_o[:, hh * HEAD_DIM:(hh + 1) * HEAD_DIM]

    row = pl.BlockSpec((tm, D_MODEL), lambda i: (i, 0))
    piece = lambda slot: pl.BlockSpec((tm, D_MODEL), lambda i: (i, slot))
    hm = pl.BlockSpec((N_HEADS, tm, HEAD_DIM), lambda i: (0, i, 0))
    const = lambda cols: pl.BlockSpec((1, cols), lambda i: (0, 0))
    any_spec = pl.BlockSpec(memory_space=pl.ANY)
    act_f32 = jax.ShapeDtypeStruct((seq, D_MODEL), F32)
    act_bf16 = jax.ShapeDtypeStruct((seq, D_MODEL), BF16)
    return pl.pallas_call(
        body, name="hub",
        out_shape=[act_f32, act_f32, act_bf16, jax.ShapeDtypeStruct((N_HEADS, seq, HEAD_DIM), F32),
                   act_bf16, act_bf16, act_bf16] + [act_bf16] * 6 + [
                   jax.ShapeDtypeStruct((1, D_MODEL), F32), jax.ShapeDtypeStruct((1, 2 * D_MODEL), F32),
                   jax.ShapeDtypeStruct((1, D_MODEL), F32), jax.ShapeDtypeStruct((1, 1), F32)],
        grid=(nsteps,),
        in_specs=[row, row, row, piece(1), piece(2), piece(3), piece(4), hm,
                  const(3 * D_MODEL), const(3 * D_MODEL), const(2 * D_MODEL), const(D_MODEL),
                  any_spec, any_spec, any_spec],
        out_specs=[row, row, row, hm, row, row, row] + [row] * 6 + [
                   const(D_MODEL), const(2 * D_MODEL), const(D_MODEL), const(1)],
        scratch_shapes=[pltpu.VMEM((D_MODEL, D_MODEL), BF16)] * 3 + [pltpu.SemaphoreType.DMA],
        compiler_params=_params(("arbitrary",), VMEM_LIMIT),
    )(x, tgt, hr, pf, pf, pf, pf, o_hm, mod, b_mod, b_gate, g_final, w_out_rnn, w_out_attn, w_o)


def _pair_grads(name, lefts, rights):
    n = len(rights)
    shared = len(lefts) == 1
    seq = rights[0].shape[0]
    tk = WGRAD_ROWS
    nk = seq // tk

    def body(*refs):
        l_refs, r_refs, out_ref = refs[:len(lefts)], refs[len(lefts):len(lefts) + n], refs[len(lefts) + n]
        j, kk = pl.program_id(0), pl.program_id(1)

        @pl.when(kk == 0)
        def _():
            out_ref[...] = jnp.zeros_like(out_ref)

        for m in range(n):
            @pl.when(j == m)
            def _(m=m):
                out_ref[...] += _dot_tn(l_refs[0 if shared else m][...], r_refs[m][...])

    def spec(m):
        return pl.BlockSpec((tk, D_MODEL), lambda j, kk: (jnp.where(j == m, kk, jnp.where(j < m, 0, nk - 1)), 0))

    left_specs = [pl.BlockSpec((tk, D_MODEL), lambda j, kk: (kk, 0))] if shared else [spec(m) for m in range(n)]
    return pl.pallas_call(
        body, name=name,
        out_shape=jax.ShapeDtypeStruct((n, D_MODEL, D_MODEL), F32),
        grid=(n, nk),
        in_specs=left_specs + [spec(m) for m in range(n)],
        out_specs=pl.BlockSpec((None, D_MODEL, D_MODEL), lambda j, kk: (j, 0, 0)),
        compiler_params=_params(("arbitrary", "arbitrary"), VMEM_LIMIT),
    )(*lefts, *rights)


def _dh_dx(pieces, w_in_all, x, dx2, mod, b_mod, g_norm):
    seq = x.shape[0]
    tm = DX_ROWS

    def body(*refs):
        p_refs = refs[:8]
        w_hbm, x_ref, dx2_ref, mod_ref, bmod_ref, g_ref = refs[8:14]
        gx_ref, dshift_ref, dscale_ref, ggn_ref, w_scr, sem = refs[14:]
        step = pl.program_id(0)

        @pl.when(step == 0)
        def _():
            cp = pltpu.make_async_copy(w_hbm, w_scr, sem)
            cp.start()
            cp.wait()
            for ref in (dshift_ref, dscale_ref, ggn_ref):
                ref[...] = jnp.zeros_like(ref)

        dh = _dot_nt(p_refs[0][...], w_scr[0])
        for j in range(1, 8):
            dh = dh + _dot_nt(p_refs[j][...], w_scr[j])
        scale1 = 1.0 + mod_ref[:, D_MODEL:2 * D_MODEL] + bmod_ref[:, D_MODEL:2 * D_MODEL]
        g = g_ref[...]
        xf = x_ref[...]
        rstd_t = lax.rsqrt(jnp.mean(xf * xf, axis=-1, keepdims=True) + NORM_EPS)
        xn = xf * rstd_t
        dshift_ref[...] += jnp.sum(dh, axis=0, keepdims=True)
        dscale_ref[...] += jnp.sum(dh * (xn * g), axis=0, keepdims=True)
        ggn_ref[...] += jnp.sum(dh * scale1 * xn, axis=0, keepdims=True)
        dxn = dh * (g * scale1)
        gx_ref[...] = rstd_t * (dxn - xn * jnp.mean(dxn * xn, axis=-1, keepdims=True)) + dx2_ref[...]

    row = pl.BlockSpec((tm, D_MODEL), lambda i: (i, 0))
    const = lambda cols: pl.BlockSpec((1, cols), lambda i: (0, 0))
    vec = jax.ShapeDtypeStruct((1, D_MODEL), F32)
    return pl.pallas_call(
        body, name="dh_dx",
        out_shape=[jax.ShapeDtypeStruct((seq, D_MODEL), F32), vec, vec, vec],
        grid=(seq // tm,),
        in_specs=[row] * 8 + [pl.BlockSpec(memory_space=pl.ANY), row, row,
                              const(3 * D_MODEL), const(3 * D_MODEL), const(D_MODEL)],
        out_specs=[row, const(D_MODEL), const(D_MODEL), const(D_MODEL)],
        scratch_shapes=[pltpu.VMEM((8, D_MODEL, D_MODEL), BF16), pltpu.SemaphoreType.DMA],
        compiler_params=_params(("arbitrary",), VMEM_LIMIT),
    )(*pieces, w_in_all, x, dx2, mod, b_mod, g_norm)


def _adamw(name, w, g, m, v):
    rows, cols = w.shape
    tr = rows if rows <= 256 else 256

    def body(w_ref, g_ref, m_ref, v_ref, d_ref, nm_ref, nv_ref):
        gv = g_ref[...]
        nm = ADAM_B1 * m_ref[...] + (1.0 - ADAM_B1) * gv
        nv = ADAM_B2 * v_ref[...] + (1.0 - ADAM_B2) * (gv * gv)
        m_hat = nm / (1.0 - ADAM_B1 ** ADAM_STEP)
        v_hat = nv / (1.0 - ADAM_B2 ** ADAM_STEP)
        d_ref[...] = -ADAM_LR * (m_hat / (jnp.sqrt(v_hat) + ADAM_EPS) + ADAM_WD * w_ref[...])
        nm_ref[...] = nm
        nv_ref[...] = nv

    spec = pl.BlockSpec((tr, cols), lambda i: (i, 0))
    shape = jax.ShapeDtypeStruct((rows, cols), F32)
    return pl.pallas_call(
        body, name=name, out_shape=[shape, shape, shape], grid=(rows // tr,),
        in_specs=[spec] * 4, out_specs=[spec] * 3,
        compiler_params=_params(("arbitrary",)),
    )(w, g, m, v)


def kernel(x, c, positions, g_norm, w_mod, b_mod, w_in, b_gate, conv_w, conv_b, w_a, b_a, w_x, b_x, lam, w_out_rnn, w_out_attn, w_o, g_final, loss_target, m_g_norm, m_w_mod, m_b_mod, m_w_in, m_b_gate, m_conv_w, m_conv_b, m_w_a, m_b_a, m_w_x, m_b_x, m_lam, m_w_out_rnn, m_w_out_attn, m_w_o, m_g_final, v_g_norm, v_w_mod, v_b_mod, v_w_in, v_b_gate, v_conv_w, v_conv_b, v_w_a, v_b_a, v_w_x, v_b_x, v_lam, v_w_out_rnn, v_w_out_attn, v_w_o, v_g_final):
    seq = x.shape[1]
    me = _index(_my_pos())
    xs, tgt = x[0], loss_target[0]

    pos = positions[0].astype(F32)[:, None]
    inv_freq = ROPE_THETA ** (-jnp.arange(0, 2 * ROT_HALF, 2, dtype=F32) / (2 * ROT_HALF))
    ang = pos * inv_freq
    rest = HEAD_DIM - 2 * ROT_HALF
    cosf = jnp.concatenate([jnp.cos(ang), jnp.cos(ang), jnp.ones((seq, rest), F32)], axis=1)
    sinf = jnp.concatenate([-jnp.sin(ang), jnp.sin(ang), jnp.zeros((seq, rest), F32)], axis=1)
    keep = (positions[0] != 0).astype(F32)[:, None]

    w_in_all, w_or_all, w_oa_all, w_o_all = _ag_big(
        "gather_weights", [w_in[0].astype(BF16), w_out_rnn[0].astype(BF16),
                           w_out_attn[0].astype(BF16), w_o[0].astype(BF16)])
    w_or_all, w_oa_all, w_o_all = (t.reshape(D_MODEL, D_MODEL) for t in (w_or_all, w_oa_all, w_o_all))
    conv_w8 = _ag_small("gather_conv_w", jnp.pad(conv_w[0], ((0, SUBLANES - 4), (0, 0))))
    c_all = _ag_small("gather_c", jnp.broadcast_to(c, (SUBLANES, D_MODEL)))[:, 0, :]
    mod_cols = w_mod.shape[2]
    mod_part = _ag_small("gather_mod", _mod_fwd(c_all, w_mod[0]))
    mod = lax.dynamic_index_in_dim(mod_part, me, axis=1, keepdims=False).reshape(1, N_DEV * mod_cols)

    blocks = lambda t: t.reshape(RNN_BLOCKS, 1, 128)
    rnn_params = (conv_w8, blocks(conv_b), w_a[0], blocks(b_a), w_x[0], blocks(b_x), blocks(lam))

    h = _norm(xs, mod, b_mod, g_norm)
    pf, q, k, v = _proj(h, w_in_all, cosf, sinf)
    hr = _rnn_fwd(pf, keep, *rnn_params)
    o, lses = _attn_fwd(q, k, v)

    (dx2, dhr, dz_rnn, d_o, dz_attn, dg_r, dg_a, u_rnn, dy_rnn, u_attn, dy_attn, merged, dmo,
     gp_g_final, gp_b_gate, dgate, loss_part) = _hub(
        xs, tgt, hr, pf, o, mod, b_mod, b_gate, g_final.reshape(1, D_MODEL), w_or_all, w_oa_all, w_o_all)
    gp_w_or, gp_w_oa, gp_w_o = _pair_grads("out_grads", [u_rnn, u_attn, merged], [dy_rnn, dy_attn, dmo])
    dq, dk, dv = _attn_bwd(q, k, v, d_o, o, lses, cosf, sinf)
    dx_rnn, gp_conv_w, gp_conv_b, gp_w_a, gp_b_a, gp_w_x, gp_b_x, gp_lam = _rnn_bwd(pf, hr, dhr, keep, *rnn_params)
    pieces = [dx_rnn, dz_rnn, dq, dk, dv, dz_attn, dg_r, dg_a]
    grad_x, dshift, dscale, gp_g_norm = _dh_dx(pieces, w_in_all, xs, dx2, mod, b_mod, g_norm)
    gp_w_in = _pair_grads("w_in_grad", [h], pieces)

    dmod = jnp.concatenate([dshift, dscale, dgate], axis=1)
    dmod_all = _ag_small("gather_dmod", jnp.broadcast_to(dmod, (SUBLANES, 3 * D_MODEL)))[:, 0, :]
    dmod_cols = lax.dynamic_slice_in_dim(dmod_all, me * mod_cols, mod_cols, axis=1)
    g_b_mod, g_w_mod = _mod_bwd(c_all, dmod_all, dmod_cols)

    flat = lambda t: t.reshape(-1, 128)
    small = [flat(gp_g_norm), flat(gp_b_gate), flat(gp_conv_b), flat(gp_b_a), flat(gp_b_x), flat(gp_lam),
             flat(gp_g_final), flat(gp_conv_w), jnp.broadcast_to(loss_part, (SUBLANES, 128)),
             flat(gp_w_a), flat(gp_w_x)]
    sizes = [t.shape[0] for t in small]
    small.append(jnp.zeros((-sum(sizes) % (2 * SUBLANES), 128), F32))
    total = _allreduce_small("allreduce_small_grads", jnp.concatenate(small, axis=0))
    offs = [sum(sizes[:i]) for i in range(len(sizes))]
    (g_g_norm, g_b_gate, g_conv_b, g_b_a, g_b_x, g_lam, g_g_final, g_conv_w_all, loss_rows, g_w_a, g_w_x) = (
        total[o_:o_ + s_] for o_, s_ in zip(offs, sizes))
    loss = loss_rows[0, 0]
    g_conv_w = lax.dynamic_index_in_dim(g_conv_w_all.reshape(RNN_BLOCKS, SUBLANES, 128), me, axis=0,
                                        keepdims=False)[:4]

    stacks = [gp_w_in, gp_w_or.reshape(N_DEV, 128, D_MODEL), gp_w_oa.reshape(N_DEV, 128, D_MODEL),
              gp_w_o.reshape(N_DEV, 128, D_MODEL)]
    from_sib = _rs_to_sibling("rs_sibling", stacks)
    targets = jnp.bitwise_xor(me, 2 * jnp.arange(4, dtype=jnp.int32)).astype(jnp.int32)
    sums = [_add_sibling("rs_add_sibling_%d" % a, s_, r_, targets) for a, (s_, r_) in enumerate(zip(stacks, from_sib))]
    from_chips = _rs_to_chips("rs_chips", [send for _, send in sums])
    g_w_in, g_w_or, g_w_oa, g_w_o = (
        _add_chips("rs_add_chips_%d" % a, own, r_) for a, ((own, _), r_) in enumerate(zip(sums, from_chips)))

    weights = [
        ("g_norm", g_norm, g_g_norm, m_g_norm, v_g_norm, (SUBLANES, 128)),
        ("w_mod", w_mod, g_w_mod, m_w_mod, v_w_mod, (D_MODEL, mod_cols)),
        ("b_mod", b_mod, g_b_mod, m_b_mod, v_b_mod, (3 * SUBLANES, 128)),
        ("w_in", w_in, g_w_in, m_w_in, v_w_in, (D_MODEL, D_MODEL)),
        ("b_gate", b_gate, g_b_gate, m_b_gate, v_b_gate, (2 * SUBLANES, 128)),
        ("conv_w", conv_w, g_conv_w, m_conv_w, v_conv_w, (4, 128)),
        ("conv_b", conv_b, g_conv_b, m_conv_b, v_conv_b, (SUBLANES, 128)),
        ("w_a", w_a, g_w_a, m_w_a, v_w_a, (RNN_BLOCKS * 128, 128)),
        ("b_a", b_a, g_b_a, m_b_a, v_b_a, (SUBLANES, 128)),
        ("w_x", w_x, g_w_x, m_w_x, v_w_x, (RNN_BLOCKS * 128, 128)),
        ("b_x", b_x, g_b_x, m_b_x, v_b_x, (SUBLANES, 128)),
        ("lam", lam, g_lam, m_lam, v_lam, (SUBLANES, 128)),
        ("w_out_rnn", w_out_rnn, g_w_or, m_w_out_rnn, v_w_out_rnn, (128, D_MODEL)),
        ("w_out_attn", w_out_attn, g_w_oa, m_w_out_attn, v_w_out_attn, (128, D_MODEL)),
        ("w_o", w_o, g_w_o, m_w_o, v_w_o, (128, D_MODEL)),
        ("g_final", g_final, g_g_final, m_g_final, v_g_final, (SUBLANES, 128)),
    ]
    out_g, out_d, out_m, out_v = [], [], [], []
    for name, w_, g_, m_, v_, shape2 in weights:
        d_, nm_, nv_ = _adamw("adamw_" + name, w_.reshape(shape2), g_.reshape(shape2), m_.reshape(shape2),
                              v_.reshape(shape2))
        out_g.append(g_.reshape(w_.shape))
        out_d.append(d_.reshape(w_.shape))
        out_m.append(nm_.reshape(w_.shape))
        out_v.append(nv_.reshape(w_.shape))
    return (loss, grad_x[None], *out_g, *out_d, *out_m, *out_v)
```

```python
import jax
import jax.numpy as jnp
from jax import lax
from jax.experimental import pallas as pl
from jax.experimental.pallas import tpu as pltpu

F32 = jnp.float32
BF16 = jnp.bfloat16
MESH = pl.DeviceIdType.MESH

D_MODEL = 1024
N_HEADS = 8
HEAD_DIM = 128
RNN_BLOCKS = 8
N_DEV = 8
ROT_HALF = 16
ROPE_THETA = 500000.0
DILATIONS = (1, 4, 16)
KEY_BLOCK = 128
SPAN = KEY_BLOCK * DILATIONS[-1]
ATTN_SCALE = HEAD_DIM ** -0.5
NORM_EPS = 1e-6
LRU_C = 8.0
NEG_INF = -1e30
ADAM_LR, ADAM_B1, ADAM_B2, ADAM_EPS, ADAM_WD, ADAM_STEP = 0.001, 0.9, 0.999, 1e-08, 0.01, 10

SUBLANES = 8
VMEM_LIMIT = 56 * 1024 * 1024
PROJ_ROWS = 1024
RNN_ROWS = 512
HUB_ROWS = 256
DX_ROWS = 256
WGRAD_ROWS = 1024
ADD_ROWS = 256


def _params(sem=None, vmem=None):
    return pltpu.CompilerParams(dimension_semantics=sem, vmem_limit_bytes=vmem)


def _dot(a, b):
    return jnp.dot(a, b, preferred_element_type=F32)


def _dot_nt(a, b):
    return lax.dot_general(a, b, (((1,), (1,)), ((), ())), preferred_element_type=F32)


def _dot_tn(a, b):
    return lax.dot_general(a, b, (((0,), (0,)), ((), ())), preferred_element_type=F32)


def _sigmoid(z):
    return 1.0 / (1.0 + jnp.exp(-z))


def _expm1_nonpos(z, exp_z):
    return jnp.where(z > -0.01, z * (1.0 + 0.5 * z), exp_z - 1.0)


def _my_pos():
    return lax.axis_index("x"), lax.axis_index("y"), lax.axis_index("c")


def _flip(pos, k):
    x, y, c = pos
    return ((1 - x) if k & 4 else x, (1 - y) if k & 2 else y, (1 - c) if k & 1 else c)


def _index(pos):
    return 4 * pos[0] + 2 * pos[1] + pos[2]


def _ag_small(name, v):
    rows, cols = v.shape

    def body(v_ref, out_ref, send_sems, recv_sems):
        me = _my_pos()
        out_ref[_index(me)] = v_ref[...]
        sends = []
        for k in range(1, N_DEV):
            cp = pltpu.make_async_remote_copy(
                src_ref=v_ref, dst_ref=out_ref.at[_index(me)], send_sem=send_sems.at[k - 1],
                recv_sem=recv_sems.at[k - 1], device_id=_flip(me, k), device_id_type=MESH)
            cp.start()
            sends.append(cp)
        for k in range(1, N_DEV):
            peer = _flip(me, k)
            pltpu.make_async_remote_copy(
                src_ref=v_ref, dst_ref=out_ref.at[_index(peer)], send_sem=send_sems.at[k - 1],
                recv_sem=recv_sems.at[k - 1], device_id=peer, device_id_type=MESH).wait_recv()
        for cp in sends:
            cp.wait_send()

    return pl.pallas_call(
        body, name=name,
        out_shape=jax.ShapeDtypeStruct((N_DEV, rows, cols), v.dtype),
        in_specs=[pl.BlockSpec(memory_space=pltpu.VMEM)],
        out_specs=pl.BlockSpec(memory_space=pltpu.VMEM),
        scratch_shapes=[pltpu.SemaphoreType.DMA((N_DEV - 1,)), pltpu.SemaphoreType.DMA((N_DEV - 1,))],
        compiler_params=_params(None, VMEM_LIMIT),
    )(v)


def _ag_big(name, shards):
    n = len(shards)

    def body(*refs):
        ins, outs = refs[:n], refs[n:2 * n]
        send_sems, recv_sems, local_sems = refs[2 * n:]
        me = _my_pos()
        sib = _flip(me, 1)
        chips = [2, 4, 6]

        def copy(a, k, block, to, src=None):
            rows = outs[a].at[_index(block)]
            return pltpu.make_async_remote_copy(
                src_ref=rows if src is None else src, dst_ref=rows,
                send_sem=send_sems.at[a * 7 + k], recv_sem=recv_sems.at[a * 7 + k],
                device_id=to, device_id_type=MESH)

        started = []
        for a in range(n):
            mine = pltpu.make_async_copy(ins[a], outs[a].at[_index(me)], local_sems.at[a])
            mine.start()
            started.append(mine)
        sends = []
        for a in range(n):
            first = [copy(a, 0, me, sib, src=ins[a])]
            first += [copy(a, 1 + j, me, _flip(me, ch), src=ins[a]) for j, ch in enumerate(chips)]
            for cp in first:
                cp.start()
            sends += first
        for j, ch in enumerate(chips):
            for a in range(n):
                copy(a, 1 + j, _flip(me, ch), me).wait_recv()
                fwd = copy(a, 4 + j, _flip(me, ch), sib)
                fwd.start()
                sends.append(fwd)
        for a in range(n):
            copy(a, 0, sib, me).wait_recv()
            for j, ch in enumerate(chips):
                copy(a, 4 + j, _flip(sib, ch), me).wait_recv()
        for cp in sends:
            cp.wait_send()
        for mine in started:
            mine.wait()

    any_spec = pl.BlockSpec(memory_space=pl.ANY)
    return pl.pallas_call(
        body, name=name,
        out_shape=[jax.ShapeDtypeStruct((N_DEV,) + s.shape, s.dtype) for s in shards],
        in_specs=[any_spec] * n, out_specs=[any_spec] * n,
        scratch_shapes=[pltpu.SemaphoreType.DMA((7 * n,)), pltpu.SemaphoreType.DMA((7 * n,)),
                        pltpu.SemaphoreType.DMA((n,))],
    )(*shards)


def _rs_to_sibling(name, stacks):
    n = len(stacks)

    def body(*refs):
        ins, outs = refs[:n], refs[n:2 * n]
        send_sems, recv_sems = refs[2 * n:]
        me = _my_pos()
        sib = _flip(me, 1)
        sends = []
        for a in range(n):
            for m in range(4):
                target = _flip(sib, 2 * m)
                cp = pltpu.make_async_remote_copy(
                    src_ref=ins[a].at[_index(target)], dst_ref=outs[a].at[m],
                    send_sem=send_sems.at[a * 4 + m], recv_sem=recv_sems.at[a * 4 + m],
                    device_id=sib, device_id_type=MESH)
                cp.start()
                sends.append(cp)
        for cp in sends:
            cp.wait_recv()
        for cp in sends:
            cp.wait_send()

    any_spec = pl.BlockSpec(memory_space=pl.ANY)
    return pl.pallas_call(
        body, name=name,
        out_shape=[jax.ShapeDtypeStruct((4,) + s.shape[1:], s.dtype) for s in stacks],
        in_specs=[any_spec] * n, out_specs=[any_spec] * n,
        scratch_shapes=[pltpu.SemaphoreType.DMA((4 * n,)), pltpu.SemaphoreType.DMA((4 * n,))],
    )(*stacks)


def _rs_to_chips(name, sums):
    n = len(sums)

    def body(*refs):
        ins, outs = refs[:n], refs[n:2 * n]
        send_sems, recv_sems = refs[2 * n:]
        me = _my_pos()
        sends = []
        for a in range(n):
            for m in range(1, 4):
                cp = pltpu.make_async_remote_copy(
                    src_ref=ins[a].at[m - 1], dst_ref=outs[a].at[m - 1],
                    send_sem=send_sems.at[a * 3 + m - 1], recv_sem=recv_sems.at[a * 3 + m - 1],
                    device_id=_flip(me, 2 * m), device_id_type=MESH)
                cp.start()
                sends.append(cp)
        for cp in sends:
            cp.wait_recv()
        for cp in sends:
            cp.wait_send()

    any_spec = pl.BlockSpec(memory_space=pl.ANY)
    return pl.pallas_call(
        body, name=name,
        out_shape=[jax.ShapeDtypeStruct((3,) + s.shape[1:], s.dtype) for s in sums],
        in_specs=[any_spec] * n, out_specs=[any_spec] * n,
        scratch_shapes=[pltpu.SemaphoreType.DMA((3 * n,)), pltpu.SemaphoreType.DMA((3 * n,))],
    )(*sums)


def _add_sibling(name, stack, recv, targets):
    _, rows, cols = stack.shape
    tr = min(rows, ADD_ROWS)

    def own_body(t_ref, a_ref, b_ref, o_ref):
        o_ref[...] = a_ref[...] + b_ref[...]

    own = pl.pallas_call(
        own_body, name=name + "_own",
        out_shape=jax.ShapeDtypeStruct((rows, cols), F32),
        grid_spec=pltpu.PrefetchScalarGridSpec(
            num_scalar_prefetch=1, grid=(rows // tr,),
            in_specs=[pl.BlockSpec((None, tr, cols), lambda i, t: (t[0], i, 0)),
                      pl.BlockSpec((None, tr, cols), lambda i, t: (0, i, 0))],
            out_specs=pl.BlockSpec((tr, cols), lambda i, t: (i, 0))),
        compiler_params=_params(("arbitrary",)),
    )(targets, stack, recv)

    def send_body(t_ref, a_ref, b_ref, o_ref):
        o_ref[...] = (a_ref[...] + b_ref[...]).astype(BF16)

    send = pl.pallas_call(
        send_body, name=name + "_send",
        out_shape=jax.ShapeDtypeStruct((3, rows, cols), BF16),
        grid_spec=pltpu.PrefetchScalarGridSpec(
            num_scalar_prefetch=1, grid=(3, rows // tr),
            in_specs=[pl.BlockSpec((None, tr, cols), lambda m, i, t: (t[m + 1], i, 0)),
                      pl.BlockSpec((None, tr, cols), lambda m, i, t: (m + 1, i, 0))],
            out_specs=pl.BlockSpec((None, tr, cols), lambda m, i, t: (m, i, 0))),
        compiler_params=_params(("arbitrary", "arbitrary")),
    )(targets, stack, recv)
    return own, send


def _add_chips(name, own, recv):
    rows, cols = own.shape
    tr = min(rows, ADD_ROWS)

    def body(a_ref, b_ref, o_ref):
        o_ref[...] = ((a_ref[...] + b_ref[0].astype(F32)) + b_ref[1].astype(F32)) + b_ref[2].astype(F32)

    return pl.pallas_call(
        body, name=name,
        out_shape=jax.ShapeDtypeStruct((rows, cols), F32),
        grid=(rows // tr,),
        in_specs=[pl.BlockSpec((tr, cols), lambda i: (i, 0)),
                  pl.BlockSpec((3, tr, cols), lambda i: (0, i, 0))],
        out_specs=pl.BlockSpec((tr, cols), lambda i: (i, 0)),
        compiler_params=_params(("arbitrary",)),
    )(own, recv)


def _allreduce_small(name, v):
    rows, cols = v.shape
    half = rows // 2
    assert rows % (2 * SUBLANES) == 0

    def body(v_ref, out_ref, from_sib, chip_half, from_chips, send_sems, recv_sems):
        me = _my_pos()
        sib = _flip(me, 1)
        mine = pl.ds(pl.multiple_of(me[2] * half, SUBLANES), half)
        theirs = pl.ds(pl.multiple_of((1 - me[2]) * half, SUBLANES), half)

        def copy(k, src, dst, to):
            return pltpu.make_async_remote_copy(src_ref=src, dst_ref=dst, send_sem=send_sems.at[k],
                                                recv_sem=recv_sems.at[k], device_id=to, device_id_type=MESH)

        to_sib = copy(0, v_ref.at[theirs], from_sib, sib)
        to_sib.start()
        to_sib.wait_recv()
        chip_half[...] = v_ref[mine, :] + from_sib[...]
        to_chips = [copy(m, chip_half, from_chips.at[m - 1], _flip(me, 2 * m)) for m in range(1, 4)]
        for cp in to_chips:
            cp.start()
        for cp in to_chips:
            cp.wait_recv()
        my_chip = 2 * me[0] + me[1]
        total = None
        for chip in range(4):
            slot = jnp.maximum(jnp.bitwise_xor(chip, my_chip) - 1, 0)
            part = jnp.where(chip == my_chip, chip_half[...], from_chips[slot])
            total = part if total is None else total + part
        out_ref[mine, :] = total
        swap = copy(4, out_ref.at[mine], out_ref.at[mine], sib)
        swap.start()
        copy(4, out_ref.at[theirs], out_ref.at[theirs], sib).wait_recv()
        for cp in [to_sib, swap] + to_chips:
            cp.wait_send()

    return pl.pallas_call(
        body, name=name, out_shape=jax.ShapeDtypeStruct((rows, cols), F32),
        in_specs=[pl.BlockSpec(memory_space=pltpu.VMEM)],
        out_specs=pl.BlockSpec(memory_space=pltpu.VMEM),
        scratch_shapes=[pltpu.VMEM((half, cols), F32), pltpu.VMEM((half, cols), F32),
                        pltpu.VMEM((3, half, cols), F32),
                        pltpu.SemaphoreType.DMA((5,)), pltpu.SemaphoreType.DMA((5,))],
        compiler_params=_params(None, VMEM_LIMIT),
    )(v)


def _mod_fwd(c_all, w_mod):
    def body(c_ref, w_ref, o_ref):
        c = c_ref[...]
        o_ref[...] = jnp.dot(c * _sigmoid(c), w_ref[...], preferred_element_type=F32,
                             precision=lax.Precision.HIGHEST)

    return pl.pallas_call(
        body, name="mod_fwd", out_shape=jax.ShapeDtypeStruct((N_DEV, w_mod.shape[1]), F32),
    )(c_all, w_mod)


def _mod_bwd(c_all, dmod_all, dmod_cols):
    def body(c_ref, da_ref, dc_ref, gb_ref, gw_ref):
        c = c_ref[...]
        acc = da_ref[0:1, :]
        for b in range(1, N_DEV):
            acc = acc + da_ref[b:b + 1, :]
        gb_ref[...] = acc
        gw_ref[...] = lax.dot_general(c * _sigmoid(c), dc_ref[...], (((0,), (0,)), ((), ())),
                                      preferred_element_type=F32, precision=lax.Precision.HIGHEST)

    return pl.pallas_call(
        body, name="mod_bwd",
        out_shape=[jax.ShapeDtypeStruct((1, dmod_all.shape[1]), F32),
                   jax.ShapeDtypeStruct((c_all.shape[1], dmod_cols.shape[1]), F32)],
    )(c_all, dmod_all, dmod_cols)


def _rope_partner(t):
    lane = lax.broadcasted_iota(jnp.int32, t.shape, 1)
    return jnp.where(lane < ROT_HALF, pltpu.roll(t, HEAD_DIM - ROT_HALF, 1), pltpu.roll(t, ROT_HALF, 1))


def _norm(x, mod, b_mod, g_norm):
    seq = x.shape[0]
    tm = PROJ_ROWS

    def body(x_ref, mod_ref, bmod_ref, g_ref, h_ref):
        xf = x_ref[...]
        rstd = lax.rsqrt(jnp.mean(xf * xf, axis=-1, keepdims=True) + NORM_EPS)
        shift = mod_ref[:, 0:D_MODEL] + bmod_ref[:, 0:D_MODEL]
        scale = mod_ref[:, D_MODEL:2 * D_MODEL] + bmod_ref[:, D_MODEL:2 * D_MODEL]
        h_ref[...] = (((xf * rstd) * g_ref[...]) * (1.0 + scale) + shift).astype(BF16)

    row = pl.BlockSpec((tm, D_MODEL), lambda i: (i, 0))
    const = lambda cols: pl.BlockSpec((1, cols), lambda i: (0, 0))
    return pl.pallas_call(
        body, name="norm", out_shape=jax.ShapeDtypeStruct((seq, D_MODEL), BF16), grid=(seq // tm,),
        in_specs=[row, const(3 * D_MODEL), const(3 * D_MODEL), const(D_MODEL)], out_specs=row,
        compiler_params=_params(("arbitrary",), VMEM_LIMIT),
    )(x, mod, b_mod, g_norm)


def _proj(h, w_in_all, cosf, sinf):
    seq = h.shape[0]
    tm = PROJ_ROWS
    last = seq // tm - 1

    def body(h_ref, w_ref, cos_ref, sin_ref, pf_ref, q_ref, k_ref, v_ref):
        j = pl.program_id(0)

        @pl.when((j < 2) | (j > 4))
        def _():
            pf_ref[...] = _dot(h_ref[...], w_ref[...])

        def heads(dst_ref, rotate, gain):
            for pair in range(N_HEADS // 2):
                both = _dot(h_ref[...], w_ref[:, 2 * pair * HEAD_DIM:2 * (pair + 1) * HEAD_DIM])
                for hh in (2 * pair, 2 * pair + 1):
                    t = both[:, (hh % 2) * HEAD_DIM:(hh % 2 + 1) * HEAD_DIM]
                    if rotate:
                        t = t * cos_ref[...] + _rope_partner(t) * sin_ref[...]
                    dst_ref[hh] = t if gain is None else t * gain

        @pl.when(j == 2)
        def _():
            heads(q_ref, True, ATTN_SCALE)

        @pl.when(j == 3)
        def _():
            heads(k_ref, True, None)

        @pl.when(j == 4)
        def _():
            heads(v_ref, False, None)

    def pf_block(j, i):
        f32_piece = (j < 2) | (j > 4)
        return (jnp.where(f32_piece, i, last), jnp.where(j < 2, j, jnp.where(j < 5, 1, j - 3)))

    def hm_block(piece):
        return lambda j, i: (0, jnp.where(j == piece, i, jnp.where(j < piece, 0, last)), 0)

    hm = jax.ShapeDtypeStruct((N_HEADS, seq, HEAD_DIM), F32)
    hm_spec = lambda piece: pl.BlockSpec((N_HEADS, tm, HEAD_DIM), hm_block(piece))
    row = lambda j, i: (i, 0)
    return pl.pallas_call(
        body, name="proj",
        out_shape=[jax.ShapeDtypeStruct((seq, 5 * D_MODEL), F32), hm, hm, hm],
        grid=(8, seq // tm),
        in_specs=[pl.BlockSpec((tm, D_MODEL), row),
                  pl.BlockSpec((None, D_MODEL, D_MODEL), lambda j, i: (j, 0, 0)),
                  pl.BlockSpec((tm, HEAD_DIM), row), pl.BlockSpec((tm, HEAD_DIM), row)],
        out_specs=[pl.BlockSpec((tm, D_MODEL), pf_block), hm_spec(2), hm_spec(3), hm_spec(4)],
        compiler_params=_params(("arbitrary", "arbitrary"), VMEM_LIMIT),
    )(h, w_in_all, cosf, sinf)


def _shift_down(v, s, head):
    rows = v.shape[0]
    row = lax.broadcasted_iota(jnp.int32, v.shape, 0)
    fill = jnp.concatenate([pltpu.roll(head, s, 0), jnp.zeros((rows - SUBLANES, v.shape[1]), v.dtype)], axis=0)
    return jnp.where(row < s, fill, pltpu.roll(v, s, 0))


def _shift_up(v, s, tail):
    rows = v.shape[0]
    row = lax.broadcasted_iota(jnp.int32, v.shape, 0)
    fill = jnp.concatenate([jnp.zeros((rows - SUBLANES, v.shape[1]), v.dtype),
                            pltpu.roll(tail, SUBLANES - s, 0)], axis=0)
    return jnp.where(row >= rows - s, fill, pltpu.roll(v, rows - s, 0))


def _doubling(a, b, period, reverse):
    rows = a.shape[0]
    pos = lax.broadcasted_iota(jnp.int32, a.shape, 0) & (period - 1)
    k = 1
    while k < period:
        inside = (pos < period - k) if reverse else (pos >= k)
        shift = rows - k if reverse else k
        a_s = jnp.where(inside, pltpu.roll(a, shift, 0), 1.0)
        b_s = jnp.where(inside, pltpu.roll(b, shift, 0), 0.0)
        b = a * b_s + b
        a = a * a_s
        k *= 2
    return a, b


def _scan(a, b, boundary, reverse, a_scr, b_scr, spread):
    rows = a.shape[0]
    ntile = rows // SUBLANES
    a_scr[...], b_scr[...] = _doubling(a, b, SUBLANES, reverse)
    ends = pl.ds(0 if reverse else SUBLANES - 1, ntile, stride=SUBLANES)
    a_end, x_end = _doubling(a_scr[ends, :], b_scr[ends, :], ntile, reverse)
    x_end = x_end + a_end * boundary
    tile = lax.broadcasted_iota(jnp.int32, x_end.shape, 0)
    if reverse:
        incoming = jnp.where(tile == ntile - 1, boundary, pltpu.roll(x_end, ntile - 1, 0))
        last = x_end[0:1, :]
    else:
        incoming = jnp.where(tile == 0, boundary, pltpu.roll(x_end, 1, 0))
        last = x_end[ntile - 1:ntile, :]
    for s in range(SUBLANES):
        spread[pl.ds(s, ntile, stride=SUBLANES), :] = incoming
    return b_scr[...] + a_scr[...] * spread[...], last


def _conv_taps(xr, head):
    return [_shift_down(xr, 3, head), _shift_down(xr, 2, head), _shift_down(xr, 1, head), xr]


def _rnn_gates(xc, wa, ba, wx, bx, lam, keep):
    xcb = xc.astype(BF16)
    r = _sigmoid(_dot(xcb, wa.astype(BF16)) + ba)
    i = _sigmoid(_dot(xcb, wx.astype(BF16)) + bx)
    softplus = jnp.maximum(-lam, 0.0) + jnp.log(1.0 + jnp.exp(-jnp.abs(lam)))
    cl = -LRU_C * softplus
    log_a = cl * r
    a_raw = jnp.exp(log_a)
    mult_raw = jnp.sqrt(-_expm1_nonpos(2.0 * log_a, a_raw * a_raw))
    live = keep > 0.0
    return r, i, cl, a_raw, mult_raw, jnp.where(live, a_raw, 0.0), jnp.where(live, mult_raw, 1.0), live


def _rnn_specs(seq, rows, time_of):
    per = rows // SUBLANES
    vec = pl.BlockSpec((None, 1, 128), lambda hb, n: (hb, 0, 0))
    mat = pl.BlockSpec((None, 128, 128), lambda hb, n: (hb, 0, 0))
    return [pl.BlockSpec((rows, 128), lambda hb, n: (time_of(n), hb)),
            pl.BlockSpec((SUBLANES, 128), lambda hb, n: (jnp.maximum(time_of(n) * per - 1, 0), hb)),
            pl.BlockSpec((rows, 1), lambda hb, n: (time_of(n), 0)),
            pl.BlockSpec((None, SUBLANES, 128), lambda hb, n: (hb, 0, 0)),
            vec, mat, vec, mat, vec, vec]


def _rnn_fwd(pf, keep, conv_w8, conv_b, w_a, b_a, w_x, b_x, lam):
    seq = pf.shape[0]
    rows = RNN_ROWS

    def body(x_ref, xh_ref, keep_ref, cw_ref, cb_ref, wa_ref, ba_ref, wx_ref, bx_ref, lam_ref, hr_ref,
             carry, a_scr, b_scr, spread):
        n = pl.program_id(1)

        @pl.when(n == 0)
        def _():
            carry[...] = jnp.zeros_like(carry)

        xr = x_ref[...]
        head = jnp.where(n > 0, xh_ref[...], 0.0)
        taps = _conv_taps(xr, head)
        xc = cb_ref[...] + sum(cw_ref[k:k + 1, :] * taps[k] for k in range(4))
        _, i, _, _, _, a, mult, _ = _rnn_gates(xc, wa_ref[...], ba_ref[...], wx_ref[...], bx_ref[...],
                                               lam_ref[...], keep_ref[...])
        h, last = _scan(a, mult * i * xc, carry[0:1, :], False, a_scr, b_scr, spread)
        hr_ref[...] = h
        carry[...] = jnp.broadcast_to(last, carry.shape)

    chunk_f32 = pltpu.VMEM((rows, 128), F32)
    return pl.pallas_call(
        body, name="rnn_fwd",
        out_shape=jax.ShapeDtypeStruct((seq, D_MODEL), F32),
        grid=(RNN_BLOCKS, seq // rows),
        in_specs=_rnn_specs(seq, rows, lambda n: n),
        out_specs=pl.BlockSpec((rows, 128), lambda hb, n: (n, hb)),
        scratch_shapes=[pltpu.VMEM((SUBLANES, 128), F32), chunk_f32, chunk_f32, chunk_f32],
        compiler_params=_params(("arbitrary", "arbitrary"), VMEM_LIMIT),
    )(pf, pf, keep, conv_w8, conv_b, w_a, b_a, w_x, b_x, lam)


def _rnn_bwd(pf, hr, dhr, keep, conv_w8, conv_b, w_a, b_a, w_x, b_x, lam):
    seq = pf.shape[0]
    rows = RNN_ROWS
    nchunk = seq // rows
    per = rows // SUBLANES
    time_of = lambda n: nchunk - 1 - n

    def body(x_ref, xh_ref, keep_ref, cw_ref, cb_ref, wa_ref, ba_ref, wx_ref, bx_ref, lam_ref,
             hr_ref, hrh_ref, dhr_ref,
             dx_ref, gcw_ref, gcb_ref, gwa_ref, gba_ref, gwx_ref, gbx_ref, glam_ref,
             g_carry, dxc_tail, a_scr, b_scr, spread):
        n = pl.program_id(1)
        first_in_time = n == nchunk - 1

        @pl.when(n == 0)
        def _():
            g_carry[...] = jnp.zeros_like(g_carry)
            dxc_tail[...] = jnp.zeros_like(dxc_tail)
            for ref in (gcw_ref, gcb_ref, gwa_ref, gba_ref, gwx_ref, gbx_ref, glam_ref):
                ref[...] = jnp.zeros_like(ref)

        xr = x_ref[...]
        head = jnp.where(first_in_time, 0.0, xh_ref[...])
        taps = _conv_taps(xr, head)
        cw = cw_ref[...]
        xc = cb_ref[...] + sum(cw[k:k + 1, :] * taps[k] for k in range(4))
        wa, wx, lam = wa_ref[...], wx_ref[...], lam_ref[...]
        r, i, cl, a_raw, mult_raw, a, mult, live = _rnn_gates(xc, wa, ba_ref[...], wx, bx_ref[...], lam,
                                                               keep_ref[...])
        h_prev = _shift_down(hr_ref[...], 1, jnp.where(first_in_time, 0.0, hrh_ref[...]))

        row = lax.broadcasted_iota(jnp.int32, xr.shape, 0)
        last = row == rows - 1
        a_next = jnp.where(last, 0.0, pltpu.roll(a, rows - 1, 0))
        g, g_first = _scan(a_next, dhr_ref[...] + jnp.where(last, g_carry[0:1, :], 0.0),
                           jnp.zeros((1, 128), F32), True, a_scr, b_scr, spread)
        g_carry[...] = jnp.broadcast_to(a[0:1, :] * g_first, g_carry.shape)

        da = g * h_prev
        dmult = g * i * xc
        di = g * mult * xc
        dxc = g * mult * i
        dlog_a = jnp.where(live, da * a_raw - dmult * a_raw * a_raw / mult_raw, 0.0)
        dpa = (dlog_a * cl) * r * (1.0 - r)
        dpx = di * i * (1.0 - i)
        glam_ref[...] += jnp.sum(dlog_a * r, axis=0, keepdims=True) * (LRU_C * _sigmoid(-lam))
        xcb, dpab, dpxb = xc.astype(BF16), dpa.astype(BF16), dpx.astype(BF16)
        gwa_ref[...] += _dot_tn(xcb, dpab)
        gwx_ref[...] += _dot_tn(xcb, dpxb)
        gba_ref[...] += jnp.sum(dpa, axis=0, keepdims=True)
        gbx_ref[...] += jnp.sum(dpx, axis=0, keepdims=True)
        dxc = dxc + _dot_nt(dpab, wa.astype(BF16)) + _dot_nt(dpxb, wx.astype(BF16))

        gcb_ref[...] += jnp.sum(dxc, axis=0, keepdims=True)
        for k in range(4):
            gcw_ref[k:k + 1, :] += jnp.sum(dxc * taps[k], axis=0, keepdims=True)
        tail = dxc_tail[...]
        dx = cw[3:4, :] * dxc
        for k in range(3):
            dx = dx + cw[k:k + 1, :] * _shift_up(dxc, 3 - k, tail)
        dx_ref[...] = dx.astype(BF16)
        dxc_tail[...] = dxc[0:SUBLANES, :]

    blk = lambda hb, n: (hb, 0, 0)
    chunk = pl.BlockSpec((rows, 128), lambda hb, n: (time_of(n), hb))
    vec_out = pl.BlockSpec((None, 1, 128), blk)
    mat_out = pl.BlockSpec((None, 128, 128), blk)
    vec_shape = jax.ShapeDtypeStruct((RNN_BLOCKS, 1, 128), F32)
    mat_shape = jax.ShapeDtypeStruct((RNN_BLOCKS, 128, 128), F32)
    return pl.pallas_call(
        body, name="rnn_bwd",
        out_shape=[jax.ShapeDtypeStruct((seq, D_MODEL), BF16),
                   jax.ShapeDtypeStruct((RNN_BLOCKS, SUBLANES, 128), F32), vec_shape,
                   mat_shape, vec_shape, mat_shape, vec_shape, vec_shape],
        grid=(RNN_BLOCKS, nchunk),
        in_specs=_rnn_specs(seq, rows, time_of) + [
            chunk, pl.BlockSpec((SUBLANES, 128), lambda hb, n: (jnp.maximum(time_of(n) * per - 1, 0), hb)), chunk],
        out_specs=[chunk, pl.BlockSpec((None, SUBLANES, 128), blk), vec_out,
                   mat_out, vec_out, mat_out, vec_out, vec_out],
        scratch_shapes=[pltpu.VMEM((SUBLANES, 128), F32), pltpu.VMEM((SUBLANES, 128), F32)]
                       + [pltpu.VMEM((rows, 128), F32)] * 3,
        compiler_params=_params(("arbitrary", "arbitrary"), VMEM_LIMIT),
    )(pf, pf, keep, conv_w8, conv_b, w_a, b_a, w_x, b_x, lam, hr, hr, dhr)


def _unit_rows(dil, r, j):
    start = j * KEY_BLOCK * dil + r
    return pl.ds(start, KEY_BLOCK) if dil == 1 else pl.ds(start, KEY_BLOCK, stride=dil)


def _attn_fwd(q, k, v):
    nh, seq, _ = q.shape
    nchunk = seq // SPAN
    nblk = SPAN // KEY_BLOCK

    def body(q_ref, k_ref, v_ref, kp_ref, vp_ref, o_ref, l1_ref, l4_ref, l16_ref, acc, m_s, l_s):
        n = pl.program_id(1)
        qi = lax.broadcasted_iota(jnp.int32, (KEY_BLOCK, KEY_BLOCK), 0)
        ki = lax.broadcasted_iota(jnp.int32, (KEY_BLOCK, KEY_BLOCK), 1)
        bias_own = jnp.where(ki <= qi, 0.0, NEG_INF)
        bias_before = jnp.where(ki >= qi, 0.0, NEG_INF)
        bias_mid = jnp.concatenate([bias_before, bias_own], axis=1)
        bias_first = jnp.concatenate([jnp.where(n > 0, bias_before, NEG_INF), bias_own], axis=1)
        ones = jnp.ones((2 * KEY_BLOCK, HEAD_DIM), BF16)
        for gi, dil in enumerate(DILATIONS):
            nb = nblk // dil
            for r in range(dil):
                for j in range(nb):
                    rows = _unit_rows(dil, r, j)
                    if j == 0:
                        prow = _unit_rows(dil, r, nb - 1)
                        kp, vp, bias = kp_ref[prow, :], vp_ref[prow, :], bias_first
                    else:
                        prow = _unit_rows(dil, r, j - 1)
                        kp, vp, bias = k_ref[prow, :], v_ref[prow, :], bias_mid
                    qb = q_ref[rows, :].astype(BF16)
                    kcat = jnp.concatenate([kp, k_ref[rows, :]], axis=0).astype(BF16)
                    vcat = jnp.concatenate([vp, v_ref[rows, :]], axis=0).astype(BF16)
                    vaug = jnp.concatenate([vcat, ones], axis=1)
                    s = _dot_nt(qb, kcat) + bias
                    mx = jnp.max(s, axis=-1, keepdims=True)
                    if gi == 0:
                        m_new = jnp.broadcast_to(mx, (KEY_BLOCK, HEAD_DIM))
                    else:
                        m_old = m_s[rows, :]
                        m_new = jnp.maximum(m_old, mx)
                    p = jnp.exp(s - jnp.concatenate([m_new, m_new], axis=1))
                    pv = _dot(p.astype(BF16), vaug)
                    if gi == 0:
                        acc[rows, :] = pv[:, :HEAD_DIM]
                        l_s[rows, :] = pv[:, HEAD_DIM:]
                    else:
                        alpha = jnp.exp(m_old - m_new)
                        acc[rows, :] = alpha * acc[rows, :] + pv[:, :HEAD_DIM]
                        l_s[rows, :] = alpha * l_s[rows, :] + pv[:, HEAD_DIM:]
                    m_s[rows, :] = m_new
        den = l_s[...]
        o_ref[...] = acc[...] * (1.0 / den)
        m_s[...] = m_s[...] + jnp.log(den)
        diag = qi == ki
        for dil, out in zip(DILATIONS, (l1_ref, l4_ref, l16_ref)):
            nb = nblk // dil
            for r in range(dil):
                for j in range(nb):
                    blk = m_s[_unit_rows(dil, r, j), :]
                    out[r * nb + j:r * nb + j + 1, :] = jnp.sum(jnp.where(diag, blk, 0.0), axis=0, keepdims=True)

    blk = pl.BlockSpec((None, SPAN, HEAD_DIM), lambda h, n: (h, n, 0))
    pblk = pl.BlockSpec((None, SPAN, HEAD_DIM), lambda h, n: (h, jnp.maximum(n - 1, 0), 0))
    lblk = pl.BlockSpec((None, nblk, KEY_BLOCK), lambda h, n: (h, n, 0))
    lshape = jax.ShapeDtypeStruct((nh, seq // KEY_BLOCK, KEY_BLOCK), F32)
    span_f32 = pltpu.VMEM((SPAN, HEAD_DIM), F32)
    o, l1, l4, l16 = pl.pallas_call(
        body, name="attn_fwd",
        out_shape=[jax.ShapeDtypeStruct((nh, seq, HEAD_DIM), F32), lshape, lshape, lshape],
        grid=(nh, nchunk), in_specs=[blk, blk, blk, pblk, pblk], out_specs=[blk, lblk, lblk, lblk],
        scratch_shapes=[span_f32, span_f32, span_f32],
        compiler_params=_params(("arbitrary", "arbitrary"), VMEM_LIMIT),
    )(q, k, v, k, v)
    return o, (l1, l4, l16)


def _to_residue_major(src, tmp, dst):
    quarter = SPAN // 4
    for r4 in range(4):
        tmp[r4 * quarter:(r4 + 1) * quarter, :] = src[pl.ds(r4, quarter, stride=4), :]
    for r4 in range(4):
        for rp in range(4):
            r = r4 + 4 * rp
            dst[r * KEY_BLOCK:(r + 1) * KEY_BLOCK, :] = tmp[pl.ds(r4 * quarter + rp, KEY_BLOCK, stride=4), :]


def _add_from_residue_major(src, tmp, acc):
    quarter = SPAN // 4
    for r4 in range(4):
        for rp in range(4):
            r = r4 + 4 * rp
            tmp[pl.ds(r4 * quarter + rp, KEY_BLOCK, stride=4), :] = src[r * KEY_BLOCK:(r + 1) * KEY_BLOCK, :]
    for r4 in range(4):
        acc[pl.ds(r4, quarter, stride=4), :] += tmp[r4 * quarter:(r4 + 1) * quarter, :]


def _attn_bwd(q, k, v, do, o, lses, cosf, sinf):
    nh, seq, _ = q.shape
    nchunk = seq // SPAN
    nblk = SPAN // KEY_BLOCK
    wide = DILATIONS[-1]
    assert SPAN == wide * KEY_BLOCK

    def body(q_ref, k_ref, v_ref, do_ref, o_ref, kp_ref, vp_ref, l1_ref, l4_ref, l16_ref,
             cos_ref, sin_ref, cosp_ref, sinp_ref, dq_ref, dk_ref, dv_ref,
             dq_acc, dkc_acc, dvc_acc, dkp_acc, dvp_acc, q16, k16, v16, do16, o16, k16p, v16p,
             dq16, dkc16, dvc16, dkp16, dvp16, tmp, pt_s, ds_s, kcat_s, qb_s, dob_s):
        n = pl.program_id(1)
        ki = lax.broadcasted_iota(jnp.int32, (KEY_BLOCK, KEY_BLOCK), 0)
        qi = lax.broadcasted_iota(jnp.int32, (KEY_BLOCK, KEY_BLOCK), 1)
        bias_own = jnp.where(ki <= qi, 0.0, NEG_INF)
        bias_before = jnp.where(ki >= qi, 0.0, NEG_INF)
        bias_mid = jnp.concatenate([bias_before, bias_own], axis=0)
        bias_first = jnp.concatenate([jnp.where(n > 0, bias_before, NEG_INF), bias_own], axis=0)
        ones8 = jnp.ones((SUBLANES, HEAD_DIM), BF16)

        def row_dot(a, b):
            prod = a * b
            hi = prod.astype(BF16)
            lo = (prod - hi.astype(F32)).astype(BF16)
            return (_dot_nt(ones8, hi) + _dot_nt(ones8, lo))[0:1, :]

        def group(units, srcs, before, l_ref, accs):
            src_q, src_do, src_o, src_k, src_v = srcs
            before_k, before_v = before
            acc_q, acc_kc, acc_vc, acc_kp, acc_vp = accs
            for u, (rows, prow, outside, lrow, _) in enumerate(units):
                dof = src_do[rows, :]
                qb, dob = src_q[rows, :].astype(BF16), dof.astype(BF16)
                kp, vp = (before_k[prow, :], before_v[prow, :]) if outside else (src_k[prow, :], src_v[prow, :])
                kcat = jnp.concatenate([kp, src_k[rows, :]], axis=0).astype(BF16)
                vcat = jnp.concatenate([vp, src_v[rows, :]], axis=0).astype(BF16)
                bias = bias_first if outside else bias_mid
                pt = jnp.exp(_dot_nt(kcat, qb) + bias - l_ref[lrow:lrow + 1, :])
                dst = pt * (_dot_nt(vcat, dob) - row_dot(dof, src_o[rows, :]))
                pt_s[u], ds_s[u], kcat_s[u], qb_s[u], dob_s[u] = pt.astype(BF16), dst.astype(BF16), kcat, qb, dob
            for u, (rows, _, _, _, _) in enumerate(units):
                acc_q[rows, :] += _dot_tn(ds_s[u], kcat_s[u])
            for u, (rows, prow, outside, _, nxt) in enumerate(units):
                dk = _dot(ds_s[u, KEY_BLOCK:, :], qb_s[u])
                dv = _dot(pt_s[u, KEY_BLOCK:, :], dob_s[u])
                if nxt is not None:
                    dk = dk + _dot(ds_s[nxt, :KEY_BLOCK, :], qb_s[nxt])
                    dv = dv + _dot(pt_s[nxt, :KEY_BLOCK, :], dob_s[nxt])
                acc_kc[rows, :] += dk
                acc_vc[rows, :] += dv
                if outside:
                    acc_kp[prow, :] += _dot(ds_s[u, :KEY_BLOCK, :], qb_s[u])
                    acc_vp[prow, :] += _dot(pt_s[u, :KEY_BLOCK, :], dob_s[u])

        @pl.when(n == 0)
        def _():
            for ref in (dkp_acc, dvp_acc, dkp16, dvp16, k16p, v16p):
                ref[...] = jnp.zeros_like(ref)

        @pl.when(n < nchunk)
        def _():
            for ref in (dq_acc, dkc_acc, dvc_acc, dq16, dkc16, dvc16):
                ref[...] = jnp.zeros_like(ref)
            for src, dst in ((q_ref, q16), (k_ref, k16), (v_ref, v16), (do_ref, do16), (o_ref, o16)):
                _to_residue_major(src, tmp, dst)
            natural = (q_ref, do_ref, o_ref, k_ref, v_ref)
            for dil, l_ref in zip(DILATIONS[:-1], (l1_ref, l4_ref)):
                nb = nblk // dil
                units = [(_unit_rows(dil, r, j), _unit_rows(dil, r, (j - 1) % nb), j == 0, r * nb + j,
                          r * nb + j + 1 if j + 1 < nb else None) for r in range(dil) for j in range(nb)]
                group(units, natural, (kp_ref, vp_ref), l_ref, (dq_acc, dkc_acc, dvc_acc, dkp_acc, dvp_acc))
            blocks = [pl.ds(r * KEY_BLOCK, KEY_BLOCK) for r in range(wide)]
            group([(rows, rows, True, r, None) for r, rows in enumerate(blocks)], (q16, do16, o16, k16, v16),
                  (k16p, v16p), l16_ref, (dq16, dkc16, dvc16, dkp16, dvp16))
            _add_from_residue_major(dq16, tmp, dq_acc)
            dq = dq_acc[...]
            dq_ref[...] = ((dq * cos_ref[...] - _rope_partner(dq) * sin_ref[...]) * ATTN_SCALE).astype(BF16)

        @pl.when(n > 0)
        def _():
            _add_from_residue_major(dkp16, tmp, dkp_acc)
            _add_from_residue_major(dvp16, tmp, dvp_acc)
            dk = dkp_acc[...]
            dk_ref[...] = (dk * cosp_ref[...] - _rope_partner(dk) * sinp_ref[...]).astype(BF16)
            dv_ref[...] = dvp_acc[...].astype(BF16)

        @pl.when(n < nchunk)
        def _():
            for src, dst in ((dkc_acc, dkp_acc), (dvc_acc, dvp_acc), (dkc16, dkp16), (dvc16, dvp16),
                             (k16, k16p), (v16, v16p)):
                dst[...] = src[...]

    last = nchunk - 1
    cur = lambda h, n: (h, jnp.minimum(n, last), 0)
    prev = lambda h, n: (h, jnp.clip(n - 1, 0, last), 0)
    blk = lambda idx: pl.BlockSpec((None, SPAN, HEAD_DIM), idx)
    lblk = pl.BlockSpec((None, nblk, KEY_BLOCK), cur)
    tab = pl.BlockSpec((SPAN, HEAD_DIM), lambda h, n: (jnp.minimum(n, last), 0))
    tabp = pl.BlockSpec((SPAN, HEAD_DIM), lambda h, n: (jnp.clip(n - 1, 0, last), 0))
    out_q = pl.BlockSpec((SPAN, HEAD_DIM), lambda h, n: (jnp.minimum(n, last), h))
    out_kv = pl.BlockSpec((SPAN, HEAD_DIM), lambda h, n: (jnp.clip(n - 1, 0, last), h))
    shape = jax.ShapeDtypeStruct((seq, nh * HEAD_DIM), BF16)
    return pl.pallas_call(
        body, name="attn_bwd", out_shape=[shape, shape, shape], grid=(nh, nchunk + 1),
        in_specs=[blk(cur)] * 5 + [blk(prev)] * 2 + [lblk] * 3 + [tab, tab, tabp, tabp],
        out_specs=[out_q, out_kv, out_kv],
        scratch_shapes=[pltpu.VMEM((SPAN, HEAD_DIM), F32)] * 18
                       + [pltpu.VMEM((nblk, 2 * KEY_BLOCK, HEAD_DIM), BF16)] * 3
                       + [pltpu.VMEM((nblk, KEY_BLOCK, HEAD_DIM), BF16)] * 2,
        compiler_params=_params(("arbitrary", "arbitrary"), VMEM_LIMIT),
    )(q, k, v, do, o, k, v, *lses, cosf, sinf, cosf, sinf)


def _hub(x, tgt, hr, pf, o_hm, mod, b_mod, b_gate, g_final, w_out_rnn, w_out_attn, w_o):
    seq = x.shape[0]
    tm = HUB_ROWS
    nsteps = seq // tm

    def body(x_ref, t_ref, hr_ref, zr_ref, za_ref, gr_ref, ga_ref, o_ref, mod_ref, bmod_ref, bg_ref, gf_ref,
             wr_hbm, wa_hbm, wo_hbm,
             dx2_ref, dhr_ref, dzr_ref, do_ref, dza_ref, dgr_ref, dga_ref,
             ur_ref, dyr_ref, ua_ref, dya_ref, mg_ref, dmo_ref,
             ggf_ref, gbg_ref, dgate_ref, loss_ref,
             wr, wa, wo, sem):
        step = pl.program_id(0)

        @pl.when(step == 0)
        def _():
            for src, dst in ((wr_hbm, wr), (wa_hbm, wa), (wo_hbm, wo)):
                cp = pltpu.make_async_copy(src, dst, sem)
                cp.start()
                cp.wait()
            for ref in (ggf_ref, gbg_ref, dgate_ref, loss_ref):
                ref[...] = jnp.zeros_like(ref)

        gate = mod_ref[:, 2 * D_MODEL:] + bmod_ref[:, 2 * D_MODEL:]
        gfin = gf_ref[...]
        hr_t, zr, za = hr_ref[...], zr_ref[...], za_ref[...]
        o = jnp.concatenate([o_ref[hh] for hh in range(N_HEADS)], axis=1)
        sig_zr, sig_za = _sigmoid(zr), _sigmoid(za)
        silu_zr, silu_za = zr * sig_zr, za * sig_za
        u_rnn = (hr_t * silu_zr).astype(BF16)
        u_attn = (o * silu_za).astype(BF16)
        y_rnn = _dot(u_rnn, wr[...])
        y_attn = _dot(u_attn, wa[...])
        sr = _sigmoid(gr_ref[...] + bg_ref[:, :D_MODEL])
        sa = _sigmoid(ga_ref[...] + bg_ref[:, D_MODEL:])
        merged = (sr * y_rnn + sa * y_attn).astype(BF16)
        mo = _dot(merged, wo[...])
        x2 = x_ref[...] + gate * mo
        rstd = lax.rsqrt(jnp.mean(x2 * x2, axis=-1, keepdims=True) + NORM_EPS)
        xn = x2 * rstd
        err = xn * gfin - t_ref[...]
        loss_ref[...] += 0.5 * jnp.sum(jnp.sum(err * err, axis=-1, keepdims=True) * (1.0 / D_MODEL),
                                       axis=0, keepdims=True)

        dy = err * (1.0 / D_MODEL)
        ggf_ref[...] += jnp.sum(dy * xn, axis=0, keepdims=True)
        dxn = dy * gfin
        dx2 = rstd * (dxn - xn * jnp.mean(dxn * xn, axis=-1, keepdims=True))
        dx2_ref[...] = dx2
        dgate_ref[...] += jnp.sum(dx2 * mo, axis=0, keepdims=True)
        dmo = (dx2 * gate).astype(BF16)
        dmerged = _dot_nt(dmo, wo[...])
        mg_ref[...] = merged
        dmo_ref[...] = dmo
        dy_rnn = (dmerged * sr).astype(BF16)
        dy_attn = (dmerged * sa).astype(BF16)
        dg_r = dmerged * y_rnn * sr * (1.0 - sr)
        dg_a = dmerged * y_attn * sa * (1.0 - sa)
        dgr_ref[...] = dg_r.astype(BF16)
        dga_ref[...] = dg_a.astype(BF16)
        gbg_ref[:, :D_MODEL] += jnp.sum(dg_r, axis=0, keepdims=True)
        gbg_ref[:, D_MODEL:] += jnp.sum(dg_a, axis=0, keepdims=True)
        du_rnn = _dot_nt(dy_rnn, wr[...])
        du_attn = _dot_nt(dy_attn, wa[...])
        ur_ref[...] = u_rnn
        dyr_ref[...] = dy_rnn
        ua_ref[...] = u_attn
        dya_ref[...] = dy_attn
        dhr_ref[...] = du_rnn * silu_zr
        dzr_ref[...] = (du_rnn * hr_t * (sig_zr * (1.0 + zr * (1.0 - sig_zr)))).astype(BF16)
        dza_ref[...] = (du_attn * o * (sig_za * (1.0 + za * (1.0 - sig_za)))).astype(BF16)
        d_o = du_attn * silu_za
        for hh in range(N_HEADS):
            do_ref[hh] = d_o[:, hh * HEAD_DIM:(hh + 1) * HEAD_DIM]

    row = pl.BlockSpec((tm, D_MODEL), lambda i: (i, 0))
    piece = lambda slot: pl.BlockSpec((tm, D_MODEL), lambda i: (i, slot))
    hm = pl.BlockSpec((N_HEADS, tm, HEAD_DIM), lambda i: (0, i, 0))
    const = lambda cols: pl.BlockSpec((1, cols), lambda i: (0, 0))
    any_spec = pl.BlockSpec(memory_space=pl.ANY)
    act_f32 = jax.ShapeDtypeStruct((seq, D_MODEL), F32)
    act_bf16 = jax.ShapeDtypeStruct((seq, D_MODEL), BF16)
    return pl.pallas_call(
        body, name="hub",
        out_shape=[act_f32, act_f32, act_bf16, jax.ShapeDtypeStruct((N_HEADS, seq, HEAD_DIM), F32),
                   act_bf16, act_bf16, act_bf16] + [act_bf16] * 6 + [
                   jax.ShapeDtypeStruct((1, D_MODEL), F32), jax.ShapeDtypeStruct((1, 2 * D_MODEL), F32),
                   jax.ShapeDtypeStruct((1, D_MODEL), F32), jax.ShapeDtypeStruct((1, 1), F32)],
        grid=(nsteps,),
        in_specs=[row, row, row, piece(1), piece(2), piece(3), piece(4), hm,
                  const(3 * D_MODEL), const(3 * D_MODEL), const(2 * D_MODEL), const(D_MODEL),
                  any_spec, any_spec, any_spec],
        out_specs=[row, row, row, hm, row, row, row] + [row] * 6 + [
                   const(D_MODEL), const(2 * D_MODEL), const(D_MODEL), const(1)],
        scratch_shapes=[pltpu.VMEM((D_MODEL, D_MODEL), BF16)] * 3 + [pltpu.SemaphoreType.DMA],
        compiler_params=_params(("arbitrary",), VMEM_LIMIT),
    )(x, tgt, hr, pf, pf, pf, pf, o_hm, mod, b_mod, b_gate, g_final, w_out_rnn, w_out_attn, w_o)


def _pair_grads(name, lefts, rights):
    n = len(rights)
    shared = len(lefts) == 1
    seq = rights[0].shape[0]
    tk = WGRAD_ROWS
    nk = seq // tk

    def body(*refs):
        l_refs, r_refs, out_ref = refs[:len(lefts)], refs[len(lefts):len(lefts) + n], refs[len(lefts) + n]
        j, kk = pl.program_id(0), pl.program_id(1)

        @pl.when(kk == 0)
        def _():
            out_ref[...] = jnp.zeros_like(out_ref)

        for m in range(n):
            @pl.when(j == m)
            def _(m=m):
                out_ref[...] += _dot_tn(l_refs[0 if shared else m][...], r_refs[m][...])

    def spec(m):
        return pl.BlockSpec((tk, D_MODEL), lambda j, kk: (jnp.where(j == m, kk, jnp.where(j < m, 0, nk - 1)), 0))

    left_specs = [pl.BlockSpec((tk, D_MODEL), lambda j, kk: (kk, 0))] if shared else [spec(m) for m in range(n)]
    return pl.pallas_call(
        body, name=name,
        out_shape=jax.ShapeDtypeStruct((n, D_MODEL, D_MODEL), F32),
        grid=(n, nk),
        in_specs=left_specs + [spec(m) for m in range(n)],
        out_specs=pl.BlockSpec((None, D_MODEL, D_MODEL), lambda j, kk: (j, 0, 0)),
        compiler_params=_params(("arbitrary", "arbitrary"), VMEM_LIMIT),
    )(*lefts, *rights)


def _dh_dx(pieces, w_in_all, x, dx2, mod, b_mod, g_norm):
    seq = x.shape[0]
    tm = DX_ROWS

    def body(*refs):
        p_refs = refs[:8]
        w_hbm, x_ref, dx2_ref, mod_ref, bmod_ref, g_ref = refs[8:14]
        gx_ref, dshift_ref, dscale_ref, ggn_ref, w_scr, sem = refs[14:]
        step = pl.program_id(0)

        @pl.when(step == 0)
        def _():
            cp = pltpu.make_async_copy(w_hbm, w_scr, sem)
            cp.start()
            cp.wait()
            for ref in (dshift_ref, dscale_ref, ggn_ref):
                ref[...] = jnp.zeros_like(ref)

        dh = _dot_nt(p_refs[0][...], w_scr[0])
        for j in range(1, 8):
            dh = dh + _dot_nt(p_refs[j][...], w_scr[j])
        scale1 = 1.0 + mod_ref[:, D_MODEL:2 * D_MODEL] + bmod_ref[:, D_MODEL:2 * D_MODEL]
        g = g_ref[...]
        xf = x_ref[...]
        rstd_t = lax.rsqrt(jnp.mean(xf * xf, axis=-1, keepdims=True) + NORM_EPS)
        xn = xf * rstd_t
        dshift_ref[...] += jnp.sum(dh, axis=0, keepdims=True)
        dscale_ref[...] += jnp.sum(dh * (xn * g), axis=0, keepdims=True)
        ggn_ref[...] += jnp.sum(dh * scale1 * xn, axis=0, keepdims=True)
        dxn = dh * (g * scale1)
        gx_ref[...] = rstd_t * (dxn - xn * jnp.mean(dxn * xn, axis=-1, keepdims=True)) + dx2_ref[...]

    row = pl.BlockSpec((tm, D_MODEL), lambda i: (i, 0))
    const = lambda cols: pl.BlockSpec((1, cols), lambda i: (0, 0))
    vec = jax.ShapeDtypeStruct((1, D_MODEL), F32)
    return pl.pallas_call(
        body, name="dh_dx",
        out_shape=[jax.ShapeDtypeStruct((seq, D_MODEL), F32), vec, vec, vec],
        grid=(seq // tm,),
        in_specs=[row] * 8 + [pl.BlockSpec(memory_space=pl.ANY), row, row,
                              const(3 * D_MODEL), const(3 * D_MODEL), const(D_MODEL)],
        out_specs=[row, const(D_MODEL), const(D_MODEL), const(D_MODEL)],
        scratch_shapes=[pltpu.VMEM((8, D_MODEL, D_MODEL), BF16), pltpu.SemaphoreType.DMA],
        compiler_params=_params(("arbitrary",), VMEM_LIMIT),
    )(*pieces, w_in_all, x, dx2, mod, b_mod, g_norm)


def _adamw(name, w, g, m, v):
    rows, cols = w.shape
    tr = rows if rows <= 256 else 256

    def body(w_ref, g_ref, m_ref, v_ref, d_ref, nm_ref, nv_ref):
        gv = g_ref[...]
        nm = ADAM_B1 * m_ref[...] + (1.0 - ADAM_B1) * gv
        nv = ADAM_B2 * v_ref[...] + (1.0 - ADAM_B2) * (gv * gv)
        m_hat = nm / (1.0 - ADAM_B1 ** ADAM_STEP)
        v_hat = nv / (1.0 - ADAM_B2 ** ADAM_STEP)
        d_ref[...] = -ADAM_LR * (m_hat / (jnp.sqrt(v_hat) + ADAM_EPS) + ADAM_WD * w_ref[...])
        nm_ref[...] = nm
        nv_ref[...] = nv

    spec = pl.BlockSpec((tr, cols), lambda i: (i, 0))
    shape = jax.ShapeDtypeStruct((rows, cols), F32)
    return pl.pallas_call(
        body, name=name, out_shape=[shape, shape, shape], grid=(rows // tr,),
        in_specs=[spec] * 4, out_specs=[spec] * 3,
        compiler_params=_params(("arbitrary",)),
    )(w, g, m, v)


def kernel(x, c, positions, g_norm, w_mod, b_mod, w_in, b_gate, conv_w, conv_b, w_a, b_a, w_x, b_x, lam, w_out_rnn, w_out_attn, w_o, g_final, loss_target, m_g_norm, m_w_mod, m_b_mod, m_w_in, m_b_gate, m_conv_w, m_conv_b, m_w_a, m_b_a, m_w_x, m_b_x, m_lam, m_w_out_rnn, m_w_out_attn, m_w_o, m_g_final, v_g_norm, v_w_mod, v_b_mod, v_w_in, v_b_gate, v_conv_w, v_conv_b, v_w_a, v_b_a, v_w_x, v_b_x, v_lam, v_w_out_rnn, v_w_out_attn, v_w_o, v_g_final):
    seq = x.shape[1]
    me = _index(_my_pos())
    xs, tgt = x[0], loss_target[0]

    pos = positions[0].astype(F32)[:, None]
    inv_freq = ROPE_THETA ** (-jnp.arange(0, 2 * ROT_HALF, 2, dtype=F32) / (2 * ROT_HALF))
    ang = pos * inv_freq
    rest = HEAD_DIM - 2 * ROT_HALF
    cosf = jnp.concatenate([jnp.cos(ang), jnp.cos(ang), jnp.ones((seq, rest), F32)], axis=1)
    sinf = jnp.concatenate([-jnp.sin(ang), jnp.sin(ang), jnp.zeros((seq, rest), F32)], axis=1)
    keep = (positions[0] != 0).astype(F32)[:, None]

    w_in_all, w_or_all, w_oa_all, w_o_all = _ag_big(
        "gather_weights", [w_in[0].astype(BF16), w_out_rnn[0].astype(BF16),
                           w_out_attn[0].astype(BF16), w_o[0].astype(BF16)])
    w_or_all, w_oa_all, w_o_all = (t.reshape(D_MODEL, D_MODEL) for t in (w_or_all, w_oa_all, w_o_all))
    conv_w8 = _ag_small("gather_conv_w", jnp.pad(conv_w[0], ((0, SUBLANES - 4), (0, 0))))
    c_all = _ag_small("gather_c", jnp.broadcast_to(c, (SUBLANES, D_MODEL)))[:, 0, :]
    mod_cols = w_mod.shape[2]
    mod_part = _ag_small("gather_mod", _mod_fwd(c_all, w_mod[0]))
    mod = lax.dynamic_index_in_dim(mod_part, me, axis=1, keepdims=False).reshape(1, N_DEV * mod_cols)

    blocks = lambda t: t.reshape(RNN_BLOCKS, 1, 128)
    rnn_params = (conv_w8, blocks(conv_b), w_a[0], blocks(b_a), w_x[0], blocks(b_x), blocks(lam))

    h = _norm(xs, mod, b_mod, g_norm)
    pf, q, k, v = _proj(h, w_in_all, cosf, sinf)
    hr = _rnn_fwd(pf, keep, *rnn_params)
    o, lses = _attn_fwd(q, k, v)

    (dx2, dhr, dz_rnn, d_o, dz_attn, dg_r, dg_a, u_rnn, dy_rnn, u_attn, dy_attn, merged, dmo,
     gp_g_final, gp_b_gate, dgate, loss_part) = _hub(
        xs, tgt, hr, pf, o, mod, b_mod, b_gate, g_final.reshape(1, D_MODEL), w_or_all, w_oa_all, w_o_all)
    gp_w_or, gp_w_oa, gp_w_o = _pair_grads("out_grads", [u_rnn, u_attn, merged], [dy_rnn, dy_attn, dmo])
    dq, dk, dv = _attn_bwd(q, k, v, d_o, o, lses, cosf, sinf)
    dx_rnn, gp_conv_w, gp_conv_b, gp_w_a, gp_b_a, gp_w_x, gp_b_x, gp_lam = _rnn_bwd(pf, hr, dhr, keep, *rnn_params)
    pieces = [dx_rnn, dz_rnn, dq, dk, dv, dz_attn, dg_r, dg_a]
    grad_x, dshift, dscale, gp_g_norm = _dh_dx(pieces, w_in_all, xs, dx2, mod, b_mod, g_norm)
    gp_w_in = _pair_grads("w_in_grad", [h], pieces)

    dmod = jnp.concatenate([dshift, dscale, dgate], axis=1)
    dmod_all = _ag_small("gather_dmod", jnp.broadcast_to(dmod, (SUBLANES, 3 * D_MODEL)))[:, 0, :]
    dmod_cols = lax.dynamic_slice_in_dim(dmod_all, me * mod_cols, mod_cols, axis=1)
    g_b_mod, g_w_mod = _mod_bwd(c_all, dmod_all, dmod_cols)

    flat = lambda t: t.reshape(-1, 128)
    small = [flat(gp_g_norm), flat(gp_b_gate), flat(gp_conv_b), flat(gp_b_a), flat(gp_b_x), flat(gp_lam),
             flat(gp_g_final), flat(gp_conv_w), jnp.broadcast_to(loss_part, (SUBLANES, 128)),
             flat(gp_w_a), flat(gp_w_x)]
    sizes = [t.shape[0] for t in small]
    small.append(jnp.zeros((-sum(sizes) % (2 * SUBLANES), 128), F32))
    total = _allreduce_small("allreduce_small_grads", jnp.concatenate(small, axis=0))
    offs = [sum(sizes[:i]) for i in range(len(sizes))]
    (g_g_norm, g_b_gate, g_conv_b, g_b_a, g_b_x, g_lam, g_g_final, g_conv_w_all, loss_rows, g_w_a, g_w_x) = (
        total[o_:o_ + s_] for o_, s_ in zip(offs, sizes))
    loss = loss_rows[0, 0]
    g_conv_w = lax.dynamic_index_in_dim(g_conv_w_all.reshape(RNN_BLOCKS, SUBLANES, 128), me, axis=0,
                                        keepdims=False)[:4]

    stacks = [gp_w_in, gp_w_or.reshape(N_DEV, 128, D_MODEL), gp_w_oa.reshape(N_DEV, 128, D_MODEL),
              gp_w_o.reshape(N_DEV, 128, D_MODEL)]
    from_sib = _rs_to_sibling("rs_sibling", stacks)
    targets = jnp.bitwise_xor(me, 2 * jnp.arange(4, dtype=jnp.int32)).astype(jnp.int32)
    sums = [_add_sibling("rs_add_sibling_%d" % a, s_, r_, targets) for a, (s_, r_) in enumerate(zip(stacks, from_sib))]
    from_chips = _rs_to_chips("rs_chips", [send for _, send in sums])
    g_w_in, g_w_or, g_w_oa, g_w_o = (
        _add_chips("rs_add_chips_%d" % a, own, r_) for a, ((own, _), r_) in enumerate(zip(sums, from_chips)))

    weights = [
        ("g_norm", g_norm, g_g_norm, m_g_norm, v_g_norm, (SUBLANES, 128)),
        ("w_mod", w_mod, g_w_mod, m_w_mod, v_w_mod, (D_MODEL, mod_cols)),
        ("b_mod", b_mod, g_b_mod, m_b_mod, v_b_mod, (3 * SUBLANES, 128)),
        ("w_in", w_in, g_w_in, m_w_in, v_w_in, (D_MODEL, D_MODEL)),
        ("b_gate", b_gate, g_b_gate, m_b_gate, v_b_gate, (2 * SUBLANES, 128)),
        ("conv_w", conv_w, g_conv_w, m_conv_w, v_conv_w, (4, 128)),
        ("conv_b", conv_b, g_conv_b, m_conv_b, v_conv_b, (SUBLANES, 128)),
        ("w_a", w_a, g_w_a, m_w_a, v_w_a, (RNN_BLOCKS * 128, 128)),
        ("b_a", b_a, g_b_a, m_b_a, v_b_a, (SUBLANES, 128)),
        ("w_x", w_x, g_w_x, m_w_x, v_w_x, (RNN_BLOCKS * 128, 128)),
        ("b_x", b_x, g_b_x, m_b_x, v_b_x, (SUBLANES, 128)),
        ("lam", lam, g_lam, m_lam, v_lam, (SUBLANES, 128)),
        ("w_out_rnn", w_out_rnn, g_w_or, m_w_out_rnn, v_w_out_rnn, (128, D_MODEL)),
        ("w_out_attn", w_out_attn, g_w_oa, m_w_out_attn, v_w_out_attn, (128, D_MODEL)),
        ("w_o", w_o, g_w_o, m_w_o, v_w_o, (128, D_MODEL)),
        ("g_final", g_final, g_g_final, m_g_final, v_g_final, (SUBLANES, 128)),
    ]
    out_g, out_d, out_m, out_v = [], [], [], []
    for name, w_, g_, m_, v_, shape2 in weights:
        d_, nm_, nv_ = _adamw("adamw_" + name, w_.reshape(shape2), g_.reshape(shape2), m_.reshape(shape2),
                              v_.reshape(shape2))
        out_g.append(g_.reshape(w_.shape))
        out_d.append(d_.reshape(w_.shape))
        out_m.append(nm_.reshape(w_.shape))
        out_v.append(nv_.reshape(w_.shape))
    return (loss, grad_x[None], *out_g, *out_d, *out_m, *out_v)
```

```python
import jax
import jax.numpy as jnp
from jax import lax
from jax.experimental import pallas as pl
from jax.experimental.pallas import tpu as pltpu

F32 = jnp.float32
BF16 = jnp.bfloat16
MESH = pl.DeviceIdType.MESH

D_MODEL = 1024
N_HEADS = 8
HEAD_DIM = 128
RNN_BLOCKS = 8
N_DEV = 8
ROT_HALF = 16
ROPE_THETA = 500000.0
DILATIONS = (1, 4, 16)
KEY_BLOCK = 128
SPAN = KEY_BLOCK * DILATIONS[-1]
ATTN_SCALE = HEAD_DIM ** -0.5
NORM_EPS = 1e-6
LRU_C = 8.0
NEG_INF = -1e30
ADAM_LR, ADAM_B1, ADAM_B2, ADAM_EPS, ADAM_WD, ADAM_STEP = 0.001, 0.9, 0.999, 1e-08, 0.01, 10

SUBLANES = 8
VMEM_LIMIT = 56 * 1024 * 1024
PROJ_ROWS = 1024
RNN_ROWS = 512
HUB_ROWS = 256
DX_ROWS = 256
WGRAD_ROWS = 1024
ADD_ROWS = 256


def _params(sem=None, vmem=None):
    return pltpu.CompilerParams(dimension_semantics=sem, vmem_limit_bytes=vmem)


def _dot(a, b):
    return jnp.dot(a, b, preferred_element_type=F32)


def _dot_nt(a, b):
    return lax.dot_general(a, b, (((1,), (1,)), ((), ())), preferred_element_type=F32)


def _dot_tn(a, b):
    return lax.dot_general(a, b, (((0,), (0,)), ((), ())), preferred_element_type=F32)


def _sigmoid(z):
    return 1.0 / (1.0 + jnp.exp(-z))


def _expm1_nonpos(z, exp_z):
    return jnp.where(z > -0.01, z * (1.0 + 0.5 * z), exp_z - 1.0)


def _my_pos():
    return lax.axis_index("x"), lax.axis_index("y"), lax.axis_index("c")


def _flip(pos, k):
    x, y, c = pos
    return ((1 - x) if k & 4 else x, (1 - y) if k & 2 else y, (1 - c) if k & 1 else c)


def _index(pos):
    return 4 * pos[0] + 2 * pos[1] + pos[2]


def _ag_small(name, v):
    rows, cols = v.shape

    def body(v_ref, out_ref, send_sems, recv_sems):
        me = _my_pos()
        out_ref[_index(me)] = v_ref[...]
        sends = []
        for k in range(1, N_DEV):
            cp = pltpu.make_async_remote_copy(
                src_ref=v_ref, dst_ref=out_ref.at[_index(me)], send_sem=send_sems.at[k - 1],
                recv_sem=recv_sems.at[k - 1], device_id=_flip(me, k), device_id_type=MESH)
            cp.start()
            sends.append(cp)
        for k in range(1, N_DEV):
            peer = _flip(me, k)
            pltpu.make_async_remote_copy(
                src_ref=v_ref, dst_ref=out_ref.at[_index(peer)], send_sem=send_sems.at[k - 1],
                recv_sem=recv_sems.at[k - 1], device_id=peer, device_id_type=MESH).wait_recv()
        for cp in sends:
            cp.wait_send()

    return pl.pallas_call(
        body, name=name,
        out_shape=jax.ShapeDtypeStruct((N_DEV, rows, cols), v.dtype),
        in_specs=[pl.BlockSpec(memory_space=pltpu.VMEM)],
        out_specs=pl.BlockSpec(memory_space=pltpu.VMEM),
        scratch_shapes=[pltpu.SemaphoreType.DMA((N_DEV - 1,)), pltpu.SemaphoreType.DMA((N_DEV - 1,))],
        compiler_params=_params(None, VMEM_LIMIT),
    )(v)


def _ag_big(name, shards):
    n = len(shards)

    def body(*refs):
        ins, outs = refs[:n], refs[n:2 * n]
        send_sems, recv_sems, local_sems = refs[2 * n:]
        me = _my_pos()
        sib = _flip(me, 1)
        chips = [2, 4, 6]

        def copy(a, k, block, to, src=None):
            rows = outs[a].at[_index(block)]
            return pltpu.make_async_remote_copy(
                src_ref=rows if src is None else src, dst_ref=rows,
                send_sem=send_sems.at[a * 7 + k], recv_sem=recv_sems.at[a * 7 + k],
                device_id=to, device_id_type=MESH)

        started = []
        for a in range(n):
            mine = pltpu.make_async_copy(ins[a], outs[a].at[_index(me)], local_sems.at[a])
            mine.start()
            started.append(mine)
        sends = []
        for a in range(n):
            first = [copy(a, 0, me, sib, src=ins[a])]
            first += [copy(a, 1 + j, me, _flip(me, ch), src=ins[a]) for j, ch in enumerate(chips)]
            for cp in first:
                cp.start()
            sends += first
        for j, ch in enumerate(chips):
            for a in range(n):
                copy(a, 1 + j, _flip(me, ch), me).wait_recv()
                fwd = copy(a, 4 + j, _flip(me, ch), sib)
                fwd.start()
                sends.append(fwd)
        for a in range(n):
            copy(a, 0, sib, me).wait_recv()
            for j, ch in enumerate(chips):
                copy(a, 4 + j, _flip(sib, ch), me).wait_recv()
        for cp in sends:
            cp.wait_send()
        for mine in started:
            mine.wait()

    any_spec = pl.BlockSpec(memory_space=pl.ANY)
    return pl.pallas_call(
        body, name=name,
        out_shape=[jax.ShapeDtypeStruct((N_DEV,) + s.shape, s.dtype) for s in shards],
        in_specs=[any_spec] * n, out_specs=[any_spec] * n,
        scratch_shapes=[pltpu.SemaphoreType.DMA((7 * n,)), pltpu.SemaphoreType.DMA((7 * n,)),
                        pltpu.SemaphoreType.DMA((n,))],
    )(*shards)


def _rs_to_sibling(name, stacks):
    n = len(stacks)

    def body(*refs):
        ins, outs = refs[:n], refs[n:2 * n]
        send_sems, recv_sems = refs[2 * n:]
        me = _my_pos()
        sib = _flip(me, 1)
        sends = []
        for a in range(n):
            for m in range(4):
                target = _flip(sib, 2 * m)
                cp = pltpu.make_async_remote_copy(
                    src_ref=ins[a].at[_index(target)], dst_ref=outs[a].at[m],
                    send_sem=send_sems.at[a * 4 + m], recv_sem=recv_sems.at[a * 4 + m],
                    device_id=sib, device_id_type=MESH)
                cp.start()
                sends.append(cp)
        for cp in sends:
            cp.wait_recv()
        for cp in sends:
            cp.wait_send()

    any_spec = pl.BlockSpec(memory_space=pl.ANY)
    return pl.pallas_call(
        body, name=name,
        out_shape=[jax.ShapeDtypeStruct((4,) + s.shape[1:], s.dtype) for s in stacks],
        in_specs=[any_spec] * n, out_specs=[any_spec] * n,
        scratch_shapes=[pltpu.SemaphoreType.DMA((4 * n,)), pltpu.SemaphoreType.DMA((4 * n,))],
    )(*stacks)


def _chip_copies(srcs, lands, send_sems, recv_sems):
    me = _my_pos()
    return [pltpu.make_async_remote_copy(
        src_ref=srcs[a].at[m - 1], dst_ref=lands[a].at[m - 1],
        send_sem=send_sems.at[a * 3 + m - 1], recv_sem=recv_sems.at[a * 3 + m - 1],
        device_id=_flip(me, 2 * m), device_id_type=MESH) for a in range(len(srcs)) for m in range(1, 4)]


def _rs_chips_start(sums):
    n = len(sums)

    def body(*refs):
        srcs, lands = refs[:n], refs[n:2 * n]
        send_sems, recv_sems = refs[2 * n:2 * n + 2]
        token = refs[-1]
        for cp in _chip_copies(srcs, lands, send_sems, recv_sems):
            cp.start()
        token[...] = jnp.zeros_like(token)

    hbm = pl.BlockSpec(memory_space=pltpu.HBM)
    sem = pl.BlockSpec(memory_space=pltpu.SEMAPHORE)
    held = [pltpu.HBM(s.shape, s.dtype) for s in sums]
    outs = pl.pallas_call(
        body, name="rs_chips_start",
        out_shape=(pltpu.SemaphoreType.DMA((3 * n,)), pltpu.SemaphoreType.DMA((3 * n,)), *held, *held,
                   jax.ShapeDtypeStruct((SUBLANES, 128), F32)),
        in_specs=[hbm] * (2 * n),
        out_specs=(sem, sem, *[hbm] * (2 * n), pl.BlockSpec(memory_space=pltpu.VMEM)),
        input_output_aliases={i: 2 + i for i in range(2 * n)},
        compiler_params=pltpu.CompilerParams(has_side_effects=pltpu.SideEffectType.DATAFLOW_SIDE_EFFECTING),
    )(*[pltpu.with_memory_space_constraint(s, pltpu.HBM) for s in sums],
      *[pltpu.with_memory_space_constraint(lax.empty(s.shape, s.dtype), pltpu.HBM) for s in sums])
    return outs[0], outs[1], outs[2:2 + n], outs[2 + n:2 + 2 * n], outs[-1]


def _rs_chips_wait(send_sems, recv_sems, srcs, lands, after):
    n = len(srcs)

    def body(*refs):
        src_refs, land_refs = refs[:n], refs[n:2 * n]
        sends, recvs = refs[2 * n:2 * n + 2]
        for cp in _chip_copies(src_refs, land_refs, sends, recvs):
            cp.wait_send()
            cp.wait_recv()

    hbm = pl.BlockSpec(memory_space=pltpu.HBM)
    sem = pl.BlockSpec(memory_space=pltpu.SEMAPHORE)
    held = [pltpu.HBM(s.shape, s.dtype) for s in srcs]
    outs = pl.pallas_call(
        body, name="rs_chips_wait", out_shape=(*held, *held),
        in_specs=[hbm] * (2 * n) + [sem, sem, pl.BlockSpec(memory_space=pl.ANY)],
        out_specs=[hbm] * (2 * n),
        input_output_aliases={i: i for i in range(2 * n)},
        compiler_params=pltpu.CompilerParams(has_side_effects=pltpu.SideEffectType.DATAFLOW_SIDE_EFFECTING),
    )(*srcs, *lands, send_sems, recv_sems, after)
    return outs[n:]


def _add_sibling(name, stack, recv, targets):
    _, rows, cols = stack.shape
    tr = min(rows, ADD_ROWS)

    def own_body(t_ref, a_ref, b_ref, o_ref):
        o_ref[...] = a_ref[...] + b_ref[...]

    own = pl.pallas_call(
        own_body, name=name + "_own",
        out_shape=jax.ShapeDtypeStruct((rows, cols), F32),
        grid_spec=pltpu.PrefetchScalarGridSpec(
            num_scalar_prefetch=1, grid=(rows // tr,),
            in_specs=[pl.BlockSpec((None, tr, cols), lambda i, t: (t[0], i, 0)),
                      pl.BlockSpec((None, tr, cols), lambda i, t: (0, i, 0))],
            out_specs=pl.BlockSpec((tr, cols), lambda i, t: (i, 0))),
        compiler_params=_params(("arbitrary",)),
    )(targets, stack, recv)

    def send_body(t_ref, a_ref, b_ref, o_ref):
        o_ref[...] = (a_ref[...] + b_ref[...]).astype(BF16)

    send = pl.pallas_call(
        send_body, name=name + "_send",
        out_shape=jax.ShapeDtypeStruct((3, rows, cols), BF16),
        grid_spec=pltpu.PrefetchScalarGridSpec(
            num_scalar_prefetch=1, grid=(3, rows // tr),
            in_specs=[pl.BlockSpec((None, tr, cols), lambda m, i, t: (t[m + 1], i, 0)),
                      pl.BlockSpec((None, tr, cols), lambda m, i, t: (m + 1, i, 0))],
            out_specs=pl.BlockSpec((None, tr, cols), lambda m, i, t: (m, i, 0))),
        compiler_params=_params(("arbitrary", "arbitrary")),
    )(targets, stack, recv)
    return own, send


def _add_chips(name, own, recv):
    rows, cols = own.shape
    tr = min(rows, ADD_ROWS)

    def body(a_ref, b_ref, o_ref):
        o_ref[...] = ((a_ref[...] + b_ref[0].astype(F32)) + b_ref[1].astype(F32)) + b_ref[2].astype(F32)

    return pl.pallas_call(
        body, name=name,
        out_shape=jax.ShapeDtypeStruct((rows, cols), F32),
        grid=(rows // tr,),
        in_specs=[pl.BlockSpec((tr, cols), lambda i: (i, 0)),
                  pl.BlockSpec((3, tr, cols), lambda i: (0, i, 0))],
        out_specs=pl.BlockSpec((tr, cols), lambda i: (i, 0)),
        compiler_params=_params(("arbitrary",)),
    )(own, recv)


def _allreduce_small(name, v):
    rows, cols = v.shape
    half = rows // 2
    assert rows % (2 * SUBLANES) == 0

    def body(v_ref, out_ref, from_sib, chip_half, from_chips, send_sems, recv_sems):
        me = _my_pos()
        sib = _flip(me, 1)
        mine = pl.ds(pl.multiple_of(me[2] * half, SUBLANES), half)
        theirs = pl.ds(pl.multiple_of((1 - me[2]) * half, SUBLANES), half)

        def copy(k, src, dst, to):
            return pltpu.make_async_remote_copy(src_ref=src, dst_ref=dst, send_sem=send_sems.at[k],
                                                recv_sem=recv_sems.at[k], device_id=to, device_id_type=MESH)

        to_sib = copy(0, v_ref.at[theirs], from_sib, sib)
        to_sib.start()
        to_sib.wait_recv()
        chip_half[...] = v_ref[mine, :] + from_sib[...]
        to_chips = [copy(m, chip_half, from_chips.at[m - 1], _flip(me, 2 * m)) for m in range(1, 4)]
        for cp in to_chips:
            cp.start()
        for cp in to_chips:
            cp.wait_recv()
        my_chip = 2 * me[0] + me[1]
        total = None
        for chip in range(4):
            slot = jnp.maximum(jnp.bitwise_xor(chip, my_chip) - 1, 0)
            part = jnp.where(chip == my_chip, chip_half[...], from_chips[slot])
            total = part if total is None else total + part
        out_ref[mine, :] = total
        swap = copy(4, out_ref.at[mine], out_ref.at[mine], sib)
        swap.start()
        copy(4, out_ref.at[theirs], out_ref.at[theirs], sib).wait_recv()
        for cp in [to_sib, swap] + to_chips:
            cp.wait_send()

    return pl.pallas_call(
        body, name=name, out_shape=jax.ShapeDtypeStruct((rows, cols), F32),
        in_specs=[pl.BlockSpec(memory_space=pltpu.VMEM)],
        out_specs=pl.BlockSpec(memory_space=pltpu.VMEM),
        scratch_shapes=[pltpu.VMEM((half, cols), F32), pltpu.VMEM((half, cols), F32),
                        pltpu.VMEM((3, half, cols), F32),
                        pltpu.SemaphoreType.DMA((5,)), pltpu.SemaphoreType.DMA((5,))],
        compiler_params=_params(None, VMEM_LIMIT),
    )(v)


def _mod_fwd(c_all, w_mod):
    def body(c_ref, w_ref, o_ref):
        c = c_ref[...]
        o_ref[...] = jnp.dot(c * _sigmoid(c), w_ref[...], preferred_element_type=F32,
                             precision=lax.Precision.HIGHEST)

    return pl.pallas_call(
        body, name="mod_fwd", out_shape=jax.ShapeDtypeStruct((N_DEV, w_mod.shape[1]), F32),
    )(c_all, w_mod)


def _mod_bwd(c_all, dmod_all, dmod_cols):
    def body(c_ref, da_ref, dc_ref, gb_ref, gw_ref):
        c = c_ref[...]
        acc = da_ref[0:1, :]
        for b in range(1, N_DEV):
            acc = acc + da_ref[b:b + 1, :]
        gb_ref[...] = acc
        gw_ref[...] = lax.dot_general(c * _sigmoid(c), dc_ref[...], (((0,), (0,)), ((), ())),
                                      preferred_element_type=F32, precision=lax.Precision.HIGHEST)

    return pl.pallas_call(
        body, name="mod_bwd",
        out_shape=[jax.ShapeDtypeStruct((1, dmod_all.shape[1]), F32),
                   jax.ShapeDtypeStruct((c_all.shape[1], dmod_cols.shape[1]), F32)],
    )(c_all, dmod_all, dmod_cols)


def _rope_partner(t):
    lane = lax.broadcasted_iota(jnp.int32, t.shape, 1)
    return jnp.where(lane < ROT_HALF, pltpu.roll(t, HEAD_DIM - ROT_HALF, 1), pltpu.roll(t, ROT_HALF, 1))


def _norm(x, mod, b_mod, g_norm):
    seq = x.shape[0]
    tm = PROJ_ROWS

    def body(x_ref, mod_ref, bmod_ref, g_ref, h_ref):
        xf = x_ref[...]
        rstd = lax.rsqrt(jnp.mean(xf * xf, axis=-1, keepdims=True) + NORM_EPS)
        shift = mod_ref[:, 0:D_MODEL] + bmod_ref[:, 0:D_MODEL]
        scale = mod_ref[:, D_MODEL:2 * D_MODEL] + bmod_ref[:, D_MODEL:2 * D_MODEL]
        h_ref[...] = (((xf * rstd) * g_ref[...]) * (1.0 + scale) + shift).astype(BF16)

    row = pl.BlockSpec((tm, D_MODEL), lambda i: (i, 0))
    const = lambda cols: pl.BlockSpec((1, cols), lambda i: (0, 0))
    return pl.pallas_call(
        body, name="norm", out_shape=jax.ShapeDtypeStruct((seq, D_MODEL), BF16), grid=(seq // tm,),
        in_specs=[row, const(3 * D_MODEL), const(3 * D_MODEL), const(D_MODEL)], out_specs=row,
        compiler_params=_params(("arbitrary",), VMEM_LIMIT),
    )(x, mod, b_mod, g_norm)


def _proj(h, w_in_all, cosf, sinf):
    seq = h.shape[0]
    tm = PROJ_ROWS
    last = seq // tm - 1

    def body(h_ref, w_ref, cos_ref, sin_ref, pf_ref, q_ref, k_ref, v_ref):
        j = pl.program_id(0)

        @pl.when((j < 2) | (j > 4))
        def _():
            pf_ref[...] = _dot(h_ref[...], w_ref[...])

        def heads(dst_ref, rotate, gain):
            for pair in range(N_HEADS // 2):
                both = _dot(h_ref[...], w_ref[:, 2 * pair * HEAD_DIM:2 * (pair + 1) * HEAD_DIM])
                for hh in (2 * pair, 2 * pair + 1):
                    t = both[:, (hh % 2) * HEAD_DIM:(hh % 2 + 1) * HEAD_DIM]
                    if rotate:
                        t = t * cos_ref[...] + _rope_partner(t) * sin_ref[...]
                    dst_ref[hh] = t if gain is None else t * gain

        @pl.when(j == 2)
        def _():
            heads(q_ref, True, ATTN_SCALE)

        @pl.when(j == 3)
        def _():
            heads(k_ref, True, None)

        @pl.when(j == 4)
        def _():
            heads(v_ref, False, None)

    def pf_block(j, i):
        f32_piece = (j < 2) | (j > 4)
        return (jnp.where(f32_piece, i, last), jnp.where(j < 2, j, jnp.where(j < 5, 1, j - 3)))

    def hm_block(piece):
        return lambda j, i: (0, jnp.where(j == piece, i, jnp.where(j < piece, 0, last)), 0)

    hm = jax.ShapeDtypeStruct((N_HEADS, seq, HEAD_DIM), F32)
    hm_spec = lambda piece: pl.BlockSpec((N_HEADS, tm, HEAD_DIM), hm_block(piece))
    row = lambda j, i: (i, 0)
    return pl.pallas_call(
        body, name="proj",
        out_shape=[jax.ShapeDtypeStruct((seq, 5 * D_MODEL), F32), hm, hm, hm],
        grid=(8, seq // tm),
        in_specs=[pl.BlockSpec((tm, D_MODEL), row),
                  pl.BlockSpec((None, D_MODEL, D_MODEL), lambda j, i: (j, 0, 0)),
                  pl.BlockSpec((tm, HEAD_DIM), row), pl.BlockSpec((tm, HEAD_DIM), row)],
        out_specs=[pl.BlockSpec((tm, D_MODEL), pf_block), hm_spec(2), hm_spec(3), hm_spec(4)],
        compiler_params=_params(("arbitrary", "arbitrary"), VMEM_LIMIT),
    )(h, w_in_all, cosf, sinf)


def _shift_down(v, s, head):
    rows = v.shape[0]
    row = lax.broadcasted_iota(jnp.int32, v.shape, 0)
    fill = jnp.concatenate([pltpu.roll(head, s, 0), jnp.zeros((rows - SUBLANES, v.shape[1]), v.dtype)], axis=0)
    return jnp.where(row < s, fill, pltpu.roll(v, s, 0))


def _shift_up(v, s, tail):
    rows = v.shape[0]
    row = lax.broadcasted_iota(jnp.int32, v.shape, 0)
    fill = jnp.concatenate([jnp.zeros((rows - SUBLANES, v.shape[1]), v.dtype),
                            pltpu.roll(tail, SUBLANES - s, 0)], axis=0)
    return jnp.where(row >= rows - s, fill, pltpu.roll(v, rows - s, 0))


def _doubling(a, b, period, reverse):
    rows = a.shape[0]
    pos = lax.broadcasted_iota(jnp.int32, a.shape, 0) & (period - 1)
    k = 1
    while k < period:
        inside = (pos < period - k) if reverse else (pos >= k)
        shift = rows - k if reverse else k
        a_s = jnp.where(inside, pltpu.roll(a, shift, 0), 1.0)
        b_s = jnp.where(inside, pltpu.roll(b, shift, 0), 0.0)
        b = a * b_s + b
        a = a * a_s
        k *= 2
    return a, b


def _scan(a, b, boundary, reverse, a_scr, b_scr, spread):
    rows = a.shape[0]
    ntile = rows // SUBLANES
    a_scr[...], b_scr[...] = _doubling(a, b, SUBLANES, reverse)
    ends = pl.ds(0 if reverse else SUBLANES - 1, ntile, stride=SUBLANES)
    a_end, x_end = _doubling(a_scr[ends, :], b_scr[ends, :], ntile, reverse)
    x_end = x_end + a_end * boundary
    tile = lax.broadcasted_iota(jnp.int32, x_end.shape, 0)
    if reverse:
        incoming = jnp.where(tile == ntile - 1, boundary, pltpu.roll(x_end, ntile - 1, 0))
        last = x_end[0:1, :]
    else:
        incoming = jnp.where(tile == 0, boundary, pltpu.roll(x_end, 1, 0))
        last = x_end[ntile - 1:ntile, :]
    for s in range(SUBLANES):
        spread[pl.ds(s, ntile, stride=SUBLANES), :] = incoming
    return b_scr[...] + a_scr[...] * spread[...], last


def _conv_taps(xr, head):
    return [_shift_down(xr, 3, head), _shift_down(xr, 2, head), _shift_down(xr, 1, head), xr]


def _rnn_gates(xc, wa, ba, wx, bx, lam, keep):
    xcb = xc.astype(BF16)
    r = _sigmoid(_dot(xcb, wa.astype(BF16)) + ba)
    i = _sigmoid(_dot(xcb, wx.astype(BF16)) + bx)
    softplus = jnp.maximum(-lam, 0.0) + jnp.log(1.0 + jnp.exp(-jnp.abs(lam)))
    cl = -LRU_C * softplus
    log_a = cl * r
    a_raw = jnp.exp(log_a)
    mult_raw = jnp.sqrt(-_expm1_nonpos(2.0 * log_a, a_raw * a_raw))
    live = keep > 0.0
    return r, i, cl, a_raw, mult_raw, jnp.where(live, a_raw, 0.0), jnp.where(live, mult_raw, 1.0), live


def _rnn_specs(seq, rows, time_of):
    per = rows // SUBLANES
    vec = pl.BlockSpec((None, 1, 128), lambda hb, n: (hb, 0, 0))
    mat = pl.BlockSpec((None, 128, 128), lambda hb, n: (hb, 0, 0))
    return [pl.BlockSpec((rows, 128), lambda hb, n: (time_of(n), hb)),
            pl.BlockSpec((SUBLANES, 128), lambda hb, n: (jnp.maximum(time_of(n) * per - 1, 0), hb)),
            pl.BlockSpec((rows, 1), lambda hb, n: (time_of(n), 0)),
            pl.BlockSpec((None, SUBLANES, 128), lambda hb, n: (hb, 0, 0)),
            vec, mat, vec, mat, vec, vec]


def _rnn_fwd(pf, keep, conv_w8, conv_b, w_a, b_a, w_x, b_x, lam):
    seq = pf.shape[0]
    rows = RNN_ROWS

    def body(x_ref, xh_ref, keep_ref, cw_ref, cb_ref, wa_ref, ba_ref, wx_ref, bx_ref, lam_ref, hr_ref,
             carry, a_scr, b_scr, spread):
        n = pl.program_id(1)

        @pl.when(n == 0)
        def _():
            carry[...] = jnp.zeros_like(carry)

        xr = x_ref[...]
        head = jnp.where(n > 0, xh_ref[...], 0.0)
        taps = _conv_taps(xr, head)
        xc = cb_ref[...] + sum(cw_ref[k:k + 1, :] * taps[k] for k in range(4))
        _, i, _, _, _, a, mult, _ = _rnn_gates(xc, wa_ref[...], ba_ref[...], wx_ref[...], bx_ref[...],
                                               lam_ref[...], keep_ref[...])
        h, last = _scan(a, mult * i * xc, carry[0:1, :], False, a_scr, b_scr, spread)
        hr_ref[...] = h
        carry[...] = jnp.broadcast_to(last, carry.shape)

    chunk_f32 = pltpu.VMEM((rows, 128), F32)
    return pl.pallas_call(
        body, name="rnn_fwd",
        out_shape=jax.ShapeDtypeStruct((seq, D_MODEL), F32),
        grid=(RNN_BLOCKS, seq // rows),
        in_specs=_rnn_specs(seq, rows, lambda n: n),
        out_specs=pl.BlockSpec((rows, 128), lambda hb, n: (n, hb)),
        scratch_shapes=[pltpu.VMEM((SUBLANES, 128), F32), chunk_f32, chunk_f32, chunk_f32],
        compiler_params=_params(("arbitrary", "arbitrary"), VMEM_LIMIT),
    )(pf, pf, keep, conv_w8, conv_b, w_a, b_a, w_x, b_x, lam)


def _rnn_bwd(pf, hr, dhr, keep, conv_w8, conv_b, w_a, b_a, w_x, b_x, lam):
    seq = pf.shape[0]
    rows = RNN_ROWS
    nchunk = seq // rows
    per = rows // SUBLANES
    time_of = lambda n: nchunk - 1 - n

    def body(x_ref, xh_ref, keep_ref, cw_ref, cb_ref, wa_ref, ba_ref, wx_ref, bx_ref, lam_ref,
             hr_ref, hrh_ref, dhr_ref,
             dx_ref, gcw_ref, gcb_ref, gwa_ref, gba_ref, gwx_ref, gbx_ref, glam_ref,
             g_carry, dxc_tail, a_scr, b_scr, spread):
        n = pl.program_id(1)
        first_in_time = n == nchunk - 1

        @pl.when(n == 0)
        def _():
            g_carry[...] = jnp.zeros_like(g_carry)
            dxc_tail[...] = jnp.zeros_like(dxc_tail)
            for ref in (gcw_ref, gcb_ref, gwa_ref, gba_ref, gwx_ref, gbx_ref, glam_ref):
                ref[...] = jnp.zeros_like(ref)

        xr = x_ref[...]
        head = jnp.where(first_in_time, 0.0, xh_ref[...])
        taps = _conv_taps(xr, head)
        cw = cw_ref[...]
        xc = cb_ref[...] + sum(cw[k:k + 1, :] * taps[k] for k in range(4))
        wa, wx, lam = wa_ref[...], wx_ref[...], lam_ref[...]
        r, i, cl, a_raw, mult_raw, a, mult, live = _rnn_gates(xc, wa, ba_ref[...], wx, bx_ref[...], lam,
                                                               keep_ref[...])
        h_prev = _shift_down(hr_ref[...], 1, jnp.where(first_in_time, 0.0, hrh_ref[...]))

        row = lax.broadcasted_iota(jnp.int32, xr.shape, 0)
        last = row == rows - 1
        a_next = jnp.where(last, 0.0, pltpu.roll(a, rows - 1, 0))
        g, g_first = _scan(a_next, dhr_ref[...] + jnp.where(last, g_carry[0:1, :], 0.0),
                           jnp.zeros((1, 128), F32), True, a_scr, b_scr, spread)
        g_carry[...] = jnp.broadcast_to(a[0:1, :] * g_first, g_carry.shape)

        da = g * h_prev
        dmult = g * i * xc
        di = g * mult * xc
        dxc = g * mult * i
        dlog_a = jnp.where(live, da * a_raw - dmult * a_raw * a_raw / mult_raw, 0.0)
        dpa = (dlog_a * cl) * r * (1.0 - r)
        dpx = di * i * (1.0 - i)
        glam_ref[...] += jnp.sum(dlog_a * r, axis=0, keepdims=True) * (LRU_C * _sigmoid(-lam))
        xcb, dpab, dpxb = xc.astype(BF16), dpa.astype(BF16), dpx.astype(BF16)
        gwa_ref[...] += _dot_tn(xcb, dpab)
        gwx_ref[...] += _dot_tn(xcb, dpxb)
        gba_ref[...] += jnp.sum(dpa, axis=0, keepdims=True)
        gbx_ref[...] += jnp.sum(dpx, axis=0, keepdims=True)
        dxc = dxc + _dot_nt(dpab, wa.astype(BF16)) + _dot_nt(dpxb, wx.astype(BF16))

        gcb_ref[...] += jnp.sum(dxc, axis=0, keepdims=True)
        for k in range(4):
            gcw_ref[k:k + 1, :] += jnp.sum(dxc * taps[k], axis=0, keepdims=True)
        tail = dxc_tail[...]
        dx = cw[3:4, :] * dxc
        for k in range(3):
            dx = dx + cw[k:k + 1, :] * _shift_up(dxc, 3 - k, tail)
        dx_ref[...] = dx.astype(BF16)
        dxc_tail[...] = dxc[0:SUBLANES, :]

    blk = lambda hb, n: (hb, 0, 0)
    chunk = pl.BlockSpec((rows, 128), lambda hb, n: (time_of(n), hb))
    vec_out = pl.BlockSpec((None, 1, 128), blk)
    mat_out = pl.BlockSpec((None, 128, 128), blk)
    vec_shape = jax.ShapeDtypeStruct((RNN_BLOCKS, 1, 128), F32)
    mat_shape = jax.ShapeDtypeStruct((RNN_BLOCKS, 128, 128), F32)
    return pl.pallas_call(
        body, name="rnn_bwd",
        out_shape=[jax.ShapeDtypeStruct((seq, D_MODEL), BF16),
                   jax.ShapeDtypeStruct((RNN_BLOCKS, SUBLANES, 128), F32), vec_shape,
                   mat_shape, vec_shape, mat_shape, vec_shape, vec_shape],
        grid=(RNN_BLOCKS, nchunk),
        in_specs=_rnn_specs(seq, rows, time_of) + [
            chunk, pl.BlockSpec((SUBLANES, 128), lambda hb, n: (jnp.maximum(time_of(n) * per - 1, 0), hb)), chunk],
        out_specs=[chunk, pl.BlockSpec((None, SUBLANES, 128), blk), vec_out,
                   mat_out, vec_out, mat_out, vec_out, vec_out],
        scratch_shapes=[pltpu.VMEM((SUBLANES, 128), F32), pltpu.VMEM((SUBLANES, 128), F32)]
                       + [pltpu.VMEM((rows, 128), F32)] * 3,
        compiler_params=_params(("arbitrary", "arbitrary"), VMEM_LIMIT),
    )(pf, pf, keep, conv_w8, conv_b, w_a, b_a, w_x, b_x, lam, hr, hr, dhr)


def _unit_rows(dil, r, j):
    start = j * KEY_BLOCK * dil + r
    return pl.ds(start, KEY_BLOCK) if dil == 1 else pl.ds(start, KEY_BLOCK, stride=dil)


def _attn_fwd(q, k, v):
    nh, seq, _ = q.shape
    nchunk = seq // SPAN
    nblk = SPAN // KEY_BLOCK

    def body(q_ref, k_ref, v_ref, kp_ref, vp_ref, o_ref, l1_ref, l4_ref, l16_ref, acc, m_s, l_s):
        n = pl.program_id(1)
        qi = lax.broadcasted_iota(jnp.int32, (KEY_BLOCK, KEY_BLOCK), 0)
        ki = lax.broadcasted_iota(jnp.int32, (KEY_BLOCK, KEY_BLOCK), 1)
        bias_own = jnp.where(ki <= qi, 0.0, NEG_INF)
        bias_before = jnp.where(ki >= qi, 0.0, NEG_INF)
        bias_mid = jnp.concatenate([bias_before, bias_own], axis=1)
        bias_first = jnp.concatenate([jnp.where(n > 0, bias_before, NEG_INF), bias_own], axis=1)
        ones = jnp.ones((2 * KEY_BLOCK, HEAD_DIM), BF16)
        for gi, dil in enumerate(DILATIONS):
            nb = nblk // dil
            for r in range(dil):
                for j in range(nb):
                    rows = _unit_rows(dil, r, j)
                    if j == 0:
                        prow = _unit_rows(dil, r, nb - 1)
                        kp, vp, bias = kp_ref[prow, :], vp_ref[prow, :], bias_first
                    else:
                        prow = _unit_rows(dil, r, j - 1)
                        kp, vp, bias = k_ref[prow, :], v_ref[prow, :], bias_mid
                    qb = q_ref[rows, :].astype(BF16)
                    kcat = jnp.concatenate([kp, k_ref[rows, :]], axis=0).astype(BF16)
                    vcat = jnp.concatenate([vp, v_ref[rows, :]], axis=0).astype(BF16)
                    vaug = jnp.concatenate([vcat, ones], axis=1)
                    s = _dot_nt(qb, kcat) + bias
                    mx = jnp.max(s, axis=-1, keepdims=True)
                    if gi == 0:
                        m_new = jnp.broadcast_to(mx, (KEY_BLOCK, HEAD_DIM))
                    else:
                        m_old = m_s[rows, :]
                        m_new = jnp.maximum(m_old, mx)
                    p = jnp.exp(s - jnp.concatenate([m_new, m_new], axis=1))
                    pv = _dot(p.astype(BF16), vaug)
                    if gi == 0:
                        acc[rows, :] = pv[:, :HEAD_DIM]
                        l_s[rows, :] = pv[:, HEAD_DIM:]
                    else:
                        alpha = jnp.exp(m_old - m_new)
                        acc[rows, :] = alpha * acc[rows, :] + pv[:, :HEAD_DIM]
                        l_s[rows, :] = alpha * l_s[rows, :] + pv[:, HEAD_DIM:]
                    m_s[rows, :] = m_new
        den = l_s[...]
        o_ref[...] = acc[...] * (1.0 / den)
        m_s[...] = m_s[...] + jnp.log(den)
        diag = qi == ki
        for dil, out in zip(DILATIONS, (l1_ref, l4_ref, l16_ref)):
            nb = nblk // dil
            for r in range(dil):
                for j in range(nb):
                    blk = m_s[_unit_rows(dil, r, j), :]
                    out[r * nb + j:r * nb + j + 1, :] = jnp.sum(jnp.where(diag, blk, 0.0), axis=0, keepdims=True)

    blk = pl.BlockSpec((None, SPAN, HEAD_DIM), lambda h, n: (h, n, 0))
    pblk = pl.BlockSpec((None, SPAN, HEAD_DIM), lambda h, n: (h, jnp.maximum(n - 1, 0), 0))
    lblk = pl.BlockSpec((None, nblk, KEY_BLOCK), lambda h, n: (h, n, 0))
    lshape = jax.ShapeDtypeStruct((nh, seq // KEY_BLOCK, KEY_BLOCK), F32)
    span_f32 = pltpu.VMEM((SPAN, HEAD_DIM), F32)
    o, l1, l4, l16 = pl.pallas_call(
        body, name="attn_fwd",
        out_shape=[jax.ShapeDtypeStruct((nh, seq, HEAD_DIM), F32), lshape, lshape, lshape],
        grid=(nh, nchunk), in_specs=[blk, blk, blk, pblk, pblk], out_specs=[blk, lblk, lblk, lblk],
        scratch_shapes=[span_f32, span_f32, span_f32],
        compiler_params=_params(("arbitrary", "arbitrary"), VMEM_LIMIT),
    )(q, k, v, k, v)
    return o, (l1, l4, l16)


def _to_residue_major(src, tmp, dst):
    quarter = SPAN // 4
    for r4 in range(4):
        tmp[r4 * quarter:(r4 + 1) * quarter, :] = src[pl.ds(r4, quarter, stride=4), :]
    for r4 in range(4):
        for rp in range(4):
            r = r4 + 4 * rp
            dst[r * KEY_BLOCK:(r + 1) * KEY_BLOCK, :] = tmp[pl.ds(r4 * quarter + rp, KEY_BLOCK, stride=4), :]


def _add_from_residue_major(src, tmp, acc):
    quarter = SPAN // 4
    for r4 in range(4):
        for rp in range(4):
            r = r4 + 4 * rp
            tmp[pl.ds(r4 * quarter + rp, KEY_BLOCK, stride=4), :] = src[r * KEY_BLOCK:(r + 1) * KEY_BLOCK, :]
    for r4 in range(4):
        acc[pl.ds(r4, quarter, stride=4), :] += tmp[r4 * quarter:(r4 + 1) * quarter, :]


def _attn_bwd(q, k, v, do, o, lses, cosf, sinf):
    nh, seq, _ = q.shape
    nchunk = seq // SPAN
    nblk = SPAN // KEY_BLOCK
    wide = DILATIONS[-1]
    assert SPAN == wide * KEY_BLOCK

    def body(q_ref, k_ref, v_ref, do_ref, o_ref, kp_ref, vp_ref, l1_ref, l4_ref, l16_ref,
             cos_ref, sin_ref, cosp_ref, sinp_ref, dq_ref, dk_ref, dv_ref,
             dq_acc, dkc_acc, dvc_acc, dkp_acc, dvp_acc, q16, k16, v16, do16, o16, k16p, v16p,
             dq16, dkc16, dvc16, dkp16, dvp16, tmp, pt_s, ds_s, kcat_s, qb_s, dob_s):
        n = pl.program_id(1)
        ki = lax.broadcasted_iota(jnp.int32, (KEY_BLOCK, KEY_BLOCK), 0)
        qi = lax.broadcasted_iota(jnp.int32, (KEY_BLOCK, KEY_BLOCK), 1)
        bias_own = jnp.where(ki <= qi, 0.0, NEG_INF)
        bias_before = jnp.where(ki >= qi, 0.0, NEG_INF)
        bias_mid = jnp.concatenate([bias_before, bias_own], axis=0)
        bias_first = jnp.concatenate([jnp.where(n > 0, bias_before, NEG_INF), bias_own], axis=0)
        ones8 = jnp.ones((SUBLANES, HEAD_DIM), BF16)

        def row_dot(a, b):
            prod = a * b
            hi = prod.astype(BF16)
            lo = (prod - hi.astype(F32)).astype(BF16)
            return (_dot_nt(ones8, hi) + _dot_nt(ones8, lo))[0:1, :]

        def group(units, srcs, before, l_ref, accs):
            src_q, src_do, src_o, src_k, src_v = srcs
            before_k, before_v = before
            acc_q, acc_kc, acc_vc, acc_kp, acc_vp = accs
            for u, (rows, prow, outside, lrow, _) in enumerate(units):
                dof = src_do[rows, :]
                qb, dob = src_q[rows, :].astype(BF16), dof.astype(BF16)
                kp, vp = (before_k[prow, :], before_v[prow, :]) if outside else (src_k[prow, :], src_v[prow, :])
                kcat = jnp.concatenate([kp, src_k[rows, :]], axis=0).astype(BF16)
                vcat = jnp.concatenate([vp, src_v[rows, :]], axis=0).astype(BF16)
                bias = bias_first if outside else bias_mid
                pt = jnp.exp(_dot_nt(kcat, qb) + bias - l_ref[lrow:lrow + 1, :])
                dst = pt * (_dot_nt(vcat, dob) - row_dot(dof, src_o[rows, :]))
                pt_s[u], ds_s[u], kcat_s[u], qb_s[u], dob_s[u] = pt.astype(BF16), dst.astype(BF16), kcat, qb, dob
            for u, (rows, _, _, _, _) in enumerate(units):
                acc_q[rows, :] += _dot_tn(ds_s[u], kcat_s[u])
            for u, (rows, prow, outside, _, nxt) in enumerate(units):
                dk = _dot(ds_s[u, KEY_BLOCK:, :], qb_s[u])
                dv = _dot(pt_s[u, KEY_BLOCK:, :], dob_s[u])
                if nxt is not None:
                    dk = dk + _dot(ds_s[nxt, :KEY_BLOCK, :], qb_s[nxt])
                    dv = dv + _dot(pt_s[nxt, :KEY_BLOCK, :], dob_s[nxt])
                acc_kc[rows, :] += dk
                acc_vc[rows, :] += dv
                if outside:
                    acc_kp[prow, :] += _dot(ds_s[u, :KEY_BLOCK, :], qb_s[u])
                    acc_vp[prow, :] += _dot(pt_s[u, :KEY_BLOCK, :], dob_s[u])

        @pl.when(n == 0)
        def _():
            for ref in (dkp_acc, dvp_acc, dkp16, dvp16, k16p, v16p):
                ref[...] = jnp.zeros_like(ref)

        @pl.when(n < nchunk)
        def _():
            for ref in (dq_acc, dkc_acc, dvc_acc, dq16, dkc16, dvc16):
                ref[...] = jnp.zeros_like(ref)
            for src, dst in ((q_ref, q16), (k_ref, k16), (v_ref, v16), (do_ref, do16), (o_ref, o16)):
                _to_residue_major(src, tmp, dst)
            natural = (q_ref, do_ref, o_ref, k_ref, v_ref)
            for dil, l_ref in zip(DILATIONS[:-1], (l1_ref, l4_ref)):
                nb = nblk // dil
                units = [(_unit_rows(dil, r, j), _unit_rows(dil, r, (j - 1) % nb), j == 0, r * nb + j,
                          r * nb + j + 1 if j + 1 < nb else None) for r in range(dil) for j in range(nb)]
                group(units, natural, (kp_ref, vp_ref), l_ref, (dq_acc, dkc_acc, dvc_acc, dkp_acc, dvp_acc))
            blocks = [pl.ds(r * KEY_BLOCK, KEY_BLOCK) for r in range(wide)]
            group([(rows, rows, True, r, None) for r, rows in enumerate(blocks)], (q16, do16, o16, k16, v16),
                  (k16p, v16p), l16_ref, (dq16, dkc16, dvc16, dkp16, dvp16))
            _add_from_residue_major(dq16, tmp, dq_acc)
            dq = dq_acc[...]
            dq_ref[...] = ((dq * cos_ref[...] - _rope_partner(dq) * sin_ref[...]) * ATTN_SCALE).astype(BF16)

        @pl.when(n > 0)
        def _():
            _add_from_residue_major(dkp16, tmp, dkp_acc)
            _add_from_residue_major(dvp16, tmp, dvp_acc)
            dk = dkp_acc[...]
            dk_ref[...] = (dk * cosp_ref[...] - _rope_partner(dk) * sinp_ref[...]).astype(BF16)
            dv_ref[...] = dvp_acc[...].astype(BF16)

        @pl.when(n < nchunk)
        def _():
            for src, dst in ((dkc_acc, dkp_acc), (dvc_acc, dvp_acc), (dkc16, dkp16), (dvc16, dvp16),
                             (k16, k16p), (v16, v16p)):
                dst[...] = src[...]

    last = nchunk - 1
    cur = lambda h, n: (h, jnp.minimum(n, last), 0)
    prev = lambda h, n: (h, jnp.clip(n - 1, 0, last), 0)
    blk = lambda idx: pl.BlockSpec((None, SPAN, HEAD_DIM), idx)
    lblk = pl.BlockSpec((None, nblk, KEY_BLOCK), cur)
    tab = pl.BlockSpec((SPAN, HEAD_DIM), lambda h, n: (jnp.minimum(n, last), 0))
    tabp = pl.BlockSpec((SPAN, HEAD_DIM), lambda h, n: (jnp.clip(n - 1, 0, last), 0))
    out_q = pl.BlockSpec((SPAN, HEAD_DIM), lambda h, n: (jnp.minimum(n, last), h))
    out_kv = pl.BlockSpec((SPAN, HEAD_DIM), lambda h, n: (jnp.clip(n - 1, 0, last), h))
    shape = jax.ShapeDtypeStruct((seq, nh * HEAD_DIM), BF16)
    return pl.pallas_call(
        body, name="attn_bwd", out_shape=[shape, shape, shape], grid=(nh, nchunk + 1),
        in_specs=[blk(cur)] * 5 + [blk(prev)] * 2 + [lblk] * 3 + [tab, tab, tabp, tabp],
        out_specs=[out_q, out_kv, out_kv],
        scratch_shapes=[pltpu.VMEM((SPAN, HEAD_DIM), F32)] * 18
                       + [pltpu.VMEM((nblk, 2 * KEY_BLOCK, HEAD_DIM), BF16)] * 3
                       + [pltpu.VMEM((nblk, KEY_BLOCK, HEAD_DIM), BF16)] * 2,
        compiler_params=_params(("arbitrary", "arbitrary"), VMEM_LIMIT),
    )(q, k, v, do, o, k, v, *lses, cosf, sinf, cosf, sinf)


def _hub(x, tgt, hr, pf, o_hm, mod, b_mod, b_gate, g_final, w_out_rnn, w_out_attn, w_o):
    seq = x.shape[0]
    tm = HUB_ROWS
    nsteps = seq // tm

    def body(x_ref, t_ref, hr_ref, zr_ref, za_ref, gr_ref, ga_ref, o_ref, mod_ref, bmod_ref, bg_ref, gf_ref,
             wr_hbm, wa_hbm, wo_hbm,
             dx2_ref, dhr_ref, dzr_ref, do_ref, dza_ref, dgr_ref, dga_ref,
             ur_ref, dyr_ref, ua_ref, dya_ref, mg_ref, dmo_ref,
             ggf_ref, gbg_ref, dgate_ref, loss_ref,
             wr, wa, wo, sem):
        step = pl.program_id(0)

        @pl.when(step == 0)
        def _():
            for src, dst in ((wr_hbm, wr), (wa_hbm, wa), (wo_hbm, wo)):
                cp = pltpu.make_async_copy(src, dst, sem)
                cp.start()
                cp.wait()
            for ref in (ggf_ref, gbg_ref, dgate_ref, loss_ref):
                ref[...] = jnp.zeros_like(ref)

        gate = mod_ref[:, 2 * D_MODEL:] + bmod_ref[:, 2 * D_MODEL:]
        gfin = gf_ref[...]
        hr_t, zr, za = hr_ref[...], zr_ref[...], za_ref[...]
        o = jnp.concatenate([o_ref[hh] for hh in range(N_HEADS)], axis=1)
        sig_zr, sig_za = _sigmoid(zr), _sigmoid(za)
        silu_zr, silu_za = zr * sig_zr, za * sig_za
        u_rnn = (hr_t * silu_zr).astype(BF16)
        u_attn = (o * silu_za).astype(BF16)
        y_rnn = _dot(u_rnn, wr[...])
        y_attn = _dot(u_attn, wa[...])
        sr = _sigmoid(gr_ref[...] + bg_ref[:, :D_MODEL])
        sa = _sigmoid(ga_ref[...] + bg_ref[:, D_MODEL:])
        merged = (sr * y_rnn + sa * y_attn).astype(BF16)
        mo = _dot(merged, wo[...])
        x2 = x_ref[...] + gate * mo
        rstd = lax.rsqrt(jnp.mean(x2 * x2, axis=-1, keepdims=True) + NORM_EPS)
        xn = x2 * rstd
        err = xn * gfin - t_ref[...]
        loss_ref[...] += 0.5 * jnp.sum(jnp.sum(err * err, axis=-1, keepdims=True) * (1.0 / D_MODEL),
                                       axis=0, keepdims=True)

        dy = err * (1.0 / D_MODEL)
        ggf_ref[...] += jnp.sum(dy * xn, axis=0, keepdims=True)
        dxn = dy * gfin
        dx2 = rstd * (dxn - xn * jnp.mean(dxn * xn, axis=-1, keepdims=True))
        dx2_ref[...] = dx2
        dgate_ref[...] += jnp.sum(dx2 * mo, axis=0, keepdims=True)
        dmo = (dx2 * gate).astype(BF16)
        dmerged = _dot_nt(dmo, wo[...])
        mg_ref[...] = merged
        dmo_ref[...] = dmo
        dy_rnn = (dmerged * sr).astype(BF16)
        dy_attn = (dmerged * sa).astype(BF16)
        dg_r = dmerged * y_rnn * sr * (1.0 - sr)
        dg_a = dmerged * y_attn * sa * (1.0 - sa)
        dgr_ref[...] = dg_r.astype(BF16)
        dga_ref[...] = dg_a.astype(BF16)
        gbg_ref[:, :D_MODEL] += jnp.sum(dg_r, axis=0, keepdims=True)
        gbg_ref[:, D_MODEL:] += jnp.sum(dg_a, axis=0, keepdims=True)
        du_rnn = _dot_nt(dy_rnn, wr[...])
        du_attn = _dot_nt(dy_attn, wa[...])
        ur_ref[...] = u_rnn
        dyr_ref[...] = dy_rnn
        ua_ref[...] = u_attn
        dya_ref[...] = dy_attn
        dhr_ref[...] = du_rnn * silu_zr
        dzr_ref[...] = (du_rnn * hr_t * (sig_zr * (1.0 + zr * (1.0 - sig_zr)))).astype(BF16)
        dza_ref[...] = (du_attn * o * (sig_za * (1.0 + za * (1.0 - sig_za)))).astype(BF16)
        d_o = du_attn * silu_za
        for hh in range(N_HEADS):
            do_ref[hh] = d_o[:, hh * HEAD_DIM:(hh + 1) * HEAD_DIM]

    row = pl.BlockSpec((tm, D_MODEL), lambda i: (i, 0))
    piece = lambda slot: pl.BlockSpec((tm, D_MODEL), lambda i: (i, slot))
    hm = pl.BlockSpec((N_HEADS, tm, HEAD_DIM), lambda i: (0, i, 0))
    const = lambda cols: pl.BlockSpec((1, cols), lambda i: (0, 0))
    any_spec = pl.BlockSpec(memory_space=pl.ANY)
    act_f32 = jax.ShapeDtypeStruct((seq, D_MODEL), F32)
    act_bf16 = jax.ShapeDtypeStruct((seq, D_MODEL), BF16)
    return pl.pallas_call(
        body, name="hub",
        out_shape=[act_f32, act_f32, act_bf16, jax.ShapeDtypeStruct((N_HEADS, seq, HEAD_DIM), F32),
                   act_bf16, act_bf16, act_bf16] + [act_bf16] * 6 + [
                   jax.ShapeDtypeStruct((1, D_MODEL), F32), jax.ShapeDtypeStruct((1, 2 * D_MODEL), F32),
                   jax.ShapeDtypeStruct((1, D_MODEL), F32), jax.ShapeDtypeStruct((1, 1), F32)],
        grid=(nsteps,),
        in_specs=[row, row, row, piece(1), piece(2), piece(3), piece(4), hm,
                  const(3 * D_MODEL), const(3 * D_MODEL), const(2 * D_MODEL), const(D_MODEL),
                  any_spec, any_spec, any_spec],
        out_specs=[row, row, row, hm, row, row, row] + [row] * 6 + [
                   const(D_MODEL), const(2 * D_MODEL), const(D_MODEL), const(1)],
        scratch_shapes=[pltpu.VMEM((D_MODEL, D_MODEL), BF16)] * 3 + [pltpu.SemaphoreType.DMA],
        compiler_params=_params(("arbitrary",), VMEM_LIMIT),
    )(x, tgt, hr, pf, pf, pf, pf, o_hm, mod, b_mod, b_gate, g_final, w_out_rnn, w_out_attn, w_o)


def _pair_grads(name, lefts, rights):
    n = len(rights)
    shared = len(lefts) == 1
    seq = rights[0].shape[0]
    tk = WGRAD_ROWS
    nk = seq // tk

    def body(*refs):
        l_refs, r_refs, out_ref = refs[:len(lefts)], refs[len(lefts):len(lefts) + n], refs[len(lefts) + n]
        j, kk = pl.program_id(0), pl.program_id(1)

        @pl.when(kk == 0)
        def _():
            out_ref[...] = jnp.zeros_like(out_ref)

        for m in range(n):
            @pl.when(j == m)
            def _(m=m):
                out_ref[...] += _dot_tn(l_refs[0 if shared else m][...], r_refs[m][...])

    def spec(m):
        return pl.BlockSpec((tk, D_MODEL), lambda j, kk: (jnp.where(j == m, kk, jnp.where(j < m, 0, nk - 1)), 0))

    left_specs = [pl.BlockSpec((tk, D_MODEL), lambda j, kk: (kk, 0))] if shared else [spec(m) for m in range(n)]
    return pl.pallas_call(
        body, name=name,
        out_shape=jax.ShapeDtypeStruct((n, D_MODEL, D_MODEL), F32),
        grid=(n, nk),
        in_specs=left_specs + [spec(m) for m in range(n)],
        out_specs=pl.BlockSpec((None, D_MODEL, D_MODEL), lambda j, kk: (j, 0, 0)),
        compiler_params=_params(("arbitrary", "arbitrary"), VMEM_LIMIT),
    )(*lefts, *rights)


def _dh_dx(pieces, w_in_all, x, dx2, mod, b_mod, g_norm):
    seq = x.shape[0]
    tm = DX_ROWS

    def body(*refs):
        p_refs = refs[:8]
        w_hbm, x_ref, dx2_ref, mod_ref, bmod_ref, g_ref = refs[8:14]
        gx_ref, dshift_ref, dscale_ref, ggn_ref, w_scr, sem = refs[14:]
        step = pl.program_id(0)

        @pl.when(step == 0)
        def _():
            cp = pltpu.make_async_copy(w_hbm, w_scr, sem)
            cp.start()
            cp.wait()
            for ref in (dshift_ref, dscale_ref, ggn_ref):
                ref[...] = jnp.zeros_like(ref)

        dh = _dot_nt(p_refs[0][...], w_scr[0])
        for j in range(1, 8):
            dh = dh + _dot_nt(p_refs[j][...], w_scr[j])
        scale1 = 1.0 + mod_ref[:, D_MODEL:2 * D_MODEL] + bmod_ref[:, D_MODEL:2 * D_MODEL]
        g = g_ref[...]
        xf = x_ref[...]
        rstd_t = lax.rsqrt(jnp.mean(xf * xf, axis=-1, keepdims=True) + NORM_EPS)
        xn = xf * rstd_t
        dshift_ref[...] += jnp.sum(dh, axis=0, keepdims=True)
        dscale_ref[...] += jnp.sum(dh * (xn * g), axis=0, keepdims=True)
        ggn_ref[...] += jnp.sum(dh * scale1 * xn, axis=0, keepdims=True)
        dxn = dh * (g * scale1)
        gx_ref[...] = rstd_t * (dxn - xn * jnp.mean(dxn * xn, axis=-1, keepdims=True)) + dx2_ref[...]

    row = pl.BlockSpec((tm, D_MODEL), lambda i: (i, 0))
    const = lambda cols: pl.BlockSpec((1, cols), lambda i: (0, 0))
    vec = jax.ShapeDtypeStruct((1, D_MODEL), F32)
    return pl.pallas_call(
        body, name="dh_dx",
        out_shape=[jax.ShapeDtypeStruct((seq, D_MODEL), F32), vec, vec, vec],
        grid=(seq // tm,),
        in_specs=[row] * 8 + [pl.BlockSpec(memory_space=pl.ANY), row, row,
                              const(3 * D_MODEL), const(3 * D_MODEL), const(D_MODEL)],
        out_specs=[row, const(D_MODEL), const(D_MODEL), const(D_MODEL)],
        scratch_shapes=[pltpu.VMEM((8, D_MODEL, D_MODEL), BF16), pltpu.SemaphoreType.DMA],
        compiler_params=_params(("arbitrary",), VMEM_LIMIT),
    )(*pieces, w_in_all, x, dx2, mod, b_mod, g_norm)


def _adamw(name, w, g, m, v):
    rows, cols = w.shape
    tr = rows if rows <= 256 else 256

    def body(w_ref, g_ref, m_ref, v_ref, d_ref, nm_ref, nv_ref):
        gv = g_ref[...]
        nm = ADAM_B1 * m_ref[...] + (1.0 - ADAM_B1) * gv
        nv = ADAM_B2 * v_ref[...] + (1.0 - ADAM_B2) * (gv * gv)
        m_hat = nm / (1.0 - ADAM_B1 ** ADAM_STEP)
        v_hat = nv / (1.0 - ADAM_B2 ** ADAM_STEP)
        d_ref[...] = -ADAM_LR * (m_hat / (jnp.sqrt(v_hat) + ADAM_EPS) + ADAM_WD * w_ref[...])
        nm_ref[...] = nm
        nv_ref[...] = nv

    spec = pl.BlockSpec((tr, cols), lambda i: (i, 0))
    shape = jax.ShapeDtypeStruct((rows, cols), F32)
    return pl.pallas_call(
        body, name=name, out_shape=[shape, shape, shape], grid=(rows // tr,),
        in_specs=[spec] * 4, out_specs=[spec] * 3,
        compiler_params=_params(("arbitrary",)),
    )(w, g, m, v)


def kernel(x, c, positions, g_norm, w_mod, b_mod, w_in, b_gate, conv_w, conv_b, w_a, b_a, w_x, b_x, lam, w_out_rnn, w_out_attn, w_o, g_final, loss_target, m_g_norm, m_w_mod, m_b_mod, m_w_in, m_b_gate, m_conv_w, m_conv_b, m_w_a, m_b_a, m_w_x, m_b_x, m_lam, m_w_out_rnn, m_w_out_attn, m_w_o, m_g_final, v_g_norm, v_w_mod, v_b_mod, v_w_in, v_b_gate, v_conv_w, v_conv_b, v_w_a, v_b_a, v_w_x, v_b_x, v_lam, v_w_out_rnn, v_w_out_attn, v_w_o, v_g_final):
    seq = x.shape[1]
    me = _index(_my_pos())
    xs, tgt = x[0], loss_target[0]

    pos = positions[0].astype(F32)[:, None]
    inv_freq = ROPE_THETA ** (-jnp.arange(0, 2 * ROT_HALF, 2, dtype=F32) / (2 * ROT_HALF))
    ang = pos * inv_freq
    rest = HEAD_DIM - 2 * ROT_HALF
    cosf = jnp.concatenate([jnp.cos(ang), jnp.cos(ang), jnp.ones((seq, rest), F32)], axis=1)
    sinf = jnp.concatenate([-jnp.sin(ang), jnp.sin(ang), jnp.zeros((seq, rest), F32)], axis=1)
    keep = (positions[0] != 0).astype(F32)[:, None]

    w_in_all, w_or_all, w_oa_all, w_o_all = _ag_big(
        "gather_weights", [w_in[0].astype(BF16), w_out_rnn[0].astype(BF16),
                           w_out_attn[0].astype(BF16), w_o[0].astype(BF16)])
    w_or_all, w_oa_all, w_o_all = (t.reshape(D_MODEL, D_MODEL) for t in (w_or_all, w_oa_all, w_o_all))
    conv_w8 = _ag_small("gather_conv_w", jnp.pad(conv_w[0], ((0, SUBLANES - 4), (0, 0))))
    c_all = _ag_small("gather_c", jnp.broadcast_to(c, (SUBLANES, D_MODEL)))[:, 0, :]
    mod_cols = w_mod.shape[2]
    mod_part = _ag_small("gather_mod", _mod_fwd(c_all, w_mod[0]))
    mod = lax.dynamic_index_in_dim(mod_part, me, axis=1, keepdims=False).reshape(1, N_DEV * mod_cols)

    blocks = lambda t: t.reshape(RNN_BLOCKS, 1, 128)
    rnn_params = (conv_w8, blocks(conv_b), w_a[0], blocks(b_a), w_x[0], blocks(b_x), blocks(lam))

    h = _norm(xs, mod, b_mod, g_norm)
    pf, q, k, v = _proj(h, w_in_all, cosf, sinf)
    hr = _rnn_fwd(pf, keep, *rnn_params)
    o, lses = _attn_fwd(q, k, v)

    (dx2, dhr, dz_rnn, d_o, dz_attn, dg_r, dg_a, u_rnn, dy_rnn, u_attn, dy_attn, merged, dmo,
     gp_g_final, gp_b_gate, dgate, loss_part) = _hub(
        xs, tgt, hr, pf, o, mod, b_mod, b_gate, g_final.reshape(1, D_MODEL), w_or_all, w_oa_all, w_o_all)
    gp_w_or, gp_w_oa, gp_w_o = _pair_grads("out_grads", [u_rnn, u_attn, merged], [dy_rnn, dy_attn, dmo])
    dq, dk, dv = _attn_bwd(q, k, v, d_o, o, lses, cosf, sinf)
    dx_rnn, gp_conv_w, gp_conv_b, gp_w_a, gp_b_a, gp_w_x, gp_b_x, gp_lam = _rnn_bwd(pf, hr, dhr, keep, *rnn_params)
    pieces = [dx_rnn, dz_rnn, dq, dk, dv, dz_attn, dg_r, dg_a]
    gp_w_in = _pair_grads("w_in_grad", [h], pieces)

    stacks = [gp_w_in, gp_w_or.reshape(N_DEV, 128, D_MODEL), gp_w_oa.reshape(N_DEV, 128, D_MODEL),
              gp_w_o.reshape(N_DEV, 128, D_MODEL)]
    from_sib = _rs_to_sibling("rs_sibling", stacks)
    targets = jnp.bitwise_xor(me, 2 * jnp.arange(4, dtype=jnp.int32)).astype(jnp.int32)
    sums = [_add_sibling("rs_add_sibling_%d" % a, s_, r_, targets) for a, (s_, r_) in enumerate(zip(stacks, from_sib))]
    send_sems, recv_sems, sent, landing, token = _rs_chips_start([send for _, send in sums])

    mod_after = mod + token[0:1, 0:1]
    grad_x, dshift, dscale, gp_g_norm = _dh_dx(pieces, w_in_all, xs, dx2, mod_after, b_mod, g_norm)

    dmod = jnp.concatenate([dshift, dscale, dgate], axis=1)
    dmod_all = _ag_small("gather_dmod", jnp.broadcast_to(dmod, (SUBLANES, 3 * D_MODEL)))[:, 0, :]
    dmod_cols = lax.dynamic_slice_in_dim(dmod_all, me * mod_cols, mod_cols, axis=1)
    g_b_mod, g_w_mod = _mod_bwd(c_all, dmod_all, dmod_cols)

    flat = lambda t: t.reshape(-1, 128)
    small = [flat(gp_g_norm), flat(gp_b_gate), flat(gp_conv_b), flat(gp_b_a), flat(gp_b_x), flat(gp_lam),
             flat(gp_g_final), flat(gp_conv_w), jnp.broadcast_to(loss_part, (SUBLANES, 128)),
             flat(gp_w_a), flat(gp_w_x)]
    sizes = [t.shape[0] for t in small]
    small.append(jnp.zeros((-sum(sizes) % (2 * SUBLANES), 128), F32))
    total = _allreduce_small("allreduce_small_grads", jnp.concatenate(small, axis=0))
    offs = [sum(sizes[:i]) for i in range(len(sizes))]
    (g_g_norm, g_b_gate, g_conv_b, g_b_a, g_b_x, g_lam, g_g_final, g_conv_w_all, loss_rows, g_w_a, g_w_x) = (
        total[o_:o_ + s_] for o_, s_ in zip(offs, sizes))
    loss = loss_rows[0, 0]
    g_conv_w = lax.dynamic_index_in_dim(g_conv_w_all.reshape(RNN_BLOCKS, SUBLANES, 128), me, axis=0,
                                        keepdims=False)[:4]

    from_chips = _rs_chips_wait(send_sems, recv_sems, sent, landing, total)
    g_w_in, g_w_or, g_w_oa, g_w_o = (
        _add_chips("rs_add_chips_%d" % a, own, r_) for a, ((own, _), r_) in enumerate(zip(sums, from_chips)))

    weights = [
        ("g_norm", g_norm, g_g_norm, m_g_norm, v_g_norm, (SUBLANES, 128)),
        ("w_mod", w_mod, g_w_mod, m_w_mod, v_w_mod, (D_MODEL, mod_cols)),
        ("b_mod", b_mod, g_b_mod, m_b_mod, v_b_mod, (3 * SUBLANES, 128)),
        ("w_in", w_in, g_w_in, m_w_in, v_w_in, (D_MODEL, D_MODEL)),
        ("b_gate", b_gate, g_b_gate, m_b_gate, v_b_gate, (2 * SUBLANES, 128)),
        ("conv_w", conv_w, g_conv_w, m_conv_w, v_conv_w, (4, 128)),
        ("conv_b", conv_b, g_conv_b, m_conv_b, v_conv_b, (SUBLANES, 128)),
        ("w_a", w_a, g_w_a, m_w_a, v_w_a, (RNN_BLOCKS * 128, 128)),
        ("b_a", b_a, g_b_a, m_b_a, v_b_a, (SUBLANES, 128)),
        ("w_x", w_x, g_w_x, m_w_x, v_w_x, (RNN_BLOCKS * 128, 128)),
        ("b_x", b_x, g_b_x, m_b_x, v_b_x, (SUBLANES, 128)),
        ("lam", lam, g_lam, m_lam, v_lam, (SUBLANES, 128)),
        ("w_out_rnn", w_out_rnn, g_w_or, m_w_out_rnn, v_w_out_rnn, (128, D_MODEL)),
        ("w_out_attn", w_out_attn, g_w_oa, m_w_out_attn, v_w_out_attn, (128, D_MODEL)),
        ("w_o", w_o, g_w_o, m_w_o, v_w_o, (128, D_MODEL)),
        ("g_final", g_final, g_g_final, m_g_final, v_g_final, (SUBLANES, 128)),
    ]
    out_g, out_d, out_m, out_v = [], [], [], []
    for name, w_, g_, m_, v_, shape2 in weights:
        d_, nm_, nv_ = _adamw("adamw_" + name, w_.reshape(shape2), g_.reshape(shape2), m_.reshape(shape2),
                              v_.reshape(shape2))
        out_g.append(g_.reshape(w_.shape))
        out_d.append(d_.reshape(w_.shape))
        out_m.append(nm_.reshape(w_.shape))
        out_v.append(nv_.reshape(w_.shape))
    return (loss, grad_x[None], *out_g, *out_d, *out_m, *out_v)
```

```python
import jax
import jax.numpy as jnp
from jax import lax
from jax.experimental import pallas as pl
from jax.experimental.pallas import tpu as pltpu

F32 = jnp.float32
BF16 = jnp.bfloat16
MESH = pl.DeviceIdType.MESH

D_MODEL = 1024
N_HEADS = 8
HEAD_DIM = 128
RNN_BLOCKS = 8
N_DEV = 8
ROT_HALF = 16
ROPE_THETA = 500000.0
DILATIONS = (1, 4, 16)
KEY_BLOCK = 128
SPAN = KEY_BLOCK * DILATIONS[-1]
ATTN_SCALE = HEAD_DIM ** -0.5
NORM_EPS = 1e-6
LRU_C = 8.0
NEG_INF = -1e30
ADAM_LR, ADAM_B1, ADAM_B2, ADAM_EPS, ADAM_WD, ADAM_STEP = 0.001, 0.9, 0.999, 1e-08, 0.01, 10

SUBLANES = 8
VMEM_LIMIT = 56 * 1024 * 1024
PROJ_ROWS = 1024
RNN_ROWS = 512
HUB_ROWS = 256
DX_ROWS = 256
WGRAD_ROWS = 1024
ADD_ROWS = 256


def _params(sem=None, vmem=None):
    return pltpu.CompilerParams(dimension_semantics=sem, vmem_limit_bytes=vmem)


def _dot(a, b):
    return jnp.dot(a, b, preferred_element_type=F32)


def _dot_nt(a, b):
    return lax.dot_general(a, b, (((1,), (1,)), ((), ())), preferred_element_type=F32)


def _dot_tn(a, b):
    return lax.dot_general(a, b, (((0,), (0,)), ((), ())), preferred_element_type=F32)


def _sigmoid(z):
    return 1.0 / (1.0 + jnp.exp(-z))


def _expm1_nonpos(z, exp_z):
    return jnp.where(z > -0.01, z * (1.0 + 0.5 * z), exp_z - 1.0)


def _my_pos():
    return lax.axis_index("x"), lax.axis_index("y"), lax.axis_index("c")


def _flip(pos, k):
    x, y, c = pos
    return ((1 - x) if k & 4 else x, (1 - y) if k & 2 else y, (1 - c) if k & 1 else c)


def _index(pos):
    return 4 * pos[0] + 2 * pos[1] + pos[2]


def _ag_small(name, v):
    rows, cols = v.shape

    def body(v_ref, out_ref, send_sems, recv_sems):
        me = _my_pos()
        out_ref[_index(me)] = v_ref[...]
        sends = []
        for k in range(1, N_DEV):
            cp = pltpu.make_async_remote_copy(
                src_ref=v_ref, dst_ref=out_ref.at[_index(me)], send_sem=send_sems.at[k - 1],
                recv_sem=recv_sems.at[k - 1], device_id=_flip(me, k), device_id_type=MESH)
            cp.start()
            sends.append(cp)
        for k in range(1, N_DEV):
            peer = _flip(me, k)
            pltpu.make_async_remote_copy(
                src_ref=v_ref, dst_ref=out_ref.at[_index(peer)], send_sem=send_sems.at[k - 1],
                recv_sem=recv_sems.at[k - 1], device_id=peer, device_id_type=MESH).wait_recv()
        for cp in sends:
            cp.wait_send()

    return pl.pallas_call(
        body, name=name,
        out_shape=jax.ShapeDtypeStruct((N_DEV, rows, cols), v.dtype),
        in_specs=[pl.BlockSpec(memory_space=pltpu.VMEM)],
        out_specs=pl.BlockSpec(memory_space=pltpu.VMEM),
        scratch_shapes=[pltpu.SemaphoreType.DMA((N_DEV - 1,)), pltpu.SemaphoreType.DMA((N_DEV - 1,))],
        compiler_params=_params(None, VMEM_LIMIT),
    )(v)


def _ag_big(name, shards):
    n = len(shards)

    def body(*refs):
        ins, outs = refs[:n], refs[n:2 * n]
        send_sems, recv_sems, local_sems = refs[2 * n:]
        me = _my_pos()
        sib = _flip(me, 1)
        chips = [2, 4, 6]

        def copy(a, k, block, to, src=None):
            rows = outs[a].at[_index(block)]
            return pltpu.make_async_remote_copy(
                src_ref=rows if src is None else src, dst_ref=rows,
                send_sem=send_sems.at[a * 7 + k], recv_sem=recv_sems.at[a * 7 + k],
                device_id=to, device_id_type=MESH)

        started = []
        for a in range(n):
            mine = pltpu.make_async_copy(ins[a], outs[a].at[_index(me)], local_sems.at[a])
            mine.start()
            started.append(mine)
        sends = []
        for a in range(n):
            first = [copy(a, 0, me, sib, src=ins[a])]
            first += [copy(a, 1 + j, me, _flip(me, ch), src=ins[a]) for j, ch in enumerate(chips)]
            for cp in first:
                cp.start()
            sends += first
        for j, ch in enumerate(chips):
            for a in range(n):
                copy(a, 1 + j, _flip(me, ch), me).wait_recv()
                fwd = copy(a, 4 + j, _flip(me, ch), sib)
                fwd.start()
                sends.append(fwd)
        for a in range(n):
            copy(a, 0, sib, me).wait_recv()
            for j, ch in enumerate(chips):
                copy(a, 4 + j, _flip(sib, ch), me).wait_recv()
        for cp in sends:
            cp.wait_send()
        for mine in started:
            mine.wait()

    any_spec = pl.BlockSpec(memory_space=pl.ANY)
    return pl.pallas_call(
        body, name=name,
        out_shape=[jax.ShapeDtypeStruct((N_DEV,) + s.shape, s.dtype) for s in shards],
        in_specs=[any_spec] * n, out_specs=[any_spec] * n,
        scratch_shapes=[pltpu.SemaphoreType.DMA((7 * n,)), pltpu.SemaphoreType.DMA((7 * n,)),
                        pltpu.SemaphoreType.DMA((n,))],
    )(*shards)


def _rs_to_sibling(name, stacks):
    n = len(stacks)

    def body(*refs):
        ins, outs = refs[:n], refs[n:2 * n]
        send_sems, recv_sems = refs[2 * n:]
        me = _my_pos()
        sib = _flip(me, 1)
        sends = []
        for a in range(n):
            for m in range(4):
                target = _flip(sib, 2 * m)
                cp = pltpu.make_async_remote_copy(
                    src_ref=ins[a].at[_index(target)], dst_ref=outs[a].at[m],
                    send_sem=send_sems.at[a * 4 + m], recv_sem=recv_sems.at[a * 4 + m],
                    device_id=sib, device_id_type=MESH)
                cp.start()
                sends.append(cp)
        for cp in sends:
            cp.wait_recv()
        for cp in sends:
            cp.wait_send()

    any_spec = pl.BlockSpec(memory_space=pl.ANY)
    return pl.pallas_call(
        body, name=name,
        out_shape=[jax.ShapeDtypeStruct((4,) + s.shape[1:], s.dtype) for s in stacks],
        in_specs=[any_spec] * n, out_specs=[any_spec] * n,
        scratch_shapes=[pltpu.SemaphoreType.DMA((4 * n,)), pltpu.SemaphoreType.DMA((4 * n,))],
    )(*stacks)


def _chip_copies(srcs, lands, send_sems, recv_sems):
    me = _my_pos()
    return [pltpu.make_async_remote_copy(
        src_ref=srcs[a].at[m - 1], dst_ref=lands[a].at[m - 1],
        send_sem=send_sems.at[a * 3 + m - 1], recv_sem=recv_sems.at[a * 3 + m - 1],
        device_id=_flip(me, 2 * m), device_id_type=MESH) for a in range(len(srcs)) for m in range(1, 4)]


def _rs_chips_start(sums):
    n = len(sums)

    def body(*refs):
        srcs, lands = refs[:n], refs[n:2 * n]
        send_sems, recv_sems = refs[2 * n:2 * n + 2]
        token = refs[-1]
        for cp in _chip_copies(srcs, lands, send_sems, recv_sems):
            cp.start()
        token[...] = jnp.zeros_like(token)

    hbm = pl.BlockSpec(memory_space=pltpu.HBM)
    sem = pl.BlockSpec(memory_space=pltpu.SEMAPHORE)
    held = [pltpu.HBM(s.shape, s.dtype) for s in sums]
    outs = pl.pallas_call(
        body, name="rs_chips_start",
        out_shape=(pltpu.SemaphoreType.DMA((3 * n,)), pltpu.SemaphoreType.DMA((3 * n,)), *held, *held,
                   jax.ShapeDtypeStruct((SUBLANES, 128), F32)),
        in_specs=[hbm] * (2 * n),
        out_specs=(sem, sem, *[hbm] * (2 * n), pl.BlockSpec(memory_space=pltpu.VMEM)),
        input_output_aliases={i: 2 + i for i in range(2 * n)},
        compiler_params=pltpu.CompilerParams(has_side_effects=pltpu.SideEffectType.DATAFLOW_SIDE_EFFECTING),
    )(*[pltpu.with_memory_space_constraint(s, pltpu.HBM) for s in sums],
      *[pltpu.with_memory_space_constraint(lax.empty(s.shape, s.dtype), pltpu.HBM) for s in sums])
    return outs[0], outs[1], outs[2:2 + n], outs[2 + n:2 + 2 * n], outs[-1]


def _rs_chips_wait(send_sems, recv_sems, srcs, lands, after):
    n = len(srcs)

    def body(*refs):
        src_refs, land_refs = refs[:n], refs[n:2 * n]
        sends, recvs = refs[2 * n:2 * n + 2]
        for cp in _chip_copies(src_refs, land_refs, sends, recvs):
            cp.wait_send()
            cp.wait_recv()

    hbm = pl.BlockSpec(memory_space=pltpu.HBM)
    sem = pl.BlockSpec(memory_space=pltpu.SEMAPHORE)
    held = [pltpu.HBM(s.shape, s.dtype) for s in srcs]
    outs = pl.pallas_call(
        body, name="rs_chips_wait", out_shape=(*held, *held),
        in_specs=[hbm] * (2 * n) + [sem, sem, pl.BlockSpec(memory_space=pl.ANY)],
        out_specs=[hbm] * (2 * n),
        input_output_aliases={i: i for i in range(2 * n)},
        compiler_params=pltpu.CompilerParams(has_side_effects=pltpu.SideEffectType.DATAFLOW_SIDE_EFFECTING),
    )(*srcs, *lands, send_sems, recv_sems, after)
    return outs[n:]


def _add_sibling(name, stack, recv, targets):
    _, rows, cols = stack.shape
    tr = min(rows, ADD_ROWS)

    def own_body(t_ref, a_ref, b_ref, o_ref):
        o_ref[...] = a_ref[...] + b_ref[...]

    own = pl.pallas_call(
        own_body, name=name + "_own",
        out_shape=jax.ShapeDtypeStruct((rows, cols), F32),
        grid_spec=pltpu.PrefetchScalarGridSpec(
            num_scalar_prefetch=1, grid=(rows // tr,),
            in_specs=[pl.BlockSpec((None, tr, cols), lambda i, t: (t[0], i, 0)),
                      pl.BlockSpec((None, tr, cols), lambda i, t: (0, i, 0))],
            out_specs=pl.BlockSpec((tr, cols), lambda i, t: (i, 0))),
        compiler_params=_params(("arbitrary",)),
    )(targets, stack, recv)

    def send_body(t_ref, a_ref, b_ref, o_ref):
        o_ref[...] = (a_ref[...] + b_ref[...]).astype(BF16)

    send = pl.pallas_call(
        send_body, name=name + "_send",
        out_shape=jax.ShapeDtypeStruct((3, rows, cols), BF16),
        grid_spec=pltpu.PrefetchScalarGridSpec(
            num_scalar_prefetch=1, grid=(3, rows // tr),
            in_specs=[pl.BlockSpec((None, tr, cols), lambda m, i, t: (t[m + 1], i, 0)),
                      pl.BlockSpec((None, tr, cols), lambda m, i, t: (m + 1, i, 0))],
            out_specs=pl.BlockSpec((None, tr, cols), lambda m, i, t: (m, i, 0))),
        compiler_params=_params(("arbitrary", "arbitrary")),
    )(targets, stack, recv)
    return own, send


def _add_chips(name, own, recv):
    rows, cols = own.shape
    tr = min(rows, ADD_ROWS)

    def body(a_ref, b_ref, o_ref):
        o_ref[...] = ((a_ref[...] + b_ref[0].astype(F32)) + b_ref[1].astype(F32)) + b_ref[2].astype(F32)

    return pl.pallas_call(
        body, name=name,
        out_shape=jax.ShapeDtypeStruct((rows, cols), F32),
        grid=(rows // tr,),
        in_specs=[pl.BlockSpec((tr, cols), lambda i: (i, 0)),
                  pl.BlockSpec((3, tr, cols), lambda i: (0, i, 0))],
        out_specs=pl.BlockSpec((tr, cols), lambda i: (i, 0)),
        compiler_params=_params(("arbitrary",)),
    )(own, recv)


def _allreduce_small(name, v):
    rows, cols = v.shape
    half = rows // 2
    assert rows % (2 * SUBLANES) == 0

    def body(v_ref, out_ref, from_sib, chip_half, from_chips, send_sems, recv_sems):
        me = _my_pos()
        sib = _flip(me, 1)
        mine = pl.ds(pl.multiple_of(me[2] * half, SUBLANES), half)
        theirs = pl.ds(pl.multiple_of((1 - me[2]) * half, SUBLANES), half)

        def copy(k, src, dst, to):
            return pltpu.make_async_remote_copy(src_ref=src, dst_ref=dst, send_sem=send_sems.at[k],
                                                recv_sem=recv_sems.at[k], device_id=to, device_id_type=MESH)

        to_sib = copy(0, v_ref.at[theirs], from_sib, sib)
        to_sib.start()
        to_sib.wait_recv()
        chip_half[...] = v_ref[mine, :] + from_sib[...]
        to_chips = [copy(m, chip_half, from_chips.at[m - 1], _flip(me, 2 * m)) for m in range(1, 4)]
        for cp in to_chips:
            cp.start()
        for cp in to_chips:
            cp.wait_recv()
        my_chip = 2 * me[0] + me[1]
        total = None
        for chip in range(4):
            slot = jnp.maximum(jnp.bitwise_xor(chip, my_chip) - 1, 0)
            part = jnp.where(chip == my_chip, chip_half[...], from_chips[slot])
            total = part if total is None else total + part
        out_ref[mine, :] = total
        swap = copy(4, out_ref.at[mine], out_ref.at[mine], sib)
        swap.start()
        copy(4, out_ref.at[theirs], out_ref.at[theirs], sib).wait_recv()
        for cp in [to_sib, swap] + to_chips:
            cp.wait_send()

    return pl.pallas_call(
        body, name=name, out_shape=jax.ShapeDtypeStruct((rows, cols), F32),
        in_specs=[pl.BlockSpec(memory_space=pltpu.VMEM)],
        out_specs=pl.BlockSpec(memory_space=pltpu.VMEM),
        scratch_shapes=[pltpu.VMEM((half, cols), F32), pltpu.VMEM((half, cols), F32),
                        pltpu.VMEM((3, half, cols), F32),
                        pltpu.SemaphoreType.DMA((5,)), pltpu.SemaphoreType.DMA((5,))],
        compiler_params=_params(None, VMEM_LIMIT),
    )(v)


def _mod_fwd(c_all, w_mod):
    def body(c_ref, w_ref, o_ref):
        c = c_ref[...]
        o_ref[...] = jnp.dot(c * _sigmoid(c), w_ref[...], preferred_element_type=F32,
                             precision=lax.Precision.HIGHEST)

    return pl.pallas_call(
        body, name="mod_fwd", out_shape=jax.ShapeDtypeStruct((N_DEV, w_mod.shape[1]), F32),
    )(c_all, w_mod)


def _mod_bwd(c_all, dmod_all, dmod_cols):
    def body(c_ref, da_ref, dc_ref, gb_ref, gw_ref):
        c = c_ref[...]
        acc = da_ref[0:1, :]
        for b in range(1, N_DEV):
            acc = acc + da_ref[b:b + 1, :]
        gb_ref[...] = acc
        gw_ref[...] = lax.dot_general(c * _sigmoid(c), dc_ref[...], (((0,), (0,)), ((), ())),
                                      preferred_element_type=F32, precision=lax.Precision.HIGHEST)

    return pl.pallas_call(
        body, name="mod_bwd",
        out_shape=[jax.ShapeDtypeStruct((1, dmod_all.shape[1]), F32),
                   jax.ShapeDtypeStruct((c_all.shape[1], dmod_cols.shape[1]), F32)],
    )(c_all, dmod_all, dmod_cols)


def _rope_partner(t):
    lane = lax.broadcasted_iota(jnp.int32, t.shape, 1)
    return jnp.where(lane < ROT_HALF, pltpu.roll(t, HEAD_DIM - ROT_HALF, 1), pltpu.roll(t, ROT_HALF, 1))


def _norm(x, mod, b_mod, g_norm):
    seq = x.shape[0]
    tm = PROJ_ROWS

    def body(x_ref, mod_ref, bmod_ref, g_ref, h_ref):
        xf = x_ref[...]
        rstd = lax.rsqrt(jnp.mean(xf * xf, axis=-1, keepdims=True) + NORM_EPS)
        shift = mod_ref[:, 0:D_MODEL] + bmod_ref[:, 0:D_MODEL]
        scale = mod_ref[:, D_MODEL:2 * D_MODEL] + bmod_ref[:, D_MODEL:2 * D_MODEL]
        h_ref[...] = (((xf * rstd) * g_ref[...]) * (1.0 + scale) + shift).astype(BF16)

    row = pl.BlockSpec((tm, D_MODEL), lambda i: (i, 0))
    const = lambda cols: pl.BlockSpec((1, cols), lambda i: (0, 0))
    return pl.pallas_call(
        body, name="norm", out_shape=jax.ShapeDtypeStruct((seq, D_MODEL), BF16), grid=(seq // tm,),
        in_specs=[row, const(3 * D_MODEL), const(3 * D_MODEL), const(D_MODEL)], out_specs=row,
        compiler_params=_params(("arbitrary",), VMEM_LIMIT),
    )(x, mod, b_mod, g_norm)


def _proj(h, w_in_all, cosf, sinf):
    seq = h.shape[0]
    tm = PROJ_ROWS
    last = seq // tm - 1

    def body(h_ref, w_ref, cos_ref, sin_ref, pf_ref, q_ref, k_ref, v_ref):
        j = pl.program_id(0)

        @pl.when((j < 2) | (j > 4))
        def _():
            pf_ref[...] = _dot(h_ref[...], w_ref[...])

        def heads(dst_ref, rotate, gain):
            for pair in range(N_HEADS // 2):
                both = _dot(h_ref[...], w_ref[:, 2 * pair * HEAD_DIM:2 * (pair + 1) * HEAD_DIM])
                for hh in (2 * pair, 2 * pair + 1):
                    t = both[:, (hh % 2) * HEAD_DIM:(hh % 2 + 1) * HEAD_DIM]
                    if rotate:
                        t = t * cos_ref[...] + _rope_partner(t) * sin_ref[...]
                    dst_ref[hh] = t if gain is None else t * gain

        @pl.when(j == 2)
        def _():
            heads(q_ref, True, ATTN_SCALE)

        @pl.when(j == 3)
        def _():
            heads(k_ref, True, None)

        @pl.when(j == 4)
        def _():
            heads(v_ref, False, None)

    def pf_block(j, i):
        f32_piece = (j < 2) | (j > 4)
        return (jnp.where(f32_piece, i, last), jnp.where(j < 2, j, jnp.where(j < 5, 1, j - 3)))

    def hm_block(piece):
        return lambda j, i: (0, jnp.where(j == piece, i, jnp.where(j < piece, 0, last)), 0)

    hm = jax.ShapeDtypeStruct((N_HEADS, seq, HEAD_DIM), F32)
    hm_spec = lambda piece: pl.BlockSpec((N_HEADS, tm, HEAD_DIM), hm_block(piece))
    row = lambda j, i: (i, 0)
    return pl.pallas_call(
        body, name="proj",
        out_shape=[jax.ShapeDtypeStruct((seq, 5 * D_MODEL), F32), hm, hm, hm],
        grid=(8, seq // tm),
        in_specs=[pl.BlockSpec((tm, D_MODEL), row),
                  pl.BlockSpec((None, D_MODEL, D_MODEL), lambda j, i: (j, 0, 0)),
                  pl.BlockSpec((tm, HEAD_DIM), row), pl.BlockSpec((tm, HEAD_DIM), row)],
        out_specs=[pl.BlockSpec((tm, D_MODEL), pf_block), hm_spec(2), hm_spec(3), hm_spec(4)],
        compiler_params=_params(("arbitrary", "arbitrary"), VMEM_LIMIT),
    )(h, w_in_all, cosf, sinf)


def _shift_down(v, s, head):
    rows = v.shape[0]
    row = lax.broadcasted_iota(jnp.int32, v.shape, 0)
    fill = jnp.concatenate([pltpu.roll(head, s, 0), jnp.zeros((rows - SUBLANES, v.shape[1]), v.dtype)], axis=0)
    return jnp.where(row < s, fill, pltpu.roll(v, s, 0))


def _shift_up(v, s, tail):
    rows = v.shape[0]
    row = lax.broadcasted_iota(jnp.int32, v.shape, 0)
    fill = jnp.concatenate([jnp.zeros((rows - SUBLANES, v.shape[1]), v.dtype),
                            pltpu.roll(tail, SUBLANES - s, 0)], axis=0)
    return jnp.where(row >= rows - s, fill, pltpu.roll(v, rows - s, 0))


def _doubling(a, b, period, reverse):
    rows = a.shape[0]
    pos = lax.broadcasted_iota(jnp.int32, a.shape, 0) & (period - 1)
    k = 1
    while k < period:
        inside = (pos < period - k) if reverse else (pos >= k)
        shift = rows - k if reverse else k
        a_s = jnp.where(inside, pltpu.roll(a, shift, 0), 1.0)
        b_s = jnp.where(inside, pltpu.roll(b, shift, 0), 0.0)
        b = a * b_s + b
        a = a * a_s
        k *= 2
    return a, b


def _scan(a, b, boundary, reverse, a_scr, b_scr, spread):
    rows = a.shape[0]
    ntile = rows // SUBLANES
    a_scr[...], b_scr[...] = _doubling(a, b, SUBLANES, reverse)
    ends = pl.ds(0 if reverse else SUBLANES - 1, ntile, stride=SUBLANES)
    a_end, x_end = _doubling(a_scr[ends, :], b_scr[ends, :], ntile, reverse)
    x_end = x_end + a_end * boundary
    tile = lax.broadcasted_iota(jnp.int32, x_end.shape, 0)
    if reverse:
        incoming = jnp.where(tile == ntile - 1, boundary, pltpu.roll(x_end, ntile - 1, 0))
        last = x_end[0:1, :]
    else:
        incoming = jnp.where(tile == 0, boundary, pltpu.roll(x_end, 1, 0))
        last = x_end[ntile - 1:ntile, :]
    for s in range(SUBLANES):
        spread[pl.ds(s, ntile, stride=SUBLANES), :] = incoming
    return b_scr[...] + a_scr[...] * spread[...], last


def _conv_taps(xr, head):
    return [_shift_down(xr, 3, head), _shift_down(xr, 2, head), _shift_down(xr, 1, head), xr]


def _rnn_gates(xc, wa, ba, wx, bx, lam, keep):
    xcb = xc.astype(BF16)
    r = _sigmoid(_dot(xcb, wa.astype(BF16)) + ba)
    i = _sigmoid(_dot(xcb, wx.astype(BF16)) + bx)
    softplus = jnp.maximum(-lam, 0.0) + jnp.log(1.0 + jnp.exp(-jnp.abs(lam)))
    cl = -LRU_C * softplus
    log_a = cl * r
    a_raw = jnp.exp(log_a)
    mult_raw = jnp.sqrt(-_expm1_nonpos(2.0 * log_a, a_raw * a_raw))
    live = keep > 0.0
    return r, i, cl, a_raw, mult_raw, jnp.where(live, a_raw, 0.0), jnp.where(live, mult_raw, 1.0), live


def _rnn_specs(seq, rows, time_of):
    per = rows // SUBLANES
    vec = pl.BlockSpec((None, 1, 128), lambda hb, n: (hb, 0, 0))
    mat = pl.BlockSpec((None, 128, 128), lambda hb, n: (hb, 0, 0))
    return [pl.BlockSpec((rows, 128), lambda hb, n: (time_of(n), hb)),
            pl.BlockSpec((SUBLANES, 128), lambda hb, n: (jnp.maximum(time_of(n) * per - 1, 0), hb)),
            pl.BlockSpec((rows, 1), lambda hb, n: (time_of(n), 0)),
            pl.BlockSpec((None, SUBLANES, 128), lambda hb, n: (hb, 0, 0)),
            vec, mat, vec, mat, vec, vec]


def _rnn_fwd(pf, keep, conv_w8, conv_b, w_a, b_a, w_x, b_x, lam):
    seq = pf.shape[0]
    rows = RNN_ROWS

    def body(x_ref, xh_ref, keep_ref, cw_ref, cb_ref, wa_ref, ba_ref, wx_ref, bx_ref, lam_ref, hr_ref,
             carry, a_scr, b_scr, spread):
        n = pl.program_id(1)

        @pl.when(n == 0)
        def _():
            carry[...] = jnp.zeros_like(carry)

        xr = x_ref[...]
        head = jnp.where(n > 0, xh_ref[...], 0.0)
        taps = _conv_taps(xr, head)
        xc = cb_ref[...] + sum(cw_ref[k:k + 1, :] * taps[k] for k in range(4))
        _, i, _, _, _, a, mult, _ = _rnn_gates(xc, wa_ref[...], ba_ref[...], wx_ref[...], bx_ref[...],
                                               lam_ref[...], keep_ref[...])
        h, last = _scan(a, mult * i * xc, carry[0:1, :], False, a_scr, b_scr, spread)
        hr_ref[...] = h
        carry[...] = jnp.broadcast_to(last, carry.shape)

    chunk_f32 = pltpu.VMEM((rows, 128), F32)
    return pl.pallas_call(
        body, name="rnn_fwd",
        out_shape=jax.ShapeDtypeStruct((seq, D_MODEL), F32),
        grid=(RNN_BLOCKS, seq // rows),
        in_specs=_rnn_specs(seq, rows, lambda n: n),
        out_specs=pl.BlockSpec((rows, 128), lambda hb, n: (n, hb)),
        scratch_shapes=[pltpu.VMEM((SUBLANES, 128), F32), chunk_f32, chunk_f32, chunk_f32],
        compiler_params=_params(("arbitrary", "arbitrary"), VMEM_LIMIT),
    )(pf, pf, keep, conv_w8, conv_b, w_a, b_a, w_x, b_x, lam)


def _rnn_bwd(pf, hr, dhr, keep, conv_w8, conv_b, w_a, b_a, w_x, b_x, lam):
    seq = pf.shape[0]
    rows = RNN_ROWS
    nchunk = seq // rows
    per = rows // SUBLANES
    time_of = lambda n: nchunk - 1 - n

    def body(x_ref, xh_ref, keep_ref, cw_ref, cb_ref, wa_ref, ba_ref, wx_ref, bx_ref, lam_ref,
             hr_ref, hrh_ref, dhr_ref,
             dx_ref, gcw_ref, gcb_ref, gwa_ref, gba_ref, gwx_ref, gbx_ref, glam_ref,
             g_carry, dxc_tail, a_scr, b_scr, spread):
        n = pl.program_id(1)
        first_in_time = n == nchunk - 1

        @pl.when(n == 0)
        def _():
            g_carry[...] = jnp.zeros_like(g_carry)
            dxc_tail[...] = jnp.zeros_like(dxc_tail)
            for ref in (gcw_ref, gcb_ref, gwa_ref, gba_ref, gwx_ref, gbx_ref, glam_ref):
                ref[...] = jnp.zeros_like(ref)

        xr = x_ref[...]
        head = jnp.where(first_in_time, 0.0, xh_ref[...])
        taps = _conv_taps(xr, head)
        cw = cw_ref[...]
        xc = cb_ref[...] + sum(cw[k:k + 1, :] * taps[k] for k in range(4))
        wa, wx, lam = wa_ref[...], wx_ref[...], lam_ref[...]
        r, i, cl, a_raw, mult_raw, a, mult, live = _rnn_gates(xc, wa, ba_ref[...], wx, bx_ref[...], lam,
                                                               keep_ref[...])
        h_prev = _shift_down(hr_ref[...], 1, jnp.where(first_in_time, 0.0, hrh_ref[...]))

        row = lax.broadcasted_iota(jnp.int32, xr.shape, 0)
        last = row == rows - 1
        a_next = jnp.where(last, 0.0, pltpu.roll(a, rows - 1, 0))
        g, g_first = _scan(a_next, dhr_ref[...] + jnp.where(last, g_carry[0:1, :], 0.0),
                           jnp.zeros((1, 128), F32), True, a_scr, b_scr, spread)
        g_carry[...] = jnp.broadcast_to(a[0:1, :] * g_first, g_carry.shape)

        da = g * h_prev
        dmult = g * i * xc
        di = g * mult * xc
        dxc = g * mult * i
        dlog_a = jnp.where(live, da * a_raw - dmult * a_raw * a_raw / mult_raw, 0.0)
        dpa = (dlog_a * cl) * r * (1.0 - r)
        dpx = di * i * (1.0 - i)
        glam_ref[...] += jnp.sum(dlog_a * r, axis=0, keepdims=True) * (LRU_C * _sigmoid(-lam))
        xcb, dpab, dpxb = xc.astype(BF16), dpa.astype(BF16), dpx.astype(BF16)
        gwa_ref[...] += _dot_tn(xcb, dpab)
        gwx_ref[...] += _dot_tn(xcb, dpxb)
        gba_ref[...] += jnp.sum(dpa, axis=0, keepdims=True)
        gbx_ref[...] += jnp.sum(dpx, axis=0, keepdims=True)
        dxc = dxc + _dot_nt(dpab, wa.astype(BF16)) + _dot_nt(dpxb, wx.astype(BF16))

        gcb_ref[...] += jnp.sum(dxc, axis=0, keepdims=True)
        for k in range(4):
            gcw_ref[k:k + 1, :] += jnp.sum(dxc * taps[k], axis=0, keepdims=True)
        tail = dxc_tail[...]
        dx = cw[3:4, :] * dxc
        for k in range(3):
            dx = dx + cw[k:k + 1, :] * _shift_up(dxc, 3 - k, tail)
        dx_ref[...] = dx.astype(BF16)
        dxc_tail[...] = dxc[0:SUBLANES, :]

    blk = lambda hb, n: (hb, 0, 0)
    chunk = pl.BlockSpec((rows, 128), lambda hb, n: (time_of(n), hb))
    vec_out = pl.BlockSpec((None, 1, 128), blk)
    mat_out = pl.BlockSpec((None, 128, 128), blk)
    vec_shape = jax.ShapeDtypeStruct((RNN_BLOCKS, 1, 128), F32)
    mat_shape = jax.ShapeDtypeStruct((RNN_BLOCKS, 128, 128), F32)
    return pl.pallas_call(
        body, name="rnn_bwd",
        out_shape=[jax.ShapeDtypeStruct((seq, D_MODEL), BF16),
                   jax.ShapeDtypeStruct((RNN_BLOCKS, SUBLANES, 128), F32), vec_shape,
                   mat_shape, vec_shape, mat_shape, vec_shape, vec_shape],
        grid=(RNN_BLOCKS, nchunk),
        in_specs=_rnn_specs(seq, rows, time_of) + [
            chunk, pl.BlockSpec((SUBLANES, 128), lambda hb, n: (jnp.maximum(time_of(n) * per - 1, 0), hb)), chunk],
        out_specs=[chunk, pl.BlockSpec((None, SUBLANES, 128), blk), vec_out,
                   mat_out, vec_out, mat_out, vec_out, vec_out],
        scratch_shapes=[pltpu.VMEM((SUBLANES, 128), F32), pltpu.VMEM((SUBLANES, 128), F32)]
                       + [pltpu.VMEM((rows, 128), F32)] * 3,
        compiler_params=_params(("arbitrary", "arbitrary"), VMEM_LIMIT),
    )(pf, pf, keep, conv_w8, conv_b, w_a, b_a, w_x, b_x, lam, hr, hr, dhr)


def _unit_rows(dil, r, j):
    start = j * KEY_BLOCK * dil + r
    return pl.ds(start, KEY_BLOCK) if dil == 1 else pl.ds(start, KEY_BLOCK, stride=dil)


def _attn_fwd(q, k, v):
    nh, seq, _ = q.shape
    nchunk = seq // SPAN
    nblk = SPAN // KEY_BLOCK
    wide = DILATIONS[-1]

    def body(q_ref, k_ref, v_ref, kp_ref, vp_ref, o_ref, l1_ref, l4_ref, l16_ref,
             acc, m_s, l_s, q16, k16, v16, k16p, v16p, acc16, m16, l16, tmp):
        n = pl.program_id(1)
        qi = lax.broadcasted_iota(jnp.int32, (KEY_BLOCK, KEY_BLOCK), 0)
        ki = lax.broadcasted_iota(jnp.int32, (KEY_BLOCK, KEY_BLOCK), 1)
        bias_own = jnp.where(ki <= qi, 0.0, NEG_INF)
        bias_before = jnp.where(ki >= qi, 0.0, NEG_INF)
        bias_mid = jnp.concatenate([bias_before, bias_own], axis=1)
        bias_first = jnp.concatenate([jnp.where(n > 0, bias_before, NEG_INF), bias_own], axis=1)
        ones = jnp.ones((2 * KEY_BLOCK, HEAD_DIM), BF16)
        diag = qi == ki

        @pl.when(n == 0)
        def _():
            k16p[...] = jnp.zeros_like(k16p)
            v16p[...] = jnp.zeros_like(v16p)

        def unit(qf, kp, kc, vp, vc, bias, state, rows, first):
            acc_r, m_r, l_r = state
            kcat = jnp.concatenate([kp, kc], axis=0).astype(BF16)
            vaug = jnp.concatenate([jnp.concatenate([vp, vc], axis=0).astype(BF16), ones], axis=1)
            s = _dot_nt(qf.astype(BF16), kcat) + bias
            mx = jnp.max(s, axis=-1, keepdims=True)
            if first:
                m_new = jnp.broadcast_to(mx, (KEY_BLOCK, HEAD_DIM))
            else:
                m_old = m_r[rows, :]
                m_new = jnp.maximum(m_old, mx)
            pv = _dot(jnp.exp(s - jnp.concatenate([m_new, m_new], axis=1)).astype(BF16), vaug)
            if first:
                acc_r[rows, :] = pv[:, :HEAD_DIM]
                l_r[rows, :] = pv[:, HEAD_DIM:]
            else:
                alpha = jnp.exp(m_old - m_new)
                acc_r[rows, :] = alpha * acc_r[rows, :] + pv[:, :HEAD_DIM]
                l_r[rows, :] = alpha * l_r[rows, :] + pv[:, HEAD_DIM:]
            m_r[rows, :] = m_new

        for gi, dil in enumerate(DILATIONS[:-1]):
            nb = nblk // dil
            for r in range(dil):
                for j in range(nb):
                    rows = _unit_rows(dil, r, j)
                    if j == 0:
                        prow = _unit_rows(dil, r, nb - 1)
                        kp, vp, bias = kp_ref[prow, :], vp_ref[prow, :], bias_first
                    else:
                        prow = _unit_rows(dil, r, j - 1)
                        kp, vp, bias = k_ref[prow, :], v_ref[prow, :], bias_mid
                    unit(q_ref[rows, :], kp, k_ref[rows, :], vp, v_ref[rows, :], bias, (acc, m_s, l_s), rows, gi == 0)

        for src, dst in ((q_ref, q16), (k_ref, k16), (v_ref, v16), (acc, acc16), (m_s, m16), (l_s, l16)):
            _to_residue_major(src, tmp, dst)
        for r in range(wide):
            rows = pl.ds(r * KEY_BLOCK, KEY_BLOCK)
            unit(q16[rows, :], k16p[rows, :], k16[rows, :], v16p[rows, :], v16[rows, :], bias_first,
                 (acc16, m16, l16), rows, False)
        k16p[...] = k16[...]
        v16p[...] = v16[...]

        den = l16[...]
        acc16[...] = acc16[...] * (1.0 / den)
        m16[...] = m16[...] + jnp.log(den)
        _from_residue_major(acc16, tmp, o_ref, False)
        _from_residue_major(m16, tmp, m_s, False)

        def lse_row(ref, rows):
            return jnp.sum(jnp.where(diag, ref[rows, :], 0.0), axis=0, keepdims=True)

        for dil, out in zip(DILATIONS[:-1], (l1_ref, l4_ref)):
            nb = nblk // dil
            for r in range(dil):
                for j in range(nb):
                    out[r * nb + j:r * nb + j + 1, :] = lse_row(m_s, _unit_rows(dil, r, j))
        for r in range(wide):
            l16_ref[r:r + 1, :] = lse_row(m16, pl.ds(r * KEY_BLOCK, KEY_BLOCK))

    blk = pl.BlockSpec((None, SPAN, HEAD_DIM), lambda h, n: (h, n, 0))
    pblk = pl.BlockSpec((None, SPAN, HEAD_DIM), lambda h, n: (h, jnp.maximum(n - 1, 0), 0))
    lblk = pl.BlockSpec((None, nblk, KEY_BLOCK), lambda h, n: (h, n, 0))
    lshape = jax.ShapeDtypeStruct((nh, seq // KEY_BLOCK, KEY_BLOCK), F32)
    o, l1, l4, l16 = pl.pallas_call(
        body, name="attn_fwd",
        out_shape=[jax.ShapeDtypeStruct((nh, seq, HEAD_DIM), F32), lshape, lshape, lshape],
        grid=(nh, nchunk), in_specs=[blk, blk, blk, pblk, pblk], out_specs=[blk, lblk, lblk, lblk],
        scratch_shapes=[pltpu.VMEM((SPAN, HEAD_DIM), F32)] * 12,
        compiler_params=_params(("arbitrary", "arbitrary"), VMEM_LIMIT),
    )(q, k, v, k, v)
    return o, (l1, l4, l16)


def _to_residue_major(src, tmp, dst):
    quarter = SPAN // 4
    for r4 in range(4):
        tmp[r4 * quarter:(r4 + 1) * quarter, :] = src[pl.ds(r4, quarter, stride=4), :]
    for r4 in range(4):
        for rp in range(4):
            r = r4 + 4 * rp
            dst[r * KEY_BLOCK:(r + 1) * KEY_BLOCK, :] = tmp[pl.ds(r4 * quarter + rp, KEY_BLOCK, stride=4), :]


def _from_residue_major(src, tmp, dst, add):
    quarter = SPAN // 4
    for r4 in range(4):
        for rp in range(4):
            r = r4 + 4 * rp
            tmp[pl.ds(r4 * quarter + rp, KEY_BLOCK, stride=4), :] = src[r * KEY_BLOCK:(r + 1) * KEY_BLOCK, :]
    for r4 in range(4):
        rows = pl.ds(r4, quarter, stride=4)
        part = tmp[r4 * quarter:(r4 + 1) * quarter, :]
        dst[rows, :] = dst[rows, :] + part if add else part


def _attn_bwd(q, k, v, do, o, lses, cosf, sinf):
    nh, seq, _ = q.shape
    nchunk = seq // SPAN
    nblk = SPAN // KEY_BLOCK
    wide = DILATIONS[-1]
    assert SPAN == wide * KEY_BLOCK

    def body(q_ref, k_ref, v_ref, do_ref, o_ref, kp_ref, vp_ref, l1_ref, l4_ref, l16_ref,
             cos_ref, sin_ref, cosp_ref, sinp_ref, dq_ref, dk_ref, dv_ref,
             dq_acc, dkc_acc, dvc_acc, dkp_acc, dvp_acc, q16, k16, v16, do16, o16, k16p, v16p,
             dq16, dkc16, dvc16, dkp16, dvp16, tmp, pt_s, ds_s, kcat_s, qb_s, dob_s):
        n = pl.program_id(1)
        ki = lax.broadcasted_iota(jnp.int32, (KEY_BLOCK, KEY_BLOCK), 0)
        qi = lax.broadcasted_iota(jnp.int32, (KEY_BLOCK, KEY_BLOCK), 1)
        bias_own = jnp.where(ki <= qi, 0.0, NEG_INF)
        bias_before = jnp.where(ki >= qi, 0.0, NEG_INF)
        bias_mid = jnp.concatenate([bias_before, bias_own], axis=0)
        bias_first = jnp.concatenate([jnp.where(n > 0, bias_before, NEG_INF), bias_own], axis=0)
        ones8 = jnp.ones((SUBLANES, HEAD_DIM), BF16)

        def row_dot(a, b):
            prod = a * b
            hi = prod.astype(BF16)
            lo = (prod - hi.astype(F32)).astype(BF16)
            return (_dot_nt(ones8, hi) + _dot_nt(ones8, lo))[0:1, :]

        def group(units, srcs, before, l_ref, accs):
            src_q, src_do, src_o, src_k, src_v = srcs
            before_k, before_v = before
            acc_q, acc_kc, acc_vc, acc_kp, acc_vp = accs
            for u, (rows, prow, outside, lrow, _) in enumerate(units):
                dof = src_do[rows, :]
                qb, dob = src_q[rows, :].astype(BF16), dof.astype(BF16)
                kp, vp = (before_k[prow, :], before_v[prow, :]) if outside else (src_k[prow, :], src_v[prow, :])
                kcat = jnp.concatenate([kp, src_k[rows, :]], axis=0).astype(BF16)
                vcat = jnp.concatenate([vp, src_v[rows, :]], axis=0).astype(BF16)
                bias = bias_first if outside else bias_mid
                pt = jnp.exp(_dot_nt(kcat, qb) + bias - l_ref[lrow:lrow + 1, :])
                dst = pt * (_dot_nt(vcat, dob) - row_dot(dof, src_o[rows, :]))
                pt_s[u], ds_s[u], kcat_s[u], qb_s[u], dob_s[u] = pt.astype(BF16), dst.astype(BF16), kcat, qb, dob
            for u, (rows, _, _, _, _) in enumerate(units):
                acc_q[rows, :] += _dot_tn(ds_s[u], kcat_s[u])
            for u, (rows, prow, outside, _, nxt) in enumerate(units):
                dk = _dot(ds_s[u, KEY_BLOCK:, :], qb_s[u])
                dv = _dot(pt_s[u, KEY_BLOCK:, :], dob_s[u])
                if nxt is not None:
                    dk = dk + _dot(ds_s[nxt, :KEY_BLOCK, :], qb_s[nxt])
                    dv = dv + _dot(pt_s[nxt, :KEY_BLOCK, :], dob_s[nxt])
                acc_kc[rows, :] += dk
                acc_vc[rows, :] += dv
                if outside:
                    acc_kp[prow, :] += _dot(ds_s[u, :KEY_BLOCK, :], qb_s[u])
                    acc_vp[prow, :] += _dot(pt_s[u, :KEY_BLOCK, :], dob_s[u])

        @pl.when(n == 0)
        def _():
            for ref in (dkp_acc, dvp_acc, dkp16, dvp16, k16p, v16p):
                ref[...] = jnp.zeros_like(ref)

        @pl.when(n < nchunk)
        def _():
            for ref in (dq_acc, dkc_acc, dvc_acc, dq16, dkc16, dvc16):
                ref[...] = jnp.zeros_like(ref)
            for src, dst in ((q_ref, q16), (k_ref, k16), (v_ref, v16), (do_ref, do16), (o_ref, o16)):
                _to_residue_major(src, tmp, dst)
            natural = (q_ref, do_ref, o_ref, k_ref, v_ref)
            for dil, l_ref in zip(DILATIONS[:-1], (l1_ref, l4_ref)):
                nb = nblk // dil
                units = [(_unit_rows(dil, r, j), _unit_rows(dil, r, (j - 1) % nb), j == 0, r * nb + j,
                          r * nb + j + 1 if j + 1 < nb else None) for r in range(dil) for j in range(nb)]
                group(units, natural, (kp_ref, vp_ref), l_ref, (dq_acc, dkc_acc, dvc_acc, dkp_acc, dvp_acc))
            blocks = [pl.ds(r * KEY_BLOCK, KEY_BLOCK) for r in range(wide)]
            group([(rows, rows, True, r, None) for r, rows in enumerate(blocks)], (q16, do16, o16, k16, v16),
                  (k16p, v16p), l16_ref, (dq16, dkc16, dvc16, dkp16, dvp16))
            _from_residue_major(dq16, tmp, dq_acc, True)
            dq = dq_acc[...]
            dq_ref[...] = ((dq * cos_ref[...] - _rope_partner(dq) * sin_ref[...]) * ATTN_SCALE).astype(BF16)

        @pl.when(n > 0)
        def _():
            _from_residue_major(dkp16, tmp, dkp_acc, True)
            _from_residue_major(dvp16, tmp, dvp_acc, True)
            dk = dkp_acc[...]
            dk_ref[...] = (dk * cosp_ref[...] - _rope_partner(dk) * sinp_ref[...]).astype(BF16)
            dv_ref[...] = dvp_acc[...].astype(BF16)

        @pl.when(n < nchunk)
        def _():
            for src, dst in ((dkc_acc, dkp_acc), (dvc_acc, dvp_acc), (dkc16, dkp16), (dvc16, dvp16),
                             (k16, k16p), (v16, v16p)):
                dst[...] = src[...]

    last = nchunk - 1
    cur = lambda h, n: (h, jnp.minimum(n, last), 0)
    prev = lambda h, n: (h, jnp.clip(n - 1, 0, last), 0)
    blk = lambda idx: pl.BlockSpec((None, SPAN, HEAD_DIM), idx)
    lblk = pl.BlockSpec((None, nblk, KEY_BLOCK), cur)
    tab = pl.BlockSpec((SPAN, HEAD_DIM), lambda h, n: (jnp.minimum(n, last), 0))
    tabp = pl.BlockSpec((SPAN, HEAD_DIM), lambda h, n: (jnp.clip(n - 1, 0, last), 0))
    out_q = pl.BlockSpec((SPAN, HEAD_DIM), lambda h, n: (jnp.minimum(n, last), h))
    out_kv = pl.BlockSpec((SPAN, HEAD_DIM), lambda h, n: (jnp.clip(n - 1, 0, last), h))
    shape = jax.ShapeDtypeStruct((seq, nh * HEAD_DIM), BF16)
    return pl.pallas_call(
        body, name="attn_bwd", out_shape=[shape, shape, shape], grid=(nh, nchunk + 1),
        in_specs=[blk(cur)] * 5 + [blk(prev)] * 2 + [lblk] * 3 + [tab, tab, tabp, tabp],
        out_specs=[out_q, out_kv, out_kv],
        scratch_shapes=[pltpu.VMEM((SPAN, HEAD_DIM), F32)] * 18
                       + [pltpu.VMEM((nblk, 2 * KEY_BLOCK, HEAD_DIM), BF16)] * 3
                       + [pltpu.VMEM((nblk, KEY_BLOCK, HEAD_DIM), BF16)] * 2,
        compiler_params=_params(("arbitrary", "arbitrary"), VMEM_LIMIT),
    )(q, k, v, do, o, k, v, *lses, cosf, sinf, cosf, sinf)


def _hub(x, tgt, hr, pf, o_hm, mod, b_mod, b_gate, g_final, w_out_rnn, w_out_attn, w_o):
    seq = x.shape[0]
    tm = HUB_ROWS
    nsteps = seq // tm

    def body(x_ref, t_ref, hr_ref, zr_ref, za_ref, gr_ref, ga_ref, o_ref, mod_ref, bmod_ref, bg_ref, gf_ref,
             wr_hbm, wa_hbm, wo_hbm,
             dx2_ref, dhr_ref, dzr_ref, do_ref, dza_ref, dgr_ref, dga_ref,
             ur_ref, dyr_ref, ua_ref, dya_ref, mg_ref, dmo_ref,
             ggf_ref, gbg_ref, dgate_ref, loss_ref,
             wr, wa, wo, sem):
        step = pl.program_id(0)

        @pl.when(step == 0)
        def _():
            for src, dst in ((wr_hbm, wr), (wa_hbm, wa), (wo_hbm, wo)):
                cp = pltpu.make_async_copy(src, dst, sem)
                cp.start()
                cp.wait()
            for ref in (ggf_ref, gbg_ref, dgate_ref, loss_ref):
                ref[...] = jnp.zeros_like(ref)

        gate = mod_ref[:, 2 * D_MODEL:] + bmod_ref[:, 2 * D_MODEL:]
        gfin = gf_ref[...]
        hr_t, zr, za = hr_ref[...], zr_ref[...], za_ref[...]
        o = jnp.concatenate([o_ref[hh] for hh in range(N_HEADS)], axis=1)
        sig_zr, sig_za = _sigmoid(zr), _sigmoid(za)
        silu_zr, silu_za = zr * sig_zr, za * sig_za
        u_rnn = (hr_t * silu_zr).astype(BF16)
        u_attn = (o * silu_za).astype(BF16)
        y_rnn = _dot(u_rnn, wr[...])
        y_attn = _dot(u_attn, wa[...])
        sr = _sigmoid(gr_ref[...] + bg_ref[:, :D_MODEL])
        sa = _sigmoid(ga_ref[...] + bg_ref[:, D_MODEL:])
        merged = (sr * y_rnn + sa * y_attn).astype(BF16)
        mo = _dot(merged, wo[...])
        x2 = x_ref[...] + gate * mo
        rstd = lax.rsqrt(jnp.mean(x2 * x2, axis=-1, keepdims=True) + NORM_EPS)
        xn = x2 * rstd
        err = xn * gfin - t_ref[...]
        loss_ref[...] += 0.5 * jnp.sum(jnp.sum(err * err, axis=-1, keepdims=True) * (1.0 / D_MODEL),
                                       axis=0, keepdims=True)

        dy = err * (1.0 / D_MODEL)
        ggf_ref[...] += jnp.sum(dy * xn, axis=0, keepdims=True)
        dxn = dy * gfin
        dx2 = rstd * (dxn - xn * jnp.mean(dxn * xn, axis=-1, keepdims=True))
        dx2_ref[...] = dx2
        dgate_ref[...] += jnp.sum(dx2 * mo, axis=0, keepdims=True)
        dmo = (dx2 * gate).astype(BF16)
        dmerged = _dot_nt(dmo, wo[...])
        mg_ref[...] = merged
        dmo_ref[...] = dmo
        dy_rnn = (dmerged * sr).astype(BF16)
        dy_attn = (dmerged * sa).astype(BF16)
        dg_r = dmerged * y_rnn * sr * (1.0 - sr)
        dg_a = dmerged * y_attn * sa * (1.0 - sa)
        dgr_ref[...] = dg_r.astype(BF16)
        dga_ref[...] = dg_a.astype(BF16)
        gbg_ref[:, :D_MODEL] += jnp.sum(dg_r, axis=0, keepdims=True)
        gbg_ref[:, D_MODEL:] += jnp.sum(dg_a, axis=0, keepdims=True)
        du_rnn = _dot_nt(dy_rnn, wr[...])
        du_attn = _dot_nt(dy_attn, wa[...])
        ur_ref[...] = u_rnn
        dyr_ref[...] = dy_rnn
        ua_ref[...] = u_attn
        dya_ref[...] = dy_attn
        dhr_ref[...] = du_rnn * silu_zr
        dzr_ref[...] = (du_rnn * hr_t * (sig_zr * (1.0 + zr * (1.0 - sig_zr)))).astype(BF16)
        dza_ref[...] = (du_attn * o * (sig_za * (1.0 + za * (1.0 - sig_za)))).astype(BF16)
        d_o = du_attn * silu_za
        for hh in range(N_HEADS):
            do_ref[hh] = d_o[:, hh * HEAD_DIM:(hh + 1) * HEAD_DIM]

    row = pl.BlockSpec((tm, D_MODEL), lambda i: (i, 0))
    piece = lambda slot: pl.BlockSpec((tm, D_MODEL), lambda i: (i, slot))
    hm = pl.BlockSpec((N_HEADS, tm, HEAD_DIM), lambda i: (0, i, 0))
    const = lambda cols: pl.BlockSpec((1, cols), lambda i: (0, 0))
    any_spec = pl.BlockSpec(memory_space=pl.ANY)
    act_f32 = jax.ShapeDtypeStruct((seq, D_MODEL), F32)
    act_bf16 = jax.ShapeDtypeStruct((seq, D_MODEL), BF16)
    return pl.pallas_call(
        body, name="hub",
        out_shape=[act_f32, act_f32, act_bf16, jax.ShapeDtypeStruct((N_HEADS, seq, HEAD_DIM), F32),
                   act_bf16, act_bf16, act_bf16] + [act_bf16] * 6 + [
                   jax.ShapeDtypeStruct((1, D_MODEL), F32), jax.ShapeDtypeStruct((1, 2 * D_MODEL), F32),
                   jax.ShapeDtypeStruct((1, D_MODEL), F32), jax.ShapeDtypeStruct((1, 1), F32)],
        grid=(nsteps,),
        in_specs=[row, row, row, piece(1), piece(2), piece(3), piece(4), hm,
                  const(3 * D_MODEL), const(3 * D_MODEL), const(2 * D_MODEL), const(D_MODEL),
                  any_spec, any_spec, any_spec],
        out_specs=[row, row, row, hm, row, row, row] + [row] * 6 + [
                   const(D_MODEL), const(2 * D_MODEL), const(D_MODEL), const(1)],
        scratch_shapes=[pltpu.VMEM((D_MODEL, D_MODEL), BF16)] * 3 + [pltpu.SemaphoreType.DMA],
        compiler_params=_params(("arbitrary",), VMEM_LIMIT),
    )(x, tgt, hr, pf, pf, pf, pf, o_hm, mod, b_mod, b_gate, g_final, w_out_rnn, w_out_attn, w_o)


def _pair_grads(name, lefts, rights):
    n = len(rights)
    shared = len(lefts) == 1
    seq = rights[0].shape[0]
    tk = WGRAD_ROWS
    nk = seq // tk

    def body(*refs):
        l_refs, r_refs, out_ref = refs[:len(lefts)], refs[len(lefts):len(lefts) + n], refs[len(lefts) + n]
        j, kk = pl.program_id(0), pl.program_id(1)

        @pl.when(kk == 0)
        def _():
            out_ref[...] = jnp.zeros_like(out_ref)

        for m in range(n):
            @pl.when(j == m)
            def _(m=m):
                out_ref[...] += _dot_tn(l_refs[0 if shared else m][...], r_refs[m][...])

    def spec(m):
        return pl.BlockSpec((tk, D_MODEL), lambda j, kk: (jnp.where(j == m, kk, jnp.where(j < m, 0, nk - 1)), 0))

    left_specs = [pl.BlockSpec((tk, D_MODEL), lambda j, kk: (kk, 0))] if shared else [spec(m) for m in range(n)]
    return pl.pallas_call(
        body, name=name,
        out_shape=jax.ShapeDtypeStruct((n, D_MODEL, D_MODEL), F32),
        grid=(n, nk),
        in_specs=left_specs + [spec(m) for m in range(n)],
        out_specs=pl.BlockSpec((None, D_MODEL, D_MODEL), lambda j, kk: (j, 0, 0)),
        compiler_params=_params(("arbitrary", "arbitrary"), VMEM_LIMIT),
    )(*lefts, *rights)


def _dh_dx(pieces, w_in_all, x, dx2, mod, b_mod, g_norm):
    seq = x.shape[0]
    tm = DX_ROWS

    def body(*refs):
        p_refs = refs[:8]
        w_hbm, x_ref, dx2_ref, mod_ref, bmod_ref, g_ref = refs[8:14]
        gx_ref, dshift_ref, dscale_ref, ggn_ref, w_scr, sem = refs[14:]
        step = pl.program_id(0)

        @pl.when(step == 0)
        def _():
            cp = pltpu.make_async_copy(w_hbm, w_scr, sem)
            cp.start()
            cp.wait()
            for ref in (dshift_ref, dscale_ref, ggn_ref):
                ref[...] = jnp.zeros_like(ref)

        dh = _dot_nt(p_refs[0][...], w_scr[0])
        for j in range(1, 8):
            dh = dh + _dot_nt(p_refs[j][...], w_scr[j])
        scale1 = 1.0 + mod_ref[:, D_MODEL:2 * D_MODEL] + bmod_ref[:, D_MODEL:2 * D_MODEL]
        g = g_ref[...]
        xf = x_ref[...]
        rstd_t = lax.rsqrt(jnp.mean(xf * xf, axis=-1, keepdims=True) + NORM_EPS)
        xn = xf * rstd_t
        dshift_ref[...] += jnp.sum(dh, axis=0, keepdims=True)
        dscale_ref[...] += jnp.sum(dh * (xn * g), axis=0, keepdims=True)
        ggn_ref[...] += jnp.sum(dh * scale1 * xn, axis=0, keepdims=True)
        dxn = dh * (g * scale1)
        gx_ref[...] = rstd_t * (dxn - xn * jnp.mean(dxn * xn, axis=-1, keepdims=True)) + dx2_ref[...]

    row = pl.BlockSpec((tm, D_MODEL), lambda i: (i, 0))
    const = lambda cols: pl.BlockSpec((1, cols), lambda i: (0, 0))
    vec = jax.ShapeDtypeStruct((1, D_MODEL), F32)
    return pl.pallas_call(
        body, name="dh_dx",
        out_shape=[jax.ShapeDtypeStruct((seq, D_MODEL), F32), vec, vec, vec],
        grid=(seq // tm,),
        in_specs=[row] * 8 + [pl.BlockSpec(memory_space=pl.ANY), row, row,
                              const(3 * D_MODEL), const(3 * D_MODEL), const(D_MODEL)],
        out_specs=[row, const(D_MODEL), const(D_MODEL), const(D_MODEL)],
        scratch_shapes=[pltpu.VMEM((8, D_MODEL, D_MODEL), BF16), pltpu.SemaphoreType.DMA],
        compiler_params=_params(("arbitrary",), VMEM_LIMIT),
    )(*pieces, w_in_all, x, dx2, mod, b_mod, g_norm)


def _adamw(name, w, g, m, v):
    rows, cols = w.shape
    tr = rows if rows <= 256 else 256

    def body(w_ref, g_ref, m_ref, v_ref, d_ref, nm_ref, nv_ref):
        gv = g_ref[...]
        nm = ADAM_B1 * m_ref[...] + (1.0 - ADAM_B1) * gv
        nv = ADAM_B2 * v_ref[...] + (1.0 - ADAM_B2) * (gv * gv)
        m_hat = nm / (1.0 - ADAM_B1 ** ADAM_STEP)
        v_hat = nv / (1.0 - ADAM_B2 ** ADAM_STEP)
        d_ref[...] = -ADAM_LR * (m_hat / (jnp.sqrt(v_hat) + ADAM_EPS) + ADAM_WD * w_ref[...])
        nm_ref[...] = nm
        nv_ref[...] = nv

    spec = pl.BlockSpec((tr, cols), lambda i: (i, 0))
    shape = jax.ShapeDtypeStruct((rows, cols), F32)
    return pl.pallas_call(
        body, name=name, out_shape=[shape, shape, shape], grid=(rows // tr,),
        in_specs=[spec] * 4, out_specs=[spec] * 3,
        compiler_params=_params(("arbitrary",)),
    )(w, g, m, v)


def kernel(x, c, positions, g_norm, w_mod, b_mod, w_in, b_gate, conv_w, conv_b, w_a, b_a, w_x, b_x, lam, w_out_rnn, w_out_attn, w_o, g_final, loss_target, m_g_norm, m_w_mod, m_b_mod, m_w_in, m_b_gate, m_conv_w, m_conv_b, m_w_a, m_b_a, m_w_x, m_b_x, m_lam, m_w_out_rnn, m_w_out_attn, m_w_o, m_g_final, v_g_norm, v_w_mod, v_b_mod, v_w_in, v_b_gate, v_conv_w, v_conv_b, v_w_a, v_b_a, v_w_x, v_b_x, v_lam, v_w_out_rnn, v_w_out_attn, v_w_o, v_g_final):
    seq = x.shape[1]
    me = _index(_my_pos())
    xs, tgt = x[0], loss_target[0]

    pos = positions[0].astype(F32)[:, None]
    inv_freq = ROPE_THETA ** (-jnp.arange(0, 2 * ROT_HALF, 2, dtype=F32) / (2 * ROT_HALF))
    ang = pos * inv_freq
    rest = HEAD_DIM - 2 * ROT_HALF
    cosf = jnp.concatenate([jnp.cos(ang), jnp.cos(ang), jnp.ones((seq, rest), F32)], axis=1)
    sinf = jnp.concatenate([-jnp.sin(ang), jnp.sin(ang), jnp.zeros((seq, rest), F32)], axis=1)
    keep = (positions[0] != 0).astype(F32)[:, None]

    w_in_all, w_or_all, w_oa_all, w_o_all = _ag_big(
        "gather_weights", [w_in[0].astype(BF16), w_out_rnn[0].astype(BF16),
                           w_out_attn[0].astype(BF16), w_o[0].astype(BF16)])
    w_or_all, w_oa_all, w_o_all = (t.reshape(D_MODEL, D_MODEL) for t in (w_or_all, w_oa_all, w_o_all))
    conv_w8 = _ag_small("gather_conv_w", jnp.pad(conv_w[0], ((0, SUBLANES - 4), (0, 0))))
    c_all = _ag_small("gather_c", jnp.broadcast_to(c, (SUBLANES, D_MODEL)))[:, 0, :]
    mod_cols = w_mod.shape[2]
    mod_part = _ag_small("gather_mod", _mod_fwd(c_all, w_mod[0]))
    mod = lax.dynamic_index_in_dim(mod_part, me, axis=1, keepdims=False).reshape(1, N_DEV * mod_cols)

    blocks = lambda t: t.reshape(RNN_BLOCKS, 1, 128)
    rnn_params = (conv_w8, blocks(conv_b), w_a[0], blocks(b_a), w_x[0], blocks(b_x), blocks(lam))

    h = _norm(xs, mod, b_mod, g_norm)
    pf, q, k, v = _proj(h, w_in_all, cosf, sinf)
    hr = _rnn_fwd(pf, keep, *rnn_params)
    o, lses = _attn_fwd(q, k, v)

    (dx2, dhr, dz_rnn, d_o, dz_attn, dg_r, dg_a, u_rnn, dy_rnn, u_attn, dy_attn, merged, dmo,
     gp_g_final, gp_b_gate, dgate, loss_part) = _hub(
        xs, tgt, hr, pf, o, mod, b_mod, b_gate, g_final.reshape(1, D_MODEL), w_or_all, w_oa_all, w_o_all)
    gp_w_or, gp_w_oa, gp_w_o = _pair_grads("out_grads", [u_rnn, u_attn, merged], [dy_rnn, dy_attn, dmo])
    dq, dk, dv = _attn_bwd(q, k, v, d_o, o, lses, cosf, sinf)
    dx_rnn, gp_conv_w, gp_conv_b, gp_w_a, gp_b_a, gp_w_x, gp_b_x, gp_lam = _rnn_bwd(pf, hr, dhr, keep, *rnn_params)
    pieces = [dx_rnn, dz_rnn, dq, dk, dv, dz_attn, dg_r, dg_a]
    gp_w_in = _pair_grads("w_in_grad", [h], pieces)

    stacks = [gp_w_in, gp_w_or.reshape(N_DEV, 128, D_MODEL), gp_w_oa.reshape(N_DEV, 128, D_MODEL),
              gp_w_o.reshape(N_DEV, 128, D_MODEL)]
    from_sib = _rs_to_sibling("rs_sibling", stacks)
    targets = jnp.bitwise_xor(me, 2 * jnp.arange(4, dtype=jnp.int32)).astype(jnp.int32)
    sums = [_add_sibling("rs_add_sibling_%d" % a, s_, r_, targets) for a, (s_, r_) in enumerate(zip(stacks, from_sib))]
    send_sems, recv_sems, sent, landing, token = _rs_chips_start([send for _, send in sums])

    mod_after = mod + token[0:1, 0:1]
    grad_x, dshift, dscale, gp_g_norm = _dh_dx(pieces, w_in_all, xs, dx2, mod_after, b_mod, g_norm)

    dmod = jnp.concatenate([dshift, dscale, dgate], axis=1)
    dmod_all = _ag_small("gather_dmod", jnp.broadcast_to(dmod, (SUBLANES, 3 * D_MODEL)))[:, 0, :]
    dmod_cols = lax.dynamic_slice_in_dim(dmod_all, me * mod_cols, mod_cols, axis=1)
    g_b_mod, g_w_mod = _mod_bwd(c_all, dmod_all, dmod_cols)

    flat = lambda t: t.reshape(-1, 128)
    small = [flat(gp_g_norm), flat(gp_b_gate), flat(gp_conv_b), flat(gp_b_a), flat(gp_b_x), flat(gp_lam),
             flat(gp_g_final), flat(gp_conv_w), jnp.broadcast_to(loss_part, (SUBLANES, 128)),
             flat(gp_w_a), flat(gp_w_x)]
    sizes = [t.shape[0] for t in small]
    small.append(jnp.zeros((-sum(sizes) % (2 * SUBLANES), 128), F32))
    total = _allreduce_small("allreduce_small_grads", jnp.concatenate(small, axis=0))
    offs = [sum(sizes[:i]) for i in range(len(sizes))]
    (g_g_norm, g_b_gate, g_conv_b, g_b_a, g_b_x, g_lam, g_g_final, g_conv_w_all, loss_rows, g_w_a, g_w_x) = (
        total[o_:o_ + s_] for o_, s_ in zip(offs, sizes))
    loss = loss_rows[0, 0]
    g_conv_w = lax.dynamic_index_in_dim(g_conv_w_all.reshape(RNN_BLOCKS, SUBLANES, 128), me, axis=0,
                                        keepdims=False)[:4]

    from_chips = _rs_chips_wait(send_sems, recv_sems, sent, landing, total)
    g_w_in, g_w_or, g_w_oa, g_w_o = (
        _add_chips("rs_add_chips_%d" % a, own, r_) for a, ((own, _), r_) in enumerate(zip(sums, from_chips)))

    weights = [
        ("g_norm", g_norm, g_g_norm, m_g_norm, v_g_norm, (SUBLANES, 128)),
        ("w_mod", w_mod, g_w_mod, m_w_mod, v_w_mod, (D_MODEL, mod_cols)),
        ("b_mod", b_mod, g_b_mod, m_b_mod, v_b_mod, (3 * SUBLANES, 128)),
        ("w_in", w_in, g_w_in, m_w_in, v_w_in, (D_MODEL, D_MODEL)),
        ("b_gate", b_gate, g_b_gate, m_b_gate, v_b_gate, (2 * SUBLANES, 128)),
        ("conv_w", conv_w, g_conv_w, m_conv_w, v_conv_w, (4, 128)),
        ("conv_b", conv_b, g_conv_b, m_conv_b, v_conv_b, (SUBLANES, 128)),
        ("w_a", w_a, g_w_a, m_w_a, v_w_a, (RNN_BLOCKS * 128, 128)),
        ("b_a", b_a, g_b_a, m_b_a, v_b_a, (SUBLANES, 128)),
        ("w_x", w_x, g_w_x, m_w_x, v_w_x, (RNN_BLOCKS * 128, 128)),
        ("b_x", b_x, g_b_x, m_b_x, v_b_x, (SUBLANES, 128)),
        ("lam", lam, g_lam, m_lam, v_lam, (SUBLANES, 128)),
        ("w_out_rnn", w_out_rnn, g_w_or, m_w_out_rnn, v_w_out_rnn, (128, D_MODEL)),
        ("w_out_attn", w_out_attn, g_w_oa, m_w_out_attn, v_w_out_attn, (128, D_MODEL)),
        ("w_o", w_o, g_w_o, m_w_o, v_w_o, (128, D_MODEL)),
        ("g_final", g_final, g_g_final, m_g_final, v_g_final, (SUBLANES, 128)),
    ]
    out_g, out_d, out_m, out_v = [], [], [], []
    for name, w_, g_, m_, v_, shape2 in weights:
        d_, nm_, nv_ = _adamw("adamw_" + name, w_.reshape(shape2), g_.reshape(shape2), m_.reshape(shape2),
                              v_.reshape(shape2))
        out_g.append(g_.reshape(w_.shape))
        out_d.append(d_.reshape(w_.shape))
        out_m.append(nm_.reshape(w_.shape))
        out_v.append(nv_.reshape(w_.shape))
    return (loss, grad_x[None], *out_g, *out_d, *out_m, *out_v)
```

```python
import jax
import jax.numpy as jnp
from jax import lax
from jax.experimental import pallas as pl
from jax.experimental.pallas import tpu as pltpu

F32 = jnp.float32
BF16 = jnp.bfloat16
MESH = pl.DeviceIdType.MESH

D_MODEL = 1024
N_HEADS = 8
HEAD_DIM = 128
RNN_BLOCKS = 8
N_DEV = 8
ROT_HALF = 16
ROPE_THETA = 500000.0
DILATIONS = (1, 4, 16)
KEY_BLOCK = 128
SPAN = KEY_BLOCK * DILATIONS[-1]
ATTN_SCALE = HEAD_DIM ** -0.5
NORM_EPS = 1e-6
LRU_C = 8.0
NEG_INF = -1e30
ADAM_LR, ADAM_B1, ADAM_B2, ADAM_EPS, ADAM_WD, ADAM_STEP = 0.001, 0.9, 0.999, 1e-08, 0.01, 10

SUBLANES = 8
VMEM_LIMIT = 56 * 1024 * 1024
PROJ_ROWS = 1024
RNN_ROWS = 512
HUB_ROWS = 256
DX_ROWS = 256
WGRAD_ROWS = 1024
ADD_ROWS = 256


def _params(sem=None, vmem=None):
    return pltpu.CompilerParams(dimension_semantics=sem, vmem_limit_bytes=vmem)


def _dot(a, b):
    return jnp.dot(a, b, preferred_element_type=F32)


def _dot_nt(a, b):
    return lax.dot_general(a, b, (((1,), (1,)), ((), ())), preferred_element_type=F32)


def _dot_tn(a, b):
    return lax.dot_general(a, b, (((0,), (0,)), ((), ())), preferred_element_type=F32)


def _sigmoid(z):
    return 1.0 / (1.0 + jnp.exp(-z))


def _expm1_nonpos(z, exp_z):
    return jnp.where(z > -0.01, z * (1.0 + 0.5 * z), exp_z - 1.0)


def _my_pos():
    return lax.axis_index("x"), lax.axis_index("y"), lax.axis_index("c")


def _flip(pos, k):
    x, y, c = pos
    return ((1 - x) if k & 4 else x, (1 - y) if k & 2 else y, (1 - c) if k & 1 else c)


def _index(pos):
    return 4 * pos[0] + 2 * pos[1] + pos[2]


def _ag_small(name, v):
    rows, cols = v.shape

    def body(v_ref, out_ref, send_sems, recv_sems):
        me = _my_pos()
        out_ref[_index(me)] = v_ref[...]
        sends = []
        for k in range(1, N_DEV):
            cp = pltpu.make_async_remote_copy(
                src_ref=v_ref, dst_ref=out_ref.at[_index(me)], send_sem=send_sems.at[k - 1],
                recv_sem=recv_sems.at[k - 1], device_id=_flip(me, k), device_id_type=MESH)
            cp.start()
            sends.append(cp)
        for k in range(1, N_DEV):
            peer = _flip(me, k)
            pltpu.make_async_remote_copy(
                src_ref=v_ref, dst_ref=out_ref.at[_index(peer)], send_sem=send_sems.at[k - 1],
                recv_sem=recv_sems.at[k - 1], device_id=peer, device_id_type=MESH).wait_recv()
        for cp in sends:
            cp.wait_send()

    return pl.pallas_call(
        body, name=name,
        out_shape=jax.ShapeDtypeStruct((N_DEV, rows, cols), v.dtype),
        in_specs=[pl.BlockSpec(memory_space=pltpu.VMEM)],
        out_specs=pl.BlockSpec(memory_space=pltpu.VMEM),
        scratch_shapes=[pltpu.SemaphoreType.DMA((N_DEV - 1,)), pltpu.SemaphoreType.DMA((N_DEV - 1,))],
        compiler_params=_params(None, VMEM_LIMIT),
    )(v)


def _ag_big(name, shards):
    n = len(shards)

    def body(*refs):
        ins, outs = refs[:n], refs[n:2 * n]
        send_sems, recv_sems, local_sems = refs[2 * n:]
        me = _my_pos()
        sib = _flip(me, 1)
        chips = [2, 4, 6]

        def copy(a, k, block, to, src=None):
            rows = outs[a].at[_index(block)]
            return pltpu.make_async_remote_copy(
                src_ref=rows if src is None else src, dst_ref=rows,
                send_sem=send_sems.at[a * 7 + k], recv_sem=recv_sems.at[a * 7 + k],
                device_id=to, device_id_type=MESH)

        started = []
        for a in range(n):
            mine = pltpu.make_async_copy(ins[a], outs[a].at[_index(me)], local_sems.at[a])
            mine.start()
            started.append(mine)
        sends = []
        for a in range(n):
            first = [copy(a, 0, me, sib, src=ins[a])]
            first += [copy(a, 1 + j, me, _flip(me, ch), src=ins[a]) for j, ch in enumerate(chips)]
            for cp in first:
                cp.start()
            sends += first
        for j, ch in enumerate(chips):
            for a in range(n):
                copy(a, 1 + j, _flip(me, ch), me).wait_recv()
                fwd = copy(a, 4 + j, _flip(me, ch), sib)
                fwd.start()
                sends.append(fwd)
        for a in range(n):
            copy(a, 0, sib, me).wait_recv()
            for j, ch in enumerate(chips):
                copy(a, 4 + j, _flip(sib, ch), me).wait_recv()
        for cp in sends:
            cp.wait_send()
        for mine in started:
            mine.wait()

    any_spec = pl.BlockSpec(memory_space=pl.ANY)
    return pl.pallas_call(
        body, name=name,
        out_shape=[jax.ShapeDtypeStruct((N_DEV,) + s.shape, s.dtype) for s in shards],
        in_specs=[any_spec] * n, out_specs=[any_spec] * n,
        scratch_shapes=[pltpu.SemaphoreType.DMA((7 * n,)), pltpu.SemaphoreType.DMA((7 * n,)),
                        pltpu.SemaphoreType.DMA((n,))],
    )(*shards)


def _peer_copies(shards, lands, send_sems, recv_sems):
    me = _my_pos()
    return [pltpu.make_async_remote_copy(
        src_ref=shards[a], dst_ref=lands[a].at[_index(me)],
        send_sem=send_sems.at[a * 7 + k - 1], recv_sem=recv_sems.at[a * 7 + k - 1],
        device_id=_flip(me, k), device_id_type=MESH) for a in range(len(shards)) for k in range(1, N_DEV)]


def _gather_start(shards, me):
    n = len(shards)

    def body(*refs):
        srcs, lands = refs[:n], refs[n:2 * n]
        send_sems, recv_sems = refs[2 * n:2 * n + 2]
        for cp in _peer_copies(srcs, lands, send_sems, recv_sems):
            cp.start()
        refs[-1][...] = jnp.zeros_like(refs[-1])

    hbm = pl.BlockSpec(memory_space=pltpu.HBM)
    sem = pl.BlockSpec(memory_space=pltpu.SEMAPHORE)
    held = [pltpu.HBM(s.shape, s.dtype) for s in shards]
    landing = [lax.dynamic_update_slice(jnp.zeros((N_DEV,) + s.shape, s.dtype), s[None], (me, 0, 0)) for s in shards]
    held_land = [pltpu.HBM(t.shape, t.dtype) for t in landing]
    outs = pl.pallas_call(
        body, name="gather_out_weights_start",
        out_shape=(pltpu.SemaphoreType.DMA((7 * n,)), pltpu.SemaphoreType.DMA((7 * n,)), *held, *held_land,
                   jax.ShapeDtypeStruct((SUBLANES, 128), F32)),
        in_specs=[hbm] * (2 * n),
        out_specs=(sem, sem, *[hbm] * (2 * n), pl.BlockSpec(memory_space=pltpu.VMEM)),
        input_output_aliases={i: 2 + i for i in range(2 * n)},
        compiler_params=pltpu.CompilerParams(has_side_effects=pltpu.SideEffectType.DATAFLOW_SIDE_EFFECTING),
    )(*[pltpu.with_memory_space_constraint(s, pltpu.HBM) for s in shards],
      *[pltpu.with_memory_space_constraint(t, pltpu.HBM) for t in landing])
    return outs[0], outs[1], outs[2:2 + n], outs[2 + n:2 + 2 * n], outs[-1]


def _gather_wait(send_sems, recv_sems, shards, lands, after):
    n = len(shards)

    def body(*refs):
        srcs, land_refs = refs[:n], refs[n:2 * n]
        sends, recvs = refs[2 * n:2 * n + 2]
        for cp in _peer_copies(srcs, land_refs, sends, recvs):
            cp.wait_send()
            cp.wait_recv()

    hbm = pl.BlockSpec(memory_space=pltpu.HBM)
    sem = pl.BlockSpec(memory_space=pltpu.SEMAPHORE)
    outs = pl.pallas_call(
        body, name="gather_out_weights_wait",
        out_shape=(*[pltpu.HBM(s.shape, s.dtype) for s in shards], *[pltpu.HBM(t.shape, t.dtype) for t in lands]),
        in_specs=[hbm] * (2 * n) + [sem, sem, pl.BlockSpec(memory_space=pl.ANY)],
        out_specs=[hbm] * (2 * n),
        input_output_aliases={i: i for i in range(2 * n)},
        compiler_params=pltpu.CompilerParams(has_side_effects=pltpu.SideEffectType.DATAFLOW_SIDE_EFFECTING),
    )(*shards, *lands, send_sems, recv_sems, after)
    return outs[n:]


def _rs_to_sibling(name, stacks):
    n = len(stacks)

    def body(*refs):
        ins, outs = refs[:n], refs[n:2 * n]
        send_sems, recv_sems = refs[2 * n:]
        me = _my_pos()
        sib = _flip(me, 1)
        sends = []
        for a in range(n):
            for m in range(4):
                target = _flip(sib, 2 * m)
                cp = pltpu.make_async_remote_copy(
                    src_ref=ins[a].at[_index(target)], dst_ref=outs[a].at[m],
                    send_sem=send_sems.at[a * 4 + m], recv_sem=recv_sems.at[a * 4 + m],
                    device_id=sib, device_id_type=MESH)
                cp.start()
                sends.append(cp)
        for cp in sends:
            cp.wait_recv()
        for cp in sends:
            cp.wait_send()

    any_spec = pl.BlockSpec(memory_space=pl.ANY)
    return pl.pallas_call(
        body, name=name,
        out_shape=[jax.ShapeDtypeStruct((4,) + s.shape[1:], s.dtype) for s in stacks],
        in_specs=[any_spec] * n, out_specs=[any_spec] * n,
        scratch_shapes=[pltpu.SemaphoreType.DMA((4 * n,)), pltpu.SemaphoreType.DMA((4 * n,))],
    )(*stacks)


def _chip_copies(srcs, lands, send_sems, recv_sems):
    me = _my_pos()
    return [pltpu.make_async_remote_copy(
        src_ref=srcs[a].at[m - 1], dst_ref=lands[a].at[m - 1],
        send_sem=send_sems.at[a * 3 + m - 1], recv_sem=recv_sems.at[a * 3 + m - 1],
        device_id=_flip(me, 2 * m), device_id_type=MESH) for a in range(len(srcs)) for m in range(1, 4)]


def _rs_chips_start(sums):
    n = len(sums)

    def body(*refs):
        srcs, lands = refs[:n], refs[n:2 * n]
        send_sems, recv_sems = refs[2 * n:2 * n + 2]
        token = refs[-1]
        for cp in _chip_copies(srcs, lands, send_sems, recv_sems):
            cp.start()
        token[...] = jnp.zeros_like(token)

    hbm = pl.BlockSpec(memory_space=pltpu.HBM)
    sem = pl.BlockSpec(memory_space=pltpu.SEMAPHORE)
    held = [pltpu.HBM(s.shape, s.dtype) for s in sums]
    outs = pl.pallas_call(
        body, name="rs_chips_start",
        out_shape=(pltpu.SemaphoreType.DMA((3 * n,)), pltpu.SemaphoreType.DMA((3 * n,)), *held, *held,
                   jax.ShapeDtypeStruct((SUBLANES, 128), F32)),
        in_specs=[hbm] * (2 * n),
        out_specs=(sem, sem, *[hbm] * (2 * n), pl.BlockSpec(memory_space=pltpu.VMEM)),
        input_output_aliases={i: 2 + i for i in range(2 * n)},
        compiler_params=pltpu.CompilerParams(has_side_effects=pltpu.SideEffectType.DATAFLOW_SIDE_EFFECTING),
    )(*[pltpu.with_memory_space_constraint(s, pltpu.HBM) for s in sums],
      *[pltpu.with_memory_space_constraint(lax.empty(s.shape, s.dtype), pltpu.HBM) for s in sums])
    return outs[0], outs[1], outs[2:2 + n], outs[2 + n:2 + 2 * n], outs[-1]


def _rs_chips_wait(send_sems, recv_sems, srcs, lands, after):
    n = len(srcs)

    def body(*refs):
        src_refs, land_refs = refs[:n], refs[n:2 * n]
        sends, recvs = refs[2 * n:2 * n + 2]
        for cp in _chip_copies(src_refs, land_refs, sends, recvs):
            cp.wait_send()
            cp.wait_recv()

    hbm = pl.BlockSpec(memory_space=pltpu.HBM)
    sem = pl.BlockSpec(memory_space=pltpu.SEMAPHORE)
    held = [pltpu.HBM(s.shape, s.dtype) for s in srcs]
    outs = pl.pallas_call(
        body, name="rs_chips_wait", out_shape=(*held, *held),
        in_specs=[hbm] * (2 * n) + [sem, sem, pl.BlockSpec(memory_space=pl.ANY)],
        out_specs=[hbm] * (2 * n),
        input_output_aliases={i: i for i in range(2 * n)},
        compiler_params=pltpu.CompilerParams(has_side_effects=pltpu.SideEffectType.DATAFLOW_SIDE_EFFECTING),
    )(*srcs, *lands, send_sems, recv_sems, after)
    return outs[n:]


def _add_sibling(name, stack, recv, targets):
    _, rows, cols = stack.shape
    tr = min(rows, ADD_ROWS)

    def own_body(t_ref, a_ref, b_ref, o_ref):
        o_ref[...] = a_ref[...] + b_ref[...]

    own = pl.pallas_call(
        own_body, name=name + "_own",
        out_shape=jax.ShapeDtypeStruct((rows, cols), F32),
        grid_spec=pltpu.PrefetchScalarGridSpec(
            num_scalar_prefetch=1, grid=(rows // tr,),
            in_specs=[pl.BlockSpec((None, tr, cols), lambda i, t: (t[0], i, 0)),
                      pl.BlockSpec((None, tr, cols), lambda i, t: (0, i, 0))],
            out_specs=pl.BlockSpec((tr, cols), lambda i, t: (i, 0))),
        compiler_params=_params(("arbitrary",)),
    )(targets, stack, recv)

    def send_body(t_ref, a_ref, b_ref, o_ref):
        o_ref[...] = (a_ref[...] + b_ref[...]).astype(BF16)

    send = pl.pallas_call(
        send_body, name=name + "_send",
        out_shape=jax.ShapeDtypeStruct((3, rows, cols), BF16),
        grid_spec=pltpu.PrefetchScalarGridSpec(
            num_scalar_prefetch=1, grid=(3, rows // tr),
            in_specs=[pl.BlockSpec((None, tr, cols), lambda m, i, t: (t[m + 1], i, 0)),
                      pl.BlockSpec((None, tr, cols), lambda m, i, t: (m + 1, i, 0))],
            out_specs=pl.BlockSpec((None, tr, cols), lambda m, i, t: (m, i, 0))),
        compiler_params=_params(("arbitrary", "arbitrary")),
    )(targets, stack, recv)
    return own, send


def _add_chips(name, own, recv):
    rows, cols = own.shape
    tr = min(rows, ADD_ROWS)

    def body(a_ref, b_ref, o_ref):
        o_ref[...] = ((a_ref[...] + b_ref[0].astype(F32)) + b_ref[1].astype(F32)) + b_ref[2].astype(F32)

    return pl.pallas_call(
        body, name=name,
        out_shape=jax.ShapeDtypeStruct((rows, cols), F32),
        grid=(rows // tr,),
        in_specs=[pl.BlockSpec((tr, cols), lambda i: (i, 0)),
                  pl.BlockSpec((3, tr, cols), lambda i: (0, i, 0))],
        out_specs=pl.BlockSpec((tr, cols), lambda i: (i, 0)),
        compiler_params=_params(("arbitrary",)),
    )(own, recv)


def _allreduce_small(name, v):
    rows, cols = v.shape
    half = rows // 2
    assert rows % (2 * SUBLANES) == 0

    def body(v_ref, out_ref, from_sib, chip_half, from_chips, send_sems, recv_sems):
        me = _my_pos()
        sib = _flip(me, 1)
        mine = pl.ds(pl.multiple_of(me[2] * half, SUBLANES), half)
        theirs = pl.ds(pl.multiple_of((1 - me[2]) * half, SUBLANES), half)

        def copy(k, src, dst, to):
            return pltpu.make_async_remote_copy(src_ref=src, dst_ref=dst, send_sem=send_sems.at[k],
                                                recv_sem=recv_sems.at[k], device_id=to, device_id_type=MESH)

        to_sib = copy(0, v_ref.at[theirs], from_sib, sib)
        to_sib.start()
        to_sib.wait_recv()
        chip_half[...] = v_ref[mine, :] + from_sib[...]
        to_chips = [copy(m, chip_half, from_chips.at[m - 1], _flip(me, 2 * m)) for m in range(1, 4)]
        for cp in to_chips:
            cp.start()
        for cp in to_chips:
            cp.wait_recv()
        my_chip = 2 * me[0] + me[1]
        total = None
        for chip in range(4):
            slot = jnp.maximum(jnp.bitwise_xor(chip, my_chip) - 1, 0)
            part = jnp.where(chip == my_chip, chip_half[...], from_chips[slot])
            total = part if total is None else total + part
        out_ref[mine, :] = total
        swap = copy(4, out_ref.at[mine], out_ref.at[mine], sib)
        swap.start()
        copy(4, out_ref.at[theirs], out_ref.at[theirs], sib).wait_recv()
        for cp in [to_sib, swap] + to_chips:
            cp.wait_send()

    return pl.pallas_call(
        body, name=name, out_shape=jax.ShapeDtypeStruct((rows, cols), F32),
        in_specs=[pl.BlockSpec(memory_space=pltpu.VMEM)],
        out_specs=pl.BlockSpec(memory_space=pltpu.VMEM),
        scratch_shapes=[pltpu.VMEM((half, cols), F32), pltpu.VMEM((half, cols), F32),
                        pltpu.VMEM((3, half, cols), F32),
                        pltpu.SemaphoreType.DMA((5,)), pltpu.SemaphoreType.DMA((5,))],
        compiler_params=_params(None, VMEM_LIMIT),
    )(v)


def _mod_fwd(c_all, w_mod):
    def body(c_ref, w_ref, o_ref):
        c = c_ref[...]
        o_ref[...] = jnp.dot(c * _sigmoid(c), w_ref[...], preferred_element_type=F32,
                             precision=lax.Precision.HIGHEST)

    return pl.pallas_call(
        body, name="mod_fwd", out_shape=jax.ShapeDtypeStruct((N_DEV, w_mod.shape[1]), F32),
    )(c_all, w_mod)


def _mod_bwd(c_all, dmod_all, dmod_cols):
    def body(c_ref, da_ref, dc_ref, gb_ref, gw_ref):
        c = c_ref[...]
        acc = da_ref[0:1, :]
        for b in range(1, N_DEV):
            acc = acc + da_ref[b:b + 1, :]
        gb_ref[...] = acc
        gw_ref[...] = lax.dot_general(c * _sigmoid(c), dc_ref[...], (((0,), (0,)), ((), ())),
                                      preferred_element_type=F32, precision=lax.Precision.HIGHEST)

    return pl.pallas_call(
        body, name="mod_bwd",
        out_shape=[jax.ShapeDtypeStruct((1, dmod_all.shape[1]), F32),
                   jax.ShapeDtypeStruct((c_all.shape[1], dmod_cols.shape[1]), F32)],
    )(c_all, dmod_all, dmod_cols)


def _rope_partner(t):
    lane = lax.broadcasted_iota(jnp.int32, t.shape, 1)
    return jnp.where(lane < ROT_HALF, pltpu.roll(t, HEAD_DIM - ROT_HALF, 1), pltpu.roll(t, ROT_HALF, 1))


def _norm(x, mod, b_mod, g_norm):
    seq = x.shape[0]
    tm = PROJ_ROWS

    def body(x_ref, mod_ref, bmod_ref, g_ref, h_ref):
        xf = x_ref[...]
        rstd = lax.rsqrt(jnp.mean(xf * xf, axis=-1, keepdims=True) + NORM_EPS)
        shift = mod_ref[:, 0:D_MODEL] + bmod_ref[:, 0:D_MODEL]
        scale = mod_ref[:, D_MODEL:2 * D_MODEL] + bmod_ref[:, D_MODEL:2 * D_MODEL]
        h_ref[...] = (((xf * rstd) * g_ref[...]) * (1.0 + scale) + shift).astype(BF16)

    row = pl.BlockSpec((tm, D_MODEL), lambda i: (i, 0))
    const = lambda cols: pl.BlockSpec((1, cols), lambda i: (0, 0))
    return pl.pallas_call(
        body, name="norm", out_shape=jax.ShapeDtypeStruct((seq, D_MODEL), BF16), grid=(seq // tm,),
        in_specs=[row, const(3 * D_MODEL), const(3 * D_MODEL), const(D_MODEL)], out_specs=row,
        compiler_params=_params(("arbitrary",), VMEM_LIMIT),
    )(x, mod, b_mod, g_norm)


def _proj(h, w_in_all, cosf, sinf):
    seq = h.shape[0]
    tm = PROJ_ROWS
    last = seq // tm - 1

    def body(h_ref, w_ref, cos_ref, sin_ref, pf_ref, q_ref, k_ref, v_ref):
        j = pl.program_id(0)

        @pl.when((j < 2) | (j > 4))
        def _():
            pf_ref[...] = _dot(h_ref[...], w_ref[...])

        def heads(dst_ref, rotate, gain):
            for pair in range(N_HEADS // 2):
                both = _dot(h_ref[...], w_ref[:, 2 * pair * HEAD_DIM:2 * (pair + 1) * HEAD_DIM])
                for hh in (2 * pair, 2 * pair + 1):
                    t = both[:, (hh % 2) * HEAD_DIM:(hh % 2 + 1) * HEAD_DIM]
                    if rotate:
                        t = t * cos_ref[...] + _rope_partner(t) * sin_ref[...]
                    dst_ref[hh] = t if gain is None else t * gain

        @pl.when(j == 2)
        def _():
            heads(q_ref, True, ATTN_SCALE)

        @pl.when(j == 3)
        def _():
            heads(k_ref, True, None)

        @pl.when(j == 4)
        def _():
            heads(v_ref, False, None)

    def pf_block(j, i):
        f32_piece = (j < 2) | (j > 4)
        return (jnp.where(f32_piece, i, last), jnp.where(j < 2, j, jnp.where(j < 5, 1, j - 3)))

    def hm_block(piece):
        return lambda j, i: (0, jnp.where(j == piece, i, jnp.where(j < piece, 0, last)), 0)

    hm = jax.ShapeDtypeStruct((N_HEADS, seq, HEAD_DIM), F32)
    hm_spec = lambda piece: pl.BlockSpec((N_HEADS, tm, HEAD_DIM), hm_block(piece))
    row = lambda j, i: (i, 0)
    return pl.pallas_call(
        body, name="proj",
        out_shape=[jax.ShapeDtypeStruct((seq, 5 * D_MODEL), F32), hm, hm, hm],
        grid=(8, seq // tm),
        in_specs=[pl.BlockSpec((tm, D_MODEL), row),
                  pl.BlockSpec((None, D_MODEL, D_MODEL), lambda j, i: (j, 0, 0)),
                  pl.BlockSpec((tm, HEAD_DIM), row), pl.BlockSpec((tm, HEAD_DIM), row)],
        out_specs=[pl.BlockSpec((tm, D_MODEL), pf_block), hm_spec(2), hm_spec(3), hm_spec(4)],
        compiler_params=_params(("arbitrary", "arbitrary"), VMEM_LIMIT),
    )(h, w_in_all, cosf, sinf)


def _shift_down(v, s, head):
    rolled = pltpu.roll(v, s, 0)
    row = lax.broadcasted_iota(jnp.int32, head.shape, 0)
    first = jnp.where(row < s, pltpu.roll(head, s, 0), rolled[:SUBLANES, :])
    return jnp.concatenate([first, rolled[SUBLANES:, :]], axis=0)


def _shift_up(v, s, tail):
    rows = v.shape[0]
    rolled = pltpu.roll(v, rows - s, 0)
    row = lax.broadcasted_iota(jnp.int32, tail.shape, 0)
    last = jnp.where(row >= SUBLANES - s, pltpu.roll(tail, SUBLANES - s, 0), rolled[rows - SUBLANES:, :])
    return jnp.concatenate([rolled[:rows - SUBLANES, :], last], axis=0)


def _doubling(a, b, period, reverse):
    rows = a.shape[0]
    pos = lax.broadcasted_iota(jnp.int32, a.shape, 0) & (period - 1)
    k = 1
    while k < period:
        inside = (pos < period - k) if reverse else (pos >= k)
        shift = rows - k if reverse else k
        a_s = jnp.where(inside, pltpu.roll(a, shift, 0), 1.0)
        b_s = jnp.where(inside, pltpu.roll(b, shift, 0), 0.0)
        b = a * b_s + b
        a = a * a_s
        k *= 2
    return a, b


def _scan(a, b, boundary, reverse, a_scr, b_scr, spread):
    rows = a.shape[0]
    ntile = rows // SUBLANES
    a_scr[...], b_scr[...] = _doubling(a, b, SUBLANES, reverse)
    ends = pl.ds(0 if reverse else SUBLANES - 1, ntile, stride=SUBLANES)
    a_end, x_end = _doubling(a_scr[ends, :], b_scr[ends, :], ntile, reverse)
    x_end = x_end + a_end * boundary
    tile = lax.broadcasted_iota(jnp.int32, x_end.shape, 0)
    if reverse:
        incoming = jnp.where(tile == ntile - 1, boundary, pltpu.roll(x_end, ntile - 1, 0))
        last = x_end[0:1, :]
    else:
        incoming = jnp.where(tile == 0, boundary, pltpu.roll(x_end, 1, 0))
        last = x_end[ntile - 1:ntile, :]
    for s in range(SUBLANES):
        spread[pl.ds(s, ntile, stride=SUBLANES), :] = incoming
    return b_scr[...] + a_scr[...] * spread[...], last


def _conv_taps(xr, head):
    return [_shift_down(xr, 3, head), _shift_down(xr, 2, head), _shift_down(xr, 1, head), xr]


def _rnn_gates(xc, wa, ba, wx, bx, lam, keep):
    xcb = xc.astype(BF16)
    r = _sigmoid(_dot(xcb, wa.astype(BF16)) + ba)
    i = _sigmoid(_dot(xcb, wx.astype(BF16)) + bx)
    softplus = jnp.maximum(-lam, 0.0) + jnp.log(1.0 + jnp.exp(-jnp.abs(lam)))
    cl = -LRU_C * softplus
    log_a = cl * r
    a_raw = jnp.exp(log_a)
    mult_raw = jnp.sqrt(-_expm1_nonpos(2.0 * log_a, a_raw * a_raw))
    live = keep > 0.0
    return r, i, cl, a_raw, mult_raw, jnp.where(live, a_raw, 0.0), jnp.where(live, mult_raw, 1.0), live


def _rnn_specs(seq, rows, time_of):
    per = rows // SUBLANES
    vec = pl.BlockSpec((None, 1, 128), lambda hb, n: (hb, 0, 0))
    mat = pl.BlockSpec((None, 128, 128), lambda hb, n: (hb, 0, 0))
    return [pl.BlockSpec((rows, 128), lambda hb, n: (time_of(n), hb)),
            pl.BlockSpec((SUBLANES, 128), lambda hb, n: (jnp.maximum(time_of(n) * per - 1, 0), hb)),
            pl.BlockSpec((rows, 1), lambda hb, n: (time_of(n), 0)),
            pl.BlockSpec((None, SUBLANES, 128), lambda hb, n: (hb, 0, 0)),
            vec, mat, vec, mat, vec, vec]


def _rnn_fwd(pf, keep, conv_w8, conv_b, w_a, b_a, w_x, b_x, lam):
    seq = pf.shape[0]
    rows = RNN_ROWS

    def body(x_ref, xh_ref, keep_ref, cw_ref, cb_ref, wa_ref, ba_ref, wx_ref, bx_ref, lam_ref, hr_ref,
             carry, a_scr, b_scr, spread):
        n = pl.program_id(1)

        @pl.when(n == 0)
        def _():
            carry[...] = jnp.zeros_like(carry)

        xr = x_ref[...]
        head = jnp.where(n > 0, xh_ref[...], 0.0)
        taps = _conv_taps(xr, head)
        xc = cb_ref[...] + sum(cw_ref[k:k + 1, :] * taps[k] for k in range(4))
        _, i, _, _, _, a, mult, _ = _rnn_gates(xc, wa_ref[...], ba_ref[...], wx_ref[...], bx_ref[...],
                                               lam_ref[...], keep_ref[...])
        h, last = _scan(a, mult * i * xc, carry[0:1, :], False, a_scr, b_scr, spread)
        hr_ref[...] = h
        carry[...] = jnp.broadcast_to(last, carry.shape)

    chunk_f32 = pltpu.VMEM((rows, 128), F32)
    return pl.pallas_call(
        body, name="rnn_fwd",
        out_shape=jax.ShapeDtypeStruct((seq, D_MODEL), F32),
        grid=(RNN_BLOCKS, seq // rows),
        in_specs=_rnn_specs(seq, rows, lambda n: n),
        out_specs=pl.BlockSpec((rows, 128), lambda hb, n: (n, hb)),
        scratch_shapes=[pltpu.VMEM((SUBLANES, 128), F32), chunk_f32, chunk_f32, chunk_f32],
        compiler_params=_params(("arbitrary", "arbitrary"), VMEM_LIMIT),
    )(pf, pf, keep, conv_w8, conv_b, w_a, b_a, w_x, b_x, lam)


def _rnn_bwd(pf, hr, dhr, keep, conv_w8, conv_b, w_a, b_a, w_x, b_x, lam):
    seq = pf.shape[0]
    rows = RNN_ROWS
    nchunk = seq // rows
    per = rows // SUBLANES
    time_of = lambda n: nchunk - 1 - n

    def body(x_ref, xh_ref, keep_ref, cw_ref, cb_ref, wa_ref, ba_ref, wx_ref, bx_ref, lam_ref,
             hr_ref, hrh_ref, dhr_ref,
             dx_ref, gcw_ref, gcb_ref, gwa_ref, gba_ref, gwx_ref, gbx_ref, glam_ref,
             g_carry, dxc_tail, a_scr, b_scr, spread):
        n = pl.program_id(1)
        first_in_time = n == nchunk - 1

        @pl.when(n == 0)
        def _():
            g_carry[...] = jnp.zeros_like(g_carry)
            dxc_tail[...] = jnp.zeros_like(dxc_tail)
            for ref in (gcw_ref, gcb_ref, gwa_ref, gba_ref, gwx_ref, gbx_ref, glam_ref):
                ref[...] = jnp.zeros_like(ref)

        xr = x_ref[...]
        head = jnp.where(first_in_time, 0.0, xh_ref[...])
        taps = _conv_taps(xr, head)
        cw = cw_ref[...]
        xc = cb_ref[...] + sum(cw[k:k + 1, :] * taps[k] for k in range(4))
        wa, wx, lam = wa_ref[...], wx_ref[...], lam_ref[...]
        r, i, cl, a_raw, mult_raw, a, mult, live = _rnn_gates(xc, wa, ba_ref[...], wx, bx_ref[...], lam,
                                                               keep_ref[...])
        h_prev = _shift_down(hr_ref[...], 1, jnp.where(first_in_time, 0.0, hrh_ref[...]))

        row = lax.broadcasted_iota(jnp.int32, xr.shape, 0)
        last = row == rows - 1
        a_next = jnp.where(last, 0.0, pltpu.roll(a, rows - 1, 0))
        g, g_first = _scan(a_next, dhr_ref[...] + jnp.where(last, g_carry[0:1, :], 0.0),
                           jnp.zeros((1, 128), F32), True, a_scr, b_scr, spread)
        g_carry[...] = jnp.broadcast_to(a[0:1, :] * g_first, g_carry.shape)

        da = g * h_prev
        dmult = g * i * xc
        di = g * mult * xc
        dxc = g * mult * i
        dlog_a = jnp.where(live, da * a_raw - dmult * a_raw * a_raw / mult_raw, 0.0)
        dpa = (dlog_a * cl) * r * (1.0 - r)
        dpx = di * i * (1.0 - i)
        glam_ref[...] += jnp.sum(dlog_a * r, axis=0, keepdims=True) * (LRU_C * _sigmoid(-lam))
        xcb, dpab, dpxb = xc.astype(BF16), dpa.astype(BF16), dpx.astype(BF16)
        gwa_ref[...] += _dot_tn(xcb, dpab)
        gwx_ref[...] += _dot_tn(xcb, dpxb)
        gba_ref[...] += jnp.sum(dpa, axis=0, keepdims=True)
        gbx_ref[...] += jnp.sum(dpx, axis=0, keepdims=True)
        dxc = dxc + _dot_nt(dpab, wa.astype(BF16)) + _dot_nt(dpxb, wx.astype(BF16))

        gcb_ref[...] += jnp.sum(dxc, axis=0, keepdims=True)
        for k in range(4):
            gcw_ref[k:k + 1, :] += jnp.sum(dxc * taps[k], axis=0, keepdims=True)
        tail = dxc_tail[...]
        dx = cw[3:4, :] * dxc
        for k in range(3):
            dx = dx + cw[k:k + 1, :] * _shift_up(dxc, 3 - k, tail)
        dx_ref[...] = dx.astype(BF16)
        dxc_tail[...] = dxc[0:SUBLANES, :]

    blk = lambda hb, n: (hb, 0, 0)
    chunk = pl.BlockSpec((rows, 128), lambda hb, n: (time_of(n), hb))
    vec_out = pl.BlockSpec((None, 1, 128), blk)
    mat_out = pl.BlockSpec((None, 128, 128), blk)
    vec_shape = jax.ShapeDtypeStruct((RNN_BLOCKS, 1, 128), F32)
    mat_shape = jax.ShapeDtypeStruct((RNN_BLOCKS, 128, 128), F32)
    return pl.pallas_call(
        body, name="rnn_bwd",
        out_shape=[jax.ShapeDtypeStruct((seq, D_MODEL), BF16),
                   jax.ShapeDtypeStruct((RNN_BLOCKS, SUBLANES, 128), F32), vec_shape,
                   mat_shape, vec_shape, mat_shape, vec_shape, vec_shape],
        grid=(RNN_BLOCKS, nchunk),
        in_specs=_rnn_specs(seq, rows, time_of) + [
            chunk, pl.BlockSpec((SUBLANES, 128), lambda hb, n: (jnp.maximum(time_of(n) * per - 1, 0), hb)), chunk],
        out_specs=[chunk, pl.BlockSpec((None, SUBLANES, 128), blk), vec_out,
                   mat_out, vec_out, mat_out, vec_out, vec_out],
        scratch_shapes=[pltpu.VMEM((SUBLANES, 128), F32), pltpu.VMEM((SUBLANES, 128), F32)]
                       + [pltpu.VMEM((rows, 128), F32)] * 3,
        compiler_params=_params(("arbitrary", "arbitrary"), VMEM_LIMIT),
    )(pf, pf, keep, conv_w8, conv_b, w_a, b_a, w_x, b_x, lam, hr, hr, dhr)


def _unit_rows(dil, r, j):
    start = j * KEY_BLOCK * dil + r
    return pl.ds(start, KEY_BLOCK) if dil == 1 else pl.ds(start, KEY_BLOCK, stride=dil)


def _attn_fwd(q, k, v):
    nh, seq, _ = q.shape
    nchunk = seq // SPAN
    nblk = SPAN // KEY_BLOCK
    wide = DILATIONS[-1]

    def body(q_ref, k_ref, v_ref, kp_ref, vp_ref, o_ref, l1_ref, l4_ref, l16_ref,
             acc, m_s, l_s, q16, k16, v16, k16p, v16p, acc16, m16, l16, tmp):
        n = pl.program_id(1)
        qi = lax.broadcasted_iota(jnp.int32, (KEY_BLOCK, KEY_BLOCK), 0)
        ki = lax.broadcasted_iota(jnp.int32, (KEY_BLOCK, KEY_BLOCK), 1)
        bias_own = jnp.where(ki <= qi, 0.0, NEG_INF)
        bias_before = jnp.where(ki >= qi, 0.0, NEG_INF)
        bias_mid = jnp.concatenate([bias_before, bias_own], axis=1)
        bias_first = jnp.concatenate([jnp.where(n > 0, bias_before, NEG_INF), bias_own], axis=1)
        ones = jnp.ones((2 * KEY_BLOCK, HEAD_DIM), BF16)
        diag = qi == ki

        @pl.when(n == 0)
        def _():
            k16p[...] = jnp.zeros_like(k16p)
            v16p[...] = jnp.zeros_like(v16p)

        def unit(qf, kpb, kb, vpb, vb, bias, state, rows, first):
            acc_r, m_r, l_r = state
            kcat = jnp.concatenate([kpb, kb], axis=0)
            vaug = jnp.concatenate([jnp.concatenate([vpb, vb], axis=0), ones], axis=1)
            s = _dot_nt(qf.astype(BF16), kcat) + bias
            mx = jnp.max(s, axis=-1, keepdims=True)
            if first:
                m_new = jnp.broadcast_to(mx, (KEY_BLOCK, HEAD_DIM))
            else:
                m_old = m_r[rows, :]
                m_new = jnp.maximum(m_old, mx)
            pv = _dot(jnp.exp(s - jnp.concatenate([m_new, m_new], axis=1)).astype(BF16), vaug)
            if first:
                acc_r[rows, :] = pv[:, :HEAD_DIM]
                l_r[rows, :] = pv[:, HEAD_DIM:]
            else:
                alpha = jnp.exp(m_old - m_new)
                acc_r[rows, :] = alpha * acc_r[rows, :] + pv[:, :HEAD_DIM]
                l_r[rows, :] = alpha * l_r[rows, :] + pv[:, HEAD_DIM:]
            m_r[rows, :] = m_new

        for gi, dil in enumerate(DILATIONS[:-1]):
            nb = nblk // dil
            for r in range(dil):
                prow = _unit_rows(dil, r, nb - 1)
                kpb, vpb = kp_ref[prow, :].astype(BF16), vp_ref[prow, :].astype(BF16)
                for j in range(nb):
                    rows = _unit_rows(dil, r, j)
                    kb, vb = k_ref[rows, :].astype(BF16), v_ref[rows, :].astype(BF16)
                    unit(q_ref[rows, :], kpb, kb, vpb, vb, bias_first if j == 0 else bias_mid,
                         (acc, m_s, l_s), rows, gi == 0)
                    kpb, vpb = kb, vb

        for src, dst in ((q_ref, q16), (k_ref, k16), (v_ref, v16), (acc, acc16), (m_s, m16), (l_s, l16)):
            _to_residue_major(src, tmp, dst)
        for r in range(wide):
            rows = pl.ds(r * KEY_BLOCK, KEY_BLOCK)
            unit(q16[rows, :], k16p[rows, :].astype(BF16), k16[rows, :].astype(BF16), v16p[rows, :].astype(BF16),
                 v16[rows, :].astype(BF16), bias_first, (acc16, m16, l16), rows, False)
        k16p[...] = k16[...]
        v16p[...] = v16[...]

        den = l16[...]
        acc16[...] = acc16[...] * (1.0 / den)
        m16[...] = m16[...] + jnp.log(den)
        _from_residue_major(acc16, tmp, o_ref, False)
        _from_residue_major(m16, tmp, m_s, False)

        def lse_row(ref, rows):
            return jnp.sum(jnp.where(diag, ref[rows, :], 0.0), axis=0, keepdims=True)

        for dil, out in zip(DILATIONS[:-1], (l1_ref, l4_ref)):
            nb = nblk // dil
            for r in range(dil):
                for j in range(nb):
                    out[r * nb + j:r * nb + j + 1, :] = lse_row(m_s, _unit_rows(dil, r, j))
        for r in range(wide):
            l16_ref[r:r + 1, :] = lse_row(m16, pl.ds(r * KEY_BLOCK, KEY_BLOCK))

    blk = pl.BlockSpec((None, SPAN, HEAD_DIM), lambda h, n: (h, n, 0))
    pblk = pl.BlockSpec((None, SPAN, HEAD_DIM), lambda h, n: (h, jnp.maximum(n - 1, 0), 0))
    lblk = pl.BlockSpec((None, nblk, KEY_BLOCK), lambda h, n: (h, n, 0))
    lshape = jax.ShapeDtypeStruct((nh, seq // KEY_BLOCK, KEY_BLOCK), F32)
    o, l1, l4, l16 = pl.pallas_call(
        body, name="attn_fwd",
        out_shape=[jax.ShapeDtypeStruct((nh, seq, HEAD_DIM), F32), lshape, lshape, lshape],
        grid=(nh, nchunk), in_specs=[blk, blk, blk, pblk, pblk], out_specs=[blk, lblk, lblk, lblk],
        scratch_shapes=[pltpu.VMEM((SPAN, HEAD_DIM), F32)] * 12,
        compiler_params=_params(("arbitrary", "arbitrary"), VMEM_LIMIT),
    )(q, k, v, k, v)
    return o, (l1, l4, l16)


def _to_residue_major(src, tmp, dst):
    quarter = SPAN // 4
    for r4 in range(4):
        tmp[r4 * quarter:(r4 + 1) * quarter, :] = src[pl.ds(r4, quarter, stride=4), :]
    for r4 in range(4):
        for rp in range(4):
            r = r4 + 4 * rp
            dst[r * KEY_BLOCK:(r + 1) * KEY_BLOCK, :] = tmp[pl.ds(r4 * quarter + rp, KEY_BLOCK, stride=4), :]


def _from_residue_major(src, tmp, dst, add):
    quarter = SPAN // 4
    for r4 in range(4):
        for rp in range(4):
            r = r4 + 4 * rp
            tmp[pl.ds(r4 * quarter + rp, KEY_BLOCK, stride=4), :] = src[r * KEY_BLOCK:(r + 1) * KEY_BLOCK, :]
    for r4 in range(4):
        rows = pl.ds(r4, quarter, stride=4)
        part = tmp[r4 * quarter:(r4 + 1) * quarter, :]
        dst[rows, :] = dst[rows, :] + part if add else part


def _attn_bwd(q, k, v, do, o, lses, cosf, sinf):
    nh, seq, _ = q.shape
    nchunk = seq // SPAN
    nblk = SPAN // KEY_BLOCK
    wide = DILATIONS[-1]
    assert SPAN == wide * KEY_BLOCK

    def body(q_ref, k_ref, v_ref, do_ref, o_ref, kp_ref, vp_ref, l1_ref, l4_ref, l16_ref,
             cos_ref, sin_ref, cosp_ref, sinp_ref, dq_ref, dk_ref, dv_ref,
             dq_acc, dkc_acc, dvc_acc, dkp_acc, dvp_acc, q16, k16, v16, do16, o16, k16p, v16p,
             dq16, dkc16, dvc16, dkp16, dvp16, tmp, pt_s, ds_s, kcat_s, qb_s, dob_s):
        n = pl.program_id(1)
        ki = lax.broadcasted_iota(jnp.int32, (KEY_BLOCK, KEY_BLOCK), 0)
        qi = lax.broadcasted_iota(jnp.int32, (KEY_BLOCK, KEY_BLOCK), 1)
        bias_own = jnp.where(ki <= qi, 0.0, NEG_INF)
        bias_before = jnp.where(ki >= qi, 0.0, NEG_INF)
        bias_mid = jnp.concatenate([bias_before, bias_own], axis=0)
        bias_first = jnp.concatenate([jnp.where(n > 0, bias_before, NEG_INF), bias_own], axis=0)
        ones8 = jnp.ones((SUBLANES, HEAD_DIM), BF16)

        def row_dot(a, b):
            prod = a * b
            hi = prod.astype(BF16)
            lo = (prod - hi.astype(F32)).astype(BF16)
            return (_dot_nt(ones8, hi) + _dot_nt(ones8, lo))[0:1, :]

        def group(units, srcs, before, l_ref, accs):
            src_q, src_do, src_o, src_k, src_v = srcs
            before_k, before_v = before
            acc_q, acc_kc, acc_vc, acc_kp, acc_vp = accs
            kb = vb = None
            for u, (rows, prow, outside, lrow, _) in enumerate(units):
                dof = src_do[rows, :]
                qb, dob = src_q[rows, :].astype(BF16), dof.astype(BF16)
                kpb, vpb = (before_k[prow, :].astype(BF16), before_v[prow, :].astype(BF16)) if outside else (kb, vb)
                kb, vb = src_k[rows, :].astype(BF16), src_v[rows, :].astype(BF16)
                kcat = jnp.concatenate([kpb, kb], axis=0)
                vcat = jnp.concatenate([vpb, vb], axis=0)
                bias = bias_first if outside else bias_mid
                pt = jnp.exp(_dot_nt(kcat, qb) + bias - l_ref[lrow:lrow + 1, :])
                dst = pt * (_dot_nt(vcat, dob) - row_dot(dof, src_o[rows, :]))
                pt_s[u], ds_s[u], kcat_s[u], qb_s[u], dob_s[u] = pt.astype(BF16), dst.astype(BF16), kcat, qb, dob
            for u, (rows, _, _, _, _) in enumerate(units):
                acc_q[rows, :] += _dot_tn(ds_s[u], kcat_s[u])
            for u, (rows, prow, outside, _, nxt) in enumerate(units):
                dk = _dot(ds_s[u, KEY_BLOCK:, :], qb_s[u])
                dv = _dot(pt_s[u, KEY_BLOCK:, :], dob_s[u])
                if nxt is not None:
                    dk = dk + _dot(ds_s[nxt, :KEY_BLOCK, :], qb_s[nxt])
                    dv = dv + _dot(pt_s[nxt, :KEY_BLOCK, :], dob_s[nxt])
                acc_kc[rows, :] += dk
                acc_vc[rows, :] += dv
                if outside:
                    acc_kp[prow, :] += _dot(ds_s[u, :KEY_BLOCK, :], qb_s[u])
                    acc_vp[prow, :] += _dot(pt_s[u, :KEY_BLOCK, :], dob_s[u])

        @pl.when(n == 0)
        def _():
            for ref in (dkp_acc, dvp_acc, dkp16, dvp16, k16p, v16p):
                ref[...] = jnp.zeros_like(ref)

        @pl.when(n < nchunk)
        def _():
            for ref in (dq_acc, dkc_acc, dvc_acc, dq16, dkc16, dvc16):
                ref[...] = jnp.zeros_like(ref)
            for src, dst in ((q_ref, q16), (k_ref, k16), (v_ref, v16), (do_ref, do16), (o_ref, o16)):
                _to_residue_major(src, tmp, dst)
            natural = (q_ref, do_ref, o_ref, k_ref, v_ref)
            for dil, l_ref in zip(DILATIONS[:-1], (l1_ref, l4_ref)):
                nb = nblk // dil
                units = [(_unit_rows(dil, r, j), _unit_rows(dil, r, (j - 1) % nb), j == 0, r * nb + j,
                          r * nb + j + 1 if j + 1 < nb else None) for r in range(dil) for j in range(nb)]
                group(units, natural, (kp_ref, vp_ref), l_ref, (dq_acc, dkc_acc, dvc_acc, dkp_acc, dvp_acc))
            blocks = [pl.ds(r * KEY_BLOCK, KEY_BLOCK) for r in range(wide)]
            group([(rows, rows, True, r, None) for r, rows in enumerate(blocks)], (q16, do16, o16, k16, v16),
                  (k16p, v16p), l16_ref, (dq16, dkc16, dvc16, dkp16, dvp16))
            _from_residue_major(dq16, tmp, dq_acc, True)
            dq = dq_acc[...]
            dq_ref[...] = ((dq * cos_ref[...] - _rope_partner(dq) * sin_ref[...]) * ATTN_SCALE).astype(BF16)

        @pl.when(n > 0)
        def _():
            _from_residue_major(dkp16, tmp, dkp_acc, True)
            _from_residue_major(dvp16, tmp, dvp_acc, True)
            dk = dkp_acc[...]
            dk_ref[...] = (dk * cosp_ref[...] - _rope_partner(dk) * sinp_ref[...]).astype(BF16)
            dv_ref[...] = dvp_acc[...].astype(BF16)

        @pl.when(n < nchunk)
        def _():
            for src, dst in ((dkc_acc, dkp_acc), (dvc_acc, dvp_acc), (dkc16, dkp16), (dvc16, dvp16),
                             (k16, k16p), (v16, v16p)):
                dst[...] = src[...]

    last = nchunk - 1
    cur = lambda h, n: (h, jnp.minimum(n, last), 0)
    prev = lambda h, n: (h, jnp.clip(n - 1, 0, last), 0)
    blk = lambda idx: pl.BlockSpec((None, SPAN, HEAD_DIM), idx)
    lblk = pl.BlockSpec((None, nblk, KEY_BLOCK), cur)
    tab = pl.BlockSpec((SPAN, HEAD_DIM), lambda h, n: (jnp.minimum(n, last), 0))
    tabp = pl.BlockSpec((SPAN, HEAD_DIM), lambda h, n: (jnp.clip(n - 1, 0, last), 0))
    out_q = pl.BlockSpec((SPAN, HEAD_DIM), lambda h, n: (jnp.minimum(n, last), h))
    out_kv = pl.BlockSpec((SPAN, HEAD_DIM), lambda h, n: (jnp.clip(n - 1, 0, last), h))
    shape = jax.ShapeDtypeStruct((seq, nh * HEAD_DIM), BF16)
    return pl.pallas_call(
        body, name="attn_bwd", out_shape=[shape, shape, shape], grid=(nh, nchunk + 1),
        in_specs=[blk(cur)] * 5 + [blk(prev)] * 2 + [lblk] * 3 + [tab, tab, tabp, tabp],
        out_specs=[out_q, out_kv, out_kv],
        scratch_shapes=[pltpu.VMEM((SPAN, HEAD_DIM), F32)] * 18
                       + [pltpu.VMEM((nblk, 2 * KEY_BLOCK, HEAD_DIM), BF16)] * 3
                       + [pltpu.VMEM((nblk, KEY_BLOCK, HEAD_DIM), BF16)] * 2,
        compiler_params=_params(("arbitrary", "arbitrary"), VMEM_LIMIT),
    )(q, k, v, do, o, k, v, *lses, cosf, sinf, cosf, sinf)


def _hub(x, tgt, hr, pf, o_hm, mod, b_mod, b_gate, g_final, w_out_rnn, w_out_attn, w_o):
    seq = x.shape[0]
    tm = HUB_ROWS
    nsteps = seq // tm

    def body(x_ref, t_ref, hr_ref, zr_ref, za_ref, gr_ref, ga_ref, o_ref, mod_ref, bmod_ref, bg_ref, gf_ref,
             wr_hbm, wa_hbm, wo_hbm,
             dx2_ref, dhr_ref, dzr_ref, do_ref, dza_ref, dgr_ref, dga_ref,
             ur_ref, dyr_ref, ua_ref, dya_ref, mg_ref, dmo_ref,
             ggf_ref, gbg_ref, dgate_ref, loss_ref,
             wr, wa, wo, sem):
        step = pl.program_id(0)

        @pl.when(step == 0)
        def _():
            for src, dst in ((wr_hbm, wr), (wa_hbm, wa), (wo_hbm, wo)):
                cp = pltpu.make_async_copy(src, dst, sem)
                cp.start()
                cp.wait()
            for ref in (ggf_ref, gbg_ref, dgate_ref, loss_ref):
                ref[...] = jnp.zeros_like(ref)

        gate = mod_ref[:, 2 * D_MODEL:] + bmod_ref[:, 2 * D_MODEL:]
        gfin = gf_ref[...]
        hr_t, zr, za = hr_ref[...], zr_ref[...], za_ref[...]
        o = jnp.concatenate([o_ref[hh] for hh in range(N_HEADS)], axis=1)
        sig_zr, sig_za = _sigmoid(zr), _sigmoid(za)
        silu_zr, silu_za = zr * sig_zr, za * sig_za
        u_rnn = (hr_t * silu_zr).astype(BF16)
        u_attn = (o * silu_za).astype(BF16)
        y_rnn = _dot(u_rnn, wr[...])
        y_attn = _dot(u_attn, wa[...])
        sr = _sigmoid(gr_ref[...] + bg_ref[:, :D_MODEL])
        sa = _sigmoid(ga_ref[...] + bg_ref[:, D_MODEL:])
        merged = (sr * y_rnn + sa * y_attn).astype(BF16)
        mo = _dot(merged, wo[...])
        x2 = x_ref[...] + gate * mo
        rstd = lax.rsqrt(jnp.mean(x2 * x2, axis=-1, keepdims=True) + NORM_EPS)
        xn = x2 * rstd
        err = xn * gfin - t_ref[...]
        loss_ref[...] += 0.5 * jnp.sum(jnp.sum(err * err, axis=-1, keepdims=True) * (1.0 / D_MODEL),
                                       axis=0, keepdims=True)

        dy = err * (1.0 / D_MODEL)
        ggf_ref[...] += jnp.sum(dy * xn, axis=0, keepdims=True)
        dxn = dy * gfin
        dx2 = rstd * (dxn - xn * jnp.mean(dxn * xn, axis=-1, keepdims=True))
        dx2_ref[...] = dx2
        dgate_ref[...] += jnp.sum(dx2 * mo, axis=0, keepdims=True)
        dmo = (dx2 * gate).astype(BF16)
        dmerged = _dot_nt(dmo, wo[...])
        mg_ref[...] = merged
        dmo_ref[...] = dmo
        dy_rnn = (dmerged * sr).astype(BF16)
        dy_attn = (dmerged * sa).astype(BF16)
        dg_r = dmerged * y_rnn * sr * (1.0 - sr)
        dg_a = dmerged * y_attn * sa * (1.0 - sa)
        dgr_ref[...] = dg_r.astype(BF16)
        dga_ref[...] = dg_a.astype(BF16)
        gbg_ref[:, :D_MODEL] += jnp.sum(dg_r, axis=0, keepdims=True)
        gbg_ref[:, D_MODEL:] += jnp.sum(dg_a, axis=0, keepdims=True)
        du_rnn = _dot_nt(dy_rnn, wr[...])
        du_attn = _dot_nt(dy_attn, wa[...])
        ur_ref[...] = u_rnn
        dyr_ref[...] = dy_rnn
        ua_ref[...] = u_attn
        dya_ref[...] = dy_attn
        dhr_ref[...] = du_rnn * silu_zr
        dzr_ref[...] = (du_rnn * hr_t * (sig_zr * (1.0 + zr * (1.0 - sig_zr)))).astype(BF16)
        dza_ref[...] = (du_attn * o * (sig_za * (1.0 + za * (1.0 - sig_za)))).astype(BF16)
        d_o = du_attn * silu_za
        for hh in range(N_HEADS):
            do_ref[hh] = d_o[:, hh * HEAD_DIM:(hh + 1) * HEAD_DIM]

    row = pl.BlockSpec((tm, D_MODEL), lambda i: (i, 0))
    piece = lambda slot: pl.BlockSpec((tm, D_MODEL), lambda i: (i, slot))
    hm = pl.BlockSpec((N_HEADS, tm, HEAD_DIM), lambda i: (0, i, 0))
    const = lambda cols: pl.BlockSpec((1, cols), lambda i: (0, 0))
    any_spec = pl.BlockSpec(memory_space=pl.ANY)
    act_f32 = jax.ShapeDtypeStruct((seq, D_MODEL), F32)
    act_bf16 = jax.ShapeDtypeStruct((seq, D_MODEL), BF16)
    return pl.pallas_call(
        body, name="hub",
        out_shape=[act_f32, act_f32, act_bf16, jax.ShapeDtypeStruct((N_HEADS, seq, HEAD_DIM), F32),
                   act_bf16, act_bf16, act_bf16] + [act_bf16] * 6 + [
                   jax.ShapeDtypeStruct((1, D_MODEL), F32), jax.ShapeDtypeStruct((1, 2 * D_MODEL), F32),
                   jax.ShapeDtypeStruct((1, D_MODEL), F32), jax.ShapeDtypeStruct((1, 1), F32)],
        grid=(nsteps,),
        in_specs=[row, row, row, piece(1), piece(2), piece(3), piece(4), hm,
                  const(3 * D_MODEL), const(3 * D_MODEL), const(2 * D_MODEL), const(D_MODEL),
                  any_spec, any_spec, any_spec],
        out_specs=[row, row, row, hm, row, row, row] + [row] * 6 + [
                   const(D_MODEL), const(2 * D_MODEL), const(D_MODEL), const(1)],
        scratch_shapes=[pltpu.VMEM((D_MODEL, D_MODEL), BF16)] * 3 + [pltpu.SemaphoreType.DMA],
        compiler_params=_params(("arbitrary",), VMEM_LIMIT),
    )(x, tgt, hr, pf, pf, pf, pf, o_hm, mod, b_mod, b_gate, g_final, w_out_rnn, w_out_attn, w_o)


def _pair_grads(name, lefts, rights):
    n = len(rights)
    shared = len(lefts) == 1
    seq = rights[0].shape[0]
    tk = WGRAD_ROWS
    nk = seq // tk

    def body(*refs):
        l_refs, r_refs, out_ref = refs[:len(lefts)], refs[len(lefts):len(lefts) + n], refs[len(lefts) + n]
        j, kk = pl.program_id(0), pl.program_id(1)

        @pl.when(kk == 0)
        def _():
            out_ref[...] = jnp.zeros_like(out_ref)

        for m in range(n):
            @pl.when(j == m)
            def _(m=m):
                out_ref[...] += _dot_tn(l_refs[0 if shared else m][...], r_refs[m][...])

    def spec(m):
        return pl.BlockSpec((tk, D_MODEL), lambda j, kk: (jnp.where(j == m, kk, jnp.where(j < m, 0, nk - 1)), 0))

    left_specs = [pl.BlockSpec((tk, D_MODEL), lambda j, kk: (kk, 0))] if shared else [spec(m) for m in range(n)]
    return pl.pallas_call(
        body, name=name,
        out_shape=jax.ShapeDtypeStruct((n, D_MODEL, D_MODEL), F32),
        grid=(n, nk),
        in_specs=left_specs + [spec(m) for m in range(n)],
        out_specs=pl.BlockSpec((None, D_MODEL, D_MODEL), lambda j, kk: (j, 0, 0)),
        compiler_params=_params(("arbitrary", "arbitrary"), VMEM_LIMIT),
    )(*lefts, *rights)


def _dh_dx(pieces, w_in_all, x, dx2, mod, b_mod, g_norm):
    seq = x.shape[0]
    tm = DX_ROWS

    def body(*refs):
        p_refs = refs[:8]
        w_hbm, x_ref, dx2_ref, mod_ref, bmod_ref, g_ref = refs[8:14]
        gx_ref, dshift_ref, dscale_ref, ggn_ref, w_scr, sem = refs[14:]
        step = pl.program_id(0)

        @pl.when(step == 0)
        def _():
            cp = pltpu.make_async_copy(w_hbm, w_scr, sem)
            cp.start()
            cp.wait()
            for ref in (dshift_ref, dscale_ref, ggn_ref):
                ref[...] = jnp.zeros_like(ref)

        dh = _dot_nt(p_refs[0][...], w_scr[0])
        for j in range(1, 8):
            dh = dh + _dot_nt(p_refs[j][...], w_scr[j])
        scale1 = 1.0 + mod_ref[:, D_MODEL:2 * D_MODEL] + bmod_ref[:, D_MODEL:2 * D_MODEL]
        g = g_ref[...]
        xf = x_ref[...]
        rstd_t = lax.rsqrt(jnp.mean(xf * xf, axis=-1, keepdims=True) + NORM_EPS)
        xn = xf * rstd_t
        dshift_ref[...] += jnp.sum(dh, axis=0, keepdims=True)
        dscale_ref[...] += jnp.sum(dh * (xn * g), axis=0, keepdims=True)
        ggn_ref[...] += jnp.sum(dh * scale1 * xn, axis=0, keepdims=True)
        dxn = dh * (g * scale1)
        gx_ref[...] = rstd_t * (dxn - xn * jnp.mean(dxn * xn, axis=-1, keepdims=True)) + dx2_ref[...]

    row = pl.BlockSpec((tm, D_MODEL), lambda i: (i, 0))
    const = lambda cols: pl.BlockSpec((1, cols), lambda i: (0, 0))
    vec = jax.ShapeDtypeStruct((1, D_MODEL), F32)
    return pl.pallas_call(
        body, name="dh_dx",
        out_shape=[jax.ShapeDtypeStruct((seq, D_MODEL), F32), vec, vec, vec],
        grid=(seq // tm,),
        in_specs=[row] * 8 + [pl.BlockSpec(memory_space=pl.ANY), row, row,
                              const(3 * D_MODEL), const(3 * D_MODEL), const(D_MODEL)],
        out_specs=[row, const(D_MODEL), const(D_MODEL), const(D_MODEL)],
        scratch_shapes=[pltpu.VMEM((8, D_MODEL, D_MODEL), BF16), pltpu.SemaphoreType.DMA],
        compiler_params=_params(("arbitrary",), VMEM_LIMIT),
    )(*pieces, w_in_all, x, dx2, mod, b_mod, g_norm)


def _adamw(name, w, g, m, v):
    rows, cols = w.shape
    tr = rows if rows <= 256 else 256

    def body(w_ref, g_ref, m_ref, v_ref, d_ref, nm_ref, nv_ref):
        gv = g_ref[...]
        nm = ADAM_B1 * m_ref[...] + (1.0 - ADAM_B1) * gv
        nv = ADAM_B2 * v_ref[...] + (1.0 - ADAM_B2) * (gv * gv)
        m_hat = nm / (1.0 - ADAM_B1 ** ADAM_STEP)
        v_hat = nv / (1.0 - ADAM_B2 ** ADAM_STEP)
        d_ref[...] = -ADAM_LR * (m_hat / (jnp.sqrt(v_hat) + ADAM_EPS) + ADAM_WD * w_ref[...])
        nm_ref[...] = nm
        nv_ref[...] = nv

    spec = pl.BlockSpec((tr, cols), lambda i: (i, 0))
    shape = jax.ShapeDtypeStruct((rows, cols), F32)
    return pl.pallas_call(
        body, name=name, out_shape=[shape, shape, shape], grid=(rows // tr,),
        in_specs=[spec] * 4, out_specs=[spec] * 3,
        compiler_params=_params(("arbitrary",)),
    )(w, g, m, v)


def kernel(x, c, positions, g_norm, w_mod, b_mod, w_in, b_gate, conv_w, conv_b, w_a, b_a, w_x, b_x, lam, w_out_rnn, w_out_attn, w_o, g_final, loss_target, m_g_norm, m_w_mod, m_b_mod, m_w_in, m_b_gate, m_conv_w, m_conv_b, m_w_a, m_b_a, m_w_x, m_b_x, m_lam, m_w_out_rnn, m_w_out_attn, m_w_o, m_g_final, v_g_norm, v_w_mod, v_b_mod, v_w_in, v_b_gate, v_conv_w, v_conv_b, v_w_a, v_b_a, v_w_x, v_b_x, v_lam, v_w_out_rnn, v_w_out_attn, v_w_o, v_g_final):
    seq = x.shape[1]
    me = _index(_my_pos())
    xs, tgt = x[0], loss_target[0]

    pos = positions[0].astype(F32)[:, None]
    inv_freq = ROPE_THETA ** (-jnp.arange(0, 2 * ROT_HALF, 2, dtype=F32) / (2 * ROT_HALF))
    ang = pos * inv_freq
    rest = HEAD_DIM - 2 * ROT_HALF
    cosf = jnp.concatenate([jnp.cos(ang), jnp.cos(ang), jnp.ones((seq, rest), F32)], axis=1)
    sinf = jnp.concatenate([-jnp.sin(ang), jnp.sin(ang), jnp.zeros((seq, rest), F32)], axis=1)
    keep = (positions[0] != 0).astype(F32)[:, None]

    (w_in_all,) = _ag_big("gather_weights", [w_in[0].astype(BF16)])
    w_in_all, late = lax.optimization_barrier(
        (w_in_all, [w_out_rnn[0].astype(BF16), w_out_attn[0].astype(BF16), w_o[0].astype(BF16)]))
    late_sends, late_recvs, late_shards, late_lands, late_token = _gather_start(late, me)
    conv_w8 = _ag_small("gather_conv_w", jnp.pad(conv_w[0], ((0, SUBLANES - 4), (0, 0))))
    c_all = _ag_small("gather_c", jnp.broadcast_to(c, (SUBLANES, D_MODEL)))[:, 0, :]
    mod_cols = w_mod.shape[2]
    mod_part = _ag_small("gather_mod", _mod_fwd(c_all, w_mod[0]))
    mod = lax.dynamic_index_in_dim(mod_part, me, axis=1, keepdims=False).reshape(1, N_DEV * mod_cols)
    mod = mod + late_token[0:1, 0:1]

    blocks = lambda t: t.reshape(RNN_BLOCKS, 1, 128)
    rnn_params = (conv_w8, blocks(conv_b), w_a[0], blocks(b_a), w_x[0], blocks(b_x), blocks(lam))

    h = _norm(xs, mod, b_mod, g_norm)
    pf, q, k, v = _proj(h, w_in_all, cosf, sinf)
    hr = _rnn_fwd(pf, keep, *rnn_params)
    o, lses = _attn_fwd(q, k, v)

    w_or_all, w_oa_all, w_o_all = (t.reshape(D_MODEL, D_MODEL) for t in _gather_wait(
        late_sends, late_recvs, late_shards, late_lands, o))
    (dx2, dhr, dz_rnn, d_o, dz_attn, dg_r, dg_a, u_rnn, dy_rnn, u_attn, dy_attn, merged, dmo,
     gp_g_final, gp_b_gate, dgate, loss_part) = _hub(
        xs, tgt, hr, pf, o, mod, b_mod, b_gate, g_final.reshape(1, D_MODEL), w_or_all, w_oa_all, w_o_all)
    gp_w_or, gp_w_oa, gp_w_o = _pair_grads("out_grads", [u_rnn, u_attn, merged], [dy_rnn, dy_attn, dmo])
    dq, dk, dv = _attn_bwd(q, k, v, d_o, o, lses, cosf, sinf)
    dx_rnn, gp_conv_w, gp_conv_b, gp_w_a, gp_b_a, gp_w_x, gp_b_x, gp_lam = _rnn_bwd(pf, hr, dhr, keep, *rnn_params)
    pieces = [dx_rnn, dz_rnn, dq, dk, dv, dz_attn, dg_r, dg_a]
    gp_w_in = _pair_grads("w_in_grad", [h], pieces)

    stacks = [gp_w_in, gp_w_or.reshape(N_DEV, 128, D_MODEL), gp_w_oa.reshape(N_DEV, 128, D_MODEL),
              gp_w_o.reshape(N_DEV, 128, D_MODEL)]
    from_sib = _rs_to_sibling("rs_sibling", stacks)
    targets = jnp.bitwise_xor(me, 2 * jnp.arange(4, dtype=jnp.int32)).astype(jnp.int32)
    sums = [_add_sibling("rs_add_sibling_%d" % a, s_, r_, targets) for a, (s_, r_) in enumerate(zip(stacks, from_sib))]
    send_sems, recv_sems, sent, landing, token = _rs_chips_start([send for _, send in sums])

    mod_after = mod + token[0:1, 0:1]
    grad_x, dshift, dscale, gp_g_norm = _dh_dx(pieces, w_in_all, xs, dx2, mod_after, b_mod, g_norm)

    dmod = jnp.concatenate([dshift, dscale, dgate], axis=1)
    dmod_all = _ag_small("gather_dmod", jnp.broadcast_to(dmod, (SUBLANES, 3 * D_MODEL)))[:, 0, :]
    dmod_cols = lax.dynamic_slice_in_dim(dmod_all, me * mod_cols, mod_cols, axis=1)
    g_b_mod, g_w_mod = _mod_bwd(c_all, dmod_all, dmod_cols)

    flat = lambda t: t.reshape(-1, 128)
    small = [flat(gp_g_norm), flat(gp_b_gate), flat(gp_conv_b), flat(gp_b_a), flat(gp_b_x), flat(gp_lam),
             flat(gp_g_final), flat(gp_conv_w), jnp.broadcast_to(loss_part, (SUBLANES, 128)),
             flat(gp_w_a), flat(gp_w_x)]
    sizes = [t.shape[0] for t in small]
    small.append(jnp.zeros((-sum(sizes) % (2 * SUBLANES), 128), F32))
    total = _allreduce_small("allreduce_small_grads", jnp.concatenate(small, axis=0))
    offs = [sum(sizes[:i]) for i in range(len(sizes))]
    (g_g_norm, g_b_gate, g_conv_b, g_b_a, g_b_x, g_lam, g_g_final, g_conv_w_all, loss_rows, g_w_a, g_w_x) = (
        total[o_:o_ + s_] for o_, s_ in zip(offs, sizes))
    loss = loss_rows[0, 0]
    g_conv_w = lax.dynamic_index_in_dim(g_conv_w_all.reshape(RNN_BLOCKS, SUBLANES, 128), me, axis=0,
                                        keepdims=False)[:4]

    from_chips = _rs_chips_wait(send_sems, recv_sems, sent, landing, total)
    g_w_in, g_w_or, g_w_oa, g_w_o = (
        _add_chips("rs_add_chips_%d" % a, own, r_) for a, ((own, _), r_) in enumerate(zip(sums, from_chips)))

    weights = [
        ("g_norm", g_norm, g_g_norm, m_g_norm, v_g_norm, (SUBLANES, 128)),
        ("w_mod", w_mod, g_w_mod, m_w_mod, v_w_mod, (D_MODEL, mod_cols)),
        ("b_mod", b_mod, g_b_mod, m_b_mod, v_b_mod, (3 * SUBLANES, 128)),
        ("w_in", w_in, g_w_in, m_w_in, v_w_in, (D_MODEL, D_MODEL)),
        ("b_gate", b_gate, g_b_gate, m_b_gate, v_b_gate, (2 * SUBLANES, 128)),
        ("conv_w", conv_w, g_conv_w, m_conv_w, v_conv_w, (4, 128)),
        ("conv_b", conv_b, g_conv_b, m_conv_b, v_conv_b, (SUBLANES, 128)),
        ("w_a", w_a, g_w_a, m_w_a, v_w_a, (RNN_BLOCKS * 128, 128)),
        ("b_a", b_a, g_b_a, m_b_a, v_b_a, (SUBLANES, 128)),
        ("w_x", w_x, g_w_x, m_w_x, v_w_x, (RNN_BLOCKS * 128, 128)),
        ("b_x", b_x, g_b_x, m_b_x, v_b_x, (SUBLANES, 128)),
        ("lam", lam, g_lam, m_lam, v_lam, (SUBLANES, 128)),
        ("w_out_rnn", w_out_rnn, g_w_or, m_w_out_rnn, v_w_out_rnn, (128, D_MODEL)),
        ("w_out_attn", w_out_attn, g_w_oa, m_w_out_attn, v_w_out_attn, (128, D_MODEL)),
        ("w_o", w_o, g_w_o, m_w_o, v_w_o, (128, D_MODEL)),
        ("g_final", g_final, g_g_final, m_g_final, v_g_final, (SUBLANES, 128)),
    ]
    out_g, out_d, out_m, out_v = [], [], [], []
    for name, w_, g_, m_, v_, shape2 in weights:
        d_, nm_, nv_ = _adamw("adamw_" + name, w_.reshape(shape2), g_.reshape(shape2), m_.reshape(shape2),
                              v_.reshape(shape2))
        out_g.append(g_.reshape(w_.shape))
        out_d.append(d_.reshape(w_.shape))
        out_m.append(nm_.reshape(w_.shape))
        out_v.append(nv_.reshape(w_.shape))
    return (loss, grad_x[None], *out_g, *out_d, *out_m, *out_v)
```

```python
import jax
import jax.numpy as jnp
from jax import lax
from jax.experimental import pallas as pl
from jax.experimental.pallas import tpu as pltpu

F32 = jnp.float32
BF16 = jnp.bfloat16
MESH = pl.DeviceIdType.MESH

D_MODEL = 1024
N_HEADS = 8
HEAD_DIM = 128
RNN_BLOCKS = 8
N_DEV = 8
ROT_HALF = 16
ROPE_THETA = 500000.0
DILATIONS = (1, 4, 16)
KEY_BLOCK = 128
SPAN = KEY_BLOCK * DILATIONS[-1]
ATTN_SCALE = HEAD_DIM ** -0.5
NORM_EPS = 1e-6
LRU_C = 8.0
NEG_INF = -1e30
ADAM_LR, ADAM_B1, ADAM_B2, ADAM_EPS, ADAM_WD, ADAM_STEP = 0.001, 0.9, 0.999, 1e-08, 0.01, 10

SUBLANES = 8
VMEM_LIMIT = 56 * 1024 * 1024
PROJ_ROWS = 1024
RNN_ROWS = 512
HUB_ROWS = 256
DX_ROWS = 256
WGRAD_ROWS = 1024
ADD_ROWS = 256


def _params(sem=None, vmem=None):
    return pltpu.CompilerParams(dimension_semantics=sem, vmem_limit_bytes=vmem)


def _dot(a, b):
    return jnp.dot(a, b, preferred_element_type=F32)


def _dot_nt(a, b):
    return lax.dot_general(a, b, (((1,), (1,)), ((), ())), preferred_element_type=F32)


def _dot_tn(a, b):
    return lax.dot_general(a, b, (((0,), (0,)), ((), ())), preferred_element_type=F32)


def _sigmoid(z):
    return 1.0 / (1.0 + jnp.exp(-z))


def _expm1_nonpos(z, exp_z):
    return jnp.where(z > -0.01, z * (1.0 + 0.5 * z), exp_z - 1.0)


def _my_pos():
    return lax.axis_index("x"), lax.axis_index("y"), lax.axis_index("c")


def _flip(pos, k):
    x, y, c = pos
    return ((1 - x) if k & 4 else x, (1 - y) if k & 2 else y, (1 - c) if k & 1 else c)


def _index(pos):
    return 4 * pos[0] + 2 * pos[1] + pos[2]


def _ag_small(name, v):
    rows, cols = v.shape

    def body(v_ref, out_ref, send_sems, recv_sems):
        me = _my_pos()
        out_ref[_index(me)] = v_ref[...]
        sends = []
        for k in range(1, N_DEV):
            cp = pltpu.make_async_remote_copy(
                src_ref=v_ref, dst_ref=out_ref.at[_index(me)], send_sem=send_sems.at[k - 1],
                recv_sem=recv_sems.at[k - 1], device_id=_flip(me, k), device_id_type=MESH)
            cp.start()
            sends.append(cp)
        for k in range(1, N_DEV):
            peer = _flip(me, k)
            pltpu.make_async_remote_copy(
                src_ref=v_ref, dst_ref=out_ref.at[_index(peer)], send_sem=send_sems.at[k - 1],
                recv_sem=recv_sems.at[k - 1], device_id=peer, device_id_type=MESH).wait_recv()
        for cp in sends:
            cp.wait_send()

    return pl.pallas_call(
        body, name=name,
        out_shape=jax.ShapeDtypeStruct((N_DEV, rows, cols), v.dtype),
        in_specs=[pl.BlockSpec(memory_space=pltpu.VMEM)],
        out_specs=pl.BlockSpec(memory_space=pltpu.VMEM),
        scratch_shapes=[pltpu.SemaphoreType.DMA((N_DEV - 1,)), pltpu.SemaphoreType.DMA((N_DEV - 1,))],
        compiler_params=_params(None, VMEM_LIMIT),
    )(v)


def _ag_big(name, shards):
    n = len(shards)

    def body(*refs):
        ins, outs = refs[:n], refs[n:2 * n]
        send_sems, recv_sems, local_sems = refs[2 * n:]
        me = _my_pos()
        sib = _flip(me, 1)
        chips = [2, 4, 6]

        def copy(a, k, block, to, src=None):
            rows = outs[a].at[_index(block)]
            return pltpu.make_async_remote_copy(
                src_ref=rows if src is None else src, dst_ref=rows,
                send_sem=send_sems.at[a * 7 + k], recv_sem=recv_sems.at[a * 7 + k],
                device_id=to, device_id_type=MESH)

        started = []
        for a in range(n):
            mine = pltpu.make_async_copy(ins[a], outs[a].at[_index(me)], local_sems.at[a])
            mine.start()
            started.append(mine)
        sends = []
        for a in range(n):
            first = [copy(a, 0, me, sib, src=ins[a])]
            first += [copy(a, 1 + j, me, _flip(me, ch), src=ins[a]) for j, ch in enumerate(chips)]
            for cp in first:
                cp.start()
            sends += first
        for j, ch in enumerate(chips):
            for a in range(n):
                copy(a, 1 + j, _flip(me, ch), me).wait_recv()
                fwd = copy(a, 4 + j, _flip(me, ch), sib)
                fwd.start()
                sends.append(fwd)
        for a in range(n):
            copy(a, 0, sib, me).wait_recv()
            for j, ch in enumerate(chips):
                copy(a, 4 + j, _flip(sib, ch), me).wait_recv()
        for cp in sends:
            cp.wait_send()
        for mine in started:
            mine.wait()

    any_spec = pl.BlockSpec(memory_space=pl.ANY)
    return pl.pallas_call(
        body, name=name,
        out_shape=[jax.ShapeDtypeStruct((N_DEV,) + s.shape, s.dtype) for s in shards],
        in_specs=[any_spec] * n, out_specs=[any_spec] * n,
        scratch_shapes=[pltpu.SemaphoreType.DMA((7 * n,)), pltpu.SemaphoreType.DMA((7 * n,)),
                        pltpu.SemaphoreType.DMA((n,))],
    )(*shards)


def _peer_copies(shards, lands, send_sems, recv_sems):
    me = _my_pos()
    return [pltpu.make_async_remote_copy(
        src_ref=shards[a], dst_ref=lands[a].at[_index(me)],
        send_sem=send_sems.at[a * 7 + k - 1], recv_sem=recv_sems.at[a * 7 + k - 1],
        device_id=_flip(me, k), device_id_type=MESH) for a in range(len(shards)) for k in range(1, N_DEV)]


def _gather_start(shards, me):
    n = len(shards)

    def body(*refs):
        srcs, lands = refs[:n], refs[n:2 * n]
        send_sems, recv_sems = refs[2 * n:2 * n + 2]
        for cp in _peer_copies(srcs, lands, send_sems, recv_sems):
            cp.start()
        refs[-1][...] = jnp.zeros_like(refs[-1])

    hbm = pl.BlockSpec(memory_space=pltpu.HBM)
    sem = pl.BlockSpec(memory_space=pltpu.SEMAPHORE)
    held = [pltpu.HBM(s.shape, s.dtype) for s in shards]
    landing = [lax.dynamic_update_slice(jnp.zeros((N_DEV,) + s.shape, s.dtype), s[None], (me, 0, 0)) for s in shards]
    held_land = [pltpu.HBM(t.shape, t.dtype) for t in landing]
    outs = pl.pallas_call(
        body, name="gather_out_weights_start",
        out_shape=(pltpu.SemaphoreType.DMA((7 * n,)), pltpu.SemaphoreType.DMA((7 * n,)), *held, *held_land,
                   jax.ShapeDtypeStruct((SUBLANES, 128), F32)),
        in_specs=[hbm] * (2 * n),
        out_specs=(sem, sem, *[hbm] * (2 * n), pl.BlockSpec(memory_space=pltpu.VMEM)),
        input_output_aliases={i: 2 + i for i in range(2 * n)},
        compiler_params=pltpu.CompilerParams(has_side_effects=pltpu.SideEffectType.DATAFLOW_SIDE_EFFECTING),
    )(*[pltpu.with_memory_space_constraint(s, pltpu.HBM) for s in shards],
      *[pltpu.with_memory_space_constraint(t, pltpu.HBM) for t in landing])
    return outs[0], outs[1], outs[2:2 + n], outs[2 + n:2 + 2 * n], outs[-1]


def _gather_wait(send_sems, recv_sems, shards, lands, after):
    n = len(shards)

    def body(*refs):
        srcs, land_refs = refs[:n], refs[n:2 * n]
        sends, recvs = refs[2 * n:2 * n + 2]
        for cp in _peer_copies(srcs, land_refs, sends, recvs):
            cp.wait_send()
            cp.wait_recv()

    hbm = pl.BlockSpec(memory_space=pltpu.HBM)
    sem = pl.BlockSpec(memory_space=pltpu.SEMAPHORE)
    outs = pl.pallas_call(
        body, name="gather_out_weights_wait",
        out_shape=(*[pltpu.HBM(s.shape, s.dtype) for s in shards], *[pltpu.HBM(t.shape, t.dtype) for t in lands]),
        in_specs=[hbm] * (2 * n) + [sem, sem, pl.BlockSpec(memory_space=pl.ANY)],
        out_specs=[hbm] * (2 * n),
        input_output_aliases={i: i for i in range(2 * n)},
        compiler_params=pltpu.CompilerParams(has_side_effects=pltpu.SideEffectType.DATAFLOW_SIDE_EFFECTING),
    )(*shards, *lands, send_sems, recv_sems, after)
    return outs[n:]


def _rs_to_sibling(name, stacks):
    n = len(stacks)

    def body(*refs):
        ins, outs = refs[:n], refs[n:2 * n]
        send_sems, recv_sems = refs[2 * n:]
        me = _my_pos()
        sib = _flip(me, 1)
        sends = []
        for a in range(n):
            for m in range(4):
                target = _flip(sib, 2 * m)
                cp = pltpu.make_async_remote_copy(
                    src_ref=ins[a].at[_index(target)], dst_ref=outs[a].at[m],
                    send_sem=send_sems.at[a * 4 + m], recv_sem=recv_sems.at[a * 4 + m],
                    device_id=sib, device_id_type=MESH)
                cp.start()
                sends.append(cp)
        for cp in sends:
            cp.wait_recv()
        for cp in sends:
            cp.wait_send()

    any_spec = pl.BlockSpec(memory_space=pl.ANY)
    return pl.pallas_call(
        body, name=name,
        out_shape=[jax.ShapeDtypeStruct((4,) + s.shape[1:], s.dtype) for s in stacks],
        in_specs=[any_spec] * n, out_specs=[any_spec] * n,
        scratch_shapes=[pltpu.SemaphoreType.DMA((4 * n,)), pltpu.SemaphoreType.DMA((4 * n,))],
    )(*stacks)


def _chip_copies(srcs, lands, send_sems, recv_sems):
    me = _my_pos()
    return [pltpu.make_async_remote_copy(
        src_ref=srcs[a].at[m - 1], dst_ref=lands[a].at[m - 1],
        send_sem=send_sems.at[a * 3 + m - 1], recv_sem=recv_sems.at[a * 3 + m - 1],
        device_id=_flip(me, 2 * m), device_id_type=MESH) for a in range(len(srcs)) for m in range(1, 4)]


def _rs_chips_start(sums):
    n = len(sums)

    def body(*refs):
        srcs, lands = refs[:n], refs[n:2 * n]
        send_sems, recv_sems = refs[2 * n:2 * n + 2]
        token = refs[-1]
        for cp in _chip_copies(srcs, lands, send_sems, recv_sems):
            cp.start()
        token[...] = jnp.zeros_like(token)

    hbm = pl.BlockSpec(memory_space=pltpu.HBM)
    sem = pl.BlockSpec(memory_space=pltpu.SEMAPHORE)
    held = [pltpu.HBM(s.shape, s.dtype) for s in sums]
    outs = pl.pallas_call(
        body, name="rs_chips_start",
        out_shape=(pltpu.SemaphoreType.DMA((3 * n,)), pltpu.SemaphoreType.DMA((3 * n,)), *held, *held,
                   jax.ShapeDtypeStruct((SUBLANES, 128), F32)),
        in_specs=[hbm] * (2 * n),
        out_specs=(sem, sem, *[hbm] * (2 * n), pl.BlockSpec(memory_space=pltpu.VMEM)),
        input_output_aliases={i: 2 + i for i in range(2 * n)},
        compiler_params=pltpu.CompilerParams(has_side_effects=pltpu.SideEffectType.DATAFLOW_SIDE_EFFECTING),
    )(*[pltpu.with_memory_space_constraint(s, pltpu.HBM) for s in sums],
      *[pltpu.with_memory_space_constraint(lax.empty(s.shape, s.dtype), pltpu.HBM) for s in sums])
    return outs[0], outs[1], outs[2:2 + n], outs[2 + n:2 + 2 * n], outs[-1]


def _rs_chips_wait(send_sems, recv_sems, srcs, lands, after):
    n = len(srcs)

    def body(*refs):
        src_refs, land_refs = refs[:n], refs[n:2 * n]
        sends, recvs = refs[2 * n:2 * n + 2]
        for cp in _chip_copies(src_refs, land_refs, sends, recvs):
            cp.wait_send()
            cp.wait_recv()

    hbm = pl.BlockSpec(memory_space=pltpu.HBM)
    sem = pl.BlockSpec(memory_space=pltpu.SEMAPHORE)
    held = [pltpu.HBM(s.shape, s.dtype) for s in srcs]
    outs = pl.pallas_call(
        body, name="rs_chips_wait", out_shape=(*held, *held),
        in_specs=[hbm] * (2 * n) + [sem, sem, pl.BlockSpec(memory_space=pl.ANY)],
        out_specs=[hbm] * (2 * n),
        input_output_aliases={i: i for i in range(2 * n)},
        compiler_params=pltpu.CompilerParams(has_side_effects=pltpu.SideEffectType.DATAFLOW_SIDE_EFFECTING),
    )(*srcs, *lands, send_sems, recv_sems, after)
    return outs[n:]


def _add_sibling(name, stack, recv, targets):
    _, rows, cols = stack.shape
    tr = min(rows, ADD_ROWS)

    def own_body(t_ref, a_ref, b_ref, o_ref):
        o_ref[...] = a_ref[...] + b_ref[...]

    own = pl.pallas_call(
        own_body, name=name + "_own",
        out_shape=jax.ShapeDtypeStruct((rows, cols), F32),
        grid_spec=pltpu.PrefetchScalarGridSpec(
            num_scalar_prefetch=1, grid=(rows // tr,),
            in_specs=[pl.BlockSpec((None, tr, cols), lambda i, t: (t[0], i, 0)),
                      pl.BlockSpec((None, tr, cols), lambda i, t: (0, i, 0))],
            out_specs=pl.BlockSpec((tr, cols), lambda i, t: (i, 0))),
        compiler_params=_params(("arbitrary",)),
    )(targets, stack, recv)

    def send_body(t_ref, a_ref, b_ref, o_ref):
        o_ref[...] = (a_ref[...] + b_ref[...]).astype(BF16)

    send = pl.pallas_call(
        send_body, name=name + "_send",
        out_shape=jax.ShapeDtypeStruct((3, rows, cols), BF16),
        grid_spec=pltpu.PrefetchScalarGridSpec(
            num_scalar_prefetch=1, grid=(3, rows // tr),
            in_specs=[pl.BlockSpec((None, tr, cols), lambda m, i, t: (t[m + 1], i, 0)),
                      pl.BlockSpec((None, tr, cols), lambda m, i, t: (m + 1, i, 0))],
            out_specs=pl.BlockSpec((None, tr, cols), lambda m, i, t: (m, i, 0))),
        compiler_params=_params(("arbitrary", "arbitrary")),
    )(targets, stack, recv)
    return own, send


def _add_chips(name, own, recv):
    rows, cols = own.shape
    tr = min(rows, ADD_ROWS)

    def body(a_ref, b_ref, o_ref):
        o_ref[...] = ((a_ref[...] + b_ref[0].astype(F32)) + b_ref[1].astype(F32)) + b_ref[2].astype(F32)

    return pl.pallas_call(
        body, name=name,
        out_shape=jax.ShapeDtypeStruct((rows, cols), F32),
        grid=(rows // tr,),
        in_specs=[pl.BlockSpec((tr, cols), lambda i: (i, 0)),
                  pl.BlockSpec((3, tr, cols), lambda i: (0, i, 0))],
        out_specs=pl.BlockSpec((tr, cols), lambda i: (i, 0)),
        compiler_params=_params(("arbitrary",)),
    )(own, recv)


def _allreduce_small(name, v):
    rows, cols = v.shape
    half = rows // 2
    assert rows % (2 * SUBLANES) == 0

    def body(v_ref, out_ref, from_sib, chip_half, from_chips, send_sems, recv_sems):
        me = _my_pos()
        sib = _flip(me, 1)
        mine = pl.ds(pl.multiple_of(me[2] * half, SUBLANES), half)
        theirs = pl.ds(pl.multiple_of((1 - me[2]) * half, SUBLANES), half)

        def copy(k, src, dst, to):
            return pltpu.make_async_remote_copy(src_ref=src, dst_ref=dst, send_sem=send_sems.at[k],
                                                recv_sem=recv_sems.at[k], device_id=to, device_id_type=MESH)

        to_sib = copy(0, v_ref.at[theirs], from_sib, sib)
        to_sib.start()
        to_sib.wait_recv()
        chip_half[...] = v_ref[mine, :] + from_sib[...]
        to_chips = [copy(m, chip_half, from_chips.at[m - 1], _flip(me, 2 * m)) for m in range(1, 4)]
        for cp in to_chips:
            cp.start()
        for cp in to_chips:
            cp.wait_recv()
        my_chip = 2 * me[0] + me[1]
        total = None
        for chip in range(4):
            slot = jnp.maximum(jnp.bitwise_xor(chip, my_chip) - 1, 0)
            part = jnp.where(chip == my_chip, chip_half[...], from_chips[slot])
            total = part if total is None else total + part
        out_ref[mine, :] = total
        swap = copy(4, out_ref.at[mine], out_ref.at[mine], sib)
        swap.start()
        copy(4, out_ref.at[theirs], out_ref.at[theirs], sib).wait_recv()
        for cp in [to_sib, swap] + to_chips:
            cp.wait_send()

    return pl.pallas_call(
        body, name=name, out_shape=jax.ShapeDtypeStruct((rows, cols), F32),
        in_specs=[pl.BlockSpec(memory_space=pltpu.VMEM)],
        out_specs=pl.BlockSpec(memory_space=pltpu.VMEM),
        scratch_shapes=[pltpu.VMEM((half, cols), F32), pltpu.VMEM((half, cols), F32),
                        pltpu.VMEM((3, half, cols), F32),
                        pltpu.SemaphoreType.DMA((5,)), pltpu.SemaphoreType.DMA((5,))],
        compiler_params=_params(None, VMEM_LIMIT),
    )(v)


def _mod_fwd(c_all, w_mod):
    def body(c_ref, w_ref, o_ref):
        c = c_ref[...]
        o_ref[...] = jnp.dot(c * _sigmoid(c), w_ref[...], preferred_element_type=F32,
                             precision=lax.Precision.HIGHEST)

    return pl.pallas_call(
        body, name="mod_fwd", out_shape=jax.ShapeDtypeStruct((N_DEV, w_mod.shape[1]), F32),
    )(c_all, w_mod)


def _mod_bwd(c_all, dmod_all, dmod_cols):
    def body(c_ref, da_ref, dc_ref, gb_ref, gw_ref):
        c = c_ref[...]
        acc = da_ref[0:1, :]
        for b in range(1, N_DEV):
            acc = acc + da_ref[b:b + 1, :]
        gb_ref[...] = acc
        gw_ref[...] = lax.dot_general(c * _sigmoid(c), dc_ref[...], (((0,), (0,)), ((), ())),
                                      preferred_element_type=F32, precision=lax.Precision.HIGHEST)

    return pl.pallas_call(
        body, name="mod_bwd",
        out_shape=[jax.ShapeDtypeStruct((1, dmod_all.shape[1]), F32),
                   jax.ShapeDtypeStruct((c_all.shape[1], dmod_cols.shape[1]), F32)],
    )(c_all, dmod_all, dmod_cols)


def _rope_partner(t):
    lane = lax.broadcasted_iota(jnp.int32, t.shape, 1)
    return jnp.where(lane < ROT_HALF, pltpu.roll(t, HEAD_DIM - ROT_HALF, 1), pltpu.roll(t, ROT_HALF, 1))


def _norm(x, mod, b_mod, g_norm):
    seq = x.shape[0]
    tm = PROJ_ROWS

    def body(x_ref, mod_ref, bmod_ref, g_ref, h_ref):
        xf = x_ref[...]
        rstd = lax.rsqrt(jnp.mean(xf * xf, axis=-1, keepdims=True) + NORM_EPS)
        shift = mod_ref[:, 0:D_MODEL] + bmod_ref[:, 0:D_MODEL]
        scale = mod_ref[:, D_MODEL:2 * D_MODEL] + bmod_ref[:, D_MODEL:2 * D_MODEL]
        h_ref[...] = (((xf * rstd) * g_ref[...]) * (1.0 + scale) + shift).astype(BF16)

    row = pl.BlockSpec((tm, D_MODEL), lambda i: (i, 0))
    const = lambda cols: pl.BlockSpec((1, cols), lambda i: (0, 0))
    return pl.pallas_call(
        body, name="norm", out_shape=jax.ShapeDtypeStruct((seq, D_MODEL), BF16), grid=(seq // tm,),
        in_specs=[row, const(3 * D_MODEL), const(3 * D_MODEL), const(D_MODEL)], out_specs=row,
        compiler_params=_params(("arbitrary",), VMEM_LIMIT),
    )(x, mod, b_mod, g_norm)


def _proj(h, w_in_all, cosf, sinf):
    seq = h.shape[0]
    tm = PROJ_ROWS
    last = seq // tm - 1

    def body(h_ref, w_ref, cos_ref, sin_ref, pf_ref, q_ref, k_ref, v_ref):
        j = pl.program_id(0)

        @pl.when((j < 2) | (j > 4))
        def _():
            pf_ref[...] = _dot(h_ref[...], w_ref[...])

        def heads(dst_ref, rotate, gain):
            for pair in range(N_HEADS // 2):
                both = _dot(h_ref[...], w_ref[:, 2 * pair * HEAD_DIM:2 * (pair + 1) * HEAD_DIM])
                for hh in (2 * pair, 2 * pair + 1):
                    t = both[:, (hh % 2) * HEAD_DIM:(hh % 2 + 1) * HEAD_DIM]
                    if rotate:
                        t = t * cos_ref[...] + _rope_partner(t) * sin_ref[...]
                    dst_ref[hh] = t if gain is None else t * gain

        @pl.when(j == 2)
        def _():
            heads(q_ref, True, ATTN_SCALE)

        @pl.when(j == 3)
        def _():
            heads(k_ref, True, None)

        @pl.when(j == 4)
        def _():
            heads(v_ref, False, None)

    def pf_block(j, i):
        f32_piece = (j < 2) | (j > 4)
        return (jnp.where(f32_piece, i, last), jnp.where(j < 2, j, jnp.where(j < 5, 1, j - 3)))

    def hm_block(piece):
        return lambda j, i: (0, jnp.where(j == piece, i, jnp.where(j < piece, 0, last)), 0)

    hm = jax.ShapeDtypeStruct((N_HEADS, seq, HEAD_DIM), F32)
    hm_spec = lambda piece: pl.BlockSpec((N_HEADS, tm, HEAD_DIM), hm_block(piece))
    row = lambda j, i: (i, 0)
    return pl.pallas_call(
        body, name="proj",
        out_shape=[jax.ShapeDtypeStruct((seq, 5 * D_MODEL), F32), hm, hm, hm],
        grid=(8, seq // tm),
        in_specs=[pl.BlockSpec((tm, D_MODEL), row),
                  pl.BlockSpec((None, D_MODEL, D_MODEL), lambda j, i: (j, 0, 0)),
                  pl.BlockSpec((tm, HEAD_DIM), row), pl.BlockSpec((tm, HEAD_DIM), row)],
        out_specs=[pl.BlockSpec((tm, D_MODEL), pf_block), hm_spec(2), hm_spec(3), hm_spec(4)],
        compiler_params=_params(("arbitrary", "arbitrary"), VMEM_LIMIT),
    )(h, w_in_all, cosf, sinf)


def _shift_down(v, s, head):
    rolled = pltpu.roll(v, s, 0)
    row = lax.broadcasted_iota(jnp.int32, head.shape, 0)
    first = jnp.where(row < s, pltpu.roll(head, s, 0), rolled[:SUBLANES, :])
    return jnp.concatenate([first, rolled[SUBLANES:, :]], axis=0)


def _shift_up(v, s, tail):
    rows = v.shape[0]
    rolled = pltpu.roll(v, rows - s, 0)
    row = lax.broadcasted_iota(jnp.int32, tail.shape, 0)
    last = jnp.where(row >= SUBLANES - s, pltpu.roll(tail, SUBLANES - s, 0), rolled[rows - SUBLANES:, :])
    return jnp.concatenate([rolled[:rows - SUBLANES, :], last], axis=0)


def _doubling(a, b, period, reverse):
    rows = a.shape[0]
    pos = lax.broadcasted_iota(jnp.int32, a.shape, 0) & (period - 1)
    k = 1
    while k < period:
        inside = (pos < period - k) if reverse else (pos >= k)
        shift = rows - k if reverse else k
        a_s = jnp.where(inside, pltpu.roll(a, shift, 0), 1.0)
        b_s = jnp.where(inside, pltpu.roll(b, shift, 0), 0.0)
        b = a * b_s + b
        a = a * a_s
        k *= 2
    return a, b


def _scan(a, b, boundary, reverse, a_scr, b_scr, spread):
    rows = a.shape[0]
    ntile = rows // SUBLANES
    a_scr[...], b_scr[...] = _doubling(a, b, SUBLANES, reverse)
    ends = pl.ds(0 if reverse else SUBLANES - 1, ntile, stride=SUBLANES)
    a_end, x_end = _doubling(a_scr[ends, :], b_scr[ends, :], ntile, reverse)
    x_end = x_end + a_end * boundary
    tile = lax.broadcasted_iota(jnp.int32, x_end.shape, 0)
    if reverse:
        incoming = jnp.where(tile == ntile - 1, boundary, pltpu.roll(x_end, ntile - 1, 0))
        last = x_end[0:1, :]
    else:
        incoming = jnp.where(tile == 0, boundary, pltpu.roll(x_end, 1, 0))
        last = x_end[ntile - 1:ntile, :]
    for s in range(SUBLANES):
        spread[pl.ds(s, ntile, stride=SUBLANES), :] = incoming
    return b_scr[...] + a_scr[...] * spread[...], last


def _conv_taps(xr, head):
    return [_shift_down(xr, 3, head), _shift_down(xr, 2, head), _shift_down(xr, 1, head), xr]


def _rnn_gates(xc, wa, ba, wx, bx, lam, keep):
    xcb = xc.astype(BF16)
    r = _sigmoid(_dot(xcb, wa.astype(BF16)) + ba)
    i = _sigmoid(_dot(xcb, wx.astype(BF16)) + bx)
    softplus = jnp.maximum(-lam, 0.0) + jnp.log(1.0 + jnp.exp(-jnp.abs(lam)))
    cl = -LRU_C * softplus
    log_a = cl * r
    a_raw = jnp.exp(log_a)
    mult_raw = jnp.sqrt(-_expm1_nonpos(2.0 * log_a, a_raw * a_raw))
    live = keep > 0.0
    return r, i, cl, a_raw, mult_raw, jnp.where(live, a_raw, 0.0), jnp.where(live, mult_raw, 1.0), live


def _rnn_specs(seq, rows, time_of):
    per = rows // SUBLANES
    vec = pl.BlockSpec((None, 1, 128), lambda hb, n: (hb, 0, 0))
    mat = pl.BlockSpec((None, 128, 128), lambda hb, n: (hb, 0, 0))
    return [pl.BlockSpec((rows, 128), lambda hb, n: (time_of(n), hb)),
            pl.BlockSpec((SUBLANES, 128), lambda hb, n: (jnp.maximum(time_of(n) * per - 1, 0), hb)),
            pl.BlockSpec((rows, 1), lambda hb, n: (time_of(n), 0)),
            pl.BlockSpec((None, SUBLANES, 128), lambda hb, n: (hb, 0, 0)),
            vec, mat, vec, mat, vec, vec]


def _rnn_fwd(pf, keep, conv_w8, conv_b, w_a, b_a, w_x, b_x, lam):
    seq = pf.shape[0]
    rows = RNN_ROWS

    def body(x_ref, xh_ref, keep_ref, cw_ref, cb_ref, wa_ref, ba_ref, wx_ref, bx_ref, lam_ref, hr_ref,
             carry, a_scr, b_scr, spread):
        n = pl.program_id(1)

        @pl.when(n == 0)
        def _():
            carry[...] = jnp.zeros_like(carry)

        xr = x_ref[...]
        head = jnp.where(n > 0, xh_ref[...], 0.0)
        taps = _conv_taps(xr, head)
        xc = cb_ref[...] + sum(cw_ref[k:k + 1, :] * taps[k] for k in range(4))
        _, i, _, _, _, a, mult, _ = _rnn_gates(xc, wa_ref[...], ba_ref[...], wx_ref[...], bx_ref[...],
                                               lam_ref[...], keep_ref[...])
        h, last = _scan(a, mult * i * xc, carry[0:1, :], False, a_scr, b_scr, spread)
        hr_ref[...] = h
        carry[...] = jnp.broadcast_to(last, carry.shape)

    chunk_f32 = pltpu.VMEM((rows, 128), F32)
    return pl.pallas_call(
        body, name="rnn_fwd",
        out_shape=jax.ShapeDtypeStruct((seq, D_MODEL), F32),
        grid=(RNN_BLOCKS, seq // rows),
        in_specs=_rnn_specs(seq, rows, lambda n: n),
        out_specs=pl.BlockSpec((rows, 128), lambda hb, n: (n, hb)),
        scratch_shapes=[pltpu.VMEM((SUBLANES, 128), F32), chunk_f32, chunk_f32, chunk_f32],
        compiler_params=_params(("arbitrary", "arbitrary"), VMEM_LIMIT),
    )(pf, pf, keep, conv_w8, conv_b, w_a, b_a, w_x, b_x, lam)


def _rnn_bwd(pf, hr, dhr, keep, conv_w8, conv_b, w_a, b_a, w_x, b_x, lam):
    seq = pf.shape[0]
    rows = RNN_ROWS
    nchunk = seq // rows
    per = rows // SUBLANES
    time_of = lambda n: nchunk - 1 - n

    def body(x_ref, xh_ref, keep_ref, cw_ref, cb_ref, wa_ref, ba_ref, wx_ref, bx_ref, lam_ref,
             hr_ref, hrh_ref, dhr_ref,
             dx_ref, gcw_ref, gcb_ref, gwa_ref, gba_ref, gwx_ref, gbx_ref, glam_ref,
             g_carry, dxc_tail, a_scr, b_scr, spread):
        n = pl.program_id(1)
        first_in_time = n == nchunk - 1

        @pl.when(n == 0)
        def _():
            g_carry[...] = jnp.zeros_like(g_carry)
            dxc_tail[...] = jnp.zeros_like(dxc_tail)
            for ref in (gcw_ref, gcb_ref, gwa_ref, gba_ref, gwx_ref, gbx_ref, glam_ref):
                ref[...] = jnp.zeros_like(ref)

        xr = x_ref[...]
        head = jnp.where(first_in_time, 0.0, xh_ref[...])
        taps = _conv_taps(xr, head)
        cw = cw_ref[...]
        xc = cb_ref[...] + sum(cw[k:k + 1, :] * taps[k] for k in range(4))
        wa, wx, lam = wa_ref[...], wx_ref[...], lam_ref[...]
        r, i, cl, a_raw, mult_raw, a, mult, live = _rnn_gates(xc, wa, ba_ref[...], wx, bx_ref[...], lam,
                                                               keep_ref[...])
        h_prev = _shift_down(hr_ref[...], 1, jnp.where(first_in_time, 0.0, hrh_ref[...]))

        row = lax.broadcasted_iota(jnp.int32, xr.shape, 0)
        last = row == rows - 1
        a_next = jnp.where(last, 0.0, pltpu.roll(a, rows - 1, 0))
        g, g_first = _scan(a_next, dhr_ref[...] + jnp.where(last, g_carry[0:1, :], 0.0),
                           jnp.zeros((1, 128), F32), True, a_scr, b_scr, spread)
        g_carry[...] = jnp.broadcast_to(a[0:1, :] * g_first, g_carry.shape)

        da = g * h_prev
        dmult = g * i * xc
        di = g * mult * xc
        dxc = g * mult * i
        dlog_a = jnp.where(live, da * a_raw - dmult * a_raw * a_raw / mult_raw, 0.0)
        dpa = (dlog_a * cl) * r * (1.0 - r)
        dpx = di * i * (1.0 - i)
        glam_ref[...] += jnp.sum(dlog_a * r, axis=0, keepdims=True) * (LRU_C * _sigmoid(-lam))
        xcb, dpab, dpxb = xc.astype(BF16), dpa.astype(BF16), dpx.astype(BF16)
        gwa_ref[...] += _dot_tn(xcb, dpab)
        gwx_ref[...] += _dot_tn(xcb, dpxb)
        gba_ref[...] += jnp.sum(dpa, axis=0, keepdims=True)
        gbx_ref[...] += jnp.sum(dpx, axis=0, keepdims=True)
        dxc = dxc + _dot_nt(dpab, wa.astype(BF16)) + _dot_nt(dpxb, wx.astype(BF16))

        gcb_ref[...] += jnp.sum(dxc, axis=0, keepdims=True)
        for k in range(4):
            gcw_ref[k:k + 1, :] += jnp.sum(dxc * taps[k], axis=0, keepdims=True)
        tail = dxc_tail[...]
        dx = cw[3:4, :] * dxc
        for k in range(3):
            dx = dx + cw[k:k + 1, :] * _shift_up(dxc, 3 - k, tail)
        dx_ref[...] = dx.astype(BF16)
        dxc_tail[...] = dxc[0:SUBLANES, :]

    blk = lambda hb, n: (hb, 0, 0)
    chunk = pl.BlockSpec((rows, 128), lambda hb, n: (time_of(n), hb))
    vec_out = pl.BlockSpec((None, 1, 128), blk)
    mat_out = pl.BlockSpec((None, 128, 128), blk)
    vec_shape = jax.ShapeDtypeStruct((RNN_BLOCKS, 1, 128), F32)
    mat_shape = jax.ShapeDtypeStruct((RNN_BLOCKS, 128, 128), F32)
    return pl.pallas_call(
        body, name="rnn_bwd",
        out_shape=[jax.ShapeDtypeStruct((seq, D_MODEL), BF16),
                   jax.ShapeDtypeStruct((RNN_BLOCKS, SUBLANES, 128), F32), vec_shape,
                   mat_shape, vec_shape, mat_shape, vec_shape, vec_shape],
        grid=(RNN_BLOCKS, nchunk),
        in_specs=_rnn_specs(seq, rows, time_of) + [
            chunk, pl.BlockSpec((SUBLANES, 128), lambda hb, n: (jnp.maximum(time_of(n) * per - 1, 0), hb)), chunk],
        out_specs=[chunk, pl.BlockSpec((None, SUBLANES, 128), blk), vec_out,
                   mat_out, vec_out, mat_out, vec_out, vec_out],
        scratch_shapes=[pltpu.VMEM((SUBLANES, 128), F32), pltpu.VMEM((SUBLANES, 128), F32)]
                       + [pltpu.VMEM((rows, 128), F32)] * 3,
        compiler_params=_params(("arbitrary", "arbitrary"), VMEM_LIMIT),
    )(pf, pf, keep, conv_w8, conv_b, w_a, b_a, w_x, b_x, lam, hr, hr, dhr)


def _unit_rows(dil, r, j):
    start = j * KEY_BLOCK * dil + r
    return pl.ds(start, KEY_BLOCK) if dil == 1 else pl.ds(start, KEY_BLOCK, stride=dil)


def _attn_fwd(q, k, v):
    nh, seq, _ = q.shape
    nchunk = seq // SPAN
    nblk = SPAN // KEY_BLOCK
    wide = DILATIONS[-1]

    def body(q_ref, k_ref, v_ref, kp_ref, vp_ref, o_ref, l1_ref, l4_ref, l16_ref,
             acc, m_s, l_s, q16, k16, v16, k16p, v16p, acc16, m16, l16, tmp):
        n = pl.program_id(1)
        qi = lax.broadcasted_iota(jnp.int32, (KEY_BLOCK, KEY_BLOCK), 0)
        ki = lax.broadcasted_iota(jnp.int32, (KEY_BLOCK, KEY_BLOCK), 1)
        bias_own = jnp.where(ki <= qi, 0.0, NEG_INF)
        bias_before = jnp.where(ki >= qi, 0.0, NEG_INF)
        bias_mid = jnp.concatenate([bias_before, bias_own], axis=1)
        bias_first = jnp.concatenate([jnp.where(n > 0, bias_before, NEG_INF), bias_own], axis=1)
        ones = jnp.ones((2 * KEY_BLOCK, HEAD_DIM), BF16)
        diag = qi == ki

        @pl.when(n == 0)
        def _():
            k16p[...] = jnp.zeros_like(k16p)
            v16p[...] = jnp.zeros_like(v16p)

        def unit(qf, kpb, kb, vpb, vb, bias, state, rows, first):
            acc_r, m_r, l_r = state
            kcat = jnp.concatenate([kpb, kb], axis=0)
            vaug = jnp.concatenate([jnp.concatenate([vpb, vb], axis=0), ones], axis=1)
            s = _dot_nt(qf.astype(BF16), kcat) + bias
            mx = jnp.max(s, axis=-1, keepdims=True)
            if first:
                m_new = jnp.broadcast_to(mx, (KEY_BLOCK, HEAD_DIM))
            else:
                m_old = m_r[rows, :]
                m_new = jnp.maximum(m_old, mx)
            pv = _dot(jnp.exp(s - jnp.concatenate([m_new, m_new], axis=1)).astype(BF16), vaug)
            if first:
                acc_r[rows, :] = pv[:, :HEAD_DIM]
                l_r[rows, :] = pv[:, HEAD_DIM:]
            else:
                alpha = jnp.exp(m_old - m_new)
                acc_r[rows, :] = alpha * acc_r[rows, :] + pv[:, :HEAD_DIM]
                l_r[rows, :] = alpha * l_r[rows, :] + pv[:, HEAD_DIM:]
            m_r[rows, :] = m_new

        for gi, dil in enumerate(DILATIONS[:-1]):
            nb = nblk // dil
            for r in range(dil):
                prow = _unit_rows(dil, r, nb - 1)
                kpb, vpb = kp_ref[prow, :].astype(BF16), vp_ref[prow, :].astype(BF16)
                for j in range(nb):
                    rows = _unit_rows(dil, r, j)
                    kb, vb = k_ref[rows, :].astype(BF16), v_ref[rows, :].astype(BF16)
                    unit(q_ref[rows, :], kpb, kb, vpb, vb, bias_first if j == 0 else bias_mid,
                         (acc, m_s, l_s), rows, gi == 0)
                    kpb, vpb = kb, vb

        for src, dst in ((q_ref, q16), (k_ref, k16), (v_ref, v16), (acc, acc16), (m_s, m16), (l_s, l16)):
            _to_residue_major(src, tmp, dst)
        for r in range(wide):
            rows = pl.ds(r * KEY_BLOCK, KEY_BLOCK)
            unit(q16[rows, :], k16p[rows, :].astype(BF16), k16[rows, :].astype(BF16), v16p[rows, :].astype(BF16),
                 v16[rows, :].astype(BF16), bias_first, (acc16, m16, l16), rows, False)
        k16p[...] = k16[...]
        v16p[...] = v16[...]

        den = l16[...]
        acc16[...] = acc16[...] * (1.0 / den)
        m16[...] = m16[...] + jnp.log(den)
        _from_residue_major(acc16, tmp, o_ref, False)
        _from_residue_major(m16, tmp, m_s, False)

        def lse_row(ref, rows):
            return jnp.sum(jnp.where(diag, ref[rows, :], 0.0), axis=0, keepdims=True)

        for dil, out in zip(DILATIONS[:-1], (l1_ref, l4_ref)):
            nb = nblk // dil
            for r in range(dil):
                for j in range(nb):
                    out[r * nb + j:r * nb + j + 1, :] = lse_row(m_s, _unit_rows(dil, r, j))
        for r in range(wide):
            l16_ref[r:r + 1, :] = lse_row(m16, pl.ds(r * KEY_BLOCK, KEY_BLOCK))

    blk = pl.BlockSpec((None, SPAN, HEAD_DIM), lambda h, n: (h, n, 0))
    pblk = pl.BlockSpec((None, SPAN, HEAD_DIM), lambda h, n: (h, jnp.maximum(n - 1, 0), 0))
    lblk = pl.BlockSpec((None, nblk, KEY_BLOCK), lambda h, n: (h, n, 0))
    lshape = jax.ShapeDtypeStruct((nh, seq // KEY_BLOCK, KEY_BLOCK), F32)
    o, l1, l4, l16 = pl.pallas_call(
        body, name="attn_fwd",
        out_shape=[jax.ShapeDtypeStruct((nh, seq, HEAD_DIM), F32), lshape, lshape, lshape],
        grid=(nh, nchunk), in_specs=[blk, blk, blk, pblk, pblk], out_specs=[blk, lblk, lblk, lblk],
        scratch_shapes=[pltpu.VMEM((SPAN, HEAD_DIM), F32)] * 12,
        compiler_params=_params(("arbitrary", "arbitrary"), VMEM_LIMIT),
    )(q, k, v, k, v)
    return o, (l1, l4, l16)


def _to_residue_major(src, tmp, dst):
    quarter = SPAN // 4
    for r4 in range(4):
        tmp[r4 * quarter:(r4 + 1) * quarter, :] = src[pl.ds(r4, quarter, stride=4), :]
    for r4 in range(4):
        for rp in range(4):
            r = r4 + 4 * rp
            dst[r * KEY_BLOCK:(r + 1) * KEY_BLOCK, :] = tmp[pl.ds(r4 * quarter + rp, KEY_BLOCK, stride=4), :]


def _from_residue_major(src, tmp, dst, add):
    quarter = SPAN // 4
    for r4 in range(4):
        for rp in range(4):
            r = r4 + 4 * rp
            tmp[pl.ds(r4 * quarter + rp, KEY_BLOCK, stride=4), :] = src[r * KEY_BLOCK:(r + 1) * KEY_BLOCK, :]
    for r4 in range(4):
        rows = pl.ds(r4, quarter, stride=4)
        part = tmp[r4 * quarter:(r4 + 1) * quarter, :]
        dst[rows, :] = dst[rows, :] + part if add else part


def _attn_bwd(q, k, v, do, o, lses, cosf, sinf):
    nh, seq, _ = q.shape
    nchunk = seq // SPAN
    nblk = SPAN // KEY_BLOCK
    wide = DILATIONS[-1]
    assert SPAN == wide * KEY_BLOCK

    def body(q_ref, k_ref, v_ref, do_ref, o_ref, kp_ref, vp_ref, l1_ref, l4_ref, l16_ref,
             cos_ref, sin_ref, cosp_ref, sinp_ref, dq_ref, dk_ref, dv_ref,
             dq_acc, dkc_acc, dvc_acc, dkp_acc, dvp_acc, q16, k16, v16, do16, o16, k16p, v16p,
             dq16, dkc16, dvc16, dkp16, dvp16, tmp, pt_s, ds_s, kcat_s, qb_s, dob_s):
        n = pl.program_id(1)
        ki = lax.broadcasted_iota(jnp.int32, (KEY_BLOCK, KEY_BLOCK), 0)
        qi = lax.broadcasted_iota(jnp.int32, (KEY_BLOCK, KEY_BLOCK), 1)
        bias_own = jnp.where(ki <= qi, 0.0, NEG_INF)
        bias_before = jnp.where(ki >= qi, 0.0, NEG_INF)
        bias_mid = jnp.concatenate([bias_before, bias_own], axis=0)
        bias_first = jnp.concatenate([jnp.where(n > 0, bias_before, NEG_INF), bias_own], axis=0)
        ones8 = jnp.ones((SUBLANES, HEAD_DIM), BF16)

        def row_dot(a, b):
            prod = a * b
            hi = prod.astype(BF16)
            lo = (prod - hi.astype(F32)).astype(BF16)
            return (_dot_nt(ones8, hi) + _dot_nt(ones8, lo))[0:1, :]

        def group(units, srcs, before, l_ref, accs):
            src_q, src_do, src_o, src_k, src_v = srcs
            before_k, before_v = before
            acc_q, acc_kc, acc_vc, acc_kp, acc_vp = accs
            kb = vb = None
            for u, (rows, prow, outside, lrow, _) in enumerate(units):
                dof = src_do[rows, :]
                qb, dob = src_q[rows, :].astype(BF16), dof.astype(BF16)
                kpb, vpb = (before_k[prow, :].astype(BF16), before_v[prow, :].astype(BF16)) if outside else (kb, vb)
                kb, vb = src_k[rows, :].astype(BF16), src_v[rows, :].astype(BF16)
                kcat = jnp.concatenate([kpb, kb], axis=0)
                vcat = jnp.concatenate([vpb, vb], axis=0)
                bias = bias_first if outside else bias_mid
                pt = jnp.exp(_dot_nt(kcat, qb) + bias - l_ref[lrow:lrow + 1, :])
                dst = pt * (_dot_nt(vcat, dob) - row_dot(dof, src_o[rows, :]))
                pt_s[u], ds_s[u], kcat_s[u], qb_s[u], dob_s[u] = pt.astype(BF16), dst.astype(BF16), kcat, qb, dob
            for u, (rows, _, _, _, _) in enumerate(units):
                acc_q[rows, :] += _dot_tn(ds_s[u], kcat_s[u])
            for u, (rows, prow, outside, _, nxt) in enumerate(units):
                dk = _dot(ds_s[u, KEY_BLOCK:, :], qb_s[u])
                dv = _dot(pt_s[u, KEY_BLOCK:, :], dob_s[u])
                if nxt is not None:
                    dk = dk + _dot(ds_s[nxt, :KEY_BLOCK, :], qb_s[nxt])
                    dv = dv + _dot(pt_s[nxt, :KEY_BLOCK, :], dob_s[nxt])
                acc_kc[rows, :] += dk
                acc_vc[rows, :] += dv
                if outside:
                    acc_kp[prow, :] += _dot(ds_s[u, :KEY_BLOCK, :], qb_s[u])
                    acc_vp[prow, :] += _dot(pt_s[u, :KEY_BLOCK, :], dob_s[u])

        @pl.when(n == 0)
        def _():
            for ref in (dkp_acc, dvp_acc, dkp16, dvp16, k16p, v16p):
                ref[...] = jnp.zeros_like(ref)

        @pl.when(n < nchunk)
        def _():
            for ref in (dq_acc, dkc_acc, dvc_acc, dq16, dkc16, dvc16):
                ref[...] = jnp.zeros_like(ref)
            for src, dst in ((q_ref, q16), (k_ref, k16), (v_ref, v16), (do_ref, do16), (o_ref, o16)):
                _to_residue_major(src, tmp, dst)
            natural = (q_ref, do_ref, o_ref, k_ref, v_ref)
            for dil, l_ref in zip(DILATIONS[:-1], (l1_ref, l4_ref)):
                nb = nblk // dil
                units = [(_unit_rows(dil, r, j), _unit_rows(dil, r, (j - 1) % nb), j == 0, r * nb + j,
                          r * nb + j + 1 if j + 1 < nb else None) for r in range(dil) for j in range(nb)]
                group(units, natural, (kp_ref, vp_ref), l_ref, (dq_acc, dkc_acc, dvc_acc, dkp_acc, dvp_acc))
            blocks = [pl.ds(r * KEY_BLOCK, KEY_BLOCK) for r in range(wide)]
            group([(rows, rows, True, r, None) for r, rows in enumerate(blocks)], (q16, do16, o16, k16, v16),
                  (k16p, v16p), l16_ref, (dq16, dkc16, dvc16, dkp16, dvp16))
            _from_residue_major(dq16, tmp, dq_acc, True)
            dq = dq_acc[...]
            dq_ref[...] = ((dq * cos_ref[...] - _rope_partner(dq) * sin_ref[...]) * ATTN_SCALE).astype(BF16)

        @pl.when(n > 0)
        def _():
            _from_residue_major(dkp16, tmp, dkp_acc, True)
            _from_residue_major(dvp16, tmp, dvp_acc, True)
            dk = dkp_acc[...]
            dk_ref[...] = (dk * cosp_ref[...] - _rope_partner(dk) * sinp_ref[...]).astype(BF16)
            dv_ref[...] = dvp_acc[...].astype(BF16)

        @pl.when(n < nchunk)
        def _():
            for src, dst in ((dkc_acc, dkp_acc), (dvc_acc, dvp_acc), (dkc16, dkp16), (dvc16, dvp16),
                             (k16, k16p), (v16, v16p)):
                dst[...] = src[...]

    last = nchunk - 1
    cur = lambda h, n: (h, jnp.minimum(n, last), 0)
    prev = lambda h, n: (h, jnp.clip(n - 1, 0, last), 0)
    blk = lambda idx: pl.BlockSpec((None, SPAN, HEAD_DIM), idx)
    lblk = pl.BlockSpec((None, nblk, KEY_BLOCK), cur)
    tab = pl.BlockSpec((SPAN, HEAD_DIM), lambda h, n: (jnp.minimum(n, last), 0))
    tabp = pl.BlockSpec((SPAN, HEAD_DIM), lambda h, n: (jnp.clip(n - 1, 0, last), 0))
    out_q = pl.BlockSpec((SPAN, HEAD_DIM), lambda h, n: (jnp.minimum(n, last), h))
    out_kv = pl.BlockSpec((SPAN, HEAD_DIM), lambda h, n: (jnp.clip(n - 1, 0, last), h))
    shape = jax.ShapeDtypeStruct((seq, nh * HEAD_DIM), BF16)
    return pl.pallas_call(
        body, name="attn_bwd", out_shape=[shape, shape, shape], grid=(nh, nchunk + 1),
        in_specs=[blk(cur)] * 5 + [blk(prev)] * 2 + [lblk] * 3 + [tab, tab, tabp, tabp],
        out_specs=[out_q, out_kv, out_kv],
        scratch_shapes=[pltpu.VMEM((SPAN, HEAD_DIM), F32)] * 18
                       + [pltpu.VMEM((nblk, 2 * KEY_BLOCK, HEAD_DIM), BF16)] * 3
                       + [pltpu.VMEM((nblk, KEY_BLOCK, HEAD_DIM), BF16)] * 2,
        compiler_params=_params(("arbitrary", "arbitrary"), VMEM_LIMIT),
    )(q, k, v, do, o, k, v, *lses, cosf, sinf, cosf, sinf)


def _hub(x, tgt, hr, pf, o_hm, mod, b_mod, b_gate, g_final, w_out_rnn, w_out_attn, w_o):
    seq = x.shape[0]
    tm = HUB_ROWS
    nsteps = seq // tm

    def body(x_ref, t_ref, hr_ref, zr_ref, za_ref, gr_ref, ga_ref, o_ref, mod_ref, bmod_ref, bg_ref, gf_ref,
             wr_hbm, wa_hbm, wo_hbm,
             dx2_ref, dhr_ref, dzr_ref, do_ref, dza_ref, dgr_ref, dga_ref,
             ur_ref, dyr_ref, ua_ref, dya_ref, mg_ref, dmo_ref,
             ggf_ref, gbg_ref, dgate_ref, loss_ref,
             wr, wa, wo, sem):
        step = pl.program_id(0)

        @pl.when(step == 0)
        def _():
            for src, dst in ((wr_hbm, wr), (wa_hbm, wa), (wo_hbm, wo)):
                cp = pltpu.make_async_copy(src, dst, sem)
                cp.start()
                cp.wait()
            for ref in (ggf_ref, gbg_ref, dgate_ref, loss_ref):
                ref[...] = jnp.zeros_like(ref)

        gate = mod_ref[:, 2 * D_MODEL:] + bmod_ref[:, 2 * D_MODEL:]
        gfin = gf_ref[...]
        hr_t, zr, za = hr_ref[...], zr_ref[...], za_ref[...]
        o = jnp.concatenate([o_ref[hh] for hh in range(N_HEADS)], axis=1)
        sig_zr, sig_za = _sigmoid(zr), _sigmoid(za)
        silu_zr, silu_za = zr * sig_zr, za * sig_za
        u_rnn = (hr_t * silu_zr).astype(BF16)
        u_attn = (o * silu_za).astype(BF16)
        y_rnn = _dot(u_rnn, wr[...])
        y_attn = _dot(u_attn, wa[...])
        sr = _sigmoid(gr_ref[...] + bg_ref[:, :D_MODEL])
        sa = _sigmoid(ga_ref[...] + bg_ref[:, D_MODEL:])
        merged = (sr * y_rnn + sa * y_attn).astype(BF16)
        mo = _dot(merged, wo[...])
        x2 = x_ref[...] + gate * mo
        rstd = lax.rsqrt(jnp.mean(x2 * x2, axis=-1, keepdims=True) + NORM_EPS)
        xn = x2 * rstd
        err = xn * gfin - t_ref[...]
        loss_ref[...] += 0.5 * jnp.sum(jnp.sum(err * err, axis=-1, keepdims=True) * (1.0 / D_MODEL),
                                       axis=0, keepdims=True)

        dy = err * (1.0 / D_MODEL)
        ggf_ref[...] += jnp.sum(dy * xn, axis=0, keepdims=True)
        dxn = dy * gfin
        dx2 = rstd * (dxn - xn * jnp.mean(dxn * xn, axis=-1, keepdims=True))
        dx2_ref[...] = dx2
        dgate_ref[...] += jnp.sum(dx2 * mo, axis=0, keepdims=True)
        dmo = (dx2 * gate).astype(BF16)
        dmerged = _dot_nt(dmo, wo[...])
        mg_ref[...] = merged
        dmo_ref[...] = dmo
        dy_rnn = (dmerged * sr).astype(BF16)
        dy_attn = (dmerged * sa).astype(BF16)
        dg_r = dmerged * y_rnn * sr * (1.0 - sr)
        dg_a = dmerged * y_attn * sa * (1.0 - sa)
        dgr_ref[...] = dg_r.astype(BF16)
        dga_ref[...] = dg_a.astype(BF16)
        gbg_ref[:, :D_MODEL] += jnp.sum(dg_r, axis=0, keepdims=True)
        gbg_ref[:, D_MODEL:] += jnp.sum(dg_a, axis=0, keepdims=True)
        du_rnn = _dot_nt(dy_rnn, wr[...])
        du_attn = _dot_nt(dy_attn, wa[...])
        ur_ref[...] = u_rnn
        dyr_ref[...] = dy_rnn
        ua_ref[...] = u_attn
        dya_ref[...] = dy_attn
        dhr_ref[...] = du_rnn * silu_zr
        dzr_ref[...] = (du_rnn * hr_t * (sig_zr * (1.0 + zr * (1.0 - sig_zr)))).astype(BF16)
        dza_ref[...] = (du_attn * o * (sig_za * (1.0 + za * (1.0 - sig_za)))).astype(BF16)
        d_o = du_attn * silu_za
        for hh in range(N_HEADS):
            do_ref[hh] = d_o[:, hh * HEAD_DIM:(hh + 1) * HEAD_DIM]

    row = pl.BlockSpec((tm, D_MODEL), lambda i: (i, 0))
    piece = lambda slot: pl.BlockSpec((tm, D_MODEL), lambda i: (i, slot))
    hm = pl.BlockSpec((N_HEADS, tm, HEAD_DIM), lambda i: (0, i, 0))
    const = lambda cols: pl.BlockSpec((1, cols), lambda i: (0, 0))
    any_spec = pl.BlockSpec(memory_space=pl.ANY)
    act_f32 = jax.ShapeDtypeStruct((seq, D_MODEL), F32)
    act_bf16 = jax.ShapeDtypeStruct((seq, D_MODEL), BF16)
    return pl.pallas_call(
        body, name="hub",
        out_shape=[act_f32, act_f32, act_bf16, jax.ShapeDtypeStruct((N_HEADS, seq, HEAD_DIM), F32),
                   act_bf16, act_bf16, act_bf16] + [act_bf16] * 6 + [
                   jax.ShapeDtypeStruct((1, D_MODEL), F32), jax.ShapeDtypeStruct((1, 2 * D_MODEL), F32),
                   jax.ShapeDtypeStruct((1, D_MODEL), F32), jax.ShapeDtypeStruct((1, 1), F32)],
        grid=(nsteps,),
        in_specs=[row, row, row, piece(1), piece(2), piece(3), piece(4), hm,
                  const(3 * D_MODEL), const(3 * D_MODEL), const(2 * D_MODEL), const(D_MODEL),
                  any_spec, any_spec, any_spec],
        out_specs=[row, row, row, hm, row, row, row] + [row] * 6 + [
                   const(D_MODEL), const(2 * D_MODEL), const(D_MODEL), const(1)],
        scratch_shapes=[pltpu.VMEM((D_MODEL, D_MODEL), BF16)] * 3 + [pltpu.SemaphoreType.DMA],
        compiler_params=_params(("arbitrary",), VMEM_LIMIT),
    )(x, tgt, hr, pf, pf, pf, pf, o_hm, mod, b_mod, b_gate, g_final, w_out_rnn, w_out_attn, w_o)


def _pair_grads(name, lefts, rights):
    n = len(rights)
    shared = len(lefts) == 1
    seq = rights[0].shape[0]
    tk = WGRAD_ROWS
    nk = seq // tk

    def body(*refs):
        l_refs, r_refs, out_ref = refs[:len(lefts)], refs[len(lefts):len(lefts) + n], refs[len(lefts) + n]
        j, kk = pl.program_id(0), pl.program_id(1)

        @pl.when(kk == 0)
        def _():
            out_ref[...] = jnp.zeros_like(out_ref)

        for m in range(n):
            @pl.when(j == m)
            def _(m=m):
                out_ref[...] += _dot_tn(l_refs[0 if shared else m][...], r_refs[m][...])

    def spec(m):
        return pl.BlockSpec((tk, D_MODEL), lambda j, kk: (jnp.where(j == m, kk, jnp.where(j < m, 0, nk - 1)), 0))

    left_specs = [pl.BlockSpec((tk, D_MODEL), lambda j, kk: (kk, 0))] if shared else [spec(m) for m in range(n)]
    return pl.pallas_call(
        body, name=name,
        out_shape=jax.ShapeDtypeStruct((n, D_MODEL, D_MODEL), F32),
        grid=(n, nk),
        in_specs=left_specs + [spec(m) for m in range(n)],
        out_specs=pl.BlockSpec((None, D_MODEL, D_MODEL), lambda j, kk: (j, 0, 0)),
        compiler_params=_params(("arbitrary", "arbitrary"), VMEM_LIMIT),
    )(*lefts, *rights)


def _dh_dx(pieces, w_in_all, x, dx2, mod, b_mod, g_norm):
    seq = x.shape[0]
    tm = DX_ROWS

    def body(*refs):
        p_refs = refs[:8]
        w_hbm, x_ref, dx2_ref, mod_ref, bmod_ref, g_ref = refs[8:14]
        gx_ref, dshift_ref, dscale_ref, ggn_ref, w_scr, sem = refs[14:]
        step = pl.program_id(0)

        @pl.when(step == 0)
        def _():
            cp = pltpu.make_async_copy(w_hbm, w_scr, sem)
            cp.start()
            cp.wait()
            for ref in (dshift_ref, dscale_ref, ggn_ref):
                ref[...] = jnp.zeros_like(ref)

        dh = _dot_nt(p_refs[0][...], w_scr[0])
        for j in range(1, 8):
            dh = dh + _dot_nt(p_refs[j][...], w_scr[j])
        scale1 = 1.0 + mod_ref[:, D_MODEL:2 * D_MODEL] + bmod_ref[:, D_MODEL:2 * D_MODEL]
        g = g_ref[...]
        xf = x_ref[...]
        rstd_t = lax.rsqrt(jnp.mean(xf * xf, axis=-1, keepdims=True) + NORM_EPS)
        xn = xf * rstd_t
        dshift_ref[...] += jnp.sum(dh, axis=0, keepdims=True)
        dscale_ref[...] += jnp.sum(dh * (xn * g), axis=0, keepdims=True)
        ggn_ref[...] += jnp.sum(dh * scale1 * xn, axis=0, keepdims=True)
        dxn = dh * (g * scale1)
        gx_ref[...] = rstd_t * (dxn - xn * jnp.mean(dxn * xn, axis=-1, keepdims=True)) + dx2_ref[...]

    row = pl.BlockSpec((tm, D_MODEL), lambda i: (i, 0))
    const = lambda cols: pl.BlockSpec((1, cols), lambda i: (0, 0))
    vec = jax.ShapeDtypeStruct((1, D_MODEL), F32)
    return pl.pallas_call(
        body, name="dh_dx",
        out_shape=[jax.ShapeDtypeStruct((seq, D_MODEL), F32), vec, vec, vec],
        grid=(seq // tm,),
        in_specs=[row] * 8 + [pl.BlockSpec(memory_space=pl.ANY), row, row,
                              const(3 * D_MODEL), const(3 * D_MODEL), const(D_MODEL)],
        out_specs=[row, const(D_MODEL), const(D_MODEL), const(D_MODEL)],
        scratch_shapes=[pltpu.VMEM((8, D_MODEL, D_MODEL), BF16), pltpu.SemaphoreType.DMA],
        compiler_params=_params(("arbitrary",), VMEM_LIMIT),
    )(*pieces, w_in_all, x, dx2, mod, b_mod, g_norm)


def _adamw(name, w, g, m, v):
    rows, cols = w.shape
    tr = rows if rows <= 256 else 256

    def body(w_ref, g_ref, m_ref, v_ref, d_ref, nm_ref, nv_ref):
        gv = g_ref[...]
        nm = ADAM_B1 * m_ref[...] + (1.0 - ADAM_B1) * gv
        nv = ADAM_B2 * v_ref[...] + (1.0 - ADAM_B2) * (gv * gv)
        m_hat = nm / (1.0 - ADAM_B1 ** ADAM_STEP)
        v_hat = nv / (1.0 - ADAM_B2 ** ADAM_STEP)
        d_ref[...] = -ADAM_LR * (m_hat / (jnp.sqrt(v_hat) + ADAM_EPS) + ADAM_WD * w_ref[...])
        nm_ref[...] = nm
        nv_ref[...] = nv

    spec = pl.BlockSpec((tr, cols), lambda i: (i, 0))
    shape = jax.ShapeDtypeStruct((rows, cols), F32)
    return pl.pallas_call(
        body, name=name, out_shape=[shape, shape, shape], grid=(rows // tr,),
        in_specs=[spec] * 4, out_specs=[spec] * 3,
        compiler_params=_params(("arbitrary",)),
    )(w, g, m, v)


def kernel(x, c, positions, g_norm, w_mod, b_mod, w_in, b_gate, conv_w, conv_b, w_a, b_a, w_x, b_x, lam, w_out_rnn, w_out_attn, w_o, g_final, loss_target, m_g_norm, m_w_mod, m_b_mod, m_w_in, m_b_gate, m_conv_w, m_conv_b, m_w_a, m_b_a, m_w_x, m_b_x, m_lam, m_w_out_rnn, m_w_out_attn, m_w_o, m_g_final, v_g_norm, v_w_mod, v_b_mod, v_w_in, v_b_gate, v_conv_w, v_conv_b, v_w_a, v_b_a, v_w_x, v_b_x, v_lam, v_w_out_rnn, v_w_out_attn, v_w_o, v_g_final):
    seq = x.shape[1]
    me = _index(_my_pos())
    xs, tgt = x[0], loss_target[0]

    pos = positions[0].astype(F32)[:, None]
    inv_freq = ROPE_THETA ** (-jnp.arange(0, 2 * ROT_HALF, 2, dtype=F32) / (2 * ROT_HALF))
    ang = pos * inv_freq
    rest = HEAD_DIM - 2 * ROT_HALF
    cosf = jnp.concatenate([jnp.cos(ang), jnp.cos(ang), jnp.ones((seq, rest), F32)], axis=1)
    sinf = jnp.concatenate([-jnp.sin(ang), jnp.sin(ang), jnp.zeros((seq, rest), F32)], axis=1)
    keep = (positions[0] != 0).astype(F32)[:, None]

    (w_in_all,) = _ag_big("gather_weights", [w_in[0].astype(BF16)])
    both = _ag_small("gather_c_conv_w", jnp.concatenate(
        [jnp.broadcast_to(c, (SUBLANES, D_MODEL)), jnp.pad(conv_w[0], ((0, SUBLANES - 4), (0, 0)))], axis=1))
    c_all, conv_w8 = both[:, 0, :D_MODEL], both[:, :, D_MODEL:]
    mod_cols = w_mod.shape[2]
    mod_part = _ag_small("gather_mod", _mod_fwd(c_all, w_mod[0]))
    mod = lax.dynamic_index_in_dim(mod_part, me, axis=1, keepdims=False).reshape(1, N_DEV * mod_cols)
    mod, late = lax.optimization_barrier(
        (mod, [w_out_rnn[0].astype(BF16), w_out_attn[0].astype(BF16), w_o[0].astype(BF16)]))
    late_sends, late_recvs, late_shards, late_lands, late_token = _gather_start(late, me)
    mod = mod + late_token[0:1, 0:1]

    blocks = lambda t: t.reshape(RNN_BLOCKS, 1, 128)
    rnn_params = (conv_w8, blocks(conv_b), w_a[0], blocks(b_a), w_x[0], blocks(b_x), blocks(lam))

    h = _norm(xs, mod, b_mod, g_norm)
    pf, q, k, v = _proj(h, w_in_all, cosf, sinf)
    hr = _rnn_fwd(pf, keep, *rnn_params)
    o, lses = _attn_fwd(q, k, v)

    w_or_all, w_oa_all, w_o_all = (t.reshape(D_MODEL, D_MODEL) for t in _gather_wait(
        late_sends, late_recvs, late_shards, late_lands, o))
    (dx2, dhr, dz_rnn, d_o, dz_attn, dg_r, dg_a, u_rnn, dy_rnn, u_attn, dy_attn, merged, dmo,
     gp_g_final, gp_b_gate, dgate, loss_part) = _hub(
        xs, tgt, hr, pf, o, mod, b_mod, b_gate, g_final.reshape(1, D_MODEL), w_or_all, w_oa_all, w_o_all)
    gp_w_or, gp_w_oa, gp_w_o = _pair_grads("out_grads", [u_rnn, u_attn, merged], [dy_rnn, dy_attn, dmo])
    dq, dk, dv = _attn_bwd(q, k, v, d_o, o, lses, cosf, sinf)
    dx_rnn, gp_conv_w, gp_conv_b, gp_w_a, gp_b_a, gp_w_x, gp_b_x, gp_lam = _rnn_bwd(pf, hr, dhr, keep, *rnn_params)
    pieces = [dx_rnn, dz_rnn, dq, dk, dv, dz_attn, dg_r, dg_a]
    gp_w_in = _pair_grads("w_in_grad", [h], pieces)

    stacks = [gp_w_in, gp_w_or.reshape(N_DEV, 128, D_MODEL), gp_w_oa.reshape(N_DEV, 128, D_MODEL),
              gp_w_o.reshape(N_DEV, 128, D_MODEL)]
    from_sib = _rs_to_sibling("rs_sibling", stacks)
    targets = jnp.bitwise_xor(me, 2 * jnp.arange(4, dtype=jnp.int32)).astype(jnp.int32)
    sums = [_add_sibling("rs_add_sibling_%d" % a, s_, r_, targets) for a, (s_, r_) in enumerate(zip(stacks, from_sib))]
    send_sems, recv_sems, sent, landing, token = _rs_chips_start([send for _, send in sums])

    mod_after = mod + token[0:1, 0:1]
    grad_x, dshift, dscale, gp_g_norm = _dh_dx(pieces, w_in_all, xs, dx2, mod_after, b_mod, g_norm)

    dmod = jnp.concatenate([dshift, dscale, dgate], axis=1)
    dmod_all = _ag_small("gather_dmod", jnp.broadcast_to(dmod, (SUBLANES, 3 * D_MODEL)))[:, 0, :]
    dmod_cols = lax.dynamic_slice_in_dim(dmod_all, me * mod_cols, mod_cols, axis=1)
    g_b_mod, g_w_mod = _mod_bwd(c_all, dmod_all, dmod_cols)

    flat = lambda t: t.reshape(-1, 128)
    small = [flat(gp_g_norm), flat(gp_b_gate), flat(gp_conv_b), flat(gp_b_a), flat(gp_b_x), flat(gp_lam),
             flat(gp_g_final), flat(gp_conv_w), jnp.broadcast_to(loss_part, (SUBLANES, 128)),
             flat(gp_w_a), flat(gp_w_x)]
    sizes = [t.shape[0] for t in small]
    small.append(jnp.zeros((-sum(sizes) % (2 * SUBLANES), 128), F32))
    total = _allreduce_small("allreduce_small_grads", jnp.concatenate(small, axis=0))
    offs = [sum(sizes[:i]) for i in range(len(sizes))]
    (g_g_norm, g_b_gate, g_conv_b, g_b_a, g_b_x, g_lam, g_g_final, g_conv_w_all, loss_rows, g_w_a, g_w_x) = (
        total[o_:o_ + s_] for o_, s_ in zip(offs, sizes))
    loss = loss_rows[0, 0]
    g_conv_w = lax.dynamic_index_in_dim(g_conv_w_all.reshape(RNN_BLOCKS, SUBLANES, 128), me, axis=0,
                                        keepdims=False)[:4]

    from_chips = _rs_chips_wait(send_sems, recv_sems, sent, landing, total)
    g_w_in, g_w_or, g_w_oa, g_w_o = (
        _add_chips("rs_add_chips_%d" % a, own, r_) for a, ((own, _), r_) in enumerate(zip(sums, from_chips)))

    weights = [
        ("g_norm", g_norm, g_g_norm, m_g_norm, v_g_norm, (SUBLANES, 128)),
        ("w_mod", w_mod, g_w_mod, m_w_mod, v_w_mod, (D_MODEL, mod_cols)),
        ("b_mod", b_mod, g_b_mod, m_b_mod, v_b_mod, (3 * SUBLANES, 128)),
        ("w_in", w_in, g_w_in, m_w_in, v_w_in, (D_MODEL, D_MODEL)),
        ("b_gate", b_gate, g_b_gate, m_b_gate, v_b_gate, (2 * SUBLANES, 128)),
        ("conv_w", conv_w, g_conv_w, m_conv_w, v_conv_w, (4, 128)),
        ("conv_b", conv_b, g_conv_b, m_conv_b, v_conv_b, (SUBLANES, 128)),
        ("w_a", w_a, g_w_a, m_w_a, v_w_a, (RNN_BLOCKS * 128, 128)),
        ("b_a", b_a, g_b_a, m_b_a, v_b_a, (SUBLANES, 128)),
        ("w_x", w_x, g_w_x, m_w_x, v_w_x, (RNN_BLOCKS * 128, 128)),
        ("b_x", b_x, g_b_x, m_b_x, v_b_x, (SUBLANES, 128)),
        ("lam", lam, g_lam, m_lam, v_lam, (SUBLANES, 128)),
        ("w_out_rnn", w_out_rnn, g_w_or, m_w_out_rnn, v_w_out_rnn, (128, D_MODEL)),
        ("w_out_attn", w_out_attn, g_w_oa, m_w_out_attn, v_w_out_attn, (128, D_MODEL)),
        ("w_o", w_o, g_w_o, m_w_o, v_w_o, (128, D_MODEL)),
        ("g_final", g_final, g_g_final, m_g_final, v_g_final, (SUBLANES, 128)),
    ]
    out_g, out_d, out_m, out_v = [], [], [], []
    for name, w_, g_, m_, v_, shape2 in weights:
        d_, nm_, nv_ = _adamw("adamw_" + name, w_.reshape(shape2), g_.reshape(shape2), m_.reshape(shape2),
                              v_.reshape(shape2))
        out_g.append(g_.reshape(w_.shape))
        out_d.append(d_.reshape(w_.shape))
        out_m.append(nm_.reshape(w_.shape))
        out_v.append(nv_.reshape(w_.shape))
    return (loss, grad_x[None], *out_g, *out_d, *out_m, *out_v)
```

```python
import jax
import jax.numpy as jnp
from jax import lax
from jax.experimental import pallas as pl
from jax.experimental.pallas import tpu as pltpu

F32 = jnp.float32
BF16 = jnp.bfloat16
MESH = pl.DeviceIdType.MESH

D_MODEL = 1024
N_HEADS = 8
HEAD_DIM = 128
RNN_BLOCKS = 8
N_DEV = 8
ROT_HALF = 16
ROPE_THETA = 500000.0
DILATIONS = (1, 4, 16)
KEY_BLOCK = 128
SPAN = KEY_BLOCK * DILATIONS[-1]
ATTN_SCALE = HEAD_DIM ** -0.5
NORM_EPS = 1e-6
LRU_C = 8.0
NEG_INF = -1e30
ADAM_LR, ADAM_B1, ADAM_B2, ADAM_EPS, ADAM_WD, ADAM_STEP = 0.001, 0.9, 0.999, 1e-08, 0.01, 10

SUBLANES = 8
VMEM_LIMIT = 56 * 1024 * 1024
PROJ_ROWS = 1024
RNN_ROWS = 1024
HUB_ROWS = 256
DX_ROWS = 512
WGRAD_ROWS = 1024
ADD_ROWS = 256


def _params(sem=None, vmem=None):
    return pltpu.CompilerParams(dimension_semantics=sem, vmem_limit_bytes=vmem)


def _dot(a, b):
    return jnp.dot(a, b, preferred_element_type=F32)


def _dot_nt(a, b):
    return lax.dot_general(a, b, (((1,), (1,)), ((), ())), preferred_element_type=F32)


def _dot_tn(a, b):
    return lax.dot_general(a, b, (((0,), (0,)), ((), ())), preferred_element_type=F32)


def _sigmoid(z):
    return 1.0 / (1.0 + jnp.exp(-z))


def _expm1_nonpos(z, exp_z):
    return jnp.where(z > -0.01, z * (1.0 + 0.5 * z), exp_z - 1.0)


def _my_pos():
    return lax.axis_index("x"), lax.axis_index("y"), lax.axis_index("c")


def _flip(pos, k):
    x, y, c = pos
    return ((1 - x) if k & 4 else x, (1 - y) if k & 2 else y, (1 - c) if k & 1 else c)


def _index(pos):
    return 4 * pos[0] + 2 * pos[1] + pos[2]


def _ag_small(name, v):
    rows, cols = v.shape

    def body(v_ref, out_ref, send_sems, recv_sems):
        me = _my_pos()
        out_ref[_index(me)] = v_ref[...]
        sends = []
        for k in range(1, N_DEV):
            cp = pltpu.make_async_remote_copy(
                src_ref=v_ref, dst_ref=out_ref.at[_index(me)], send_sem=send_sems.at[k - 1],
                recv_sem=recv_sems.at[k - 1], device_id=_flip(me, k), device_id_type=MESH)
            cp.start()
            sends.append(cp)
        for k in range(1, N_DEV):
            peer = _flip(me, k)
            pltpu.make_async_remote_copy(
                src_ref=v_ref, dst_ref=out_ref.at[_index(peer)], send_sem=send_sems.at[k - 1],
                recv_sem=recv_sems.at[k - 1], device_id=peer, device_id_type=MESH).wait_recv()
        for cp in sends:
            cp.wait_send()

    return pl.pallas_call(
        body, name=name,
        out_shape=jax.ShapeDtypeStruct((N_DEV, rows, cols), v.dtype),
        in_specs=[pl.BlockSpec(memory_space=pltpu.VMEM)],
        out_specs=pl.BlockSpec(memory_space=pltpu.VMEM),
        scratch_shapes=[pltpu.SemaphoreType.DMA((N_DEV - 1,)), pltpu.SemaphoreType.DMA((N_DEV - 1,))],
        compiler_params=_params(None, VMEM_LIMIT),
    )(v)


def _ag_big(name, shards):
    n = len(shards)

    def body(*refs):
        ins, outs = refs[:n], refs[n:2 * n]
        send_sems, recv_sems, local_sems = refs[2 * n:]
        me = _my_pos()
        sib = _flip(me, 1)
        chips = [2, 4, 6]

        def copy(a, k, block, to, src=None):
            rows = outs[a].at[_index(block)]
            return pltpu.make_async_remote_copy(
                src_ref=rows if src is None else src, dst_ref=rows,
                send_sem=send_sems.at[a * 7 + k], recv_sem=recv_sems.at[a * 7 + k],
                device_id=to, device_id_type=MESH)

        started = []
        for a in range(n):
            mine = pltpu.make_async_copy(ins[a], outs[a].at[_index(me)], local_sems.at[a])
            mine.start()
            started.append(mine)
        sends = []
        for a in range(n):
            first = [copy(a, 0, me, sib, src=ins[a])]
            first += [copy(a, 1 + j, me, _flip(me, ch), src=ins[a]) for j, ch in enumerate(chips)]
            for cp in first:
                cp.start()
            sends += first
        for j, ch in enumerate(chips):
            for a in range(n):
                copy(a, 1 + j, _flip(me, ch), me).wait_recv()
                fwd = copy(a, 4 + j, _flip(me, ch), sib)
                fwd.start()
                sends.append(fwd)
        for a in range(n):
            copy(a, 0, sib, me).wait_recv()
            for j, ch in enumerate(chips):
                copy(a, 4 + j, _flip(sib, ch), me).wait_recv()
        for cp in sends:
            cp.wait_send()
        for mine in started:
            mine.wait()

    any_spec = pl.BlockSpec(memory_space=pl.ANY)
    return pl.pallas_call(
        body, name=name,
        out_shape=[jax.ShapeDtypeStruct((N_DEV,) + s.shape, s.dtype) for s in shards],
        in_specs=[any_spec] * n, out_specs=[any_spec] * n,
        scratch_shapes=[pltpu.SemaphoreType.DMA((7 * n,)), pltpu.SemaphoreType.DMA((7 * n,)),
                        pltpu.SemaphoreType.DMA((n,))],
    )(*shards)


def _peer_copies(shards, lands, send_sems, recv_sems):
    me = _my_pos()
    return [pltpu.make_async_remote_copy(
        src_ref=shards[a], dst_ref=lands[a].at[_index(me)],
        send_sem=send_sems.at[a * 7 + k - 1], recv_sem=recv_sems.at[a * 7 + k - 1],
        device_id=_flip(me, k), device_id_type=MESH) for a in range(len(shards)) for k in range(1, N_DEV)]


def _gather_start(shards, me):
    n = len(shards)

    def body(*refs):
        srcs, lands = refs[:n], refs[n:2 * n]
        send_sems, recv_sems = refs[2 * n:2 * n + 2]
        for cp in _peer_copies(srcs, lands, send_sems, recv_sems):
            cp.start()
        refs[-1][...] = jnp.zeros_like(refs[-1])

    hbm = pl.BlockSpec(memory_space=pltpu.HBM)
    sem = pl.BlockSpec(memory_space=pltpu.SEMAPHORE)
    held = [pltpu.HBM(s.shape, s.dtype) for s in shards]
    landing = [lax.dynamic_update_slice(jnp.zeros((N_DEV,) + s.shape, s.dtype), s[None], (me, 0, 0)) for s in shards]
    held_land = [pltpu.HBM(t.shape, t.dtype) for t in landing]
    outs = pl.pallas_call(
        body, name="gather_out_weights_start",
        out_shape=(pltpu.SemaphoreType.DMA((7 * n,)), pltpu.SemaphoreType.DMA((7 * n,)), *held, *held_land,
                   jax.ShapeDtypeStruct((SUBLANES, 128), F32)),
        in_specs=[hbm] * (2 * n),
        out_specs=(sem, sem, *[hbm] * (2 * n), pl.BlockSpec(memory_space=pltpu.VMEM)),
        input_output_aliases={i: 2 + i for i in range(2 * n)},
        compiler_params=pltpu.CompilerParams(has_side_effects=pltpu.SideEffectType.DATAFLOW_SIDE_EFFECTING),
    )(*[pltpu.with_memory_space_constraint(s, pltpu.HBM) for s in shards],
      *[pltpu.with_memory_space_constraint(t, pltpu.HBM) for t in landing])
    return outs[0], outs[1], outs[2:2 + n], outs[2 + n:2 + 2 * n], outs[-1]


def _gather_wait(send_sems, recv_sems, shards, lands, after):
    n = len(shards)

    def body(*refs):
        srcs, land_refs = refs[:n], refs[n:2 * n]
        sends, recvs = refs[2 * n:2 * n + 2]
        for cp in _peer_copies(srcs, land_refs, sends, recvs):
            cp.wait_send()
            cp.wait_recv()

    hbm = pl.BlockSpec(memory_space=pltpu.HBM)
    sem = pl.BlockSpec(memory_space=pltpu.SEMAPHORE)
    outs = pl.pallas_call(
        body, name="gather_out_weights_wait",
        out_shape=(*[pltpu.HBM(s.shape, s.dtype) for s in shards], *[pltpu.HBM(t.shape, t.dtype) for t in lands]),
        in_specs=[hbm] * (2 * n) + [sem, sem, pl.BlockSpec(memory_space=pl.ANY)],
        out_specs=[hbm] * (2 * n),
        input_output_aliases={i: i for i in range(2 * n)},
        compiler_params=pltpu.CompilerParams(has_side_effects=pltpu.SideEffectType.DATAFLOW_SIDE_EFFECTING),
    )(*shards, *lands, send_sems, recv_sems, after)
    return outs[n:]


def _rs_to_sibling(name, stacks):
    n = len(stacks)

    def body(*refs):
        ins, outs = refs[:n], refs[n:2 * n]
        send_sems, recv_sems = refs[2 * n:]
        me = _my_pos()
        sib = _flip(me, 1)
        sends = []
        for a in range(n):
            for m in range(4):
                target = _flip(sib, 2 * m)
                cp = pltpu.make_async_remote_copy(
                    src_ref=ins[a].at[_index(target)], dst_ref=outs[a].at[m],
                    send_sem=send_sems.at[a * 4 + m], recv_sem=recv_sems.at[a * 4 + m],
                    device_id=sib, device_id_type=MESH)
                cp.start()
                sends.append(cp)
        for cp in sends:
            cp.wait_recv()
        for cp in sends:
            cp.wait_send()

    any_spec = pl.BlockSpec(memory_space=pl.ANY)
    return pl.pallas_call(
        body, name=name,
        out_shape=[jax.ShapeDtypeStruct((4,) + s.shape[1:], s.dtype) for s in stacks],
        in_specs=[any_spec] * n, out_specs=[any_spec] * n,
        scratch_shapes=[pltpu.SemaphoreType.DMA((4 * n,)), pltpu.SemaphoreType.DMA((4 * n,))],
    )(*stacks)


def _chip_copies(srcs, lands, send_sems, recv_sems):
    me = _my_pos()
    return [pltpu.make_async_remote_copy(
        src_ref=srcs[a].at[m - 1], dst_ref=lands[a].at[m - 1],
        send_sem=send_sems.at[a * 3 + m - 1], recv_sem=recv_sems.at[a * 3 + m - 1],
        device_id=_flip(me, 2 * m), device_id_type=MESH) for a in range(len(srcs)) for m in range(1, 4)]


def _rs_chips_start(sums):
    n = len(sums)

    def body(*refs):
        srcs, lands = refs[:n], refs[n:2 * n]
        send_sems, recv_sems = refs[2 * n:2 * n + 2]
        token = refs[-1]
        for cp in _chip_copies(srcs, lands, send_sems, recv_sems):
            cp.start()
        token[...] = jnp.zeros_like(token)

    hbm = pl.BlockSpec(memory_space=pltpu.HBM)
    sem = pl.BlockSpec(memory_space=pltpu.SEMAPHORE)
    held = [pltpu.HBM(s.shape, s.dtype) for s in sums]
    outs = pl.pallas_call(
        body, name="rs_chips_start",
        out_shape=(pltpu.SemaphoreType.DMA((3 * n,)), pltpu.SemaphoreType.DMA((3 * n,)), *held, *held,
                   jax.ShapeDtypeStruct((SUBLANES, 128), F32)),
        in_specs=[hbm] * (2 * n),
        out_specs=(sem, sem, *[hbm] * (2 * n), pl.BlockSpec(memory_space=pltpu.VMEM)),
        input_output_aliases={i: 2 + i for i in range(2 * n)},
        compiler_params=pltpu.CompilerParams(has_side_effects=pltpu.SideEffectType.DATAFLOW_SIDE_EFFECTING),
    )(*[pltpu.with_memory_space_constraint(s, pltpu.HBM) for s in sums],
      *[pltpu.with_memory_space_constraint(lax.empty(s.shape, s.dtype), pltpu.HBM) for s in sums])
    return outs[0], outs[1], outs[2:2 + n], outs[2 + n:2 + 2 * n], outs[-1]


def _rs_chips_wait(send_sems, recv_sems, srcs, lands, after):
    n = len(srcs)

    def body(*refs):
        src_refs, land_refs = refs[:n], refs[n:2 * n]
        sends, recvs = refs[2 * n:2 * n + 2]
        for cp in _chip_copies(src_refs, land_refs, sends, recvs):
            cp.wait_send()
            cp.wait_recv()

    hbm = pl.BlockSpec(memory_space=pltpu.HBM)
    sem = pl.BlockSpec(memory_space=pltpu.SEMAPHORE)
    held = [pltpu.HBM(s.shape, s.dtype) for s in srcs]
    outs = pl.pallas_call(
        body, name="rs_chips_wait", out_shape=(*held, *held),
        in_specs=[hbm] * (2 * n) + [sem, sem, pl.BlockSpec(memory_space=pl.ANY)],
        out_specs=[hbm] * (2 * n),
        input_output_aliases={i: i for i in range(2 * n)},
        compiler_params=pltpu.CompilerParams(has_side_effects=pltpu.SideEffectType.DATAFLOW_SIDE_EFFECTING),
    )(*srcs, *lands, send_sems, recv_sems, after)
    return outs[n:]


def _add_sibling(name, stack, recv, targets):
    _, rows, cols = stack.shape
    tr = min(rows, ADD_ROWS)

    def own_body(t_ref, a_ref, b_ref, o_ref):
        o_ref[...] = a_ref[...] + b_ref[...]

    own = pl.pallas_call(
        own_body, name=name + "_own",
        out_shape=jax.ShapeDtypeStruct((rows, cols), F32),
        grid_spec=pltpu.PrefetchScalarGridSpec(
            num_scalar_prefetch=1, grid=(rows // tr,),
            in_specs=[pl.BlockSpec((None, tr, cols), lambda i, t: (t[0], i, 0)),
                      pl.BlockSpec((None, tr, cols), lambda i, t: (0, i, 0))],
            out_specs=pl.BlockSpec((tr, cols), lambda i, t: (i, 0))),
        compiler_params=_params(("arbitrary",)),
    )(targets, stack, recv)

    def send_body(t_ref, a_ref, b_ref, o_ref):
        o_ref[...] = (a_ref[...] + b_ref[...]).astype(BF16)

    send = pl.pallas_call(
        send_body, name=name + "_send",
        out_shape=jax.ShapeDtypeStruct((3, rows, cols), BF16),
        grid_spec=pltpu.PrefetchScalarGridSpec(
            num_scalar_prefetch=1, grid=(3, rows // tr),
            in_specs=[pl.BlockSpec((None, tr, cols), lambda m, i, t: (t[m + 1], i, 0)),
                      pl.BlockSpec((None, tr, cols), lambda m, i, t: (m + 1, i, 0))],
            out_specs=pl.BlockSpec((None, tr, cols), lambda m, i, t: (m, i, 0))),
        compiler_params=_params(("arbitrary", "arbitrary")),
    )(targets, stack, recv)
    return own, send


def _add_chips(name, own, recv):
    rows, cols = own.shape
    tr = min(rows, ADD_ROWS)

    def body(a_ref, b_ref, o_ref):
        o_ref[...] = ((a_ref[...] + b_ref[0].astype(F32)) + b_ref[1].astype(F32)) + b_ref[2].astype(F32)

    return pl.pallas_call(
        body, name=name,
        out_shape=jax.ShapeDtypeStruct((rows, cols), F32),
        grid=(rows // tr,),
        in_specs=[pl.BlockSpec((tr, cols), lambda i: (i, 0)),
                  pl.BlockSpec((3, tr, cols), lambda i: (0, i, 0))],
        out_specs=pl.BlockSpec((tr, cols), lambda i: (i, 0)),
        compiler_params=_params(("arbitrary",)),
    )(own, recv)


def _allreduce_small(name, v):
    rows, cols = v.shape
    half = rows // 2
    assert rows % (2 * SUBLANES) == 0

    def body(v_ref, out_ref, from_sib, chip_half, from_chips, send_sems, recv_sems):
        me = _my_pos()
        sib = _flip(me, 1)
        mine = pl.ds(pl.multiple_of(me[2] * half, SUBLANES), half)
        theirs = pl.ds(pl.multiple_of((1 - me[2]) * half, SUBLANES), half)

        def copy(k, src, dst, to):
            return pltpu.make_async_remote_copy(src_ref=src, dst_ref=dst, send_sem=send_sems.at[k],
                                                recv_sem=recv_sems.at[k], device_id=to, device_id_type=MESH)

        to_sib = copy(0, v_ref.at[theirs], from_sib, sib)
        to_sib.start()
        to_sib.wait_recv()
        chip_half[...] = v_ref[mine, :] + from_sib[...]
        to_chips = [copy(m, chip_half, from_chips.at[m - 1], _flip(me, 2 * m)) for m in range(1, 4)]
        for cp in to_chips:
            cp.start()
        for cp in to_chips:
            cp.wait_recv()
        my_chip = 2 * me[0] + me[1]
        total = None
        for chip in range(4):
            slot = jnp.maximum(jnp.bitwise_xor(chip, my_chip) - 1, 0)
            part = jnp.where(chip == my_chip, chip_half[...], from_chips[slot])
            total = part if total is None else total + part
        out_ref[mine, :] = total
        swap = copy(4, out_ref.at[mine], out_ref.at[mine], sib)
        swap.start()
        copy(4, out_ref.at[theirs], out_ref.at[theirs], sib).wait_recv()
        for cp in [to_sib, swap] + to_chips:
            cp.wait_send()

    return pl.pallas_call(
        body, name=name, out_shape=jax.ShapeDtypeStruct((rows, cols), F32),
        in_specs=[pl.BlockSpec(memory_space=pltpu.VMEM)],
        out_specs=pl.BlockSpec(memory_space=pltpu.VMEM),
        scratch_shapes=[pltpu.VMEM((half, cols), F32), pltpu.VMEM((half, cols), F32),
                        pltpu.VMEM((3, half, cols), F32),
                        pltpu.SemaphoreType.DMA((5,)), pltpu.SemaphoreType.DMA((5,))],
        compiler_params=_params(None, VMEM_LIMIT),
    )(v)


def _mod_fwd(c_all, w_mod):
    def body(c_ref, w_ref, o_ref):
        c = c_ref[...]
        o_ref[...] = jnp.dot(c * _sigmoid(c), w_ref[...], preferred_element_type=F32,
                             precision=lax.Precision.HIGHEST)

    return pl.pallas_call(
        body, name="mod_fwd", out_shape=jax.ShapeDtypeStruct((N_DEV, w_mod.shape[1]), F32),
    )(c_all, w_mod)


def _mod_bwd(c_all, dmod_all, dmod_cols):
    def body(c_ref, da_ref, dc_ref, gb_ref, gw_ref):
        c = c_ref[...]
        acc = da_ref[0:1, :]
        for b in range(1, N_DEV):
            acc = acc + da_ref[b:b + 1, :]
        gb_ref[...] = acc
        gw_ref[...] = lax.dot_general(c * _sigmoid(c), dc_ref[...], (((0,), (0,)), ((), ())),
                                      preferred_element_type=F32, precision=lax.Precision.HIGHEST)

    return pl.pallas_call(
        body, name="mod_bwd",
        out_shape=[jax.ShapeDtypeStruct((1, dmod_all.shape[1]), F32),
                   jax.ShapeDtypeStruct((c_all.shape[1], dmod_cols.shape[1]), F32)],
    )(c_all, dmod_all, dmod_cols)


def _rope_partner(t):
    lane = lax.broadcasted_iota(jnp.int32, t.shape, 1)
    return jnp.where(lane < ROT_HALF, pltpu.roll(t, HEAD_DIM - ROT_HALF, 1), pltpu.roll(t, ROT_HALF, 1))


def _norm(x, mod, b_mod, g_norm):
    seq = x.shape[0]
    tm = PROJ_ROWS

    def body(x_ref, mod_ref, bmod_ref, g_ref, h_ref):
        xf = x_ref[...]
        rstd = lax.rsqrt(jnp.mean(xf * xf, axis=-1, keepdims=True) + NORM_EPS)
        shift = mod_ref[:, 0:D_MODEL] + bmod_ref[:, 0:D_MODEL]
        scale = mod_ref[:, D_MODEL:2 * D_MODEL] + bmod_ref[:, D_MODEL:2 * D_MODEL]
        h_ref[...] = (((xf * rstd) * g_ref[...]) * (1.0 + scale) + shift).astype(BF16)

    row = pl.BlockSpec((tm, D_MODEL), lambda i: (i, 0))
    const = lambda cols: pl.BlockSpec((1, cols), lambda i: (0, 0))
    return pl.pallas_call(
        body, name="norm", out_shape=jax.ShapeDtypeStruct((seq, D_MODEL), BF16), grid=(seq // tm,),
        in_specs=[row, const(3 * D_MODEL), const(3 * D_MODEL), const(D_MODEL)], out_specs=row,
        compiler_params=_params(("arbitrary",), VMEM_LIMIT),
    )(x, mod, b_mod, g_norm)


def _proj(h, w_in_all, cosf, sinf):
    seq = h.shape[0]
    tm = PROJ_ROWS
    last = seq // tm - 1

    def body(h_ref, w_ref, cos_ref, sin_ref, pf_ref, q_ref, k_ref, v_ref):
        j = pl.program_id(0)

        @pl.when((j < 2) | (j > 4))
        def _():
            pf_ref[...] = _dot(h_ref[...], w_ref[...])

        def heads(dst_ref, rotate, gain):
            for pair in range(N_HEADS // 2):
                both = _dot(h_ref[...], w_ref[:, 2 * pair * HEAD_DIM:2 * (pair + 1) * HEAD_DIM])
                for hh in (2 * pair, 2 * pair + 1):
                    t = both[:, (hh % 2) * HEAD_DIM:(hh % 2 + 1) * HEAD_DIM]
                    if rotate:
                        t = t * cos_ref[...] + _rope_partner(t) * sin_ref[...]
                    dst_ref[hh] = t if gain is None else t * gain

        @pl.when(j == 2)
        def _():
            heads(q_ref, True, ATTN_SCALE)

        @pl.when(j == 3)
        def _():
            heads(k_ref, True, None)

        @pl.when(j == 4)
        def _():
            heads(v_ref, False, None)

    def pf_block(j, i):
        f32_piece = (j < 2) | (j > 4)
        return (jnp.where(f32_piece, i, last), jnp.where(j < 2, j, jnp.where(j < 5, 1, j - 3)))

    def hm_block(piece):
        return lambda j, i: (0, jnp.where(j == piece, i, jnp.where(j < piece, 0, last)), 0)

    hm = jax.ShapeDtypeStruct((N_HEADS, seq, HEAD_DIM), F32)
    hm_spec = lambda piece: pl.BlockSpec((N_HEADS, tm, HEAD_DIM), hm_block(piece))
    row = lambda j, i: (i, 0)
    return pl.pallas_call(
        body, name="proj",
        out_shape=[jax.ShapeDtypeStruct((seq, 5 * D_MODEL), F32), hm, hm, hm],
        grid=(8, seq // tm),
        in_specs=[pl.BlockSpec((tm, D_MODEL), row),
                  pl.BlockSpec((None, D_MODEL, D_MODEL), lambda j, i: (j, 0, 0)),
                  pl.BlockSpec((tm, HEAD_DIM), row), pl.BlockSpec((tm, HEAD_DIM), row)],
        out_specs=[pl.BlockSpec((tm, D_MODEL), pf_block), hm_spec(2), hm_spec(3), hm_spec(4)],
        compiler_params=_params(("arbitrary", "arbitrary"), VMEM_LIMIT),
    )(h, w_in_all, cosf, sinf)


def _shift_down(v, s, head):
    rolled = pltpu.roll(v, s, 0)
    row = lax.broadcasted_iota(jnp.int32, head.shape, 0)
    first = jnp.where(row < s, pltpu.roll(head, s, 0), rolled[:SUBLANES, :])
    return jnp.concatenate([first, rolled[SUBLANES:, :]], axis=0)


def _shift_up(v, s, tail):
    rows = v.shape[0]
    rolled = pltpu.roll(v, rows - s, 0)
    row = lax.broadcasted_iota(jnp.int32, tail.shape, 0)
    last = jnp.where(row >= SUBLANES - s, pltpu.roll(tail, SUBLANES - s, 0), rolled[rows - SUBLANES:, :])
    return jnp.concatenate([rolled[:rows - SUBLANES, :], last], axis=0)


def _doubling(a, b, period, reverse):
    rows = a.shape[0]
    pos = lax.broadcasted_iota(jnp.int32, a.shape, 0) & (period - 1)
    k = 1
    while k < period:
        inside = (pos < period - k) if reverse else (pos >= k)
        shift = rows - k if reverse else k
        a_s = jnp.where(inside, pltpu.roll(a, shift, 0), 1.0)
        b_s = jnp.where(inside, pltpu.roll(b, shift, 0), 0.0)
        b = a * b_s + b
        a = a * a_s
        k *= 2
    return a, b


def _scan(a, b, boundary, reverse, a_scr, b_scr, spread):
    rows = a.shape[0]
    ntile = rows // SUBLANES
    a_scr[...], b_scr[...] = _doubling(a, b, SUBLANES, reverse)
    ends = pl.ds(0 if reverse else SUBLANES - 1, ntile, stride=SUBLANES)
    a_end, x_end = _doubling(a_scr[ends, :], b_scr[ends, :], ntile, reverse)
    x_end = x_end + a_end * boundary
    tile = lax.broadcasted_iota(jnp.int32, x_end.shape, 0)
    if reverse:
        incoming = jnp.where(tile == ntile - 1, boundary, pltpu.roll(x_end, ntile - 1, 0))
        last = x_end[0:1, :]
    else:
        incoming = jnp.where(tile == 0, boundary, pltpu.roll(x_end, 1, 0))
        last = x_end[ntile - 1:ntile, :]
    for s in range(SUBLANES):
        spread[pl.ds(s, ntile, stride=SUBLANES), :] = incoming
    return b_scr[...] + a_scr[...] * spread[...], last


def _conv_taps(xr, head):
    return [_shift_down(xr, 3, head), _shift_down(xr, 2, head), _shift_down(xr, 1, head), xr]


def _rnn_gates(xc, wa, ba, wx, bx, lam, keep):
    xcb = xc.astype(BF16)
    r = _sigmoid(_dot(xcb, wa.astype(BF16)) + ba)
    i = _sigmoid(_dot(xcb, wx.astype(BF16)) + bx)
    softplus = jnp.maximum(-lam, 0.0) + jnp.log(1.0 + jnp.exp(-jnp.abs(lam)))
    cl = -LRU_C * softplus
    log_a = cl * r
    a_raw = jnp.exp(log_a)
    mult_raw = jnp.sqrt(-_expm1_nonpos(2.0 * log_a, a_raw * a_raw))
    live = keep > 0.0
    return r, i, cl, a_raw, mult_raw, jnp.where(live, a_raw, 0.0), jnp.where(live, mult_raw, 1.0), live


def _rnn_specs(seq, rows, time_of):
    per = rows // SUBLANES
    vec = pl.BlockSpec((None, 1, 128), lambda hb, n: (hb, 0, 0))
    mat = pl.BlockSpec((None, 128, 128), lambda hb, n: (hb, 0, 0))
    return [pl.BlockSpec((rows, 128), lambda hb, n: (time_of(n), hb)),
            pl.BlockSpec((SUBLANES, 128), lambda hb, n: (jnp.maximum(time_of(n) * per - 1, 0), hb)),
            pl.BlockSpec((rows, 1), lambda hb, n: (time_of(n), 0)),
            pl.BlockSpec((None, SUBLANES, 128), lambda hb, n: (hb, 0, 0)),
            vec, mat, vec, mat, vec, vec]


def _rnn_fwd(pf, keep, conv_w8, conv_b, w_a, b_a, w_x, b_x, lam):
    seq = pf.shape[0]
    rows = RNN_ROWS

    def body(x_ref, xh_ref, keep_ref, cw_ref, cb_ref, wa_ref, ba_ref, wx_ref, bx_ref, lam_ref, hr_ref,
             carry, a_scr, b_scr, spread):
        n = pl.program_id(1)

        @pl.when(n == 0)
        def _():
            carry[...] = jnp.zeros_like(carry)

        xr = x_ref[...]
        head = jnp.where(n > 0, xh_ref[...], 0.0)
        taps = _conv_taps(xr, head)
        xc = cb_ref[...] + sum(cw_ref[k:k + 1, :] * taps[k] for k in range(4))
        _, i, _, _, _, a, mult, _ = _rnn_gates(xc, wa_ref[...], ba_ref[...], wx_ref[...], bx_ref[...],
                                               lam_ref[...], keep_ref[...])
        h, last = _scan(a, mult * i * xc, carry[0:1, :], False, a_scr, b_scr, spread)
        hr_ref[...] = h
        carry[...] = jnp.broadcast_to(last, carry.shape)

    chunk_f32 = pltpu.VMEM((rows, 128), F32)
    return pl.pallas_call(
        body, name="rnn_fwd",
        out_shape=jax.ShapeDtypeStruct((seq, D_MODEL), F32),
        grid=(RNN_BLOCKS, seq // rows),
        in_specs=_rnn_specs(seq, rows, lambda n: n),
        out_specs=pl.BlockSpec((rows, 128), lambda hb, n: (n, hb)),
        scratch_shapes=[pltpu.VMEM((SUBLANES, 128), F32), chunk_f32, chunk_f32, chunk_f32],
        compiler_params=_params(("arbitrary", "arbitrary"), VMEM_LIMIT),
    )(pf, pf, keep, conv_w8, conv_b, w_a, b_a, w_x, b_x, lam)


def _rnn_bwd(pf, hr, dhr, keep, conv_w8, conv_b, w_a, b_a, w_x, b_x, lam):
    seq = pf.shape[0]
    rows = RNN_ROWS
    nchunk = seq // rows
    per = rows // SUBLANES
    time_of = lambda n: nchunk - 1 - n

    def body(x_ref, xh_ref, keep_ref, cw_ref, cb_ref, wa_ref, ba_ref, wx_ref, bx_ref, lam_ref,
             hr_ref, hrh_ref, dhr_ref,
             dx_ref, gcw_ref, gcb_ref, gwa_ref, gba_ref, gwx_ref, gbx_ref, glam_ref,
             g_carry, dxc_tail, a_scr, b_scr, spread):
        n = pl.program_id(1)
        first_in_time = n == nchunk - 1

        @pl.when(n == 0)
        def _():
            g_carry[...] = jnp.zeros_like(g_carry)
            dxc_tail[...] = jnp.zeros_like(dxc_tail)
            for ref in (gcw_ref, gcb_ref, gwa_ref, gba_ref, gwx_ref, gbx_ref, glam_ref):
                ref[...] = jnp.zeros_like(ref)

        xr = x_ref[...]
        head = jnp.where(first_in_time, 0.0, xh_ref[...])
        taps = _conv_taps(xr, head)
        cw = cw_ref[...]
        xc = cb_ref[...] + sum(cw[k:k + 1, :] * taps[k] for k in range(4))
        wa, wx, lam = wa_ref[...], wx_ref[...], lam_ref[...]
        r, i, cl, a_raw, mult_raw, a, mult, live = _rnn_gates(xc, wa, ba_ref[...], wx, bx_ref[...], lam,
                                                               keep_ref[...])
        h_prev = _shift_down(hr_ref[...], 1, jnp.where(first_in_time, 0.0, hrh_ref[...]))

        row = lax.broadcasted_iota(jnp.int32, xr.shape, 0)
        last = row == rows - 1
        a_next = jnp.where(last, 0.0, pltpu.roll(a, rows - 1, 0))
        g, g_first = _scan(a_next, dhr_ref[...] + jnp.where(last, g_carry[0:1, :], 0.0),
                           jnp.zeros((1, 128), F32), True, a_scr, b_scr, spread)
        g_carry[...] = jnp.broadcast_to(a[0:1, :] * g_first, g_carry.shape)

        da = g * h_prev
        dmult = g * i * xc
        di = g * mult * xc
        dxc = g * mult * i
        dlog_a = jnp.where(live, da * a_raw - dmult * a_raw * a_raw / mult_raw, 0.0)
        dpa = (dlog_a * cl) * r * (1.0 - r)
        dpx = di * i * (1.0 - i)
        glam_ref[...] += jnp.sum(dlog_a * r, axis=0, keepdims=True) * (LRU_C * _sigmoid(-lam))
        xcb, dpab, dpxb = xc.astype(BF16), dpa.astype(BF16), dpx.astype(BF16)
        gwa_ref[...] += _dot_tn(xcb, dpab)
        gwx_ref[...] += _dot_tn(xcb, dpxb)
        gba_ref[...] += jnp.sum(dpa, axis=0, keepdims=True)
        gbx_ref[...] += jnp.sum(dpx, axis=0, keepdims=True)
        dxc = dxc + _dot_nt(dpab, wa.astype(BF16)) + _dot_nt(dpxb, wx.astype(BF16))

        gcb_ref[...] += jnp.sum(dxc, axis=0, keepdims=True)
        for k in range(4):
            gcw_ref[k:k + 1, :] += jnp.sum(dxc * taps[k], axis=0, keepdims=True)
        tail = dxc_tail[...]
        dx = cw[3:4, :] * dxc
        for k in range(3):
            dx = dx + cw[k:k + 1, :] * _shift_up(dxc, 3 - k, tail)
        dx_ref[...] = dx.astype(BF16)
        dxc_tail[...] = dxc[0:SUBLANES, :]

    blk = lambda hb, n: (hb, 0, 0)
    chunk = pl.BlockSpec((rows, 128), lambda hb, n: (time_of(n), hb))
    vec_out = pl.BlockSpec((None, 1, 128), blk)
    mat_out = pl.BlockSpec((None, 128, 128), blk)
    vec_shape = jax.ShapeDtypeStruct((RNN_BLOCKS, 1, 128), F32)
    mat_shape = jax.ShapeDtypeStruct((RNN_BLOCKS, 128, 128), F32)
    return pl.pallas_call(
        body, name="rnn_bwd",
        out_shape=[jax.ShapeDtypeStruct((seq, D_MODEL), BF16),
                   jax.ShapeDtypeStruct((RNN_BLOCKS, SUBLANES, 128), F32), vec_shape,
                   mat_shape, vec_shape, mat_shape, vec_shape, vec_shape],
        grid=(RNN_BLOCKS, nchunk),
        in_specs=_rnn_specs(seq, rows, time_of) + [
            chunk, pl.BlockSpec((SUBLANES, 128), lambda hb, n: (jnp.maximum(time_of(n) * per - 1, 0), hb)), chunk],
        out_specs=[chunk, pl.BlockSpec((None, SUBLANES, 128), blk), vec_out,
                   mat_out, vec_out, mat_out, vec_out, vec_out],
        scratch_shapes=[pltpu.VMEM((SUBLANES, 128), F32), pltpu.VMEM((SUBLANES, 128), F32)]
                       + [pltpu.VMEM((rows, 128), F32)] * 3,
        compiler_params=_params(("arbitrary", "arbitrary"), VMEM_LIMIT),
    )(pf, pf, keep, conv_w8, conv_b, w_a, b_a, w_x, b_x, lam, hr, hr, dhr)


def _unit_rows(dil, r, j):
    start = j * KEY_BLOCK * dil + r
    return pl.ds(start, KEY_BLOCK) if dil == 1 else pl.ds(start, KEY_BLOCK, stride=dil)


def _attn_fwd(q, k, v):
    nh, seq, _ = q.shape
    nchunk = seq // SPAN
    nblk = SPAN // KEY_BLOCK
    wide = DILATIONS[-1]

    def body(q_ref, k_ref, v_ref, kp_ref, vp_ref, o_ref, l1_ref, l4_ref, l16_ref,
             acc, m_s, l_s, q16, k16, v16, k16p, v16p, acc16, m16, l16, tmp):
        n = pl.program_id(1)
        qi = lax.broadcasted_iota(jnp.int32, (KEY_BLOCK, KEY_BLOCK), 0)
        ki = lax.broadcasted_iota(jnp.int32, (KEY_BLOCK, KEY_BLOCK), 1)
        bias_own = jnp.where(ki <= qi, 0.0, NEG_INF)
        bias_before = jnp.where(ki >= qi, 0.0, NEG_INF)
        bias_mid = jnp.concatenate([bias_before, bias_own], axis=1)
        bias_first = jnp.concatenate([jnp.where(n > 0, bias_before, NEG_INF), bias_own], axis=1)
        ones = jnp.ones((2 * KEY_BLOCK, HEAD_DIM), BF16)
        diag = qi == ki

        @pl.when(n == 0)
        def _():
            k16p[...] = jnp.zeros_like(k16p)
            v16p[...] = jnp.zeros_like(v16p)

        def unit(qf, kpb, kb, vpb, vb, bias, state, rows, first):
            acc_r, m_r, l_r = state
            kcat = jnp.concatenate([kpb, kb], axis=0)
            vaug = jnp.concatenate([jnp.concatenate([vpb, vb], axis=0), ones], axis=1)
            s = _dot_nt(qf.astype(BF16), kcat) + bias
            mx = jnp.max(s, axis=-1, keepdims=True)
            if first:
                m_new = jnp.broadcast_to(mx, (KEY_BLOCK, HEAD_DIM))
            else:
                m_old = m_r[rows, :]
                m_new = jnp.maximum(m_old, mx)
            pv = _dot(jnp.exp(s - jnp.concatenate([m_new, m_new], axis=1)).astype(BF16), vaug)
            if first:
                acc_r[rows, :] = pv[:, :HEAD_DIM]
                l_r[rows, :] = pv[:, HEAD_DIM:]
            else:
                alpha = jnp.exp(m_old - m_new)
                acc_r[rows, :] = alpha * acc_r[rows, :] + pv[:, :HEAD_DIM]
                l_r[rows, :] = alpha * l_r[rows, :] + pv[:, HEAD_DIM:]
            m_r[rows, :] = m_new

        for gi, dil in enumerate(DILATIONS[:-1]):
            nb = nblk // dil
            for r in range(dil):
                prow = _unit_rows(dil, r, nb - 1)
                kpb, vpb = kp_ref[prow, :].astype(BF16), vp_ref[prow, :].astype(BF16)
                for j in range(nb):
                    rows = _unit_rows(dil, r, j)
                    kb, vb = k_ref[rows, :].astype(BF16), v_ref[rows, :].astype(BF16)
                    unit(q_ref[rows, :], kpb, kb, vpb, vb, bias_first if j == 0 else bias_mid,
                         (acc, m_s, l_s), rows, gi == 0)
                    kpb, vpb = kb, vb

        for src, dst in ((q_ref, q16), (k_ref, k16), (v_ref, v16), (acc, acc16), (m_s, m16), (l_s, l16)):
            _to_residue_major(src, tmp, dst)
        for r in range(wide):
            rows = pl.ds(r * KEY_BLOCK, KEY_BLOCK)
            unit(q16[rows, :], k16p[rows, :].astype(BF16), k16[rows, :].astype(BF16), v16p[rows, :].astype(BF16),
                 v16[rows, :].astype(BF16), bias_first, (acc16, m16, l16), rows, False)
        k16p[...] = k16[...]
        v16p[...] = v16[...]

        den = l16[...]
        acc16[...] = acc16[...] * (1.0 / den)
        m16[...] = m16[...] + jnp.log(den)
        _from_residue_major(acc16, tmp, o_ref, False)
        _from_residue_major(m16, tmp, m_s, False)

        def lse_row(ref, rows):
            return jnp.sum(jnp.where(diag, ref[rows, :], 0.0), axis=0, keepdims=True)

        for dil, out in zip(DILATIONS[:-1], (l1_ref, l4_ref)):
            nb = nblk // dil
            for r in range(dil):
                for j in range(nb):
                    out[r * nb + j:r * nb + j + 1, :] = lse_row(m_s, _unit_rows(dil, r, j))
        for r in range(wide):
            l16_ref[r:r + 1, :] = lse_row(m16, pl.ds(r * KEY_BLOCK, KEY_BLOCK))

    blk = pl.BlockSpec((None, SPAN, HEAD_DIM), lambda h, n: (h, n, 0))
    pblk = pl.BlockSpec((None, SPAN, HEAD_DIM), lambda h, n: (h, jnp.maximum(n - 1, 0), 0))
    lblk = pl.BlockSpec((None, nblk, KEY_BLOCK), lambda h, n: (h, n, 0))
    lshape = jax.ShapeDtypeStruct((nh, seq // KEY_BLOCK, KEY_BLOCK), F32)
    o, l1, l4, l16 = pl.pallas_call(
        body, name="attn_fwd",
        out_shape=[jax.ShapeDtypeStruct((nh, seq, HEAD_DIM), F32), lshape, lshape, lshape],
        grid=(nh, nchunk), in_specs=[blk, blk, blk, pblk, pblk], out_specs=[blk, lblk, lblk, lblk],
        scratch_shapes=[pltpu.VMEM((SPAN, HEAD_DIM), F32)] * 12,
        compiler_params=_params(("arbitrary", "arbitrary"), VMEM_LIMIT),
    )(q, k, v, k, v)
    return o, (l1, l4, l16)


def _to_residue_major(src, tmp, dst):
    quarter = SPAN // 4
    for r4 in range(4):
        tmp[r4 * quarter:(r4 + 1) * quarter, :] = src[pl.ds(r4, quarter, stride=4), :]
    for r4 in range(4):
        for rp in range(4):
            r = r4 + 4 * rp
            dst[r * KEY_BLOCK:(r + 1) * KEY_BLOCK, :] = tmp[pl.ds(r4 * quarter + rp, KEY_BLOCK, stride=4), :]


def _from_residue_major(src, tmp, dst, add):
    quarter = SPAN // 4
    for r4 in range(4):
        for rp in range(4):
            r = r4 + 4 * rp
            tmp[pl.ds(r4 * quarter + rp, KEY_BLOCK, stride=4), :] = src[r * KEY_BLOCK:(r + 1) * KEY_BLOCK, :]
    for r4 in range(4):
        rows = pl.ds(r4, quarter, stride=4)
        part = tmp[r4 * quarter:(r4 + 1) * quarter, :]
        dst[rows, :] = dst[rows, :] + part if add else part


def _attn_bwd(q, k, v, do, o, lses, cosf, sinf):
    nh, seq, _ = q.shape
    nchunk = seq // SPAN
    nblk = SPAN // KEY_BLOCK
    wide = DILATIONS[-1]
    assert SPAN == wide * KEY_BLOCK

    def body(q_ref, k_ref, v_ref, do_ref, o_ref, kp_ref, vp_ref, l1_ref, l4_ref, l16_ref,
             cos_ref, sin_ref, cosp_ref, sinp_ref, dq_ref, dk_ref, dv_ref,
             dq_acc, dkc_acc, dvc_acc, dkp_acc, dvp_acc, q16, k16, v16, do16, o16, k16p, v16p,
             dq16, dkc16, dvc16, dkp16, dvp16, tmp, pt_s, ds_s, kcat_s, qb_s, dob_s):
        n = pl.program_id(1)
        ki = lax.broadcasted_iota(jnp.int32, (KEY_BLOCK, KEY_BLOCK), 0)
        qi = lax.broadcasted_iota(jnp.int32, (KEY_BLOCK, KEY_BLOCK), 1)
        bias_own = jnp.where(ki <= qi, 0.0, NEG_INF)
        bias_before = jnp.where(ki >= qi, 0.0, NEG_INF)
        bias_mid = jnp.concatenate([bias_before, bias_own], axis=0)
        bias_first = jnp.concatenate([jnp.where(n > 0, bias_before, NEG_INF), bias_own], axis=0)
        ones8 = jnp.ones((SUBLANES, HEAD_DIM), BF16)

        def row_dot(a, b):
            prod = a * b
            hi = prod.astype(BF16)
            lo = (prod - hi.astype(F32)).astype(BF16)
            return (_dot_nt(ones8, hi) + _dot_nt(ones8, lo))[0:1, :]

        def group(units, srcs, before, l_ref, accs):
            src_q, src_do, src_o, src_k, src_v = srcs
            before_k, before_v = before
            acc_q, acc_kc, acc_vc, acc_kp, acc_vp = accs
            kb = vb = None
            for u, (rows, prow, outside, lrow, _) in enumerate(units):
                dof = src_do[rows, :]
                qb, dob = src_q[rows, :].astype(BF16), dof.astype(BF16)
                kpb, vpb = (before_k[prow, :].astype(BF16), before_v[prow, :].astype(BF16)) if outside else (kb, vb)
                kb, vb = src_k[rows, :].astype(BF16), src_v[rows, :].astype(BF16)
                kcat = jnp.concatenate([kpb, kb], axis=0)
                vcat = jnp.concatenate([vpb, vb], axis=0)
                bias = bias_first if outside else bias_mid
                pt = jnp.exp(_dot_nt(kcat, qb) + bias - l_ref[lrow:lrow + 1, :])
                dst = pt * (_dot_nt(vcat, dob) - row_dot(dof, src_o[rows, :]))
                pt_s[u], ds_s[u], kcat_s[u], qb_s[u], dob_s[u] = pt.astype(BF16), dst.astype(BF16), kcat, qb, dob
            for u, (rows, _, _, _, _) in enumerate(units):
                acc_q[rows, :] += _dot_tn(ds_s[u], kcat_s[u])
            for u, (rows, prow, outside, _, nxt) in enumerate(units):
                dk = _dot(ds_s[u, KEY_BLOCK:, :], qb_s[u])
                dv = _dot(pt_s[u, KEY_BLOCK:, :], dob_s[u])
                if nxt is not None:
                    dk = dk + _dot(ds_s[nxt, :KEY_BLOCK, :], qb_s[nxt])
                    dv = dv + _dot(pt_s[nxt, :KEY_BLOCK, :], dob_s[nxt])
                acc_kc[rows, :] += dk
                acc_vc[rows, :] += dv
                if outside:
                    acc_kp[prow, :] += _dot(ds_s[u, :KEY_BLOCK, :], qb_s[u])
                    acc_vp[prow, :] += _dot(pt_s[u, :KEY_BLOCK, :], dob_s[u])

        @pl.when(n == 0)
        def _():
            for ref in (dkp_acc, dvp_acc, dkp16, dvp16, k16p, v16p):
                ref[...] = jnp.zeros_like(ref)

        @pl.when(n < nchunk)
        def _():
            for ref in (dq_acc, dkc_acc, dvc_acc, dq16, dkc16, dvc16):
                ref[...] = jnp.zeros_like(ref)
            for src, dst in ((q_ref, q16), (k_ref, k16), (v_ref, v16), (do_ref, do16), (o_ref, o16)):
                _to_residue_major(src, tmp, dst)
            natural = (q_ref, do_ref, o_ref, k_ref, v_ref)
            for dil, l_ref in zip(DILATIONS[:-1], (l1_ref, l4_ref)):
                nb = nblk // dil
                units = [(_unit_rows(dil, r, j), _unit_rows(dil, r, (j - 1) % nb), j == 0, r * nb + j,
                          r * nb + j + 1 if j + 1 < nb else None) for r in range(dil) for j in range(nb)]
                group(units, natural, (kp_ref, vp_ref), l_ref, (dq_acc, dkc_acc, dvc_acc, dkp_acc, dvp_acc))
            blocks = [pl.ds(r * KEY_BLOCK, KEY_BLOCK) for r in range(wide)]
            group([(rows, rows, True, r, None) for r, rows in enumerate(blocks)], (q16, do16, o16, k16, v16),
                  (k16p, v16p), l16_ref, (dq16, dkc16, dvc16, dkp16, dvp16))
            _from_residue_major(dq16, tmp, dq_acc, True)
            dq = dq_acc[...]
            dq_ref[...] = ((dq * cos_ref[...] - _rope_partner(dq) * sin_ref[...]) * ATTN_SCALE).astype(BF16)

        @pl.when(n > 0)
        def _():
            _from_residue_major(dkp16, tmp, dkp_acc, True)
            _from_residue_major(dvp16, tmp, dvp_acc, True)
            dk = dkp_acc[...]
            dk_ref[...] = (dk * cosp_ref[...] - _rope_partner(dk) * sinp_ref[...]).astype(BF16)
            dv_ref[...] = dvp_acc[...].astype(BF16)

        @pl.when(n < nchunk)
        def _():
            for src, dst in ((dkc_acc, dkp_acc), (dvc_acc, dvp_acc), (dkc16, dkp16), (dvc16, dvp16),
                             (k16, k16p), (v16, v16p)):
                dst[...] = src[...]

    last = nchunk - 1
    cur = lambda h, n: (h, jnp.minimum(n, last), 0)
    prev = lambda h, n: (h, jnp.clip(n - 1, 0, last), 0)
    blk = lambda idx: pl.BlockSpec((None, SPAN, HEAD_DIM), idx)
    lblk = pl.BlockSpec((None, nblk, KEY_BLOCK), cur)
    tab = pl.BlockSpec((SPAN, HEAD_DIM), lambda h, n: (jnp.minimum(n, last), 0))
    tabp = pl.BlockSpec((SPAN, HEAD_DIM), lambda h, n: (jnp.clip(n - 1, 0, last), 0))
    out_q = pl.BlockSpec((SPAN, HEAD_DIM), lambda h, n: (jnp.minimum(n, last), h))
    out_kv = pl.BlockSpec((SPAN, HEAD_DIM), lambda h, n: (jnp.clip(n - 1, 0, last), h))
    shape = jax.ShapeDtypeStruct((seq, nh * HEAD_DIM), BF16)
    return pl.pallas_call(
        body, name="attn_bwd", out_shape=[shape, shape, shape], grid=(nh, nchunk + 1),
        in_specs=[blk(cur)] * 5 + [blk(prev)] * 2 + [lblk] * 3 + [tab, tab, tabp, tabp],
        out_specs=[out_q, out_kv, out_kv],
        scratch_shapes=[pltpu.VMEM((SPAN, HEAD_DIM), F32)] * 18
                       + [pltpu.VMEM((nblk, 2 * KEY_BLOCK, HEAD_DIM), BF16)] * 3
                       + [pltpu.VMEM((nblk, KEY_BLOCK, HEAD_DIM), BF16)] * 2,
        compiler_params=_params(("arbitrary", "arbitrary"), VMEM_LIMIT),
    )(q, k, v, do, o, k, v, *lses, cosf, sinf, cosf, sinf)


def _hub(x, tgt, hr, pf, o_hm, mod, b_mod, b_gate, g_final, w_out_rnn, w_out_attn, w_o):
    seq = x.shape[0]
    tm = HUB_ROWS
    nsteps = seq // tm

    def body(x_ref, t_ref, hr_ref, zr_ref, za_ref, gr_ref, ga_ref, o_ref, mod_ref, bmod_ref, bg_ref, gf_ref,
             wr_hbm, wa_hbm, wo_hbm,
             dx2_ref, dhr_ref, dzr_ref, do_ref, dza_ref, dgr_ref, dga_ref,
             ur_ref, dyr_ref, ua_ref, dya_ref, mg_ref, dmo_ref,
             ggf_ref, gbg_ref, dgate_ref, loss_ref,
             wr, wa, wo, sem):
        step = pl.program_id(0)

        @pl.when(step == 0)
        def _():
            for src, dst in ((wr_hbm, wr), (wa_hbm, wa), (wo_hbm, wo)):
                cp = pltpu.make_async_copy(src, dst, sem)
                cp.start()
                cp.wait()
            for ref in (ggf_ref, gbg_ref, dgate_ref, loss_ref):
                ref[...] = jnp.zeros_like(ref)

        gate = mod_ref[:, 2 * D_MODEL:] + bmod_ref[:, 2 * D_MODEL:]
        gfin = gf_ref[...]
        hr_t, zr, za = hr_ref[...], zr_ref[...], za_ref[...]
        o = jnp.concatenate([o_ref[hh] for hh in range(N_HEADS)], axis=1)
        sig_zr, sig_za = _sigmoid(zr), _sigmoid(za)
        silu_zr, silu_za = zr * sig_zr, za * sig_za
        u_rnn = (hr_t * silu_zr).astype(BF16)
        u_attn = (o * silu_za).astype(BF16)
        y_rnn = _dot(u_rnn, wr[...])
        y_attn = _dot(u_attn, wa[...])
        sr = _sigmoid(gr_ref[...] + bg_ref[:, :D_MODEL])
        sa = _sigmoid(ga_ref[...] + bg_ref[:, D_MODEL:])
        merged = (sr * y_rnn + sa * y_attn).astype(BF16)
        mo = _dot(merged, wo[...])
        x2 = x_ref[...] + gate * mo
        rstd = lax.rsqrt(jnp.mean(x2 * x2, axis=-1, keepdims=True) + NORM_EPS)
        xn = x2 * rstd
        err = xn * gfin - t_ref[...]
        loss_ref[...] += 0.5 * jnp.sum(jnp.sum(err * err, axis=-1, keepdims=True) * (1.0 / D_MODEL),
                                       axis=0, keepdims=True)

        dy = err * (1.0 / D_MODEL)
        ggf_ref[...] += jnp.sum(dy * xn, axis=0, keepdims=True)
        dxn = dy * gfin
        dx2 = rstd * (dxn - xn * jnp.mean(dxn * xn, axis=-1, keepdims=True))
        dx2_ref[...] = dx2
        dgate_ref[...] += jnp.sum(dx2 * mo, axis=0, keepdims=True)
        dmo = (dx2 * gate).astype(BF16)
        dmerged = _dot_nt(dmo, wo[...])
        mg_ref[...] = merged
        dmo_ref[...] = dmo
        dy_rnn = (dmerged * sr).astype(BF16)
        dy_attn = (dmerged * sa).astype(BF16)
        dg_r = dmerged * y_rnn * sr * (1.0 - sr)
        dg_a = dmerged * y_attn * sa * (1.0 - sa)
        dgr_ref[...] = dg_r.astype(BF16)
        dga_ref[...] = dg_a.astype(BF16)
        gbg_ref[:, :D_MODEL] += jnp.sum(dg_r, axis=0, keepdims=True)
        gbg_ref[:, D_MODEL:] += jnp.sum(dg_a, axis=0, keepdims=True)
        du_rnn = _dot_nt(dy_rnn, wr[...])
        du_attn = _dot_nt(dy_attn, wa[...])
        ur_ref[...] = u_rnn
        dyr_ref[...] = dy_rnn
        ua_ref[...] = u_attn
        dya_ref[...] = dy_attn
        dhr_ref[...] = du_rnn * silu_zr
        dzr_ref[...] = (du_rnn * hr_t * (sig_zr * (1.0 + zr * (1.0 - sig_zr)))).astype(BF16)
        dza_ref[...] = (du_attn * o * (sig_za * (1.0 + za * (1.0 - sig_za)))).astype(BF16)
        d_o = du_attn * silu_za
        for hh in range(N_HEADS):
            do_ref[hh] = d_o[:, hh * HEAD_DIM:(hh + 1) * HEAD_DIM]

    row = pl.BlockSpec((tm, D_MODEL), lambda i: (i, 0))
    piece = lambda slot: pl.BlockSpec((tm, D_MODEL), lambda i: (i, slot))
    hm = pl.BlockSpec((N_HEADS, tm, HEAD_DIM), lambda i: (0, i, 0))
    const = lambda cols: pl.BlockSpec((1, cols), lambda i: (0, 0))
    any_spec = pl.BlockSpec(memory_space=pl.ANY)
    act_f32 = jax.ShapeDtypeStruct((seq, D_MODEL), F32)
    act_bf16 = jax.ShapeDtypeStruct((seq, D_MODEL), BF16)
    return pl.pallas_call(
        body, name="hub",
        out_shape=[act_f32, act_f32, act_bf16, jax.ShapeDtypeStruct((N_HEADS, seq, HEAD_DIM), F32),
                   act_bf16, act_bf16, act_bf16] + [act_bf16] * 6 + [
                   jax.ShapeDtypeStruct((1, D_MODEL), F32), jax.ShapeDtypeStruct((1, 2 * D_MODEL), F32),
                   jax.ShapeDtypeStruct((1, D_MODEL), F32), jax.ShapeDtypeStruct((1, 1), F32)],
        grid=(nsteps,),
        in_specs=[row, row, row, piece(1), piece(2), piece(3), piece(4), hm,
                  const(3 * D_MODEL), const(3 * D_MODEL), const(2 * D_MODEL), const(D_MODEL),
                  any_spec, any_spec, any_spec],
        out_specs=[row, row, row, hm, row, row, row] + [row] * 6 + [
                   const(D_MODEL), const(2 * D_MODEL), const(D_MODEL), const(1)],
        scratch_shapes=[pltpu.VMEM((D_MODEL, D_MODEL), BF16)] * 3 + [pltpu.SemaphoreType.DMA],
        compiler_params=_params(("arbitrary",), VMEM_LIMIT),
    )(x, tgt, hr, pf, pf, pf, pf, o_hm, mod, b_mod, b_gate, g_final, w_out_rnn, w_out_attn, w_o)


def _pair_grads(name, lefts, rights):
    n = len(rights)
    shared = len(lefts) == 1
    seq = rights[0].shape[0]
    tk = WGRAD_ROWS
    nk = seq // tk

    def body(*refs):
        l_refs, r_refs, out_ref = refs[:len(lefts)], refs[len(lefts):len(lefts) + n], refs[len(lefts) + n]
        j, kk = pl.program_id(0), pl.program_id(1)

        @pl.when(kk == 0)
        def _():
            out_ref[...] = jnp.zeros_like(out_ref)

        for m in range(n):
            @pl.when(j == m)
            def _(m=m):
                out_ref[...] += _dot_tn(l_refs[0 if shared else m][...], r_refs[m][...])

    def spec(m):
        return pl.BlockSpec((tk, D_MODEL), lambda j, kk: (jnp.where(j == m, kk, jnp.where(j < m, 0, nk - 1)), 0))

    left_specs = [pl.BlockSpec((tk, D_MODEL), lambda j, kk: (kk, 0))] if shared else [spec(m) for m in range(n)]
    return pl.pallas_call(
        body, name=name,
        out_shape=jax.ShapeDtypeStruct((n, D_MODEL, D_MODEL), F32),
        grid=(n, nk),
        in_specs=left_specs + [spec(m) for m in range(n)],
        out_specs=pl.BlockSpec((None, D_MODEL, D_MODEL), lambda j, kk: (j, 0, 0)),
        compiler_params=_params(("arbitrary", "arbitrary"), VMEM_LIMIT),
    )(*lefts, *rights)


def _dh_dx(pieces, w_in_all, x, dx2, mod, b_mod, g_norm):
    seq = x.shape[0]
    tm = DX_ROWS

    def body(*refs):
        p_refs = refs[:8]
        w_hbm, x_ref, dx2_ref, mod_ref, bmod_ref, g_ref = refs[8:14]
        gx_ref, dshift_ref, dscale_ref, ggn_ref, w_scr, sem = refs[14:]
        step = pl.program_id(0)

        @pl.when(step == 0)
        def _():
            cp = pltpu.make_async_copy(w_hbm, w_scr, sem)
            cp.start()
            cp.wait()
            for ref in (dshift_ref, dscale_ref, ggn_ref):
                ref[...] = jnp.zeros_like(ref)

        dh = _dot_nt(p_refs[0][...], w_scr[0])
        for j in range(1, 8):
            dh = dh + _dot_nt(p_refs[j][...], w_scr[j])
        scale1 = 1.0 + mod_ref[:, D_MODEL:2 * D_MODEL] + bmod_ref[:, D_MODEL:2 * D_MODEL]
        g = g_ref[...]
        xf = x_ref[...]
        rstd_t = lax.rsqrt(jnp.mean(xf * xf, axis=-1, keepdims=True) + NORM_EPS)
        xn = xf * rstd_t
        dshift_ref[...] += jnp.sum(dh, axis=0, keepdims=True)
        dscale_ref[...] += jnp.sum(dh * (xn * g), axis=0, keepdims=True)
        ggn_ref[...] += jnp.sum(dh * scale1 * xn, axis=0, keepdims=True)
        dxn = dh * (g * scale1)
        gx_ref[...] = rstd_t * (dxn - xn * jnp.mean(dxn * xn, axis=-1, keepdims=True)) + dx2_ref[...]

    row = pl.BlockSpec((tm, D_MODEL), lambda i: (i, 0))
    const = lambda cols: pl.BlockSpec((1, cols), lambda i: (0, 0))
    vec = jax.ShapeDtypeStruct((1, D_MODEL), F32)
    return pl.pallas_call(
        body, name="dh_dx",
        out_shape=[jax.ShapeDtypeStruct((seq, D_MODEL), F32), vec, vec, vec],
        grid=(seq // tm,),
        in_specs=[row] * 8 + [pl.BlockSpec(memory_space=pl.ANY), row, row,
                              const(3 * D_MODEL), const(3 * D_MODEL), const(D_MODEL)],
        out_specs=[row, const(D_MODEL), const(D_MODEL), const(D_MODEL)],
        scratch_shapes=[pltpu.VMEM((8, D_MODEL, D_MODEL), BF16), pltpu.SemaphoreType.DMA],
        compiler_params=_params(("arbitrary",), VMEM_LIMIT),
    )(*pieces, w_in_all, x, dx2, mod, b_mod, g_norm)


def _adamw(name, w, g, m, v):
    rows, cols = w.shape
    tr = rows if rows <= 256 else 256

    def body(w_ref, g_ref, m_ref, v_ref, d_ref, nm_ref, nv_ref):
        gv = g_ref[...]
        nm = ADAM_B1 * m_ref[...] + (1.0 - ADAM_B1) * gv
        nv = ADAM_B2 * v_ref[...] + (1.0 - ADAM_B2) * (gv * gv)
        m_hat = nm / (1.0 - ADAM_B1 ** ADAM_STEP)
        v_hat = nv / (1.0 - ADAM_B2 ** ADAM_STEP)
        d_ref[...] = -ADAM_LR * (m_hat / (jnp.sqrt(v_hat) + ADAM_EPS) + ADAM_WD * w_ref[...])
        nm_ref[...] = nm
        nv_ref[...] = nv

    spec = pl.BlockSpec((tr, cols), lambda i: (i, 0))
    shape = jax.ShapeDtypeStruct((rows, cols), F32)
    return pl.pallas_call(
        body, name=name, out_shape=[shape, shape, shape], grid=(rows // tr,),
        in_specs=[spec] * 4, out_specs=[spec] * 3,
        compiler_params=_params(("arbitrary",)),
    )(w, g, m, v)


def kernel(x, c, positions, g_norm, w_mod, b_mod, w_in, b_gate, conv_w, conv_b, w_a, b_a, w_x, b_x, lam, w_out_rnn, w_out_attn, w_o, g_final, loss_target, m_g_norm, m_w_mod, m_b_mod, m_w_in, m_b_gate, m_conv_w, m_conv_b, m_w_a, m_b_a, m_w_x, m_b_x, m_lam, m_w_out_rnn, m_w_out_attn, m_w_o, m_g_final, v_g_norm, v_w_mod, v_b_mod, v_w_in, v_b_gate, v_conv_w, v_conv_b, v_w_a, v_b_a, v_w_x, v_b_x, v_lam, v_w_out_rnn, v_w_out_attn, v_w_o, v_g_final):
    seq = x.shape[1]
    me = _index(_my_pos())
    xs, tgt = x[0], loss_target[0]

    pos = positions[0].astype(F32)[:, None]
    inv_freq = ROPE_THETA ** (-jnp.arange(0, 2 * ROT_HALF, 2, dtype=F32) / (2 * ROT_HALF))
    ang = pos * inv_freq
    rest = HEAD_DIM - 2 * ROT_HALF
    cosf = jnp.concatenate([jnp.cos(ang), jnp.cos(ang), jnp.ones((seq, rest), F32)], axis=1)
    sinf = jnp.concatenate([-jnp.sin(ang), jnp.sin(ang), jnp.zeros((seq, rest), F32)], axis=1)
    keep = (positions[0] != 0).astype(F32)[:, None]

    (w_in_all,) = _ag_big("gather_weights", [w_in[0].astype(BF16)])
    both = _ag_small("gather_c_conv_w", jnp.concatenate(
        [jnp.broadcast_to(c, (SUBLANES, D_MODEL)), jnp.pad(conv_w[0], ((0, SUBLANES - 4), (0, 0)))], axis=1))
    c_all, conv_w8 = both[:, 0, :D_MODEL], both[:, :, D_MODEL:]
    mod_cols = w_mod.shape[2]
    mod_part = _ag_small("gather_mod", _mod_fwd(c_all, w_mod[0]))
    mod = lax.dynamic_index_in_dim(mod_part, me, axis=1, keepdims=False).reshape(1, N_DEV * mod_cols)
    mod, late = lax.optimization_barrier(
        (mod, [w_out_rnn[0].astype(BF16), w_out_attn[0].astype(BF16), w_o[0].astype(BF16)]))
    late_sends, late_recvs, late_shards, late_lands, late_token = _gather_start(late, me)
    mod = mod + late_token[0:1, 0:1]

    blocks = lambda t: t.reshape(RNN_BLOCKS, 1, 128)
    rnn_params = (conv_w8, blocks(conv_b), w_a[0], blocks(b_a), w_x[0], blocks(b_x), blocks(lam))

    h = _norm(xs, mod, b_mod, g_norm)
    pf, q, k, v = _proj(h, w_in_all, cosf, sinf)
    hr = _rnn_fwd(pf, keep, *rnn_params)
    o, lses = _attn_fwd(q, k, v)

    w_or_all, w_oa_all, w_o_all = (t.reshape(D_MODEL, D_MODEL) for t in _gather_wait(
        late_sends, late_recvs, late_shards, late_lands, o))
    (dx2, dhr, dz_rnn, d_o, dz_attn, dg_r, dg_a, u_rnn, dy_rnn, u_attn, dy_attn, merged, dmo,
     gp_g_final, gp_b_gate, dgate, loss_part) = _hub(
        xs, tgt, hr, pf, o, mod, b_mod, b_gate, g_final.reshape(1, D_MODEL), w_or_all, w_oa_all, w_o_all)
    gp_w_or, gp_w_oa, gp_w_o = _pair_grads("out_grads", [u_rnn, u_attn, merged], [dy_rnn, dy_attn, dmo])
    dq, dk, dv = _attn_bwd(q, k, v, d_o, o, lses, cosf, sinf)
    dx_rnn, gp_conv_w, gp_conv_b, gp_w_a, gp_b_a, gp_w_x, gp_b_x, gp_lam = _rnn_bwd(pf, hr, dhr, keep, *rnn_params)
    pieces = [dx_rnn, dz_rnn, dq, dk, dv, dz_attn, dg_r, dg_a]
    gp_w_in = _pair_grads("w_in_grad", [h], pieces)

    stacks = [gp_w_in, gp_w_or.reshape(N_DEV, 128, D_MODEL), gp_w_oa.reshape(N_DEV, 128, D_MODEL),
              gp_w_o.reshape(N_DEV, 128, D_MODEL)]
    from_sib = _rs_to_sibling("rs_sibling", stacks)
    targets = jnp.bitwise_xor(me, 2 * jnp.arange(4, dtype=jnp.int32)).astype(jnp.int32)
    sums = [_add_sibling("rs_add_sibling_%d" % a, s_, r_, targets) for a, (s_, r_) in enumerate(zip(stacks, from_sib))]
    send_sems, recv_sems, sent, landing, token = _rs_chips_start([send for _, send in sums])

    mod_after = mod + token[0:1, 0:1]
    grad_x, dshift, dscale, gp_g_norm = _dh_dx(pieces, w_in_all, xs, dx2, mod_after, b_mod, g_norm)

    dmod = jnp.concatenate([dshift, dscale, dgate], axis=1)
    dmod_all = _ag_small("gather_dmod", jnp.broadcast_to(dmod, (SUBLANES, 3 * D_MODEL)))[:, 0, :]
    dmod_cols = lax.dynamic_slice_in_dim(dmod_all, me * mod_cols, mod_cols, axis=1)
    g_b_mod, g_w_mod = _mod_bwd(c_all, dmod_all, dmod_cols)

    flat = lambda t: t.reshape(-1, 128)
    small = [flat(gp_g_norm), flat(gp_b_gate), flat(gp_conv_b), flat(gp_b_a), flat(gp_b_x), flat(gp_lam),
             flat(gp_g_final), flat(gp_conv_w), jnp.broadcast_to(loss_part, (SUBLANES, 128)),
             flat(gp_w_a), flat(gp_w_x)]
    sizes = [t.shape[0] for t in small]
    small.append(jnp.zeros((-sum(sizes) % (2 * SUBLANES), 128), F32))
    total = _allreduce_small("allreduce_small_grads", jnp.concatenate(small, axis=0))
    offs = [sum(sizes[:i]) for i in range(len(sizes))]
    (g_g_norm, g_b_gate, g_conv_b, g_b_a, g_b_x, g_lam, g_g_final, g_conv_w_all, loss_rows, g_w_a, g_w_x) = (
        total[o_:o_ + s_] for o_, s_ in zip(offs, sizes))
    loss = loss_rows[0, 0]
    g_conv_w = lax.dynamic_index_in_dim(g_conv_w_all.reshape(RNN_BLOCKS, SUBLANES, 128), me, axis=0,
                                        keepdims=False)[:4]

    from_chips = _rs_chips_wait(send_sems, recv_sems, sent, landing, total)
    g_w_in, g_w_or, g_w_oa, g_w_o = (
        _add_chips("rs_add_chips_%d" % a, own, r_) for a, ((own, _), r_) in enumerate(zip(sums, from_chips)))

    weights = [
        ("g_norm", g_norm, g_g_norm, m_g_norm, v_g_norm, (SUBLANES, 128)),
        ("w_mod", w_mod, g_w_mod, m_w_mod, v_w_mod, (D_MODEL, mod_cols)),
        ("b_mod", b_mod, g_b_mod, m_b_mod, v_b_mod, (3 * SUBLANES, 128)),
        ("w_in", w_in, g_w_in, m_w_in, v_w_in, (D_MODEL, D_MODEL)),
        ("b_gate", b_gate, g_b_gate, m_b_gate, v_b_gate, (2 * SUBLANES, 128)),
        ("conv_w", conv_w, g_conv_w, m_conv_w, v_conv_w, (4, 128)),
        ("conv_b", conv_b, g_conv_b, m_conv_b, v_conv_b, (SUBLANES, 128)),
        ("w_a", w_a, g_w_a, m_w_a, v_w_a, (RNN_BLOCKS * 128, 128)),
        ("b_a", b_a, g_b_a, m_b_a, v_b_a, (SUBLANES, 128)),
        ("w_x", w_x, g_w_x, m_w_x, v_w_x, (RNN_BLOCKS * 128, 128)),
        ("b_x", b_x, g_b_x, m_b_x, v_b_x, (SUBLANES, 128)),
        ("lam", lam, g_lam, m_lam, v_lam, (SUBLANES, 128)),
        ("w_out_rnn", w_out_rnn, g_w_or, m_w_out_rnn, v_w_out_rnn, (128, D_MODEL)),
        ("w_out_attn", w_out_attn, g_w_oa, m_w_out_attn, v_w_out_attn, (128, D_MODEL)),
        ("w_o", w_o, g_w_o, m_w_o, v_w_o, (128, D_MODEL)),
        ("g_final", g_final, g_g_final, m_g_final, v_g_final, (SUBLANES, 128)),
    ]
    out_g, out_d, out_m, out_v = [], [], [], []
    for name, w_, g_, m_, v_, shape2 in weights:
        d_, nm_, nv_ = _adamw("adamw_" + name, w_.reshape(shape2), g_.reshape(shape2), m_.reshape(shape2),
                              v_.reshape(shape2))
        out_g.append(g_.reshape(w_.shape))
        out_d.append(d_.reshape(w_.shape))
        out_m.append(nm_.reshape(w_.shape))
        out_v.append(nv_.reshape(w_.shape))
    return (loss, grad_x[None], *out_g, *out_d, *out_m, *out_v)
```

```python
import jax
import jax.numpy as jnp
from jax import lax
from jax.experimental import pallas as pl
from jax.experimental.pallas import tpu as pltpu

F32 = jnp.float32
BF16 = jnp.bfloat16
MESH = pl.DeviceIdType.MESH

D_MODEL = 1024
N_HEADS = 8
HEAD_DIM = 128
RNN_BLOCKS = 8
N_DEV = 8
ROT_HALF = 16
ROPE_THETA = 500000.0
DILATIONS = (1, 4, 16)
KEY_BLOCK = 128
SPAN = KEY_BLOCK * DILATIONS[-1]
ATTN_SCALE = HEAD_DIM ** -0.5
NORM_EPS = 1e-6
LRU_C = 8.0
NEG_INF = -1e30
ADAM_LR, ADAM_B1, ADAM_B2, ADAM_EPS, ADAM_WD, ADAM_STEP = 0.001, 0.9, 0.999, 1e-08, 0.01, 10

SUBLANES = 8
VMEM_LIMIT = 56 * 1024 * 1024
PROJ_ROWS = 1024
RNN_ROWS = 2048
HUB_ROWS = 256
DX_ROWS = 512
WGRAD_ROWS = 1024
ADD_ROWS = 256


def _params(sem=None, vmem=None):
    return pltpu.CompilerParams(dimension_semantics=sem, vmem_limit_bytes=vmem)


def _dot(a, b):
    return jnp.dot(a, b, preferred_element_type=F32)


def _dot_nt(a, b):
    return lax.dot_general(a, b, (((1,), (1,)), ((), ())), preferred_element_type=F32)


def _dot_tn(a, b):
    return lax.dot_general(a, b, (((0,), (0,)), ((), ())), preferred_element_type=F32)


def _sigmoid(z):
    return 1.0 / (1.0 + jnp.exp(-z))


def _expm1_nonpos(z, exp_z):
    return jnp.where(z > -0.01, z * (1.0 + 0.5 * z), exp_z - 1.0)


def _my_pos():
    return lax.axis_index("x"), lax.axis_index("y"), lax.axis_index("c")


def _flip(pos, k):
    x, y, c = pos
    return ((1 - x) if k & 4 else x, (1 - y) if k & 2 else y, (1 - c) if k & 1 else c)


def _index(pos):
    return 4 * pos[0] + 2 * pos[1] + pos[2]


def _ag_small(name, v):
    rows, cols = v.shape

    def body(v_ref, out_ref, send_sems, recv_sems):
        me = _my_pos()
        out_ref[_index(me)] = v_ref[...]
        sends = []
        for k in range(1, N_DEV):
            cp = pltpu.make_async_remote_copy(
                src_ref=v_ref, dst_ref=out_ref.at[_index(me)], send_sem=send_sems.at[k - 1],
                recv_sem=recv_sems.at[k - 1], device_id=_flip(me, k), device_id_type=MESH)
            cp.start()
            sends.append(cp)
        for k in range(1, N_DEV):
            peer = _flip(me, k)
            pltpu.make_async_remote_copy(
                src_ref=v_ref, dst_ref=out_ref.at[_index(peer)], send_sem=send_sems.at[k - 1],
                recv_sem=recv_sems.at[k - 1], device_id=peer, device_id_type=MESH).wait_recv()
        for cp in sends:
            cp.wait_send()

    return pl.pallas_call(
        body, name=name,
        out_shape=jax.ShapeDtypeStruct((N_DEV, rows, cols), v.dtype),
        in_specs=[pl.BlockSpec(memory_space=pltpu.VMEM)],
        out_specs=pl.BlockSpec(memory_space=pltpu.VMEM),
        scratch_shapes=[pltpu.SemaphoreType.DMA((N_DEV - 1,)), pltpu.SemaphoreType.DMA((N_DEV - 1,))],
        compiler_params=_params(None, VMEM_LIMIT),
    )(v)


def _ag_big(name, shards):
    n = len(shards)

    def body(*refs):
        ins, outs = refs[:n], refs[n:2 * n]
        send_sems, recv_sems, local_sems = refs[2 * n:]
        me = _my_pos()
        sib = _flip(me, 1)
        chips = [2, 4, 6]

        def copy(a, k, block, to, src=None):
            rows = outs[a].at[_index(block)]
            return pltpu.make_async_remote_copy(
                src_ref=rows if src is None else src, dst_ref=rows,
                send_sem=send_sems.at[a * 7 + k], recv_sem=recv_sems.at[a * 7 + k],
                device_id=to, device_id_type=MESH)

        started = []
        for a in range(n):
            mine = pltpu.make_async_copy(ins[a], outs[a].at[_index(me)], local_sems.at[a])
            mine.start()
            started.append(mine)
        sends = []
        for a in range(n):
            first = [copy(a, 0, me, sib, src=ins[a])]
            first += [copy(a, 1 + j, me, _flip(me, ch), src=ins[a]) for j, ch in enumerate(chips)]
            for cp in first:
                cp.start()
            sends += first
        for j, ch in enumerate(chips):
            for a in range(n):
                copy(a, 1 + j, _flip(me, ch), me).wait_recv()
                fwd = copy(a, 4 + j, _flip(me, ch), sib)
                fwd.start()
                sends.append(fwd)
        for a in range(n):
            copy(a, 0, sib, me).wait_recv()
            for j, ch in enumerate(chips):
                copy(a, 4 + j, _flip(sib, ch), me).wait_recv()
        for cp in sends:
            cp.wait_send()
        for mine in started:
            mine.wait()

    any_spec = pl.BlockSpec(memory_space=pl.ANY)
    return pl.pallas_call(
        body, name=name,
        out_shape=[jax.ShapeDtypeStruct((N_DEV,) + s.shape, s.dtype) for s in shards],
        in_specs=[any_spec] * n, out_specs=[any_spec] * n,
        scratch_shapes=[pltpu.SemaphoreType.DMA((7 * n,)), pltpu.SemaphoreType.DMA((7 * n,)),
                        pltpu.SemaphoreType.DMA((n,))],
    )(*shards)


def _peer_copies(shards, lands, send_sems, recv_sems):
    me = _my_pos()
    return [pltpu.make_async_remote_copy(
        src_ref=shards[a], dst_ref=lands[a].at[_index(me)],
        send_sem=send_sems.at[a * 7 + k - 1], recv_sem=recv_sems.at[a * 7 + k - 1],
        device_id=_flip(me, k), device_id_type=MESH) for a in range(len(shards)) for k in range(1, N_DEV)]


def _gather_start(shards, me):
    n = len(shards)

    def body(*refs):
        srcs, lands = refs[:n], refs[n:2 * n]
        send_sems, recv_sems = refs[2 * n:2 * n + 2]
        for cp in _peer_copies(srcs, lands, send_sems, recv_sems):
            cp.start()
        refs[-1][...] = jnp.zeros_like(refs[-1])

    hbm = pl.BlockSpec(memory_space=pltpu.HBM)
    sem = pl.BlockSpec(memory_space=pltpu.SEMAPHORE)
    held = [pltpu.HBM(s.shape, s.dtype) for s in shards]
    landing = [lax.dynamic_update_slice(jnp.zeros((N_DEV,) + s.shape, s.dtype), s[None], (me, 0, 0)) for s in shards]
    held_land = [pltpu.HBM(t.shape, t.dtype) for t in landing]
    outs = pl.pallas_call(
        body, name="gather_out_weights_start",
        out_shape=(pltpu.SemaphoreType.DMA((7 * n,)), pltpu.SemaphoreType.DMA((7 * n,)), *held, *held_land,
                   jax.ShapeDtypeStruct((SUBLANES, 128), F32)),
        in_specs=[hbm] * (2 * n),
        out_specs=(sem, sem, *[hbm] * (2 * n), pl.BlockSpec(memory_space=pltpu.VMEM)),
        input_output_aliases={i: 2 + i for i in range(2 * n)},
        compiler_params=pltpu.CompilerParams(has_side_effects=pltpu.SideEffectType.DATAFLOW_SIDE_EFFECTING),
    )(*[pltpu.with_memory_space_constraint(s, pltpu.HBM) for s in shards],
      *[pltpu.with_memory_space_constraint(t, pltpu.HBM) for t in landing])
    return outs[0], outs[1], outs[2:2 + n], outs[2 + n:2 + 2 * n], outs[-1]


def _gather_wait(send_sems, recv_sems, shards, lands, after):
    n = len(shards)

    def body(*refs):
        srcs, land_refs = refs[:n], refs[n:2 * n]
        sends, recvs = refs[2 * n:2 * n + 2]
        for cp in _peer_copies(srcs, land_refs, sends, recvs):
            cp.wait_send()
            cp.wait_recv()

    hbm = pl.BlockSpec(memory_space=pltpu.HBM)
    sem = pl.BlockSpec(memory_space=pltpu.SEMAPHORE)
    outs = pl.pallas_call(
        body, name="gather_out_weights_wait",
        out_shape=(*[pltpu.HBM(s.shape, s.dtype) for s in shards], *[pltpu.HBM(t.shape, t.dtype) for t in lands]),
        in_specs=[hbm] * (2 * n) + [sem, sem, pl.BlockSpec(memory_space=pl.ANY)],
        out_specs=[hbm] * (2 * n),
        input_output_aliases={i: i for i in range(2 * n)},
        compiler_params=pltpu.CompilerParams(has_side_effects=pltpu.SideEffectType.DATAFLOW_SIDE_EFFECTING),
    )(*shards, *lands, send_sems, recv_sems, after)
    return outs[n:]


def _rs_to_sibling(name, stacks):
    n = len(stacks)

    def body(*refs):
        ins, outs = refs[:n], refs[n:2 * n]
        send_sems, recv_sems = refs[2 * n:]
        me = _my_pos()
        sib = _flip(me, 1)
        sends = []
        for a in range(n):
            for m in range(4):
                target = _flip(sib, 2 * m)
                cp = pltpu.make_async_remote_copy(
                    src_ref=ins[a].at[_index(target)], dst_ref=outs[a].at[m],
                    send_sem=send_sems.at[a * 4 + m], recv_sem=recv_sems.at[a * 4 + m],
                    device_id=sib, device_id_type=MESH)
                cp.start()
                sends.append(cp)
        for cp in sends:
            cp.wait_recv()
        for cp in sends:
            cp.wait_send()

    any_spec = pl.BlockSpec(memory_space=pl.ANY)
    return pl.pallas_call(
        body, name=name,
        out_shape=[jax.ShapeDtypeStruct((4,) + s.shape[1:], s.dtype) for s in stacks],
        in_specs=[any_spec] * n, out_specs=[any_spec] * n,
        scratch_shapes=[pltpu.SemaphoreType.DMA((4 * n,)), pltpu.SemaphoreType.DMA((4 * n,))],
    )(*stacks)


def _chip_copies(srcs, lands, send_sems, recv_sems):
    me = _my_pos()
    return [pltpu.make_async_remote_copy(
        src_ref=srcs[a].at[m - 1], dst_ref=lands[a].at[m - 1],
        send_sem=send_sems.at[a * 3 + m - 1], recv_sem=recv_sems.at[a * 3 + m - 1],
        device_id=_flip(me, 2 * m), device_id_type=MESH) for a in range(len(srcs)) for m in range(1, 4)]


def _rs_chips_start(sums):
    n = len(sums)

    def body(*refs):
        srcs, lands = refs[:n], refs[n:2 * n]
        send_sems, recv_sems = refs[2 * n:2 * n + 2]
        token = refs[-1]
        for cp in _chip_copies(srcs, lands, send_sems, recv_sems):
            cp.start()
        token[...] = jnp.zeros_like(token)

    hbm = pl.BlockSpec(memory_space=pltpu.HBM)
    sem = pl.BlockSpec(memory_space=pltpu.SEMAPHORE)
    held = [pltpu.HBM(s.shape, s.dtype) for s in sums]
    outs = pl.pallas_call(
        body, name="rs_chips_start",
        out_shape=(pltpu.SemaphoreType.DMA((3 * n,)), pltpu.SemaphoreType.DMA((3 * n,)), *held, *held,
                   jax.ShapeDtypeStruct((SUBLANES, 128), F32)),
        in_specs=[hbm] * (2 * n),
        out_specs=(sem, sem, *[hbm] * (2 * n), pl.BlockSpec(memory_space=pltpu.VMEM)),
        input_output_aliases={i: 2 + i for i in range(2 * n)},
        compiler_params=pltpu.CompilerParams(has_side_effects=pltpu.SideEffectType.DATAFLOW_SIDE_EFFECTING),
    )(*[pltpu.with_memory_space_constraint(s, pltpu.HBM) for s in sums],
      *[pltpu.with_memory_space_constraint(lax.empty(s.shape, s.dtype), pltpu.HBM) for s in sums])
    return outs[0], outs[1], outs[2:2 + n], outs[2 + n:2 + 2 * n], outs[-1]


def _rs_chips_wait(send_sems, recv_sems, srcs, lands, after):
    n = len(srcs)

    def body(*refs):
        src_refs, land_refs = refs[:n], refs[n:2 * n]
        sends, recvs = refs[2 * n:2 * n + 2]
        for cp in _chip_copies(src_refs, land_refs, sends, recvs):
            cp.wait_send()
            cp.wait_recv()

    hbm = pl.BlockSpec(memory_space=pltpu.HBM)
    sem = pl.BlockSpec(memory_space=pltpu.SEMAPHORE)
    held = [pltpu.HBM(s.shape, s.dtype) for s in srcs]
    outs = pl.pallas_call(
        body, name="rs_chips_wait", out_shape=(*held, *held),
        in_specs=[hbm] * (2 * n) + [sem, sem, pl.BlockSpec(memory_space=pl.ANY)],
        out_specs=[hbm] * (2 * n),
        input_output_aliases={i: i for i in range(2 * n)},
        compiler_params=pltpu.CompilerParams(has_side_effects=pltpu.SideEffectType.DATAFLOW_SIDE_EFFECTING),
    )(*srcs, *lands, send_sems, recv_sems, after)
    return outs[n:]


def _add_sibling(name, stack, recv, targets):
    _, rows, cols = stack.shape
    tr = min(rows, ADD_ROWS)

    def own_body(t_ref, a_ref, b_ref, o_ref):
        o_ref[...] = a_ref[...] + b_ref[...].astype(F32)

    own = pl.pallas_call(
        own_body, name=name + "_own",
        out_shape=jax.ShapeDtypeStruct((rows, cols), F32),
        grid_spec=pltpu.PrefetchScalarGridSpec(
            num_scalar_prefetch=1, grid=(rows // tr,),
            in_specs=[pl.BlockSpec((None, tr, cols), lambda i, t: (t[0], i, 0)),
                      pl.BlockSpec((None, tr, cols), lambda i, t: (0, i, 0))],
            out_specs=pl.BlockSpec((tr, cols), lambda i, t: (i, 0))),
        compiler_params=_params(("arbitrary",)),
    )(targets, stack, recv)

    def send_body(t_ref, a_ref, b_ref, o_ref):
        o_ref[...] = (a_ref[...] + b_ref[...].astype(F32)).astype(BF16)

    send = pl.pallas_call(
        send_body, name=name + "_send",
        out_shape=jax.ShapeDtypeStruct((3, rows, cols), BF16),
        grid_spec=pltpu.PrefetchScalarGridSpec(
            num_scalar_prefetch=1, grid=(3, rows // tr),
            in_specs=[pl.BlockSpec((None, tr, cols), lambda m, i, t: (t[m + 1], i, 0)),
                      pl.BlockSpec((None, tr, cols), lambda m, i, t: (m + 1, i, 0))],
            out_specs=pl.BlockSpec((None, tr, cols), lambda m, i, t: (m, i, 0))),
        compiler_params=_params(("arbitrary", "arbitrary")),
    )(targets, stack, recv)
    return own, send


def _allreduce_small(name, v):
    rows, cols = v.shape
    half = rows // 2
    assert rows % (2 * SUBLANES) == 0

    def body(v_ref, out_ref, from_sib, chip_half, from_chips, send_sems, recv_sems):
        me = _my_pos()
        sib = _flip(me, 1)
        mine = pl.ds(pl.multiple_of(me[2] * half, SUBLANES), half)
        theirs = pl.ds(pl.multiple_of((1 - me[2]) * half, SUBLANES), half)

        def copy(k, src, dst, to):
            return pltpu.make_async_remote_copy(src_ref=src, dst_ref=dst, send_sem=send_sems.at[k],
                                                recv_sem=recv_sems.at[k], device_id=to, device_id_type=MESH)

        to_sib = copy(0, v_ref.at[theirs], from_sib, sib)
        to_sib.start()
        to_sib.wait_recv()
        chip_half[...] = v_ref[mine, :] + from_sib[...]
        to_chips = [copy(m, chip_half, from_chips.at[m - 1], _flip(me, 2 * m)) for m in range(1, 4)]
        for cp in to_chips:
            cp.start()
        for cp in to_chips:
            cp.wait_recv()
        my_chip = 2 * me[0] + me[1]
        total = None
        for chip in range(4):
            slot = jnp.maximum(jnp.bitwise_xor(chip, my_chip) - 1, 0)
            part = jnp.where(chip == my_chip, chip_half[...], from_chips[slot])
            total = part if total is None else total + part
        out_ref[mine, :] = total
        swap = copy(4, out_ref.at[mine], out_ref.at[mine], sib)
        swap.start()
        copy(4, out_ref.at[theirs], out_ref.at[theirs], sib).wait_recv()
        for cp in [to_sib, swap] + to_chips:
            cp.wait_send()

    return pl.pallas_call(
        body, name=name, out_shape=jax.ShapeDtypeStruct((rows, cols), F32),
        in_specs=[pl.BlockSpec(memory_space=pltpu.VMEM)],
        out_specs=pl.BlockSpec(memory_space=pltpu.VMEM),
        scratch_shapes=[pltpu.VMEM((half, cols), F32), pltpu.VMEM((half, cols), F32),
                        pltpu.VMEM((3, half, cols), F32),
                        pltpu.SemaphoreType.DMA((5,)), pltpu.SemaphoreType.DMA((5,))],
        compiler_params=_params(None, VMEM_LIMIT),
    )(v)


def _mod_fwd(c_all, w_mod):
    def body(c_ref, w_ref, o_ref):
        c = c_ref[...]
        o_ref[...] = jnp.dot(c * _sigmoid(c), w_ref[...], preferred_element_type=F32,
                             precision=lax.Precision.HIGHEST)

    return pl.pallas_call(
        body, name="mod_fwd", out_shape=jax.ShapeDtypeStruct((N_DEV, w_mod.shape[1]), F32),
    )(c_all, w_mod)


def _mod_bwd(c_all, dmod_all, dmod_cols):
    def body(c_ref, da_ref, dc_ref, gb_ref, gw_ref):
        c = c_ref[...]
        acc = da_ref[0:1, :]
        for b in range(1, N_DEV):
            acc = acc + da_ref[b:b + 1, :]
        gb_ref[...] = acc
        gw_ref[...] = lax.dot_general(c * _sigmoid(c), dc_ref[...], (((0,), (0,)), ((), ())),
                                      preferred_element_type=F32, precision=lax.Precision.HIGHEST)

    return pl.pallas_call(
        body, name="mod_bwd",
        out_shape=[jax.ShapeDtypeStruct((1, dmod_all.shape[1]), F32),
                   jax.ShapeDtypeStruct((c_all.shape[1], dmod_cols.shape[1]), F32)],
    )(c_all, dmod_all, dmod_cols)


def _rope_partner(t):
    lane = lax.broadcasted_iota(jnp.int32, t.shape, 1)
    return jnp.where(lane < ROT_HALF, pltpu.roll(t, HEAD_DIM - ROT_HALF, 1), pltpu.roll(t, ROT_HALF, 1))


def _norm(x, mod, b_mod, g_norm):
    seq = x.shape[0]
    tm = PROJ_ROWS

    def body(x_ref, mod_ref, bmod_ref, g_ref, h_ref):
        xf = x_ref[...]
        rstd = lax.rsqrt(jnp.mean(xf * xf, axis=-1, keepdims=True) + NORM_EPS)
        shift = mod_ref[:, 0:D_MODEL] + bmod_ref[:, 0:D_MODEL]
        scale = mod_ref[:, D_MODEL:2 * D_MODEL] + bmod_ref[:, D_MODEL:2 * D_MODEL]
        h_ref[...] = (((xf * rstd) * g_ref[...]) * (1.0 + scale) + shift).astype(BF16)

    row = pl.BlockSpec((tm, D_MODEL), lambda i: (i, 0))
    const = lambda cols: pl.BlockSpec((1, cols), lambda i: (0, 0))
    return pl.pallas_call(
        body, name="norm", out_shape=jax.ShapeDtypeStruct((seq, D_MODEL), BF16), grid=(seq // tm,),
        in_specs=[row, const(3 * D_MODEL), const(3 * D_MODEL), const(D_MODEL)], out_specs=row,
        compiler_params=_params(("arbitrary",), VMEM_LIMIT),
    )(x, mod, b_mod, g_norm)


def _proj(h, w_in_all, cosf, sinf):
    seq = h.shape[0]
    tm = PROJ_ROWS
    last = seq // tm - 1

    def body(h_ref, w_ref, cos_ref, sin_ref, pf_ref, q_ref, k_ref, v_ref):
        j = pl.program_id(0)

        @pl.when((j < 2) | (j > 4))
        def _():
            pf_ref[...] = _dot(h_ref[...], w_ref[...])

        def heads(dst_ref, rotate, gain):
            for pair in range(N_HEADS // 2):
                both = _dot(h_ref[...], w_ref[:, 2 * pair * HEAD_DIM:2 * (pair + 1) * HEAD_DIM])
                for hh in (2 * pair, 2 * pair + 1):
                    t = both[:, (hh % 2) * HEAD_DIM:(hh % 2 + 1) * HEAD_DIM]
                    if rotate:
                        t = t * cos_ref[...] + _rope_partner(t) * sin_ref[...]
                    dst_ref[hh] = t if gain is None else t * gain

        @pl.when(j == 2)
        def _():
            heads(q_ref, True, ATTN_SCALE)

        @pl.when(j == 3)
        def _():
            heads(k_ref, True, None)

        @pl.when(j == 4)
        def _():
            heads(v_ref, False, None)

    def pf_block(j, i):
        f32_piece = (j < 2) | (j > 4)
        return (jnp.where(f32_piece, i, last), jnp.where(j < 2, j, jnp.where(j < 5, 1, j - 3)))

    def hm_block(piece):
        return lambda j, i: (0, jnp.where(j == piece, i, jnp.where(j < piece, 0, last)), 0)

    hm = jax.ShapeDtypeStruct((N_HEADS, seq, HEAD_DIM), F32)
    hm_spec = lambda piece: pl.BlockSpec((N_HEADS, tm, HEAD_DIM), hm_block(piece))
    row = lambda j, i: (i, 0)
    return pl.pallas_call(
        body, name="proj",
        out_shape=[jax.ShapeDtypeStruct((seq, 5 * D_MODEL), F32), hm, hm, hm],
        grid=(8, seq // tm),
        in_specs=[pl.BlockSpec((tm, D_MODEL), row),
                  pl.BlockSpec((None, D_MODEL, D_MODEL), lambda j, i: (j, 0, 0)),
                  pl.BlockSpec((tm, HEAD_DIM), row), pl.BlockSpec((tm, HEAD_DIM), row)],
        out_specs=[pl.BlockSpec((tm, D_MODEL), pf_block), hm_spec(2), hm_spec(3), hm_spec(4)],
        compiler_params=_params(("arbitrary", "arbitrary"), VMEM_LIMIT),
    )(h, w_in_all, cosf, sinf)


def _shift_down(v, s, head):
    rolled = pltpu.roll(v, s, 0)
    row = lax.broadcasted_iota(jnp.int32, head.shape, 0)
    first = jnp.where(row < s, pltpu.roll(head, s, 0), rolled[:SUBLANES, :])
    return jnp.concatenate([first, rolled[SUBLANES:, :]], axis=0)


def _shift_up(v, s, tail):
    rows = v.shape[0]
    rolled = pltpu.roll(v, rows - s, 0)
    row = lax.broadcasted_iota(jnp.int32, tail.shape, 0)
    last = jnp.where(row >= SUBLANES - s, pltpu.roll(tail, SUBLANES - s, 0), rolled[rows - SUBLANES:, :])
    return jnp.concatenate([rolled[:rows - SUBLANES, :], last], axis=0)


def _doubling(a, b, period, reverse):
    rows = a.shape[0]
    pos = lax.broadcasted_iota(jnp.int32, a.shape, 0) & (period - 1)
    k = 1
    while k < period:
        inside = (pos < period - k) if reverse else (pos >= k)
        shift = rows - k if reverse else k
        a_s = jnp.where(inside, pltpu.roll(a, shift, 0), 1.0)
        b_s = jnp.where(inside, pltpu.roll(b, shift, 0), 0.0)
        b = a * b_s + b
        a = a * a_s
        k *= 2
    return a, b


def _scan(a, b, boundary, reverse, a_scr, b_scr, spread):
    rows = a.shape[0]
    ntile = rows // SUBLANES
    a_scr[...], b_scr[...] = _doubling(a, b, SUBLANES, reverse)
    ends = pl.ds(0 if reverse else SUBLANES - 1, ntile, stride=SUBLANES)
    a_end, x_end = _doubling(a_scr[ends, :], b_scr[ends, :], ntile, reverse)
    x_end = x_end + a_end * boundary
    tile = lax.broadcasted_iota(jnp.int32, x_end.shape, 0)
    if reverse:
        incoming = jnp.where(tile == ntile - 1, boundary, pltpu.roll(x_end, ntile - 1, 0))
        last = x_end[0:1, :]
    else:
        incoming = jnp.where(tile == 0, boundary, pltpu.roll(x_end, 1, 0))
        last = x_end[ntile - 1:ntile, :]
    for s in range(SUBLANES):
        spread[pl.ds(s, ntile, stride=SUBLANES), :] = incoming
    return b_scr[...] + a_scr[...] * spread[...], last


def _conv_taps(xr, head):
    return [_shift_down(xr, 3, head), _shift_down(xr, 2, head), _shift_down(xr, 1, head), xr]


def _rnn_gates(xc, wa, ba, wx, bx, lam, keep):
    xcb = xc.astype(BF16)
    r = _sigmoid(_dot(xcb, wa.astype(BF16)) + ba)
    i = _sigmoid(_dot(xcb, wx.astype(BF16)) + bx)
    softplus = jnp.maximum(-lam, 0.0) + jnp.log(1.0 + jnp.exp(-jnp.abs(lam)))
    cl = -LRU_C * softplus
    log_a = cl * r
    a_raw = jnp.exp(log_a)
    mult_raw = jnp.sqrt(-_expm1_nonpos(2.0 * log_a, a_raw * a_raw))
    live = keep > 0.0
    return r, i, cl, a_raw, mult_raw, jnp.where(live, a_raw, 0.0), jnp.where(live, mult_raw, 1.0), live


def _rnn_specs(seq, rows, time_of):
    per = rows // SUBLANES
    vec = pl.BlockSpec((None, 1, 128), lambda hb, n: (hb, 0, 0))
    mat = pl.BlockSpec((None, 128, 128), lambda hb, n: (hb, 0, 0))
    return [pl.BlockSpec((rows, 128), lambda hb, n: (time_of(n), hb)),
            pl.BlockSpec((SUBLANES, 128), lambda hb, n: (jnp.maximum(time_of(n) * per - 1, 0), hb)),
            pl.BlockSpec((rows, 1), lambda hb, n: (time_of(n), 0)),
            pl.BlockSpec((None, SUBLANES, 128), lambda hb, n: (hb, 0, 0)),
            vec, mat, vec, mat, vec, vec]


def _rnn_fwd(pf, keep, conv_w8, conv_b, w_a, b_a, w_x, b_x, lam):
    seq = pf.shape[0]
    rows = RNN_ROWS

    def body(x_ref, xh_ref, keep_ref, cw_ref, cb_ref, wa_ref, ba_ref, wx_ref, bx_ref, lam_ref, hr_ref,
             carry, a_scr, b_scr, spread):
        n = pl.program_id(1)

        @pl.when(n == 0)
        def _():
            carry[...] = jnp.zeros_like(carry)

        xr = x_ref[...]
        head = jnp.where(n > 0, xh_ref[...], 0.0)
        taps = _conv_taps(xr, head)
        xc = cb_ref[...] + sum(cw_ref[k:k + 1, :] * taps[k] for k in range(4))
        _, i, _, _, _, a, mult, _ = _rnn_gates(xc, wa_ref[...], ba_ref[...], wx_ref[...], bx_ref[...],
                                               lam_ref[...], keep_ref[...])
        h, last = _scan(a, mult * i * xc, carry[0:1, :], False, a_scr, b_scr, spread)
        hr_ref[...] = h
        carry[...] = jnp.broadcast_to(last, carry.shape)

    chunk_f32 = pltpu.VMEM((rows, 128), F32)
    return pl.pallas_call(
        body, name="rnn_fwd",
        out_shape=jax.ShapeDtypeStruct((seq, D_MODEL), F32),
        grid=(RNN_BLOCKS, seq // rows),
        in_specs=_rnn_specs(seq, rows, lambda n: n),
        out_specs=pl.BlockSpec((rows, 128), lambda hb, n: (n, hb)),
        scratch_shapes=[pltpu.VMEM((SUBLANES, 128), F32), chunk_f32, chunk_f32, chunk_f32],
        compiler_params=_params(("arbitrary", "arbitrary"), VMEM_LIMIT),
    )(pf, pf, keep, conv_w8, conv_b, w_a, b_a, w_x, b_x, lam)


def _rnn_bwd(pf, hr, dhr, keep, conv_w8, conv_b, w_a, b_a, w_x, b_x, lam):
    seq = pf.shape[0]
    rows = RNN_ROWS
    nchunk = seq // rows
    per = rows // SUBLANES
    time_of = lambda n: nchunk - 1 - n

    def body(x_ref, xh_ref, keep_ref, cw_ref, cb_ref, wa_ref, ba_ref, wx_ref, bx_ref, lam_ref,
             hr_ref, hrh_ref, dhr_ref,
             dx_ref, gcw_ref, gcb_ref, gwa_ref, gba_ref, gwx_ref, gbx_ref, glam_ref,
             g_carry, dxc_tail, a_scr, b_scr, spread):
        n = pl.program_id(1)
        first_in_time = n == nchunk - 1

        @pl.when(n == 0)
        def _():
            g_carry[...] = jnp.zeros_like(g_carry)
            dxc_tail[...] = jnp.zeros_like(dxc_tail)
            for ref in (gcw_ref, gcb_ref, gwa_ref, gba_ref, gwx_ref, gbx_ref, glam_ref):
                ref[...] = jnp.zeros_like(ref)

        xr = x_ref[...]
        head = jnp.where(first_in_time, 0.0, xh_ref[...])
        taps = _conv_taps(xr, head)
        cw = cw_ref[...]
        xc = cb_ref[...] + sum(cw[k:k + 1, :] * taps[k] for k in range(4))
        wa, wx, lam = wa_ref[...], wx_ref[...], lam_ref[...]
        r, i, cl, a_raw, mult_raw, a, mult, live = _rnn_gates(xc, wa, ba_ref[...], wx, bx_ref[...], lam,
                                                               keep_ref[...])
        h_prev = _shift_down(hr_ref[...], 1, jnp.where(first_in_time, 0.0, hrh_ref[...]))

        row = lax.broadcasted_iota(jnp.int32, xr.shape, 0)
        last = row == rows - 1
        a_next = jnp.where(last, 0.0, pltpu.roll(a, rows - 1, 0))
        g, g_first = _scan(a_next, dhr_ref[...] + jnp.where(last, g_carry[0:1, :], 0.0),
                           jnp.zeros((1, 128), F32), True, a_scr, b_scr, spread)
        g_carry[...] = jnp.broadcast_to(a[0:1, :] * g_first, g_carry.shape)

        da = g * h_prev
        dmult = g * i * xc
        di = g * mult * xc
        dxc = g * mult * i
        dlog_a = jnp.where(live, da * a_raw - dmult * a_raw * a_raw / mult_raw, 0.0)
        dpa = (dlog_a * cl) * r * (1.0 - r)
        dpx = di * i * (1.0 - i)
        glam_ref[...] += jnp.sum(dlog_a * r, axis=0, keepdims=True) * (LRU_C * _sigmoid(-lam))
        xcb, dpab, dpxb = xc.astype(BF16), dpa.astype(BF16), dpx.astype(BF16)
        gwa_ref[...] += _dot_tn(xcb, dpab)
        gwx_ref[...] += _dot_tn(xcb, dpxb)
        gba_ref[...] += jnp.sum(dpa, axis=0, keepdims=True)
        gbx_ref[...] += jnp.sum(dpx, axis=0, keepdims=True)
        dxc = dxc + _dot_nt(dpab, wa.astype(BF16)) + _dot_nt(dpxb, wx.astype(BF16))

        gcb_ref[...] += jnp.sum(dxc, axis=0, keepdims=True)
        for k in range(4):
            gcw_ref[k:k + 1, :] += jnp.sum(dxc * taps[k], axis=0, keepdims=True)
        tail = dxc_tail[...]
        dx = cw[3:4, :] * dxc
        for k in range(3):
            dx = dx + cw[k:k + 1, :] * _shift_up(dxc, 3 - k, tail)
        dx_ref[...] = dx.astype(BF16)
        dxc_tail[...] = dxc[0:SUBLANES, :]

    blk = lambda hb, n: (hb, 0, 0)
    chunk = pl.BlockSpec((rows, 128), lambda hb, n: (time_of(n), hb))
    vec_out = pl.BlockSpec((None, 1, 128), blk)
    mat_out = pl.BlockSpec((None, 128, 128), blk)
    vec_shape = jax.ShapeDtypeStruct((RNN_BLOCKS, 1, 128), F32)
    mat_shape = jax.ShapeDtypeStruct((RNN_BLOCKS, 128, 128), F32)
    return pl.pallas_call(
        body, name="rnn_bwd",
        out_shape=[jax.ShapeDtypeStruct((seq, D_MODEL), BF16),
                   jax.ShapeDtypeStruct((RNN_BLOCKS, SUBLANES, 128), F32), vec_shape,
                   mat_shape, vec_shape, mat_shape, vec_shape, vec_shape],
        grid=(RNN_BLOCKS, nchunk),
        in_specs=_rnn_specs(seq, rows, time_of) + [
            chunk, pl.BlockSpec((SUBLANES, 128), lambda hb, n: (jnp.maximum(time_of(n) * per - 1, 0), hb)), chunk],
        out_specs=[chunk, pl.BlockSpec((None, SUBLANES, 128), blk), vec_out,
                   mat_out, vec_out, mat_out, vec_out, vec_out],
        scratch_shapes=[pltpu.VMEM((SUBLANES, 128), F32), pltpu.VMEM((SUBLANES, 128), F32)]
                       + [pltpu.VMEM((rows, 128), F32)] * 3,
        compiler_params=_params(("arbitrary", "arbitrary"), VMEM_LIMIT),
    )(pf, pf, keep, conv_w8, conv_b, w_a, b_a, w_x, b_x, lam, hr, hr, dhr)


def _unit_rows(dil, r, j):
    start = j * KEY_BLOCK * dil + r
    return pl.ds(start, KEY_BLOCK) if dil == 1 else pl.ds(start, KEY_BLOCK, stride=dil)


def _attn_fwd(q, k, v):
    nh, seq, _ = q.shape
    nchunk = seq // SPAN
    nblk = SPAN // KEY_BLOCK
    wide = DILATIONS[-1]

    def body(q_ref, k_ref, v_ref, kp_ref, vp_ref, o_ref, l1_ref, l4_ref, l16_ref,
             acc, m_s, l_s, q16, k16, v16, k16p, v16p, acc16, m16, l16, tmp):
        n = pl.program_id(1)
        qi = lax.broadcasted_iota(jnp.int32, (KEY_BLOCK, KEY_BLOCK), 0)
        ki = lax.broadcasted_iota(jnp.int32, (KEY_BLOCK, KEY_BLOCK), 1)
        bias_own = jnp.where(ki <= qi, 0.0, NEG_INF)
        bias_before = jnp.where(ki >= qi, 0.0, NEG_INF)
        bias_mid = jnp.concatenate([bias_before, bias_own], axis=1)
        bias_first = jnp.concatenate([jnp.where(n > 0, bias_before, NEG_INF), bias_own], axis=1)
        ones = jnp.ones((2 * KEY_BLOCK, HEAD_DIM), BF16)
        diag = qi == ki

        @pl.when(n == 0)
        def _():
            k16p[...] = jnp.zeros_like(k16p)
            v16p[...] = jnp.zeros_like(v16p)

        def unit(qf, kpb, kb, vpb, vb, bias, state, rows, first):
            acc_r, m_r, l_r = state
            kcat = jnp.concatenate([kpb, kb], axis=0)
            vaug = jnp.concatenate([jnp.concatenate([vpb, vb], axis=0), ones], axis=1)
            s = _dot_nt(qf.astype(BF16), kcat) + bias
            mx = jnp.max(s, axis=-1, keepdims=True)
            if first:
                m_new = jnp.broadcast_to(mx, (KEY_BLOCK, HEAD_DIM))
            else:
                m_old = m_r[rows, :]
                m_new = jnp.maximum(m_old, mx)
            pv = _dot(jnp.exp(s - jnp.concatenate([m_new, m_new], axis=1)).astype(BF16), vaug)
            if first:
                acc_r[rows, :] = pv[:, :HEAD_DIM]
                l_r[rows, :] = pv[:, HEAD_DIM:]
            else:
                alpha = jnp.exp(m_old - m_new)
                acc_r[rows, :] = alpha * acc_r[rows, :] + pv[:, :HEAD_DIM]
                l_r[rows, :] = alpha * l_r[rows, :] + pv[:, HEAD_DIM:]
            m_r[rows, :] = m_new

        for gi, dil in enumerate(DILATIONS[:-1]):
            nb = nblk // dil
            for r in range(dil):
                prow = _unit_rows(dil, r, nb - 1)
                kpb, vpb = kp_ref[prow, :].astype(BF16), vp_ref[prow, :].astype(BF16)
                for j in range(nb):
                    rows = _unit_rows(dil, r, j)
                    kb, vb = k_ref[rows, :].astype(BF16), v_ref[rows, :].astype(BF16)
                    unit(q_ref[rows, :], kpb, kb, vpb, vb, bias_first if j == 0 else bias_mid,
                         (acc, m_s, l_s), rows, gi == 0)
                    kpb, vpb = kb, vb

        for src, dst in ((q_ref, q16), (k_ref, k16), (v_ref, v16), (acc, acc16), (m_s, m16), (l_s, l16)):
            _to_residue_major(src, tmp, dst)
        for r in range(wide):
            rows = pl.ds(r * KEY_BLOCK, KEY_BLOCK)
            unit(q16[rows, :], k16p[rows, :].astype(BF16), k16[rows, :].astype(BF16), v16p[rows, :].astype(BF16),
                 v16[rows, :].astype(BF16), bias_first, (acc16, m16, l16), rows, False)
        k16p[...] = k16[...]
        v16p[...] = v16[...]

        den = l16[...]
        acc16[...] = acc16[...] * (1.0 / den)
        m16[...] = m16[...] + jnp.log(den)
        _from_residue_major(acc16, tmp, o_ref, False)
        _from_residue_major(m16, tmp, m_s, False)

        def lse_row(ref, rows):
            return jnp.sum(jnp.where(diag, ref[rows, :], 0.0), axis=0, keepdims=True)

        for dil, out in zip(DILATIONS[:-1], (l1_ref, l4_ref)):
            nb = nblk // dil
            for r in range(dil):
                for j in range(nb):
                    out[r * nb + j:r * nb + j + 1, :] = lse_row(m_s, _unit_rows(dil, r, j))
        for r in range(wide):
            l16_ref[r:r + 1, :] = lse_row(m16, pl.ds(r * KEY_BLOCK, KEY_BLOCK))

    blk = pl.BlockSpec((None, SPAN, HEAD_DIM), lambda h, n: (h, n, 0))
    pblk = pl.BlockSpec((None, SPAN, HEAD_DIM), lambda h, n: (h, jnp.maximum(n - 1, 0), 0))
    lblk = pl.BlockSpec((None, nblk, KEY_BLOCK), lambda h, n: (h, n, 0))
    lshape = jax.ShapeDtypeStruct((nh, seq // KEY_BLOCK, KEY_BLOCK), F32)
    o, l1, l4, l16 = pl.pallas_call(
        body, name="attn_fwd",
        out_shape=[jax.ShapeDtypeStruct((nh, seq, HEAD_DIM), F32), lshape, lshape, lshape],
        grid=(nh, nchunk), in_specs=[blk, blk, blk, pblk, pblk], out_specs=[blk, lblk, lblk, lblk],
        scratch_shapes=[pltpu.VMEM((SPAN, HEAD_DIM), F32)] * 12,
        compiler_params=_params(("arbitrary", "arbitrary"), VMEM_LIMIT),
    )(q, k, v, k, v)
    return o, (l1, l4, l16)


def _to_residue_major(src, tmp, dst):
    quarter = SPAN // 4
    for r4 in range(4):
        tmp[r4 * quarter:(r4 + 1) * quarter, :] = src[pl.ds(r4, quarter, stride=4), :]
    for r4 in range(4):
        for rp in range(4):
            r = r4 + 4 * rp
            dst[r * KEY_BLOCK:(r + 1) * KEY_BLOCK, :] = tmp[pl.ds(r4 * quarter + rp, KEY_BLOCK, stride=4), :]


def _from_residue_major(src, tmp, dst, add):
    quarter = SPAN // 4
    for r4 in range(4):
        for rp in range(4):
            r = r4 + 4 * rp
            tmp[pl.ds(r4 * quarter + rp, KEY_BLOCK, stride=4), :] = src[r * KEY_BLOCK:(r + 1) * KEY_BLOCK, :]
    for r4 in range(4):
        rows = pl.ds(r4, quarter, stride=4)
        part = tmp[r4 * quarter:(r4 + 1) * quarter, :]
        dst[rows, :] = dst[rows, :] + part if add else part


def _attn_bwd(q, k, v, do, o, lses, cosf, sinf):
    nh, seq, _ = q.shape
    nchunk = seq // SPAN
    nblk = SPAN // KEY_BLOCK
    wide = DILATIONS[-1]
    assert SPAN == wide * KEY_BLOCK

    def body(q_ref, k_ref, v_ref, do_ref, o_ref, kp_ref, vp_ref, l1_ref, l4_ref, l16_ref,
             cos_ref, sin_ref, cosp_ref, sinp_ref, dq_ref, dk_ref, dv_ref,
             dq_acc, dkc_acc, dvc_acc, dkp_acc, dvp_acc, q16, k16, v16, do16, o16, k16p, v16p,
             dq16, dkc16, dvc16, dkp16, dvp16, tmp, pt_s, ds_s, kcat_s, qb_s, dob_s):
        n = pl.program_id(1)
        ki = lax.broadcasted_iota(jnp.int32, (KEY_BLOCK, KEY_BLOCK), 0)
        qi = lax.broadcasted_iota(jnp.int32, (KEY_BLOCK, KEY_BLOCK), 1)
        bias_own = jnp.where(ki <= qi, 0.0, NEG_INF)
        bias_before = jnp.where(ki >= qi, 0.0, NEG_INF)
        bias_mid = jnp.concatenate([bias_before, bias_own], axis=0)
        bias_first = jnp.concatenate([jnp.where(n > 0, bias_before, NEG_INF), bias_own], axis=0)
        ones8 = jnp.ones((SUBLANES, HEAD_DIM), BF16)

        def row_dot(a, b):
            prod = a * b
            hi = prod.astype(BF16)
            lo = (prod - hi.astype(F32)).astype(BF16)
            return (_dot_nt(ones8, hi) + _dot_nt(ones8, lo))[0:1, :]

        def group(units, srcs, before, l_ref, accs):
            src_q, src_do, src_o, src_k, src_v = srcs
            before_k, before_v = before
            acc_q, acc_kc, acc_vc, acc_kp, acc_vp = accs
            kb = vb = None
            for u, (rows, prow, outside, lrow, _) in enumerate(units):
                dof = src_do[rows, :]
                qb, dob = src_q[rows, :].astype(BF16), dof.astype(BF16)
                kpb, vpb = (before_k[prow, :].astype(BF16), before_v[prow, :].astype(BF16)) if outside else (kb, vb)
                kb, vb = src_k[rows, :].astype(BF16), src_v[rows, :].astype(BF16)
                kcat = jnp.concatenate([kpb, kb], axis=0)
                vcat = jnp.concatenate([vpb, vb], axis=0)
                bias = bias_first if outside else bias_mid
                pt = jnp.exp(_dot_nt(kcat, qb) + bias - l_ref[lrow:lrow + 1, :])
                dst = pt * (_dot_nt(vcat, dob) - row_dot(dof, src_o[rows, :]))
                pt_s[u], ds_s[u], kcat_s[u], qb_s[u], dob_s[u] = pt.astype(BF16), dst.astype(BF16), kcat, qb, dob
            for u, (rows, _, _, _, _) in enumerate(units):
                acc_q[rows, :] += _dot_tn(ds_s[u], kcat_s[u])
            for u, (rows, prow, outside, _, nxt) in enumerate(units):
                dk = _dot(ds_s[u, KEY_BLOCK:, :], qb_s[u])
                dv = _dot(pt_s[u, KEY_BLOCK:, :], dob_s[u])
                if nxt is not None:
                    dk = dk + _dot(ds_s[nxt, :KEY_BLOCK, :], qb_s[nxt])
                    dv = dv + _dot(pt_s[nxt, :KEY_BLOCK, :], dob_s[nxt])
                acc_kc[rows, :] += dk
                acc_vc[rows, :] += dv
                if outside:
                    acc_kp[prow, :] += _dot(ds_s[u, :KEY_BLOCK, :], qb_s[u])
                    acc_vp[prow, :] += _dot(pt_s[u, :KEY_BLOCK, :], dob_s[u])

        @pl.when(n == 0)
        def _():
            for ref in (dkp_acc, dvp_acc, dkp16, dvp16, k16p, v16p):
                ref[...] = jnp.zeros_like(ref)

        @pl.when(n < nchunk)
        def _():
            for ref in (dq_acc, dkc_acc, dvc_acc, dq16, dkc16, dvc16):
                ref[...] = jnp.zeros_like(ref)
            for src, dst in ((q_ref, q16), (k_ref, k16), (v_ref, v16), (do_ref, do16), (o_ref, o16)):
                _to_residue_major(src, tmp, dst)
            natural = (q_ref, do_ref, o_ref, k_ref, v_ref)
            for dil, l_ref in zip(DILATIONS[:-1], (l1_ref, l4_ref)):
                nb = nblk // dil
                units = [(_unit_rows(dil, r, j), _unit_rows(dil, r, (j - 1) % nb), j == 0, r * nb + j,
                          r * nb + j + 1 if j + 1 < nb else None) for r in range(dil) for j in range(nb)]
                group(units, natural, (kp_ref, vp_ref), l_ref, (dq_acc, dkc_acc, dvc_acc, dkp_acc, dvp_acc))
            blocks = [pl.ds(r * KEY_BLOCK, KEY_BLOCK) for r in range(wide)]
            group([(rows, rows, True, r, None) for r, rows in enumerate(blocks)], (q16, do16, o16, k16, v16),
                  (k16p, v16p), l16_ref, (dq16, dkc16, dvc16, dkp16, dvp16))
            _from_residue_major(dq16, tmp, dq_acc, True)
            dq = dq_acc[...]
            dq_ref[...] = ((dq * cos_ref[...] - _rope_partner(dq) * sin_ref[...]) * ATTN_SCALE).astype(BF16)

        @pl.when(n > 0)
        def _():
            _from_residue_major(dkp16, tmp, dkp_acc, True)
            _from_residue_major(dvp16, tmp, dvp_acc, True)
            dk = dkp_acc[...]
            dk_ref[...] = (dk * cosp_ref[...] - _rope_partner(dk) * sinp_ref[...]).astype(BF16)
            dv_ref[...] = dvp_acc[...].astype(BF16)

        @pl.when(n < nchunk)
        def _():
            for src, dst in ((dkc_acc, dkp_acc), (dvc_acc, dvp_acc), (dkc16, dkp16), (dvc16, dvp16),
                             (k16, k16p), (v16, v16p)):
                dst[...] = src[...]

    last = nchunk - 1
    cur = lambda h, n: (h, jnp.minimum(n, last), 0)
    prev = lambda h, n: (h, jnp.clip(n - 1, 0, last), 0)
    blk = lambda idx: pl.BlockSpec((None, SPAN, HEAD_DIM), idx)
    lblk = pl.BlockSpec((None, nblk, KEY_BLOCK), cur)
    tab = pl.BlockSpec((SPAN, HEAD_DIM), lambda h, n: (jnp.minimum(n, last), 0))
    tabp = pl.BlockSpec((SPAN, HEAD_DIM), lambda h, n: (jnp.clip(n - 1, 0, last), 0))
    out_q = pl.BlockSpec((SPAN, HEAD_DIM), lambda h, n: (jnp.minimum(n, last), h))
    out_kv = pl.BlockSpec((SPAN, HEAD_DIM), lambda h, n: (jnp.clip(n - 1, 0, last), h))
    shape = jax.ShapeDtypeStruct((seq, nh * HEAD_DIM), BF16)
    return pl.pallas_call(
        body, name="attn_bwd", out_shape=[shape, shape, shape], grid=(nh, nchunk + 1),
        in_specs=[blk(cur)] * 5 + [blk(prev)] * 2 + [lblk] * 3 + [tab, tab, tabp, tabp],
        out_specs=[out_q, out_kv, out_kv],
        scratch_shapes=[pltpu.VMEM((SPAN, HEAD_DIM), F32)] * 18
                       + [pltpu.VMEM((nblk, 2 * KEY_BLOCK, HEAD_DIM), BF16)] * 3
                       + [pltpu.VMEM((nblk, KEY_BLOCK, HEAD_DIM), BF16)] * 2,
        compiler_params=_params(("arbitrary", "arbitrary"), VMEM_LIMIT),
    )(q, k, v, do, o, k, v, *lses, cosf, sinf, cosf, sinf)


def _hub(x, tgt, hr, pf, o_hm, mod, b_mod, b_gate, g_final, w_out_rnn, w_out_attn, w_o):
    seq = x.shape[0]
    tm = HUB_ROWS
    nsteps = seq // tm

    def body(x_ref, t_ref, hr_ref, zr_ref, za_ref, gr_ref, ga_ref, o_ref, mod_ref, bmod_ref, bg_ref, gf_ref,
             wr_hbm, wa_hbm, wo_hbm,
             dx2_ref, dhr_ref, dzr_ref, do_ref, dza_ref, dgr_ref, dga_ref,
             ur_ref, dyr_ref, ua_ref, dya_ref, mg_ref, dmo_ref,
             ggf_ref, gbg_ref, dgate_ref, loss_ref,
             wr, wa, wo, sem):
        step = pl.program_id(0)

        @pl.when(step == 0)
        def _():
            for src, dst in ((wr_hbm, wr), (wa_hbm, wa), (wo_hbm, wo)):
                cp = pltpu.make_async_copy(src, dst, sem)
                cp.start()
                cp.wait()
            for ref in (ggf_ref, gbg_ref, dgate_ref, loss_ref):
                ref[...] = jnp.zeros_like(ref)

        gate = mod_ref[:, 2 * D_MODEL:] + bmod_ref[:, 2 * D_MODEL:]
        gfin = gf_ref[...]
        hr_t, zr, za = hr_ref[...], zr_ref[...], za_ref[...]
        o = jnp.concatenate([o_ref[hh] for hh in range(N_HEADS)], axis=1)
        sig_zr, sig_za = _sigmoid(zr), _sigmoid(za)
        silu_zr, silu_za = zr * sig_zr, za * sig_za
        u_rnn = (hr_t * silu_zr).astype(BF16)
        u_attn = (o * silu_za).astype(BF16)
        y_rnn = _dot(u_rnn, wr[...])
        y_attn = _dot(u_attn, wa[...])
        sr = _sigmoid(gr_ref[...] + bg_ref[:, :D_MODEL])
        sa = _sigmoid(ga_ref[...] + bg_ref[:, D_MODEL:])
        merged = (sr * y_rnn + sa * y_attn).astype(BF16)
        mo = _dot(merged, wo[...])
        x2 = x_ref[...] + gate * mo
        rstd = lax.rsqrt(jnp.mean(x2 * x2, axis=-1, keepdims=True) + NORM_EPS)
        xn = x2 * rstd
        err = xn * gfin - t_ref[...]
        loss_ref[...] += 0.5 * jnp.sum(jnp.sum(err * err, axis=-1, keepdims=True) * (1.0 / D_MODEL),
                                       axis=0, keepdims=True)

        dy = err * (1.0 / D_MODEL)
        ggf_ref[...] += jnp.sum(dy * xn, axis=0, keepdims=True)
        dxn = dy * gfin
        dx2 = rstd * (dxn - xn * jnp.mean(dxn * xn, axis=-1, keepdims=True))
        dx2_ref[...] = dx2
        dgate_ref[...] += jnp.sum(dx2 * mo, axis=0, keepdims=True)
        dmo = (dx2 * gate).astype(BF16)
        dmerged = _dot_nt(dmo, wo[...])
        mg_ref[...] = merged
        dmo_ref[...] = dmo
        dy_rnn = (dmerged * sr).astype(BF16)
        dy_attn = (dmerged * sa).astype(BF16)
        dg_r = dmerged * y_rnn * sr * (1.0 - sr)
        dg_a = dmerged * y_attn * sa * (1.0 - sa)
        dgr_ref[...] = dg_r.astype(BF16)
        dga_ref[...] = dg_a.astype(BF16)
        gbg_ref[:, :D_MODEL] += jnp.sum(dg_r, axis=0, keepdims=True)
        gbg_ref[:, D_MODEL:] += jnp.sum(dg_a, axis=0, keepdims=True)
        du_rnn = _dot_nt(dy_rnn, wr[...])
        du_attn = _dot_nt(dy_attn, wa[...])
        ur_ref[...] = u_rnn
        dyr_ref[...] = dy_rnn
        ua_ref[...] = u_attn
        dya_ref[...] = dy_attn
        dhr_ref[...] = du_rnn * silu_zr
        dzr_ref[...] = (du_rnn * hr_t * (sig_zr * (1.0 + zr * (1.0 - sig_zr)))).astype(BF16)
        dza_ref[...] = (du_attn * o * (sig_za * (1.0 + za * (1.0 - sig_za)))).astype(BF16)
        d_o = du_attn * silu_za
        for hh in range(N_HEADS):
            do_ref[hh] = d_o[:, hh * HEAD_DIM:(hh + 1) * HEAD_DIM]

    row = pl.BlockSpec((tm, D_MODEL), lambda i: (i, 0))
    piece = lambda slot: pl.BlockSpec((tm, D_MODEL), lambda i: (i, slot))
    hm = pl.BlockSpec((N_HEADS, tm, HEAD_DIM), lambda i: (0, i, 0))
    const = lambda cols: pl.BlockSpec((1, cols), lambda i: (0, 0))
    any_spec = pl.BlockSpec(memory_space=pl.ANY)
    act_f32 = jax.ShapeDtypeStruct((seq, D_MODEL), F32)
    act_bf16 = jax.ShapeDtypeStruct((seq, D_MODEL), BF16)
    return pl.pallas_call(
        body, name="hub",
        out_shape=[act_f32, act_f32, act_bf16, jax.ShapeDtypeStruct((N_HEADS, seq, HEAD_DIM), F32),
                   act_bf16, act_bf16, act_bf16] + [act_bf16] * 6 + [
                   jax.ShapeDtypeStruct((1, D_MODEL), F32), jax.ShapeDtypeStruct((1, 2 * D_MODEL), F32),
                   jax.ShapeDtypeStruct((1, D_MODEL), F32), jax.ShapeDtypeStruct((1, 1), F32)],
        grid=(nsteps,),
        in_specs=[row, row, row, piece(1), piece(2), piece(3), piece(4), hm,
                  const(3 * D_MODEL), const(3 * D_MODEL), const(2 * D_MODEL), const(D_MODEL),
                  any_spec, any_spec, any_spec],
        out_specs=[row, row, row, hm, row, row, row] + [row] * 6 + [
                   const(D_MODEL), const(2 * D_MODEL), const(D_MODEL), const(1)],
        scratch_shapes=[pltpu.VMEM((D_MODEL, D_MODEL), BF16)] * 3 + [pltpu.SemaphoreType.DMA],
        compiler_params=_params(("arbitrary",), VMEM_LIMIT),
    )(x, tgt, hr, pf, pf, pf, pf, o_hm, mod, b_mod, b_gate, g_final, w_out_rnn, w_out_attn, w_o)


def _pair_grads(name, lefts, rights):
    n = len(rights)
    shared = len(lefts) == 1
    seq = rights[0].shape[0]
    tk = WGRAD_ROWS
    nk = seq // tk

    def body(*refs):
        l_refs, r_refs = refs[:len(lefts)], refs[len(lefts):len(lefts) + n]
        out_ref, low_ref = refs[len(lefts) + n:]
        j, kk = pl.program_id(0), pl.program_id(1)

        @pl.when(kk == 0)
        def _():
            out_ref[...] = jnp.zeros_like(out_ref)

        for m in range(n):
            @pl.when(j == m)
            def _(m=m):
                out_ref[...] += _dot_tn(l_refs[0 if shared else m][...], r_refs[m][...])

        @pl.when(kk == nk - 1)
        def _():
            low_ref[...] = out_ref[...].astype(BF16)

    def spec(m):
        return pl.BlockSpec((tk, D_MODEL), lambda j, kk: (jnp.where(j == m, kk, jnp.where(j < m, 0, nk - 1)), 0))

    left_specs = [pl.BlockSpec((tk, D_MODEL), lambda j, kk: (kk, 0))] if shared else [spec(m) for m in range(n)]
    out_spec = pl.BlockSpec((None, D_MODEL, D_MODEL), lambda j, kk: (j, 0, 0))
    return pl.pallas_call(
        body, name=name,
        out_shape=[jax.ShapeDtypeStruct((n, D_MODEL, D_MODEL), F32), jax.ShapeDtypeStruct((n, D_MODEL, D_MODEL), BF16)],
        grid=(n, nk),
        in_specs=left_specs + [spec(m) for m in range(n)],
        out_specs=[out_spec, out_spec],
        compiler_params=_params(("arbitrary", "arbitrary"), VMEM_LIMIT),
    )(*lefts, *rights)


def _dh_dx(pieces, w_in_all, x, dx2, mod, b_mod, g_norm):
    seq = x.shape[0]
    tm = DX_ROWS

    def body(*refs):
        p_refs = refs[:8]
        w_hbm, x_ref, dx2_ref, mod_ref, bmod_ref, g_ref = refs[8:14]
        gx_ref, dshift_ref, dscale_ref, ggn_ref, w_scr, sem = refs[14:]
        step = pl.program_id(0)

        @pl.when(step == 0)
        def _():
            cp = pltpu.make_async_copy(w_hbm, w_scr, sem)
            cp.start()
            cp.wait()
            for ref in (dshift_ref, dscale_ref, ggn_ref):
                ref[...] = jnp.zeros_like(ref)

        dh = _dot_nt(p_refs[0][...], w_scr[0])
        for j in range(1, 8):
            dh = dh + _dot_nt(p_refs[j][...], w_scr[j])
        scale1 = 1.0 + mod_ref[:, D_MODEL:2 * D_MODEL] + bmod_ref[:, D_MODEL:2 * D_MODEL]
        g = g_ref[...]
        xf = x_ref[...]
        rstd_t = lax.rsqrt(jnp.mean(xf * xf, axis=-1, keepdims=True) + NORM_EPS)
        xn = xf * rstd_t
        dshift_ref[...] += jnp.sum(dh, axis=0, keepdims=True)
        dscale_ref[...] += jnp.sum(dh * (xn * g), axis=0, keepdims=True)
        ggn_ref[...] += jnp.sum(dh * scale1 * xn, axis=0, keepdims=True)
        dxn = dh * (g * scale1)
        gx_ref[...] = rstd_t * (dxn - xn * jnp.mean(dxn * xn, axis=-1, keepdims=True)) + dx2_ref[...]

    row = pl.BlockSpec((tm, D_MODEL), lambda i: (i, 0))
    const = lambda cols: pl.BlockSpec((1, cols), lambda i: (0, 0))
    vec = jax.ShapeDtypeStruct((1, D_MODEL), F32)
    return pl.pallas_call(
        body, name="dh_dx",
        out_shape=[jax.ShapeDtypeStruct((seq, D_MODEL), F32), vec, vec, vec],
        grid=(seq // tm,),
        in_specs=[row] * 8 + [pl.BlockSpec(memory_space=pl.ANY), row, row,
                              const(3 * D_MODEL), const(3 * D_MODEL), const(D_MODEL)],
        out_specs=[row, const(D_MODEL), const(D_MODEL), const(D_MODEL)],
        scratch_shapes=[pltpu.VMEM((8, D_MODEL, D_MODEL), BF16), pltpu.SemaphoreType.DMA],
        compiler_params=_params(("arbitrary",), VMEM_LIMIT),
    )(*pieces, w_in_all, x, dx2, mod, b_mod, g_norm)


def _adamw(name, w, g, m, v, recv=None):
    rows, cols = w.shape
    tr = rows if rows <= 256 else 256

    def body(*refs):
        w_ref, g_ref, m_ref, v_ref = refs[:4]
        d_ref, nm_ref, nv_ref = refs[-3:] if recv is None else refs[5:8]
        gv = g_ref[...]
        if recv is not None:
            r_ref, g_out = refs[4], refs[8]
            gv = ((gv + r_ref[0].astype(F32)) + r_ref[1].astype(F32)) + r_ref[2].astype(F32)
            g_out[...] = gv
        nm = ADAM_B1 * m_ref[...] + (1.0 - ADAM_B1) * gv
        nv = ADAM_B2 * v_ref[...] + (1.0 - ADAM_B2) * (gv * gv)
        m_hat = nm / (1.0 - ADAM_B1 ** ADAM_STEP)
        v_hat = nv / (1.0 - ADAM_B2 ** ADAM_STEP)
        d_ref[...] = -ADAM_LR * (m_hat / (jnp.sqrt(v_hat) + ADAM_EPS) + ADAM_WD * w_ref[...])
        nm_ref[...] = nm
        nv_ref[...] = nv

    spec = pl.BlockSpec((tr, cols), lambda i: (i, 0))
    shape = jax.ShapeDtypeStruct((rows, cols), F32)
    if recv is None:
        return pl.pallas_call(
            body, name=name, out_shape=[shape, shape, shape], grid=(rows // tr,),
            in_specs=[spec] * 4, out_specs=[spec] * 3,
            compiler_params=_params(("arbitrary",)),
        )(w, g, m, v)
    return pl.pallas_call(
        body, name=name, out_shape=[shape] * 4, grid=(rows // tr,),
        in_specs=[spec] * 4 + [pl.BlockSpec((3, tr, cols), lambda i: (0, i, 0))], out_specs=[spec] * 4,
        compiler_params=_params(("arbitrary",)),
    )(w, g, m, v, recv)


def kernel(x, c, positions, g_norm, w_mod, b_mod, w_in, b_gate, conv_w, conv_b, w_a, b_a, w_x, b_x, lam, w_out_rnn, w_out_attn, w_o, g_final, loss_target, m_g_norm, m_w_mod, m_b_mod, m_w_in, m_b_gate, m_conv_w, m_conv_b, m_w_a, m_b_a, m_w_x, m_b_x, m_lam, m_w_out_rnn, m_w_out_attn, m_w_o, m_g_final, v_g_norm, v_w_mod, v_b_mod, v_w_in, v_b_gate, v_conv_w, v_conv_b, v_w_a, v_b_a, v_w_x, v_b_x, v_lam, v_w_out_rnn, v_w_out_attn, v_w_o, v_g_final):
    seq = x.shape[1]
    me = _index(_my_pos())
    xs, tgt = x[0], loss_target[0]

    pos = positions[0].astype(F32)[:, None]
    inv_freq = ROPE_THETA ** (-jnp.arange(0, 2 * ROT_HALF, 2, dtype=F32) / (2 * ROT_HALF))
    ang = pos * inv_freq
    rest = HEAD_DIM - 2 * ROT_HALF
    cosf = jnp.concatenate([jnp.cos(ang), jnp.cos(ang), jnp.ones((seq, rest), F32)], axis=1)
    sinf = jnp.concatenate([-jnp.sin(ang), jnp.sin(ang), jnp.zeros((seq, rest), F32)], axis=1)
    keep = (positions[0] != 0).astype(F32)[:, None]

    (w_in_all,) = _ag_big("gather_weights", [w_in[0].astype(BF16)])
    both = _ag_small("gather_c_conv_w", jnp.concatenate(
        [jnp.broadcast_to(c, (SUBLANES, D_MODEL)), jnp.pad(conv_w[0], ((0, SUBLANES - 4), (0, 0)))], axis=1))
    c_all, conv_w8 = both[:, 0, :D_MODEL], both[:, :, D_MODEL:]
    mod_cols = w_mod.shape[2]
    mod_part = _ag_small("gather_mod", _mod_fwd(c_all, w_mod[0]))
    mod = lax.dynamic_index_in_dim(mod_part, me, axis=1, keepdims=False).reshape(1, N_DEV * mod_cols)
    mod, late = lax.optimization_barrier(
        (mod, [w_out_rnn[0].astype(BF16), w_out_attn[0].astype(BF16), w_o[0].astype(BF16)]))
    late_sends, late_recvs, late_shards, late_lands, late_token = _gather_start(late, me)
    mod = mod + late_token[0:1, 0:1]

    blocks = lambda t: t.reshape(RNN_BLOCKS, 1, 128)
    rnn_params = (conv_w8, blocks(conv_b), w_a[0], blocks(b_a), w_x[0], blocks(b_x), blocks(lam))

    h = _norm(xs, mod, b_mod, g_norm)
    pf, q, k, v = _proj(h, w_in_all, cosf, sinf)
    hr = _rnn_fwd(pf, keep, *rnn_params)
    o, lses = _attn_fwd(q, k, v)

    w_or_all, w_oa_all, w_o_all = (t.reshape(D_MODEL, D_MODEL) for t in _gather_wait(
        late_sends, late_recvs, late_shards, late_lands, o))
    (dx2, dhr, dz_rnn, d_o, dz_attn, dg_r, dg_a, u_rnn, dy_rnn, u_attn, dy_attn, merged, dmo,
     gp_g_final, gp_b_gate, dgate, loss_part) = _hub(
        xs, tgt, hr, pf, o, mod, b_mod, b_gate, g_final.reshape(1, D_MODEL), w_or_all, w_oa_all, w_o_all)
    gp_out, gp_out_low = _pair_grads("out_grads", [u_rnn, u_attn, merged], [dy_rnn, dy_attn, dmo])
    dq, dk, dv = _attn_bwd(q, k, v, d_o, o, lses, cosf, sinf)
    dx_rnn, gp_conv_w, gp_conv_b, gp_w_a, gp_b_a, gp_w_x, gp_b_x, gp_lam = _rnn_bwd(pf, hr, dhr, keep, *rnn_params)
    pieces = [dx_rnn, dz_rnn, dq, dk, dv, dz_attn, dg_r, dg_a]
    gp_w_in, gp_w_in_low = _pair_grads("w_in_grad", [h], pieces)

    by_target = lambda t: [t[i].reshape(N_DEV, 128, D_MODEL) for i in range(3)]
    stacks = [gp_w_in] + by_target(gp_out)
    from_sib = _rs_to_sibling("rs_sibling", [gp_w_in_low] + by_target(gp_out_low))
    targets = jnp.bitwise_xor(me, 2 * jnp.arange(4, dtype=jnp.int32)).astype(jnp.int32)
    sums = [_add_sibling("rs_add_sibling_%d" % a, s_, r_, targets) for a, (s_, r_) in enumerate(zip(stacks, from_sib))]
    send_sems, recv_sems, sent, landing, token = _rs_chips_start([send for _, send in sums])

    mod_after = mod + token[0:1, 0:1]
    grad_x, dshift, dscale, gp_g_norm = _dh_dx(pieces, w_in_all, xs, dx2, mod_after, b_mod, g_norm)

    dmod = jnp.concatenate([dshift, dscale, dgate], axis=1)
    dmod_all = _ag_small("gather_dmod", jnp.broadcast_to(dmod, (SUBLANES, 3 * D_MODEL)))[:, 0, :]
    dmod_cols = lax.dynamic_slice_in_dim(dmod_all, me * mod_cols, mod_cols, axis=1)
    g_b_mod, g_w_mod = _mod_bwd(c_all, dmod_all, dmod_cols)

    flat = lambda t: t.reshape(-1, 128)
    small = [flat(gp_g_norm), flat(gp_b_gate), flat(gp_conv_b), flat(gp_b_a), flat(gp_b_x), flat(gp_lam),
             flat(gp_g_final), flat(gp_conv_w), jnp.broadcast_to(loss_part, (SUBLANES, 128)),
             flat(gp_w_a), flat(gp_w_x)]
    sizes = [t.shape[0] for t in small]
    small.append(jnp.zeros((-sum(sizes) % (2 * SUBLANES), 128), F32))
    total = _allreduce_small("allreduce_small_grads", jnp.concatenate(small, axis=0))
    offs = [sum(sizes[:i]) for i in range(len(sizes))]
    (g_g_norm, g_b_gate, g_conv_b, g_b_a, g_b_x, g_lam, g_g_final, g_conv_w_all, loss_rows, g_w_a, g_w_x) = (
        total[o_:o_ + s_] for o_, s_ in zip(offs, sizes))
    loss = loss_rows[0, 0]
    g_conv_w = lax.dynamic_index_in_dim(g_conv_w_all.reshape(RNN_BLOCKS, SUBLANES, 128), me, axis=0,
                                        keepdims=False)[:4]

    from_chips = _rs_chips_wait(send_sems, recv_sems, sent, landing, total)

    results = {}
    sharded = (("w_in", w_in, m_w_in, v_w_in, (D_MODEL, D_MODEL)),
               ("w_out_rnn", w_out_rnn, m_w_out_rnn, v_w_out_rnn, (128, D_MODEL)),
               ("w_out_attn", w_out_attn, m_w_out_attn, v_w_out_attn, (128, D_MODEL)),
               ("w_o", w_o, m_w_o, v_w_o, (128, D_MODEL)))
    for (name, w_, m_, v_, shape2), (own, _), arrived in zip(sharded, sums, from_chips):
        d_, nm_, nv_, g_ = _adamw("adamw_" + name, w_.reshape(shape2), own, m_.reshape(shape2), v_.reshape(shape2),
                                  arrived)
        results[name] = (g_, d_, nm_, nv_)
    shape2 = (D_MODEL, mod_cols)
    results["w_mod"] = (g_w_mod,) + tuple(_adamw("adamw_w_mod", w_mod.reshape(shape2), g_w_mod,
                                                 m_w_mod.reshape(shape2), v_w_mod.reshape(shape2)))
    lanes = (("g_norm", g_norm, g_g_norm, m_g_norm, v_g_norm), ("b_mod", b_mod, g_b_mod, m_b_mod, v_b_mod),
             ("b_gate", b_gate, g_b_gate, m_b_gate, v_b_gate), ("conv_w", conv_w, g_conv_w, m_conv_w, v_conv_w),
             ("conv_b", conv_b, g_conv_b, m_conv_b, v_conv_b), ("w_a", w_a, g_w_a, m_w_a, v_w_a),
             ("b_a", b_a, g_b_a, m_b_a, v_b_a), ("w_x", w_x, g_w_x, m_w_x, v_w_x), ("b_x", b_x, g_b_x, m_b_x, v_b_x),
             ("lam", lam, g_lam, m_lam, v_lam), ("g_final", g_final, g_g_final, m_g_final, v_g_final))
    used = [t[1].size // 128 for t in lanes]
    held = [-(-u // SUBLANES) * SUBLANES for u in used]
    tail = -sum(held) % 256

    def packed(k):
        parts = [jnp.pad(t[k].reshape(-1, 128), ((0, h_ - u_), (0, 0))) for t, u_, h_ in zip(lanes, used, held)]
        return jnp.concatenate(parts + [jnp.zeros((tail, 128), F32)], axis=0)

    moved = _adamw("adamw_lanes", packed(1), packed(2), packed(3), packed(4))
    start = 0
    for (name, w_, g_, _, _), u_, h_ in zip(lanes, used, held):
        results[name] = (g_,) + tuple(t[start:start + u_] for t in moved)
        start += h_
    order = ("g_norm", "w_mod", "b_mod", "w_in", "b_gate", "conv_w", "conv_b", "w_a", "b_a", "w_x", "b_x", "lam",
             "w_out_rnn", "w_out_attn", "w_o", "g_final")
    given = dict(g_norm=g_norm, w_mod=w_mod, b_mod=b_mod, w_in=w_in, b_gate=b_gate, conv_w=conv_w, conv_b=conv_b,
                 w_a=w_a, b_a=b_a, w_x=w_x, b_x=b_x, lam=lam, w_out_rnn=w_out_rnn, w_out_attn=w_out_attn, w_o=w_o,
                 g_final=g_final)
    outs = [[results[name][k].reshape(given[name].shape) for name in order] for k in range(4)]
    return (loss, grad_x[None], *outs[0], *outs[1], *outs[2], *outs[3])
```

```python
import jax
import jax.numpy as jnp
from jax import lax
from jax.experimental import pallas as pl
from jax.experimental.pallas import tpu as pltpu

F32 = jnp.float32
BF16 = jnp.bfloat16
MESH = pl.DeviceIdType.MESH

D_MODEL = 1024
N_HEADS = 8
HEAD_DIM = 128
RNN_BLOCKS = 8
N_DEV = 8
ROT_HALF = 16
ROPE_THETA = 500000.0
DILATIONS = (1, 4, 16)
KEY_BLOCK = 128
SPAN = KEY_BLOCK * DILATIONS[-1]
ATTN_SCALE = HEAD_DIM ** -0.5
NORM_EPS = 1e-6
LRU_C = 8.0
NEG_INF = -1e30
ADAM_LR, ADAM_B1, ADAM_B2, ADAM_EPS, ADAM_WD, ADAM_STEP = 0.001, 0.9, 0.999, 1e-08, 0.01, 10

SUBLANES = 8
VMEM_LIMIT = 56 * 1024 * 1024
PROJ_ROWS = 1024
RNN_ROWS = 2048
HUB_ROWS = 256
DX_ROWS = 512
WGRAD_ROWS = 1024
ADD_ROWS = 256


def _params(sem=None, vmem=None):
    return pltpu.CompilerParams(dimension_semantics=sem, vmem_limit_bytes=vmem)


def _dot(a, b):
    return jnp.dot(a, b, preferred_element_type=F32)


def _dot_nt(a, b):
    return lax.dot_general(a, b, (((1,), (1,)), ((), ())), preferred_element_type=F32)


def _dot_tn(a, b):
    return lax.dot_general(a, b, (((0,), (0,)), ((), ())), preferred_element_type=F32)


def _sigmoid(z):
    return 1.0 / (1.0 + jnp.exp(-z))


def _expm1_nonpos(z, exp_z):
    return jnp.where(z > -0.01, z * (1.0 + 0.5 * z), exp_z - 1.0)


def _my_pos():
    return lax.axis_index("x"), lax.axis_index("y"), lax.axis_index("c")


def _flip(pos, k):
    x, y, c = pos
    return ((1 - x) if k & 4 else x, (1 - y) if k & 2 else y, (1 - c) if k & 1 else c)


def _index(pos):
    return 4 * pos[0] + 2 * pos[1] + pos[2]


def _ag_small(name, v):
    rows, cols = v.shape

    def body(v_ref, out_ref, send_sems, recv_sems):
        me = _my_pos()
        out_ref[_index(me)] = v_ref[...]
        sends = []
        for k in range(1, N_DEV):
            cp = pltpu.make_async_remote_copy(
                src_ref=v_ref, dst_ref=out_ref.at[_index(me)], send_sem=send_sems.at[k - 1],
                recv_sem=recv_sems.at[k - 1], device_id=_flip(me, k), device_id_type=MESH)
            cp.start()
            sends.append(cp)
        for k in range(1, N_DEV):
            peer = _flip(me, k)
            pltpu.make_async_remote_copy(
                src_ref=v_ref, dst_ref=out_ref.at[_index(peer)], send_sem=send_sems.at[k - 1],
                recv_sem=recv_sems.at[k - 1], device_id=peer, device_id_type=MESH).wait_recv()
        for cp in sends:
            cp.wait_send()

    return pl.pallas_call(
        body, name=name,
        out_shape=jax.ShapeDtypeStruct((N_DEV, rows, cols), v.dtype),
        in_specs=[pl.BlockSpec(memory_space=pltpu.VMEM)],
        out_specs=pl.BlockSpec(memory_space=pltpu.VMEM),
        scratch_shapes=[pltpu.SemaphoreType.DMA((N_DEV - 1,)), pltpu.SemaphoreType.DMA((N_DEV - 1,))],
        compiler_params=_params(None, VMEM_LIMIT),
    )(v)


def _ag_big(name, shards):
    n = len(shards)

    def body(*refs):
        ins, outs = refs[:n], refs[n:2 * n]
        send_sems, recv_sems, local_sems = refs[2 * n:]
        me = _my_pos()
        sib = _flip(me, 1)
        chips = [2, 4, 6]

        def copy(a, k, block, to, src=None):
            rows = outs[a].at[_index(block)]
            return pltpu.make_async_remote_copy(
                src_ref=rows if src is None else src, dst_ref=rows,
                send_sem=send_sems.at[a * 7 + k], recv_sem=recv_sems.at[a * 7 + k],
                device_id=to, device_id_type=MESH)

        started = []
        for a in range(n):
            mine = pltpu.make_async_copy(ins[a], outs[a].at[_index(me)], local_sems.at[a])
            mine.start()
            started.append(mine)
        sends = []
        for a in range(n):
            first = [copy(a, 0, me, sib, src=ins[a])]
            first += [copy(a, 1 + j, me, _flip(me, ch), src=ins[a]) for j, ch in enumerate(chips)]
            for cp in first:
                cp.start()
            sends += first
        for j, ch in enumerate(chips):
            for a in range(n):
                copy(a, 1 + j, _flip(me, ch), me).wait_recv()
                fwd = copy(a, 4 + j, _flip(me, ch), sib)
                fwd.start()
                sends.append(fwd)
        for a in range(n):
            copy(a, 0, sib, me).wait_recv()
            for j, ch in enumerate(chips):
                copy(a, 4 + j, _flip(sib, ch), me).wait_recv()
        for cp in sends:
            cp.wait_send()
        for mine in started:
            mine.wait()

    any_spec = pl.BlockSpec(memory_space=pl.ANY)
    return pl.pallas_call(
        body, name=name,
        out_shape=[jax.ShapeDtypeStruct((N_DEV,) + s.shape, s.dtype) for s in shards],
        in_specs=[any_spec] * n, out_specs=[any_spec] * n,
        scratch_shapes=[pltpu.SemaphoreType.DMA((7 * n,)), pltpu.SemaphoreType.DMA((7 * n,)),
                        pltpu.SemaphoreType.DMA((n,))],
    )(*shards)


def _peer_copies(shards, lands, send_sems, recv_sems):
    me = _my_pos()
    return [pltpu.make_async_remote_copy(
        src_ref=shards[a], dst_ref=lands[a].at[_index(me)],
        send_sem=send_sems.at[a * 7 + k - 1], recv_sem=recv_sems.at[a * 7 + k - 1],
        device_id=_flip(me, k), device_id_type=MESH) for a in range(len(shards)) for k in range(1, N_DEV)]


def _gather_start(shards, me):
    n = len(shards)

    def body(*refs):
        srcs, lands = refs[:n], refs[n:2 * n]
        send_sems, recv_sems = refs[2 * n:2 * n + 2]
        for cp in _peer_copies(srcs, lands, send_sems, recv_sems):
            cp.start()
        refs[-1][...] = jnp.zeros_like(refs[-1])

    hbm = pl.BlockSpec(memory_space=pltpu.HBM)
    sem = pl.BlockSpec(memory_space=pltpu.SEMAPHORE)
    held = [pltpu.HBM(s.shape, s.dtype) for s in shards]
    landing = [lax.dynamic_update_slice(jnp.zeros((N_DEV,) + s.shape, s.dtype), s[None], (me, 0, 0)) for s in shards]
    held_land = [pltpu.HBM(t.shape, t.dtype) for t in landing]
    outs = pl.pallas_call(
        body, name="gather_out_weights_start",
        out_shape=(pltpu.SemaphoreType.DMA((7 * n,)), pltpu.SemaphoreType.DMA((7 * n,)), *held, *held_land,
                   jax.ShapeDtypeStruct((SUBLANES, 128), F32)),
        in_specs=[hbm] * (2 * n),
        out_specs=(sem, sem, *[hbm] * (2 * n), pl.BlockSpec(memory_space=pltpu.VMEM)),
        input_output_aliases={i: 2 + i for i in range(2 * n)},
        compiler_params=pltpu.CompilerParams(has_side_effects=pltpu.SideEffectType.DATAFLOW_SIDE_EFFECTING),
    )(*[pltpu.with_memory_space_constraint(s, pltpu.HBM) for s in shards],
      *[pltpu.with_memory_space_constraint(t, pltpu.HBM) for t in landing])
    return outs[0], outs[1], outs[2:2 + n], outs[2 + n:2 + 2 * n], outs[-1]


def _gather_wait(send_sems, recv_sems, shards, lands, after):
    n = len(shards)

    def body(*refs):
        srcs, land_refs = refs[:n], refs[n:2 * n]
        sends, recvs = refs[2 * n:2 * n + 2]
        for cp in _peer_copies(srcs, land_refs, sends, recvs):
            cp.wait_send()
            cp.wait_recv()

    hbm = pl.BlockSpec(memory_space=pltpu.HBM)
    sem = pl.BlockSpec(memory_space=pltpu.SEMAPHORE)
    outs = pl.pallas_call(
        body, name="gather_out_weights_wait",
        out_shape=(*[pltpu.HBM(s.shape, s.dtype) for s in shards], *[pltpu.HBM(t.shape, t.dtype) for t in lands]),
        in_specs=[hbm] * (2 * n) + [sem, sem, pl.BlockSpec(memory_space=pl.ANY)],
        out_specs=[hbm] * (2 * n),
        input_output_aliases={i: i for i in range(2 * n)},
        compiler_params=pltpu.CompilerParams(has_side_effects=pltpu.SideEffectType.DATAFLOW_SIDE_EFFECTING),
    )(*shards, *lands, send_sems, recv_sems, after)
    return outs[n:]


def _rs_to_sibling(name, stacks):
    n = len(stacks)

    def body(*refs):
        ins, outs = refs[:n], refs[n:2 * n]
        send_sems, recv_sems = refs[2 * n:]
        me = _my_pos()
        sib = _flip(me, 1)
        sends = []
        for a in range(n):
            for m in range(4):
                target = _flip(sib, 2 * m)
                cp = pltpu.make_async_remote_copy(
                    src_ref=ins[a].at[_index(target)], dst_ref=outs[a].at[m],
                    send_sem=send_sems.at[a * 4 + m], recv_sem=recv_sems.at[a * 4 + m],
                    device_id=sib, device_id_type=MESH)
                cp.start()
                sends.append(cp)
        for cp in sends:
            cp.wait_recv()
        for cp in sends:
            cp.wait_send()

    any_spec = pl.BlockSpec(memory_space=pl.ANY)
    return pl.pallas_call(
        body, name=name,
        out_shape=[jax.ShapeDtypeStruct((4,) + s.shape[1:], s.dtype) for s in stacks],
        in_specs=[any_spec] * n, out_specs=[any_spec] * n,
        scratch_shapes=[pltpu.SemaphoreType.DMA((4 * n,)), pltpu.SemaphoreType.DMA((4 * n,))],
    )(*stacks)


def _chip_copies(srcs, lands, send_sems, recv_sems):
    me = _my_pos()
    return [pltpu.make_async_remote_copy(
        src_ref=srcs[a].at[m - 1], dst_ref=lands[a].at[m - 1],
        send_sem=send_sems.at[a * 3 + m - 1], recv_sem=recv_sems.at[a * 3 + m - 1],
        device_id=_flip(me, 2 * m), device_id_type=MESH) for a in range(len(srcs)) for m in range(1, 4)]


def _rs_chips_start(sums):
    n = len(sums)

    def body(*refs):
        srcs, lands = refs[:n], refs[n:2 * n]
        send_sems, recv_sems = refs[2 * n:2 * n + 2]
        token = refs[-1]
        for cp in _chip_copies(srcs, lands, send_sems, recv_sems):
            cp.start()
        token[...] = jnp.zeros_like(token)

    hbm = pl.BlockSpec(memory_space=pltpu.HBM)
    sem = pl.BlockSpec(memory_space=pltpu.SEMAPHORE)
    held = [pltpu.HBM(s.shape, s.dtype) for s in sums]
    outs = pl.pallas_call(
        body, name="rs_chips_start",
        out_shape=(pltpu.SemaphoreType.DMA((3 * n,)), pltpu.SemaphoreType.DMA((3 * n,)), *held, *held,
                   jax.ShapeDtypeStruct((SUBLANES, 128), F32)),
        in_specs=[hbm] * (2 * n),
        out_specs=(sem, sem, *[hbm] * (2 * n), pl.BlockSpec(memory_space=pltpu.VMEM)),
        input_output_aliases={i: 2 + i for i in range(2 * n)},
        compiler_params=pltpu.CompilerParams(has_side_effects=pltpu.SideEffectType.DATAFLOW_SIDE_EFFECTING),
    )(*[pltpu.with_memory_space_constraint(s, pltpu.HBM) for s in sums],
      *[pltpu.with_memory_space_constraint(lax.empty(s.shape, s.dtype), pltpu.HBM) for s in sums])
    return outs[0], outs[1], outs[2:2 + n], outs[2 + n:2 + 2 * n], outs[-1]


def _rs_chips_wait(send_sems, recv_sems, srcs, lands, after):
    n = len(srcs)

    def body(*refs):
        src_refs, land_refs = refs[:n], refs[n:2 * n]
        sends, recvs = refs[2 * n:2 * n + 2]
        for cp in _chip_copies(src_refs, land_refs, sends, recvs):
            cp.wait_send()
            cp.wait_recv()

    hbm = pl.BlockSpec(memory_space=pltpu.HBM)
    sem = pl.BlockSpec(memory_space=pltpu.SEMAPHORE)
    held = [pltpu.HBM(s.shape, s.dtype) for s in srcs]
    outs = pl.pallas_call(
        body, name="rs_chips_wait", out_shape=(*held, *held),
        in_specs=[hbm] * (2 * n) + [sem, sem, pl.BlockSpec(memory_space=pl.ANY)],
        out_specs=[hbm] * (2 * n),
        input_output_aliases={i: i for i in range(2 * n)},
        compiler_params=pltpu.CompilerParams(has_side_effects=pltpu.SideEffectType.DATAFLOW_SIDE_EFFECTING),
    )(*srcs, *lands, send_sems, recv_sems, after)
    return outs[n:]


def _add_sibling(name, stack, recv, targets):
    _, rows, cols = stack.shape
    tr = min(rows, ADD_ROWS)

    def own_body(t_ref, a_ref, b_ref, o_ref):
        o_ref[...] = a_ref[...] + b_ref[...].astype(F32)

    own = pl.pallas_call(
        own_body, name=name + "_own",
        out_shape=jax.ShapeDtypeStruct((rows, cols), F32),
        grid_spec=pltpu.PrefetchScalarGridSpec(
            num_scalar_prefetch=1, grid=(rows // tr,),
            in_specs=[pl.BlockSpec((None, tr, cols), lambda i, t: (t[0], i, 0)),
                      pl.BlockSpec((None, tr, cols), lambda i, t: (0, i, 0))],
            out_specs=pl.BlockSpec((tr, cols), lambda i, t: (i, 0))),
        compiler_params=_params(("arbitrary",)),
    )(targets, stack, recv)

    def send_body(t_ref, a_ref, b_ref, o_ref):
        o_ref[...] = (a_ref[...] + b_ref[...].astype(F32)).astype(BF16)

    send = pl.pallas_call(
        send_body, name=name + "_send",
        out_shape=jax.ShapeDtypeStruct((3, rows, cols), BF16),
        grid_spec=pltpu.PrefetchScalarGridSpec(
            num_scalar_prefetch=1, grid=(3, rows // tr),
            in_specs=[pl.BlockSpec((None, tr, cols), lambda m, i, t: (t[m + 1], i, 0)),
                      pl.BlockSpec((None, tr, cols), lambda m, i, t: (m + 1, i, 0))],
            out_specs=pl.BlockSpec((None, tr, cols), lambda m, i, t: (m, i, 0))),
        compiler_params=_params(("arbitrary", "arbitrary")),
    )(targets, stack, recv)
    return own, send


def _allreduce_small(name, v):
    rows, cols = v.shape
    half = rows // 2
    assert rows % (2 * SUBLANES) == 0

    def body(v_ref, out_ref, from_sib, chip_half, from_chips, send_sems, recv_sems):
        me = _my_pos()
        sib = _flip(me, 1)
        mine = pl.ds(pl.multiple_of(me[2] * half, SUBLANES), half)
        theirs = pl.ds(pl.multiple_of((1 - me[2]) * half, SUBLANES), half)

        def copy(k, src, dst, to):
            return pltpu.make_async_remote_copy(src_ref=src, dst_ref=dst, send_sem=send_sems.at[k],
                                                recv_sem=recv_sems.at[k], device_id=to, device_id_type=MESH)

        to_sib = copy(0, v_ref.at[theirs], from_sib, sib)
        to_sib.start()
        to_sib.wait_recv()
        chip_half[...] = v_ref[mine, :] + from_sib[...]
        to_chips = [copy(m, chip_half, from_chips.at[m - 1], _flip(me, 2 * m)) for m in range(1, 4)]
        for cp in to_chips:
            cp.start()
        for cp in to_chips:
            cp.wait_recv()
        my_chip = 2 * me[0] + me[1]
        total = None
        for chip in range(4):
            slot = jnp.maximum(jnp.bitwise_xor(chip, my_chip) - 1, 0)
            part = jnp.where(chip == my_chip, chip_half[...], from_chips[slot])
            total = part if total is None else total + part
        out_ref[mine, :] = total
        swap = copy(4, out_ref.at[mine], out_ref.at[mine], sib)
        swap.start()
        copy(4, out_ref.at[theirs], out_ref.at[theirs], sib).wait_recv()
        for cp in [to_sib, swap] + to_chips:
            cp.wait_send()

    return pl.pallas_call(
        body, name=name, out_shape=jax.ShapeDtypeStruct((rows, cols), F32),
        in_specs=[pl.BlockSpec(memory_space=pltpu.VMEM)],
        out_specs=pl.BlockSpec(memory_space=pltpu.VMEM),
        scratch_shapes=[pltpu.VMEM((half, cols), F32), pltpu.VMEM((half, cols), F32),
                        pltpu.VMEM((3, half, cols), F32),
                        pltpu.SemaphoreType.DMA((5,)), pltpu.SemaphoreType.DMA((5,))],
        compiler_params=_params(None, VMEM_LIMIT),
    )(v)


def _mod_fwd(c_all, w_mod):
    def body(c_ref, w_ref, o_ref):
        c = c_ref[...]
        o_ref[...] = jnp.dot(c * _sigmoid(c), w_ref[...], preferred_element_type=F32,
                             precision=lax.Precision.HIGHEST)

    return pl.pallas_call(
        body, name="mod_fwd", out_shape=jax.ShapeDtypeStruct((N_DEV, w_mod.shape[1]), F32),
    )(c_all, w_mod)


def _mod_bwd(c_all, dmod_all, dmod_cols):
    def body(c_ref, da_ref, dc_ref, gb_ref, gw_ref):
        c = c_ref[...]
        acc = da_ref[0:1, :]
        for b in range(1, N_DEV):
            acc = acc + da_ref[b:b + 1, :]
        gb_ref[...] = acc
        gw_ref[...] = lax.dot_general(c * _sigmoid(c), dc_ref[...], (((0,), (0,)), ((), ())),
                                      preferred_element_type=F32, precision=lax.Precision.HIGHEST)

    return pl.pallas_call(
        body, name="mod_bwd",
        out_shape=[jax.ShapeDtypeStruct((1, dmod_all.shape[1]), F32),
                   jax.ShapeDtypeStruct((c_all.shape[1], dmod_cols.shape[1]), F32)],
    )(c_all, dmod_all, dmod_cols)


def _rope_partner(t):
    lane = lax.broadcasted_iota(jnp.int32, t.shape, 1)
    return jnp.where(lane < ROT_HALF, pltpu.roll(t, HEAD_DIM - ROT_HALF, 1), pltpu.roll(t, ROT_HALF, 1))


def _norm(x, mod, b_mod, g_norm):
    seq = x.shape[0]
    tm = PROJ_ROWS

    def body(x_ref, mod_ref, bmod_ref, g_ref, h_ref):
        xf = x_ref[...]
        rstd = lax.rsqrt(jnp.mean(xf * xf, axis=-1, keepdims=True) + NORM_EPS)
        shift = mod_ref[:, 0:D_MODEL] + bmod_ref[:, 0:D_MODEL]
        scale = mod_ref[:, D_MODEL:2 * D_MODEL] + bmod_ref[:, D_MODEL:2 * D_MODEL]
        h_ref[...] = (((xf * rstd) * g_ref[...]) * (1.0 + scale) + shift).astype(BF16)

    row = pl.BlockSpec((tm, D_MODEL), lambda i: (i, 0))
    const = lambda cols: pl.BlockSpec((1, cols), lambda i: (0, 0))
    return pl.pallas_call(
        body, name="norm", out_shape=jax.ShapeDtypeStruct((seq, D_MODEL), BF16), grid=(seq // tm,),
        in_specs=[row, const(3 * D_MODEL), const(3 * D_MODEL), const(D_MODEL)], out_specs=row,
        compiler_params=_params(("arbitrary",), VMEM_LIMIT),
    )(x, mod, b_mod, g_norm)


def _proj(h, w_in_all, cosf, sinf):
    seq = h.shape[0]
    tm = PROJ_ROWS
    last = seq // tm - 1

    def body(h_ref, w_ref, cos_ref, sin_ref, pf_ref, q_ref, k_ref, v_ref):
        j = pl.program_id(0)

        @pl.when((j < 2) | (j > 4))
        def _():
            pf_ref[...] = _dot(h_ref[...], w_ref[...])

        def heads(dst_ref, rotate, gain):
            for pair in range(N_HEADS // 2):
                both = _dot(h_ref[...], w_ref[:, 2 * pair * HEAD_DIM:2 * (pair + 1) * HEAD_DIM])
                for hh in (2 * pair, 2 * pair + 1):
                    t = both[:, (hh % 2) * HEAD_DIM:(hh % 2 + 1) * HEAD_DIM]
                    if rotate:
                        t = t * cos_ref[...] + _rope_partner(t) * sin_ref[...]
                    dst_ref[hh] = t if gain is None else t * gain

        @pl.when(j == 2)
        def _():
            heads(q_ref, True, ATTN_SCALE)

        @pl.when(j == 3)
        def _():
            heads(k_ref, True, None)

        @pl.when(j == 4)
        def _():
            heads(v_ref, False, None)

    def pf_block(j, i):
        f32_piece = (j < 2) | (j > 4)
        return (jnp.where(f32_piece, i, last), jnp.where(j < 2, j, jnp.where(j < 5, 1, j - 3)))

    def hm_block(piece):
        return lambda j, i: (0, jnp.where(j == piece, i, jnp.where(j < piece, 0, last)), 0)

    hm = jax.ShapeDtypeStruct((N_HEADS, seq, HEAD_DIM), F32)
    hm_spec = lambda piece: pl.BlockSpec((N_HEADS, tm, HEAD_DIM), hm_block(piece))
    row = lambda j, i: (i, 0)
    return pl.pallas_call(
        body, name="proj",
        out_shape=[jax.ShapeDtypeStruct((seq, 5 * D_MODEL), F32), hm, hm, hm],
        grid=(8, seq // tm),
        in_specs=[pl.BlockSpec((tm, D_MODEL), row),
                  pl.BlockSpec((None, D_MODEL, D_MODEL), lambda j, i: (j, 0, 0)),
                  pl.BlockSpec((tm, HEAD_DIM), row), pl.BlockSpec((tm, HEAD_DIM), row)],
        out_specs=[pl.BlockSpec((tm, D_MODEL), pf_block), hm_spec(2), hm_spec(3), hm_spec(4)],
        compiler_params=_params(("arbitrary", "arbitrary"), VMEM_LIMIT),
    )(h, w_in_all, cosf, sinf)


def _shift_down(v, s, head):
    rolled = pltpu.roll(v, s, 0)
    row = lax.broadcasted_iota(jnp.int32, head.shape, 0)
    first = jnp.where(row < s, pltpu.roll(head, s, 0), rolled[:SUBLANES, :])
    return jnp.concatenate([first, rolled[SUBLANES:, :]], axis=0)


def _shift_up(v, s, tail):
    rows = v.shape[0]
    rolled = pltpu.roll(v, rows - s, 0)
    row = lax.broadcasted_iota(jnp.int32, tail.shape, 0)
    last = jnp.where(row >= SUBLANES - s, pltpu.roll(tail, SUBLANES - s, 0), rolled[rows - SUBLANES:, :])
    return jnp.concatenate([rolled[:rows - SUBLANES, :], last], axis=0)


def _doubling(a, b, period, reverse):
    rows = a.shape[0]
    pos = lax.broadcasted_iota(jnp.int32, a.shape, 0) & (period - 1)
    k = 1
    while k < period:
        inside = (pos < period - k) if reverse else (pos >= k)
        shift = rows - k if reverse else k
        a_s = jnp.where(inside, pltpu.roll(a, shift, 0), 1.0)
        b_s = jnp.where(inside, pltpu.roll(b, shift, 0), 0.0)
        b = a * b_s + b
        a = a * a_s
        k *= 2
    return a, b


def _scan(a, b, boundary, reverse, a_scr, b_scr, spread):
    rows = a.shape[0]
    ntile = rows // SUBLANES
    a_scr[...], b_scr[...] = _doubling(a, b, SUBLANES, reverse)
    ends = pl.ds(0 if reverse else SUBLANES - 1, ntile, stride=SUBLANES)
    a_end, x_end = _doubling(a_scr[ends, :], b_scr[ends, :], ntile, reverse)
    x_end = x_end + a_end * boundary
    tile = lax.broadcasted_iota(jnp.int32, x_end.shape, 0)
    if reverse:
        incoming = jnp.where(tile == ntile - 1, boundary, pltpu.roll(x_end, ntile - 1, 0))
        last = x_end[0:1, :]
    else:
        incoming = jnp.where(tile == 0, boundary, pltpu.roll(x_end, 1, 0))
        last = x_end[ntile - 1:ntile, :]
    for s in range(SUBLANES):
        spread[pl.ds(s, ntile, stride=SUBLANES), :] = incoming
    return b_scr[...] + a_scr[...] * spread[...], last


def _conv_taps(xr, head):
    return [_shift_down(xr, 3, head), _shift_down(xr, 2, head), _shift_down(xr, 1, head), xr]


def _rnn_gates(xc, wa, ba, wx, bx, lam, keep):
    xcb = xc.astype(BF16)
    r = _sigmoid(_dot(xcb, wa.astype(BF16)) + ba)
    i = _sigmoid(_dot(xcb, wx.astype(BF16)) + bx)
    softplus = jnp.maximum(-lam, 0.0) + jnp.log(1.0 + jnp.exp(-jnp.abs(lam)))
    cl = -LRU_C * softplus
    log_a = cl * r
    a_raw = jnp.exp(log_a)
    mult_raw = jnp.sqrt(-_expm1_nonpos(2.0 * log_a, a_raw * a_raw))
    live = keep > 0.0
    return r, i, cl, a_raw, mult_raw, jnp.where(live, a_raw, 0.0), jnp.where(live, mult_raw, 1.0), live


def _rnn_specs(seq, rows, time_of):
    per = rows // SUBLANES
    vec = pl.BlockSpec((None, 1, 128), lambda hb, n: (hb, 0, 0))
    mat = pl.BlockSpec((None, 128, 128), lambda hb, n: (hb, 0, 0))
    return [pl.BlockSpec((rows, 128), lambda hb, n: (time_of(n), hb)),
            pl.BlockSpec((SUBLANES, 128), lambda hb, n: (jnp.maximum(time_of(n) * per - 1, 0), hb)),
            pl.BlockSpec((rows, 1), lambda hb, n: (time_of(n), 0)),
            pl.BlockSpec((None, SUBLANES, 128), lambda hb, n: (hb, 0, 0)),
            vec, mat, vec, mat, vec, vec]


def _rnn_fwd(pf, keep, conv_w8, conv_b, w_a, b_a, w_x, b_x, lam):
    seq = pf.shape[0]
    rows = RNN_ROWS

    def body(x_ref, xh_ref, keep_ref, cw_ref, cb_ref, wa_ref, ba_ref, wx_ref, bx_ref, lam_ref, hr_ref,
             carry, a_scr, b_scr, spread):
        n = pl.program_id(1)

        @pl.when(n == 0)
        def _():
            carry[...] = jnp.zeros_like(carry)

        xr = x_ref[...]
        head = jnp.where(n > 0, xh_ref[...], 0.0)
        taps = _conv_taps(xr, head)
        xc = cb_ref[...] + sum(cw_ref[k:k + 1, :] * taps[k] for k in range(4))
        _, i, _, _, _, a, mult, _ = _rnn_gates(xc, wa_ref[...], ba_ref[...], wx_ref[...], bx_ref[...],
                                               lam_ref[...], keep_ref[...])
        h, last = _scan(a, mult * i * xc, carry[0:1, :], False, a_scr, b_scr, spread)
        hr_ref[...] = h
        carry[...] = jnp.broadcast_to(last, carry.shape)

    chunk_f32 = pltpu.VMEM((rows, 128), F32)
    return pl.pallas_call(
        body, name="rnn_fwd",
        out_shape=jax.ShapeDtypeStruct((seq, D_MODEL), F32),
        grid=(RNN_BLOCKS, seq // rows),
        in_specs=_rnn_specs(seq, rows, lambda n: n),
        out_specs=pl.BlockSpec((rows, 128), lambda hb, n: (n, hb)),
        scratch_shapes=[pltpu.VMEM((SUBLANES, 128), F32), chunk_f32, chunk_f32, chunk_f32],
        compiler_params=_params(("arbitrary", "arbitrary"), VMEM_LIMIT),
    )(pf, pf, keep, conv_w8, conv_b, w_a, b_a, w_x, b_x, lam)


def _rnn_bwd(pf, hr, dhr, keep, conv_w8, conv_b, w_a, b_a, w_x, b_x, lam):
    seq = pf.shape[0]
    rows = RNN_ROWS
    nchunk = seq // rows
    per = rows // SUBLANES
    time_of = lambda n: nchunk - 1 - n

    def body(x_ref, xh_ref, keep_ref, cw_ref, cb_ref, wa_ref, ba_ref, wx_ref, bx_ref, lam_ref,
             hr_ref, hrh_ref, dhr_ref,
             dx_ref, gcw_ref, gcb_ref, gwa_ref, gba_ref, gwx_ref, gbx_ref, glam_ref,
             g_carry, dxc_tail, a_scr, b_scr, spread):
        n = pl.program_id(1)
        first_in_time = n == nchunk - 1

        @pl.when(n == 0)
        def _():
            g_carry[...] = jnp.zeros_like(g_carry)
            dxc_tail[...] = jnp.zeros_like(dxc_tail)
            for ref in (gcw_ref, gcb_ref, gwa_ref, gba_ref, gwx_ref, gbx_ref, glam_ref):
                ref[...] = jnp.zeros_like(ref)

        xr = x_ref[...]
        head = jnp.where(first_in_time, 0.0, xh_ref[...])
        taps = _conv_taps(xr, head)
        cw = cw_ref[...]
        xc = cb_ref[...] + sum(cw[k:k + 1, :] * taps[k] for k in range(4))
        wa, wx, lam = wa_ref[...], wx_ref[...], lam_ref[...]
        r, i, cl, a_raw, mult_raw, a, mult, live = _rnn_gates(xc, wa, ba_ref[...], wx, bx_ref[...], lam,
                                                               keep_ref[...])
        h_prev = _shift_down(hr_ref[...], 1, jnp.where(first_in_time, 0.0, hrh_ref[...]))

        row = lax.broadcasted_iota(jnp.int32, xr.shape, 0)
        last = row == rows - 1
        a_next = jnp.where(last, 0.0, pltpu.roll(a, rows - 1, 0))
        g, g_first = _scan(a_next, dhr_ref[...] + jnp.where(last, g_carry[0:1, :], 0.0),
                           jnp.zeros((1, 128), F32), True, a_scr, b_scr, spread)
        g_carry[...] = jnp.broadcast_to(a[0:1, :] * g_first, g_carry.shape)

        da = g * h_prev
        dmult = g * i * xc
        di = g * mult * xc
        dxc = g * mult * i
        dlog_a = jnp.where(live, da * a_raw - dmult * a_raw * a_raw / mult_raw, 0.0)
        dpa = (dlog_a * cl) * r * (1.0 - r)
        dpx = di * i * (1.0 - i)
        glam_ref[...] += jnp.sum(dlog_a * r, axis=0, keepdims=True) * (LRU_C * _sigmoid(-lam))
        xcb, dpab, dpxb = xc.astype(BF16), dpa.astype(BF16), dpx.astype(BF16)
        gwa_ref[...] += _dot_tn(xcb, dpab)
        gwx_ref[...] += _dot_tn(xcb, dpxb)
        gba_ref[...] += jnp.sum(dpa, axis=0, keepdims=True)
        gbx_ref[...] += jnp.sum(dpx, axis=0, keepdims=True)
        dxc = dxc + _dot_nt(dpab, wa.astype(BF16)) + _dot_nt(dpxb, wx.astype(BF16))

        gcb_ref[...] += jnp.sum(dxc, axis=0, keepdims=True)
        for k in range(4):
            gcw_ref[k:k + 1, :] += jnp.sum(dxc * taps[k], axis=0, keepdims=True)
        tail = dxc_tail[...]
        dx = cw[3:4, :] * dxc
        for k in range(3):
            dx = dx + cw[k:k + 1, :] * _shift_up(dxc, 3 - k, tail)
        dx_ref[...] = dx.astype(BF16)
        dxc_tail[...] = dxc[0:SUBLANES, :]

    blk = lambda hb, n: (hb, 0, 0)
    chunk = pl.BlockSpec((rows, 128), lambda hb, n: (time_of(n), hb))
    vec_out = pl.BlockSpec((None, 1, 128), blk)
    mat_out = pl.BlockSpec((None, 128, 128), blk)
    vec_shape = jax.ShapeDtypeStruct((RNN_BLOCKS, 1, 128), F32)
    mat_shape = jax.ShapeDtypeStruct((RNN_BLOCKS, 128, 128), F32)
    return pl.pallas_call(
        body, name="rnn_bwd",
        out_shape=[jax.ShapeDtypeStruct((seq, D_MODEL), BF16),
                   jax.ShapeDtypeStruct((RNN_BLOCKS, SUBLANES, 128), F32), vec_shape,
                   mat_shape, vec_shape, mat_shape, vec_shape, vec_shape],
        grid=(RNN_BLOCKS, nchunk),
        in_specs=_rnn_specs(seq, rows, time_of) + [
            chunk, pl.BlockSpec((SUBLANES, 128), lambda hb, n: (jnp.maximum(time_of(n) * per - 1, 0), hb)), chunk],
        out_specs=[chunk, pl.BlockSpec((None, SUBLANES, 128), blk), vec_out,
                   mat_out, vec_out, mat_out, vec_out, vec_out],
        scratch_shapes=[pltpu.VMEM((SUBLANES, 128), F32), pltpu.VMEM((SUBLANES, 128), F32)]
                       + [pltpu.VMEM((rows, 128), F32)] * 3,
        compiler_params=_params(("arbitrary", "arbitrary"), VMEM_LIMIT),
    )(pf, pf, keep, conv_w8, conv_b, w_a, b_a, w_x, b_x, lam, hr, hr, dhr)


def _unit_rows(dil, r, j):
    start = j * KEY_BLOCK * dil + r
    return pl.ds(start, KEY_BLOCK) if dil == 1 else pl.ds(start, KEY_BLOCK, stride=dil)


def _attn_fwd(q, k, v):
    nh, seq, _ = q.shape
    nchunk = seq // SPAN
    nblk = SPAN // KEY_BLOCK
    wide = DILATIONS[-1]

    def body(q_ref, k_ref, v_ref, kp_ref, vp_ref, o_ref, l1_ref, l4_ref, l16_ref,
             acc, m_s, l_s, q16, k16, v16, k16p, v16p, acc16, m16, l16, tmp):
        n = pl.program_id(1)
        qi = lax.broadcasted_iota(jnp.int32, (KEY_BLOCK, KEY_BLOCK), 0)
        ki = lax.broadcasted_iota(jnp.int32, (KEY_BLOCK, KEY_BLOCK), 1)
        bias_own = jnp.where(ki <= qi, 0.0, NEG_INF)
        bias_before = jnp.where(ki >= qi, 0.0, NEG_INF)
        bias_mid = jnp.concatenate([bias_before, bias_own], axis=1)
        bias_first = jnp.concatenate([jnp.where(n > 0, bias_before, NEG_INF), bias_own], axis=1)
        ones = jnp.ones((2 * KEY_BLOCK, HEAD_DIM), BF16)
        diag = qi == ki

        @pl.when(n == 0)
        def _():
            k16p[...] = jnp.zeros_like(k16p)
            v16p[...] = jnp.zeros_like(v16p)

        def unit(qf, kpb, kb, vpb, vb, bias, state, rows, first):
            acc_r, m_r, l_r = state
            kcat = jnp.concatenate([kpb, kb], axis=0)
            vaug = jnp.concatenate([jnp.concatenate([vpb, vb], axis=0), ones], axis=1)
            s = _dot_nt(qf.astype(BF16), kcat) + bias
            mx = jnp.max(s, axis=-1, keepdims=True)
            if first:
                m_new = jnp.broadcast_to(mx, (KEY_BLOCK, HEAD_DIM))
            else:
                m_old = m_r[rows, :]
                m_new = jnp.maximum(m_old, mx)
            pv = _dot(jnp.exp(s - jnp.concatenate([m_new, m_new], axis=1)).astype(BF16), vaug)
            if first:
                acc_r[rows, :] = pv[:, :HEAD_DIM]
                l_r[rows, :] = pv[:, HEAD_DIM:]
            else:
                alpha = jnp.exp(m_old - m_new)
                acc_r[rows, :] = alpha * acc_r[rows, :] + pv[:, :HEAD_DIM]
                l_r[rows, :] = alpha * l_r[rows, :] + pv[:, HEAD_DIM:]
            m_r[rows, :] = m_new

        for gi, dil in enumerate(DILATIONS[:-1]):
            nb = nblk // dil
            for r in range(dil):
                prow = _unit_rows(dil, r, nb - 1)
                kpb, vpb = kp_ref[prow, :].astype(BF16), vp_ref[prow, :].astype(BF16)
                for j in range(nb):
                    rows = _unit_rows(dil, r, j)
                    kb, vb = k_ref[rows, :].astype(BF16), v_ref[rows, :].astype(BF16)
                    unit(q_ref[rows, :], kpb, kb, vpb, vb, bias_first if j == 0 else bias_mid,
                         (acc, m_s, l_s), rows, gi == 0)
                    kpb, vpb = kb, vb

        for src, dst in ((q_ref, q16), (k_ref, k16), (v_ref, v16), (acc, acc16), (m_s, m16), (l_s, l16)):
            _to_residue_major(src, tmp, dst)
        for r in range(wide):
            rows = pl.ds(r * KEY_BLOCK, KEY_BLOCK)
            unit(q16[rows, :], k16p[rows, :].astype(BF16), k16[rows, :].astype(BF16), v16p[rows, :].astype(BF16),
                 v16[rows, :].astype(BF16), bias_first, (acc16, m16, l16), rows, False)
        k16p[...] = k16[...]
        v16p[...] = v16[...]

        den = l16[...]
        acc16[...] = acc16[...] * (1.0 / den)
        m16[...] = m16[...] + jnp.log(den)
        _from_residue_major(acc16, tmp, o_ref, False)
        _from_residue_major(m16, tmp, m_s, False)

        def lse_row(ref, rows):
            return jnp.sum(jnp.where(diag, ref[rows, :], 0.0), axis=0, keepdims=True)

        for dil, out in zip(DILATIONS[:-1], (l1_ref, l4_ref)):
            nb = nblk // dil
            for r in range(dil):
                for j in range(nb):
                    out[r * nb + j:r * nb + j + 1, :] = lse_row(m_s, _unit_rows(dil, r, j))
        for r in range(wide):
            l16_ref[r:r + 1, :] = lse_row(m16, pl.ds(r * KEY_BLOCK, KEY_BLOCK))

    blk = pl.BlockSpec((None, SPAN, HEAD_DIM), lambda h, n: (h, n, 0))
    pblk = pl.BlockSpec((None, SPAN, HEAD_DIM), lambda h, n: (h, jnp.maximum(n - 1, 0), 0))
    lblk = pl.BlockSpec((None, nblk, KEY_BLOCK), lambda h, n: (h, n, 0))
    lshape = jax.ShapeDtypeStruct((nh, seq // KEY_BLOCK, KEY_BLOCK), F32)
    o, l1, l4, l16 = pl.pallas_call(
        body, name="attn_fwd",
        out_shape=[jax.ShapeDtypeStruct((nh, seq, HEAD_DIM), F32), lshape, lshape, lshape],
        grid=(nh, nchunk), in_specs=[blk, blk, blk, pblk, pblk], out_specs=[blk, lblk, lblk, lblk],
        scratch_shapes=[pltpu.VMEM((SPAN, HEAD_DIM), F32)] * 12,
        compiler_params=_params(("arbitrary", "arbitrary"), VMEM_LIMIT),
    )(q, k, v, k, v)
    return o, (l1, l4, l16)


def _to_residue_major(src, tmp, dst):
    quarter = SPAN // 4
    for r4 in range(4):
        tmp[r4 * quarter:(r4 + 1) * quarter, :] = src[pl.ds(r4, quarter, stride=4), :]
    for r4 in range(4):
        for rp in range(4):
            r = r4 + 4 * rp
            dst[r * KEY_BLOCK:(r + 1) * KEY_BLOCK, :] = tmp[pl.ds(r4 * quarter + rp, KEY_BLOCK, stride=4), :]


def _from_residue_major(src, tmp, dst, add):
    quarter = SPAN // 4
    for r4 in range(4):
        for rp in range(4):
            r = r4 + 4 * rp
            tmp[pl.ds(r4 * quarter + rp, KEY_BLOCK, stride=4), :] = src[r * KEY_BLOCK:(r + 1) * KEY_BLOCK, :]
    for r4 in range(4):
        rows = pl.ds(r4, quarter, stride=4)
        part = tmp[r4 * quarter:(r4 + 1) * quarter, :]
        dst[rows, :] = dst[rows, :] + part if add else part


def _attn_bwd(q, k, v, do, o, lses, cosf, sinf):
    nh, seq, _ = q.shape
    nchunk = seq // SPAN
    nblk = SPAN // KEY_BLOCK
    wide = DILATIONS[-1]
    assert SPAN == wide * KEY_BLOCK

    def body(q_ref, k_ref, v_ref, do_ref, o_ref, kp_ref, vp_ref, l1_ref, l4_ref, l16_ref,
             cos_ref, sin_ref, cosp_ref, sinp_ref, dq_ref, dk_ref, dv_ref,
             dq_acc, dkc_acc, dvc_acc, dkp_acc, dvp_acc, q16, k16, v16, do16, o16, k16p, v16p,
             dq16, dkc16, dvc16, dkp16, dvp16, tmp, pt_s, ds_s, kcat_s, qb_s, dob_s):
        n = pl.program_id(1)
        ki = lax.broadcasted_iota(jnp.int32, (KEY_BLOCK, KEY_BLOCK), 0)
        qi = lax.broadcasted_iota(jnp.int32, (KEY_BLOCK, KEY_BLOCK), 1)
        bias_own = jnp.where(ki <= qi, 0.0, NEG_INF)
        bias_before = jnp.where(ki >= qi, 0.0, NEG_INF)
        bias_mid = jnp.concatenate([bias_before, bias_own], axis=0)
        bias_first = jnp.concatenate([jnp.where(n > 0, bias_before, NEG_INF), bias_own], axis=0)
        ones8 = jnp.ones((SUBLANES, HEAD_DIM), BF16)

        def row_dot(a, b):
            prod = a * b
            hi = prod.astype(BF16)
            lo = (prod - hi.astype(F32)).astype(BF16)
            return (_dot_nt(ones8, hi) + _dot_nt(ones8, lo))[0:1, :]

        def group(units, srcs, before, l_ref, accs):
            src_q, src_do, src_o, src_k, src_v = srcs
            before_k, before_v = before
            acc_q, acc_kc, acc_vc, acc_kp, acc_vp = accs
            kb = vb = None
            for u, (rows, prow, outside, lrow, _) in enumerate(units):
                dof = src_do[rows, :]
                qb, dob = src_q[rows, :].astype(BF16), dof.astype(BF16)
                kpb, vpb = (before_k[prow, :].astype(BF16), before_v[prow, :].astype(BF16)) if outside else (kb, vb)
                kb, vb = src_k[rows, :].astype(BF16), src_v[rows, :].astype(BF16)
                kcat = jnp.concatenate([kpb, kb], axis=0)
                vcat = jnp.concatenate([vpb, vb], axis=0)
                bias = bias_first if outside else bias_mid
                pt = jnp.exp(_dot_nt(kcat, qb) + bias - l_ref[lrow:lrow + 1, :])
                dst = pt * (_dot_nt(vcat, dob) - row_dot(dof, src_o[rows, :]))
                pt_s[u], ds_s[u], kcat_s[u], qb_s[u], dob_s[u] = pt.astype(BF16), dst.astype(BF16), kcat, qb, dob
            for u, (rows, _, _, _, _) in enumerate(units):
                acc_q[rows, :] += _dot_tn(ds_s[u], kcat_s[u])
            for u, (rows, prow, outside, _, nxt) in enumerate(units):
                dk = _dot(ds_s[u, KEY_BLOCK:, :], qb_s[u])
                dv = _dot(pt_s[u, KEY_BLOCK:, :], dob_s[u])
                if nxt is not None:
                    dk = dk + _dot(ds_s[nxt, :KEY_BLOCK, :], qb_s[nxt])
                    dv = dv + _dot(pt_s[nxt, :KEY_BLOCK, :], dob_s[nxt])
                acc_kc[rows, :] += dk
                acc_vc[rows, :] += dv
                if outside:
                    acc_kp[prow, :] += _dot(ds_s[u, :KEY_BLOCK, :], qb_s[u])
                    acc_vp[prow, :] += _dot(pt_s[u, :KEY_BLOCK, :], dob_s[u])

        @pl.when(n == 0)
        def _():
            for ref in (dkp_acc, dvp_acc, dkp16, dvp16, k16p, v16p):
                ref[...] = jnp.zeros_like(ref)

        @pl.when(n < nchunk)
        def _():
            for ref in (dq_acc, dkc_acc, dvc_acc, dq16, dkc16, dvc16):
                ref[...] = jnp.zeros_like(ref)
            for src, dst in ((q_ref, q16), (k_ref, k16), (v_ref, v16), (do_ref, do16), (o_ref, o16)):
                _to_residue_major(src, tmp, dst)
            natural = (q_ref, do_ref, o_ref, k_ref, v_ref)
            for dil, l_ref in zip(DILATIONS[:-1], (l1_ref, l4_ref)):
                nb = nblk // dil
                units = [(_unit_rows(dil, r, j), _unit_rows(dil, r, (j - 1) % nb), j == 0, r * nb + j,
                          r * nb + j + 1 if j + 1 < nb else None) for r in range(dil) for j in range(nb)]
                group(units, natural, (kp_ref, vp_ref), l_ref, (dq_acc, dkc_acc, dvc_acc, dkp_acc, dvp_acc))
            blocks = [pl.ds(r * KEY_BLOCK, KEY_BLOCK) for r in range(wide)]
            group([(rows, rows, True, r, None) for r, rows in enumerate(blocks)], (q16, do16, o16, k16, v16),
                  (k16p, v16p), l16_ref, (dq16, dkc16, dvc16, dkp16, dvp16))
            _from_residue_major(dq16, tmp, dq_acc, True)
            dq = dq_acc[...]
            dq_ref[...] = ((dq * cos_ref[...] - _rope_partner(dq) * sin_ref[...]) * ATTN_SCALE).astype(BF16)

        @pl.when(n > 0)
        def _():
            _from_residue_major(dkp16, tmp, dkp_acc, True)
            _from_residue_major(dvp16, tmp, dvp_acc, True)
            dk = dkp_acc[...]
            dk_ref[...] = (dk * cosp_ref[...] - _rope_partner(dk) * sinp_ref[...]).astype(BF16)
            dv_ref[...] = dvp_acc[...].astype(BF16)

        @pl.when(n < nchunk)
        def _():
            for src, dst in ((dkc_acc, dkp_acc), (dvc_acc, dvp_acc), (dkc16, dkp16), (dvc16, dvp16),
                             (k16, k16p), (v16, v16p)):
                dst[...] = src[...]

    last = nchunk - 1
    cur = lambda h, n: (h, jnp.minimum(n, last), 0)
    prev = lambda h, n: (h, jnp.clip(n - 1, 0, last), 0)
    blk = lambda idx: pl.BlockSpec((None, SPAN, HEAD_DIM), idx)
    lblk = pl.BlockSpec((None, nblk, KEY_BLOCK), cur)
    tab = pl.BlockSpec((SPAN, HEAD_DIM), lambda h, n: (jnp.minimum(n, last), 0))
    tabp = pl.BlockSpec((SPAN, HEAD_DIM), lambda h, n: (jnp.clip(n - 1, 0, last), 0))
    out_q = pl.BlockSpec((SPAN, HEAD_DIM), lambda h, n: (jnp.minimum(n, last), h))
    out_kv = pl.BlockSpec((SPAN, HEAD_DIM), lambda h, n: (jnp.clip(n - 1, 0, last), h))
    shape = jax.ShapeDtypeStruct((seq, nh * HEAD_DIM), BF16)
    return pl.pallas_call(
        body, name="attn_bwd", out_shape=[shape, shape, shape], grid=(nh, nchunk + 1),
        in_specs=[blk(cur)] * 5 + [blk(prev)] * 2 + [lblk] * 3 + [tab, tab, tabp, tabp],
        out_specs=[out_q, out_kv, out_kv],
        scratch_shapes=[pltpu.VMEM((SPAN, HEAD_DIM), F32)] * 18
                       + [pltpu.VMEM((nblk, 2 * KEY_BLOCK, HEAD_DIM), BF16)] * 3
                       + [pltpu.VMEM((nblk, KEY_BLOCK, HEAD_DIM), BF16)] * 2,
        compiler_params=_params(("arbitrary", "arbitrary"), VMEM_LIMIT),
    )(q, k, v, do, o, k, v, *lses, cosf, sinf, cosf, sinf)


def _hub(x, tgt, hr, pf, o_hm, mod, b_mod, b_gate, g_final, w_out_rnn, w_out_attn, w_o):
    seq = x.shape[0]
    tm = HUB_ROWS
    nsteps = seq // tm

    def body(x_ref, t_ref, hr_ref, zr_ref, za_ref, gr_ref, ga_ref, o_ref, mod_ref, bmod_ref, bg_ref, gf_ref,
             wr_hbm, wa_hbm, wo_hbm,
             dx2_ref, dhr_ref, dzr_ref, do_ref, dza_ref, dgr_ref, dga_ref,
             ur_ref, dyr_ref, ua_ref, dya_ref, mg_ref, dmo_ref,
             ggf_ref, gbg_ref, dgate_ref, loss_ref,
             wr, wa, wo, sem):
        step = pl.program_id(0)

        @pl.when(step == 0)
        def _():
            for src, dst in ((wr_hbm, wr), (wa_hbm, wa), (wo_hbm, wo)):
                cp = pltpu.make_async_copy(src, dst, sem)
                cp.start()
                cp.wait()
            for ref in (ggf_ref, gbg_ref, dgate_ref, loss_ref):
                ref[...] = jnp.zeros_like(ref)

        gate = mod_ref[:, 2 * D_MODEL:] + bmod_ref[:, 2 * D_MODEL:]
        gfin = gf_ref[...]
        hr_t, zr, za = hr_ref[...], zr_ref[...], za_ref[...]
        o = jnp.concatenate([o_ref[hh] for hh in range(N_HEADS)], axis=1)
        sig_zr, sig_za = _sigmoid(zr), _sigmoid(za)
        silu_zr, silu_za = zr * sig_zr, za * sig_za
        u_rnn = (hr_t * silu_zr).astype(BF16)
        u_attn = (o * silu_za).astype(BF16)
        y_rnn = _dot(u_rnn, wr[...])
        y_attn = _dot(u_attn, wa[...])
        sr = _sigmoid(gr_ref[...] + bg_ref[:, :D_MODEL])
        sa = _sigmoid(ga_ref[...] + bg_ref[:, D_MODEL:])
        merged = (sr * y_rnn + sa * y_attn).astype(BF16)
        mo = _dot(merged, wo[...])
        x2 = x_ref[...] + gate * mo
        rstd = lax.rsqrt(jnp.mean(x2 * x2, axis=-1, keepdims=True) + NORM_EPS)
        xn = x2 * rstd
        err = xn * gfin - t_ref[...]
        loss_ref[...] += 0.5 * jnp.sum(jnp.sum(err * err, axis=-1, keepdims=True) * (1.0 / D_MODEL),
                                       axis=0, keepdims=True)

        dy = err * (1.0 / D_MODEL)
        ggf_ref[...] += jnp.sum(dy * xn, axis=0, keepdims=True)
        dxn = dy * gfin
        dx2 = rstd * (dxn - xn * jnp.mean(dxn * xn, axis=-1, keepdims=True))
        dx2_ref[...] = dx2
        dgate_ref[...] += jnp.sum(dx2 * mo, axis=0, keepdims=True)
        dmo = (dx2 * gate).astype(BF16)
        dmerged = _dot_nt(dmo, wo[...])
        mg_ref[...] = merged
        dmo_ref[...] = dmo
        dy_rnn = (dmerged * sr).astype(BF16)
        dy_attn = (dmerged * sa).astype(BF16)
        dg_r = dmerged * y_rnn * sr * (1.0 - sr)
        dg_a = dmerged * y_attn * sa * (1.0 - sa)
        dgr_ref[...] = dg_r.astype(BF16)
        dga_ref[...] = dg_a.astype(BF16)
        gbg_ref[:, :D_MODEL] += jnp.sum(dg_r, axis=0, keepdims=True)
        gbg_ref[:, D_MODEL:] += jnp.sum(dg_a, axis=0, keepdims=True)
        du_rnn = _dot_nt(dy_rnn, wr[...])
        du_attn = _dot_nt(dy_attn, wa[...])
        ur_ref[...] = u_rnn
        dyr_ref[...] = dy_rnn
        ua_ref[...] = u_attn
        dya_ref[...] = dy_attn
        dhr_ref[...] = du_rnn * silu_zr
        dzr_ref[...] = (du_rnn * hr_t * (sig_zr * (1.0 + zr * (1.0 - sig_zr)))).astype(BF16)
        dza_ref[...] = (du_attn * o * (sig_za * (1.0 + za * (1.0 - sig_za)))).astype(BF16)
        d_o = du_attn * silu_za
        for hh in range(N_HEADS):
            do_ref[hh] = d_o[:, hh * HEAD_DIM:(hh + 1) * HEAD_DIM]

    row = pl.BlockSpec((tm, D_MODEL), lambda i: (i, 0))
    piece = lambda slot: pl.BlockSpec((tm, D_MODEL), lambda i: (i, slot))
    hm = pl.BlockSpec((N_HEADS, tm, HEAD_DIM), lambda i: (0, i, 0))
    const = lambda cols: pl.BlockSpec((1, cols), lambda i: (0, 0))
    any_spec = pl.BlockSpec(memory_space=pl.ANY)
    act_f32 = jax.ShapeDtypeStruct((seq, D_MODEL), F32)
    act_bf16 = jax.ShapeDtypeStruct((seq, D_MODEL), BF16)
    return pl.pallas_call(
        body, name="hub",
        out_shape=[act_f32, act_f32, act_bf16, jax.ShapeDtypeStruct((N_HEADS, seq, HEAD_DIM), F32),
                   act_bf16, act_bf16, act_bf16] + [act_bf16] * 6 + [
                   jax.ShapeDtypeStruct((1, D_MODEL), F32), jax.ShapeDtypeStruct((1, 2 * D_MODEL), F32),
                   jax.ShapeDtypeStruct((1, D_MODEL), F32), jax.ShapeDtypeStruct((1, 1), F32)],
        grid=(nsteps,),
        in_specs=[row, row, row, piece(1), piece(2), piece(3), piece(4), hm,
                  const(3 * D_MODEL), const(3 * D_MODEL), const(2 * D_MODEL), const(D_MODEL),
                  any_spec, any_spec, any_spec],
        out_specs=[row, row, row, hm, row, row, row] + [row] * 6 + [
                   const(D_MODEL), const(2 * D_MODEL), const(D_MODEL), const(1)],
        scratch_shapes=[pltpu.VMEM((D_MODEL, D_MODEL), BF16)] * 3 + [pltpu.SemaphoreType.DMA],
        compiler_params=_params(("arbitrary",), VMEM_LIMIT),
    )(x, tgt, hr, pf, pf, pf, pf, o_hm, mod, b_mod, b_gate, g_final, w_out_rnn, w_out_attn, w_o)


def _pair_grads(name, lefts, rights):
    n = len(rights)
    shared = len(lefts) == 1
    seq = rights[0].shape[0]
    tk = WGRAD_ROWS
    nk = seq // tk

    def body(*refs):
        l_refs, r_refs = refs[:len(lefts)], refs[len(lefts):len(lefts) + n]
        out_ref, low_ref = refs[len(lefts) + n:]
        j, kk = pl.program_id(0), pl.program_id(1)

        @pl.when(kk == 0)
        def _():
            out_ref[...] = jnp.zeros_like(out_ref)

        for m in range(n):
            @pl.when(j == m)
            def _(m=m):
                out_ref[...] += _dot_tn(l_refs[0 if shared else m][...], r_refs[m][...])

        @pl.when(kk == nk - 1)
        def _():
            low_ref[...] = out_ref[...].astype(BF16)

    def spec(m):
        return pl.BlockSpec((tk, D_MODEL), lambda j, kk: (jnp.where(j == m, kk, jnp.where(j < m, 0, nk - 1)), 0))

    left_specs = [pl.BlockSpec((tk, D_MODEL), lambda j, kk: (kk, 0))] if shared else [spec(m) for m in range(n)]
    out_spec = pl.BlockSpec((None, D_MODEL, D_MODEL), lambda j, kk: (j, 0, 0))
    return pl.pallas_call(
        body, name=name,
        out_shape=[jax.ShapeDtypeStruct((n, D_MODEL, D_MODEL), F32), jax.ShapeDtypeStruct((n, D_MODEL, D_MODEL), BF16)],
        grid=(n, nk),
        in_specs=left_specs + [spec(m) for m in range(n)],
        out_specs=[out_spec, out_spec],
        compiler_params=_params(("arbitrary", "arbitrary"), VMEM_LIMIT),
    )(*lefts, *rights)


def _dh_dx(pieces, w_in_all, x, dx2, mod, b_mod, g_norm):
    seq = x.shape[0]
    tm = DX_ROWS

    def body(*refs):
        p_refs = refs[:8]
        w_hbm, x_ref, dx2_ref, mod_ref, bmod_ref, g_ref = refs[8:14]
        gx_ref, dshift_ref, dscale_ref, ggn_ref, w_scr, sem = refs[14:]
        step = pl.program_id(0)

        @pl.when(step == 0)
        def _():
            cp = pltpu.make_async_copy(w_hbm, w_scr, sem)
            cp.start()
            cp.wait()
            for ref in (dshift_ref, dscale_ref, ggn_ref):
                ref[...] = jnp.zeros_like(ref)

        dh = _dot_nt(p_refs[0][...], w_scr[0])
        for j in range(1, 8):
            dh = dh + _dot_nt(p_refs[j][...], w_scr[j])
        scale1 = 1.0 + mod_ref[:, D_MODEL:2 * D_MODEL] + bmod_ref[:, D_MODEL:2 * D_MODEL]
        g = g_ref[...]
        xf = x_ref[...]
        rstd_t = lax.rsqrt(jnp.mean(xf * xf, axis=-1, keepdims=True) + NORM_EPS)
        xn = xf * rstd_t
        dshift_ref[...] += jnp.sum(dh, axis=0, keepdims=True)
        dscale_ref[...] += jnp.sum(dh * (xn * g), axis=0, keepdims=True)
        ggn_ref[...] += jnp.sum(dh * scale1 * xn, axis=0, keepdims=True)
        dxn = dh * (g * scale1)
        gx_ref[...] = rstd_t * (dxn - xn * jnp.mean(dxn * xn, axis=-1, keepdims=True)) + dx2_ref[...]

    row = pl.BlockSpec((tm, D_MODEL), lambda i: (i, 0))
    const = lambda cols: pl.BlockSpec((1, cols), lambda i: (0, 0))
    vec = jax.ShapeDtypeStruct((1, D_MODEL), F32)
    return pl.pallas_call(
        body, name="dh_dx",
        out_shape=[jax.ShapeDtypeStruct((seq, D_MODEL), F32), vec, vec, vec],
        grid=(seq // tm,),
        in_specs=[row] * 8 + [pl.BlockSpec(memory_space=pl.ANY), row, row,
                              const(3 * D_MODEL), const(3 * D_MODEL), const(D_MODEL)],
        out_specs=[row, const(D_MODEL), const(D_MODEL), const(D_MODEL)],
        scratch_shapes=[pltpu.VMEM((8, D_MODEL, D_MODEL), BF16), pltpu.SemaphoreType.DMA],
        compiler_params=_params(("arbitrary",), VMEM_LIMIT),
    )(*pieces, w_in_all, x, dx2, mod, b_mod, g_norm)


def _adamw(name, w, g, m, v, recv=None):
    rows, cols = w.shape
    tr = rows if rows <= 256 else 256

    def body(*refs):
        w_ref, g_ref, m_ref, v_ref = refs[:4]
        d_ref, nm_ref, nv_ref = refs[-3:] if recv is None else refs[5:8]
        gv = g_ref[...]
        if recv is not None:
            r_ref, g_out = refs[4], refs[8]
            gv = ((gv + r_ref[0].astype(F32)) + r_ref[1].astype(F32)) + r_ref[2].astype(F32)
            g_out[...] = gv
        nm = ADAM_B1 * m_ref[...] + (1.0 - ADAM_B1) * gv
        nv = ADAM_B2 * v_ref[...] + (1.0 - ADAM_B2) * (gv * gv)
        m_hat = nm / (1.0 - ADAM_B1 ** ADAM_STEP)
        v_hat = nv / (1.0 - ADAM_B2 ** ADAM_STEP)
        d_ref[...] = -ADAM_LR * (m_hat / (jnp.sqrt(v_hat) + ADAM_EPS) + ADAM_WD * w_ref[...])
        nm_ref[...] = nm
        nv_ref[...] = nv

    spec = pl.BlockSpec((tr, cols), lambda i: (i, 0))
    shape = jax.ShapeDtypeStruct((rows, cols), F32)
    if recv is None:
        return pl.pallas_call(
            body, name=name, out_shape=[shape, shape, shape], grid=(rows // tr,),
            in_specs=[spec] * 4, out_specs=[spec] * 3,
            compiler_params=_params(("arbitrary",)),
        )(w, g, m, v)
    return pl.pallas_call(
        body, name=name, out_shape=[shape] * 4, grid=(rows // tr,),
        in_specs=[spec] * 4 + [pl.BlockSpec((3, tr, cols), lambda i: (0, i, 0))], out_specs=[spec] * 4,
        compiler_params=_params(("arbitrary",)),
    )(w, g, m, v, recv)


def kernel(x, c, positions, g_norm, w_mod, b_mod, w_in, b_gate, conv_w, conv_b, w_a, b_a, w_x, b_x, lam, w_out_rnn, w_out_attn, w_o, g_final, loss_target, m_g_norm, m_w_mod, m_b_mod, m_w_in, m_b_gate, m_conv_w, m_conv_b, m_w_a, m_b_a, m_w_x, m_b_x, m_lam, m_w_out_rnn, m_w_out_attn, m_w_o, m_g_final, v_g_norm, v_w_mod, v_b_mod, v_w_in, v_b_gate, v_conv_w, v_conv_b, v_w_a, v_b_a, v_w_x, v_b_x, v_lam, v_w_out_rnn, v_w_out_attn, v_w_o, v_g_final):
    seq = x.shape[1]
    me = _index(_my_pos())
    xs, tgt = x[0], loss_target[0]

    pos = positions[0].astype(F32)[:, None]
    inv_freq = ROPE_THETA ** (-jnp.arange(0, 2 * ROT_HALF, 2, dtype=F32) / (2 * ROT_HALF))
    ang = pos * inv_freq
    rest = HEAD_DIM - 2 * ROT_HALF
    cosf = jnp.concatenate([jnp.cos(ang), jnp.cos(ang), jnp.ones((seq, rest), F32)], axis=1)
    sinf = jnp.concatenate([-jnp.sin(ang), jnp.sin(ang), jnp.zeros((seq, rest), F32)], axis=1)
    keep = (positions[0] != 0).astype(F32)[:, None]

    (w_in_all,) = _ag_big("gather_weights", [w_in[0].astype(BF16)])
    both = _ag_small("gather_c_conv_w", jnp.concatenate(
        [jnp.broadcast_to(c, (SUBLANES, D_MODEL)), jnp.pad(conv_w[0], ((0, SUBLANES - 4), (0, 0)))], axis=1))
    c_all, conv_w8 = both[:, 0, :D_MODEL], both[:, :, D_MODEL:]
    mod_cols = w_mod.shape[2]
    mod_part = _ag_small("gather_mod", _mod_fwd(c_all, w_mod[0]))
    mod = lax.dynamic_index_in_dim(mod_part, me, axis=1, keepdims=False).reshape(1, N_DEV * mod_cols)
    mod, late = lax.optimization_barrier(
        (mod, [w_out_rnn[0].astype(BF16), w_out_attn[0].astype(BF16), w_o[0].astype(BF16)]))
    late_sends, late_recvs, late_shards, late_lands, late_token = _gather_start(late, me)
    mod = mod + late_token[0:1, 0:1]

    blocks = lambda t: t.reshape(RNN_BLOCKS, 1, 128)
    rnn_params = (conv_w8, blocks(conv_b), w_a[0], blocks(b_a), w_x[0], blocks(b_x), blocks(lam))

    h = _norm(xs, mod, b_mod, g_norm)
    pf, q, k, v = _proj(h, w_in_all, cosf, sinf)
    hr = _rnn_fwd(pf, keep, *rnn_params)
    o, lses = _attn_fwd(q, k, v)

    w_or_all, w_oa_all, w_o_all = (t.reshape(D_MODEL, D_MODEL) for t in _gather_wait(
        late_sends, late_recvs, late_shards, late_lands, o))
    (dx2, dhr, dz_rnn, d_o, dz_attn, dg_r, dg_a, u_rnn, dy_rnn, u_attn, dy_attn, merged, dmo,
     gp_g_final, gp_b_gate, dgate, loss_part) = _hub(
        xs, tgt, hr, pf, o, mod, b_mod, b_gate, g_final.reshape(1, D_MODEL), w_or_all, w_oa_all, w_o_all)
    gp_out, gp_out_low = _pair_grads("out_grads", [u_rnn, u_attn, merged], [dy_rnn, dy_attn, dmo])
    dq, dk, dv = _attn_bwd(q, k, v, d_o, o, lses, cosf, sinf)
    dx_rnn, gp_conv_w, gp_conv_b, gp_w_a, gp_b_a, gp_w_x, gp_b_x, gp_lam = _rnn_bwd(pf, hr, dhr, keep, *rnn_params)
    pieces = [dx_rnn, dz_rnn, dq, dk, dv, dz_attn, dg_r, dg_a]
    gp_w_in, gp_w_in_low = _pair_grads("w_in_grad", [h], pieces)

    by_target = lambda t: [t[i].reshape(N_DEV, 128, D_MODEL) for i in range(3)]
    stacks = [gp_w_in] + by_target(gp_out)
    from_sib = _rs_to_sibling("rs_sibling", [gp_w_in_low] + by_target(gp_out_low))
    targets = jnp.bitwise_xor(me, 2 * jnp.arange(4, dtype=jnp.int32)).astype(jnp.int32)
    sums = [_add_sibling("rs_add_sibling_%d" % a, s_, r_, targets) for a, (s_, r_) in enumerate(zip(stacks, from_sib))]
    send_sems, recv_sems, sent, landing, token = _rs_chips_start([send for _, send in sums])

    mod_after = mod + token[0:1, 0:1]
    grad_x, dshift, dscale, gp_g_norm = _dh_dx(pieces, w_in_all, xs, dx2, mod_after, b_mod, g_norm)

    dmod = jnp.concatenate([dshift, dscale, dgate], axis=1)
    dmod_all = _ag_small("gather_dmod", jnp.broadcast_to(dmod, (SUBLANES, 3 * D_MODEL)))[:, 0, :]
    dmod_cols = lax.dynamic_slice_in_dim(dmod_all, me * mod_cols, mod_cols, axis=1)
    g_b_mod, g_w_mod = _mod_bwd(c_all, dmod_all, dmod_cols)

    flat = lambda t: t.reshape(-1, 128)
    small = [flat(gp_g_norm), flat(gp_b_gate), flat(gp_conv_b), flat(gp_b_a), flat(gp_b_x), flat(gp_lam),
             flat(gp_g_final), flat(gp_conv_w), jnp.broadcast_to(loss_part, (SUBLANES, 128)),
             flat(gp_w_a), flat(gp_w_x)]
    sizes = [t.shape[0] for t in small]
    small.append(jnp.zeros((-sum(sizes) % (2 * SUBLANES), 128), F32))
    total = _allreduce_small("allreduce_small_grads", jnp.concatenate(small, axis=0))
    offs = [sum(sizes[:i]) for i in range(len(sizes))]
    (g_g_norm, g_b_gate, g_conv_b, g_b_a, g_b_x, g_lam, g_g_final, g_conv_w_all, loss_rows, g_w_a, g_w_x) = (
        total[o_:o_ + s_] for o_, s_ in zip(offs, sizes))
    loss = loss_rows[0, 0]
    g_conv_w = lax.dynamic_index_in_dim(g_conv_w_all.reshape(RNN_BLOCKS, SUBLANES, 128), me, axis=0,
                                        keepdims=False)[:4]

    from_chips = _rs_chips_wait(send_sems, recv_sems, sent, landing, total)

    results = {}
    sharded = (("w_in", w_in, m_w_in, v_w_in, (D_MODEL, D_MODEL)),
               ("w_out_rnn", w_out_rnn, m_w_out_rnn, v_w_out_rnn, (128, D_MODEL)),
               ("w_out_attn", w_out_attn, m_w_out_attn, v_w_out_attn, (128, D_MODEL)),
               ("w_o", w_o, m_w_o, v_w_o, (128, D_MODEL)))
    for (name, w_, m_, v_, shape2), (own, _), arrived in zip(sharded, sums, from_chips):
        d_, nm_, nv_, g_ = _adamw("adamw_" + name, w_.reshape(shape2), own, m_.reshape(shape2), v_.reshape(shape2),
                                  arrived)
        results[name] = (g_, d_, nm_, nv_)
    shape2 = (D_MODEL, mod_cols)
    results["w_mod"] = (g_w_mod,) + tuple(_adamw("adamw_w_mod", w_mod.reshape(shape2), g_w_mod,
                                                 m_w_mod.reshape(shape2), v_w_mod.reshape(shape2)))
    lanes = (("g_norm", g_norm, g_g_norm, m_g_norm, v_g_norm), ("b_mod", b_mod, g_b_mod, m_b_mod, v_b_mod),
             ("b_gate", b_gate, g_b_gate, m_b_gate, v_b_gate), ("conv_w", conv_w, g_conv_w, m_conv_w, v_conv_w),
             ("conv_b", conv_b, g_conv_b, m_conv_b, v_conv_b), ("w_a", w_a, g_w_a, m_w_a, v_w_a),
             ("b_a", b_a, g_b_a, m_b_a, v_b_a), ("w_x", w_x, g_w_x, m_w_x, v_w_x), ("b_x", b_x, g_b_x, m_b_x, v_b_x),
             ("lam", lam, g_lam, m_lam, v_lam), ("g_final", g_final, g_g_final, m_g_final, v_g_final))
    for name, w_, g_, m_, v_ in lanes:
        rows128 = lambda t: t.reshape(-1, 128)
        results[name] = (g_,) + tuple(_adamw("adamw_" + name, rows128(w_), rows128(g_), rows128(m_), rows128(v_)))
    order = ("g_norm", "w_mod", "b_mod", "w_in", "b_gate", "conv_w", "conv_b", "w_a", "b_a", "w_x", "b_x", "lam",
             "w_out_rnn", "w_out_attn", "w_o", "g_final")
    given = dict(g_norm=g_norm, w_mod=w_mod, b_mod=b_mod, w_in=w_in, b_gate=b_gate, conv_w=conv_w, conv_b=conv_b,
                 w_a=w_a, b_a=b_a, w_x=w_x, b_x=b_x, lam=lam, w_out_rnn=w_out_rnn, w_out_attn=w_out_attn, w_o=w_o,
                 g_final=g_final)
    outs = [[results[name][k].reshape(given[name].shape) for name in order] for k in range(4)]
    return (loss, grad_x[None], *outs[0], *outs[1], *outs[2], *outs[3])
```

```python
import jax
import jax.numpy as jnp
from jax import lax
from jax.experimental import pallas as pl
from jax.experimental.pallas import tpu as pltpu

F32 = jnp.float32
BF16 = jnp.bfloat16
MESH = pl.DeviceIdType.MESH

D_MODEL = 1024
N_HEADS = 8
HEAD_DIM = 128
RNN_BLOCKS = 8
N_DEV = 8
ROT_HALF = 16
ROPE_THETA = 500000.0
DILATIONS = (1, 4, 16)
KEY_BLOCK = 128
SPAN = KEY_BLOCK * DILATIONS[-1]
ATTN_SCALE = HEAD_DIM ** -0.5
NORM_EPS = 1e-6
LRU_C = 8.0
NEG_INF = -1e30
ADAM_LR, ADAM_B1, ADAM_B2, ADAM_EPS, ADAM_WD, ADAM_STEP = 0.001, 0.9, 0.999, 1e-08, 0.01, 10

SUBLANES = 8
VMEM_LIMIT = 56 * 1024 * 1024
PROJ_ROWS = 1024
RNN_ROWS = 2048
HUB_ROWS = 256
DX_ROWS = 512
WGRAD_ROWS = 1024
ADD_ROWS = 256


def _params(sem=None, vmem=None):
    return pltpu.CompilerParams(dimension_semantics=sem, vmem_limit_bytes=vmem)


def _dot(a, b):
    return jnp.dot(a, b, preferred_element_type=F32)


def _dot_nt(a, b):
    return lax.dot_general(a, b, (((1,), (1,)), ((), ())), preferred_element_type=F32)


def _dot_tn(a, b):
    return lax.dot_general(a, b, (((0,), (0,)), ((), ())), preferred_element_type=F32)


def _sigmoid(z):
    return 1.0 / (1.0 + jnp.exp(-z))


def _expm1_nonpos(z, exp_z):
    return jnp.where(z > -0.01, z * (1.0 + 0.5 * z), exp_z - 1.0)


def _my_pos():
    return lax.axis_index("x"), lax.axis_index("y"), lax.axis_index("c")


def _flip(pos, k):
    x, y, c = pos
    return ((1 - x) if k & 4 else x, (1 - y) if k & 2 else y, (1 - c) if k & 1 else c)


def _index(pos):
    return 4 * pos[0] + 2 * pos[1] + pos[2]


def _ag_small(name, v):
    rows, cols = v.shape

    def body(v_ref, out_ref, send_sems, recv_sems):
        me = _my_pos()
        out_ref[_index(me)] = v_ref[...]
        sends = []
        for k in range(1, N_DEV):
            cp = pltpu.make_async_remote_copy(
                src_ref=v_ref, dst_ref=out_ref.at[_index(me)], send_sem=send_sems.at[k - 1],
                recv_sem=recv_sems.at[k - 1], device_id=_flip(me, k), device_id_type=MESH)
            cp.start()
            sends.append(cp)
        for k in range(1, N_DEV):
            peer = _flip(me, k)
            pltpu.make_async_remote_copy(
                src_ref=v_ref, dst_ref=out_ref.at[_index(peer)], send_sem=send_sems.at[k - 1],
                recv_sem=recv_sems.at[k - 1], device_id=peer, device_id_type=MESH).wait_recv()
        for cp in sends:
            cp.wait_send()

    return pl.pallas_call(
        body, name=name,
        out_shape=jax.ShapeDtypeStruct((N_DEV, rows, cols), v.dtype),
        in_specs=[pl.BlockSpec(memory_space=pltpu.VMEM)],
        out_specs=pl.BlockSpec(memory_space=pltpu.VMEM),
        scratch_shapes=[pltpu.SemaphoreType.DMA((N_DEV - 1,)), pltpu.SemaphoreType.DMA((N_DEV - 1,))],
        compiler_params=_params(None, VMEM_LIMIT),
    )(v)


def _split_start(name, make_copies, nsem, srcs, lands):
    n, k = len(srcs), len(lands)

    def body(*refs):
        for cp in make_copies(refs[:n], refs[n:n + k], refs[n + k], refs[n + k + 1]):
            cp.start()
        refs[-1][...] = jnp.zeros_like(refs[-1])

    hbm = pl.BlockSpec(memory_space=pltpu.HBM)
    sem = pl.BlockSpec(memory_space=pltpu.SEMAPHORE)
    arrays = [*srcs, *lands]
    outs = pl.pallas_call(
        body, name=name,
        out_shape=(pltpu.SemaphoreType.DMA((nsem,)), pltpu.SemaphoreType.DMA((nsem,)),
                   *[pltpu.HBM(t.shape, t.dtype) for t in arrays], jax.ShapeDtypeStruct((SUBLANES, 128), F32)),
        in_specs=[hbm] * (n + k),
        out_specs=(sem, sem, *[hbm] * (n + k), pl.BlockSpec(memory_space=pltpu.VMEM)),
        input_output_aliases={i: 2 + i for i in range(n + k)},
        compiler_params=pltpu.CompilerParams(has_side_effects=pltpu.SideEffectType.DATAFLOW_SIDE_EFFECTING),
    )(*[pltpu.with_memory_space_constraint(t, pltpu.HBM) for t in arrays])
    return outs[0], outs[1], outs[2:2 + n], outs[2 + n:2 + n + k], outs[-1]


def _split_wait(name, make_copies, flight, after):
    send_sems, recv_sems, srcs, lands, _ = flight
    n, k = len(srcs), len(lands)

    def body(*refs):
        for cp in make_copies(refs[:n], refs[n:n + k], refs[n + k], refs[n + k + 1]):
            cp.wait_send()
            cp.wait_recv()

    hbm = pl.BlockSpec(memory_space=pltpu.HBM)
    sem = pl.BlockSpec(memory_space=pltpu.SEMAPHORE)
    arrays = [*srcs, *lands]
    outs = pl.pallas_call(
        body, name=name, out_shape=tuple(pltpu.HBM(t.shape, t.dtype) for t in arrays),
        in_specs=[hbm] * (n + k) + [sem, sem, pl.BlockSpec(memory_space=pl.ANY)],
        out_specs=[hbm] * (n + k),
        input_output_aliases={i: i for i in range(n + k)},
        compiler_params=pltpu.CompilerParams(has_side_effects=pltpu.SideEffectType.DATAFLOW_SIDE_EFFECTING),
    )(*arrays, send_sems, recv_sems, after)
    return outs[:n], outs[n:]


def _remote(src, dst, send_sems, recv_sems, k, to):
    return pltpu.make_async_remote_copy(src_ref=src, dst_ref=dst, send_sem=send_sems.at[k], recv_sem=recv_sems.at[k],
                                        device_id=to, device_id_type=MESH)


def _peer_copies(shards, lands, send_sems, recv_sems):
    me = _my_pos()
    return [_remote(shards[a], lands[a].at[_index(me)], send_sems, recv_sems, a * 7 + k - 1, _flip(me, k))
            for a in range(len(shards)) for k in range(1, N_DEV)]


def _own_block_copies(shards, lands, send_sems, recv_sems):
    me = _my_pos()
    return [_remote(shards[0], lands[0].at[_index(me)], send_sems, recv_sems, i, _flip(me, k))
            for i, k in enumerate((1, 2, 4, 6))]


def _forward_copies(arrived, lands, send_sems, recv_sems):
    me = _my_pos()
    return [_remote(arrived[0].at[_index(_flip(me, 2 * m))], lands[0].at[_index(_flip(me, 2 * m))],
                    send_sems, recv_sems, m - 1, _flip(me, 1)) for m in range(1, 4)]


def _merge_slots(name, base, other):
    def body(base_ref, other_ref, out_ref, sems):
        sib = _flip(_my_pos(), 1)
        moves = [pltpu.make_async_copy(other_ref.at[_index(_flip(sib, 2 * m))], out_ref.at[_index(_flip(sib, 2 * m))],
                                       sems.at[m - 1]) for m in range(1, 4)]
        for cp in moves:
            cp.start()
        for cp in moves:
            cp.wait()

    any_spec = pl.BlockSpec(memory_space=pl.ANY)
    return pl.pallas_call(
        body, name=name, out_shape=jax.ShapeDtypeStruct(base.shape, base.dtype),
        in_specs=[any_spec, any_spec], out_specs=any_spec, input_output_aliases={0: 0},
        scratch_shapes=[pltpu.SemaphoreType.DMA((3,))],
    )(base, other)


def _rs_to_sibling(name, stacks):
    n = len(stacks)

    def body(*refs):
        ins, outs = refs[:n], refs[n:2 * n]
        send_sems, recv_sems = refs[2 * n:]
        me = _my_pos()
        sib = _flip(me, 1)
        sends = []
        for a in range(n):
            for m in range(4):
                target = _flip(sib, 2 * m)
                cp = pltpu.make_async_remote_copy(
                    src_ref=ins[a].at[_index(target)], dst_ref=outs[a].at[m],
                    send_sem=send_sems.at[a * 4 + m], recv_sem=recv_sems.at[a * 4 + m],
                    device_id=sib, device_id_type=MESH)
                cp.start()
                sends.append(cp)
        for cp in sends:
            cp.wait_recv()
        for cp in sends:
            cp.wait_send()

    any_spec = pl.BlockSpec(memory_space=pl.ANY)
    return pl.pallas_call(
        body, name=name,
        out_shape=[jax.ShapeDtypeStruct((4,) + s.shape[1:], s.dtype) for s in stacks],
        in_specs=[any_spec] * n, out_specs=[any_spec] * n,
        scratch_shapes=[pltpu.SemaphoreType.DMA((4 * n,)), pltpu.SemaphoreType.DMA((4 * n,))],
    )(*stacks)


def _chip_copies(srcs, lands, send_sems, recv_sems):
    me = _my_pos()
    return [_remote(srcs[a].at[m - 1], lands[a].at[m - 1], send_sems, recv_sems, a * 3 + m - 1, _flip(me, 2 * m))
            for a in range(len(srcs)) for m in range(1, 4)]


def _add_sibling(name, stack, recv, targets):
    _, rows, cols = stack.shape
    tr = min(rows, ADD_ROWS)

    def own_body(t_ref, a_ref, b_ref, o_ref):
        o_ref[...] = a_ref[...] + b_ref[...].astype(F32)

    own = pl.pallas_call(
        own_body, name=name + "_own",
        out_shape=jax.ShapeDtypeStruct((rows, cols), F32),
        grid_spec=pltpu.PrefetchScalarGridSpec(
            num_scalar_prefetch=1, grid=(rows // tr,),
            in_specs=[pl.BlockSpec((None, tr, cols), lambda i, t: (t[0], i, 0)),
                      pl.BlockSpec((None, tr, cols), lambda i, t: (0, i, 0))],
            out_specs=pl.BlockSpec((tr, cols), lambda i, t: (i, 0))),
        compiler_params=_params(("arbitrary",)),
    )(targets, stack, recv)

    def send_body(t_ref, a_ref, b_ref, o_ref):
        o_ref[...] = (a_ref[...] + b_ref[...].astype(F32)).astype(BF16)

    send = pl.pallas_call(
        send_body, name=name + "_send",
        out_shape=jax.ShapeDtypeStruct((3, rows, cols), BF16),
        grid_spec=pltpu.PrefetchScalarGridSpec(
            num_scalar_prefetch=1, grid=(3, rows // tr),
            in_specs=[pl.BlockSpec((None, tr, cols), lambda m, i, t: (t[m + 1], i, 0)),
                      pl.BlockSpec((None, tr, cols), lambda m, i, t: (m + 1, i, 0))],
            out_specs=pl.BlockSpec((None, tr, cols), lambda m, i, t: (m, i, 0))),
        compiler_params=_params(("arbitrary", "arbitrary")),
    )(targets, stack, recv)
    return own, send


def _allreduce_small(name, v):
    rows, cols = v.shape
    half = rows // 2
    assert rows % (2 * SUBLANES) == 0

    def body(v_ref, out_ref, from_sib, chip_half, from_chips, send_sems, recv_sems):
        me = _my_pos()
        sib = _flip(me, 1)
        mine = pl.ds(pl.multiple_of(me[2] * half, SUBLANES), half)
        theirs = pl.ds(pl.multiple_of((1 - me[2]) * half, SUBLANES), half)

        def copy(k, src, dst, to):
            return pltpu.make_async_remote_copy(src_ref=src, dst_ref=dst, send_sem=send_sems.at[k],
                                                recv_sem=recv_sems.at[k], device_id=to, device_id_type=MESH)

        to_sib = copy(0, v_ref.at[theirs], from_sib, sib)
        to_sib.start()
        to_sib.wait_recv()
        chip_half[...] = v_ref[mine, :] + from_sib[...]
        to_chips = [copy(m, chip_half, from_chips.at[m - 1], _flip(me, 2 * m)) for m in range(1, 4)]
        for cp in to_chips:
            cp.start()
        for cp in to_chips:
            cp.wait_recv()
        my_chip = 2 * me[0] + me[1]
        total = None
        for chip in range(4):
            slot = jnp.maximum(jnp.bitwise_xor(chip, my_chip) - 1, 0)
            part = jnp.where(chip == my_chip, chip_half[...], from_chips[slot])
            total = part if total is None else total + part
        out_ref[mine, :] = total
        swap = copy(4, out_ref.at[mine], out_ref.at[mine], sib)
        swap.start()
        copy(4, out_ref.at[theirs], out_ref.at[theirs], sib).wait_recv()
        for cp in [to_sib, swap] + to_chips:
            cp.wait_send()

    return pl.pallas_call(
        body, name=name, out_shape=jax.ShapeDtypeStruct((rows, cols), F32),
        in_specs=[pl.BlockSpec(memory_space=pltpu.VMEM)],
        out_specs=pl.BlockSpec(memory_space=pltpu.VMEM),
        scratch_shapes=[pltpu.VMEM((half, cols), F32), pltpu.VMEM((half, cols), F32),
                        pltpu.VMEM((3, half, cols), F32),
                        pltpu.SemaphoreType.DMA((5,)), pltpu.SemaphoreType.DMA((5,))],
        compiler_params=_params(None, VMEM_LIMIT),
    )(v)


def _mod_fwd(c_all, w_mod):
    def body(c_ref, w_ref, o_ref):
        c = c_ref[...]
        o_ref[...] = jnp.dot(c * _sigmoid(c), w_ref[...], preferred_element_type=F32,
                             precision=lax.Precision.HIGHEST)

    return pl.pallas_call(
        body, name="mod_fwd", out_shape=jax.ShapeDtypeStruct((N_DEV, w_mod.shape[1]), F32),
    )(c_all, w_mod)


def _mod_bwd(c_all, dmod_all, dmod_cols):
    def body(c_ref, da_ref, dc_ref, gb_ref, gw_ref):
        c = c_ref[...]
        acc = da_ref[0:1, :]
        for b in range(1, N_DEV):
            acc = acc + da_ref[b:b + 1, :]
        gb_ref[...] = acc
        gw_ref[...] = lax.dot_general(c * _sigmoid(c), dc_ref[...], (((0,), (0,)), ((), ())),
                                      preferred_element_type=F32, precision=lax.Precision.HIGHEST)

    return pl.pallas_call(
        body, name="mod_bwd",
        out_shape=[jax.ShapeDtypeStruct((1, dmod_all.shape[1]), F32),
                   jax.ShapeDtypeStruct((c_all.shape[1], dmod_cols.shape[1]), F32)],
    )(c_all, dmod_all, dmod_cols)


def _rope_partner(t):
    lane = lax.broadcasted_iota(jnp.int32, t.shape, 1)
    return jnp.where(lane < ROT_HALF, pltpu.roll(t, HEAD_DIM - ROT_HALF, 1), pltpu.roll(t, ROT_HALF, 1))


def _norm(x, mod, b_mod, g_norm):
    seq = x.shape[0]
    tm = PROJ_ROWS

    def body(x_ref, mod_ref, bmod_ref, g_ref, h_ref):
        xf = x_ref[...]
        rstd = lax.rsqrt(jnp.mean(xf * xf, axis=-1, keepdims=True) + NORM_EPS)
        shift = mod_ref[:, 0:D_MODEL] + bmod_ref[:, 0:D_MODEL]
        scale = mod_ref[:, D_MODEL:2 * D_MODEL] + bmod_ref[:, D_MODEL:2 * D_MODEL]
        h_ref[...] = (((xf * rstd) * g_ref[...]) * (1.0 + scale) + shift).astype(BF16)

    row = pl.BlockSpec((tm, D_MODEL), lambda i: (i, 0))
    const = lambda cols: pl.BlockSpec((1, cols), lambda i: (0, 0))
    return pl.pallas_call(
        body, name="norm", out_shape=jax.ShapeDtypeStruct((seq, D_MODEL), BF16), grid=(seq // tm,),
        in_specs=[row, const(3 * D_MODEL), const(3 * D_MODEL), const(D_MODEL)], out_specs=row,
        compiler_params=_params(("arbitrary",), VMEM_LIMIT),
    )(x, mod, b_mod, g_norm)


def _proj(name, h, w, slots, pieces, cosf, sinf, prior):
    seq = h.shape[0]
    tm = PROJ_ROWS
    count = pieces.shape[0]

    def body(slots_ref, pieces_ref, h_ref, w_ref, cos_ref, sin_ref, *rest):
        out_ref = rest[-1]
        piece = pieces_ref[pl.program_id(0)]

        @pl.when((piece < 2) | (piece > 3))
        def _():
            out_ref[...] = _dot(h_ref[...], w_ref[...])

        def rotated(gain):
            for pair in range(N_HEADS // 2):
                both = _dot(h_ref[...], w_ref[:, 2 * pair * HEAD_DIM:2 * (pair + 1) * HEAD_DIM])
                for hh in (2 * pair, 2 * pair + 1):
                    t = both[:, (hh % 2) * HEAD_DIM:(hh % 2 + 1) * HEAD_DIM]
                    t = t * cos_ref[...] + _rope_partner(t) * sin_ref[...]
                    out_ref[:, hh * HEAD_DIM:(hh + 1) * HEAD_DIM] = t if gain is None else t * gain

        @pl.when(piece == 2)
        def _():
            rotated(ATTN_SCALE)

        @pl.when(piece == 3)
        def _():
            rotated(None)

    row = lambda j, i, sl, pc: (i, 0)
    in_specs = [pl.BlockSpec((tm, D_MODEL), row),
                pl.BlockSpec((None, D_MODEL, D_MODEL), lambda j, i, sl, pc: (sl[j], 0, 0)),
                pl.BlockSpec((tm, HEAD_DIM), row), pl.BlockSpec((tm, HEAD_DIM), row)]
    args = [slots, pieces, h, w, cosf, sinf]
    aliases = {}
    if prior is not None:
        in_specs.append(pl.BlockSpec(memory_space=pl.ANY))
        args.append(prior)
        aliases = {6: 0}
    return pl.pallas_call(
        body, name=name,
        out_shape=jax.ShapeDtypeStruct((seq, 8 * D_MODEL), F32),
        grid_spec=pltpu.PrefetchScalarGridSpec(
            num_scalar_prefetch=2, grid=(count, seq // tm), in_specs=in_specs,
            out_specs=pl.BlockSpec((tm, D_MODEL), lambda j, i, sl, pc: (i, pc[j]))),
        input_output_aliases=aliases,
        compiler_params=_params(("arbitrary", "arbitrary"), VMEM_LIMIT),
    )(*args)


def _shift_down(v, s, head):
    rolled = pltpu.roll(v, s, 0)
    row = lax.broadcasted_iota(jnp.int32, head.shape, 0)
    first = jnp.where(row < s, pltpu.roll(head, s, 0), rolled[:SUBLANES, :])
    return jnp.concatenate([first, rolled[SUBLANES:, :]], axis=0)


def _shift_up(v, s, tail):
    rows = v.shape[0]
    rolled = pltpu.roll(v, rows - s, 0)
    row = lax.broadcasted_iota(jnp.int32, tail.shape, 0)
    last = jnp.where(row >= SUBLANES - s, pltpu.roll(tail, SUBLANES - s, 0), rolled[rows - SUBLANES:, :])
    return jnp.concatenate([rolled[:rows - SUBLANES, :], last], axis=0)


def _doubling(a, b, period, reverse):
    rows = a.shape[0]
    pos = lax.broadcasted_iota(jnp.int32, a.shape, 0) & (period - 1)
    k = 1
    while k < period:
        inside = (pos < period - k) if reverse else (pos >= k)
        shift = rows - k if reverse else k
        a_s = jnp.where(inside, pltpu.roll(a, shift, 0), 1.0)
        b_s = jnp.where(inside, pltpu.roll(b, shift, 0), 0.0)
        b = a * b_s + b
        a = a * a_s
        k *= 2
    return a, b


def _scan(a, b, boundary, reverse, a_scr, b_scr, spread):
    rows = a.shape[0]
    ntile = rows // SUBLANES
    a_scr[...], b_scr[...] = _doubling(a, b, SUBLANES, reverse)
    ends = pl.ds(0 if reverse else SUBLANES - 1, ntile, stride=SUBLANES)
    a_end, x_end = _doubling(a_scr[ends, :], b_scr[ends, :], ntile, reverse)
    x_end = x_end + a_end * boundary
    tile = lax.broadcasted_iota(jnp.int32, x_end.shape, 0)
    if reverse:
        incoming = jnp.where(tile == ntile - 1, boundary, pltpu.roll(x_end, ntile - 1, 0))
        last = x_end[0:1, :]
    else:
        incoming = jnp.where(tile == 0, boundary, pltpu.roll(x_end, 1, 0))
        last = x_end[ntile - 1:ntile, :]
    for s in range(SUBLANES):
        spread[pl.ds(s, ntile, stride=SUBLANES), :] = incoming
    return b_scr[...] + a_scr[...] * spread[...], last


def _conv_taps(xr, head):
    return [_shift_down(xr, 3, head), _shift_down(xr, 2, head), _shift_down(xr, 1, head), xr]


def _rnn_gates(xc, wa, ba, wx, bx, lam, keep):
    xcb = xc.astype(BF16)
    r = _sigmoid(_dot(xcb, wa.astype(BF16)) + ba)
    i = _sigmoid(_dot(xcb, wx.astype(BF16)) + bx)
    softplus = jnp.maximum(-lam, 0.0) + jnp.log(1.0 + jnp.exp(-jnp.abs(lam)))
    cl = -LRU_C * softplus
    log_a = cl * r
    a_raw = jnp.exp(log_a)
    mult_raw = jnp.sqrt(-_expm1_nonpos(2.0 * log_a, a_raw * a_raw))
    live = keep > 0.0
    return r, i, cl, a_raw, mult_raw, jnp.where(live, a_raw, 0.0), jnp.where(live, mult_raw, 1.0), live


def _rnn_specs(seq, rows, time_of):
    per = rows // SUBLANES
    vec = pl.BlockSpec((None, 1, 128), lambda hb, n: (hb, 0, 0))
    mat = pl.BlockSpec((None, 128, 128), lambda hb, n: (hb, 0, 0))
    return [pl.BlockSpec((rows, 128), lambda hb, n: (time_of(n), hb)),
            pl.BlockSpec((SUBLANES, 128), lambda hb, n: (jnp.maximum(time_of(n) * per - 1, 0), hb)),
            pl.BlockSpec((rows, 1), lambda hb, n: (time_of(n), 0)),
            pl.BlockSpec((None, SUBLANES, 128), lambda hb, n: (hb, 0, 0)),
            vec, mat, vec, mat, vec, vec]


def _rnn_fwd(pf, keep, conv_w8, conv_b, w_a, b_a, w_x, b_x, lam):
    seq = pf.shape[0]
    rows = RNN_ROWS

    def body(x_ref, xh_ref, keep_ref, cw_ref, cb_ref, wa_ref, ba_ref, wx_ref, bx_ref, lam_ref, hr_ref,
             carry, a_scr, b_scr, spread):
        n = pl.program_id(1)

        @pl.when(n == 0)
        def _():
            carry[...] = jnp.zeros_like(carry)

        xr = x_ref[...]
        head = jnp.where(n > 0, xh_ref[...], 0.0)
        taps = _conv_taps(xr, head)
        xc = cb_ref[...] + sum(cw_ref[k:k + 1, :] * taps[k] for k in range(4))
        _, i, _, _, _, a, mult, _ = _rnn_gates(xc, wa_ref[...], ba_ref[...], wx_ref[...], bx_ref[...],
                                               lam_ref[...], keep_ref[...])
        h, last = _scan(a, mult * i * xc, carry[0:1, :], False, a_scr, b_scr, spread)
        hr_ref[...] = h
        carry[...] = jnp.broadcast_to(last, carry.shape)

    chunk_f32 = pltpu.VMEM((rows, 128), F32)
    return pl.pallas_call(
        body, name="rnn_fwd",
        out_shape=jax.ShapeDtypeStruct((seq, D_MODEL), F32),
        grid=(RNN_BLOCKS, seq // rows),
        in_specs=_rnn_specs(seq, rows, lambda n: n),
        out_specs=pl.BlockSpec((rows, 128), lambda hb, n: (n, hb)),
        scratch_shapes=[pltpu.VMEM((SUBLANES, 128), F32), chunk_f32, chunk_f32, chunk_f32],
        compiler_params=_params(("arbitrary", "arbitrary"), VMEM_LIMIT),
    )(pf, pf, keep, conv_w8, conv_b, w_a, b_a, w_x, b_x, lam)


def _rnn_bwd(pf, hr, dhr, keep, conv_w8, conv_b, w_a, b_a, w_x, b_x, lam):
    seq = pf.shape[0]
    rows = RNN_ROWS
    nchunk = seq // rows
    per = rows // SUBLANES
    time_of = lambda n: nchunk - 1 - n

    def body(x_ref, xh_ref, keep_ref, cw_ref, cb_ref, wa_ref, ba_ref, wx_ref, bx_ref, lam_ref,
             hr_ref, hrh_ref, dhr_ref,
             dx_ref, gcw_ref, gcb_ref, gwa_ref, gba_ref, gwx_ref, gbx_ref, glam_ref,
             g_carry, dxc_tail, a_scr, b_scr, spread):
        n = pl.program_id(1)
        first_in_time = n == nchunk - 1

        @pl.when(n == 0)
        def _():
            g_carry[...] = jnp.zeros_like(g_carry)
            dxc_tail[...] = jnp.zeros_like(dxc_tail)
            for ref in (gcw_ref, gcb_ref, gwa_ref, gba_ref, gwx_ref, gbx_ref, glam_ref):
                ref[...] = jnp.zeros_like(ref)

        xr = x_ref[...]
        head = jnp.where(first_in_time, 0.0, xh_ref[...])
        taps = _conv_taps(xr, head)
        cw = cw_ref[...]
        xc = cb_ref[...] + sum(cw[k:k + 1, :] * taps[k] for k in range(4))
        wa, wx, lam = wa_ref[...], wx_ref[...], lam_ref[...]
        r, i, cl, a_raw, mult_raw, a, mult, live = _rnn_gates(xc, wa, ba_ref[...], wx, bx_ref[...], lam,
                                                               keep_ref[...])
        h_prev = _shift_down(hr_ref[...], 1, jnp.where(first_in_time, 0.0, hrh_ref[...]))

        row = lax.broadcasted_iota(jnp.int32, xr.shape, 0)
        last = row == rows - 1
        a_next = jnp.where(last, 0.0, pltpu.roll(a, rows - 1, 0))
        g, g_first = _scan(a_next, dhr_ref[...] + jnp.where(last, g_carry[0:1, :], 0.0),
                           jnp.zeros((1, 128), F32), True, a_scr, b_scr, spread)
        g_carry[...] = jnp.broadcast_to(a[0:1, :] * g_first, g_carry.shape)

        da = g * h_prev
        dmult = g * i * xc
        di = g * mult * xc
        dxc = g * mult * i
        dlog_a = jnp.where(live, da * a_raw - dmult * a_raw * a_raw / mult_raw, 0.0)
        dpa = (dlog_a * cl) * r * (1.0 - r)
        dpx = di * i * (1.0 - i)
        glam_ref[...] += jnp.sum(dlog_a * r, axis=0, keepdims=True) * (LRU_C * _sigmoid(-lam))
        xcb, dpab, dpxb = xc.astype(BF16), dpa.astype(BF16), dpx.astype(BF16)
        gwa_ref[...] += _dot_tn(xcb, dpab)
        gwx_ref[...] += _dot_tn(xcb, dpxb)
        gba_ref[...] += jnp.sum(dpa, axis=0, keepdims=True)
        gbx_ref[...] += jnp.sum(dpx, axis=0, keepdims=True)
        dxc = dxc + _dot_nt(dpab, wa.astype(BF16)) + _dot_nt(dpxb, wx.astype(BF16))

        gcb_ref[...] += jnp.sum(dxc, axis=0, keepdims=True)
        for k in range(4):
            gcw_ref[k:k + 1, :] += jnp.sum(dxc * taps[k], axis=0, keepdims=True)
        tail = dxc_tail[...]
        dx = cw[3:4, :] * dxc
        for k in range(3):
            dx = dx + cw[k:k + 1, :] * _shift_up(dxc, 3 - k, tail)
        dx_ref[...] = dx.astype(BF16)
        dxc_tail[...] = dxc[0:SUBLANES, :]

    blk = lambda hb, n: (hb, 0, 0)
    chunk = pl.BlockSpec((rows, 128), lambda hb, n: (time_of(n), hb))
    vec_out = pl.BlockSpec((None, 1, 128), blk)
    mat_out = pl.BlockSpec((None, 128, 128), blk)
    vec_shape = jax.ShapeDtypeStruct((RNN_BLOCKS, 1, 128), F32)
    mat_shape = jax.ShapeDtypeStruct((RNN_BLOCKS, 128, 128), F32)
    return pl.pallas_call(
        body, name="rnn_bwd",
        out_shape=[jax.ShapeDtypeStruct((seq, D_MODEL), BF16),
                   jax.ShapeDtypeStruct((RNN_BLOCKS, SUBLANES, 128), F32), vec_shape,
                   mat_shape, vec_shape, mat_shape, vec_shape, vec_shape],
        grid=(RNN_BLOCKS, nchunk),
        in_specs=_rnn_specs(seq, rows, time_of) + [
            chunk, pl.BlockSpec((SUBLANES, 128), lambda hb, n: (jnp.maximum(time_of(n) * per - 1, 0), hb)), chunk],
        out_specs=[chunk, pl.BlockSpec((None, SUBLANES, 128), blk), vec_out,
                   mat_out, vec_out, mat_out, vec_out, vec_out],
        scratch_shapes=[pltpu.VMEM((SUBLANES, 128), F32), pltpu.VMEM((SUBLANES, 128), F32)]
                       + [pltpu.VMEM((rows, 128), F32)] * 3,
        compiler_params=_params(("arbitrary", "arbitrary"), VMEM_LIMIT),
    )(pf, pf, keep, conv_w8, conv_b, w_a, b_a, w_x, b_x, lam, hr, hr, dhr)


def _unit_rows(dil, r, j):
    start = j * KEY_BLOCK * dil + r
    return pl.ds(start, KEY_BLOCK) if dil == 1 else pl.ds(start, KEY_BLOCK, stride=dil)


def _attn_fwd(proj):
    nh, seq = N_HEADS, proj.shape[0]
    nchunk = seq // SPAN
    nblk = SPAN // KEY_BLOCK
    wide = DILATIONS[-1]

    def body(q_ref, k_ref, v_ref, kp_ref, vp_ref, o_ref, l1_ref, l4_ref, l16_ref,
             acc, m_s, l_s, q16, k16, v16, k16p, v16p, acc16, m16, l16, tmp):
        n = pl.program_id(1)
        qi = lax.broadcasted_iota(jnp.int32, (KEY_BLOCK, KEY_BLOCK), 0)
        ki = lax.broadcasted_iota(jnp.int32, (KEY_BLOCK, KEY_BLOCK), 1)
        bias_own = jnp.where(ki <= qi, 0.0, NEG_INF)
        bias_before = jnp.where(ki >= qi, 0.0, NEG_INF)
        bias_mid = jnp.concatenate([bias_before, bias_own], axis=1)
        bias_first = jnp.concatenate([jnp.where(n > 0, bias_before, NEG_INF), bias_own], axis=1)
        ones = jnp.ones((2 * KEY_BLOCK, HEAD_DIM), BF16)
        diag = qi == ki

        @pl.when(n == 0)
        def _():
            k16p[...] = jnp.zeros_like(k16p)
            v16p[...] = jnp.zeros_like(v16p)

        def unit(qf, kpb, kb, vpb, vb, bias, state, rows, first):
            acc_r, m_r, l_r = state
            kcat = jnp.concatenate([kpb, kb], axis=0)
            vaug = jnp.concatenate([jnp.concatenate([vpb, vb], axis=0), ones], axis=1)
            s = _dot_nt(qf.astype(BF16), kcat) + bias
            mx = jnp.max(s, axis=-1, keepdims=True)
            if first:
                m_new = jnp.broadcast_to(mx, (KEY_BLOCK, HEAD_DIM))
            else:
                m_old = m_r[rows, :]
                m_new = jnp.maximum(m_old, mx)
            pv = _dot(jnp.exp(s - jnp.concatenate([m_new, m_new], axis=1)).astype(BF16), vaug)
            if first:
                acc_r[rows, :] = pv[:, :HEAD_DIM]
                l_r[rows, :] = pv[:, HEAD_DIM:]
            else:
                alpha = jnp.exp(m_old - m_new)
                acc_r[rows, :] = alpha * acc_r[rows, :] + pv[:, :HEAD_DIM]
                l_r[rows, :] = alpha * l_r[rows, :] + pv[:, HEAD_DIM:]
            m_r[rows, :] = m_new

        for gi, dil in enumerate(DILATIONS[:-1]):
            nb = nblk // dil
            for r in range(dil):
                prow = _unit_rows(dil, r, nb - 1)
                kpb, vpb = kp_ref[prow, :].astype(BF16), vp_ref[prow, :].astype(BF16)
                for j in range(nb):
                    rows = _unit_rows(dil, r, j)
                    kb, vb = k_ref[rows, :].astype(BF16), v_ref[rows, :].astype(BF16)
                    unit(q_ref[rows, :], kpb, kb, vpb, vb, bias_first if j == 0 else bias_mid,
                         (acc, m_s, l_s), rows, gi == 0)
                    kpb, vpb = kb, vb

        for src, dst in ((q_ref, q16), (k_ref, k16), (v_ref, v16), (acc, acc16), (m_s, m16), (l_s, l16)):
            _to_residue_major(src, tmp, dst)
        for r in range(wide):
            rows = pl.ds(r * KEY_BLOCK, KEY_BLOCK)
            unit(q16[rows, :], k16p[rows, :].astype(BF16), k16[rows, :].astype(BF16), v16p[rows, :].astype(BF16),
                 v16[rows, :].astype(BF16), bias_first, (acc16, m16, l16), rows, False)
        k16p[...] = k16[...]
        v16p[...] = v16[...]

        den = l16[...]
        acc16[...] = acc16[...] * (1.0 / den)
        m16[...] = m16[...] + jnp.log(den)
        _from_residue_major(acc16, tmp, o_ref, False)
        _from_residue_major(m16, tmp, m_s, False)

        def lse_row(ref, rows):
            return jnp.sum(jnp.where(diag, ref[rows, :], 0.0), axis=0, keepdims=True)

        for dil, out in zip(DILATIONS[:-1], (l1_ref, l4_ref)):
            nb = nblk // dil
            for r in range(dil):
                for j in range(nb):
                    out[r * nb + j:r * nb + j + 1, :] = lse_row(m_s, _unit_rows(dil, r, j))
        for r in range(wide):
            l16_ref[r:r + 1, :] = lse_row(m16, pl.ds(r * KEY_BLOCK, KEY_BLOCK))

    cur = lambda piece: pl.BlockSpec((SPAN, HEAD_DIM), lambda h, n: (n, piece * nh + h))
    before = lambda piece: pl.BlockSpec((SPAN, HEAD_DIM), lambda h, n: (jnp.maximum(n - 1, 0), piece * nh + h))
    blk = pl.BlockSpec((None, SPAN, HEAD_DIM), lambda h, n: (h, n, 0))
    lblk = pl.BlockSpec((None, nblk, KEY_BLOCK), lambda h, n: (h, n, 0))
    lshape = jax.ShapeDtypeStruct((nh, seq // KEY_BLOCK, KEY_BLOCK), F32)
    o, l1, l4, l16 = pl.pallas_call(
        body, name="attn_fwd",
        out_shape=[jax.ShapeDtypeStruct((nh, seq, HEAD_DIM), F32), lshape, lshape, lshape],
        grid=(nh, nchunk), in_specs=[cur(2), cur(3), cur(4), before(3), before(4)],
        out_specs=[blk, lblk, lblk, lblk],
        scratch_shapes=[pltpu.VMEM((SPAN, HEAD_DIM), F32)] * 12,
        compiler_params=_params(("arbitrary", "arbitrary"), VMEM_LIMIT),
    )(proj, proj, proj, proj, proj)
    return o, (l1, l4, l16)


def _to_residue_major(src, tmp, dst):
    quarter = SPAN // 4
    for r4 in range(4):
        tmp[r4 * quarter:(r4 + 1) * quarter, :] = src[pl.ds(r4, quarter, stride=4), :]
    for r4 in range(4):
        for rp in range(4):
            r = r4 + 4 * rp
            dst[r * KEY_BLOCK:(r + 1) * KEY_BLOCK, :] = tmp[pl.ds(r4 * quarter + rp, KEY_BLOCK, stride=4), :]


def _from_residue_major(src, tmp, dst, add):
    quarter = SPAN // 4
    for r4 in range(4):
        for rp in range(4):
            r = r4 + 4 * rp
            tmp[pl.ds(r4 * quarter + rp, KEY_BLOCK, stride=4), :] = src[r * KEY_BLOCK:(r + 1) * KEY_BLOCK, :]
    for r4 in range(4):
        rows = pl.ds(r4, quarter, stride=4)
        part = tmp[r4 * quarter:(r4 + 1) * quarter, :]
        dst[rows, :] = dst[rows, :] + part if add else part


def _attn_bwd(proj, do, o, lses, cosf, sinf):
    nh, seq = N_HEADS, proj.shape[0]
    nchunk = seq // SPAN
    nblk = SPAN // KEY_BLOCK
    wide = DILATIONS[-1]
    assert SPAN == wide * KEY_BLOCK

    def body(q_ref, k_ref, v_ref, do_ref, o_ref, kp_ref, vp_ref, l1_ref, l4_ref, l16_ref,
             cos_ref, sin_ref, cosp_ref, sinp_ref, dq_ref, dk_ref, dv_ref,
             dq_acc, dkc_acc, dvc_acc, dkp_acc, dvp_acc, q16, k16, v16, do16, o16, k16p, v16p,
             dq16, dkc16, dvc16, dkp16, dvp16, tmp, pt_s, ds_s, kcat_s, qb_s, dob_s):
        n = pl.program_id(1)
        ki = lax.broadcasted_iota(jnp.int32, (KEY_BLOCK, KEY_BLOCK), 0)
        qi = lax.broadcasted_iota(jnp.int32, (KEY_BLOCK, KEY_BLOCK), 1)
        bias_own = jnp.where(ki <= qi, 0.0, NEG_INF)
        bias_before = jnp.where(ki >= qi, 0.0, NEG_INF)
        bias_mid = jnp.concatenate([bias_before, bias_own], axis=0)
        bias_first = jnp.concatenate([jnp.where(n > 0, bias_before, NEG_INF), bias_own], axis=0)
        ones8 = jnp.ones((SUBLANES, HEAD_DIM), BF16)

        def row_dot(a, b):
            prod = a * b
            hi = prod.astype(BF16)
            lo = (prod - hi.astype(F32)).astype(BF16)
            return (_dot_nt(ones8, hi) + _dot_nt(ones8, lo))[0:1, :]

        def group(units, srcs, before, l_ref, accs):
            src_q, src_do, src_o, src_k, src_v = srcs
            before_k, before_v = before
            acc_q, acc_kc, acc_vc, acc_kp, acc_vp = accs
            kb = vb = None
            for u, (rows, prow, outside, lrow, _) in enumerate(units):
                dof = src_do[rows, :]
                qb, dob = src_q[rows, :].astype(BF16), dof.astype(BF16)
                kpb, vpb = (before_k[prow, :].astype(BF16), before_v[prow, :].astype(BF16)) if outside else (kb, vb)
                kb, vb = src_k[rows, :].astype(BF16), src_v[rows, :].astype(BF16)
                kcat = jnp.concatenate([kpb, kb], axis=0)
                vcat = jnp.concatenate([vpb, vb], axis=0)
                bias = bias_first if outside else bias_mid
                pt = jnp.exp(_dot_nt(kcat, qb) + bias - l_ref[lrow:lrow + 1, :])
                dst = pt * (_dot_nt(vcat, dob) - row_dot(dof, src_o[rows, :]))
                pt_s[u], ds_s[u], kcat_s[u], qb_s[u], dob_s[u] = pt.astype(BF16), dst.astype(BF16), kcat, qb, dob
            for u, (rows, _, _, _, _) in enumerate(units):
                acc_q[rows, :] += _dot_tn(ds_s[u], kcat_s[u])
            for u, (rows, prow, outside, _, nxt) in enumerate(units):
                dk = _dot(ds_s[u, KEY_BLOCK:, :], qb_s[u])
                dv = _dot(pt_s[u, KEY_BLOCK:, :], dob_s[u])
                if nxt is not None:
                    dk = dk + _dot(ds_s[nxt, :KEY_BLOCK, :], qb_s[nxt])
                    dv = dv + _dot(pt_s[nxt, :KEY_BLOCK, :], dob_s[nxt])
                acc_kc[rows, :] += dk
                acc_vc[rows, :] += dv
                if outside:
                    acc_kp[prow, :] += _dot(ds_s[u, :KEY_BLOCK, :], qb_s[u])
                    acc_vp[prow, :] += _dot(pt_s[u, :KEY_BLOCK, :], dob_s[u])

        @pl.when(n == 0)
        def _():
            for ref in (dkp_acc, dvp_acc, dkp16, dvp16, k16p, v16p):
                ref[...] = jnp.zeros_like(ref)

        @pl.when(n < nchunk)
        def _():
            for ref in (dq_acc, dkc_acc, dvc_acc, dq16, dkc16, dvc16):
                ref[...] = jnp.zeros_like(ref)
            for src, dst in ((q_ref, q16), (k_ref, k16), (v_ref, v16), (do_ref, do16), (o_ref, o16)):
                _to_residue_major(src, tmp, dst)
            natural = (q_ref, do_ref, o_ref, k_ref, v_ref)
            for dil, l_ref in zip(DILATIONS[:-1], (l1_ref, l4_ref)):
                nb = nblk // dil
                units = [(_unit_rows(dil, r, j), _unit_rows(dil, r, (j - 1) % nb), j == 0, r * nb + j,
                          r * nb + j + 1 if j + 1 < nb else None) for r in range(dil) for j in range(nb)]
                group(units, natural, (kp_ref, vp_ref), l_ref, (dq_acc, dkc_acc, dvc_acc, dkp_acc, dvp_acc))
            blocks = [pl.ds(r * KEY_BLOCK, KEY_BLOCK) for r in range(wide)]
            group([(rows, rows, True, r, None) for r, rows in enumerate(blocks)], (q16, do16, o16, k16, v16),
                  (k16p, v16p), l16_ref, (dq16, dkc16, dvc16, dkp16, dvp16))
            _from_residue_major(dq16, tmp, dq_acc, True)
            dq = dq_acc[...]
            dq_ref[...] = ((dq * cos_ref[...] - _rope_partner(dq) * sin_ref[...]) * ATTN_SCALE).astype(BF16)

        @pl.when(n > 0)
        def _():
            _from_residue_major(dkp16, tmp, dkp_acc, True)
            _from_residue_major(dvp16, tmp, dvp_acc, True)
            dk = dkp_acc[...]
            dk_ref[...] = (dk * cosp_ref[...] - _rope_partner(dk) * sinp_ref[...]).astype(BF16)
            dv_ref[...] = dvp_acc[...].astype(BF16)

        @pl.when(n < nchunk)
        def _():
            for src, dst in ((dkc_acc, dkp_acc), (dvc_acc, dvp_acc), (dkc16, dkp16), (dvc16, dvp16),
                             (k16, k16p), (v16, v16p)):
                dst[...] = src[...]

    last = nchunk - 1
    cur = lambda h, n: (h, jnp.minimum(n, last), 0)
    prev = lambda h, n: (h, jnp.clip(n - 1, 0, last), 0)
    blk = lambda idx: pl.BlockSpec((None, SPAN, HEAD_DIM), idx)
    lblk = pl.BlockSpec((None, nblk, KEY_BLOCK), cur)
    tab = pl.BlockSpec((SPAN, HEAD_DIM), lambda h, n: (jnp.minimum(n, last), 0))
    tabp = pl.BlockSpec((SPAN, HEAD_DIM), lambda h, n: (jnp.clip(n - 1, 0, last), 0))
    out_q = pl.BlockSpec((SPAN, HEAD_DIM), lambda h, n: (jnp.minimum(n, last), h))
    out_kv = pl.BlockSpec((SPAN, HEAD_DIM), lambda h, n: (jnp.clip(n - 1, 0, last), h))
    shape = jax.ShapeDtypeStruct((seq, nh * HEAD_DIM), BF16)
    tok = lambda piece, row: pl.BlockSpec((SPAN, HEAD_DIM), lambda h, n: (row(n), piece * nh + h))
    row_cur, row_prev = (lambda n: jnp.minimum(n, last)), (lambda n: jnp.clip(n - 1, 0, last))
    return pl.pallas_call(
        body, name="attn_bwd", out_shape=[shape, shape, shape], grid=(nh, nchunk + 1),
        in_specs=[tok(2, row_cur), tok(3, row_cur), tok(4, row_cur), blk(cur), blk(cur),
                  tok(3, row_prev), tok(4, row_prev)] + [lblk] * 3 + [tab, tab, tabp, tabp],
        out_specs=[out_q, out_kv, out_kv],
        scratch_shapes=[pltpu.VMEM((SPAN, HEAD_DIM), F32)] * 18
                       + [pltpu.VMEM((nblk, 2 * KEY_BLOCK, HEAD_DIM), BF16)] * 3
                       + [pltpu.VMEM((nblk, KEY_BLOCK, HEAD_DIM), BF16)] * 2,
        compiler_params=_params(("arbitrary", "arbitrary"), VMEM_LIMIT),
    )(proj, proj, proj, do, o, proj, proj, *lses, cosf, sinf, cosf, sinf)


def _hub(x, tgt, hr, pf, o_hm, mod, b_mod, b_gate, g_final, w_out_rnn, w_out_attn, w_o):
    seq = x.shape[0]
    tm = HUB_ROWS
    nsteps = seq // tm

    def body(x_ref, t_ref, hr_ref, zr_ref, za_ref, gr_ref, ga_ref, o_ref, mod_ref, bmod_ref, bg_ref, gf_ref,
             wr_hbm, wa_hbm, wo_hbm,
             dx2_ref, dhr_ref, dzr_ref, do_ref, dza_ref, dgr_ref, dga_ref,
             ur_ref, dyr_ref, ua_ref, dya_ref, mg_ref, dmo_ref,
             ggf_ref, gbg_ref, dgate_ref, loss_ref,
             wr, wa, wo, sem):
        step = pl.program_id(0)

        @pl.when(step == 0)
        def _():
            for src, dst in ((wr_hbm, wr), (wa_hbm, wa), (wo_hbm, wo)):
                cp = pltpu.make_async_copy(src, dst, sem)
                cp.start()
                cp.wait()
            for ref in (ggf_ref, gbg_ref, dgate_ref, loss_ref):
                ref[...] = jnp.zeros_like(ref)

        gate = mod_ref[:, 2 * D_MODEL:] + bmod_ref[:, 2 * D_MODEL:]
        gfin = gf_ref[...]
        hr_t, zr, za = hr_ref[...], zr_ref[...], za_ref[...]
        o = jnp.concatenate([o_ref[hh] for hh in range(N_HEADS)], axis=1)
        sig_zr, sig_za = _sigmoid(zr), _sigmoid(za)
        silu_zr, silu_za = zr * sig_zr, za * sig_za
        u_rnn = (hr_t * silu_zr).astype(BF16)
        u_attn = (o * silu_za).astype(BF16)
        y_rnn = _dot(u_rnn, wr[...])
        y_attn = _dot(u_attn, wa[...])
        sr = _sigmoid(gr_ref[...] + bg_ref[:, :D_MODEL])
        sa = _sigmoid(ga_ref[...] + bg_ref[:, D_MODEL:])
        merged = (sr * y_rnn + sa * y_attn).astype(BF16)
        mo = _dot(merged, wo[...])
        x2 = x_ref[...] + gate * mo
        rstd = lax.rsqrt(jnp.mean(x2 * x2, axis=-1, keepdims=True) + NORM_EPS)
        xn = x2 * rstd
        err = xn * gfin - t_ref[...]
        loss_ref[...] += 0.5 * jnp.sum(jnp.sum(err * err, axis=-1, keepdims=True) * (1.0 / D_MODEL),
                                       axis=0, keepdims=True)

        dy = err * (1.0 / D_MODEL)
        ggf_ref[...] += jnp.sum(dy * xn, axis=0, keepdims=True)
        dxn = dy * gfin
        dx2 = rstd * (dxn - xn * jnp.mean(dxn * xn, axis=-1, keepdims=True))
        dx2_ref[...] = dx2
        dgate_ref[...] += jnp.sum(dx2 * mo, axis=0, keepdims=True)
        dmo = (dx2 * gate).astype(BF16)
        dmerged = _dot_nt(dmo, wo[...])
        mg_ref[...] = merged
        dmo_ref[...] = dmo
        dy_rnn = (dmerged * sr).astype(BF16)
        dy_attn = (dmerged * sa).astype(BF16)
        dg_r = dmerged * y_rnn * sr * (1.0 - sr)
        dg_a = dmerged * y_attn * sa * (1.0 - sa)
        dgr_ref[...] = dg_r.astype(BF16)
        dga_ref[...] = dg_a.astype(BF16)
        gbg_ref[:, :D_MODEL] += jnp.sum(dg_r, axis=0, keepdims=True)
        gbg_ref[:, D_MODEL:] += jnp.sum(dg_a, axis=0, keepdims=True)
        du_rnn = _dot_nt(dy_rnn, wr[...])
        du_attn = _dot_nt(dy_attn, wa[...])
        ur_ref[...] = u_rnn
        dyr_ref[...] = dy_rnn
        ua_ref[...] = u_attn
        dya_ref[...] = dy_attn
        dhr_ref[...] = du_rnn * silu_zr
        dzr_ref[...] = (du_rnn * hr_t * (sig_zr * (1.0 + zr * (1.0 - sig_zr)))).astype(BF16)
        dza_ref[...] = (du_attn * o * (sig_za * (1.0 + za * (1.0 - sig_za)))).astype(BF16)
        d_o = du_attn * silu_za
        for hh in range(N_HEADS):
            do_ref[hh] = d_o[:, hh * HEAD_DIM:(hh + 1) * HEAD_DIM]

    row = pl.BlockSpec((tm, D_MODEL), lambda i: (i, 0))
    piece = lambda slot: pl.BlockSpec((tm, D_MODEL), lambda i: (i, slot))
    hm = pl.BlockSpec((N_HEADS, tm, HEAD_DIM), lambda i: (0, i, 0))
    const = lambda cols: pl.BlockSpec((1, cols), lambda i: (0, 0))
    any_spec = pl.BlockSpec(memory_space=pl.ANY)
    act_f32 = jax.ShapeDtypeStruct((seq, D_MODEL), F32)
    act_bf16 = jax.ShapeDtypeStruct((seq, D_MODEL), BF16)
    return pl.pallas_call(
        body, name="hub",
        out_shape=[act_f32, act_f32, act_bf16, jax.ShapeDtypeStruct((N_HEADS, seq, HEAD_DIM), F32),
                   act_bf16, act_bf16, act_bf16] + [act_bf16] * 6 + [
                   jax.ShapeDtypeStruct((1, D_MODEL), F32), jax.ShapeDtypeStruct((1, 2 * D_MODEL), F32),
                   jax.ShapeDtypeStruct((1, D_MODEL), F32), jax.ShapeDtypeStruct((1, 1), F32)],
        grid=(nsteps,),
        in_specs=[row, row, row, piece(1), piece(5), piece(6), piece(7), hm,
                  const(3 * D_MODEL), const(3 * D_MODEL), const(2 * D_MODEL), const(D_MODEL),
                  any_spec, any_spec, any_spec],
        out_specs=[row, row, row, hm, row, row, row] + [row] * 6 + [
                   const(D_MODEL), const(2 * D_MODEL), const(D_MODEL), const(1)],
        scratch_shapes=[pltpu.VMEM((D_MODEL, D_MODEL), BF16)] * 3 + [pltpu.SemaphoreType.DMA],
        compiler_params=_params(("arbitrary",), VMEM_LIMIT),
    )(x, tgt, hr, pf, pf, pf, pf, o_hm, mod, b_mod, b_gate, g_final, w_out_rnn, w_out_attn, w_o)


def _pair_grads(name, lefts, rights):
    n = len(rights)
    shared = len(lefts) == 1
    seq = rights[0].shape[0]
    tk = WGRAD_ROWS
    nk = seq // tk

    def body(*refs):
        l_refs, r_refs = refs[:len(lefts)], refs[len(lefts):len(lefts) + n]
        out_ref, low_ref = refs[len(lefts) + n:]
        j, kk = pl.program_id(0), pl.program_id(1)

        @pl.when(kk == 0)
        def _():
            out_ref[...] = jnp.zeros_like(out_ref)

        for m in range(n):
            @pl.when(j == m)
            def _(m=m):
                out_ref[...] += _dot_tn(l_refs[0 if shared else m][...], r_refs[m][...])

        @pl.when(kk == nk - 1)
        def _():
            low_ref[...] = out_ref[...].astype(BF16)

    def spec(m):
        return pl.BlockSpec((tk, D_MODEL), lambda j, kk: (jnp.where(j == m, kk, jnp.where(j < m, 0, nk - 1)), 0))

    left_specs = [pl.BlockSpec((tk, D_MODEL), lambda j, kk: (kk, 0))] if shared else [spec(m) for m in range(n)]
    out_spec = pl.BlockSpec((None, D_MODEL, D_MODEL), lambda j, kk: (j, 0, 0))
    return pl.pallas_call(
        body, name=name,
        out_shape=[jax.ShapeDtypeStruct((n, D_MODEL, D_MODEL), F32), jax.ShapeDtypeStruct((n, D_MODEL, D_MODEL), BF16)],
        grid=(n, nk),
        in_specs=left_specs + [spec(m) for m in range(n)],
        out_specs=[out_spec, out_spec],
        compiler_params=_params(("arbitrary", "arbitrary"), VMEM_LIMIT),
    )(*lefts, *rights)


def _dh_dx(pieces, w_in_all, x, dx2, mod, b_mod, g_norm):
    seq = x.shape[0]
    tm = DX_ROWS

    def body(*refs):
        p_refs = refs[:8]
        w_hbm, x_ref, dx2_ref, mod_ref, bmod_ref, g_ref = refs[8:14]
        gx_ref, dshift_ref, dscale_ref, ggn_ref, w_scr, sem = refs[14:]
        step = pl.program_id(0)

        @pl.when(step == 0)
        def _():
            cp = pltpu.make_async_copy(w_hbm, w_scr, sem)
            cp.start()
            cp.wait()
            for ref in (dshift_ref, dscale_ref, ggn_ref):
                ref[...] = jnp.zeros_like(ref)

        dh = _dot_nt(p_refs[0][...], w_scr[0])
        for j in range(1, 8):
            dh = dh + _dot_nt(p_refs[j][...], w_scr[j])
        scale1 = 1.0 + mod_ref[:, D_MODEL:2 * D_MODEL] + bmod_ref[:, D_MODEL:2 * D_MODEL]
        g = g_ref[...]
        xf = x_ref[...]
        rstd_t = lax.rsqrt(jnp.mean(xf * xf, axis=-1, keepdims=True) + NORM_EPS)
        xn = xf * rstd_t
        dshift_ref[...] += jnp.sum(dh, axis=0, keepdims=True)
        dscale_ref[...] += jnp.sum(dh * (xn * g), axis=0, keepdims=True)
        ggn_ref[...] += jnp.sum(dh * scale1 * xn, axis=0, keepdims=True)
        dxn = dh * (g * scale1)
        gx_ref[...] = rstd_t * (dxn - xn * jnp.mean(dxn * xn, axis=-1, keepdims=True)) + dx2_ref[...]

    row = pl.BlockSpec((tm, D_MODEL), lambda i: (i, 0))
    const = lambda cols: pl.BlockSpec((1, cols), lambda i: (0, 0))
    vec = jax.ShapeDtypeStruct((1, D_MODEL), F32)
    return pl.pallas_call(
        body, name="dh_dx",
        out_shape=[jax.ShapeDtypeStruct((seq, D_MODEL), F32), vec, vec, vec],
        grid=(seq // tm,),
        in_specs=[row] * 8 + [pl.BlockSpec(memory_space=pl.ANY), row, row,
                              const(3 * D_MODEL), const(3 * D_MODEL), const(D_MODEL)],
        out_specs=[row, const(D_MODEL), const(D_MODEL), const(D_MODEL)],
        scratch_shapes=[pltpu.VMEM((8, D_MODEL, D_MODEL), BF16), pltpu.SemaphoreType.DMA],
        compiler_params=_params(("arbitrary",), VMEM_LIMIT),
    )(*pieces, w_in_all, x, dx2, mod, b_mod, g_norm)


def _adamw(name, w, g, m, v, recv=None):
    rows, cols = w.shape
    tr = rows if rows <= 256 else 256

    def body(*refs):
        w_ref, g_ref, m_ref, v_ref = refs[:4]
        d_ref, nm_ref, nv_ref = refs[-3:] if recv is None else refs[5:8]
        gv = g_ref[...]
        if recv is not None:
            r_ref, g_out = refs[4], refs[8]
            gv = ((gv + r_ref[0].astype(F32)) + r_ref[1].astype(F32)) + r_ref[2].astype(F32)
            g_out[...] = gv
        nm = ADAM_B1 * m_ref[...] + (1.0 - ADAM_B1) * gv
        nv = ADAM_B2 * v_ref[...] + (1.0 - ADAM_B2) * (gv * gv)
        m_hat = nm / (1.0 - ADAM_B1 ** ADAM_STEP)
        v_hat = nv / (1.0 - ADAM_B2 ** ADAM_STEP)
        d_ref[...] = -ADAM_LR * (m_hat / (jnp.sqrt(v_hat) + ADAM_EPS) + ADAM_WD * w_ref[...])
        nm_ref[...] = nm
        nv_ref[...] = nv

    spec = pl.BlockSpec((tr, cols), lambda i: (i, 0))
    shape = jax.ShapeDtypeStruct((rows, cols), F32)
    if recv is None:
        return pl.pallas_call(
            body, name=name, out_shape=[shape, shape, shape], grid=(rows // tr,),
            in_specs=[spec] * 4, out_specs=[spec] * 3,
            compiler_params=_params(("arbitrary",)),
        )(w, g, m, v)
    return pl.pallas_call(
        body, name=name, out_shape=[shape] * 4, grid=(rows // tr,),
        in_specs=[spec] * 4 + [pl.BlockSpec((3, tr, cols), lambda i: (0, i, 0))], out_specs=[spec] * 4,
        compiler_params=_params(("arbitrary",)),
    )(w, g, m, v, recv)


def kernel(x, c, positions, g_norm, w_mod, b_mod, w_in, b_gate, conv_w, conv_b, w_a, b_a, w_x, b_x, lam, w_out_rnn, w_out_attn, w_o, g_final, loss_target, m_g_norm, m_w_mod, m_b_mod, m_w_in, m_b_gate, m_conv_w, m_conv_b, m_w_a, m_b_a, m_w_x, m_b_x, m_lam, m_w_out_rnn, m_w_out_attn, m_w_o, m_g_final, v_g_norm, v_w_mod, v_b_mod, v_w_in, v_b_gate, v_conv_w, v_conv_b, v_w_a, v_b_a, v_w_x, v_b_x, v_lam, v_w_out_rnn, v_w_out_attn, v_w_o, v_g_final):
    seq = x.shape[1]
    me = _index(_my_pos())
    xs, tgt = x[0], loss_target[0]

    pos = positions[0].astype(F32)[:, None]
    inv_freq = ROPE_THETA ** (-jnp.arange(0, 2 * ROT_HALF, 2, dtype=F32) / (2 * ROT_HALF))
    ang = pos * inv_freq
    rest = HEAD_DIM - 2 * ROT_HALF
    cosf = jnp.concatenate([jnp.cos(ang), jnp.cos(ang), jnp.ones((seq, rest), F32)], axis=1)
    sinf = jnp.concatenate([-jnp.sin(ang), jnp.sin(ang), jnp.zeros((seq, rest), F32)], axis=1)
    keep = (positions[0] != 0).astype(F32)[:, None]

    both = _ag_small("gather_c_conv_w", jnp.concatenate(
        [jnp.broadcast_to(c, (SUBLANES, D_MODEL)), jnp.pad(conv_w[0], ((0, SUBLANES - 4), (0, 0)))], axis=1))
    c_all, conv_w8 = both[:, 0, :D_MODEL], both[:, :, D_MODEL:]
    mod_cols = w_mod.shape[2]
    mod_part = _ag_small("gather_mod", _mod_fwd(c_all, w_mod[0]))
    mod = lax.dynamic_index_in_dim(mod_part, me, axis=1, keepdims=False).reshape(1, N_DEV * mod_cols)

    slot = lambda t: lax.dynamic_update_slice(lax.empty((N_DEV,) + t.shape, t.dtype), t[None], (me, 0, 0))
    w_in_own = w_in[0].astype(BF16)
    mod, w_in_own = lax.optimization_barrier((mod, w_in_own))
    first = _split_start("gather_w_in_start", _own_block_copies, 4, [w_in_own], [slot(w_in_own)])
    mod = mod + first[4][0:1, 0:1]

    blocks = lambda t: t.reshape(RNN_BLOCKS, 1, 128)
    rnn_params = (conv_w8, blocks(conv_b), w_a[0], blocks(b_a), w_x[0], blocks(b_x), blocks(lam))

    h = _norm(xs, mod, b_mod, g_norm)
    ids = lambda ks: jnp.bitwise_xor(me, jnp.array(ks, jnp.int32)).astype(jnp.int32)
    pf = _proj("proj_own", h, first[2][0][None], jnp.zeros((1,), jnp.int32), ids([0]), cosf, sinf, None)
    _, (w_in_near,) = _split_wait("gather_w_in_wait", _own_block_copies, first, pf)
    second = _split_start("forward_w_in_start", _forward_copies, 3, [w_in_near],
                          [lax.empty(w_in_near.shape, w_in_near.dtype)])
    near = ids([1, 2, 4, 6])
    pf = _proj("proj_near", h, second[2][0], near, near, cosf, sinf, pf)
    (w_in_near,), (w_in_far,) = _split_wait("forward_w_in_wait", _forward_copies, second, pf)
    far = ids([3, 5, 7])
    pf = _proj("proj_far", h, w_in_far, far, far, cosf, sinf, pf)
    w_in_all = _merge_slots("merge_w_in", w_in_near, w_in_far)
    late = [w_out_rnn[0].astype(BF16), w_out_attn[0].astype(BF16), w_o[0].astype(BF16)]
    pf, late = lax.optimization_barrier((pf, late))
    flight = _split_start("gather_out_weights_start", _peer_copies, 7 * len(late), late, [slot(t) for t in late])
    rnn_params = (rnn_params[0], rnn_params[1] + flight[4][0:1, 0:1]) + rnn_params[2:]
    hr = _rnn_fwd(pf, keep, *rnn_params)
    o, lses = _attn_fwd(pf)

    w_or_all, w_oa_all, w_o_all = (t.reshape(D_MODEL, D_MODEL) for t in _split_wait(
        "gather_out_weights_wait", _peer_copies, flight, o)[1])
    (dx2, dhr, dz_rnn, d_o, dz_attn, dg_r, dg_a, u_rnn, dy_rnn, u_attn, dy_attn, merged, dmo,
     gp_g_final, gp_b_gate, dgate, loss_part) = _hub(
        xs, tgt, hr, pf, o, mod, b_mod, b_gate, g_final.reshape(1, D_MODEL), w_or_all, w_oa_all, w_o_all)
    gp_out, gp_out_low = _pair_grads("out_grads", [u_rnn, u_attn, merged], [dy_rnn, dy_attn, dmo])
    dq, dk, dv = _attn_bwd(pf, d_o, o, lses, cosf, sinf)
    dx_rnn, gp_conv_w, gp_conv_b, gp_w_a, gp_b_a, gp_w_x, gp_b_x, gp_lam = _rnn_bwd(pf, hr, dhr, keep, *rnn_params)
    pieces = [dx_rnn, dz_rnn, dq, dk, dv, dz_attn, dg_r, dg_a]
    gp_w_in, gp_w_in_low = _pair_grads("w_in_grad", [h], pieces)

    by_target = lambda t: [t[i].reshape(N_DEV, 128, D_MODEL) for i in range(3)]
    stacks = [gp_w_in] + by_target(gp_out)
    from_sib = _rs_to_sibling("rs_sibling", [gp_w_in_low] + by_target(gp_out_low))
    targets = jnp.bitwise_xor(me, 2 * jnp.arange(4, dtype=jnp.int32)).astype(jnp.int32)
    sums = [_add_sibling("rs_add_sibling_%d" % a, s_, r_, targets) for a, (s_, r_) in enumerate(zip(stacks, from_sib))]
    sends = [send for _, send in sums]
    reduce_flight = _split_start("rs_chips_start", _chip_copies, 3 * len(sends), sends,
                                 [lax.empty(t.shape, t.dtype) for t in sends])

    mod_after = mod + reduce_flight[4][0:1, 0:1]
    grad_x, dshift, dscale, gp_g_norm = _dh_dx(pieces, w_in_all, xs, dx2, mod_after, b_mod, g_norm)

    dmod = jnp.concatenate([dshift, dscale, dgate], axis=1)
    dmod_all = _ag_small("gather_dmod", jnp.broadcast_to(dmod, (SUBLANES, 3 * D_MODEL)))[:, 0, :]
    dmod_cols = lax.dynamic_slice_in_dim(dmod_all, me * mod_cols, mod_cols, axis=1)
    g_b_mod, g_w_mod = _mod_bwd(c_all, dmod_all, dmod_cols)

    flat = lambda t: t.reshape(-1, 128)
    small = [flat(gp_g_norm), flat(gp_b_gate), flat(gp_conv_b), flat(gp_b_a), flat(gp_b_x), flat(gp_lam),
             flat(gp_g_final), flat(gp_conv_w), jnp.broadcast_to(loss_part, (SUBLANES, 128)),
             flat(gp_w_a), flat(gp_w_x)]
    sizes = [t.shape[0] for t in small]
    small.append(jnp.zeros((-sum(sizes) % (2 * SUBLANES), 128), F32))
    total = _allreduce_small("allreduce_small_grads", jnp.concatenate(small, axis=0))
    offs = [sum(sizes[:i]) for i in range(len(sizes))]
    (g_g_norm, g_b_gate, g_conv_b, g_b_a, g_b_x, g_lam, g_g_final, g_conv_w_all, loss_rows, g_w_a, g_w_x) = (
        total[o_:o_ + s_] for o_, s_ in zip(offs, sizes))
    loss = loss_rows[0, 0]
    g_conv_w = lax.dynamic_index_in_dim(g_conv_w_all.reshape(RNN_BLOCKS, SUBLANES, 128), me, axis=0,
                                        keepdims=False)[:4]

    _, from_chips = _split_wait("rs_chips_wait", _chip_copies, reduce_flight, total)

    results = {}
    sharded = (("w_in", w_in, m_w_in, v_w_in, (D_MODEL, D_MODEL)),
               ("w_out_rnn", w_out_rnn, m_w_out_rnn, v_w_out_rnn, (128, D_MODEL)),
               ("w_out_attn", w_out_attn, m_w_out_attn, v_w_out_attn, (128, D_MODEL)),
               ("w_o", w_o, m_w_o, v_w_o, (128, D_MODEL)))
    for (name, w_, m_, v_, shape2), (own, _), arrived in zip(sharded, sums, from_chips):
        d_, nm_, nv_, g_ = _adamw("adamw_" + name, w_.reshape(shape2), own, m_.reshape(shape2), v_.reshape(shape2),
                                  arrived)
        results[name] = (g_, d_, nm_, nv_)
    shape2 = (D_MODEL, mod_cols)
    results["w_mod"] = (g_w_mod,) + tuple(_adamw("adamw_w_mod", w_mod.reshape(shape2), g_w_mod,
                                                 m_w_mod.reshape(shape2), v_w_mod.reshape(shape2)))
    lanes = (("g_norm", g_norm, g_g_norm, m_g_norm, v_g_norm), ("b_mod", b_mod, g_b_mod, m_b_mod, v_b_mod),
             ("b_gate", b_gate, g_b_gate, m_b_gate, v_b_gate), ("conv_w", conv_w, g_conv_w, m_conv_w, v_conv_w),
             ("conv_b", conv_b, g_conv_b, m_conv_b, v_conv_b), ("w_a", w_a, g_w_a, m_w_a, v_w_a),
             ("b_a", b_a, g_b_a, m_b_a, v_b_a), ("w_x", w_x, g_w_x, m_w_x, v_w_x), ("b_x", b_x, g_b_x, m_b_x, v_b_x),
             ("lam", lam, g_lam, m_lam, v_lam), ("g_final", g_final, g_g_final, m_g_final, v_g_final))
    for name, w_, g_, m_, v_ in lanes:
        rows128 = lambda t: t.reshape(-1, 128)
        results[name] = (g_,) + tuple(_adamw("adamw_" + name, rows128(w_), rows128(g_), rows128(m_), rows128(v_)))
    order = ("g_norm", "w_mod", "b_mod", "w_in", "b_gate", "conv_w", "conv_b", "w_a", "b_a", "w_x", "b_x", "lam",
             "w_out_rnn", "w_out_attn", "w_o", "g_final")
    given = dict(g_norm=g_norm, w_mod=w_mod, b_mod=b_mod, w_in=w_in, b_gate=b_gate, conv_w=conv_w, conv_b=conv_b,
                 w_a=w_a, b_a=b_a, w_x=w_x, b_x=b_x, lam=lam, w_out_rnn=w_out_rnn, w_out_attn=w_out_attn, w_o=w_o,
                 g_final=g_final)
    outs = [[results[name][k].reshape(given[name].shape) for name in order] for k in range(4)]
    return (loss, grad_x[None], *outs[0], *outs[1], *outs[2], *outs[3])
```

```python
import jax
import jax.numpy as jnp
from jax import lax
from jax.experimental import pallas as pl
from jax.experimental.pallas import tpu as pltpu

F32 = jnp.float32
BF16 = jnp.bfloat16
MESH = pl.DeviceIdType.MESH

D_MODEL = 1024
N_HEADS = 8
HEAD_DIM = 128
RNN_BLOCKS = 8
N_DEV = 8
ROT_HALF = 16
ROPE_THETA = 500000.0
DILATIONS = (1, 4, 16)
KEY_BLOCK = 128
SPAN = KEY_BLOCK * DILATIONS[-1]
ATTN_SCALE = HEAD_DIM ** -0.5
NORM_EPS = 1e-6
LRU_C = 8.0
NEG_INF = -1e30
ADAM_LR, ADAM_B1, ADAM_B2, ADAM_EPS, ADAM_WD, ADAM_STEP = 0.001, 0.9, 0.999, 1e-08, 0.01, 10

SUBLANES = 8
VMEM_LIMIT = 56 * 1024 * 1024
PROJ_ROWS = 1024
RNN_ROWS = 2048
HUB_ROWS = 256
DX_ROWS = 512
WGRAD_ROWS = 1024
ADD_ROWS = 256


def _params(sem=None, vmem=None):
    return pltpu.CompilerParams(dimension_semantics=sem, vmem_limit_bytes=vmem)


def _dot(a, b):
    return jnp.dot(a, b, preferred_element_type=F32)


def _dot_nt(a, b):
    return lax.dot_general(a, b, (((1,), (1,)), ((), ())), preferred_element_type=F32)


def _dot_tn(a, b):
    return lax.dot_general(a, b, (((0,), (0,)), ((), ())), preferred_element_type=F32)


def _sigmoid(z):
    return 1.0 / (1.0 + jnp.exp(-z))


def _expm1_nonpos(z, exp_z):
    return jnp.where(z > -0.01, z * (1.0 + 0.5 * z), exp_z - 1.0)


def _my_pos():
    return lax.axis_index("x"), lax.axis_index("y"), lax.axis_index("c")


def _flip(pos, k):
    x, y, c = pos
    return ((1 - x) if k & 4 else x, (1 - y) if k & 2 else y, (1 - c) if k & 1 else c)


def _index(pos):
    return 4 * pos[0] + 2 * pos[1] + pos[2]


def _ag_small(name, v):
    rows, cols = v.shape

    def body(v_ref, out_ref, send_sems, recv_sems):
        me = _my_pos()
        out_ref[_index(me)] = v_ref[...]
        sends = []
        for k in range(1, N_DEV):
            cp = pltpu.make_async_remote_copy(
                src_ref=v_ref, dst_ref=out_ref.at[_index(me)], send_sem=send_sems.at[k - 1],
                recv_sem=recv_sems.at[k - 1], device_id=_flip(me, k), device_id_type=MESH)
            cp.start()
            sends.append(cp)
        for k in range(1, N_DEV):
            peer = _flip(me, k)
            pltpu.make_async_remote_copy(
                src_ref=v_ref, dst_ref=out_ref.at[_index(peer)], send_sem=send_sems.at[k - 1],
                recv_sem=recv_sems.at[k - 1], device_id=peer, device_id_type=MESH).wait_recv()
        for cp in sends:
            cp.wait_send()

    return pl.pallas_call(
        body, name=name,
        out_shape=jax.ShapeDtypeStruct((N_DEV, rows, cols), v.dtype),
        in_specs=[pl.BlockSpec(memory_space=pltpu.VMEM)],
        out_specs=pl.BlockSpec(memory_space=pltpu.VMEM),
        scratch_shapes=[pltpu.SemaphoreType.DMA((N_DEV - 1,)), pltpu.SemaphoreType.DMA((N_DEV - 1,))],
        compiler_params=_params(None, VMEM_LIMIT),
    )(v)


def _split_start(name, make_copies, nsem, srcs, lands):
    n, k = len(srcs), len(lands)

    def body(*refs):
        for cp in make_copies(refs[:n], refs[n:n + k], refs[n + k], refs[n + k + 1]):
            cp.start()
        refs[-1][...] = jnp.zeros_like(refs[-1])

    hbm = pl.BlockSpec(memory_space=pltpu.HBM)
    sem = pl.BlockSpec(memory_space=pltpu.SEMAPHORE)
    arrays = [*srcs, *lands]
    outs = pl.pallas_call(
        body, name=name,
        out_shape=(pltpu.SemaphoreType.DMA((nsem,)), pltpu.SemaphoreType.DMA((nsem,)),
                   *[pltpu.HBM(t.shape, t.dtype) for t in arrays], jax.ShapeDtypeStruct((SUBLANES, 128), F32)),
        in_specs=[hbm] * (n + k),
        out_specs=(sem, sem, *[hbm] * (n + k), pl.BlockSpec(memory_space=pltpu.VMEM)),
        input_output_aliases={i: 2 + i for i in range(n + k)},
        compiler_params=pltpu.CompilerParams(has_side_effects=pltpu.SideEffectType.DATAFLOW_SIDE_EFFECTING),
    )(*[pltpu.with_memory_space_constraint(t, pltpu.HBM) for t in arrays])
    return outs[0], outs[1], outs[2:2 + n], outs[2 + n:2 + n + k], outs[-1]


def _split_wait(name, make_copies, flight, after):
    send_sems, recv_sems, srcs, lands, _ = flight
    n, k = len(srcs), len(lands)

    def body(*refs):
        for cp in make_copies(refs[:n], refs[n:n + k], refs[n + k], refs[n + k + 1]):
            cp.wait_send()
            cp.wait_recv()

    hbm = pl.BlockSpec(memory_space=pltpu.HBM)
    sem = pl.BlockSpec(memory_space=pltpu.SEMAPHORE)
    arrays = [*srcs, *lands]
    outs = pl.pallas_call(
        body, name=name, out_shape=tuple(pltpu.HBM(t.shape, t.dtype) for t in arrays),
        in_specs=[hbm] * (n + k) + [sem, sem, pl.BlockSpec(memory_space=pl.ANY)],
        out_specs=[hbm] * (n + k),
        input_output_aliases={i: i for i in range(n + k)},
        compiler_params=pltpu.CompilerParams(has_side_effects=pltpu.SideEffectType.DATAFLOW_SIDE_EFFECTING),
    )(*arrays, send_sems, recv_sems, after)
    return outs[:n], outs[n:]


def _remote(src, dst, send_sems, recv_sems, k, to):
    return pltpu.make_async_remote_copy(src_ref=src, dst_ref=dst, send_sem=send_sems.at[k], recv_sem=recv_sems.at[k],
                                        device_id=to, device_id_type=MESH)


def _peer_copies(shards, lands, send_sems, recv_sems):
    me = _my_pos()
    return [_remote(shards[a], lands[a].at[_index(me)], send_sems, recv_sems, a * 7 + k - 1, _flip(me, k))
            for a in range(len(shards)) for k in range(1, N_DEV)]


def _own_block_copies(shards, lands, send_sems, recv_sems):
    me = _my_pos()
    return [_remote(shards[0], lands[0].at[_index(me)], send_sems, recv_sems, i, _flip(me, k))
            for i, k in enumerate((1, 2, 4, 6))]


def _forward_copies(arrived, lands, send_sems, recv_sems):
    me = _my_pos()
    return [_remote(arrived[0].at[_index(_flip(me, 2 * m))], lands[0].at[_index(_flip(me, 2 * m))],
                    send_sems, recv_sems, m - 1, _flip(me, 1)) for m in range(1, 4)]


def _rs_to_sibling(name, stacks):
    n = len(stacks)

    def body(*refs):
        ins, outs = refs[:n], refs[n:2 * n]
        send_sems, recv_sems = refs[2 * n:]
        me = _my_pos()
        sib = _flip(me, 1)
        sends = []
        for a in range(n):
            for m in range(4):
                target = _flip(sib, 2 * m)
                cp = pltpu.make_async_remote_copy(
                    src_ref=ins[a].at[_index(target)], dst_ref=outs[a].at[m],
                    send_sem=send_sems.at[a * 4 + m], recv_sem=recv_sems.at[a * 4 + m],
                    device_id=sib, device_id_type=MESH)
                cp.start()
                sends.append(cp)
        for cp in sends:
            cp.wait_recv()
        for cp in sends:
            cp.wait_send()

    any_spec = pl.BlockSpec(memory_space=pl.ANY)
    return pl.pallas_call(
        body, name=name,
        out_shape=[jax.ShapeDtypeStruct((4,) + s.shape[1:], s.dtype) for s in stacks],
        in_specs=[any_spec] * n, out_specs=[any_spec] * n,
        scratch_shapes=[pltpu.SemaphoreType.DMA((4 * n,)), pltpu.SemaphoreType.DMA((4 * n,))],
    )(*stacks)


def _chip_copies(srcs, lands, send_sems, recv_sems):
    me = _my_pos()
    return [_remote(srcs[a].at[m - 1], lands[a].at[m - 1], send_sems, recv_sems, a * 3 + m - 1, _flip(me, 2 * m))
            for a in range(len(srcs)) for m in range(1, 4)]


def _add_sibling(name, stack, recv, targets):
    _, rows, cols = stack.shape
    tr = min(rows, ADD_ROWS)

    def own_body(t_ref, a_ref, b_ref, o_ref):
        o_ref[...] = a_ref[...] + b_ref[...].astype(F32)

    own = pl.pallas_call(
        own_body, name=name + "_own",
        out_shape=jax.ShapeDtypeStruct((rows, cols), F32),
        grid_spec=pltpu.PrefetchScalarGridSpec(
            num_scalar_prefetch=1, grid=(rows // tr,),
            in_specs=[pl.BlockSpec((None, tr, cols), lambda i, t: (t[0], i, 0)),
                      pl.BlockSpec((None, tr, cols), lambda i, t: (0, i, 0))],
            out_specs=pl.BlockSpec((tr, cols), lambda i, t: (i, 0))),
        compiler_params=_params(("arbitrary",)),
    )(targets, stack, recv)

    def send_body(t_ref, a_ref, b_ref, o_ref):
        o_ref[...] = (a_ref[...] + b_ref[...].astype(F32)).astype(BF16)

    send = pl.pallas_call(
        send_body, name=name + "_send",
        out_shape=jax.ShapeDtypeStruct((3, rows, cols), BF16),
        grid_spec=pltpu.PrefetchScalarGridSpec(
            num_scalar_prefetch=1, grid=(3, rows // tr),
            in_specs=[pl.BlockSpec((None, tr, cols), lambda m, i, t: (t[m + 1], i, 0)),
                      pl.BlockSpec((None, tr, cols), lambda m, i, t: (m + 1, i, 0))],
            out_specs=pl.BlockSpec((None, tr, cols), lambda m, i, t: (m, i, 0))),
        compiler_params=_params(("arbitrary", "arbitrary")),
    )(targets, stack, recv)
    return own, send


def _allreduce_small(name, v):
    rows, cols = v.shape
    half = rows // 2
    assert rows % (2 * SUBLANES) == 0

    def body(v_ref, out_ref, from_sib, chip_half, from_chips, send_sems, recv_sems):
        me = _my_pos()
        sib = _flip(me, 1)
        mine = pl.ds(pl.multiple_of(me[2] * half, SUBLANES), half)
        theirs = pl.ds(pl.multiple_of((1 - me[2]) * half, SUBLANES), half)

        def copy(k, src, dst, to):
            return pltpu.make_async_remote_copy(src_ref=src, dst_ref=dst, send_sem=send_sems.at[k],
                                                recv_sem=recv_sems.at[k], device_id=to, device_id_type=MESH)

        to_sib = copy(0, v_ref.at[theirs], from_sib, sib)
        to_sib.start()
        to_sib.wait_recv()
        chip_half[...] = v_ref[mine, :] + from_sib[...]
        to_chips = [copy(m, chip_half, from_chips.at[m - 1], _flip(me, 2 * m)) for m in range(1, 4)]
        for cp in to_chips:
            cp.start()
        for cp in to_chips:
            cp.wait_recv()
        my_chip = 2 * me[0] + me[1]
        total = None
        for chip in range(4):
            slot = jnp.maximum(jnp.bitwise_xor(chip, my_chip) - 1, 0)
            part = jnp.where(chip == my_chip, chip_half[...], from_chips[slot])
            total = part if total is None else total + part
        out_ref[mine, :] = total
        swap = copy(4, out_ref.at[mine], out_ref.at[mine], sib)
        swap.start()
        copy(4, out_ref.at[theirs], out_ref.at[theirs], sib).wait_recv()
        for cp in [to_sib, swap] + to_chips:
            cp.wait_send()

    return pl.pallas_call(
        body, name=name, out_shape=jax.ShapeDtypeStruct((rows, cols), F32),
        in_specs=[pl.BlockSpec(memory_space=pltpu.VMEM)],
        out_specs=pl.BlockSpec(memory_space=pltpu.VMEM),
        scratch_shapes=[pltpu.VMEM((half, cols), F32), pltpu.VMEM((half, cols), F32),
                        pltpu.VMEM((3, half, cols), F32),
                        pltpu.SemaphoreType.DMA((5,)), pltpu.SemaphoreType.DMA((5,))],
        compiler_params=_params(None, VMEM_LIMIT),
    )(v)


def _mod_fwd(c_all, w_mod):
    def body(c_ref, w_ref, o_ref):
        c = c_ref[...]
        o_ref[...] = jnp.dot(c * _sigmoid(c), w_ref[...], preferred_element_type=F32,
                             precision=lax.Precision.HIGHEST)

    return pl.pallas_call(
        body, name="mod_fwd", out_shape=jax.ShapeDtypeStruct((N_DEV, w_mod.shape[1]), F32),
    )(c_all, w_mod)


def _mod_bwd(c_all, dmod_all, dmod_cols):
    def body(c_ref, da_ref, dc_ref, gb_ref, gw_ref):
        c = c_ref[...]
        acc = da_ref[0:1, :]
        for b in range(1, N_DEV):
            acc = acc + da_ref[b:b + 1, :]
        gb_ref[...] = acc
        gw_ref[...] = lax.dot_general(c * _sigmoid(c), dc_ref[...], (((0,), (0,)), ((), ())),
                                      preferred_element_type=F32, precision=lax.Precision.HIGHEST)

    return pl.pallas_call(
        body, name="mod_bwd",
        out_shape=[jax.ShapeDtypeStruct((1, dmod_all.shape[1]), F32),
                   jax.ShapeDtypeStruct((c_all.shape[1], dmod_cols.shape[1]), F32)],
    )(c_all, dmod_all, dmod_cols)


def _rope_partner(t):
    lane = lax.broadcasted_iota(jnp.int32, t.shape, 1)
    return jnp.where(lane < ROT_HALF, pltpu.roll(t, HEAD_DIM - ROT_HALF, 1), pltpu.roll(t, ROT_HALF, 1))


def _norm(x, mod, b_mod, g_norm):
    seq = x.shape[0]
    tm = PROJ_ROWS

    def body(x_ref, mod_ref, bmod_ref, g_ref, h_ref):
        xf = x_ref[...]
        rstd = lax.rsqrt(jnp.mean(xf * xf, axis=-1, keepdims=True) + NORM_EPS)
        shift = mod_ref[:, 0:D_MODEL] + bmod_ref[:, 0:D_MODEL]
        scale = mod_ref[:, D_MODEL:2 * D_MODEL] + bmod_ref[:, D_MODEL:2 * D_MODEL]
        h_ref[...] = (((xf * rstd) * g_ref[...]) * (1.0 + scale) + shift).astype(BF16)

    row = pl.BlockSpec((tm, D_MODEL), lambda i: (i, 0))
    const = lambda cols: pl.BlockSpec((1, cols), lambda i: (0, 0))
    return pl.pallas_call(
        body, name="norm", out_shape=jax.ShapeDtypeStruct((seq, D_MODEL), BF16), grid=(seq // tm,),
        in_specs=[row, const(3 * D_MODEL), const(3 * D_MODEL), const(D_MODEL)], out_specs=row,
        compiler_params=_params(("arbitrary",), VMEM_LIMIT),
    )(x, mod, b_mod, g_norm)


def _proj(name, h, w, slots, pieces, cosf, sinf, prior):
    seq = h.shape[0]
    tm = PROJ_ROWS
    count = pieces.shape[0]

    def body(slots_ref, pieces_ref, h_ref, w_ref, cos_ref, sin_ref, *rest):
        out_ref = rest[-1]
        piece = pieces_ref[pl.program_id(0)]

        @pl.when((piece < 2) | (piece > 3))
        def _():
            out_ref[...] = _dot(h_ref[...], w_ref[...])

        def rotated(gain):
            for pair in range(N_HEADS // 2):
                both = _dot(h_ref[...], w_ref[:, 2 * pair * HEAD_DIM:2 * (pair + 1) * HEAD_DIM])
                for hh in (2 * pair, 2 * pair + 1):
                    t = both[:, (hh % 2) * HEAD_DIM:(hh % 2 + 1) * HEAD_DIM]
                    t = t * cos_ref[...] + _rope_partner(t) * sin_ref[...]
                    out_ref[:, hh * HEAD_DIM:(hh + 1) * HEAD_DIM] = t if gain is None else t * gain

        @pl.when(piece == 2)
        def _():
            rotated(ATTN_SCALE)

        @pl.when(piece == 3)
        def _():
            rotated(None)

    row = lambda j, i, sl, pc: (i, 0)
    in_specs = [pl.BlockSpec((tm, D_MODEL), row),
                pl.BlockSpec((None, D_MODEL, D_MODEL), lambda j, i, sl, pc: (sl[j], 0, 0)),
                pl.BlockSpec((tm, HEAD_DIM), row), pl.BlockSpec((tm, HEAD_DIM), row)]
    args = [slots, pieces, h, w, cosf, sinf]
    aliases = {}
    if prior is not None:
        in_specs.append(pl.BlockSpec(memory_space=pl.ANY))
        args.append(prior)
        aliases = {6: 0}
    return pl.pallas_call(
        body, name=name,
        out_shape=jax.ShapeDtypeStruct((seq, 8 * D_MODEL), F32),
        grid_spec=pltpu.PrefetchScalarGridSpec(
            num_scalar_prefetch=2, grid=(count, seq // tm), in_specs=in_specs,
            out_specs=pl.BlockSpec((tm, D_MODEL), lambda j, i, sl, pc: (i, pc[j]))),
        input_output_aliases=aliases,
        compiler_params=_params(("arbitrary", "arbitrary"), VMEM_LIMIT),
    )(*args)


def _shift_down(v, s, head):
    rolled = pltpu.roll(v, s, 0)
    row = lax.broadcasted_iota(jnp.int32, head.shape, 0)
    first = jnp.where(row < s, pltpu.roll(head, s, 0), rolled[:SUBLANES, :])
    return jnp.concatenate([first, rolled[SUBLANES:, :]], axis=0)


def _shift_up(v, s, tail):
    rows = v.shape[0]
    rolled = pltpu.roll(v, rows - s, 0)
    row = lax.broadcasted_iota(jnp.int32, tail.shape, 0)
    last = jnp.where(row >= SUBLANES - s, pltpu.roll(tail, SUBLANES - s, 0), rolled[rows - SUBLANES:, :])
    return jnp.concatenate([rolled[:rows - SUBLANES, :], last], axis=0)


def _doubling(a, b, period, reverse):
    rows = a.shape[0]
    pos = lax.broadcasted_iota(jnp.int32, a.shape, 0) & (period - 1)
    k = 1
    while k < period:
        inside = (pos < period - k) if reverse else (pos >= k)
        shift = rows - k if reverse else k
        a_s = jnp.where(inside, pltpu.roll(a, shift, 0), 1.0)
        b_s = jnp.where(inside, pltpu.roll(b, shift, 0), 0.0)
        b = a * b_s + b
        a = a * a_s
        k *= 2
    return a, b


def _scan(a, b, boundary, reverse, a_scr, b_scr, spread):
    rows = a.shape[0]
    ntile = rows // SUBLANES
    a_scr[...], b_scr[...] = _doubling(a, b, SUBLANES, reverse)
    ends = pl.ds(0 if reverse else SUBLANES - 1, ntile, stride=SUBLANES)
    a_end, x_end = _doubling(a_scr[ends, :], b_scr[ends, :], ntile, reverse)
    x_end = x_end + a_end * boundary
    tile = lax.broadcasted_iota(jnp.int32, x_end.shape, 0)
    if reverse:
        incoming = jnp.where(tile == ntile - 1, boundary, pltpu.roll(x_end, ntile - 1, 0))
        last = x_end[0:1, :]
    else:
        incoming = jnp.where(tile == 0, boundary, pltpu.roll(x_end, 1, 0))
        last = x_end[ntile - 1:ntile, :]
    for s in range(SUBLANES):
        spread[pl.ds(s, ntile, stride=SUBLANES), :] = incoming
    return b_scr[...] + a_scr[...] * spread[...], last


def _conv_taps(xr, head):
    return [_shift_down(xr, 3, head), _shift_down(xr, 2, head), _shift_down(xr, 1, head), xr]


def _rnn_gates(xc, wa, ba, wx, bx, lam, keep):
    xcb = xc.astype(BF16)
    r = _sigmoid(_dot(xcb, wa.astype(BF16)) + ba)
    i = _sigmoid(_dot(xcb, wx.astype(BF16)) + bx)
    softplus = jnp.maximum(-lam, 0.0) + jnp.log(1.0 + jnp.exp(-jnp.abs(lam)))
    cl = -LRU_C * softplus
    log_a = cl * r
    a_raw = jnp.exp(log_a)
    mult_raw = jnp.sqrt(-_expm1_nonpos(2.0 * log_a, a_raw * a_raw))
    live = keep > 0.0
    return r, i, cl, a_raw, mult_raw, jnp.where(live, a_raw, 0.0), jnp.where(live, mult_raw, 1.0), live


def _rnn_specs(seq, rows, time_of):
    per = rows // SUBLANES
    vec = pl.BlockSpec((None, 1, 128), lambda hb, n: (hb, 0, 0))
    mat = pl.BlockSpec((None, 128, 128), lambda hb, n: (hb, 0, 0))
    return [pl.BlockSpec((rows, 128), lambda hb, n: (time_of(n), hb)),
            pl.BlockSpec((SUBLANES, 128), lambda hb, n: (jnp.maximum(time_of(n) * per - 1, 0), hb)),
            pl.BlockSpec((rows, 1), lambda hb, n: (time_of(n), 0)),
            pl.BlockSpec((None, SUBLANES, 128), lambda hb, n: (hb, 0, 0)),
            vec, mat, vec, mat, vec, vec]


def _rnn_fwd(pf, keep, conv_w8, conv_b, w_a, b_a, w_x, b_x, lam):
    seq = pf.shape[0]
    rows = RNN_ROWS

    def body(x_ref, xh_ref, keep_ref, cw_ref, cb_ref, wa_ref, ba_ref, wx_ref, bx_ref, lam_ref, hr_ref,
             carry, a_scr, b_scr, spread):
        n = pl.program_id(1)

        @pl.when(n == 0)
        def _():
            carry[...] = jnp.zeros_like(carry)

        xr = x_ref[...]
        head = jnp.where(n > 0, xh_ref[...], 0.0)
        taps = _conv_taps(xr, head)
        xc = cb_ref[...] + sum(cw_ref[k:k + 1, :] * taps[k] for k in range(4))
        _, i, _, _, _, a, mult, _ = _rnn_gates(xc, wa_ref[...], ba_ref[...], wx_ref[...], bx_ref[...],
                                               lam_ref[...], keep_ref[...])
        h, last = _scan(a, mult * i * xc, carry[0:1, :], False, a_scr, b_scr, spread)
        hr_ref[...] = h
        carry[...] = jnp.broadcast_to(last, carry.shape)

    chunk_f32 = pltpu.VMEM((rows, 128), F32)
    return pl.pallas_call(
        body, name="rnn_fwd",
        out_shape=jax.ShapeDtypeStruct((seq, D_MODEL), F32),
        grid=(RNN_BLOCKS, seq // rows),
        in_specs=_rnn_specs(seq, rows, lambda n: n),
        out_specs=pl.BlockSpec((rows, 128), lambda hb, n: (n, hb)),
        scratch_shapes=[pltpu.VMEM((SUBLANES, 128), F32), chunk_f32, chunk_f32, chunk_f32],
        compiler_params=_params(("arbitrary", "arbitrary"), VMEM_LIMIT),
    )(pf, pf, keep, conv_w8, conv_b, w_a, b_a, w_x, b_x, lam)


def _rnn_bwd(pf, hr, dhr, keep, conv_w8, conv_b, w_a, b_a, w_x, b_x, lam):
    seq = pf.shape[0]
    rows = RNN_ROWS
    nchunk = seq // rows
    per = rows // SUBLANES
    time_of = lambda n: nchunk - 1 - n

    def body(x_ref, xh_ref, keep_ref, cw_ref, cb_ref, wa_ref, ba_ref, wx_ref, bx_ref, lam_ref,
             hr_ref, hrh_ref, dhr_ref,
             dx_ref, gcw_ref, gcb_ref, gwa_ref, gba_ref, gwx_ref, gbx_ref, glam_ref,
             g_carry, dxc_tail, a_scr, b_scr, spread):
        n = pl.program_id(1)
        first_in_time = n == nchunk - 1

        @pl.when(n == 0)
        def _():
            g_carry[...] = jnp.zeros_like(g_carry)
            dxc_tail[...] = jnp.zeros_like(dxc_tail)
            for ref in (gcw_ref, gcb_ref, gwa_ref, gba_ref, gwx_ref, gbx_ref, glam_ref):
                ref[...] = jnp.zeros_like(ref)

        xr = x_ref[...]
        head = jnp.where(first_in_time, 0.0, xh_ref[...])
        taps = _conv_taps(xr, head)
        cw = cw_ref[...]
        xc = cb_ref[...] + sum(cw[k:k + 1, :] * taps[k] for k in range(4))
        wa, wx, lam = wa_ref[...], wx_ref[...], lam_ref[...]
        r, i, cl, a_raw, mult_raw, a, mult, live = _rnn_gates(xc, wa, ba_ref[...], wx, bx_ref[...], lam,
                                                               keep_ref[...])
        h_prev = _shift_down(hr_ref[...], 1, jnp.where(first_in_time, 0.0, hrh_ref[...]))

        row = lax.broadcasted_iota(jnp.int32, xr.shape, 0)
        last = row == rows - 1
        a_next = jnp.where(last, 0.0, pltpu.roll(a, rows - 1, 0))
        g, g_first = _scan(a_next, dhr_ref[...] + jnp.where(last, g_carry[0:1, :], 0.0),
                           jnp.zeros((1, 128), F32), True, a_scr, b_scr, spread)
        g_carry[...] = jnp.broadcast_to(a[0:1, :] * g_first, g_carry.shape)

        da = g * h_prev
        dmult = g * i * xc
        di = g * mult * xc
        dxc = g * mult * i
        dlog_a = jnp.where(live, da * a_raw - dmult * a_raw * a_raw / mult_raw, 0.0)
        dpa = (dlog_a * cl) * r * (1.0 - r)
        dpx = di * i * (1.0 - i)
        glam_ref[...] += jnp.sum(dlog_a * r, axis=0, keepdims=True) * (LRU_C * _sigmoid(-lam))
        xcb, dpab, dpxb = xc.astype(BF16), dpa.astype(BF16), dpx.astype(BF16)
        gwa_ref[...] += _dot_tn(xcb, dpab)
        gwx_ref[...] += _dot_tn(xcb, dpxb)
        gba_ref[...] += jnp.sum(dpa, axis=0, keepdims=True)
        gbx_ref[...] += jnp.sum(dpx, axis=0, keepdims=True)
        dxc = dxc + _dot_nt(dpab, wa.astype(BF16)) + _dot_nt(dpxb, wx.astype(BF16))

        gcb_ref[...] += jnp.sum(dxc, axis=0, keepdims=True)
        for k in range(4):
            gcw_ref[k:k + 1, :] += jnp.sum(dxc * taps[k], axis=0, keepdims=True)
        tail = dxc_tail[...]
        dx = cw[3:4, :] * dxc
        for k in range(3):
            dx = dx + cw[k:k + 1, :] * _shift_up(dxc, 3 - k, tail)
        dx_ref[...] = dx.astype(BF16)
        dxc_tail[...] = dxc[0:SUBLANES, :]

    blk = lambda hb, n: (hb, 0, 0)
    chunk = pl.BlockSpec((rows, 128), lambda hb, n: (time_of(n), hb))
    vec_out = pl.BlockSpec((None, 1, 128), blk)
    mat_out = pl.BlockSpec((None, 128, 128), blk)
    vec_shape = jax.ShapeDtypeStruct((RNN_BLOCKS, 1, 128), F32)
    mat_shape = jax.ShapeDtypeStruct((RNN_BLOCKS, 128, 128), F32)
    return pl.pallas_call(
        body, name="rnn_bwd",
        out_shape=[jax.ShapeDtypeStruct((seq, D_MODEL), BF16),
                   jax.ShapeDtypeStruct((RNN_BLOCKS, SUBLANES, 128), F32), vec_shape,
                   mat_shape, vec_shape, mat_shape, vec_shape, vec_shape],
        grid=(RNN_BLOCKS, nchunk),
        in_specs=_rnn_specs(seq, rows, time_of) + [
            chunk, pl.BlockSpec((SUBLANES, 128), lambda hb, n: (jnp.maximum(time_of(n) * per - 1, 0), hb)), chunk],
        out_specs=[chunk, pl.BlockSpec((None, SUBLANES, 128), blk), vec_out,
                   mat_out, vec_out, mat_out, vec_out, vec_out],
        scratch_shapes=[pltpu.VMEM((SUBLANES, 128), F32), pltpu.VMEM((SUBLANES, 128), F32)]
                       + [pltpu.VMEM((rows, 128), F32)] * 3,
        compiler_params=_params(("arbitrary", "arbitrary"), VMEM_LIMIT),
    )(pf, pf, keep, conv_w8, conv_b, w_a, b_a, w_x, b_x, lam, hr, hr, dhr)


def _unit_rows(dil, r, j):
    start = j * KEY_BLOCK * dil + r
    return pl.ds(start, KEY_BLOCK) if dil == 1 else pl.ds(start, KEY_BLOCK, stride=dil)


def _attn_fwd(proj):
    nh, seq = N_HEADS, proj.shape[0]
    nchunk = seq // SPAN
    nblk = SPAN // KEY_BLOCK
    wide = DILATIONS[-1]

    def body(q_ref, k_ref, v_ref, kp_ref, vp_ref, o_ref, l1_ref, l4_ref, l16_ref,
             acc, m_s, l_s, q16, k16, v16, k16p, v16p, acc16, m16, l16, tmp):
        n = pl.program_id(1)
        qi = lax.broadcasted_iota(jnp.int32, (KEY_BLOCK, KEY_BLOCK), 0)
        ki = lax.broadcasted_iota(jnp.int32, (KEY_BLOCK, KEY_BLOCK), 1)
        bias_own = jnp.where(ki <= qi, 0.0, NEG_INF)
        bias_before = jnp.where(ki >= qi, 0.0, NEG_INF)
        bias_mid = jnp.concatenate([bias_before, bias_own], axis=1)
        bias_first = jnp.concatenate([jnp.where(n > 0, bias_before, NEG_INF), bias_own], axis=1)
        ones = jnp.ones((2 * KEY_BLOCK, HEAD_DIM), BF16)
        diag = qi == ki

        @pl.when(n == 0)
        def _():
            k16p[...] = jnp.zeros_like(k16p)
            v16p[...] = jnp.zeros_like(v16p)

        def unit(qf, kpb, kb, vpb, vb, bias, state, rows, first):
            acc_r, m_r, l_r = state
            kcat = jnp.concatenate([kpb, kb], axis=0)
            vaug = jnp.concatenate([jnp.concatenate([vpb, vb], axis=0), ones], axis=1)
            s = _dot_nt(qf.astype(BF16), kcat) + bias
            mx = jnp.max(s, axis=-1, keepdims=True)
            if first:
                m_new = jnp.broadcast_to(mx, (KEY_BLOCK, HEAD_DIM))
            else:
                m_old = m_r[rows, :]
                m_new = jnp.maximum(m_old, mx)
            pv = _dot(jnp.exp(s - jnp.concatenate([m_new, m_new], axis=1)).astype(BF16), vaug)
            if first:
                acc_r[rows, :] = pv[:, :HEAD_DIM]
                l_r[rows, :] = pv[:, HEAD_DIM:]
            else:
                alpha = jnp.exp(m_old - m_new)
                acc_r[rows, :] = alpha * acc_r[rows, :] + pv[:, :HEAD_DIM]
                l_r[rows, :] = alpha * l_r[rows, :] + pv[:, HEAD_DIM:]
            m_r[rows, :] = m_new

        for gi, dil in enumerate(DILATIONS[:-1]):
            nb = nblk // dil
            for r in range(dil):
                prow = _unit_rows(dil, r, nb - 1)
                kpb, vpb = kp_ref[prow, :].astype(BF16), vp_ref[prow, :].astype(BF16)
                for j in range(nb):
                    rows = _unit_rows(dil, r, j)
                    kb, vb = k_ref[rows, :].astype(BF16), v_ref[rows, :].astype(BF16)
                    unit(q_ref[rows, :], kpb, kb, vpb, vb, bias_first if j == 0 else bias_mid,
                         (acc, m_s, l_s), rows, gi == 0)
                    kpb, vpb = kb, vb

        for src, dst in ((q_ref, q16), (k_ref, k16), (v_ref, v16), (acc, acc16), (m_s, m16), (l_s, l16)):
            _to_residue_major(src, tmp, dst)
        for r in range(wide):
            rows = pl.ds(r * KEY_BLOCK, KEY_BLOCK)
            unit(q16[rows, :], k16p[rows, :].astype(BF16), k16[rows, :].astype(BF16), v16p[rows, :].astype(BF16),
                 v16[rows, :].astype(BF16), bias_first, (acc16, m16, l16), rows, False)
        k16p[...] = k16[...]
        v16p[...] = v16[...]

        den = l16[...]
        acc16[...] = acc16[...] * (1.0 / den)
        m16[...] = m16[...] + jnp.log(den)
        _from_residue_major(acc16, tmp, o_ref, False)
        _from_residue_major(m16, tmp, m_s, False)

        def lse_row(ref, rows):
            return jnp.sum(jnp.where(diag, ref[rows, :], 0.0), axis=0, keepdims=True)

        for dil, out in zip(DILATIONS[:-1], (l1_ref, l4_ref)):
            nb = nblk // dil
            for r in range(dil):
                for j in range(nb):
                    out[r * nb + j:r * nb + j + 1, :] = lse_row(m_s, _unit_rows(dil, r, j))
        for r in range(wide):
            l16_ref[r:r + 1, :] = lse_row(m16, pl.ds(r * KEY_BLOCK, KEY_BLOCK))

    cur = lambda piece: pl.BlockSpec((SPAN, HEAD_DIM), lambda h, n: (n, piece * nh + h))
    before = lambda piece: pl.BlockSpec((SPAN, HEAD_DIM), lambda h, n: (jnp.maximum(n - 1, 0), piece * nh + h))
    blk = pl.BlockSpec((None, SPAN, HEAD_DIM), lambda h, n: (h, n, 0))
    lblk = pl.BlockSpec((None, nblk, KEY_BLOCK), lambda h, n: (h, n, 0))
    lshape = jax.ShapeDtypeStruct((nh, seq // KEY_BLOCK, KEY_BLOCK), F32)
    o, l1, l4, l16 = pl.pallas_call(
        body, name="attn_fwd",
        out_shape=[jax.ShapeDtypeStruct((nh, seq, HEAD_DIM), F32), lshape, lshape, lshape],
        grid=(nh, nchunk), in_specs=[cur(2), cur(3), cur(4), before(3), before(4)],
        out_specs=[blk, lblk, lblk, lblk],
        scratch_shapes=[pltpu.VMEM((SPAN, HEAD_DIM), F32)] * 12,
        compiler_params=_params(("arbitrary", "arbitrary"), VMEM_LIMIT),
    )(proj, proj, proj, proj, proj)
    return o, (l1, l4, l16)


def _to_residue_major(src, tmp, dst):
    quarter = SPAN // 4
    for r4 in range(4):
        tmp[r4 * quarter:(r4 + 1) * quarter, :] = src[pl.ds(r4, quarter, stride=4), :]
    for r4 in range(4):
        for rp in range(4):
            r = r4 + 4 * rp
            dst[r * KEY_BLOCK:(r + 1) * KEY_BLOCK, :] = tmp[pl.ds(r4 * quarter + rp, KEY_BLOCK, stride=4), :]


def _from_residue_major(src, tmp, dst, add):
    quarter = SPAN // 4
    for r4 in range(4):
        for rp in range(4):
            r = r4 + 4 * rp
            tmp[pl.ds(r4 * quarter + rp, KEY_BLOCK, stride=4), :] = src[r * KEY_BLOCK:(r + 1) * KEY_BLOCK, :]
    for r4 in range(4):
        rows = pl.ds(r4, quarter, stride=4)
        part = tmp[r4 * quarter:(r4 + 1) * quarter, :]
        dst[rows, :] = dst[rows, :] + part if add else part


def _attn_bwd(proj, do, o, lses, cosf, sinf):
    nh, seq = N_HEADS, proj.shape[0]
    nchunk = seq // SPAN
    nblk = SPAN // KEY_BLOCK
    wide = DILATIONS[-1]
    assert SPAN == wide * KEY_BLOCK

    def body(q_ref, k_ref, v_ref, do_ref, o_ref, kp_ref, vp_ref, l1_ref, l4_ref, l16_ref,
             cos_ref, sin_ref, cosp_ref, sinp_ref, dq_ref, dk_ref, dv_ref,
             dq_acc, dkc_acc, dvc_acc, dkp_acc, dvp_acc, q16, k16, v16, do16, o16, k16p, v16p,
             dq16, dkc16, dvc16, dkp16, dvp16, tmp, pt_s, ds_s, kcat_s, qb_s, dob_s):
        n = pl.program_id(1)
        ki = lax.broadcasted_iota(jnp.int32, (KEY_BLOCK, KEY_BLOCK), 0)
        qi = lax.broadcasted_iota(jnp.int32, (KEY_BLOCK, KEY_BLOCK), 1)
        bias_own = jnp.where(ki <= qi, 0.0, NEG_INF)
        bias_before = jnp.where(ki >= qi, 0.0, NEG_INF)
        bias_mid = jnp.concatenate([bias_before, bias_own], axis=0)
        bias_first = jnp.concatenate([jnp.where(n > 0, bias_before, NEG_INF), bias_own], axis=0)
        ones8 = jnp.ones((SUBLANES, HEAD_DIM), BF16)

        def row_dot(a, b):
            prod = a * b
            hi = prod.astype(BF16)
            lo = (prod - hi.astype(F32)).astype(BF16)
            return (_dot_nt(ones8, hi) + _dot_nt(ones8, lo))[0:1, :]

        def group(units, srcs, before, l_ref, accs):
            src_q, src_do, src_o, src_k, src_v = srcs
            before_k, before_v = before
            acc_q, acc_kc, acc_vc, acc_kp, acc_vp = accs
            kb = vb = None
            for u, (rows, prow, outside, lrow, _) in enumerate(units):
                dof = src_do[rows, :]
                qb, dob = src_q[rows, :].astype(BF16), dof.astype(BF16)
                kpb, vpb = (before_k[prow, :].astype(BF16), before_v[prow, :].astype(BF16)) if outside else (kb, vb)
                kb, vb = src_k[rows, :].astype(BF16), src_v[rows, :].astype(BF16)
                kcat = jnp.concatenate([kpb, kb], axis=0)
                vcat = jnp.concatenate([vpb, vb], axis=0)
                bias = bias_first if outside else bias_mid
                pt = jnp.exp(_dot_nt(kcat, qb) + bias - l_ref[lrow:lrow + 1, :])
                dst = pt * (_dot_nt(vcat, dob) - row_dot(dof, src_o[rows, :]))
                pt_s[u], ds_s[u], kcat_s[u], qb_s[u], dob_s[u] = pt.astype(BF16), dst.astype(BF16), kcat, qb, dob
            for u, (rows, _, _, _, _) in enumerate(units):
                acc_q[rows, :] += _dot_tn(ds_s[u], kcat_s[u])
            for u, (rows, prow, outside, _, nxt) in enumerate(units):
                dk = _dot(ds_s[u, KEY_BLOCK:, :], qb_s[u])
                dv = _dot(pt_s[u, KEY_BLOCK:, :], dob_s[u])
                if nxt is not None:
                    dk = dk + _dot(ds_s[nxt, :KEY_BLOCK, :], qb_s[nxt])
                    dv = dv + _dot(pt_s[nxt, :KEY_BLOCK, :], dob_s[nxt])
                acc_kc[rows, :] += dk
                acc_vc[rows, :] += dv
                if outside:
                    acc_kp[prow, :] += _dot(ds_s[u, :KEY_BLOCK, :], qb_s[u])
                    acc_vp[prow, :] += _dot(pt_s[u, :KEY_BLOCK, :], dob_s[u])

        @pl.when(n == 0)
        def _():
            for ref in (dkp_acc, dvp_acc, dkp16, dvp16, k16p, v16p):
                ref[...] = jnp.zeros_like(ref)

        @pl.when(n < nchunk)
        def _():
            for ref in (dq_acc, dkc_acc, dvc_acc, dq16, dkc16, dvc16):
                ref[...] = jnp.zeros_like(ref)
            for src, dst in ((q_ref, q16), (k_ref, k16), (v_ref, v16), (do_ref, do16), (o_ref, o16)):
                _to_residue_major(src, tmp, dst)
            natural = (q_ref, do_ref, o_ref, k_ref, v_ref)
            for dil, l_ref in zip(DILATIONS[:-1], (l1_ref, l4_ref)):
                nb = nblk // dil
                units = [(_unit_rows(dil, r, j), _unit_rows(dil, r, (j - 1) % nb), j == 0, r * nb + j,
                          r * nb + j + 1 if j + 1 < nb else None) for r in range(dil) for j in range(nb)]
                group(units, natural, (kp_ref, vp_ref), l_ref, (dq_acc, dkc_acc, dvc_acc, dkp_acc, dvp_acc))
            blocks = [pl.ds(r * KEY_BLOCK, KEY_BLOCK) for r in range(wide)]
            group([(rows, rows, True, r, None) for r, rows in enumerate(blocks)], (q16, do16, o16, k16, v16),
                  (k16p, v16p), l16_ref, (dq16, dkc16, dvc16, dkp16, dvp16))
            _from_residue_major(dq16, tmp, dq_acc, True)
            dq = dq_acc[...]
            dq_ref[...] = ((dq * cos_ref[...] - _rope_partner(dq) * sin_ref[...]) * ATTN_SCALE).astype(BF16)

        @pl.when(n > 0)
        def _():
            _from_residue_major(dkp16, tmp, dkp_acc, True)
            _from_residue_major(dvp16, tmp, dvp_acc, True)
            dk = dkp_acc[...]
            dk_ref[...] = (dk * cosp_ref[...] - _rope_partner(dk) * sinp_ref[...]).astype(BF16)
            dv_ref[...] = dvp_acc[...].astype(BF16)

        @pl.when(n < nchunk)
        def _():
            for src, dst in ((dkc_acc, dkp_acc), (dvc_acc, dvp_acc), (dkc16, dkp16), (dvc16, dvp16),
                             (k16, k16p), (v16, v16p)):
                dst[...] = src[...]

    last = nchunk - 1
    cur = lambda h, n: (h, jnp.minimum(n, last), 0)
    prev = lambda h, n: (h, jnp.clip(n - 1, 0, last), 0)
    blk = lambda idx: pl.BlockSpec((None, SPAN, HEAD_DIM), idx)
    lblk = pl.BlockSpec((None, nblk, KEY_BLOCK), cur)
    tab = pl.BlockSpec((SPAN, HEAD_DIM), lambda h, n: (jnp.minimum(n, last), 0))
    tabp = pl.BlockSpec((SPAN, HEAD_DIM), lambda h, n: (jnp.clip(n - 1, 0, last), 0))
    out_q = pl.BlockSpec((SPAN, HEAD_DIM), lambda h, n: (jnp.minimum(n, last), h))
    out_kv = pl.BlockSpec((SPAN, HEAD_DIM), lambda h, n: (jnp.clip(n - 1, 0, last), h))
    shape = jax.ShapeDtypeStruct((seq, nh * HEAD_DIM), BF16)
    tok = lambda piece, row: pl.BlockSpec((SPAN, HEAD_DIM), lambda h, n: (row(n), piece * nh + h))
    row_cur, row_prev = (lambda n: jnp.minimum(n, last)), (lambda n: jnp.clip(n - 1, 0, last))
    return pl.pallas_call(
        body, name="attn_bwd", out_shape=[shape, shape, shape], grid=(nh, nchunk + 1),
        in_specs=[tok(2, row_cur), tok(3, row_cur), tok(4, row_cur), blk(cur), blk(cur),
                  tok(3, row_prev), tok(4, row_prev)] + [lblk] * 3 + [tab, tab, tabp, tabp],
        out_specs=[out_q, out_kv, out_kv],
        scratch_shapes=[pltpu.VMEM((SPAN, HEAD_DIM), F32)] * 18
                       + [pltpu.VMEM((nblk, 2 * KEY_BLOCK, HEAD_DIM), BF16)] * 3
                       + [pltpu.VMEM((nblk, KEY_BLOCK, HEAD_DIM), BF16)] * 2,
        compiler_params=_params(("arbitrary", "arbitrary"), VMEM_LIMIT),
    )(proj, proj, proj, do, o, proj, proj, *lses, cosf, sinf, cosf, sinf)


def _hub(x, tgt, hr, pf, o_hm, mod, b_mod, b_gate, g_final, w_out_rnn, w_out_attn, w_o):
    seq = x.shape[0]
    tm = HUB_ROWS
    nsteps = seq // tm

    def body(x_ref, t_ref, hr_ref, zr_ref, za_ref, gr_ref, ga_ref, o_ref, mod_ref, bmod_ref, bg_ref, gf_ref,
             wr_hbm, wa_hbm, wo_hbm,
             dx2_ref, dhr_ref, dzr_ref, do_ref, dza_ref, dgr_ref, dga_ref,
             ur_ref, dyr_ref, ua_ref, dya_ref, mg_ref, dmo_ref,
             ggf_ref, gbg_ref, dgate_ref, loss_ref,
             wr, wa, wo, sem):
        step = pl.program_id(0)

        @pl.when(step == 0)
        def _():
            for src, dst in ((wr_hbm, wr), (wa_hbm, wa), (wo_hbm, wo)):
                cp = pltpu.make_async_copy(src, dst, sem)
                cp.start()
                cp.wait()
            for ref in (ggf_ref, gbg_ref, dgate_ref, loss_ref):
                ref[...] = jnp.zeros_like(ref)

        gate = mod_ref[:, 2 * D_MODEL:] + bmod_ref[:, 2 * D_MODEL:]
        gfin = gf_ref[...]
        hr_t, zr, za = hr_ref[...], zr_ref[...], za_ref[...]
        o = jnp.concatenate([o_ref[hh] for hh in range(N_HEADS)], axis=1)
        sig_zr, sig_za = _sigmoid(zr), _sigmoid(za)
        silu_zr, silu_za = zr * sig_zr, za * sig_za
        u_rnn = (hr_t * silu_zr).astype(BF16)
        u_attn = (o * silu_za).astype(BF16)
        y_rnn = _dot(u_rnn, wr[...])
        y_attn = _dot(u_attn, wa[...])
        sr = _sigmoid(gr_ref[...] + bg_ref[:, :D_MODEL])
        sa = _sigmoid(ga_ref[...] + bg_ref[:, D_MODEL:])
        merged = (sr * y_rnn + sa * y_attn).astype(BF16)
        mo = _dot(merged, wo[...])
        x2 = x_ref[...] + gate * mo
        rstd = lax.rsqrt(jnp.mean(x2 * x2, axis=-1, keepdims=True) + NORM_EPS)
        xn = x2 * rstd
        err = xn * gfin - t_ref[...]
        loss_ref[...] += 0.5 * jnp.sum(jnp.sum(err * err, axis=-1, keepdims=True) * (1.0 / D_MODEL),
                                       axis=0, keepdims=True)

        dy = err * (1.0 / D_MODEL)
        ggf_ref[...] += jnp.sum(dy * xn, axis=0, keepdims=True)
        dxn = dy * gfin
        dx2 = rstd * (dxn - xn * jnp.mean(dxn * xn, axis=-1, keepdims=True))
        dx2_ref[...] = dx2
        dgate_ref[...] += jnp.sum(dx2 * mo, axis=0, keepdims=True)
        dmo = (dx2 * gate).astype(BF16)
        dmerged = _dot_nt(dmo, wo[...])
        mg_ref[...] = merged
        dmo_ref[...] = dmo
        dy_rnn = (dmerged * sr).astype(BF16)
        dy_attn = (dmerged * sa).astype(BF16)
        dg_r = dmerged * y_rnn * sr * (1.0 - sr)
        dg_a = dmerged * y_attn * sa * (1.0 - sa)
        dgr_ref[...] = dg_r.astype(BF16)
        dga_ref[...] = dg_a.astype(BF16)
        gbg_ref[:, :D_MODEL] += jnp.sum(dg_r, axis=0, keepdims=True)
        gbg_ref[:, D_MODEL:] += jnp.sum(dg_a, axis=0, keepdims=True)
        du_rnn = _dot_nt(dy_rnn, wr[...])
        du_attn = _dot_nt(dy_attn, wa[...])
        ur_ref[...] = u_rnn
        dyr_ref[...] = dy_rnn
        ua_ref[...] = u_attn
        dya_ref[...] = dy_attn
        dhr_ref[...] = du_rnn * silu_zr
        dzr_ref[...] = (du_rnn * hr_t * (sig_zr * (1.0 + zr * (1.0 - sig_zr)))).astype(BF16)
        dza_ref[...] = (du_attn * o * (sig_za * (1.0 + za * (1.0 - sig_za)))).astype(BF16)
        d_o = du_attn * silu_za
        for hh in range(N_HEADS):
            do_ref[hh] = d_o[:, hh * HEAD_DIM:(hh + 1) * HEAD_DIM]

    row = pl.BlockSpec((tm, D_MODEL), lambda i: (i, 0))
    piece = lambda slot: pl.BlockSpec((tm, D_MODEL), lambda i: (i, slot))
    hm = pl.BlockSpec((N_HEADS, tm, HEAD_DIM), lambda i: (0, i, 0))
    const = lambda cols: pl.BlockSpec((1, cols), lambda i: (0, 0))
    any_spec = pl.BlockSpec(memory_space=pl.ANY)
    act_f32 = jax.ShapeDtypeStruct((seq, D_MODEL), F32)
    act_bf16 = jax.ShapeDtypeStruct((seq, D_MODEL), BF16)
    return pl.pallas_call(
        body, name="hub",
        out_shape=[act_f32, act_f32, act_bf16, jax.ShapeDtypeStruct((N_HEADS, seq, HEAD_DIM), F32),
                   act_bf16, act_bf16, act_bf16] + [act_bf16] * 6 + [
                   jax.ShapeDtypeStruct((1, D_MODEL), F32), jax.ShapeDtypeStruct((1, 2 * D_MODEL), F32),
                   jax.ShapeDtypeStruct((1, D_MODEL), F32), jax.ShapeDtypeStruct((1, 1), F32)],
        grid=(nsteps,),
        in_specs=[row, row, row, piece(1), piece(5), piece(6), piece(7), hm,
                  const(3 * D_MODEL), const(3 * D_MODEL), const(2 * D_MODEL), const(D_MODEL),
                  any_spec, any_spec, any_spec],
        out_specs=[row, row, row, hm, row, row, row] + [row] * 6 + [
                   const(D_MODEL), const(2 * D_MODEL), const(D_MODEL), const(1)],
        scratch_shapes=[pltpu.VMEM((D_MODEL, D_MODEL), BF16)] * 3 + [pltpu.SemaphoreType.DMA],
        compiler_params=_params(("arbitrary",), VMEM_LIMIT),
    )(x, tgt, hr, pf, pf, pf, pf, o_hm, mod, b_mod, b_gate, g_final, w_out_rnn, w_out_attn, w_o)


def _pair_grads(name, lefts, rights):
    n = len(rights)
    shared = len(lefts) == 1
    seq = rights[0].shape[0]
    tk = WGRAD_ROWS
    nk = seq // tk

    def body(*refs):
        l_refs, r_refs = refs[:len(lefts)], refs[len(lefts):len(lefts) + n]
        out_ref, low_ref = refs[len(lefts) + n:]
        j, kk = pl.program_id(0), pl.program_id(1)

        @pl.when(kk == 0)
        def _():
            out_ref[...] = jnp.zeros_like(out_ref)

        for m in range(n):
            @pl.when(j == m)
            def _(m=m):
                out_ref[...] += _dot_tn(l_refs[0 if shared else m][...], r_refs[m][...])

        @pl.when(kk == nk - 1)
        def _():
            low_ref[...] = out_ref[...].astype(BF16)

    def spec(m):
        return pl.BlockSpec((tk, D_MODEL), lambda j, kk: (jnp.where(j == m, kk, jnp.where(j < m, 0, nk - 1)), 0))

    left_specs = [pl.BlockSpec((tk, D_MODEL), lambda j, kk: (kk, 0))] if shared else [spec(m) for m in range(n)]
    out_spec = pl.BlockSpec((None, D_MODEL, D_MODEL), lambda j, kk: (j, 0, 0))
    return pl.pallas_call(
        body, name=name,
        out_shape=[jax.ShapeDtypeStruct((n, D_MODEL, D_MODEL), F32), jax.ShapeDtypeStruct((n, D_MODEL, D_MODEL), BF16)],
        grid=(n, nk),
        in_specs=left_specs + [spec(m) for m in range(n)],
        out_specs=[out_spec, out_spec],
        compiler_params=_params(("arbitrary", "arbitrary"), VMEM_LIMIT),
    )(*lefts, *rights)


def _dh_dx(pieces, w_near, w_far, x, dx2, mod, b_mod, g_norm):
    seq = x.shape[0]
    tm = DX_ROWS

    def body(*refs):
        p_refs = refs[:8]
        near_hbm, far_hbm, x_ref, dx2_ref, mod_ref, bmod_ref, g_ref = refs[8:15]
        gx_ref, dshift_ref, dscale_ref, ggn_ref, w_scr, sem = refs[15:]
        step = pl.program_id(0)

        @pl.when(step == 0)
        def _():
            me = _my_pos()
            sib = _flip(me, 1)
            moves = [(near_hbm, _index(_flip(me, 2 * m))) for m in range(4)] + [(near_hbm, _index(sib))]
            moves += [(far_hbm, _index(_flip(sib, 2 * m))) for m in range(1, 4)]
            for src, t in moves:
                cp = pltpu.make_async_copy(src.at[t], w_scr.at[t], sem)
                cp.start()
                cp.wait()
            for ref in (dshift_ref, dscale_ref, ggn_ref):
                ref[...] = jnp.zeros_like(ref)

        dh = _dot_nt(p_refs[0][...], w_scr[0])
        for j in range(1, 8):
            dh = dh + _dot_nt(p_refs[j][...], w_scr[j])
        scale1 = 1.0 + mod_ref[:, D_MODEL:2 * D_MODEL] + bmod_ref[:, D_MODEL:2 * D_MODEL]
        g = g_ref[...]
        xf = x_ref[...]
        rstd_t = lax.rsqrt(jnp.mean(xf * xf, axis=-1, keepdims=True) + NORM_EPS)
        xn = xf * rstd_t
        dshift_ref[...] += jnp.sum(dh, axis=0, keepdims=True)
        dscale_ref[...] += jnp.sum(dh * (xn * g), axis=0, keepdims=True)
        ggn_ref[...] += jnp.sum(dh * scale1 * xn, axis=0, keepdims=True)
        dxn = dh * (g * scale1)
        gx_ref[...] = rstd_t * (dxn - xn * jnp.mean(dxn * xn, axis=-1, keepdims=True)) + dx2_ref[...]

    row = pl.BlockSpec((tm, D_MODEL), lambda i: (i, 0))
    const = lambda cols: pl.BlockSpec((1, cols), lambda i: (0, 0))
    vec = jax.ShapeDtypeStruct((1, D_MODEL), F32)
    return pl.pallas_call(
        body, name="dh_dx",
        out_shape=[jax.ShapeDtypeStruct((seq, D_MODEL), F32), vec, vec, vec],
        grid=(seq // tm,),
        in_specs=[row] * 8 + [pl.BlockSpec(memory_space=pl.ANY), pl.BlockSpec(memory_space=pl.ANY), row, row,
                              const(3 * D_MODEL), const(3 * D_MODEL), const(D_MODEL)],
        out_specs=[row, const(D_MODEL), const(D_MODEL), const(D_MODEL)],
        scratch_shapes=[pltpu.VMEM((8, D_MODEL, D_MODEL), BF16), pltpu.SemaphoreType.DMA],
        compiler_params=_params(("arbitrary",), VMEM_LIMIT),
    )(*pieces, w_near, w_far, x, dx2, mod, b_mod, g_norm)


def _adamw(name, w, g, m, v, recv=None):
    rows, cols = w.shape
    tr = rows if rows <= 256 else 256

    def body(*refs):
        w_ref, g_ref, m_ref, v_ref = refs[:4]
        d_ref, nm_ref, nv_ref = refs[-3:] if recv is None else refs[5:8]
        gv = g_ref[...]
        if recv is not None:
            r_ref, g_out = refs[4], refs[8]
            gv = ((gv + r_ref[0].astype(F32)) + r_ref[1].astype(F32)) + r_ref[2].astype(F32)
            g_out[...] = gv
        nm = ADAM_B1 * m_ref[...] + (1.0 - ADAM_B1) * gv
        nv = ADAM_B2 * v_ref[...] + (1.0 - ADAM_B2) * (gv * gv)
        m_hat = nm / (1.0 - ADAM_B1 ** ADAM_STEP)
        v_hat = nv / (1.0 - ADAM_B2 ** ADAM_STEP)
        d_ref[...] = -ADAM_LR * (m_hat / (jnp.sqrt(v_hat) + ADAM_EPS) + ADAM_WD * w_ref[...])
        nm_ref[...] = nm
        nv_ref[...] = nv

    spec = pl.BlockSpec((tr, cols), lambda i: (i, 0))
    shape = jax.ShapeDtypeStruct((rows, cols), F32)
    if recv is None:
        return pl.pallas_call(
            body, name=name, out_shape=[shape, shape, shape], grid=(rows // tr,),
            in_specs=[spec] * 4, out_specs=[spec] * 3,
            compiler_params=_params(("arbitrary",)),
        )(w, g, m, v)
    return pl.pallas_call(
        body, name=name, out_shape=[shape] * 4, grid=(rows // tr,),
        in_specs=[spec] * 4 + [pl.BlockSpec((3, tr, cols), lambda i: (0, i, 0))], out_specs=[spec] * 4,
        compiler_params=_params(("arbitrary",)),
    )(w, g, m, v, recv)


def kernel(x, c, positions, g_norm, w_mod, b_mod, w_in, b_gate, conv_w, conv_b, w_a, b_a, w_x, b_x, lam, w_out_rnn, w_out_attn, w_o, g_final, loss_target, m_g_norm, m_w_mod, m_b_mod, m_w_in, m_b_gate, m_conv_w, m_conv_b, m_w_a, m_b_a, m_w_x, m_b_x, m_lam, m_w_out_rnn, m_w_out_attn, m_w_o, m_g_final, v_g_norm, v_w_mod, v_b_mod, v_w_in, v_b_gate, v_conv_w, v_conv_b, v_w_a, v_b_a, v_w_x, v_b_x, v_lam, v_w_out_rnn, v_w_out_attn, v_w_o, v_g_final):
    seq = x.shape[1]
    me = _index(_my_pos())
    xs, tgt = x[0], loss_target[0]

    pos = positions[0].astype(F32)[:, None]
    inv_freq = ROPE_THETA ** (-jnp.arange(0, 2 * ROT_HALF, 2, dtype=F32) / (2 * ROT_HALF))
    ang = pos * inv_freq
    rest = HEAD_DIM - 2 * ROT_HALF
    cosf = jnp.concatenate([jnp.cos(ang), jnp.cos(ang), jnp.ones((seq, rest), F32)], axis=1)
    sinf = jnp.concatenate([-jnp.sin(ang), jnp.sin(ang), jnp.zeros((seq, rest), F32)], axis=1)
    keep = (positions[0] != 0).astype(F32)[:, None]

    both = _ag_small("gather_c_conv_w", jnp.concatenate(
        [jnp.broadcast_to(c, (SUBLANES, D_MODEL)), jnp.pad(conv_w[0], ((0, SUBLANES - 4), (0, 0)))], axis=1))
    c_all, conv_w8 = both[:, 0, :D_MODEL], both[:, :, D_MODEL:]
    mod_cols = w_mod.shape[2]
    mod_part = _ag_small("gather_mod", _mod_fwd(c_all, w_mod[0]))
    mod = lax.dynamic_index_in_dim(mod_part, me, axis=1, keepdims=False).reshape(1, N_DEV * mod_cols)

    slot = lambda t: lax.dynamic_update_slice(lax.empty((N_DEV,) + t.shape, t.dtype), t[None], (me, 0, 0))
    w_in_own = w_in[0].astype(BF16)
    mod, w_in_own = lax.optimization_barrier((mod, w_in_own))
    first = _split_start("gather_w_in_start", _own_block_copies, 4, [w_in_own], [slot(w_in_own)])
    mod = mod + first[4][0:1, 0:1]

    blocks = lambda t: t.reshape(RNN_BLOCKS, 1, 128)
    rnn_params = (conv_w8, blocks(conv_b), w_a[0], blocks(b_a), w_x[0], blocks(b_x), blocks(lam))

    h = _norm(xs, mod, b_mod, g_norm)
    ids = lambda ks: jnp.bitwise_xor(me, jnp.array(ks, jnp.int32)).astype(jnp.int32)
    pf = _proj("proj_own", h, first[2][0][None], jnp.zeros((1,), jnp.int32), ids([0]), cosf, sinf, None)
    _, (w_in_near,) = _split_wait("gather_w_in_wait", _own_block_copies, first, pf)
    second = _split_start("forward_w_in_start", _forward_copies, 3, [w_in_near],
                          [lax.empty(w_in_near.shape, w_in_near.dtype)])
    near = ids([1, 2, 4, 6])
    pf = _proj("proj_near", h, second[2][0], near, near, cosf, sinf, pf)
    (w_in_near,), (w_in_far,) = _split_wait("forward_w_in_wait", _forward_copies, second, pf)
    far = ids([3, 5, 7])
    pf = _proj("proj_far", h, w_in_far, far, far, cosf, sinf, pf)
    late = [w_out_rnn[0].astype(BF16), w_out_attn[0].astype(BF16), w_o[0].astype(BF16)]
    pf, late = lax.optimization_barrier((pf, late))
    flight = _split_start("gather_out_weights_start", _peer_copies, 7 * len(late), late, [slot(t) for t in late])
    rnn_params = (rnn_params[0], rnn_params[1] + flight[4][0:1, 0:1]) + rnn_params[2:]
    hr = _rnn_fwd(pf, keep, *rnn_params)
    o, lses = _attn_fwd(pf)

    w_or_all, w_oa_all, w_o_all = (t.reshape(D_MODEL, D_MODEL) for t in _split_wait(
        "gather_out_weights_wait", _peer_copies, flight, o)[1])
    (dx2, dhr, dz_rnn, d_o, dz_attn, dg_r, dg_a, u_rnn, dy_rnn, u_attn, dy_attn, merged, dmo,
     gp_g_final, gp_b_gate, dgate, loss_part) = _hub(
        xs, tgt, hr, pf, o, mod, b_mod, b_gate, g_final.reshape(1, D_MODEL), w_or_all, w_oa_all, w_o_all)
    gp_out, gp_out_low = _pair_grads("out_grads", [u_rnn, u_attn, merged], [dy_rnn, dy_attn, dmo])
    dq, dk, dv = _attn_bwd(pf, d_o, o, lses, cosf, sinf)
    dx_rnn, gp_conv_w, gp_conv_b, gp_w_a, gp_b_a, gp_w_x, gp_b_x, gp_lam = _rnn_bwd(pf, hr, dhr, keep, *rnn_params)
    pieces = [dx_rnn, dz_rnn, dq, dk, dv, dz_attn, dg_r, dg_a]
    gp_w_in, gp_w_in_low = _pair_grads("w_in_grad", [h], pieces)

    by_target = lambda t: [t[i].reshape(N_DEV, 128, D_MODEL) for i in range(3)]
    stacks = [gp_w_in] + by_target(gp_out)
    from_sib = _rs_to_sibling("rs_sibling", [gp_w_in_low] + by_target(gp_out_low))
    targets = jnp.bitwise_xor(me, 2 * jnp.arange(4, dtype=jnp.int32)).astype(jnp.int32)
    sums = [_add_sibling("rs_add_sibling_%d" % a, s_, r_, targets) for a, (s_, r_) in enumerate(zip(stacks, from_sib))]
    sends = [send for _, send in sums]
    reduce_flight = _split_start("rs_chips_start", _chip_copies, 3 * len(sends), sends,
                                 [lax.empty(t.shape, t.dtype) for t in sends])

    mod_after = mod + reduce_flight[4][0:1, 0:1]
    grad_x, dshift, dscale, gp_g_norm = _dh_dx(pieces, w_in_near, w_in_far, xs, dx2, mod_after, b_mod, g_norm)

    dmod = jnp.concatenate([dshift, dscale, dgate], axis=1)
    dmod_all = _ag_small("gather_dmod", jnp.broadcast_to(dmod, (SUBLANES, 3 * D_MODEL)))[:, 0, :]
    dmod_cols = lax.dynamic_slice_in_dim(dmod_all, me * mod_cols, mod_cols, axis=1)
    g_b_mod, g_w_mod = _mod_bwd(c_all, dmod_all, dmod_cols)

    flat = lambda t: t.reshape(-1, 128)
    small = [flat(gp_g_norm), flat(gp_b_gate), flat(gp_conv_b), flat(gp_b_a), flat(gp_b_x), flat(gp_lam),
             flat(gp_g_final), flat(gp_conv_w), jnp.broadcast_to(loss_part, (SUBLANES, 128)),
             flat(gp_w_a), flat(gp_w_x)]
    sizes = [t.shape[0] for t in small]
    small.append(jnp.zeros((-sum(sizes) % (2 * SUBLANES), 128), F32))
    total = _allreduce_small("allreduce_small_grads", jnp.concatenate(small, axis=0))
    offs = [sum(sizes[:i]) for i in range(len(sizes))]
    (g_g_norm, g_b_gate, g_conv_b, g_b_a, g_b_x, g_lam, g_g_final, g_conv_w_all, loss_rows, g_w_a, g_w_x) = (
        total[o_:o_ + s_] for o_, s_ in zip(offs, sizes))
    loss = loss_rows[0, 0]
    g_conv_w = lax.dynamic_index_in_dim(g_conv_w_all.reshape(RNN_BLOCKS, SUBLANES, 128), me, axis=0,
                                        keepdims=False)[:4]

    _, from_chips = _split_wait("rs_chips_wait", _chip_copies, reduce_flight, total)

    results = {}
    sharded = (("w_in", w_in, m_w_in, v_w_in, (D_MODEL, D_MODEL)),
               ("w_out_rnn", w_out_rnn, m_w_out_rnn, v_w_out_rnn, (128, D_MODEL)),
               ("w_out_attn", w_out_attn, m_w_out_attn, v_w_out_attn, (128, D_MODEL)),
               ("w_o", w_o, m_w_o, v_w_o, (128, D_MODEL)))
    for (name, w_, m_, v_, shape2), (own, _), arrived in zip(sharded, sums, from_chips):
        d_, nm_, nv_, g_ = _adamw("adamw_" + name, w_.reshape(shape2), own, m_.reshape(shape2), v_.reshape(shape2),
                                  arrived)
        results[name] = (g_, d_, nm_, nv_)
    shape2 = (D_MODEL, mod_cols)
    results["w_mod"] = (g_w_mod,) + tuple(_adamw("adamw_w_mod", w_mod.reshape(shape2), g_w_mod,
                                                 m_w_mod.reshape(shape2), v_w_mod.reshape(shape2)))
    lanes = (("g_norm", g_norm, g_g_norm, m_g_norm, v_g_norm), ("b_mod", b_mod, g_b_mod, m_b_mod, v_b_mod),
             ("b_gate", b_gate, g_b_gate, m_b_gate, v_b_gate), ("conv_w", conv_w, g_conv_w, m_conv_w, v_conv_w),
             ("conv_b", conv_b, g_conv_b, m_conv_b, v_conv_b), ("w_a", w_a, g_w_a, m_w_a, v_w_a),
             ("b_a", b_a, g_b_a, m_b_a, v_b_a), ("w_x", w_x, g_w_x, m_w_x, v_w_x), ("b_x", b_x, g_b_x, m_b_x, v_b_x),
             ("lam", lam, g_lam, m_lam, v_lam), ("g_final", g_final, g_g_final, m_g_final, v_g_final))
    for name, w_, g_, m_, v_ in lanes:
        rows128 = lambda t: t.reshape(-1, 128)
        results[name] = (g_,) + tuple(_adamw("adamw_" + name, rows128(w_), rows128(g_), rows128(m_), rows128(v_)))
    order = ("g_norm", "w_mod", "b_mod", "w_in", "b_gate", "conv_w", "conv_b", "w_a", "b_a", "w_x", "b_x", "lam",
             "w_out_rnn", "w_out_attn", "w_o", "g_final")
    given = dict(g_norm=g_norm, w_mod=w_mod, b_mod=b_mod, w_in=w_in, b_gate=b_gate, conv_w=conv_w, conv_b=conv_b,
                 w_a=w_a, b_a=b_a, w_x=w_x, b_x=b_x, lam=lam, w_out_rnn=w_out_rnn, w_out_attn=w_out_attn, w_o=w_o,
                 g_final=g_final)
    outs = [[results[name][k].reshape(given[name].shape) for name in order] for k in range(4)]
    return (loss, grad_x[None], *outs[0], *outs[1], *outs[2], *outs[3])
```

```python
import jax
import jax.numpy as jnp
from jax import lax
from jax.experimental import pallas as pl
from jax.experimental.pallas import tpu as pltpu

F32 = jnp.float32
BF16 = jnp.bfloat16
MESH = pl.DeviceIdType.MESH

D_MODEL = 1024
N_HEADS = 8
HEAD_DIM = 128
RNN_BLOCKS = 8
N_DEV = 8
ROT_HALF = 16
ROPE_THETA = 500000.0
DILATIONS = (1, 4, 16)
KEY_BLOCK = 128
SPAN = KEY_BLOCK * DILATIONS[-1]
ATTN_SCALE = HEAD_DIM ** -0.5
NORM_EPS = 1e-6
LRU_C = 8.0
NEG_INF = -1e30
ADAM_LR, ADAM_B1, ADAM_B2, ADAM_EPS, ADAM_WD, ADAM_STEP = 0.001, 0.9, 0.999, 1e-08, 0.01, 10

SUBLANES = 8
VMEM_LIMIT = 56 * 1024 * 1024
PROJ_ROWS = 1024
RNN_ROWS = 2048
HUB_ROWS = 256
DX_ROWS = 512
WGRAD_ROWS = 1024
ADD_ROWS = 256


def _params(sem=None, vmem=None):
    return pltpu.CompilerParams(dimension_semantics=sem, vmem_limit_bytes=vmem)


def _dot(a, b):
    return jnp.dot(a, b, preferred_element_type=F32)


def _dot_nt(a, b):
    return lax.dot_general(a, b, (((1,), (1,)), ((), ())), preferred_element_type=F32)


def _dot_tn(a, b):
    return lax.dot_general(a, b, (((0,), (0,)), ((), ())), preferred_element_type=F32)


def _sigmoid(z):
    return 1.0 / (1.0 + jnp.exp(-z))


def _expm1_nonpos(z, exp_z):
    return jnp.where(z > -0.01, z * (1.0 + 0.5 * z), exp_z - 1.0)


def _my_pos():
    return lax.axis_index("x"), lax.axis_index("y"), lax.axis_index("c")


def _flip(pos, k):
    x, y, c = pos
    return ((1 - x) if k & 4 else x, (1 - y) if k & 2 else y, (1 - c) if k & 1 else c)


def _index(pos):
    return 4 * pos[0] + 2 * pos[1] + pos[2]


def _ag_small(name, v):
    rows, cols = v.shape

    def body(v_ref, out_ref, send_sems, recv_sems):
        me = _my_pos()
        out_ref[_index(me)] = v_ref[...]
        sends = []
        for k in range(1, N_DEV):
            cp = pltpu.make_async_remote_copy(
                src_ref=v_ref, dst_ref=out_ref.at[_index(me)], send_sem=send_sems.at[k - 1],
                recv_sem=recv_sems.at[k - 1], device_id=_flip(me, k), device_id_type=MESH)
            cp.start()
            sends.append(cp)
        for k in range(1, N_DEV):
            peer = _flip(me, k)
            pltpu.make_async_remote_copy(
                src_ref=v_ref, dst_ref=out_ref.at[_index(peer)], send_sem=send_sems.at[k - 1],
                recv_sem=recv_sems.at[k - 1], device_id=peer, device_id_type=MESH).wait_recv()
        for cp in sends:
            cp.wait_send()

    return pl.pallas_call(
        body, name=name,
        out_shape=jax.ShapeDtypeStruct((N_DEV, rows, cols), v.dtype),
        in_specs=[pl.BlockSpec(memory_space=pltpu.VMEM)],
        out_specs=pl.BlockSpec(memory_space=pltpu.VMEM),
        scratch_shapes=[pltpu.SemaphoreType.DMA((N_DEV - 1,)), pltpu.SemaphoreType.DMA((N_DEV - 1,))],
        compiler_params=_params(None, VMEM_LIMIT),
    )(v)


def _split_start(name, make_copies, nsem, srcs, lands):
    n, k = len(srcs), len(lands)

    def body(*refs):
        for cp in make_copies(refs[:n], refs[n:n + k], refs[n + k], refs[n + k + 1]):
            cp.start()
        refs[-1][...] = jnp.zeros_like(refs[-1])

    hbm = pl.BlockSpec(memory_space=pltpu.HBM)
    sem = pl.BlockSpec(memory_space=pltpu.SEMAPHORE)
    arrays = [*srcs, *lands]
    outs = pl.pallas_call(
        body, name=name,
        out_shape=(pltpu.SemaphoreType.DMA((nsem,)), pltpu.SemaphoreType.DMA((nsem,)),
                   *[pltpu.HBM(t.shape, t.dtype) for t in arrays], jax.ShapeDtypeStruct((SUBLANES, 128), F32)),
        in_specs=[hbm] * (n + k),
        out_specs=(sem, sem, *[hbm] * (n + k), pl.BlockSpec(memory_space=pltpu.VMEM)),
        input_output_aliases={i: 2 + i for i in range(n + k)},
        compiler_params=pltpu.CompilerParams(has_side_effects=pltpu.SideEffectType.DATAFLOW_SIDE_EFFECTING),
    )(*[pltpu.with_memory_space_constraint(t, pltpu.HBM) for t in arrays])
    return outs[0], outs[1], outs[2:2 + n], outs[2 + n:2 + n + k], outs[-1]


def _split_wait(name, make_copies, flight, after):
    send_sems, recv_sems, srcs, lands, _ = flight
    n, k = len(srcs), len(lands)

    def body(*refs):
        for cp in make_copies(refs[:n], refs[n:n + k], refs[n + k], refs[n + k + 1]):
            cp.wait_send()
            cp.wait_recv()

    hbm = pl.BlockSpec(memory_space=pltpu.HBM)
    sem = pl.BlockSpec(memory_space=pltpu.SEMAPHORE)
    arrays = [*srcs, *lands]
    outs = pl.pallas_call(
        body, name=name, out_shape=tuple(pltpu.HBM(t.shape, t.dtype) for t in arrays),
        in_specs=[hbm] * (n + k) + [sem, sem, pl.BlockSpec(memory_space=pl.ANY)],
        out_specs=[hbm] * (n + k),
        input_output_aliases={i: i for i in range(n + k)},
        compiler_params=pltpu.CompilerParams(has_side_effects=pltpu.SideEffectType.DATAFLOW_SIDE_EFFECTING),
    )(*arrays, send_sems, recv_sems, after)
    return outs[:n], outs[n:]


def _remote(src, dst, send_sems, recv_sems, k, to):
    return pltpu.make_async_remote_copy(src_ref=src, dst_ref=dst, send_sem=send_sems.at[k], recv_sem=recv_sems.at[k],
                                        device_id=to, device_id_type=MESH)


def _peer_copies(shards, lands, send_sems, recv_sems):
    me = _my_pos()
    return [_remote(shards[a], lands[a].at[_index(me)], send_sems, recv_sems, a * 7 + k - 1, _flip(me, k))
            for a in range(len(shards)) for k in range(1, N_DEV)]


def _own_block_copies(shards, lands, send_sems, recv_sems):
    me = _my_pos()
    return [_remote(shards[0], lands[0].at[_index(me)], send_sems, recv_sems, i, _flip(me, k))
            for i, k in enumerate((1, 2, 4, 6))]


def _forward_copies(arrived, lands, send_sems, recv_sems):
    me = _my_pos()
    return [_remote(arrived[0].at[_index(_flip(me, 2 * m))], lands[0].at[_index(_flip(me, 2 * m))],
                    send_sems, recv_sems, m - 1, _flip(me, 1)) for m in range(1, 4)]


def _rs_to_sibling(name, stacks):
    n = len(stacks)

    def body(*refs):
        ins, outs = refs[:n], refs[n:2 * n]
        send_sems, recv_sems = refs[2 * n:]
        me = _my_pos()
        sib = _flip(me, 1)
        sends = []
        for a, (_, which) in enumerate(stacks):
            by_target = ins[a] if which is None else ins[a].at[which]
            for m in range(4):
                target = _flip(sib, 2 * m)
                cp = pltpu.make_async_remote_copy(
                    src_ref=by_target.at[_index(target)], dst_ref=outs[a].at[m],
                    send_sem=send_sems.at[a * 4 + m], recv_sem=recv_sems.at[a * 4 + m],
                    device_id=sib, device_id_type=MESH)
                cp.start()
                sends.append(cp)
        for cp in sends:
            cp.wait_recv()
        for cp in sends:
            cp.wait_send()

    any_spec = pl.BlockSpec(memory_space=pl.ANY)
    return pl.pallas_call(
        body, name=name,
        out_shape=[jax.ShapeDtypeStruct((4,) + s.shape[-2:], s.dtype) for s, _ in stacks],
        in_specs=[any_spec] * n, out_specs=[any_spec] * n,
        scratch_shapes=[pltpu.SemaphoreType.DMA((4 * n,)), pltpu.SemaphoreType.DMA((4 * n,))],
    )(*[s for s, _ in stacks])


def _chip_copies(srcs, lands, send_sems, recv_sems):
    me = _my_pos()
    return [_remote(srcs[a].at[m - 1], lands[a].at[m - 1], send_sems, recv_sems, a * 3 + m - 1, _flip(me, 2 * m))
            for a in range(len(srcs)) for m in range(1, 4)]


def _add_sibling(name, stack, recv, targets):
    stack, which = stack
    rows, cols = stack.shape[-2:]
    tr = min(rows, ADD_ROWS)

    def by_target(index):
        if which is None:
            return pl.BlockSpec((None, tr, cols), lambda *g: (index(*g), g[-2], 0))
        return pl.BlockSpec((None, None, tr, cols), lambda *g: (which, index(*g), g[-2], 0))

    def own_body(t_ref, a_ref, b_ref, o_ref):
        o_ref[...] = a_ref[...] + b_ref[...].astype(F32)

    own = pl.pallas_call(
        own_body, name=name + "_own",
        out_shape=jax.ShapeDtypeStruct((rows, cols), F32),
        grid_spec=pltpu.PrefetchScalarGridSpec(
            num_scalar_prefetch=1, grid=(rows // tr,),
            in_specs=[by_target(lambda i, t: t[0]),
                      pl.BlockSpec((None, tr, cols), lambda i, t: (0, i, 0))],
            out_specs=pl.BlockSpec((tr, cols), lambda i, t: (i, 0))),
        compiler_params=_params(("arbitrary",)),
    )(targets, stack, recv)

    def send_body(t_ref, a_ref, b_ref, o_ref):
        o_ref[...] = (a_ref[...] + b_ref[...].astype(F32)).astype(BF16)

    send = pl.pallas_call(
        send_body, name=name + "_send",
        out_shape=jax.ShapeDtypeStruct((3, rows, cols), BF16),
        grid_spec=pltpu.PrefetchScalarGridSpec(
            num_scalar_prefetch=1, grid=(3, rows // tr),
            in_specs=[by_target(lambda m, i, t: t[m + 1]),
                      pl.BlockSpec((None, tr, cols), lambda m, i, t: (m + 1, i, 0))],
            out_specs=pl.BlockSpec((None, tr, cols), lambda m, i, t: (m, i, 0))),
        compiler_params=_params(("arbitrary", "arbitrary")),
    )(targets, stack, recv)
    return own, send


def _allreduce_small(name, v):
    rows, cols = v.shape
    half = rows // 2
    assert rows % (2 * SUBLANES) == 0

    def body(v_ref, out_ref, from_sib, chip_half, from_chips, send_sems, recv_sems):
        me = _my_pos()
        sib = _flip(me, 1)
        mine = pl.ds(pl.multiple_of(me[2] * half, SUBLANES), half)
        theirs = pl.ds(pl.multiple_of((1 - me[2]) * half, SUBLANES), half)

        def copy(k, src, dst, to):
            return pltpu.make_async_remote_copy(src_ref=src, dst_ref=dst, send_sem=send_sems.at[k],
                                                recv_sem=recv_sems.at[k], device_id=to, device_id_type=MESH)

        to_sib = copy(0, v_ref.at[theirs], from_sib, sib)
        to_sib.start()
        to_sib.wait_recv()
        chip_half[...] = v_ref[mine, :] + from_sib[...]
        to_chips = [copy(m, chip_half, from_chips.at[m - 1], _flip(me, 2 * m)) for m in range(1, 4)]
        for cp in to_chips:
            cp.start()
        for cp in to_chips:
            cp.wait_recv()
        my_chip = 2 * me[0] + me[1]
        total = None
        for chip in range(4):
            slot = jnp.maximum(jnp.bitwise_xor(chip, my_chip) - 1, 0)
            part = jnp.where(chip == my_chip, chip_half[...], from_chips[slot])
            total = part if total is None else total + part
        out_ref[mine, :] = total
        swap = copy(4, out_ref.at[mine], out_ref.at[mine], sib)
        swap.start()
        copy(4, out_ref.at[theirs], out_ref.at[theirs], sib).wait_recv()
        for cp in [to_sib, swap] + to_chips:
            cp.wait_send()

    return pl.pallas_call(
        body, name=name, out_shape=jax.ShapeDtypeStruct((rows, cols), F32),
        in_specs=[pl.BlockSpec(memory_space=pltpu.VMEM)],
        out_specs=pl.BlockSpec(memory_space=pltpu.VMEM),
        scratch_shapes=[pltpu.VMEM((half, cols), F32), pltpu.VMEM((half, cols), F32),
                        pltpu.VMEM((3, half, cols), F32),
                        pltpu.SemaphoreType.DMA((5,)), pltpu.SemaphoreType.DMA((5,))],
        compiler_params=_params(None, VMEM_LIMIT),
    )(v)


def _mod_fwd(c_all, w_mod):
    def body(c_ref, w_ref, o_ref):
        c = c_ref[...]
        o_ref[...] = jnp.dot(c * _sigmoid(c), w_ref[...], preferred_element_type=F32,
                             precision=lax.Precision.HIGHEST)

    return pl.pallas_call(
        body, name="mod_fwd", out_shape=jax.ShapeDtypeStruct((N_DEV, w_mod.shape[1]), F32),
    )(c_all, w_mod)


def _mod_bwd(c_all, dmod_all, dmod_cols):
    def body(c_ref, da_ref, dc_ref, gb_ref, gw_ref):
        c = c_ref[...]
        acc = da_ref[0:1, :]
        for b in range(1, N_DEV):
            acc = acc + da_ref[b:b + 1, :]
        gb_ref[...] = acc
        gw_ref[...] = lax.dot_general(c * _sigmoid(c), dc_ref[...], (((0,), (0,)), ((), ())),
                                      preferred_element_type=F32, precision=lax.Precision.HIGHEST)

    return pl.pallas_call(
        body, name="mod_bwd",
        out_shape=[jax.ShapeDtypeStruct((1, dmod_all.shape[1]), F32),
                   jax.ShapeDtypeStruct((c_all.shape[1], dmod_cols.shape[1]), F32)],
    )(c_all, dmod_all, dmod_cols)


def _rope_partner(t):
    lane = lax.broadcasted_iota(jnp.int32, t.shape, 1)
    return jnp.where(lane < ROT_HALF, pltpu.roll(t, HEAD_DIM - ROT_HALF, 1), pltpu.roll(t, ROT_HALF, 1))


def _norm(x, mod, b_mod, g_norm):
    seq = x.shape[0]
    tm = PROJ_ROWS

    def body(x_ref, mod_ref, bmod_ref, g_ref, h_ref):
        xf = x_ref[...]
        rstd = lax.rsqrt(jnp.mean(xf * xf, axis=-1, keepdims=True) + NORM_EPS)
        shift = mod_ref[:, 0:D_MODEL] + bmod_ref[:, 0:D_MODEL]
        scale = mod_ref[:, D_MODEL:2 * D_MODEL] + bmod_ref[:, D_MODEL:2 * D_MODEL]
        h_ref[...] = (((xf * rstd) * g_ref[...]) * (1.0 + scale) + shift).astype(BF16)

    row = pl.BlockSpec((tm, D_MODEL), lambda i: (i, 0))
    const = lambda cols: pl.BlockSpec((1, cols), lambda i: (0, 0))
    return pl.pallas_call(
        body, name="norm", out_shape=jax.ShapeDtypeStruct((seq, D_MODEL), BF16), grid=(seq // tm,),
        in_specs=[row, const(3 * D_MODEL), const(3 * D_MODEL), const(D_MODEL)], out_specs=row,
        compiler_params=_params(("arbitrary",), VMEM_LIMIT),
    )(x, mod, b_mod, g_norm)


def _proj(name, h, w, slots, pieces, cosf, sinf, prior):
    seq = h.shape[0]
    tm = PROJ_ROWS
    count = pieces.shape[0]

    def body(slots_ref, pieces_ref, h_ref, w_ref, cos_ref, sin_ref, *rest):
        out_ref = rest[-1]
        piece = pieces_ref[pl.program_id(0)]

        @pl.when((piece < 2) | (piece > 3))
        def _():
            out_ref[...] = _dot(h_ref[...], w_ref[...])

        def rotated(gain):
            for pair in range(N_HEADS // 2):
                both = _dot(h_ref[...], w_ref[:, 2 * pair * HEAD_DIM:2 * (pair + 1) * HEAD_DIM])
                for hh in (2 * pair, 2 * pair + 1):
                    t = both[:, (hh % 2) * HEAD_DIM:(hh % 2 + 1) * HEAD_DIM]
                    t = t * cos_ref[...] + _rope_partner(t) * sin_ref[...]
                    out_ref[:, hh * HEAD_DIM:(hh + 1) * HEAD_DIM] = t if gain is None else t * gain

        @pl.when(piece == 2)
        def _():
            rotated(ATTN_SCALE)

        @pl.when(piece == 3)
        def _():
            rotated(None)

    row = lambda j, i, sl, pc: (i, 0)
    in_specs = [pl.BlockSpec((tm, D_MODEL), row),
                pl.BlockSpec((None, D_MODEL, D_MODEL), lambda j, i, sl, pc: (sl[j], 0, 0)),
                pl.BlockSpec((tm, HEAD_DIM), row), pl.BlockSpec((tm, HEAD_DIM), row)]
    args = [slots, pieces, h, w, cosf, sinf]
    aliases = {}
    if prior is not None:
        in_specs.append(pl.BlockSpec(memory_space=pl.ANY))
        args.append(prior)
        aliases = {6: 0}
    return pl.pallas_call(
        body, name=name,
        out_shape=jax.ShapeDtypeStruct((seq, 8 * D_MODEL), F32),
        grid_spec=pltpu.PrefetchScalarGridSpec(
            num_scalar_prefetch=2, grid=(count, seq // tm), in_specs=in_specs,
            out_specs=pl.BlockSpec((tm, D_MODEL), lambda j, i, sl, pc: (i, pc[j]))),
        input_output_aliases=aliases,
        compiler_params=_params(("arbitrary", "arbitrary"), VMEM_LIMIT),
    )(*args)


def _shift_down(v, s, head):
    rolled = pltpu.roll(v, s, 0)
    row = lax.broadcasted_iota(jnp.int32, head.shape, 0)
    first = jnp.where(row < s, pltpu.roll(head, s, 0), rolled[:SUBLANES, :])
    return jnp.concatenate([first, rolled[SUBLANES:, :]], axis=0)


def _shift_up(v, s, tail):
    rows = v.shape[0]
    rolled = pltpu.roll(v, rows - s, 0)
    row = lax.broadcasted_iota(jnp.int32, tail.shape, 0)
    last = jnp.where(row >= SUBLANES - s, pltpu.roll(tail, SUBLANES - s, 0), rolled[rows - SUBLANES:, :])
    return jnp.concatenate([rolled[:rows - SUBLANES, :], last], axis=0)


def _doubling(a, b, period, reverse):
    rows = a.shape[0]
    pos = lax.broadcasted_iota(jnp.int32, a.shape, 0) & (period - 1)
    k = 1
    while k < period:
        inside = (pos < period - k) if reverse else (pos >= k)
        shift = rows - k if reverse else k
        a_s = jnp.where(inside, pltpu.roll(a, shift, 0), 1.0)
        b_s = jnp.where(inside, pltpu.roll(b, shift, 0), 0.0)
        b = a * b_s + b
        a = a * a_s
        k *= 2
    return a, b


def _scan(a, b, boundary, reverse, a_scr, b_scr, spread):
    rows = a.shape[0]
    ntile = rows // SUBLANES
    a_scr[...], b_scr[...] = _doubling(a, b, SUBLANES, reverse)
    ends = pl.ds(0 if reverse else SUBLANES - 1, ntile, stride=SUBLANES)
    a_end, x_end = _doubling(a_scr[ends, :], b_scr[ends, :], ntile, reverse)
    x_end = x_end + a_end * boundary
    tile = lax.broadcasted_iota(jnp.int32, x_end.shape, 0)
    if reverse:
        incoming = jnp.where(tile == ntile - 1, boundary, pltpu.roll(x_end, ntile - 1, 0))
        last = x_end[0:1, :]
    else:
        incoming = jnp.where(tile == 0, boundary, pltpu.roll(x_end, 1, 0))
        last = x_end[ntile - 1:ntile, :]
    for s in range(SUBLANES):
        spread[pl.ds(s, ntile, stride=SUBLANES), :] = incoming
    return b_scr[...] + a_scr[...] * spread[...], last


def _conv_taps(xr, head):
    return [_shift_down(xr, 3, head), _shift_down(xr, 2, head), _shift_down(xr, 1, head), xr]


def _rnn_gates(xc, wa, ba, wx, bx, lam, keep):
    xcb = xc.astype(BF16)
    r = _sigmoid(_dot(xcb, wa.astype(BF16)) + ba)
    i = _sigmoid(_dot(xcb, wx.astype(BF16)) + bx)
    softplus = jnp.maximum(-lam, 0.0) + jnp.log(1.0 + jnp.exp(-jnp.abs(lam)))
    cl = -LRU_C * softplus
    log_a = cl * r
    a_raw = jnp.exp(log_a)
    mult_raw = jnp.sqrt(-_expm1_nonpos(2.0 * log_a, a_raw * a_raw))
    live = keep > 0.0
    return r, i, cl, a_raw, mult_raw, jnp.where(live, a_raw, 0.0), jnp.where(live, mult_raw, 1.0), live


def _rnn_specs(seq, rows, time_of):
    per = rows // SUBLANES
    vec = pl.BlockSpec((None, 1, 128), lambda hb, n: (hb, 0, 0))
    mat = pl.BlockSpec((None, 128, 128), lambda hb, n: (hb, 0, 0))
    return [pl.BlockSpec((rows, 128), lambda hb, n: (time_of(n), hb)),
            pl.BlockSpec((SUBLANES, 128), lambda hb, n: (jnp.maximum(time_of(n) * per - 1, 0), hb)),
            pl.BlockSpec((rows, 1), lambda hb, n: (time_of(n), 0)),
            pl.BlockSpec((None, SUBLANES, 128), lambda hb, n: (hb, 0, 0)),
            vec, mat, vec, mat, vec, vec]


def _rnn_fwd(pf, keep, conv_w8, conv_b, w_a, b_a, w_x, b_x, lam):
    seq = pf.shape[0]
    rows = RNN_ROWS

    def body(x_ref, xh_ref, keep_ref, cw_ref, cb_ref, wa_ref, ba_ref, wx_ref, bx_ref, lam_ref, hr_ref,
             carry, a_scr, b_scr, spread):
        n = pl.program_id(1)

        @pl.when(n == 0)
        def _():
            carry[...] = jnp.zeros_like(carry)

        xr = x_ref[...]
        head = jnp.where(n > 0, xh_ref[...], 0.0)
        taps = _conv_taps(xr, head)
        xc = cb_ref[...] + sum(cw_ref[k:k + 1, :] * taps[k] for k in range(4))
        _, i, _, _, _, a, mult, _ = _rnn_gates(xc, wa_ref[...], ba_ref[...], wx_ref[...], bx_ref[...],
                                               lam_ref[...], keep_ref[...])
        h, last = _scan(a, mult * i * xc, carry[0:1, :], False, a_scr, b_scr, spread)
        hr_ref[...] = h
        carry[...] = jnp.broadcast_to(last, carry.shape)

    chunk_f32 = pltpu.VMEM((rows, 128), F32)
    return pl.pallas_call(
        body, name="rnn_fwd",
        out_shape=jax.ShapeDtypeStruct((seq, D_MODEL), F32),
        grid=(RNN_BLOCKS, seq // rows),
        in_specs=_rnn_specs(seq, rows, lambda n: n),
        out_specs=pl.BlockSpec((rows, 128), lambda hb, n: (n, hb)),
        scratch_shapes=[pltpu.VMEM((SUBLANES, 128), F32), chunk_f32, chunk_f32, chunk_f32],
        compiler_params=_params(("arbitrary", "arbitrary"), VMEM_LIMIT),
    )(pf, pf, keep, conv_w8, conv_b, w_a, b_a, w_x, b_x, lam)


def _rnn_bwd(pf, hr, dhr, keep, conv_w8, conv_b, w_a, b_a, w_x, b_x, lam):
    seq = pf.shape[0]
    rows = RNN_ROWS
    nchunk = seq // rows
    per = rows // SUBLANES
    time_of = lambda n: nchunk - 1 - n

    def body(x_ref, xh_ref, keep_ref, cw_ref, cb_ref, wa_ref, ba_ref, wx_ref, bx_ref, lam_ref,
             hr_ref, hrh_ref, dhr_ref,
             dx_ref, gcw_ref, gcb_ref, gwa_ref, gba_ref, gwx_ref, gbx_ref, glam_ref,
             g_carry, dxc_tail, a_scr, b_scr, spread):
        n = pl.program_id(1)
        first_in_time = n == nchunk - 1

        @pl.when(n == 0)
        def _():
            g_carry[...] = jnp.zeros_like(g_carry)
            dxc_tail[...] = jnp.zeros_like(dxc_tail)
            for ref in (gcw_ref, gcb_ref, gwa_ref, gba_ref, gwx_ref, gbx_ref, glam_ref):
                ref[...] = jnp.zeros_like(ref)

        xr = x_ref[...]
        head = jnp.where(first_in_time, 0.0, xh_ref[...])
        taps = _conv_taps(xr, head)
        cw = cw_ref[...]
        xc = cb_ref[...] + sum(cw[k:k + 1, :] * taps[k] for k in range(4))
        wa, wx, lam = wa_ref[...], wx_ref[...], lam_ref[...]
        r, i, cl, a_raw, mult_raw, a, mult, live = _rnn_gates(xc, wa, ba_ref[...], wx, bx_ref[...], lam,
                                                               keep_ref[...])
        h_prev = _shift_down(hr_ref[...], 1, jnp.where(first_in_time, 0.0, hrh_ref[...]))

        row = lax.broadcasted_iota(jnp.int32, xr.shape, 0)
        last = row == rows - 1
        a_next = jnp.where(last, 0.0, pltpu.roll(a, rows - 1, 0))
        g, g_first = _scan(a_next, dhr_ref[...] + jnp.where(last, g_carry[0:1, :], 0.0),
                           jnp.zeros((1, 128), F32), True, a_scr, b_scr, spread)
        g_carry[...] = jnp.broadcast_to(a[0:1, :] * g_first, g_carry.shape)

        da = g * h_prev
        dmult = g * i * xc
        di = g * mult * xc
        dxc = g * mult * i
        dlog_a = jnp.where(live, da * a_raw - dmult * a_raw * a_raw / mult_raw, 0.0)
        dpa = (dlog_a * cl) * r * (1.0 - r)
        dpx = di * i * (1.0 - i)
        glam_ref[...] += jnp.sum(dlog_a * r, axis=0, keepdims=True) * (LRU_C * _sigmoid(-lam))
        xcb, dpab, dpxb = xc.astype(BF16), dpa.astype(BF16), dpx.astype(BF16)
        gwa_ref[...] += _dot_tn(xcb, dpab)
        gwx_ref[...] += _dot_tn(xcb, dpxb)
        gba_ref[...] += jnp.sum(dpa, axis=0, keepdims=True)
        gbx_ref[...] += jnp.sum(dpx, axis=0, keepdims=True)
        dxc = dxc + _dot_nt(dpab, wa.astype(BF16)) + _dot_nt(dpxb, wx.astype(BF16))

        gcb_ref[...] += jnp.sum(dxc, axis=0, keepdims=True)
        for k in range(4):
            gcw_ref[k:k + 1, :] += jnp.sum(dxc * taps[k], axis=0, keepdims=True)
        tail = dxc_tail[...]
        dx = cw[3:4, :] * dxc
        for k in range(3):
            dx = dx + cw[k:k + 1, :] * _shift_up(dxc, 3 - k, tail)
        dx_ref[...] = dx.astype(BF16)
        dxc_tail[...] = dxc[0:SUBLANES, :]

    blk = lambda hb, n: (hb, 0, 0)
    chunk = pl.BlockSpec((rows, 128), lambda hb, n: (time_of(n), hb))
    vec_out = pl.BlockSpec((None, 1, 128), blk)
    mat_out = pl.BlockSpec((None, 128, 128), blk)
    vec_shape = jax.ShapeDtypeStruct((RNN_BLOCKS, 1, 128), F32)
    mat_shape = jax.ShapeDtypeStruct((RNN_BLOCKS, 128, 128), F32)
    return pl.pallas_call(
        body, name="rnn_bwd",
        out_shape=[jax.ShapeDtypeStruct((seq, D_MODEL), BF16),
                   jax.ShapeDtypeStruct((RNN_BLOCKS, SUBLANES, 128), F32), vec_shape,
                   mat_shape, vec_shape, mat_shape, vec_shape, vec_shape],
        grid=(RNN_BLOCKS, nchunk),
        in_specs=_rnn_specs(seq, rows, time_of) + [
            chunk, pl.BlockSpec((SUBLANES, 128), lambda hb, n: (jnp.maximum(time_of(n) * per - 1, 0), hb)), chunk],
        out_specs=[chunk, pl.BlockSpec((None, SUBLANES, 128), blk), vec_out,
                   mat_out, vec_out, mat_out, vec_out, vec_out],
        scratch_shapes=[pltpu.VMEM((SUBLANES, 128), F32), pltpu.VMEM((SUBLANES, 128), F32)]
                       + [pltpu.VMEM((rows, 128), F32)] * 3,
        compiler_params=_params(("arbitrary", "arbitrary"), VMEM_LIMIT),
    )(pf, pf, keep, conv_w8, conv_b, w_a, b_a, w_x, b_x, lam, hr, hr, dhr)


def _unit_rows(dil, r, j):
    start = j * KEY_BLOCK * dil + r
    return pl.ds(start, KEY_BLOCK) if dil == 1 else pl.ds(start, KEY_BLOCK, stride=dil)


def _attn_fwd(proj):
    nh, seq = N_HEADS, proj.shape[0]
    nchunk = seq // SPAN
    nblk = SPAN // KEY_BLOCK
    wide = DILATIONS[-1]

    def body(q_ref, k_ref, v_ref, kp_ref, vp_ref, o_ref, l1_ref, l4_ref, l16_ref,
             acc, m_s, l_s, q16, k16, v16, k16p, v16p, acc16, m16, l16, tmp):
        n = pl.program_id(1)
        qi = lax.broadcasted_iota(jnp.int32, (KEY_BLOCK, KEY_BLOCK), 0)
        ki = lax.broadcasted_iota(jnp.int32, (KEY_BLOCK, KEY_BLOCK), 1)
        bias_own = jnp.where(ki <= qi, 0.0, NEG_INF)
        bias_before = jnp.where(ki >= qi, 0.0, NEG_INF)
        bias_mid = jnp.concatenate([bias_before, bias_own], axis=1)
        bias_first = jnp.concatenate([jnp.where(n > 0, bias_before, NEG_INF), bias_own], axis=1)
        ones = jnp.ones((2 * KEY_BLOCK, HEAD_DIM), BF16)
        diag = qi == ki

        @pl.when(n == 0)
        def _():
            k16p[...] = jnp.zeros_like(k16p)
            v16p[...] = jnp.zeros_like(v16p)

        def unit(qf, kpb, kb, vpb, vb, bias, state, rows, first):
            acc_r, m_r, l_r = state
            kcat = jnp.concatenate([kpb, kb], axis=0)
            vaug = jnp.concatenate([jnp.concatenate([vpb, vb], axis=0), ones], axis=1)
            s = _dot_nt(qf.astype(BF16), kcat) + bias
            mx = jnp.max(s, axis=-1, keepdims=True)
            if first:
                m_new = jnp.broadcast_to(mx, (KEY_BLOCK, HEAD_DIM))
            else:
                m_old = m_r[rows, :]
                m_new = jnp.maximum(m_old, mx)
            pv = _dot(jnp.exp(s - jnp.concatenate([m_new, m_new], axis=1)).astype(BF16), vaug)
            if first:
                acc_r[rows, :] = pv[:, :HEAD_DIM]
                l_r[rows, :] = pv[:, HEAD_DIM:]
            else:
                alpha = jnp.exp(m_old - m_new)
                acc_r[rows, :] = alpha * acc_r[rows, :] + pv[:, :HEAD_DIM]
                l_r[rows, :] = alpha * l_r[rows, :] + pv[:, HEAD_DIM:]
            m_r[rows, :] = m_new

        for gi, dil in enumerate(DILATIONS[:-1]):
            nb = nblk // dil
            for r in range(dil):
                prow = _unit_rows(dil, r, nb - 1)
                kpb, vpb = kp_ref[prow, :].astype(BF16), vp_ref[prow, :].astype(BF16)
                for j in range(nb):
                    rows = _unit_rows(dil, r, j)
                    kb, vb = k_ref[rows, :].astype(BF16), v_ref[rows, :].astype(BF16)
                    unit(q_ref[rows, :], kpb, kb, vpb, vb, bias_first if j == 0 else bias_mid,
                         (acc, m_s, l_s), rows, gi == 0)
                    kpb, vpb = kb, vb

        for src, dst in ((q_ref, q16), (k_ref, k16), (v_ref, v16), (acc, acc16), (m_s, m16), (l_s, l16)):
            _to_residue_major(src, tmp, dst)
        for r in range(wide):
            rows = pl.ds(r * KEY_BLOCK, KEY_BLOCK)
            unit(q16[rows, :], k16p[rows, :].astype(BF16), k16[rows, :].astype(BF16), v16p[rows, :].astype(BF16),
                 v16[rows, :].astype(BF16), bias_first, (acc16, m16, l16), rows, False)
        k16p[...] = k16[...]
        v16p[...] = v16[...]

        den = l16[...]
        acc16[...] = acc16[...] * (1.0 / den)
        m16[...] = m16[...] + jnp.log(den)
        _from_residue_major(acc16, tmp, o_ref, False)
        _from_residue_major(m16, tmp, m_s, False)

        def lse_row(ref, rows):
            return jnp.sum(jnp.where(diag, ref[rows, :], 0.0), axis=0, keepdims=True)

        for dil, out in zip(DILATIONS[:-1], (l1_ref, l4_ref)):
            nb = nblk // dil
            for r in range(dil):
                for j in range(nb):
                    out[r * nb + j:r * nb + j + 1, :] = lse_row(m_s, _unit_rows(dil, r, j))
        for r in range(wide):
            l16_ref[r:r + 1, :] = lse_row(m16, pl.ds(r * KEY_BLOCK, KEY_BLOCK))

    cur = lambda piece: pl.BlockSpec((SPAN, HEAD_DIM), lambda h, n: (n, piece * nh + h))
    before = lambda piece: pl.BlockSpec((SPAN, HEAD_DIM), lambda h, n: (jnp.maximum(n - 1, 0), piece * nh + h))
    blk = pl.BlockSpec((None, SPAN, HEAD_DIM), lambda h, n: (h, n, 0))
    lblk = pl.BlockSpec((None, nblk, KEY_BLOCK), lambda h, n: (h, n, 0))
    lshape = jax.ShapeDtypeStruct((nh, seq // KEY_BLOCK, KEY_BLOCK), F32)
    o, l1, l4, l16 = pl.pallas_call(
        body, name="attn_fwd",
        out_shape=[jax.ShapeDtypeStruct((nh, seq, HEAD_DIM), F32), lshape, lshape, lshape],
        grid=(nh, nchunk), in_specs=[cur(2), cur(3), cur(4), before(3), before(4)],
        out_specs=[blk, lblk, lblk, lblk],
        scratch_shapes=[pltpu.VMEM((SPAN, HEAD_DIM), F32)] * 12,
        compiler_params=_params(("arbitrary", "arbitrary"), VMEM_LIMIT),
    )(proj, proj, proj, proj, proj)
    return o, (l1, l4, l16)


def _to_residue_major(src, tmp, dst):
    quarter = SPAN // 4
    for r4 in range(4):
        tmp[r4 * quarter:(r4 + 1) * quarter, :] = src[pl.ds(r4, quarter, stride=4), :]
    for r4 in range(4):
        for rp in range(4):
            r = r4 + 4 * rp
            dst[r * KEY_BLOCK:(r + 1) * KEY_BLOCK, :] = tmp[pl.ds(r4 * quarter + rp, KEY_BLOCK, stride=4), :]


def _from_residue_major(src, tmp, dst, add):
    quarter = SPAN // 4
    for r4 in range(4):
        for rp in range(4):
            r = r4 + 4 * rp
            tmp[pl.ds(r4 * quarter + rp, KEY_BLOCK, stride=4), :] = src[r * KEY_BLOCK:(r + 1) * KEY_BLOCK, :]
    for r4 in range(4):
        rows = pl.ds(r4, quarter, stride=4)
        part = tmp[r4 * quarter:(r4 + 1) * quarter, :]
        dst[rows, :] = dst[rows, :] + part if add else part


def _attn_bwd(proj, do, o, lses, cosf, sinf):
    nh, seq = N_HEADS, proj.shape[0]
    nchunk = seq // SPAN
    nblk = SPAN // KEY_BLOCK
    wide = DILATIONS[-1]
    assert SPAN == wide * KEY_BLOCK

    def body(q_ref, k_ref, v_ref, do_ref, o_ref, kp_ref, vp_ref, l1_ref, l4_ref, l16_ref,
             cos_ref, sin_ref, cosp_ref, sinp_ref, dq_ref, dk_ref, dv_ref,
             dq_acc, dkc_acc, dvc_acc, dkp_acc, dvp_acc, q16, k16, v16, do16, o16, k16p, v16p,
             dq16, dkc16, dvc16, dkp16, dvp16, tmp, pt_s, ds_s, kcat_s, qb_s, dob_s):
        n = pl.program_id(1)
        ki = lax.broadcasted_iota(jnp.int32, (KEY_BLOCK, KEY_BLOCK), 0)
        qi = lax.broadcasted_iota(jnp.int32, (KEY_BLOCK, KEY_BLOCK), 1)
        bias_own = jnp.where(ki <= qi, 0.0, NEG_INF)
        bias_before = jnp.where(ki >= qi, 0.0, NEG_INF)
        bias_mid = jnp.concatenate([bias_before, bias_own], axis=0)
        bias_first = jnp.concatenate([jnp.where(n > 0, bias_before, NEG_INF), bias_own], axis=0)
        ones8 = jnp.ones((SUBLANES, HEAD_DIM), BF16)

        def row_dot(a, b):
            prod = a * b
            hi = prod.astype(BF16)
            lo = (prod - hi.astype(F32)).astype(BF16)
            return (_dot_nt(ones8, hi) + _dot_nt(ones8, lo))[0:1, :]

        def group(units, srcs, before, l_ref, accs):
            src_q, src_do, src_o, src_k, src_v = srcs
            before_k, before_v = before
            acc_q, acc_kc, acc_vc, acc_kp, acc_vp = accs
            kb = vb = None
            for u, (rows, prow, outside, lrow, _) in enumerate(units):
                dof = src_do[rows, :]
                qb, dob = src_q[rows, :].astype(BF16), dof.astype(BF16)
                kpb, vpb = (before_k[prow, :].astype(BF16), before_v[prow, :].astype(BF16)) if outside else (kb, vb)
                kb, vb = src_k[rows, :].astype(BF16), src_v[rows, :].astype(BF16)
                kcat = jnp.concatenate([kpb, kb], axis=0)
                vcat = jnp.concatenate([vpb, vb], axis=0)
                bias = bias_first if outside else bias_mid
                pt = jnp.exp(_dot_nt(kcat, qb) + bias - l_ref[lrow:lrow + 1, :])
                dst = pt * (_dot_nt(vcat, dob) - row_dot(dof, src_o[rows, :]))
                pt_s[u], ds_s[u], kcat_s[u], qb_s[u], dob_s[u] = pt.astype(BF16), dst.astype(BF16), kcat, qb, dob
            for u, (rows, _, _, _, _) in enumerate(units):
                acc_q[rows, :] += _dot_tn(ds_s[u], kcat_s[u])
            for u, (rows, prow, outside, _, nxt) in enumerate(units):
                dk = _dot(ds_s[u, KEY_BLOCK:, :], qb_s[u])
                dv = _dot(pt_s[u, KEY_BLOCK:, :], dob_s[u])
                if nxt is not None:
                    dk = dk + _dot(ds_s[nxt, :KEY_BLOCK, :], qb_s[nxt])
                    dv = dv + _dot(pt_s[nxt, :KEY_BLOCK, :], dob_s[nxt])
                acc_kc[rows, :] += dk
                acc_vc[rows, :] += dv
                if outside:
                    acc_kp[prow, :] += _dot(ds_s[u, :KEY_BLOCK, :], qb_s[u])
                    acc_vp[prow, :] += _dot(pt_s[u, :KEY_BLOCK, :], dob_s[u])

        @pl.when(n == 0)
        def _():
            for ref in (dkp_acc, dvp_acc, dkp16, dvp16, k16p, v16p):
                ref[...] = jnp.zeros_like(ref)

        @pl.when(n < nchunk)
        def _():
            for ref in (dq_acc, dkc_acc, dvc_acc, dq16, dkc16, dvc16):
                ref[...] = jnp.zeros_like(ref)
            for src, dst in ((q_ref, q16), (k_ref, k16), (v_ref, v16), (do_ref, do16), (o_ref, o16)):
                _to_residue_major(src, tmp, dst)
            natural = (q_ref, do_ref, o_ref, k_ref, v_ref)
            for dil, l_ref in zip(DILATIONS[:-1], (l1_ref, l4_ref)):
                nb = nblk // dil
                units = [(_unit_rows(dil, r, j), _unit_rows(dil, r, (j - 1) % nb), j == 0, r * nb + j,
                          r * nb + j + 1 if j + 1 < nb else None) for r in range(dil) for j in range(nb)]
                group(units, natural, (kp_ref, vp_ref), l_ref, (dq_acc, dkc_acc, dvc_acc, dkp_acc, dvp_acc))
            blocks = [pl.ds(r * KEY_BLOCK, KEY_BLOCK) for r in range(wide)]
            group([(rows, rows, True, r, None) for r, rows in enumerate(blocks)], (q16, do16, o16, k16, v16),
                  (k16p, v16p), l16_ref, (dq16, dkc16, dvc16, dkp16, dvp16))
            _from_residue_major(dq16, tmp, dq_acc, True)
            dq = dq_acc[...]
            dq_ref[...] = ((dq * cos_ref[...] - _rope_partner(dq) * sin_ref[...]) * ATTN_SCALE).astype(BF16)

        @pl.when(n > 0)
        def _():
            _from_residue_major(dkp16, tmp, dkp_acc, True)
            _from_residue_major(dvp16, tmp, dvp_acc, True)
            dk = dkp_acc[...]
            dk_ref[...] = (dk * cosp_ref[...] - _rope_partner(dk) * sinp_ref[...]).astype(BF16)
            dv_ref[...] = dvp_acc[...].astype(BF16)

        @pl.when(n < nchunk)
        def _():
            for src, dst in ((dkc_acc, dkp_acc), (dvc_acc, dvp_acc), (dkc16, dkp16), (dvc16, dvp16),
                             (k16, k16p), (v16, v16p)):
                dst[...] = src[...]

    last = nchunk - 1
    cur = lambda h, n: (h, jnp.minimum(n, last), 0)
    prev = lambda h, n: (h, jnp.clip(n - 1, 0, last), 0)
    blk = lambda idx: pl.BlockSpec((None, SPAN, HEAD_DIM), idx)
    lblk = pl.BlockSpec((None, nblk, KEY_BLOCK), cur)
    tab = pl.BlockSpec((SPAN, HEAD_DIM), lambda h, n: (jnp.minimum(n, last), 0))
    tabp = pl.BlockSpec((SPAN, HEAD_DIM), lambda h, n: (jnp.clip(n - 1, 0, last), 0))
    out_q = pl.BlockSpec((SPAN, HEAD_DIM), lambda h, n: (jnp.minimum(n, last), h))
    out_kv = pl.BlockSpec((SPAN, HEAD_DIM), lambda h, n: (jnp.clip(n - 1, 0, last), h))
    shape = jax.ShapeDtypeStruct((seq, nh * HEAD_DIM), BF16)
    tok = lambda piece, row: pl.BlockSpec((SPAN, HEAD_DIM), lambda h, n: (row(n), piece * nh + h))
    row_cur, row_prev = (lambda n: jnp.minimum(n, last)), (lambda n: jnp.clip(n - 1, 0, last))
    return pl.pallas_call(
        body, name="attn_bwd", out_shape=[shape, shape, shape], grid=(nh, nchunk + 1),
        in_specs=[tok(2, row_cur), tok(3, row_cur), tok(4, row_cur), blk(cur), blk(cur),
                  tok(3, row_prev), tok(4, row_prev)] + [lblk] * 3 + [tab, tab, tabp, tabp],
        out_specs=[out_q, out_kv, out_kv],
        scratch_shapes=[pltpu.VMEM((SPAN, HEAD_DIM), F32)] * 18
                       + [pltpu.VMEM((nblk, 2 * KEY_BLOCK, HEAD_DIM), BF16)] * 3
                       + [pltpu.VMEM((nblk, KEY_BLOCK, HEAD_DIM), BF16)] * 2,
        compiler_params=_params(("arbitrary", "arbitrary"), VMEM_LIMIT),
    )(proj, proj, proj, do, o, proj, proj, *lses, cosf, sinf, cosf, sinf)


def _hub(x, tgt, hr, pf, o_hm, mod, b_mod, b_gate, g_final, w_out_rnn, w_out_attn, w_o):
    seq = x.shape[0]
    tm = HUB_ROWS
    nsteps = seq // tm

    def body(x_ref, t_ref, hr_ref, zr_ref, za_ref, gr_ref, ga_ref, o_ref, mod_ref, bmod_ref, bg_ref, gf_ref,
             wr_hbm, wa_hbm, wo_hbm,
             dx2_ref, dhr_ref, dzr_ref, do_ref, dza_ref, dgr_ref, dga_ref,
             ur_ref, dyr_ref, ua_ref, dya_ref, mg_ref, dmo_ref,
             ggf_ref, gbg_ref, dgate_ref, loss_ref,
             wr, wa, wo, sem):
        step = pl.program_id(0)

        @pl.when(step == 0)
        def _():
            for src, dst in ((wr_hbm, wr), (wa_hbm, wa), (wo_hbm, wo)):
                cp = pltpu.make_async_copy(src, dst, sem)
                cp.start()
                cp.wait()
            for ref in (ggf_ref, gbg_ref, dgate_ref, loss_ref):
                ref[...] = jnp.zeros_like(ref)

        gate = mod_ref[:, 2 * D_MODEL:] + bmod_ref[:, 2 * D_MODEL:]
        gfin = gf_ref[...]
        hr_t, zr, za = hr_ref[...], zr_ref[...], za_ref[...]
        o = jnp.concatenate([o_ref[hh] for hh in range(N_HEADS)], axis=1)
        sig_zr, sig_za = _sigmoid(zr), _sigmoid(za)
        silu_zr, silu_za = zr * sig_zr, za * sig_za
        u_rnn = (hr_t * silu_zr).astype(BF16)
        u_attn = (o * silu_za).astype(BF16)
        y_rnn = _dot(u_rnn, wr[...])
        y_attn = _dot(u_attn, wa[...])
        sr = _sigmoid(gr_ref[...] + bg_ref[:, :D_MODEL])
        sa = _sigmoid(ga_ref[...] + bg_ref[:, D_MODEL:])
        merged = (sr * y_rnn + sa * y_attn).astype(BF16)
        mo = _dot(merged, wo[...])
        x2 = x_ref[...] + gate * mo
        rstd = lax.rsqrt(jnp.mean(x2 * x2, axis=-1, keepdims=True) + NORM_EPS)
        xn = x2 * rstd
        err = xn * gfin - t_ref[...]
        loss_ref[...] += 0.5 * jnp.sum(jnp.sum(err * err, axis=-1, keepdims=True) * (1.0 / D_MODEL),
                                       axis=0, keepdims=True)

        dy = err * (1.0 / D_MODEL)
        ggf_ref[...] += jnp.sum(dy * xn, axis=0, keepdims=True)
        dxn = dy * gfin
        dx2 = rstd * (dxn - xn * jnp.mean(dxn * xn, axis=-1, keepdims=True))
        dx2_ref[...] = dx2
        dgate_ref[...] += jnp.sum(dx2 * mo, axis=0, keepdims=True)
        dmo = (dx2 * gate).astype(BF16)
        dmerged = _dot_nt(dmo, wo[...])
        mg_ref[...] = merged
        dmo_ref[...] = dmo
        dy_rnn = (dmerged * sr).astype(BF16)
        dy_attn = (dmerged * sa).astype(BF16)
        dg_r = dmerged * y_rnn * sr * (1.0 - sr)
        dg_a = dmerged * y_attn * sa * (1.0 - sa)
        dgr_ref[...] = dg_r.astype(BF16)
        dga_ref[...] = dg_a.astype(BF16)
        gbg_ref[:, :D_MODEL] += jnp.sum(dg_r, axis=0, keepdims=True)
        gbg_ref[:, D_MODEL:] += jnp.sum(dg_a, axis=0, keepdims=True)
        du_rnn = _dot_nt(dy_rnn, wr[...])
        du_attn = _dot_nt(dy_attn, wa[...])
        ur_ref[...] = u_rnn
        dyr_ref[...] = dy_rnn
        ua_ref[...] = u_attn
        dya_ref[...] = dy_attn
        dhr_ref[...] = du_rnn * silu_zr
        dzr_ref[...] = (du_rnn * hr_t * (sig_zr * (1.0 + zr * (1.0 - sig_zr)))).astype(BF16)
        dza_ref[...] = (du_attn * o * (sig_za * (1.0 + za * (1.0 - sig_za)))).astype(BF16)
        d_o = du_attn * silu_za
        for hh in range(N_HEADS):
            do_ref[hh] = d_o[:, hh * HEAD_DIM:(hh + 1) * HEAD_DIM]

    row = pl.BlockSpec((tm, D_MODEL), lambda i: (i, 0))
    piece = lambda slot: pl.BlockSpec((tm, D_MODEL), lambda i: (i, slot))
    hm = pl.BlockSpec((N_HEADS, tm, HEAD_DIM), lambda i: (0, i, 0))
    const = lambda cols: pl.BlockSpec((1, cols), lambda i: (0, 0))
    any_spec = pl.BlockSpec(memory_space=pl.ANY)
    act_f32 = jax.ShapeDtypeStruct((seq, D_MODEL), F32)
    act_bf16 = jax.ShapeDtypeStruct((seq, D_MODEL), BF16)
    return pl.pallas_call(
        body, name="hub",
        out_shape=[act_f32, act_f32, act_bf16, jax.ShapeDtypeStruct((N_HEADS, seq, HEAD_DIM), F32),
                   act_bf16, act_bf16, act_bf16] + [act_bf16] * 6 + [
                   jax.ShapeDtypeStruct((1, D_MODEL), F32), jax.ShapeDtypeStruct((1, 2 * D_MODEL), F32),
                   jax.ShapeDtypeStruct((1, D_MODEL), F32), jax.ShapeDtypeStruct((1, 1), F32)],
        grid=(nsteps,),
        in_specs=[row, row, row, piece(1), piece(5), piece(6), piece(7), hm,
                  const(3 * D_MODEL), const(3 * D_MODEL), const(2 * D_MODEL), const(D_MODEL),
                  any_spec, any_spec, any_spec],
        out_specs=[row, row, row, hm, row, row, row] + [row] * 6 + [
                   const(D_MODEL), const(2 * D_MODEL), const(D_MODEL), const(1)],
        scratch_shapes=[pltpu.VMEM((D_MODEL, D_MODEL), BF16)] * 3 + [pltpu.SemaphoreType.DMA],
        compiler_params=_params(("arbitrary",), VMEM_LIMIT),
    )(x, tgt, hr, pf, pf, pf, pf, o_hm, mod, b_mod, b_gate, g_final, w_out_rnn, w_out_attn, w_o)


def _pair_grads(name, lefts, rights):
    n = len(rights)
    shared = len(lefts) == 1
    seq = rights[0].shape[0]
    tk = WGRAD_ROWS
    nk = seq // tk

    def body(*refs):
        l_refs, r_refs = refs[:len(lefts)], refs[len(lefts):len(lefts) + n]
        out_ref, low_ref = refs[len(lefts) + n:]
        j, kk = pl.program_id(0), pl.program_id(1)

        @pl.when(kk == 0)
        def _():
            out_ref[...] = jnp.zeros_like(out_ref)

        for m in range(n):
            @pl.when(j == m)
            def _(m=m):
                out_ref[...] += _dot_tn(l_refs[0 if shared else m][...], r_refs[m][...])

        @pl.when(kk == nk - 1)
        def _():
            low_ref[...] = out_ref[...].astype(BF16)

    def spec(m):
        return pl.BlockSpec((tk, D_MODEL), lambda j, kk: (jnp.where(j == m, kk, jnp.where(j < m, 0, nk - 1)), 0))

    left_specs = [pl.BlockSpec((tk, D_MODEL), lambda j, kk: (kk, 0))] if shared else [spec(m) for m in range(n)]
    out_spec = pl.BlockSpec((None, D_MODEL, D_MODEL), lambda j, kk: (j, 0, 0))
    return pl.pallas_call(
        body, name=name,
        out_shape=[jax.ShapeDtypeStruct((n, D_MODEL, D_MODEL), F32), jax.ShapeDtypeStruct((n, D_MODEL, D_MODEL), BF16)],
        grid=(n, nk),
        in_specs=left_specs + [spec(m) for m in range(n)],
        out_specs=[out_spec, out_spec],
        compiler_params=_params(("arbitrary", "arbitrary"), VMEM_LIMIT),
    )(*lefts, *rights)


def _dh_dx(pieces, w_near, w_far, x, dx2, mod, b_mod, g_norm):
    seq = x.shape[0]
    tm = DX_ROWS

    def body(*refs):
        p_refs = refs[:8]
        near_hbm, far_hbm, x_ref, dx2_ref, mod_ref, bmod_ref, g_ref = refs[8:15]
        gx_ref, dshift_ref, dscale_ref, ggn_ref, w_scr, sem = refs[15:]
        step = pl.program_id(0)

        @pl.when(step == 0)
        def _():
            me = _my_pos()
            sib = _flip(me, 1)
            moves = [(near_hbm, _index(_flip(me, 2 * m))) for m in range(4)] + [(near_hbm, _index(sib))]
            moves += [(far_hbm, _index(_flip(sib, 2 * m))) for m in range(1, 4)]
            for src, t in moves:
                cp = pltpu.make_async_copy(src.at[t], w_scr.at[t], sem)
                cp.start()
                cp.wait()
            for ref in (dshift_ref, dscale_ref, ggn_ref):
                ref[...] = jnp.zeros_like(ref)

        dh = _dot_nt(p_refs[0][...], w_scr[0])
        for j in range(1, 8):
            dh = dh + _dot_nt(p_refs[j][...], w_scr[j])
        scale1 = 1.0 + mod_ref[:, D_MODEL:2 * D_MODEL] + bmod_ref[:, D_MODEL:2 * D_MODEL]
        g = g_ref[...]
        xf = x_ref[...]
        rstd_t = lax.rsqrt(jnp.mean(xf * xf, axis=-1, keepdims=True) + NORM_EPS)
        xn = xf * rstd_t
        dshift_ref[...] += jnp.sum(dh, axis=0, keepdims=True)
        dscale_ref[...] += jnp.sum(dh * (xn * g), axis=0, keepdims=True)
        ggn_ref[...] += jnp.sum(dh * scale1 * xn, axis=0, keepdims=True)
        dxn = dh * (g * scale1)
        gx_ref[...] = rstd_t * (dxn - xn * jnp.mean(dxn * xn, axis=-1, keepdims=True)) + dx2_ref[...]

    row = pl.BlockSpec((tm, D_MODEL), lambda i: (i, 0))
    const = lambda cols: pl.BlockSpec((1, cols), lambda i: (0, 0))
    vec = jax.ShapeDtypeStruct((1, D_MODEL), F32)
    return pl.pallas_call(
        body, name="dh_dx",
        out_shape=[jax.ShapeDtypeStruct((seq, D_MODEL), F32), vec, vec, vec],
        grid=(seq // tm,),
        in_specs=[row] * 8 + [pl.BlockSpec(memory_space=pl.ANY), pl.BlockSpec(memory_space=pl.ANY), row, row,
                              const(3 * D_MODEL), const(3 * D_MODEL), const(D_MODEL)],
        out_specs=[row, const(D_MODEL), const(D_MODEL), const(D_MODEL)],
        scratch_shapes=[pltpu.VMEM((8, D_MODEL, D_MODEL), BF16), pltpu.SemaphoreType.DMA],
        compiler_params=_params(("arbitrary",), VMEM_LIMIT),
    )(*pieces, w_near, w_far, x, dx2, mod, b_mod, g_norm)


def _adamw(name, w, g, m, v, recv=None):
    rows, cols = w.shape
    tr = rows if rows <= 256 else 256

    def body(*refs):
        w_ref, g_ref, m_ref, v_ref = refs[:4]
        d_ref, nm_ref, nv_ref = refs[-3:] if recv is None else refs[5:8]
        gv = g_ref[...]
        if recv is not None:
            r_ref, g_out = refs[4], refs[8]
            gv = ((gv + r_ref[0].astype(F32)) + r_ref[1].astype(F32)) + r_ref[2].astype(F32)
            g_out[...] = gv
        nm = ADAM_B1 * m_ref[...] + (1.0 - ADAM_B1) * gv
        nv = ADAM_B2 * v_ref[...] + (1.0 - ADAM_B2) * (gv * gv)
        m_hat = nm / (1.0 - ADAM_B1 ** ADAM_STEP)
        v_hat = nv / (1.0 - ADAM_B2 ** ADAM_STEP)
        d_ref[...] = -ADAM_LR * (m_hat / (jnp.sqrt(v_hat) + ADAM_EPS) + ADAM_WD * w_ref[...])
        nm_ref[...] = nm
        nv_ref[...] = nv

    spec = pl.BlockSpec((tr, cols), lambda i: (i, 0))
    shape = jax.ShapeDtypeStruct((rows, cols), F32)
    if recv is None:
        return pl.pallas_call(
            body, name=name, out_shape=[shape, shape, shape], grid=(rows // tr,),
            in_specs=[spec] * 4, out_specs=[spec] * 3,
            compiler_params=_params(("arbitrary",)),
        )(w, g, m, v)
    return pl.pallas_call(
        body, name=name, out_shape=[shape] * 4, grid=(rows // tr,),
        in_specs=[spec] * 4 + [pl.BlockSpec((3, tr, cols), lambda i: (0, i, 0))], out_specs=[spec] * 4,
        compiler_params=_params(("arbitrary",)),
    )(w, g, m, v, recv)


def kernel(x, c, positions, g_norm, w_mod, b_mod, w_in, b_gate, conv_w, conv_b, w_a, b_a, w_x, b_x, lam, w_out_rnn, w_out_attn, w_o, g_final, loss_target, m_g_norm, m_w_mod, m_b_mod, m_w_in, m_b_gate, m_conv_w, m_conv_b, m_w_a, m_b_a, m_w_x, m_b_x, m_lam, m_w_out_rnn, m_w_out_attn, m_w_o, m_g_final, v_g_norm, v_w_mod, v_b_mod, v_w_in, v_b_gate, v_conv_w, v_conv_b, v_w_a, v_b_a, v_w_x, v_b_x, v_lam, v_w_out_rnn, v_w_out_attn, v_w_o, v_g_final):
    seq = x.shape[1]
    me = _index(_my_pos())
    xs, tgt = x[0], loss_target[0]

    inv_freq = ROPE_THETA ** (-jnp.arange(0, 2 * ROT_HALF, 2, dtype=F32) / (2 * ROT_HALF))
    ang = (positions[0].astype(F32).reshape(seq // SUBLANES, SUBLANES, 1) * inv_freq).reshape(seq // SUBLANES, 128)
    cos, sin = jnp.cos(ang).reshape(seq, ROT_HALF), jnp.sin(ang).reshape(seq, ROT_HALF)
    rest = HEAD_DIM - 2 * ROT_HALF
    cosf = jnp.concatenate([cos, cos, jnp.ones((seq, rest), F32)], axis=1)
    sinf = jnp.concatenate([-sin, sin, jnp.zeros((seq, rest), F32)], axis=1)
    keep = (positions[0] != 0).astype(F32)[:, None]

    both = _ag_small("gather_c_conv_w", jnp.concatenate(
        [jnp.broadcast_to(c, (SUBLANES, D_MODEL)), jnp.pad(conv_w[0], ((0, SUBLANES - 4), (0, 0)))], axis=1))
    c_all, conv_w8 = both[:, 0, :D_MODEL], both[:, :, D_MODEL:]
    mod_cols = w_mod.shape[2]
    mod_part = _ag_small("gather_mod", _mod_fwd(c_all, w_mod[0]))
    mod = lax.dynamic_index_in_dim(mod_part, me, axis=1, keepdims=False).reshape(1, N_DEV * mod_cols)

    slot = lambda t: lax.dynamic_update_slice(lax.empty((N_DEV,) + t.shape, t.dtype), t[None], (me, 0, 0))
    w_in_own = w_in[0].astype(BF16)
    mod, w_in_own = lax.optimization_barrier((mod, w_in_own))
    first = _split_start("gather_w_in_start", _own_block_copies, 4, [w_in_own], [slot(w_in_own)])
    mod = mod + first[4][0:1, 0:1]

    blocks = lambda t: t.reshape(RNN_BLOCKS, 1, 128)
    rnn_params = (conv_w8, blocks(conv_b), w_a[0], blocks(b_a), w_x[0], blocks(b_x), blocks(lam))

    h = _norm(xs, mod, b_mod, g_norm)
    ids = lambda ks: jnp.bitwise_xor(me, jnp.array(ks, jnp.int32)).astype(jnp.int32)
    pf = _proj("proj_own", h, first[2][0][None], jnp.zeros((1,), jnp.int32), ids([0]), cosf, sinf, None)
    _, (w_in_near,) = _split_wait("gather_w_in_wait", _own_block_copies, first, pf)
    second = _split_start("forward_w_in_start", _forward_copies, 3, [w_in_near],
                          [lax.empty(w_in_near.shape, w_in_near.dtype)])
    near = ids([1, 2, 4, 6])
    pf = _proj("proj_near", h, second[2][0], near, near, cosf, sinf, pf)
    (w_in_near,), (w_in_far,) = _split_wait("forward_w_in_wait", _forward_copies, second, pf)
    far = ids([3, 5, 7])
    pf = _proj("proj_far", h, w_in_far, far, far, cosf, sinf, pf)
    late = [w_out_rnn[0].astype(BF16), w_out_attn[0].astype(BF16), w_o[0].astype(BF16)]
    pf, late = lax.optimization_barrier((pf, late))
    flight = _split_start("gather_out_weights_start", _peer_copies, 7 * len(late), late, [slot(t) for t in late])
    rnn_params = (rnn_params[0], rnn_params[1] + flight[4][0:1, 0:1]) + rnn_params[2:]
    hr = _rnn_fwd(pf, keep, *rnn_params)
    o, lses = _attn_fwd(pf)

    w_or_all, w_oa_all, w_o_all = (t.reshape(D_MODEL, D_MODEL) for t in _split_wait(
        "gather_out_weights_wait", _peer_copies, flight, o)[1])
    (dx2, dhr, dz_rnn, d_o, dz_attn, dg_r, dg_a, u_rnn, dy_rnn, u_attn, dy_attn, merged, dmo,
     gp_g_final, gp_b_gate, dgate, loss_part) = _hub(
        xs, tgt, hr, pf, o, mod, b_mod, b_gate, g_final.reshape(1, D_MODEL), w_or_all, w_oa_all, w_o_all)
    gp_out, gp_out_low = _pair_grads("out_grads", [u_rnn, u_attn, merged], [dy_rnn, dy_attn, dmo])
    dq, dk, dv = _attn_bwd(pf, d_o, o, lses, cosf, sinf)
    dx_rnn, gp_conv_w, gp_conv_b, gp_w_a, gp_b_a, gp_w_x, gp_b_x, gp_lam = _rnn_bwd(pf, hr, dhr, keep, *rnn_params)
    pieces = [dx_rnn, dz_rnn, dq, dk, dv, dz_attn, dg_r, dg_a]
    gp_w_in, gp_w_in_low = _pair_grads("w_in_grad", [h], pieces)

    by_target = lambda t: [(t.reshape(3, N_DEV, 128, D_MODEL), i) for i in range(3)]
    stacks = [(gp_w_in, None)] + by_target(gp_out)
    from_sib = _rs_to_sibling("rs_sibling", [(gp_w_in_low, None)] + by_target(gp_out_low))
    targets = jnp.bitwise_xor(me, 2 * jnp.arange(4, dtype=jnp.int32)).astype(jnp.int32)
    sums = [_add_sibling("rs_add_sibling_%d" % a, s_, r_, targets) for a, (s_, r_) in enumerate(zip(stacks, from_sib))]
    sends = [send for _, send in sums]
    reduce_flight = _split_start("rs_chips_start", _chip_copies, 3 * len(sends), sends,
                                 [lax.empty(t.shape, t.dtype) for t in sends])

    mod_after = mod + reduce_flight[4][0:1, 0:1]
    grad_x, dshift, dscale, gp_g_norm = _dh_dx(pieces, w_in_near, w_in_far, xs, dx2, mod_after, b_mod, g_norm)

    dmod = jnp.concatenate([dshift, dscale, dgate], axis=1)
    dmod_all = _ag_small("gather_dmod", jnp.broadcast_to(dmod, (SUBLANES, 3 * D_MODEL)))[:, 0, :]
    dmod_cols = lax.dynamic_slice_in_dim(dmod_all, me * mod_cols, mod_cols, axis=1)
    g_b_mod, g_w_mod = _mod_bwd(c_all, dmod_all, dmod_cols)

    flat = lambda t: t.reshape(-1, 128)
    small = [flat(gp_g_norm), flat(gp_b_gate), flat(gp_conv_b), flat(gp_b_a), flat(gp_b_x), flat(gp_lam),
             flat(gp_g_final), flat(gp_conv_w), jnp.broadcast_to(loss_part, (SUBLANES, 128)),
             flat(gp_w_a), flat(gp_w_x)]
    sizes = [t.shape[0] for t in small]
    small.append(jnp.zeros((-sum(sizes) % (2 * SUBLANES), 128), F32))
    total = _allreduce_small("allreduce_small_grads", jnp.concatenate(small, axis=0))
    offs = [sum(sizes[:i]) for i in range(len(sizes))]
    (g_g_norm, g_b_gate, g_conv_b, g_b_a, g_b_x, g_lam, g_g_final, g_conv_w_all, loss_rows, g_w_a, g_w_x) = (
        total[o_:o_ + s_] for o_, s_ in zip(offs, sizes))
    loss = loss_rows[0, 0]
    g_conv_w = lax.dynamic_index_in_dim(g_conv_w_all.reshape(RNN_BLOCKS, SUBLANES, 128), me, axis=0,
                                        keepdims=False)[:4]

    _, from_chips = _split_wait("rs_chips_wait", _chip_copies, reduce_flight, total)

    results = {}
    sharded = (("w_in", w_in, m_w_in, v_w_in, (D_MODEL, D_MODEL)),
               ("w_out_rnn", w_out_rnn, m_w_out_rnn, v_w_out_rnn, (128, D_MODEL)),
               ("w_out_attn", w_out_attn, m_w_out_attn, v_w_out_attn, (128, D_MODEL)),
               ("w_o", w_o, m_w_o, v_w_o, (128, D_MODEL)))
    for (name, w_, m_, v_, shape2), (own, _), arrived in zip(sharded, sums, from_chips):
        d_, nm_, nv_, g_ = _adamw("adamw_" + name, w_.reshape(shape2), own, m_.reshape(shape2), v_.reshape(shape2),
                                  arrived)
        results[name] = (g_, d_, nm_, nv_)
    shape2 = (D_MODEL, mod_cols)
    results["w_mod"] = (g_w_mod,) + tuple(_adamw("adamw_w_mod", w_mod.reshape(shape2), g_w_mod,
                                                 m_w_mod.reshape(shape2), v_w_mod.reshape(shape2)))
    lanes = (("g_norm", g_norm, g_g_norm, m_g_norm, v_g_norm), ("b_mod", b_mod, g_b_mod, m_b_mod, v_b_mod),
             ("b_gate", b_gate, g_b_gate, m_b_gate, v_b_gate), ("conv_w", conv_w, g_conv_w, m_conv_w, v_conv_w),
             ("conv_b", conv_b, g_conv_b, m_conv_b, v_conv_b), ("w_a", w_a, g_w_a, m_w_a, v_w_a),
             ("b_a", b_a, g_b_a, m_b_a, v_b_a), ("w_x", w_x, g_w_x, m_w_x, v_w_x), ("b_x", b_x, g_b_x, m_b_x, v_b_x),
             ("lam", lam, g_lam, m_lam, v_lam), ("g_final", g_final, g_g_final, m_g_final, v_g_final))
    for name, w_, g_, m_, v_ in lanes:
        rows128 = lambda t: t.reshape(-1, 128)
        results[name] = (g_,) + tuple(_adamw("adamw_" + name, rows128(w_), rows128(g_), rows128(m_), rows128(v_)))
    order = ("g_norm", "w_mod", "b_mod", "w_in", "b_gate", "conv_w", "conv_b", "w_a", "b_a", "w_x", "b_x", "lam",
             "w_out_rnn", "w_out_attn", "w_o", "g_final")
    given = dict(g_norm=g_norm, w_mod=w_mod, b_mod=b_mod, w_in=w_in, b_gate=b_gate, conv_w=conv_w, conv_b=conv_b,
                 w_a=w_a, b_a=b_a, w_x=w_x, b_x=b_x, lam=lam, w_out_rnn=w_out_rnn, w_out_attn=w_out_attn, w_o=w_o,
                 g_final=g_final)
    outs = [[results[name][k].reshape(given[name].shape) for name in order] for k in range(4)]
    return (loss, grad_x[None], *outs[0], *outs[1], *outs[2], *outs[3])
```

```python
import jax
import jax.numpy as jnp
from jax import lax
from jax.experimental import pallas as pl
from jax.experimental.pallas import tpu as pltpu

F32 = jnp.float32
BF16 = jnp.bfloat16
MESH = pl.DeviceIdType.MESH

D_MODEL = 1024
N_HEADS = 8
HEAD_DIM = 128
RNN_BLOCKS = 8
N_DEV = 8
ROT_HALF = 16
ROPE_THETA = 500000.0
DILATIONS = (1, 4, 16)
KEY_BLOCK = 128
SPAN = KEY_BLOCK * DILATIONS[-1]
ATTN_SCALE = HEAD_DIM ** -0.5
NORM_EPS = 1e-6
LRU_C = 8.0
NEG_INF = -1e30
ADAM_LR, ADAM_B1, ADAM_B2, ADAM_EPS, ADAM_WD, ADAM_STEP = 0.001, 0.9, 0.999, 1e-08, 0.01, 10

SUBLANES = 8
VMEM_LIMIT = 56 * 1024 * 1024
PROJ_ROWS = 1024
RNN_ROWS = 2048
HUB_ROWS = 256
DX_ROWS = 512
WGRAD_ROWS = 1024
ADD_ROWS = 256


def _params(sem=None, vmem=None):
    return pltpu.CompilerParams(dimension_semantics=sem, vmem_limit_bytes=vmem)


def _dot(a, b):
    return jnp.dot(a, b, preferred_element_type=F32)


def _dot_nt(a, b):
    return lax.dot_general(a, b, (((1,), (1,)), ((), ())), preferred_element_type=F32)


def _dot_tn(a, b):
    return lax.dot_general(a, b, (((0,), (0,)), ((), ())), preferred_element_type=F32)


def _sigmoid(z):
    return 1.0 / (1.0 + jnp.exp(-z))


def _expm1_nonpos(z, exp_z):
    return jnp.where(z > -0.01, z * (1.0 + 0.5 * z), exp_z - 1.0)


def _my_pos():
    return lax.axis_index("x"), lax.axis_index("y"), lax.axis_index("c")


def _flip(pos, k):
    x, y, c = pos
    return ((1 - x) if k & 4 else x, (1 - y) if k & 2 else y, (1 - c) if k & 1 else c)


def _index(pos):
    return 4 * pos[0] + 2 * pos[1] + pos[2]


def _ag_small(name, v):
    rows, cols = v.shape

    def body(v_ref, out_ref, send_sems, recv_sems):
        me = _my_pos()
        out_ref[_index(me)] = v_ref[...]
        sends = []
        for k in range(1, N_DEV):
            cp = pltpu.make_async_remote_copy(
                src_ref=v_ref, dst_ref=out_ref.at[_index(me)], send_sem=send_sems.at[k - 1],
                recv_sem=recv_sems.at[k - 1], device_id=_flip(me, k), device_id_type=MESH)
            cp.start()
            sends.append(cp)
        for k in range(1, N_DEV):
            peer = _flip(me, k)
            pltpu.make_async_remote_copy(
                src_ref=v_ref, dst_ref=out_ref.at[_index(peer)], send_sem=send_sems.at[k - 1],
                recv_sem=recv_sems.at[k - 1], device_id=peer, device_id_type=MESH).wait_recv()
        for cp in sends:
            cp.wait_send()

    return pl.pallas_call(
        body, name=name,
        out_shape=jax.ShapeDtypeStruct((N_DEV, rows, cols), v.dtype),
        in_specs=[pl.BlockSpec(memory_space=pltpu.VMEM)],
        out_specs=pl.BlockSpec(memory_space=pltpu.VMEM),
        scratch_shapes=[pltpu.SemaphoreType.DMA((N_DEV - 1,)), pltpu.SemaphoreType.DMA((N_DEV - 1,))],
        compiler_params=_params(None, VMEM_LIMIT),
    )(v)


def _split_start(name, make_copies, nsem, srcs, lands):
    n, k = len(srcs), len(lands)

    def body(*refs):
        for cp in make_copies(refs[:n], refs[n:n + k], refs[n + k], refs[n + k + 1]):
            cp.start()
        refs[-1][...] = jnp.zeros_like(refs[-1])

    hbm = pl.BlockSpec(memory_space=pltpu.HBM)
    sem = pl.BlockSpec(memory_space=pltpu.SEMAPHORE)
    arrays = [*srcs, *lands]
    outs = pl.pallas_call(
        body, name=name,
        out_shape=(pltpu.SemaphoreType.DMA((nsem,)), pltpu.SemaphoreType.DMA((nsem,)),
                   *[pltpu.HBM(t.shape, t.dtype) for t in arrays], jax.ShapeDtypeStruct((SUBLANES, 128), F32)),
        in_specs=[hbm] * (n + k),
        out_specs=(sem, sem, *[hbm] * (n + k), pl.BlockSpec(memory_space=pltpu.VMEM)),
        input_output_aliases={i: 2 + i for i in range(n + k)},
        compiler_params=pltpu.CompilerParams(has_side_effects=pltpu.SideEffectType.DATAFLOW_SIDE_EFFECTING),
    )(*[pltpu.with_memory_space_constraint(t, pltpu.HBM) for t in arrays])
    return outs[0], outs[1], outs[2:2 + n], outs[2 + n:2 + n + k], outs[-1]


def _split_wait(name, make_copies, flight, after):
    send_sems, recv_sems, srcs, lands, _ = flight
    n, k = len(srcs), len(lands)

    def body(*refs):
        for cp in make_copies(refs[:n], refs[n:n + k], refs[n + k], refs[n + k + 1]):
            cp.wait_send()
            cp.wait_recv()

    hbm = pl.BlockSpec(memory_space=pltpu.HBM)
    sem = pl.BlockSpec(memory_space=pltpu.SEMAPHORE)
    arrays = [*srcs, *lands]
    outs = pl.pallas_call(
        body, name=name, out_shape=tuple(pltpu.HBM(t.shape, t.dtype) for t in arrays),
        in_specs=[hbm] * (n + k) + [sem, sem, pl.BlockSpec(memory_space=pl.ANY)],
        out_specs=[hbm] * (n + k),
        input_output_aliases={i: i for i in range(n + k)},
        compiler_params=pltpu.CompilerParams(has_side_effects=pltpu.SideEffectType.DATAFLOW_SIDE_EFFECTING),
    )(*arrays, send_sems, recv_sems, after)
    return outs[:n], outs[n:]


def _remote(src, dst, send_sems, recv_sems, k, to):
    return pltpu.make_async_remote_copy(src_ref=src, dst_ref=dst, send_sem=send_sems.at[k], recv_sem=recv_sems.at[k],
                                        device_id=to, device_id_type=MESH)


def _peer_copies(shards, lands, send_sems, recv_sems):
    me = _my_pos()
    return [_remote(shards[a], lands[a].at[_index(me)], send_sems, recv_sems, a * 7 + k - 1, _flip(me, k))
            for a in range(len(shards)) for k in range(1, N_DEV)]


def _own_block_copies(shards, lands, send_sems, recv_sems):
    me = _my_pos()
    return [_remote(shards[0], lands[0].at[_index(me)], send_sems, recv_sems, i, _flip(me, k))
            for i, k in enumerate((1, 2, 4, 6))]


def _forward_copies(arrived, lands, send_sems, recv_sems):
    me = _my_pos()
    return [_remote(arrived[0].at[_index(_flip(me, 2 * m))], lands[0].at[_index(_flip(me, 2 * m))],
                    send_sems, recv_sems, m - 1, _flip(me, 1)) for m in range(1, 4)]


def _rs_to_sibling(name, stacks):
    n = len(stacks)

    def body(*refs):
        ins, outs = refs[:n], refs[n:2 * n]
        send_sems, recv_sems = refs[2 * n:]
        me = _my_pos()
        sib = _flip(me, 1)
        sends = []
        for a, (_, which) in enumerate(stacks):
            by_target = ins[a] if which is None else ins[a].at[which]
            for m in range(4):
                target = _flip(sib, 2 * m)
                cp = pltpu.make_async_remote_copy(
                    src_ref=by_target.at[_index(target)], dst_ref=outs[a].at[m],
                    send_sem=send_sems.at[a * 4 + m], recv_sem=recv_sems.at[a * 4 + m],
                    device_id=sib, device_id_type=MESH)
                cp.start()
                sends.append(cp)
        for cp in sends:
            cp.wait_recv()
        for cp in sends:
            cp.wait_send()

    any_spec = pl.BlockSpec(memory_space=pl.ANY)
    return pl.pallas_call(
        body, name=name,
        out_shape=[jax.ShapeDtypeStruct((4,) + s.shape[-2:], s.dtype) for s, _ in stacks],
        in_specs=[any_spec] * n, out_specs=[any_spec] * n,
        scratch_shapes=[pltpu.SemaphoreType.DMA((4 * n,)), pltpu.SemaphoreType.DMA((4 * n,))],
    )(*[s for s, _ in stacks])


def _chip_copies(srcs, lands, send_sems, recv_sems):
    me = _my_pos()
    return [_remote(srcs[a].at[m - 1], lands[a].at[m - 1], send_sems, recv_sems, a * 3 + m - 1, _flip(me, 2 * m))
            for a in range(len(srcs)) for m in range(1, 4)]


def _add_sibling(name, stack, recv, targets):
    stack, which = stack
    rows, cols = stack.shape[-2:]
    tr = min(rows, ADD_ROWS)

    def by_target(index):
        if which is None:
            return pl.BlockSpec((None, tr, cols), lambda *g: (index(*g), g[-2], 0))
        return pl.BlockSpec((None, None, tr, cols), lambda *g: (which, index(*g), g[-2], 0))

    def own_body(t_ref, a_ref, b_ref, o_ref):
        o_ref[...] = a_ref[...] + b_ref[...].astype(F32)

    own = pl.pallas_call(
        own_body, name=name + "_own",
        out_shape=jax.ShapeDtypeStruct((rows, cols), F32),
        grid_spec=pltpu.PrefetchScalarGridSpec(
            num_scalar_prefetch=1, grid=(rows // tr,),
            in_specs=[by_target(lambda i, t: t[0]),
                      pl.BlockSpec((None, tr, cols), lambda i, t: (0, i, 0))],
            out_specs=pl.BlockSpec((tr, cols), lambda i, t: (i, 0))),
        compiler_params=_params(("arbitrary",)),
    )(targets, stack, recv)

    def send_body(t_ref, a_ref, b_ref, o_ref):
        o_ref[...] = (a_ref[...] + b_ref[...].astype(F32)).astype(BF16)

    send = pl.pallas_call(
        send_body, name=name + "_send",
        out_shape=jax.ShapeDtypeStruct((3, rows, cols), BF16),
        grid_spec=pltpu.PrefetchScalarGridSpec(
            num_scalar_prefetch=1, grid=(3, rows // tr),
            in_specs=[by_target(lambda m, i, t: t[m + 1]),
                      pl.BlockSpec((None, tr, cols), lambda m, i, t: (m + 1, i, 0))],
            out_specs=pl.BlockSpec((None, tr, cols), lambda m, i, t: (m, i, 0))),
        compiler_params=_params(("arbitrary", "arbitrary")),
    )(targets, stack, recv)
    return own, send


def _allreduce_small(name, v):
    rows, cols = v.shape
    half = rows // 2
    assert rows % (2 * SUBLANES) == 0

    def body(v_ref, out_ref, from_sib, chip_half, from_chips, send_sems, recv_sems):
        me = _my_pos()
        sib = _flip(me, 1)
        mine = pl.ds(pl.multiple_of(me[2] * half, SUBLANES), half)
        theirs = pl.ds(pl.multiple_of((1 - me[2]) * half, SUBLANES), half)

        def copy(k, src, dst, to):
            return pltpu.make_async_remote_copy(src_ref=src, dst_ref=dst, send_sem=send_sems.at[k],
                                                recv_sem=recv_sems.at[k], device_id=to, device_id_type=MESH)

        to_sib = copy(0, v_ref.at[theirs], from_sib, sib)
        to_sib.start()
        to_sib.wait_recv()
        chip_half[...] = v_ref[mine, :] + from_sib[...]
        to_chips = [copy(m, chip_half, from_chips.at[m - 1], _flip(me, 2 * m)) for m in range(1, 4)]
        for cp in to_chips:
            cp.start()
        for cp in to_chips:
            cp.wait_recv()
        my_chip = 2 * me[0] + me[1]
        total = None
        for chip in range(4):
            slot = jnp.maximum(jnp.bitwise_xor(chip, my_chip) - 1, 0)
            part = jnp.where(chip == my_chip, chip_half[...], from_chips[slot])
            total = part if total is None else total + part
        out_ref[mine, :] = total
        swap = copy(4, out_ref.at[mine], out_ref.at[mine], sib)
        swap.start()
        copy(4, out_ref.at[theirs], out_ref.at[theirs], sib).wait_recv()
        for cp in [to_sib, swap] + to_chips:
            cp.wait_send()

    return pl.pallas_call(
        body, name=name, out_shape=jax.ShapeDtypeStruct((rows, cols), F32),
        in_specs=[pl.BlockSpec(memory_space=pltpu.VMEM)],
        out_specs=pl.BlockSpec(memory_space=pltpu.VMEM),
        scratch_shapes=[pltpu.VMEM((half, cols), F32), pltpu.VMEM((half, cols), F32),
                        pltpu.VMEM((3, half, cols), F32),
                        pltpu.SemaphoreType.DMA((5,)), pltpu.SemaphoreType.DMA((5,))],
        compiler_params=_params(None, VMEM_LIMIT),
    )(v)


def _mod_fwd(c_all, w_mod):
    def body(c_ref, w_ref, o_ref):
        c = c_ref[...]
        o_ref[...] = jnp.dot(c * _sigmoid(c), w_ref[...], preferred_element_type=F32,
                             precision=lax.Precision.HIGHEST)

    return pl.pallas_call(
        body, name="mod_fwd", out_shape=jax.ShapeDtypeStruct((N_DEV, w_mod.shape[1]), F32),
    )(c_all, w_mod)


def _mod_bwd(c_all, dmod_all, dmod_cols):
    def body(c_ref, da_ref, dc_ref, gb_ref, gw_ref):
        c = c_ref[...]
        acc = da_ref[0:1, :]
        for b in range(1, N_DEV):
            acc = acc + da_ref[b:b + 1, :]
        gb_ref[...] = acc
        gw_ref[...] = lax.dot_general(c * _sigmoid(c), dc_ref[...], (((0,), (0,)), ((), ())),
                                      preferred_element_type=F32, precision=lax.Precision.HIGHEST)

    return pl.pallas_call(
        body, name="mod_bwd",
        out_shape=[jax.ShapeDtypeStruct((1, dmod_all.shape[1]), F32),
                   jax.ShapeDtypeStruct((c_all.shape[1], dmod_cols.shape[1]), F32)],
    )(c_all, dmod_all, dmod_cols)


def _rope_partner(t):
    lane = lax.broadcasted_iota(jnp.int32, t.shape, 1)
    return jnp.where(lane < ROT_HALF, pltpu.roll(t, HEAD_DIM - ROT_HALF, 1), pltpu.roll(t, ROT_HALF, 1))


def _norm(x, mod, b_mod, g_norm):
    seq = x.shape[0]
    tm = PROJ_ROWS

    def body(x_ref, mod_ref, bmod_ref, g_ref, h_ref):
        xf = x_ref[...]
        rstd = lax.rsqrt(jnp.mean(xf * xf, axis=-1, keepdims=True) + NORM_EPS)
        shift = mod_ref[:, 0:D_MODEL] + bmod_ref[:, 0:D_MODEL]
        scale = mod_ref[:, D_MODEL:2 * D_MODEL] + bmod_ref[:, D_MODEL:2 * D_MODEL]
        h_ref[...] = (((xf * rstd) * g_ref[...]) * (1.0 + scale) + shift).astype(BF16)

    row = pl.BlockSpec((tm, D_MODEL), lambda i: (i, 0))
    const = lambda cols: pl.BlockSpec((1, cols), lambda i: (0, 0))
    return pl.pallas_call(
        body, name="norm", out_shape=jax.ShapeDtypeStruct((seq, D_MODEL), BF16), grid=(seq // tm,),
        in_specs=[row, const(3 * D_MODEL), const(3 * D_MODEL), const(D_MODEL)], out_specs=row,
        compiler_params=_params(("arbitrary",), VMEM_LIMIT),
    )(x, mod, b_mod, g_norm)


def _proj(name, h, w, slots, pieces, cosf, sinf, prior):
    seq = h.shape[0]
    tm = PROJ_ROWS
    count = pieces.shape[0]

    def body(slots_ref, pieces_ref, h_ref, w_ref, cos_ref, sin_ref, *rest):
        out_ref = rest[-1]
        piece = pieces_ref[pl.program_id(0)]

        @pl.when((piece < 2) | (piece > 3))
        def _():
            out_ref[...] = _dot(h_ref[...], w_ref[...])

        def rotated(gain):
            for pair in range(N_HEADS // 2):
                both = _dot(h_ref[...], w_ref[:, 2 * pair * HEAD_DIM:2 * (pair + 1) * HEAD_DIM])
                for hh in (2 * pair, 2 * pair + 1):
                    t = both[:, (hh % 2) * HEAD_DIM:(hh % 2 + 1) * HEAD_DIM]
                    t = t * cos_ref[...] + _rope_partner(t) * sin_ref[...]
                    out_ref[:, hh * HEAD_DIM:(hh + 1) * HEAD_DIM] = t if gain is None else t * gain

        @pl.when(piece == 2)
        def _():
            rotated(ATTN_SCALE)

        @pl.when(piece == 3)
        def _():
            rotated(None)

    row = lambda j, i, sl, pc: (i, 0)
    in_specs = [pl.BlockSpec((tm, D_MODEL), row),
                pl.BlockSpec((None, D_MODEL, D_MODEL), lambda j, i, sl, pc: (sl[j], 0, 0)),
                pl.BlockSpec((tm, HEAD_DIM), row), pl.BlockSpec((tm, HEAD_DIM), row)]
    args = [slots, pieces, h, w, cosf, sinf]
    aliases = {}
    if prior is not None:
        in_specs.append(pl.BlockSpec(memory_space=pl.ANY))
        args.append(prior)
        aliases = {6: 0}
    return pl.pallas_call(
        body, name=name,
        out_shape=jax.ShapeDtypeStruct((seq, 8 * D_MODEL), F32),
        grid_spec=pltpu.PrefetchScalarGridSpec(
            num_scalar_prefetch=2, grid=(count, seq // tm), in_specs=in_specs,
            out_specs=pl.BlockSpec((tm, D_MODEL), lambda j, i, sl, pc: (i, pc[j]))),
        input_output_aliases=aliases,
        compiler_params=_params(("arbitrary", "arbitrary"), VMEM_LIMIT),
    )(*args)


def _shift_down(v, s, head):
    rolled = pltpu.roll(v, s, 0)
    row = lax.broadcasted_iota(jnp.int32, head.shape, 0)
    first = jnp.where(row < s, pltpu.roll(head, s, 0), rolled[:SUBLANES, :])
    return jnp.concatenate([first, rolled[SUBLANES:, :]], axis=0)


def _shift_up(v, s, tail):
    rows = v.shape[0]
    rolled = pltpu.roll(v, rows - s, 0)
    row = lax.broadcasted_iota(jnp.int32, tail.shape, 0)
    last = jnp.where(row >= SUBLANES - s, pltpu.roll(tail, SUBLANES - s, 0), rolled[rows - SUBLANES:, :])
    return jnp.concatenate([rolled[:rows - SUBLANES, :], last], axis=0)


def _doubling(a, b, period, reverse):
    rows = a.shape[0]
    pos = lax.broadcasted_iota(jnp.int32, a.shape, 0) & (period - 1)
    k = 1
    while k < period:
        inside = (pos < period - k) if reverse else (pos >= k)
        shift = rows - k if reverse else k
        a_s = jnp.where(inside, pltpu.roll(a, shift, 0), 1.0)
        b_s = jnp.where(inside, pltpu.roll(b, shift, 0), 0.0)
        b = a * b_s + b
        a = a * a_s
        k *= 2
    return a, b


def _scan(a, b, boundary, reverse, a_scr, b_scr, spread):
    rows = a.shape[0]
    ntile = rows // SUBLANES
    a_scr[...], b_scr[...] = _doubling(a, b, SUBLANES, reverse)
    ends = pl.ds(0 if reverse else SUBLANES - 1, ntile, stride=SUBLANES)
    a_end, x_end = _doubling(a_scr[ends, :], b_scr[ends, :], ntile, reverse)
    x_end = x_end + a_end * boundary
    tile = lax.broadcasted_iota(jnp.int32, x_end.shape, 0)
    if reverse:
        incoming = jnp.where(tile == ntile - 1, boundary, pltpu.roll(x_end, ntile - 1, 0))
        last = x_end[0:1, :]
    else:
        incoming = jnp.where(tile == 0, boundary, pltpu.roll(x_end, 1, 0))
        last = x_end[ntile - 1:ntile, :]
    for s in range(SUBLANES):
        spread[pl.ds(s, ntile, stride=SUBLANES), :] = incoming
    return b_scr[...] + a_scr[...] * spread[...], last


def _conv_taps(xr, head):
    return [_shift_down(xr, 3, head), _shift_down(xr, 2, head), _shift_down(xr, 1, head), xr]


def _rnn_gates(xc, wa, ba, wx, bx, lam, keep):
    xcb = xc.astype(BF16)
    r = _sigmoid(_dot(xcb, wa.astype(BF16)) + ba)
    i = _sigmoid(_dot(xcb, wx.astype(BF16)) + bx)
    softplus = jnp.maximum(-lam, 0.0) + jnp.log(1.0 + jnp.exp(-jnp.abs(lam)))
    cl = -LRU_C * softplus
    log_a = cl * r
    a_raw = jnp.exp(log_a)
    mult_raw = jnp.sqrt(-_expm1_nonpos(2.0 * log_a, a_raw * a_raw))
    live = keep > 0.0
    return r, i, cl, a_raw, mult_raw, jnp.where(live, a_raw, 0.0), jnp.where(live, mult_raw, 1.0), live


def _rnn_specs(seq, rows, time_of):
    per = rows // SUBLANES
    vec = pl.BlockSpec((None, 1, 128), lambda hb, n: (hb, 0, 0))
    mat = pl.BlockSpec((None, 128, 128), lambda hb, n: (hb, 0, 0))
    return [pl.BlockSpec((rows, 128), lambda hb, n: (time_of(n), hb)),
            pl.BlockSpec((SUBLANES, 128), lambda hb, n: (jnp.maximum(time_of(n) * per - 1, 0), hb)),
            pl.BlockSpec((rows, 1), lambda hb, n: (time_of(n), 0)),
            pl.BlockSpec((None, SUBLANES, 128), lambda hb, n: (hb, 0, 0)),
            vec, mat, vec, mat, vec, vec]


def _rnn_fwd(pf, keep, conv_w8, conv_b, w_a, b_a, w_x, b_x, lam):
    seq = pf.shape[0]
    rows = RNN_ROWS

    def body(x_ref, xh_ref, keep_ref, cw_ref, cb_ref, wa_ref, ba_ref, wx_ref, bx_ref, lam_ref, hr_ref,
             carry, a_scr, b_scr, spread):
        n = pl.program_id(1)

        @pl.when(n == 0)
        def _():
            carry[...] = jnp.zeros_like(carry)

        xr = x_ref[...]
        head = jnp.where(n > 0, xh_ref[...], 0.0)
        taps = _conv_taps(xr, head)
        xc = cb_ref[...] + sum(cw_ref[k:k + 1, :] * taps[k] for k in range(4))
        _, i, _, _, _, a, mult, _ = _rnn_gates(xc, wa_ref[...], ba_ref[...], wx_ref[...], bx_ref[...],
                                               lam_ref[...], keep_ref[...])
        h, last = _scan(a, mult * i * xc, carry[0:1, :], False, a_scr, b_scr, spread)
        hr_ref[...] = h
        carry[...] = jnp.broadcast_to(last, carry.shape)

    chunk_f32 = pltpu.VMEM((rows, 128), F32)
    return pl.pallas_call(
        body, name="rnn_fwd",
        out_shape=jax.ShapeDtypeStruct((seq, D_MODEL), F32),
        grid=(RNN_BLOCKS, seq // rows),
        in_specs=_rnn_specs(seq, rows, lambda n: n),
        out_specs=pl.BlockSpec((rows, 128), lambda hb, n: (n, hb)),
        scratch_shapes=[pltpu.VMEM((SUBLANES, 128), F32), chunk_f32, chunk_f32, chunk_f32],
        compiler_params=_params(("arbitrary", "arbitrary"), VMEM_LIMIT),
    )(pf, pf, keep, conv_w8, conv_b, w_a, b_a, w_x, b_x, lam)


def _rnn_bwd(pf, hr, dhr, keep, conv_w8, conv_b, w_a, b_a, w_x, b_x, lam):
    seq = pf.shape[0]
    rows = RNN_ROWS
    nchunk = seq // rows
    per = rows // SUBLANES
    time_of = lambda n: nchunk - 1 - n

    def body(x_ref, xh_ref, keep_ref, cw_ref, cb_ref, wa_ref, ba_ref, wx_ref, bx_ref, lam_ref,
             hr_ref, hrh_ref, dhr_ref,
             dx_ref, gcw_ref, gcb_ref, gwa_ref, gba_ref, gwx_ref, gbx_ref, glam_ref,
             g_carry, dxc_tail, a_scr, b_scr, spread):
        n = pl.program_id(1)
        first_in_time = n == nchunk - 1

        @pl.when(n == 0)
        def _():
            g_carry[...] = jnp.zeros_like(g_carry)
            dxc_tail[...] = jnp.zeros_like(dxc_tail)
            for ref in (gcw_ref, gcb_ref, gwa_ref, gba_ref, gwx_ref, gbx_ref, glam_ref):
                ref[...] = jnp.zeros_like(ref)

        xr = x_ref[...]
        head = jnp.where(first_in_time, 0.0, xh_ref[...])
        taps = _conv_taps(xr, head)
        cw = cw_ref[...]
        xc = cb_ref[...] + sum(cw[k:k + 1, :] * taps[k] for k in range(4))
        wa, wx, lam = wa_ref[...], wx_ref[...], lam_ref[...]
        r, i, cl, a_raw, mult_raw, a, mult, live = _rnn_gates(xc, wa, ba_ref[...], wx, bx_ref[...], lam,
                                                               keep_ref[...])
        h_prev = _shift_down(hr_ref[...], 1, jnp.where(first_in_time, 0.0, hrh_ref[...]))

        row = lax.broadcasted_iota(jnp.int32, xr.shape, 0)
        last = row == rows - 1
        a_next = jnp.where(last, 0.0, pltpu.roll(a, rows - 1, 0))
        g, g_first = _scan(a_next, dhr_ref[...] + jnp.where(last, g_carry[0:1, :], 0.0),
                           jnp.zeros((1, 128), F32), True, a_scr, b_scr, spread)
        g_carry[...] = jnp.broadcast_to(a[0:1, :] * g_first, g_carry.shape)

        da = g * h_prev
        dmult = g * i * xc
        di = g * mult * xc
        dxc = g * mult * i
        dlog_a = jnp.where(live, da * a_raw - dmult * a_raw * a_raw / mult_raw, 0.0)
        dpa = (dlog_a * cl) * r * (1.0 - r)
        dpx = di * i * (1.0 - i)
        glam_ref[...] += jnp.sum(dlog_a * r, axis=0, keepdims=True) * (LRU_C * _sigmoid(-lam))
        xcb, dpab, dpxb = xc.astype(BF16), dpa.astype(BF16), dpx.astype(BF16)
        gwa_ref[...] += _dot_tn(xcb, dpab)
        gwx_ref[...] += _dot_tn(xcb, dpxb)
        gba_ref[...] += jnp.sum(dpa, axis=0, keepdims=True)
        gbx_ref[...] += jnp.sum(dpx, axis=0, keepdims=True)
        dxc = dxc + _dot_nt(dpab, wa.astype(BF16)) + _dot_nt(dpxb, wx.astype(BF16))

        gcb_ref[...] += jnp.sum(dxc, axis=0, keepdims=True)
        for k in range(4):
            gcw_ref[k:k + 1, :] += jnp.sum(dxc * taps[k], axis=0, keepdims=True)
        tail = dxc_tail[...]
        dx = cw[3:4, :] * dxc
        for k in range(3):
            dx = dx + cw[k:k + 1, :] * _shift_up(dxc, 3 - k, tail)
        dx_ref[...] = dx.astype(BF16)
        dxc_tail[...] = dxc[0:SUBLANES, :]

    blk = lambda hb, n: (hb, 0, 0)
    chunk = pl.BlockSpec((rows, 128), lambda hb, n: (time_of(n), hb))
    vec_out = pl.BlockSpec((None, 1, 128), blk)
    mat_out = pl.BlockSpec((None, 128, 128), blk)
    vec_shape = jax.ShapeDtypeStruct((RNN_BLOCKS, 1, 128), F32)
    mat_shape = jax.ShapeDtypeStruct((RNN_BLOCKS, 128, 128), F32)
    return pl.pallas_call(
        body, name="rnn_bwd",
        out_shape=[jax.ShapeDtypeStruct((seq, D_MODEL), BF16),
                   jax.ShapeDtypeStruct((RNN_BLOCKS, SUBLANES, 128), F32), vec_shape,
                   mat_shape, vec_shape, mat_shape, vec_shape, vec_shape],
        grid=(RNN_BLOCKS, nchunk),
        in_specs=_rnn_specs(seq, rows, time_of) + [
            chunk, pl.BlockSpec((SUBLANES, 128), lambda hb, n: (jnp.maximum(time_of(n) * per - 1, 0), hb)), chunk],
        out_specs=[chunk, pl.BlockSpec((None, SUBLANES, 128), blk), vec_out,
                   mat_out, vec_out, mat_out, vec_out, vec_out],
        scratch_shapes=[pltpu.VMEM((SUBLANES, 128), F32), pltpu.VMEM((SUBLANES, 128), F32)]
                       + [pltpu.VMEM((rows, 128), F32)] * 3,
        compiler_params=_params(("arbitrary", "arbitrary"), VMEM_LIMIT),
    )(pf, pf, keep, conv_w8, conv_b, w_a, b_a, w_x, b_x, lam, hr, hr, dhr)


def _unit_rows(dil, r, j):
    start = j * KEY_BLOCK * dil + r
    return pl.ds(start, KEY_BLOCK) if dil == 1 else pl.ds(start, KEY_BLOCK, stride=dil)


def _attn_fwd(proj):
    nh, seq = N_HEADS, proj.shape[0]
    nchunk = seq // SPAN
    nblk = SPAN // KEY_BLOCK
    wide = DILATIONS[-1]

    def body(q_ref, k_ref, v_ref, kp_ref, vp_ref, o_ref, l1_ref, l4_ref, l16_ref,
             acc, m_s, l_s, q16, k16, v16, k16p, v16p, acc16, m16, l16, tmp):
        n = pl.program_id(1)
        qi = lax.broadcasted_iota(jnp.int32, (KEY_BLOCK, KEY_BLOCK), 0)
        ki = lax.broadcasted_iota(jnp.int32, (KEY_BLOCK, KEY_BLOCK), 1)
        bias_own = jnp.where(ki <= qi, 0.0, NEG_INF)
        bias_before = jnp.where(ki >= qi, 0.0, NEG_INF)
        bias_mid = jnp.concatenate([bias_before, bias_own], axis=1)
        bias_first = jnp.concatenate([jnp.where(n > 0, bias_before, NEG_INF), bias_own], axis=1)
        ones = jnp.ones((2 * KEY_BLOCK, HEAD_DIM), BF16)
        diag = qi == ki

        @pl.when(n == 0)
        def _():
            k16p[...] = jnp.zeros_like(k16p)
            v16p[...] = jnp.zeros_like(v16p)

        def unit(qf, kpb, kb, vpb, vb, bias, state, rows, first):
            acc_r, m_r, l_r = state
            kcat = jnp.concatenate([kpb, kb], axis=0)
            vaug = jnp.concatenate([jnp.concatenate([vpb, vb], axis=0), ones], axis=1)
            s = _dot_nt(qf.astype(BF16), kcat) + bias
            mx = jnp.max(s, axis=-1, keepdims=True)
            if first:
                m_new = jnp.broadcast_to(mx, (KEY_BLOCK, HEAD_DIM))
            else:
                m_old = m_r[rows, :]
                m_new = jnp.maximum(m_old, mx)
            pv = _dot(jnp.exp(s - jnp.concatenate([m_new, m_new], axis=1)).astype(BF16), vaug)
            if first:
                acc_r[rows, :] = pv[:, :HEAD_DIM]
                l_r[rows, :] = pv[:, HEAD_DIM:]
            else:
                alpha = jnp.exp(m_old - m_new)
                acc_r[rows, :] = alpha * acc_r[rows, :] + pv[:, :HEAD_DIM]
                l_r[rows, :] = alpha * l_r[rows, :] + pv[:, HEAD_DIM:]
            m_r[rows, :] = m_new

        for gi, dil in enumerate(DILATIONS[:-1]):
            nb = nblk // dil
            for r in range(dil):
                prow = _unit_rows(dil, r, nb - 1)
                kpb, vpb = kp_ref[prow, :].astype(BF16), vp_ref[prow, :].astype(BF16)
                for j in range(nb):
                    rows = _unit_rows(dil, r, j)
                    kb, vb = k_ref[rows, :].astype(BF16), v_ref[rows, :].astype(BF16)
                    unit(q_ref[rows, :], kpb, kb, vpb, vb, bias_first if j == 0 else bias_mid,
                         (acc, m_s, l_s), rows, gi == 0)
                    kpb, vpb = kb, vb

        for src, dst in ((q_ref, q16), (k_ref, k16), (v_ref, v16), (acc, acc16), (m_s, m16), (l_s, l16)):
            _to_residue_major(src, tmp, dst)
        for r in range(wide):
            rows = pl.ds(r * KEY_BLOCK, KEY_BLOCK)
            unit(q16[rows, :], k16p[rows, :].astype(BF16), k16[rows, :].astype(BF16), v16p[rows, :].astype(BF16),
                 v16[rows, :].astype(BF16), bias_first, (acc16, m16, l16), rows, False)
        k16p[...] = k16[...]
        v16p[...] = v16[...]

        den = l16[...]
        acc16[...] = acc16[...] * (1.0 / den)
        m16[...] = m16[...] + jnp.log(den)
        _from_residue_major(acc16, tmp, o_ref, False)
        _from_residue_major(m16, tmp, m_s, False)

        def lse_row(ref, rows):
            return jnp.sum(jnp.where(diag, ref[rows, :], 0.0), axis=0, keepdims=True)

        for dil, out in zip(DILATIONS[:-1], (l1_ref, l4_ref)):
            nb = nblk // dil
            for r in range(dil):
                for j in range(nb):
                    out[r * nb + j:r * nb + j + 1, :] = lse_row(m_s, _unit_rows(dil, r, j))
        for r in range(wide):
            l16_ref[r:r + 1, :] = lse_row(m16, pl.ds(r * KEY_BLOCK, KEY_BLOCK))

    cur = lambda piece: pl.BlockSpec((SPAN, HEAD_DIM), lambda h, n: (n, piece * nh + h))
    before = lambda piece: pl.BlockSpec((SPAN, HEAD_DIM), lambda h, n: (jnp.maximum(n - 1, 0), piece * nh + h))
    blk = pl.BlockSpec((None, SPAN, HEAD_DIM), lambda h, n: (h, n, 0))
    lblk = pl.BlockSpec((None, nblk, KEY_BLOCK), lambda h, n: (h, n, 0))
    lshape = jax.ShapeDtypeStruct((nh, seq // KEY_BLOCK, KEY_BLOCK), F32)
    o, l1, l4, l16 = pl.pallas_call(
        body, name="attn_fwd",
        out_shape=[jax.ShapeDtypeStruct((nh, seq, HEAD_DIM), F32), lshape, lshape, lshape],
        grid=(nh, nchunk), in_specs=[cur(2), cur(3), cur(4), before(3), before(4)],
        out_specs=[blk, lblk, lblk, lblk],
        scratch_shapes=[pltpu.VMEM((SPAN, HEAD_DIM), F32)] * 12,
        compiler_params=_params(("arbitrary", "arbitrary"), VMEM_LIMIT),
    )(proj, proj, proj, proj, proj)
    return o, (l1, l4, l16)


def _to_residue_major(src, tmp, dst):
    quarter = SPAN // 4
    for r4 in range(4):
        tmp[r4 * quarter:(r4 + 1) * quarter, :] = src[pl.ds(r4, quarter, stride=4), :]
    for r4 in range(4):
        for rp in range(4):
            r = r4 + 4 * rp
            dst[r * KEY_BLOCK:(r + 1) * KEY_BLOCK, :] = tmp[pl.ds(r4 * quarter + rp, KEY_BLOCK, stride=4), :]


def _from_residue_major(src, tmp, dst, add):
    quarter = SPAN // 4
    for r4 in range(4):
        for rp in range(4):
            r = r4 + 4 * rp
            tmp[pl.ds(r4 * quarter + rp, KEY_BLOCK, stride=4), :] = src[r * KEY_BLOCK:(r + 1) * KEY_BLOCK, :]
    for r4 in range(4):
        rows = pl.ds(r4, quarter, stride=4)
        part = tmp[r4 * quarter:(r4 + 1) * quarter, :]
        dst[rows, :] = dst[rows, :] + part if add else part


def _attn_bwd(proj, do, o, lses, cosf, sinf):
    nh, seq = N_HEADS, proj.shape[0]
    nchunk = seq // SPAN
    nblk = SPAN // KEY_BLOCK
    wide = DILATIONS[-1]
    assert SPAN == wide * KEY_BLOCK

    def body(q_ref, k_ref, v_ref, do_ref, o_ref, kp_ref, vp_ref, l1_ref, l4_ref, l16_ref,
             cos_ref, sin_ref, cosp_ref, sinp_ref, dq_ref, dk_ref, dv_ref,
             dq_acc, dkc_acc, dvc_acc, dkp_acc, dvp_acc, q16, k16, v16, do16, o16, k16p, v16p,
             dq16, dkc16, dvc16, dkp16, dvp16, tmp, pt_s, ds_s, kcat_s, qb_s, dob_s):
        n = pl.program_id(1)
        ki = lax.broadcasted_iota(jnp.int32, (KEY_BLOCK, KEY_BLOCK), 0)
        qi = lax.broadcasted_iota(jnp.int32, (KEY_BLOCK, KEY_BLOCK), 1)
        bias_own = jnp.where(ki <= qi, 0.0, NEG_INF)
        bias_before = jnp.where(ki >= qi, 0.0, NEG_INF)
        bias_mid = jnp.concatenate([bias_before, bias_own], axis=0)
        bias_first = jnp.concatenate([jnp.where(n > 0, bias_before, NEG_INF), bias_own], axis=0)
        ones8 = jnp.ones((SUBLANES, HEAD_DIM), BF16)

        def row_dot(a, b):
            prod = a * b
            hi = prod.astype(BF16)
            lo = (prod - hi.astype(F32)).astype(BF16)
            return (_dot_nt(ones8, hi) + _dot_nt(ones8, lo))[0:1, :]

        def group(units, srcs, before, l_ref, accs):
            src_q, src_do, src_o, src_k, src_v = srcs
            before_k, before_v = before
            acc_q, acc_kc, acc_vc, acc_kp, acc_vp = accs
            kb = vb = None
            for u, (rows, prow, outside, lrow, _) in enumerate(units):
                dof = src_do[rows, :]
                qb, dob = src_q[rows, :].astype(BF16), dof.astype(BF16)
                kpb, vpb = (before_k[prow, :].astype(BF16), before_v[prow, :].astype(BF16)) if outside else (kb, vb)
                kb, vb = src_k[rows, :].astype(BF16), src_v[rows, :].astype(BF16)
                kcat = jnp.concatenate([kpb, kb], axis=0)
                vcat = jnp.concatenate([vpb, vb], axis=0)
                bias = bias_first if outside else bias_mid
                pt = jnp.exp(_dot_nt(kcat, qb) + bias - l_ref[lrow:lrow + 1, :])
                dst = pt * (_dot_nt(vcat, dob) - row_dot(dof, src_o[rows, :]))
                pt_s[u], ds_s[u], kcat_s[u], qb_s[u], dob_s[u] = pt.astype(BF16), dst.astype(BF16), kcat, qb, dob
            for u, (rows, _, _, _, _) in enumerate(units):
                acc_q[rows, :] += _dot_tn(ds_s[u], kcat_s[u])
            for u, (rows, prow, outside, _, nxt) in enumerate(units):
                dk = _dot(ds_s[u, KEY_BLOCK:, :], qb_s[u])
                dv = _dot(pt_s[u, KEY_BLOCK:, :], dob_s[u])
                if nxt is not None:
                    dk = dk + _dot(ds_s[nxt, :KEY_BLOCK, :], qb_s[nxt])
                    dv = dv + _dot(pt_s[nxt, :KEY_BLOCK, :], dob_s[nxt])
                acc_kc[rows, :] += dk
                acc_vc[rows, :] += dv
                if outside:
                    acc_kp[prow, :] += _dot(ds_s[u, :KEY_BLOCK, :], qb_s[u])
                    acc_vp[prow, :] += _dot(pt_s[u, :KEY_BLOCK, :], dob_s[u])

        @pl.when(n == 0)
        def _():
            for ref in (dkp_acc, dvp_acc, dkp16, dvp16, k16p, v16p):
                ref[...] = jnp.zeros_like(ref)

        @pl.when(n < nchunk)
        def _():
            for ref in (dq_acc, dkc_acc, dvc_acc, dq16, dkc16, dvc16):
                ref[...] = jnp.zeros_like(ref)
            for src, dst in ((q_ref, q16), (k_ref, k16), (v_ref, v16), (do_ref, do16), (o_ref, o16)):
                _to_residue_major(src, tmp, dst)
            natural = (q_ref, do_ref, o_ref, k_ref, v_ref)
            for dil, l_ref in zip(DILATIONS[:-1], (l1_ref, l4_ref)):
                nb = nblk // dil
                units = [(_unit_rows(dil, r, j), _unit_rows(dil, r, (j - 1) % nb), j == 0, r * nb + j,
                          r * nb + j + 1 if j + 1 < nb else None) for r in range(dil) for j in range(nb)]
                group(units, natural, (kp_ref, vp_ref), l_ref, (dq_acc, dkc_acc, dvc_acc, dkp_acc, dvp_acc))
            blocks = [pl.ds(r * KEY_BLOCK, KEY_BLOCK) for r in range(wide)]
            group([(rows, rows, True, r, None) for r, rows in enumerate(blocks)], (q16, do16, o16, k16, v16),
                  (k16p, v16p), l16_ref, (dq16, dkc16, dvc16, dkp16, dvp16))
            _from_residue_major(dq16, tmp, dq_acc, True)
            dq = dq_acc[...]
            dq_ref[...] = ((dq * cos_ref[...] - _rope_partner(dq) * sin_ref[...]) * ATTN_SCALE).astype(BF16)

        @pl.when(n > 0)
        def _():
            _from_residue_major(dkp16, tmp, dkp_acc, True)
            _from_residue_major(dvp16, tmp, dvp_acc, True)
            dk = dkp_acc[...]
            dk_ref[...] = (dk * cosp_ref[...] - _rope_partner(dk) * sinp_ref[...]).astype(BF16)
            dv_ref[...] = dvp_acc[...].astype(BF16)

        @pl.when(n < nchunk)
        def _():
            for src, dst in ((dkc_acc, dkp_acc), (dvc_acc, dvp_acc), (dkc16, dkp16), (dvc16, dvp16),
                             (k16, k16p), (v16, v16p)):
                dst[...] = src[...]

    last = nchunk - 1
    cur = lambda h, n: (h, jnp.minimum(n, last), 0)
    prev = lambda h, n: (h, jnp.clip(n - 1, 0, last), 0)
    blk = lambda idx: pl.BlockSpec((None, SPAN, HEAD_DIM), idx)
    lblk = pl.BlockSpec((None, nblk, KEY_BLOCK), cur)
    tab = pl.BlockSpec((SPAN, HEAD_DIM), lambda h, n: (jnp.minimum(n, last), 0))
    tabp = pl.BlockSpec((SPAN, HEAD_DIM), lambda h, n: (jnp.clip(n - 1, 0, last), 0))
    out_q = pl.BlockSpec((SPAN, HEAD_DIM), lambda h, n: (jnp.minimum(n, last), h))
    out_kv = pl.BlockSpec((SPAN, HEAD_DIM), lambda h, n: (jnp.clip(n - 1, 0, last), h))
    shape = jax.ShapeDtypeStruct((seq, nh * HEAD_DIM), BF16)
    tok = lambda piece, row: pl.BlockSpec((SPAN, HEAD_DIM), lambda h, n: (row(n), piece * nh + h))
    row_cur, row_prev = (lambda n: jnp.minimum(n, last)), (lambda n: jnp.clip(n - 1, 0, last))
    return pl.pallas_call(
        body, name="attn_bwd", out_shape=[shape, shape, shape], grid=(nh, nchunk + 1),
        in_specs=[tok(2, row_cur), tok(3, row_cur), tok(4, row_cur), blk(cur), blk(cur),
                  tok(3, row_prev), tok(4, row_prev)] + [lblk] * 3 + [tab, tab, tabp, tabp],
        out_specs=[out_q, out_kv, out_kv],
        scratch_shapes=[pltpu.VMEM((SPAN, HEAD_DIM), F32)] * 18
                       + [pltpu.VMEM((nblk, 2 * KEY_BLOCK, HEAD_DIM), BF16)] * 3
                       + [pltpu.VMEM((nblk, KEY_BLOCK, HEAD_DIM), BF16)] * 2,
        compiler_params=_params(("arbitrary", "arbitrary"), VMEM_LIMIT),
    )(proj, proj, proj, do, o, proj, proj, *lses, cosf, sinf, cosf, sinf)


def _hub(x, tgt, hr, pf, o_hm, mod, b_mod, b_gate, g_final, w_out_rnn, w_out_attn, w_o):
    seq = x.shape[0]
    tm = HUB_ROWS
    nsteps = seq // tm

    def body(x_ref, t_ref, hr_ref, zr_ref, za_ref, gr_ref, ga_ref, o_ref, mod_ref, bmod_ref, bg_ref, gf_ref,
             wr_hbm, wa_hbm, wo_hbm,
             dx2_ref, dhr_ref, dzr_ref, do_ref, dza_ref, dgr_ref, dga_ref,
             ur_ref, dyr_ref, ua_ref, dya_ref, mg_ref, dmo_ref,
             ggf_ref, gbg_ref, dgate_ref, loss_ref,
             wr, wa, wo, sem):
        step = pl.program_id(0)

        @pl.when(step == 0)
        def _():
            for src, dst in ((wr_hbm, wr), (wa_hbm, wa), (wo_hbm, wo)):
                cp = pltpu.make_async_copy(src, dst, sem)
                cp.start()
                cp.wait()
            for ref in (ggf_ref, gbg_ref, dgate_ref, loss_ref):
                ref[...] = jnp.zeros_like(ref)

        gate = mod_ref[:, 2 * D_MODEL:] + bmod_ref[:, 2 * D_MODEL:]
        gfin = gf_ref[...]
        hr_t, zr, za = hr_ref[...], zr_ref[...], za_ref[...]
        o = jnp.concatenate([o_ref[hh] for hh in range(N_HEADS)], axis=1)
        sig_zr, sig_za = _sigmoid(zr), _sigmoid(za)
        silu_zr, silu_za = zr * sig_zr, za * sig_za
        u_rnn = (hr_t * silu_zr).astype(BF16)
        u_attn = (o * silu_za).astype(BF16)
        y_rnn = _dot(u_rnn, wr[...])
        y_attn = _dot(u_attn, wa[...])
        sr = _sigmoid(gr_ref[...] + bg_ref[:, :D_MODEL])
        sa = _sigmoid(ga_ref[...] + bg_ref[:, D_MODEL:])
        merged = (sr * y_rnn + sa * y_attn).astype(BF16)
        mo = _dot(merged, wo[...])
        x2 = x_ref[...] + gate * mo
        rstd = lax.rsqrt(jnp.mean(x2 * x2, axis=-1, keepdims=True) + NORM_EPS)
        xn = x2 * rstd
        err = xn * gfin - t_ref[...]
        loss_ref[...] += 0.5 * jnp.sum(jnp.sum(err * err, axis=-1, keepdims=True) * (1.0 / D_MODEL),
                                       axis=0, keepdims=True)

        dy = err * (1.0 / D_MODEL)
        ggf_ref[...] += jnp.sum(dy * xn, axis=0, keepdims=True)
        dxn = dy * gfin
        dx2 = rstd * (dxn - xn * jnp.mean(dxn * xn, axis=-1, keepdims=True))
        dx2_ref[...] = dx2
        dgate_ref[...] += jnp.sum(dx2 * mo, axis=0, keepdims=True)
        dmo = (dx2 * gate).astype(BF16)
        dmerged = _dot_nt(dmo, wo[...])
        mg_ref[...] = merged
        dmo_ref[...] = dmo
        dy_rnn = (dmerged * sr).astype(BF16)
        dy_attn = (dmerged * sa).astype(BF16)
        dg_r = dmerged * y_rnn * sr * (1.0 - sr)
        dg_a = dmerged * y_attn * sa * (1.0 - sa)
        dgr_ref[...] = dg_r.astype(BF16)
        dga_ref[...] = dg_a.astype(BF16)
        gbg_ref[:, :D_MODEL] += jnp.sum(dg_r, axis=0, keepdims=True)
        gbg_ref[:, D_MODEL:] += jnp.sum(dg_a, axis=0, keepdims=True)
        du_rnn = _dot_nt(dy_rnn, wr[...])
        du_attn = _dot_nt(dy_attn, wa[...])
        ur_ref[...] = u_rnn
        dyr_ref[...] = dy_rnn
        ua_ref[...] = u_attn
        dya_ref[...] = dy_attn
        dhr_ref[...] = du_rnn * silu_zr
        dzr_ref[...] = (du_rnn * hr_t * (sig_zr * (1.0 + zr * (1.0 - sig_zr)))).astype(BF16)
        dza_ref[...] = (du_attn * o * (sig_za * (1.0 + za * (1.0 - sig_za)))).astype(BF16)
        d_o = du_attn * silu_za
        for hh in range(N_HEADS):
            do_ref[hh] = d_o[:, hh * HEAD_DIM:(hh + 1) * HEAD_DIM]

    row = pl.BlockSpec((tm, D_MODEL), lambda i: (i, 0))
    piece = lambda slot: pl.BlockSpec((tm, D_MODEL), lambda i: (i, slot))
    hm = pl.BlockSpec((N_HEADS, tm, HEAD_DIM), lambda i: (0, i, 0))
    const = lambda cols: pl.BlockSpec((1, cols), lambda i: (0, 0))
    any_spec = pl.BlockSpec(memory_space=pl.ANY)
    act_f32 = jax.ShapeDtypeStruct((seq, D_MODEL), F32)
    act_bf16 = jax.ShapeDtypeStruct((seq, D_MODEL), BF16)
    return pl.pallas_call(
        body, name="hub",
        out_shape=[act_f32, act_f32, act_bf16, jax.ShapeDtypeStruct((N_HEADS, seq, HEAD_DIM), F32),
                   act_bf16, act_bf16, act_bf16] + [act_bf16] * 6 + [
                   jax.ShapeDtypeStruct((1, D_MODEL), F32), jax.ShapeDtypeStruct((1, 2 * D_MODEL), F32),
                   jax.ShapeDtypeStruct((1, D_MODEL), F32), jax.ShapeDtypeStruct((1, 1), F32)],
        grid=(nsteps,),
        in_specs=[row, row, row, piece(1), piece(5), piece(6), piece(7), hm,
                  const(3 * D_MODEL), const(3 * D_MODEL), const(2 * D_MODEL), const(D_MODEL),
                  any_spec, any_spec, any_spec],
        out_specs=[row, row, row, hm, row, row, row] + [row] * 6 + [
                   const(D_MODEL), const(2 * D_MODEL), const(D_MODEL), const(1)],
        scratch_shapes=[pltpu.VMEM((D_MODEL, D_MODEL), BF16)] * 3 + [pltpu.SemaphoreType.DMA],
        compiler_params=_params(("arbitrary",), VMEM_LIMIT),
    )(x, tgt, hr, pf, pf, pf, pf, o_hm, mod, b_mod, b_gate, g_final, w_out_rnn, w_out_attn, w_o)


def _pair_grads(name, lefts, rights):
    n = len(rights)
    shared = len(lefts) == 1
    seq = rights[0].shape[0]
    tk = WGRAD_ROWS
    nk = seq // tk

    def body(*refs):
        l_refs, r_refs = refs[:len(lefts)], refs[len(lefts):len(lefts) + n]
        out_ref, low_ref = refs[len(lefts) + n:]
        j, kk = pl.program_id(0), pl.program_id(1)

        @pl.when(kk == 0)
        def _():
            out_ref[...] = jnp.zeros_like(out_ref)

        for m in range(n):
            @pl.when(j == m)
            def _(m=m):
                out_ref[...] += _dot_tn(l_refs[0 if shared else m][...], r_refs[m][...])

        @pl.when(kk == nk - 1)
        def _():
            low_ref[...] = out_ref[...].astype(BF16)

    def spec(m):
        return pl.BlockSpec((tk, D_MODEL), lambda j, kk: (jnp.where(j == m, kk, jnp.where(j < m, 0, nk - 1)), 0))

    left_specs = [pl.BlockSpec((tk, D_MODEL), lambda j, kk: (kk, 0))] if shared else [spec(m) for m in range(n)]
    out_spec = pl.BlockSpec((None, D_MODEL, D_MODEL), lambda j, kk: (j, 0, 0))
    return pl.pallas_call(
        body, name=name,
        out_shape=[jax.ShapeDtypeStruct((n, D_MODEL, D_MODEL), F32), jax.ShapeDtypeStruct((n, D_MODEL, D_MODEL), BF16)],
        grid=(n, nk),
        in_specs=left_specs + [spec(m) for m in range(n)],
        out_specs=[out_spec, out_spec],
        compiler_params=_params(("arbitrary", "arbitrary"), VMEM_LIMIT),
    )(*lefts, *rights)


def _dh_dx(pieces, w_near, w_far, x, dx2, mod, b_mod, g_norm):
    seq = x.shape[0]
    tm = DX_ROWS

    def body(*refs):
        p_refs = refs[:8]
        near_hbm, far_hbm, x_ref, dx2_ref, mod_ref, bmod_ref, g_ref = refs[8:15]
        gx_ref, dshift_ref, dscale_ref, ggn_ref, w_scr, sem = refs[15:]
        step = pl.program_id(0)

        @pl.when(step == 0)
        def _():
            me = _my_pos()
            sib = _flip(me, 1)
            moves = [(near_hbm, _index(_flip(me, 2 * m))) for m in range(4)] + [(near_hbm, _index(sib))]
            moves += [(far_hbm, _index(_flip(sib, 2 * m))) for m in range(1, 4)]
            for src, t in moves:
                cp = pltpu.make_async_copy(src.at[t], w_scr.at[t], sem)
                cp.start()
                cp.wait()
            for ref in (dshift_ref, dscale_ref, ggn_ref):
                ref[...] = jnp.zeros_like(ref)

        dh = _dot_nt(p_refs[0][...], w_scr[0])
        for j in range(1, 8):
            dh = dh + _dot_nt(p_refs[j][...], w_scr[j])
        scale1 = 1.0 + mod_ref[:, D_MODEL:2 * D_MODEL] + bmod_ref[:, D_MODEL:2 * D_MODEL]
        g = g_ref[...]
        xf = x_ref[...]
        rstd_t = lax.rsqrt(jnp.mean(xf * xf, axis=-1, keepdims=True) + NORM_EPS)
        xn = xf * rstd_t
        dshift_ref[...] += jnp.sum(dh, axis=0, keepdims=True)
        dscale_ref[...] += jnp.sum(dh * (xn * g), axis=0, keepdims=True)
        ggn_ref[...] += jnp.sum(dh * scale1 * xn, axis=0, keepdims=True)
        dxn = dh * (g * scale1)
        gx_ref[...] = rstd_t * (dxn - xn * jnp.mean(dxn * xn, axis=-1, keepdims=True)) + dx2_ref[...]

    row = pl.BlockSpec((tm, D_MODEL), lambda i: (i, 0))
    const = lambda cols: pl.BlockSpec((1, cols), lambda i: (0, 0))
    vec = jax.ShapeDtypeStruct((1, D_MODEL), F32)
    return pl.pallas_call(
        body, name="dh_dx",
        out_shape=[jax.ShapeDtypeStruct((seq, D_MODEL), F32), vec, vec, vec],
        grid=(seq // tm,),
        in_specs=[row] * 8 + [pl.BlockSpec(memory_space=pl.ANY), pl.BlockSpec(memory_space=pl.ANY), row, row,
                              const(3 * D_MODEL), const(3 * D_MODEL), const(D_MODEL)],
        out_specs=[row, const(D_MODEL), const(D_MODEL), const(D_MODEL)],
        scratch_shapes=[pltpu.VMEM((8, D_MODEL, D_MODEL), BF16), pltpu.SemaphoreType.DMA],
        compiler_params=_params(("arbitrary",), VMEM_LIMIT),
    )(*pieces, w_near, w_far, x, dx2, mod, b_mod, g_norm)


def _adamw(name, w, g, m, v, recv=None):
    rows, cols = w.shape
    tr = rows if rows <= 256 else 256

    def body(*refs):
        w_ref, g_ref, m_ref, v_ref = refs[:4]
        d_ref, nm_ref, nv_ref = refs[-3:] if recv is None else refs[5:8]
        gv = g_ref[...]
        if recv is not None:
            r_ref, g_out = refs[4], refs[8]
            gv = ((gv + r_ref[0].astype(F32)) + r_ref[1].astype(F32)) + r_ref[2].astype(F32)
            g_out[...] = gv
        nm = ADAM_B1 * m_ref[...] + (1.0 - ADAM_B1) * gv
        nv = ADAM_B2 * v_ref[...] + (1.0 - ADAM_B2) * (gv * gv)
        m_hat = nm / (1.0 - ADAM_B1 ** ADAM_STEP)
        v_hat = nv / (1.0 - ADAM_B2 ** ADAM_STEP)
        d_ref[...] = -ADAM_LR * (m_hat / (jnp.sqrt(v_hat) + ADAM_EPS) + ADAM_WD * w_ref[...])
        nm_ref[...] = nm
        nv_ref[...] = nv

    spec = pl.BlockSpec((tr, cols), lambda i: (i, 0))
    shape = jax.ShapeDtypeStruct((rows, cols), F32)
    if recv is None:
        return pl.pallas_call(
            body, name=name, out_shape=[shape, shape, shape], grid=(rows // tr,),
            in_specs=[spec] * 4, out_specs=[spec] * 3,
            compiler_params=_params(("arbitrary",)),
        )(w, g, m, v)
    return pl.pallas_call(
        body, name=name, out_shape=[shape] * 4, grid=(rows // tr,),
        in_specs=[spec] * 4 + [pl.BlockSpec((3, tr, cols), lambda i: (0, i, 0))], out_specs=[spec] * 4,
        compiler_params=_params(("arbitrary",)),
    )(w, g, m, v, recv)


def kernel(x, c, positions, g_norm, w_mod, b_mod, w_in, b_gate, conv_w, conv_b, w_a, b_a, w_x, b_x, lam, w_out_rnn, w_out_attn, w_o, g_final, loss_target, m_g_norm, m_w_mod, m_b_mod, m_w_in, m_b_gate, m_conv_w, m_conv_b, m_w_a, m_b_a, m_w_x, m_b_x, m_lam, m_w_out_rnn, m_w_out_attn, m_w_o, m_g_final, v_g_norm, v_w_mod, v_b_mod, v_w_in, v_b_gate, v_conv_w, v_conv_b, v_w_a, v_b_a, v_w_x, v_b_x, v_lam, v_w_out_rnn, v_w_out_attn, v_w_o, v_g_final):
    seq = x.shape[1]
    me = _index(_my_pos())
    xs, tgt = x[0], loss_target[0]

    inv_freq = ROPE_THETA ** (-jnp.arange(0, 2 * ROT_HALF, 2, dtype=F32) / (2 * ROT_HALF))
    ang = (positions[0].astype(F32).reshape(seq // SUBLANES, SUBLANES, 1) * inv_freq).reshape(seq // SUBLANES, 128)
    cos, sin = jnp.cos(ang).reshape(seq, ROT_HALF), jnp.sin(ang).reshape(seq, ROT_HALF)
    rest = HEAD_DIM - 2 * ROT_HALF
    cosf = jnp.concatenate([cos, cos, jnp.ones((seq, rest), F32)], axis=1)
    sinf = jnp.concatenate([-sin, sin, jnp.zeros((seq, rest), F32)], axis=1)
    keep = (positions[0] != 0).astype(F32)[:, None]

    both = _ag_small("gather_c_conv_w", jnp.concatenate(
        [jnp.broadcast_to(c, (SUBLANES, D_MODEL)), jnp.pad(conv_w[0], ((0, SUBLANES - 4), (0, 0)))], axis=1))
    c_all, conv_w8 = both[:, 0, :D_MODEL], both[:, :, D_MODEL:]
    mod_cols = w_mod.shape[2]
    mod_part = _ag_small("gather_mod", _mod_fwd(c_all, w_mod[0]))
    mod = lax.dynamic_index_in_dim(mod_part, me, axis=1, keepdims=False).reshape(1, N_DEV * mod_cols)

    slot = lambda t: lax.dynamic_update_slice(lax.empty((N_DEV,) + t.shape, t.dtype), t[None], (me, 0, 0))
    w_in_own = w_in[0].astype(BF16)
    mod, w_in_own = lax.optimization_barrier((mod, w_in_own))
    first = _split_start("gather_w_in_start", _own_block_copies, 4, [w_in_own], [slot(w_in_own)])
    mod = mod + first[4][0:1, 0:1]

    blocks = lambda t: t.reshape(RNN_BLOCKS, 1, 128)
    rnn_params = (conv_w8, blocks(conv_b), w_a[0], blocks(b_a), w_x[0], blocks(b_x), blocks(lam))

    h = _norm(xs, mod, b_mod, g_norm)
    ids = lambda ks: jnp.bitwise_xor(me, jnp.array(ks, jnp.int32)).astype(jnp.int32)
    pf = _proj("proj_own", h, first[2][0][None], jnp.zeros((1,), jnp.int32), ids([0]), cosf, sinf, None)
    _, (w_in_near,) = _split_wait("gather_w_in_wait", _own_block_copies, first, pf)
    second = _split_start("forward_w_in_start", _forward_copies, 3, [w_in_near],
                          [lax.empty(w_in_near.shape, w_in_near.dtype)])
    near = ids([1, 2, 4, 6])
    pf = _proj("proj_near", h, second[2][0], near, near, cosf, sinf, pf)
    (w_in_near,), (w_in_far,) = _split_wait("forward_w_in_wait", _forward_copies, second, pf)
    far = ids([3, 5, 7])
    pf = _proj("proj_far", h, w_in_far, far, far, cosf, sinf, pf)
    late = [w_out_rnn[0].astype(BF16), w_out_attn[0].astype(BF16), w_o[0].astype(BF16)]
    pf, late = lax.optimization_barrier((pf, late))
    flight = _split_start("gather_out_weights_start", _peer_copies, 7 * len(late), late, [slot(t) for t in late])
    rnn_params = (rnn_params[0], rnn_params[1] + flight[4][0:1, 0:1]) + rnn_params[2:]
    hr = _rnn_fwd(pf, keep, *rnn_params)
    o, lses = _attn_fwd(pf)

    w_or_all, w_oa_all, w_o_all = (t.reshape(D_MODEL, D_MODEL) for t in _split_wait(
        "gather_out_weights_wait", _peer_copies, flight, o)[1])
    (dx2, dhr, dz_rnn, d_o, dz_attn, dg_r, dg_a, u_rnn, dy_rnn, u_attn, dy_attn, merged, dmo,
     gp_g_final, gp_b_gate, dgate, loss_part) = _hub(
        xs, tgt, hr, pf, o, mod, b_mod, b_gate, g_final.reshape(1, D_MODEL), w_or_all, w_oa_all, w_o_all)
    gp_out, gp_out_low = _pair_grads("out_grads", [u_rnn, u_attn, merged], [dy_rnn, dy_attn, dmo])
    dq, dk, dv = _attn_bwd(pf, d_o, o, lses, cosf, sinf)
    dx_rnn, gp_conv_w, gp_conv_b, gp_w_a, gp_b_a, gp_w_x, gp_b_x, gp_lam = _rnn_bwd(pf, hr, dhr, keep, *rnn_params)
    pieces = [dx_rnn, dz_rnn, dq, dk, dv, dz_attn, dg_r, dg_a]
    gp_w_in, gp_w_in_low = _pair_grads("w_in_grad", [h], pieces)

    by_target = lambda t: [(t.reshape(3, N_DEV, 128, D_MODEL), i) for i in range(3)]
    stacks = [(gp_w_in, None)] + by_target(gp_out)
    from_sib = _rs_to_sibling("rs_sibling", [(gp_w_in_low, None)] + by_target(gp_out_low))
    targets = jnp.bitwise_xor(me, 2 * jnp.arange(4, dtype=jnp.int32)).astype(jnp.int32)
    sums = [_add_sibling("rs_add_sibling_%d" % a, s_, r_, targets) for a, (s_, r_) in enumerate(zip(stacks, from_sib))]
    sends = [send for _, send in sums]
    reduce_flight = _split_start("rs_chips_start", _chip_copies, 3 * len(sends), sends,
                                 [lax.empty(t.shape, t.dtype) for t in sends])

    mod_after = mod + reduce_flight[4][0:1, 0:1]
    grad_x, dshift, dscale, gp_g_norm = _dh_dx(pieces, w_in_near, w_in_far, xs, dx2, mod_after, b_mod, g_norm)

    flat = lambda t: t.reshape(-1, 128)
    dmod = flat(jnp.concatenate([dshift, dscale, dgate], axis=1))
    dmod_placed = lax.dynamic_update_slice(jnp.zeros((N_DEV * dmod.shape[0], 128), F32), dmod, (me * dmod.shape[0], 0))
    small = [flat(gp_g_norm), flat(gp_b_gate), flat(gp_conv_b), flat(gp_b_a), flat(gp_b_x), flat(gp_lam),
             flat(gp_g_final), flat(gp_conv_w), jnp.broadcast_to(loss_part, (SUBLANES, 128)),
             flat(gp_w_a), flat(gp_w_x), dmod_placed]
    sizes = [t.shape[0] for t in small]
    small.append(jnp.zeros((-sum(sizes) % (2 * SUBLANES), 128), F32))
    total = _allreduce_small("allreduce_small_grads", jnp.concatenate(small, axis=0))
    offs = [sum(sizes[:i]) for i in range(len(sizes))]
    (g_g_norm, g_b_gate, g_conv_b, g_b_a, g_b_x, g_lam, g_g_final, g_conv_w_all, loss_rows, g_w_a, g_w_x,
     dmod_rows) = (total[o_:o_ + s_] for o_, s_ in zip(offs, sizes))
    loss = loss_rows[0, 0]
    g_conv_w = lax.dynamic_index_in_dim(g_conv_w_all.reshape(RNN_BLOCKS, SUBLANES, 128), me, axis=0,
                                        keepdims=False)[:4]

    dmod_all = dmod_rows.reshape(N_DEV, 3 * D_MODEL)
    dmod_cols = lax.dynamic_slice_in_dim(dmod_all, me * mod_cols, mod_cols, axis=1)
    g_b_mod, g_w_mod = _mod_bwd(c_all, dmod_all, dmod_cols)

    _, from_chips = _split_wait("rs_chips_wait", _chip_copies, reduce_flight, total)

    results = {}
    sharded = (("w_in", w_in, m_w_in, v_w_in, (D_MODEL, D_MODEL)),
               ("w_out_rnn", w_out_rnn, m_w_out_rnn, v_w_out_rnn, (128, D_MODEL)),
               ("w_out_attn", w_out_attn, m_w_out_attn, v_w_out_attn, (128, D_MODEL)),
               ("w_o", w_o, m_w_o, v_w_o, (128, D_MODEL)))
    for (name, w_, m_, v_, shape2), (own, _), arrived in zip(sharded, sums, from_chips):
        d_, nm_, nv_, g_ = _adamw("adamw_" + name, w_.reshape(shape2), own, m_.reshape(shape2), v_.reshape(shape2),
                                  arrived)
        results[name] = (g_, d_, nm_, nv_)
    shape2 = (D_MODEL, mod_cols)
    results["w_mod"] = (g_w_mod,) + tuple(_adamw("adamw_w_mod", w_mod.reshape(shape2), g_w_mod,
                                                 m_w_mod.reshape(shape2), v_w_mod.reshape(shape2)))
    lanes = (("g_norm", g_norm, g_g_norm, m_g_norm, v_g_norm), ("b_mod", b_mod, g_b_mod, m_b_mod, v_b_mod),
             ("b_gate", b_gate, g_b_gate, m_b_gate, v_b_gate), ("conv_w", conv_w, g_conv_w, m_conv_w, v_conv_w),
             ("conv_b", conv_b, g_conv_b, m_conv_b, v_conv_b), ("w_a", w_a, g_w_a, m_w_a, v_w_a),
             ("b_a", b_a, g_b_a, m_b_a, v_b_a), ("w_x", w_x, g_w_x, m_w_x, v_w_x), ("b_x", b_x, g_b_x, m_b_x, v_b_x),
             ("lam", lam, g_lam, m_lam, v_lam), ("g_final", g_final, g_g_final, m_g_final, v_g_final))
    for name, w_, g_, m_, v_ in lanes:
        rows128 = lambda t: t.reshape(-1, 128)
        results[name] = (g_,) + tuple(_adamw("adamw_" + name, rows128(w_), rows128(g_), rows128(m_), rows128(v_)))
    order = ("g_norm", "w_mod", "b_mod", "w_in", "b_gate", "conv_w", "conv_b", "w_a", "b_a", "w_x", "b_x", "lam",
             "w_out_rnn", "w_out_attn", "w_o", "g_final")
    given = dict(g_norm=g_norm, w_mod=w_mod, b_mod=b_mod, w_in=w_in, b_gate=b_gate, conv_w=conv_w, conv_b=conv_b,
                 w_a=w_a, b_a=b_a, w_x=w_x, b_x=b_x, lam=lam, w_out_rnn=w_out_rnn, w_out_attn=w_out_attn, w_o=w_o,
                 g_final=g_final)
    outs = [[results[name][k].reshape(given[name].shape) for name in order] for k in range(4)]
    return (loss, grad_x[None], *outs[0], *outs[1], *outs[2], *outs[3])
```

```python
import jax
import jax.numpy as jnp
from jax import lax
from jax.experimental import pallas as pl
from jax.experimental.pallas import tpu as pltpu

F32 = jnp.float32
BF16 = jnp.bfloat16
MESH = pl.DeviceIdType.MESH

D_MODEL = 1024
N_HEADS = 8
HEAD_DIM = 128
RNN_BLOCKS = 8
N_DEV = 8
ROT_HALF = 16
ROPE_THETA = 500000.0
DILATIONS = (1, 4, 16)
KEY_BLOCK = 128
SPAN = KEY_BLOCK * DILATIONS[-1]
ATTN_SCALE = HEAD_DIM ** -0.5
NORM_EPS = 1e-6
LRU_C = 8.0
NEG_INF = -1e30
ADAM_LR, ADAM_B1, ADAM_B2, ADAM_EPS, ADAM_WD, ADAM_STEP = 0.001, 0.9, 0.999, 1e-08, 0.01, 10

SUBLANES = 8
VMEM_LIMIT = 56 * 1024 * 1024
PROJ_ROWS = 1024
RNN_ROWS = 2048
HUB_ROWS = 256
DX_ROWS = 512
WGRAD_ROWS = 1024
ADD_ROWS = 256


def _params(sem=None, vmem=None):
    return pltpu.CompilerParams(dimension_semantics=sem, vmem_limit_bytes=vmem)


def _dot(a, b):
    return jnp.dot(a, b, preferred_element_type=F32)


def _dot_nt(a, b):
    return lax.dot_general(a, b, (((1,), (1,)), ((), ())), preferred_element_type=F32)


def _dot_tn(a, b):
    return lax.dot_general(a, b, (((0,), (0,)), ((), ())), preferred_element_type=F32)


def _sigmoid(z):
    return 1.0 / (1.0 + jnp.exp(-z))


def _expm1_nonpos(z, exp_z):
    return jnp.where(z > -0.01, z * (1.0 + 0.5 * z), exp_z - 1.0)


def _my_pos():
    return lax.axis_index("x"), lax.axis_index("y"), lax.axis_index("c")


def _flip(pos, k):
    x, y, c = pos
    return ((1 - x) if k & 4 else x, (1 - y) if k & 2 else y, (1 - c) if k & 1 else c)


def _index(pos):
    return 4 * pos[0] + 2 * pos[1] + pos[2]


def _ag_small(name, v):
    rows, cols = v.shape

    def body(v_ref, out_ref, send_sems, recv_sems):
        me = _my_pos()
        out_ref[_index(me)] = v_ref[...]
        sends = []
        for k in range(1, N_DEV):
            cp = pltpu.make_async_remote_copy(
                src_ref=v_ref, dst_ref=out_ref.at[_index(me)], send_sem=send_sems.at[k - 1],
                recv_sem=recv_sems.at[k - 1], device_id=_flip(me, k), device_id_type=MESH)
            cp.start()
            sends.append(cp)
        for k in range(1, N_DEV):
            peer = _flip(me, k)
            pltpu.make_async_remote_copy(
                src_ref=v_ref, dst_ref=out_ref.at[_index(peer)], send_sem=send_sems.at[k - 1],
                recv_sem=recv_sems.at[k - 1], device_id=peer, device_id_type=MESH).wait_recv()
        for cp in sends:
            cp.wait_send()

    return pl.pallas_call(
        body, name=name,
        out_shape=jax.ShapeDtypeStruct((N_DEV, rows, cols), v.dtype),
        in_specs=[pl.BlockSpec(memory_space=pltpu.VMEM)],
        out_specs=pl.BlockSpec(memory_space=pltpu.VMEM),
        scratch_shapes=[pltpu.SemaphoreType.DMA((N_DEV - 1,)), pltpu.SemaphoreType.DMA((N_DEV - 1,))],
        compiler_params=_params(None, VMEM_LIMIT),
    )(v)


def _split_start(name, make_copies, nsem, srcs, lands):
    n, k = len(srcs), len(lands)

    def body(*refs):
        for cp in make_copies(refs[:n], refs[n:n + k], refs[n + k], refs[n + k + 1]):
            cp.start()
        refs[-1][...] = jnp.zeros_like(refs[-1])

    hbm = pl.BlockSpec(memory_space=pltpu.HBM)
    sem = pl.BlockSpec(memory_space=pltpu.SEMAPHORE)
    arrays = [*srcs, *lands]
    outs = pl.pallas_call(
        body, name=name,
        out_shape=(pltpu.SemaphoreType.DMA((nsem,)), pltpu.SemaphoreType.DMA((nsem,)),
                   *[pltpu.HBM(t.shape, t.dtype) for t in arrays], jax.ShapeDtypeStruct((SUBLANES, 128), F32)),
        in_specs=[hbm] * (n + k),
        out_specs=(sem, sem, *[hbm] * (n + k), pl.BlockSpec(memory_space=pltpu.VMEM)),
        input_output_aliases={i: 2 + i for i in range(n + k)},
        compiler_params=pltpu.CompilerParams(has_side_effects=pltpu.SideEffectType.DATAFLOW_SIDE_EFFECTING),
    )(*[pltpu.with_memory_space_constraint(t, pltpu.HBM) for t in arrays])
    return outs[0], outs[1], outs[2:2 + n], outs[2 + n:2 + n + k], outs[-1]


def _split_wait(name, make_copies, flight, after):
    send_sems, recv_sems, srcs, lands, _ = flight
    n, k = len(srcs), len(lands)

    def body(*refs):
        for cp in make_copies(refs[:n], refs[n:n + k], refs[n + k], refs[n + k + 1]):
            cp.wait_send()
            cp.wait_recv()

    hbm = pl.BlockSpec(memory_space=pltpu.HBM)
    sem = pl.BlockSpec(memory_space=pltpu.SEMAPHORE)
    arrays = [*srcs, *lands]
    outs = pl.pallas_call(
        body, name=name, out_shape=tuple(pltpu.HBM(t.shape, t.dtype) for t in arrays),
        in_specs=[hbm] * (n + k) + [sem, sem, pl.BlockSpec(memory_space=pl.ANY)],
        out_specs=[hbm] * (n + k),
        input_output_aliases={i: i for i in range(n + k)},
        compiler_params=pltpu.CompilerParams(has_side_effects=pltpu.SideEffectType.DATAFLOW_SIDE_EFFECTING),
    )(*arrays, send_sems, recv_sems, after)
    return outs[:n], outs[n:]


def _remote(src, dst, send_sems, recv_sems, k, to):
    return pltpu.make_async_remote_copy(src_ref=src, dst_ref=dst, send_sem=send_sems.at[k], recv_sem=recv_sems.at[k],
                                        device_id=to, device_id_type=MESH)


def _peer_copies(shards, lands, send_sems, recv_sems):
    me = _my_pos()
    return [_remote(shards[a], lands[a].at[_index(me)], send_sems, recv_sems, a * 7 + k - 1, _flip(me, k))
            for a in range(len(shards)) for k in range(1, N_DEV)]


def _own_block_copies(shards, lands, send_sems, recv_sems):
    me = _my_pos()
    return [_remote(shards[0], lands[0].at[_index(me)], send_sems, recv_sems, i, _flip(me, k))
            for i, k in enumerate((1, 2, 4, 6))]


def _forward_copies(arrived, lands, send_sems, recv_sems):
    me = _my_pos()
    return [_remote(arrived[0].at[_index(_flip(me, 2 * m))], lands[0].at[_index(_flip(me, 2 * m))],
                    send_sems, recv_sems, m - 1, _flip(me, 1)) for m in range(1, 4)]


def _rs_to_sibling(name, stacks):
    n = len(stacks)

    def body(*refs):
        ins, outs = refs[:n], refs[n:2 * n]
        send_sems, recv_sems = refs[2 * n:]
        me = _my_pos()
        sib = _flip(me, 1)
        sends = []
        for a, (_, which) in enumerate(stacks):
            by_target = ins[a] if which is None else ins[a].at[which]
            for m in range(4):
                target = _flip(sib, 2 * m)
                cp = pltpu.make_async_remote_copy(
                    src_ref=by_target.at[_index(target)], dst_ref=outs[a].at[m],
                    send_sem=send_sems.at[a * 4 + m], recv_sem=recv_sems.at[a * 4 + m],
                    device_id=sib, device_id_type=MESH)
                cp.start()
                sends.append(cp)
        for cp in sends:
            cp.wait_recv()
        for cp in sends:
            cp.wait_send()

    any_spec = pl.BlockSpec(memory_space=pl.ANY)
    return pl.pallas_call(
        body, name=name,
        out_shape=[jax.ShapeDtypeStruct((4,) + s.shape[-2:], s.dtype) for s, _ in stacks],
        in_specs=[any_spec] * n, out_specs=[any_spec] * n,
        scratch_shapes=[pltpu.SemaphoreType.DMA((4 * n,)), pltpu.SemaphoreType.DMA((4 * n,))],
    )(*[s for s, _ in stacks])


def _chip_copies(srcs, lands, send_sems, recv_sems):
    me = _my_pos()
    return [_remote(srcs[a].at[m - 1], lands[a].at[m - 1], send_sems, recv_sems, a * 3 + m - 1, _flip(me, 2 * m))
            for a in range(len(srcs)) for m in range(1, 4)]


def _add_sibling(name, stack, recv, targets):
    stack, which = stack
    rows, cols = stack.shape[-2:]
    tr = min(rows, ADD_ROWS)

    def by_target(index):
        if which is None:
            return pl.BlockSpec((None, tr, cols), lambda *g: (index(*g), g[-2], 0))
        return pl.BlockSpec((None, None, tr, cols), lambda *g: (which, index(*g), g[-2], 0))

    def own_body(t_ref, a_ref, b_ref, o_ref):
        o_ref[...] = a_ref[...] + b_ref[...].astype(F32)

    own = pl.pallas_call(
        own_body, name=name + "_own",
        out_shape=jax.ShapeDtypeStruct((rows, cols), F32),
        grid_spec=pltpu.PrefetchScalarGridSpec(
            num_scalar_prefetch=1, grid=(rows // tr,),
            in_specs=[by_target(lambda i, t: t[0]),
                      pl.BlockSpec((None, tr, cols), lambda i, t: (0, i, 0))],
            out_specs=pl.BlockSpec((tr, cols), lambda i, t: (i, 0))),
        compiler_params=_params(("arbitrary",)),
    )(targets, stack, recv)

    def send_body(t_ref, a_ref, b_ref, o_ref):
        o_ref[...] = (a_ref[...] + b_ref[...].astype(F32)).astype(BF16)

    send = pl.pallas_call(
        send_body, name=name + "_send",
        out_shape=jax.ShapeDtypeStruct((3, rows, cols), BF16),
        grid_spec=pltpu.PrefetchScalarGridSpec(
            num_scalar_prefetch=1, grid=(3, rows // tr),
            in_specs=[by_target(lambda m, i, t: t[m + 1]),
                      pl.BlockSpec((None, tr, cols), lambda m, i, t: (m + 1, i, 0))],
            out_specs=pl.BlockSpec((None, tr, cols), lambda m, i, t: (m, i, 0))),
        compiler_params=_params(("arbitrary", "arbitrary")),
    )(targets, stack, recv)
    return own, send


def _allreduce_small(name, v):
    rows, cols = v.shape
    half = rows // 2
    assert rows % (2 * SUBLANES) == 0

    def body(v_ref, out_ref, from_sib, chip_half, from_chips, send_sems, recv_sems):
        me = _my_pos()
        sib = _flip(me, 1)
        mine = pl.ds(pl.multiple_of(me[2] * half, SUBLANES), half)
        theirs = pl.ds(pl.multiple_of((1 - me[2]) * half, SUBLANES), half)

        def copy(k, src, dst, to):
            return pltpu.make_async_remote_copy(src_ref=src, dst_ref=dst, send_sem=send_sems.at[k],
                                                recv_sem=recv_sems.at[k], device_id=to, device_id_type=MESH)

        to_sib = copy(0, v_ref.at[theirs], from_sib, sib)
        to_sib.start()
        to_sib.wait_recv()
        chip_half[...] = v_ref[mine, :] + from_sib[...]
        to_chips = [copy(m, chip_half, from_chips.at[m - 1], _flip(me, 2 * m)) for m in range(1, 4)]
        for cp in to_chips:
            cp.start()
        for cp in to_chips:
            cp.wait_recv()
        my_chip = 2 * me[0] + me[1]
        total = None
        for chip in range(4):
            slot = jnp.maximum(jnp.bitwise_xor(chip, my_chip) - 1, 0)
            part = jnp.where(chip == my_chip, chip_half[...], from_chips[slot])
            total = part if total is None else total + part
        out_ref[mine, :] = total
        swap = copy(4, out_ref.at[mine], out_ref.at[mine], sib)
        swap.start()
        copy(4, out_ref.at[theirs], out_ref.at[theirs], sib).wait_recv()
        for cp in [to_sib, swap] + to_chips:
            cp.wait_send()

    return pl.pallas_call(
        body, name=name, out_shape=jax.ShapeDtypeStruct((rows, cols), F32),
        in_specs=[pl.BlockSpec(memory_space=pltpu.VMEM)],
        out_specs=pl.BlockSpec(memory_space=pltpu.VMEM),
        scratch_shapes=[pltpu.VMEM((half, cols), F32), pltpu.VMEM((half, cols), F32),
                        pltpu.VMEM((3, half, cols), F32),
                        pltpu.SemaphoreType.DMA((5,)), pltpu.SemaphoreType.DMA((5,))],
        compiler_params=_params(None, VMEM_LIMIT),
    )(v)


def _mod_fwd(c_all, w_mod):
    def body(c_ref, w_ref, o_ref):
        c = c_ref[...]
        o_ref[...] = jnp.dot(c * _sigmoid(c), w_ref[...], preferred_element_type=F32,
                             precision=lax.Precision.HIGHEST)

    return pl.pallas_call(
        body, name="mod_fwd", out_shape=jax.ShapeDtypeStruct((N_DEV, w_mod.shape[1]), F32),
    )(c_all, w_mod)


def _mod_bwd(c_all, dmod_all, dmod_cols):
    def body(c_ref, da_ref, dc_ref, gb_ref, gw_ref):
        c = c_ref[...]
        acc = da_ref[0:1, :]
        for b in range(1, N_DEV):
            acc = acc + da_ref[b:b + 1, :]
        gb_ref[...] = acc
        gw_ref[...] = lax.dot_general(c * _sigmoid(c), dc_ref[...], (((0,), (0,)), ((), ())),
                                      preferred_element_type=F32, precision=lax.Precision.HIGHEST)

    return pl.pallas_call(
        body, name="mod_bwd",
        out_shape=[jax.ShapeDtypeStruct((1, dmod_all.shape[1]), F32),
                   jax.ShapeDtypeStruct((c_all.shape[1], dmod_cols.shape[1]), F32)],
    )(c_all, dmod_all, dmod_cols)


def _rope_partner(t):
    lane = lax.broadcasted_iota(jnp.int32, t.shape, 1)
    return jnp.where(lane < ROT_HALF, pltpu.roll(t, HEAD_DIM - ROT_HALF, 1), pltpu.roll(t, ROT_HALF, 1))


def _norm(x, mod, b_mod, g_norm):
    seq = x.shape[0]
    tm = PROJ_ROWS

    def body(x_ref, mod_ref, bmod_ref, g_ref, h_ref):
        xf = x_ref[...]
        rstd = lax.rsqrt(jnp.mean(xf * xf, axis=-1, keepdims=True) + NORM_EPS)
        shift = mod_ref[:, 0:D_MODEL] + bmod_ref[:, 0:D_MODEL]
        scale = mod_ref[:, D_MODEL:2 * D_MODEL] + bmod_ref[:, D_MODEL:2 * D_MODEL]
        h_ref[...] = (((xf * rstd) * g_ref[...]) * (1.0 + scale) + shift).astype(BF16)

    row = pl.BlockSpec((tm, D_MODEL), lambda i: (i, 0))
    const = lambda cols: pl.BlockSpec((1, cols), lambda i: (0, 0))
    return pl.pallas_call(
        body, name="norm", out_shape=jax.ShapeDtypeStruct((seq, D_MODEL), BF16), grid=(seq // tm,),
        in_specs=[row, const(3 * D_MODEL), const(3 * D_MODEL), const(D_MODEL)], out_specs=row,
        compiler_params=_params(("arbitrary",), VMEM_LIMIT),
    )(x, mod, b_mod, g_norm)


def _proj(name, h, w, slots, pieces, cosf, sinf, prior):
    seq = h.shape[0]
    tm = PROJ_ROWS
    count = pieces.shape[0]

    def body(slots_ref, pieces_ref, h_ref, w_ref, cos_ref, sin_ref, *rest):
        out_ref = rest[-1]
        piece = pieces_ref[pl.program_id(0)]

        @pl.when((piece < 2) | (piece > 3))
        def _():
            out_ref[...] = _dot(h_ref[...], w_ref[...])

        def rotated(gain):
            for pair in range(N_HEADS // 2):
                both = _dot(h_ref[...], w_ref[:, 2 * pair * HEAD_DIM:2 * (pair + 1) * HEAD_DIM])
                for hh in (2 * pair, 2 * pair + 1):
                    t = both[:, (hh % 2) * HEAD_DIM:(hh % 2 + 1) * HEAD_DIM]
                    t = t * cos_ref[...] + _rope_partner(t) * sin_ref[...]
                    out_ref[:, hh * HEAD_DIM:(hh + 1) * HEAD_DIM] = t if gain is None else t * gain

        @pl.when(piece == 2)
        def _():
            rotated(ATTN_SCALE)

        @pl.when(piece == 3)
        def _():
            rotated(None)

    row = lambda j, i, sl, pc: (i, 0)
    in_specs = [pl.BlockSpec((tm, D_MODEL), row),
                pl.BlockSpec((None, D_MODEL, D_MODEL), lambda j, i, sl, pc: (sl[j], 0, 0)),
                pl.BlockSpec((tm, HEAD_DIM), row), pl.BlockSpec((tm, HEAD_DIM), row)]
    args = [slots, pieces, h, w, cosf, sinf]
    aliases = {}
    if prior is not None:
        in_specs.append(pl.BlockSpec(memory_space=pl.ANY))
        args.append(prior)
        aliases = {6: 0}
    return pl.pallas_call(
        body, name=name,
        out_shape=jax.ShapeDtypeStruct((seq, 8 * D_MODEL), F32),
        grid_spec=pltpu.PrefetchScalarGridSpec(
            num_scalar_prefetch=2, grid=(count, seq // tm), in_specs=in_specs,
            out_specs=pl.BlockSpec((tm, D_MODEL), lambda j, i, sl, pc: (i, pc[j]))),
        input_output_aliases=aliases,
        compiler_params=_params(("arbitrary", "arbitrary"), VMEM_LIMIT),
    )(*args)


def _shift_down(v, s, head):
    rolled = pltpu.roll(v, s, 0)
    row = lax.broadcasted_iota(jnp.int32, head.shape, 0)
    first = jnp.where(row < s, pltpu.roll(head, s, 0), rolled[:SUBLANES, :])
    return jnp.concatenate([first, rolled[SUBLANES:, :]], axis=0)


def _shift_up(v, s, tail):
    rows = v.shape[0]
    rolled = pltpu.roll(v, rows - s, 0)
    row = lax.broadcasted_iota(jnp.int32, tail.shape, 0)
    last = jnp.where(row >= SUBLANES - s, pltpu.roll(tail, SUBLANES - s, 0), rolled[rows - SUBLANES:, :])
    return jnp.concatenate([rolled[:rows - SUBLANES, :], last], axis=0)


def _doubling(a, b, period, reverse):
    rows = a.shape[0]
    pos = lax.broadcasted_iota(jnp.int32, a.shape, 0) & (period - 1)
    k = 1
    while k < period:
        inside = (pos < period - k) if reverse else (pos >= k)
        shift = rows - k if reverse else k
        a_s = jnp.where(inside, pltpu.roll(a, shift, 0), 1.0)
        b_s = jnp.where(inside, pltpu.roll(b, shift, 0), 0.0)
        b = a * b_s + b
        a = a * a_s
        k *= 2
    return a, b


def _scan(a, b, boundary, reverse, a_scr, b_scr, spread):
    rows = a.shape[0]
    ntile = rows // SUBLANES
    a_scr[...], b_scr[...] = _doubling(a, b, SUBLANES, reverse)
    ends = pl.ds(0 if reverse else SUBLANES - 1, ntile, stride=SUBLANES)
    a_end, x_end = _doubling(a_scr[ends, :], b_scr[ends, :], ntile, reverse)
    x_end = x_end + a_end * boundary
    tile = lax.broadcasted_iota(jnp.int32, x_end.shape, 0)
    if reverse:
        incoming = jnp.where(tile == ntile - 1, boundary, pltpu.roll(x_end, ntile - 1, 0))
        last = x_end[0:1, :]
    else:
        incoming = jnp.where(tile == 0, boundary, pltpu.roll(x_end, 1, 0))
        last = x_end[ntile - 1:ntile, :]
    for s in range(SUBLANES):
        spread[pl.ds(s, ntile, stride=SUBLANES), :] = incoming
    return b_scr[...] + a_scr[...] * spread[...], last


def _conv_taps(xr, head):
    return [_shift_down(xr, 3, head), _shift_down(xr, 2, head), _shift_down(xr, 1, head), xr]


def _rnn_gates(xc, wa, ba, wx, bx, lam, keep):
    xcb = xc.astype(BF16)
    r = _sigmoid(_dot(xcb, wa.astype(BF16)) + ba)
    i = _sigmoid(_dot(xcb, wx.astype(BF16)) + bx)
    softplus = jnp.maximum(-lam, 0.0) + jnp.log(1.0 + jnp.exp(-jnp.abs(lam)))
    cl = -LRU_C * softplus
    log_a = cl * r
    a_raw = jnp.exp(log_a)
    mult_raw = jnp.sqrt(-_expm1_nonpos(2.0 * log_a, a_raw * a_raw))
    live = keep > 0.0
    return r, i, cl, a_raw, mult_raw, jnp.where(live, a_raw, 0.0), jnp.where(live, mult_raw, 1.0), live


def _rnn_specs(seq, rows, time_of):
    per = rows // SUBLANES
    vec = pl.BlockSpec((None, 1, 128), lambda hb, n: (hb, 0, 0))
    mat = pl.BlockSpec((None, 128, 128), lambda hb, n: (hb, 0, 0))
    return [pl.BlockSpec((rows, 128), lambda hb, n: (time_of(n), hb)),
            pl.BlockSpec((SUBLANES, 128), lambda hb, n: (jnp.maximum(time_of(n) * per - 1, 0), hb)),
            pl.BlockSpec((rows, 1), lambda hb, n: (time_of(n), 0)),
            pl.BlockSpec((None, SUBLANES, 128), lambda hb, n: (hb, 0, 0)),
            vec, mat, vec, mat, vec, vec]


def _rnn_fwd(pf, keep, conv_w8, conv_b, w_a, b_a, w_x, b_x, lam):
    seq = pf.shape[0]
    rows = RNN_ROWS

    def body(x_ref, xh_ref, keep_ref, cw_ref, cb_ref, wa_ref, ba_ref, wx_ref, bx_ref, lam_ref,
             hr_ref, xc_ref, r_ref, i_ref, araw_ref, mraw_ref, carry, a_scr, b_scr, spread):
        n = pl.program_id(1)

        @pl.when(n == 0)
        def _():
            carry[...] = jnp.zeros_like(carry)

        xr = x_ref[...]
        head = jnp.where(n > 0, xh_ref[...], 0.0)
        taps = _conv_taps(xr, head)
        xc = cb_ref[...] + sum(cw_ref[k:k + 1, :] * taps[k] for k in range(4))
        r, i, _, a_raw, mult_raw, a, mult, _ = _rnn_gates(xc, wa_ref[...], ba_ref[...], wx_ref[...], bx_ref[...],
                                                          lam_ref[...], keep_ref[...])
        xc_ref[...], r_ref[...], i_ref[...], araw_ref[...], mraw_ref[...] = xc, r, i, a_raw, mult_raw
        h, last = _scan(a, mult * i * xc, carry[0:1, :], False, a_scr, b_scr, spread)
        hr_ref[...] = h
        carry[...] = jnp.broadcast_to(last, carry.shape)

    chunk_f32 = pltpu.VMEM((rows, 128), F32)
    chunk = pl.BlockSpec((rows, 128), lambda hb, n: (n, hb))
    shape = jax.ShapeDtypeStruct((seq, D_MODEL), F32)
    outs = pl.pallas_call(
        body, name="rnn_fwd",
        out_shape=[shape] * 6,
        grid=(RNN_BLOCKS, seq // rows),
        in_specs=_rnn_specs(seq, rows, lambda n: n),
        out_specs=[chunk] * 6,
        scratch_shapes=[pltpu.VMEM((SUBLANES, 128), F32), chunk_f32, chunk_f32, chunk_f32],
        compiler_params=_params(("arbitrary", "arbitrary"), VMEM_LIMIT),
    )(pf, pf, keep, conv_w8, conv_b, w_a, b_a, w_x, b_x, lam)
    return outs[0], tuple(outs[1:])


def _rnn_bwd(pf, hr, dhr, saved, keep, conv_w8, w_a, w_x, lam):
    seq = pf.shape[0]
    rows = RNN_ROWS
    nchunk = seq // rows
    per = rows // SUBLANES
    time_of = lambda n: nchunk - 1 - n

    def body(x_ref, keep_ref, cw_ref, wa_ref, wx_ref, lam_ref, hr_ref, hrh_ref, dhr_ref,
             xc_ref, r_ref, i_ref, araw_ref, mraw_ref,
             dx_ref, gcw_ref, gcb_ref, gwa_ref, gba_ref, gwx_ref, gbx_ref, glam_ref,
             g_carry, dxc_tail, a_scr, b_scr, spread):
        n = pl.program_id(1)
        first_in_time = n == nchunk - 1

        @pl.when(n == 0)
        def _():
            g_carry[...] = jnp.zeros_like(g_carry)
            dxc_tail[...] = jnp.zeros_like(dxc_tail)
            for ref in (gcw_ref, gcb_ref, gwa_ref, gba_ref, gwx_ref, gbx_ref, glam_ref):
                ref[...] = jnp.zeros_like(ref)

        cw, wa, wx, lam = cw_ref[...], wa_ref[...], wx_ref[...], lam_ref[...]
        xc, r, i, a_raw, mult_raw = xc_ref[...], r_ref[...], i_ref[...], araw_ref[...], mraw_ref[...]
        cl = -LRU_C * (jnp.maximum(-lam, 0.0) + jnp.log(1.0 + jnp.exp(-jnp.abs(lam))))
        live = keep_ref[...] > 0.0
        a, mult = jnp.where(live, a_raw, 0.0), jnp.where(live, mult_raw, 1.0)
        h_prev = _shift_down(hr_ref[...], 1, jnp.where(first_in_time, 0.0, hrh_ref[...]))

        row = lax.broadcasted_iota(jnp.int32, xc.shape, 0)
        last = row == rows - 1
        a_next = jnp.where(last, 0.0, pltpu.roll(a, rows - 1, 0))
        g, g_first = _scan(a_next, dhr_ref[...] + jnp.where(last, g_carry[0:1, :], 0.0),
                           jnp.zeros((1, 128), F32), True, a_scr, b_scr, spread)
        g_carry[...] = jnp.broadcast_to(a[0:1, :] * g_first, g_carry.shape)

        da = g * h_prev
        dmult = g * i * xc
        di = g * mult * xc
        dxc = g * mult * i
        dlog_a = jnp.where(live, da * a_raw - dmult * a_raw * a_raw / mult_raw, 0.0)
        dpa = (dlog_a * cl) * r * (1.0 - r)
        dpx = di * i * (1.0 - i)
        glam_ref[...] += jnp.sum(dlog_a * r, axis=0, keepdims=True) * (LRU_C * _sigmoid(-lam))
        xcb, dpab, dpxb = xc.astype(BF16), dpa.astype(BF16), dpx.astype(BF16)
        gwa_ref[...] += _dot_tn(xcb, dpab)
        gwx_ref[...] += _dot_tn(xcb, dpxb)
        gba_ref[...] += jnp.sum(dpa, axis=0, keepdims=True)
        gbx_ref[...] += jnp.sum(dpx, axis=0, keepdims=True)
        dxc = dxc + _dot_nt(dpab, wa.astype(BF16)) + _dot_nt(dpxb, wx.astype(BF16))

        gcb_ref[...] += jnp.sum(dxc, axis=0, keepdims=True)
        xr = x_ref[...]
        tail = dxc_tail[...]
        later = [_shift_up(dxc, 3 - k, tail) for k in range(3)] + [dxc]
        dx = cw[3:4, :] * dxc
        for k in range(3):
            dx = dx + cw[k:k + 1, :] * later[k]
        for k in range(4):
            gcw_ref[k:k + 1, :] += jnp.sum(xr * later[k], axis=0, keepdims=True)
        dx_ref[...] = dx.astype(BF16)
        dxc_tail[...] = dxc[0:SUBLANES, :]

    blk = lambda hb, n: (hb, 0, 0)
    chunk = pl.BlockSpec((rows, 128), lambda hb, n: (time_of(n), hb))
    vec = pl.BlockSpec((None, 1, 128), blk)
    mat = pl.BlockSpec((None, 128, 128), blk)
    vec_shape = jax.ShapeDtypeStruct((RNN_BLOCKS, 1, 128), F32)
    mat_shape = jax.ShapeDtypeStruct((RNN_BLOCKS, 128, 128), F32)
    return pl.pallas_call(
        body, name="rnn_bwd",
        out_shape=[jax.ShapeDtypeStruct((seq, D_MODEL), BF16),
                   jax.ShapeDtypeStruct((RNN_BLOCKS, SUBLANES, 128), F32), vec_shape,
                   mat_shape, vec_shape, mat_shape, vec_shape, vec_shape],
        grid=(RNN_BLOCKS, nchunk),
        in_specs=[chunk, pl.BlockSpec((rows, 1), lambda hb, n: (time_of(n), 0)),
                  pl.BlockSpec((None, SUBLANES, 128), blk), mat, mat, vec, chunk,
                  pl.BlockSpec((SUBLANES, 128), lambda hb, n: (jnp.maximum(time_of(n) * per - 1, 0), hb)), chunk]
                 + [chunk] * 5,
        out_specs=[chunk, pl.BlockSpec((None, SUBLANES, 128), blk), vec, mat, vec, mat, vec, vec],
        scratch_shapes=[pltpu.VMEM((SUBLANES, 128), F32), pltpu.VMEM((SUBLANES, 128), F32)]
                       + [pltpu.VMEM((rows, 128), F32)] * 3,
        compiler_params=_params(("arbitrary", "arbitrary"), VMEM_LIMIT),
    )(pf, keep, conv_w8, w_a, w_x, lam, hr, hr, dhr, *saved)


def _unit_rows(dil, r, j):
    start = j * KEY_BLOCK * dil + r
    return pl.ds(start, KEY_BLOCK) if dil == 1 else pl.ds(start, KEY_BLOCK, stride=dil)


def _attn_fwd(proj):
    nh, seq = N_HEADS, proj.shape[0]
    nchunk = seq // SPAN
    nblk = SPAN // KEY_BLOCK
    wide = DILATIONS[-1]

    def body(q_ref, k_ref, v_ref, kp_ref, vp_ref, o_ref, l1_ref, l4_ref, l16_ref,
             acc, m_s, l_s, q16, k16, v16, k16p, v16p, acc16, m16, l16, tmp):
        n = pl.program_id(1)
        qi = lax.broadcasted_iota(jnp.int32, (KEY_BLOCK, KEY_BLOCK), 0)
        ki = lax.broadcasted_iota(jnp.int32, (KEY_BLOCK, KEY_BLOCK), 1)
        bias_own = jnp.where(ki <= qi, 0.0, NEG_INF)
        bias_before = jnp.where(ki >= qi, 0.0, NEG_INF)
        bias_mid = jnp.concatenate([bias_before, bias_own], axis=1)
        bias_first = jnp.concatenate([jnp.where(n > 0, bias_before, NEG_INF), bias_own], axis=1)
        ones = jnp.ones((2 * KEY_BLOCK, HEAD_DIM), BF16)
        diag = qi == ki

        @pl.when(n == 0)
        def _():
            k16p[...] = jnp.zeros_like(k16p)
            v16p[...] = jnp.zeros_like(v16p)

        def unit(qf, kpb, kb, vpb, vb, bias, state, rows, first):
            acc_r, m_r, l_r = state
            kcat = jnp.concatenate([kpb, kb], axis=0)
            vaug = jnp.concatenate([jnp.concatenate([vpb, vb], axis=0), ones], axis=1)
            s = _dot_nt(qf.astype(BF16), kcat) + bias
            mx = jnp.max(s, axis=-1, keepdims=True)
            if first:
                m_new = jnp.broadcast_to(mx, (KEY_BLOCK, HEAD_DIM))
            else:
                m_old = m_r[rows, :]
                m_new = jnp.maximum(m_old, mx)
            pv = _dot(jnp.exp(s - jnp.concatenate([m_new, m_new], axis=1)).astype(BF16), vaug)
            if first:
                acc_r[rows, :] = pv[:, :HEAD_DIM]
                l_r[rows, :] = pv[:, HEAD_DIM:]
            else:
                alpha = jnp.exp(m_old - m_new)
                acc_r[rows, :] = alpha * acc_r[rows, :] + pv[:, :HEAD_DIM]
                l_r[rows, :] = alpha * l_r[rows, :] + pv[:, HEAD_DIM:]
            m_r[rows, :] = m_new

        for gi, dil in enumerate(DILATIONS[:-1]):
            nb = nblk // dil
            for r in range(dil):
                prow = _unit_rows(dil, r, nb - 1)
                kpb, vpb = kp_ref[prow, :].astype(BF16), vp_ref[prow, :].astype(BF16)
                for j in range(nb):
                    rows = _unit_rows(dil, r, j)
                    kb, vb = k_ref[rows, :].astype(BF16), v_ref[rows, :].astype(BF16)
                    unit(q_ref[rows, :], kpb, kb, vpb, vb, bias_first if j == 0 else bias_mid,
                         (acc, m_s, l_s), rows, gi == 0)
                    kpb, vpb = kb, vb

        for src, dst in ((q_ref, q16), (k_ref, k16), (v_ref, v16), (acc, acc16), (m_s, m16), (l_s, l16)):
            _to_residue_major(src, tmp, dst)
        for r in range(wide):
            rows = pl.ds(r * KEY_BLOCK, KEY_BLOCK)
            unit(q16[rows, :], k16p[rows, :].astype(BF16), k16[rows, :].astype(BF16), v16p[rows, :].astype(BF16),
                 v16[rows, :].astype(BF16), bias_first, (acc16, m16, l16), rows, False)
        k16p[...] = k16[...]
        v16p[...] = v16[...]

        den = l16[...]
        acc16[...] = acc16[...] * (1.0 / den)
        m16[...] = m16[...] + jnp.log(den)
        _from_residue_major(acc16, tmp, o_ref, False)
        _from_residue_major(m16, tmp, m_s, False)

        def lse_row(ref, rows):
            return jnp.sum(jnp.where(diag, ref[rows, :], 0.0), axis=0, keepdims=True)

        for dil, out in zip(DILATIONS[:-1], (l1_ref, l4_ref)):
            nb = nblk // dil
            for r in range(dil):
                for j in range(nb):
                    out[r * nb + j:r * nb + j + 1, :] = lse_row(m_s, _unit_rows(dil, r, j))
        for r in range(wide):
            l16_ref[r:r + 1, :] = lse_row(m16, pl.ds(r * KEY_BLOCK, KEY_BLOCK))

    cur = lambda piece: pl.BlockSpec((SPAN, HEAD_DIM), lambda h, n: (n, piece * nh + h))
    before = lambda piece: pl.BlockSpec((SPAN, HEAD_DIM), lambda h, n: (jnp.maximum(n - 1, 0), piece * nh + h))
    blk = pl.BlockSpec((None, SPAN, HEAD_DIM), lambda h, n: (h, n, 0))
    lblk = pl.BlockSpec((None, nblk, KEY_BLOCK), lambda h, n: (h, n, 0))
    lshape = jax.ShapeDtypeStruct((nh, seq // KEY_BLOCK, KEY_BLOCK), F32)
    o, l1, l4, l16 = pl.pallas_call(
        body, name="attn_fwd",
        out_shape=[jax.ShapeDtypeStruct((nh, seq, HEAD_DIM), F32), lshape, lshape, lshape],
        grid=(nh, nchunk), in_specs=[cur(2), cur(3), cur(4), before(3), before(4)],
        out_specs=[blk, lblk, lblk, lblk],
        scratch_shapes=[pltpu.VMEM((SPAN, HEAD_DIM), F32)] * 12,
        compiler_params=_params(("arbitrary", "arbitrary"), VMEM_LIMIT),
    )(proj, proj, proj, proj, proj)
    return o, (l1, l4, l16)


def _to_residue_major(src, tmp, dst):
    quarter = SPAN // 4
    for r4 in range(4):
        tmp[r4 * quarter:(r4 + 1) * quarter, :] = src[pl.ds(r4, quarter, stride=4), :]
    for r4 in range(4):
        for rp in range(4):
            r = r4 + 4 * rp
            dst[r * KEY_BLOCK:(r + 1) * KEY_BLOCK, :] = tmp[pl.ds(r4 * quarter + rp, KEY_BLOCK, stride=4), :]


def _from_residue_major(src, tmp, dst, add):
    quarter = SPAN // 4
    for r4 in range(4):
        for rp in range(4):
            r = r4 + 4 * rp
            tmp[pl.ds(r4 * quarter + rp, KEY_BLOCK, stride=4), :] = src[r * KEY_BLOCK:(r + 1) * KEY_BLOCK, :]
    for r4 in range(4):
        rows = pl.ds(r4, quarter, stride=4)
        part = tmp[r4 * quarter:(r4 + 1) * quarter, :]
        dst[rows, :] = dst[rows, :] + part if add else part


def _attn_bwd(proj, do, o, lses, cosf, sinf):
    nh, seq = N_HEADS, proj.shape[0]
    nchunk = seq // SPAN
    nblk = SPAN // KEY_BLOCK
    wide = DILATIONS[-1]
    assert SPAN == wide * KEY_BLOCK

    def body(q_ref, k_ref, v_ref, do_ref, o_ref, kp_ref, vp_ref, l1_ref, l4_ref, l16_ref,
             cos_ref, sin_ref, cosp_ref, sinp_ref, dq_ref, dk_ref, dv_ref,
             dq_acc, dkc_acc, dvc_acc, dkp_acc, dvp_acc, q16, k16, v16, do16, o16, k16p, v16p,
             dq16, dkc16, dvc16, dkp16, dvp16, tmp, pt_s, ds_s, kcat_s, qb_s, dob_s):
        n = pl.program_id(1)
        ki = lax.broadcasted_iota(jnp.int32, (KEY_BLOCK, KEY_BLOCK), 0)
        qi = lax.broadcasted_iota(jnp.int32, (KEY_BLOCK, KEY_BLOCK), 1)
        bias_own = jnp.where(ki <= qi, 0.0, NEG_INF)
        bias_before = jnp.where(ki >= qi, 0.0, NEG_INF)
        bias_mid = jnp.concatenate([bias_before, bias_own], axis=0)
        bias_first = jnp.concatenate([jnp.where(n > 0, bias_before, NEG_INF), bias_own], axis=0)
        ones8 = jnp.ones((SUBLANES, HEAD_DIM), BF16)

        def row_dot(a, b):
            prod = a * b
            hi = prod.astype(BF16)
            lo = (prod - hi.astype(F32)).astype(BF16)
            return (_dot_nt(ones8, hi) + _dot_nt(ones8, lo))[0:1, :]

        def group(units, srcs, before, l_ref, accs):
            src_q, src_do, src_o, src_k, src_v = srcs
            before_k, before_v = before
            acc_q, acc_kc, acc_vc, acc_kp, acc_vp = accs
            kb = vb = None
            for u, (rows, prow, outside, lrow, _) in enumerate(units):
                dof = src_do[rows, :]
                qb, dob = src_q[rows, :].astype(BF16), dof.astype(BF16)
                kpb, vpb = (before_k[prow, :].astype(BF16), before_v[prow, :].astype(BF16)) if outside else (kb, vb)
                kb, vb = src_k[rows, :].astype(BF16), src_v[rows, :].astype(BF16)
                kcat = jnp.concatenate([kpb, kb], axis=0)
                vcat = jnp.concatenate([vpb, vb], axis=0)
                bias = bias_first if outside else bias_mid
                pt = jnp.exp(_dot_nt(kcat, qb) + bias - l_ref[lrow:lrow + 1, :])
                dst = pt * (_dot_nt(vcat, dob) - row_dot(dof, src_o[rows, :]))
                pt_s[u], ds_s[u], kcat_s[u], qb_s[u], dob_s[u] = pt.astype(BF16), dst.astype(BF16), kcat, qb, dob
            for u, (rows, _, _, _, _) in enumerate(units):
                acc_q[rows, :] += _dot_tn(ds_s[u], kcat_s[u])
            for u, (rows, prow, outside, _, nxt) in enumerate(units):
                dk = _dot(ds_s[u, KEY_BLOCK:, :], qb_s[u])
                dv = _dot(pt_s[u, KEY_BLOCK:, :], dob_s[u])
                if nxt is not None:
                    dk = dk + _dot(ds_s[nxt, :KEY_BLOCK, :], qb_s[nxt])
                    dv = dv + _dot(pt_s[nxt, :KEY_BLOCK, :], dob_s[nxt])
                acc_kc[rows, :] += dk
                acc_vc[rows, :] += dv
                if outside:
                    acc_kp[prow, :] += _dot(ds_s[u, :KEY_BLOCK, :], qb_s[u])
                    acc_vp[prow, :] += _dot(pt_s[u, :KEY_BLOCK, :], dob_s[u])

        @pl.when(n == 0)
        def _():
            for ref in (dkp_acc, dvp_acc, dkp16, dvp16, k16p, v16p):
                ref[...] = jnp.zeros_like(ref)

        @pl.when(n < nchunk)
        def _():
            for ref in (dq_acc, dkc_acc, dvc_acc, dq16, dkc16, dvc16):
                ref[...] = jnp.zeros_like(ref)
            for src, dst in ((q_ref, q16), (k_ref, k16), (v_ref, v16), (do_ref, do16), (o_ref, o16)):
                _to_residue_major(src, tmp, dst)
            natural = (q_ref, do_ref, o_ref, k_ref, v_ref)
            for dil, l_ref in zip(DILATIONS[:-1], (l1_ref, l4_ref)):
                nb = nblk // dil
                units = [(_unit_rows(dil, r, j), _unit_rows(dil, r, (j - 1) % nb), j == 0, r * nb + j,
                          r * nb + j + 1 if j + 1 < nb else None) for r in range(dil) for j in range(nb)]
                group(units, natural, (kp_ref, vp_ref), l_ref, (dq_acc, dkc_acc, dvc_acc, dkp_acc, dvp_acc))
            blocks = [pl.ds(r * KEY_BLOCK, KEY_BLOCK) for r in range(wide)]
            group([(rows, rows, True, r, None) for r, rows in enumerate(blocks)], (q16, do16, o16, k16, v16),
                  (k16p, v16p), l16_ref, (dq16, dkc16, dvc16, dkp16, dvp16))
            _from_residue_major(dq16, tmp, dq_acc, True)
            dq = dq_acc[...]
            dq_ref[...] = ((dq * cos_ref[...] - _rope_partner(dq) * sin_ref[...]) * ATTN_SCALE).astype(BF16)

        @pl.when(n > 0)
        def _():
            _from_residue_major(dkp16, tmp, dkp_acc, True)
            _from_residue_major(dvp16, tmp, dvp_acc, True)
            dk = dkp_acc[...]
            dk_ref[...] = (dk * cosp_ref[...] - _rope_partner(dk) * sinp_ref[...]).astype(BF16)
            dv_ref[...] = dvp_acc[...].astype(BF16)

        @pl.when(n < nchunk)
        def _():
            for src, dst in ((dkc_acc, dkp_acc), (dvc_acc, dvp_acc), (dkc16, dkp16), (dvc16, dvp16),
                             (k16, k16p), (v16, v16p)):
                dst[...] = src[...]

    last = nchunk - 1
    cur = lambda h, n: (h, jnp.minimum(n, last), 0)
    prev = lambda h, n: (h, jnp.clip(n - 1, 0, last), 0)
    blk = lambda idx: pl.BlockSpec((None, SPAN, HEAD_DIM), idx)
    lblk = pl.BlockSpec((None, nblk, KEY_BLOCK), cur)
    tab = pl.BlockSpec((SPAN, HEAD_DIM), lambda h, n: (jnp.minimum(n, last), 0))
    tabp = pl.BlockSpec((SPAN, HEAD_DIM), lambda h, n: (jnp.clip(n - 1, 0, last), 0))
    out_q = pl.BlockSpec((SPAN, HEAD_DIM), lambda h, n: (jnp.minimum(n, last), h))
    out_kv = pl.BlockSpec((SPAN, HEAD_DIM), lambda h, n: (jnp.clip(n - 1, 0, last), h))
    shape = jax.ShapeDtypeStruct((seq, nh * HEAD_DIM), BF16)
    tok = lambda piece, row: pl.BlockSpec((SPAN, HEAD_DIM), lambda h, n: (row(n), piece * nh + h))
    row_cur, row_prev = (lambda n: jnp.minimum(n, last)), (lambda n: jnp.clip(n - 1, 0, last))
    return pl.pallas_call(
        body, name="attn_bwd", out_shape=[shape, shape, shape], grid=(nh, nchunk + 1),
        in_specs=[tok(2, row_cur), tok(3, row_cur), tok(4, row_cur), blk(cur), blk(cur),
                  tok(3, row_prev), tok(4, row_prev)] + [lblk] * 3 + [tab, tab, tabp, tabp],
        out_specs=[out_q, out_kv, out_kv],
        scratch_shapes=[pltpu.VMEM((SPAN, HEAD_DIM), F32)] * 18
                       + [pltpu.VMEM((nblk, 2 * KEY_BLOCK, HEAD_DIM), BF16)] * 3
                       + [pltpu.VMEM((nblk, KEY_BLOCK, HEAD_DIM), BF16)] * 2,
        compiler_params=_params(("arbitrary", "arbitrary"), VMEM_LIMIT),
    )(proj, proj, proj, do, o, proj, proj, *lses, cosf, sinf, cosf, sinf)


def _hub(x, tgt, hr, pf, o_hm, mod, b_mod, b_gate, g_final, w_out_rnn, w_out_attn, w_o):
    seq = x.shape[0]
    tm = HUB_ROWS
    nsteps = seq // tm

    def body(x_ref, t_ref, hr_ref, zr_ref, za_ref, gr_ref, ga_ref, o_ref, mod_ref, bmod_ref, bg_ref, gf_ref,
             wr_hbm, wa_hbm, wo_hbm,
             dx2_ref, dhr_ref, dzr_ref, do_ref, dza_ref, dgr_ref, dga_ref,
             ur_ref, dyr_ref, ua_ref, dya_ref, mg_ref, dmo_ref,
             ggf_ref, gbg_ref, dgate_ref, loss_ref,
             wr, wa, wo, sem):
        step = pl.program_id(0)

        @pl.when(step == 0)
        def _():
            for src, dst in ((wr_hbm, wr), (wa_hbm, wa), (wo_hbm, wo)):
                cp = pltpu.make_async_copy(src, dst, sem)
                cp.start()
                cp.wait()
            for ref in (ggf_ref, gbg_ref, dgate_ref, loss_ref):
                ref[...] = jnp.zeros_like(ref)

        gate = mod_ref[:, 2 * D_MODEL:] + bmod_ref[:, 2 * D_MODEL:]
        gfin = gf_ref[...]
        hr_t, zr, za = hr_ref[...], zr_ref[...], za_ref[...]
        o = jnp.concatenate([o_ref[hh] for hh in range(N_HEADS)], axis=1)
        sig_zr, sig_za = _sigmoid(zr), _sigmoid(za)
        silu_zr, silu_za = zr * sig_zr, za * sig_za
        u_rnn = (hr_t * silu_zr).astype(BF16)
        u_attn = (o * silu_za).astype(BF16)
        y_rnn = _dot(u_rnn, wr[...])
        y_attn = _dot(u_attn, wa[...])
        sr = _sigmoid(gr_ref[...] + bg_ref[:, :D_MODEL])
        sa = _sigmoid(ga_ref[...] + bg_ref[:, D_MODEL:])
        merged = (sr * y_rnn + sa * y_attn).astype(BF16)
        mo = _dot(merged, wo[...])
        x2 = x_ref[...] + gate * mo
        rstd = lax.rsqrt(jnp.mean(x2 * x2, axis=-1, keepdims=True) + NORM_EPS)
        xn = x2 * rstd
        err = xn * gfin - t_ref[...]
        loss_ref[...] += 0.5 * jnp.sum(jnp.sum(err * err, axis=-1, keepdims=True) * (1.0 / D_MODEL),
                                       axis=0, keepdims=True)

        dy = err * (1.0 / D_MODEL)
        ggf_ref[...] += jnp.sum(dy * xn, axis=0, keepdims=True)
        dxn = dy * gfin
        dx2 = rstd * (dxn - xn * jnp.mean(dxn * xn, axis=-1, keepdims=True))
        dx2_ref[...] = dx2
        dgate_ref[...] += jnp.sum(dx2 * mo, axis=0, keepdims=True)
        dmo = (dx2 * gate).astype(BF16)
        dmerged = _dot_nt(dmo, wo[...])
        mg_ref[...] = merged
        dmo_ref[...] = dmo
        dy_rnn = (dmerged * sr).astype(BF16)
        dy_attn = (dmerged * sa).astype(BF16)
        dg_r = dmerged * y_rnn * sr * (1.0 - sr)
        dg_a = dmerged * y_attn * sa * (1.0 - sa)
        dgr_ref[...] = dg_r.astype(BF16)
        dga_ref[...] = dg_a.astype(BF16)
        gbg_ref[:, :D_MODEL] += jnp.sum(dg_r, axis=0, keepdims=True)
        gbg_ref[:, D_MODEL:] += jnp.sum(dg_a, axis=0, keepdims=True)
        du_rnn = _dot_nt(dy_rnn, wr[...])
        du_attn = _dot_nt(dy_attn, wa[...])
        ur_ref[...] = u_rnn
        dyr_ref[...] = dy_rnn
        ua_ref[...] = u_attn
        dya_ref[...] = dy_attn
        dhr_ref[...] = du_rnn * silu_zr
        dzr_ref[...] = (du_rnn * hr_t * (sig_zr * (1.0 + zr * (1.0 - sig_zr)))).astype(BF16)
        dza_ref[...] = (du_attn * o * (sig_za * (1.0 + za * (1.0 - sig_za)))).astype(BF16)
        d_o = du_attn * silu_za
        for hh in range(N_HEADS):
            do_ref[hh] = d_o[:, hh * HEAD_DIM:(hh + 1) * HEAD_DIM]

    row = pl.BlockSpec((tm, D_MODEL), lambda i: (i, 0))
    piece = lambda slot: pl.BlockSpec((tm, D_MODEL), lambda i: (i, slot))
    hm = pl.BlockSpec((N_HEADS, tm, HEAD_DIM), lambda i: (0, i, 0))
    const = lambda cols: pl.BlockSpec((1, cols), lambda i: (0, 0))
    any_spec = pl.BlockSpec(memory_space=pl.ANY)
    act_f32 = jax.ShapeDtypeStruct((seq, D_MODEL), F32)
    act_bf16 = jax.ShapeDtypeStruct((seq, D_MODEL), BF16)
    return pl.pallas_call(
        body, name="hub",
        out_shape=[act_f32, act_f32, act_bf16, jax.ShapeDtypeStruct((N_HEADS, seq, HEAD_DIM), F32),
                   act_bf16, act_bf16, act_bf16] + [act_bf16] * 6 + [
                   jax.ShapeDtypeStruct((1, D_MODEL), F32), jax.ShapeDtypeStruct((1, 2 * D_MODEL), F32),
                   jax.ShapeDtypeStruct((1, D_MODEL), F32), jax.ShapeDtypeStruct((1, 1), F32)],
        grid=(nsteps,),
        in_specs=[row, row, row, piece(1), piece(5), piece(6), piece(7), hm,
                  const(3 * D_MODEL), const(3 * D_MODEL), const(2 * D_MODEL), const(D_MODEL),
                  any_spec, any_spec, any_spec],
        out_specs=[row, row, row, hm, row, row, row] + [row] * 6 + [
                   const(D_MODEL), const(2 * D_MODEL), const(D_MODEL), const(1)],
        scratch_shapes=[pltpu.VMEM((D_MODEL, D_MODEL), BF16)] * 3 + [pltpu.SemaphoreType.DMA],
        compiler_params=_params(("arbitrary",), VMEM_LIMIT),
    )(x, tgt, hr, pf, pf, pf, pf, o_hm, mod, b_mod, b_gate, g_final, w_out_rnn, w_out_attn, w_o)


def _pair_grads(name, lefts, rights):
    n = len(rights)
    shared = len(lefts) == 1
    seq = rights[0].shape[0]
    tk = WGRAD_ROWS
    nk = seq // tk

    def body(*refs):
        l_refs, r_refs = refs[:len(lefts)], refs[len(lefts):len(lefts) + n]
        out_ref, low_ref = refs[len(lefts) + n:]
        j, kk = pl.program_id(0), pl.program_id(1)

        @pl.when(kk == 0)
        def _():
            out_ref[...] = jnp.zeros_like(out_ref)

        for m in range(n):
            @pl.when(j == m)
            def _(m=m):
                out_ref[...] += _dot_tn(l_refs[0 if shared else m][...], r_refs[m][...])

        @pl.when(kk == nk - 1)
        def _():
            low_ref[...] = out_ref[...].astype(BF16)

    def spec(m):
        return pl.BlockSpec((tk, D_MODEL), lambda j, kk: (jnp.where(j == m, kk, jnp.where(j < m, 0, nk - 1)), 0))

    left_specs = [pl.BlockSpec((tk, D_MODEL), lambda j, kk: (kk, 0))] if shared else [spec(m) for m in range(n)]
    out_spec = pl.BlockSpec((None, D_MODEL, D_MODEL), lambda j, kk: (j, 0, 0))
    return pl.pallas_call(
        body, name=name,
        out_shape=[jax.ShapeDtypeStruct((n, D_MODEL, D_MODEL), F32), jax.ShapeDtypeStruct((n, D_MODEL, D_MODEL), BF16)],
        grid=(n, nk),
        in_specs=left_specs + [spec(m) for m in range(n)],
        out_specs=[out_spec, out_spec],
        compiler_params=_params(("arbitrary", "arbitrary"), VMEM_LIMIT),
    )(*lefts, *rights)


def _dh_dx(pieces, w_near, w_far, x, dx2, mod, b_mod, g_norm):
    seq = x.shape[0]
    tm = DX_ROWS

    def body(*refs):
        p_refs = refs[:8]
        near_hbm, far_hbm, x_ref, dx2_ref, mod_ref, bmod_ref, g_ref = refs[8:15]
        gx_ref, dshift_ref, dscale_ref, ggn_ref, w_scr, sem = refs[15:]
        step = pl.program_id(0)

        @pl.when(step == 0)
        def _():
            me = _my_pos()
            sib = _flip(me, 1)
            moves = [(near_hbm, _index(_flip(me, 2 * m))) for m in range(4)] + [(near_hbm, _index(sib))]
            moves += [(far_hbm, _index(_flip(sib, 2 * m))) for m in range(1, 4)]
            for src, t in moves:
                cp = pltpu.make_async_copy(src.at[t], w_scr.at[t], sem)
                cp.start()
                cp.wait()
            for ref in (dshift_ref, dscale_ref, ggn_ref):
                ref[...] = jnp.zeros_like(ref)

        dh = _dot_nt(p_refs[0][...], w_scr[0])
        for j in range(1, 8):
            dh = dh + _dot_nt(p_refs[j][...], w_scr[j])
        scale1 = 1.0 + mod_ref[:, D_MODEL:2 * D_MODEL] + bmod_ref[:, D_MODEL:2 * D_MODEL]
        g = g_ref[...]
        xf = x_ref[...]
        rstd_t = lax.rsqrt(jnp.mean(xf * xf, axis=-1, keepdims=True) + NORM_EPS)
        xn = xf * rstd_t
        dshift_ref[...] += jnp.sum(dh, axis=0, keepdims=True)
        dscale_ref[...] += jnp.sum(dh * (xn * g), axis=0, keepdims=True)
        ggn_ref[...] += jnp.sum(dh * scale1 * xn, axis=0, keepdims=True)
        dxn = dh * (g * scale1)
        gx_ref[...] = rstd_t * (dxn - xn * jnp.mean(dxn * xn, axis=-1, keepdims=True)) + dx2_ref[...]

    row = pl.BlockSpec((tm, D_MODEL), lambda i: (i, 0))
    const = lambda cols: pl.BlockSpec((1, cols), lambda i: (0, 0))
    vec = jax.ShapeDtypeStruct((1, D_MODEL), F32)
    return pl.pallas_call(
        body, name="dh_dx",
        out_shape=[jax.ShapeDtypeStruct((seq, D_MODEL), F32), vec, vec, vec],
        grid=(seq // tm,),
        in_specs=[row] * 8 + [pl.BlockSpec(memory_space=pl.ANY), pl.BlockSpec(memory_space=pl.ANY), row, row,
                              const(3 * D_MODEL), const(3 * D_MODEL), const(D_MODEL)],
        out_specs=[row, const(D_MODEL), const(D_MODEL), const(D_MODEL)],
        scratch_shapes=[pltpu.VMEM((8, D_MODEL, D_MODEL), BF16), pltpu.SemaphoreType.DMA],
        compiler_params=_params(("arbitrary",), VMEM_LIMIT),
    )(*pieces, w_near, w_far, x, dx2, mod, b_mod, g_norm)


def _adamw(name, w, g, m, v, recv=None):
    rows, cols = w.shape
    tr = rows if rows <= 256 else 256

    def body(*refs):
        w_ref, g_ref, m_ref, v_ref = refs[:4]
        d_ref, nm_ref, nv_ref = refs[-3:] if recv is None else refs[5:8]
        gv = g_ref[...]
        if recv is not None:
            r_ref, g_out = refs[4], refs[8]
            gv = ((gv + r_ref[0].astype(F32)) + r_ref[1].astype(F32)) + r_ref[2].astype(F32)
            g_out[...] = gv
        nm = ADAM_B1 * m_ref[...] + (1.0 - ADAM_B1) * gv
        nv = ADAM_B2 * v_ref[...] + (1.0 - ADAM_B2) * (gv * gv)
        m_hat = nm / (1.0 - ADAM_B1 ** ADAM_STEP)
        v_hat = nv / (1.0 - ADAM_B2 ** ADAM_STEP)
        d_ref[...] = -ADAM_LR * (m_hat / (jnp.sqrt(v_hat) + ADAM_EPS) + ADAM_WD * w_ref[...])
        nm_ref[...] = nm
        nv_ref[...] = nv

    spec = pl.BlockSpec((tr, cols), lambda i: (i, 0))
    shape = jax.ShapeDtypeStruct((rows, cols), F32)
    if recv is None:
        return pl.pallas_call(
            body, name=name, out_shape=[shape, shape, shape], grid=(rows // tr,),
            in_specs=[spec] * 4, out_specs=[spec] * 3,
            compiler_params=_params(("arbitrary",)),
        )(w, g, m, v)
    return pl.pallas_call(
        body, name=name, out_shape=[shape] * 4, grid=(rows // tr,),
        in_specs=[spec] * 4 + [pl.BlockSpec((3, tr, cols), lambda i: (0, i, 0))], out_specs=[spec] * 4,
        compiler_params=_params(("arbitrary",)),
    )(w, g, m, v, recv)


def kernel(x, c, positions, g_norm, w_mod, b_mod, w_in, b_gate, conv_w, conv_b, w_a, b_a, w_x, b_x, lam, w_out_rnn, w_out_attn, w_o, g_final, loss_target, m_g_norm, m_w_mod, m_b_mod, m_w_in, m_b_gate, m_conv_w, m_conv_b, m_w_a, m_b_a, m_w_x, m_b_x, m_lam, m_w_out_rnn, m_w_out_attn, m_w_o, m_g_final, v_g_norm, v_w_mod, v_b_mod, v_w_in, v_b_gate, v_conv_w, v_conv_b, v_w_a, v_b_a, v_w_x, v_b_x, v_lam, v_w_out_rnn, v_w_out_attn, v_w_o, v_g_final):
    seq = x.shape[1]
    me = _index(_my_pos())
    xs, tgt = x[0], loss_target[0]

    inv_freq = ROPE_THETA ** (-jnp.arange(0, 2 * ROT_HALF, 2, dtype=F32) / (2 * ROT_HALF))
    ang = (positions[0].astype(F32).reshape(seq // SUBLANES, SUBLANES, 1) * inv_freq).reshape(seq // SUBLANES, 128)
    cos, sin = jnp.cos(ang).reshape(seq, ROT_HALF), jnp.sin(ang).reshape(seq, ROT_HALF)
    rest = HEAD_DIM - 2 * ROT_HALF
    cosf = jnp.concatenate([cos, cos, jnp.ones((seq, rest), F32)], axis=1)
    sinf = jnp.concatenate([-sin, sin, jnp.zeros((seq, rest), F32)], axis=1)
    keep = (positions[0] != 0).astype(F32)[:, None]

    both = _ag_small("gather_c_conv_w", jnp.concatenate(
        [jnp.broadcast_to(c, (SUBLANES, D_MODEL)), jnp.pad(conv_w[0], ((0, SUBLANES - 4), (0, 0)))], axis=1))
    c_all, conv_w8 = both[:, 0, :D_MODEL], both[:, :, D_MODEL:]
    mod_cols = w_mod.shape[2]
    mod_part = _ag_small("gather_mod", _mod_fwd(c_all, w_mod[0]))
    mod = lax.dynamic_index_in_dim(mod_part, me, axis=1, keepdims=False).reshape(1, N_DEV * mod_cols)

    slot = lambda t: lax.dynamic_update_slice(lax.empty((N_DEV,) + t.shape, t.dtype), t[None], (me, 0, 0))
    w_in_own = w_in[0].astype(BF16)
    mod, w_in_own = lax.optimization_barrier((mod, w_in_own))
    first = _split_start("gather_w_in_start", _own_block_copies, 4, [w_in_own], [slot(w_in_own)])
    mod = mod + first[4][0:1, 0:1]

    blocks = lambda t: t.reshape(RNN_BLOCKS, 1, 128)
    rnn_params = (conv_w8, blocks(conv_b), w_a[0], blocks(b_a), w_x[0], blocks(b_x), blocks(lam))

    h = _norm(xs, mod, b_mod, g_norm)
    ids = lambda ks: jnp.bitwise_xor(me, jnp.array(ks, jnp.int32)).astype(jnp.int32)
    pf = _proj("proj_own", h, first[2][0][None], jnp.zeros((1,), jnp.int32), ids([0]), cosf, sinf, None)
    _, (w_in_near,) = _split_wait("gather_w_in_wait", _own_block_copies, first, pf)
    second = _split_start("forward_w_in_start", _forward_copies, 3, [w_in_near],
                          [lax.empty(w_in_near.shape, w_in_near.dtype)])
    near = ids([1, 2, 4, 6])
    pf = _proj("proj_near", h, second[2][0], near, near, cosf, sinf, pf)
    (w_in_near,), (w_in_far,) = _split_wait("forward_w_in_wait", _forward_copies, second, pf)
    far = ids([3, 5, 7])
    pf = _proj("proj_far", h, w_in_far, far, far, cosf, sinf, pf)
    late = [w_out_rnn[0].astype(BF16), w_out_attn[0].astype(BF16), w_o[0].astype(BF16)]
    pf, late = lax.optimization_barrier((pf, late))
    flight = _split_start("gather_out_weights_start", _peer_copies, 7 * len(late), late, [slot(t) for t in late])
    rnn_params = (rnn_params[0], rnn_params[1] + flight[4][0:1, 0:1]) + rnn_params[2:]
    hr, rnn_saved = _rnn_fwd(pf, keep, *rnn_params)
    o, lses = _attn_fwd(pf)

    w_or_all, w_oa_all, w_o_all = (t.reshape(D_MODEL, D_MODEL) for t in _split_wait(
        "gather_out_weights_wait", _peer_copies, flight, o)[1])
    (dx2, dhr, dz_rnn, d_o, dz_attn, dg_r, dg_a, u_rnn, dy_rnn, u_attn, dy_attn, merged, dmo,
     gp_g_final, gp_b_gate, dgate, loss_part) = _hub(
        xs, tgt, hr, pf, o, mod, b_mod, b_gate, g_final.reshape(1, D_MODEL), w_or_all, w_oa_all, w_o_all)
    gp_out, gp_out_low = _pair_grads("out_grads", [u_rnn, u_attn, merged], [dy_rnn, dy_attn, dmo])
    dq, dk, dv = _attn_bwd(pf, d_o, o, lses, cosf, sinf)
    dx_rnn, gp_conv_w, gp_conv_b, gp_w_a, gp_b_a, gp_w_x, gp_b_x, gp_lam = _rnn_bwd(
        pf, hr, dhr, rnn_saved, keep, rnn_params[0], rnn_params[2], rnn_params[4], rnn_params[6])
    pieces = [dx_rnn, dz_rnn, dq, dk, dv, dz_attn, dg_r, dg_a]
    gp_w_in, gp_w_in_low = _pair_grads("w_in_grad", [h], pieces)

    by_target = lambda t: [(t.reshape(3, N_DEV, 128, D_MODEL), i) for i in range(3)]
    stacks = [(gp_w_in, None)] + by_target(gp_out)
    from_sib = _rs_to_sibling("rs_sibling", [(gp_w_in_low, None)] + by_target(gp_out_low))
    targets = jnp.bitwise_xor(me, 2 * jnp.arange(4, dtype=jnp.int32)).astype(jnp.int32)
    sums = [_add_sibling("rs_add_sibling_%d" % a, s_, r_, targets) for a, (s_, r_) in enumerate(zip(stacks, from_sib))]
    sends = [send for _, send in sums]
    reduce_flight = _split_start("rs_chips_start", _chip_copies, 3 * len(sends), sends,
                                 [lax.empty(t.shape, t.dtype) for t in sends])

    mod_after = mod + reduce_flight[4][0:1, 0:1]
    grad_x, dshift, dscale, gp_g_norm = _dh_dx(pieces, w_in_near, w_in_far, xs, dx2, mod_after, b_mod, g_norm)

    flat = lambda t: t.reshape(-1, 128)
    dmod = flat(jnp.concatenate([dshift, dscale, dgate], axis=1))
    dmod_placed = lax.dynamic_update_slice(jnp.zeros((N_DEV * dmod.shape[0], 128), F32), dmod, (me * dmod.shape[0], 0))
    small = [flat(gp_g_norm), flat(gp_b_gate), flat(gp_conv_b), flat(gp_b_a), flat(gp_b_x), flat(gp_lam),
             flat(gp_g_final), flat(gp_conv_w), jnp.broadcast_to(loss_part, (SUBLANES, 128)),
             flat(gp_w_a), flat(gp_w_x), dmod_placed]
    sizes = [t.shape[0] for t in small]
    small.append(jnp.zeros((-sum(sizes) % (2 * SUBLANES), 128), F32))
    total = _allreduce_small("allreduce_small_grads", jnp.concatenate(small, axis=0))
    offs = [sum(sizes[:i]) for i in range(len(sizes))]
    (g_g_norm, g_b_gate, g_conv_b, g_b_a, g_b_x, g_lam, g_g_final, g_conv_w_all, loss_rows, g_w_a, g_w_x,
     dmod_rows) = (total[o_:o_ + s_] for o_, s_ in zip(offs, sizes))
    loss = loss_rows[0, 0]
    g_conv_w = lax.dynamic_index_in_dim(g_conv_w_all.reshape(RNN_BLOCKS, SUBLANES, 128), me, axis=0,
                                        keepdims=False)[:4]

    dmod_all = dmod_rows.reshape(N_DEV, 3 * D_MODEL)
    dmod_cols = lax.dynamic_slice_in_dim(dmod_all, me * mod_cols, mod_cols, axis=1)
    g_b_mod, g_w_mod = _mod_bwd(c_all, dmod_all, dmod_cols)

    _, from_chips = _split_wait("rs_chips_wait", _chip_copies, reduce_flight, total)

    results = {}
    sharded = (("w_in", w_in, m_w_in, v_w_in, (D_MODEL, D_MODEL)),
               ("w_out_rnn", w_out_rnn, m_w_out_rnn, v_w_out_rnn, (128, D_MODEL)),
               ("w_out_attn", w_out_attn, m_w_out_attn, v_w_out_attn, (128, D_MODEL)),
               ("w_o", w_o, m_w_o, v_w_o, (128, D_MODEL)))
    for (name, w_, m_, v_, shape2), (own, _), arrived in zip(sharded, sums, from_chips):
        d_, nm_, nv_, g_ = _adamw("adamw_" + name, w_.reshape(shape2), own, m_.reshape(shape2), v_.reshape(shape2),
                                  arrived)
        results[name] = (g_, d_, nm_, nv_)
    shape2 = (D_MODEL, mod_cols)
    results["w_mod"] = (g_w_mod,) + tuple(_adamw("adamw_w_mod", w_mod.reshape(shape2), g_w_mod,
                                                 m_w_mod.reshape(shape2), v_w_mod.reshape(shape2)))
    lanes = (("g_norm", g_norm, g_g_norm, m_g_norm, v_g_norm), ("b_mod", b_mod, g_b_mod, m_b_mod, v_b_mod),
             ("b_gate", b_gate, g_b_gate, m_b_gate, v_b_gate), ("conv_w", conv_w, g_conv_w, m_conv_w, v_conv_w),
             ("conv_b", conv_b, g_conv_b, m_conv_b, v_conv_b), ("w_a", w_a, g_w_a, m_w_a, v_w_a),
             ("b_a", b_a, g_b_a, m_b_a, v_b_a), ("w_x", w_x, g_w_x, m_w_x, v_w_x), ("b_x", b_x, g_b_x, m_b_x, v_b_x),
             ("lam", lam, g_lam, m_lam, v_lam), ("g_final", g_final, g_g_final, m_g_final, v_g_final))
    for name, w_, g_, m_, v_ in lanes:
        rows128 = lambda t: t.reshape(-1, 128)
        results[name] = (g_,) + tuple(_adamw("adamw_" + name, rows128(w_), rows128(g_), rows128(m_), rows128(v_)))
    order = ("g_norm", "w_mod", "b_mod", "w_in", "b_gate", "conv_w", "conv_b", "w_a", "b_a", "w_x", "b_x", "lam",
             "w_out_rnn", "w_out_attn", "w_o", "g_final")
    given = dict(g_norm=g_norm, w_mod=w_mod, b_mod=b_mod, w_in=w_in, b_gate=b_gate, conv_w=conv_w, conv_b=conv_b,
                 w_a=w_a, b_a=b_a, w_x=w_x, b_x=b_x, lam=lam, w_out_rnn=w_out_rnn, w_out_attn=w_out_attn, w_o=w_o,
                 g_final=g_final)
    outs = [[results[name][k].reshape(given[name].shape) for name in order] for k in range(4)]
    return (loss, grad_x[None], *outs[0], *outs[1], *outs[2], *outs[3])
```

```python
import jax
import jax.numpy as jnp
from jax import lax
from jax.experimental import pallas as pl
from jax.experimental.pallas import tpu as pltpu

F32 = jnp.float32
BF16 = jnp.bfloat16
MESH = pl.DeviceIdType.MESH

D_MODEL = 1024
N_HEADS = 8
HEAD_DIM = 128
RNN_BLOCKS = 8
N_DEV = 8
ROT_HALF = 16
ROPE_THETA = 500000.0
DILATIONS = (1, 4, 16)
KEY_BLOCK = 128
SPAN = KEY_BLOCK * DILATIONS[-1]
ATTN_SCALE = HEAD_DIM ** -0.5
NORM_EPS = 1e-6
LRU_C = 8.0
NEG_INF = -1e30
ADAM_LR, ADAM_B1, ADAM_B2, ADAM_EPS, ADAM_WD, ADAM_STEP = 0.001, 0.9, 0.999, 1e-08, 0.01, 10

SUBLANES = 8
VMEM_LIMIT = 56 * 1024 * 1024
PROJ_ROWS = 1024
RNN_ROWS = 2048
HUB_ROWS = 256
DX_ROWS = 512
WGRAD_ROWS = 1024
ADD_ROWS = 256


def _params(sem=None, vmem=None):
    return pltpu.CompilerParams(dimension_semantics=sem, vmem_limit_bytes=vmem)


def _dot(a, b):
    return jnp.dot(a, b, preferred_element_type=F32)


def _dot_nt(a, b):
    return lax.dot_general(a, b, (((1,), (1,)), ((), ())), preferred_element_type=F32)


def _dot_tn(a, b):
    return lax.dot_general(a, b, (((0,), (0,)), ((), ())), preferred_element_type=F32)


def _sigmoid(z):
    return 1.0 / (1.0 + jnp.exp(-z))


def _expm1_nonpos(z, exp_z):
    return jnp.where(z > -0.01, z * (1.0 + 0.5 * z), exp_z - 1.0)


def _my_pos():
    return lax.axis_index("x"), lax.axis_index("y"), lax.axis_index("c")


def _flip(pos, k):
    x, y, c = pos
    return ((1 - x) if k & 4 else x, (1 - y) if k & 2 else y, (1 - c) if k & 1 else c)


def _index(pos):
    return 4 * pos[0] + 2 * pos[1] + pos[2]


def _ag_small(name, v):
    rows, cols = v.shape

    def body(v_ref, out_ref, send_sems, recv_sems):
        me = _my_pos()
        out_ref[_index(me)] = v_ref[...]
        sends = []
        for k in range(1, N_DEV):
            cp = pltpu.make_async_remote_copy(
                src_ref=v_ref, dst_ref=out_ref.at[_index(me)], send_sem=send_sems.at[k - 1],
                recv_sem=recv_sems.at[k - 1], device_id=_flip(me, k), device_id_type=MESH)
            cp.start()
            sends.append(cp)
        for k in range(1, N_DEV):
            peer = _flip(me, k)
            pltpu.make_async_remote_copy(
                src_ref=v_ref, dst_ref=out_ref.at[_index(peer)], send_sem=send_sems.at[k - 1],
                recv_sem=recv_sems.at[k - 1], device_id=peer, device_id_type=MESH).wait_recv()
        for cp in sends:
            cp.wait_send()

    return pl.pallas_call(
        body, name=name,
        out_shape=jax.ShapeDtypeStruct((N_DEV, rows, cols), v.dtype),
        in_specs=[pl.BlockSpec(memory_space=pltpu.VMEM)],
        out_specs=pl.BlockSpec(memory_space=pltpu.VMEM),
        scratch_shapes=[pltpu.SemaphoreType.DMA((N_DEV - 1,)), pltpu.SemaphoreType.DMA((N_DEV - 1,))],
        compiler_params=_params(None, VMEM_LIMIT),
    )(v)


def _split_start(name, make_copies, nsem, srcs, lands):
    n, k = len(srcs), len(lands)

    def body(*refs):
        for cp in make_copies(refs[:n], refs[n:n + k], refs[n + k], refs[n + k + 1]):
            cp.start()
        refs[-1][...] = jnp.zeros_like(refs[-1])

    hbm = pl.BlockSpec(memory_space=pltpu.HBM)
    sem = pl.BlockSpec(memory_space=pltpu.SEMAPHORE)
    arrays = [*srcs, *lands]
    outs = pl.pallas_call(
        body, name=name,
        out_shape=(pltpu.SemaphoreType.DMA((nsem,)), pltpu.SemaphoreType.DMA((nsem,)),
                   *[pltpu.HBM(t.shape, t.dtype) for t in arrays], jax.ShapeDtypeStruct((SUBLANES, 128), F32)),
        in_specs=[hbm] * (n + k),
        out_specs=(sem, sem, *[hbm] * (n + k), pl.BlockSpec(memory_space=pltpu.VMEM)),
        input_output_aliases={i: 2 + i for i in range(n + k)},
        compiler_params=pltpu.CompilerParams(has_side_effects=pltpu.SideEffectType.DATAFLOW_SIDE_EFFECTING),
    )(*[pltpu.with_memory_space_constraint(t, pltpu.HBM) for t in arrays])
    return outs[0], outs[1], outs[2:2 + n], outs[2 + n:2 + n + k], outs[-1]


def _split_wait(name, make_copies, flight, after):
    send_sems, recv_sems, srcs, lands, _ = flight
    n, k = len(srcs), len(lands)

    def body(*refs):
        for cp in make_copies(refs[:n], refs[n:n + k], refs[n + k], refs[n + k + 1]):
            cp.wait_send()
            cp.wait_recv()

    hbm = pl.BlockSpec(memory_space=pltpu.HBM)
    sem = pl.BlockSpec(memory_space=pltpu.SEMAPHORE)
    arrays = [*srcs, *lands]
    outs = pl.pallas_call(
        body, name=name, out_shape=tuple(pltpu.HBM(t.shape, t.dtype) for t in arrays),
        in_specs=[hbm] * (n + k) + [sem, sem, pl.BlockSpec(memory_space=pl.ANY)],
        out_specs=[hbm] * (n + k),
        input_output_aliases={i: i for i in range(n + k)},
        compiler_params=pltpu.CompilerParams(has_side_effects=pltpu.SideEffectType.DATAFLOW_SIDE_EFFECTING),
    )(*arrays, send_sems, recv_sems, after)
    return outs[:n], outs[n:]


def _remote(src, dst, send_sems, recv_sems, k, to):
    return pltpu.make_async_remote_copy(src_ref=src, dst_ref=dst, send_sem=send_sems.at[k], recv_sem=recv_sems.at[k],
                                        device_id=to, device_id_type=MESH)


def _peer_copies(shards, lands, send_sems, recv_sems):
    me = _my_pos()
    return [_remote(shards[a], lands[a].at[_index(me)], send_sems, recv_sems, a * 7 + k - 1, _flip(me, k))
            for a in range(len(shards)) for k in range(1, N_DEV)]


def _own_block_copies(shards, lands, send_sems, recv_sems):
    me = _my_pos()
    return [_remote(shards[0], lands[0].at[_index(me)], send_sems, recv_sems, i, _flip(me, k))
            for i, k in enumerate((1, 2, 4, 6))]


def _forward_copies(arrived, lands, send_sems, recv_sems):
    me = _my_pos()
    return [_remote(arrived[0].at[_index(_flip(me, 2 * m))], lands[0].at[_index(_flip(me, 2 * m))],
                    send_sems, recv_sems, m - 1, _flip(me, 1)) for m in range(1, 4)]


def _rs_to_sibling(name, stacks):
    n = len(stacks)

    def body(*refs):
        ins, outs = refs[:n], refs[n:2 * n]
        send_sems, recv_sems = refs[2 * n:]
        me = _my_pos()
        sib = _flip(me, 1)
        sends = []
        for a, (_, which) in enumerate(stacks):
            by_target = ins[a] if which is None else ins[a].at[which]
            for m in range(4):
                target = _flip(sib, 2 * m)
                cp = pltpu.make_async_remote_copy(
                    src_ref=by_target.at[_index(target)], dst_ref=outs[a].at[m],
                    send_sem=send_sems.at[a * 4 + m], recv_sem=recv_sems.at[a * 4 + m],
                    device_id=sib, device_id_type=MESH)
                cp.start()
                sends.append(cp)
        for cp in sends:
            cp.wait_recv()
        for cp in sends:
            cp.wait_send()

    any_spec = pl.BlockSpec(memory_space=pl.ANY)
    return pl.pallas_call(
        body, name=name,
        out_shape=[jax.ShapeDtypeStruct((4,) + s.shape[-2:], s.dtype) for s, _ in stacks],
        in_specs=[any_spec] * n, out_specs=[any_spec] * n,
        scratch_shapes=[pltpu.SemaphoreType.DMA((4 * n,)), pltpu.SemaphoreType.DMA((4 * n,))],
    )(*[s for s, _ in stacks])


def _chip_copies(srcs, lands, send_sems, recv_sems):
    me = _my_pos()
    return [_remote(srcs[a].at[m - 1], lands[a].at[m - 1], send_sems, recv_sems, a * 3 + m - 1, _flip(me, 2 * m))
            for a in range(len(srcs)) for m in range(1, 4)]


def _add_sibling(name, stack, recv, targets):
    stack, which = stack
    rows, cols = stack.shape[-2:]
    tr = min(rows, ADD_ROWS)

    def by_target(index):
        if which is None:
            return pl.BlockSpec((None, tr, cols), lambda *g: (index(*g), g[-2], 0))
        return pl.BlockSpec((None, None, tr, cols), lambda *g: (which, index(*g), g[-2], 0))

    def own_body(t_ref, a_ref, b_ref, o_ref):
        o_ref[...] = a_ref[...] + b_ref[...].astype(F32)

    own = pl.pallas_call(
        own_body, name=name + "_own",
        out_shape=jax.ShapeDtypeStruct((rows, cols), F32),
        grid_spec=pltpu.PrefetchScalarGridSpec(
            num_scalar_prefetch=1, grid=(rows // tr,),
            in_specs=[by_target(lambda i, t: t[0]),
                      pl.BlockSpec((None, tr, cols), lambda i, t: (0, i, 0))],
            out_specs=pl.BlockSpec((tr, cols), lambda i, t: (i, 0))),
        compiler_params=_params(("arbitrary",)),
    )(targets, stack, recv)

    def send_body(t_ref, a_ref, b_ref, o_ref):
        o_ref[...] = (a_ref[...] + b_ref[...].astype(F32)).astype(BF16)

    send = pl.pallas_call(
        send_body, name=name + "_send",
        out_shape=jax.ShapeDtypeStruct((3, rows, cols), BF16),
        grid_spec=pltpu.PrefetchScalarGridSpec(
            num_scalar_prefetch=1, grid=(3, rows // tr),
            in_specs=[by_target(lambda m, i, t: t[m + 1]),
                      pl.BlockSpec((None, tr, cols), lambda m, i, t: (m + 1, i, 0))],
            out_specs=pl.BlockSpec((None, tr, cols), lambda m, i, t: (m, i, 0))),
        compiler_params=_params(("arbitrary", "arbitrary")),
    )(targets, stack, recv)
    return own, send


def _allreduce_small(name, v):
    rows, cols = v.shape
    half = rows // 2
    assert rows % (2 * SUBLANES) == 0

    def body(v_ref, out_ref, from_sib, chip_half, from_chips, send_sems, recv_sems):
        me = _my_pos()
        sib = _flip(me, 1)
        mine = pl.ds(pl.multiple_of(me[2] * half, SUBLANES), half)
        theirs = pl.ds(pl.multiple_of((1 - me[2]) * half, SUBLANES), half)

        def copy(k, src, dst, to):
            return pltpu.make_async_remote_copy(src_ref=src, dst_ref=dst, send_sem=send_sems.at[k],
                                                recv_sem=recv_sems.at[k], device_id=to, device_id_type=MESH)

        to_sib = copy(0, v_ref.at[theirs], from_sib, sib)
        to_sib.start()
        to_sib.wait_recv()
        chip_half[...] = v_ref[mine, :] + from_sib[...]
        to_chips = [copy(m, chip_half, from_chips.at[m - 1], _flip(me, 2 * m)) for m in range(1, 4)]
        for cp in to_chips:
            cp.start()
        for cp in to_chips:
            cp.wait_recv()
        my_chip = 2 * me[0] + me[1]
        total = None
        for chip in range(4):
            slot = jnp.maximum(jnp.bitwise_xor(chip, my_chip) - 1, 0)
            part = jnp.where(chip == my_chip, chip_half[...], from_chips[slot])
            total = part if total is None else total + part
        out_ref[mine, :] = total
        swap = copy(4, out_ref.at[mine], out_ref.at[mine], sib)
        swap.start()
        copy(4, out_ref.at[theirs], out_ref.at[theirs], sib).wait_recv()
        for cp in [to_sib, swap] + to_chips:
            cp.wait_send()

    return pl.pallas_call(
        body, name=name, out_shape=jax.ShapeDtypeStruct((rows, cols), F32),
        in_specs=[pl.BlockSpec(memory_space=pltpu.VMEM)],
        out_specs=pl.BlockSpec(memory_space=pltpu.VMEM),
        scratch_shapes=[pltpu.VMEM((half, cols), F32), pltpu.VMEM((half, cols), F32),
                        pltpu.VMEM((3, half, cols), F32),
                        pltpu.SemaphoreType.DMA((5,)), pltpu.SemaphoreType.DMA((5,))],
        compiler_params=_params(None, VMEM_LIMIT),
    )(v)


def _mod_fwd(c_all, w_mod):
    def body(c_ref, w_ref, o_ref):
        c = c_ref[...]
        o_ref[...] = jnp.dot(c * _sigmoid(c), w_ref[...], preferred_element_type=F32,
                             precision=lax.Precision.HIGHEST)

    return pl.pallas_call(
        body, name="mod_fwd", out_shape=jax.ShapeDtypeStruct((N_DEV, w_mod.shape[1]), F32),
    )(c_all, w_mod)


def _mod_bwd(c_all, dmod_all, dmod_cols):
    def body(c_ref, da_ref, dc_ref, gb_ref, gw_ref):
        c = c_ref[...]
        acc = da_ref[0:1, :]
        for b in range(1, N_DEV):
            acc = acc + da_ref[b:b + 1, :]
        gb_ref[...] = acc
        gw_ref[...] = lax.dot_general(c * _sigmoid(c), dc_ref[...], (((0,), (0,)), ((), ())),
                                      preferred_element_type=F32, precision=lax.Precision.HIGHEST)

    return pl.pallas_call(
        body, name="mod_bwd",
        out_shape=[jax.ShapeDtypeStruct((1, dmod_all.shape[1]), F32),
                   jax.ShapeDtypeStruct((c_all.shape[1], dmod_cols.shape[1]), F32)],
    )(c_all, dmod_all, dmod_cols)


def _rope_partner(t):
    lane = lax.broadcasted_iota(jnp.int32, t.shape, 1)
    return jnp.where(lane < ROT_HALF, pltpu.roll(t, HEAD_DIM - ROT_HALF, 1), pltpu.roll(t, ROT_HALF, 1))


def _norm(x, mod, b_mod, g_norm):
    seq = x.shape[0]
    tm = PROJ_ROWS

    def body(x_ref, mod_ref, bmod_ref, g_ref, h_ref):
        xf = x_ref[...]
        rstd = lax.rsqrt(jnp.mean(xf * xf, axis=-1, keepdims=True) + NORM_EPS)
        shift = mod_ref[:, 0:D_MODEL] + bmod_ref[:, 0:D_MODEL]
        scale = mod_ref[:, D_MODEL:2 * D_MODEL] + bmod_ref[:, D_MODEL:2 * D_MODEL]
        h_ref[...] = (((xf * rstd) * g_ref[...]) * (1.0 + scale) + shift).astype(BF16)

    row = pl.BlockSpec((tm, D_MODEL), lambda i: (i, 0))
    const = lambda cols: pl.BlockSpec((1, cols), lambda i: (0, 0))
    return pl.pallas_call(
        body, name="norm", out_shape=jax.ShapeDtypeStruct((seq, D_MODEL), BF16), grid=(seq // tm,),
        in_specs=[row, const(3 * D_MODEL), const(3 * D_MODEL), const(D_MODEL)], out_specs=row,
        compiler_params=_params(("arbitrary",), VMEM_LIMIT),
    )(x, mod, b_mod, g_norm)


def _proj(name, h, w, slots, pieces, cosf, sinf, prior):
    seq = h.shape[0]
    tm = PROJ_ROWS
    count = pieces.shape[0]

    def body(slots_ref, pieces_ref, h_ref, w_ref, cos_ref, sin_ref, *rest):
        out_ref = rest[-1]
        piece = pieces_ref[pl.program_id(0)]

        @pl.when((piece < 2) | (piece > 3))
        def _():
            out_ref[...] = _dot(h_ref[...], w_ref[...])

        def rotated(gain):
            for pair in range(N_HEADS // 2):
                both = _dot(h_ref[...], w_ref[:, 2 * pair * HEAD_DIM:2 * (pair + 1) * HEAD_DIM])
                for hh in (2 * pair, 2 * pair + 1):
                    t = both[:, (hh % 2) * HEAD_DIM:(hh % 2 + 1) * HEAD_DIM]
                    t = t * cos_ref[...] + _rope_partner(t) * sin_ref[...]
                    out_ref[:, hh * HEAD_DIM:(hh + 1) * HEAD_DIM] = t if gain is None else t * gain

        @pl.when(piece == 2)
        def _():
            rotated(ATTN_SCALE)

        @pl.when(piece == 3)
        def _():
            rotated(None)

    row = lambda j, i, sl, pc: (i, 0)
    in_specs = [pl.BlockSpec((tm, D_MODEL), row),
                pl.BlockSpec((None, D_MODEL, D_MODEL), lambda j, i, sl, pc: (sl[j], 0, 0)),
                pl.BlockSpec((tm, HEAD_DIM), row), pl.BlockSpec((tm, HEAD_DIM), row)]
    args = [slots, pieces, h, w, cosf, sinf]
    aliases = {}
    if prior is not None:
        in_specs.append(pl.BlockSpec(memory_space=pl.ANY))
        args.append(prior)
        aliases = {6: 0}
    return pl.pallas_call(
        body, name=name,
        out_shape=jax.ShapeDtypeStruct((seq, 8 * D_MODEL), F32),
        grid_spec=pltpu.PrefetchScalarGridSpec(
            num_scalar_prefetch=2, grid=(count, seq // tm), in_specs=in_specs,
            out_specs=pl.BlockSpec((tm, D_MODEL), lambda j, i, sl, pc: (i, pc[j]))),
        input_output_aliases=aliases,
        compiler_params=_params(("arbitrary", "arbitrary"), VMEM_LIMIT),
    )(*args)


def _shift_down(v, s, head):
    rolled = pltpu.roll(v, s, 0)
    row = lax.broadcasted_iota(jnp.int32, head.shape, 0)
    first = jnp.where(row < s, pltpu.roll(head, s, 0), rolled[:SUBLANES, :])
    return jnp.concatenate([first, rolled[SUBLANES:, :]], axis=0)


def _shift_up(v, s, tail):
    rows = v.shape[0]
    rolled = pltpu.roll(v, rows - s, 0)
    row = lax.broadcasted_iota(jnp.int32, tail.shape, 0)
    last = jnp.where(row >= SUBLANES - s, pltpu.roll(tail, SUBLANES - s, 0), rolled[rows - SUBLANES:, :])
    return jnp.concatenate([rolled[:rows - SUBLANES, :], last], axis=0)


def _doubling(a, b, period, reverse):
    rows = a.shape[0]
    pos = lax.broadcasted_iota(jnp.int32, a.shape, 0) & (period - 1)
    k = 1
    while k < period:
        inside = (pos < period - k) if reverse else (pos >= k)
        shift = rows - k if reverse else k
        a_s = jnp.where(inside, pltpu.roll(a, shift, 0), 1.0)
        b_s = jnp.where(inside, pltpu.roll(b, shift, 0), 0.0)
        b = a * b_s + b
        a = a * a_s
        k *= 2
    return a, b


def _scan(a, b, boundary, reverse, a_scr, b_scr, spread):
    rows = a.shape[0]
    ntile = rows // SUBLANES
    a_scr[...], b_scr[...] = _doubling(a, b, SUBLANES, reverse)
    ends = pl.ds(0 if reverse else SUBLANES - 1, ntile, stride=SUBLANES)
    a_end, x_end = _doubling(a_scr[ends, :], b_scr[ends, :], ntile, reverse)
    x_end = x_end + a_end * boundary
    tile = lax.broadcasted_iota(jnp.int32, x_end.shape, 0)
    if reverse:
        incoming = jnp.where(tile == ntile - 1, boundary, pltpu.roll(x_end, ntile - 1, 0))
        last = x_end[0:1, :]
    else:
        incoming = jnp.where(tile == 0, boundary, pltpu.roll(x_end, 1, 0))
        last = x_end[ntile - 1:ntile, :]
    for s in range(SUBLANES):
        spread[pl.ds(s, ntile, stride=SUBLANES), :] = incoming
    return b_scr[...] + a_scr[...] * spread[...], last


def _conv_taps(xr, head):
    return [_shift_down(xr, 3, head), _shift_down(xr, 2, head), _shift_down(xr, 1, head), xr]


def _rnn_gates(xc, wa, ba, wx, bx, lam, keep):
    xcb = xc.astype(BF16)
    r = _sigmoid(_dot(xcb, wa.astype(BF16)) + ba)
    i = _sigmoid(_dot(xcb, wx.astype(BF16)) + bx)
    softplus = jnp.maximum(-lam, 0.0) + jnp.log(1.0 + jnp.exp(-jnp.abs(lam)))
    cl = -LRU_C * softplus
    log_a = cl * r
    a_raw = jnp.exp(log_a)
    mult_raw = jnp.sqrt(-_expm1_nonpos(2.0 * log_a, a_raw * a_raw))
    live = keep > 0.0
    return r, i, cl, a_raw, mult_raw, jnp.where(live, a_raw, 0.0), jnp.where(live, mult_raw, 1.0), live


def _rnn_specs(seq, rows, time_of):
    per = rows // SUBLANES
    vec = pl.BlockSpec((None, 1, 128), lambda hb, n: (hb, 0, 0))
    mat = pl.BlockSpec((None, 128, 128), lambda hb, n: (hb, 0, 0))
    return [pl.BlockSpec((rows, 128), lambda hb, n: (time_of(n), hb)),
            pl.BlockSpec((SUBLANES, 128), lambda hb, n: (jnp.maximum(time_of(n) * per - 1, 0), hb)),
            pl.BlockSpec((rows, 1), lambda hb, n: (time_of(n), 0)),
            pl.BlockSpec((None, SUBLANES, 128), lambda hb, n: (hb, 0, 0)),
            vec, mat, vec, mat, vec, vec]


def _rnn_fwd(pf, keep, conv_w8, conv_b, w_a, b_a, w_x, b_x, lam):
    seq = pf.shape[0]
    rows = RNN_ROWS

    def body(x_ref, xh_ref, keep_ref, cw_ref, cb_ref, wa_ref, ba_ref, wx_ref, bx_ref, lam_ref,
             hr_ref, xc_ref, r_ref, i_ref, araw_ref, mraw_ref, carry, a_scr, b_scr, spread):
        n = pl.program_id(1)

        @pl.when(n == 0)
        def _():
            carry[...] = jnp.zeros_like(carry)

        xr = x_ref[...]
        head = jnp.where(n > 0, xh_ref[...], 0.0)
        taps = _conv_taps(xr, head)
        xc = cb_ref[...] + sum(cw_ref[k:k + 1, :] * taps[k] for k in range(4))
        r, i, _, a_raw, mult_raw, a, mult, _ = _rnn_gates(xc, wa_ref[...], ba_ref[...], wx_ref[...], bx_ref[...],
                                                          lam_ref[...], keep_ref[...])
        xc_ref[...], r_ref[...], i_ref[...], araw_ref[...], mraw_ref[...] = xc, r, i, a_raw, mult_raw
        h, last = _scan(a, mult * i * xc, carry[0:1, :], False, a_scr, b_scr, spread)
        hr_ref[...] = h
        carry[...] = jnp.broadcast_to(last, carry.shape)

    chunk_f32 = pltpu.VMEM((rows, 128), F32)
    chunk = pl.BlockSpec((rows, 128), lambda hb, n: (n, hb))
    shape = jax.ShapeDtypeStruct((seq, D_MODEL), F32)
    outs = pl.pallas_call(
        body, name="rnn_fwd",
        out_shape=[shape] * 6,
        grid=(RNN_BLOCKS, seq // rows),
        in_specs=_rnn_specs(seq, rows, lambda n: n),
        out_specs=[chunk] * 6,
        scratch_shapes=[pltpu.VMEM((SUBLANES, 128), F32), chunk_f32, chunk_f32, chunk_f32],
        compiler_params=_params(("arbitrary", "arbitrary"), VMEM_LIMIT),
    )(pf, pf, keep, conv_w8, conv_b, w_a, b_a, w_x, b_x, lam)
    return outs[0], tuple(outs[1:])


def _rnn_bwd(pf, hr, dhr, saved, keep, conv_w8, w_a, w_x, lam):
    seq = pf.shape[0]
    rows = RNN_ROWS
    nchunk = seq // rows
    per = rows // SUBLANES
    time_of = lambda n: nchunk - 1 - n

    def body(x_ref, keep_ref, cw_ref, wa_ref, wx_ref, lam_ref, hr_ref, hrh_ref, dhr_ref,
             xc_ref, r_ref, i_ref, araw_ref, mraw_ref,
             dx_ref, gcw_ref, gcb_ref, gwa_ref, gba_ref, gwx_ref, gbx_ref, glam_ref,
             g_carry, dxc_tail, a_scr, b_scr, spread):
        n = pl.program_id(1)
        first_in_time = n == nchunk - 1

        @pl.when(n == 0)
        def _():
            g_carry[...] = jnp.zeros_like(g_carry)
            dxc_tail[...] = jnp.zeros_like(dxc_tail)
            for ref in (gcw_ref, gcb_ref, gwa_ref, gba_ref, gwx_ref, gbx_ref, glam_ref):
                ref[...] = jnp.zeros_like(ref)

        cw, wa, wx, lam = cw_ref[...], wa_ref[...], wx_ref[...], lam_ref[...]
        xc, r, i, a_raw, mult_raw = xc_ref[...], r_ref[...], i_ref[...], araw_ref[...], mraw_ref[...]
        cl = -LRU_C * (jnp.maximum(-lam, 0.0) + jnp.log(1.0 + jnp.exp(-jnp.abs(lam))))
        live = keep_ref[...] > 0.0
        a, mult = jnp.where(live, a_raw, 0.0), jnp.where(live, mult_raw, 1.0)
        h_prev = _shift_down(hr_ref[...], 1, jnp.where(first_in_time, 0.0, hrh_ref[...]))

        row = lax.broadcasted_iota(jnp.int32, xc.shape, 0)
        last = row == rows - 1
        a_next = jnp.where(last, 0.0, pltpu.roll(a, rows - 1, 0))
        g, g_first = _scan(a_next, dhr_ref[...] + jnp.where(last, g_carry[0:1, :], 0.0),
                           jnp.zeros((1, 128), F32), True, a_scr, b_scr, spread)
        g_carry[...] = jnp.broadcast_to(a[0:1, :] * g_first, g_carry.shape)

        da = g * h_prev
        dmult = g * i * xc
        di = g * mult * xc
        dxc = g * mult * i
        dlog_a = jnp.where(live, da * a_raw - dmult * a_raw * a_raw / mult_raw, 0.0)
        dpa = (dlog_a * cl) * r * (1.0 - r)
        dpx = di * i * (1.0 - i)
        glam_ref[...] += jnp.sum(dlog_a * r, axis=0, keepdims=True) * (LRU_C * _sigmoid(-lam))
        xcb, dpab, dpxb = xc.astype(BF16), dpa.astype(BF16), dpx.astype(BF16)
        gwa_ref[...] += _dot_tn(xcb, dpab)
        gwx_ref[...] += _dot_tn(xcb, dpxb)
        gba_ref[...] += jnp.sum(dpa, axis=0, keepdims=True)
        gbx_ref[...] += jnp.sum(dpx, axis=0, keepdims=True)
        dxc = dxc + _dot_nt(dpab, wa.astype(BF16)) + _dot_nt(dpxb, wx.astype(BF16))

        gcb_ref[...] += jnp.sum(dxc, axis=0, keepdims=True)
        xr = x_ref[...]
        tail = dxc_tail[...]
        later = [_shift_up(dxc, 3 - k, tail) for k in range(3)] + [dxc]
        dx = cw[3:4, :] * dxc
        for k in range(3):
            dx = dx + cw[k:k + 1, :] * later[k]
        for k in range(4):
            gcw_ref[k:k + 1, :] += jnp.sum(xr * later[k], axis=0, keepdims=True)
        dx_ref[...] = dx.astype(BF16)
        dxc_tail[...] = dxc[0:SUBLANES, :]

    blk = lambda hb, n: (hb, 0, 0)
    chunk = pl.BlockSpec((rows, 128), lambda hb, n: (time_of(n), hb))
    vec = pl.BlockSpec((None, 1, 128), blk)
    mat = pl.BlockSpec((None, 128, 128), blk)
    vec_shape = jax.ShapeDtypeStruct((RNN_BLOCKS, 1, 128), F32)
    mat_shape = jax.ShapeDtypeStruct((RNN_BLOCKS, 128, 128), F32)
    return pl.pallas_call(
        body, name="rnn_bwd",
        out_shape=[jax.ShapeDtypeStruct((seq, D_MODEL), BF16),
                   jax.ShapeDtypeStruct((RNN_BLOCKS, SUBLANES, 128), F32), vec_shape,
                   mat_shape, vec_shape, mat_shape, vec_shape, vec_shape],
        grid=(RNN_BLOCKS, nchunk),
        in_specs=[chunk, pl.BlockSpec((rows, 1), lambda hb, n: (time_of(n), 0)),
                  pl.BlockSpec((None, SUBLANES, 128), blk), mat, mat, vec, chunk,
                  pl.BlockSpec((SUBLANES, 128), lambda hb, n: (jnp.maximum(time_of(n) * per - 1, 0), hb)), chunk]
                 + [chunk] * 5,
        out_specs=[chunk, pl.BlockSpec((None, SUBLANES, 128), blk), vec, mat, vec, mat, vec, vec],
        scratch_shapes=[pltpu.VMEM((SUBLANES, 128), F32), pltpu.VMEM((SUBLANES, 128), F32)]
                       + [pltpu.VMEM((rows, 128), F32)] * 3,
        compiler_params=_params(("arbitrary", "arbitrary"), VMEM_LIMIT),
    )(pf, keep, conv_w8, w_a, w_x, lam, hr, hr, dhr, *saved)


def _unit_rows(dil, r, j):
    start = j * KEY_BLOCK * dil + r
    return pl.ds(start, KEY_BLOCK) if dil == 1 else pl.ds(start, KEY_BLOCK, stride=dil)


def _attn_fwd(proj):
    nh, seq = N_HEADS, proj.shape[0]
    nchunk = seq // SPAN
    nblk = SPAN // KEY_BLOCK
    wide = DILATIONS[-1]

    def body(q_ref, k_ref, v_ref, kp_ref, vp_ref, o_ref, l1_ref, l4_ref, l16_ref, q16, k16, v16, o16,
             acc, m_s, l_s, k16p, v16p, acc16, m16, l16, tmp):
        n = pl.program_id(1)
        qi = lax.broadcasted_iota(jnp.int32, (KEY_BLOCK, KEY_BLOCK), 0)
        ki = lax.broadcasted_iota(jnp.int32, (KEY_BLOCK, KEY_BLOCK), 1)
        bias_own = jnp.where(ki <= qi, 0.0, NEG_INF)
        bias_before = jnp.where(ki >= qi, 0.0, NEG_INF)
        bias_mid = jnp.concatenate([bias_before, bias_own], axis=1)
        bias_first = jnp.concatenate([jnp.where(n > 0, bias_before, NEG_INF), bias_own], axis=1)
        ones = jnp.ones((2 * KEY_BLOCK, HEAD_DIM), BF16)
        diag = qi == ki

        @pl.when(n == 0)
        def _():
            k16p[...] = jnp.zeros_like(k16p)
            v16p[...] = jnp.zeros_like(v16p)

        def unit(qf, kpb, kb, vpb, vb, bias, state, rows, first):
            acc_r, m_r, l_r = state
            kcat = jnp.concatenate([kpb, kb], axis=0)
            vaug = jnp.concatenate([jnp.concatenate([vpb, vb], axis=0), ones], axis=1)
            s = _dot_nt(qf.astype(BF16), kcat) + bias
            mx = jnp.max(s, axis=-1, keepdims=True)
            if first:
                m_new = jnp.broadcast_to(mx, (KEY_BLOCK, HEAD_DIM))
            else:
                m_old = m_r[rows, :]
                m_new = jnp.maximum(m_old, mx)
            pv = _dot(jnp.exp(s - jnp.concatenate([m_new, m_new], axis=1)).astype(BF16), vaug)
            if first:
                acc_r[rows, :] = pv[:, :HEAD_DIM]
                l_r[rows, :] = pv[:, HEAD_DIM:]
            else:
                alpha = jnp.exp(m_old - m_new)
                acc_r[rows, :] = alpha * acc_r[rows, :] + pv[:, :HEAD_DIM]
                l_r[rows, :] = alpha * l_r[rows, :] + pv[:, HEAD_DIM:]
            m_r[rows, :] = m_new

        for gi, dil in enumerate(DILATIONS[:-1]):
            nb = nblk // dil
            for r in range(dil):
                prow = _unit_rows(dil, r, nb - 1)
                kpb, vpb = kp_ref[prow, :].astype(BF16), vp_ref[prow, :].astype(BF16)
                for j in range(nb):
                    rows = _unit_rows(dil, r, j)
                    kb, vb = k_ref[rows, :].astype(BF16), v_ref[rows, :].astype(BF16)
                    unit(q_ref[rows, :], kpb, kb, vpb, vb, bias_first if j == 0 else bias_mid,
                         (acc, m_s, l_s), rows, gi == 0)
                    kpb, vpb = kb, vb

        for src, dst in ((q_ref, q16), (k_ref, k16), (v_ref, v16), (acc, acc16), (m_s, m16), (l_s, l16)):
            _to_residue_major(src, tmp, dst)
        for r in range(wide):
            rows = pl.ds(r * KEY_BLOCK, KEY_BLOCK)
            unit(q16[rows, :], k16p[rows, :].astype(BF16), k16[rows, :].astype(BF16), v16p[rows, :].astype(BF16),
                 v16[rows, :].astype(BF16), bias_first, (acc16, m16, l16), rows, False)
        k16p[...] = k16[...]
        v16p[...] = v16[...]

        den = l16[...]
        o16[...] = acc16[...] * (1.0 / den)
        m16[...] = m16[...] + jnp.log(den)
        _from_residue_major(o16, tmp, o_ref, False)
        _from_residue_major(m16, tmp, m_s, False)

        def lse_row(ref, rows):
            return jnp.sum(jnp.where(diag, ref[rows, :], 0.0), axis=0, keepdims=True)

        for dil, out in zip(DILATIONS[:-1], (l1_ref, l4_ref)):
            nb = nblk // dil
            for r in range(dil):
                for j in range(nb):
                    out[r * nb + j:r * nb + j + 1, :] = lse_row(m_s, _unit_rows(dil, r, j))
        for r in range(wide):
            l16_ref[r:r + 1, :] = lse_row(m16, pl.ds(r * KEY_BLOCK, KEY_BLOCK))

    cur = lambda piece: pl.BlockSpec((SPAN, HEAD_DIM), lambda h, n: (n, piece * nh + h))
    before = lambda piece: pl.BlockSpec((SPAN, HEAD_DIM), lambda h, n: (jnp.maximum(n - 1, 0), piece * nh + h))
    blk = pl.BlockSpec((None, SPAN, HEAD_DIM), lambda h, n: (h, n, 0))
    lblk = pl.BlockSpec((None, nblk, KEY_BLOCK), lambda h, n: (h, n, 0))
    lshape = jax.ShapeDtypeStruct((nh, seq // KEY_BLOCK, KEY_BLOCK), F32)
    full = jax.ShapeDtypeStruct((nh, seq, HEAD_DIM), F32)
    o, l1, l4, l16, *major = pl.pallas_call(
        body, name="attn_fwd",
        out_shape=[full, lshape, lshape, lshape] + [full] * 4,
        grid=(nh, nchunk), in_specs=[cur(2), cur(3), cur(4), before(3), before(4)],
        out_specs=[blk, lblk, lblk, lblk] + [blk] * 4,
        scratch_shapes=[pltpu.VMEM((SPAN, HEAD_DIM), F32)] * 9,
        compiler_params=_params(("arbitrary", "arbitrary"), VMEM_LIMIT),
    )(proj, proj, proj, proj, proj)
    return o, (l1, l4, l16), tuple(major)


def _to_residue_major(src, tmp, dst):
    quarter = SPAN // 4
    for r4 in range(4):
        tmp[r4 * quarter:(r4 + 1) * quarter, :] = src[pl.ds(r4, quarter, stride=4), :]
    for r4 in range(4):
        for rp in range(4):
            r = r4 + 4 * rp
            dst[r * KEY_BLOCK:(r + 1) * KEY_BLOCK, :] = tmp[pl.ds(r4 * quarter + rp, KEY_BLOCK, stride=4), :]


def _from_residue_major(src, tmp, dst, add):
    quarter = SPAN // 4
    for r4 in range(4):
        for rp in range(4):
            r = r4 + 4 * rp
            tmp[pl.ds(r4 * quarter + rp, KEY_BLOCK, stride=4), :] = src[r * KEY_BLOCK:(r + 1) * KEY_BLOCK, :]
    for r4 in range(4):
        rows = pl.ds(r4, quarter, stride=4)
        part = tmp[r4 * quarter:(r4 + 1) * quarter, :]
        dst[rows, :] = dst[rows, :] + part if add else part


def _attn_bwd(proj, do, o, lses, major, cosf, sinf):
    nh, seq = N_HEADS, proj.shape[0]
    nchunk = seq // SPAN
    nblk = SPAN // KEY_BLOCK
    wide = DILATIONS[-1]
    assert SPAN == wide * KEY_BLOCK

    def body(q_ref, k_ref, v_ref, do_ref, o_ref, kp_ref, vp_ref, q16, k16, v16, o16, l1_ref, l4_ref, l16_ref,
             cos_ref, sin_ref, cosp_ref, sinp_ref, dq_ref, dk_ref, dv_ref,
             dq_acc, dkc_acc, dvc_acc, dkp_acc, dvp_acc, do16, k16p, v16p,
             dq16, dkc16, dvc16, dkp16, dvp16, tmp, pt_s, ds_s, kcat_s, qb_s, dob_s):
        n = pl.program_id(1)
        ki = lax.broadcasted_iota(jnp.int32, (KEY_BLOCK, KEY_BLOCK), 0)
        qi = lax.broadcasted_iota(jnp.int32, (KEY_BLOCK, KEY_BLOCK), 1)
        bias_own = jnp.where(ki <= qi, 0.0, NEG_INF)
        bias_before = jnp.where(ki >= qi, 0.0, NEG_INF)
        bias_mid = jnp.concatenate([bias_before, bias_own], axis=0)
        bias_first = jnp.concatenate([jnp.where(n > 0, bias_before, NEG_INF), bias_own], axis=0)
        ones8 = jnp.ones((SUBLANES, HEAD_DIM), BF16)

        def row_dot(a, b):
            prod = a * b
            hi = prod.astype(BF16)
            lo = (prod - hi.astype(F32)).astype(BF16)
            return (_dot_nt(ones8, hi) + _dot_nt(ones8, lo))[0:1, :]

        def group(units, srcs, before, l_ref, accs):
            src_q, src_do, src_o, src_k, src_v = srcs
            before_k, before_v = before
            acc_q, acc_kc, acc_vc, acc_kp, acc_vp = accs
            kb = vb = None
            for u, (rows, prow, outside, lrow, _) in enumerate(units):
                dof = src_do[rows, :]
                qb, dob = src_q[rows, :].astype(BF16), dof.astype(BF16)
                kpb, vpb = (before_k[prow, :].astype(BF16), before_v[prow, :].astype(BF16)) if outside else (kb, vb)
                kb, vb = src_k[rows, :].astype(BF16), src_v[rows, :].astype(BF16)
                kcat = jnp.concatenate([kpb, kb], axis=0)
                vcat = jnp.concatenate([vpb, vb], axis=0)
                bias = bias_first if outside else bias_mid
                pt = jnp.exp(_dot_nt(kcat, qb) + bias - l_ref[lrow:lrow + 1, :])
                dst = pt * (_dot_nt(vcat, dob) - row_dot(dof, src_o[rows, :]))
                pt_s[u], ds_s[u], kcat_s[u], qb_s[u], dob_s[u] = pt.astype(BF16), dst.astype(BF16), kcat, qb, dob
            for u, (rows, _, _, _, _) in enumerate(units):
                acc_q[rows, :] += _dot_tn(ds_s[u], kcat_s[u])
            for u, (rows, prow, outside, _, nxt) in enumerate(units):
                dk = _dot(ds_s[u, KEY_BLOCK:, :], qb_s[u])
                dv = _dot(pt_s[u, KEY_BLOCK:, :], dob_s[u])
                if nxt is not None:
                    dk = dk + _dot(ds_s[nxt, :KEY_BLOCK, :], qb_s[nxt])
                    dv = dv + _dot(pt_s[nxt, :KEY_BLOCK, :], dob_s[nxt])
                acc_kc[rows, :] += dk
                acc_vc[rows, :] += dv
                if outside:
                    acc_kp[prow, :] += _dot(ds_s[u, :KEY_BLOCK, :], qb_s[u])
                    acc_vp[prow, :] += _dot(pt_s[u, :KEY_BLOCK, :], dob_s[u])

        @pl.when(n == 0)
        def _():
            for ref in (dkp_acc, dvp_acc, dkp16, dvp16, k16p, v16p):
                ref[...] = jnp.zeros_like(ref)

        @pl.when(n < nchunk)
        def _():
            for ref in (dq_acc, dkc_acc, dvc_acc, dq16, dkc16, dvc16):
                ref[...] = jnp.zeros_like(ref)
            _to_residue_major(do_ref, tmp, do16)
            natural = (q_ref, do_ref, o_ref, k_ref, v_ref)
            for dil, l_ref in zip(DILATIONS[:-1], (l1_ref, l4_ref)):
                nb = nblk // dil
                units = [(_unit_rows(dil, r, j), _unit_rows(dil, r, (j - 1) % nb), j == 0, r * nb + j,
                          r * nb + j + 1 if j + 1 < nb else None) for r in range(dil) for j in range(nb)]
                group(units, natural, (kp_ref, vp_ref), l_ref, (dq_acc, dkc_acc, dvc_acc, dkp_acc, dvp_acc))
            blocks = [pl.ds(r * KEY_BLOCK, KEY_BLOCK) for r in range(wide)]
            group([(rows, rows, True, r, None) for r, rows in enumerate(blocks)], (q16, do16, o16, k16, v16),
                  (k16p, v16p), l16_ref, (dq16, dkc16, dvc16, dkp16, dvp16))
            _from_residue_major(dq16, tmp, dq_acc, True)
            dq = dq_acc[...]
            dq_ref[...] = ((dq * cos_ref[...] - _rope_partner(dq) * sin_ref[...]) * ATTN_SCALE).astype(BF16)

        @pl.when(n > 0)
        def _():
            _from_residue_major(dkp16, tmp, dkp_acc, True)
            _from_residue_major(dvp16, tmp, dvp_acc, True)
            dk = dkp_acc[...]
            dk_ref[...] = (dk * cosp_ref[...] - _rope_partner(dk) * sinp_ref[...]).astype(BF16)
            dv_ref[...] = dvp_acc[...].astype(BF16)

        @pl.when(n < nchunk)
        def _():
            for src, dst in ((dkc_acc, dkp_acc), (dvc_acc, dvp_acc), (dkc16, dkp16), (dvc16, dvp16),
                             (k16, k16p), (v16, v16p)):
                dst[...] = src[...]

    last = nchunk - 1
    cur = lambda h, n: (h, jnp.minimum(n, last), 0)
    prev = lambda h, n: (h, jnp.clip(n - 1, 0, last), 0)
    blk = lambda idx: pl.BlockSpec((None, SPAN, HEAD_DIM), idx)
    lblk = pl.BlockSpec((None, nblk, KEY_BLOCK), cur)
    tab = pl.BlockSpec((SPAN, HEAD_DIM), lambda h, n: (jnp.minimum(n, last), 0))
    tabp = pl.BlockSpec((SPAN, HEAD_DIM), lambda h, n: (jnp.clip(n - 1, 0, last), 0))
    out_q = pl.BlockSpec((SPAN, HEAD_DIM), lambda h, n: (jnp.minimum(n, last), h))
    out_kv = pl.BlockSpec((SPAN, HEAD_DIM), lambda h, n: (jnp.clip(n - 1, 0, last), h))
    shape = jax.ShapeDtypeStruct((seq, nh * HEAD_DIM), BF16)
    tok = lambda piece, row: pl.BlockSpec((SPAN, HEAD_DIM), lambda h, n: (row(n), piece * nh + h))
    row_cur, row_prev = (lambda n: jnp.minimum(n, last)), (lambda n: jnp.clip(n - 1, 0, last))
    return pl.pallas_call(
        body, name="attn_bwd", out_shape=[shape, shape, shape], grid=(nh, nchunk + 1),
        in_specs=[tok(2, row_cur), tok(3, row_cur), tok(4, row_cur), blk(cur), blk(cur),
                  tok(3, row_prev), tok(4, row_prev)] + [blk(cur)] * 4 + [lblk] * 3 + [tab, tab, tabp, tabp],
        out_specs=[out_q, out_kv, out_kv],
        scratch_shapes=[pltpu.VMEM((SPAN, HEAD_DIM), F32)] * 14
                       + [pltpu.VMEM((nblk, 2 * KEY_BLOCK, HEAD_DIM), BF16)] * 3
                       + [pltpu.VMEM((nblk, KEY_BLOCK, HEAD_DIM), BF16)] * 2,
        compiler_params=_params(("arbitrary", "arbitrary"), VMEM_LIMIT),
    )(proj, proj, proj, do, o, proj, proj, *major, *lses, cosf, sinf, cosf, sinf)


def _hub(x, tgt, hr, pf, o_hm, mod, b_mod, b_gate, g_final, w_out_rnn, w_out_attn, w_o):
    seq = x.shape[0]
    tm = HUB_ROWS
    nsteps = seq // tm

    def body(x_ref, t_ref, hr_ref, zr_ref, za_ref, gr_ref, ga_ref, o_ref, mod_ref, bmod_ref, bg_ref, gf_ref,
             wr_hbm, wa_hbm, wo_hbm,
             dx2_ref, dhr_ref, dzr_ref, do_ref, dza_ref, dgr_ref, dga_ref,
             ur_ref, dyr_ref, ua_ref, dya_ref, mg_ref, dmo_ref,
             ggf_ref, gbg_ref, dgate_ref, loss_ref,
             wr, wa, wo, sem):
        step = pl.program_id(0)

        @pl.when(step == 0)
        def _():
            for src, dst in ((wr_hbm, wr), (wa_hbm, wa), (wo_hbm, wo)):
                cp = pltpu.make_async_copy(src, dst, sem)
                cp.start()
                cp.wait()
            for ref in (ggf_ref, gbg_ref, dgate_ref, loss_ref):
                ref[...] = jnp.zeros_like(ref)

        gate = mod_ref[:, 2 * D_MODEL:] + bmod_ref[:, 2 * D_MODEL:]
        gfin = gf_ref[...]
        hr_t, zr, za = hr_ref[...], zr_ref[...], za_ref[...]
        o = jnp.concatenate([o_ref[hh] for hh in range(N_HEADS)], axis=1)
        sig_zr, sig_za = _sigmoid(zr), _sigmoid(za)
        silu_zr, silu_za = zr * sig_zr, za * sig_za
        u_rnn = (hr_t * silu_zr).astype(BF16)
        u_attn = (o * silu_za).astype(BF16)
        y_rnn = _dot(u_rnn, wr[...])
        y_attn = _dot(u_attn, wa[...])
        sr = _sigmoid(gr_ref[...] + bg_ref[:, :D_MODEL])
        sa = _sigmoid(ga_ref[...] + bg_ref[:, D_MODEL:])
        merged = (sr * y_rnn + sa * y_attn).astype(BF16)
        mo = _dot(merged, wo[...])
        x2 = x_ref[...] + gate * mo
        rstd = lax.rsqrt(jnp.mean(x2 * x2, axis=-1, keepdims=True) + NORM_EPS)
        xn = x2 * rstd
        err = xn * gfin - t_ref[...]
        loss_ref[...] += 0.5 * jnp.sum(jnp.sum(err * err, axis=-1, keepdims=True) * (1.0 / D_MODEL),
                                       axis=0, keepdims=True)

        dy = err * (1.0 / D_MODEL)
        ggf_ref[...] += jnp.sum(dy * xn, axis=0, keepdims=True)
        dxn = dy * gfin
        dx2 = rstd * (dxn - xn * jnp.mean(dxn * xn, axis=-1, keepdims=True))
        dx2_ref[...] = dx2
        dgate_ref[...] += jnp.sum(dx2 * mo, axis=0, keepdims=True)
        dmo = (dx2 * gate).astype(BF16)
        dmerged = _dot_nt(dmo, wo[...])
        mg_ref[...] = merged
        dmo_ref[...] = dmo
        dy_rnn = (dmerged * sr).astype(BF16)
        dy_attn = (dmerged * sa).astype(BF16)
        dg_r = dmerged * y_rnn * sr * (1.0 - sr)
        dg_a = dmerged * y_attn * sa * (1.0 - sa)
        dgr_ref[...] = dg_r.astype(BF16)
        dga_ref[...] = dg_a.astype(BF16)
        gbg_ref[:, :D_MODEL] += jnp.sum(dg_r, axis=0, keepdims=True)
        gbg_ref[:, D_MODEL:] += jnp.sum(dg_a, axis=0, keepdims=True)
        du_rnn = _dot_nt(dy_rnn, wr[...])
        du_attn = _dot_nt(dy_attn, wa[...])
        ur_ref[...] = u_rnn
        dyr_ref[...] = dy_rnn
        ua_ref[...] = u_attn
        dya_ref[...] = dy_attn
        dhr_ref[...] = du_rnn * silu_zr
        dzr_ref[...] = (du_rnn * hr_t * (sig_zr * (1.0 + zr * (1.0 - sig_zr)))).astype(BF16)
        dza_ref[...] = (du_attn * o * (sig_za * (1.0 + za * (1.0 - sig_za)))).astype(BF16)
        d_o = du_attn * silu_za
        for hh in range(N_HEADS):
            do_ref[hh] = d_o[:, hh * HEAD_DIM:(hh + 1) * HEAD_DIM]

    row = pl.BlockSpec((tm, D_MODEL), lambda i: (i, 0))
    piece = lambda slot: pl.BlockSpec((tm, D_MODEL), lambda i: (i, slot))
    hm = pl.BlockSpec((N_HEADS, tm, HEAD_DIM), lambda i: (0, i, 0))
    const = lambda cols: pl.BlockSpec((1, cols), lambda i: (0, 0))
    any_spec = pl.BlockSpec(memory_space=pl.ANY)
    act_f32 = jax.ShapeDtypeStruct((seq, D_MODEL), F32)
    act_bf16 = jax.ShapeDtypeStruct((seq, D_MODEL), BF16)
    return pl.pallas_call(
        body, name="hub",
        out_shape=[act_f32, act_f32, act_bf16, jax.ShapeDtypeStruct((N_HEADS, seq, HEAD_DIM), F32),
                   act_bf16, act_bf16, act_bf16] + [act_bf16] * 6 + [
                   jax.ShapeDtypeStruct((1, D_MODEL), F32), jax.ShapeDtypeStruct((1, 2 * D_MODEL), F32),
                   jax.ShapeDtypeStruct((1, D_MODEL), F32), jax.ShapeDtypeStruct((1, 1), F32)],
        grid=(nsteps,),
        in_specs=[row, row, row, piece(1), piece(5), piece(6), piece(7), hm,
                  const(3 * D_MODEL), const(3 * D_MODEL), const(2 * D_MODEL), const(D_MODEL),
                  any_spec, any_spec, any_spec],
        out_specs=[row, row, row, hm, row, row, row] + [row] * 6 + [
                   const(D_MODEL), const(2 * D_MODEL), const(D_MODEL), const(1)],
        scratch_shapes=[pltpu.VMEM((D_MODEL, D_MODEL), BF16)] * 3 + [pltpu.SemaphoreType.DMA],
        compiler_params=_params(("arbitrary",), VMEM_LIMIT),
    )(x, tgt, hr, pf, pf, pf, pf, o_hm, mod, b_mod, b_gate, g_final, w_out_rnn, w_out_attn, w_o)


def _pair_grads(name, lefts, rights):
    n = len(rights)
    shared = len(lefts) == 1
    seq = rights[0].shape[0]
    tk = WGRAD_ROWS
    nk = seq // tk

    def body(*refs):
        l_refs, r_refs = refs[:len(lefts)], refs[len(lefts):len(lefts) + n]
        out_ref, low_ref = refs[len(lefts) + n:]
        j, kk = pl.program_id(0), pl.program_id(1)

        @pl.when(kk == 0)
        def _():
            out_ref[...] = jnp.zeros_like(out_ref)

        for m in range(n):
            @pl.when(j == m)
            def _(m=m):
                out_ref[...] += _dot_tn(l_refs[0 if shared else m][...], r_refs[m][...])

        @pl.when(kk == nk - 1)
        def _():
            low_ref[...] = out_ref[...].astype(BF16)

    def spec(m):
        return pl.BlockSpec((tk, D_MODEL), lambda j, kk: (jnp.where(j == m, kk, jnp.where(j < m, 0, nk - 1)), 0))

    left_specs = [pl.BlockSpec((tk, D_MODEL), lambda j, kk: (kk, 0))] if shared else [spec(m) for m in range(n)]
    out_spec = pl.BlockSpec((None, D_MODEL, D_MODEL), lambda j, kk: (j, 0, 0))
    return pl.pallas_call(
        body, name=name,
        out_shape=[jax.ShapeDtypeStruct((n, D_MODEL, D_MODEL), F32), jax.ShapeDtypeStruct((n, D_MODEL, D_MODEL), BF16)],
        grid=(n, nk),
        in_specs=left_specs + [spec(m) for m in range(n)],
        out_specs=[out_spec, out_spec],
        compiler_params=_params(("arbitrary", "arbitrary"), VMEM_LIMIT),
    )(*lefts, *rights)


def _dh_dx(pieces, w_near, w_far, x, dx2, mod, b_mod, g_norm):
    seq = x.shape[0]
    tm = DX_ROWS

    def body(*refs):
        p_refs = refs[:8]
        near_hbm, far_hbm, x_ref, dx2_ref, mod_ref, bmod_ref, g_ref = refs[8:15]
        gx_ref, dshift_ref, dscale_ref, ggn_ref, w_scr, sem = refs[15:]
        step = pl.program_id(0)

        @pl.when(step == 0)
        def _():
            me = _my_pos()
            sib = _flip(me, 1)
            moves = [(near_hbm, _index(_flip(me, 2 * m))) for m in range(4)] + [(near_hbm, _index(sib))]
            moves += [(far_hbm, _index(_flip(sib, 2 * m))) for m in range(1, 4)]
            loads = [pltpu.make_async_copy(src.at[t], w_scr.at[t], sem.at[i]) for i, (src, t) in enumerate(moves)]
            for cp in loads:
                cp.start()
            for cp in loads:
                cp.wait()
            for ref in (dshift_ref, dscale_ref, ggn_ref):
                ref[...] = jnp.zeros_like(ref)

        dh = _dot_nt(p_refs[0][...], w_scr[0])
        for j in range(1, 8):
            dh = dh + _dot_nt(p_refs[j][...], w_scr[j])
        scale1 = 1.0 + mod_ref[:, D_MODEL:2 * D_MODEL] + bmod_ref[:, D_MODEL:2 * D_MODEL]
        g = g_ref[...]
        xf = x_ref[...]
        rstd_t = lax.rsqrt(jnp.mean(xf * xf, axis=-1, keepdims=True) + NORM_EPS)
        xn = xf * rstd_t
        dshift_ref[...] += jnp.sum(dh, axis=0, keepdims=True)
        dscale_ref[...] += jnp.sum(dh * (xn * g), axis=0, keepdims=True)
        ggn_ref[...] += jnp.sum(dh * scale1 * xn, axis=0, keepdims=True)
        dxn = dh * (g * scale1)
        gx_ref[...] = rstd_t * (dxn - xn * jnp.mean(dxn * xn, axis=-1, keepdims=True)) + dx2_ref[...]

    row = pl.BlockSpec((tm, D_MODEL), lambda i: (i, 0))
    const = lambda cols: pl.BlockSpec((1, cols), lambda i: (0, 0))
    vec = jax.ShapeDtypeStruct((1, D_MODEL), F32)
    return pl.pallas_call(
        body, name="dh_dx",
        out_shape=[jax.ShapeDtypeStruct((seq, D_MODEL), F32), vec, vec, vec],
        grid=(seq // tm,),
        in_specs=[row] * 8 + [pl.BlockSpec(memory_space=pl.ANY), pl.BlockSpec(memory_space=pl.ANY), row, row,
                              const(3 * D_MODEL), const(3 * D_MODEL), const(D_MODEL)],
        out_specs=[row, const(D_MODEL), const(D_MODEL), const(D_MODEL)],
        scratch_shapes=[pltpu.VMEM((8, D_MODEL, D_MODEL), BF16), pltpu.SemaphoreType.DMA((8,))],
        compiler_params=_params(("arbitrary",), VMEM_LIMIT),
    )(*pieces, w_near, w_far, x, dx2, mod, b_mod, g_norm)


def _adamw(name, w, g, m, v, recv=None):
    rows, cols = w.shape
    tr = rows if rows <= 256 else 256

    def body(*refs):
        w_ref, g_ref, m_ref, v_ref = refs[:4]
        d_ref, nm_ref, nv_ref = refs[-3:] if recv is None else refs[5:8]
        gv = g_ref[...]
        if recv is not None:
            r_ref, g_out = refs[4], refs[8]
            gv = ((gv + r_ref[0].astype(F32)) + r_ref[1].astype(F32)) + r_ref[2].astype(F32)
            g_out[...] = gv
        nm = ADAM_B1 * m_ref[...] + (1.0 - ADAM_B1) * gv
        nv = ADAM_B2 * v_ref[...] + (1.0 - ADAM_B2) * (gv * gv)
        m_hat = nm / (1.0 - ADAM_B1 ** ADAM_STEP)
        v_hat = nv / (1.0 - ADAM_B2 ** ADAM_STEP)
        d_ref[...] = -ADAM_LR * (m_hat / (jnp.sqrt(v_hat) + ADAM_EPS) + ADAM_WD * w_ref[...])
        nm_ref[...] = nm
        nv_ref[...] = nv

    spec = pl.BlockSpec((tr, cols), lambda i: (i, 0))
    shape = jax.ShapeDtypeStruct((rows, cols), F32)
    if recv is None:
        return pl.pallas_call(
            body, name=name, out_shape=[shape, shape, shape], grid=(rows // tr,),
            in_specs=[spec] * 4, out_specs=[spec] * 3,
            compiler_params=_params(("arbitrary",)),
        )(w, g, m, v)
    return pl.pallas_call(
        body, name=name, out_shape=[shape] * 4, grid=(rows // tr,),
        in_specs=[spec] * 4 + [pl.BlockSpec((3, tr, cols), lambda i: (0, i, 0))], out_specs=[spec] * 4,
        compiler_params=_params(("arbitrary",)),
    )(w, g, m, v, recv)


def kernel(x, c, positions, g_norm, w_mod, b_mod, w_in, b_gate, conv_w, conv_b, w_a, b_a, w_x, b_x, lam, w_out_rnn, w_out_attn, w_o, g_final, loss_target, m_g_norm, m_w_mod, m_b_mod, m_w_in, m_b_gate, m_conv_w, m_conv_b, m_w_a, m_b_a, m_w_x, m_b_x, m_lam, m_w_out_rnn, m_w_out_attn, m_w_o, m_g_final, v_g_norm, v_w_mod, v_b_mod, v_w_in, v_b_gate, v_conv_w, v_conv_b, v_w_a, v_b_a, v_w_x, v_b_x, v_lam, v_w_out_rnn, v_w_out_attn, v_w_o, v_g_final):
    seq = x.shape[1]
    me = _index(_my_pos())
    xs, tgt = x[0], loss_target[0]

    inv_freq = ROPE_THETA ** (-jnp.arange(0, 2 * ROT_HALF, 2, dtype=F32) / (2 * ROT_HALF))
    ang = (positions[0].astype(F32).reshape(seq // SUBLANES, SUBLANES, 1) * inv_freq).reshape(seq // SUBLANES, 128)
    cos, sin = lax.optimization_barrier((jnp.cos(ang), jnp.sin(ang)))
    cos, sin = cos.reshape(seq, ROT_HALF), sin.reshape(seq, ROT_HALF)
    rest = HEAD_DIM - 2 * ROT_HALF
    cosf = jnp.concatenate([cos, cos, jnp.ones((seq, rest), F32)], axis=1)
    sinf = jnp.concatenate([-sin, sin, jnp.zeros((seq, rest), F32)], axis=1)
    keep = (positions[0] != 0).astype(F32)[:, None]

    both = _ag_small("gather_c_conv_w", jnp.concatenate(
        [jnp.broadcast_to(c, (SUBLANES, D_MODEL)), jnp.pad(conv_w[0], ((0, SUBLANES - 4), (0, 0)))], axis=1))
    c_all, conv_w8 = both[:, 0, :D_MODEL], both[:, :, D_MODEL:]
    mod_cols = w_mod.shape[2]
    mod_part = _ag_small("gather_mod", _mod_fwd(c_all, w_mod[0]))
    mod = lax.dynamic_index_in_dim(mod_part, me, axis=1, keepdims=False).reshape(1, N_DEV * mod_cols)

    slot = lambda t: lax.dynamic_update_slice(lax.empty((N_DEV,) + t.shape, t.dtype), t[None], (me, 0, 0))
    w_in_own = w_in[0].astype(BF16)
    mod, w_in_own = lax.optimization_barrier((mod, w_in_own))
    first = _split_start("gather_w_in_start", _own_block_copies, 4, [w_in_own], [slot(w_in_own)])
    mod = mod + first[4][0:1, 0:1]

    blocks = lambda t: t.reshape(RNN_BLOCKS, 1, 128)
    rnn_params = (conv_w8, blocks(conv_b), w_a[0], blocks(b_a), w_x[0], blocks(b_x), blocks(lam))

    h = _norm(xs, mod, b_mod, g_norm)
    ids = lambda ks: jnp.bitwise_xor(me, jnp.array(ks, jnp.int32)).astype(jnp.int32)
    pf = _proj("proj_own", h, first[2][0][None], jnp.zeros((1,), jnp.int32), ids([0]), cosf, sinf, None)
    _, (w_in_near,) = _split_wait("gather_w_in_wait", _own_block_copies, first, pf)
    second = _split_start("forward_w_in_start", _forward_copies, 3, [w_in_near],
                          [lax.empty(w_in_near.shape, w_in_near.dtype)])
    near = ids([1, 2, 4, 6])
    pf = _proj("proj_near", h, second[2][0], near, near, cosf, sinf, pf)
    (w_in_near,), (w_in_far,) = _split_wait("forward_w_in_wait", _forward_copies, second, pf)
    far = ids([3, 5, 7])
    pf = _proj("proj_far", h, w_in_far, far, far, cosf, sinf, pf)
    late = [w_out_rnn[0].astype(BF16), w_out_attn[0].astype(BF16), w_o[0].astype(BF16)]
    pf, late = lax.optimization_barrier((pf, late))
    flight = _split_start("gather_out_weights_start", _peer_copies, 7 * len(late), late, [slot(t) for t in late])
    rnn_params = (rnn_params[0], rnn_params[1] + flight[4][0:1, 0:1]) + rnn_params[2:]
    hr, rnn_saved = _rnn_fwd(pf, keep, *rnn_params)
    o, lses, major = _attn_fwd(pf)

    w_or_all, w_oa_all, w_o_all = (t.reshape(D_MODEL, D_MODEL) for t in _split_wait(
        "gather_out_weights_wait", _peer_copies, flight, o)[1])
    (dx2, dhr, dz_rnn, d_o, dz_attn, dg_r, dg_a, u_rnn, dy_rnn, u_attn, dy_attn, merged, dmo,
     gp_g_final, gp_b_gate, dgate, loss_part) = _hub(
        xs, tgt, hr, pf, o, mod, b_mod, b_gate, g_final.reshape(1, D_MODEL), w_or_all, w_oa_all, w_o_all)
    gp_out, gp_out_low = _pair_grads("out_grads", [u_rnn, u_attn, merged], [dy_rnn, dy_attn, dmo])
    dq, dk, dv = _attn_bwd(pf, d_o, o, lses, major, cosf, sinf)
    dx_rnn, gp_conv_w, gp_conv_b, gp_w_a, gp_b_a, gp_w_x, gp_b_x, gp_lam = _rnn_bwd(
        pf, hr, dhr, rnn_saved, keep, rnn_params[0], rnn_params[2], rnn_params[4], rnn_params[6])
    pieces = [dx_rnn, dz_rnn, dq, dk, dv, dz_attn, dg_r, dg_a]
    gp_w_in, gp_w_in_low = _pair_grads("w_in_grad", [h], pieces)

    by_target = lambda t: [(t.reshape(3, N_DEV, 128, D_MODEL), i) for i in range(3)]
    stacks = [(gp_w_in, None)] + by_target(gp_out)
    from_sib = _rs_to_sibling("rs_sibling", [(gp_w_in_low, None)] + by_target(gp_out_low))
    targets = jnp.bitwise_xor(me, 2 * jnp.arange(4, dtype=jnp.int32)).astype(jnp.int32)
    sums = [_add_sibling("rs_add_sibling_%d" % a, s_, r_, targets) for a, (s_, r_) in enumerate(zip(stacks, from_sib))]
    sends = [send for _, send in sums]
    reduce_flight = _split_start("rs_chips_start", _chip_copies, 3 * len(sends), sends,
                                 [lax.empty(t.shape, t.dtype) for t in sends])

    mod_after = mod + reduce_flight[4][0:1, 0:1]
    grad_x, dshift, dscale, gp_g_norm = _dh_dx(pieces, w_in_near, w_in_far, xs, dx2, mod_after, b_mod, g_norm)

    flat = lambda t: t.reshape(-1, 128)
    dmod = flat(jnp.concatenate([dshift, dscale, dgate], axis=1))
    dmod_placed = lax.dynamic_update_slice(jnp.zeros((N_DEV * dmod.shape[0], 128), F32), dmod, (me * dmod.shape[0], 0))
    small = [flat(gp_g_norm), flat(gp_b_gate), flat(gp_conv_b), flat(gp_b_a), flat(gp_b_x), flat(gp_lam),
             flat(gp_g_final), flat(gp_conv_w), jnp.broadcast_to(loss_part, (SUBLANES, 128)),
             flat(gp_w_a), flat(gp_w_x), dmod_placed]
    sizes = [t.shape[0] for t in small]
    small.append(jnp.zeros((-sum(sizes) % (2 * SUBLANES), 128), F32))
    total = _allreduce_small("allreduce_small_grads", jnp.concatenate(small, axis=0))
    offs = [sum(sizes[:i]) for i in range(len(sizes))]
    (g_g_norm, g_b_gate, g_conv_b, g_b_a, g_b_x, g_lam, g_g_final, g_conv_w_all, loss_rows, g_w_a, g_w_x,
     dmod_rows) = (total[o_:o_ + s_] for o_, s_ in zip(offs, sizes))
    loss = loss_rows[0, 0]
    g_conv_w = lax.dynamic_index_in_dim(g_conv_w_all.reshape(RNN_BLOCKS, SUBLANES, 128), me, axis=0,
                                        keepdims=False)[:4]

    dmod_all = dmod_rows.reshape(N_DEV, 3 * D_MODEL)
    dmod_cols = lax.dynamic_slice_in_dim(dmod_all, me * mod_cols, mod_cols, axis=1)
    g_b_mod, g_w_mod = _mod_bwd(c_all, dmod_all, dmod_cols)

    _, from_chips = _split_wait("rs_chips_wait", _chip_copies, reduce_flight, total)

    results = {}
    sharded = (("w_in", w_in, m_w_in, v_w_in, (D_MODEL, D_MODEL)),
               ("w_out_rnn", w_out_rnn, m_w_out_rnn, v_w_out_rnn, (128, D_MODEL)),
               ("w_out_attn", w_out_attn, m_w_out_attn, v_w_out_attn, (128, D_MODEL)),
               ("w_o", w_o, m_w_o, v_w_o, (128, D_MODEL)))
    for (name, w_, m_, v_, shape2), (own, _), arrived in zip(sharded, sums, from_chips):
        d_, nm_, nv_, g_ = _adamw("adamw_" + name, w_.reshape(shape2), own, m_.reshape(shape2), v_.reshape(shape2),
                                  arrived)
        results[name] = (g_, d_, nm_, nv_)
    shape2 = (D_MODEL, mod_cols)
    results["w_mod"] = (g_w_mod,) + tuple(_adamw("adamw_w_mod", w_mod.reshape(shape2), g_w_mod,
                                                 m_w_mod.reshape(shape2), v_w_mod.reshape(shape2)))
    lanes = (("g_norm", g_norm, g_g_norm, m_g_norm, v_g_norm), ("b_mod", b_mod, g_b_mod, m_b_mod, v_b_mod),
             ("b_gate", b_gate, g_b_gate, m_b_gate, v_b_gate), ("conv_w", conv_w, g_conv_w, m_conv_w, v_conv_w),
             ("conv_b", conv_b, g_conv_b, m_conv_b, v_conv_b), ("w_a", w_a, g_w_a, m_w_a, v_w_a),
             ("b_a", b_a, g_b_a, m_b_a, v_b_a), ("w_x", w_x, g_w_x, m_w_x, v_w_x), ("b_x", b_x, g_b_x, m_b_x, v_b_x),
             ("lam", lam, g_lam, m_lam, v_lam), ("g_final", g_final, g_g_final, m_g_final, v_g_final))
    for name, w_, g_, m_, v_ in lanes:
        rows128 = lambda t: t.reshape(-1, 128)
        results[name] = (g_,) + tuple(_adamw("adamw_" + name, rows128(w_), rows128(g_), rows128(m_), rows128(v_)))
    order = ("g_norm", "w_mod", "b_mod", "w_in", "b_gate", "conv_w", "conv_b", "w_a", "b_a", "w_x", "b_x", "lam",
             "w_out_rnn", "w_out_attn", "w_o", "g_final")
    given = dict(g_norm=g_norm, w_mod=w_mod, b_mod=b_mod, w_in=w_in, b_gate=b_gate, conv_w=conv_w, conv_b=conv_b,
                 w_a=w_a, b_a=b_a, w_x=w_x, b_x=b_x, lam=lam, w_out_rnn=w_out_rnn, w_out_attn=w_out_attn, w_o=w_o,
                 g_final=g_final)
    outs = [[results[name][k].reshape(given[name].shape) for name in order] for k in range(4)]
    return (loss, grad_x[None], *outs[0], *outs[1], *outs[2], *outs[3])
```

```python
import jax
import jax.numpy as jnp
from jax import lax
from jax.experimental import pallas as pl
from jax.experimental.pallas import tpu as pltpu

F32 = jnp.float32
BF16 = jnp.bfloat16
MESH = pl.DeviceIdType.MESH

D_MODEL = 1024
N_HEADS = 8
HEAD_DIM = 128
RNN_BLOCKS = 8
N_DEV = 8
ROT_HALF = 16
ROPE_THETA = 500000.0
DILATIONS = (1, 4, 16)
KEY_BLOCK = 128
SPAN = KEY_BLOCK * DILATIONS[-1]
ATTN_SCALE = HEAD_DIM ** -0.5
NORM_EPS = 1e-6
LRU_C = 8.0
NEG_INF = -1e30
ADAM_LR, ADAM_B1, ADAM_B2, ADAM_EPS, ADAM_WD, ADAM_STEP = 0.001, 0.9, 0.999, 1e-08, 0.01, 10

SUBLANES = 8
VMEM_LIMIT = 56 * 1024 * 1024
PROJ_ROWS = 1024
RNN_ROWS = 2048
HUB_ROWS = 256
DX_ROWS = 512
WGRAD_ROWS = 1024
ADD_ROWS = 256


def _params(sem=None, vmem=None):
    return pltpu.CompilerParams(dimension_semantics=sem, vmem_limit_bytes=vmem)


def _dot(a, b):
    return jnp.dot(a, b, preferred_element_type=F32)


def _dot_nt(a, b):
    return lax.dot_general(a, b, (((1,), (1,)), ((), ())), preferred_element_type=F32)


def _dot_tn(a, b):
    return lax.dot_general(a, b, (((0,), (0,)), ((), ())), preferred_element_type=F32)


def _sigmoid(z):
    return 1.0 / (1.0 + jnp.exp(-z))


def _expm1_nonpos(z, exp_z):
    return jnp.where(z > -0.01, z * (1.0 + 0.5 * z), exp_z - 1.0)


def _my_pos():
    return lax.axis_index("x"), lax.axis_index("y"), lax.axis_index("c")


def _flip(pos, k):
    x, y, c = pos
    return ((1 - x) if k & 4 else x, (1 - y) if k & 2 else y, (1 - c) if k & 1 else c)


def _index(pos):
    return 4 * pos[0] + 2 * pos[1] + pos[2]


def _ag_small(name, v):
    rows, cols = v.shape

    def body(v_ref, out_ref, send_sems, recv_sems):
        me = _my_pos()
        out_ref[_index(me)] = v_ref[...]
        sends = []
        for k in range(1, N_DEV):
            cp = pltpu.make_async_remote_copy(
                src_ref=v_ref, dst_ref=out_ref.at[_index(me)], send_sem=send_sems.at[k - 1],
                recv_sem=recv_sems.at[k - 1], device_id=_flip(me, k), device_id_type=MESH)
            cp.start()
            sends.append(cp)
        for k in range(1, N_DEV):
            peer = _flip(me, k)
            pltpu.make_async_remote_copy(
                src_ref=v_ref, dst_ref=out_ref.at[_index(peer)], send_sem=send_sems.at[k - 1],
                recv_sem=recv_sems.at[k - 1], device_id=peer, device_id_type=MESH).wait_recv()
        for cp in sends:
            cp.wait_send()

    return pl.pallas_call(
        body, name=name,
        out_shape=jax.ShapeDtypeStruct((N_DEV, rows, cols), v.dtype),
        in_specs=[pl.BlockSpec(memory_space=pltpu.VMEM)],
        out_specs=pl.BlockSpec(memory_space=pltpu.VMEM),
        scratch_shapes=[pltpu.SemaphoreType.DMA((N_DEV - 1,)), pltpu.SemaphoreType.DMA((N_DEV - 1,))],
        compiler_params=_params(None, VMEM_LIMIT),
    )(v)


def _split_start(name, make_copies, nsem, srcs, lands):
    n, k = len(srcs), len(lands)

    def body(*refs):
        for cp in make_copies(refs[:n], refs[n:n + k], refs[n + k], refs[n + k + 1]):
            cp.start()
        refs[-1][...] = jnp.zeros_like(refs[-1])

    hbm = pl.BlockSpec(memory_space=pltpu.HBM)
    sem = pl.BlockSpec(memory_space=pltpu.SEMAPHORE)
    arrays = [*srcs, *lands]
    outs = pl.pallas_call(
        body, name=name,
        out_shape=(pltpu.SemaphoreType.DMA((nsem,)), pltpu.SemaphoreType.DMA((nsem,)),
                   *[pltpu.HBM(t.shape, t.dtype) for t in arrays], jax.ShapeDtypeStruct((SUBLANES, 128), F32)),
        in_specs=[hbm] * (n + k),
        out_specs=(sem, sem, *[hbm] * (n + k), pl.BlockSpec(memory_space=pltpu.VMEM)),
        input_output_aliases={i: 2 + i for i in range(n + k)},
        compiler_params=pltpu.CompilerParams(has_side_effects=pltpu.SideEffectType.DATAFLOW_SIDE_EFFECTING),
    )(*[pltpu.with_memory_space_constraint(t, pltpu.HBM) for t in arrays])
    return outs[0], outs[1], outs[2:2 + n], outs[2 + n:2 + n + k], outs[-1]


def _split_wait(name, make_copies, flight, after):
    send_sems, recv_sems, srcs, lands, _ = flight
    n, k = len(srcs), len(lands)

    def body(*refs):
        for cp in make_copies(refs[:n], refs[n:n + k], refs[n + k], refs[n + k + 1]):
            cp.wait_send()
            cp.wait_recv()

    hbm = pl.BlockSpec(memory_space=pltpu.HBM)
    sem = pl.BlockSpec(memory_space=pltpu.SEMAPHORE)
    arrays = [*srcs, *lands]
    outs = pl.pallas_call(
        body, name=name, out_shape=tuple(pltpu.HBM(t.shape, t.dtype) for t in arrays),
        in_specs=[hbm] * (n + k) + [sem, sem, pl.BlockSpec(memory_space=pl.ANY)],
        out_specs=[hbm] * (n + k),
        input_output_aliases={i: i for i in range(n + k)},
        compiler_params=pltpu.CompilerParams(has_side_effects=pltpu.SideEffectType.DATAFLOW_SIDE_EFFECTING),
    )(*arrays, send_sems, recv_sems, after)
    return outs[:n], outs[n:]


def _remote(src, dst, send_sems, recv_sems, k, to):
    return pltpu.make_async_remote_copy(src_ref=src, dst_ref=dst, send_sem=send_sems.at[k], recv_sem=recv_sems.at[k],
                                        device_id=to, device_id_type=MESH)


def _peer_copies(shards, lands, send_sems, recv_sems):
    me = _my_pos()
    return [_remote(shards[a], lands[a].at[_index(me)], send_sems, recv_sems, a * 7 + k - 1, _flip(me, k))
            for a in range(len(shards)) for k in range(1, N_DEV)]


def _own_block_copies(shards, lands, send_sems, recv_sems):
    me = _my_pos()
    return [_remote(shards[0], lands[0].at[_index(me)], send_sems, recv_sems, m - 1, _flip(me, 2 * m))
            for m in range(1, 4)]


def _sibling_copy(shards, lands, send_sems, recv_sems):
    me = _my_pos()
    return [_remote(shards[0], lands[0].at[_index(me)], send_sems, recv_sems, 0, _flip(me, 1))]


def _forward_copies(arrived, lands, send_sems, recv_sems):
    me = _my_pos()
    return [_remote(arrived[0].at[_index(_flip(me, 2 * m))], lands[0].at[_index(_flip(me, 2 * m))],
                    send_sems, recv_sems, m - 1, _flip(me, 1)) for m in range(1, 4)]


def _rs_to_sibling(name, stacks):
    n = len(stacks)

    def body(*refs):
        ins, outs = refs[:n], refs[n:2 * n]
        send_sems, recv_sems = refs[2 * n:]
        me = _my_pos()
        sib = _flip(me, 1)
        sends = []
        for a, (_, which) in enumerate(stacks):
            by_target = ins[a] if which is None else ins[a].at[which]
            for m in range(4):
                target = _flip(sib, 2 * m)
                cp = pltpu.make_async_remote_copy(
                    src_ref=by_target.at[_index(target)], dst_ref=outs[a].at[m],
                    send_sem=send_sems.at[a * 4 + m], recv_sem=recv_sems.at[a * 4 + m],
                    device_id=sib, device_id_type=MESH)
                cp.start()
                sends.append(cp)
        for cp in sends:
            cp.wait_recv()
        for cp in sends:
            cp.wait_send()

    any_spec = pl.BlockSpec(memory_space=pl.ANY)
    return pl.pallas_call(
        body, name=name,
        out_shape=[jax.ShapeDtypeStruct((4,) + s.shape[-2:], s.dtype) for s, _ in stacks],
        in_specs=[any_spec] * n, out_specs=[any_spec] * n,
        scratch_shapes=[pltpu.SemaphoreType.DMA((4 * n,)), pltpu.SemaphoreType.DMA((4 * n,))],
    )(*[s for s, _ in stacks])


def _chip_copies(srcs, lands, send_sems, recv_sems):
    me = _my_pos()
    return [_remote(srcs[a].at[m - 1], lands[a].at[m - 1], send_sems, recv_sems, a * 3 + m - 1, _flip(me, 2 * m))
            for a in range(len(srcs)) for m in range(1, 4)]


def _add_sibling(name, stack, recv, targets):
    stack, which = stack
    rows, cols = stack.shape[-2:]
    tr = min(rows, ADD_ROWS)

    def by_target(index):
        if which is None:
            return pl.BlockSpec((None, tr, cols), lambda *g: (index(*g), g[-2], 0))
        return pl.BlockSpec((None, None, tr, cols), lambda *g: (which, index(*g), g[-2], 0))

    def own_body(t_ref, a_ref, b_ref, o_ref):
        o_ref[...] = a_ref[...] + b_ref[...].astype(F32)

    own = pl.pallas_call(
        own_body, name=name + "_own",
        out_shape=jax.ShapeDtypeStruct((rows, cols), F32),
        grid_spec=pltpu.PrefetchScalarGridSpec(
            num_scalar_prefetch=1, grid=(rows // tr,),
            in_specs=[by_target(lambda i, t: t[0]),
                      pl.BlockSpec((None, tr, cols), lambda i, t: (0, i, 0))],
            out_specs=pl.BlockSpec((tr, cols), lambda i, t: (i, 0))),
        compiler_params=_params(("arbitrary",)),
    )(targets, stack, recv)

    def send_body(t_ref, a_ref, b_ref, o_ref):
        o_ref[...] = (a_ref[...] + b_ref[...].astype(F32)).astype(BF16)

    send = pl.pallas_call(
        send_body, name=name + "_send",
        out_shape=jax.ShapeDtypeStruct((3, rows, cols), BF16),
        grid_spec=pltpu.PrefetchScalarGridSpec(
            num_scalar_prefetch=1, grid=(3, rows // tr),
            in_specs=[by_target(lambda m, i, t: t[m + 1]),
                      pl.BlockSpec((None, tr, cols), lambda m, i, t: (m + 1, i, 0))],
            out_specs=pl.BlockSpec((None, tr, cols), lambda m, i, t: (m, i, 0))),
        compiler_params=_params(("arbitrary", "arbitrary")),
    )(targets, stack, recv)
    return own, send


def _allreduce_small(name, v):
    rows, cols = v.shape
    half = rows // 2
    assert rows % (2 * SUBLANES) == 0

    def body(v_ref, out_ref, from_sib, chip_half, from_chips, send_sems, recv_sems):
        me = _my_pos()
        sib = _flip(me, 1)
        mine = pl.ds(pl.multiple_of(me[2] * half, SUBLANES), half)
        theirs = pl.ds(pl.multiple_of((1 - me[2]) * half, SUBLANES), half)

        def copy(k, src, dst, to):
            return pltpu.make_async_remote_copy(src_ref=src, dst_ref=dst, send_sem=send_sems.at[k],
                                                recv_sem=recv_sems.at[k], device_id=to, device_id_type=MESH)

        to_sib = copy(0, v_ref.at[theirs], from_sib, sib)
        to_sib.start()
        to_sib.wait_recv()
        chip_half[...] = v_ref[mine, :] + from_sib[...]
        to_chips = [copy(m, chip_half, from_chips.at[m - 1], _flip(me, 2 * m)) for m in range(1, 4)]
        for cp in to_chips:
            cp.start()
        for cp in to_chips:
            cp.wait_recv()
        my_chip = 2 * me[0] + me[1]
        total = None
        for chip in range(4):
            slot = jnp.maximum(jnp.bitwise_xor(chip, my_chip) - 1, 0)
            part = jnp.where(chip == my_chip, chip_half[...], from_chips[slot])
            total = part if total is None else total + part
        out_ref[mine, :] = total
        swap = copy(4, out_ref.at[mine], out_ref.at[mine], sib)
        swap.start()
        copy(4, out_ref.at[theirs], out_ref.at[theirs], sib).wait_recv()
        for cp in [to_sib, swap] + to_chips:
            cp.wait_send()

    return pl.pallas_call(
        body, name=name, out_shape=jax.ShapeDtypeStruct((rows, cols), F32),
        in_specs=[pl.BlockSpec(memory_space=pltpu.VMEM)],
        out_specs=pl.BlockSpec(memory_space=pltpu.VMEM),
        scratch_shapes=[pltpu.VMEM((half, cols), F32), pltpu.VMEM((half, cols), F32),
                        pltpu.VMEM((3, half, cols), F32),
                        pltpu.SemaphoreType.DMA((5,)), pltpu.SemaphoreType.DMA((5,))],
        compiler_params=_params(None, VMEM_LIMIT),
    )(v)


def _mod_fwd(c_all, w_mod):
    def body(c_ref, w_ref, o_ref):
        c = c_ref[...]
        o_ref[...] = jnp.dot(c * _sigmoid(c), w_ref[...], preferred_element_type=F32,
                             precision=lax.Precision.HIGHEST)

    return pl.pallas_call(
        body, name="mod_fwd", out_shape=jax.ShapeDtypeStruct((N_DEV, w_mod.shape[1]), F32),
    )(c_all, w_mod)


def _mod_bwd(c_all, dmod_all, dmod_cols):
    def body(c_ref, da_ref, dc_ref, gb_ref, gw_ref):
        c = c_ref[...]
        acc = da_ref[0:1, :]
        for b in range(1, N_DEV):
            acc = acc + da_ref[b:b + 1, :]
        gb_ref[...] = acc
        gw_ref[...] = lax.dot_general(c * _sigmoid(c), dc_ref[...], (((0,), (0,)), ((), ())),
                                      preferred_element_type=F32, precision=lax.Precision.HIGHEST)

    return pl.pallas_call(
        body, name="mod_bwd",
        out_shape=[jax.ShapeDtypeStruct((1, dmod_all.shape[1]), F32),
                   jax.ShapeDtypeStruct((c_all.shape[1], dmod_cols.shape[1]), F32)],
    )(c_all, dmod_all, dmod_cols)


def _rope_partner(t):
    lane = lax.broadcasted_iota(jnp.int32, t.shape, 1)
    return jnp.where(lane < ROT_HALF, pltpu.roll(t, HEAD_DIM - ROT_HALF, 1), pltpu.roll(t, ROT_HALF, 1))


def _norm(x, mod, b_mod, g_norm):
    seq = x.shape[0]
    tm = PROJ_ROWS

    def body(x_ref, mod_ref, bmod_ref, g_ref, h_ref):
        xf = x_ref[...]
        rstd = lax.rsqrt(jnp.mean(xf * xf, axis=-1, keepdims=True) + NORM_EPS)
        shift = mod_ref[:, 0:D_MODEL] + bmod_ref[:, 0:D_MODEL]
        scale = mod_ref[:, D_MODEL:2 * D_MODEL] + bmod_ref[:, D_MODEL:2 * D_MODEL]
        h_ref[...] = (((xf * rstd) * g_ref[...]) * (1.0 + scale) + shift).astype(BF16)

    row = pl.BlockSpec((tm, D_MODEL), lambda i: (i, 0))
    const = lambda cols: pl.BlockSpec((1, cols), lambda i: (0, 0))
    return pl.pallas_call(
        body, name="norm", out_shape=jax.ShapeDtypeStruct((seq, D_MODEL), BF16), grid=(seq // tm,),
        in_specs=[row, const(3 * D_MODEL), const(3 * D_MODEL), const(D_MODEL)], out_specs=row,
        compiler_params=_params(("arbitrary",), VMEM_LIMIT),
    )(x, mod, b_mod, g_norm)


def _proj(name, h, w, slots, pieces, cosf, sinf, prior):
    seq = h.shape[0]
    tm = PROJ_ROWS
    count = pieces.shape[0]

    def body(slots_ref, pieces_ref, h_ref, w_ref, cos_ref, sin_ref, *rest):
        out_ref = rest[-1]
        piece = pieces_ref[pl.program_id(0)]

        @pl.when((piece < 2) | (piece > 3))
        def _():
            out_ref[...] = _dot(h_ref[...], w_ref[...])

        def rotated(gain):
            for pair in range(N_HEADS // 2):
                both = _dot(h_ref[...], w_ref[:, 2 * pair * HEAD_DIM:2 * (pair + 1) * HEAD_DIM])
                for hh in (2 * pair, 2 * pair + 1):
                    t = both[:, (hh % 2) * HEAD_DIM:(hh % 2 + 1) * HEAD_DIM]
                    t = t * cos_ref[...] + _rope_partner(t) * sin_ref[...]
                    out_ref[:, hh * HEAD_DIM:(hh + 1) * HEAD_DIM] = t if gain is None else t * gain

        @pl.when(piece == 2)
        def _():
            rotated(ATTN_SCALE)

        @pl.when(piece == 3)
        def _():
            rotated(None)

    row = lambda j, i, sl, pc: (i, 0)
    in_specs = [pl.BlockSpec((tm, D_MODEL), row),
                pl.BlockSpec((None, D_MODEL, D_MODEL), lambda j, i, sl, pc: (sl[j], 0, 0)),
                pl.BlockSpec((tm, HEAD_DIM), row), pl.BlockSpec((tm, HEAD_DIM), row)]
    args = [slots, pieces, h, w, cosf, sinf]
    aliases = {}
    if prior is not None:
        in_specs.append(pl.BlockSpec(memory_space=pl.ANY))
        args.append(prior)
        aliases = {6: 0}
    return pl.pallas_call(
        body, name=name,
        out_shape=jax.ShapeDtypeStruct((seq, 8 * D_MODEL), F32),
        grid_spec=pltpu.PrefetchScalarGridSpec(
            num_scalar_prefetch=2, grid=(count, seq // tm), in_specs=in_specs,
            out_specs=pl.BlockSpec((tm, D_MODEL), lambda j, i, sl, pc: (i, pc[j]))),
        input_output_aliases=aliases,
        compiler_params=_params(("arbitrary", "arbitrary"), VMEM_LIMIT),
    )(*args)


def _shift_down(v, s, head):
    rolled = pltpu.roll(v, s, 0)
    row = lax.broadcasted_iota(jnp.int32, head.shape, 0)
    first = jnp.where(row < s, pltpu.roll(head, s, 0), rolled[:SUBLANES, :])
    return jnp.concatenate([first, rolled[SUBLANES:, :]], axis=0)


def _shift_up(v, s, tail):
    rows = v.shape[0]
    rolled = pltpu.roll(v, rows - s, 0)
    row = lax.broadcasted_iota(jnp.int32, tail.shape, 0)
    last = jnp.where(row >= SUBLANES - s, pltpu.roll(tail, SUBLANES - s, 0), rolled[rows - SUBLANES:, :])
    return jnp.concatenate([rolled[:rows - SUBLANES, :], last], axis=0)


def _doubling(a, b, period, reverse):
    rows = a.shape[0]
    pos = lax.broadcasted_iota(jnp.int32, a.shape, 0) & (period - 1)
    k = 1
    while k < period:
        inside = (pos < period - k) if reverse else (pos >= k)
        shift = rows - k if reverse else k
        a_s = jnp.where(inside, pltpu.roll(a, shift, 0), 1.0)
        b_s = jnp.where(inside, pltpu.roll(b, shift, 0), 0.0)
        b = a * b_s + b
        a = a * a_s
        k *= 2
    return a, b


def _scan(a, b, boundary, reverse, a_scr, b_scr, spread):
    rows = a.shape[0]
    ntile = rows // SUBLANES
    a_scr[...], b_scr[...] = _doubling(a, b, SUBLANES, reverse)
    ends = pl.ds(0 if reverse else SUBLANES - 1, ntile, stride=SUBLANES)
    a_end, x_end = _doubling(a_scr[ends, :], b_scr[ends, :], ntile, reverse)
    x_end = x_end + a_end * boundary
    tile = lax.broadcasted_iota(jnp.int32, x_end.shape, 0)
    if reverse:
        incoming = jnp.where(tile == ntile - 1, boundary, pltpu.roll(x_end, ntile - 1, 0))
        last = x_end[0:1, :]
    else:
        incoming = jnp.where(tile == 0, boundary, pltpu.roll(x_end, 1, 0))
        last = x_end[ntile - 1:ntile, :]
    for s in range(SUBLANES):
        spread[pl.ds(s, ntile, stride=SUBLANES), :] = incoming
    return b_scr[...] + a_scr[...] * spread[...], last


def _conv_taps(xr, head):
    return [_shift_down(xr, 3, head), _shift_down(xr, 2, head), _shift_down(xr, 1, head), xr]


def _rnn_gates(xc, wa, ba, wx, bx, lam, keep):
    xcb = xc.astype(BF16)
    r = _sigmoid(_dot(xcb, wa.astype(BF16)) + ba)
    i = _sigmoid(_dot(xcb, wx.astype(BF16)) + bx)
    softplus = jnp.maximum(-lam, 0.0) + jnp.log(1.0 + jnp.exp(-jnp.abs(lam)))
    cl = -LRU_C * softplus
    log_a = cl * r
    a_raw = jnp.exp(log_a)
    mult_raw = jnp.sqrt(-_expm1_nonpos(2.0 * log_a, a_raw * a_raw))
    live = keep > 0.0
    return r, i, cl, a_raw, mult_raw, jnp.where(live, a_raw, 0.0), jnp.where(live, mult_raw, 1.0), live


def _rnn_specs(seq, rows, time_of):
    per = rows // SUBLANES
    vec = pl.BlockSpec((None, 1, 128), lambda hb, n: (hb, 0, 0))
    mat = pl.BlockSpec((None, 128, 128), lambda hb, n: (hb, 0, 0))
    return [pl.BlockSpec((rows, 128), lambda hb, n: (time_of(n), hb)),
            pl.BlockSpec((SUBLANES, 128), lambda hb, n: (jnp.maximum(time_of(n) * per - 1, 0), hb)),
            pl.BlockSpec((rows, 1), lambda hb, n: (time_of(n), 0)),
            pl.BlockSpec((None, SUBLANES, 128), lambda hb, n: (hb, 0, 0)),
            vec, mat, vec, mat, vec, vec]


def _rnn_fwd(pf, keep, conv_w8, conv_b, w_a, b_a, w_x, b_x, lam):
    seq = pf.shape[0]
    rows = RNN_ROWS

    def body(x_ref, xh_ref, keep_ref, cw_ref, cb_ref, wa_ref, ba_ref, wx_ref, bx_ref, lam_ref,
             hr_ref, xc_ref, r_ref, i_ref, araw_ref, mraw_ref, carry, a_scr, b_scr, spread):
        n = pl.program_id(1)

        @pl.when(n == 0)
        def _():
            carry[...] = jnp.zeros_like(carry)

        xr = x_ref[...]
        head = jnp.where(n > 0, xh_ref[...], 0.0)
        taps = _conv_taps(xr, head)
        xc = cb_ref[...] + sum(cw_ref[k:k + 1, :] * taps[k] for k in range(4))
        r, i, _, a_raw, mult_raw, a, mult, _ = _rnn_gates(xc, wa_ref[...], ba_ref[...], wx_ref[...], bx_ref[...],
                                                          lam_ref[...], keep_ref[...])
        xc_ref[...], r_ref[...], i_ref[...], araw_ref[...], mraw_ref[...] = xc, r, i, a_raw, mult_raw
        h, last = _scan(a, mult * i * xc, carry[0:1, :], False, a_scr, b_scr, spread)
        hr_ref[...] = h
        carry[...] = jnp.broadcast_to(last, carry.shape)

    chunk_f32 = pltpu.VMEM((rows, 128), F32)
    chunk = pl.BlockSpec((rows, 128), lambda hb, n: (n, hb))
    shape = jax.ShapeDtypeStruct((seq, D_MODEL), F32)
    outs = pl.pallas_call(
        body, name="rnn_fwd",
        out_shape=[shape] * 6,
        grid=(RNN_BLOCKS, seq // rows),
        in_specs=_rnn_specs(seq, rows, lambda n: n),
        out_specs=[chunk] * 6,
        scratch_shapes=[pltpu.VMEM((SUBLANES, 128), F32), chunk_f32, chunk_f32, chunk_f32],
        compiler_params=_params(("arbitrary", "arbitrary"), VMEM_LIMIT),
    )(pf, pf, keep, conv_w8, conv_b, w_a, b_a, w_x, b_x, lam)
    return outs[0], tuple(outs[1:])


def _rnn_bwd(pf, hr, dhr, saved, keep, conv_w8, w_a, w_x, lam):
    seq = pf.shape[0]
    rows = RNN_ROWS
    nchunk = seq // rows
    per = rows // SUBLANES
    time_of = lambda n: nchunk - 1 - n

    def body(x_ref, keep_ref, cw_ref, wa_ref, wx_ref, lam_ref, hr_ref, hrh_ref, dhr_ref,
             xc_ref, r_ref, i_ref, araw_ref, mraw_ref,
             dx_ref, gcw_ref, gcb_ref, gwa_ref, gba_ref, gwx_ref, gbx_ref, glam_ref,
             g_carry, dxc_tail, a_scr, b_scr, spread):
        n = pl.program_id(1)
        first_in_time = n == nchunk - 1

        @pl.when(n == 0)
        def _():
            g_carry[...] = jnp.zeros_like(g_carry)
            dxc_tail[...] = jnp.zeros_like(dxc_tail)
            for ref in (gcw_ref, gcb_ref, gwa_ref, gba_ref, gwx_ref, gbx_ref, glam_ref):
                ref[...] = jnp.zeros_like(ref)

        cw, wa, wx, lam = cw_ref[...], wa_ref[...], wx_ref[...], lam_ref[...]
        xc, r, i, a_raw, mult_raw = xc_ref[...], r_ref[...], i_ref[...], araw_ref[...], mraw_ref[...]
        cl = -LRU_C * (jnp.maximum(-lam, 0.0) + jnp.log(1.0 + jnp.exp(-jnp.abs(lam))))
        live = keep_ref[...] > 0.0
        a, mult = jnp.where(live, a_raw, 0.0), jnp.where(live, mult_raw, 1.0)
        h_prev = _shift_down(hr_ref[...], 1, jnp.where(first_in_time, 0.0, hrh_ref[...]))

        row = lax.broadcasted_iota(jnp.int32, xc.shape, 0)
        last = row == rows - 1
        a_next = jnp.where(last, 0.0, pltpu.roll(a, rows - 1, 0))
        g, g_first = _scan(a_next, dhr_ref[...] + jnp.where(last, g_carry[0:1, :], 0.0),
                           jnp.zeros((1, 128), F32), True, a_scr, b_scr, spread)
        g_carry[...] = jnp.broadcast_to(a[0:1, :] * g_first, g_carry.shape)

        da = g * h_prev
        dmult = g * i * xc
        di = g * mult * xc
        dxc = g * mult * i
        dlog_a = jnp.where(live, da * a_raw - dmult * a_raw * a_raw / mult_raw, 0.0)
        dpa = (dlog_a * cl) * r * (1.0 - r)
        dpx = di * i * (1.0 - i)
        glam_ref[...] += jnp.sum(dlog_a * r, axis=0, keepdims=True) * (LRU_C * _sigmoid(-lam))
        xcb, dpab, dpxb = xc.astype(BF16), dpa.astype(BF16), dpx.astype(BF16)
        gwa_ref[...] += _dot_tn(xcb, dpab)
        gwx_ref[...] += _dot_tn(xcb, dpxb)
        gba_ref[...] += jnp.sum(dpa, axis=0, keepdims=True)
        gbx_ref[...] += jnp.sum(dpx, axis=0, keepdims=True)
        dxc = dxc + _dot_nt(dpab, wa.astype(BF16)) + _dot_nt(dpxb, wx.astype(BF16))

        gcb_ref[...] += jnp.sum(dxc, axis=0, keepdims=True)
        xr = x_ref[...]
        tail = dxc_tail[...]
        later = [_shift_up(dxc, 3 - k, tail) for k in range(3)] + [dxc]
        dx = cw[3:4, :] * dxc
        for k in range(3):
            dx = dx + cw[k:k + 1, :] * later[k]
        for k in range(4):
            gcw_ref[k:k + 1, :] += jnp.sum(xr * later[k], axis=0, keepdims=True)
        dx_ref[...] = dx.astype(BF16)
        dxc_tail[...] = dxc[0:SUBLANES, :]

    blk = lambda hb, n: (hb, 0, 0)
    chunk = pl.BlockSpec((rows, 128), lambda hb, n: (time_of(n), hb))
    vec = pl.BlockSpec((None, 1, 128), blk)
    mat = pl.BlockSpec((None, 128, 128), blk)
    vec_shape = jax.ShapeDtypeStruct((RNN_BLOCKS, 1, 128), F32)
    mat_shape = jax.ShapeDtypeStruct((RNN_BLOCKS, 128, 128), F32)
    return pl.pallas_call(
        body, name="rnn_bwd",
        out_shape=[jax.ShapeDtypeStruct((seq, D_MODEL), BF16),
                   jax.ShapeDtypeStruct((RNN_BLOCKS, SUBLANES, 128), F32), vec_shape,
                   mat_shape, vec_shape, mat_shape, vec_shape, vec_shape],
        grid=(RNN_BLOCKS, nchunk),
        in_specs=[chunk, pl.BlockSpec((rows, 1), lambda hb, n: (time_of(n), 0)),
                  pl.BlockSpec((None, SUBLANES, 128), blk), mat, mat, vec, chunk,
                  pl.BlockSpec((SUBLANES, 128), lambda hb, n: (jnp.maximum(time_of(n) * per - 1, 0), hb)), chunk]
                 + [chunk] * 5,
        out_specs=[chunk, pl.BlockSpec((None, SUBLANES, 128), blk), vec, mat, vec, mat, vec, vec],
        scratch_shapes=[pltpu.VMEM((SUBLANES, 128), F32), pltpu.VMEM((SUBLANES, 128), F32)]
                       + [pltpu.VMEM((rows, 128), F32)] * 3,
        compiler_params=_params(("arbitrary", "arbitrary"), VMEM_LIMIT),
    )(pf, keep, conv_w8, w_a, w_x, lam, hr, hr, dhr, *saved)


def _unit_rows(dil, r, j):
    start = j * KEY_BLOCK * dil + r
    return pl.ds(start, KEY_BLOCK) if dil == 1 else pl.ds(start, KEY_BLOCK, stride=dil)


def _attn_fwd(proj):
    nh, seq = N_HEADS, proj.shape[0]
    nchunk = seq // SPAN
    nblk = SPAN // KEY_BLOCK
    wide = DILATIONS[-1]

    def body(q_ref, k_ref, v_ref, kp_ref, vp_ref, o_ref, l1_ref, l4_ref, l16_ref, q16, k16, v16, o16,
             acc, m_s, l_s, k16p, v16p, acc16, m16, l16, tmp):
        n = pl.program_id(1)
        qi = lax.broadcasted_iota(jnp.int32, (KEY_BLOCK, KEY_BLOCK), 0)
        ki = lax.broadcasted_iota(jnp.int32, (KEY_BLOCK, KEY_BLOCK), 1)
        bias_own = jnp.where(ki <= qi, 0.0, NEG_INF)
        bias_before = jnp.where(ki >= qi, 0.0, NEG_INF)
        bias_mid = jnp.concatenate([bias_before, bias_own], axis=1)
        bias_first = jnp.concatenate([jnp.where(n > 0, bias_before, NEG_INF), bias_own], axis=1)
        ones = jnp.ones((2 * KEY_BLOCK, HEAD_DIM), BF16)
        diag = qi == ki

        @pl.when(n == 0)
        def _():
            k16p[...] = jnp.zeros_like(k16p)
            v16p[...] = jnp.zeros_like(v16p)

        def unit(qf, kpb, kb, vpb, vb, bias, state, rows, first):
            acc_r, m_r, l_r = state
            kcat = jnp.concatenate([kpb, kb], axis=0)
            vaug = jnp.concatenate([jnp.concatenate([vpb, vb], axis=0), ones], axis=1)
            s = _dot_nt(qf.astype(BF16), kcat) + bias
            mx = jnp.max(s, axis=-1, keepdims=True)
            if first:
                m_new = jnp.broadcast_to(mx, (KEY_BLOCK, HEAD_DIM))
            else:
                m_old = m_r[rows, :]
                m_new = jnp.maximum(m_old, mx)
            pv = _dot(jnp.exp(s - jnp.concatenate([m_new, m_new], axis=1)).astype(BF16), vaug)
            if first:
                acc_r[rows, :] = pv[:, :HEAD_DIM]
                l_r[rows, :] = pv[:, HEAD_DIM:]
            else:
                alpha = jnp.exp(m_old - m_new)
                acc_r[rows, :] = alpha * acc_r[rows, :] + pv[:, :HEAD_DIM]
                l_r[rows, :] = alpha * l_r[rows, :] + pv[:, HEAD_DIM:]
            m_r[rows, :] = m_new

        for gi, dil in enumerate(DILATIONS[:-1]):
            nb = nblk // dil
            for r in range(dil):
                prow = _unit_rows(dil, r, nb - 1)
                kpb, vpb = kp_ref[prow, :].astype(BF16), vp_ref[prow, :].astype(BF16)
                for j in range(nb):
                    rows = _unit_rows(dil, r, j)
                    kb, vb = k_ref[rows, :].astype(BF16), v_ref[rows, :].astype(BF16)
                    unit(q_ref[rows, :], kpb, kb, vpb, vb, bias_first if j == 0 else bias_mid,
                         (acc, m_s, l_s), rows, gi == 0)
                    kpb, vpb = kb, vb

        for src, dst in ((q_ref, q16), (k_ref, k16), (v_ref, v16), (acc, acc16), (m_s, m16), (l_s, l16)):
            _to_residue_major(src, tmp, dst)
        for r in range(wide):
            rows = pl.ds(r * KEY_BLOCK, KEY_BLOCK)
            unit(q16[rows, :], k16p[rows, :].astype(BF16), k16[rows, :].astype(BF16), v16p[rows, :].astype(BF16),
                 v16[rows, :].astype(BF16), bias_first, (acc16, m16, l16), rows, False)
        k16p[...] = k16[...]
        v16p[...] = v16[...]

        den = l16[...]
        o16[...] = acc16[...] * (1.0 / den)
        m16[...] = m16[...] + jnp.log(den)
        _from_residue_major(o16, tmp, o_ref, False)
        _from_residue_major(m16, tmp, m_s, False)

        def lse_row(ref, rows):
            return jnp.sum(jnp.where(diag, ref[rows, :], 0.0), axis=0, keepdims=True)

        for dil, out in zip(DILATIONS[:-1], (l1_ref, l4_ref)):
            nb = nblk // dil
            for r in range(dil):
                for j in range(nb):
                    out[r * nb + j:r * nb + j + 1, :] = lse_row(m_s, _unit_rows(dil, r, j))
        for r in range(wide):
            l16_ref[r:r + 1, :] = lse_row(m16, pl.ds(r * KEY_BLOCK, KEY_BLOCK))

    cur = lambda piece: pl.BlockSpec((SPAN, HEAD_DIM), lambda h, n: (n, piece * nh + h))
    before = lambda piece: pl.BlockSpec((SPAN, HEAD_DIM), lambda h, n: (jnp.maximum(n - 1, 0), piece * nh + h))
    blk = pl.BlockSpec((None, SPAN, HEAD_DIM), lambda h, n: (h, n, 0))
    lblk = pl.BlockSpec((None, nblk, KEY_BLOCK), lambda h, n: (h, n, 0))
    lshape = jax.ShapeDtypeStruct((nh, seq // KEY_BLOCK, KEY_BLOCK), F32)
    full = jax.ShapeDtypeStruct((nh, seq, HEAD_DIM), F32)
    o, l1, l4, l16, *major = pl.pallas_call(
        body, name="attn_fwd",
        out_shape=[full, lshape, lshape, lshape] + [full] * 4,
        grid=(nh, nchunk), in_specs=[cur(2), cur(3), cur(4), before(3), before(4)],
        out_specs=[blk, lblk, lblk, lblk] + [blk] * 4,
        scratch_shapes=[pltpu.VMEM((SPAN, HEAD_DIM), F32)] * 9,
        compiler_params=_params(("arbitrary", "arbitrary"), VMEM_LIMIT),
    )(proj, proj, proj, proj, proj)
    return o, (l1, l4, l16), tuple(major)


def _to_residue_major(src, tmp, dst):
    quarter = SPAN // 4
    for r4 in range(4):
        tmp[r4 * quarter:(r4 + 1) * quarter, :] = src[pl.ds(r4, quarter, stride=4), :]
    for r4 in range(4):
        for rp in range(4):
            r = r4 + 4 * rp
            dst[r * KEY_BLOCK:(r + 1) * KEY_BLOCK, :] = tmp[pl.ds(r4 * quarter + rp, KEY_BLOCK, stride=4), :]


def _from_residue_major(src, tmp, dst, add):
    quarter = SPAN // 4
    for r4 in range(4):
        for rp in range(4):
            r = r4 + 4 * rp
            tmp[pl.ds(r4 * quarter + rp, KEY_BLOCK, stride=4), :] = src[r * KEY_BLOCK:(r + 1) * KEY_BLOCK, :]
    for r4 in range(4):
        rows = pl.ds(r4, quarter, stride=4)
        part = tmp[r4 * quarter:(r4 + 1) * quarter, :]
        dst[rows, :] = dst[rows, :] + part if add else part


def _attn_bwd(proj, do, o, lses, major, cosf, sinf):
    nh, seq = N_HEADS, proj.shape[0]
    nchunk = seq // SPAN
    nblk = SPAN // KEY_BLOCK
    wide = DILATIONS[-1]
    assert SPAN == wide * KEY_BLOCK

    def body(q_ref, k_ref, v_ref, do_ref, o_ref, kp_ref, vp_ref, q16, k16, v16, o16, l1_ref, l4_ref, l16_ref,
             cos_ref, sin_ref, cosp_ref, sinp_ref, dq_ref, dk_ref, dv_ref,
             dq_acc, dkc_acc, dvc_acc, dkp_acc, dvp_acc, do16, k16p, v16p,
             dq16, dkc16, dvc16, dkp16, dvp16, tmp, pt_s, ds_s, kcat_s, qb_s, dob_s):
        n = pl.program_id(1)
        ki = lax.broadcasted_iota(jnp.int32, (KEY_BLOCK, KEY_BLOCK), 0)
        qi = lax.broadcasted_iota(jnp.int32, (KEY_BLOCK, KEY_BLOCK), 1)
        bias_own = jnp.where(ki <= qi, 0.0, NEG_INF)
        bias_before = jnp.where(ki >= qi, 0.0, NEG_INF)
        bias_mid = jnp.concatenate([bias_before, bias_own], axis=0)
        bias_first = jnp.concatenate([jnp.where(n > 0, bias_before, NEG_INF), bias_own], axis=0)
        ones8 = jnp.ones((SUBLANES, HEAD_DIM), BF16)

        def row_dot(a, b):
            prod = a * b
            hi = prod.astype(BF16)
            lo = (prod - hi.astype(F32)).astype(BF16)
            return (_dot_nt(ones8, hi) + _dot_nt(ones8, lo))[0:1, :]

        def group(units, srcs, before, l_ref, accs):
            src_q, src_do, src_o, src_k, src_v = srcs
            before_k, before_v = before
            acc_q, acc_kc, acc_vc, acc_kp, acc_vp = accs
            kb = vb = None
            for u, (rows, prow, outside, lrow, _) in enumerate(units):
                dof = src_do[rows, :]
                qb, dob = src_q[rows, :].astype(BF16), dof.astype(BF16)
                kpb, vpb = (before_k[prow, :].astype(BF16), before_v[prow, :].astype(BF16)) if outside else (kb, vb)
                kb, vb = src_k[rows, :].astype(BF16), src_v[rows, :].astype(BF16)
                kcat = jnp.concatenate([kpb, kb], axis=0)
                vcat = jnp.concatenate([vpb, vb], axis=0)
                bias = bias_first if outside else bias_mid
                pt = jnp.exp(_dot_nt(kcat, qb) + bias - l_ref[lrow:lrow + 1, :])
                dst = pt * (_dot_nt(vcat, dob) - row_dot(dof, src_o[rows, :]))
                pt_s[u], ds_s[u], kcat_s[u], qb_s[u], dob_s[u] = pt.astype(BF16), dst.astype(BF16), kcat, qb, dob
            for u, (rows, _, _, _, _) in enumerate(units):
                acc_q[rows, :] += _dot_tn(ds_s[u], kcat_s[u])
            for u, (rows, prow, outside, _, nxt) in enumerate(units):
                dk = _dot(ds_s[u, KEY_BLOCK:, :], qb_s[u])
                dv = _dot(pt_s[u, KEY_BLOCK:, :], dob_s[u])
                if nxt is not None:
                    dk = dk + _dot(ds_s[nxt, :KEY_BLOCK, :], qb_s[nxt])
                    dv = dv + _dot(pt_s[nxt, :KEY_BLOCK, :], dob_s[nxt])
                acc_kc[rows, :] += dk
                acc_vc[rows, :] += dv
                if outside:
                    acc_kp[prow, :] += _dot(ds_s[u, :KEY_BLOCK, :], qb_s[u])
                    acc_vp[prow, :] += _dot(pt_s[u, :KEY_BLOCK, :], dob_s[u])

        @pl.when(n == 0)
        def _():
            for ref in (dkp_acc, dvp_acc, dkp16, dvp16, k16p, v16p):
                ref[...] = jnp.zeros_like(ref)

        @pl.when(n < nchunk)
        def _():
            for ref in (dq_acc, dkc_acc, dvc_acc, dq16, dkc16, dvc16):
                ref[...] = jnp.zeros_like(ref)
            _to_residue_major(do_ref, tmp, do16)
            natural = (q_ref, do_ref, o_ref, k_ref, v_ref)
            for dil, l_ref in zip(DILATIONS[:-1], (l1_ref, l4_ref)):
                nb = nblk // dil
                units = [(_unit_rows(dil, r, j), _unit_rows(dil, r, (j - 1) % nb), j == 0, r * nb + j,
                          r * nb + j + 1 if j + 1 < nb else None) for r in range(dil) for j in range(nb)]
                group(units, natural, (kp_ref, vp_ref), l_ref, (dq_acc, dkc_acc, dvc_acc, dkp_acc, dvp_acc))
            blocks = [pl.ds(r * KEY_BLOCK, KEY_BLOCK) for r in range(wide)]
            group([(rows, rows, True, r, None) for r, rows in enumerate(blocks)], (q16, do16, o16, k16, v16),
                  (k16p, v16p), l16_ref, (dq16, dkc16, dvc16, dkp16, dvp16))
            _from_residue_major(dq16, tmp, dq_acc, True)
            dq = dq_acc[...]
            dq_ref[...] = ((dq * cos_ref[...] - _rope_partner(dq) * sin_ref[...]) * ATTN_SCALE).astype(BF16)

        @pl.when(n > 0)
        def _():
            _from_residue_major(dkp16, tmp, dkp_acc, True)
            _from_residue_major(dvp16, tmp, dvp_acc, True)
            dk = dkp_acc[...]
            dk_ref[...] = (dk * cosp_ref[...] - _rope_partner(dk) * sinp_ref[...]).astype(BF16)
            dv_ref[...] = dvp_acc[...].astype(BF16)

        @pl.when(n < nchunk)
        def _():
            for src, dst in ((dkc_acc, dkp_acc), (dvc_acc, dvp_acc), (dkc16, dkp16), (dvc16, dvp16),
                             (k16, k16p), (v16, v16p)):
                dst[...] = src[...]

    last = nchunk - 1
    cur = lambda h, n: (h, jnp.minimum(n, last), 0)
    prev = lambda h, n: (h, jnp.clip(n - 1, 0, last), 0)
    blk = lambda idx: pl.BlockSpec((None, SPAN, HEAD_DIM), idx)
    lblk = pl.BlockSpec((None, nblk, KEY_BLOCK), cur)
    tab = pl.BlockSpec((SPAN, HEAD_DIM), lambda h, n: (jnp.minimum(n, last), 0))
    tabp = pl.BlockSpec((SPAN, HEAD_DIM), lambda h, n: (jnp.clip(n - 1, 0, last), 0))
    out_q = pl.BlockSpec((SPAN, HEAD_DIM), lambda h, n: (jnp.minimum(n, last), h))
    out_kv = pl.BlockSpec((SPAN, HEAD_DIM), lambda h, n: (jnp.clip(n - 1, 0, last), h))
    shape = jax.ShapeDtypeStruct((seq, nh * HEAD_DIM), BF16)
    tok = lambda piece, row: pl.BlockSpec((SPAN, HEAD_DIM), lambda h, n: (row(n), piece * nh + h))
    row_cur, row_prev = (lambda n: jnp.minimum(n, last)), (lambda n: jnp.clip(n - 1, 0, last))
    return pl.pallas_call(
        body, name="attn_bwd", out_shape=[shape, shape, shape], grid=(nh, nchunk + 1),
        in_specs=[tok(2, row_cur), tok(3, row_cur), tok(4, row_cur), blk(cur), blk(cur),
                  tok(3, row_prev), tok(4, row_prev)] + [blk(cur)] * 4 + [lblk] * 3 + [tab, tab, tabp, tabp],
        out_specs=[out_q, out_kv, out_kv],
        scratch_shapes=[pltpu.VMEM((SPAN, HEAD_DIM), F32)] * 14
                       + [pltpu.VMEM((nblk, 2 * KEY_BLOCK, HEAD_DIM), BF16)] * 3
                       + [pltpu.VMEM((nblk, KEY_BLOCK, HEAD_DIM), BF16)] * 2,
        compiler_params=_params(("arbitrary", "arbitrary"), VMEM_LIMIT),
    )(proj, proj, proj, do, o, proj, proj, *major, *lses, cosf, sinf, cosf, sinf)


def _hub(x, tgt, hr, pf, o_hm, mod, b_mod, b_gate, g_final, w_out_rnn, w_out_attn, w_o):
    seq = x.shape[0]
    tm = HUB_ROWS
    nsteps = seq // tm

    def body(x_ref, t_ref, hr_ref, zr_ref, za_ref, gr_ref, ga_ref, o_ref, mod_ref, bmod_ref, bg_ref, gf_ref,
             wr_hbm, wa_hbm, wo_hbm,
             dx2_ref, dhr_ref, dzr_ref, do_ref, dza_ref, dgr_ref, dga_ref,
             ur_ref, dyr_ref, ua_ref, dya_ref, mg_ref, dmo_ref,
             ggf_ref, gbg_ref, dgate_ref, loss_ref,
             wr, wa, wo, sem):
        step = pl.program_id(0)

        @pl.when(step == 0)
        def _():
            for src, dst in ((wr_hbm, wr), (wa_hbm, wa), (wo_hbm, wo)):
                cp = pltpu.make_async_copy(src, dst, sem)
                cp.start()
                cp.wait()
            for ref in (ggf_ref, gbg_ref, dgate_ref, loss_ref):
                ref[...] = jnp.zeros_like(ref)

        gate = mod_ref[:, 2 * D_MODEL:] + bmod_ref[:, 2 * D_MODEL:]
        gfin = gf_ref[...]
        hr_t, zr, za = hr_ref[...], zr_ref[...], za_ref[...]
        o = jnp.concatenate([o_ref[hh] for hh in range(N_HEADS)], axis=1)
        sig_zr, sig_za = _sigmoid(zr), _sigmoid(za)
        silu_zr, silu_za = zr * sig_zr, za * sig_za
        u_rnn = (hr_t * silu_zr).astype(BF16)
        u_attn = (o * silu_za).astype(BF16)
        y_rnn = _dot(u_rnn, wr[...])
        y_attn = _dot(u_attn, wa[...])
        sr = _sigmoid(gr_ref[...] + bg_ref[:, :D_MODEL])
        sa = _sigmoid(ga_ref[...] + bg_ref[:, D_MODEL:])
        merged = (sr * y_rnn + sa * y_attn).astype(BF16)
        mo = _dot(merged, wo[...])
        x2 = x_ref[...] + gate * mo
        rstd = lax.rsqrt(jnp.mean(x2 * x2, axis=-1, keepdims=True) + NORM_EPS)
        xn = x2 * rstd
        err = xn * gfin - t_ref[...]
        loss_ref[...] += 0.5 * jnp.sum(jnp.sum(err * err, axis=-1, keepdims=True) * (1.0 / D_MODEL),
                                       axis=0, keepdims=True)

        dy = err * (1.0 / D_MODEL)
        ggf_ref[...] += jnp.sum(dy * xn, axis=0, keepdims=True)
        dxn = dy * gfin
        dx2 = rstd * (dxn - xn * jnp.mean(dxn * xn, axis=-1, keepdims=True))
        dx2_ref[...] = dx2
        dgate_ref[...] += jnp.sum(dx2 * mo, axis=0, keepdims=True)
        dmo = (dx2 * gate).astype(BF16)
        dmerged = _dot_nt(dmo, wo[...])
        mg_ref[...] = merged
        dmo_ref[...] = dmo
        dy_rnn = (dmerged * sr).astype(BF16)
        dy_attn = (dmerged * sa).astype(BF16)
        dg_r = dmerged * y_rnn * sr * (1.0 - sr)
        dg_a = dmerged * y_attn * sa * (1.0 - sa)
        dgr_ref[...] = dg_r.astype(BF16)
        dga_ref[...] = dg_a.astype(BF16)
        gbg_ref[:, :D_MODEL] += jnp.sum(dg_r, axis=0, keepdims=True)
        gbg_ref[:, D_MODEL:] += jnp.sum(dg_a, axis=0, keepdims=True)
        du_rnn = _dot_nt(dy_rnn, wr[...])
        du_attn = _dot_nt(dy_attn, wa[...])
        ur_ref[...] = u_rnn
        dyr_ref[...] = dy_rnn
        ua_ref[...] = u_attn
        dya_ref[...] = dy_attn
        dhr_ref[...] = du_rnn * silu_zr
        dzr_ref[...] = (du_rnn * hr_t * (sig_zr * (1.0 + zr * (1.0 - sig_zr)))).astype(BF16)
        dza_ref[...] = (du_attn * o * (sig_za * (1.0 + za * (1.0 - sig_za)))).astype(BF16)
        d_o = du_attn * silu_za
        for hh in range(N_HEADS):
            do_ref[hh] = d_o[:, hh * HEAD_DIM:(hh + 1) * HEAD_DIM]

    row = pl.BlockSpec((tm, D_MODEL), lambda i: (i, 0))
    piece = lambda slot: pl.BlockSpec((tm, D_MODEL), lambda i: (i, slot))
    hm = pl.BlockSpec((N_HEADS, tm, HEAD_DIM), lambda i: (0, i, 0))
    const = lambda cols: pl.BlockSpec((1, cols), lambda i: (0, 0))
    any_spec = pl.BlockSpec(memory_space=pl.ANY)
    act_f32 = jax.ShapeDtypeStruct((seq, D_MODEL), F32)
    act_bf16 = jax.ShapeDtypeStruct((seq, D_MODEL), BF16)
    return pl.pallas_call(
        body, name="hub",
        out_shape=[act_f32, act_f32, act_bf16, jax.ShapeDtypeStruct((N_HEADS, seq, HEAD_DIM), F32),
                   act_bf16, act_bf16, act_bf16] + [act_bf16] * 6 + [
                   jax.ShapeDtypeStruct((1, D_MODEL), F32), jax.ShapeDtypeStruct((1, 2 * D_MODEL), F32),
                   jax.ShapeDtypeStruct((1, D_MODEL), F32), jax.ShapeDtypeStruct((1, 1), F32)],
        grid=(nsteps,),
        in_specs=[row, row, row, piece(1), piece(5), piece(6), piece(7), hm,
                  const(3 * D_MODEL), const(3 * D_MODEL), const(2 * D_MODEL), const(D_MODEL),
                  any_spec, any_spec, any_spec],
        out_specs=[row, row, row, hm, row, row, row] + [row] * 6 + [
                   const(D_MODEL), const(2 * D_MODEL), const(D_MODEL), const(1)],
        scratch_shapes=[pltpu.VMEM((D_MODEL, D_MODEL), BF16)] * 3 + [pltpu.SemaphoreType.DMA],
        compiler_params=_params(("arbitrary",), VMEM_LIMIT),
    )(x, tgt, hr, pf, pf, pf, pf, o_hm, mod, b_mod, b_gate, g_final, w_out_rnn, w_out_attn, w_o)


def _pair_grads(name, lefts, rights):
    n = len(rights)
    shared = len(lefts) == 1
    seq = rights[0].shape[0]
    tk = WGRAD_ROWS
    nk = seq // tk

    def body(*refs):
        l_refs, r_refs = refs[:len(lefts)], refs[len(lefts):len(lefts) + n]
        out_ref, low_ref = refs[len(lefts) + n:]
        j, kk = pl.program_id(0), pl.program_id(1)

        @pl.when(kk == 0)
        def _():
            out_ref[...] = jnp.zeros_like(out_ref)

        for m in range(n):
            @pl.when(j == m)
            def _(m=m):
                out_ref[...] += _dot_tn(l_refs[0 if shared else m][...], r_refs[m][...])

        @pl.when(kk == nk - 1)
        def _():
            low_ref[...] = out_ref[...].astype(BF16)

    def spec(m):
        return pl.BlockSpec((tk, D_MODEL), lambda j, kk: (jnp.where(j == m, kk, jnp.where(j < m, 0, nk - 1)), 0))

    left_specs = [pl.BlockSpec((tk, D_MODEL), lambda j, kk: (kk, 0))] if shared else [spec(m) for m in range(n)]
    out_spec = pl.BlockSpec((None, D_MODEL, D_MODEL), lambda j, kk: (j, 0, 0))
    return pl.pallas_call(
        body, name=name,
        out_shape=[jax.ShapeDtypeStruct((n, D_MODEL, D_MODEL), F32), jax.ShapeDtypeStruct((n, D_MODEL, D_MODEL), BF16)],
        grid=(n, nk),
        in_specs=left_specs + [spec(m) for m in range(n)],
        out_specs=[out_spec, out_spec],
        compiler_params=_params(("arbitrary", "arbitrary"), VMEM_LIMIT),
    )(*lefts, *rights)


def _dh_dx(pieces, w_near, w_sib, w_far, x, dx2, mod, b_mod, g_norm):
    seq = x.shape[0]
    tm = DX_ROWS

    def body(*refs):
        p_refs = refs[:8]
        near_hbm, sib_hbm, far_hbm, x_ref, dx2_ref, mod_ref, bmod_ref, g_ref = refs[8:16]
        gx_ref, dshift_ref, dscale_ref, ggn_ref, w_scr, sem = refs[16:]
        step = pl.program_id(0)

        @pl.when(step == 0)
        def _():
            me = _my_pos()
            sib = _flip(me, 1)
            moves = [(near_hbm, _index(_flip(me, 2 * m))) for m in range(4)] + [(sib_hbm, _index(sib))]
            moves += [(far_hbm, _index(_flip(sib, 2 * m))) for m in range(1, 4)]
            loads = [pltpu.make_async_copy(src.at[t], w_scr.at[t], sem.at[i]) for i, (src, t) in enumerate(moves)]
            for cp in loads:
                cp.start()
            for cp in loads:
                cp.wait()
            for ref in (dshift_ref, dscale_ref, ggn_ref):
                ref[...] = jnp.zeros_like(ref)

        dh = _dot_nt(p_refs[0][...], w_scr[0])
        for j in range(1, 8):
            dh = dh + _dot_nt(p_refs[j][...], w_scr[j])
        scale1 = 1.0 + mod_ref[:, D_MODEL:2 * D_MODEL] + bmod_ref[:, D_MODEL:2 * D_MODEL]
        g = g_ref[...]
        xf = x_ref[...]
        rstd_t = lax.rsqrt(jnp.mean(xf * xf, axis=-1, keepdims=True) + NORM_EPS)
        xn = xf * rstd_t
        dshift_ref[...] += jnp.sum(dh, axis=0, keepdims=True)
        dscale_ref[...] += jnp.sum(dh * (xn * g), axis=0, keepdims=True)
        ggn_ref[...] += jnp.sum(dh * scale1 * xn, axis=0, keepdims=True)
        dxn = dh * (g * scale1)
        gx_ref[...] = rstd_t * (dxn - xn * jnp.mean(dxn * xn, axis=-1, keepdims=True)) + dx2_ref[...]

    row = pl.BlockSpec((tm, D_MODEL), lambda i: (i, 0))
    const = lambda cols: pl.BlockSpec((1, cols), lambda i: (0, 0))
    vec = jax.ShapeDtypeStruct((1, D_MODEL), F32)
    return pl.pallas_call(
        body, name="dh_dx",
        out_shape=[jax.ShapeDtypeStruct((seq, D_MODEL), F32), vec, vec, vec],
        grid=(seq // tm,),
        in_specs=[row] * 8 + [pl.BlockSpec(memory_space=pl.ANY)] * 3 + [row, row,
                              const(3 * D_MODEL), const(3 * D_MODEL), const(D_MODEL)],
        out_specs=[row, const(D_MODEL), const(D_MODEL), const(D_MODEL)],
        scratch_shapes=[pltpu.VMEM((8, D_MODEL, D_MODEL), BF16), pltpu.SemaphoreType.DMA((8,))],
        compiler_params=_params(("arbitrary",), VMEM_LIMIT),
    )(*pieces, w_near, w_sib, w_far, x, dx2, mod, b_mod, g_norm)


def _adamw(name, w, g, m, v, recv=None):
    rows, cols = w.shape
    tr = rows if rows <= 256 else 256

    def body(*refs):
        w_ref, g_ref, m_ref, v_ref = refs[:4]
        d_ref, nm_ref, nv_ref = refs[-3:] if recv is None else refs[5:8]
        gv = g_ref[...]
        if recv is not None:
            r_ref, g_out = refs[4], refs[8]
            gv = ((gv + r_ref[0].astype(F32)) + r_ref[1].astype(F32)) + r_ref[2].astype(F32)
            g_out[...] = gv
        nm = ADAM_B1 * m_ref[...] + (1.0 - ADAM_B1) * gv
        nv = ADAM_B2 * v_ref[...] + (1.0 - ADAM_B2) * (gv * gv)
        m_hat = nm / (1.0 - ADAM_B1 ** ADAM_STEP)
        v_hat = nv / (1.0 - ADAM_B2 ** ADAM_STEP)
        d_ref[...] = -ADAM_LR * (m_hat / (jnp.sqrt(v_hat) + ADAM_EPS) + ADAM_WD * w_ref[...])
        nm_ref[...] = nm
        nv_ref[...] = nv

    spec = pl.BlockSpec((tr, cols), lambda i: (i, 0))
    shape = jax.ShapeDtypeStruct((rows, cols), F32)
    if recv is None:
        return pl.pallas_call(
            body, name=name, out_shape=[shape, shape, shape], grid=(rows // tr,),
            in_specs=[spec] * 4, out_specs=[spec] * 3,
            compiler_params=_params(("arbitrary",)),
        )(w, g, m, v)
    return pl.pallas_call(
        body, name=name, out_shape=[shape] * 4, grid=(rows // tr,),
        in_specs=[spec] * 4 + [pl.BlockSpec((3, tr, cols), lambda i: (0, i, 0))], out_specs=[spec] * 4,
        compiler_params=_params(("arbitrary",)),
    )(w, g, m, v, recv)


def kernel(x, c, positions, g_norm, w_mod, b_mod, w_in, b_gate, conv_w, conv_b, w_a, b_a, w_x, b_x, lam, w_out_rnn, w_out_attn, w_o, g_final, loss_target, m_g_norm, m_w_mod, m_b_mod, m_w_in, m_b_gate, m_conv_w, m_conv_b, m_w_a, m_b_a, m_w_x, m_b_x, m_lam, m_w_out_rnn, m_w_out_attn, m_w_o, m_g_final, v_g_norm, v_w_mod, v_b_mod, v_w_in, v_b_gate, v_conv_w, v_conv_b, v_w_a, v_b_a, v_w_x, v_b_x, v_lam, v_w_out_rnn, v_w_out_attn, v_w_o, v_g_final):
    seq = x.shape[1]
    me = _index(_my_pos())
    xs, tgt = x[0], loss_target[0]

    inv_freq = ROPE_THETA ** (-jnp.arange(0, 2 * ROT_HALF, 2, dtype=F32) / (2 * ROT_HALF))
    ang = (positions[0].astype(F32).reshape(seq // SUBLANES, SUBLANES, 1) * inv_freq).reshape(seq // SUBLANES, 128)
    cos, sin = lax.optimization_barrier((jnp.cos(ang), jnp.sin(ang)))
    cos, sin = cos.reshape(seq, ROT_HALF), sin.reshape(seq, ROT_HALF)
    rest = HEAD_DIM - 2 * ROT_HALF
    cosf = jnp.concatenate([cos, cos, jnp.ones((seq, rest), F32)], axis=1)
    sinf = jnp.concatenate([-sin, sin, jnp.zeros((seq, rest), F32)], axis=1)
    keep = (positions[0] != 0).astype(F32)[:, None]

    both = _ag_small("gather_c_conv_w", jnp.concatenate(
        [jnp.broadcast_to(c, (SUBLANES, D_MODEL)), jnp.pad(conv_w[0], ((0, SUBLANES - 4), (0, 0)))], axis=1))
    c_all, conv_w8 = both[:, 0, :D_MODEL], both[:, :, D_MODEL:]
    mod_cols = w_mod.shape[2]
    mod_part = _ag_small("gather_mod", _mod_fwd(c_all, w_mod[0]))
    mod = lax.dynamic_index_in_dim(mod_part, me, axis=1, keepdims=False).reshape(1, N_DEV * mod_cols)

    slot = lambda t: lax.dynamic_update_slice(lax.empty((N_DEV,) + t.shape, t.dtype), t[None], (me, 0, 0))
    w_in_own = w_in[0].astype(BF16)
    mod, w_in_own = lax.optimization_barrier((mod, w_in_own))
    first = _split_start("gather_w_in_start", _own_block_copies, 3, [w_in_own], [slot(w_in_own)])
    swap = _split_start("swap_w_in_start", _sibling_copy, 1, first[2], [lax.empty((N_DEV,) + w_in_own.shape, BF16)])
    mod = mod + swap[4][0:1, 0:1]

    blocks = lambda t: t.reshape(RNN_BLOCKS, 1, 128)
    rnn_params = (conv_w8, blocks(conv_b), w_a[0], blocks(b_a), w_x[0], blocks(b_x), blocks(lam))

    h = _norm(xs, mod, b_mod, g_norm)
    ids = lambda ks: jnp.bitwise_xor(me, jnp.array(ks, jnp.int32)).astype(jnp.int32)
    pf = _proj("proj_own", h, swap[2][0][None], jnp.zeros((1,), jnp.int32), ids([0]), cosf, sinf, None)
    own_thru, (w_in_sib,) = _split_wait("swap_w_in_wait", _sibling_copy, swap, pf)
    pf = _proj("proj_sibling", h, w_in_sib, ids([1]), ids([1]), cosf, sinf, pf)
    _, (w_in_near,) = _split_wait("gather_w_in_wait", _own_block_copies, (first[0], first[1], own_thru, first[3], None), pf)
    second = _split_start("forward_w_in_start", _forward_copies, 3, [w_in_near],
                          [lax.empty(w_in_near.shape, w_in_near.dtype)])
    near = ids([2, 4, 6])
    pf = _proj("proj_near", h, second[2][0], near, near, cosf, sinf, pf)
    (w_in_near,), (w_in_far,) = _split_wait("forward_w_in_wait", _forward_copies, second, pf)
    far = ids([3, 5, 7])
    pf = _proj("proj_far", h, w_in_far, far, far, cosf, sinf, pf)
    late = [w_out_rnn[0].astype(BF16), w_out_attn[0].astype(BF16), w_o[0].astype(BF16)]
    pf, late = lax.optimization_barrier((pf, late))
    flight = _split_start("gather_out_weights_start", _peer_copies, 7 * len(late), late, [slot(t) for t in late])
    rnn_params = (rnn_params[0], rnn_params[1] + flight[4][0:1, 0:1]) + rnn_params[2:]
    hr, rnn_saved = _rnn_fwd(pf, keep, *rnn_params)
    o, lses, major = _attn_fwd(pf)

    w_or_all, w_oa_all, w_o_all = (t.reshape(D_MODEL, D_MODEL) for t in _split_wait(
        "gather_out_weights_wait", _peer_copies, flight, o)[1])
    (dx2, dhr, dz_rnn, d_o, dz_attn, dg_r, dg_a, u_rnn, dy_rnn, u_attn, dy_attn, merged, dmo,
     gp_g_final, gp_b_gate, dgate, loss_part) = _hub(
        xs, tgt, hr, pf, o, mod, b_mod, b_gate, g_final.reshape(1, D_MODEL), w_or_all, w_oa_all, w_o_all)
    gp_out, gp_out_low = _pair_grads("out_grads", [u_rnn, u_attn, merged], [dy_rnn, dy_attn, dmo])
    dq, dk, dv = _attn_bwd(pf, d_o, o, lses, major, cosf, sinf)
    dx_rnn, gp_conv_w, gp_conv_b, gp_w_a, gp_b_a, gp_w_x, gp_b_x, gp_lam = _rnn_bwd(
        pf, hr, dhr, rnn_saved, keep, rnn_params[0], rnn_params[2], rnn_params[4], rnn_params[6])
    pieces = [dx_rnn, dz_rnn, dq, dk, dv, dz_attn, dg_r, dg_a]
    gp_w_in, gp_w_in_low = _pair_grads("w_in_grad", [h], pieces)

    by_target = lambda t: [(t.reshape(3, N_DEV, 128, D_MODEL), i) for i in range(3)]
    stacks = [(gp_w_in, None)] + by_target(gp_out)
    from_sib = _rs_to_sibling("rs_sibling", [(gp_w_in_low, None)] + by_target(gp_out_low))
    targets = jnp.bitwise_xor(me, 2 * jnp.arange(4, dtype=jnp.int32)).astype(jnp.int32)
    sums = [_add_sibling("rs_add_sibling_%d" % a, s_, r_, targets) for a, (s_, r_) in enumerate(zip(stacks, from_sib))]
    sends = [send for _, send in sums]
    reduce_flight = _split_start("rs_chips_start", _chip_copies, 3 * len(sends), sends,
                                 [lax.empty(t.shape, t.dtype) for t in sends])

    mod_after = mod + reduce_flight[4][0:1, 0:1]
    grad_x, dshift, dscale, gp_g_norm = _dh_dx(pieces, w_in_near, w_in_sib, w_in_far, xs, dx2, mod_after, b_mod,
                                               g_norm)

    flat = lambda t: t.reshape(-1, 128)
    dmod = flat(jnp.concatenate([dshift, dscale, dgate], axis=1))
    dmod_placed = lax.dynamic_update_slice(jnp.zeros((N_DEV * dmod.shape[0], 128), F32), dmod, (me * dmod.shape[0], 0))
    small = [flat(gp_g_norm), flat(gp_b_gate), flat(gp_conv_b), flat(gp_b_a), flat(gp_b_x), flat(gp_lam),
             flat(gp_g_final), flat(gp_conv_w), jnp.broadcast_to(loss_part, (SUBLANES, 128)),
             flat(gp_w_a), flat(gp_w_x), dmod_placed]
    sizes = [t.shape[0] for t in small]
    small.append(jnp.zeros((-sum(sizes) % (2 * SUBLANES), 128), F32))
    total = _allreduce_small("allreduce_small_grads", jnp.concatenate(small, axis=0))
    offs = [sum(sizes[:i]) for i in range(len(sizes))]
    (g_g_norm, g_b_gate, g_conv_b, g_b_a, g_b_x, g_lam, g_g_final, g_conv_w_all, loss_rows, g_w_a, g_w_x,
     dmod_rows) = (total[o_:o_ + s_] for o_, s_ in zip(offs, sizes))
    loss = loss_rows[0, 0]
    g_conv_w = lax.dynamic_index_in_dim(g_conv_w_all.reshape(RNN_BLOCKS, SUBLANES, 128), me, axis=0,
                                        keepdims=False)[:4]

    dmod_all = dmod_rows.reshape(N_DEV, 3 * D_MODEL)
    dmod_cols = lax.dynamic_slice_in_dim(dmod_all, me * mod_cols, mod_cols, axis=1)
    g_b_mod, g_w_mod = _mod_bwd(c_all, dmod_all, dmod_cols)

    _, from_chips = _split_wait("rs_chips_wait", _chip_copies, reduce_flight, total)

    results = {}
    sharded = (("w_in", w_in, m_w_in, v_w_in, (D_MODEL, D_MODEL)),
               ("w_out_rnn", w_out_rnn, m_w_out_rnn, v_w_out_rnn, (128, D_MODEL)),
               ("w_out_attn", w_out_attn, m_w_out_attn, v_w_out_attn, (128, D_MODEL)),
               ("w_o", w_o, m_w_o, v_w_o, (128, D_MODEL)))
    for (name, w_, m_, v_, shape2), (own, _), arrived in zip(sharded, sums, from_chips):
        d_, nm_, nv_, g_ = _adamw("adamw_" + name, w_.reshape(shape2), own, m_.reshape(shape2), v_.reshape(shape2),
                                  arrived)
        results[name] = (g_, d_, nm_, nv_)
    shape2 = (D_MODEL, mod_cols)
    results["w_mod"] = (g_w_mod,) + tuple(_adamw("adamw_w_mod", w_mod.reshape(shape2), g_w_mod,
                                                 m_w_mod.reshape(shape2), v_w_mod.reshape(shape2)))
    lanes = (("g_norm", g_norm, g_g_norm, m_g_norm, v_g_norm), ("b_mod", b_mod, g_b_mod, m_b_mod, v_b_mod),
             ("b_gate", b_gate, g_b_gate, m_b_gate, v_b_gate), ("conv_w", conv_w, g_conv_w, m_conv_w, v_conv_w),
             ("conv_b", conv_b, g_conv_b, m_conv_b, v_conv_b), ("w_a", w_a, g_w_a, m_w_a, v_w_a),
             ("b_a", b_a, g_b_a, m_b_a, v_b_a), ("w_x", w_x, g_w_x, m_w_x, v_w_x), ("b_x", b_x, g_b_x, m_b_x, v_b_x),
             ("lam", lam, g_lam, m_lam, v_lam), ("g_final", g_final, g_g_final, m_g_final, v_g_final))
    for name, w_, g_, m_, v_ in lanes:
        rows128 = lambda t: t.reshape(-1, 128)
        results[name] = (g_,) + tuple(_adamw("adamw_" + name, rows128(w_), rows128(g_), rows128(m_), rows128(v_)))
    order = ("g_norm", "w_mod", "b_mod", "w_in", "b_gate", "conv_w", "conv_b", "w_a", "b_a", "w_x", "b_x", "lam",
             "w_out_rnn", "w_out_attn", "w_o", "g_final")
    given = dict(g_norm=g_norm, w_mod=w_mod, b_mod=b_mod, w_in=w_in, b_gate=b_gate, conv_w=conv_w, conv_b=conv_b,
                 w_a=w_a, b_a=b_a, w_x=w_x, b_x=b_x, lam=lam, w_out_rnn=w_out_rnn, w_out_attn=w_out_attn, w_o=w_o,
                 g_final=g_final)
    outs = [[results[name][k].reshape(given[name].shape) for name in order] for k in range(4)]
    return (loss, grad_x[None], *outs[0], *outs[1], *outs[2], *outs[3])
```

```python
import jax
import jax.numpy as jnp
from jax import lax
from jax.experimental import pallas as pl
from jax.experimental.pallas import tpu as pltpu

F32 = jnp.float32
BF16 = jnp.bfloat16
MESH = pl.DeviceIdType.MESH

D_MODEL = 1024
N_HEADS = 8
HEAD_DIM = 128
RNN_BLOCKS = 8
N_DEV = 8
ROT_HALF = 16
ROPE_THETA = 500000.0
DILATIONS = (1, 4, 16)
KEY_BLOCK = 128
SPAN = KEY_BLOCK * DILATIONS[-1]
ATTN_SCALE = HEAD_DIM ** -0.5
NORM_EPS = 1e-6
LRU_C = 8.0
NEG_INF = -1e30
ADAM_LR, ADAM_B1, ADAM_B2, ADAM_EPS, ADAM_WD, ADAM_STEP = 0.001, 0.9, 0.999, 1e-08, 0.01, 10

SUBLANES = 8
VMEM_LIMIT = 56 * 1024 * 1024
PROJ_ROWS = 1024
RNN_ROWS = 2048
HUB_ROWS = 256
DX_ROWS = 512
WGRAD_ROWS = 1024
ADD_ROWS = 256


def _params(sem=None, vmem=None):
    return pltpu.CompilerParams(dimension_semantics=sem, vmem_limit_bytes=vmem)


def _dot(a, b):
    return jnp.dot(a, b, preferred_element_type=F32)


def _dot_nt(a, b):
    return lax.dot_general(a, b, (((1,), (1,)), ((), ())), preferred_element_type=F32)


def _dot_tn(a, b):
    return lax.dot_general(a, b, (((0,), (0,)), ((), ())), preferred_element_type=F32)


def _sigmoid(z):
    return 1.0 / (1.0 + jnp.exp(-z))


def _expm1_nonpos(z, exp_z):
    return jnp.where(z > -0.01, z * (1.0 + 0.5 * z), exp_z - 1.0)


def _my_pos():
    return lax.axis_index("x"), lax.axis_index("y"), lax.axis_index("c")


def _flip(pos, k):
    x, y, c = pos
    return ((1 - x) if k & 4 else x, (1 - y) if k & 2 else y, (1 - c) if k & 1 else c)


def _index(pos):
    return 4 * pos[0] + 2 * pos[1] + pos[2]


def _ag_small(name, v):
    rows, cols = v.shape

    def body(v_ref, out_ref, send_sems, recv_sems):
        me = _my_pos()
        out_ref[_index(me)] = v_ref[...]
        sends = []
        for k in range(1, N_DEV):
            cp = pltpu.make_async_remote_copy(
                src_ref=v_ref, dst_ref=out_ref.at[_index(me)], send_sem=send_sems.at[k - 1],
                recv_sem=recv_sems.at[k - 1], device_id=_flip(me, k), device_id_type=MESH)
            cp.start()
            sends.append(cp)
        for k in range(1, N_DEV):
            peer = _flip(me, k)
            pltpu.make_async_remote_copy(
                src_ref=v_ref, dst_ref=out_ref.at[_index(peer)], send_sem=send_sems.at[k - 1],
                recv_sem=recv_sems.at[k - 1], device_id=peer, device_id_type=MESH).wait_recv()
        for cp in sends:
            cp.wait_send()

    return pl.pallas_call(
        body, name=name,
        out_shape=jax.ShapeDtypeStruct((N_DEV, rows, cols), v.dtype),
        in_specs=[pl.BlockSpec(memory_space=pltpu.VMEM)],
        out_specs=pl.BlockSpec(memory_space=pltpu.VMEM),
        scratch_shapes=[pltpu.SemaphoreType.DMA((N_DEV - 1,)), pltpu.SemaphoreType.DMA((N_DEV - 1,))],
        compiler_params=_params(None, VMEM_LIMIT),
    )(v)


def _split_start(name, make_copies, nsem, srcs, lands):
    n, k = len(srcs), len(lands)

    def body(*refs):
        for cp in make_copies(refs[:n], refs[n:n + k], refs[n + k], refs[n + k + 1]):
            cp.start()
        refs[-1][...] = jnp.zeros_like(refs[-1])

    hbm = pl.BlockSpec(memory_space=pltpu.HBM)
    sem = pl.BlockSpec(memory_space=pltpu.SEMAPHORE)
    arrays = [*srcs, *lands]
    outs = pl.pallas_call(
        body, name=name,
        out_shape=(pltpu.SemaphoreType.DMA((nsem,)), pltpu.SemaphoreType.DMA((nsem,)),
                   *[pltpu.HBM(t.shape, t.dtype) for t in arrays], jax.ShapeDtypeStruct((SUBLANES, 128), F32)),
        in_specs=[hbm] * (n + k),
        out_specs=(sem, sem, *[hbm] * (n + k), pl.BlockSpec(memory_space=pltpu.VMEM)),
        input_output_aliases={i: 2 + i for i in range(n + k)},
        compiler_params=pltpu.CompilerParams(has_side_effects=pltpu.SideEffectType.DATAFLOW_SIDE_EFFECTING),
    )(*[pltpu.with_memory_space_constraint(t, pltpu.HBM) for t in arrays])
    return outs[0], outs[1], outs[2:2 + n], outs[2 + n:2 + n + k], outs[-1]


def _split_wait(name, make_copies, flight, after):
    send_sems, recv_sems, srcs, lands, _ = flight
    n, k = len(srcs), len(lands)

    def body(*refs):
        for cp in make_copies(refs[:n], refs[n:n + k], refs[n + k], refs[n + k + 1]):
            cp.wait_send()
            cp.wait_recv()

    hbm = pl.BlockSpec(memory_space=pltpu.HBM)
    sem = pl.BlockSpec(memory_space=pltpu.SEMAPHORE)
    arrays = [*srcs, *lands]
    outs = pl.pallas_call(
        body, name=name, out_shape=tuple(pltpu.HBM(t.shape, t.dtype) for t in arrays),
        in_specs=[hbm] * (n + k) + [sem, sem, pl.BlockSpec(memory_space=pl.ANY)],
        out_specs=[hbm] * (n + k),
        input_output_aliases={i: i for i in range(n + k)},
        compiler_params=pltpu.CompilerParams(has_side_effects=pltpu.SideEffectType.DATAFLOW_SIDE_EFFECTING),
    )(*arrays, send_sems, recv_sems, after)
    return outs[:n], outs[n:]


def _remote(src, dst, send_sems, recv_sems, k, to):
    return pltpu.make_async_remote_copy(src_ref=src, dst_ref=dst, send_sem=send_sems.at[k], recv_sem=recv_sems.at[k],
                                        device_id=to, device_id_type=MESH)


def _peer_copies(shards, lands, send_sems, recv_sems):
    me = _my_pos()
    return [_remote(shards[a], lands[a].at[_index(me)], send_sems, recv_sems, a * 7 + k - 1, _flip(me, k))
            for a in range(len(shards)) for k in range(1, N_DEV)]


def _own_block_copies(shards, lands, send_sems, recv_sems):
    me = _my_pos()
    return [_remote(shards[0], lands[0].at[_index(me)], send_sems, recv_sems, m - 1, _flip(me, 2 * m))
            for m in range(1, 4)]


def _sibling_copy(shards, lands, send_sems, recv_sems):
    me = _my_pos()
    return [_remote(shards[0], lands[0].at[_index(me)], send_sems, recv_sems, 0, _flip(me, 1))]


def _forward_copies(arrived, lands, send_sems, recv_sems):
    me = _my_pos()
    return [_remote(arrived[0].at[_index(_flip(me, 2 * m))], lands[0].at[_index(_flip(me, 2 * m))],
                    send_sems, recv_sems, m - 1, _flip(me, 1)) for m in range(1, 4)]


def _rs_to_sibling(name, stacks):
    n = len(stacks)

    def body(*refs):
        ins, outs = refs[:n], refs[n:2 * n]
        send_sems, recv_sems = refs[2 * n:]
        me = _my_pos()
        sib = _flip(me, 1)
        sends = []
        for a, (_, which) in enumerate(stacks):
            by_target = ins[a] if which is None else ins[a].at[which]
            for m in range(4):
                target = _flip(sib, 2 * m)
                cp = pltpu.make_async_remote_copy(
                    src_ref=by_target.at[_index(target)], dst_ref=outs[a].at[m],
                    send_sem=send_sems.at[a * 4 + m], recv_sem=recv_sems.at[a * 4 + m],
                    device_id=sib, device_id_type=MESH)
                cp.start()
                sends.append(cp)
        for cp in sends:
            cp.wait_recv()
        for cp in sends:
            cp.wait_send()

    any_spec = pl.BlockSpec(memory_space=pl.ANY)
    return pl.pallas_call(
        body, name=name,
        out_shape=[jax.ShapeDtypeStruct((4,) + s.shape[-2:], s.dtype) for s, _ in stacks],
        in_specs=[any_spec] * n, out_specs=[any_spec] * n,
        scratch_shapes=[pltpu.SemaphoreType.DMA((4 * n,)), pltpu.SemaphoreType.DMA((4 * n,))],
    )(*[s for s, _ in stacks])


def _chip_copies(srcs, lands, send_sems, recv_sems):
    me = _my_pos()
    return [_remote(srcs[a].at[m - 1], lands[a].at[m - 1], send_sems, recv_sems, a * 3 + m - 1, _flip(me, 2 * m))
            for a in range(len(srcs)) for m in range(1, 4)]


def _add_sibling(name, stack, recv, targets):
    stack, which = stack
    rows, cols = stack.shape[-2:]
    tr = min(rows, ADD_ROWS)

    def by_target(index):
        if which is None:
            return pl.BlockSpec((None, tr, cols), lambda *g: (index(*g), g[-2], 0))
        return pl.BlockSpec((None, None, tr, cols), lambda *g: (which, index(*g), g[-2], 0))

    def own_body(t_ref, a_ref, b_ref, o_ref):
        o_ref[...] = a_ref[...] + b_ref[...].astype(F32)

    own = pl.pallas_call(
        own_body, name=name + "_own",
        out_shape=jax.ShapeDtypeStruct((rows, cols), F32),
        grid_spec=pltpu.PrefetchScalarGridSpec(
            num_scalar_prefetch=1, grid=(rows // tr,),
            in_specs=[by_target(lambda i, t: t[0]),
                      pl.BlockSpec((None, tr, cols), lambda i, t: (0, i, 0))],
            out_specs=pl.BlockSpec((tr, cols), lambda i, t: (i, 0))),
        compiler_params=_params(("arbitrary",)),
    )(targets, stack, recv)

    def send_body(t_ref, a_ref, b_ref, o_ref):
        o_ref[...] = (a_ref[...] + b_ref[...].astype(F32)).astype(BF16)

    send = pl.pallas_call(
        send_body, name=name + "_send",
        out_shape=jax.ShapeDtypeStruct((3, rows, cols), BF16),
        grid_spec=pltpu.PrefetchScalarGridSpec(
            num_scalar_prefetch=1, grid=(3, rows // tr),
            in_specs=[by_target(lambda m, i, t: t[m + 1]),
                      pl.BlockSpec((None, tr, cols), lambda m, i, t: (m + 1, i, 0))],
            out_specs=pl.BlockSpec((None, tr, cols), lambda m, i, t: (m, i, 0))),
        compiler_params=_params(("arbitrary", "arbitrary")),
    )(targets, stack, recv)
    return own, send


def _allreduce_small(name, v):
    rows, cols = v.shape
    half = rows // 2
    assert rows % (2 * SUBLANES) == 0

    def body(v_ref, out_ref, from_sib, chip_half, from_chips, send_sems, recv_sems):
        me = _my_pos()
        sib = _flip(me, 1)
        mine = pl.ds(pl.multiple_of(me[2] * half, SUBLANES), half)
        theirs = pl.ds(pl.multiple_of((1 - me[2]) * half, SUBLANES), half)

        def copy(k, src, dst, to):
            return pltpu.make_async_remote_copy(src_ref=src, dst_ref=dst, send_sem=send_sems.at[k],
                                                recv_sem=recv_sems.at[k], device_id=to, device_id_type=MESH)

        to_sib = copy(0, v_ref.at[theirs], from_sib, sib)
        to_sib.start()
        to_sib.wait_recv()
        chip_half[...] = v_ref[mine, :] + from_sib[...]
        to_chips = [copy(m, chip_half, from_chips.at[m - 1], _flip(me, 2 * m)) for m in range(1, 4)]
        for cp in to_chips:
            cp.start()
        for cp in to_chips:
            cp.wait_recv()
        my_chip = 2 * me[0] + me[1]
        total = None
        for chip in range(4):
            slot = jnp.maximum(jnp.bitwise_xor(chip, my_chip) - 1, 0)
            part = jnp.where(chip == my_chip, chip_half[...], from_chips[slot])
            total = part if total is None else total + part
        out_ref[mine, :] = total
        swap = copy(4, out_ref.at[mine], out_ref.at[mine], sib)
        swap.start()
        copy(4, out_ref.at[theirs], out_ref.at[theirs], sib).wait_recv()
        for cp in [to_sib, swap] + to_chips:
            cp.wait_send()

    return pl.pallas_call(
        body, name=name, out_shape=jax.ShapeDtypeStruct((rows, cols), F32),
        in_specs=[pl.BlockSpec(memory_space=pltpu.VMEM)],
        out_specs=pl.BlockSpec(memory_space=pltpu.VMEM),
        scratch_shapes=[pltpu.VMEM((half, cols), F32), pltpu.VMEM((half, cols), F32),
                        pltpu.VMEM((3, half, cols), F32),
                        pltpu.SemaphoreType.DMA((5,)), pltpu.SemaphoreType.DMA((5,))],
        compiler_params=_params(None, VMEM_LIMIT),
    )(v)


def _mod_fwd(c_all, w_mod):
    def body(c_ref, w_ref, o_ref):
        c = c_ref[...]
        o_ref[...] = jnp.dot(c * _sigmoid(c), w_ref[...], preferred_element_type=F32,
                             precision=lax.Precision.HIGHEST)

    return pl.pallas_call(
        body, name="mod_fwd", out_shape=jax.ShapeDtypeStruct((N_DEV, w_mod.shape[1]), F32),
    )(c_all, w_mod)


def _mod_bwd(c_all, dmod_all, dmod_cols):
    def body(c_ref, da_ref, dc_ref, gb_ref, gw_ref):
        c = c_ref[...]
        acc = da_ref[0:1, :]
        for b in range(1, N_DEV):
            acc = acc + da_ref[b:b + 1, :]
        gb_ref[...] = acc
        gw_ref[...] = lax.dot_general(c * _sigmoid(c), dc_ref[...], (((0,), (0,)), ((), ())),
                                      preferred_element_type=F32, precision=lax.Precision.HIGHEST)

    return pl.pallas_call(
        body, name="mod_bwd",
        out_shape=[jax.ShapeDtypeStruct((1, dmod_all.shape[1]), F32),
                   jax.ShapeDtypeStruct((c_all.shape[1], dmod_cols.shape[1]), F32)],
    )(c_all, dmod_all, dmod_cols)


def _rope_partner(t):
    lane = lax.broadcasted_iota(jnp.int32, t.shape, 1)
    return jnp.where(lane < ROT_HALF, pltpu.roll(t, HEAD_DIM - ROT_HALF, 1), pltpu.roll(t, ROT_HALF, 1))


def _norm(x, mod, b_mod, g_norm):
    seq = x.shape[0]
    tm = PROJ_ROWS

    def body(x_ref, mod_ref, bmod_ref, g_ref, h_ref):
        xf = x_ref[...]
        rstd = lax.rsqrt(jnp.mean(xf * xf, axis=-1, keepdims=True) + NORM_EPS)
        shift = mod_ref[:, 0:D_MODEL] + bmod_ref[:, 0:D_MODEL]
        scale = mod_ref[:, D_MODEL:2 * D_MODEL] + bmod_ref[:, D_MODEL:2 * D_MODEL]
        h_ref[...] = (((xf * rstd) * g_ref[...]) * (1.0 + scale) + shift).astype(BF16)

    row = pl.BlockSpec((tm, D_MODEL), lambda i: (i, 0))
    const = lambda cols: pl.BlockSpec((1, cols), lambda i: (0, 0))
    return pl.pallas_call(
        body, name="norm", out_shape=jax.ShapeDtypeStruct((seq, D_MODEL), BF16), grid=(seq // tm,),
        in_specs=[row, const(3 * D_MODEL), const(3 * D_MODEL), const(D_MODEL)], out_specs=row,
        compiler_params=_params(("arbitrary",), VMEM_LIMIT),
    )(x, mod, b_mod, g_norm)


def _proj(name, h, w, slots, pieces, cosf, sinf, prior):
    seq = h.shape[0]
    tm = PROJ_ROWS
    count = pieces.shape[0]

    def body(slots_ref, pieces_ref, h_ref, w_ref, cos_ref, sin_ref, *rest):
        out_ref = rest[-1]
        piece = pieces_ref[pl.program_id(0)]

        @pl.when((piece < 2) | (piece > 3))
        def _():
            out_ref[...] = _dot(h_ref[...], w_ref[...])

        def rotated(gain):
            for pair in range(N_HEADS // 2):
                both = _dot(h_ref[...], w_ref[:, 2 * pair * HEAD_DIM:2 * (pair + 1) * HEAD_DIM])
                for hh in (2 * pair, 2 * pair + 1):
                    t = both[:, (hh % 2) * HEAD_DIM:(hh % 2 + 1) * HEAD_DIM]
                    t = t * cos_ref[...] + _rope_partner(t) * sin_ref[...]
                    out_ref[:, hh * HEAD_DIM:(hh + 1) * HEAD_DIM] = t if gain is None else t * gain

        @pl.when(piece == 2)
        def _():
            rotated(ATTN_SCALE)

        @pl.when(piece == 3)
        def _():
            rotated(None)

    row = lambda j, i, sl, pc: (i, 0)
    table = lambda j, i, sl, pc: (jnp.where((pc[j] == 2) | (pc[j] == 3), i, 0), 0)
    in_specs = [pl.BlockSpec((tm, D_MODEL), row),
                pl.BlockSpec((None, D_MODEL, D_MODEL), lambda j, i, sl, pc: (sl[j], 0, 0)),
                pl.BlockSpec((tm, HEAD_DIM), table), pl.BlockSpec((tm, HEAD_DIM), table)]
    args = [slots, pieces, h, w, cosf, sinf]
    aliases = {}
    if prior is not None:
        in_specs.append(pl.BlockSpec(memory_space=pl.ANY))
        args.append(prior)
        aliases = {6: 0}
    return pl.pallas_call(
        body, name=name,
        out_shape=jax.ShapeDtypeStruct((seq, 8 * D_MODEL), F32),
        grid_spec=pltpu.PrefetchScalarGridSpec(
            num_scalar_prefetch=2, grid=(count, seq // tm), in_specs=in_specs,
            out_specs=pl.BlockSpec((tm, D_MODEL), lambda j, i, sl, pc: (i, pc[j]))),
        input_output_aliases=aliases,
        compiler_params=_params(("arbitrary", "arbitrary"), VMEM_LIMIT),
    )(*args)


def _shift_down(v, s, head):
    rolled = pltpu.roll(v, s, 0)
    row = lax.broadcasted_iota(jnp.int32, head.shape, 0)
    first = jnp.where(row < s, pltpu.roll(head, s, 0), rolled[:SUBLANES, :])
    return jnp.concatenate([first, rolled[SUBLANES:, :]], axis=0)


def _shift_up(v, s, tail):
    rows = v.shape[0]
    rolled = pltpu.roll(v, rows - s, 0)
    row = lax.broadcasted_iota(jnp.int32, tail.shape, 0)
    last = jnp.where(row >= SUBLANES - s, pltpu.roll(tail, SUBLANES - s, 0), rolled[rows - SUBLANES:, :])
    return jnp.concatenate([rolled[:rows - SUBLANES, :], last], axis=0)


def _doubling(a, b, period, reverse):
    rows = a.shape[0]
    pos = lax.broadcasted_iota(jnp.int32, a.shape, 0) & (period - 1)
    k = 1
    while k < period:
        inside = (pos < period - k) if reverse else (pos >= k)
        shift = rows - k if reverse else k
        a_s = jnp.where(inside, pltpu.roll(a, shift, 0), 1.0)
        b_s = jnp.where(inside, pltpu.roll(b, shift, 0), 0.0)
        b = a * b_s + b
        a = a * a_s
        k *= 2
    return a, b


def _scan(a, b, boundary, reverse, a_scr, b_scr, spread):
    rows = a.shape[0]
    ntile = rows // SUBLANES
    a_scr[...], b_scr[...] = _doubling(a, b, SUBLANES, reverse)
    ends = pl.ds(0 if reverse else SUBLANES - 1, ntile, stride=SUBLANES)
    a_end, x_end = _doubling(a_scr[ends, :], b_scr[ends, :], ntile, reverse)
    x_end = x_end + a_end * boundary
    tile = lax.broadcasted_iota(jnp.int32, x_end.shape, 0)
    if reverse:
        incoming = jnp.where(tile == ntile - 1, boundary, pltpu.roll(x_end, ntile - 1, 0))
        last = x_end[0:1, :]
    else:
        incoming = jnp.where(tile == 0, boundary, pltpu.roll(x_end, 1, 0))
        last = x_end[ntile - 1:ntile, :]
    for s in range(SUBLANES):
        spread[pl.ds(s, ntile, stride=SUBLANES), :] = incoming
    return b_scr[...] + a_scr[...] * spread[...], last


def _conv_taps(xr, head):
    return [_shift_down(xr, 3, head), _shift_down(xr, 2, head), _shift_down(xr, 1, head), xr]


def _rnn_gates(xc, wa, ba, wx, bx, lam, keep):
    xcb = xc.astype(BF16)
    r = _sigmoid(_dot(xcb, wa.astype(BF16)) + ba)
    i = _sigmoid(_dot(xcb, wx.astype(BF16)) + bx)
    softplus = jnp.maximum(-lam, 0.0) + jnp.log(1.0 + jnp.exp(-jnp.abs(lam)))
    cl = -LRU_C * softplus
    log_a = cl * r
    a_raw = jnp.exp(log_a)
    mult_raw = jnp.sqrt(-_expm1_nonpos(2.0 * log_a, a_raw * a_raw))
    live = keep > 0.0
    return r, i, cl, a_raw, mult_raw, jnp.where(live, a_raw, 0.0), jnp.where(live, mult_raw, 1.0), live


def _rnn_specs(seq, rows, time_of):
    per = rows // SUBLANES
    vec = pl.BlockSpec((None, 1, 128), lambda hb, n: (hb, 0, 0))
    mat = pl.BlockSpec((None, 128, 128), lambda hb, n: (hb, 0, 0))
    return [pl.BlockSpec((rows, 128), lambda hb, n: (time_of(n), hb)),
            pl.BlockSpec((SUBLANES, 128), lambda hb, n: (jnp.maximum(time_of(n) * per - 1, 0), hb)),
            pl.BlockSpec((rows, 1), lambda hb, n: (time_of(n), 0)),
            pl.BlockSpec((None, SUBLANES, 128), lambda hb, n: (hb, 0, 0)),
            vec, mat, vec, mat, vec, vec]


def _rnn_fwd(pf, keep, conv_w8, conv_b, w_a, b_a, w_x, b_x, lam):
    seq = pf.shape[0]
    rows = RNN_ROWS

    def body(x_ref, xh_ref, keep_ref, cw_ref, cb_ref, wa_ref, ba_ref, wx_ref, bx_ref, lam_ref,
             hr_ref, xc_ref, r_ref, i_ref, araw_ref, mraw_ref, carry, a_scr, b_scr, spread):
        n = pl.program_id(1)

        @pl.when(n == 0)
        def _():
            carry[...] = jnp.zeros_like(carry)

        xr = x_ref[...]
        head = jnp.where(n > 0, xh_ref[...], 0.0)
        taps = _conv_taps(xr, head)
        xc = cb_ref[...] + sum(cw_ref[k:k + 1, :] * taps[k] for k in range(4))
        r, i, _, a_raw, mult_raw, a, mult, _ = _rnn_gates(xc, wa_ref[...], ba_ref[...], wx_ref[...], bx_ref[...],
                                                          lam_ref[...], keep_ref[...])
        xc_ref[...], r_ref[...], i_ref[...], araw_ref[...], mraw_ref[...] = xc, r, i, a_raw, mult_raw
        h, last = _scan(a, mult * i * xc, carry[0:1, :], False, a_scr, b_scr, spread)
        hr_ref[...] = h
        carry[...] = jnp.broadcast_to(last, carry.shape)

    chunk_f32 = pltpu.VMEM((rows, 128), F32)
    chunk = pl.BlockSpec((rows, 128), lambda hb, n: (n, hb))
    shape = jax.ShapeDtypeStruct((seq, D_MODEL), F32)
    outs = pl.pallas_call(
        body, name="rnn_fwd",
        out_shape=[shape] * 6,
        grid=(RNN_BLOCKS, seq // rows),
        in_specs=_rnn_specs(seq, rows, lambda n: n),
        out_specs=[chunk] * 6,
        scratch_shapes=[pltpu.VMEM((SUBLANES, 128), F32), chunk_f32, chunk_f32, chunk_f32],
        compiler_params=_params(("arbitrary", "arbitrary"), VMEM_LIMIT),
    )(pf, pf, keep, conv_w8, conv_b, w_a, b_a, w_x, b_x, lam)
    return outs[0], tuple(outs[1:])


def _rnn_bwd(pf, hr, dhr, saved, keep, conv_w8, w_a, w_x, lam):
    seq = pf.shape[0]
    rows = RNN_ROWS
    nchunk = seq // rows
    per = rows // SUBLANES
    time_of = lambda n: nchunk - 1 - n

    def body(x_ref, keep_ref, cw_ref, wa_ref, wx_ref, lam_ref, hr_ref, hrh_ref, dhr_ref,
             xc_ref, r_ref, i_ref, araw_ref, mraw_ref,
             dx_ref, gcw_ref, gcb_ref, gwa_ref, gba_ref, gwx_ref, gbx_ref, glam_ref,
             g_carry, dxc_tail, a_scr, b_scr, spread):
        n = pl.program_id(1)
        first_in_time = n == nchunk - 1

        @pl.when(n == 0)
        def _():
            g_carry[...] = jnp.zeros_like(g_carry)
            dxc_tail[...] = jnp.zeros_like(dxc_tail)
            for ref in (gcw_ref, gcb_ref, gwa_ref, gba_ref, gwx_ref, gbx_ref, glam_ref):
                ref[...] = jnp.zeros_like(ref)

        cw, wa, wx, lam = cw_ref[...], wa_ref[...], wx_ref[...], lam_ref[...]
        xc, r, i, a_raw, mult_raw = xc_ref[...], r_ref[...], i_ref[...], araw_ref[...], mraw_ref[...]
        cl = -LRU_C * (jnp.maximum(-lam, 0.0) + jnp.log(1.0 + jnp.exp(-jnp.abs(lam))))
        live = keep_ref[...] > 0.0
        a, mult = jnp.where(live, a_raw, 0.0), jnp.where(live, mult_raw, 1.0)
        h_prev = _shift_down(hr_ref[...], 1, jnp.where(first_in_time, 0.0, hrh_ref[...]))

        row = lax.broadcasted_iota(jnp.int32, xc.shape, 0)
        last = row == rows - 1
        a_next = jnp.where(last, 0.0, pltpu.roll(a, rows - 1, 0))
        g, g_first = _scan(a_next, dhr_ref[...] + jnp.where(last, g_carry[0:1, :], 0.0),
                           jnp.zeros((1, 128), F32), True, a_scr, b_scr, spread)
        g_carry[...] = jnp.broadcast_to(a[0:1, :] * g_first, g_carry.shape)

        da = g * h_prev
        dmult = g * i * xc
        di = g * mult * xc
        dxc = g * mult * i
        dlog_a = jnp.where(live, da * a_raw - dmult * a_raw * a_raw / mult_raw, 0.0)
        dpa = (dlog_a * cl) * r * (1.0 - r)
        dpx = di * i * (1.0 - i)
        glam_ref[...] += jnp.sum(dlog_a * r, axis=0, keepdims=True) * (LRU_C * _sigmoid(-lam))
        xcb, dpab, dpxb = xc.astype(BF16), dpa.astype(BF16), dpx.astype(BF16)
        gwa_ref[...] += _dot_tn(xcb, dpab)
        gwx_ref[...] += _dot_tn(xcb, dpxb)
        gba_ref[...] += jnp.sum(dpa, axis=0, keepdims=True)
        gbx_ref[...] += jnp.sum(dpx, axis=0, keepdims=True)
        dxc = dxc + _dot_nt(dpab, wa.astype(BF16)) + _dot_nt(dpxb, wx.astype(BF16))

        gcb_ref[...] += jnp.sum(dxc, axis=0, keepdims=True)
        xr = x_ref[...]
        tail = dxc_tail[...]
        later = [_shift_up(dxc, 3 - k, tail) for k in range(3)] + [dxc]
        dx = cw[3:4, :] * dxc
        for k in range(3):
            dx = dx + cw[k:k + 1, :] * later[k]
        for k in range(4):
            gcw_ref[k:k + 1, :] += jnp.sum(xr * later[k], axis=0, keepdims=True)
        dx_ref[...] = dx.astype(BF16)
        dxc_tail[...] = dxc[0:SUBLANES, :]

    blk = lambda hb, n: (hb, 0, 0)
    chunk = pl.BlockSpec((rows, 128), lambda hb, n: (time_of(n), hb))
    vec = pl.BlockSpec((None, 1, 128), blk)
    mat = pl.BlockSpec((None, 128, 128), blk)
    vec_shape = jax.ShapeDtypeStruct((RNN_BLOCKS, 1, 128), F32)
    mat_shape = jax.ShapeDtypeStruct((RNN_BLOCKS, 128, 128), F32)
    return pl.pallas_call(
        body, name="rnn_bwd",
        out_shape=[jax.ShapeDtypeStruct((seq, D_MODEL), BF16),
                   jax.ShapeDtypeStruct((RNN_BLOCKS, SUBLANES, 128), F32), vec_shape,
                   mat_shape, vec_shape, mat_shape, vec_shape, vec_shape],
        grid=(RNN_BLOCKS, nchunk),
        in_specs=[chunk, pl.BlockSpec((rows, 1), lambda hb, n: (time_of(n), 0)),
                  pl.BlockSpec((None, SUBLANES, 128), blk), mat, mat, vec, chunk,
                  pl.BlockSpec((SUBLANES, 128), lambda hb, n: (jnp.maximum(time_of(n) * per - 1, 0), hb)), chunk]
                 + [chunk] * 5,
        out_specs=[chunk, pl.BlockSpec((None, SUBLANES, 128), blk), vec, mat, vec, mat, vec, vec],
        scratch_shapes=[pltpu.VMEM((SUBLANES, 128), F32), pltpu.VMEM((SUBLANES, 128), F32)]
                       + [pltpu.VMEM((rows, 128), F32)] * 3,
        compiler_params=_params(("arbitrary", "arbitrary"), VMEM_LIMIT),
    )(pf, keep, conv_w8, w_a, w_x, lam, hr, hr, dhr, *saved)


def _unit_rows(dil, r, j):
    start = j * KEY_BLOCK * dil + r
    return pl.ds(start, KEY_BLOCK) if dil == 1 else pl.ds(start, KEY_BLOCK, stride=dil)


def _attn_fwd(proj):
    nh, seq = N_HEADS, proj.shape[0]
    nchunk = seq // SPAN
    nblk = SPAN // KEY_BLOCK
    wide = DILATIONS[-1]

    def body(q_ref, k_ref, v_ref, kp_ref, vp_ref, o_ref, l1_ref, l4_ref, l16_ref, q16, k16, v16, o16,
             acc, m_s, l_s, k16p, v16p, acc16, m16, l16, tmp):
        n = pl.program_id(1)
        qi = lax.broadcasted_iota(jnp.int32, (KEY_BLOCK, KEY_BLOCK), 0)
        ki = lax.broadcasted_iota(jnp.int32, (KEY_BLOCK, KEY_BLOCK), 1)
        bias_own = jnp.where(ki <= qi, 0.0, NEG_INF)
        bias_before = jnp.where(ki >= qi, 0.0, NEG_INF)
        bias_mid = jnp.concatenate([bias_before, bias_own], axis=1)
        bias_first = jnp.concatenate([jnp.where(n > 0, bias_before, NEG_INF), bias_own], axis=1)
        ones = jnp.ones((2 * KEY_BLOCK, HEAD_DIM), BF16)
        diag = qi == ki

        @pl.when(n == 0)
        def _():
            k16p[...] = jnp.zeros_like(k16p)
            v16p[...] = jnp.zeros_like(v16p)

        def unit(qf, kpb, kb, vpb, vb, bias, state, rows, first):
            acc_r, m_r, l_r = state
            kcat = jnp.concatenate([kpb, kb], axis=0)
            vaug = jnp.concatenate([jnp.concatenate([vpb, vb], axis=0), ones], axis=1)
            s = _dot_nt(qf.astype(BF16), kcat) + bias
            mx = jnp.max(s, axis=-1, keepdims=True)
            if first:
                m_new = jnp.broadcast_to(mx, (KEY_BLOCK, HEAD_DIM))
            else:
                m_old = m_r[rows, :]
                m_new = jnp.maximum(m_old, mx)
            pv = _dot(jnp.exp(s - jnp.concatenate([m_new, m_new], axis=1)).astype(BF16), vaug)
            if first:
                acc_r[rows, :] = pv[:, :HEAD_DIM]
                l_r[rows, :] = pv[:, HEAD_DIM:]
            else:
                alpha = jnp.exp(m_old - m_new)
                acc_r[rows, :] = alpha * acc_r[rows, :] + pv[:, :HEAD_DIM]
                l_r[rows, :] = alpha * l_r[rows, :] + pv[:, HEAD_DIM:]
            m_r[rows, :] = m_new

        for gi, dil in enumerate(DILATIONS[:-1]):
            nb = nblk // dil
            for r in range(dil):
                prow = _unit_rows(dil, r, nb - 1)
                kpb, vpb = kp_ref[prow, :].astype(BF16), vp_ref[prow, :].astype(BF16)
                for j in range(nb):
                    rows = _unit_rows(dil, r, j)
                    kb, vb = k_ref[rows, :].astype(BF16), v_ref[rows, :].astype(BF16)
                    unit(q_ref[rows, :], kpb, kb, vpb, vb, bias_first if j == 0 else bias_mid,
                         (acc, m_s, l_s), rows, gi == 0)
                    kpb, vpb = kb, vb

        for src, dst in ((q_ref, q16), (k_ref, k16), (v_ref, v16), (acc, acc16), (m_s, m16), (l_s, l16)):
            _to_residue_major(src, tmp, dst)
        for r in range(wide):
            rows = pl.ds(r * KEY_BLOCK, KEY_BLOCK)
            unit(q16[rows, :], k16p[rows, :].astype(BF16), k16[rows, :].astype(BF16), v16p[rows, :].astype(BF16),
                 v16[rows, :].astype(BF16), bias_first, (acc16, m16, l16), rows, False)
        k16p[...] = k16[...]
        v16p[...] = v16[...]

        den = l16[...]
        o16[...] = acc16[...] * (1.0 / den)
        m16[...] = m16[...] + jnp.log(den)
        _from_residue_major(o16, tmp, o_ref, False)
        _from_residue_major(m16, tmp, m_s, False)

        def lse_row(ref, rows):
            return jnp.sum(jnp.where(diag, ref[rows, :], 0.0), axis=0, keepdims=True)

        for dil, out in zip(DILATIONS[:-1], (l1_ref, l4_ref)):
            nb = nblk // dil
            for r in range(dil):
                for j in range(nb):
                    out[r * nb + j:r * nb + j + 1, :] = lse_row(m_s, _unit_rows(dil, r, j))
        for r in range(wide):
            l16_ref[r:r + 1, :] = lse_row(m16, pl.ds(r * KEY_BLOCK, KEY_BLOCK))

    cur = lambda piece: pl.BlockSpec((SPAN, HEAD_DIM), lambda h, n: (n, piece * nh + h))
    before = lambda piece: pl.BlockSpec((SPAN, HEAD_DIM), lambda h, n: (jnp.maximum(n - 1, 0), piece * nh + h))
    blk = pl.BlockSpec((None, SPAN, HEAD_DIM), lambda h, n: (h, n, 0))
    lblk = pl.BlockSpec((None, nblk, KEY_BLOCK), lambda h, n: (h, n, 0))
    lshape = jax.ShapeDtypeStruct((nh, seq // KEY_BLOCK, KEY_BLOCK), F32)
    full = jax.ShapeDtypeStruct((nh, seq, HEAD_DIM), F32)
    o, l1, l4, l16, *major = pl.pallas_call(
        body, name="attn_fwd",
        out_shape=[full, lshape, lshape, lshape] + [full] * 4,
        grid=(nh, nchunk), in_specs=[cur(2), cur(3), cur(4), before(3), before(4)],
        out_specs=[blk, lblk, lblk, lblk] + [blk] * 4,
        scratch_shapes=[pltpu.VMEM((SPAN, HEAD_DIM), F32)] * 9,
        compiler_params=_params(("arbitrary", "arbitrary"), VMEM_LIMIT),
    )(proj, proj, proj, proj, proj)
    return o, (l1, l4, l16), tuple(major)


def _to_residue_major(src, tmp, dst):
    quarter = SPAN // 4
    for r4 in range(4):
        tmp[r4 * quarter:(r4 + 1) * quarter, :] = src[pl.ds(r4, quarter, stride=4), :]
    for r4 in range(4):
        for rp in range(4):
            r = r4 + 4 * rp
            dst[r * KEY_BLOCK:(r + 1) * KEY_BLOCK, :] = tmp[pl.ds(r4 * quarter + rp, KEY_BLOCK, stride=4), :]


def _from_residue_major(src, tmp, dst, add):
    quarter = SPAN // 4
    for r4 in range(4):
        for rp in range(4):
            r = r4 + 4 * rp
            tmp[pl.ds(r4 * quarter + rp, KEY_BLOCK, stride=4), :] = src[r * KEY_BLOCK:(r + 1) * KEY_BLOCK, :]
    for r4 in range(4):
        rows = pl.ds(r4, quarter, stride=4)
        part = tmp[r4 * quarter:(r4 + 1) * quarter, :]
        dst[rows, :] = dst[rows, :] + part if add else part


def _attn_bwd(proj, do, o, lses, major, cosf, sinf):
    nh, seq = N_HEADS, proj.shape[0]
    nchunk = seq // SPAN
    nblk = SPAN // KEY_BLOCK
    wide = DILATIONS[-1]
    assert SPAN == wide * KEY_BLOCK

    def body(q_ref, k_ref, v_ref, do_ref, o_ref, kp_ref, vp_ref, q16, k16, v16, o16, l1_ref, l4_ref, l16_ref,
             cos_ref, sin_ref, cosp_ref, sinp_ref, dq_ref, dk_ref, dv_ref,
             dq_acc, dkc_acc, dvc_acc, dkp_acc, dvp_acc, do16, k16p, v16p,
             dq16, dkc16, dvc16, dkp16, dvp16, tmp, pt_s, ds_s, kcat_s, qb_s, dob_s):
        n = pl.program_id(1)
        ki = lax.broadcasted_iota(jnp.int32, (KEY_BLOCK, KEY_BLOCK), 0)
        qi = lax.broadcasted_iota(jnp.int32, (KEY_BLOCK, KEY_BLOCK), 1)
        bias_own = jnp.where(ki <= qi, 0.0, NEG_INF)
        bias_before = jnp.where(ki >= qi, 0.0, NEG_INF)
        bias_mid = jnp.concatenate([bias_before, bias_own], axis=0)
        bias_first = jnp.concatenate([jnp.where(n > 0, bias_before, NEG_INF), bias_own], axis=0)
        ones8 = jnp.ones((SUBLANES, HEAD_DIM), BF16)

        def row_dot(a, b):
            prod = a * b
            hi = prod.astype(BF16)
            lo = (prod - hi.astype(F32)).astype(BF16)
            return (_dot_nt(ones8, hi) + _dot_nt(ones8, lo))[0:1, :]

        def group(units, srcs, before, l_ref, accs):
            src_q, src_do, src_o, src_k, src_v = srcs
            before_k, before_v = before
            acc_q, acc_kc, acc_vc, acc_kp, acc_vp = accs
            kb = vb = None
            for u, (rows, prow, outside, lrow, _) in enumerate(units):
                dof = src_do[rows, :]
                qb, dob = src_q[rows, :].astype(BF16), dof.astype(BF16)
                kpb, vpb = (before_k[prow, :].astype(BF16), before_v[prow, :].astype(BF16)) if outside else (kb, vb)
                kb, vb = src_k[rows, :].astype(BF16), src_v[rows, :].astype(BF16)
                kcat = jnp.concatenate([kpb, kb], axis=0)
                vcat = jnp.concatenate([vpb, vb], axis=0)
                bias = bias_first if outside else bias_mid
                pt = jnp.exp(_dot_nt(kcat, qb) + bias - l_ref[lrow:lrow + 1, :])
                dst = pt * (_dot_nt(vcat, dob) - row_dot(dof, src_o[rows, :]))
                pt_s[u], ds_s[u], kcat_s[u], qb_s[u], dob_s[u] = pt.astype(BF16), dst.astype(BF16), kcat, qb, dob
            for u, (rows, _, _, _, _) in enumerate(units):
                acc_q[rows, :] += _dot_tn(ds_s[u], kcat_s[u])
            for u, (rows, prow, outside, _, nxt) in enumerate(units):
                dk = _dot(ds_s[u, KEY_BLOCK:, :], qb_s[u])
                dv = _dot(pt_s[u, KEY_BLOCK:, :], dob_s[u])
                if nxt is not None:
                    dk = dk + _dot(ds_s[nxt, :KEY_BLOCK, :], qb_s[nxt])
                    dv = dv + _dot(pt_s[nxt, :KEY_BLOCK, :], dob_s[nxt])
                acc_kc[rows, :] += dk
                acc_vc[rows, :] += dv
                if outside:
                    acc_kp[prow, :] += _dot(ds_s[u, :KEY_BLOCK, :], qb_s[u])
                    acc_vp[prow, :] += _dot(pt_s[u, :KEY_BLOCK, :], dob_s[u])

        @pl.when(n == 0)
        def _():
            for ref in (dkp_acc, dvp_acc, dkp16, dvp16, k16p, v16p):
                ref[...] = jnp.zeros_like(ref)

        @pl.when(n < nchunk)
        def _():
            for ref in (dq_acc, dkc_acc, dvc_acc, dq16, dkc16, dvc16):
                ref[...] = jnp.zeros_like(ref)
            _to_residue_major(do_ref, tmp, do16)
            natural = (q_ref, do_ref, o_ref, k_ref, v_ref)
            for dil, l_ref in zip(DILATIONS[:-1], (l1_ref, l4_ref)):
                nb = nblk // dil
                units = [(_unit_rows(dil, r, j), _unit_rows(dil, r, (j - 1) % nb), j == 0, r * nb + j,
                          r * nb + j + 1 if j + 1 < nb else None) for r in range(dil) for j in range(nb)]
                group(units, natural, (kp_ref, vp_ref), l_ref, (dq_acc, dkc_acc, dvc_acc, dkp_acc, dvp_acc))
            blocks = [pl.ds(r * KEY_BLOCK, KEY_BLOCK) for r in range(wide)]
            group([(rows, rows, True, r, None) for r, rows in enumerate(blocks)], (q16, do16, o16, k16, v16),
                  (k16p, v16p), l16_ref, (dq16, dkc16, dvc16, dkp16, dvp16))
            _from_residue_major(dq16, tmp, dq_acc, True)
            dq = dq_acc[...]
            dq_ref[...] = ((dq * cos_ref[...] - _rope_partner(dq) * sin_ref[...]) * ATTN_SCALE).astype(BF16)

        @pl.when(n > 0)
        def _():
            _from_residue_major(dkp16, tmp, dkp_acc, True)
            _from_residue_major(dvp16, tmp, dvp_acc, True)
            dk = dkp_acc[...]
            dk_ref[...] = (dk * cosp_ref[...] - _rope_partner(dk) * sinp_ref[...]).astype(BF16)
            dv_ref[...] = dvp_acc[...].astype(BF16)

        @pl.when(n < nchunk)
        def _():
            for src, dst in ((dkc_acc, dkp_acc), (dvc_acc, dvp_acc), (dkc16, dkp16), (dvc16, dvp16),
                             (k16, k16p), (v16, v16p)):
                dst[...] = src[...]

    last = nchunk - 1
    cur = lambda h, n: (h, jnp.minimum(n, last), 0)
    prev = lambda h, n: (h, jnp.clip(n - 1, 0, last), 0)
    blk = lambda idx: pl.BlockSpec((None, SPAN, HEAD_DIM), idx)
    lblk = pl.BlockSpec((None, nblk, KEY_BLOCK), cur)
    tab = pl.BlockSpec((SPAN, HEAD_DIM), lambda h, n: (jnp.minimum(n, last), 0))
    tabp = pl.BlockSpec((SPAN, HEAD_DIM), lambda h, n: (jnp.clip(n - 1, 0, last), 0))
    out_q = pl.BlockSpec((SPAN, HEAD_DIM), lambda h, n: (jnp.minimum(n, last), h))
    out_kv = pl.BlockSpec((SPAN, HEAD_DIM), lambda h, n: (jnp.clip(n - 1, 0, last), h))
    shape = jax.ShapeDtypeStruct((seq, nh * HEAD_DIM), BF16)
    tok = lambda piece, row: pl.BlockSpec((SPAN, HEAD_DIM), lambda h, n: (row(n), piece * nh + h))
    row_cur, row_prev = (lambda n: jnp.minimum(n, last)), (lambda n: jnp.clip(n - 1, 0, last))
    return pl.pallas_call(
        body, name="attn_bwd", out_shape=[shape, shape, shape], grid=(nh, nchunk + 1),
        in_specs=[tok(2, row_cur), tok(3, row_cur), tok(4, row_cur), blk(cur), blk(cur),
                  tok(3, row_prev), tok(4, row_prev)] + [blk(cur)] * 4 + [lblk] * 3 + [tab, tab, tabp, tabp],
        out_specs=[out_q, out_kv, out_kv],
        scratch_shapes=[pltpu.VMEM((SPAN, HEAD_DIM), F32)] * 14
                       + [pltpu.VMEM((nblk, 2 * KEY_BLOCK, HEAD_DIM), BF16)] * 3
                       + [pltpu.VMEM((nblk, KEY_BLOCK, HEAD_DIM), BF16)] * 2,
        compiler_params=_params(("arbitrary", "arbitrary"), VMEM_LIMIT),
    )(proj, proj, proj, do, o, proj, proj, *major, *lses, cosf, sinf, cosf, sinf)


def _hub(x, tgt, hr, pf, o_hm, mod, b_mod, b_gate, g_final, w_out_rnn, w_out_attn, w_o):
    seq = x.shape[0]
    tm = HUB_ROWS
    nsteps = seq // tm

    def body(x_ref, t_ref, hr_ref, zr_ref, za_ref, gr_ref, ga_ref, o_ref, mod_ref, bmod_ref, bg_ref, gf_ref,
             wr_hbm, wa_hbm, wo_hbm,
             dx2_ref, dhr_ref, dzr_ref, do_ref, dza_ref, dgr_ref, dga_ref,
             ur_ref, dyr_ref, ua_ref, dya_ref, mg_ref, dmo_ref,
             ggf_ref, gbg_ref, dgate_ref, loss_ref,
             wr, wa, wo, sem):
        step = pl.program_id(0)

        @pl.when(step == 0)
        def _():
            for src, dst in ((wr_hbm, wr), (wa_hbm, wa), (wo_hbm, wo)):
                cp = pltpu.make_async_copy(src, dst, sem)
                cp.start()
                cp.wait()
            for ref in (ggf_ref, gbg_ref, dgate_ref, loss_ref):
                ref[...] = jnp.zeros_like(ref)

        gate = mod_ref[:, 2 * D_MODEL:] + bmod_ref[:, 2 * D_MODEL:]
        gfin = gf_ref[...]
        hr_t, zr, za = hr_ref[...], zr_ref[...], za_ref[...]
        o = jnp.concatenate([o_ref[hh] for hh in range(N_HEADS)], axis=1)
        sig_zr, sig_za = _sigmoid(zr), _sigmoid(za)
        silu_zr, silu_za = zr * sig_zr, za * sig_za
        u_rnn = (hr_t * silu_zr).astype(BF16)
        u_attn = (o * silu_za).astype(BF16)
        y_rnn = _dot(u_rnn, wr[...])
        y_attn = _dot(u_attn, wa[...])
        sr = _sigmoid(gr_ref[...] + bg_ref[:, :D_MODEL])
        sa = _sigmoid(ga_ref[...] + bg_ref[:, D_MODEL:])
        merged = (sr * y_rnn + sa * y_attn).astype(BF16)
        mo = _dot(merged, wo[...])
        x2 = x_ref[...] + gate * mo
        rstd = lax.rsqrt(jnp.mean(x2 * x2, axis=-1, keepdims=True) + NORM_EPS)
        xn = x2 * rstd
        err = xn * gfin - t_ref[...]
        loss_ref[...] += 0.5 * jnp.sum(jnp.sum(err * err, axis=-1, keepdims=True) * (1.0 / D_MODEL),
                                       axis=0, keepdims=True)

        dy = err * (1.0 / D_MODEL)
        ggf_ref[...] += jnp.sum(dy * xn, axis=0, keepdims=True)
        dxn = dy * gfin
        dx2 = rstd * (dxn - xn * jnp.mean(dxn * xn, axis=-1, keepdims=True))
        dx2_ref[...] = dx2
        dgate_ref[...] += jnp.sum(dx2 * mo, axis=0, keepdims=True)
        dmo = (dx2 * gate).astype(BF16)
        dmerged = _dot_nt(dmo, wo[...])
        mg_ref[...] = merged
        dmo_ref[...] = dmo
        dy_rnn = (dmerged * sr).astype(BF16)
        dy_attn = (dmerged * sa).astype(BF16)
        dg_r = dmerged * y_rnn * sr * (1.0 - sr)
        dg_a = dmerged * y_attn * sa * (1.0 - sa)
        dgr_ref[...] = dg_r.astype(BF16)
        dga_ref[...] = dg_a.astype(BF16)
        gbg_ref[:, :D_MODEL] += jnp.sum(dg_r, axis=0, keepdims=True)
        gbg_ref[:, D_MODEL:] += jnp.sum(dg_a, axis=0, keepdims=True)
        du_rnn = _dot_nt(dy_rnn, wr[...])
        du_attn = _dot_nt(dy_attn, wa[...])
        ur_ref[...] = u_rnn
        dyr_ref[...] = dy_rnn
        ua_ref[...] = u_attn
        dya_ref[...] = dy_attn
        dhr_ref[...] = du_rnn * silu_zr
        dzr_ref[...] = (du_rnn * hr_t * (sig_zr * (1.0 + zr * (1.0 - sig_zr)))).astype(BF16)
        dza_ref[...] = (du_attn * o * (sig_za * (1.0 + za * (1.0 - sig_za)))).astype(BF16)
        d_o = du_attn * silu_za
        for hh in range(N_HEADS):
            do_ref[hh] = d_o[:, hh * HEAD_DIM:(hh + 1) * HEAD_DIM]

    row = pl.BlockSpec((tm, D_MODEL), lambda i: (i, 0))
    piece = lambda slot: pl.BlockSpec((tm, D_MODEL), lambda i: (i, slot))
    hm = pl.BlockSpec((N_HEADS, tm, HEAD_DIM), lambda i: (0, i, 0))
    const = lambda cols: pl.BlockSpec((1, cols), lambda i: (0, 0))
    any_spec = pl.BlockSpec(memory_space=pl.ANY)
    act_f32 = jax.ShapeDtypeStruct((seq, D_MODEL), F32)
    act_bf16 = jax.ShapeDtypeStruct((seq, D_MODEL), BF16)
    return pl.pallas_call(
        body, name="hub",
        out_shape=[act_f32, act_f32, act_bf16, jax.ShapeDtypeStruct((N_HEADS, seq, HEAD_DIM), F32),
                   act_bf16, act_bf16, act_bf16] + [act_bf16] * 6 + [
                   jax.ShapeDtypeStruct((1, D_MODEL), F32), jax.ShapeDtypeStruct((1, 2 * D_MODEL), F32),
                   jax.ShapeDtypeStruct((1, D_MODEL), F32), jax.ShapeDtypeStruct((1, 1), F32)],
        grid=(nsteps,),
        in_specs=[row, row, row, piece(1), piece(5), piece(6), piece(7), hm,
                  const(3 * D_MODEL), const(3 * D_MODEL), const(2 * D_MODEL), const(D_MODEL),
                  any_spec, any_spec, any_spec],
        out_specs=[row, row, row, hm, row, row, row] + [row] * 6 + [
                   const(D_MODEL), const(2 * D_MODEL), const(D_MODEL), const(1)],
        scratch_shapes=[pltpu.VMEM((D_MODEL, D_MODEL), BF16)] * 3 + [pltpu.SemaphoreType.DMA],
        compiler_params=_params(("arbitrary",), VMEM_LIMIT),
    )(x, tgt, hr, pf, pf, pf, pf, o_hm, mod, b_mod, b_gate, g_final, w_out_rnn, w_out_attn, w_o)


def _pair_grads(name, lefts, rights):
    n = len(rights)
    shared = len(lefts) == 1
    seq = rights[0].shape[0]
    tk = WGRAD_ROWS
    nk = seq // tk

    def body(*refs):
        l_refs, r_refs = refs[:len(lefts)], refs[len(lefts):len(lefts) + n]
        out_ref, low_ref = refs[len(lefts) + n:]
        j, kk = pl.program_id(0), pl.program_id(1)

        @pl.when(kk == 0)
        def _():
            out_ref[...] = jnp.zeros_like(out_ref)

        for m in range(n):
            @pl.when(j == m)
            def _(m=m):
                out_ref[...] += _dot_tn(l_refs[0 if shared else m][...], r_refs[m][...])

        @pl.when(kk == nk - 1)
        def _():
            low_ref[...] = out_ref[...].astype(BF16)

    def spec(m):
        return pl.BlockSpec((tk, D_MODEL), lambda j, kk: (jnp.where(j == m, kk, jnp.where(j < m, 0, nk - 1)), 0))

    left_specs = [pl.BlockSpec((tk, D_MODEL), lambda j, kk: (kk, 0))] if shared else [spec(m) for m in range(n)]
    out_spec = pl.BlockSpec((None, D_MODEL, D_MODEL), lambda j, kk: (j, 0, 0))
    return pl.pallas_call(
        body, name=name,
        out_shape=[jax.ShapeDtypeStruct((n, D_MODEL, D_MODEL), F32), jax.ShapeDtypeStruct((n, D_MODEL, D_MODEL), BF16)],
        grid=(n, nk),
        in_specs=left_specs + [spec(m) for m in range(n)],
        out_specs=[out_spec, out_spec],
        compiler_params=_params(("arbitrary", "arbitrary"), VMEM_LIMIT),
    )(*lefts, *rights)


def _dh_dx(pieces, w_near, w_sib, w_far, x, dx2, mod, b_mod, g_norm):
    seq = x.shape[0]
    tm = DX_ROWS

    def body(*refs):
        p_refs = refs[:8]
        near_hbm, sib_hbm, far_hbm, x_ref, dx2_ref, mod_ref, bmod_ref, g_ref = refs[8:16]
        gx_ref, dshift_ref, dscale_ref, ggn_ref, w_scr, sem = refs[16:]
        step = pl.program_id(0)

        @pl.when(step == 0)
        def _():
            me = _my_pos()
            sib = _flip(me, 1)
            moves = [(near_hbm, _index(_flip(me, 2 * m))) for m in range(4)] + [(sib_hbm, _index(sib))]
            moves += [(far_hbm, _index(_flip(sib, 2 * m))) for m in range(1, 4)]
            loads = [pltpu.make_async_copy(src.at[t], w_scr.at[t], sem.at[i]) for i, (src, t) in enumerate(moves)]
            for cp in loads:
                cp.start()
            for cp in loads:
                cp.wait()
            for ref in (dshift_ref, dscale_ref, ggn_ref):
                ref[...] = jnp.zeros_like(ref)

        dh = _dot_nt(p_refs[0][...], w_scr[0])
        for j in range(1, 8):
            dh = dh + _dot_nt(p_refs[j][...], w_scr[j])
        scale1 = 1.0 + mod_ref[:, D_MODEL:2 * D_MODEL] + bmod_ref[:, D_MODEL:2 * D_MODEL]
        g = g_ref[...]
        xf = x_ref[...]
        rstd_t = lax.rsqrt(jnp.mean(xf * xf, axis=-1, keepdims=True) + NORM_EPS)
        xn = xf * rstd_t
        dshift_ref[...] += jnp.sum(dh, axis=0, keepdims=True)
        dscale_ref[...] += jnp.sum(dh * (xn * g), axis=0, keepdims=True)
        ggn_ref[...] += jnp.sum(dh * scale1 * xn, axis=0, keepdims=True)
        dxn = dh * (g * scale1)
        gx_ref[...] = rstd_t * (dxn - xn * jnp.mean(dxn * xn, axis=-1, keepdims=True)) + dx2_ref[...]

    row = pl.BlockSpec((tm, D_MODEL), lambda i: (i, 0))
    const = lambda cols: pl.BlockSpec((1, cols), lambda i: (0, 0))
    vec = jax.ShapeDtypeStruct((1, D_MODEL), F32)
    return pl.pallas_call(
        body, name="dh_dx",
        out_shape=[jax.ShapeDtypeStruct((seq, D_MODEL), F32), vec, vec, vec],
        grid=(seq // tm,),
        in_specs=[row] * 8 + [pl.BlockSpec(memory_space=pl.ANY)] * 3 + [row, row,
                              const(3 * D_MODEL), const(3 * D_MODEL), const(D_MODEL)],
        out_specs=[row, const(D_MODEL), const(D_MODEL), const(D_MODEL)],
        scratch_shapes=[pltpu.VMEM((8, D_MODEL, D_MODEL), BF16), pltpu.SemaphoreType.DMA((8,))],
        compiler_params=_params(("arbitrary",), VMEM_LIMIT),
    )(*pieces, w_near, w_sib, w_far, x, dx2, mod, b_mod, g_norm)


def _adamw(name, w, g, m, v, recv=None):
    rows, cols = w.shape
    tr = rows if rows <= 256 else 256

    def body(*refs):
        w_ref, g_ref, m_ref, v_ref = refs[:4]
        d_ref, nm_ref, nv_ref = refs[-3:] if recv is None else refs[5:8]
        gv = g_ref[...]
        if recv is not None:
            r_ref, g_out = refs[4], refs[8]
            gv = ((gv + r_ref[0].astype(F32)) + r_ref[1].astype(F32)) + r_ref[2].astype(F32)
            g_out[...] = gv
        nm = ADAM_B1 * m_ref[...] + (1.0 - ADAM_B1) * gv
        nv = ADAM_B2 * v_ref[...] + (1.0 - ADAM_B2) * (gv * gv)
        m_hat = nm / (1.0 - ADAM_B1 ** ADAM_STEP)
        v_hat = nv / (1.0 - ADAM_B2 ** ADAM_STEP)
        d_ref[...] = -ADAM_LR * (m_hat / (jnp.sqrt(v_hat) + ADAM_EPS) + ADAM_WD * w_ref[...])
        nm_ref[...] = nm
        nv_ref[...] = nv

    spec = pl.BlockSpec((tr, cols), lambda i: (i, 0))
    shape = jax.ShapeDtypeStruct((rows, cols), F32)
    if recv is None:
        return pl.pallas_call(
            body, name=name, out_shape=[shape, shape, shape], grid=(rows // tr,),
            in_specs=[spec] * 4, out_specs=[spec] * 3,
            compiler_params=_params(("arbitrary",)),
        )(w, g, m, v)
    return pl.pallas_call(
        body, name=name, out_shape=[shape] * 4, grid=(rows // tr,),
        in_specs=[spec] * 4 + [pl.BlockSpec((3, tr, cols), lambda i: (0, i, 0))], out_specs=[spec] * 4,
        compiler_params=_params(("arbitrary",)),
    )(w, g, m, v, recv)


def kernel(x, c, positions, g_norm, w_mod, b_mod, w_in, b_gate, conv_w, conv_b, w_a, b_a, w_x, b_x, lam, w_out_rnn, w_out_attn, w_o, g_final, loss_target, m_g_norm, m_w_mod, m_b_mod, m_w_in, m_b_gate, m_conv_w, m_conv_b, m_w_a, m_b_a, m_w_x, m_b_x, m_lam, m_w_out_rnn, m_w_out_attn, m_w_o, m_g_final, v_g_norm, v_w_mod, v_b_mod, v_w_in, v_b_gate, v_conv_w, v_conv_b, v_w_a, v_b_a, v_w_x, v_b_x, v_lam, v_w_out_rnn, v_w_out_attn, v_w_o, v_g_final):
    seq = x.shape[1]
    me = _index(_my_pos())
    xs, tgt = x[0], loss_target[0]

    inv_freq = ROPE_THETA ** (-jnp.arange(0, 2 * ROT_HALF, 2, dtype=F32) / (2 * ROT_HALF))
    ang = (positions[0].astype(F32).reshape(seq // SUBLANES, SUBLANES, 1) * inv_freq).reshape(seq // SUBLANES, 128)
    cos, sin = lax.optimization_barrier((jnp.cos(ang), jnp.sin(ang)))
    cos, sin = cos.reshape(seq, ROT_HALF), sin.reshape(seq, ROT_HALF)
    rest = HEAD_DIM - 2 * ROT_HALF
    cosf = jnp.concatenate([cos, cos, jnp.ones((seq, rest), F32)], axis=1)
    sinf = jnp.concatenate([-sin, sin, jnp.zeros((seq, rest), F32)], axis=1)
    keep = (positions[0] != 0).astype(F32)[:, None]

    both = _ag_small("gather_c_conv_w", jnp.concatenate(
        [jnp.broadcast_to(c, (SUBLANES, D_MODEL)), jnp.pad(conv_w[0], ((0, SUBLANES - 4), (0, 0)))], axis=1))
    c_all, conv_w8 = both[:, 0, :D_MODEL], both[:, :, D_MODEL:]
    mod_cols = w_mod.shape[2]
    mod_part = _ag_small("gather_mod", _mod_fwd(c_all, w_mod[0]))
    mod = lax.dynamic_index_in_dim(mod_part, me, axis=1, keepdims=False).reshape(1, N_DEV * mod_cols)

    slot = lambda t: lax.dynamic_update_slice(lax.empty((N_DEV,) + t.shape, t.dtype), t[None], (me, 0, 0))
    w_in_own = w_in[0].astype(BF16)
    mod, w_in_own = lax.optimization_barrier((mod, w_in_own))
    first = _split_start("gather_w_in_start", _own_block_copies, 3, [w_in_own], [slot(w_in_own)])
    swap = _split_start("swap_w_in_start", _sibling_copy, 1, first[2], [lax.empty((N_DEV,) + w_in_own.shape, BF16)])
    mod = mod + swap[4][0:1, 0:1]

    blocks = lambda t: t.reshape(RNN_BLOCKS, 1, 128)
    rnn_params = (conv_w8, blocks(conv_b), w_a[0], blocks(b_a), w_x[0], blocks(b_x), blocks(lam))

    h = _norm(xs, mod, b_mod, g_norm)
    ids = lambda ks: jnp.bitwise_xor(me, jnp.array(ks, jnp.int32)).astype(jnp.int32)
    pf = _proj("proj_own", h, swap[2][0][None], jnp.zeros((1,), jnp.int32), ids([0]), cosf, sinf, None)
    own_thru, (w_in_sib,) = _split_wait("swap_w_in_wait", _sibling_copy, swap, pf)
    pf = _proj("proj_sibling", h, w_in_sib, ids([1]), ids([1]), cosf, sinf, pf)
    _, (w_in_near,) = _split_wait("gather_w_in_wait", _own_block_copies, (first[0], first[1], own_thru, first[3], None), pf)
    second = _split_start("forward_w_in_start", _forward_copies, 3, [w_in_near],
                          [lax.empty(w_in_near.shape, w_in_near.dtype)])
    near = ids([2, 4, 6])
    pf = _proj("proj_near", h, second[2][0], near, near, cosf, sinf, pf)
    (w_in_near,), (w_in_far,) = _split_wait("forward_w_in_wait", _forward_copies, second, pf)
    far = ids([3, 5, 7])
    pf = _proj("proj_far", h, w_in_far, far, far, cosf, sinf, pf)
    late = [w_out_rnn[0].astype(BF16), w_out_attn[0].astype(BF16), w_o[0].astype(BF16)]
    pf, late = lax.optimization_barrier((pf, late))
    flight = _split_start("gather_out_weights_start", _peer_copies, 7 * len(late), late, [slot(t) for t in late])
    rnn_params = (rnn_params[0], rnn_params[1] + flight[4][0:1, 0:1]) + rnn_params[2:]
    hr, rnn_saved = _rnn_fwd(pf, keep, *rnn_params)
    o, lses, major = _attn_fwd(pf)

    w_or_all, w_oa_all, w_o_all = (t.reshape(D_MODEL, D_MODEL) for t in _split_wait(
        "gather_out_weights_wait", _peer_copies, flight, o)[1])
    (dx2, dhr, dz_rnn, d_o, dz_attn, dg_r, dg_a, u_rnn, dy_rnn, u_attn, dy_attn, merged, dmo,
     gp_g_final, gp_b_gate, dgate, loss_part) = _hub(
        xs, tgt, hr, pf, o, mod, b_mod, b_gate, g_final.reshape(1, D_MODEL), w_or_all, w_oa_all, w_o_all)
    gp_out, gp_out_low = _pair_grads("out_grads", [u_rnn, u_attn, merged], [dy_rnn, dy_attn, dmo])
    dq, dk, dv = _attn_bwd(pf, d_o, o, lses, major, cosf, sinf)
    dx_rnn, gp_conv_w, gp_conv_b, gp_w_a, gp_b_a, gp_w_x, gp_b_x, gp_lam = _rnn_bwd(
        pf, hr, dhr, rnn_saved, keep, rnn_params[0], rnn_params[2], rnn_params[4], rnn_params[6])
    pieces = [dx_rnn, dz_rnn, dq, dk, dv, dz_attn, dg_r, dg_a]
    gp_w_in, gp_w_in_low = _pair_grads("w_in_grad", [h], pieces)

    by_target = lambda t: [(t.reshape(3, N_DEV, 128, D_MODEL), i) for i in range(3)]
    stacks = [(gp_w_in, None)] + by_target(gp_out)
    from_sib = _rs_to_sibling("rs_sibling", [(gp_w_in_low, None)] + by_target(gp_out_low))
    targets = jnp.bitwise_xor(me, 2 * jnp.arange(4, dtype=jnp.int32)).astype(jnp.int32)
    sums = [_add_sibling("rs_add_sibling_%d" % a, s_, r_, targets) for a, (s_, r_) in enumerate(zip(stacks, from_sib))]
    sends = [send for _, send in sums]
    reduce_flight = _split_start("rs_chips_start", _chip_copies, 3 * len(sends), sends,
                                 [lax.empty(t.shape, t.dtype) for t in sends])

    mod_after = mod + reduce_flight[4][0:1, 0:1]
    grad_x, dshift, dscale, gp_g_norm = _dh_dx(pieces, w_in_near, w_in_sib, w_in_far, xs, dx2, mod_after, b_mod,
                                               g_norm)

    flat = lambda t: t.reshape(-1, 128)
    dmod = flat(jnp.concatenate([dshift, dscale, dgate], axis=1))
    dmod_placed = lax.dynamic_update_slice(jnp.zeros((N_DEV * dmod.shape[0], 128), F32), dmod, (me * dmod.shape[0], 0))
    small = [flat(gp_g_norm), flat(gp_b_gate), flat(gp_conv_b), flat(gp_b_a), flat(gp_b_x), flat(gp_lam),
             flat(gp_g_final), flat(gp_conv_w), jnp.broadcast_to(loss_part, (SUBLANES, 128)),
             flat(gp_w_a), flat(gp_w_x), dmod_placed]
    sizes = [t.shape[0] for t in small]
    small.append(jnp.zeros((-sum(sizes) % (2 * SUBLANES), 128), F32))
    total = _allreduce_small("allreduce_small_grads", jnp.concatenate(small, axis=0))
    offs = [sum(sizes[:i]) for i in range(len(sizes))]
    (g_g_norm, g_b_gate, g_conv_b, g_b_a, g_b_x, g_lam, g_g_final, g_conv_w_all, loss_rows, g_w_a, g_w_x,
     dmod_rows) = (total[o_:o_ + s_] for o_, s_ in zip(offs, sizes))
    loss = loss_rows[0, 0]
    g_conv_w = lax.dynamic_index_in_dim(g_conv_w_all.reshape(RNN_BLOCKS, SUBLANES, 128), me, axis=0,
                                        keepdims=False)[:4]

    dmod_all = dmod_rows.reshape(N_DEV, 3 * D_MODEL)
    dmod_cols = lax.dynamic_slice_in_dim(dmod_all, me * mod_cols, mod_cols, axis=1)
    g_b_mod, g_w_mod = _mod_bwd(c_all, dmod_all, dmod_cols)

    _, from_chips = _split_wait("rs_chips_wait", _chip_copies, reduce_flight, total)

    results = {}
    sharded = (("w_in", w_in, m_w_in, v_w_in, (D_MODEL, D_MODEL)),
               ("w_out_rnn", w_out_rnn, m_w_out_rnn, v_w_out_rnn, (128, D_MODEL)),
               ("w_out_attn", w_out_attn, m_w_out_attn, v_w_out_attn, (128, D_MODEL)),
               ("w_o", w_o, m_w_o, v_w_o, (128, D_MODEL)))
    for (name, w_, m_, v_, shape2), (own, _), arrived in zip(sharded, sums, from_chips):
        d_, nm_, nv_, g_ = _adamw("adamw_" + name, w_.reshape(shape2), own, m_.reshape(shape2), v_.reshape(shape2),
                                  arrived)
        results[name] = (g_, d_, nm_, nv_)
    shape2 = (D_MODEL, mod_cols)
    results["w_mod"] = (g_w_mod,) + tuple(_adamw("adamw_w_mod", w_mod.reshape(shape2), g_w_mod,
                                                 m_w_mod.reshape(shape2), v_w_mod.reshape(shape2)))
    lanes = (("g_norm", g_norm, g_g_norm, m_g_norm, v_g_norm), ("b_mod", b_mod, g_b_mod, m_b_mod, v_b_mod),
             ("b_gate", b_gate, g_b_gate, m_b_gate, v_b_gate), ("conv_w", conv_w, g_conv_w, m_conv_w, v_conv_w),
             ("conv_b", conv_b, g_conv_b, m_conv_b, v_conv_b), ("w_a", w_a, g_w_a, m_w_a, v_w_a),
             ("b_a", b_a, g_b_a, m_b_a, v_b_a), ("w_x", w_x, g_w_x, m_w_x, v_w_x), ("b_x", b_x, g_b_x, m_b_x, v_b_x),
             ("lam", lam, g_lam, m_lam, v_lam), ("g_final", g_final, g_g_final, m_g_final, v_g_final))
    for name, w_, g_, m_, v_ in lanes:
        rows128 = lambda t: t.reshape(-1, 128)
        results[name] = (g_,) + tuple(_adamw("adamw_" + name, rows128(w_), rows128(g_), rows128(m_), rows128(v_)))
    order = ("g_norm", "w_mod", "b_mod", "w_in", "b_gate", "conv_w", "conv_b", "w_a", "b_a", "w_x", "b_x", "lam",
             "w_out_rnn", "w_out_attn", "w_o", "g_final")
    given = dict(g_norm=g_norm, w_mod=w_mod, b_mod=b_mod, w_in=w_in, b_gate=b_gate, conv_w=conv_w, conv_b=conv_b,
                 w_a=w_a, b_a=b_a, w_x=w_x, b_x=b_x, lam=lam, w_out_rnn=w_out_rnn, w_out_attn=w_out_attn, w_o=w_o,
                 g_final=g_final)
    outs = [[results[name][k].reshape(given[name].shape) for name in order] for k in range(4)]
    return (loss, grad_x[None], *outs[0], *outs[1], *outs[2], *outs[3])
```

```python
import jax
import jax.numpy as jnp
from jax import lax
from jax.experimental import pallas as pl
from jax.experimental.pallas import tpu as pltpu

F32 = jnp.float32
BF16 = jnp.bfloat16
MESH = pl.DeviceIdType.MESH

D_MODEL = 1024
N_HEADS = 8
HEAD_DIM = 128
RNN_BLOCKS = 8
N_DEV = 8
ROT_HALF = 16
ROPE_THETA = 500000.0
DILATIONS = (1, 4, 16)
KEY_BLOCK = 128
SPAN = KEY_BLOCK * DILATIONS[-1]
ATTN_SCALE = HEAD_DIM ** -0.5
NORM_EPS = 1e-6
LRU_C = 8.0
NEG_INF = -1e30
ADAM_LR, ADAM_B1, ADAM_B2, ADAM_EPS, ADAM_WD, ADAM_STEP = 0.001, 0.9, 0.999, 1e-08, 0.01, 10

SUBLANES = 8
VMEM_LIMIT = 56 * 1024 * 1024
PROJ_ROWS = 1024
RNN_ROWS = 2048
HUB_ROWS = 256
DX_ROWS = 512
WGRAD_ROWS = 1024
ADD_ROWS = 256


def _params(sem=None, vmem=None):
    return pltpu.CompilerParams(dimension_semantics=sem, vmem_limit_bytes=vmem)


def _dot(a, b):
    return jnp.dot(a, b, preferred_element_type=F32)


def _dot_nt(a, b):
    return lax.dot_general(a, b, (((1,), (1,)), ((), ())), preferred_element_type=F32)


def _dot_tn(a, b):
    return lax.dot_general(a, b, (((0,), (0,)), ((), ())), preferred_element_type=F32)


def _sigmoid(z):
    return 1.0 / (1.0 + jnp.exp(-z))


def _expm1_nonpos(z, exp_z):
    return jnp.where(z > -0.01, z * (1.0 + 0.5 * z), exp_z - 1.0)


def _my_pos():
    return lax.axis_index("x"), lax.axis_index("y"), lax.axis_index("c")


def _flip(pos, k):
    x, y, c = pos
    return ((1 - x) if k & 4 else x, (1 - y) if k & 2 else y, (1 - c) if k & 1 else c)


def _index(pos):
    return 4 * pos[0] + 2 * pos[1] + pos[2]


def _ag_small(name, v):
    rows, cols = v.shape

    def body(v_ref, out_ref, send_sems, recv_sems):
        me = _my_pos()
        out_ref[_index(me)] = v_ref[...]
        sends = []
        for k in range(1, N_DEV):
            cp = pltpu.make_async_remote_copy(
                src_ref=v_ref, dst_ref=out_ref.at[_index(me)], send_sem=send_sems.at[k - 1],
                recv_sem=recv_sems.at[k - 1], device_id=_flip(me, k), device_id_type=MESH)
            cp.start()
            sends.append(cp)
        for k in range(1, N_DEV):
            peer = _flip(me, k)
            pltpu.make_async_remote_copy(
                src_ref=v_ref, dst_ref=out_ref.at[_index(peer)], send_sem=send_sems.at[k - 1],
                recv_sem=recv_sems.at[k - 1], device_id=peer, device_id_type=MESH).wait_recv()
        for cp in sends:
            cp.wait_send()

    return pl.pallas_call(
        body, name=name,
        out_shape=jax.ShapeDtypeStruct((N_DEV, rows, cols), v.dtype),
        in_specs=[pl.BlockSpec(memory_space=pltpu.VMEM)],
        out_specs=pl.BlockSpec(memory_space=pltpu.VMEM),
        scratch_shapes=[pltpu.SemaphoreType.DMA((N_DEV - 1,)), pltpu.SemaphoreType.DMA((N_DEV - 1,))],
        compiler_params=_params(None, VMEM_LIMIT),
    )(v)


def _split_start(name, make_copies, nsem, srcs, lands):
    n, k = len(srcs), len(lands)

    def body(*refs):
        for cp in make_copies(refs[:n], refs[n:n + k], refs[n + k], refs[n + k + 1]):
            cp.start()
        refs[-1][...] = jnp.zeros_like(refs[-1])

    hbm = pl.BlockSpec(memory_space=pltpu.HBM)
    sem = pl.BlockSpec(memory_space=pltpu.SEMAPHORE)
    arrays = [*srcs, *lands]
    outs = pl.pallas_call(
        body, name=name,
        out_shape=(pltpu.SemaphoreType.DMA((nsem,)), pltpu.SemaphoreType.DMA((nsem,)),
                   *[pltpu.HBM(t.shape, t.dtype) for t in arrays], jax.ShapeDtypeStruct((SUBLANES, 128), F32)),
        in_specs=[hbm] * (n + k),
        out_specs=(sem, sem, *[hbm] * (n + k), pl.BlockSpec(memory_space=pltpu.VMEM)),
        input_output_aliases={i: 2 + i for i in range(n + k)},
        compiler_params=pltpu.CompilerParams(has_side_effects=pltpu.SideEffectType.DATAFLOW_SIDE_EFFECTING),
    )(*[pltpu.with_memory_space_constraint(t, pltpu.HBM) for t in arrays])
    return outs[0], outs[1], outs[2:2 + n], outs[2 + n:2 + n + k], outs[-1]


def _split_wait(name, make_copies, flight, after):
    send_sems, recv_sems, srcs, lands, _ = flight
    n, k = len(srcs), len(lands)

    def body(*refs):
        for cp in make_copies(refs[:n], refs[n:n + k], refs[n + k], refs[n + k + 1]):
            cp.wait_send()
            cp.wait_recv()

    hbm = pl.BlockSpec(memory_space=pltpu.HBM)
    sem = pl.BlockSpec(memory_space=pltpu.SEMAPHORE)
    arrays = [*srcs, *lands]
    outs = pl.pallas_call(
        body, name=name, out_shape=tuple(pltpu.HBM(t.shape, t.dtype) for t in arrays),
        in_specs=[hbm] * (n + k) + [sem, sem, pl.BlockSpec(memory_space=pl.ANY)],
        out_specs=[hbm] * (n + k),
        input_output_aliases={i: i for i in range(n + k)},
        compiler_params=pltpu.CompilerParams(has_side_effects=pltpu.SideEffectType.DATAFLOW_SIDE_EFFECTING),
    )(*arrays, send_sems, recv_sems, after)
    return outs[:n], outs[n:]


def _remote(src, dst, send_sems, recv_sems, k, to):
    return pltpu.make_async_remote_copy(src_ref=src, dst_ref=dst, send_sem=send_sems.at[k], recv_sem=recv_sems.at[k],
                                        device_id=to, device_id_type=MESH)


def _peer_copies(shards, lands, send_sems, recv_sems):
    me = _my_pos()
    return [_remote(shards[a], lands[a].at[_index(me)], send_sems, recv_sems, a * 7 + k - 1, _flip(me, k))
            for a in range(len(shards)) for k in range(1, N_DEV)]


def _own_block_copies(shards, lands, send_sems, recv_sems):
    me = _my_pos()
    return [_remote(shards[0], lands[0].at[_index(me)], send_sems, recv_sems, m - 1, _flip(me, 2 * m))
            for m in range(1, 4)]


def _sibling_copy(shards, lands, send_sems, recv_sems):
    me = _my_pos()
    return [_remote(shards[0], lands[0].at[_index(me)], send_sems, recv_sems, 0, _flip(me, 1))]


def _forward_copies(arrived, lands, send_sems, recv_sems):
    me = _my_pos()
    return [_remote(arrived[0].at[_index(_flip(me, 2 * m))], lands[0].at[_index(_flip(me, 2 * m))],
                    send_sems, recv_sems, m - 1, _flip(me, 1)) for m in range(1, 4)]


def _rs_to_sibling(name, stacks):
    n = len(stacks)

    def body(*refs):
        ins, outs = refs[:n], refs[n:2 * n]
        send_sems, recv_sems = refs[2 * n:]
        me = _my_pos()
        sib = _flip(me, 1)
        sends = []
        for a, (_, which) in enumerate(stacks):
            by_target = ins[a] if which is None else ins[a].at[which]
            for m in range(4):
                target = _flip(sib, 2 * m)
                cp = pltpu.make_async_remote_copy(
                    src_ref=by_target.at[_index(target)], dst_ref=outs[a].at[m],
                    send_sem=send_sems.at[a * 4 + m], recv_sem=recv_sems.at[a * 4 + m],
                    device_id=sib, device_id_type=MESH)
                cp.start()
                sends.append(cp)
        for cp in sends:
            cp.wait_recv()
        for cp in sends:
            cp.wait_send()

    any_spec = pl.BlockSpec(memory_space=pl.ANY)
    return pl.pallas_call(
        body, name=name,
        out_shape=[jax.ShapeDtypeStruct((4,) + s.shape[-2:], s.dtype) for s, _ in stacks],
        in_specs=[any_spec] * n, out_specs=[any_spec] * n,
        scratch_shapes=[pltpu.SemaphoreType.DMA((4 * n,)), pltpu.SemaphoreType.DMA((4 * n,))],
    )(*[s for s, _ in stacks])


def _chip_copies(srcs, lands, send_sems, recv_sems):
    me = _my_pos()
    return [_remote(srcs[a].at[m - 1], lands[a].at[m - 1], send_sems, recv_sems, a * 3 + m - 1, _flip(me, 2 * m))
            for a in range(len(srcs)) for m in range(1, 4)]


def _add_sibling(name, stack, recv, targets):
    stack, which = stack
    rows, cols = stack.shape[-2:]
    tr = min(rows, ADD_ROWS)

    def by_target(index):
        if which is None:
            return pl.BlockSpec((None, tr, cols), lambda *g: (index(*g), g[-2], 0))
        return pl.BlockSpec((None, None, tr, cols), lambda *g: (which, index(*g), g[-2], 0))

    def own_body(t_ref, a_ref, b_ref, o_ref):
        o_ref[...] = a_ref[...] + b_ref[...].astype(F32)

    own = pl.pallas_call(
        own_body, name=name + "_own",
        out_shape=jax.ShapeDtypeStruct((rows, cols), F32),
        grid_spec=pltpu.PrefetchScalarGridSpec(
            num_scalar_prefetch=1, grid=(rows // tr,),
            in_specs=[by_target(lambda i, t: t[0]),
                      pl.BlockSpec((None, tr, cols), lambda i, t: (0, i, 0))],
            out_specs=pl.BlockSpec((tr, cols), lambda i, t: (i, 0))),
        compiler_params=_params(("arbitrary",)),
    )(targets, stack, recv)

    def send_body(t_ref, a_ref, b_ref, o_ref):
        o_ref[...] = (a_ref[...] + b_ref[...].astype(F32)).astype(BF16)

    send = pl.pallas_call(
        send_body, name=name + "_send",
        out_shape=jax.ShapeDtypeStruct((3, rows, cols), BF16),
        grid_spec=pltpu.PrefetchScalarGridSpec(
            num_scalar_prefetch=1, grid=(3, rows // tr),
            in_specs=[by_target(lambda m, i, t: t[m + 1]),
                      pl.BlockSpec((None, tr, cols), lambda m, i, t: (m + 1, i, 0))],
            out_specs=pl.BlockSpec((None, tr, cols), lambda m, i, t: (m, i, 0))),
        compiler_params=_params(("arbitrary", "arbitrary")),
    )(targets, stack, recv)
    return own, send


def _allreduce_small(name, v):
    rows, cols = v.shape
    half = rows // 2
    assert rows % (2 * SUBLANES) == 0

    def body(v_ref, out_ref, from_sib, chip_half, from_chips, send_sems, recv_sems):
        me = _my_pos()
        sib = _flip(me, 1)
        mine = pl.ds(pl.multiple_of(me[2] * half, SUBLANES), half)
        theirs = pl.ds(pl.multiple_of((1 - me[2]) * half, SUBLANES), half)

        def copy(k, src, dst, to):
            return pltpu.make_async_remote_copy(src_ref=src, dst_ref=dst, send_sem=send_sems.at[k],
                                                recv_sem=recv_sems.at[k], device_id=to, device_id_type=MESH)

        to_sib = copy(0, v_ref.at[theirs], from_sib, sib)
        to_sib.start()
        to_sib.wait_recv()
        chip_half[...] = v_ref[mine, :] + from_sib[...]
        to_chips = [copy(m, chip_half, from_chips.at[m - 1], _flip(me, 2 * m)) for m in range(1, 4)]
        for cp in to_chips:
            cp.start()
        for cp in to_chips:
            cp.wait_recv()
        my_chip = 2 * me[0] + me[1]
        total = None
        for chip in range(4):
            slot = jnp.maximum(jnp.bitwise_xor(chip, my_chip) - 1, 0)
            part = jnp.where(chip == my_chip, chip_half[...], from_chips[slot])
            total = part if total is None else total + part
        out_ref[mine, :] = total
        swap = copy(4, out_ref.at[mine], out_ref.at[mine], sib)
        swap.start()
        copy(4, out_ref.at[theirs], out_ref.at[theirs], sib).wait_recv()
        for cp in [to_sib, swap] + to_chips:
            cp.wait_send()

    return pl.pallas_call(
        body, name=name, out_shape=jax.ShapeDtypeStruct((rows, cols), F32),
        in_specs=[pl.BlockSpec(memory_space=pltpu.VMEM)],
        out_specs=pl.BlockSpec(memory_space=pltpu.VMEM),
        scratch_shapes=[pltpu.VMEM((half, cols), F32), pltpu.VMEM((half, cols), F32),
                        pltpu.VMEM((3, half, cols), F32),
                        pltpu.SemaphoreType.DMA((5,)), pltpu.SemaphoreType.DMA((5,))],
        compiler_params=_params(None, VMEM_LIMIT),
    )(v)


def _mod_fwd(c_all, w_mod):
    def body(c_ref, w_ref, o_ref):
        c = c_ref[...]
        o_ref[...] = jnp.dot(c * _sigmoid(c), w_ref[...], preferred_element_type=F32,
                             precision=lax.Precision.HIGHEST)

    return pl.pallas_call(
        body, name="mod_fwd", out_shape=jax.ShapeDtypeStruct((N_DEV, w_mod.shape[1]), F32),
    )(c_all, w_mod)


def _mod_bwd(c_all, dmod_all, dmod_cols):
    def body(c_ref, da_ref, dc_ref, gb_ref, gw_ref):
        c = c_ref[...]
        acc = da_ref[0:1, :]
        for b in range(1, N_DEV):
            acc = acc + da_ref[b:b + 1, :]
        gb_ref[...] = acc
        gw_ref[...] = lax.dot_general(c * _sigmoid(c), dc_ref[...], (((0,), (0,)), ((), ())),
                                      preferred_element_type=F32, precision=lax.Precision.HIGHEST)

    return pl.pallas_call(
        body, name="mod_bwd",
        out_shape=[jax.ShapeDtypeStruct((1, dmod_all.shape[1]), F32),
                   jax.ShapeDtypeStruct((c_all.shape[1], dmod_cols.shape[1]), F32)],
    )(c_all, dmod_all, dmod_cols)


def _rope_partner(t):
    lane = lax.broadcasted_iota(jnp.int32, t.shape, 1)
    return jnp.where(lane < ROT_HALF, pltpu.roll(t, HEAD_DIM - ROT_HALF, 1), pltpu.roll(t, ROT_HALF, 1))


def _norm(x, mod, b_mod, g_norm):
    seq = x.shape[0]
    tm = PROJ_ROWS

    def body(x_ref, mod_ref, bmod_ref, g_ref, h_ref):
        xf = x_ref[...]
        rstd = lax.rsqrt(jnp.mean(xf * xf, axis=-1, keepdims=True) + NORM_EPS)
        shift = mod_ref[:, 0:D_MODEL] + bmod_ref[:, 0:D_MODEL]
        scale = mod_ref[:, D_MODEL:2 * D_MODEL] + bmod_ref[:, D_MODEL:2 * D_MODEL]
        h_ref[...] = (((xf * rstd) * g_ref[...]) * (1.0 + scale) + shift).astype(BF16)

    row = pl.BlockSpec((tm, D_MODEL), lambda i: (i, 0))
    const = lambda cols: pl.BlockSpec((1, cols), lambda i: (0, 0))
    return pl.pallas_call(
        body, name="norm", out_shape=jax.ShapeDtypeStruct((seq, D_MODEL), BF16), grid=(seq // tm,),
        in_specs=[row, const(3 * D_MODEL), const(3 * D_MODEL), const(D_MODEL)], out_specs=row,
        compiler_params=_params(("arbitrary",), VMEM_LIMIT),
    )(x, mod, b_mod, g_norm)


def _proj(name, h, w, slots, pieces, cosf, sinf, prior):
    seq = h.shape[0]
    tm = PROJ_ROWS
    count = pieces.shape[0]

    def body(slots_ref, pieces_ref, h_ref, w_hbm, cos_ref, sin_ref, *rest):
        out_ref, w_vmem, sems = rest[-3:]
        j = pl.program_id(1)
        piece = pieces_ref[j]

        @pl.when((pl.program_id(0) == 0) & (j == 0))
        def _():
            copies = [pltpu.make_async_copy(w_hbm.at[slots_ref[m]], w_vmem.at[m], sems.at[m])
                      for m in range(count)]
            for cp in copies:
                cp.start()
            for cp in copies:
                cp.wait()

        w_ref = w_vmem.at[j]

        @pl.when((piece < 2) | (piece > 3))
        def _():
            out_ref[...] = _dot(h_ref[...], w_ref[...])

        def rotated(gain):
            for pair in range(N_HEADS // 2):
                both = _dot(h_ref[...], w_ref[:, 2 * pair * HEAD_DIM:2 * (pair + 1) * HEAD_DIM])
                for hh in (2 * pair, 2 * pair + 1):
                    t = both[:, (hh % 2) * HEAD_DIM:(hh % 2 + 1) * HEAD_DIM]
                    t = t * cos_ref[...] + _rope_partner(t) * sin_ref[...]
                    out_ref[:, hh * HEAD_DIM:(hh + 1) * HEAD_DIM] = t if gain is None else t * gain

        @pl.when(piece == 2)
        def _():
            rotated(ATTN_SCALE)

        @pl.when(piece == 3)
        def _():
            rotated(None)

    row = lambda i, j, sl, pc: (i, 0)

    def table(i, j, sl, pc):
        rotates = (pc[0] == 2) | (pc[0] == 3)
        for m in range(1, count):
            rotates |= (pc[m] == 2) | (pc[m] == 3)
        return (jnp.where(rotates, i, 0), 0)

    in_specs = [pl.BlockSpec((tm, D_MODEL), row), pl.BlockSpec(memory_space=pl.ANY),
                pl.BlockSpec((tm, HEAD_DIM), table), pl.BlockSpec((tm, HEAD_DIM), table)]
    args = [slots, pieces, h, w, cosf, sinf]
    aliases = {}
    if prior is not None:
        in_specs.append(pl.BlockSpec(memory_space=pl.ANY))
        args.append(prior)
        aliases = {6: 0}
    return pl.pallas_call(
        body, name=name,
        out_shape=jax.ShapeDtypeStruct((seq, 8 * D_MODEL), F32),
        grid_spec=pltpu.PrefetchScalarGridSpec(
            num_scalar_prefetch=2, grid=(seq // tm, count), in_specs=in_specs,
            out_specs=pl.BlockSpec((tm, D_MODEL), lambda i, j, sl, pc: (i, pc[j])),
            scratch_shapes=[pltpu.VMEM((count, D_MODEL, D_MODEL), BF16), pltpu.SemaphoreType.DMA((count,))]),
        input_output_aliases=aliases,
        compiler_params=_params(("arbitrary", "arbitrary"), VMEM_LIMIT),
    )(*args)


def _shift_down(v, s, head):
    rolled = pltpu.roll(v, s, 0)
    row = lax.broadcasted_iota(jnp.int32, head.shape, 0)
    first = jnp.where(row < s, pltpu.roll(head, s, 0), rolled[:SUBLANES, :])
    return jnp.concatenate([first, rolled[SUBLANES:, :]], axis=0)


def _shift_up(v, s, tail):
    rows = v.shape[0]
    rolled = pltpu.roll(v, rows - s, 0)
    row = lax.broadcasted_iota(jnp.int32, tail.shape, 0)
    last = jnp.where(row >= SUBLANES - s, pltpu.roll(tail, SUBLANES - s, 0), rolled[rows - SUBLANES:, :])
    return jnp.concatenate([rolled[:rows - SUBLANES, :], last], axis=0)


def _doubling(a, b, period, reverse):
    rows = a.shape[0]
    pos = lax.broadcasted_iota(jnp.int32, a.shape, 0) & (period - 1)
    k = 1
    while k < period:
        inside = (pos < period - k) if reverse else (pos >= k)
        shift = rows - k if reverse else k
        a_s = jnp.where(inside, pltpu.roll(a, shift, 0), 1.0)
        b_s = jnp.where(inside, pltpu.roll(b, shift, 0), 0.0)
        b = a * b_s + b
        a = a * a_s
        k *= 2
    return a, b


def _scan(a, b, boundary, reverse, a_scr, b_scr, spread):
    rows = a.shape[0]
    ntile = rows // SUBLANES
    a_scr[...], b_scr[...] = _doubling(a, b, SUBLANES, reverse)
    ends = pl.ds(0 if reverse else SUBLANES - 1, ntile, stride=SUBLANES)
    a_end, x_end = _doubling(a_scr[ends, :], b_scr[ends, :], ntile, reverse)
    x_end = x_end + a_end * boundary
    tile = lax.broadcasted_iota(jnp.int32, x_end.shape, 0)
    if reverse:
        incoming = jnp.where(tile == ntile - 1, boundary, pltpu.roll(x_end, ntile - 1, 0))
        last = x_end[0:1, :]
    else:
        incoming = jnp.where(tile == 0, boundary, pltpu.roll(x_end, 1, 0))
        last = x_end[ntile - 1:ntile, :]
    for s in range(SUBLANES):
        spread[pl.ds(s, ntile, stride=SUBLANES), :] = incoming
    return b_scr[...] + a_scr[...] * spread[...], last


def _conv_taps(xr, head):
    return [_shift_down(xr, 3, head), _shift_down(xr, 2, head), _shift_down(xr, 1, head), xr]


def _rnn_gates(xc, wa, ba, wx, bx, lam, keep):
    xcb = xc.astype(BF16)
    r = _sigmoid(_dot(xcb, wa.astype(BF16)) + ba)
    i = _sigmoid(_dot(xcb, wx.astype(BF16)) + bx)
    softplus = jnp.maximum(-lam, 0.0) + jnp.log(1.0 + jnp.exp(-jnp.abs(lam)))
    cl = -LRU_C * softplus
    log_a = cl * r
    a_raw = jnp.exp(log_a)
    mult_raw = jnp.sqrt(-_expm1_nonpos(2.0 * log_a, a_raw * a_raw))
    live = keep > 0.0
    return r, i, cl, a_raw, mult_raw, jnp.where(live, a_raw, 0.0), jnp.where(live, mult_raw, 1.0), live


def _rnn_specs(seq, rows, time_of):
    per = rows // SUBLANES
    vec = pl.BlockSpec((None, 1, 128), lambda hb, n: (hb, 0, 0))
    mat = pl.BlockSpec((None, 128, 128), lambda hb, n: (hb, 0, 0))
    return [pl.BlockSpec((rows, 128), lambda hb, n: (time_of(n), hb)),
            pl.BlockSpec((SUBLANES, 128), lambda hb, n: (jnp.maximum(time_of(n) * per - 1, 0), hb)),
            pl.BlockSpec((rows, 1), lambda hb, n: (time_of(n), 0)),
            pl.BlockSpec((None, SUBLANES, 128), lambda hb, n: (hb, 0, 0)),
            vec, mat, vec, mat, vec, vec]


def _rnn_fwd(pf, keep, conv_w8, conv_b, w_a, b_a, w_x, b_x, lam):
    seq = pf.shape[0]
    rows = RNN_ROWS

    def body(x_ref, xh_ref, keep_ref, cw_ref, cb_ref, wa_ref, ba_ref, wx_ref, bx_ref, lam_ref,
             hr_ref, xc_ref, r_ref, i_ref, araw_ref, mraw_ref, carry, a_scr, b_scr, spread):
        n = pl.program_id(1)

        @pl.when(n == 0)
        def _():
            carry[...] = jnp.zeros_like(carry)

        xr = x_ref[...]
        head = jnp.where(n > 0, xh_ref[...], 0.0)
        taps = _conv_taps(xr, head)
        xc = cb_ref[...] + sum(cw_ref[k:k + 1, :] * taps[k] for k in range(4))
        r, i, _, a_raw, mult_raw, a, mult, _ = _rnn_gates(xc, wa_ref[...], ba_ref[...], wx_ref[...], bx_ref[...],
                                                          lam_ref[...], keep_ref[...])
        xc_ref[...], r_ref[...], i_ref[...], araw_ref[...], mraw_ref[...] = xc, r, i, a_raw, mult_raw
        h, last = _scan(a, mult * i * xc, carry[0:1, :], False, a_scr, b_scr, spread)
        hr_ref[...] = h
        carry[...] = jnp.broadcast_to(last, carry.shape)

    chunk_f32 = pltpu.VMEM((rows, 128), F32)
    chunk = pl.BlockSpec((rows, 128), lambda hb, n: (n, hb))
    shape = jax.ShapeDtypeStruct((seq, D_MODEL), F32)
    outs = pl.pallas_call(
        body, name="rnn_fwd",
        out_shape=[shape] * 6,
        grid=(RNN_BLOCKS, seq // rows),
        in_specs=_rnn_specs(seq, rows, lambda n: n),
        out_specs=[chunk] * 6,
        scratch_shapes=[pltpu.VMEM((SUBLANES, 128), F32), chunk_f32, chunk_f32, chunk_f32],
        compiler_params=_params(("arbitrary", "arbitrary"), VMEM_LIMIT),
    )(pf, pf, keep, conv_w8, conv_b, w_a, b_a, w_x, b_x, lam)
    return outs[0], tuple(outs[1:])


def _rnn_bwd(pf, hr, dhr, saved, keep, conv_w8, w_a, w_x, lam):
    seq = pf.shape[0]
    rows = RNN_ROWS
    nchunk = seq // rows
    per = rows // SUBLANES
    time_of = lambda n: nchunk - 1 - n

    def body(x_ref, keep_ref, cw_ref, wa_ref, wx_ref, lam_ref, hr_ref, hrh_ref, dhr_ref,
             xc_ref, r_ref, i_ref, araw_ref, mraw_ref,
             dx_ref, gcw_ref, gcb_ref, gwa_ref, gba_ref, gwx_ref, gbx_ref, glam_ref,
             g_carry, dxc_tail, a_scr, b_scr, spread):
        n = pl.program_id(1)
        first_in_time = n == nchunk - 1

        @pl.when(n == 0)
        def _():
            g_carry[...] = jnp.zeros_like(g_carry)
            dxc_tail[...] = jnp.zeros_like(dxc_tail)
            for ref in (gcw_ref, gcb_ref, gwa_ref, gba_ref, gwx_ref, gbx_ref, glam_ref):
                ref[...] = jnp.zeros_like(ref)

        cw, wa, wx, lam = cw_ref[...], wa_ref[...], wx_ref[...], lam_ref[...]
        xc, r, i, a_raw, mult_raw = xc_ref[...], r_ref[...], i_ref[...], araw_ref[...], mraw_ref[...]
        cl = -LRU_C * (jnp.maximum(-lam, 0.0) + jnp.log(1.0 + jnp.exp(-jnp.abs(lam))))
        live = keep_ref[...] > 0.0
        a, mult = jnp.where(live, a_raw, 0.0), jnp.where(live, mult_raw, 1.0)
        h_prev = _shift_down(hr_ref[...], 1, jnp.where(first_in_time, 0.0, hrh_ref[...]))

        row = lax.broadcasted_iota(jnp.int32, xc.shape, 0)
        last = row == rows - 1
        a_next = jnp.where(last, 0.0, pltpu.roll(a, rows - 1, 0))
        g, g_first = _scan(a_next, dhr_ref[...] + jnp.where(last, g_carry[0:1, :], 0.0),
                           jnp.zeros((1, 128), F32), True, a_scr, b_scr, spread)
        g_carry[...] = jnp.broadcast_to(a[0:1, :] * g_first, g_carry.shape)

        da = g * h_prev
        dmult = g * i * xc
        di = g * mult * xc
        dxc = g * mult * i
        dlog_a = jnp.where(live, da * a_raw - dmult * a_raw * a_raw / mult_raw, 0.0)
        dpa = (dlog_a * cl) * r * (1.0 - r)
        dpx = di * i * (1.0 - i)
        glam_ref[...] += jnp.sum(dlog_a * r, axis=0, keepdims=True) * (LRU_C * _sigmoid(-lam))
        xcb, dpab, dpxb = xc.astype(BF16), dpa.astype(BF16), dpx.astype(BF16)
        gwa_ref[...] += _dot_tn(xcb, dpab)
        gwx_ref[...] += _dot_tn(xcb, dpxb)
        gba_ref[...] += jnp.sum(dpa, axis=0, keepdims=True)
        gbx_ref[...] += jnp.sum(dpx, axis=0, keepdims=True)
        dxc = dxc + _dot_nt(dpab, wa.astype(BF16)) + _dot_nt(dpxb, wx.astype(BF16))

        gcb_ref[...] += jnp.sum(dxc, axis=0, keepdims=True)
        xr = x_ref[...]
        tail = dxc_tail[...]
        later = [_shift_up(dxc, 3 - k, tail) for k in range(3)] + [dxc]
        dx = cw[3:4, :] * dxc
        for k in range(3):
            dx = dx + cw[k:k + 1, :] * later[k]
        for k in range(4):
            gcw_ref[k:k + 1, :] += jnp.sum(xr * later[k], axis=0, keepdims=True)
        dx_ref[...] = dx.astype(BF16)
        dxc_tail[...] = dxc[0:SUBLANES, :]

    blk = lambda hb, n: (hb, 0, 0)
    chunk = pl.BlockSpec((rows, 128), lambda hb, n: (time_of(n), hb))
    vec = pl.BlockSpec((None, 1, 128), blk)
    mat = pl.BlockSpec((None, 128, 128), blk)
    vec_shape = jax.ShapeDtypeStruct((RNN_BLOCKS, 1, 128), F32)
    mat_shape = jax.ShapeDtypeStruct((RNN_BLOCKS, 128, 128), F32)
    return pl.pallas_call(
        body, name="rnn_bwd",
        out_shape=[jax.ShapeDtypeStruct((seq, D_MODEL), BF16),
                   jax.ShapeDtypeStruct((RNN_BLOCKS, SUBLANES, 128), F32), vec_shape,
                   mat_shape, vec_shape, mat_shape, vec_shape, vec_shape],
        grid=(RNN_BLOCKS, nchunk),
        in_specs=[chunk, pl.BlockSpec((rows, 1), lambda hb, n: (time_of(n), 0)),
                  pl.BlockSpec((None, SUBLANES, 128), blk), mat, mat, vec, chunk,
                  pl.BlockSpec((SUBLANES, 128), lambda hb, n: (jnp.maximum(time_of(n) * per - 1, 0), hb)), chunk]
                 + [chunk] * 5,
        out_specs=[chunk, pl.BlockSpec((None, SUBLANES, 128), blk), vec, mat, vec, mat, vec, vec],
        scratch_shapes=[pltpu.VMEM((SUBLANES, 128), F32), pltpu.VMEM((SUBLANES, 128), F32)]
                       + [pltpu.VMEM((rows, 128), F32)] * 3,
        compiler_params=_params(("arbitrary", "arbitrary"), VMEM_LIMIT),
    )(pf, keep, conv_w8, w_a, w_x, lam, hr, hr, dhr, *saved)


def _unit_rows(dil, r, j):
    start = j * KEY_BLOCK * dil + r
    return pl.ds(start, KEY_BLOCK) if dil == 1 else pl.ds(start, KEY_BLOCK, stride=dil)


def _attn_fwd(proj):
    nh, seq = N_HEADS, proj.shape[0]
    nchunk = seq // SPAN
    nblk = SPAN // KEY_BLOCK
    wide = DILATIONS[-1]

    def body(q_ref, k_ref, v_ref, kp_ref, vp_ref, o_ref, l1_ref, l4_ref, l16_ref, q16, k16, v16, o16,
             acc, m_s, l_s, k16p, v16p, acc16, m16, l16, tmp):
        n = pl.program_id(1)
        qi = lax.broadcasted_iota(jnp.int32, (KEY_BLOCK, KEY_BLOCK), 0)
        ki = lax.broadcasted_iota(jnp.int32, (KEY_BLOCK, KEY_BLOCK), 1)
        bias_own = jnp.where(ki <= qi, 0.0, NEG_INF)
        bias_before = jnp.where(ki >= qi, 0.0, NEG_INF)
        bias_mid = jnp.concatenate([bias_before, bias_own], axis=1)
        bias_first = jnp.concatenate([jnp.where(n > 0, bias_before, NEG_INF), bias_own], axis=1)
        ones = jnp.ones((2 * KEY_BLOCK, HEAD_DIM), BF16)
        diag = qi == ki

        @pl.when(n == 0)
        def _():
            k16p[...] = jnp.zeros_like(k16p)
            v16p[...] = jnp.zeros_like(v16p)

        def unit(qf, kpb, kb, vpb, vb, bias, state, rows, first):
            acc_r, m_r, l_r = state
            kcat = jnp.concatenate([kpb, kb], axis=0)
            vaug = jnp.concatenate([jnp.concatenate([vpb, vb], axis=0), ones], axis=1)
            s = _dot_nt(qf.astype(BF16), kcat) + bias
            mx = jnp.max(s, axis=-1, keepdims=True)
            if first:
                m_new = jnp.broadcast_to(mx, (KEY_BLOCK, HEAD_DIM))
            else:
                m_old = m_r[rows, :]
                m_new = jnp.maximum(m_old, mx)
            pv = _dot(jnp.exp(s - jnp.concatenate([m_new, m_new], axis=1)).astype(BF16), vaug)
            if first:
                acc_r[rows, :] = pv[:, :HEAD_DIM]
                l_r[rows, :] = pv[:, HEAD_DIM:]
            else:
                alpha = jnp.exp(m_old - m_new)
                acc_r[rows, :] = alpha * acc_r[rows, :] + pv[:, :HEAD_DIM]
                l_r[rows, :] = alpha * l_r[rows, :] + pv[:, HEAD_DIM:]
            m_r[rows, :] = m_new

        for gi, dil in enumerate(DILATIONS[:-1]):
            nb = nblk // dil
            for r in range(dil):
                prow = _unit_rows(dil, r, nb - 1)
                kpb, vpb = kp_ref[prow, :].astype(BF16), vp_ref[prow, :].astype(BF16)
                for j in range(nb):
                    rows = _unit_rows(dil, r, j)
                    kb, vb = k_ref[rows, :].astype(BF16), v_ref[rows, :].astype(BF16)
                    unit(q_ref[rows, :], kpb, kb, vpb, vb, bias_first if j == 0 else bias_mid,
                         (acc, m_s, l_s), rows, gi == 0)
                    kpb, vpb = kb, vb

        for src, dst in ((q_ref, q16), (k_ref, k16), (v_ref, v16), (acc, acc16), (m_s, m16), (l_s, l16)):
            _to_residue_major(src, tmp, dst)
        for r in range(wide):
            rows = pl.ds(r * KEY_BLOCK, KEY_BLOCK)
            unit(q16[rows, :], k16p[rows, :].astype(BF16), k16[rows, :].astype(BF16), v16p[rows, :].astype(BF16),
                 v16[rows, :].astype(BF16), bias_first, (acc16, m16, l16), rows, False)
        k16p[...] = k16[...]
        v16p[...] = v16[...]

        den = l16[...]
        o16[...] = acc16[...] * (1.0 / den)
        m16[...] = m16[...] + jnp.log(den)
        _from_residue_major(o16, tmp, o_ref, False)
        _from_residue_major(m16, tmp, m_s, False)

        def lse_row(ref, rows):
            return jnp.sum(jnp.where(diag, ref[rows, :], 0.0), axis=0, keepdims=True)

        for dil, out in zip(DILATIONS[:-1], (l1_ref, l4_ref)):
            nb = nblk // dil
            for r in range(dil):
                for j in range(nb):
                    out[r * nb + j:r * nb + j + 1, :] = lse_row(m_s, _unit_rows(dil, r, j))
        for r in range(wide):
            l16_ref[r:r + 1, :] = lse_row(m16, pl.ds(r * KEY_BLOCK, KEY_BLOCK))

    cur = lambda piece: pl.BlockSpec((SPAN, HEAD_DIM), lambda h, n: (n, piece * nh + h))
    before = lambda piece: pl.BlockSpec((SPAN, HEAD_DIM), lambda h, n: (jnp.maximum(n - 1, 0), piece * nh + h))
    blk = pl.BlockSpec((None, SPAN, HEAD_DIM), lambda h, n: (h, n, 0))
    lblk = pl.BlockSpec((None, nblk, KEY_BLOCK), lambda h, n: (h, n, 0))
    lshape = jax.ShapeDtypeStruct((nh, seq // KEY_BLOCK, KEY_BLOCK), F32)
    full = jax.ShapeDtypeStruct((nh, seq, HEAD_DIM), F32)
    o, l1, l4, l16, *major = pl.pallas_call(
        body, name="attn_fwd",
        out_shape=[full, lshape, lshape, lshape] + [full] * 4,
        grid=(nh, nchunk), in_specs=[cur(2), cur(3), cur(4), before(3), before(4)],
        out_specs=[blk, lblk, lblk, lblk] + [blk] * 4,
        scratch_shapes=[pltpu.VMEM((SPAN, HEAD_DIM), F32)] * 9,
        compiler_params=_params(("arbitrary", "arbitrary"), VMEM_LIMIT),
    )(proj, proj, proj, proj, proj)
    return o, (l1, l4, l16), tuple(major)


def _to_residue_major(src, tmp, dst):
    quarter = SPAN // 4
    for r4 in range(4):
        tmp[r4 * quarter:(r4 + 1) * quarter, :] = src[pl.ds(r4, quarter, stride=4), :]
    for r4 in range(4):
        for rp in range(4):
            r = r4 + 4 * rp
            dst[r * KEY_BLOCK:(r + 1) * KEY_BLOCK, :] = tmp[pl.ds(r4 * quarter + rp, KEY_BLOCK, stride=4), :]


def _from_residue_major(src, tmp, dst, add):
    quarter = SPAN // 4
    for r4 in range(4):
        for rp in range(4):
            r = r4 + 4 * rp
            tmp[pl.ds(r4 * quarter + rp, KEY_BLOCK, stride=4), :] = src[r * KEY_BLOCK:(r + 1) * KEY_BLOCK, :]
    for r4 in range(4):
        rows = pl.ds(r4, quarter, stride=4)
        part = tmp[r4 * quarter:(r4 + 1) * quarter, :]
        dst[rows, :] = dst[rows, :] + part if add else part


def _attn_bwd(proj, do, o, lses, major, cosf, sinf):
    nh, seq = N_HEADS, proj.shape[0]
    nchunk = seq // SPAN
    nblk = SPAN // KEY_BLOCK
    wide = DILATIONS[-1]
    assert SPAN == wide * KEY_BLOCK

    def body(q_ref, k_ref, v_ref, do_ref, o_ref, kp_ref, vp_ref, q16, k16, v16, o16, l1_ref, l4_ref, l16_ref,
             cos_ref, sin_ref, cosp_ref, sinp_ref, dq_ref, dk_ref, dv_ref,
             dq_acc, dkc_acc, dvc_acc, dkp_acc, dvp_acc, do16, k16p, v16p,
             dq16, dkc16, dvc16, dkp16, dvp16, tmp, pt_s, ds_s, kcat_s, qb_s, dob_s):
        n = pl.program_id(1)
        ki = lax.broadcasted_iota(jnp.int32, (KEY_BLOCK, KEY_BLOCK), 0)
        qi = lax.broadcasted_iota(jnp.int32, (KEY_BLOCK, KEY_BLOCK), 1)
        bias_own = jnp.where(ki <= qi, 0.0, NEG_INF)
        bias_before = jnp.where(ki >= qi, 0.0, NEG_INF)
        bias_mid = jnp.concatenate([bias_before, bias_own], axis=0)
        bias_first = jnp.concatenate([jnp.where(n > 0, bias_before, NEG_INF), bias_own], axis=0)
        ones8 = jnp.ones((SUBLANES, HEAD_DIM), BF16)

        def row_dot(a, b):
            prod = a * b
            hi = prod.astype(BF16)
            lo = (prod - hi.astype(F32)).astype(BF16)
            return (_dot_nt(ones8, hi) + _dot_nt(ones8, lo))[0:1, :]

        def group(units, srcs, before, l_ref, accs):
            src_q, src_do, src_o, src_k, src_v = srcs
            before_k, before_v = before
            acc_q, acc_kc, acc_vc, acc_kp, acc_vp = accs
            kb = vb = None
            for u, (rows, prow, outside, lrow, _) in enumerate(units):
                dof = src_do[rows, :]
                qb, dob = src_q[rows, :].astype(BF16), dof.astype(BF16)
                kpb, vpb = (before_k[prow, :].astype(BF16), before_v[prow, :].astype(BF16)) if outside else (kb, vb)
                kb, vb = src_k[rows, :].astype(BF16), src_v[rows, :].astype(BF16)
                kcat = jnp.concatenate([kpb, kb], axis=0)
                vcat = jnp.concatenate([vpb, vb], axis=0)
                bias = bias_first if outside else bias_mid
                pt = jnp.exp(_dot_nt(kcat, qb) + bias - l_ref[lrow:lrow + 1, :])
                dst = pt * (_dot_nt(vcat, dob) - row_dot(dof, src_o[rows, :]))
                pt_s[u], ds_s[u], kcat_s[u], qb_s[u], dob_s[u] = pt.astype(BF16), dst.astype(BF16), kcat, qb, dob
            for u, (rows, _, _, _, _) in enumerate(units):
                acc_q[rows, :] += _dot_tn(ds_s[u], kcat_s[u])
            for u, (rows, prow, outside, _, nxt) in enumerate(units):
                dk = _dot(ds_s[u, KEY_BLOCK:, :], qb_s[u])
                dv = _dot(pt_s[u, KEY_BLOCK:, :], dob_s[u])
                if nxt is not None:
                    dk = dk + _dot(ds_s[nxt, :KEY_BLOCK, :], qb_s[nxt])
                    dv = dv + _dot(pt_s[nxt, :KEY_BLOCK, :], dob_s[nxt])
                acc_kc[rows, :] += dk
                acc_vc[rows, :] += dv
                if outside:
                    acc_kp[prow, :] += _dot(ds_s[u, :KEY_BLOCK, :], qb_s[u])
                    acc_vp[prow, :] += _dot(pt_s[u, :KEY_BLOCK, :], dob_s[u])

        @pl.when(n == 0)
        def _():
            for ref in (dkp_acc, dvp_acc, dkp16, dvp16, k16p, v16p):
                ref[...] = jnp.zeros_like(ref)

        @pl.when(n < nchunk)
        def _():
            for ref in (dq_acc, dkc_acc, dvc_acc, dq16, dkc16, dvc16):
                ref[...] = jnp.zeros_like(ref)
            _to_residue_major(do_ref, tmp, do16)
            natural = (q_ref, do_ref, o_ref, k_ref, v_ref)
            for dil, l_ref in zip(DILATIONS[:-1], (l1_ref, l4_ref)):
                nb = nblk // dil
                units = [(_unit_rows(dil, r, j), _unit_rows(dil, r, (j - 1) % nb), j == 0, r * nb + j,
                          r * nb + j + 1 if j + 1 < nb else None) for r in range(dil) for j in range(nb)]
                group(units, natural, (kp_ref, vp_ref), l_ref, (dq_acc, dkc_acc, dvc_acc, dkp_acc, dvp_acc))
            blocks = [pl.ds(r * KEY_BLOCK, KEY_BLOCK) for r in range(wide)]
            group([(rows, rows, True, r, None) for r, rows in enumerate(blocks)], (q16, do16, o16, k16, v16),
                  (k16p, v16p), l16_ref, (dq16, dkc16, dvc16, dkp16, dvp16))
            _from_residue_major(dq16, tmp, dq_acc, True)
            dq = dq_acc[...]
            dq_ref[...] = ((dq * cos_ref[...] - _rope_partner(dq) * sin_ref[...]) * ATTN_SCALE).astype(BF16)

        @pl.when(n > 0)
        def _():
            _from_residue_major(dkp16, tmp, dkp_acc, True)
            _from_residue_major(dvp16, tmp, dvp_acc, True)
            dk = dkp_acc[...]
            dk_ref[...] = (dk * cosp_ref[...] - _rope_partner(dk) * sinp_ref[...]).astype(BF16)
            dv_ref[...] = dvp_acc[...].astype(BF16)

        @pl.when(n < nchunk)
        def _():
            for src, dst in ((dkc_acc, dkp_acc), (dvc_acc, dvp_acc), (dkc16, dkp16), (dvc16, dvp16),
                             (k16, k16p), (v16, v16p)):
                dst[...] = src[...]

    last = nchunk - 1
    cur = lambda h, n: (h, jnp.minimum(n, last), 0)
    prev = lambda h, n: (h, jnp.clip(n - 1, 0, last), 0)
    blk = lambda idx: pl.BlockSpec((None, SPAN, HEAD_DIM), idx)
    lblk = pl.BlockSpec((None, nblk, KEY_BLOCK), cur)
    tab = pl.BlockSpec((SPAN, HEAD_DIM), lambda h, n: (jnp.minimum(n, last), 0))
    tabp = pl.BlockSpec((SPAN, HEAD_DIM), lambda h, n: (jnp.clip(n - 1, 0, last), 0))
    out_q = pl.BlockSpec((SPAN, HEAD_DIM), lambda h, n: (jnp.minimum(n, last), h))
    out_kv = pl.BlockSpec((SPAN, HEAD_DIM), lambda h, n: (jnp.clip(n - 1, 0, last), h))
    shape = jax.ShapeDtypeStruct((seq, nh * HEAD_DIM), BF16)
    tok = lambda piece, row: pl.BlockSpec((SPAN, HEAD_DIM), lambda h, n: (row(n), piece * nh + h))
    row_cur, row_prev = (lambda n: jnp.minimum(n, last)), (lambda n: jnp.clip(n - 1, 0, last))
    return pl.pallas_call(
        body, name="attn_bwd", out_shape=[shape, shape, shape], grid=(nh, nchunk + 1),
        in_specs=[tok(2, row_cur), tok(3, row_cur), tok(4, row_cur), blk(cur), blk(cur),
                  tok(3, row_prev), tok(4, row_prev)] + [blk(cur)] * 4 + [lblk] * 3 + [tab, tab, tabp, tabp],
        out_specs=[out_q, out_kv, out_kv],
        scratch_shapes=[pltpu.VMEM((SPAN, HEAD_DIM), F32)] * 14
                       + [pltpu.VMEM((nblk, 2 * KEY_BLOCK, HEAD_DIM), BF16)] * 3
                       + [pltpu.VMEM((nblk, KEY_BLOCK, HEAD_DIM), BF16)] * 2,
        compiler_params=_params(("arbitrary", "arbitrary"), VMEM_LIMIT),
    )(proj, proj, proj, do, o, proj, proj, *major, *lses, cosf, sinf, cosf, sinf)


def _hub(x, tgt, hr, pf, o_hm, mod, b_mod, b_gate, g_final, w_out_rnn, w_out_attn, w_o):
    seq = x.shape[0]
    tm = HUB_ROWS
    nsteps = seq // tm

    def body(x_ref, t_ref, hr_ref, zr_ref, za_ref, gr_ref, ga_ref, o_ref, mod_ref, bmod_ref, bg_ref, gf_ref,
             wr_hbm, wa_hbm, wo_hbm,
             dx2_ref, dhr_ref, dzr_ref, do_ref, dza_ref, dgr_ref, dga_ref,
             ur_ref, dyr_ref, ua_ref, dya_ref, mg_ref, dmo_ref,
             ggf_ref, gbg_ref, dgate_ref, loss_ref,
             wr, wa, wo, sem):
        step = pl.program_id(0)

        @pl.when(step == 0)
        def _():
            for src, dst in ((wr_hbm, wr), (wa_hbm, wa), (wo_hbm, wo)):
                cp = pltpu.make_async_copy(src, dst, sem)
                cp.start()
                cp.wait()
            for ref in (ggf_ref, gbg_ref, dgate_ref, loss_ref):
                ref[...] = jnp.zeros_like(ref)

        gate = mod_ref[:, 2 * D_MODEL:] + bmod_ref[:, 2 * D_MODEL:]
        gfin = gf_ref[...]
        hr_t, zr, za = hr_ref[...], zr_ref[...], za_ref[...]
        o = jnp.concatenate([o_ref[hh] for hh in range(N_HEADS)], axis=1)
        sig_zr, sig_za = _sigmoid(zr), _sigmoid(za)
        silu_zr, silu_za = zr * sig_zr, za * sig_za
        u_rnn = (hr_t * silu_zr).astype(BF16)
        u_attn = (o * silu_za).astype(BF16)
        y_rnn = _dot(u_rnn, wr[...])
        y_attn = _dot(u_attn, wa[...])
        sr = _sigmoid(gr_ref[...] + bg_ref[:, :D_MODEL])
        sa = _sigmoid(ga_ref[...] + bg_ref[:, D_MODEL:])
        merged = (sr * y_rnn + sa * y_attn).astype(BF16)
        mo = _dot(merged, wo[...])
        x2 = x_ref[...] + gate * mo
        rstd = lax.rsqrt(jnp.mean(x2 * x2, axis=-1, keepdims=True) + NORM_EPS)
        xn = x2 * rstd
        err = xn * gfin - t_ref[...]
        loss_ref[...] += 0.5 * jnp.sum(jnp.sum(err * err, axis=-1, keepdims=True) * (1.0 / D_MODEL),
                                       axis=0, keepdims=True)

        dy = err * (1.0 / D_MODEL)
        ggf_ref[...] += jnp.sum(dy * xn, axis=0, keepdims=True)
        dxn = dy * gfin
        dx2 = rstd * (dxn - xn * jnp.mean(dxn * xn, axis=-1, keepdims=True))
        dx2_ref[...] = dx2
        dgate_ref[...] += jnp.sum(dx2 * mo, axis=0, keepdims=True)
        dmo = (dx2 * gate).astype(BF16)
        dmerged = _dot_nt(dmo, wo[...])
        mg_ref[...] = merged
        dmo_ref[...] = dmo
        dy_rnn = (dmerged * sr).astype(BF16)
        dy_attn = (dmerged * sa).astype(BF16)
        dg_r = dmerged * y_rnn * sr * (1.0 - sr)
        dg_a = dmerged * y_attn * sa * (1.0 - sa)
        dgr_ref[...] = dg_r.astype(BF16)
        dga_ref[...] = dg_a.astype(BF16)
        gbg_ref[:, :D_MODEL] += jnp.sum(dg_r, axis=0, keepdims=True)
        gbg_ref[:, D_MODEL:] += jnp.sum(dg_a, axis=0, keepdims=True)
        du_rnn = _dot_nt(dy_rnn, wr[...])
        du_attn = _dot_nt(dy_attn, wa[...])
        ur_ref[...] = u_rnn
        dyr_ref[...] = dy_rnn
        ua_ref[...] = u_attn
        dya_ref[...] = dy_attn
        dhr_ref[...] = du_rnn * silu_zr
        dzr_ref[...] = (du_rnn * hr_t * (sig_zr * (1.0 + zr * (1.0 - sig_zr)))).astype(BF16)
        dza_ref[...] = (du_attn * o * (sig_za * (1.0 + za * (1.0 - sig_za)))).astype(BF16)
        d_o = du_attn * silu_za
        for hh in range(N_HEADS):
            do_ref[hh] = d_o[:, hh * HEAD_DIM:(hh + 1) * HEAD_DIM]

    row = pl.BlockSpec((tm, D_MODEL), lambda i: (i, 0))
    piece = lambda slot: pl.BlockSpec((tm, D_MODEL), lambda i: (i, slot))
    hm = pl.BlockSpec((N_HEADS, tm, HEAD_DIM), lambda i: (0, i, 0))
    const = lambda cols: pl.BlockSpec((1, cols), lambda i: (0, 0))
    any_spec = pl.BlockSpec(memory_space=pl.ANY)
    act_f32 = jax.ShapeDtypeStruct((seq, D_MODEL), F32)
    act_bf16 = jax.ShapeDtypeStruct((seq, D_MODEL), BF16)
    return pl.pallas_call(
        body, name="hub",
        out_shape=[act_f32, act_f32, act_bf16, jax.ShapeDtypeStruct((N_HEADS, seq, HEAD_DIM), F32),
                   act_bf16, act_bf16, act_bf16] + [act_bf16] * 6 + [
                   jax.ShapeDtypeStruct((1, D_MODEL), F32), jax.ShapeDtypeStruct((1, 2 * D_MODEL), F32),
                   jax.ShapeDtypeStruct((1, D_MODEL), F32), jax.ShapeDtypeStruct((1, 1), F32)],
        grid=(nsteps,),
        in_specs=[row, row, row, piece(1), piece(5), piece(6), piece(7), hm,
                  const(3 * D_MODEL), const(3 * D_MODEL), const(2 * D_MODEL), const(D_MODEL),
                  any_spec, any_spec, any_spec],
        out_specs=[row, row, row, hm, row, row, row] + [row] * 6 + [
                   const(D_MODEL), const(2 * D_MODEL), const(D_MODEL), const(1)],
        scratch_shapes=[pltpu.VMEM((D_MODEL, D_MODEL), BF16)] * 3 + [pltpu.SemaphoreType.DMA],
        compiler_params=_params(("arbitrary",), VMEM_LIMIT),
    )(x, tgt, hr, pf, pf, pf, pf, o_hm, mod, b_mod, b_gate, g_final, w_out_rnn, w_out_attn, w_o)


def _pair_grads(name, lefts, rights):
    n = len(rights)
    shared = len(lefts) == 1
    seq = rights[0].shape[0]
    tk = WGRAD_ROWS
    nk = seq // tk

    def body(*refs):
        l_refs, r_refs = refs[:len(lefts)], refs[len(lefts):len(lefts) + n]
        out_ref, low_ref = refs[len(lefts) + n:]
        j, kk = pl.program_id(0), pl.program_id(1)

        @pl.when(kk == 0)
        def _():
            out_ref[...] = jnp.zeros_like(out_ref)

        for m in range(n):
            @pl.when(j == m)
            def _(m=m):
                out_ref[...] += _dot_tn(l_refs[0 if shared else m][...], r_refs[m][...])

        @pl.when(kk == nk - 1)
        def _():
            low_ref[...] = out_ref[...].astype(BF16)

    def spec(m):
        return pl.BlockSpec((tk, D_MODEL), lambda j, kk: (jnp.where(j == m, kk, jnp.where(j < m, 0, nk - 1)), 0))

    left_specs = [pl.BlockSpec((tk, D_MODEL), lambda j, kk: (kk, 0))] if shared else [spec(m) for m in range(n)]
    out_spec = pl.BlockSpec((None, D_MODEL, D_MODEL), lambda j, kk: (j, 0, 0))
    return pl.pallas_call(
        body, name=name,
        out_shape=[jax.ShapeDtypeStruct((n, D_MODEL, D_MODEL), F32), jax.ShapeDtypeStruct((n, D_MODEL, D_MODEL), BF16)],
        grid=(n, nk),
        in_specs=left_specs + [spec(m) for m in range(n)],
        out_specs=[out_spec, out_spec],
        compiler_params=_params(("arbitrary", "arbitrary"), VMEM_LIMIT),
    )(*lefts, *rights)


def _dh_dx(pieces, w_near, w_sib, w_far, x, dx2, mod, b_mod, g_norm):
    seq = x.shape[0]
    tm = DX_ROWS

    def body(*refs):
        p_refs = refs[:8]
        near_hbm, sib_hbm, far_hbm, x_ref, dx2_ref, mod_ref, bmod_ref, g_ref = refs[8:16]
        gx_ref, dshift_ref, dscale_ref, ggn_ref, w_scr, sem = refs[16:]
        step = pl.program_id(0)

        @pl.when(step == 0)
        def _():
            me = _my_pos()
            sib = _flip(me, 1)
            moves = [(near_hbm, _index(_flip(me, 2 * m))) for m in range(4)] + [(sib_hbm, _index(sib))]
            moves += [(far_hbm, _index(_flip(sib, 2 * m))) for m in range(1, 4)]
            loads = [pltpu.make_async_copy(src.at[t], w_scr.at[t], sem.at[i]) for i, (src, t) in enumerate(moves)]
            for cp in loads:
                cp.start()
            for cp in loads:
                cp.wait()
            for ref in (dshift_ref, dscale_ref, ggn_ref):
                ref[...] = jnp.zeros_like(ref)

        dh = _dot_nt(p_refs[0][...], w_scr[0])
        for j in range(1, 8):
            dh = dh + _dot_nt(p_refs[j][...], w_scr[j])
        scale1 = 1.0 + mod_ref[:, D_MODEL:2 * D_MODEL] + bmod_ref[:, D_MODEL:2 * D_MODEL]
        g = g_ref[...]
        xf = x_ref[...]
        rstd_t = lax.rsqrt(jnp.mean(xf * xf, axis=-1, keepdims=True) + NORM_EPS)
        xn = xf * rstd_t
        dshift_ref[...] += jnp.sum(dh, axis=0, keepdims=True)
        dscale_ref[...] += jnp.sum(dh * (xn * g), axis=0, keepdims=True)
        ggn_ref[...] += jnp.sum(dh * scale1 * xn, axis=0, keepdims=True)
        dxn = dh * (g * scale1)
        gx_ref[...] = rstd_t * (dxn - xn * jnp.mean(dxn * xn, axis=-1, keepdims=True)) + dx2_ref[...]

    row = pl.BlockSpec((tm, D_MODEL), lambda i: (i, 0))
    const = lambda cols: pl.BlockSpec((1, cols), lambda i: (0, 0))
    vec = jax.ShapeDtypeStruct((1, D_MODEL), F32)
    return pl.pallas_call(
        body, name="dh_dx",
        out_shape=[jax.ShapeDtypeStruct((seq, D_MODEL), F32), vec, vec, vec],
        grid=(seq // tm,),
        in_specs=[row] * 8 + [pl.BlockSpec(memory_space=pl.ANY)] * 3 + [row, row,
                              const(3 * D_MODEL), const(3 * D_MODEL), const(D_MODEL)],
        out_specs=[row, const(D_MODEL), const(D_MODEL), const(D_MODEL)],
        scratch_shapes=[pltpu.VMEM((8, D_MODEL, D_MODEL), BF16), pltpu.SemaphoreType.DMA((8,))],
        compiler_params=_params(("arbitrary",), VMEM_LIMIT),
    )(*pieces, w_near, w_sib, w_far, x, dx2, mod, b_mod, g_norm)


def _adamw(name, w, g, m, v, recv=None):
    rows, cols = w.shape
    tr = rows if rows <= 256 else 256

    def body(*refs):
        w_ref, g_ref, m_ref, v_ref = refs[:4]
        d_ref, nm_ref, nv_ref = refs[-3:] if recv is None else refs[5:8]
        gv = g_ref[...]
        if recv is not None:
            r_ref, g_out = refs[4], refs[8]
            gv = ((gv + r_ref[0].astype(F32)) + r_ref[1].astype(F32)) + r_ref[2].astype(F32)
            g_out[...] = gv
        nm = ADAM_B1 * m_ref[...] + (1.0 - ADAM_B1) * gv
        nv = ADAM_B2 * v_ref[...] + (1.0 - ADAM_B2) * (gv * gv)
        m_hat = nm / (1.0 - ADAM_B1 ** ADAM_STEP)
        v_hat = nv / (1.0 - ADAM_B2 ** ADAM_STEP)
        d_ref[...] = -ADAM_LR * (m_hat / (jnp.sqrt(v_hat) + ADAM_EPS) + ADAM_WD * w_ref[...])
        nm_ref[...] = nm
        nv_ref[...] = nv

    spec = pl.BlockSpec((tr, cols), lambda i: (i, 0))
    shape = jax.ShapeDtypeStruct((rows, cols), F32)
    if recv is None:
        return pl.pallas_call(
            body, name=name, out_shape=[shape, shape, shape], grid=(rows // tr,),
            in_specs=[spec] * 4, out_specs=[spec] * 3,
            compiler_params=_params(("arbitrary",)),
        )(w, g, m, v)
    return pl.pallas_call(
        body, name=name, out_shape=[shape] * 4, grid=(rows // tr,),
        in_specs=[spec] * 4 + [pl.BlockSpec((3, tr, cols), lambda i: (0, i, 0))], out_specs=[spec] * 4,
        compiler_params=_params(("arbitrary",)),
    )(w, g, m, v, recv)


def kernel(x, c, positions, g_norm, w_mod, b_mod, w_in, b_gate, conv_w, conv_b, w_a, b_a, w_x, b_x, lam, w_out_rnn, w_out_attn, w_o, g_final, loss_target, m_g_norm, m_w_mod, m_b_mod, m_w_in, m_b_gate, m_conv_w, m_conv_b, m_w_a, m_b_a, m_w_x, m_b_x, m_lam, m_w_out_rnn, m_w_out_attn, m_w_o, m_g_final, v_g_norm, v_w_mod, v_b_mod, v_w_in, v_b_gate, v_conv_w, v_conv_b, v_w_a, v_b_a, v_w_x, v_b_x, v_lam, v_w_out_rnn, v_w_out_attn, v_w_o, v_g_final):
    seq = x.shape[1]
    me = _index(_my_pos())
    xs, tgt = x[0], loss_target[0]

    inv_freq = ROPE_THETA ** (-jnp.arange(0, 2 * ROT_HALF, 2, dtype=F32) / (2 * ROT_HALF))
    ang = (positions[0].astype(F32).reshape(seq // SUBLANES, SUBLANES, 1) * inv_freq).reshape(seq // SUBLANES, 128)
    cos, sin = lax.optimization_barrier((jnp.cos(ang), jnp.sin(ang)))
    cos, sin = cos.reshape(seq, ROT_HALF), sin.reshape(seq, ROT_HALF)
    rest = HEAD_DIM - 2 * ROT_HALF
    cosf = jnp.concatenate([cos, cos, jnp.ones((seq, rest), F32)], axis=1)
    sinf = jnp.concatenate([-sin, sin, jnp.zeros((seq, rest), F32)], axis=1)
    keep = (positions[0] != 0).astype(F32)[:, None]

    both = _ag_small("gather_c_conv_w", jnp.concatenate(
        [jnp.broadcast_to(c, (SUBLANES, D_MODEL)), jnp.pad(conv_w[0], ((0, SUBLANES - 4), (0, 0)))], axis=1))
    c_all, conv_w8 = both[:, 0, :D_MODEL], both[:, :, D_MODEL:]
    mod_cols = w_mod.shape[2]
    mod_part = _ag_small("gather_mod", _mod_fwd(c_all, w_mod[0]))
    mod = lax.dynamic_index_in_dim(mod_part, me, axis=1, keepdims=False).reshape(1, N_DEV * mod_cols)

    slot = lambda t: lax.dynamic_update_slice(lax.empty((N_DEV,) + t.shape, t.dtype), t[None], (me, 0, 0))
    w_in_own = w_in[0].astype(BF16)
    mod, w_in_own = lax.optimization_barrier((mod, w_in_own))
    first = _split_start("gather_w_in_start", _own_block_copies, 3, [w_in_own], [slot(w_in_own)])
    swap = _split_start("swap_w_in_start", _sibling_copy, 1, first[2], [lax.empty((N_DEV,) + w_in_own.shape, BF16)])
    mod = mod + swap[4][0:1, 0:1]

    blocks = lambda t: t.reshape(RNN_BLOCKS, 1, 128)
    rnn_params = (conv_w8, blocks(conv_b), w_a[0], blocks(b_a), w_x[0], blocks(b_x), blocks(lam))

    h = _norm(xs, mod, b_mod, g_norm)
    ids = lambda ks: jnp.bitwise_xor(me, jnp.array(ks, jnp.int32)).astype(jnp.int32)
    pf = _proj("proj_own", h, swap[2][0][None], jnp.zeros((1,), jnp.int32), ids([0]), cosf, sinf, None)
    own_thru, (w_in_sib,) = _split_wait("swap_w_in_wait", _sibling_copy, swap, pf)
    pf = _proj("proj_sibling", h, w_in_sib, ids([1]), ids([1]), cosf, sinf, pf)
    _, (w_in_near,) = _split_wait("gather_w_in_wait", _own_block_copies, (first[0], first[1], own_thru, first[3], None), pf)
    second = _split_start("forward_w_in_start", _forward_copies, 3, [w_in_near],
                          [lax.empty(w_in_near.shape, w_in_near.dtype)])
    near = ids([2, 4, 6])
    pf = _proj("proj_near", h, second[2][0], near, near, cosf, sinf, pf)
    (w_in_near,), (w_in_far,) = _split_wait("forward_w_in_wait", _forward_copies, second, pf)
    far = ids([3, 5, 7])
    pf = _proj("proj_far", h, w_in_far, far, far, cosf, sinf, pf)
    late = [w_out_rnn[0].astype(BF16), w_out_attn[0].astype(BF16), w_o[0].astype(BF16)]
    pf, late = lax.optimization_barrier((pf, late))
    flight = _split_start("gather_out_weights_start", _peer_copies, 7 * len(late), late, [slot(t) for t in late])
    rnn_params = (rnn_params[0], rnn_params[1] + flight[4][0:1, 0:1]) + rnn_params[2:]
    hr, rnn_saved = _rnn_fwd(pf, keep, *rnn_params)
    o, lses, major = _attn_fwd(pf)

    w_or_all, w_oa_all, w_o_all = (t.reshape(D_MODEL, D_MODEL) for t in _split_wait(
        "gather_out_weights_wait", _peer_copies, flight, o)[1])
    (dx2, dhr, dz_rnn, d_o, dz_attn, dg_r, dg_a, u_rnn, dy_rnn, u_attn, dy_attn, merged, dmo,
     gp_g_final, gp_b_gate, dgate, loss_part) = _hub(
        xs, tgt, hr, pf, o, mod, b_mod, b_gate, g_final.reshape(1, D_MODEL), w_or_all, w_oa_all, w_o_all)
    gp_out, gp_out_low = _pair_grads("out_grads", [u_rnn, u_attn, merged], [dy_rnn, dy_attn, dmo])
    dq, dk, dv = _attn_bwd(pf, d_o, o, lses, major, cosf, sinf)
    dx_rnn, gp_conv_w, gp_conv_b, gp_w_a, gp_b_a, gp_w_x, gp_b_x, gp_lam = _rnn_bwd(
        pf, hr, dhr, rnn_saved, keep, rnn_params[0], rnn_params[2], rnn_params[4], rnn_params[6])
    pieces = [dx_rnn, dz_rnn, dq, dk, dv, dz_attn, dg_r, dg_a]
    gp_w_in, gp_w_in_low = _pair_grads("w_in_grad", [h], pieces)

    by_target = lambda t: [(t.reshape(3, N_DEV, 128, D_MODEL), i) for i in range(3)]
    stacks = [(gp_w_in, None)] + by_target(gp_out)
    from_sib = _rs_to_sibling("rs_sibling", [(gp_w_in_low, None)] + by_target(gp_out_low))
    targets = jnp.bitwise_xor(me, 2 * jnp.arange(4, dtype=jnp.int32)).astype(jnp.int32)
    sums = [_add_sibling("rs_add_sibling_%d" % a, s_, r_, targets) for a, (s_, r_) in enumerate(zip(stacks, from_sib))]
    sends = [send for _, send in sums]
    reduce_flight = _split_start("rs_chips_start", _chip_copies, 3 * len(sends), sends,
                                 [lax.empty(t.shape, t.dtype) for t in sends])

    mod_after = mod + reduce_flight[4][0:1, 0:1]
    grad_x, dshift, dscale, gp_g_norm = _dh_dx(pieces, w_in_near, w_in_sib, w_in_far, xs, dx2, mod_after, b_mod,
                                               g_norm)

    flat = lambda t: t.reshape(-1, 128)
    dmod = flat(jnp.concatenate([dshift, dscale, dgate], axis=1))
    dmod_placed = lax.dynamic_update_slice(jnp.zeros((N_DEV * dmod.shape[0], 128), F32), dmod, (me * dmod.shape[0], 0))
    small = [flat(gp_g_norm), flat(gp_b_gate), flat(gp_conv_b), flat(gp_b_a), flat(gp_b_x), flat(gp_lam),
             flat(gp_g_final), flat(gp_conv_w), jnp.broadcast_to(loss_part, (SUBLANES, 128)),
             flat(gp_w_a), flat(gp_w_x), dmod_placed]
    sizes = [t.shape[0] for t in small]
    small.append(jnp.zeros((-sum(sizes) % (2 * SUBLANES), 128), F32))
    total = _allreduce_small("allreduce_small_grads", jnp.concatenate(small, axis=0))
    offs = [sum(sizes[:i]) for i in range(len(sizes))]
    (g_g_norm, g_b_gate, g_conv_b, g_b_a, g_b_x, g_lam, g_g_final, g_conv_w_all, loss_rows, g_w_a, g_w_x,
     dmod_rows) = (total[o_:o_ + s_] for o_, s_ in zip(offs, sizes))
    loss = loss_rows[0, 0]
    g_conv_w = lax.dynamic_index_in_dim(g_conv_w_all.reshape(RNN_BLOCKS, SUBLANES, 128), me, axis=0,
                                        keepdims=False)[:4]

    dmod_all = dmod_rows.reshape(N_DEV, 3 * D_MODEL)
    dmod_cols = lax.dynamic_slice_in_dim(dmod_all, me * mod_cols, mod_cols, axis=1)
    g_b_mod, g_w_mod = _mod_bwd(c_all, dmod_all, dmod_cols)

    _, from_chips = _split_wait("rs_chips_wait", _chip_copies, reduce_flight, total)

    results = {}
    sharded = (("w_in", w_in, m_w_in, v_w_in, (D_MODEL, D_MODEL)),
               ("w_out_rnn", w_out_rnn, m_w_out_rnn, v_w_out_rnn, (128, D_MODEL)),
               ("w_out_attn", w_out_attn, m_w_out_attn, v_w_out_attn, (128, D_MODEL)),
               ("w_o", w_o, m_w_o, v_w_o, (128, D_MODEL)))
    for (name, w_, m_, v_, shape2), (own, _), arrived in zip(sharded, sums, from_chips):
        d_, nm_, nv_, g_ = _adamw("adamw_" + name, w_.reshape(shape2), own, m_.reshape(shape2), v_.reshape(shape2),
                                  arrived)
        results[name] = (g_, d_, nm_, nv_)
    shape2 = (D_MODEL, mod_cols)
    results["w_mod"] = (g_w_mod,) + tuple(_adamw("adamw_w_mod", w_mod.reshape(shape2), g_w_mod,
                                                 m_w_mod.reshape(shape2), v_w_mod.reshape(shape2)))
    lanes = (("g_norm", g_norm, g_g_norm, m_g_norm, v_g_norm), ("b_mod", b_mod, g_b_mod, m_b_mod, v_b_mod),
             ("b_gate", b_gate, g_b_gate, m_b_gate, v_b_gate), ("conv_w", conv_w, g_conv_w, m_conv_w, v_conv_w),
             ("conv_b", conv_b, g_conv_b, m_conv_b, v_conv_b), ("w_a", w_a, g_w_a, m_w_a, v_w_a),
             ("b_a", b_a, g_b_a, m_b_a, v_b_a), ("w_x", w_x, g_w_x, m_w_x, v_w_x), ("b_x", b_x, g_b_x, m_b_x, v_b_x),
             ("lam", lam, g_lam, m_lam, v_lam), ("g_final", g_final, g_g_final, m_g_final, v_g_final))
    for name, w_, g_, m_, v_ in lanes:
        rows128 = lambda t: t.reshape(-1, 128)
        results[name] = (g_,) + tuple(_adamw("adamw_" + name, rows128(w_), rows128(g_), rows128(m_), rows128(v_)))
    order = ("g_norm", "w_mod", "b_mod", "w_in", "b_gate", "conv_w", "conv_b", "w_a", "b_a", "w_x", "b_x", "lam",
             "w_out_rnn", "w_out_attn", "w_o", "g_final")
    given = dict(g_norm=g_norm, w_mod=w_mod, b_mod=b_mod, w_in=w_in, b_gate=b_gate, conv_w=conv_w, conv_b=conv_b,
                 w_a=w_a, b_a=b_a, w_x=w_x, b_x=b_x, lam=lam, w_out_rnn=w_out_rnn, w_out_attn=w_out_attn, w_o=w_o,
                 g_final=g_final)
    outs = [[results[name][k].reshape(given[name].shape) for name in order] for k in range(4)]
    return (loss, grad_x[None], *outs[0], *outs[1], *outs[2], *outs[3])
```

```python
import jax
import jax.numpy as jnp
from jax import lax
from jax.experimental import pallas as pl
from jax.experimental.pallas import tpu as pltpu

F32 = jnp.float32
BF16 = jnp.bfloat16
MESH = pl.DeviceIdType.MESH

D_MODEL = 1024
N_HEADS = 8
HEAD_DIM = 128
RNN_BLOCKS = 8
N_DEV = 8
ROT_HALF = 16
ROPE_THETA = 500000.0
DILATIONS = (1, 4, 16)
KEY_BLOCK = 128
SPAN = KEY_BLOCK * DILATIONS[-1]
ATTN_SCALE = HEAD_DIM ** -0.5
NORM_EPS = 1e-6
LRU_C = 8.0
NEG_INF = -1e30
ADAM_LR, ADAM_B1, ADAM_B2, ADAM_EPS, ADAM_WD, ADAM_STEP = 0.001, 0.9, 0.999, 1e-08, 0.01, 10

SUBLANES = 8
VMEM_LIMIT = 56 * 1024 * 1024
PROJ_ROWS = 1024
PROJ_RING = 3
RNN_ROWS = 2048
HUB_ROWS = 256
DX_ROWS = 512
WGRAD_ROWS = 1024
ADD_ROWS = 256


def _params(sem=None, vmem=None):
    return pltpu.CompilerParams(dimension_semantics=sem, vmem_limit_bytes=vmem)


def _dot(a, b):
    return jnp.dot(a, b, preferred_element_type=F32)


def _dot_nt(a, b):
    return lax.dot_general(a, b, (((1,), (1,)), ((), ())), preferred_element_type=F32)


def _dot_tn(a, b):
    return lax.dot_general(a, b, (((0,), (0,)), ((), ())), preferred_element_type=F32)


def _sigmoid(z):
    return 1.0 / (1.0 + jnp.exp(-z))


def _expm1_nonpos(z, exp_z):
    return jnp.where(z > -0.01, z * (1.0 + 0.5 * z), exp_z - 1.0)


def _my_pos():
    return lax.axis_index("x"), lax.axis_index("y"), lax.axis_index("c")


def _flip(pos, k):
    x, y, c = pos
    return ((1 - x) if k & 4 else x, (1 - y) if k & 2 else y, (1 - c) if k & 1 else c)


def _index(pos):
    return 4 * pos[0] + 2 * pos[1] + pos[2]


def _ag_small(name, v):
    rows, cols = v.shape

    def body(v_ref, out_ref, send_sems, recv_sems):
        me = _my_pos()
        out_ref[_index(me)] = v_ref[...]
        sends = []
        for k in range(1, N_DEV):
            cp = pltpu.make_async_remote_copy(
                src_ref=v_ref, dst_ref=out_ref.at[_index(me)], send_sem=send_sems.at[k - 1],
                recv_sem=recv_sems.at[k - 1], device_id=_flip(me, k), device_id_type=MESH)
            cp.start()
            sends.append(cp)
        for k in range(1, N_DEV):
            peer = _flip(me, k)
            pltpu.make_async_remote_copy(
                src_ref=v_ref, dst_ref=out_ref.at[_index(peer)], send_sem=send_sems.at[k - 1],
                recv_sem=recv_sems.at[k - 1], device_id=peer, device_id_type=MESH).wait_recv()
        for cp in sends:
            cp.wait_send()

    return pl.pallas_call(
        body, name=name,
        out_shape=jax.ShapeDtypeStruct((N_DEV, rows, cols), v.dtype),
        in_specs=[pl.BlockSpec(memory_space=pltpu.VMEM)],
        out_specs=pl.BlockSpec(memory_space=pltpu.VMEM),
        scratch_shapes=[pltpu.SemaphoreType.DMA((N_DEV - 1,)), pltpu.SemaphoreType.DMA((N_DEV - 1,))],
        compiler_params=_params(None, VMEM_LIMIT),
    )(v)


def _split_start(name, make_copies, nsem, srcs, lands):
    n, k = len(srcs), len(lands)

    def body(*refs):
        for cp in make_copies(refs[:n], refs[n:n + k], refs[n + k], refs[n + k + 1]):
            cp.start()
        refs[-1][...] = jnp.zeros_like(refs[-1])

    hbm = pl.BlockSpec(memory_space=pltpu.HBM)
    sem = pl.BlockSpec(memory_space=pltpu.SEMAPHORE)
    arrays = [*srcs, *lands]
    outs = pl.pallas_call(
        body, name=name,
        out_shape=(pltpu.SemaphoreType.DMA((nsem,)), pltpu.SemaphoreType.DMA((nsem,)),
                   *[pltpu.HBM(t.shape, t.dtype) for t in arrays], jax.ShapeDtypeStruct((SUBLANES, 128), F32)),
        in_specs=[hbm] * (n + k),
        out_specs=(sem, sem, *[hbm] * (n + k), pl.BlockSpec(memory_space=pltpu.VMEM)),
        input_output_aliases={i: 2 + i for i in range(n + k)},
        compiler_params=pltpu.CompilerParams(has_side_effects=pltpu.SideEffectType.DATAFLOW_SIDE_EFFECTING),
    )(*[pltpu.with_memory_space_constraint(t, pltpu.HBM) for t in arrays])
    return outs[0], outs[1], outs[2:2 + n], outs[2 + n:2 + n + k], outs[-1]


def _split_wait(name, make_copies, flight, after):
    send_sems, recv_sems, srcs, lands, _ = flight
    n, k = len(srcs), len(lands)

    def body(*refs):
        for cp in make_copies(refs[:n], refs[n:n + k], refs[n + k], refs[n + k + 1]):
            cp.wait_send()
            cp.wait_recv()

    hbm = pl.BlockSpec(memory_space=pltpu.HBM)
    sem = pl.BlockSpec(memory_space=pltpu.SEMAPHORE)
    arrays = [*srcs, *lands]
    outs = pl.pallas_call(
        body, name=name, out_shape=tuple(pltpu.HBM(t.shape, t.dtype) for t in arrays),
        in_specs=[hbm] * (n + k) + [sem, sem, pl.BlockSpec(memory_space=pl.ANY)],
        out_specs=[hbm] * (n + k),
        input_output_aliases={i: i for i in range(n + k)},
        compiler_params=pltpu.CompilerParams(has_side_effects=pltpu.SideEffectType.DATAFLOW_SIDE_EFFECTING),
    )(*arrays, send_sems, recv_sems, after)
    return outs[:n], outs[n:]


def _remote(src, dst, send_sems, recv_sems, k, to):
    return pltpu.make_async_remote_copy(src_ref=src, dst_ref=dst, send_sem=send_sems.at[k], recv_sem=recv_sems.at[k],
                                        device_id=to, device_id_type=MESH)


def _peer_copies(shards, lands, send_sems, recv_sems):
    me = _my_pos()
    return [_remote(shards[a], lands[a].at[_index(me)], send_sems, recv_sems, a * 7 + k - 1, _flip(me, k))
            for a in range(len(shards)) for k in range(1, N_DEV)]


def _own_block_copies(shards, lands, send_sems, recv_sems):
    me = _my_pos()
    return [_remote(shards[0], lands[0].at[_index(me)], send_sems, recv_sems, m - 1, _flip(me, 2 * m))
            for m in range(1, 4)]


def _sibling_copy(shards, lands, send_sems, recv_sems):
    me = _my_pos()
    return [_remote(shards[0], lands[0].at[_index(me)], send_sems, recv_sems, 0, _flip(me, 1))]


def _forward_copies(arrived, lands, send_sems, recv_sems):
    me = _my_pos()
    return [_remote(arrived[0].at[_index(_flip(me, 2 * m))], lands[0].at[_index(_flip(me, 2 * m))],
                    send_sems, recv_sems, m - 1, _flip(me, 1)) for m in range(1, 4)]


def _rs_to_sibling(name, stacks):
    n = len(stacks)

    def body(*refs):
        ins, outs = refs[:n], refs[n:2 * n]
        send_sems, recv_sems = refs[2 * n:]
        me = _my_pos()
        sib = _flip(me, 1)
        sends = []
        for a, (_, which) in enumerate(stacks):
            by_target = ins[a] if which is None else ins[a].at[which]
            for m in range(4):
                target = _flip(sib, 2 * m)
                cp = pltpu.make_async_remote_copy(
                    src_ref=by_target.at[_index(target)], dst_ref=outs[a].at[m],
                    send_sem=send_sems.at[a * 4 + m], recv_sem=recv_sems.at[a * 4 + m],
                    device_id=sib, device_id_type=MESH)
                cp.start()
                sends.append(cp)
        for cp in sends:
            cp.wait_recv()
        for cp in sends:
            cp.wait_send()

    any_spec = pl.BlockSpec(memory_space=pl.ANY)
    return pl.pallas_call(
        body, name=name,
        out_shape=[jax.ShapeDtypeStruct((4,) + s.shape[-2:], s.dtype) for s, _ in stacks],
        in_specs=[any_spec] * n, out_specs=[any_spec] * n,
        scratch_shapes=[pltpu.SemaphoreType.DMA((4 * n,)), pltpu.SemaphoreType.DMA((4 * n,))],
    )(*[s for s, _ in stacks])


def _chip_copies(srcs, lands, send_sems, recv_sems):
    me = _my_pos()
    return [_remote(srcs[a].at[m - 1], lands[a].at[m - 1], send_sems, recv_sems, a * 3 + m - 1, _flip(me, 2 * m))
            for a in range(len(srcs)) for m in range(1, 4)]


def _add_sibling(name, stack, recv, targets):
    stack, which = stack
    rows, cols = stack.shape[-2:]
    tr = min(rows, ADD_ROWS)

    def by_target(index):
        if which is None:
            return pl.BlockSpec((None, tr, cols), lambda *g: (index(*g), g[-2], 0))
        return pl.BlockSpec((None, None, tr, cols), lambda *g: (which, index(*g), g[-2], 0))

    def own_body(t_ref, a_ref, b_ref, o_ref):
        o_ref[...] = a_ref[...] + b_ref[...].astype(F32)

    own = pl.pallas_call(
        own_body, name=name + "_own",
        out_shape=jax.ShapeDtypeStruct((rows, cols), F32),
        grid_spec=pltpu.PrefetchScalarGridSpec(
            num_scalar_prefetch=1, grid=(rows // tr,),
            in_specs=[by_target(lambda i, t: t[0]),
                      pl.BlockSpec((None, tr, cols), lambda i, t: (0, i, 0))],
            out_specs=pl.BlockSpec((tr, cols), lambda i, t: (i, 0))),
        compiler_params=_params(("arbitrary",)),
    )(targets, stack, recv)

    def send_body(t_ref, a_ref, b_ref, o_ref):
        o_ref[...] = (a_ref[...] + b_ref[...].astype(F32)).astype(BF16)

    send = pl.pallas_call(
        send_body, name=name + "_send",
        out_shape=jax.ShapeDtypeStruct((3, rows, cols), BF16),
        grid_spec=pltpu.PrefetchScalarGridSpec(
            num_scalar_prefetch=1, grid=(3, rows // tr),
            in_specs=[by_target(lambda m, i, t: t[m + 1]),
                      pl.BlockSpec((None, tr, cols), lambda m, i, t: (m + 1, i, 0))],
            out_specs=pl.BlockSpec((None, tr, cols), lambda m, i, t: (m, i, 0))),
        compiler_params=_params(("arbitrary", "arbitrary")),
    )(targets, stack, recv)
    return own, send


def _allreduce_small(name, v):
    rows, cols = v.shape
    half = rows // 2
    assert rows % (2 * SUBLANES) == 0

    def body(v_ref, out_ref, from_sib, chip_half, from_chips, send_sems, recv_sems):
        me = _my_pos()
        sib = _flip(me, 1)
        mine = pl.ds(pl.multiple_of(me[2] * half, SUBLANES), half)
        theirs = pl.ds(pl.multiple_of((1 - me[2]) * half, SUBLANES), half)

        def copy(k, src, dst, to):
            return pltpu.make_async_remote_copy(src_ref=src, dst_ref=dst, send_sem=send_sems.at[k],
                                                recv_sem=recv_sems.at[k], device_id=to, device_id_type=MESH)

        to_sib = copy(0, v_ref.at[theirs], from_sib, sib)
        to_sib.start()
        to_sib.wait_recv()
        chip_half[...] = v_ref[mine, :] + from_sib[...]
        to_chips = [copy(m, chip_half, from_chips.at[m - 1], _flip(me, 2 * m)) for m in range(1, 4)]
        for cp in to_chips:
            cp.start()
        for cp in to_chips:
            cp.wait_recv()
        my_chip = 2 * me[0] + me[1]
        total = None
        for chip in range(4):
            slot = jnp.maximum(jnp.bitwise_xor(chip, my_chip) - 1, 0)
            part = jnp.where(chip == my_chip, chip_half[...], from_chips[slot])
            total = part if total is None else total + part
        out_ref[mine, :] = total
        swap = copy(4, out_ref.at[mine], out_ref.at[mine], sib)
        swap.start()
        copy(4, out_ref.at[theirs], out_ref.at[theirs], sib).wait_recv()
        for cp in [to_sib, swap] + to_chips:
            cp.wait_send()

    return pl.pallas_call(
        body, name=name, out_shape=jax.ShapeDtypeStruct((rows, cols), F32),
        in_specs=[pl.BlockSpec(memory_space=pltpu.VMEM)],
        out_specs=pl.BlockSpec(memory_space=pltpu.VMEM),
        scratch_shapes=[pltpu.VMEM((half, cols), F32), pltpu.VMEM((half, cols), F32),
                        pltpu.VMEM((3, half, cols), F32),
                        pltpu.SemaphoreType.DMA((5,)), pltpu.SemaphoreType.DMA((5,))],
        compiler_params=_params(None, VMEM_LIMIT),
    )(v)


def _mod_fwd(c_all, w_mod):
    def body(c_ref, w_ref, o_ref):
        c = c_ref[...]
        o_ref[...] = jnp.dot(c * _sigmoid(c), w_ref[...], preferred_element_type=F32,
                             precision=lax.Precision.HIGHEST)

    return pl.pallas_call(
        body, name="mod_fwd", out_shape=jax.ShapeDtypeStruct((N_DEV, w_mod.shape[1]), F32),
    )(c_all, w_mod)


def _mod_bwd(c_all, dmod_all, dmod_cols):
    def body(c_ref, da_ref, dc_ref, gb_ref, gw_ref):
        c = c_ref[...]
        acc = da_ref[0:1, :]
        for b in range(1, N_DEV):
            acc = acc + da_ref[b:b + 1, :]
        gb_ref[...] = acc
        gw_ref[...] = lax.dot_general(c * _sigmoid(c), dc_ref[...], (((0,), (0,)), ((), ())),
                                      preferred_element_type=F32, precision=lax.Precision.HIGHEST)

    return pl.pallas_call(
        body, name="mod_bwd",
        out_shape=[jax.ShapeDtypeStruct((1, dmod_all.shape[1]), F32),
                   jax.ShapeDtypeStruct((c_all.shape[1], dmod_cols.shape[1]), F32)],
    )(c_all, dmod_all, dmod_cols)


def _rope_partner(t):
    lane = lax.broadcasted_iota(jnp.int32, t.shape, 1)
    return jnp.where(lane < ROT_HALF, pltpu.roll(t, HEAD_DIM - ROT_HALF, 1), pltpu.roll(t, ROT_HALF, 1))


def _norm(x, mod, b_mod, g_norm):
    seq = x.shape[0]
    tm = PROJ_ROWS

    def body(x_ref, mod_ref, bmod_ref, g_ref, h_ref):
        xf = x_ref[...]
        rstd = lax.rsqrt(jnp.mean(xf * xf, axis=-1, keepdims=True) + NORM_EPS)
        shift = mod_ref[:, 0:D_MODEL] + bmod_ref[:, 0:D_MODEL]
        scale = mod_ref[:, D_MODEL:2 * D_MODEL] + bmod_ref[:, D_MODEL:2 * D_MODEL]
        h_ref[...] = (((xf * rstd) * g_ref[...]) * (1.0 + scale) + shift).astype(BF16)

    row = pl.BlockSpec((tm, D_MODEL), lambda i: (i, 0))
    const = lambda cols: pl.BlockSpec((1, cols), lambda i: (0, 0))
    return pl.pallas_call(
        body, name="norm", out_shape=jax.ShapeDtypeStruct((seq, D_MODEL), BF16), grid=(seq // tm,),
        in_specs=[row, const(3 * D_MODEL), const(3 * D_MODEL), const(D_MODEL)], out_specs=row,
        compiler_params=_params(("arbitrary",), VMEM_LIMIT),
    )(x, mod, b_mod, g_norm)


def _proj(name, h, w, slots, pieces, cosf, sinf, prior):
    seq = h.shape[0]
    tm = PROJ_ROWS
    count = pieces.shape[0]
    nrow = seq // tm
    total = count * nrow
    assert total >= PROJ_RING - 1

    def body(slots_ref, pieces_ref, h_hbm, w_ref, cos_ref, sin_ref, *rest):
        out_ref, h_ring, sems = rest[-3:]
        piece = pieces_ref[pl.program_id(0)]
        step = pl.program_id(0) * nrow + pl.program_id(1)

        def fetch(s):
            slot = s % PROJ_RING
            return pltpu.make_async_copy(h_hbm.at[pl.ds((s % nrow) * tm, tm)], h_ring.at[slot], sems.at[slot])

        @pl.when(step == 0)
        def _():
            for s in range(PROJ_RING - 1):
                fetch(s).start()

        @pl.when(step + PROJ_RING - 1 < total)
        def _():
            fetch(step + PROJ_RING - 1).start()

        fetch(step).wait()
        h_ref = h_ring.at[step % PROJ_RING]

        @pl.when((piece < 2) | (piece > 3))
        def _():
            out_ref[...] = _dot(h_ref[...], w_ref[...])

        def rotated(gain):
            for pair in range(N_HEADS // 2):
                both = _dot(h_ref[...], w_ref[:, 2 * pair * HEAD_DIM:2 * (pair + 1) * HEAD_DIM])
                for hh in (2 * pair, 2 * pair + 1):
                    t = both[:, (hh % 2) * HEAD_DIM:(hh % 2 + 1) * HEAD_DIM]
                    t = t * cos_ref[...] + _rope_partner(t) * sin_ref[...]
                    out_ref[:, hh * HEAD_DIM:(hh + 1) * HEAD_DIM] = t if gain is None else t * gain

        @pl.when(piece == 2)
        def _():
            rotated(ATTN_SCALE)

        @pl.when(piece == 3)
        def _():
            rotated(None)

    table = lambda j, i, sl, pc: (jnp.where((pc[j] == 2) | (pc[j] == 3), i, 0), 0)
    in_specs = [pl.BlockSpec(memory_space=pl.ANY),
                pl.BlockSpec((None, D_MODEL, D_MODEL), lambda j, i, sl, pc: (sl[j], 0, 0)),
                pl.BlockSpec((tm, HEAD_DIM), table), pl.BlockSpec((tm, HEAD_DIM), table)]
    args = [slots, pieces, h, w, cosf, sinf]
    aliases = {}
    if prior is not None:
        in_specs.append(pl.BlockSpec(memory_space=pl.ANY))
        args.append(prior)
        aliases = {6: 0}
    return pl.pallas_call(
        body, name=name,
        out_shape=jax.ShapeDtypeStruct((seq, 8 * D_MODEL), F32),
        grid_spec=pltpu.PrefetchScalarGridSpec(
            num_scalar_prefetch=2, grid=(count, nrow), in_specs=in_specs,
            out_specs=pl.BlockSpec((tm, D_MODEL), lambda j, i, sl, pc: (i, pc[j])),
            scratch_shapes=[pltpu.VMEM((PROJ_RING, tm, D_MODEL), BF16), pltpu.SemaphoreType.DMA((PROJ_RING,))]),
        input_output_aliases=aliases,
        compiler_params=_params(("arbitrary", "arbitrary"), VMEM_LIMIT),
    )(*args)


def _shift_down(v, s, head):
    rolled = pltpu.roll(v, s, 0)
    row = lax.broadcasted_iota(jnp.int32, head.shape, 0)
    first = jnp.where(row < s, pltpu.roll(head, s, 0), rolled[:SUBLANES, :])
    return jnp.concatenate([first, rolled[SUBLANES:, :]], axis=0)


def _shift_up(v, s, tail):
    rows = v.shape[0]
    rolled = pltpu.roll(v, rows - s, 0)
    row = lax.broadcasted_iota(jnp.int32, tail.shape, 0)
    last = jnp.where(row >= SUBLANES - s, pltpu.roll(tail, SUBLANES - s, 0), rolled[rows - SUBLANES:, :])
    return jnp.concatenate([rolled[:rows - SUBLANES, :], last], axis=0)


def _doubling(a, b, period, reverse):
    rows = a.shape[0]
    pos = lax.broadcasted_iota(jnp.int32, a.shape, 0) & (period - 1)
    k = 1
    while k < period:
        inside = (pos < period - k) if reverse else (pos >= k)
        shift = rows - k if reverse else k
        a_s = jnp.where(inside, pltpu.roll(a, shift, 0), 1.0)
        b_s = jnp.where(inside, pltpu.roll(b, shift, 0), 0.0)
        b = a * b_s + b
        a = a * a_s
        k *= 2
    return a, b


def _scan(a, b, boundary, reverse, a_scr, b_scr, spread):
    rows = a.shape[0]
    ntile = rows // SUBLANES
    a_scr[...], b_scr[...] = _doubling(a, b, SUBLANES, reverse)
    ends = pl.ds(0 if reverse else SUBLANES - 1, ntile, stride=SUBLANES)
    a_end, x_end = _doubling(a_scr[ends, :], b_scr[ends, :], ntile, reverse)
    x_end = x_end + a_end * boundary
    tile = lax.broadcasted_iota(jnp.int32, x_end.shape, 0)
    if reverse:
        incoming = jnp.where(tile == ntile - 1, boundary, pltpu.roll(x_end, ntile - 1, 0))
        last = x_end[0:1, :]
    else:
        incoming = jnp.where(tile == 0, boundary, pltpu.roll(x_end, 1, 0))
        last = x_end[ntile - 1:ntile, :]
    for s in range(SUBLANES):
        spread[pl.ds(s, ntile, stride=SUBLANES), :] = incoming
    return b_scr[...] + a_scr[...] * spread[...], last


def _conv_taps(xr, head):
    return [_shift_down(xr, 3, head), _shift_down(xr, 2, head), _shift_down(xr, 1, head), xr]


def _rnn_gates(xc, wa, ba, wx, bx, lam, keep):
    xcb = xc.astype(BF16)
    r = _sigmoid(_dot(xcb, wa.astype(BF16)) + ba)
    i = _sigmoid(_dot(xcb, wx.astype(BF16)) + bx)
    softplus = jnp.maximum(-lam, 0.0) + jnp.log(1.0 + jnp.exp(-jnp.abs(lam)))
    cl = -LRU_C * softplus
    log_a = cl * r
    a_raw = jnp.exp(log_a)
    mult_raw = jnp.sqrt(-_expm1_nonpos(2.0 * log_a, a_raw * a_raw))
    live = keep > 0.0
    return r, i, cl, a_raw, mult_raw, jnp.where(live, a_raw, 0.0), jnp.where(live, mult_raw, 1.0), live


def _rnn_specs(seq, rows, time_of):
    per = rows // SUBLANES
    vec = pl.BlockSpec((None, 1, 128), lambda hb, n: (hb, 0, 0))
    mat = pl.BlockSpec((None, 128, 128), lambda hb, n: (hb, 0, 0))
    return [pl.BlockSpec((rows, 128), lambda hb, n: (time_of(n), hb)),
            pl.BlockSpec((SUBLANES, 128), lambda hb, n: (jnp.maximum(time_of(n) * per - 1, 0), hb)),
            pl.BlockSpec((rows, 1), lambda hb, n: (time_of(n), 0)),
            pl.BlockSpec((None, SUBLANES, 128), lambda hb, n: (hb, 0, 0)),
            vec, mat, vec, mat, vec, vec]


def _rnn_fwd(pf, keep, conv_w8, conv_b, w_a, b_a, w_x, b_x, lam):
    seq = pf.shape[0]
    rows = RNN_ROWS

    def body(x_ref, xh_ref, keep_ref, cw_ref, cb_ref, wa_ref, ba_ref, wx_ref, bx_ref, lam_ref,
             hr_ref, xc_ref, r_ref, i_ref, araw_ref, mraw_ref, carry, a_scr, b_scr, spread):
        n = pl.program_id(1)

        @pl.when(n == 0)
        def _():
            carry[...] = jnp.zeros_like(carry)

        xr = x_ref[...]
        head = jnp.where(n > 0, xh_ref[...], 0.0)
        taps = _conv_taps(xr, head)
        xc = cb_ref[...] + sum(cw_ref[k:k + 1, :] * taps[k] for k in range(4))
        r, i, _, a_raw, mult_raw, a, mult, _ = _rnn_gates(xc, wa_ref[...], ba_ref[...], wx_ref[...], bx_ref[...],
                                                          lam_ref[...], keep_ref[...])
        xc_ref[...], r_ref[...], i_ref[...], araw_ref[...], mraw_ref[...] = xc, r, i, a_raw, mult_raw
        h, last = _scan(a, mult * i * xc, carry[0:1, :], False, a_scr, b_scr, spread)
        hr_ref[...] = h
        carry[...] = jnp.broadcast_to(last, carry.shape)

    chunk_f32 = pltpu.VMEM((rows, 128), F32)
    chunk = pl.BlockSpec((rows, 128), lambda hb, n: (n, hb))
    shape = jax.ShapeDtypeStruct((seq, D_MODEL), F32)
    outs = pl.pallas_call(
        body, name="rnn_fwd",
        out_shape=[shape] * 6,
        grid=(RNN_BLOCKS, seq // rows),
        in_specs=_rnn_specs(seq, rows, lambda n: n),
        out_specs=[chunk] * 6,
        scratch_shapes=[pltpu.VMEM((SUBLANES, 128), F32), chunk_f32, chunk_f32, chunk_f32],
        compiler_params=_params(("arbitrary", "arbitrary"), VMEM_LIMIT),
    )(pf, pf, keep, conv_w8, conv_b, w_a, b_a, w_x, b_x, lam)
    return outs[0], tuple(outs[1:])


def _rnn_bwd(pf, hr, dhr, saved, keep, conv_w8, w_a, w_x, lam):
    seq = pf.shape[0]
    rows = RNN_ROWS
    nchunk = seq // rows
    per = rows // SUBLANES
    time_of = lambda n: nchunk - 1 - n

    def body(x_ref, keep_ref, cw_ref, wa_ref, wx_ref, lam_ref, hr_ref, hrh_ref, dhr_ref,
             xc_ref, r_ref, i_ref, araw_ref, mraw_ref,
             dx_ref, gcw_ref, gcb_ref, gwa_ref, gba_ref, gwx_ref, gbx_ref, glam_ref,
             g_carry, dxc_tail, a_scr, b_scr, spread):
        n = pl.program_id(1)
        first_in_time = n == nchunk - 1

        @pl.when(n == 0)
        def _():
            g_carry[...] = jnp.zeros_like(g_carry)
            dxc_tail[...] = jnp.zeros_like(dxc_tail)
            for ref in (gcw_ref, gcb_ref, gwa_ref, gba_ref, gwx_ref, gbx_ref, glam_ref):
                ref[...] = jnp.zeros_like(ref)

        cw, wa, wx, lam = cw_ref[...], wa_ref[...], wx_ref[...], lam_ref[...]
        xc, r, i, a_raw, mult_raw = xc_ref[...], r_ref[...], i_ref[...], araw_ref[...], mraw_ref[...]
        cl = -LRU_C * (jnp.maximum(-lam, 0.0) + jnp.log(1.0 + jnp.exp(-jnp.abs(lam))))
        live = keep_ref[...] > 0.0
        a, mult = jnp.where(live, a_raw, 0.0), jnp.where(live, mult_raw, 1.0)
        h_prev = _shift_down(hr_ref[...], 1, jnp.where(first_in_time, 0.0, hrh_ref[...]))

        row = lax.broadcasted_iota(jnp.int32, xc.shape, 0)
        last = row == rows - 1
        a_next = jnp.where(last, 0.0, pltpu.roll(a, rows - 1, 0))
        g, g_first = _scan(a_next, dhr_ref[...] + jnp.where(last, g_carry[0:1, :], 0.0),
                           jnp.zeros((1, 128), F32), True, a_scr, b_scr, spread)
        g_carry[...] = jnp.broadcast_to(a[0:1, :] * g_first, g_carry.shape)

        da = g * h_prev
        dmult = g * i * xc
        di = g * mult * xc
        dxc = g * mult * i
        dlog_a = jnp.where(live, da * a_raw - dmult * a_raw * a_raw / mult_raw, 0.0)
        dpa = (dlog_a * cl) * r * (1.0 - r)
        dpx = di * i * (1.0 - i)
        glam_ref[...] += jnp.sum(dlog_a * r, axis=0, keepdims=True) * (LRU_C * _sigmoid(-lam))
        xcb, dpab, dpxb = xc.astype(BF16), dpa.astype(BF16), dpx.astype(BF16)
        gwa_ref[...] += _dot_tn(xcb, dpab)
        gwx_ref[...] += _dot_tn(xcb, dpxb)
        gba_ref[...] += jnp.sum(dpa, axis=0, keepdims=True)
        gbx_ref[...] += jnp.sum(dpx, axis=0, keepdims=True)
        dxc = dxc + _dot_nt(dpab, wa.astype(BF16)) + _dot_nt(dpxb, wx.astype(BF16))

        gcb_ref[...] += jnp.sum(dxc, axis=0, keepdims=True)
        xr = x_ref[...]
        tail = dxc_tail[...]
        later = [_shift_up(dxc, 3 - k, tail) for k in range(3)] + [dxc]
        dx = cw[3:4, :] * dxc
        for k in range(3):
            dx = dx + cw[k:k + 1, :] * later[k]
        for k in range(4):
            gcw_ref[k:k + 1, :] += jnp.sum(xr * later[k], axis=0, keepdims=True)
        dx_ref[...] = dx.astype(BF16)
        dxc_tail[...] = dxc[0:SUBLANES, :]

    blk = lambda hb, n: (hb, 0, 0)
    chunk = pl.BlockSpec((rows, 128), lambda hb, n: (time_of(n), hb))
    vec = pl.BlockSpec((None, 1, 128), blk)
    mat = pl.BlockSpec((None, 128, 128), blk)
    vec_shape = jax.ShapeDtypeStruct((RNN_BLOCKS, 1, 128), F32)
    mat_shape = jax.ShapeDtypeStruct((RNN_BLOCKS, 128, 128), F32)
    return pl.pallas_call(
        body, name="rnn_bwd",
        out_shape=[jax.ShapeDtypeStruct((seq, D_MODEL), BF16),
                   jax.ShapeDtypeStruct((RNN_BLOCKS, SUBLANES, 128), F32), vec_shape,
                   mat_shape, vec_shape, mat_shape, vec_shape, vec_shape],
        grid=(RNN_BLOCKS, nchunk),
        in_specs=[chunk, pl.BlockSpec((rows, 1), lambda hb, n: (time_of(n), 0)),
                  pl.BlockSpec((None, SUBLANES, 128), blk), mat, mat, vec, chunk,
                  pl.BlockSpec((SUBLANES, 128), lambda hb, n: (jnp.maximum(time_of(n) * per - 1, 0), hb)), chunk]
                 + [chunk] * 5,
        out_specs=[chunk, pl.BlockSpec((None, SUBLANES, 128), blk), vec, mat, vec, mat, vec, vec],
        scratch_shapes=[pltpu.VMEM((SUBLANES, 128), F32), pltpu.VMEM((SUBLANES, 128), F32)]
                       + [pltpu.VMEM((rows, 128), F32)] * 3,
        compiler_params=_params(("arbitrary", "arbitrary"), VMEM_LIMIT),
    )(pf, keep, conv_w8, w_a, w_x, lam, hr, hr, dhr, *saved)


def _unit_rows(dil, r, j):
    start = j * KEY_BLOCK * dil + r
    return pl.ds(start, KEY_BLOCK) if dil == 1 else pl.ds(start, KEY_BLOCK, stride=dil)


def _attn_fwd(proj):
    nh, seq = N_HEADS, proj.shape[0]
    nchunk = seq // SPAN
    nblk = SPAN // KEY_BLOCK
    wide = DILATIONS[-1]

    def body(q_ref, k_ref, v_ref, kp_ref, vp_ref, o_ref, l1_ref, l4_ref, l16_ref, q16, k16, v16, o16,
             acc, m_s, l_s, k16p, v16p, acc16, m16, l16, tmp):
        n = pl.program_id(1)
        qi = lax.broadcasted_iota(jnp.int32, (KEY_BLOCK, KEY_BLOCK), 0)
        ki = lax.broadcasted_iota(jnp.int32, (KEY_BLOCK, KEY_BLOCK), 1)
        bias_own = jnp.where(ki <= qi, 0.0, NEG_INF)
        bias_before = jnp.where(ki >= qi, 0.0, NEG_INF)
        bias_mid = jnp.concatenate([bias_before, bias_own], axis=1)
        bias_first = jnp.concatenate([jnp.where(n > 0, bias_before, NEG_INF), bias_own], axis=1)
        ones = jnp.ones((2 * KEY_BLOCK, HEAD_DIM), BF16)
        diag = qi == ki

        @pl.when(n == 0)
        def _():
            k16p[...] = jnp.zeros_like(k16p)
            v16p[...] = jnp.zeros_like(v16p)

        def unit(qf, kpb, kb, vpb, vb, bias, state, rows, first):
            acc_r, m_r, l_r = state
            kcat = jnp.concatenate([kpb, kb], axis=0)
            vaug = jnp.concatenate([jnp.concatenate([vpb, vb], axis=0), ones], axis=1)
            s = _dot_nt(qf.astype(BF16), kcat) + bias
            mx = jnp.max(s, axis=-1, keepdims=True)
            if first:
                m_new = jnp.broadcast_to(mx, (KEY_BLOCK, HEAD_DIM))
            else:
                m_old = m_r[rows, :]
                m_new = jnp.maximum(m_old, mx)
            pv = _dot(jnp.exp(s - jnp.concatenate([m_new, m_new], axis=1)).astype(BF16), vaug)
            if first:
                acc_r[rows, :] = pv[:, :HEAD_DIM]
                l_r[rows, :] = pv[:, HEAD_DIM:]
            else:
                alpha = jnp.exp(m_old - m_new)
                acc_r[rows, :] = alpha * acc_r[rows, :] + pv[:, :HEAD_DIM]
                l_r[rows, :] = alpha * l_r[rows, :] + pv[:, HEAD_DIM:]
            m_r[rows, :] = m_new

        for gi, dil in enumerate(DILATIONS[:-1]):
            nb = nblk // dil
            for r in range(dil):
                prow = _unit_rows(dil, r, nb - 1)
                kpb, vpb = kp_ref[prow, :].astype(BF16), vp_ref[prow, :].astype(BF16)
                for j in range(nb):
                    rows = _unit_rows(dil, r, j)
                    kb, vb = k_ref[rows, :].astype(BF16), v_ref[rows, :].astype(BF16)
                    unit(q_ref[rows, :], kpb, kb, vpb, vb, bias_first if j == 0 else bias_mid,
                         (acc, m_s, l_s), rows, gi == 0)
                    kpb, vpb = kb, vb

        for src, dst in ((q_ref, q16), (k_ref, k16), (v_ref, v16), (acc, acc16), (m_s, m16), (l_s, l16)):
            _to_residue_major(src, tmp, dst)
        for r in range(wide):
            rows = pl.ds(r * KEY_BLOCK, KEY_BLOCK)
            unit(q16[rows, :], k16p[rows, :].astype(BF16), k16[rows, :].astype(BF16), v16p[rows, :].astype(BF16),
                 v16[rows, :].astype(BF16), bias_first, (acc16, m16, l16), rows, False)
        k16p[...] = k16[...]
        v16p[...] = v16[...]

        den = l16[...]
        o16[...] = acc16[...] * (1.0 / den)
        m16[...] = m16[...] + jnp.log(den)
        _from_residue_major(o16, tmp, o_ref, False)
        _from_residue_major(m16, tmp, m_s, False)

        def lse_row(ref, rows):
            return jnp.sum(jnp.where(diag, ref[rows, :], 0.0), axis=0, keepdims=True)

        for dil, out in zip(DILATIONS[:-1], (l1_ref, l4_ref)):
            nb = nblk // dil
            for r in range(dil):
                for j in range(nb):
                    out[r * nb + j:r * nb + j + 1, :] = lse_row(m_s, _unit_rows(dil, r, j))
        for r in range(wide):
            l16_ref[r:r + 1, :] = lse_row(m16, pl.ds(r * KEY_BLOCK, KEY_BLOCK))

    cur = lambda piece: pl.BlockSpec((SPAN, HEAD_DIM), lambda h, n: (n, piece * nh + h))
    before = lambda piece: pl.BlockSpec((SPAN, HEAD_DIM), lambda h, n: (jnp.maximum(n - 1, 0), piece * nh + h))
    blk = pl.BlockSpec((None, SPAN, HEAD_DIM), lambda h, n: (h, n, 0))
    lblk = pl.BlockSpec((None, nblk, KEY_BLOCK), lambda h, n: (h, n, 0))
    lshape = jax.ShapeDtypeStruct((nh, seq // KEY_BLOCK, KEY_BLOCK), F32)
    full = jax.ShapeDtypeStruct((nh, seq, HEAD_DIM), F32)
    o, l1, l4, l16, *major = pl.pallas_call(
        body, name="attn_fwd",
        out_shape=[full, lshape, lshape, lshape] + [full] * 4,
        grid=(nh, nchunk), in_specs=[cur(2), cur(3), cur(4), before(3), before(4)],
        out_specs=[blk, lblk, lblk, lblk] + [blk] * 4,
        scratch_shapes=[pltpu.VMEM((SPAN, HEAD_DIM), F32)] * 9,
        compiler_params=_params(("arbitrary", "arbitrary"), VMEM_LIMIT),
    )(proj, proj, proj, proj, proj)
    return o, (l1, l4, l16), tuple(major)


def _to_residue_major(src, tmp, dst):
    quarter = SPAN // 4
    for r4 in range(4):
        tmp[r4 * quarter:(r4 + 1) * quarter, :] = src[pl.ds(r4, quarter, stride=4), :]
    for r4 in range(4):
        for rp in range(4):
            r = r4 + 4 * rp
            dst[r * KEY_BLOCK:(r + 1) * KEY_BLOCK, :] = tmp[pl.ds(r4 * quarter + rp, KEY_BLOCK, stride=4), :]


def _from_residue_major(src, tmp, dst, add):
    quarter = SPAN // 4
    for r4 in range(4):
        for rp in range(4):
            r = r4 + 4 * rp
            tmp[pl.ds(r4 * quarter + rp, KEY_BLOCK, stride=4), :] = src[r * KEY_BLOCK:(r + 1) * KEY_BLOCK, :]
    for r4 in range(4):
        rows = pl.ds(r4, quarter, stride=4)
        part = tmp[r4 * quarter:(r4 + 1) * quarter, :]
        dst[rows, :] = dst[rows, :] + part if add else part


def _attn_bwd(proj, do, o, lses, major, cosf, sinf):
    nh, seq = N_HEADS, proj.shape[0]
    nchunk = seq // SPAN
    nblk = SPAN // KEY_BLOCK
    wide = DILATIONS[-1]
    assert SPAN == wide * KEY_BLOCK

    def body(q_ref, k_ref, v_ref, do_ref, o_ref, kp_ref, vp_ref, q16, k16, v16, o16, l1_ref, l4_ref, l16_ref,
             cos_ref, sin_ref, cosp_ref, sinp_ref, dq_ref, dk_ref, dv_ref,
             dq_acc, dkc_acc, dvc_acc, dkp_acc, dvp_acc, do16, k16p, v16p,
             dq16, dkc16, dvc16, dkp16, dvp16, tmp, pt_s, ds_s, kcat_s, qb_s, dob_s):
        n = pl.program_id(1)
        ki = lax.broadcasted_iota(jnp.int32, (KEY_BLOCK, KEY_BLOCK), 0)
        qi = lax.broadcasted_iota(jnp.int32, (KEY_BLOCK, KEY_BLOCK), 1)
        bias_own = jnp.where(ki <= qi, 0.0, NEG_INF)
        bias_before = jnp.where(ki >= qi, 0.0, NEG_INF)
        bias_mid = jnp.concatenate([bias_before, bias_own], axis=0)
        bias_first = jnp.concatenate([jnp.where(n > 0, bias_before, NEG_INF), bias_own], axis=0)
        ones8 = jnp.ones((SUBLANES, HEAD_DIM), BF16)

        def row_dot(a, b):
            prod = a * b
            hi = prod.astype(BF16)
            lo = (prod - hi.astype(F32)).astype(BF16)
            return (_dot_nt(ones8, hi) + _dot_nt(ones8, lo))[0:1, :]

        def group(units, srcs, before, l_ref, accs):
            src_q, src_do, src_o, src_k, src_v = srcs
            before_k, before_v = before
            acc_q, acc_kc, acc_vc, acc_kp, acc_vp = accs
            kb = vb = None
            for u, (rows, prow, outside, lrow, _) in enumerate(units):
                dof = src_do[rows, :]
                qb, dob = src_q[rows, :].astype(BF16), dof.astype(BF16)
                kpb, vpb = (before_k[prow, :].astype(BF16), before_v[prow, :].astype(BF16)) if outside else (kb, vb)
                kb, vb = src_k[rows, :].astype(BF16), src_v[rows, :].astype(BF16)
                kcat = jnp.concatenate([kpb, kb], axis=0)
                vcat = jnp.concatenate([vpb, vb], axis=0)
                bias = bias_first if outside else bias_mid
                pt = jnp.exp(_dot_nt(kcat, qb) + bias - l_ref[lrow:lrow + 1, :])
                dst = pt * (_dot_nt(vcat, dob) - row_dot(dof, src_o[rows, :]))
                pt_s[u], ds_s[u], kcat_s[u], qb_s[u], dob_s[u] = pt.astype(BF16), dst.astype(BF16), kcat, qb, dob
            for u, (rows, _, _, _, _) in enumerate(units):
                acc_q[rows, :] += _dot_tn(ds_s[u], kcat_s[u])
            for u, (rows, prow, outside, _, nxt) in enumerate(units):
                dk = _dot(ds_s[u, KEY_BLOCK:, :], qb_s[u])
                dv = _dot(pt_s[u, KEY_BLOCK:, :], dob_s[u])
                if nxt is not None:
                    dk = dk + _dot(ds_s[nxt, :KEY_BLOCK, :], qb_s[nxt])
                    dv = dv + _dot(pt_s[nxt, :KEY_BLOCK, :], dob_s[nxt])
                acc_kc[rows, :] += dk
                acc_vc[rows, :] += dv
                if outside:
                    acc_kp[prow, :] += _dot(ds_s[u, :KEY_BLOCK, :], qb_s[u])
                    acc_vp[prow, :] += _dot(pt_s[u, :KEY_BLOCK, :], dob_s[u])

        @pl.when(n == 0)
        def _():
            for ref in (dkp_acc, dvp_acc, dkp16, dvp16, k16p, v16p):
                ref[...] = jnp.zeros_like(ref)

        @pl.when(n < nchunk)
        def _():
            for ref in (dq_acc, dkc_acc, dvc_acc, dq16, dkc16, dvc16):
                ref[...] = jnp.zeros_like(ref)
            _to_residue_major(do_ref, tmp, do16)
            natural = (q_ref, do_ref, o_ref, k_ref, v_ref)
            for dil, l_ref in zip(DILATIONS[:-1], (l1_ref, l4_ref)):
                nb = nblk // dil
                units = [(_unit_rows(dil, r, j), _unit_rows(dil, r, (j - 1) % nb), j == 0, r * nb + j,
                          r * nb + j + 1 if j + 1 < nb else None) for r in range(dil) for j in range(nb)]
                group(units, natural, (kp_ref, vp_ref), l_ref, (dq_acc, dkc_acc, dvc_acc, dkp_acc, dvp_acc))
            blocks = [pl.ds(r * KEY_BLOCK, KEY_BLOCK) for r in range(wide)]
            group([(rows, rows, True, r, None) for r, rows in enumerate(blocks)], (q16, do16, o16, k16, v16),
                  (k16p, v16p), l16_ref, (dq16, dkc16, dvc16, dkp16, dvp16))
            _from_residue_major(dq16, tmp, dq_acc, True)
            dq = dq_acc[...]
            dq_ref[...] = ((dq * cos_ref[...] - _rope_partner(dq) * sin_ref[...]) * ATTN_SCALE).astype(BF16)

        @pl.when(n > 0)
        def _():
            _from_residue_major(dkp16, tmp, dkp_acc, True)
            _from_residue_major(dvp16, tmp, dvp_acc, True)
            dk = dkp_acc[...]
            dk_ref[...] = (dk * cosp_ref[...] - _rope_partner(dk) * sinp_ref[...]).astype(BF16)
            dv_ref[...] = dvp_acc[...].astype(BF16)

        @pl.when(n < nchunk)
        def _():
            for src, dst in ((dkc_acc, dkp_acc), (dvc_acc, dvp_acc), (dkc16, dkp16), (dvc16, dvp16),
                             (k16, k16p), (v16, v16p)):
                dst[...] = src[...]

    last = nchunk - 1
    cur = lambda h, n: (h, jnp.minimum(n, last), 0)
    prev = lambda h, n: (h, jnp.clip(n - 1, 0, last), 0)
    blk = lambda idx: pl.BlockSpec((None, SPAN, HEAD_DIM), idx)
    lblk = pl.BlockSpec((None, nblk, KEY_BLOCK), cur)
    tab = pl.BlockSpec((SPAN, HEAD_DIM), lambda h, n: (jnp.minimum(n, last), 0))
    tabp = pl.BlockSpec((SPAN, HEAD_DIM), lambda h, n: (jnp.clip(n - 1, 0, last), 0))
    out_q = pl.BlockSpec((SPAN, HEAD_DIM), lambda h, n: (jnp.minimum(n, last), h))
    out_kv = pl.BlockSpec((SPAN, HEAD_DIM), lambda h, n: (jnp.clip(n - 1, 0, last), h))
    shape = jax.ShapeDtypeStruct((seq, nh * HEAD_DIM), BF16)
    tok = lambda piece, row: pl.BlockSpec((SPAN, HEAD_DIM), lambda h, n: (row(n), piece * nh + h))
    row_cur, row_prev = (lambda n: jnp.minimum(n, last)), (lambda n: jnp.clip(n - 1, 0, last))
    return pl.pallas_call(
        body, name="attn_bwd", out_shape=[shape, shape, shape], grid=(nh, nchunk + 1),
        in_specs=[tok(2, row_cur), tok(3, row_cur), tok(4, row_cur), blk(cur), blk(cur),
                  tok(3, row_prev), tok(4, row_prev)] + [blk(cur)] * 4 + [lblk] * 3 + [tab, tab, tabp, tabp],
        out_specs=[out_q, out_kv, out_kv],
        scratch_shapes=[pltpu.VMEM((SPAN, HEAD_DIM), F32)] * 14
                       + [pltpu.VMEM((nblk, 2 * KEY_BLOCK, HEAD_DIM), BF16)] * 3
                       + [pltpu.VMEM((nblk, KEY_BLOCK, HEAD_DIM), BF16)] * 2,
        compiler_params=_params(("arbitrary", "arbitrary"), VMEM_LIMIT),
    )(proj, proj, proj, do, o, proj, proj, *major, *lses, cosf, sinf, cosf, sinf)


def _hub(x, tgt, hr, pf, o_hm, mod, b_mod, b_gate, g_final, w_out_rnn, w_out_attn, w_o):
    seq = x.shape[0]
    tm = HUB_ROWS
    nsteps = seq // tm

    def body(x_ref, t_ref, hr_ref, zr_ref, za_ref, gr_ref, ga_ref, o_ref, mod_ref, bmod_ref, bg_ref, gf_ref,
             wr_hbm, wa_hbm, wo_hbm,
             dx2_ref, dhr_ref, dzr_ref, do_ref, dza_ref, dgr_ref, dga_ref,
             ur_ref, dyr_ref, ua_ref, dya_ref, mg_ref, dmo_ref,
             ggf_ref, gbg_ref, dgate_ref, loss_ref,
             wr, wa, wo, sem):
        step = pl.program_id(0)

        @pl.when(step == 0)
        def _():
            for src, dst in ((wr_hbm, wr), (wa_hbm, wa), (wo_hbm, wo)):
                cp = pltpu.make_async_copy(src, dst, sem)
                cp.start()
                cp.wait()
            for ref in (ggf_ref, gbg_ref, dgate_ref, loss_ref):
                ref[...] = jnp.zeros_like(ref)

        gate = mod_ref[:, 2 * D_MODEL:] + bmod_ref[:, 2 * D_MODEL:]
        gfin = gf_ref[...]
        hr_t, zr, za = hr_ref[...], zr_ref[...], za_ref[...]
        o = jnp.concatenate([o_ref[hh] for hh in range(N_HEADS)], axis=1)
        sig_zr, sig_za = _sigmoid(zr), _sigmoid(za)
        silu_zr, silu_za = zr * sig_zr, za * sig_za
        u_rnn = (hr_t * silu_zr).astype(BF16)
        u_attn = (o * silu_za).astype(BF16)
        y_rnn = _dot(u_rnn, wr[...])
        y_attn = _dot(u_attn, wa[...])
        sr = _sigmoid(gr_ref[...] + bg_ref[:, :D_MODEL])
        sa = _sigmoid(ga_ref[...] + bg_ref[:, D_MODEL:])
        merged = (sr * y_rnn + sa * y_attn).astype(BF16)
        mo = _dot(merged, wo[...])
        x2 = x_ref[...] + gate * mo
        rstd = lax.rsqrt(jnp.mean(x2 * x2, axis=-1, keepdims=True) + NORM_EPS)
        xn = x2 * rstd
        err = xn * gfin - t_ref[...]
        loss_ref[...] += 0.5 * jnp.sum(jnp.sum(err * err, axis=-1, keepdims=True) * (1.0 / D_MODEL),
                                       axis=0, keepdims=True)

        dy = err * (1.0 / D_MODEL)
        ggf_ref[...] += jnp.sum(dy * xn, axis=0, keepdims=True)
        dxn = dy * gfin
        dx2 = rstd * (dxn - xn * jnp.mean(dxn * xn, axis=-1, keepdims=True))
        dx2_ref[...] = dx2
        dgate_ref[...] += jnp.sum(dx2 * mo, axis=0, keepdims=True)
        dmo = (dx2 * gate).astype(BF16)
        dmerged = _dot_nt(dmo, wo[...])
        mg_ref[...] = merged
        dmo_ref[...] = dmo
        dy_rnn = (dmerged * sr).astype(BF16)
        dy_attn = (dmerged * sa).astype(BF16)
        dg_r = dmerged * y_rnn * sr * (1.0 - sr)
        dg_a = dmerged * y_attn * sa * (1.0 - sa)
        dgr_ref[...] = dg_r.astype(BF16)
        dga_ref[...] = dg_a.astype(BF16)
        gbg_ref[:, :D_MODEL] += jnp.sum(dg_r, axis=0, keepdims=True)
        gbg_ref[:, D_MODEL:] += jnp.sum(dg_a, axis=0, keepdims=True)
        du_rnn = _dot_nt(dy_rnn, wr[...])
        du_attn = _dot_nt(dy_attn, wa[...])
        ur_ref[...] = u_rnn
        dyr_ref[...] = dy_rnn
        ua_ref[...] = u_attn
        dya_ref[...] = dy_attn
        dhr_ref[...] = du_rnn * silu_zr
        dzr_ref[...] = (du_rnn * hr_t * (sig_zr * (1.0 + zr * (1.0 - sig_zr)))).astype(BF16)
        dza_ref[...] = (du_attn * o * (sig_za * (1.0 + za * (1.0 - sig_za)))).astype(BF16)
        d_o = du_attn * silu_za
        for hh in range(N_HEADS):
            do_ref[hh] = d_o[:, hh * HEAD_DIM:(hh + 1) * HEAD_DIM]

    row = pl.BlockSpec((tm, D_MODEL), lambda i: (i, 0))
    piece = lambda slot: pl.BlockSpec((tm, D_MODEL), lambda i: (i, slot))
    hm = pl.BlockSpec((N_HEADS, tm, HEAD_DIM), lambda i: (0, i, 0))
    const = lambda cols: pl.BlockSpec((1, cols), lambda i: (0, 0))
    any_spec = pl.BlockSpec(memory_space=pl.ANY)
    act_f32 = jax.ShapeDtypeStruct((seq, D_MODEL), F32)
    act_bf16 = jax.ShapeDtypeStruct((seq, D_MODEL), BF16)
    return pl.pallas_call(
        body, name="hub",
        out_shape=[act_f32, act_f32, act_bf16, jax.ShapeDtypeStruct((N_HEADS, seq, HEAD_DIM), F32),
                   act_bf16, act_bf16, act_bf16] + [act_bf16] * 6 + [
                   jax.ShapeDtypeStruct((1, D_MODEL), F32), jax.ShapeDtypeStruct((1, 2 * D_MODEL), F32),
                   jax.ShapeDtypeStruct((1, D_MODEL), F32), jax.ShapeDtypeStruct((1, 1), F32)],
        grid=(nsteps,),
        in_specs=[row, row, row, piece(1), piece(5), piece(6), piece(7), hm,
                  const(3 * D_MODEL), const(3 * D_MODEL), const(2 * D_MODEL), const(D_MODEL),
                  any_spec, any_spec, any_spec],
        out_specs=[row, row, row, hm, row, row, row] + [row] * 6 + [
                   const(D_MODEL), const(2 * D_MODEL), const(D_MODEL), const(1)],
        scratch_shapes=[pltpu.VMEM((D_MODEL, D_MODEL), BF16)] * 3 + [pltpu.SemaphoreType.DMA],
        compiler_params=_params(("arbitrary",), VMEM_LIMIT),
    )(x, tgt, hr, pf, pf, pf, pf, o_hm, mod, b_mod, b_gate, g_final, w_out_rnn, w_out_attn, w_o)


def _pair_grads(name, lefts, rights):
    n = len(rights)
    shared = len(lefts) == 1
    seq = rights[0].shape[0]
    tk = WGRAD_ROWS
    nk = seq // tk

    def body(*refs):
        l_refs, r_refs = refs[:len(lefts)], refs[len(lefts):len(lefts) + n]
        out_ref, low_ref = refs[len(lefts) + n:]
        j, kk = pl.program_id(0), pl.program_id(1)

        @pl.when(kk == 0)
        def _():
            out_ref[...] = jnp.zeros_like(out_ref)

        for m in range(n):
            @pl.when(j == m)
            def _(m=m):
                out_ref[...] += _dot_tn(l_refs[0 if shared else m][...], r_refs[m][...])

        @pl.when(kk == nk - 1)
        def _():
            low_ref[...] = out_ref[...].astype(BF16)

    def spec(m):
        return pl.BlockSpec((tk, D_MODEL), lambda j, kk: (jnp.where(j == m, kk, jnp.where(j < m, 0, nk - 1)), 0))

    left_specs = [pl.BlockSpec((tk, D_MODEL), lambda j, kk: (kk, 0))] if shared else [spec(m) for m in range(n)]
    out_spec = pl.BlockSpec((None, D_MODEL, D_MODEL), lambda j, kk: (j, 0, 0))
    return pl.pallas_call(
        body, name=name,
        out_shape=[jax.ShapeDtypeStruct((n, D_MODEL, D_MODEL), F32), jax.ShapeDtypeStruct((n, D_MODEL, D_MODEL), BF16)],
        grid=(n, nk),
        in_specs=left_specs + [spec(m) for m in range(n)],
        out_specs=[out_spec, out_spec],
        compiler_params=_params(("arbitrary", "arbitrary"), VMEM_LIMIT),
    )(*lefts, *rights)


def _dh_dx(pieces, w_near, w_sib, w_far, x, dx2, mod, b_mod, g_norm):
    seq = x.shape[0]
    tm = DX_ROWS

    def body(*refs):
        p_refs = refs[:8]
        near_hbm, sib_hbm, far_hbm, x_ref, dx2_ref, mod_ref, bmod_ref, g_ref = refs[8:16]
        gx_ref, dshift_ref, dscale_ref, ggn_ref, w_scr, sem = refs[16:]
        step = pl.program_id(0)

        @pl.when(step == 0)
        def _():
            me = _my_pos()
            sib = _flip(me, 1)
            moves = [(near_hbm, _index(_flip(me, 2 * m))) for m in range(4)] + [(sib_hbm, _index(sib))]
            moves += [(far_hbm, _index(_flip(sib, 2 * m))) for m in range(1, 4)]
            loads = [pltpu.make_async_copy(src.at[t], w_scr.at[t], sem.at[i]) for i, (src, t) in enumerate(moves)]
            for cp in loads:
                cp.start()
            for cp in loads:
                cp.wait()
            for ref in (dshift_ref, dscale_ref, ggn_ref):
                ref[...] = jnp.zeros_like(ref)

        dh = _dot_nt(p_refs[0][...], w_scr[0])
        for j in range(1, 8):
            dh = dh + _dot_nt(p_refs[j][...], w_scr[j])
        scale1 = 1.0 + mod_ref[:, D_MODEL:2 * D_MODEL] + bmod_ref[:, D_MODEL:2 * D_MODEL]
        g = g_ref[...]
        xf = x_ref[...]
        rstd_t = lax.rsqrt(jnp.mean(xf * xf, axis=-1, keepdims=True) + NORM_EPS)
        xn = xf * rstd_t
        dshift_ref[...] += jnp.sum(dh, axis=0, keepdims=True)
        dscale_ref[...] += jnp.sum(dh * (xn * g), axis=0, keepdims=True)
        ggn_ref[...] += jnp.sum(dh * scale1 * xn, axis=0, keepdims=True)
        dxn = dh * (g * scale1)
        gx_ref[...] = rstd_t * (dxn - xn * jnp.mean(dxn * xn, axis=-1, keepdims=True)) + dx2_ref[...]

    row = pl.BlockSpec((tm, D_MODEL), lambda i: (i, 0))
    const = lambda cols: pl.BlockSpec((1, cols), lambda i: (0, 0))
    vec = jax.ShapeDtypeStruct((1, D_MODEL), F32)
    return pl.pallas_call(
        body, name="dh_dx",
        out_shape=[jax.ShapeDtypeStruct((seq, D_MODEL), F32), vec, vec, vec],
        grid=(seq // tm,),
        in_specs=[row] * 8 + [pl.BlockSpec(memory_space=pl.ANY)] * 3 + [row, row,
                              const(3 * D_MODEL), const(3 * D_MODEL), const(D_MODEL)],
        out_specs=[row, const(D_MODEL), const(D_MODEL), const(D_MODEL)],
        scratch_shapes=[pltpu.VMEM((8, D_MODEL, D_MODEL), BF16), pltpu.SemaphoreType.DMA((8,))],
        compiler_params=_params(("arbitrary",), VMEM_LIMIT),
    )(*pieces, w_near, w_sib, w_far, x, dx2, mod, b_mod, g_norm)


def _adamw(name, w, g, m, v, recv=None):
    rows, cols = w.shape
    tr = rows if rows <= 256 else 256

    def body(*refs):
        w_ref, g_ref, m_ref, v_ref = refs[:4]
        d_ref, nm_ref, nv_ref = refs[-3:] if recv is None else refs[5:8]
        gv = g_ref[...]
        if recv is not None:
            r_ref, g_out = refs[4], refs[8]
            gv = ((gv + r_ref[0].astype(F32)) + r_ref[1].astype(F32)) + r_ref[2].astype(F32)
            g_out[...] = gv
        nm = ADAM_B1 * m_ref[...] + (1.0 - ADAM_B1) * gv
        nv = ADAM_B2 * v_ref[...] + (1.0 - ADAM_B2) * (gv * gv)
        m_hat = nm / (1.0 - ADAM_B1 ** ADAM_STEP)
        v_hat = nv / (1.0 - ADAM_B2 ** ADAM_STEP)
        d_ref[...] = -ADAM_LR * (m_hat / (jnp.sqrt(v_hat) + ADAM_EPS) + ADAM_WD * w_ref[...])
        nm_ref[...] = nm
        nv_ref[...] = nv

    spec = pl.BlockSpec((tr, cols), lambda i: (i, 0))
    shape = jax.ShapeDtypeStruct((rows, cols), F32)
    if recv is None:
        return pl.pallas_call(
            body, name=name, out_shape=[shape, shape, shape], grid=(rows // tr,),
            in_specs=[spec] * 4, out_specs=[spec] * 3,
            compiler_params=_params(("arbitrary",)),
        )(w, g, m, v)
    return pl.pallas_call(
        body, name=name, out_shape=[shape] * 4, grid=(rows // tr,),
        in_specs=[spec] * 4 + [pl.BlockSpec((3, tr, cols), lambda i: (0, i, 0))], out_specs=[spec] * 4,
        compiler_params=_params(("arbitrary",)),
    )(w, g, m, v, recv)


def kernel(x, c, positions, g_norm, w_mod, b_mod, w_in, b_gate, conv_w, conv_b, w_a, b_a, w_x, b_x, lam, w_out_rnn, w_out_attn, w_o, g_final, loss_target, m_g_norm, m_w_mod, m_b_mod, m_w_in, m_b_gate, m_conv_w, m_conv_b, m_w_a, m_b_a, m_w_x, m_b_x, m_lam, m_w_out_rnn, m_w_out_attn, m_w_o, m_g_final, v_g_norm, v_w_mod, v_b_mod, v_w_in, v_b_gate, v_conv_w, v_conv_b, v_w_a, v_b_a, v_w_x, v_b_x, v_lam, v_w_out_rnn, v_w_out_attn, v_w_o, v_g_final):
    seq = x.shape[1]
    me = _index(_my_pos())
    xs, tgt = x[0], loss_target[0]

    inv_freq = ROPE_THETA ** (-jnp.arange(0, 2 * ROT_HALF, 2, dtype=F32) / (2 * ROT_HALF))
    ang = (positions[0].astype(F32).reshape(seq // SUBLANES, SUBLANES, 1) * inv_freq).reshape(seq // SUBLANES, 128)
    cos, sin = lax.optimization_barrier((jnp.cos(ang), jnp.sin(ang)))
    cos, sin = cos.reshape(seq, ROT_HALF), sin.reshape(seq, ROT_HALF)
    rest = HEAD_DIM - 2 * ROT_HALF
    cosf = jnp.concatenate([cos, cos, jnp.ones((seq, rest), F32)], axis=1)
    sinf = jnp.concatenate([-sin, sin, jnp.zeros((seq, rest), F32)], axis=1)
    keep = (positions[0] != 0).astype(F32)[:, None]

    both = _ag_small("gather_c_conv_w", jnp.concatenate(
        [jnp.broadcast_to(c, (SUBLANES, D_MODEL)), jnp.pad(conv_w[0], ((0, SUBLANES - 4), (0, 0)))], axis=1))
    c_all, conv_w8 = both[:, 0, :D_MODEL], both[:, :, D_MODEL:]
    mod_cols = w_mod.shape[2]
    mod_part = _ag_small("gather_mod", _mod_fwd(c_all, w_mod[0]))
    mod = lax.dynamic_index_in_dim(mod_part, me, axis=1, keepdims=False).reshape(1, N_DEV * mod_cols)

    slot = lambda t: lax.dynamic_update_slice(lax.empty((N_DEV,) + t.shape, t.dtype), t[None], (me, 0, 0))
    w_in_own = w_in[0].astype(BF16)
    mod, w_in_own = lax.optimization_barrier((mod, w_in_own))
    first = _split_start("gather_w_in_start", _own_block_copies, 3, [w_in_own], [slot(w_in_own)])
    swap = _split_start("swap_w_in_start", _sibling_copy, 1, first[2], [lax.empty((N_DEV,) + w_in_own.shape, BF16)])
    mod = mod + swap[4][0:1, 0:1]

    blocks = lambda t: t.reshape(RNN_BLOCKS, 1, 128)
    rnn_params = (conv_w8, blocks(conv_b), w_a[0], blocks(b_a), w_x[0], blocks(b_x), blocks(lam))

    h = _norm(xs, mod, b_mod, g_norm)
    ids = lambda ks: jnp.bitwise_xor(me, jnp.array(ks, jnp.int32)).astype(jnp.int32)
    pf = _proj("proj_own", h, swap[2][0][None], jnp.zeros((1,), jnp.int32), ids([0]), cosf, sinf, None)
    own_thru, (w_in_sib,) = _split_wait("swap_w_in_wait", _sibling_copy, swap, pf)
    pf = _proj("proj_sibling", h, w_in_sib, ids([1]), ids([1]), cosf, sinf, pf)
    _, (w_in_near,) = _split_wait("gather_w_in_wait", _own_block_copies, (first[0], first[1], own_thru, first[3], None), pf)
    second = _split_start("forward_w_in_start", _forward_copies, 3, [w_in_near],
                          [lax.empty(w_in_near.shape, w_in_near.dtype)])
    near = ids([2, 4, 6])
    pf = _proj("proj_near", h, second[2][0], near, near, cosf, sinf, pf)
    (w_in_near,), (w_in_far,) = _split_wait("forward_w_in_wait", _forward_copies, second, pf)
    far = ids([3, 5, 7])
    pf = _proj("proj_far", h, w_in_far, far, far, cosf, sinf, pf)
    late = [w_out_rnn[0].astype(BF16), w_out_attn[0].astype(BF16), w_o[0].astype(BF16)]
    pf, late = lax.optimization_barrier((pf, late))
    flight = _split_start("gather_out_weights_start", _peer_copies, 7 * len(late), late, [slot(t) for t in late])
    rnn_params = (rnn_params[0], rnn_params[1] + flight[4][0:1, 0:1]) + rnn_params[2:]
    hr, rnn_saved = _rnn_fwd(pf, keep, *rnn_params)
    o, lses, major = _attn_fwd(pf)

    w_or_all, w_oa_all, w_o_all = (t.reshape(D_MODEL, D_MODEL) for t in _split_wait(
        "gather_out_weights_wait", _peer_copies, flight, o)[1])
    (dx2, dhr, dz_rnn, d_o, dz_attn, dg_r, dg_a, u_rnn, dy_rnn, u_attn, dy_attn, merged, dmo,
     gp_g_final, gp_b_gate, dgate, loss_part) = _hub(
        xs, tgt, hr, pf, o, mod, b_mod, b_gate, g_final.reshape(1, D_MODEL), w_or_all, w_oa_all, w_o_all)
    gp_out, gp_out_low = _pair_grads("out_grads", [u_rnn, u_attn, merged], [dy_rnn, dy_attn, dmo])
    dq, dk, dv = _attn_bwd(pf, d_o, o, lses, major, cosf, sinf)
    dx_rnn, gp_conv_w, gp_conv_b, gp_w_a, gp_b_a, gp_w_x, gp_b_x, gp_lam = _rnn_bwd(
        pf, hr, dhr, rnn_saved, keep, rnn_params[0], rnn_params[2], rnn_params[4], rnn_params[6])
    pieces = [dx_rnn, dz_rnn, dq, dk, dv, dz_attn, dg_r, dg_a]
    gp_w_in, gp_w_in_low = _pair_grads("w_in_grad", [h], pieces)

    by_target = lambda t: [(t.reshape(3, N_DEV, 128, D_MODEL), i) for i in range(3)]
    stacks = [(gp_w_in, None)] + by_target(gp_out)
    from_sib = _rs_to_sibling("rs_sibling", [(gp_w_in_low, None)] + by_target(gp_out_low))
    targets = jnp.bitwise_xor(me, 2 * jnp.arange(4, dtype=jnp.int32)).astype(jnp.int32)
    sums = [_add_sibling("rs_add_sibling_%d" % a, s_, r_, targets) for a, (s_, r_) in enumerate(zip(stacks, from_sib))]
    sends = [send for _, send in sums]
    reduce_flight = _split_start("rs_chips_start", _chip_copies, 3 * len(sends), sends,
                                 [lax.empty(t.shape, t.dtype) for t in sends])

    mod_after = mod + reduce_flight[4][0:1, 0:1]
    grad_x, dshift, dscale, gp_g_norm = _dh_dx(pieces, w_in_near, w_in_sib, w_in_far, xs, dx2, mod_after, b_mod,
                                               g_norm)

    flat = lambda t: t.reshape(-1, 128)
    dmod = flat(jnp.concatenate([dshift, dscale, dgate], axis=1))
    dmod_placed = lax.dynamic_update_slice(jnp.zeros((N_DEV * dmod.shape[0], 128), F32), dmod, (me * dmod.shape[0], 0))
    small = [flat(gp_g_norm), flat(gp_b_gate), flat(gp_conv_b), flat(gp_b_a), flat(gp_b_x), flat(gp_lam),
             flat(gp_g_final), flat(gp_conv_w), jnp.broadcast_to(loss_part, (SUBLANES, 128)),
             flat(gp_w_a), flat(gp_w_x), dmod_placed]
    sizes = [t.shape[0] for t in small]
    small.append(jnp.zeros((-sum(sizes) % (2 * SUBLANES), 128), F32))
    total = _allreduce_small("allreduce_small_grads", jnp.concatenate(small, axis=0))
    offs = [sum(sizes[:i]) for i in range(len(sizes))]
    (g_g_norm, g_b_gate, g_conv_b, g_b_a, g_b_x, g_lam, g_g_final, g_conv_w_all, loss_rows, g_w_a, g_w_x,
     dmod_rows) = (total[o_:o_ + s_] for o_, s_ in zip(offs, sizes))
    loss = loss_rows[0, 0]
    g_conv_w = lax.dynamic_index_in_dim(g_conv_w_all.reshape(RNN_BLOCKS, SUBLANES, 128), me, axis=0,
                                        keepdims=False)[:4]

    dmod_all = dmod_rows.reshape(N_DEV, 3 * D_MODEL)
    dmod_cols = lax.dynamic_slice_in_dim(dmod_all, me * mod_cols, mod_cols, axis=1)
    g_b_mod, g_w_mod = _mod_bwd(c_all, dmod_all, dmod_cols)

    _, from_chips = _split_wait("rs_chips_wait", _chip_copies, reduce_flight, total)

    results = {}
    sharded = (("w_in", w_in, m_w_in, v_w_in, (D_MODEL, D_MODEL)),
               ("w_out_rnn", w_out_rnn, m_w_out_rnn, v_w_out_rnn, (128, D_MODEL)),
               ("w_out_attn", w_out_attn, m_w_out_attn, v_w_out_attn, (128, D_MODEL)),
               ("w_o", w_o, m_w_o, v_w_o, (128, D_MODEL)))
    for (name, w_, m_, v_, shape2), (own, _), arrived in zip(sharded, sums, from_chips):
        d_, nm_, nv_, g_ = _adamw("adamw_" + name, w_.reshape(shape2), own, m_.reshape(shape2), v_.reshape(shape2),
                                  arrived)
        results[name] = (g_, d_, nm_, nv_)
    shape2 = (D_MODEL, mod_cols)
    results["w_mod"] = (g_w_mod,) + tuple(_adamw("adamw_w_mod", w_mod.reshape(shape2), g_w_mod,
                                                 m_w_mod.reshape(shape2), v_w_mod.reshape(shape2)))
    lanes = (("g_norm", g_norm, g_g_norm, m_g_norm, v_g_norm), ("b_mod", b_mod, g_b_mod, m_b_mod, v_b_mod),
             ("b_gate", b_gate, g_b_gate, m_b_gate, v_b_gate), ("conv_w", conv_w, g_conv_w, m_conv_w, v_conv_w),
             ("conv_b", conv_b, g_conv_b, m_conv_b, v_conv_b), ("w_a", w_a, g_w_a, m_w_a, v_w_a),
             ("b_a", b_a, g_b_a, m_b_a, v_b_a), ("w_x", w_x, g_w_x, m_w_x, v_w_x), ("b_x", b_x, g_b_x, m_b_x, v_b_x),
             ("lam", lam, g_lam, m_lam, v_lam), ("g_final", g_final, g_g_final, m_g_final, v_g_final))
    for name, w_, g_, m_, v_ in lanes:
        rows128 = lambda t: t.reshape(-1, 128)
        results[name] = (g_,) + tuple(_adamw("adamw_" + name, rows128(w_), rows128(g_), rows128(m_), rows128(v_)))
    order = ("g_norm", "w_mod", "b_mod", "w_in", "b_gate", "conv_w", "conv_b", "w_a", "b_a", "w_x", "b_x", "lam",
             "w_out_rnn", "w_out_attn", "w_o", "g_final")
    given = dict(g_norm=g_norm, w_mod=w_mod, b_mod=b_mod, w_in=w_in, b_gate=b_gate, conv_w=conv_w, conv_b=conv_b,
                 w_a=w_a, b_a=b_a, w_x=w_x, b_x=b_x, lam=lam, w_out_rnn=w_out_rnn, w_out_attn=w_out_attn, w_o=w_o,
                 g_final=g_final)
    outs = [[results[name][k].reshape(given[name].shape) for name in order] for k in range(4)]
    return (loss, grad_x[None], *outs[0], *outs[1], *outs[2], *outs[3])
```

```python
import jax
import jax.numpy as jnp
from jax import lax
from jax.experimental import pallas as pl
from jax.experimental.pallas import tpu as pltpu

F32 = jnp.float32
BF16 = jnp.bfloat16
MESH = pl.DeviceIdType.MESH

D_MODEL = 1024
N_HEADS = 8
HEAD_DIM = 128
RNN_BLOCKS = 8
N_DEV = 8
ROT_HALF = 16
ROPE_THETA = 500000.0
DILATIONS = (1, 4, 16)
KEY_BLOCK = 128
SPAN = KEY_BLOCK * DILATIONS[-1]
ATTN_SCALE = HEAD_DIM ** -0.5
NORM_EPS = 1e-6
LRU_C = 8.0
NEG_INF = -1e30
ADAM_LR, ADAM_B1, ADAM_B2, ADAM_EPS, ADAM_WD, ADAM_STEP = 0.001, 0.9, 0.999, 1e-08, 0.01, 10

SUBLANES = 8
VMEM_LIMIT = 56 * 1024 * 1024
PROJ_ROWS = 1024
PROJ_RING = 3
HUB_RING = 3
RNN_ROWS = 2048
HUB_ROWS = 256
DX_ROWS = 512
WGRAD_ROWS = 1024
ADD_ROWS = 256


def _params(sem=None, vmem=None):
    return pltpu.CompilerParams(dimension_semantics=sem, vmem_limit_bytes=vmem)


def _dot(a, b):
    return jnp.dot(a, b, preferred_element_type=F32)


def _dot_nt(a, b):
    return lax.dot_general(a, b, (((1,), (1,)), ((), ())), preferred_element_type=F32)


def _dot_tn(a, b):
    return lax.dot_general(a, b, (((0,), (0,)), ((), ())), preferred_element_type=F32)


def _sigmoid(z):
    return 1.0 / (1.0 + jnp.exp(-z))


def _expm1_nonpos(z, exp_z):
    return jnp.where(z > -0.01, z * (1.0 + 0.5 * z), exp_z - 1.0)


def _my_pos():
    return lax.axis_index("x"), lax.axis_index("y"), lax.axis_index("c")


def _flip(pos, k):
    x, y, c = pos
    return ((1 - x) if k & 4 else x, (1 - y) if k & 2 else y, (1 - c) if k & 1 else c)


def _index(pos):
    return 4 * pos[0] + 2 * pos[1] + pos[2]


def _ag_small(name, v):
    rows, cols = v.shape

    def body(v_ref, out_ref, send_sems, recv_sems):
        me = _my_pos()
        out_ref[_index(me)] = v_ref[...]
        sends = []
        for k in range(1, N_DEV):
            cp = pltpu.make_async_remote_copy(
                src_ref=v_ref, dst_ref=out_ref.at[_index(me)], send_sem=send_sems.at[k - 1],
                recv_sem=recv_sems.at[k - 1], device_id=_flip(me, k), device_id_type=MESH)
            cp.start()
            sends.append(cp)
        for k in range(1, N_DEV):
            peer = _flip(me, k)
            pltpu.make_async_remote_copy(
                src_ref=v_ref, dst_ref=out_ref.at[_index(peer)], send_sem=send_sems.at[k - 1],
                recv_sem=recv_sems.at[k - 1], device_id=peer, device_id_type=MESH).wait_recv()
        for cp in sends:
            cp.wait_send()

    return pl.pallas_call(
        body, name=name,
        out_shape=jax.ShapeDtypeStruct((N_DEV, rows, cols), v.dtype),
        in_specs=[pl.BlockSpec(memory_space=pltpu.VMEM)],
        out_specs=pl.BlockSpec(memory_space=pltpu.VMEM),
        scratch_shapes=[pltpu.SemaphoreType.DMA((N_DEV - 1,)), pltpu.SemaphoreType.DMA((N_DEV - 1,))],
        compiler_params=_params(None, VMEM_LIMIT),
    )(v)


def _split_start(name, make_copies, nsem, srcs, lands):
    n, k = len(srcs), len(lands)

    def body(*refs):
        for cp in make_copies(refs[:n], refs[n:n + k], refs[n + k], refs[n + k + 1]):
            cp.start()
        refs[-1][...] = jnp.zeros_like(refs[-1])

    hbm = pl.BlockSpec(memory_space=pltpu.HBM)
    sem = pl.BlockSpec(memory_space=pltpu.SEMAPHORE)
    arrays = [*srcs, *lands]
    outs = pl.pallas_call(
        body, name=name,
        out_shape=(pltpu.SemaphoreType.DMA((nsem,)), pltpu.SemaphoreType.DMA((nsem,)),
                   *[pltpu.HBM(t.shape, t.dtype) for t in arrays], jax.ShapeDtypeStruct((SUBLANES, 128), F32)),
        in_specs=[hbm] * (n + k),
        out_specs=(sem, sem, *[hbm] * (n + k), pl.BlockSpec(memory_space=pltpu.VMEM)),
        input_output_aliases={i: 2 + i for i in range(n + k)},
        compiler_params=pltpu.CompilerParams(has_side_effects=pltpu.SideEffectType.DATAFLOW_SIDE_EFFECTING),
    )(*[pltpu.with_memory_space_constraint(t, pltpu.HBM) for t in arrays])
    return outs[0], outs[1], outs[2:2 + n], outs[2 + n:2 + n + k], outs[-1]


def _split_wait(name, make_copies, flight, after):
    send_sems, recv_sems, srcs, lands, _ = flight
    n, k = len(srcs), len(lands)

    def body(*refs):
        for cp in make_copies(refs[:n], refs[n:n + k], refs[n + k], refs[n + k + 1]):
            cp.wait_send()
            cp.wait_recv()

    hbm = pl.BlockSpec(memory_space=pltpu.HBM)
    sem = pl.BlockSpec(memory_space=pltpu.SEMAPHORE)
    arrays = [*srcs, *lands]
    outs = pl.pallas_call(
        body, name=name, out_shape=tuple(pltpu.HBM(t.shape, t.dtype) for t in arrays),
        in_specs=[hbm] * (n + k) + [sem, sem, pl.BlockSpec(memory_space=pl.ANY)],
        out_specs=[hbm] * (n + k),
        input_output_aliases={i: i for i in range(n + k)},
        compiler_params=pltpu.CompilerParams(has_side_effects=pltpu.SideEffectType.DATAFLOW_SIDE_EFFECTING),
    )(*arrays, send_sems, recv_sems, after)
    return outs[:n], outs[n:]


def _remote(src, dst, send_sems, recv_sems, k, to):
    return pltpu.make_async_remote_copy(src_ref=src, dst_ref=dst, send_sem=send_sems.at[k], recv_sem=recv_sems.at[k],
                                        device_id=to, device_id_type=MESH)


def _peer_copies(shards, lands, send_sems, recv_sems):
    me = _my_pos()
    return [_remote(shards[a], lands[a].at[_index(me)], send_sems, recv_sems, a * 7 + k - 1, _flip(me, k))
            for a in range(len(shards)) for k in range(1, N_DEV)]


def _own_block_copies(shards, lands, send_sems, recv_sems):
    me = _my_pos()
    return [_remote(shards[0], lands[0].at[_index(me)], send_sems, recv_sems, m - 1, _flip(me, 2 * m))
            for m in range(1, 4)]


def _sibling_copy(shards, lands, send_sems, recv_sems):
    me = _my_pos()
    return [_remote(shards[0], lands[0].at[_index(me)], send_sems, recv_sems, 0, _flip(me, 1))]


def _forward_copies(arrived, lands, send_sems, recv_sems):
    me = _my_pos()
    return [_remote(arrived[0].at[_index(_flip(me, 2 * m))], lands[0].at[_index(_flip(me, 2 * m))],
                    send_sems, recv_sems, m - 1, _flip(me, 1)) for m in range(1, 4)]


def _rs_to_sibling(name, stacks):
    n = len(stacks)

    def body(*refs):
        ins, outs = refs[:n], refs[n:2 * n]
        send_sems, recv_sems = refs[2 * n:]
        me = _my_pos()
        sib = _flip(me, 1)
        sends = []
        for a, (_, which) in enumerate(stacks):
            by_target = ins[a] if which is None else ins[a].at[which]
            for m in range(4):
                target = _flip(sib, 2 * m)
                cp = pltpu.make_async_remote_copy(
                    src_ref=by_target.at[_index(target)], dst_ref=outs[a].at[m],
                    send_sem=send_sems.at[a * 4 + m], recv_sem=recv_sems.at[a * 4 + m],
                    device_id=sib, device_id_type=MESH)
                cp.start()
                sends.append(cp)
        for cp in sends:
            cp.wait_recv()
        for cp in sends:
            cp.wait_send()

    any_spec = pl.BlockSpec(memory_space=pl.ANY)
    return pl.pallas_call(
        body, name=name,
        out_shape=[jax.ShapeDtypeStruct((4,) + s.shape[-2:], s.dtype) for s, _ in stacks],
        in_specs=[any_spec] * n, out_specs=[any_spec] * n,
        scratch_shapes=[pltpu.SemaphoreType.DMA((4 * n,)), pltpu.SemaphoreType.DMA((4 * n,))],
    )(*[s for s, _ in stacks])


def _chip_copies(srcs, lands, send_sems, recv_sems):
    me = _my_pos()
    return [_remote(srcs[a].at[m - 1], lands[a].at[m - 1], send_sems, recv_sems, a * 3 + m - 1, _flip(me, 2 * m))
            for a in range(len(srcs)) for m in range(1, 4)]


def _add_sibling(name, stack, recv, targets):
    stack, which = stack
    rows, cols = stack.shape[-2:]
    tr = min(rows, ADD_ROWS)

    def by_target(index):
        if which is None:
            return pl.BlockSpec((None, tr, cols), lambda *g: (index(*g), g[-2], 0))
        return pl.BlockSpec((None, None, tr, cols), lambda *g: (which, index(*g), g[-2], 0))

    def own_body(t_ref, a_ref, b_ref, o_ref):
        o_ref[...] = a_ref[...] + b_ref[...].astype(F32)

    own = pl.pallas_call(
        own_body, name=name + "_own",
        out_shape=jax.ShapeDtypeStruct((rows, cols), F32),
        grid_spec=pltpu.PrefetchScalarGridSpec(
            num_scalar_prefetch=1, grid=(rows // tr,),
            in_specs=[by_target(lambda i, t: t[0]),
                      pl.BlockSpec((None, tr, cols), lambda i, t: (0, i, 0))],
            out_specs=pl.BlockSpec((tr, cols), lambda i, t: (i, 0))),
        compiler_params=_params(("arbitrary",)),
    )(targets, stack, recv)

    def send_body(t_ref, a_ref, b_ref, o_ref):
        o_ref[...] = (a_ref[...] + b_ref[...].astype(F32)).astype(BF16)

    send = pl.pallas_call(
        send_body, name=name + "_send",
        out_shape=jax.ShapeDtypeStruct((3, rows, cols), BF16),
        grid_spec=pltpu.PrefetchScalarGridSpec(
            num_scalar_prefetch=1, grid=(3, rows // tr),
            in_specs=[by_target(lambda m, i, t: t[m + 1]),
                      pl.BlockSpec((None, tr, cols), lambda m, i, t: (m + 1, i, 0))],
            out_specs=pl.BlockSpec((None, tr, cols), lambda m, i, t: (m, i, 0))),
        compiler_params=_params(("arbitrary", "arbitrary")),
    )(targets, stack, recv)
    return own, send


def _allreduce_small(name, v):
    rows, cols = v.shape
    half = rows // 2
    assert rows % (2 * SUBLANES) == 0

    def body(v_ref, out_ref, from_sib, chip_half, from_chips, send_sems, recv_sems):
        me = _my_pos()
        sib = _flip(me, 1)
        mine = pl.ds(pl.multiple_of(me[2] * half, SUBLANES), half)
        theirs = pl.ds(pl.multiple_of((1 - me[2]) * half, SUBLANES), half)

        def copy(k, src, dst, to):
            return pltpu.make_async_remote_copy(src_ref=src, dst_ref=dst, send_sem=send_sems.at[k],
                                                recv_sem=recv_sems.at[k], device_id=to, device_id_type=MESH)

        to_sib = copy(0, v_ref.at[theirs], from_sib, sib)
        to_sib.start()
        to_sib.wait_recv()
        chip_half[...] = v_ref[mine, :] + from_sib[...]
        to_chips = [copy(m, chip_half, from_chips.at[m - 1], _flip(me, 2 * m)) for m in range(1, 4)]
        for cp in to_chips:
            cp.start()
        for cp in to_chips:
            cp.wait_recv()
        my_chip = 2 * me[0] + me[1]
        total = None
        for chip in range(4):
            slot = jnp.maximum(jnp.bitwise_xor(chip, my_chip) - 1, 0)
            part = jnp.where(chip == my_chip, chip_half[...], from_chips[slot])
            total = part if total is None else total + part
        out_ref[mine, :] = total
        swap = copy(4, out_ref.at[mine], out_ref.at[mine], sib)
        swap.start()
        copy(4, out_ref.at[theirs], out_ref.at[theirs], sib).wait_recv()
        for cp in [to_sib, swap] + to_chips:
            cp.wait_send()

    return pl.pallas_call(
        body, name=name, out_shape=jax.ShapeDtypeStruct((rows, cols), F32),
        in_specs=[pl.BlockSpec(memory_space=pltpu.VMEM)],
        out_specs=pl.BlockSpec(memory_space=pltpu.VMEM),
        scratch_shapes=[pltpu.VMEM((half, cols), F32), pltpu.VMEM((half, cols), F32),
                        pltpu.VMEM((3, half, cols), F32),
                        pltpu.SemaphoreType.DMA((5,)), pltpu.SemaphoreType.DMA((5,))],
        compiler_params=_params(None, VMEM_LIMIT),
    )(v)


def _mod_fwd(c_all, w_mod):
    def body(c_ref, w_ref, o_ref):
        c = c_ref[...]
        o_ref[...] = jnp.dot(c * _sigmoid(c), w_ref[...], preferred_element_type=F32,
                             precision=lax.Precision.HIGHEST)

    return pl.pallas_call(
        body, name="mod_fwd", out_shape=jax.ShapeDtypeStruct((N_DEV, w_mod.shape[1]), F32),
    )(c_all, w_mod)


def _mod_bwd(c_all, dmod_all, dmod_cols):
    def body(c_ref, da_ref, dc_ref, gb_ref, gw_ref):
        c = c_ref[...]
        acc = da_ref[0:1, :]
        for b in range(1, N_DEV):
            acc = acc + da_ref[b:b + 1, :]
        gb_ref[...] = acc
        gw_ref[...] = lax.dot_general(c * _sigmoid(c), dc_ref[...], (((0,), (0,)), ((), ())),
                                      preferred_element_type=F32, precision=lax.Precision.HIGHEST)

    return pl.pallas_call(
        body, name="mod_bwd",
        out_shape=[jax.ShapeDtypeStruct((1, dmod_all.shape[1]), F32),
                   jax.ShapeDtypeStruct((c_all.shape[1], dmod_cols.shape[1]), F32)],
    )(c_all, dmod_all, dmod_cols)


def _rope_partner(t):
    lane = lax.broadcasted_iota(jnp.int32, t.shape, 1)
    return jnp.where(lane < ROT_HALF, pltpu.roll(t, HEAD_DIM - ROT_HALF, 1), pltpu.roll(t, ROT_HALF, 1))


def _norm(x, mod, b_mod, g_norm):
    seq = x.shape[0]
    tm = PROJ_ROWS

    def body(x_ref, mod_ref, bmod_ref, g_ref, h_ref):
        xf = x_ref[...]
        rstd = lax.rsqrt(jnp.mean(xf * xf, axis=-1, keepdims=True) + NORM_EPS)
        shift = mod_ref[:, 0:D_MODEL] + bmod_ref[:, 0:D_MODEL]
        scale = mod_ref[:, D_MODEL:2 * D_MODEL] + bmod_ref[:, D_MODEL:2 * D_MODEL]
        h_ref[...] = (((xf * rstd) * g_ref[...]) * (1.0 + scale) + shift).astype(BF16)

    row = pl.BlockSpec((tm, D_MODEL), lambda i: (i, 0))
    const = lambda cols: pl.BlockSpec((1, cols), lambda i: (0, 0))
    return pl.pallas_call(
        body, name="norm", out_shape=jax.ShapeDtypeStruct((seq, D_MODEL), BF16), grid=(seq // tm,),
        in_specs=[row, const(3 * D_MODEL), const(3 * D_MODEL), const(D_MODEL)], out_specs=row,
        compiler_params=_params(("arbitrary",), VMEM_LIMIT),
    )(x, mod, b_mod, g_norm)


def _proj(name, h, w, slots, pieces, cosf, sinf, prior):
    seq = h.shape[0]
    tm = PROJ_ROWS
    count = pieces.shape[0]
    nrow = seq // tm
    total = count * nrow
    assert total >= PROJ_RING - 1

    def body(slots_ref, pieces_ref, h_hbm, w_ref, cos_ref, sin_ref, *rest):
        out_ref, h_ring, sems = rest[-3:]
        piece = pieces_ref[pl.program_id(0)]
        step = pl.program_id(0) * nrow + pl.program_id(1)

        def fetch(s):
            slot = s % PROJ_RING
            return pltpu.make_async_copy(h_hbm.at[pl.ds((s % nrow) * tm, tm)], h_ring.at[slot], sems.at[slot])

        @pl.when(step == 0)
        def _():
            for s in range(PROJ_RING - 1):
                fetch(s).start()

        @pl.when(step + PROJ_RING - 1 < total)
        def _():
            fetch(step + PROJ_RING - 1).start()

        fetch(step).wait()
        h_ref = h_ring.at[step % PROJ_RING]

        @pl.when((piece < 2) | (piece > 3))
        def _():
            out_ref[...] = _dot(h_ref[...], w_ref[...])

        def rotated(gain):
            for pair in range(N_HEADS // 2):
                both = _dot(h_ref[...], w_ref[:, 2 * pair * HEAD_DIM:2 * (pair + 1) * HEAD_DIM])
                for hh in (2 * pair, 2 * pair + 1):
                    t = both[:, (hh % 2) * HEAD_DIM:(hh % 2 + 1) * HEAD_DIM]
                    t = t * cos_ref[...] + _rope_partner(t) * sin_ref[...]
                    out_ref[:, hh * HEAD_DIM:(hh + 1) * HEAD_DIM] = t if gain is None else t * gain

        @pl.when(piece == 2)
        def _():
            rotated(ATTN_SCALE)

        @pl.when(piece == 3)
        def _():
            rotated(None)

    table = lambda j, i, sl, pc: (jnp.where((pc[j] == 2) | (pc[j] == 3), i, 0), 0)
    in_specs = [pl.BlockSpec(memory_space=pl.ANY),
                pl.BlockSpec((None, D_MODEL, D_MODEL), lambda j, i, sl, pc: (sl[j], 0, 0)),
                pl.BlockSpec((tm, HEAD_DIM), table), pl.BlockSpec((tm, HEAD_DIM), table)]
    args = [slots, pieces, h, w, cosf, sinf]
    aliases = {}
    if prior is not None:
        in_specs.append(pl.BlockSpec(memory_space=pl.ANY))
        args.append(prior)
        aliases = {6: 0}
    return pl.pallas_call(
        body, name=name,
        out_shape=jax.ShapeDtypeStruct((seq, 8 * D_MODEL), F32),
        grid_spec=pltpu.PrefetchScalarGridSpec(
            num_scalar_prefetch=2, grid=(count, nrow), in_specs=in_specs,
            out_specs=pl.BlockSpec((tm, D_MODEL), lambda j, i, sl, pc: (i, pc[j])),
            scratch_shapes=[pltpu.VMEM((PROJ_RING, tm, D_MODEL), BF16), pltpu.SemaphoreType.DMA((PROJ_RING,))]),
        input_output_aliases=aliases,
        compiler_params=_params(("arbitrary", "arbitrary"), VMEM_LIMIT),
    )(*args)


def _shift_down(v, s, head):
    rolled = pltpu.roll(v, s, 0)
    row = lax.broadcasted_iota(jnp.int32, head.shape, 0)
    first = jnp.where(row < s, pltpu.roll(head, s, 0), rolled[:SUBLANES, :])
    return jnp.concatenate([first, rolled[SUBLANES:, :]], axis=0)


def _shift_up(v, s, tail):
    rows = v.shape[0]
    rolled = pltpu.roll(v, rows - s, 0)
    row = lax.broadcasted_iota(jnp.int32, tail.shape, 0)
    last = jnp.where(row >= SUBLANES - s, pltpu.roll(tail, SUBLANES - s, 0), rolled[rows - SUBLANES:, :])
    return jnp.concatenate([rolled[:rows - SUBLANES, :], last], axis=0)


def _doubling(a, b, period, reverse):
    rows = a.shape[0]
    pos = lax.broadcasted_iota(jnp.int32, a.shape, 0) & (period - 1)
    k = 1
    while k < period:
        inside = (pos < period - k) if reverse else (pos >= k)
        shift = rows - k if reverse else k
        a_s = jnp.where(inside, pltpu.roll(a, shift, 0), 1.0)
        b_s = jnp.where(inside, pltpu.roll(b, shift, 0), 0.0)
        b = a * b_s + b
        a = a * a_s
        k *= 2
    return a, b


def _scan(a, b, boundary, reverse, a_scr, b_scr, spread):
    rows = a.shape[0]
    ntile = rows // SUBLANES
    a_scr[...], b_scr[...] = _doubling(a, b, SUBLANES, reverse)
    ends = pl.ds(0 if reverse else SUBLANES - 1, ntile, stride=SUBLANES)
    a_end, x_end = _doubling(a_scr[ends, :], b_scr[ends, :], ntile, reverse)
    x_end = x_end + a_end * boundary
    tile = lax.broadcasted_iota(jnp.int32, x_end.shape, 0)
    if reverse:
        incoming = jnp.where(tile == ntile - 1, boundary, pltpu.roll(x_end, ntile - 1, 0))
        last = x_end[0:1, :]
    else:
        incoming = jnp.where(tile == 0, boundary, pltpu.roll(x_end, 1, 0))
        last = x_end[ntile - 1:ntile, :]
    for s in range(SUBLANES):
        spread[pl.ds(s, ntile, stride=SUBLANES), :] = incoming
    return b_scr[...] + a_scr[...] * spread[...], last


def _conv_taps(xr, head):
    return [_shift_down(xr, 3, head), _shift_down(xr, 2, head), _shift_down(xr, 1, head), xr]


def _rnn_gates(xc, wa, ba, wx, bx, lam, keep):
    xcb = xc.astype(BF16)
    r = _sigmoid(_dot(xcb, wa.astype(BF16)) + ba)
    i = _sigmoid(_dot(xcb, wx.astype(BF16)) + bx)
    softplus = jnp.maximum(-lam, 0.0) + jnp.log(1.0 + jnp.exp(-jnp.abs(lam)))
    cl = -LRU_C * softplus
    log_a = cl * r
    a_raw = jnp.exp(log_a)
    mult_raw = jnp.sqrt(-_expm1_nonpos(2.0 * log_a, a_raw * a_raw))
    live = keep > 0.0
    return r, i, cl, a_raw, mult_raw, jnp.where(live, a_raw, 0.0), jnp.where(live, mult_raw, 1.0), live


def _rnn_specs(seq, rows, time_of):
    per = rows // SUBLANES
    vec = pl.BlockSpec((None, 1, 128), lambda hb, n: (hb, 0, 0))
    mat = pl.BlockSpec((None, 128, 128), lambda hb, n: (hb, 0, 0))
    return [pl.BlockSpec((rows, 128), lambda hb, n: (time_of(n), hb)),
            pl.BlockSpec((SUBLANES, 128), lambda hb, n: (jnp.maximum(time_of(n) * per - 1, 0), hb)),
            pl.BlockSpec((rows, 1), lambda hb, n: (time_of(n), 0)),
            pl.BlockSpec((None, SUBLANES, 128), lambda hb, n: (hb, 0, 0)),
            vec, mat, vec, mat, vec, vec]


def _rnn_fwd(pf, keep, conv_w8, conv_b, w_a, b_a, w_x, b_x, lam):
    seq = pf.shape[0]
    rows = RNN_ROWS

    def body(x_ref, xh_ref, keep_ref, cw_ref, cb_ref, wa_ref, ba_ref, wx_ref, bx_ref, lam_ref,
             hr_ref, xc_ref, r_ref, i_ref, araw_ref, mraw_ref, carry, a_scr, b_scr, spread):
        n = pl.program_id(1)

        @pl.when(n == 0)
        def _():
            carry[...] = jnp.zeros_like(carry)

        xr = x_ref[...]
        head = jnp.where(n > 0, xh_ref[...], 0.0)
        taps = _conv_taps(xr, head)
        xc = cb_ref[...] + sum(cw_ref[k:k + 1, :] * taps[k] for k in range(4))
        r, i, _, a_raw, mult_raw, a, mult, _ = _rnn_gates(xc, wa_ref[...], ba_ref[...], wx_ref[...], bx_ref[...],
                                                          lam_ref[...], keep_ref[...])
        xc_ref[...], r_ref[...], i_ref[...], araw_ref[...], mraw_ref[...] = xc, r, i, a_raw, mult_raw
        h, last = _scan(a, mult * i * xc, carry[0:1, :], False, a_scr, b_scr, spread)
        hr_ref[...] = h
        carry[...] = jnp.broadcast_to(last, carry.shape)

    chunk_f32 = pltpu.VMEM((rows, 128), F32)
    chunk = pl.BlockSpec((rows, 128), lambda hb, n: (n, hb))
    shape = jax.ShapeDtypeStruct((seq, D_MODEL), F32)
    outs = pl.pallas_call(
        body, name="rnn_fwd",
        out_shape=[shape] * 6,
        grid=(RNN_BLOCKS, seq // rows),
        in_specs=_rnn_specs(seq, rows, lambda n: n),
        out_specs=[chunk] * 6,
        scratch_shapes=[pltpu.VMEM((SUBLANES, 128), F32), chunk_f32, chunk_f32, chunk_f32],
        compiler_params=_params(("arbitrary", "arbitrary"), VMEM_LIMIT),
    )(pf, pf, keep, conv_w8, conv_b, w_a, b_a, w_x, b_x, lam)
    return outs[0], tuple(outs[1:])


def _rnn_bwd(pf, hr, dhr, saved, keep, conv_w8, w_a, w_x, lam):
    seq = pf.shape[0]
    rows = RNN_ROWS
    nchunk = seq // rows
    per = rows // SUBLANES
    time_of = lambda n: nchunk - 1 - n

    def body(x_ref, keep_ref, cw_ref, wa_ref, wx_ref, lam_ref, hr_ref, hrh_ref, dhr_ref,
             xc_ref, r_ref, i_ref, araw_ref, mraw_ref,
             dx_ref, gcw_ref, gcb_ref, gwa_ref, gba_ref, gwx_ref, gbx_ref, glam_ref,
             g_carry, dxc_tail, a_scr, b_scr, spread):
        n = pl.program_id(1)
        first_in_time = n == nchunk - 1

        @pl.when(n == 0)
        def _():
            g_carry[...] = jnp.zeros_like(g_carry)
            dxc_tail[...] = jnp.zeros_like(dxc_tail)
            for ref in (gcw_ref, gcb_ref, gwa_ref, gba_ref, gwx_ref, gbx_ref, glam_ref):
                ref[...] = jnp.zeros_like(ref)

        cw, wa, wx, lam = cw_ref[...], wa_ref[...], wx_ref[...], lam_ref[...]
        xc, r, i, a_raw, mult_raw = xc_ref[...], r_ref[...], i_ref[...], araw_ref[...], mraw_ref[...]
        cl = -LRU_C * (jnp.maximum(-lam, 0.0) + jnp.log(1.0 + jnp.exp(-jnp.abs(lam))))
        live = keep_ref[...] > 0.0
        a, mult = jnp.where(live, a_raw, 0.0), jnp.where(live, mult_raw, 1.0)
        h_prev = _shift_down(hr_ref[...], 1, jnp.where(first_in_time, 0.0, hrh_ref[...]))

        row = lax.broadcasted_iota(jnp.int32, xc.shape, 0)
        last = row == rows - 1
        a_next = jnp.where(last, 0.0, pltpu.roll(a, rows - 1, 0))
        g, g_first = _scan(a_next, dhr_ref[...] + jnp.where(last, g_carry[0:1, :], 0.0),
                           jnp.zeros((1, 128), F32), True, a_scr, b_scr, spread)
        g_carry[...] = jnp.broadcast_to(a[0:1, :] * g_first, g_carry.shape)

        da = g * h_prev
        dmult = g * i * xc
        di = g * mult * xc
        dxc = g * mult * i
        dlog_a = jnp.where(live, da * a_raw - dmult * a_raw * a_raw / mult_raw, 0.0)
        dpa = (dlog_a * cl) * r * (1.0 - r)
        dpx = di * i * (1.0 - i)
        glam_ref[...] += jnp.sum(dlog_a * r, axis=0, keepdims=True) * (LRU_C * _sigmoid(-lam))
        xcb, dpab, dpxb = xc.astype(BF16), dpa.astype(BF16), dpx.astype(BF16)
        gwa_ref[...] += _dot_tn(xcb, dpab)
        gwx_ref[...] += _dot_tn(xcb, dpxb)
        gba_ref[...] += jnp.sum(dpa, axis=0, keepdims=True)
        gbx_ref[...] += jnp.sum(dpx, axis=0, keepdims=True)
        dxc = dxc + _dot_nt(dpab, wa.astype(BF16)) + _dot_nt(dpxb, wx.astype(BF16))

        gcb_ref[...] += jnp.sum(dxc, axis=0, keepdims=True)
        xr = x_ref[...]
        tail = dxc_tail[...]
        later = [_shift_up(dxc, 3 - k, tail) for k in range(3)] + [dxc]
        dx = cw[3:4, :] * dxc
        for k in range(3):
            dx = dx + cw[k:k + 1, :] * later[k]
        for k in range(4):
            gcw_ref[k:k + 1, :] += jnp.sum(xr * later[k], axis=0, keepdims=True)
        dx_ref[...] = dx.astype(BF16)
        dxc_tail[...] = dxc[0:SUBLANES, :]

    blk = lambda hb, n: (hb, 0, 0)
    chunk = pl.BlockSpec((rows, 128), lambda hb, n: (time_of(n), hb))
    vec = pl.BlockSpec((None, 1, 128), blk)
    mat = pl.BlockSpec((None, 128, 128), blk)
    vec_shape = jax.ShapeDtypeStruct((RNN_BLOCKS, 1, 128), F32)
    mat_shape = jax.ShapeDtypeStruct((RNN_BLOCKS, 128, 128), F32)
    return pl.pallas_call(
        body, name="rnn_bwd",
        out_shape=[jax.ShapeDtypeStruct((seq, D_MODEL), BF16),
                   jax.ShapeDtypeStruct((RNN_BLOCKS, SUBLANES, 128), F32), vec_shape,
                   mat_shape, vec_shape, mat_shape, vec_shape, vec_shape],
        grid=(RNN_BLOCKS, nchunk),
        in_specs=[chunk, pl.BlockSpec((rows, 1), lambda hb, n: (time_of(n), 0)),
                  pl.BlockSpec((None, SUBLANES, 128), blk), mat, mat, vec, chunk,
                  pl.BlockSpec((SUBLANES, 128), lambda hb, n: (jnp.maximum(time_of(n) * per - 1, 0), hb)), chunk]
                 + [chunk] * 5,
        out_specs=[chunk, pl.BlockSpec((None, SUBLANES, 128), blk), vec, mat, vec, mat, vec, vec],
        scratch_shapes=[pltpu.VMEM((SUBLANES, 128), F32), pltpu.VMEM((SUBLANES, 128), F32)]
                       + [pltpu.VMEM((rows, 128), F32)] * 3,
        compiler_params=_params(("arbitrary", "arbitrary"), VMEM_LIMIT),
    )(pf, keep, conv_w8, w_a, w_x, lam, hr, hr, dhr, *saved)


def _unit_rows(dil, r, j):
    start = j * KEY_BLOCK * dil + r
    return pl.ds(start, KEY_BLOCK) if dil == 1 else pl.ds(start, KEY_BLOCK, stride=dil)


def _attn_fwd(proj):
    nh, seq = N_HEADS, proj.shape[0]
    nchunk = seq // SPAN
    nblk = SPAN // KEY_BLOCK
    wide = DILATIONS[-1]

    def body(q_ref, k_ref, v_ref, kp_ref, vp_ref, o_ref, l1_ref, l4_ref, l16_ref, q16, k16, v16, o16,
             acc, m_s, l_s, k16p, v16p, acc16, m16, l16, tmp):
        n = pl.program_id(1)
        qi = lax.broadcasted_iota(jnp.int32, (KEY_BLOCK, KEY_BLOCK), 0)
        ki = lax.broadcasted_iota(jnp.int32, (KEY_BLOCK, KEY_BLOCK), 1)
        bias_own = jnp.where(ki <= qi, 0.0, NEG_INF)
        bias_before = jnp.where(ki >= qi, 0.0, NEG_INF)
        bias_mid = jnp.concatenate([bias_before, bias_own], axis=1)
        bias_first = jnp.concatenate([jnp.where(n > 0, bias_before, NEG_INF), bias_own], axis=1)
        ones = jnp.ones((2 * KEY_BLOCK, HEAD_DIM), BF16)
        diag = qi == ki

        @pl.when(n == 0)
        def _():
            k16p[...] = jnp.zeros_like(k16p)
            v16p[...] = jnp.zeros_like(v16p)

        def unit(qf, kpb, kb, vpb, vb, bias, state, rows, first):
            acc_r, m_r, l_r = state
            kcat = jnp.concatenate([kpb, kb], axis=0)
            vaug = jnp.concatenate([jnp.concatenate([vpb, vb], axis=0), ones], axis=1)
            s = _dot_nt(qf.astype(BF16), kcat) + bias
            mx = jnp.max(s, axis=-1, keepdims=True)
            if first:
                m_new = jnp.broadcast_to(mx, (KEY_BLOCK, HEAD_DIM))
            else:
                m_old = m_r[rows, :]
                m_new = jnp.maximum(m_old, mx)
            pv = _dot(jnp.exp(s - jnp.concatenate([m_new, m_new], axis=1)).astype(BF16), vaug)
            if first:
                acc_r[rows, :] = pv[:, :HEAD_DIM]
                l_r[rows, :] = pv[:, HEAD_DIM:]
            else:
                alpha = jnp.exp(m_old - m_new)
                acc_r[rows, :] = alpha * acc_r[rows, :] + pv[:, :HEAD_DIM]
                l_r[rows, :] = alpha * l_r[rows, :] + pv[:, HEAD_DIM:]
            m_r[rows, :] = m_new

        for gi, dil in enumerate(DILATIONS[:-1]):
            nb = nblk // dil
            for r in range(dil):
                prow = _unit_rows(dil, r, nb - 1)
                kpb, vpb = kp_ref[prow, :].astype(BF16), vp_ref[prow, :].astype(BF16)
                for j in range(nb):
                    rows = _unit_rows(dil, r, j)
                    kb, vb = k_ref[rows, :].astype(BF16), v_ref[rows, :].astype(BF16)
                    unit(q_ref[rows, :], kpb, kb, vpb, vb, bias_first if j == 0 else bias_mid,
                         (acc, m_s, l_s), rows, gi == 0)
                    kpb, vpb = kb, vb

        for src, dst in ((q_ref, q16), (k_ref, k16), (v_ref, v16), (acc, acc16), (m_s, m16), (l_s, l16)):
            _to_residue_major(src, tmp, dst)
        for r in range(wide):
            rows = pl.ds(r * KEY_BLOCK, KEY_BLOCK)
            unit(q16[rows, :], k16p[rows, :].astype(BF16), k16[rows, :].astype(BF16), v16p[rows, :].astype(BF16),
                 v16[rows, :].astype(BF16), bias_first, (acc16, m16, l16), rows, False)
        k16p[...] = k16[...]
        v16p[...] = v16[...]

        den = l16[...]
        o16[...] = acc16[...] * (1.0 / den)
        m16[...] = m16[...] + jnp.log(den)
        _from_residue_major(o16, tmp, o_ref, False)
        _from_residue_major(m16, tmp, m_s, False)

        def lse_row(ref, rows):
            return jnp.sum(jnp.where(diag, ref[rows, :], 0.0), axis=0, keepdims=True)

        for dil, out in zip(DILATIONS[:-1], (l1_ref, l4_ref)):
            nb = nblk // dil
            for r in range(dil):
                for j in range(nb):
                    out[r * nb + j:r * nb + j + 1, :] = lse_row(m_s, _unit_rows(dil, r, j))
        for r in range(wide):
            l16_ref[r:r + 1, :] = lse_row(m16, pl.ds(r * KEY_BLOCK, KEY_BLOCK))

    cur = lambda piece: pl.BlockSpec((SPAN, HEAD_DIM), lambda h, n: (n, piece * nh + h))
    before = lambda piece: pl.BlockSpec((SPAN, HEAD_DIM), lambda h, n: (jnp.maximum(n - 1, 0), piece * nh + h))
    blk = pl.BlockSpec((None, SPAN, HEAD_DIM), lambda h, n: (h, n, 0))
    lblk = pl.BlockSpec((None, nblk, KEY_BLOCK), lambda h, n: (h, n, 0))
    lshape = jax.ShapeDtypeStruct((nh, seq // KEY_BLOCK, KEY_BLOCK), F32)
    full = jax.ShapeDtypeStruct((nh, seq, HEAD_DIM), F32)
    o, l1, l4, l16, *major = pl.pallas_call(
        body, name="attn_fwd",
        out_shape=[full, lshape, lshape, lshape] + [full] * 4,
        grid=(nh, nchunk), in_specs=[cur(2), cur(3), cur(4), before(3), before(4)],
        out_specs=[blk, lblk, lblk, lblk] + [blk] * 4,
        scratch_shapes=[pltpu.VMEM((SPAN, HEAD_DIM), F32)] * 9,
        compiler_params=_params(("arbitrary", "arbitrary"), VMEM_LIMIT),
    )(proj, proj, proj, proj, proj)
    return o, (l1, l4, l16), tuple(major)


def _to_residue_major(src, tmp, dst):
    quarter = SPAN // 4
    for r4 in range(4):
        tmp[r4 * quarter:(r4 + 1) * quarter, :] = src[pl.ds(r4, quarter, stride=4), :]
    for r4 in range(4):
        for rp in range(4):
            r = r4 + 4 * rp
            dst[r * KEY_BLOCK:(r + 1) * KEY_BLOCK, :] = tmp[pl.ds(r4 * quarter + rp, KEY_BLOCK, stride=4), :]


def _from_residue_major(src, tmp, dst, add):
    quarter = SPAN // 4
    for r4 in range(4):
        for rp in range(4):
            r = r4 + 4 * rp
            tmp[pl.ds(r4 * quarter + rp, KEY_BLOCK, stride=4), :] = src[r * KEY_BLOCK:(r + 1) * KEY_BLOCK, :]
    for r4 in range(4):
        rows = pl.ds(r4, quarter, stride=4)
        part = tmp[r4 * quarter:(r4 + 1) * quarter, :]
        dst[rows, :] = dst[rows, :] + part if add else part


def _attn_bwd(proj, do, o, lses, major, cosf, sinf):
    nh, seq = N_HEADS, proj.shape[0]
    nchunk = seq // SPAN
    nblk = SPAN // KEY_BLOCK
    wide = DILATIONS[-1]
    assert SPAN == wide * KEY_BLOCK

    def body(q_ref, k_ref, v_ref, do_ref, o_ref, kp_ref, vp_ref, q16, k16, v16, o16, l1_ref, l4_ref, l16_ref,
             cos_ref, sin_ref, cosp_ref, sinp_ref, dq_ref, dk_ref, dv_ref,
             dq_acc, dkc_acc, dvc_acc, dkp_acc, dvp_acc, do16, k16p, v16p,
             dq16, dkc16, dvc16, dkp16, dvp16, tmp, pt_s, ds_s, kcat_s, qb_s, dob_s):
        n = pl.program_id(1)
        ki = lax.broadcasted_iota(jnp.int32, (KEY_BLOCK, KEY_BLOCK), 0)
        qi = lax.broadcasted_iota(jnp.int32, (KEY_BLOCK, KEY_BLOCK), 1)
        bias_own = jnp.where(ki <= qi, 0.0, NEG_INF)
        bias_before = jnp.where(ki >= qi, 0.0, NEG_INF)
        bias_mid = jnp.concatenate([bias_before, bias_own], axis=0)
        bias_first = jnp.concatenate([jnp.where(n > 0, bias_before, NEG_INF), bias_own], axis=0)
        ones8 = jnp.ones((SUBLANES, HEAD_DIM), BF16)

        def row_dot(a, b):
            prod = a * b
            hi = prod.astype(BF16)
            lo = (prod - hi.astype(F32)).astype(BF16)
            return (_dot_nt(ones8, hi) + _dot_nt(ones8, lo))[0:1, :]

        def group(units, srcs, before, l_ref, accs):
            src_q, src_do, src_o, src_k, src_v = srcs
            before_k, before_v = before
            acc_q, acc_kc, acc_vc, acc_kp, acc_vp = accs
            kb = vb = None
            for u, (rows, prow, outside, lrow, _) in enumerate(units):
                dof = src_do[rows, :]
                qb, dob = src_q[rows, :].astype(BF16), dof.astype(BF16)
                kpb, vpb = (before_k[prow, :].astype(BF16), before_v[prow, :].astype(BF16)) if outside else (kb, vb)
                kb, vb = src_k[rows, :].astype(BF16), src_v[rows, :].astype(BF16)
                kcat = jnp.concatenate([kpb, kb], axis=0)
                vcat = jnp.concatenate([vpb, vb], axis=0)
                bias = bias_first if outside else bias_mid
                pt = jnp.exp(_dot_nt(kcat, qb) + bias - l_ref[lrow:lrow + 1, :])
                dst = pt * (_dot_nt(vcat, dob) - row_dot(dof, src_o[rows, :]))
                pt_s[u], ds_s[u], kcat_s[u], qb_s[u], dob_s[u] = pt.astype(BF16), dst.astype(BF16), kcat, qb, dob
            for u, (rows, _, _, _, _) in enumerate(units):
                acc_q[rows, :] += _dot_tn(ds_s[u], kcat_s[u])
            for u, (rows, prow, outside, _, nxt) in enumerate(units):
                dk = _dot(ds_s[u, KEY_BLOCK:, :], qb_s[u])
                dv = _dot(pt_s[u, KEY_BLOCK:, :], dob_s[u])
                if nxt is not None:
                    dk = dk + _dot(ds_s[nxt, :KEY_BLOCK, :], qb_s[nxt])
                    dv = dv + _dot(pt_s[nxt, :KEY_BLOCK, :], dob_s[nxt])
                acc_kc[rows, :] += dk
                acc_vc[rows, :] += dv
                if outside:
                    acc_kp[prow, :] += _dot(ds_s[u, :KEY_BLOCK, :], qb_s[u])
                    acc_vp[prow, :] += _dot(pt_s[u, :KEY_BLOCK, :], dob_s[u])

        @pl.when(n == 0)
        def _():
            for ref in (dkp_acc, dvp_acc, dkp16, dvp16, k16p, v16p):
                ref[...] = jnp.zeros_like(ref)

        @pl.when(n < nchunk)
        def _():
            for ref in (dq_acc, dkc_acc, dvc_acc, dq16, dkc16, dvc16):
                ref[...] = jnp.zeros_like(ref)
            _to_residue_major(do_ref, tmp, do16)
            natural = (q_ref, do_ref, o_ref, k_ref, v_ref)
            for dil, l_ref in zip(DILATIONS[:-1], (l1_ref, l4_ref)):
                nb = nblk // dil
                units = [(_unit_rows(dil, r, j), _unit_rows(dil, r, (j - 1) % nb), j == 0, r * nb + j,
                          r * nb + j + 1 if j + 1 < nb else None) for r in range(dil) for j in range(nb)]
                group(units, natural, (kp_ref, vp_ref), l_ref, (dq_acc, dkc_acc, dvc_acc, dkp_acc, dvp_acc))
            blocks = [pl.ds(r * KEY_BLOCK, KEY_BLOCK) for r in range(wide)]
            group([(rows, rows, True, r, None) for r, rows in enumerate(blocks)], (q16, do16, o16, k16, v16),
                  (k16p, v16p), l16_ref, (dq16, dkc16, dvc16, dkp16, dvp16))
            _from_residue_major(dq16, tmp, dq_acc, True)
            dq = dq_acc[...]
            dq_ref[...] = ((dq * cos_ref[...] - _rope_partner(dq) * sin_ref[...]) * ATTN_SCALE).astype(BF16)

        @pl.when(n > 0)
        def _():
            _from_residue_major(dkp16, tmp, dkp_acc, True)
            _from_residue_major(dvp16, tmp, dvp_acc, True)
            dk = dkp_acc[...]
            dk_ref[...] = (dk * cosp_ref[...] - _rope_partner(dk) * sinp_ref[...]).astype(BF16)
            dv_ref[...] = dvp_acc[...].astype(BF16)

        @pl.when(n < nchunk)
        def _():
            for src, dst in ((dkc_acc, dkp_acc), (dvc_acc, dvp_acc), (dkc16, dkp16), (dvc16, dvp16),
                             (k16, k16p), (v16, v16p)):
                dst[...] = src[...]

    last = nchunk - 1
    cur = lambda h, n: (h, jnp.minimum(n, last), 0)
    prev = lambda h, n: (h, jnp.clip(n - 1, 0, last), 0)
    blk = lambda idx: pl.BlockSpec((None, SPAN, HEAD_DIM), idx)
    lblk = pl.BlockSpec((None, nblk, KEY_BLOCK), cur)
    tab = pl.BlockSpec((SPAN, HEAD_DIM), lambda h, n: (jnp.minimum(n, last), 0))
    tabp = pl.BlockSpec((SPAN, HEAD_DIM), lambda h, n: (jnp.clip(n - 1, 0, last), 0))
    out_q = pl.BlockSpec((SPAN, HEAD_DIM), lambda h, n: (jnp.minimum(n, last), h))
    out_kv = pl.BlockSpec((SPAN, HEAD_DIM), lambda h, n: (jnp.clip(n - 1, 0, last), h))
    shape = jax.ShapeDtypeStruct((seq, nh * HEAD_DIM), BF16)
    tok = lambda piece, row: pl.BlockSpec((SPAN, HEAD_DIM), lambda h, n: (row(n), piece * nh + h))
    row_cur, row_prev = (lambda n: jnp.minimum(n, last)), (lambda n: jnp.clip(n - 1, 0, last))
    return pl.pallas_call(
        body, name="attn_bwd", out_shape=[shape, shape, shape], grid=(nh, nchunk + 1),
        in_specs=[tok(2, row_cur), tok(3, row_cur), tok(4, row_cur), blk(cur), blk(cur),
                  tok(3, row_prev), tok(4, row_prev)] + [blk(cur)] * 4 + [lblk] * 3 + [tab, tab, tabp, tabp],
        out_specs=[out_q, out_kv, out_kv],
        scratch_shapes=[pltpu.VMEM((SPAN, HEAD_DIM), F32)] * 14
                       + [pltpu.VMEM((nblk, 2 * KEY_BLOCK, HEAD_DIM), BF16)] * 3
                       + [pltpu.VMEM((nblk, KEY_BLOCK, HEAD_DIM), BF16)] * 2,
        compiler_params=_params(("arbitrary", "arbitrary"), VMEM_LIMIT),
    )(proj, proj, proj, do, o, proj, proj, *major, *lses, cosf, sinf, cosf, sinf)


def _hub(x, tgt, hr, pf, o_hm, mod, b_mod, b_gate, g_final, w_out_rnn, w_out_attn, w_o):
    seq = x.shape[0]
    tm = HUB_ROWS
    nsteps = seq // tm

    def body(x_ref, t_ref, hr_ref, pf_hbm, o_ref, mod_ref, bmod_ref, bg_ref, gf_ref,
             wr_hbm, wa_hbm, wo_hbm,
             dx2_ref, dhr_ref, dzr_ref, do_ref, dza_ref, dgr_ref, dga_ref,
             ur_ref, dyr_ref, ua_ref, dya_ref, mg_ref, dmo_ref,
             ggf_ref, gbg_ref, dgate_ref, loss_ref,
             wr, wa, wo, sem, ring, ring_sems):
        step = pl.program_id(0)
        streams = ((pf_hbm, 1), (pf_hbm, 5), (pf_hbm, 6), (pf_hbm, 7))

        def fetch(s, m):
            src, col = streams[m]
            slot = s % HUB_RING
            return pltpu.make_async_copy(src.at[pl.ds(s * tm, tm), pl.ds(col * D_MODEL, D_MODEL)],
                                         ring.at[m, slot], ring_sems.at[m, slot])

        @pl.when(step == 0)
        def _():
            for s in range(HUB_RING - 1):
                for m in range(len(streams)):
                    fetch(s, m).start()
            for src, dst in ((wr_hbm, wr), (wa_hbm, wa), (wo_hbm, wo)):
                cp = pltpu.make_async_copy(src, dst, sem)
                cp.start()
                cp.wait()
            for ref in (ggf_ref, gbg_ref, dgate_ref, loss_ref):
                ref[...] = jnp.zeros_like(ref)

        @pl.when(step + HUB_RING - 1 < nsteps)
        def _():
            for m in range(len(streams)):
                fetch(step + HUB_RING - 1, m).start()

        for m in range(len(streams)):
            fetch(step, m).wait()
        zr_ref, za_ref, gr_ref, ga_ref = (ring.at[m, step % HUB_RING] for m in range(len(streams)))

        gate = mod_ref[:, 2 * D_MODEL:] + bmod_ref[:, 2 * D_MODEL:]
        gfin = gf_ref[...]
        hr_t, zr, za = hr_ref[...], zr_ref[...], za_ref[...]
        o = jnp.concatenate([o_ref[hh] for hh in range(N_HEADS)], axis=1)
        sig_zr, sig_za = _sigmoid(zr), _sigmoid(za)
        silu_zr, silu_za = zr * sig_zr, za * sig_za
        u_rnn = (hr_t * silu_zr).astype(BF16)
        u_attn = (o * silu_za).astype(BF16)
        y_rnn = _dot(u_rnn, wr[...])
        y_attn = _dot(u_attn, wa[...])
        sr = _sigmoid(gr_ref[...] + bg_ref[:, :D_MODEL])
        sa = _sigmoid(ga_ref[...] + bg_ref[:, D_MODEL:])
        merged = (sr * y_rnn + sa * y_attn).astype(BF16)
        mo = _dot(merged, wo[...])
        x2 = x_ref[...] + gate * mo
        rstd = lax.rsqrt(jnp.mean(x2 * x2, axis=-1, keepdims=True) + NORM_EPS)
        xn = x2 * rstd
        err = xn * gfin - t_ref[...]
        loss_ref[...] += 0.5 * jnp.sum(jnp.sum(err * err, axis=-1, keepdims=True) * (1.0 / D_MODEL),
                                       axis=0, keepdims=True)

        dy = err * (1.0 / D_MODEL)
        ggf_ref[...] += jnp.sum(dy * xn, axis=0, keepdims=True)
        dxn = dy * gfin
        dx2 = rstd * (dxn - xn * jnp.mean(dxn * xn, axis=-1, keepdims=True))
        dx2_ref[...] = dx2
        dgate_ref[...] += jnp.sum(dx2 * mo, axis=0, keepdims=True)
        dmo = (dx2 * gate).astype(BF16)
        dmerged = _dot_nt(dmo, wo[...])
        mg_ref[...] = merged
        dmo_ref[...] = dmo
        dy_rnn = (dmerged * sr).astype(BF16)
        dy_attn = (dmerged * sa).astype(BF16)
        dg_r = dmerged * y_rnn * sr * (1.0 - sr)
        dg_a = dmerged * y_attn * sa * (1.0 - sa)
        dgr_ref[...] = dg_r.astype(BF16)
        dga_ref[...] = dg_a.astype(BF16)
        gbg_ref[:, :D_MODEL] += jnp.sum(dg_r, axis=0, keepdims=True)
        gbg_ref[:, D_MODEL:] += jnp.sum(dg_a, axis=0, keepdims=True)
        du_rnn = _dot_nt(dy_rnn, wr[...])
        du_attn = _dot_nt(dy_attn, wa[...])
        ur_ref[...] = u_rnn
        dyr_ref[...] = dy_rnn
        ua_ref[...] = u_attn
        dya_ref[...] = dy_attn
        dhr_ref[...] = du_rnn * silu_zr
        dzr_ref[...] = (du_rnn * hr_t * (sig_zr * (1.0 + zr * (1.0 - sig_zr)))).astype(BF16)
        dza_ref[...] = (du_attn * o * (sig_za * (1.0 + za * (1.0 - sig_za)))).astype(BF16)
        d_o = du_attn * silu_za
        for hh in range(N_HEADS):
            do_ref[hh] = d_o[:, hh * HEAD_DIM:(hh + 1) * HEAD_DIM]

    row = pl.BlockSpec((tm, D_MODEL), lambda i: (i, 0))
    hm = pl.BlockSpec((N_HEADS, tm, HEAD_DIM), lambda i: (0, i, 0))
    const = lambda cols: pl.BlockSpec((1, cols), lambda i: (0, 0))
    any_spec = pl.BlockSpec(memory_space=pl.ANY)
    act_f32 = jax.ShapeDtypeStruct((seq, D_MODEL), F32)
    act_bf16 = jax.ShapeDtypeStruct((seq, D_MODEL), BF16)
    return pl.pallas_call(
        body, name="hub",
        out_shape=[act_f32, act_f32, act_bf16, jax.ShapeDtypeStruct((N_HEADS, seq, HEAD_DIM), F32),
                   act_bf16, act_bf16, act_bf16] + [act_bf16] * 6 + [
                   jax.ShapeDtypeStruct((1, D_MODEL), F32), jax.ShapeDtypeStruct((1, 2 * D_MODEL), F32),
                   jax.ShapeDtypeStruct((1, D_MODEL), F32), jax.ShapeDtypeStruct((1, 1), F32)],
        grid=(nsteps,),
        in_specs=[row, row, row, any_spec, hm,
                  const(3 * D_MODEL), const(3 * D_MODEL), const(2 * D_MODEL), const(D_MODEL),
                  any_spec, any_spec, any_spec],
        out_specs=[row, row, row, hm, row, row, row] + [row] * 6 + [
                   const(D_MODEL), const(2 * D_MODEL), const(D_MODEL), const(1)],
        scratch_shapes=[pltpu.VMEM((D_MODEL, D_MODEL), BF16)] * 3 + [
            pltpu.SemaphoreType.DMA, pltpu.VMEM((4, HUB_RING, tm, D_MODEL), F32),
            pltpu.SemaphoreType.DMA((4, HUB_RING))],
        compiler_params=_params(("arbitrary",), VMEM_LIMIT),
    )(x, tgt, hr, pf, o_hm, mod, b_mod, b_gate, g_final, w_out_rnn, w_out_attn, w_o)


def _pair_grads(name, lefts, rights):
    n = len(rights)
    shared = len(lefts) == 1
    seq = rights[0].shape[0]
    tk = WGRAD_ROWS
    nk = seq // tk

    def body(*refs):
        l_refs, r_refs = refs[:len(lefts)], refs[len(lefts):len(lefts) + n]
        out_ref, low_ref = refs[len(lefts) + n:]
        j, kk = pl.program_id(0), pl.program_id(1)

        @pl.when(kk == 0)
        def _():
            out_ref[...] = jnp.zeros_like(out_ref)

        for m in range(n):
            @pl.when(j == m)
            def _(m=m):
                out_ref[...] += _dot_tn(l_refs[0 if shared else m][...], r_refs[m][...])

        @pl.when(kk == nk - 1)
        def _():
            low_ref[...] = out_ref[...].astype(BF16)

    def spec(m):
        return pl.BlockSpec((tk, D_MODEL), lambda j, kk: (jnp.where(j == m, kk, jnp.where(j < m, 0, nk - 1)), 0))

    left_specs = [pl.BlockSpec((tk, D_MODEL), lambda j, kk: (kk, 0))] if shared else [spec(m) for m in range(n)]
    out_spec = pl.BlockSpec((None, D_MODEL, D_MODEL), lambda j, kk: (j, 0, 0))
    return pl.pallas_call(
        body, name=name,
        out_shape=[jax.ShapeDtypeStruct((n, D_MODEL, D_MODEL), F32), jax.ShapeDtypeStruct((n, D_MODEL, D_MODEL), BF16)],
        grid=(n, nk),
        in_specs=left_specs + [spec(m) for m in range(n)],
        out_specs=[out_spec, out_spec],
        compiler_params=_params(("arbitrary", "arbitrary"), VMEM_LIMIT),
    )(*lefts, *rights)


def _dh_dx(pieces, w_near, w_sib, w_far, x, dx2, mod, b_mod, g_norm):
    seq = x.shape[0]
    tm = DX_ROWS

    def body(*refs):
        p_refs = refs[:8]
        near_hbm, sib_hbm, far_hbm, x_ref, dx2_ref, mod_ref, bmod_ref, g_ref = refs[8:16]
        gx_ref, dshift_ref, dscale_ref, ggn_ref, w_scr, sem = refs[16:]
        step = pl.program_id(0)

        @pl.when(step == 0)
        def _():
            me = _my_pos()
            sib = _flip(me, 1)
            moves = [(near_hbm, _index(_flip(me, 2 * m))) for m in range(4)] + [(sib_hbm, _index(sib))]
            moves += [(far_hbm, _index(_flip(sib, 2 * m))) for m in range(1, 4)]
            loads = [pltpu.make_async_copy(src.at[t], w_scr.at[t], sem.at[i]) for i, (src, t) in enumerate(moves)]
            for cp in loads:
                cp.start()
            for cp in loads:
                cp.wait()
            for ref in (dshift_ref, dscale_ref, ggn_ref):
                ref[...] = jnp.zeros_like(ref)

        dh = _dot_nt(p_refs[0][...], w_scr[0])
        for j in range(1, 8):
            dh = dh + _dot_nt(p_refs[j][...], w_scr[j])
        scale1 = 1.0 + mod_ref[:, D_MODEL:2 * D_MODEL] + bmod_ref[:, D_MODEL:2 * D_MODEL]
        g = g_ref[...]
        xf = x_ref[...]
        rstd_t = lax.rsqrt(jnp.mean(xf * xf, axis=-1, keepdims=True) + NORM_EPS)
        xn = xf * rstd_t
        dshift_ref[...] += jnp.sum(dh, axis=0, keepdims=True)
        dscale_ref[...] += jnp.sum(dh * (xn * g), axis=0, keepdims=True)
        ggn_ref[...] += jnp.sum(dh * scale1 * xn, axis=0, keepdims=True)
        dxn = dh * (g * scale1)
        gx_ref[...] = rstd_t * (dxn - xn * jnp.mean(dxn * xn, axis=-1, keepdims=True)) + dx2_ref[...]

    row = pl.BlockSpec((tm, D_MODEL), lambda i: (i, 0))
    const = lambda cols: pl.BlockSpec((1, cols), lambda i: (0, 0))
    vec = jax.ShapeDtypeStruct((1, D_MODEL), F32)
    return pl.pallas_call(
        body, name="dh_dx",
        out_shape=[jax.ShapeDtypeStruct((seq, D_MODEL), F32), vec, vec, vec],
        grid=(seq // tm,),
        in_specs=[row] * 8 + [pl.BlockSpec(memory_space=pl.ANY)] * 3 + [row, row,
                              const(3 * D_MODEL), const(3 * D_MODEL), const(D_MODEL)],
        out_specs=[row, const(D_MODEL), const(D_MODEL), const(D_MODEL)],
        scratch_shapes=[pltpu.VMEM((8, D_MODEL, D_MODEL), BF16), pltpu.SemaphoreType.DMA((8,))],
        compiler_params=_params(("arbitrary",), VMEM_LIMIT),
    )(*pieces, w_near, w_sib, w_far, x, dx2, mod, b_mod, g_norm)


def _adamw(name, w, g, m, v, recv=None):
    rows, cols = w.shape
    tr = rows if rows <= 256 else 256

    def body(*refs):
        w_ref, g_ref, m_ref, v_ref = refs[:4]
        d_ref, nm_ref, nv_ref = refs[-3:] if recv is None else refs[5:8]
        gv = g_ref[...]
        if recv is not None:
            r_ref, g_out = refs[4], refs[8]
            gv = ((gv + r_ref[0].astype(F32)) + r_ref[1].astype(F32)) + r_ref[2].astype(F32)
            g_out[...] = gv
        nm = ADAM_B1 * m_ref[...] + (1.0 - ADAM_B1) * gv
        nv = ADAM_B2 * v_ref[...] + (1.0 - ADAM_B2) * (gv * gv)
        m_hat = nm / (1.0 - ADAM_B1 ** ADAM_STEP)
        v_hat = nv / (1.0 - ADAM_B2 ** ADAM_STEP)
        d_ref[...] = -ADAM_LR * (m_hat / (jnp.sqrt(v_hat) + ADAM_EPS) + ADAM_WD * w_ref[...])
        nm_ref[...] = nm
        nv_ref[...] = nv

    spec = pl.BlockSpec((tr, cols), lambda i: (i, 0))
    shape = jax.ShapeDtypeStruct((rows, cols), F32)
    if recv is None:
        return pl.pallas_call(
            body, name=name, out_shape=[shape, shape, shape], grid=(rows // tr,),
            in_specs=[spec] * 4, out_specs=[spec] * 3,
            compiler_params=_params(("arbitrary",)),
        )(w, g, m, v)
    return pl.pallas_call(
        body, name=name, out_shape=[shape] * 4, grid=(rows // tr,),
        in_specs=[spec] * 4 + [pl.BlockSpec((3, tr, cols), lambda i: (0, i, 0))], out_specs=[spec] * 4,
        compiler_params=_params(("arbitrary",)),
    )(w, g, m, v, recv)


def kernel(x, c, positions, g_norm, w_mod, b_mod, w_in, b_gate, conv_w, conv_b, w_a, b_a, w_x, b_x, lam, w_out_rnn, w_out_attn, w_o, g_final, loss_target, m_g_norm, m_w_mod, m_b_mod, m_w_in, m_b_gate, m_conv_w, m_conv_b, m_w_a, m_b_a, m_w_x, m_b_x, m_lam, m_w_out_rnn, m_w_out_attn, m_w_o, m_g_final, v_g_norm, v_w_mod, v_b_mod, v_w_in, v_b_gate, v_conv_w, v_conv_b, v_w_a, v_b_a, v_w_x, v_b_x, v_lam, v_w_out_rnn, v_w_out_attn, v_w_o, v_g_final):
    seq = x.shape[1]
    me = _index(_my_pos())
    xs, tgt = x[0], loss_target[0]

    inv_freq = ROPE_THETA ** (-jnp.arange(0, 2 * ROT_HALF, 2, dtype=F32) / (2 * ROT_HALF))
    ang = (positions[0].astype(F32).reshape(seq // SUBLANES, SUBLANES, 1) * inv_freq).reshape(seq // SUBLANES, 128)
    cos, sin = lax.optimization_barrier((jnp.cos(ang), jnp.sin(ang)))
    cos, sin = cos.reshape(seq, ROT_HALF), sin.reshape(seq, ROT_HALF)
    rest = HEAD_DIM - 2 * ROT_HALF
    cosf = jnp.concatenate([cos, cos, jnp.ones((seq, rest), F32)], axis=1)
    sinf = jnp.concatenate([-sin, sin, jnp.zeros((seq, rest), F32)], axis=1)
    keep = (positions[0] != 0).astype(F32)[:, None]

    both = _ag_small("gather_c_conv_w", jnp.concatenate(
        [jnp.broadcast_to(c, (SUBLANES, D_MODEL)), jnp.pad(conv_w[0], ((0, SUBLANES - 4), (0, 0)))], axis=1))
    c_all, conv_w8 = both[:, 0, :D_MODEL], both[:, :, D_MODEL:]
    mod_cols = w_mod.shape[2]
    mod_part = _ag_small("gather_mod", _mod_fwd(c_all, w_mod[0]))
    mod = lax.dynamic_index_in_dim(mod_part, me, axis=1, keepdims=False).reshape(1, N_DEV * mod_cols)

    slot = lambda t: lax.dynamic_update_slice(lax.empty((N_DEV,) + t.shape, t.dtype), t[None], (me, 0, 0))
    w_in_own = w_in[0].astype(BF16)
    mod, w_in_own = lax.optimization_barrier((mod, w_in_own))
    first = _split_start("gather_w_in_start", _own_block_copies, 3, [w_in_own], [slot(w_in_own)])
    swap = _split_start("swap_w_in_start", _sibling_copy, 1, first[2], [lax.empty((N_DEV,) + w_in_own.shape, BF16)])
    mod = mod + swap[4][0:1, 0:1]

    blocks = lambda t: t.reshape(RNN_BLOCKS, 1, 128)
    rnn_params = (conv_w8, blocks(conv_b), w_a[0], blocks(b_a), w_x[0], blocks(b_x), blocks(lam))

    h = _norm(xs, mod, b_mod, g_norm)
    ids = lambda ks: jnp.bitwise_xor(me, jnp.array(ks, jnp.int32)).astype(jnp.int32)
    pf = _proj("proj_own", h, swap[2][0][None], jnp.zeros((1,), jnp.int32), ids([0]), cosf, sinf, None)
    own_thru, (w_in_sib,) = _split_wait("swap_w_in_wait", _sibling_copy, swap, pf)
    pf = _proj("proj_sibling", h, w_in_sib, ids([1]), ids([1]), cosf, sinf, pf)
    _, (w_in_near,) = _split_wait("gather_w_in_wait", _own_block_copies, (first[0], first[1], own_thru, first[3], None), pf)
    second = _split_start("forward_w_in_start", _forward_copies, 3, [w_in_near],
                          [lax.empty(w_in_near.shape, w_in_near.dtype)])
    near = ids([2, 4, 6])
    pf = _proj("proj_near", h, second[2][0], near, near, cosf, sinf, pf)
    (w_in_near,), (w_in_far,) = _split_wait("forward_w_in_wait", _forward_copies, second, pf)
    far = ids([3, 5, 7])
    pf = _proj("proj_far", h, w_in_far, far, far, cosf, sinf, pf)
    late = [w_out_rnn[0].astype(BF16), w_out_attn[0].astype(BF16), w_o[0].astype(BF16)]
    pf, late = lax.optimization_barrier((pf, late))
    flight = _split_start("gather_out_weights_start", _peer_copies, 7 * len(late), late, [slot(t) for t in late])
    rnn_params = (rnn_params[0], rnn_params[1] + flight[4][0:1, 0:1]) + rnn_params[2:]
    hr, rnn_saved = _rnn_fwd(pf, keep, *rnn_params)
    o, lses, major = _attn_fwd(pf)

    w_or_all, w_oa_all, w_o_all = (t.reshape(D_MODEL, D_MODEL) for t in _split_wait(
        "gather_out_weights_wait", _peer_copies, flight, o)[1])
    (dx2, dhr, dz_rnn, d_o, dz_attn, dg_r, dg_a, u_rnn, dy_rnn, u_attn, dy_attn, merged, dmo,
     gp_g_final, gp_b_gate, dgate, loss_part) = _hub(
        xs, tgt, hr, pf, o, mod, b_mod, b_gate, g_final.reshape(1, D_MODEL), w_or_all, w_oa_all, w_o_all)
    gp_out, gp_out_low = _pair_grads("out_grads", [u_rnn, u_attn, merged], [dy_rnn, dy_attn, dmo])
    dq, dk, dv = _attn_bwd(pf, d_o, o, lses, major, cosf, sinf)
    dx_rnn, gp_conv_w, gp_conv_b, gp_w_a, gp_b_a, gp_w_x, gp_b_x, gp_lam = _rnn_bwd(
        pf, hr, dhr, rnn_saved, keep, rnn_params[0], rnn_params[2], rnn_params[4], rnn_params[6])
    pieces = [dx_rnn, dz_rnn, dq, dk, dv, dz_attn, dg_r, dg_a]
    gp_w_in, gp_w_in_low = _pair_grads("w_in_grad", [h], pieces)

    by_target = lambda t: [(t.reshape(3, N_DEV, 128, D_MODEL), i) for i in range(3)]
    stacks = [(gp_w_in, None)] + by_target(gp_out)
    from_sib = _rs_to_sibling("rs_sibling", [(gp_w_in_low, None)] + by_target(gp_out_low))
    targets = jnp.bitwise_xor(me, 2 * jnp.arange(4, dtype=jnp.int32)).astype(jnp.int32)
    sums = [_add_sibling("rs_add_sibling_%d" % a, s_, r_, targets) for a, (s_, r_) in enumerate(zip(stacks, from_sib))]
    sends = [send for _, send in sums]
    reduce_flight = _split_start("rs_chips_start", _chip_copies, 3 * len(sends), sends,
                                 [lax.empty(t.shape, t.dtype) for t in sends])

    mod_after = mod + reduce_flight[4][0:1, 0:1]
    grad_x, dshift, dscale, gp_g_norm = _dh_dx(pieces, w_in_near, w_in_sib, w_in_far, xs, dx2, mod_after, b_mod,
                                               g_norm)

    flat = lambda t: t.reshape(-1, 128)
    dmod = flat(jnp.concatenate([dshift, dscale, dgate], axis=1))
    dmod_placed = lax.dynamic_update_slice(jnp.zeros((N_DEV * dmod.shape[0], 128), F32), dmod, (me * dmod.shape[0], 0))
    small = [flat(gp_g_norm), flat(gp_b_gate), flat(gp_conv_b), flat(gp_b_a), flat(gp_b_x), flat(gp_lam),
             flat(gp_g_final), flat(gp_conv_w), jnp.broadcast_to(loss_part, (SUBLANES, 128)),
             flat(gp_w_a), flat(gp_w_x), dmod_placed]
    sizes = [t.shape[0] for t in small]
    small.append(jnp.zeros((-sum(sizes) % (2 * SUBLANES), 128), F32))
    total = _allreduce_small("allreduce_small_grads", jnp.concatenate(small, axis=0))
    offs = [sum(sizes[:i]) for i in range(len(sizes))]
    (g_g_norm, g_b_gate, g_conv_b, g_b_a, g_b_x, g_lam, g_g_final, g_conv_w_all, loss_rows, g_w_a, g_w_x,
     dmod_rows) = (total[o_:o_ + s_] for o_, s_ in zip(offs, sizes))
    loss = loss_rows[0, 0]
    g_conv_w = lax.dynamic_index_in_dim(g_conv_w_all.reshape(RNN_BLOCKS, SUBLANES, 128), me, axis=0,
                                        keepdims=False)[:4]

    dmod_all = dmod_rows.reshape(N_DEV, 3 * D_MODEL)
    dmod_cols = lax.dynamic_slice_in_dim(dmod_all, me * mod_cols, mod_cols, axis=1)
    g_b_mod, g_w_mod = _mod_bwd(c_all, dmod_all, dmod_cols)

    _, from_chips = _split_wait("rs_chips_wait", _chip_copies, reduce_flight, total)

    results = {}
    sharded = (("w_in", w_in, m_w_in, v_w_in, (D_MODEL, D_MODEL)),
               ("w_out_rnn", w_out_rnn, m_w_out_rnn, v_w_out_rnn, (128, D_MODEL)),
               ("w_out_attn", w_out_attn, m_w_out_attn, v_w_out_attn, (128, D_MODEL)),
               ("w_o", w_o, m_w_o, v_w_o, (128, D_MODEL)))
    for (name, w_, m_, v_, shape2), (own, _), arrived in zip(sharded, sums, from_chips):
        d_, nm_, nv_, g_ = _adamw("adamw_" + name, w_.reshape(shape2), own, m_.reshape(shape2), v_.reshape(shape2),
                                  arrived)
        results[name] = (g_, d_, nm_, nv_)
    shape2 = (D_MODEL, mod_cols)
    results["w_mod"] = (g_w_mod,) + tuple(_adamw("adamw_w_mod", w_mod.reshape(shape2), g_w_mod,
                                                 m_w_mod.reshape(shape2), v_w_mod.reshape(shape2)))
    lanes = (("g_norm", g_norm, g_g_norm, m_g_norm, v_g_norm), ("b_mod", b_mod, g_b_mod, m_b_mod, v_b_mod),
             ("b_gate", b_gate, g_b_gate, m_b_gate, v_b_gate), ("conv_w", conv_w, g_conv_w, m_conv_w, v_conv_w),
             ("conv_b", conv_b, g_conv_b, m_conv_b, v_conv_b), ("w_a", w_a, g_w_a, m_w_a, v_w_a),
             ("b_a", b_a, g_b_a, m_b_a, v_b_a), ("w_x", w_x, g_w_x, m_w_x, v_w_x), ("b_x", b_x, g_b_x, m_b_x, v_b_x),
             ("lam", lam, g_lam, m_lam, v_lam), ("g_final", g_final, g_g_final, m_g_final, v_g_final))
    for name, w_, g_, m_, v_ in lanes:
        rows128 = lambda t: t.reshape(-1, 128)
        results[name] = (g_,) + tuple(_adamw("adamw_" + name, rows128(w_), rows128(g_), rows128(m_), rows128(v_)))
    order = ("g_norm", "w_mod", "b_mod", "w_in", "b_gate", "conv_w", "conv_b", "w_a", "b_a", "w_x", "b_x", "lam",
             "w_out_rnn", "w_out_attn", "w_o", "g_final")
    given = dict(g_norm=g_norm, w_mod=w_mod, b_mod=b_mod, w_in=w_in, b_gate=b_gate, conv_w=conv_w, conv_b=conv_b,
                 w_a=w_a, b_a=b_a, w_x=w_x, b_x=b_x, lam=lam, w_out_rnn=w_out_rnn, w_out_attn=w_out_attn, w_o=w_o,
                 g_final=g_final)
    outs = [[results[name][k].reshape(given[name].shape) for name in order] for k in range(4)]
    return (loss, grad_x[None], *outs[0], *outs[1], *outs[2], *outs[3])
```
